```python
import jax, jax.numpy as jnp
from jax import lax
import numpy as np

D_MODEL = 1024
BATCH = 32
SEQ = 2048
DEPTH = 1

N_META = 16
MIX_WIDTH = D_MODEL
CONV_DIM = MIX_WIDTH // 2
ATTN_DIM = MIX_WIDTH - CONV_DIM
HEAD_DIM = 64
N_HEADS = ATTN_DIM // HEAD_DIM
N_CONV_GROUPS = CONV_DIM // HEAD_DIM
CONV_K = 3
D_FF = ((8 * D_MODEL // 3 + 255) // 256) * 256
Q_BLOCK = 128
IN_DIM = 3 * CONV_DIM + 3 * ATTN_DIM + N_HEADS
EPS = 1e-6

kernel_name = "hymba_conv_fox_macaron_layer"


def rms_norm(x, g):
    xf = x.astype(jnp.float32)
    y = xf * lax.rsqrt(jnp.mean(xf * xf, axis=-1, keepdims=True) + EPS)
    return (y * g.astype(jnp.float32)).astype(x.dtype)


def group_rms_norm(x, g, n_groups):
    b, l, c = x.shape
    xg = x.astype(jnp.float32).reshape(b, l, n_groups, c // n_groups)
    xg = xg * lax.rsqrt(jnp.mean(xg * xg, axis=-1, keepdims=True) + EPS)
    return (xg.reshape(b, l, c) * g.astype(jnp.float32)).astype(x.dtype)


def swiglu_ffn(h, w_gu, w_down):
    gate, up = jnp.split(h @ w_gu, 2, axis=-1)
    return (jax.nn.silu(gate) * up) @ w_down


def short_conv_mixer(b_gate, c_gate, hc, conv_w):
    u = c_gate * hc
    y = lax.conv_general_dilated(
        u, conv_w.astype(u.dtype)[:, None, :], window_strides=(1,),
        padding=[(CONV_K - 1, 0)], dimension_numbers=('NWC', 'WIO', 'NWC'),
        feature_group_count=CONV_DIM)
    return b_gate * y


def fox_block(q_blk, fq_blk, q_pos, k, v, fk, k_pos):
    s = jnp.einsum('bhqd,bhkd->bhqk', q_blk, k, preferred_element_type=jnp.float32) * (HEAD_DIM ** -0.5)
    s = s + fq_blk[..., :, None] - fk[..., None, :]
    s = jnp.where(k_pos[None, :] <= q_pos[:, None], s, -jnp.inf)
    p = jax.nn.softmax(s, axis=-1)
    return jnp.einsum('bhqk,bhkd->bhqd', p.astype(v.dtype), v)


def forgetting_attention(q, k, v, fg_logit, b_f):
    bsz, L = q.shape[0], q.shape[1]
    log_f = jax.nn.log_sigmoid(fg_logit.astype(jnp.float32) + b_f.astype(jnp.float32))
    F = jnp.cumsum(log_f, axis=1).transpose(0, 2, 1)
    q, k, v = (t.transpose(0, 2, 1, 3) for t in (q, k, v))
    pos = jnp.arange(L)
    o_meta = fox_block(q[:, :, :N_META], F[:, :, :N_META], pos[:N_META],
                       k[:, :, :N_META], v[:, :, :N_META], F[:, :, :N_META], pos[:N_META])
    n_blk = (L - N_META) // Q_BLOCK
    qr = q[:, :, N_META:].reshape(bsz, N_HEADS, n_blk, Q_BLOCK, HEAD_DIM).transpose(2, 0, 1, 3, 4)
    fr = F[:, :, N_META:].reshape(bsz, N_HEADS, n_blk, Q_BLOCK).transpose(2, 0, 1, 3)
    pr = pos[N_META:].reshape(n_blk, Q_BLOCK)
    o_real = lax.map(lambda a: fox_block(a[0], a[1], a[2], k, v, F, pos), (qr, fr, pr))
    o_real = o_real.transpose(1, 2, 0, 3, 4).reshape(bsz, N_HEADS, L - N_META, HEAD_DIM)
    o = jnp.concatenate([o_meta, o_real], axis=2)
    return o.transpose(0, 2, 1, 3).reshape(bsz, L, ATTN_DIM)


def hybrid_mixer(h, w_in, conv_w, b_f, g_conv, g_attn, w_out):
    bsz, L, _ = h.shape
    proj = h @ w_in
    c0 = 3 * CONV_DIM
    b_gate, c_gate, hc, q, k, v, fg = jnp.split(
        proj, [CONV_DIM, 2 * CONV_DIM, c0, c0 + ATTN_DIM, c0 + 2 * ATTN_DIM, c0 + 3 * ATTN_DIM], axis=-1)
    y_conv = short_conv_mixer(b_gate, c_gate, hc, conv_w)
    hs = (bsz, L, N_HEADS, HEAD_DIM)
    y_attn = forgetting_attention(q.reshape(hs), k.reshape(hs), v.reshape(hs), fg, b_f)
    y = jnp.concatenate([group_rms_norm(y_conv, g_conv, N_CONV_GROUPS),
                         group_rms_norm(y_attn, g_attn, N_HEADS)], axis=-1)
    return y @ w_out


def _fwd_setup_inputs(seed: int = 0) -> dict:
    key = jax.random.key(seed)
    ks = jax.random.split(key, 20)
    nrm = lambda k, shape, scale: jax.random.normal(k, shape, jnp.float32) * scale
    gain = lambda k, shape: 1.0 + 0.02 * jax.random.normal(k, shape, jnp.float32)
    return {
        'x': nrm(ks[0], (BATCH, SEQ, D_MODEL), 1.0),
        'meta_tokens': nrm(ks[1], (N_META, D_MODEL), 1.0),
        'ffn1_norm': gain(ks[2], (DEPTH, D_MODEL)),
        'ffn1_w_gu': nrm(ks[3], (DEPTH, D_MODEL, 2 * D_FF), D_MODEL ** -0.5),
        'ffn1_w_down': nrm(ks[4], (DEPTH, D_FF, D_MODEL), D_FF ** -0.5),
        'mix_norm': gain(ks[5], (DEPTH, D_MODEL)),
        'w_in': nrm(ks[6], (DEPTH, D_MODEL, IN_DIM), D_MODEL ** -0.5),
        'conv_w': nrm(ks[7], (DEPTH, CONV_K, CONV_DIM), CONV_K ** -0.5),
        'b_f': jnp.linspace(1.0, 6.0, N_HEADS, dtype=jnp.float32)[None, :] + nrm(ks[8], (DEPTH, N_HEADS), 0.1),
        'out_norm_conv': gain(ks[9], (DEPTH, CONV_DIM)),
        'out_norm_attn': gain(ks[10], (DEPTH, ATTN_DIM)),
        'w_out': nrm(ks[11], (DEPTH, MIX_WIDTH, D_MODEL), MIX_WIDTH ** -0.5),
        'ffn2_norm': gain(ks[12], (DEPTH, D_MODEL)),
        'ffn2_w_gu': nrm(ks[13], (DEPTH, D_MODEL, 2 * D_FF), D_MODEL ** -0.5),
        'ffn2_w_down': nrm(ks[14], (DEPTH, D_FF, D_MODEL), D_FF ** -0.5),
        'final_norm': gain(ks[15], (D_MODEL,)),
    }


def _fwd_reference(x, meta_tokens, ffn1_norm, ffn1_w_gu, ffn1_w_down, mix_norm, w_in, conv_w, b_f,
              out_norm_conv, out_norm_attn, w_out, ffn2_norm, ffn2_w_gu, ffn2_w_down, final_norm):
    bsz = x.shape[0]
    meta = jnp.broadcast_to(meta_tokens.astype(x.dtype)[None], (bsz, N_META, D_MODEL))
    h = jnp.concatenate([meta, x], axis=1)
    for l in range(DEPTH):
        h = h + 0.5 * swiglu_ffn(rms_norm(h, ffn1_norm[l]), ffn1_w_gu[l], ffn1_w_down[l])
        h = h + hybrid_mixer(rms_norm(h, mix_norm[l]), w_in[l], conv_w[l], b_f[l],
                             out_norm_conv[l], out_norm_attn[l], w_out[l])
        h = h + 0.5 * swiglu_ffn(rms_norm(h, ffn2_norm[l]), ffn2_w_gu[l], ffn2_w_down[l])
    h = h[:, N_META:]
    return rms_norm(h, final_norm)


import jax as _jax
import jax.numpy as _jnp

TWIN_FORMAT = 'train_step'
FWD_PARAMS = ['x', 'meta_tokens', 'ffn1_norm', 'ffn1_w_gu', 'ffn1_w_down', 'mix_norm', 'w_in', 'conv_w', 'b_f', 'out_norm_conv', 'out_norm_attn', 'w_out', 'ffn2_norm', 'ffn2_w_gu', 'ffn2_w_down', 'final_norm']
TWIN_WEIGHTS = ['meta_tokens', 'ffn1_norm', 'ffn1_w_gu', 'ffn1_w_down', 'mix_norm', 'w_in', 'conv_w', 'b_f', 'out_norm_conv', 'out_norm_attn', 'w_out', 'ffn2_norm', 'ffn2_w_gu', 'ffn2_w_down', 'final_norm']
TWIN_DIFF_INPUT = 'x'
TWIN_INPUTS = ['x', 'meta_tokens', 'ffn1_norm', 'ffn1_w_gu', 'ffn1_w_down', 'mix_norm', 'w_in', 'conv_w', 'b_f', 'out_norm_conv', 'out_norm_attn', 'w_out', 'ffn2_norm', 'ffn2_w_gu', 'ffn2_w_down', 'final_norm', 'loss_target', 'm_meta_tokens', 'm_ffn1_norm', 'm_ffn1_w_gu', 'm_ffn1_w_down', 'm_mix_norm', 'm_w_in', 'm_conv_w', 'm_b_f', 'm_out_norm_conv', 'm_out_norm_attn', 'm_w_out', 'm_ffn2_norm', 'm_ffn2_w_gu', 'm_ffn2_w_down', 'm_final_norm', 'v_meta_tokens', 'v_ffn1_norm', 'v_ffn1_w_gu', 'v_ffn1_w_down', 'v_mix_norm', 'v_w_in', 'v_conv_w', 'v_b_f', 'v_out_norm_conv', 'v_out_norm_attn', 'v_w_out', 'v_ffn2_norm', 'v_ffn2_w_gu', 'v_ffn2_w_down', 'v_final_norm']
TWIN_OUTPUTS = ['loss', 'grad_x', 'grad_meta_tokens', 'grad_ffn1_norm', 'grad_ffn1_w_gu', 'grad_ffn1_w_down', 'grad_mix_norm', 'grad_w_in', 'grad_conv_w', 'grad_b_f', 'grad_out_norm_conv', 'grad_out_norm_attn', 'grad_w_out', 'grad_ffn2_norm', 'grad_ffn2_w_gu', 'grad_ffn2_w_down', 'grad_final_norm', 'delta_meta_tokens', 'delta_ffn1_norm', 'delta_ffn1_w_gu', 'delta_ffn1_w_down', 'delta_mix_norm', 'delta_w_in', 'delta_conv_w', 'delta_b_f', 'delta_out_norm_conv', 'delta_out_norm_attn', 'delta_w_out', 'delta_ffn2_norm', 'delta_ffn2_w_gu', 'delta_ffn2_w_down', 'delta_final_norm', 'new_m_meta_tokens', 'new_m_ffn1_norm', 'new_m_ffn1_w_gu', 'new_m_ffn1_w_down', 'new_m_mix_norm', 'new_m_w_in', 'new_m_conv_w', 'new_m_b_f', 'new_m_out_norm_conv', 'new_m_out_norm_attn', 'new_m_w_out', 'new_m_ffn2_norm', 'new_m_ffn2_w_gu', 'new_m_ffn2_w_down', 'new_m_final_norm', 'new_v_meta_tokens', 'new_v_ffn1_norm', 'new_v_ffn1_w_gu', 'new_v_ffn1_w_down', 'new_v_mix_norm', 'new_v_w_in', 'new_v_conv_w', 'new_v_b_f', 'new_v_out_norm_conv', 'new_v_out_norm_attn', 'new_v_w_out', 'new_v_ffn2_norm', 'new_v_ffn2_w_gu', 'new_v_ffn2_w_down', 'new_v_final_norm']
TWIN_LEAF_KINDS = {'loss': 'loss', 'grad_x': 'grad_x', 'grad_meta_tokens': 'grad_w', 'grad_ffn1_norm': 'grad_w', 'grad_ffn1_w_gu': 'grad_w', 'grad_ffn1_w_down': 'grad_w', 'grad_mix_norm': 'grad_w', 'grad_w_in': 'grad_w', 'grad_conv_w': 'grad_w', 'grad_b_f': 'grad_w', 'grad_out_norm_conv': 'grad_w', 'grad_out_norm_attn': 'grad_w', 'grad_w_out': 'grad_w', 'grad_ffn2_norm': 'grad_w', 'grad_ffn2_w_gu': 'grad_w', 'grad_ffn2_w_down': 'grad_w', 'grad_final_norm': 'grad_w', 'delta_meta_tokens': 'delta_w', 'delta_ffn1_norm': 'delta_w', 'delta_ffn1_w_gu': 'delta_w', 'delta_ffn1_w_down': 'delta_w', 'delta_mix_norm': 'delta_w', 'delta_w_in': 'delta_w', 'delta_conv_w': 'delta_w', 'delta_b_f': 'delta_w', 'delta_out_norm_conv': 'delta_w', 'delta_out_norm_attn': 'delta_w', 'delta_w_out': 'delta_w', 'delta_ffn2_norm': 'delta_w', 'delta_ffn2_w_gu': 'delta_w', 'delta_ffn2_w_down': 'delta_w', 'delta_final_norm': 'delta_w', 'new_m_meta_tokens': 'new_m', 'new_m_ffn1_norm': 'new_m', 'new_m_ffn1_w_gu': 'new_m', 'new_m_ffn1_w_down': 'new_m', 'new_m_mix_norm': 'new_m', 'new_m_w_in': 'new_m', 'new_m_conv_w': 'new_m', 'new_m_b_f': 'new_m', 'new_m_out_norm_conv': 'new_m', 'new_m_out_norm_attn': 'new_m', 'new_m_w_out': 'new_m', 'new_m_ffn2_norm': 'new_m', 'new_m_ffn2_w_gu': 'new_m', 'new_m_ffn2_w_down': 'new_m', 'new_m_final_norm': 'new_m', 'new_v_meta_tokens': 'new_v', 'new_v_ffn1_norm': 'new_v', 'new_v_ffn1_w_gu': 'new_v', 'new_v_ffn1_w_down': 'new_v', 'new_v_mix_norm': 'new_v', 'new_v_w_in': 'new_v', 'new_v_conv_w': 'new_v', 'new_v_b_f': 'new_v', 'new_v_out_norm_conv': 'new_v', 'new_v_out_norm_attn': 'new_v', 'new_v_w_out': 'new_v', 'new_v_ffn2_norm': 'new_v', 'new_v_ffn2_w_gu': 'new_v', 'new_v_ffn2_w_down': 'new_v', 'new_v_final_norm': 'new_v'}


def _forward(args):
    return _fwd_reference(*[args[k] for k in FWD_PARAMS])


def _output_shape():
    out = _jax.eval_shape(lambda: _forward(_fwd_setup_inputs(0)))
    return out.shape, out.dtype

N_MICROBATCH = 1
ADAM_LR = 0.001
ADAM_B1 = 0.9
ADAM_B2 = 0.999
ADAM_EPS = 1e-08
ADAM_WD = 0.01
ADAM_STEP = 10
PER_EXAMPLE_BATCH_AXIS = {'x': 0, 'loss_target': 0}
SHARED_INPUTS = []
_WEIGHT_DTYPES = {'meta_tokens': _jnp.float32, 'ffn1_norm': _jnp.float32, 'ffn1_w_gu': _jnp.float32, 'ffn1_w_down': _jnp.float32, 'mix_norm': _jnp.float32, 'w_in': _jnp.float32, 'conv_w': _jnp.float32, 'b_f': _jnp.float32, 'out_norm_conv': _jnp.float32, 'out_norm_attn': _jnp.float32, 'w_out': _jnp.float32, 'ffn2_norm': _jnp.float32, 'ffn2_w_gu': _jnp.float32, 'ffn2_w_down': _jnp.float32, 'final_norm': _jnp.float32}
MOMENT_SCALE = {'meta_tokens': 1.074747e-02, 'ffn1_norm': 1.482925e-01, 'ffn1_w_gu': 6.193404e-02, 'ffn1_w_down': 1.010250e-01, 'mix_norm': 3.034636e-01, 'w_in': 1.747245e-01, 'conv_w': 1.882023e-01, 'b_f': 3.058025e-01, 'out_norm_conv': 1.737838e-01, 'out_norm_attn': 1.913811e-01, 'w_out': 1.775399e-01, 'ffn2_norm': 7.613584e-02, 'ffn2_w_gu': 3.114746e-02, 'ffn2_w_down': 5.079382e-02, 'final_norm': 6.389982e+01}


def _to_microbatches(a, axis):
    t = _jnp.moveaxis(a, axis, 0)
    t = t.reshape((N_MICROBATCH, t.shape[0] // N_MICROBATCH) + t.shape[1:])
    return _jnp.moveaxis(t, 1, axis + 1)


def setup_inputs(seed: int = 0) -> dict:
    inp = _fwd_setup_inputs(seed)
    key = _jax.random.fold_in(_jax.random.key(seed), 7919)
    shape, _ = _output_shape()
    out = dict(inp)
    out["loss_target"] = _jax.random.normal(_jax.random.fold_in(key, 0), shape, _jnp.float32)
    for i, name in enumerate(TWIN_WEIGHTS):
        w = inp[name].astype(_jnp.float32)
        if MOMENT_SCALE is None:
            s = _jnp.sqrt(_jnp.mean(_jnp.square(w)) + 1e-30)
        else:
            s = MOMENT_SCALE[name]
        km, kv = _jax.random.split(_jax.random.fold_in(key, i + 1))
        out[name] = w
        out["m_" + name] = s * _jax.random.normal(km, w.shape, _jnp.float32)
        out["v_" + name] = (s * s) * _jax.random.uniform(kv, w.shape, _jnp.float32, 0.5, 1.5)
    if N_MICROBATCH > 1:
        for name, axis in PER_EXAMPLE_BATCH_AXIS.items():
            out[name] = _to_microbatches(out[name], axis)
    return {'x': out['x'], 'meta_tokens': out['meta_tokens'], 'ffn1_norm': out['ffn1_norm'], 'ffn1_w_gu': out['ffn1_w_gu'], 'ffn1_w_down': out['ffn1_w_down'], 'mix_norm': out['mix_norm'], 'w_in': out['w_in'], 'conv_w': out['conv_w'], 'b_f': out['b_f'], 'out_norm_conv': out['out_norm_conv'], 'out_norm_attn': out['out_norm_attn'], 'w_out': out['w_out'], 'ffn2_norm': out['ffn2_norm'], 'ffn2_w_gu': out['ffn2_w_gu'], 'ffn2_w_down': out['ffn2_w_down'], 'final_norm': out['final_norm'], 'loss_target': out['loss_target'], 'm_meta_tokens': out['m_meta_tokens'], 'm_ffn1_norm': out['m_ffn1_norm'], 'm_ffn1_w_gu': out['m_ffn1_w_gu'], 'm_ffn1_w_down': out['m_ffn1_w_down'], 'm_mix_norm': out['m_mix_norm'], 'm_w_in': out['m_w_in'], 'm_conv_w': out['m_conv_w'], 'm_b_f': out['m_b_f'], 'm_out_norm_conv': out['m_out_norm_conv'], 'm_out_norm_attn': out['m_out_norm_attn'], 'm_w_out': out['m_w_out'], 'm_ffn2_norm': out['m_ffn2_norm'], 'm_ffn2_w_gu': out['m_ffn2_w_gu'], 'm_ffn2_w_down': out['m_ffn2_w_down'], 'm_final_norm': out['m_final_norm'], 'v_meta_tokens': out['v_meta_tokens'], 'v_ffn1_norm': out['v_ffn1_norm'], 'v_ffn1_w_gu': out['v_ffn1_w_gu'], 'v_ffn1_w_down': out['v_ffn1_w_down'], 'v_mix_norm': out['v_mix_norm'], 'v_w_in': out['v_w_in'], 'v_conv_w': out['v_conv_w'], 'v_b_f': out['v_b_f'], 'v_out_norm_conv': out['v_out_norm_conv'], 'v_out_norm_attn': out['v_out_norm_attn'], 'v_w_out': out['v_w_out'], 'v_ffn2_norm': out['v_ffn2_norm'], 'v_ffn2_w_gu': out['v_ffn2_w_gu'], 'v_ffn2_w_down': out['v_ffn2_w_down'], 'v_final_norm': out['v_final_norm']}


def _loss(weights, diff, rest, loss_target):
    with _jax.named_scope("forward"):
        args = {**rest, TWIN_DIFF_INPUT: diff, **{k: w.astype(_WEIGHT_DTYPES[k]) for k, w in weights.items()}}
        y = _forward(args)
    with _jax.named_scope("loss_head"):
        err = _jnp.square(y.astype(_jnp.float32) - loss_target)
        return 0.5 * _jnp.sum(_jnp.mean(err, axis=-1)) if err.ndim else 0.5 * err


def _adamw(w, g, m, v):
    m = ADAM_B1 * m + (1.0 - ADAM_B1) * g
    v = ADAM_B2 * v + (1.0 - ADAM_B2) * _jnp.square(g)
    m_hat = m / (1.0 - ADAM_B1 ** ADAM_STEP)
    v_hat = v / (1.0 - ADAM_B2 ** ADAM_STEP)
    delta = -ADAM_LR * (m_hat / (_jnp.sqrt(v_hat) + ADAM_EPS) + ADAM_WD * w)
    return delta, m, v


def reference(x, meta_tokens, ffn1_norm, ffn1_w_gu, ffn1_w_down, mix_norm, w_in, conv_w, b_f, out_norm_conv, out_norm_attn, w_out, ffn2_norm, ffn2_w_gu, ffn2_w_down, final_norm, loss_target, m_meta_tokens, m_ffn1_norm, m_ffn1_w_gu, m_ffn1_w_down, m_mix_norm, m_w_in, m_conv_w, m_b_f, m_out_norm_conv, m_out_norm_attn, m_w_out, m_ffn2_norm, m_ffn2_w_gu, m_ffn2_w_down, m_final_norm, v_meta_tokens, v_ffn1_norm, v_ffn1_w_gu, v_ffn1_w_down, v_mix_norm, v_w_in, v_conv_w, v_b_f, v_out_norm_conv, v_out_norm_attn, v_w_out, v_ffn2_norm, v_ffn2_w_gu, v_ffn2_w_down, v_final_norm):
    given = dict(x=x, meta_tokens=meta_tokens, ffn1_norm=ffn1_norm, ffn1_w_gu=ffn1_w_gu, ffn1_w_down=ffn1_w_down, mix_norm=mix_norm, w_in=w_in, conv_w=conv_w, b_f=b_f, out_norm_conv=out_norm_conv, out_norm_attn=out_norm_attn, w_out=w_out, ffn2_norm=ffn2_norm, ffn2_w_gu=ffn2_w_gu, ffn2_w_down=ffn2_w_down, final_norm=final_norm, loss_target=loss_target, m_meta_tokens=m_meta_tokens, m_ffn1_norm=m_ffn1_norm, m_ffn1_w_gu=m_ffn1_w_gu, m_ffn1_w_down=m_ffn1_w_down, m_mix_norm=m_mix_norm, m_w_in=m_w_in, m_conv_w=m_conv_w, m_b_f=m_b_f, m_out_norm_conv=m_out_norm_conv, m_out_norm_attn=m_out_norm_attn, m_w_out=m_w_out, m_ffn2_norm=m_ffn2_norm, m_ffn2_w_gu=m_ffn2_w_gu, m_ffn2_w_down=m_ffn2_w_down, m_final_norm=m_final_norm, v_meta_tokens=v_meta_tokens, v_ffn1_norm=v_ffn1_norm, v_ffn1_w_gu=v_ffn1_w_gu, v_ffn1_w_down=v_ffn1_w_down, v_mix_norm=v_mix_norm, v_w_in=v_w_in, v_conv_w=v_conv_w, v_b_f=v_b_f, v_out_norm_conv=v_out_norm_conv, v_out_norm_attn=v_out_norm_attn, v_w_out=v_w_out, v_ffn2_norm=v_ffn2_norm, v_ffn2_w_gu=v_ffn2_w_gu, v_ffn2_w_down=v_ffn2_w_down, v_final_norm=v_final_norm)
    weights = {n: given[n] for n in TWIN_WEIGHTS}
    shared = {n: given[n] for n in SHARED_INPUTS}
    per_example = {n: given[n] for n in ['x']}
    grad_fn = _jax.value_and_grad(_loss, argnums=(0, 1))

    def one_microbatch(ex, loss_target):
        ex = dict(ex)
        diff = ex.pop(TWIN_DIFF_INPUT)
        return grad_fn(weights, diff, {**shared, **ex}, loss_target)

    if N_MICROBATCH == 1:
        loss, (grad_w, grad_x) = one_microbatch(per_example, given["loss_target"])
    else:
        def body(carry, xs):
            loss_sum, grad_sum = carry
            l_k, (gw_k, gx_k) = one_microbatch(xs[0], xs[1])
            with _jax.named_scope("update"):
                return (loss_sum + l_k, _jax.tree.map(_jnp.add, grad_sum, gw_k)), gx_k

        init = (_jnp.zeros((), _jnp.float32), _jax.tree.map(_jnp.zeros_like, weights))
        (loss, grad_w), grad_x = _jax.lax.scan(body, init, (per_example, given["loss_target"]))
    with _jax.named_scope("update"):
        delta_w, new_m, new_v = {}, {}, {}
        for n in TWIN_WEIGHTS:
            delta_w[n], new_m[n], new_v[n] = _adamw(weights[n], grad_w[n], given["m_" + n], given["v_" + n])
    return (loss, grad_x, *[grad_w[n] for n in TWIN_WEIGHTS], *[delta_w[n] for n in TWIN_WEIGHTS],
            *[new_m[n] for n in TWIN_WEIGHTS], *[new_v[n] for n in TWIN_WEIGHTS])
```

```python
import jax
import jax.numpy as jnp
from jax import lax
from jax.experimental import pallas as pl
from jax.experimental.pallas import tpu as pltpu

F32 = jnp.float32
BF16 = jnp.bfloat16

N_DEV = 8
D_MODEL = 1024
N_META = 16
PAD = 128 - N_META
CONV_DIM = 512
ATTN_DIM = 512
HEAD_DIM = 64
N_HEADS = 8
N_PAIRS = N_HEADS // 2
D_FF = 2816
N_CHUNK = 4
F_CHUNK = D_FF // N_CHUNK
IN_DIM = 3080
IN_PAD = 3200
IN_MAIN = 3072
EPS = 1e-6
NEG = -1e30
TQ = 128
VMEM_LIMIT = 56 * 1024 * 1024

ADAM_LR = 0.001
ADAM_B1 = 0.9
ADAM_B2 = 0.999
ADAM_EPS = 1e-08
ADAM_WD = 0.01
ADAM_STEP = 10

MESH = pl.DeviceIdType.MESH
ANY = pl.BlockSpec(memory_space=pl.ANY)


def _params(sem=None):
    return pltpu.CompilerParams(dimension_semantics=sem, vmem_limit_bytes=VMEM_LIMIT)


def _row_tile(n, prefer):
    for t in (prefer, 512, 256, 128):
        if t <= n and n % t == 0:
            return t
    raise ValueError(f"no row tile for {n}")


def _dot(a, b):
    return jnp.dot(a, b, preferred_element_type=F32)


def _dot_nt(a, b):
    return lax.dot_general(a, b, (((1,), (1,)), ((), ())), preferred_element_type=F32)


def _dot_tn(a, b):
    return lax.dot_general(a, b, (((0,), (0,)), ((), ())), preferred_element_type=F32)


def _rms(x, g):
    r = lax.rsqrt(jnp.mean(x * x, axis=-1, keepdims=True) + EPS)
    xhat = x * r
    return xhat * g, xhat, r


def _rms_bwd(dn, xhat, r, g):
    dxhat = dn * g
    return r * (dxhat - xhat * jnp.mean(dxhat * xhat, axis=-1, keepdims=True))


def _sigmoid(x):
    return 1.0 / (1.0 + jnp.exp(-x))


def _ffn_fwd(h, gain, wgu, wd, name):
    n = h.shape[0]
    tm = _row_tile(n, 512)

    def body(h_ref, g_ref, wgu_ref, wd_ref, out_ref, gate_ref, up_ref, n_scr, acc_scr):
        j = pl.program_id(1)

        @pl.when(j == 0)
        def _():
            y, _, _ = _rms(h_ref[...], g_ref[...])
            n_scr[...] = y.astype(BF16)
            acc_scr[...] = jnp.zeros_like(acc_scr)

        nb = n_scr[...]
        gate = _dot(nb, wgu_ref[0, 0])
        up = _dot(nb, wgu_ref[1, 0])
        gate_ref[0] = gate.astype(BF16)
        up_ref[0] = up.astype(BF16)
        act = (gate * _sigmoid(gate) * up).astype(BF16)
        acc_scr[...] += _dot(act, wd_ref[0])

        @pl.when(j == N_CHUNK - 1)
        def _():
            out_ref[...] = h_ref[...] + 0.5 * acc_scr[...]

    return pl.pallas_call(
        body, name=name, grid=(n // tm, N_CHUNK),
        in_specs=[pl.BlockSpec((tm, D_MODEL), lambda i, j: (i, 0)),
                  pl.BlockSpec((1, D_MODEL), lambda i, j: (0, 0)),
                  pl.BlockSpec((2, 1, D_MODEL, F_CHUNK), lambda i, j: (0, j, 0, 0)),
                  pl.BlockSpec((1, F_CHUNK, D_MODEL), lambda i, j: (j, 0, 0))],
        out_specs=[pl.BlockSpec((tm, D_MODEL), lambda i, j: (i, 0)),
                   pl.BlockSpec((1, tm, F_CHUNK), lambda i, j: (j, i, 0)),
                   pl.BlockSpec((1, tm, F_CHUNK), lambda i, j: (j, i, 0))],
        out_shape=[jax.ShapeDtypeStruct((n, D_MODEL), F32),
                   jax.ShapeDtypeStruct((N_CHUNK, n, F_CHUNK), BF16),
                   jax.ShapeDtypeStruct((N_CHUNK, n, F_CHUNK), BF16)],
        scratch_shapes=[pltpu.VMEM((tm, D_MODEL), BF16), pltpu.VMEM((tm, D_MODEL), F32)],
        compiler_params=_params(("parallel", "arbitrary")),
    )(h, gain, wgu, wd)


def _ffn_bwd_x(dh_out, h_in, gain, gate, up, wgu, wd, name):
    n = h_in.shape[0]
    tm = _row_tile(n, 512)

    def body(dh_ref, h_ref, g_ref, gate_ref, up_ref, wgu_ref, wd_ref,
             dhin_ref, dgate_ref, dup_ref, dgain_ref, dhb_scr, acc_scr):
        i = pl.program_id(0)
        j = pl.program_id(1)

        @pl.when((i == 0) & (j == 0))
        def _():
            dgain_ref[...] = jnp.zeros_like(dgain_ref)

        @pl.when(j == 0)
        def _():
            dhb_scr[...] = (0.5 * dh_ref[...]).astype(BF16)
            acc_scr[...] = jnp.zeros_like(acc_scr)

        da = _dot_nt(dhb_scr[...], wd_ref[0])
        g = gate_ref[0].astype(F32)
        u = up_ref[0].astype(F32)
        sig = _sigmoid(g)
        dgate = (da * u * (sig * (1.0 + g * (1.0 - sig)))).astype(BF16)
        dup = (da * (g * sig)).astype(BF16)
        dgate_ref[0] = dgate
        dup_ref[0] = dup
        acc_scr[...] += _dot_nt(dgate, wgu_ref[0, 0]) + _dot_nt(dup, wgu_ref[1, 0])

        @pl.when(j == N_CHUNK - 1)
        def _():
            gain_v = g_ref[...]
            _, xhat, r = _rms(h_ref[...], gain_v)
            dn = acc_scr[...]
            dhin_ref[...] = dh_ref[...] + _rms_bwd(dn, xhat, r, gain_v)
            dgain_ref[...] += jnp.sum(dn * xhat, axis=0, keepdims=True)

    chunk = pl.BlockSpec((1, tm, F_CHUNK), lambda i, j: (j, i, 0))
    rows = pl.BlockSpec((tm, D_MODEL), lambda i, j: (i, 0))
    vec = pl.BlockSpec((1, D_MODEL), lambda i, j: (0, 0))
    return pl.pallas_call(
        body, name=name, grid=(n // tm, N_CHUNK),
        in_specs=[rows, rows, vec, chunk, chunk,
                  pl.BlockSpec((2, 1, D_MODEL, F_CHUNK), lambda i, j: (0, j, 0, 0)),
                  pl.BlockSpec((1, F_CHUNK, D_MODEL), lambda i, j: (j, 0, 0))],
        out_specs=[rows, chunk, chunk, vec],
        out_shape=[jax.ShapeDtypeStruct((n, D_MODEL), F32),
                   jax.ShapeDtypeStruct((N_CHUNK, n, F_CHUNK), BF16),
                   jax.ShapeDtypeStruct((N_CHUNK, n, F_CHUNK), BF16),
                   jax.ShapeDtypeStruct((1, D_MODEL), F32)],
        scratch_shapes=[pltpu.VMEM((tm, D_MODEL), BF16), pltpu.VMEM((tm, D_MODEL), F32)],
        compiler_params=_params(("arbitrary", "arbitrary")),
    )(dh_out, h_in, gain, gate, up, wgu, wd)


def _ffn_bwd_w(dh_out, h_in, gain, gate, up, dgate, dup, name):
    n = h_in.shape[0]
    tm = _row_tile(n, 512)
    n_i = n // tm

    def body(dh_ref, h_ref, g_ref, gate_ref, up_ref, dgate_ref, dup_ref, dwgu_ref, dwd_ref,
             ag_scr, au_scr, ad_scr):
        i = pl.program_id(1)

        @pl.when(i == 0)
        def _():
            ag_scr[...] = jnp.zeros_like(ag_scr)
            au_scr[...] = jnp.zeros_like(au_scr)
            ad_scr[...] = jnp.zeros_like(ad_scr)

        y, _, _ = _rms(h_ref[...], g_ref[...])
        nb = y.astype(BF16)
        ag_scr[...] += _dot_tn(nb, dgate_ref[0])
        au_scr[...] += _dot_tn(nb, dup_ref[0])
        g = gate_ref[0].astype(F32)
        act = (g * _sigmoid(g) * up_ref[0].astype(F32)).astype(BF16)
        ad_scr[...] += _dot_tn(act, (0.5 * dh_ref[...]).astype(BF16))

        @pl.when(i == n_i - 1)
        def _():
            dwgu_ref[0, 0] = ag_scr[...].astype(BF16)
            dwgu_ref[1, 0] = au_scr[...].astype(BF16)
            dwd_ref[0] = ad_scr[...].astype(BF16)

    chunk = pl.BlockSpec((1, tm, F_CHUNK), lambda j, i: (j, i, 0))
    rows = pl.BlockSpec((tm, D_MODEL), lambda j, i: (i, 0))
    return pl.pallas_call(
        body, name=name, grid=(N_CHUNK, n_i),
        in_specs=[rows, rows, pl.BlockSpec((1, D_MODEL), lambda j, i: (0, 0)), chunk, chunk, chunk, chunk],
        out_specs=[pl.BlockSpec((2, 1, D_MODEL, F_CHUNK), lambda j, i: (0, j, 0, 0)),
                   pl.BlockSpec((1, F_CHUNK, D_MODEL), lambda j, i: (j, 0, 0))],
        out_shape=[jax.ShapeDtypeStruct((2, N_CHUNK, D_MODEL, F_CHUNK), BF16),
                   jax.ShapeDtypeStruct((N_CHUNK, F_CHUNK, D_MODEL), BF16)],
        scratch_shapes=[pltpu.VMEM((D_MODEL, F_CHUNK), F32), pltpu.VMEM((D_MODEL, F_CHUNK), F32),
                        pltpu.VMEM((F_CHUNK, D_MODEL), F32)],
        compiler_params=_params(("parallel", "arbitrary")),
    )(dh_out, h_in, gain, gate, up, dgate, dup)


N_PIECE = IN_MAIN // 512


def _inproj_fwd(h, gain, w_in):
    n = h.shape[0]
    tm = _row_tile(n, 512)

    def body(h_ref, g_ref, w_ref, *outs):
        y, _, _ = _rms(h_ref[...], g_ref[...])
        nb = y.astype(BF16)
        for p in range(N_PIECE):
            outs[p][...] = _dot(nb, w_ref[:, 512 * p:512 * (p + 1)]).astype(BF16)
        outs[N_PIECE][...] = _dot(nb, w_ref[:, IN_MAIN:IN_PAD])

    piece = pl.BlockSpec((tm, 512), lambda i: (i, 0))
    return pl.pallas_call(
        body, name="inproj_fwd", grid=(n // tm,),
        in_specs=[pl.BlockSpec((tm, D_MODEL), lambda i: (i, 0)),
                  pl.BlockSpec((1, D_MODEL), lambda i: (0, 0)),
                  pl.BlockSpec((D_MODEL, IN_PAD), lambda i: (0, 0))],
        out_specs=[piece] * N_PIECE + [pl.BlockSpec((tm, 128), lambda i: (i, 0))],
        out_shape=[jax.ShapeDtypeStruct((n, 512), BF16)] * N_PIECE + [jax.ShapeDtypeStruct((n, 128), F32)],
        compiler_params=_params(("parallel",)),
    )(h, gain, w_in)


def _inproj_bwd(dpieces, dfg, dh_out, h_in, gain, w_in):
    n = h_in.shape[0]
    tm = _row_tile(n, 512)
    n_i = n // tm

    def body(*refs):
        dp_refs = refs[:N_PIECE]
        dfg_ref, dh_ref, h_ref, g_ref, w_ref, dhin_ref, dw_ref, dgain_ref, acc_scr = refs[N_PIECE:]
        i = pl.program_id(0)

        @pl.when(i == 0)
        def _():
            acc_scr[...] = jnp.zeros_like(acc_scr)
            dgain_ref[...] = jnp.zeros_like(dgain_ref)

        gain_v = g_ref[...]
        y, xhat, r = _rms(h_ref[...], gain_v)
        nb = y.astype(BF16)
        dn = jnp.zeros((tm, D_MODEL), F32)
        for p in range(N_PIECE + 1):
            lo, hi = (512 * p, 512 * (p + 1)) if p < N_PIECE else (IN_MAIN, IN_PAD)
            dp = (dp_refs[p][...] if p < N_PIECE else dfg_ref[...]).astype(BF16)
            dn = dn + _dot_nt(dp, w_ref[:, lo:hi])
            acc_scr[:, lo:hi] += _dot_tn(nb, dp)
        dhin_ref[...] = dh_ref[...] + _rms_bwd(dn, xhat, r, gain_v)
        dgain_ref[...] += jnp.sum(dn * xhat, axis=0, keepdims=True)

        @pl.when(i == n_i - 1)
        def _():
            dw_ref[...] = acc_scr[...].astype(BF16)

    piece = pl.BlockSpec((tm, 512), lambda i: (i, 0))
    rows = pl.BlockSpec((tm, D_MODEL), lambda i: (i, 0))
    vec = pl.BlockSpec((1, D_MODEL), lambda i: (0, 0))
    wspec = pl.BlockSpec((D_MODEL, IN_PAD), lambda i: (0, 0))
    return pl.pallas_call(
        body, name="inproj_bwd", grid=(n_i,),
        in_specs=[piece] * N_PIECE + [pl.BlockSpec((tm, 128), lambda i: (i, 0)), rows, rows, vec, wspec],
        out_specs=[rows, wspec, vec],
        out_shape=[jax.ShapeDtypeStruct((n, D_MODEL), F32),
                   jax.ShapeDtypeStruct((D_MODEL, IN_PAD), BF16),
                   jax.ShapeDtypeStruct((1, D_MODEL), F32)],
        scratch_shapes=[pltpu.VMEM((D_MODEL, IN_PAD), F32)],
        compiler_params=_params(("arbitrary",)),
    )(*dpieces, dfg, dh_out, h_in, gain, w_in)


def _outproj_fwd(zc, za, w_out, h):
    n = h.shape[0]
    tm = _row_tile(n, 512)

    def body(zc_ref, za_ref, w_ref, h_ref, out_ref):
        out_ref[...] = (h_ref[...] + _dot(zc_ref[...], w_ref[0:CONV_DIM, :])
                        + _dot(za_ref[...], w_ref[CONV_DIM:, :]))

    half = pl.BlockSpec((tm, 512), lambda i: (i, 0))
    rows = pl.BlockSpec((tm, D_MODEL), lambda i: (i, 0))
    return pl.pallas_call(
        body, name="outproj_fwd", grid=(n // tm,),
        in_specs=[half, half, pl.BlockSpec((D_MODEL, D_MODEL), lambda i: (0, 0)), rows],
        out_specs=rows,
        out_shape=jax.ShapeDtypeStruct((n, D_MODEL), F32),
        compiler_params=_params(("parallel",)),
    )(zc, za, w_out, h)


def _outproj_bwd(dh, zc, za, w_out):
    n = dh.shape[0]
    tm = _row_tile(n, 512)
    n_i = n // tm

    def body(dh_ref, zc_ref, za_ref, w_ref, dzc_ref, dza_ref, dw_ref, acc_scr):
        i = pl.program_id(0)

        @pl.when(i == 0)
        def _():
            acc_scr[...] = jnp.zeros_like(acc_scr)

        dhb = dh_ref[...].astype(BF16)
        dzc_ref[...] = _dot_nt(dhb, w_ref[0:CONV_DIM, :]).astype(BF16)
        dza_ref[...] = _dot_nt(dhb, w_ref[CONV_DIM:, :]).astype(BF16)
        acc_scr[0:CONV_DIM, :] += _dot_tn(zc_ref[...], dhb)
        acc_scr[CONV_DIM:, :] += _dot_tn(za_ref[...], dhb)

        @pl.when(i == n_i - 1)
        def _():
            dw_ref[...] = acc_scr[...].astype(BF16)

    half = pl.BlockSpec((tm, 512), lambda i: (i, 0))
    wspec = pl.BlockSpec((D_MODEL, D_MODEL), lambda i: (0, 0))
    return pl.pallas_call(
        body, name="outproj_bwd", grid=(n_i,),
        in_specs=[pl.BlockSpec((tm, D_MODEL), lambda i: (i, 0)), half, half, wspec],
        out_specs=[half, half, wspec],
        out_shape=[jax.ShapeDtypeStruct((n, 512), BF16), jax.ShapeDtypeStruct((n, 512), BF16),
                   jax.ShapeDtypeStruct((D_MODEL, D_MODEL), BF16)],
        scratch_shapes=[pltpu.VMEM((D_MODEL, D_MODEL), F32)],
        compiler_params=_params(("arbitrary",)),
    )(dh, zc, za, w_out)


def _group_matrix():
    r = lax.broadcasted_iota(jnp.int32, (128, 128), 0) // HEAD_DIM
    c = lax.broadcasted_iota(jnp.int32, (128, 128), 1) // HEAD_DIM
    return jnp.where(r == c, 1.0 / HEAD_DIM, 0.0).astype(BF16)


def _group_mean(x, gmat):
    hi = x.astype(BF16)
    lo = (x - hi.astype(F32)).astype(BF16)
    return _dot(hi, gmat) + _dot(lo, gmat)


def _shift_rows(x, s):
    rows = x.shape[0]
    t = lax.broadcasted_iota(jnp.int32, x.shape, 0)
    rolled = pltpu.roll(x, s % rows, 0)
    keep = (t >= s) if s > 0 else (t < rows + s)
    return jnp.where(keep, rolled, 0.0)


def _conv_parts(bg_ref, cg_ref, hc_ref, w_ref):
    bg = bg_ref[...].astype(F32)
    cg = cg_ref[...].astype(F32)
    hc = hc_ref[...].astype(F32)
    u = cg * hc
    u1 = _shift_rows(u, 1)
    u2 = _shift_rows(u, 2)
    conv = w_ref[2:3, :] * u + w_ref[1:2, :] * u1 + w_ref[0:1, :] * u2
    return bg, cg, hc, u, u1, u2, conv


def _conv_fwd(bg, cg, hc, conv_w, gain, gmat, lp):
    n = bg.shape[0]
    nb = n // lp

    def body(bg_ref, cg_ref, hc_ref, w_ref, g_ref, gm_ref, z_ref):
        bgv, _, _, _, _, _, conv = _conv_parts(bg_ref, cg_ref, hc_ref, w_ref)
        yc = bgv * conv
        r = lax.rsqrt(_group_mean(yc * yc, gm_ref[...]) + EPS)
        z_ref[...] = (yc * r * g_ref[...]).astype(BF16)

    blk = pl.BlockSpec((lp, 128), lambda c, b: (b, c))
    return pl.pallas_call(
        body, name="conv_fwd", grid=(CONV_DIM // 128, nb),
        in_specs=[blk, blk, blk, pl.BlockSpec((3, 128), lambda c, b: (0, c)),
                  pl.BlockSpec((1, 128), lambda c, b: (0, c)), pl.BlockSpec((128, 128), lambda c, b: (0, 0))],
        out_specs=blk,
        out_shape=jax.ShapeDtypeStruct((n, CONV_DIM), BF16),
        compiler_params=_params(("parallel", "parallel")),
    )(bg, cg, hc, conv_w, gain, gmat)


def _conv_bwd(dz, bg, cg, hc, conv_w, gain, gmat, lp):
    n = bg.shape[0]
    nb = n // lp

    def body(dz_ref, bg_ref, cg_ref, hc_ref, w_ref, g_ref, gm_ref,
             dbg_ref, dcg_ref, dhc_ref, dw_ref, dgain_ref):
        b = pl.program_id(1)

        @pl.when(b == 0)
        def _():
            dw_ref[...] = jnp.zeros_like(dw_ref)
            dgain_ref[...] = jnp.zeros_like(dgain_ref)

        bgv, cgv, hcv, u, u1, u2, conv = _conv_parts(bg_ref, cg_ref, hc_ref, w_ref)
        gm = gm_ref[...]
        yc = bgv * conv
        r = lax.rsqrt(_group_mean(yc * yc, gm) + EPS)
        yhat = yc * r
        dzv = dz_ref[...].astype(F32)
        dyhat = dzv * g_ref[...]
        dgain_ref[...] += jnp.sum(dzv * yhat, axis=0, keepdims=True)
        dyc = r * (dyhat - yhat * _group_mean(dyhat * yhat, gm))
        dbg_ref[...] = (dyc * conv).astype(BF16)
        dconv = dyc * bgv
        du = (w_ref[2:3, :] * dconv + w_ref[1:2, :] * _shift_rows(dconv, -1)
              + w_ref[0:1, :] * _shift_rows(dconv, -2))
        dcg_ref[...] = (du * hcv).astype(BF16)
        dhc_ref[...] = (du * cgv).astype(BF16)
        dw_ref[0:1, :] += jnp.sum(dconv * u2, axis=0, keepdims=True)
        dw_ref[1:2, :] += jnp.sum(dconv * u1, axis=0, keepdims=True)
        dw_ref[2:3, :] += jnp.sum(dconv * u, axis=0, keepdims=True)

    blk = pl.BlockSpec((lp, 128), lambda c, b: (b, c))
    wspec = pl.BlockSpec((3, 128), lambda c, b: (0, c))
    gspec = pl.BlockSpec((1, 128), lambda c, b: (0, c))
    return pl.pallas_call(
        body, name="conv_bwd", grid=(CONV_DIM // 128, nb),
        in_specs=[blk, blk, blk, blk, wspec, gspec, pl.BlockSpec((128, 128), lambda c, b: (0, 0))],
        out_specs=[blk, blk, blk, wspec, gspec],
        out_shape=[jax.ShapeDtypeStruct((n, CONV_DIM), BF16)] * 3
        + [jax.ShapeDtypeStruct((3, CONV_DIM), F32), jax.ShapeDtypeStruct((1, CONV_DIM), F32)],
        compiler_params=_params(("parallel", "arbitrary")),
    )(dz, bg, cg, hc, conv_w, gain, gmat)


def _scan_steps(rows):
    s, out = 1, []
    while s < rows:
        out.append(s)
        s *= 2
    return out


def _fgate_fwd(fg, b_f, lp):
    n = fg.shape[0]
    nb = n // lp

    def body(fg_ref, b_ref, fc_ref, fr_ref):
        x = fg_ref[...] + b_ref[...]
        logf = jnp.minimum(x, 0.0) - jnp.log(1.0 + jnp.exp(-jnp.abs(x)))
        t = lax.broadcasted_iota(jnp.int32, (lp, 128), 0)
        lane = lax.broadcasted_iota(jnp.int32, (lp, 128), 1)
        f = jnp.where((t >= PAD) & (lane < N_HEADS), logf, 0.0)
        for s in _scan_steps(lp):
            f = f + _shift_rows(f, s)
        fc_ref[...] = f
        fr_ref[0] = f.T[0:N_HEADS, :]

    return pl.pallas_call(
        body, name="fgate_fwd", grid=(nb,),
        in_specs=[pl.BlockSpec((lp, 128), lambda b: (b, 0)), pl.BlockSpec((1, 128), lambda b: (0, 0))],
        out_specs=[pl.BlockSpec((lp, 128), lambda b: (b, 0)), pl.BlockSpec((1, N_HEADS, lp), lambda b: (b, 0, 0))],
        out_shape=[jax.ShapeDtypeStruct((n, 128), F32), jax.ShapeDtypeStruct((nb, N_HEADS, lp), F32)],
        compiler_params=_params(("parallel",)),
    )(fg, b_f)


def _fgate_bwd(dfc, dfr, fg, b_f, lp):
    n = fg.shape[0]
    nb = n // lp

    def body(dfc_ref, dfr_ref, fg_ref, b_ref, dfg_ref, db_ref):
        b = pl.program_id(0)

        @pl.when(b == 0)
        def _():
            db_ref[...] = jnp.zeros_like(db_ref)

        wide = jnp.concatenate([dfr_ref[0], jnp.zeros((128 - N_HEADS, lp), F32)], axis=0)
        d = dfc_ref[...] + wide.T
        for s in _scan_steps(lp):
            d = d + _shift_rows(d, -s)
        t = lax.broadcasted_iota(jnp.int32, (lp, 128), 0)
        lane = lax.broadcasted_iota(jnp.int32, (lp, 128), 1)
        x = fg_ref[...] + b_ref[...]
        dx = jnp.where((t >= PAD) & (lane < N_HEADS), d * _sigmoid(-x), 0.0)
        dfg_ref[...] = dx
        db_ref[...] += jnp.sum(dx, axis=0, keepdims=True)

    return pl.pallas_call(
        body, name="fgate_bwd", grid=(nb,),
        in_specs=[pl.BlockSpec((lp, 128), lambda b: (b, 0)), pl.BlockSpec((1, N_HEADS, lp), lambda b: (b, 0, 0)),
                  pl.BlockSpec((lp, 128), lambda b: (b, 0)), pl.BlockSpec((1, 128), lambda b: (0, 0))],
        out_specs=[pl.BlockSpec((lp, 128), lambda b: (b, 0)), pl.BlockSpec((1, 128), lambda b: (0, 0))],
        out_shape=[jax.ShapeDtypeStruct((n, 128), F32), jax.ShapeDtypeStruct((1, 128), F32)],
        compiler_params=_params(("arbitrary",)),
    )(dfc, dfr, fg, b_f)


def _head_masks():
    lane = lax.broadcasted_iota(jnp.int32, (1, 128), 1)
    return lane < HEAD_DIM


def _stack_heads(x2, first):
    zero = jnp.zeros_like(x2)
    return jnp.concatenate([jnp.where(first, x2, zero), jnp.where(first, zero, x2)], axis=0)


def _pair_cols(col0, col1, first):
    return jnp.where(first, col0, col1)


def _score_mask(i, j):
    r = lax.broadcasted_iota(jnp.int32, (2 * TQ, TQ), 0)
    c = lax.broadcasted_iota(jnp.int32, (2 * TQ, TQ), 1)
    qpos = i * TQ + (r & (TQ - 1))
    kpos = j * TQ + c
    return (kpos <= qpos) & (kpos >= PAD)


def _attn_fwd(q, k, v, fc, fr, gain, lp):
    n = q.shape[0]
    nb = n // lp
    nq = lp // TQ

    def body(q_ref, k_ref, v_ref, fc_ref, fr_ref, g_ref, z_ref, o_ref, lse_ref):
        i = pl.program_id(1)
        first = _head_masks()
        fcv = fc_ref[...]
        qs, fq = [], []
        for p in range(N_PAIRS):
            q2 = q_ref[:, 128 * p:128 * (p + 1)] * 0.125
            qs.append(_stack_heads(q2, first))
            fq.append(jnp.concatenate([fcv[:, 2 * p:2 * p + 1], fcv[:, 2 * p + 1:2 * p + 2]], axis=0))

        def step(j, carry):
            koff = pl.multiple_of(j * TQ, TQ)
            valid = _score_mask(i, j)
            new = []
            for p in range(N_PAIRS):
                m, l, acc = carry[p]
                k2 = k_ref[pl.ds(koff, TQ), 128 * p:128 * (p + 1)]
                v2 = v_ref[pl.ds(koff, TQ), 128 * p:128 * (p + 1)]
                fk = jnp.concatenate(
                    [jnp.broadcast_to(fr_ref[0, 2 * p:2 * p + 1, pl.ds(koff, TQ)], (TQ, TQ)),
                     jnp.broadcast_to(fr_ref[0, 2 * p + 1:2 * p + 2, pl.ds(koff, TQ)], (TQ, TQ))], axis=0)
                s = _dot_nt(qs[p], k2) + fq[p] - fk
                s = jnp.where(valid, s, NEG)
                m_new = jnp.maximum(m, jnp.max(s, axis=1, keepdims=True))
                pe = jnp.exp(s - m_new)
                alpha = jnp.exp(m - m_new)
                l = alpha * l + jnp.sum(pe, axis=1, keepdims=True)
                pb = pe.astype(BF16)
                pv = _dot(jnp.concatenate([pb[0:TQ], pb[TQ:]], axis=1), _stack_heads(v2, first))
                acc = acc * _pair_cols(alpha[0:TQ], alpha[TQ:], first) + pv
                new.append((m_new, l, acc))
            return tuple(new)

        init = tuple((jnp.full((2 * TQ, 1), NEG, F32), jnp.zeros((2 * TQ, 1), F32), jnp.zeros((TQ, 128), F32))
                     for _ in range(N_PAIRS))
        final = lax.fori_loop(0, i + 1, step, init)

        row = lax.broadcasted_iota(jnp.int32, (TQ, 128), 0)
        lane = lax.broadcasted_iota(jnp.int32, (TQ, 128), 1)
        real = (i * TQ + row) >= PAD
        lse_out = jnp.zeros((TQ, 128), F32)
        for p in range(N_PAIRS):
            m, l, acc = final[p]
            inv = 1.0 / l
            o = jnp.where(real, acc * _pair_cols(inv[0:TQ], inv[TQ:], first), 0.0)
            lse = m + jnp.log(l)
            lse_out = jnp.where(lane == 2 * p, lse[0:TQ], lse_out)
            lse_out = jnp.where(lane == 2 * p + 1, lse[TQ:], lse_out)
            sq = o * o
            ms0 = jnp.sum(jnp.where(first, sq, 0.0), axis=1, keepdims=True) * (1.0 / HEAD_DIM)
            ms1 = jnp.sum(jnp.where(first, 0.0, sq), axis=1, keepdims=True) * (1.0 / HEAD_DIM)
            r = _pair_cols(lax.rsqrt(ms0 + EPS), lax.rsqrt(ms1 + EPS), first)
            cols = slice(128 * p, 128 * (p + 1))
            o_ref[:, cols] = o.astype(BF16)
            z_ref[:, cols] = (o * r * g_ref[:, cols]).astype(BF16)
        lse_ref[...] = lse_out

    qblk = pl.BlockSpec((TQ, ATTN_DIM), lambda b, i: (b * nq + i, 0))
    seq = pl.BlockSpec((lp, ATTN_DIM), lambda b, i: (b, 0))
    colblk = pl.BlockSpec((TQ, 128), lambda b, i: (b * nq + i, 0))
    return pl.pallas_call(
        body, name="attn_fwd", grid=(nb, nq),
        in_specs=[qblk, seq, seq, colblk, pl.BlockSpec((1, N_HEADS, lp), lambda b, i: (b, 0, 0)),
                  pl.BlockSpec((1, ATTN_DIM), lambda b, i: (0, 0))],
        out_specs=[qblk, qblk, colblk],
        out_shape=[jax.ShapeDtypeStruct((n, ATTN_DIM), BF16), jax.ShapeDtypeStruct((n, ATTN_DIM), BF16),
                   jax.ShapeDtypeStruct((n, 128), F32)],
        compiler_params=_params(("parallel", "parallel")),
    )(q, k, v, fc, fr, gain)


def _attn_bwd(dz, q, k, v, fc, fr, o, lse, gain, lp):
    n = q.shape[0]
    nb = n // lp
    nq = lp // TQ

    def body(dz_ref, q_ref, k_ref, v_ref, fc_ref, fr_ref, o_ref, lse_ref, g_ref,
             dq_ref, dk_ref, dv_ref, dfc_ref, dfr_ref, dgain_ref):
        b = pl.program_id(0)
        i = pl.program_id(1)
        first = _head_masks()

        @pl.when((b == 0) & (i == 0))
        def _():
            dgain_ref[...] = jnp.zeros_like(dgain_ref)

        @pl.when(i == 0)
        def _():
            dk_ref[...] = jnp.zeros_like(dk_ref)
            dv_ref[...] = jnp.zeros_like(dv_ref)
            dfr_ref[...] = jnp.zeros_like(dfr_ref)

        fcv = fc_ref[...]
        lsev = lse_ref[...]
        qs, fq, lses, dos, deltas = [], [], [], [], []
        for p in range(N_PAIRS):
            cols = slice(128 * p, 128 * (p + 1))
            q2 = q_ref[:, cols] * 0.125
            qs.append(_stack_heads(q2, first))
            fq.append(jnp.concatenate([fcv[:, 2 * p:2 * p + 1], fcv[:, 2 * p + 1:2 * p + 2]], axis=0))
            lses.append(jnp.concatenate([lsev[:, 2 * p:2 * p + 1], lsev[:, 2 * p + 1:2 * p + 2]], axis=0))
            ov = o_ref[:, cols].astype(F32)
            dzv = dz_ref[:, cols].astype(F32)
            gv = g_ref[:, cols]
            sq = ov * ov
            ms0 = jnp.sum(jnp.where(first, sq, 0.0), axis=1, keepdims=True) * (1.0 / HEAD_DIM)
            ms1 = jnp.sum(jnp.where(first, 0.0, sq), axis=1, keepdims=True) * (1.0 / HEAD_DIM)
            r = _pair_cols(lax.rsqrt(ms0 + EPS), lax.rsqrt(ms1 + EPS), first)
            ohat = ov * r
            dyhat = dzv * gv
            dgain_ref[:, cols] += jnp.sum(dzv * ohat, axis=0, keepdims=True)
            pr = dyhat * ohat
            mean0 = jnp.sum(jnp.where(first, pr, 0.0), axis=1, keepdims=True) * (1.0 / HEAD_DIM)
            mean1 = jnp.sum(jnp.where(first, 0.0, pr), axis=1, keepdims=True) * (1.0 / HEAD_DIM)
            do = r * (dyhat - ohat * _pair_cols(mean0, mean1, first))
            dd = do * ov
            deltas.append(jnp.concatenate([jnp.sum(jnp.where(first, dd, 0.0), axis=1, keepdims=True),
                                           jnp.sum(jnp.where(first, 0.0, dd), axis=1, keepdims=True)], axis=0))
            dos.append(_stack_heads(do.astype(BF16), first))

        def step(j, carry):
            koff = pl.multiple_of(j * TQ, TQ)
            valid = _score_mask(i, j)
            new = []
            for p in range(N_PAIRS):
                dq_acc, dfq = carry[p]
                cols = slice(128 * p, 128 * (p + 1))
                k2 = k_ref[pl.ds(koff, TQ), cols]
                v2 = v_ref[pl.ds(koff, TQ), cols]
                fk = jnp.concatenate(
                    [jnp.broadcast_to(fr_ref[0, 2 * p:2 * p + 1, pl.ds(koff, TQ)], (TQ, TQ)),
                     jnp.broadcast_to(fr_ref[0, 2 * p + 1:2 * p + 2, pl.ds(koff, TQ)], (TQ, TQ))], axis=0)
                s = _dot_nt(qs[p], k2) + fq[p] - fk
                s = jnp.where(valid, s, NEG)
                pe = jnp.exp(s - lses[p])
                dp = _dot_nt(dos[p], v2)
                ds = pe * (dp - deltas[p])
                dfq = dfq + jnp.sum(ds, axis=1, keepdims=True)
                dfr_ref[0, 2 * p:2 * p + 1, pl.ds(koff, TQ)] -= jnp.sum(ds[0:TQ], axis=0, keepdims=True)
                dfr_ref[0, 2 * p + 1:2 * p + 2, pl.ds(koff, TQ)] -= jnp.sum(ds[TQ:], axis=0, keepdims=True)
                dsb = ds.astype(BF16)
                dq_acc = dq_acc + _dot(jnp.concatenate([dsb[0:TQ], dsb[TQ:]], axis=1), _stack_heads(k2, first))
                dk_ref[pl.ds(koff, TQ), cols] += _dot_tn(dsb, qs[p])
                dv_ref[pl.ds(koff, TQ), cols] += _dot_tn(pe.astype(BF16), dos[p])
                new.append((dq_acc, dfq))
            return tuple(new)

        init = tuple((jnp.zeros((TQ, 128), F32), jnp.zeros((2 * TQ, 1), F32)) for _ in range(N_PAIRS))
        final = lax.fori_loop(0, i + 1, step, init)

        lane = lax.broadcasted_iota(jnp.int32, (TQ, 128), 1)
        dfc_out = jnp.zeros((TQ, 128), F32)
        for p in range(N_PAIRS):
            dq_acc, dfq = final[p]
            dq_ref[:, 128 * p:128 * (p + 1)] = (dq_acc * 0.125).astype(BF16)
            dfc_out = jnp.where(lane == 2 * p, dfq[0:TQ], dfc_out)
            dfc_out = jnp.where(lane == 2 * p + 1, dfq[TQ:], dfc_out)
        dfc_ref[...] = dfc_out

    qblk = pl.BlockSpec((TQ, ATTN_DIM), lambda b, i: (b * nq + i, 0))
    seq = pl.BlockSpec((lp, ATTN_DIM), lambda b, i: (b, 0))
    colblk = pl.BlockSpec((TQ, 128), lambda b, i: (b * nq + i, 0))
    rowblk = pl.BlockSpec((1, N_HEADS, lp), lambda b, i: (b, 0, 0))
    gspec = pl.BlockSpec((1, ATTN_DIM), lambda b, i: (0, 0))
    return pl.pallas_call(
        body, name="attn_bwd", grid=(nb, nq),
        in_specs=[qblk, qblk, seq, seq, colblk, rowblk, qblk, colblk, gspec],
        out_specs=[qblk, seq, seq, colblk, rowblk, gspec],
        out_shape=[jax.ShapeDtypeStruct((n, ATTN_DIM), BF16), jax.ShapeDtypeStruct((n, ATTN_DIM), F32),
                   jax.ShapeDtypeStruct((n, ATTN_DIM), F32), jax.ShapeDtypeStruct((n, 128), F32),
                   jax.ShapeDtypeStruct((nb, N_HEADS, lp), F32), jax.ShapeDtypeStruct((1, ATTN_DIM), F32)],
        compiler_params=_params(("arbitrary", "arbitrary")),
    )(dz, q, k, v, fc, fr, o, lse, gain)


def _loss_head(h, gain, target, lp):
    n = h.shape[0]
    nb = n // lp
    nq = lp // 128

    def body(h_ref, g_ref, t_ref, loss_ref, dh_ref, dgain_ref):
        b = pl.program_id(0)
        i = pl.program_id(1)

        @pl.when((b == 0) & (i == 0))
        def _():
            loss_ref[...] = jnp.zeros_like(loss_ref)
            dgain_ref[...] = jnp.zeros_like(dgain_ref)

        @pl.when(i == 0)
        def _():
            dh_ref[...] = jnp.zeros_like(dh_ref)

        @pl.when(i > 0)
        def _():
            gain_v = g_ref[...]
            y, xhat, r = _rms(h_ref[...], gain_v)
            err = y - t_ref[...]
            loss_ref[...] += 0.5 * jnp.sum(jnp.sum(err * err, axis=1, keepdims=True), axis=0,
                                           keepdims=True) * (1.0 / D_MODEL)
            dy = err * (1.0 / D_MODEL)
            dh_ref[...] = _rms_bwd(dy, xhat, r, gain_v)
            dgain_ref[...] += jnp.sum(dy * xhat, axis=0, keepdims=True)

    rows = pl.BlockSpec((128, D_MODEL), lambda b, i: (b * nq + i, 0))
    trows = pl.BlockSpec((128, D_MODEL), lambda b, i: (b * (nq - 1) + jnp.maximum(i, 1) - 1, 0))
    return pl.pallas_call(
        body, name="loss_head", grid=(nb, nq),
        in_specs=[rows, pl.BlockSpec((1, D_MODEL), lambda b, i: (0, 0)), trows],
        out_specs=[pl.BlockSpec((1, 1), lambda b, i: (0, 0)), rows, pl.BlockSpec((1, D_MODEL), lambda b, i: (0, 0))],
        out_shape=[jax.ShapeDtypeStruct((1, 1), F32), jax.ShapeDtypeStruct((n, D_MODEL), F32),
                   jax.ShapeDtypeStruct((1, D_MODEL), F32)],
        compiler_params=_params(("arbitrary", "arbitrary")),
    )(h, gain, target)


def _place():
    return lax.axis_index("x"), lax.axis_index("y"), lax.axis_index("c")


def _all_gather(xs, name):
    nw = len(xs)

    def body(*refs):
        ins, outs = refs[:nw], refs[nw:2 * nw]
        send_sems, recv_sems, local_sems = refs[2 * nw:]
        x, y, c = _place()
        me, sibling = (x, y, c), (x, y, 1 - c)
        chips = [(1 - x, y), (x, 1 - y), (1 - x, 1 - y)]

        def copy(w, k, block, to, src=None):
            slot = outs[w].at[4 * block[0] + 2 * block[1] + block[2]]
            return pltpu.make_async_remote_copy(
                src_ref=slot if src is None else src, dst_ref=slot,
                send_sem=send_sems.at[w, k], recv_sem=recv_sems.at[w, k], device_id=to, device_id_type=MESH)

        started = []
        for w in range(nw):
            mine = pltpu.make_async_copy(ins[w], outs[w].at[4 * x + 2 * y + c], local_sems.at[w])
            mine.start()
            started.append(mine)
        sends = []
        for w in range(nw):
            sends.append(copy(w, 0, me, sibling, src=ins[w]))
            sends += [copy(w, 1 + j, me, (*chip, c), src=ins[w]) for j, chip in enumerate(chips)]
        for cp in sends:
            cp.start()
        for w in range(nw):
            for j, chip in enumerate(chips):
                copy(w, 1 + j, (*chip, c), me).wait_recv()
                passed = copy(w, 4 + j, (*chip, c), sibling)
                passed.start()
                sends.append(passed)
        for w in range(nw):
            copy(w, 0, sibling, me).wait_recv()
            for j, chip in enumerate(chips):
                copy(w, 4 + j, (*chip, 1 - c), me).wait_recv()
        for cp in sends:
            cp.wait_send()
        for mine in started:
            mine.wait()

    return pl.pallas_call(
        body, name=name,
        in_specs=[ANY] * nw, out_specs=[ANY] * nw,
        out_shape=[jax.ShapeDtypeStruct((N_DEV,) + a.shape, a.dtype) for a in xs],
        scratch_shapes=[pltpu.SemaphoreType.DMA((nw, 7)), pltpu.SemaphoreType.DMA((nw, 7)),
                        pltpu.SemaphoreType.DMA((nw,))],
    )(*xs)


def _exchange_partials(xs, name):
    nw = len(xs)

    def body(*refs):
        ins, outs = refs[:nw], refs[nw:2 * nw]
        send_sems, recv_sems, local_sems = refs[2 * nw:]
        x, y, c = _place()
        me = 4 * x + 2 * y + c

        def flip(v, bit):
            return 1 - v if bit else v

        def copy(w, k):
            bx, by, bc = ((k + 1) >> 2) & 1, ((k + 1) >> 1) & 1, (k + 1) & 1
            peer = (flip(x, bx), flip(y, by), flip(c, bc))
            peer_idx = 4 * peer[0] + 2 * peer[1] + peer[2]
            return pltpu.make_async_remote_copy(
                src_ref=ins[w].at[peer_idx], dst_ref=outs[w].at[me],
                send_sem=send_sems.at[w, k], recv_sem=recv_sems.at[w, k], device_id=peer, device_id_type=MESH)

        local = [pltpu.make_async_copy(ins[w].at[me], outs[w].at[me], local_sems.at[w]) for w in range(nw)]
        for cp in local:
            cp.start()
        remote = [copy(w, k) for w in range(nw) for k in range(N_DEV - 1)]
        for cp in remote:
            cp.start()
        for cp in remote:
            cp.wait()
        for cp in local:
            cp.wait()

    return pl.pallas_call(
        body, name=name,
        in_specs=[ANY] * nw, out_specs=[ANY] * nw,
        out_shape=[jax.ShapeDtypeStruct(a.shape, a.dtype) for a in xs],
        scratch_shapes=[pltpu.SemaphoreType.DMA((nw, 7)), pltpu.SemaphoreType.DMA((nw, 7)),
                        pltpu.SemaphoreType.DMA((nw,))],
    )(*xs)


def _adamw(parts, w, m, v, name):
    s_parts, r, c = parts.shape
    tr = r
    for t in (256, 128, 64, 32, 16):
        if r % t == 0 and r > t:
            tr = t
            break

    def body(p_ref, w_ref, m_ref, v_ref, g_ref, d_ref, nm_ref, nv_ref):
        g = p_ref[0].astype(F32)
        for s in range(1, s_parts):
            g = g + p_ref[s].astype(F32)
        nm = ADAM_B1 * m_ref[...] + (1.0 - ADAM_B1) * g
        nv = ADAM_B2 * v_ref[...] + (1.0 - ADAM_B2) * (g * g)
        m_hat = nm / (1.0 - ADAM_B1 ** ADAM_STEP)
        v_hat = nv / (1.0 - ADAM_B2 ** ADAM_STEP)
        g_ref[...] = g
        d_ref[...] = -ADAM_LR * (m_hat / (jnp.sqrt(v_hat) + ADAM_EPS) + ADAM_WD * w_ref[...])
        nm_ref[...] = nm
        nv_ref[...] = nv

    blk = pl.BlockSpec((tr, c), lambda i: (i, 0))
    return pl.pallas_call(
        body, name=name, grid=(r // tr,),
        in_specs=[pl.BlockSpec((s_parts, tr, c), lambda i: (0, i, 0)), blk, blk, blk],
        out_specs=[blk] * 4,
        out_shape=[jax.ShapeDtypeStruct((r, c), F32)] * 4,
        compiler_params=_params(("parallel",)),
    )(parts, w, m, v)


def _sum_parts(parts, name):
    s_parts, r, c = parts.shape

    def body(p_ref, out_ref):
        acc = p_ref[0]
        for s in range(1, s_parts):
            acc = acc + p_ref[s]
        out_ref[...] = acc

    return pl.pallas_call(
        body, name=name, out_shape=jax.ShapeDtypeStruct((r, c), F32),
        in_specs=[pl.BlockSpec(memory_space=pltpu.VMEM)], out_specs=pl.BlockSpec(memory_space=pltpu.VMEM),
    )(parts)


SMALL_ROWS = 184


def _pack_small(d_gains, d_gc, d_ga, d_bf, d_conv, d_meta):
    rows = [g.reshape(8, 128) for g in d_gains]
    rows += [d_gc.reshape(4, 128), d_ga.reshape(4, 128), d_bf.reshape(1, 128)]
    rows += [d_conv.reshape(12, 128), d_meta.reshape(128, 128)]
    packed = jnp.concatenate(rows, axis=0)
    return jnp.pad(packed, ((0, SMALL_ROWS - packed.shape[0]), (0, 0)))


def kernel(x, meta_tokens, ffn1_norm, ffn1_w_gu, ffn1_w_down, mix_norm, w_in, conv_w, b_f, out_norm_conv, out_norm_attn, w_out, ffn2_norm, ffn2_w_gu, ffn2_w_down, final_norm, loss_target, m_meta_tokens, m_ffn1_norm, m_ffn1_w_gu, m_ffn1_w_down, m_mix_norm, m_w_in, m_conv_w, m_b_f, m_out_norm_conv, m_out_norm_attn, m_w_out, m_ffn2_norm, m_ffn2_w_gu, m_ffn2_w_down, m_final_norm, v_meta_tokens, v_ffn1_norm, v_ffn1_w_gu, v_ffn1_w_down, v_mix_norm, v_w_in, v_conv_w, v_b_f, v_out_norm_conv, v_out_norm_attn, v_w_out, v_ffn2_norm, v_ffn2_w_gu, v_ffn2_w_down, v_final_norm):
    nb, seq, _ = x.shape
    lp = PAD + N_META + seq
    n = nb * lp
    me = 4 * lax.axis_index("x") + 2 * lax.axis_index("y") + lax.axis_index("c")

    big = [ffn1_w_gu[0], ffn1_w_down[0], w_in[0], w_out[0], ffn2_w_gu[0], ffn2_w_down[0]]
    wgu1_8, wd1_8, win_8, wout_8, wgu2_8, wd2_8 = _all_gather([a.astype(BF16) for a in big], "gather_weights")
    small_in = jnp.concatenate(
        [meta_tokens, jnp.pad(conv_w[0], ((0, 0), (0, 128 - conv_w.shape[2]))), jnp.zeros((5, 128), F32)], axis=0)
    (small_8,) = _all_gather([small_in], "gather_small")
    meta_full = small_8[:, 0:N_META, :].transpose(1, 0, 2).reshape(N_META, D_MODEL)
    conv_full = small_8[:, N_META:N_META + 3, 0:CONV_DIM // N_DEV].transpose(1, 0, 2).reshape(3, CONV_DIM)

    wgu1 = wgu1_8.reshape(2, N_CHUNK, D_MODEL, F_CHUNK)
    wgu2 = wgu2_8.reshape(2, N_CHUNK, D_MODEL, F_CHUNK)
    wd1 = wd1_8.reshape(N_CHUNK, F_CHUNK, D_MODEL)
    wd2 = wd2_8.reshape(N_CHUNK, F_CHUNK, D_MODEL)
    w_in_full = jnp.pad(win_8.transpose(1, 0, 2).reshape(D_MODEL, IN_DIM), ((0, 0), (0, IN_PAD - IN_DIM)))
    w_out_full = wout_8.reshape(D_MODEL, D_MODEL)
    b_f_row = jnp.pad(b_f, ((0, 0), (0, 128 - N_HEADS)))
    gmat = _group_matrix()

    h0 = jnp.concatenate([jnp.zeros((nb, PAD, D_MODEL), F32),
                          jnp.broadcast_to(meta_full[None], (nb, N_META, D_MODEL)), x], axis=1).reshape(n, D_MODEL)
    h1, gate1, up1 = _ffn_fwd(h0, ffn1_norm, wgu1, wd1, "ffn1_fwd")
    bg, cg, hc, q, k, v, fg = _inproj_fwd(h1, mix_norm, w_in_full)
    zc = _conv_fwd(bg, cg, hc, conv_full, out_norm_conv, gmat, lp)
    fc, fr = _fgate_fwd(fg, b_f_row, lp)
    za, o, lse = _attn_fwd(q, k, v, fc, fr, out_norm_attn, lp)
    h2 = _outproj_fwd(zc, za, w_out_full, h1)
    h3, gate2, up2 = _ffn_fwd(h2, ffn2_norm, wgu2, wd2, "ffn2_fwd")
    loss_part, dh3, d_final = _loss_head(h3, final_norm.reshape(1, D_MODEL), loss_target.reshape(nb * seq, D_MODEL), lp)

    dh2, dgate2, dup2, d_ffn2 = _ffn_bwd_x(dh3, h2, ffn2_norm, gate2, up2, wgu2, wd2, "ffn2_bwd_x")
    dwgu2, dwd2 = _ffn_bwd_w(dh3, h2, ffn2_norm, gate2, up2, dgate2, dup2, "ffn2_bwd_w")
    dzc, dza, dwout = _outproj_bwd(dh2, zc, za, w_out_full)
    dq, dk, dv, dfc, dfr, d_ga = _attn_bwd(dza, q, k, v, fc, fr, o, lse, out_norm_attn, lp)
    dfg, d_bf = _fgate_bwd(dfc, dfr, fg, b_f_row, lp)
    dbg, dcg, dhc, d_conv, d_gc = _conv_bwd(dzc, bg, cg, hc, conv_full, out_norm_conv, gmat, lp)
    dh1, dwin, d_mix = _inproj_bwd([dbg, dcg, dhc, dq, dk, dv], dfg, dh2, h1, mix_norm, w_in_full)
    dh0, dgate1, dup1, d_ffn1 = _ffn_bwd_x(dh1, h0, ffn1_norm, gate1, up1, wgu1, wd1, "ffn1_bwd_x")
    dwgu1, dwd1 = _ffn_bwd_w(dh1, h0, ffn1_norm, gate1, up1, dgate1, dup1, "ffn1_bwd_w")

    dh0 = dh0.reshape(nb, lp, D_MODEL)
    grad_x = dh0[:, PAD + N_META:, :]
    d_meta = jnp.sum(dh0[:, PAD:PAD + N_META, :], axis=0)

    dwin_8 = dwin[:, 0:IN_DIM].reshape(D_MODEL, N_DEV, IN_DIM // N_DEV).transpose(1, 0, 2)
    send = [dwgu1.reshape(N_DEV, D_MODEL, F_CHUNK), dwd1.reshape(N_DEV, F_CHUNK // 2, D_MODEL), dwin_8,
            dwout.reshape(N_DEV, D_MODEL // N_DEV, D_MODEL),
            dwgu2.reshape(N_DEV, D_MODEL, F_CHUNK), dwd2.reshape(N_DEV, F_CHUNK // 2, D_MODEL)]
    parts = _exchange_partials(send, "exchange_grads")

    small = _pack_small([d_ffn1, d_mix, d_ffn2, d_final], d_gc, d_ga, d_bf, d_conv, d_meta)
    (small_all,) = _all_gather([small], "gather_small_grads")
    small_sum = _sum_parts(small_all, "sum_small_grads")
    g_ffn1n, g_mixn, g_ffn2n, g_finaln = (small_sum[8 * t:8 * t + 8].reshape(1, D_MODEL) for t in range(4))
    g_gc = small_sum[32:36].reshape(1, CONV_DIM)
    g_ga = small_sum[36:40].reshape(1, ATTN_DIM)
    g_bf = small_sum[40:41, 0:N_HEADS]
    g_conv_full = small_sum[41:53].reshape(3, CONV_DIM)
    g_meta_full = small_sum[53:181].reshape(N_META, D_MODEL)
    g_conv = lax.dynamic_slice_in_dim(g_conv_full, me * (CONV_DIM // N_DEV), CONV_DIM // N_DEV, axis=1)
    g_meta = lax.dynamic_slice_in_dim(g_meta_full, me * (D_MODEL // N_DEV), D_MODEL // N_DEV, axis=1)

    weights = {
        "meta_tokens": (g_meta[None], meta_tokens, m_meta_tokens, v_meta_tokens),
        "ffn1_norm": (g_ffn1n[None], ffn1_norm, m_ffn1_norm, v_ffn1_norm),
        "ffn1_w_gu": (parts[0], ffn1_w_gu[0], m_ffn1_w_gu[0], v_ffn1_w_gu[0]),
        "ffn1_w_down": (parts[1], ffn1_w_down[0], m_ffn1_w_down[0], v_ffn1_w_down[0]),
        "mix_norm": (g_mixn[None], mix_norm, m_mix_norm, v_mix_norm),
        "w_in": (parts[2], w_in[0], m_w_in[0], v_w_in[0]),
        "conv_w": (g_conv[None], conv_w[0], m_conv_w[0], v_conv_w[0]),
        "b_f": (g_bf[None], b_f, m_b_f, v_b_f),
        "out_norm_conv": (g_gc[None], out_norm_conv, m_out_norm_conv, v_out_norm_conv),
        "out_norm_attn": (g_ga[None], out_norm_attn, m_out_norm_attn, v_out_norm_attn),
        "w_out": (parts[3], w_out[0], m_w_out[0], v_w_out[0]),
        "ffn2_norm": (g_ffn2n[None], ffn2_norm, m_ffn2_norm, v_ffn2_norm),
        "ffn2_w_gu": (parts[4], ffn2_w_gu[0], m_ffn2_w_gu[0], v_ffn2_w_gu[0]),
        "ffn2_w_down": (parts[5], ffn2_w_down[0], m_ffn2_w_down[0], v_ffn2_w_down[0]),
        "final_norm": (g_finaln[None], final_norm.reshape(1, D_MODEL), m_final_norm.reshape(1, D_MODEL),
                       v_final_norm.reshape(1, D_MODEL)),
    }
    shapes = {"meta_tokens": meta_tokens.shape, "ffn1_norm": ffn1_norm.shape, "ffn1_w_gu": ffn1_w_gu.shape,
              "ffn1_w_down": ffn1_w_down.shape, "mix_norm": mix_norm.shape, "w_in": w_in.shape,
              "conv_w": conv_w.shape, "b_f": b_f.shape, "out_norm_conv": out_norm_conv.shape,
              "out_norm_attn": out_norm_attn.shape, "w_out": w_out.shape, "ffn2_norm": ffn2_norm.shape,
              "ffn2_w_gu": ffn2_w_gu.shape, "ffn2_w_down": ffn2_w_down.shape, "final_norm": final_norm.shape}
    grads, deltas, new_m, new_v = [], [], [], []
    for name, (p, w, m, vv) in weights.items():
        g, d, nm, nv = _adamw(p, w, m, vv, "adamw_" + name)
        shape = shapes[name]
        grads.append(g.reshape(shape))
        deltas.append(d.reshape(shape))
        new_m.append(nm.reshape(shape))
        new_v.append(nv.reshape(shape))

    loss = lax.psum(loss_part[0, 0], ("x", "y", "c"))
    return (loss, grad_x, *grads, *deltas, *new_m, *new_v)
```

```python
import jax
import jax.numpy as jnp
from jax import lax
from jax.experimental import pallas as pl
from jax.experimental.pallas import tpu as pltpu

F32 = jnp.float32
BF16 = jnp.bfloat16

N_DEV = 8
D_MODEL = 1024
N_META = 16
PAD = 128 - N_META
CONV_DIM = 512
ATTN_DIM = 512
HEAD_DIM = 64
N_HEADS = 8
N_PAIRS = N_HEADS // 2
D_FF = 2816
N_CHUNK = 4
F_CHUNK = D_FF // N_CHUNK
IN_DIM = 3080
IN_PAD = 3200
IN_MAIN = 3072
EPS = 1e-6
NEG = -1e30
TQ = 128
VMEM_LIMIT = 56 * 1024 * 1024

ADAM_LR = 0.001
ADAM_B1 = 0.9
ADAM_B2 = 0.999
ADAM_EPS = 1e-08
ADAM_WD = 0.01
ADAM_STEP = 10

MESH = pl.DeviceIdType.MESH
ANY = pl.BlockSpec(memory_space=pl.ANY)


def _params(sem=None):
    return pltpu.CompilerParams(dimension_semantics=sem, vmem_limit_bytes=VMEM_LIMIT)


def _row_tile(n, prefer):
    for t in (prefer, 512, 256, 128):
        if t <= n and n % t == 0:
            return t
    raise ValueError(f"no row tile for {n}")


def _dot(a, b):
    return jnp.dot(a, b, preferred_element_type=F32)


def _dot_nt(a, b):
    return lax.dot_general(a, b, (((1,), (1,)), ((), ())), preferred_element_type=F32)


def _dot_tn(a, b):
    return lax.dot_general(a, b, (((0,), (0,)), ((), ())), preferred_element_type=F32)


def _rms(x, g):
    r = lax.rsqrt(jnp.mean(x * x, axis=-1, keepdims=True) + EPS)
    xhat = x * r
    return xhat * g, xhat, r


def _rms_bwd(dn, xhat, r, g):
    dxhat = dn * g
    return r * (dxhat - xhat * jnp.mean(dxhat * xhat, axis=-1, keepdims=True))


def _sigmoid(x):
    return 1.0 / (1.0 + jnp.exp(-x))


def _ffn_fwd(h, gain, wgu, wd, name):
    n = h.shape[0]
    tm = _row_tile(n, 512)

    def body(h_ref, g_ref, wgu_ref, wd_ref, out_ref, gate_ref, up_ref, n_scr, acc_scr):
        j = pl.program_id(1)

        @pl.when(j == 0)
        def _():
            y, _, _ = _rms(h_ref[...], g_ref[...])
            n_scr[...] = y.astype(BF16)
            acc_scr[...] = jnp.zeros_like(acc_scr)

        nb = n_scr[...]
        gate = _dot(nb, wgu_ref[0, 0])
        up = _dot(nb, wgu_ref[1, 0])
        gate_ref[0] = gate.astype(BF16)
        up_ref[0] = up.astype(BF16)
        act = (gate * _sigmoid(gate) * up).astype(BF16)
        acc_scr[...] += _dot(act, wd_ref[0])

        @pl.when(j == N_CHUNK - 1)
        def _():
            out_ref[...] = h_ref[...] + 0.5 * acc_scr[...]

    return pl.pallas_call(
        body, name=name, grid=(n // tm, N_CHUNK),
        in_specs=[pl.BlockSpec((tm, D_MODEL), lambda i, j: (i, 0)),
                  pl.BlockSpec((1, D_MODEL), lambda i, j: (0, 0)),
                  pl.BlockSpec((2, 1, D_MODEL, F_CHUNK), lambda i, j: (0, j, 0, 0)),
                  pl.BlockSpec((1, F_CHUNK, D_MODEL), lambda i, j: (j, 0, 0))],
        out_specs=[pl.BlockSpec((tm, D_MODEL), lambda i, j: (i, 0)),
                   pl.BlockSpec((1, tm, F_CHUNK), lambda i, j: (j, i, 0)),
                   pl.BlockSpec((1, tm, F_CHUNK), lambda i, j: (j, i, 0))],
        out_shape=[jax.ShapeDtypeStruct((n, D_MODEL), F32),
                   jax.ShapeDtypeStruct((N_CHUNK, n, F_CHUNK), BF16),
                   jax.ShapeDtypeStruct((N_CHUNK, n, F_CHUNK), BF16)],
        scratch_shapes=[pltpu.VMEM((tm, D_MODEL), BF16), pltpu.VMEM((tm, D_MODEL), F32)],
        compiler_params=_params(("parallel", "arbitrary")),
    )(h, gain, wgu, wd)


def _ffn_bwd_x(dh_out, h_in, gain, gate, up, wgu, wd, name):
    n = h_in.shape[0]
    tm = _row_tile(n, 512)

    def body(dh_ref, h_ref, g_ref, gate_ref, up_ref, wgu_ref, wd_ref,
             dhin_ref, dgate_ref, dup_ref, dgain_ref, dhb_scr, acc_scr):
        i = pl.program_id(0)
        j = pl.program_id(1)

        @pl.when((i == 0) & (j == 0))
        def _():
            dgain_ref[...] = jnp.zeros_like(dgain_ref)

        @pl.when(j == 0)
        def _():
            dhb_scr[...] = (0.5 * dh_ref[...]).astype(BF16)
            acc_scr[...] = jnp.zeros_like(acc_scr)

        da = _dot_nt(dhb_scr[...], wd_ref[0])
        g = gate_ref[0].astype(F32)
        u = up_ref[0].astype(F32)
        sig = _sigmoid(g)
        dgate = (da * u * (sig * (1.0 + g * (1.0 - sig)))).astype(BF16)
        dup = (da * (g * sig)).astype(BF16)
        dgate_ref[0] = dgate
        dup_ref[0] = dup
        acc_scr[...] += _dot_nt(dgate, wgu_ref[0, 0]) + _dot_nt(dup, wgu_ref[1, 0])

        @pl.when(j == N_CHUNK - 1)
        def _():
            gain_v = g_ref[...]
            _, xhat, r = _rms(h_ref[...], gain_v)
            dn = acc_scr[...]
            dhin_ref[...] = dh_ref[...] + _rms_bwd(dn, xhat, r, gain_v)
            dgain_ref[...] += jnp.sum(dn * xhat, axis=0, keepdims=True)

    chunk = pl.BlockSpec((1, tm, F_CHUNK), lambda i, j: (j, i, 0))
    rows = pl.BlockSpec((tm, D_MODEL), lambda i, j: (i, 0))
    vec = pl.BlockSpec((1, D_MODEL), lambda i, j: (0, 0))
    return pl.pallas_call(
        body, name=name, grid=(n // tm, N_CHUNK),
        in_specs=[rows, rows, vec, chunk, chunk,
                  pl.BlockSpec((2, 1, D_MODEL, F_CHUNK), lambda i, j: (0, j, 0, 0)),
                  pl.BlockSpec((1, F_CHUNK, D_MODEL), lambda i, j: (j, 0, 0))],
        out_specs=[rows, chunk, chunk, vec],
        out_shape=[jax.ShapeDtypeStruct((n, D_MODEL), F32),
                   jax.ShapeDtypeStruct((N_CHUNK, n, F_CHUNK), BF16),
                   jax.ShapeDtypeStruct((N_CHUNK, n, F_CHUNK), BF16),
                   jax.ShapeDtypeStruct((1, D_MODEL), F32)],
        scratch_shapes=[pltpu.VMEM((tm, D_MODEL), BF16), pltpu.VMEM((tm, D_MODEL), F32)],
        compiler_params=_params(("arbitrary", "arbitrary")),
    )(dh_out, h_in, gain, gate, up, wgu, wd)


def _ffn_bwd_w(dh_out, h_in, gain, gate, up, dgate, dup, name):
    n = h_in.shape[0]
    tm = _row_tile(n, 512)
    n_i = n // tm

    def body(dh_ref, h_ref, g_ref, gate_ref, up_ref, dgate_ref, dup_ref, dwgu_ref, dwd_ref,
             ag_scr, au_scr, ad_scr):
        i = pl.program_id(1)

        @pl.when(i == 0)
        def _():
            ag_scr[...] = jnp.zeros_like(ag_scr)
            au_scr[...] = jnp.zeros_like(au_scr)
            ad_scr[...] = jnp.zeros_like(ad_scr)

        y, _, _ = _rms(h_ref[...], g_ref[...])
        nb = y.astype(BF16)
        ag_scr[...] += _dot_tn(nb, dgate_ref[0])
        au_scr[...] += _dot_tn(nb, dup_ref[0])
        g = gate_ref[0].astype(F32)
        act = (g * _sigmoid(g) * up_ref[0].astype(F32)).astype(BF16)
        ad_scr[...] += _dot_tn(act, (0.5 * dh_ref[...]).astype(BF16))

        @pl.when(i == n_i - 1)
        def _():
            dwgu_ref[0, 0] = ag_scr[...].astype(BF16)
            dwgu_ref[1, 0] = au_scr[...].astype(BF16)
            dwd_ref[0] = ad_scr[...].astype(BF16)

    chunk = pl.BlockSpec((1, tm, F_CHUNK), lambda j, i: (j, i, 0))
    rows = pl.BlockSpec((tm, D_MODEL), lambda j, i: (i, 0))
    return pl.pallas_call(
        body, name=name, grid=(N_CHUNK, n_i),
        in_specs=[rows, rows, pl.BlockSpec((1, D_MODEL), lambda j, i: (0, 0)), chunk, chunk, chunk, chunk],
        out_specs=[pl.BlockSpec((2, 1, D_MODEL, F_CHUNK), lambda j, i: (0, j, 0, 0)),
                   pl.BlockSpec((1, F_CHUNK, D_MODEL), lambda j, i: (j, 0, 0))],
        out_shape=[jax.ShapeDtypeStruct((2, N_CHUNK, D_MODEL, F_CHUNK), BF16),
                   jax.ShapeDtypeStruct((N_CHUNK, F_CHUNK, D_MODEL), BF16)],
        scratch_shapes=[pltpu.VMEM((D_MODEL, F_CHUNK), F32), pltpu.VMEM((D_MODEL, F_CHUNK), F32),
                        pltpu.VMEM((F_CHUNK, D_MODEL), F32)],
        compiler_params=_params(("parallel", "arbitrary")),
    )(dh_out, h_in, gain, gate, up, dgate, dup)


N_PIECE = IN_MAIN // 512


def _inproj_fwd(h, gain, w_in):
    n = h.shape[0]
    tm = _row_tile(n, 512)

    def body(h_ref, g_ref, w_ref, *outs):
        y, _, _ = _rms(h_ref[...], g_ref[...])
        nb = y.astype(BF16)
        for p in range(N_PIECE):
            outs[p][...] = _dot(nb, w_ref[:, 512 * p:512 * (p + 1)]).astype(BF16)
        outs[N_PIECE][...] = _dot(nb, w_ref[:, IN_MAIN:IN_PAD])

    piece = pl.BlockSpec((tm, 512), lambda i: (i, 0))
    return pl.pallas_call(
        body, name="inproj_fwd", grid=(n // tm,),
        in_specs=[pl.BlockSpec((tm, D_MODEL), lambda i: (i, 0)),
                  pl.BlockSpec((1, D_MODEL), lambda i: (0, 0)),
                  pl.BlockSpec((D_MODEL, IN_PAD), lambda i: (0, 0))],
        out_specs=[piece] * N_PIECE + [pl.BlockSpec((tm, 128), lambda i: (i, 0))],
        out_shape=[jax.ShapeDtypeStruct((n, 512), BF16)] * N_PIECE + [jax.ShapeDtypeStruct((n, 128), F32)],
        compiler_params=_params(("parallel",)),
    )(h, gain, w_in)


def _inproj_bwd(dpieces, dfg, dh_out, h_in, gain, w_in):
    n = h_in.shape[0]
    tm = _row_tile(n, 512)
    n_i = n // tm

    def body(*refs):
        dp_refs = refs[:N_PIECE]
        dfg_ref, dh_ref, h_ref, g_ref, w_ref, dhin_ref, dw_ref, dgain_ref, acc_scr = refs[N_PIECE:]
        i = pl.program_id(0)

        @pl.when(i == 0)
        def _():
            acc_scr[...] = jnp.zeros_like(acc_scr)
            dgain_ref[...] = jnp.zeros_like(dgain_ref)

        gain_v = g_ref[...]
        y, xhat, r = _rms(h_ref[...], gain_v)
        nb = y.astype(BF16)
        dn = jnp.zeros((tm, D_MODEL), F32)
        for p in range(N_PIECE + 1):
            lo, hi = (512 * p, 512 * (p + 1)) if p < N_PIECE else (IN_MAIN, IN_PAD)
            dp = (dp_refs[p][...] if p < N_PIECE else dfg_ref[...]).astype(BF16)
            dn = dn + _dot_nt(dp, w_ref[:, lo:hi])
            acc_scr[:, lo:hi] += _dot_tn(nb, dp)
        dhin_ref[...] = dh_ref[...] + _rms_bwd(dn, xhat, r, gain_v)
        dgain_ref[...] += jnp.sum(dn * xhat, axis=0, keepdims=True)

        @pl.when(i == n_i - 1)
        def _():
            dw_ref[...] = acc_scr[...].astype(BF16)

    piece = pl.BlockSpec((tm, 512), lambda i: (i, 0))
    rows = pl.BlockSpec((tm, D_MODEL), lambda i: (i, 0))
    vec = pl.BlockSpec((1, D_MODEL), lambda i: (0, 0))
    wspec = pl.BlockSpec((D_MODEL, IN_PAD), lambda i: (0, 0))
    return pl.pallas_call(
        body, name="inproj_bwd", grid=(n_i,),
        in_specs=[piece] * N_PIECE + [pl.BlockSpec((tm, 128), lambda i: (i, 0)), rows, rows, vec, wspec],
        out_specs=[rows, wspec, vec],
        out_shape=[jax.ShapeDtypeStruct((n, D_MODEL), F32),
                   jax.ShapeDtypeStruct((D_MODEL, IN_PAD), BF16),
                   jax.ShapeDtypeStruct((1, D_MODEL), F32)],
        scratch_shapes=[pltpu.VMEM((D_MODEL, IN_PAD), F32)],
        compiler_params=_params(("arbitrary",)),
    )(*dpieces, dfg, dh_out, h_in, gain, w_in)


def _outproj_fwd(zc, za, w_out, h):
    n = h.shape[0]
    tm = _row_tile(n, 512)

    def body(zc_ref, za_ref, w_ref, h_ref, out_ref):
        out_ref[...] = (h_ref[...] + _dot(zc_ref[...], w_ref[0:CONV_DIM, :])
                        + _dot(za_ref[...], w_ref[CONV_DIM:, :]))

    half = pl.BlockSpec((tm, 512), lambda i: (i, 0))
    rows = pl.BlockSpec((tm, D_MODEL), lambda i: (i, 0))
    return pl.pallas_call(
        body, name="outproj_fwd", grid=(n // tm,),
        in_specs=[half, half, pl.BlockSpec((D_MODEL, D_MODEL), lambda i: (0, 0)), rows],
        out_specs=rows,
        out_shape=jax.ShapeDtypeStruct((n, D_MODEL), F32),
        compiler_params=_params(("parallel",)),
    )(zc, za, w_out, h)


def _outproj_bwd(dh, zc, za, w_out):
    n = dh.shape[0]
    tm = _row_tile(n, 512)
    n_i = n // tm

    def body(dh_ref, zc_ref, za_ref, w_ref, dzc_ref, dza_ref, dw_ref, acc_scr):
        i = pl.program_id(0)

        @pl.when(i == 0)
        def _():
            acc_scr[...] = jnp.zeros_like(acc_scr)

        dhb = dh_ref[...].astype(BF16)
        dzc_ref[...] = _dot_nt(dhb, w_ref[0:CONV_DIM, :]).astype(BF16)
        dza_ref[...] = _dot_nt(dhb, w_ref[CONV_DIM:, :]).astype(BF16)
        acc_scr[0:CONV_DIM, :] += _dot_tn(zc_ref[...], dhb)
        acc_scr[CONV_DIM:, :] += _dot_tn(za_ref[...], dhb)

        @pl.when(i == n_i - 1)
        def _():
            dw_ref[...] = acc_scr[...].astype(BF16)

    half = pl.BlockSpec((tm, 512), lambda i: (i, 0))
    wspec = pl.BlockSpec((D_MODEL, D_MODEL), lambda i: (0, 0))
    return pl.pallas_call(
        body, name="outproj_bwd", grid=(n_i,),
        in_specs=[pl.BlockSpec((tm, D_MODEL), lambda i: (i, 0)), half, half, wspec],
        out_specs=[half, half, wspec],
        out_shape=[jax.ShapeDtypeStruct((n, 512), BF16), jax.ShapeDtypeStruct((n, 512), BF16),
                   jax.ShapeDtypeStruct((D_MODEL, D_MODEL), BF16)],
        scratch_shapes=[pltpu.VMEM((D_MODEL, D_MODEL), F32)],
        compiler_params=_params(("arbitrary",)),
    )(dh, zc, za, w_out)


def _group_matrix():
    r = lax.broadcasted_iota(jnp.int32, (128, 128), 0) // HEAD_DIM
    c = lax.broadcasted_iota(jnp.int32, (128, 128), 1) // HEAD_DIM
    return jnp.where(r == c, 1.0 / HEAD_DIM, 0.0).astype(BF16)


def _group_mean(x, gmat):
    hi = x.astype(BF16)
    lo = (x - hi.astype(F32)).astype(BF16)
    return _dot(hi, gmat) + _dot(lo, gmat)


def _shift_rows(x, s):
    rows = x.shape[0]
    t = lax.broadcasted_iota(jnp.int32, x.shape, 0)
    rolled = pltpu.roll(x, s % rows, 0)
    keep = (t >= s) if s > 0 else (t < rows + s)
    return jnp.where(keep, rolled, 0.0)


def _conv_parts(bg_ref, cg_ref, hc_ref, w_ref):
    bg = bg_ref[...].astype(F32)
    cg = cg_ref[...].astype(F32)
    hc = hc_ref[...].astype(F32)
    u = cg * hc
    u1 = _shift_rows(u, 1)
    u2 = _shift_rows(u, 2)
    conv = w_ref[2:3, :] * u + w_ref[1:2, :] * u1 + w_ref[0:1, :] * u2
    return bg, cg, hc, u, u1, u2, conv


def _conv_fwd(bg, cg, hc, conv_w, gain, gmat, lp):
    n = bg.shape[0]
    nb = n // lp

    def body(bg_ref, cg_ref, hc_ref, w_ref, g_ref, gm_ref, z_ref):
        bgv, _, _, _, _, _, conv = _conv_parts(bg_ref, cg_ref, hc_ref, w_ref)
        yc = bgv * conv
        r = lax.rsqrt(_group_mean(yc * yc, gm_ref[...]) + EPS)
        z_ref[...] = (yc * r * g_ref[...]).astype(BF16)

    blk = pl.BlockSpec((lp, 128), lambda c, b: (b, c))
    return pl.pallas_call(
        body, name="conv_fwd", grid=(CONV_DIM // 128, nb),
        in_specs=[blk, blk, blk, pl.BlockSpec((3, 128), lambda c, b: (0, c)),
                  pl.BlockSpec((1, 128), lambda c, b: (0, c)), pl.BlockSpec((128, 128), lambda c, b: (0, 0))],
        out_specs=blk,
        out_shape=jax.ShapeDtypeStruct((n, CONV_DIM), BF16),
        compiler_params=_params(("parallel", "parallel")),
    )(bg, cg, hc, conv_w, gain, gmat)


def _conv_bwd(dz, bg, cg, hc, conv_w, gain, gmat, lp):
    n = bg.shape[0]
    nb = n // lp

    def body(dz_ref, bg_ref, cg_ref, hc_ref, w_ref, g_ref, gm_ref,
             dbg_ref, dcg_ref, dhc_ref, dw_ref, dgain_ref):
        b = pl.program_id(1)

        @pl.when(b == 0)
        def _():
            dw_ref[...] = jnp.zeros_like(dw_ref)
            dgain_ref[...] = jnp.zeros_like(dgain_ref)

        bgv, cgv, hcv, u, u1, u2, conv = _conv_parts(bg_ref, cg_ref, hc_ref, w_ref)
        gm = gm_ref[...]
        yc = bgv * conv
        r = lax.rsqrt(_group_mean(yc * yc, gm) + EPS)
        yhat = yc * r
        dzv = dz_ref[...].astype(F32)
        dyhat = dzv * g_ref[...]
        dgain_ref[...] += jnp.sum(dzv * yhat, axis=0, keepdims=True)
        dyc = r * (dyhat - yhat * _group_mean(dyhat * yhat, gm))
        dbg_ref[...] = (dyc * conv).astype(BF16)
        dconv = dyc * bgv
        du = (w_ref[2:3, :] * dconv + w_ref[1:2, :] * _shift_rows(dconv, -1)
              + w_ref[0:1, :] * _shift_rows(dconv, -2))
        dcg_ref[...] = (du * hcv).astype(BF16)
        dhc_ref[...] = (du * cgv).astype(BF16)
        dw_ref[0:1, :] += jnp.sum(dconv * u2, axis=0, keepdims=True)
        dw_ref[1:2, :] += jnp.sum(dconv * u1, axis=0, keepdims=True)
        dw_ref[2:3, :] += jnp.sum(dconv * u, axis=0, keepdims=True)

    blk = pl.BlockSpec((lp, 128), lambda c, b: (b, c))
    wspec = pl.BlockSpec((3, 128), lambda c, b: (0, c))
    gspec = pl.BlockSpec((1, 128), lambda c, b: (0, c))
    return pl.pallas_call(
        body, name="conv_bwd", grid=(CONV_DIM // 128, nb),
        in_specs=[blk, blk, blk, blk, wspec, gspec, pl.BlockSpec((128, 128), lambda c, b: (0, 0))],
        out_specs=[blk, blk, blk, wspec, gspec],
        out_shape=[jax.ShapeDtypeStruct((n, CONV_DIM), BF16)] * 3
        + [jax.ShapeDtypeStruct((3, CONV_DIM), F32), jax.ShapeDtypeStruct((1, CONV_DIM), F32)],
        compiler_params=_params(("parallel", "arbitrary")),
    )(dz, bg, cg, hc, conv_w, gain, gmat)


def _scan_steps(rows):
    s, out = 1, []
    while s < rows:
        out.append(s)
        s *= 2
    return out


KEY_MASKED = 1e30
ONE_LANE = 24


def _fgate_fwd(fg, b_f, lp):
    n = fg.shape[0]
    nb = n // lp

    def body(fg_ref, b_ref, ka_ref, qa_ref):
        x = fg_ref[...] + b_ref[...]
        logf = jnp.minimum(x, 0.0) - jnp.log(1.0 + jnp.exp(-jnp.abs(x)))
        t = lax.broadcasted_iota(jnp.int32, (lp, 128), 0)
        lane = lax.broadcasted_iota(jnp.int32, (lp, 128), 1)
        f = jnp.where((t >= PAD) & (lane < N_HEADS), logf, 0.0)
        for s in _scan_steps(lp):
            f = f + _shift_rows(f, s)
        hi = f.astype(BF16).astype(F32)
        rest = f - hi
        mid = rest.astype(BF16).astype(F32)
        lo = (rest - mid).astype(BF16).astype(F32)
        ones = jnp.where((lane >= ONE_LANE) & (lane < ONE_LANE + 3), 1.0, 0.0)
        hi_key = jnp.where((t < PAD) & (lane < N_HEADS), KEY_MASKED, hi)
        ka_ref[...] = (hi_key + pltpu.roll(mid, 8, 1) + pltpu.roll(lo, 16, 1) + ones).astype(BF16)
        for h in range(N_HEADS):
            minus = jnp.where((lane == h) | (lane == 8 + h) | (lane == 16 + h), -1.0, 0.0)
            terms = (jnp.where(lane == ONE_LANE, pltpu.roll(hi, ONE_LANE - h, 1), 0.0)
                     + jnp.where(lane == ONE_LANE + 1, pltpu.roll(mid, ONE_LANE + 1 - h, 1), 0.0)
                     + jnp.where(lane == ONE_LANE + 2, pltpu.roll(lo, ONE_LANE + 2 - h, 1), 0.0))
            qa_ref[:, 128 * h:128 * (h + 1)] = (minus + terms).astype(BF16)

    return pl.pallas_call(
        body, name="fgate_fwd", grid=(nb,),
        in_specs=[pl.BlockSpec((lp, 128), lambda b: (b, 0)), pl.BlockSpec((1, 128), lambda b: (0, 0))],
        out_specs=[pl.BlockSpec((lp, 128), lambda b: (b, 0)), pl.BlockSpec((lp, N_HEADS * 128), lambda b: (b, 0))],
        out_shape=[jax.ShapeDtypeStruct((n, 128), BF16), jax.ShapeDtypeStruct((n, N_HEADS * 128), BF16)],
        compiler_params=_params(("parallel",)),
    )(fg, b_f)


def _fgate_bwd(dka, dfr, fg, b_f, lp):
    n = fg.shape[0]
    nb = n // lp

    def body(dka_ref, dfr_ref, fg_ref, b_ref, dfg_ref, db_ref):
        b = pl.program_id(0)

        @pl.when(b == 0)
        def _():
            db_ref[...] = jnp.zeros_like(db_ref)

        wide = jnp.concatenate([dfr_ref[0], jnp.zeros((128 - N_HEADS, lp), F32)], axis=0)
        lane0 = lax.broadcasted_iota(jnp.int32, (lp, 128), 1)
        d = jnp.where(lane0 < N_HEADS, dka_ref[...], 0.0) + wide.T
        for s in _scan_steps(lp):
            d = d + _shift_rows(d, -s)
        t = lax.broadcasted_iota(jnp.int32, (lp, 128), 0)
        lane = lax.broadcasted_iota(jnp.int32, (lp, 128), 1)
        x = fg_ref[...] + b_ref[...]
        dx = jnp.where((t >= PAD) & (lane < N_HEADS), d * _sigmoid(-x), 0.0)
        dfg_ref[...] = dx
        db_ref[...] += jnp.sum(dx, axis=0, keepdims=True)

    return pl.pallas_call(
        body, name="fgate_bwd", grid=(nb,),
        in_specs=[pl.BlockSpec((lp, 128), lambda b: (b, 0)), pl.BlockSpec((1, N_HEADS, lp), lambda b: (b, 0, 0)),
                  pl.BlockSpec((lp, 128), lambda b: (b, 0)), pl.BlockSpec((1, 128), lambda b: (0, 0))],
        out_specs=[pl.BlockSpec((lp, 128), lambda b: (b, 0)), pl.BlockSpec((1, 128), lambda b: (0, 0))],
        out_shape=[jax.ShapeDtypeStruct((n, 128), F32), jax.ShapeDtypeStruct((1, 128), F32)],
        compiler_params=_params(("arbitrary",)),
    )(dka, dfr, fg, b_f)


def _head_masks():
    lane = lax.broadcasted_iota(jnp.int32, (1, 128), 1)
    return lane < HEAD_DIM


def _stack_heads(x2, first):
    zero = jnp.zeros_like(x2)
    return jnp.concatenate([jnp.where(first, x2, zero), jnp.where(first, zero, x2)], axis=0)


def _pair_cols(col0, col1, first):
    return jnp.where(first, col0, col1)


def _score_mask(i, j):
    r = lax.broadcasted_iota(jnp.int32, (2 * TQ, TQ), 0)
    c = lax.broadcasted_iota(jnp.int32, (2 * TQ, TQ), 1)
    qpos = i * TQ + (r & (TQ - 1))
    kpos = j * TQ + c
    return (kpos <= qpos) & (kpos >= PAD)


def _causal_t():
    r = lax.broadcasted_iota(jnp.int32, (TQ, 2 * TQ), 0)
    c = lax.broadcasted_iota(jnp.int32, (TQ, 2 * TQ), 1)
    return r <= (c & (TQ - 1))


def _query_side(q_ref, qa_ref, p, first):
    q2 = q_ref[:, 128 * p:128 * (p + 1)] * 0.125
    zero = jnp.zeros_like(q2)
    top = jnp.concatenate([jnp.where(first, q2, zero), qa_ref[:, 128 * (2 * p):128 * (2 * p + 1)]], axis=1)
    bot = jnp.concatenate([jnp.where(first, zero, q2), qa_ref[:, 128 * (2 * p + 1):128 * (2 * p + 2)]], axis=1)
    return jnp.concatenate([top, bot], axis=0)


def _pair_rows(row0, row1):
    r = lax.broadcasted_iota(jnp.int32, (128, TQ), 0)
    return jnp.where(r < HEAD_DIM, row0, row1)


def _attn_fwd(q, qa, k, v, ka, gain, lp):
    n = q.shape[0]
    nb = n // lp
    nq = lp // TQ

    def body(q_ref, qa_ref, k_ref, v_ref, ka_ref, g_ref, z_ref, o_ref, lse_ref):
        i = pl.program_id(1)
        first = _head_masks()
        rhs = [_query_side(q_ref, qa_ref, p, first) for p in range(N_PAIRS)]

        def step(j, carry, diag):
            koff = pl.multiple_of(j * TQ, TQ)
            kav = ka_ref[pl.ds(koff, TQ), :]
            new = []
            for p in range(N_PAIRS):
                m, l, acc = carry[p]
                cols = slice(128 * p, 128 * (p + 1))
                k2 = k_ref[pl.ds(koff, TQ), cols]
                v2 = v_ref[pl.ds(koff, TQ), cols]
                st = _dot_nt(jnp.concatenate([k2, kav], axis=1), rhs[p])
                if diag:
                    st = jnp.where(_causal_t(), st, NEG)
                m_new = jnp.maximum(m, jnp.max(st, axis=0, keepdims=True))
                pt = jnp.exp(st - m_new)
                alpha = jnp.exp(m - m_new)
                l = alpha * l + jnp.sum(pt, axis=0, keepdims=True)
                pb = pt.astype(BF16)
                pv = _dot_tn(_stack_heads(v2, first), jnp.concatenate([pb[:, 0:TQ], pb[:, TQ:]], axis=0))
                acc = acc * _pair_rows(alpha[:, 0:TQ], alpha[:, TQ:]) + pv
                new.append((m_new, l, acc))
            return tuple(new)

        init = tuple((jnp.full((1, 2 * TQ), NEG, F32), jnp.zeros((1, 2 * TQ), F32), jnp.zeros((128, TQ), F32))
                     for _ in range(N_PAIRS))
        carry = lax.fori_loop(0, i, lambda j, c: step(j, c, False), init)
        final = step(i, carry, True)

        row = lax.broadcasted_iota(jnp.int32, (TQ, 128), 0)
        real = (i * TQ + row) >= PAD
        for p in range(N_PAIRS):
            m, l, acc = final[p]
            inv = 1.0 / l
            ot = acc * _pair_rows(inv[:, 0:TQ], inv[:, TQ:])
            sq = ot * ot
            r0 = lax.rsqrt(jnp.sum(sq[0:HEAD_DIM], axis=0, keepdims=True) * (1.0 / HEAD_DIM) + EPS)
            r1 = lax.rsqrt(jnp.sum(sq[HEAD_DIM:], axis=0, keepdims=True) * (1.0 / HEAD_DIM) + EPS)
            cols = slice(128 * p, 128 * (p + 1))
            o_ref[:, cols] = jnp.where(real, ot.T, 0.0).astype(BF16)
            z_ref[:, cols] = (jnp.where(real, (ot * _pair_rows(r0, r1)).T, 0.0) * g_ref[:, cols]).astype(BF16)
            lse = m + jnp.log(l)
            lse_ref[0, 2 * p:2 * p + 1, :] = lse[:, 0:TQ]
            lse_ref[0, 2 * p + 1:2 * p + 2, :] = lse[:, TQ:]

    qblk = pl.BlockSpec((TQ, ATTN_DIM), lambda b, i: (b * nq + i, 0))
    qablk = pl.BlockSpec((TQ, N_HEADS * 128), lambda b, i: (b * nq + i, 0))
    seq = pl.BlockSpec((lp, ATTN_DIM), lambda b, i: (b, 0))
    rowblk = pl.BlockSpec((1, N_HEADS, TQ), lambda b, i: (b, 0, i))
    return pl.pallas_call(
        body, name="attn_fwd", grid=(nb, nq),
        in_specs=[qblk, qablk, seq, seq, pl.BlockSpec((lp, 128), lambda b, i: (b, 0)),
                  pl.BlockSpec((1, ATTN_DIM), lambda b, i: (0, 0))],
        out_specs=[qblk, qblk, rowblk],
        out_shape=[jax.ShapeDtypeStruct((n, ATTN_DIM), BF16), jax.ShapeDtypeStruct((n, ATTN_DIM), BF16),
                   jax.ShapeDtypeStruct((nb, N_HEADS, lp), F32)],
        compiler_params=_params(("parallel", "parallel")),
    )(q, qa, k, v, ka, gain)


def _attn_bwd(dz, q, qa, k, v, ka, o, lse, gain, lp):
    n = q.shape[0]
    nb = n // lp
    nq = lp // TQ

    def body(dz_ref, q_ref, qa_ref, k_ref, v_ref, ka_ref, o_ref, lse_ref, g_ref,
             dq_ref, dk_ref, dv_ref, dka_ref, dfr_ref, dgain_ref):
        b = pl.program_id(0)
        i = pl.program_id(1)
        first = _head_masks()

        @pl.when((b == 0) & (i == 0))
        def _():
            dgain_ref[...] = jnp.zeros_like(dgain_ref)

        @pl.when(i == 0)
        def _():
            dk_ref[...] = jnp.zeros_like(dk_ref)
            dv_ref[...] = jnp.zeros_like(dv_ref)
            dka_ref[...] = jnp.zeros_like(dka_ref)

        rhs, lses, dos, deltas = [], [], [], []
        for p in range(N_PAIRS):
            cols = slice(128 * p, 128 * (p + 1))
            rhs.append(_query_side(q_ref, qa_ref, p, first))
            lses.append(jnp.concatenate([lse_ref[0, 2 * p:2 * p + 1, :], lse_ref[0, 2 * p + 1:2 * p + 2, :]], axis=1))
            ov = o_ref[:, cols].astype(F32)
            dzv = dz_ref[:, cols].astype(F32)
            gv = g_ref[:, cols]
            sq = ov * ov
            ms0 = jnp.sum(jnp.where(first, sq, 0.0), axis=1, keepdims=True) * (1.0 / HEAD_DIM)
            ms1 = jnp.sum(jnp.where(first, 0.0, sq), axis=1, keepdims=True) * (1.0 / HEAD_DIM)
            r = _pair_cols(lax.rsqrt(ms0 + EPS), lax.rsqrt(ms1 + EPS), first)
            ohat = ov * r
            dyhat = dzv * gv
            dgain_ref[:, cols] += jnp.sum(dzv * ohat, axis=0, keepdims=True)
            pr = dyhat * ohat
            mean0 = jnp.sum(jnp.where(first, pr, 0.0), axis=1, keepdims=True) * (1.0 / HEAD_DIM)
            mean1 = jnp.sum(jnp.where(first, 0.0, pr), axis=1, keepdims=True) * (1.0 / HEAD_DIM)
            do = r * (dyhat - ohat * _pair_cols(mean0, mean1, first))
            ddt = (do * ov).T
            deltas.append(jnp.concatenate([jnp.sum(ddt[0:HEAD_DIM], axis=0, keepdims=True),
                                           jnp.sum(ddt[HEAD_DIM:], axis=0, keepdims=True)], axis=1))
            dos.append(_stack_heads(do.astype(BF16), first))

        def step(j, carry, diag):
            koff = pl.multiple_of(j * TQ, TQ)
            kav = ka_ref[pl.ds(koff, TQ), :]
            dka = jnp.zeros((TQ, 128), F32)
            new = []
            for p in range(N_PAIRS):
                dq_acc, dfq = carry[p]
                cols = slice(128 * p, 128 * (p + 1))
                k2 = k_ref[pl.ds(koff, TQ), cols]
                v2 = v_ref[pl.ds(koff, TQ), cols]
                st = _dot_nt(jnp.concatenate([k2, kav], axis=1), rhs[p])
                if diag:
                    st = jnp.where(_causal_t(), st, NEG)
                pt = jnp.exp(st - lses[p])
                dpt = _dot_nt(v2, dos[p])
                dst = pt * (dpt - deltas[p])
                dsb = dst.astype(BF16)
                dfq = dfq + jnp.sum(dsb.astype(F32), axis=0, keepdims=True)
                dk_ext = _dot(dsb, rhs[p])
                dk_ref[pl.ds(koff, TQ), cols] += dk_ext[:, 0:128]
                dka = dka + dk_ext[:, 128:]
                dv_ref[pl.ds(koff, TQ), cols] += _dot(pt.astype(BF16), dos[p])
                dq_acc = dq_acc + _dot_tn(jnp.concatenate([dsb[:, 0:TQ], dsb[:, TQ:]], axis=0),
                                          _stack_heads(k2, first))
                new.append((dq_acc, dfq))
            dka_ref[pl.ds(koff, TQ), :] += dka
            return tuple(new)

        init = tuple((jnp.zeros((TQ, 128), F32), jnp.zeros((1, 2 * TQ), F32)) for _ in range(N_PAIRS))
        carry = lax.fori_loop(0, i, lambda j, c: step(j, c, False), init)
        final = step(i, carry, True)

        for p in range(N_PAIRS):
            dq_acc, dfq = final[p]
            dq_ref[:, 128 * p:128 * (p + 1)] = (dq_acc * 0.125).astype(BF16)
            dfr_ref[0, 2 * p:2 * p + 1, :] = dfq[:, 0:TQ]
            dfr_ref[0, 2 * p + 1:2 * p + 2, :] = dfq[:, TQ:]

    qblk = pl.BlockSpec((TQ, ATTN_DIM), lambda b, i: (b * nq + i, 0))
    qablk = pl.BlockSpec((TQ, N_HEADS * 128), lambda b, i: (b * nq + i, 0))
    seq = pl.BlockSpec((lp, ATTN_DIM), lambda b, i: (b, 0))
    kaseq = pl.BlockSpec((lp, 128), lambda b, i: (b, 0))
    rowblk = pl.BlockSpec((1, N_HEADS, TQ), lambda b, i: (b, 0, i))
    gspec = pl.BlockSpec((1, ATTN_DIM), lambda b, i: (0, 0))
    return pl.pallas_call(
        body, name="attn_bwd", grid=(nb, nq),
        in_specs=[qblk, qblk, qablk, seq, seq, kaseq, qblk, rowblk, gspec],
        out_specs=[qblk, seq, seq, kaseq, rowblk, gspec],
        out_shape=[jax.ShapeDtypeStruct((n, ATTN_DIM), BF16), jax.ShapeDtypeStruct((n, ATTN_DIM), F32),
                   jax.ShapeDtypeStruct((n, ATTN_DIM), F32), jax.ShapeDtypeStruct((n, 128), F32),
                   jax.ShapeDtypeStruct((nb, N_HEADS, lp), F32), jax.ShapeDtypeStruct((1, ATTN_DIM), F32)],
        compiler_params=_params(("arbitrary", "arbitrary")),
    )(dz, q, qa, k, v, ka, o, lse, gain)


def _attn_fwd_old(q, k, v, fc, fr, gain, lp):
    n = q.shape[0]
    nb = n // lp
    nq = lp // TQ

    def body(q_ref, k_ref, v_ref, fc_ref, fr_ref, g_ref, z_ref, o_ref, lse_ref):
        i = pl.program_id(1)
        first = _head_masks()
        fcv = fc_ref[...]
        qs, fq = [], []
        for p in range(N_PAIRS):
            q2 = q_ref[:, 128 * p:128 * (p + 1)] * 0.125
            qs.append(_stack_heads(q2, first))
            fq.append(jnp.concatenate([fcv[:, 2 * p:2 * p + 1], fcv[:, 2 * p + 1:2 * p + 2]], axis=0))

        def step(j, carry):
            koff = pl.multiple_of(j * TQ, TQ)
            valid = _score_mask(i, j)
            new = []
            for p in range(N_PAIRS):
                m, l, acc = carry[p]
                k2 = k_ref[pl.ds(koff, TQ), 128 * p:128 * (p + 1)]
                v2 = v_ref[pl.ds(koff, TQ), 128 * p:128 * (p + 1)]
                fk = jnp.concatenate(
                    [jnp.broadcast_to(fr_ref[0, 2 * p:2 * p + 1, pl.ds(koff, TQ)], (TQ, TQ)),
                     jnp.broadcast_to(fr_ref[0, 2 * p + 1:2 * p + 2, pl.ds(koff, TQ)], (TQ, TQ))], axis=0)
                s = _dot_nt(qs[p], k2) + fq[p] - fk
                s = jnp.where(valid, s, NEG)
                m_new = jnp.maximum(m, jnp.max(s, axis=1, keepdims=True))
                pe = jnp.exp(s - m_new)
                alpha = jnp.exp(m - m_new)
                l = alpha * l + jnp.sum(pe, axis=1, keepdims=True)
                pb = pe.astype(BF16)
                pv = _dot(jnp.concatenate([pb[0:TQ], pb[TQ:]], axis=1), _stack_heads(v2, first))
                acc = acc * _pair_cols(alpha[0:TQ], alpha[TQ:], first) + pv
                new.append((m_new, l, acc))
            return tuple(new)

        init = tuple((jnp.full((2 * TQ, 1), NEG, F32), jnp.zeros((2 * TQ, 1), F32), jnp.zeros((TQ, 128), F32))
                     for _ in range(N_PAIRS))
        final = lax.fori_loop(0, i + 1, step, init)

        row = lax.broadcasted_iota(jnp.int32, (TQ, 128), 0)
        lane = lax.broadcasted_iota(jnp.int32, (TQ, 128), 1)
        real = (i * TQ + row) >= PAD
        lse_out = jnp.zeros((TQ, 128), F32)
        for p in range(N_PAIRS):
            m, l, acc = final[p]
            inv = 1.0 / l
            o = jnp.where(real, acc * _pair_cols(inv[0:TQ], inv[TQ:], first), 0.0)
            lse = m + jnp.log(l)
            lse_out = jnp.where(lane == 2 * p, lse[0:TQ], lse_out)
            lse_out = jnp.where(lane == 2 * p + 1, lse[TQ:], lse_out)
            sq = o * o
            ms0 = jnp.sum(jnp.where(first, sq, 0.0), axis=1, keepdims=True) * (1.0 / HEAD_DIM)
            ms1 = jnp.sum(jnp.where(first, 0.0, sq), axis=1, keepdims=True) * (1.0 / HEAD_DIM)
            r = _pair_cols(lax.rsqrt(ms0 + EPS), lax.rsqrt(ms1 + EPS), first)
            cols = slice(128 * p, 128 * (p + 1))
            o_ref[:, cols] = o.astype(BF16)
            z_ref[:, cols] = (o * r * g_ref[:, cols]).astype(BF16)
        lse_ref[...] = lse_out

    qblk = pl.BlockSpec((TQ, ATTN_DIM), lambda b, i: (b * nq + i, 0))
    seq = pl.BlockSpec((lp, ATTN_DIM), lambda b, i: (b, 0))
    colblk = pl.BlockSpec((TQ, 128), lambda b, i: (b * nq + i, 0))
    return pl.pallas_call(
        body, name="attn_fwd", grid=(nb, nq),
        in_specs=[qblk, seq, seq, colblk, pl.BlockSpec((1, N_HEADS, lp), lambda b, i: (b, 0, 0)),
                  pl.BlockSpec((1, ATTN_DIM), lambda b, i: (0, 0))],
        out_specs=[qblk, qblk, colblk],
        out_shape=[jax.ShapeDtypeStruct((n, ATTN_DIM), BF16), jax.ShapeDtypeStruct((n, ATTN_DIM), BF16),
                   jax.ShapeDtypeStruct((n, 128), F32)],
        compiler_params=_params(("parallel", "parallel")),
    )(q, k, v, fc, fr, gain)


def _attn_bwd_old(dz, q, k, v, fc, fr, o, lse, gain, lp):
    n = q.shape[0]
    nb = n // lp
    nq = lp // TQ

    def body(dz_ref, q_ref, k_ref, v_ref, fc_ref, fr_ref, o_ref, lse_ref, g_ref,
             dq_ref, dk_ref, dv_ref, dfc_ref, dfr_ref, dgain_ref):
        b = pl.program_id(0)
        i = pl.program_id(1)
        first = _head_masks()

        @pl.when((b == 0) & (i == 0))
        def _():
            dgain_ref[...] = jnp.zeros_like(dgain_ref)

        @pl.when(i == 0)
        def _():
            dk_ref[...] = jnp.zeros_like(dk_ref)
            dv_ref[...] = jnp.zeros_like(dv_ref)
            dfr_ref[...] = jnp.zeros_like(dfr_ref)

        fcv = fc_ref[...]
        lsev = lse_ref[...]
        qs, fq, lses, dos, deltas = [], [], [], [], []
        for p in range(N_PAIRS):
            cols = slice(128 * p, 128 * (p + 1))
            q2 = q_ref[:, cols] * 0.125
            qs.append(_stack_heads(q2, first))
            fq.append(jnp.concatenate([fcv[:, 2 * p:2 * p + 1], fcv[:, 2 * p + 1:2 * p + 2]], axis=0))
            lses.append(jnp.concatenate([lsev[:, 2 * p:2 * p + 1], lsev[:, 2 * p + 1:2 * p + 2]], axis=0))
            ov = o_ref[:, cols].astype(F32)
            dzv = dz_ref[:, cols].astype(F32)
            gv = g_ref[:, cols]
            sq = ov * ov
            ms0 = jnp.sum(jnp.where(first, sq, 0.0), axis=1, keepdims=True) * (1.0 / HEAD_DIM)
            ms1 = jnp.sum(jnp.where(first, 0.0, sq), axis=1, keepdims=True) * (1.0 / HEAD_DIM)
            r = _pair_cols(lax.rsqrt(ms0 + EPS), lax.rsqrt(ms1 + EPS), first)
            ohat = ov * r
            dyhat = dzv * gv
            dgain_ref[:, cols] += jnp.sum(dzv * ohat, axis=0, keepdims=True)
            pr = dyhat * ohat
            mean0 = jnp.sum(jnp.where(first, pr, 0.0), axis=1, keepdims=True) * (1.0 / HEAD_DIM)
            mean1 = jnp.sum(jnp.where(first, 0.0, pr), axis=1, keepdims=True) * (1.0 / HEAD_DIM)
            do = r * (dyhat - ohat * _pair_cols(mean0, mean1, first))
            dd = do * ov
            deltas.append(jnp.concatenate([jnp.sum(jnp.where(first, dd, 0.0), axis=1, keepdims=True),
                                           jnp.sum(jnp.where(first, 0.0, dd), axis=1, keepdims=True)], axis=0))
            dos.append(_stack_heads(do.astype(BF16), first))

        def step(j, carry):
            koff = pl.multiple_of(j * TQ, TQ)
            valid = _score_mask(i, j)
            new = []
            for p in range(N_PAIRS):
                dq_acc, dfq = carry[p]
                cols = slice(128 * p, 128 * (p + 1))
                k2 = k_ref[pl.ds(koff, TQ), cols]
                v2 = v_ref[pl.ds(koff, TQ), cols]
                fk = jnp.concatenate(
                    [jnp.broadcast_to(fr_ref[0, 2 * p:2 * p + 1, pl.ds(koff, TQ)], (TQ, TQ)),
                     jnp.broadcast_to(fr_ref[0, 2 * p + 1:2 * p + 2, pl.ds(koff, TQ)], (TQ, TQ))], axis=0)
                s = _dot_nt(qs[p], k2) + fq[p] - fk
                s = jnp.where(valid, s, NEG)
                pe = jnp.exp(s - lses[p])
                dp = _dot_nt(dos[p], v2)
                ds = pe * (dp - deltas[p])
                dfq = dfq + jnp.sum(ds, axis=1, keepdims=True)
                dfr_ref[0, 2 * p:2 * p + 1, pl.ds(koff, TQ)] -= jnp.sum(ds[0:TQ], axis=0, keepdims=True)
                dfr_ref[0, 2 * p + 1:2 * p + 2, pl.ds(koff, TQ)] -= jnp.sum(ds[TQ:], axis=0, keepdims=True)
                dsb = ds.astype(BF16)
                dq_acc = dq_acc + _dot(jnp.concatenate([dsb[0:TQ], dsb[TQ:]], axis=1), _stack_heads(k2, first))
                dk_ref[pl.ds(koff, TQ), cols] += _dot_tn(dsb, qs[p])
                dv_ref[pl.ds(koff, TQ), cols] += _dot_tn(pe.astype(BF16), dos[p])
                new.append((dq_acc, dfq))
            return tuple(new)

        init = tuple((jnp.zeros((TQ, 128), F32), jnp.zeros((2 * TQ, 1), F32)) for _ in range(N_PAIRS))
        final = lax.fori_loop(0, i + 1, step, init)

        lane = lax.broadcasted_iota(jnp.int32, (TQ, 128), 1)
        dfc_out = jnp.zeros((TQ, 128), F32)
        for p in range(N_PAIRS):
            dq_acc, dfq = final[p]
            dq_ref[:, 128 * p:128 * (p + 1)] = (dq_acc * 0.125).astype(BF16)
            dfc_out = jnp.where(lane == 2 * p, dfq[0:TQ], dfc_out)
            dfc_out = jnp.where(lane == 2 * p + 1, dfq[TQ:], dfc_out)
        dfc_ref[...] = dfc_out

    qblk = pl.BlockSpec((TQ, ATTN_DIM), lambda b, i: (b * nq + i, 0))
    seq = pl.BlockSpec((lp, ATTN_DIM), lambda b, i: (b, 0))
    colblk = pl.BlockSpec((TQ, 128), lambda b, i: (b * nq + i, 0))
    rowblk = pl.BlockSpec((1, N_HEADS, lp), lambda b, i: (b, 0, 0))
    gspec = pl.BlockSpec((1, ATTN_DIM), lambda b, i: (0, 0))
    return pl.pallas_call(
        body, name="attn_bwd", grid=(nb, nq),
        in_specs=[qblk, qblk, seq, seq, colblk, rowblk, qblk, colblk, gspec],
        out_specs=[qblk, seq, seq, colblk, rowblk, gspec],
        out_shape=[jax.ShapeDtypeStruct((n, ATTN_DIM), BF16), jax.ShapeDtypeStruct((n, ATTN_DIM), F32),
                   jax.ShapeDtypeStruct((n, ATTN_DIM), F32), jax.ShapeDtypeStruct((n, 128), F32),
                   jax.ShapeDtypeStruct((nb, N_HEADS, lp), F32), jax.ShapeDtypeStruct((1, ATTN_DIM), F32)],
        compiler_params=_params(("arbitrary", "arbitrary")),
    )(dz, q, k, v, fc, fr, o, lse, gain)


def _loss_head(h, gain, target, lp):
    n = h.shape[0]
    nb = n // lp
    nq = lp // 128

    def body(h_ref, g_ref, t_ref, loss_ref, dh_ref, dgain_ref):
        b = pl.program_id(0)
        i = pl.program_id(1)

        @pl.when((b == 0) & (i == 0))
        def _():
            loss_ref[...] = jnp.zeros_like(loss_ref)
            dgain_ref[...] = jnp.zeros_like(dgain_ref)

        @pl.when(i == 0)
        def _():
            dh_ref[...] = jnp.zeros_like(dh_ref)

        @pl.when(i > 0)
        def _():
            gain_v = g_ref[...]
            y, xhat, r = _rms(h_ref[...], gain_v)
            err = y - t_ref[...]
            loss_ref[...] += 0.5 * jnp.sum(jnp.sum(err * err, axis=1, keepdims=True), axis=0,
                                           keepdims=True) * (1.0 / D_MODEL)
            dy = err * (1.0 / D_MODEL)
            dh_ref[...] = _rms_bwd(dy, xhat, r, gain_v)
            dgain_ref[...] += jnp.sum(dy * xhat, axis=0, keepdims=True)

    rows = pl.BlockSpec((128, D_MODEL), lambda b, i: (b * nq + i, 0))
    trows = pl.BlockSpec((128, D_MODEL), lambda b, i: (b * (nq - 1) + jnp.maximum(i, 1) - 1, 0))
    return pl.pallas_call(
        body, name="loss_head", grid=(nb, nq),
        in_specs=[rows, pl.BlockSpec((1, D_MODEL), lambda b, i: (0, 0)), trows],
        out_specs=[pl.BlockSpec((1, 1), lambda b, i: (0, 0)), rows, pl.BlockSpec((1, D_MODEL), lambda b, i: (0, 0))],
        out_shape=[jax.ShapeDtypeStruct((1, 1), F32), jax.ShapeDtypeStruct((n, D_MODEL), F32),
                   jax.ShapeDtypeStruct((1, D_MODEL), F32)],
        compiler_params=_params(("arbitrary", "arbitrary")),
    )(h, gain, target)


def _place():
    return lax.axis_index("x"), lax.axis_index("y"), lax.axis_index("c")


def _all_gather(xs, name):
    nw = len(xs)

    def body(*refs):
        ins, outs = refs[:nw], refs[nw:2 * nw]
        send_sems, recv_sems, local_sems = refs[2 * nw:]
        x, y, c = _place()
        me, sibling = (x, y, c), (x, y, 1 - c)
        chips = [(1 - x, y), (x, 1 - y), (1 - x, 1 - y)]

        def copy(w, k, block, to, src=None):
            slot = outs[w].at[4 * block[0] + 2 * block[1] + block[2]]
            return pltpu.make_async_remote_copy(
                src_ref=slot if src is None else src, dst_ref=slot,
                send_sem=send_sems.at[w, k], recv_sem=recv_sems.at[w, k], device_id=to, device_id_type=MESH)

        started = []
        for w in range(nw):
            mine = pltpu.make_async_copy(ins[w], outs[w].at[4 * x + 2 * y + c], local_sems.at[w])
            mine.start()
            started.append(mine)
        sends = []
        for w in range(nw):
            sends.append(copy(w, 0, me, sibling, src=ins[w]))
            sends += [copy(w, 1 + j, me, (*chip, c), src=ins[w]) for j, chip in enumerate(chips)]
        for cp in sends:
            cp.start()
        for w in range(nw):
            for j, chip in enumerate(chips):
                copy(w, 1 + j, (*chip, c), me).wait_recv()
                passed = copy(w, 4 + j, (*chip, c), sibling)
                passed.start()
                sends.append(passed)
        for w in range(nw):
            copy(w, 0, sibling, me).wait_recv()
            for j, chip in enumerate(chips):
                copy(w, 4 + j, (*chip, 1 - c), me).wait_recv()
        for cp in sends:
            cp.wait_send()
        for mine in started:
            mine.wait()

    return pl.pallas_call(
        body, name=name,
        in_specs=[ANY] * nw, out_specs=[ANY] * nw,
        out_shape=[jax.ShapeDtypeStruct((N_DEV,) + a.shape, a.dtype) for a in xs],
        scratch_shapes=[pltpu.SemaphoreType.DMA((nw, 7)), pltpu.SemaphoreType.DMA((nw, 7)),
                        pltpu.SemaphoreType.DMA((nw,))],
    )(*xs)


def _exchange_partials(xs, name):
    nw = len(xs)

    def body(*refs):
        ins, outs = refs[:nw], refs[nw:2 * nw]
        send_sems, recv_sems, local_sems = refs[2 * nw:]
        x, y, c = _place()
        me = 4 * x + 2 * y + c

        def flip(v, bit):
            return 1 - v if bit else v

        def copy(w, k):
            bx, by, bc = ((k + 1) >> 2) & 1, ((k + 1) >> 1) & 1, (k + 1) & 1
            peer = (flip(x, bx), flip(y, by), flip(c, bc))
            peer_idx = 4 * peer[0] + 2 * peer[1] + peer[2]
            return pltpu.make_async_remote_copy(
                src_ref=ins[w].at[peer_idx], dst_ref=outs[w].at[me],
                send_sem=send_sems.at[w, k], recv_sem=recv_sems.at[w, k], device_id=peer, device_id_type=MESH)

        local = [pltpu.make_async_copy(ins[w].at[me], outs[w].at[me], local_sems.at[w]) for w in range(nw)]
        for cp in local:
            cp.start()
        remote = [copy(w, k) for w in range(nw) for k in range(N_DEV - 1)]
        for cp in remote:
            cp.start()
        for cp in remote:
            cp.wait()
        for cp in local:
            cp.wait()

    return pl.pallas_call(
        body, name=name,
        in_specs=[ANY] * nw, out_specs=[ANY] * nw,
        out_shape=[jax.ShapeDtypeStruct(a.shape, a.dtype) for a in xs],
        scratch_shapes=[pltpu.SemaphoreType.DMA((nw, 7)), pltpu.SemaphoreType.DMA((nw, 7)),
                        pltpu.SemaphoreType.DMA((nw,))],
    )(*xs)


def _adamw(parts, w, m, v, name):
    s_parts, r, c = parts.shape
    tr = r
    for t in (256, 128, 64, 32, 16):
        if r % t == 0 and r > t:
            tr = t
            break

    def body(p_ref, w_ref, m_ref, v_ref, g_ref, d_ref, nm_ref, nv_ref):
        g = p_ref[0].astype(F32)
        for s in range(1, s_parts):
            g = g + p_ref[s].astype(F32)
        nm = ADAM_B1 * m_ref[...] + (1.0 - ADAM_B1) * g
        nv = ADAM_B2 * v_ref[...] + (1.0 - ADAM_B2) * (g * g)
        m_hat = nm / (1.0 - ADAM_B1 ** ADAM_STEP)
        v_hat = nv / (1.0 - ADAM_B2 ** ADAM_STEP)
        g_ref[...] = g
        d_ref[...] = -ADAM_LR * (m_hat / (jnp.sqrt(v_hat) + ADAM_EPS) + ADAM_WD * w_ref[...])
        nm_ref[...] = nm
        nv_ref[...] = nv

    blk = pl.BlockSpec((tr, c), lambda i: (i, 0))
    return pl.pallas_call(
        body, name=name, grid=(r // tr,),
        in_specs=[pl.BlockSpec((s_parts, tr, c), lambda i: (0, i, 0)), blk, blk, blk],
        out_specs=[blk] * 4,
        out_shape=[jax.ShapeDtypeStruct((r, c), F32)] * 4,
        compiler_params=_params(("parallel",)),
    )(parts, w, m, v)


def _sum_parts(parts, name):
    s_parts, r, c = parts.shape

    def body(p_ref, out_ref):
        acc = p_ref[0]
        for s in range(1, s_parts):
            acc = acc + p_ref[s]
        out_ref[...] = acc

    return pl.pallas_call(
        body, name=name, out_shape=jax.ShapeDtypeStruct((r, c), F32),
        in_specs=[pl.BlockSpec(memory_space=pltpu.VMEM)], out_specs=pl.BlockSpec(memory_space=pltpu.VMEM),
    )(parts)


SMALL_ROWS = 184


def _pack_small(d_gains, d_gc, d_ga, d_bf, d_conv, d_meta):
    rows = [g.reshape(8, 128) for g in d_gains]
    rows += [d_gc.reshape(4, 128), d_ga.reshape(4, 128), d_bf.reshape(1, 128)]
    rows += [d_conv.reshape(12, 128), d_meta.reshape(128, 128)]
    packed = jnp.concatenate(rows, axis=0)
    return jnp.pad(packed, ((0, SMALL_ROWS - packed.shape[0]), (0, 0)))


def kernel(x, meta_tokens, ffn1_norm, ffn1_w_gu, ffn1_w_down, mix_norm, w_in, conv_w, b_f, out_norm_conv, out_norm_attn, w_out, ffn2_norm, ffn2_w_gu, ffn2_w_down, final_norm, loss_target, m_meta_tokens, m_ffn1_norm, m_ffn1_w_gu, m_ffn1_w_down, m_mix_norm, m_w_in, m_conv_w, m_b_f, m_out_norm_conv, m_out_norm_attn, m_w_out, m_ffn2_norm, m_ffn2_w_gu, m_ffn2_w_down, m_final_norm, v_meta_tokens, v_ffn1_norm, v_ffn1_w_gu, v_ffn1_w_down, v_mix_norm, v_w_in, v_conv_w, v_b_f, v_out_norm_conv, v_out_norm_attn, v_w_out, v_ffn2_norm, v_ffn2_w_gu, v_ffn2_w_down, v_final_norm):
    nb, seq, _ = x.shape
    lp = PAD + N_META + seq
    n = nb * lp
    me = 4 * lax.axis_index("x") + 2 * lax.axis_index("y") + lax.axis_index("c")

    big = [ffn1_w_gu[0], ffn1_w_down[0], w_in[0], w_out[0], ffn2_w_gu[0], ffn2_w_down[0]]
    wgu1_8, wd1_8, win_8, wout_8, wgu2_8, wd2_8 = _all_gather([a.astype(BF16) for a in big], "gather_weights")
    small_in = jnp.concatenate(
        [meta_tokens, jnp.pad(conv_w[0], ((0, 0), (0, 128 - conv_w.shape[2]))), jnp.zeros((5, 128), F32)], axis=0)
    (small_8,) = _all_gather([small_in], "gather_small")
    meta_full = small_8[:, 0:N_META, :].transpose(1, 0, 2).reshape(N_META, D_MODEL)
    conv_full = small_8[:, N_META:N_META + 3, 0:CONV_DIM // N_DEV].transpose(1, 0, 2).reshape(3, CONV_DIM)

    wgu1 = wgu1_8.reshape(2, N_CHUNK, D_MODEL, F_CHUNK)
    wgu2 = wgu2_8.reshape(2, N_CHUNK, D_MODEL, F_CHUNK)
    wd1 = wd1_8.reshape(N_CHUNK, F_CHUNK, D_MODEL)
    wd2 = wd2_8.reshape(N_CHUNK, F_CHUNK, D_MODEL)
    w_in_full = jnp.pad(win_8.transpose(1, 0, 2).reshape(D_MODEL, IN_DIM), ((0, 0), (0, IN_PAD - IN_DIM)))
    w_out_full = wout_8.reshape(D_MODEL, D_MODEL)
    b_f_row = jnp.pad(b_f, ((0, 0), (0, 128 - N_HEADS)))
    gmat = _group_matrix()

    h0 = jnp.concatenate([jnp.zeros((nb, PAD, D_MODEL), F32),
                          jnp.broadcast_to(meta_full[None], (nb, N_META, D_MODEL)), x], axis=1).reshape(n, D_MODEL)
    h1, gate1, up1 = _ffn_fwd(h0, ffn1_norm, wgu1, wd1, "ffn1_fwd")
    bg, cg, hc, q, k, v, fg = _inproj_fwd(h1, mix_norm, w_in_full)
    zc = _conv_fwd(bg, cg, hc, conv_full, out_norm_conv, gmat, lp)
    ka, qa = _fgate_fwd(fg, b_f_row, lp)
    za, o, lse = _attn_fwd(q, qa, k, v, ka, out_norm_attn, lp)
    h2 = _outproj_fwd(zc, za, w_out_full, h1)
    h3, gate2, up2 = _ffn_fwd(h2, ffn2_norm, wgu2, wd2, "ffn2_fwd")
    loss_part, dh3, d_final = _loss_head(h3, final_norm.reshape(1, D_MODEL), loss_target.reshape(nb * seq, D_MODEL), lp)

    dh2, dgate2, dup2, d_ffn2 = _ffn_bwd_x(dh3, h2, ffn2_norm, gate2, up2, wgu2, wd2, "ffn2_bwd_x")
    dwgu2, dwd2 = _ffn_bwd_w(dh3, h2, ffn2_norm, gate2, up2, dgate2, dup2, "ffn2_bwd_w")
    dzc, dza, dwout = _outproj_bwd(dh2, zc, za, w_out_full)
    dq, dk, dv, dka, dfr, d_ga = _attn_bwd(dza, q, qa, k, v, ka, o, lse, out_norm_attn, lp)
    dfg, d_bf = _fgate_bwd(dka, dfr, fg, b_f_row, lp)
    dbg, dcg, dhc, d_conv, d_gc = _conv_bwd(dzc, bg, cg, hc, conv_full, out_norm_conv, gmat, lp)
    dh1, dwin, d_mix = _inproj_bwd([dbg, dcg, dhc, dq, dk, dv], dfg, dh2, h1, mix_norm, w_in_full)
    dh0, dgate1, dup1, d_ffn1 = _ffn_bwd_x(dh1, h0, ffn1_norm, gate1, up1, wgu1, wd1, "ffn1_bwd_x")
    dwgu1, dwd1 = _ffn_bwd_w(dh1, h0, ffn1_norm, gate1, up1, dgate1, dup1, "ffn1_bwd_w")

    dh0 = dh0.reshape(nb, lp, D_MODEL)
    grad_x = dh0[:, PAD + N_META:, :]
    d_meta = jnp.sum(dh0[:, PAD:PAD + N_META, :], axis=0)

    dwin_8 = dwin[:, 0:IN_DIM].reshape(D_MODEL, N_DEV, IN_DIM // N_DEV).transpose(1, 0, 2)
    send = [dwgu1.reshape(N_DEV, D_MODEL, F_CHUNK), dwd1.reshape(N_DEV, F_CHUNK // 2, D_MODEL), dwin_8,
            dwout.reshape(N_DEV, D_MODEL // N_DEV, D_MODEL),
            dwgu2.reshape(N_DEV, D_MODEL, F_CHUNK), dwd2.reshape(N_DEV, F_CHUNK // 2, D_MODEL)]
    parts = _exchange_partials(send, "exchange_grads")

    small = _pack_small([d_ffn1, d_mix, d_ffn2, d_final], d_gc, d_ga, d_bf, d_conv, d_meta)
    (small_all,) = _all_gather([small], "gather_small_grads")
    small_sum = _sum_parts(small_all, "sum_small_grads")
    g_ffn1n, g_mixn, g_ffn2n, g_finaln = (small_sum[8 * t:8 * t + 8].reshape(1, D_MODEL) for t in range(4))
    g_gc = small_sum[32:36].reshape(1, CONV_DIM)
    g_ga = small_sum[36:40].reshape(1, ATTN_DIM)
    g_bf = small_sum[40:41, 0:N_HEADS]
    g_conv_full = small_sum[41:53].reshape(3, CONV_DIM)
    g_meta_full = small_sum[53:181].reshape(N_META, D_MODEL)
    g_conv = lax.dynamic_slice_in_dim(g_conv_full, me * (CONV_DIM // N_DEV), CONV_DIM // N_DEV, axis=1)
    g_meta = lax.dynamic_slice_in_dim(g_meta_full, me * (D_MODEL // N_DEV), D_MODEL // N_DEV, axis=1)

    weights = {
        "meta_tokens": (g_meta[None], meta_tokens, m_meta_tokens, v_meta_tokens),
        "ffn1_norm": (g_ffn1n[None], ffn1_norm, m_ffn1_norm, v_ffn1_norm),
        "ffn1_w_gu": (parts[0], ffn1_w_gu[0], m_ffn1_w_gu[0], v_ffn1_w_gu[0]),
        "ffn1_w_down": (parts[1], ffn1_w_down[0], m_ffn1_w_down[0], v_ffn1_w_down[0]),
        "mix_norm": (g_mixn[None], mix_norm, m_mix_norm, v_mix_norm),
        "w_in": (parts[2], w_in[0], m_w_in[0], v_w_in[0]),
        "conv_w": (g_conv[None], conv_w[0], m_conv_w[0], v_conv_w[0]),
        "b_f": (g_bf[None], b_f, m_b_f, v_b_f),
        "out_norm_conv": (g_gc[None], out_norm_conv, m_out_norm_conv, v_out_norm_conv),
        "out_norm_attn": (g_ga[None], out_norm_attn, m_out_norm_attn, v_out_norm_attn),
        "w_out": (parts[3], w_out[0], m_w_out[0], v_w_out[0]),
        "ffn2_norm": (g_ffn2n[None], ffn2_norm, m_ffn2_norm, v_ffn2_norm),
        "ffn2_w_gu": (parts[4], ffn2_w_gu[0], m_ffn2_w_gu[0], v_ffn2_w_gu[0]),
        "ffn2_w_down": (parts[5], ffn2_w_down[0], m_ffn2_w_down[0], v_ffn2_w_down[0]),
        "final_norm": (g_finaln[None], final_norm.reshape(1, D_MODEL), m_final_norm.reshape(1, D_MODEL),
                       v_final_norm.reshape(1, D_MODEL)),
    }
    shapes = {"meta_tokens": meta_tokens.shape, "ffn1_norm": ffn1_norm.shape, "ffn1_w_gu": ffn1_w_gu.shape,
              "ffn1_w_down": ffn1_w_down.shape, "mix_norm": mix_norm.shape, "w_in": w_in.shape,
              "conv_w": conv_w.shape, "b_f": b_f.shape, "out_norm_conv": out_norm_conv.shape,
              "out_norm_attn": out_norm_attn.shape, "w_out": w_out.shape, "ffn2_norm": ffn2_norm.shape,
              "ffn2_w_gu": ffn2_w_gu.shape, "ffn2_w_down": ffn2_w_down.shape, "final_norm": final_norm.shape}
    grads, deltas, new_m, new_v = [], [], [], []
    for name, (p, w, m, vv) in weights.items():
        g, d, nm, nv = _adamw(p, w, m, vv, "adamw_" + name)
        shape = shapes[name]
        grads.append(g.reshape(shape))
        deltas.append(d.reshape(shape))
        new_m.append(nm.reshape(shape))
        new_v.append(nv.reshape(shape))

    loss = lax.psum(loss_part[0, 0], ("x", "y", "c"))
    return (loss, grad_x, *grads, *deltas, *new_m, *new_v)
```

```python
import jax
import jax.numpy as jnp
from jax import lax
from jax.experimental import pallas as pl
from jax.experimental.pallas import tpu as pltpu

F32 = jnp.float32
BF16 = jnp.bfloat16

N_DEV = 8
D_MODEL = 1024
N_META = 16
PAD = 128 - N_META
CONV_DIM = 512
ATTN_DIM = 512
HEAD_DIM = 64
N_HEADS = 8
N_PAIRS = N_HEADS // 2
D_FF = 2816
N_CHUNK = 4
F_CHUNK = D_FF // N_CHUNK
IN_DIM = 3080
IN_PAD = 3200
IN_MAIN = 3072
EPS = 1e-6
NEG = -1e30
TQ = 128
VMEM_LIMIT = 56 * 1024 * 1024

ADAM_LR = 0.001
ADAM_B1 = 0.9
ADAM_B2 = 0.999
ADAM_EPS = 1e-08
ADAM_WD = 0.01
ADAM_STEP = 10

MESH = pl.DeviceIdType.MESH
ANY = pl.BlockSpec(memory_space=pl.ANY)


def _params(sem=None):
    return pltpu.CompilerParams(dimension_semantics=sem, vmem_limit_bytes=VMEM_LIMIT)


def _row_tile(n, prefer):
    for t in (prefer, 512, 256, 128):
        if t <= n and n % t == 0:
            return t
    raise ValueError(f"no row tile for {n}")


def _dot(a, b):
    return jnp.dot(a, b, preferred_element_type=F32)


def _dot_nt(a, b):
    return lax.dot_general(a, b, (((1,), (1,)), ((), ())), preferred_element_type=F32)


def _dot_tn(a, b):
    return lax.dot_general(a, b, (((0,), (0,)), ((), ())), preferred_element_type=F32)


def _rms(x, g):
    r = lax.rsqrt(jnp.mean(x * x, axis=-1, keepdims=True) + EPS)
    xhat = x * r
    return xhat * g, xhat, r


def _rms_bwd(dn, xhat, r, g):
    dxhat = dn * g
    return r * (dxhat - xhat * jnp.mean(dxhat * xhat, axis=-1, keepdims=True))


def _sigmoid(x):
    return 1.0 / (1.0 + jnp.exp(-x))


def _ffn_fwd(h, gain, wgu, wd, name):
    n = h.shape[0]
    tm = _row_tile(n, 512)

    def body(h_ref, g_ref, wgu_ref, wd_ref, out_ref, gate_ref, up_ref, n_scr, acc_scr):
        j = pl.program_id(1)

        @pl.when(j == 0)
        def _():
            y, _, _ = _rms(h_ref[...], g_ref[...])
            n_scr[...] = y.astype(BF16)
            acc_scr[...] = jnp.zeros_like(acc_scr)

        nb = n_scr[...]
        gate = _dot(nb, wgu_ref[0, 0])
        up = _dot(nb, wgu_ref[1, 0])
        gate_ref[0] = gate.astype(BF16)
        up_ref[0] = up.astype(BF16)
        act = (gate * _sigmoid(gate) * up).astype(BF16)
        acc_scr[...] += _dot(act, wd_ref[0])

        @pl.when(j == N_CHUNK - 1)
        def _():
            out_ref[...] = h_ref[...] + 0.5 * acc_scr[...]

    return pl.pallas_call(
        body, name=name, grid=(n // tm, N_CHUNK),
        in_specs=[pl.BlockSpec((tm, D_MODEL), lambda i, j: (i, 0)),
                  pl.BlockSpec((1, D_MODEL), lambda i, j: (0, 0)),
                  pl.BlockSpec((2, 1, D_MODEL, F_CHUNK), lambda i, j: (0, j, 0, 0)),
                  pl.BlockSpec((1, F_CHUNK, D_MODEL), lambda i, j: (j, 0, 0))],
        out_specs=[pl.BlockSpec((tm, D_MODEL), lambda i, j: (i, 0)),
                   pl.BlockSpec((1, tm, F_CHUNK), lambda i, j: (j, i, 0)),
                   pl.BlockSpec((1, tm, F_CHUNK), lambda i, j: (j, i, 0))],
        out_shape=[jax.ShapeDtypeStruct((n, D_MODEL), F32),
                   jax.ShapeDtypeStruct((N_CHUNK, n, F_CHUNK), BF16),
                   jax.ShapeDtypeStruct((N_CHUNK, n, F_CHUNK), BF16)],
        scratch_shapes=[pltpu.VMEM((tm, D_MODEL), BF16), pltpu.VMEM((tm, D_MODEL), F32)],
        compiler_params=_params(("parallel", "arbitrary")),
    )(h, gain, wgu, wd)


def _ffn_bwd_x(dh_out, h_in, gain, gate, up, wgu, wd, name):
    n = h_in.shape[0]
    tm = _row_tile(n, 512)

    def body(dh_ref, h_ref, g_ref, gate_ref, up_ref, wgu_ref, wd_ref,
             dhin_ref, dgate_ref, dup_ref, dgain_ref, dhb_scr, acc_scr):
        i = pl.program_id(0)
        j = pl.program_id(1)

        @pl.when((i == 0) & (j == 0))
        def _():
            dgain_ref[...] = jnp.zeros_like(dgain_ref)

        @pl.when(j == 0)
        def _():
            dhb_scr[...] = (0.5 * dh_ref[...]).astype(BF16)
            acc_scr[...] = jnp.zeros_like(acc_scr)

        da = _dot_nt(dhb_scr[...], wd_ref[0])
        g = gate_ref[0].astype(F32)
        u = up_ref[0].astype(F32)
        sig = _sigmoid(g)
        dgate = (da * u * (sig * (1.0 + g * (1.0 - sig)))).astype(BF16)
        dup = (da * (g * sig)).astype(BF16)
        dgate_ref[0] = dgate
        dup_ref[0] = dup
        acc_scr[...] += _dot_nt(dgate, wgu_ref[0, 0]) + _dot_nt(dup, wgu_ref[1, 0])

        @pl.when(j == N_CHUNK - 1)
        def _():
            gain_v = g_ref[...]
            _, xhat, r = _rms(h_ref[...], gain_v)
            dn = acc_scr[...]
            dhin_ref[...] = dh_ref[...] + _rms_bwd(dn, xhat, r, gain_v)
            dgain_ref[...] += jnp.sum(dn * xhat, axis=0, keepdims=True)

    chunk = pl.BlockSpec((1, tm, F_CHUNK), lambda i, j: (j, i, 0))
    rows = pl.BlockSpec((tm, D_MODEL), lambda i, j: (i, 0))
    vec = pl.BlockSpec((1, D_MODEL), lambda i, j: (0, 0))
    return pl.pallas_call(
        body, name=name, grid=(n // tm, N_CHUNK),
        in_specs=[rows, rows, vec, chunk, chunk,
                  pl.BlockSpec((2, 1, D_MODEL, F_CHUNK), lambda i, j: (0, j, 0, 0)),
                  pl.BlockSpec((1, F_CHUNK, D_MODEL), lambda i, j: (j, 0, 0))],
        out_specs=[rows, chunk, chunk, vec],
        out_shape=[jax.ShapeDtypeStruct((n, D_MODEL), F32),
                   jax.ShapeDtypeStruct((N_CHUNK, n, F_CHUNK), BF16),
                   jax.ShapeDtypeStruct((N_CHUNK, n, F_CHUNK), BF16),
                   jax.ShapeDtypeStruct((1, D_MODEL), F32)],
        scratch_shapes=[pltpu.VMEM((tm, D_MODEL), BF16), pltpu.VMEM((tm, D_MODEL), F32)],
        compiler_params=_params(("arbitrary", "arbitrary")),
    )(dh_out, h_in, gain, gate, up, wgu, wd)


def _ffn_bwd_w(dh_out, h_in, gain, gate, up, dgate, dup, name):
    n = h_in.shape[0]
    tm = _row_tile(n, 512)
    n_i = n // tm

    def body(dh_ref, h_ref, g_ref, gate_ref, up_ref, dgate_ref, dup_ref, dwgu_ref, dwd_ref,
             ag_scr, au_scr, ad_scr):
        i = pl.program_id(1)

        @pl.when(i == 0)
        def _():
            ag_scr[...] = jnp.zeros_like(ag_scr)
            au_scr[...] = jnp.zeros_like(au_scr)
            ad_scr[...] = jnp.zeros_like(ad_scr)

        y, _, _ = _rms(h_ref[...], g_ref[...])
        nb = y.astype(BF16)
        ag_scr[...] += _dot_tn(nb, dgate_ref[0])
        au_scr[...] += _dot_tn(nb, dup_ref[0])
        g = gate_ref[0].astype(F32)
        act = (g * _sigmoid(g) * up_ref[0].astype(F32)).astype(BF16)
        ad_scr[...] += _dot_tn(act, (0.5 * dh_ref[...]).astype(BF16))

        @pl.when(i == n_i - 1)
        def _():
            dwgu_ref[0, 0] = ag_scr[...].astype(BF16)
            dwgu_ref[1, 0] = au_scr[...].astype(BF16)
            dwd_ref[0] = ad_scr[...].astype(BF16)

    chunk = pl.BlockSpec((1, tm, F_CHUNK), lambda j, i: (j, i, 0))
    rows = pl.BlockSpec((tm, D_MODEL), lambda j, i: (i, 0))
    return pl.pallas_call(
        body, name=name, grid=(N_CHUNK, n_i),
        in_specs=[rows, rows, pl.BlockSpec((1, D_MODEL), lambda j, i: (0, 0)), chunk, chunk, chunk, chunk],
        out_specs=[pl.BlockSpec((2, 1, D_MODEL, F_CHUNK), lambda j, i: (0, j, 0, 0)),
                   pl.BlockSpec((1, F_CHUNK, D_MODEL), lambda j, i: (j, 0, 0))],
        out_shape=[jax.ShapeDtypeStruct((2, N_CHUNK, D_MODEL, F_CHUNK), BF16),
                   jax.ShapeDtypeStruct((N_CHUNK, F_CHUNK, D_MODEL), BF16)],
        scratch_shapes=[pltpu.VMEM((D_MODEL, F_CHUNK), F32), pltpu.VMEM((D_MODEL, F_CHUNK), F32),
                        pltpu.VMEM((F_CHUNK, D_MODEL), F32)],
        compiler_params=_params(("parallel", "arbitrary")),
    )(dh_out, h_in, gain, gate, up, dgate, dup)


N_PIECE = IN_MAIN // 512


def _inproj_fwd(h, gain, w_in):
    n = h.shape[0]
    tm = _row_tile(n, 512)

    def body(h_ref, g_ref, w_ref, *outs):
        y, _, _ = _rms(h_ref[...], g_ref[...])
        nb = y.astype(BF16)
        for p in range(N_PIECE):
            outs[p][...] = _dot(nb, w_ref[:, 512 * p:512 * (p + 1)]).astype(BF16)
        outs[N_PIECE][...] = _dot(nb, w_ref[:, IN_MAIN:IN_PAD])

    piece = pl.BlockSpec((tm, 512), lambda i: (i, 0))
    return pl.pallas_call(
        body, name="inproj_fwd", grid=(n // tm,),
        in_specs=[pl.BlockSpec((tm, D_MODEL), lambda i: (i, 0)),
                  pl.BlockSpec((1, D_MODEL), lambda i: (0, 0)),
                  pl.BlockSpec((D_MODEL, IN_PAD), lambda i: (0, 0))],
        out_specs=[piece] * N_PIECE + [pl.BlockSpec((tm, 128), lambda i: (i, 0))],
        out_shape=[jax.ShapeDtypeStruct((n, 512), BF16)] * N_PIECE + [jax.ShapeDtypeStruct((n, 128), F32)],
        compiler_params=_params(("parallel",)),
    )(h, gain, w_in)


def _inproj_bwd(dpieces, dfg, dh_out, h_in, gain, w_in):
    n = h_in.shape[0]
    tm = _row_tile(n, 512)
    n_i = n // tm

    def body(*refs):
        dp_refs = refs[:N_PIECE]
        dfg_ref, dh_ref, h_ref, g_ref, w_ref, dhin_ref, dw_ref, dgain_ref, acc_scr = refs[N_PIECE:]
        i = pl.program_id(0)

        @pl.when(i == 0)
        def _():
            acc_scr[...] = jnp.zeros_like(acc_scr)
            dgain_ref[...] = jnp.zeros_like(dgain_ref)

        gain_v = g_ref[...]
        y, xhat, r = _rms(h_ref[...], gain_v)
        nb = y.astype(BF16)
        dn = jnp.zeros((tm, D_MODEL), F32)
        for p in range(N_PIECE + 1):
            lo, hi = (512 * p, 512 * (p + 1)) if p < N_PIECE else (IN_MAIN, IN_PAD)
            dp = (dp_refs[p][...] if p < N_PIECE else dfg_ref[...]).astype(BF16)
            dn = dn + _dot_nt(dp, w_ref[:, lo:hi])
            acc_scr[:, lo:hi] += _dot_tn(nb, dp)
        dhin_ref[...] = dh_ref[...] + _rms_bwd(dn, xhat, r, gain_v)
        dgain_ref[...] += jnp.sum(dn * xhat, axis=0, keepdims=True)

        @pl.when(i == n_i - 1)
        def _():
            dw_ref[...] = acc_scr[...].astype(BF16)

    piece = pl.BlockSpec((tm, 512), lambda i: (i, 0))
    rows = pl.BlockSpec((tm, D_MODEL), lambda i: (i, 0))
    vec = pl.BlockSpec((1, D_MODEL), lambda i: (0, 0))
    wspec = pl.BlockSpec((D_MODEL, IN_PAD), lambda i: (0, 0))
    return pl.pallas_call(
        body, name="inproj_bwd", grid=(n_i,),
        in_specs=[piece] * N_PIECE + [pl.BlockSpec((tm, 128), lambda i: (i, 0)), rows, rows, vec, wspec],
        out_specs=[rows, wspec, vec],
        out_shape=[jax.ShapeDtypeStruct((n, D_MODEL), F32),
                   jax.ShapeDtypeStruct((D_MODEL, IN_PAD), BF16),
                   jax.ShapeDtypeStruct((1, D_MODEL), F32)],
        scratch_shapes=[pltpu.VMEM((D_MODEL, IN_PAD), F32)],
        compiler_params=_params(("arbitrary",)),
    )(*dpieces, dfg, dh_out, h_in, gain, w_in)


def _outproj_fwd(zc, za, w_out, h):
    n = h.shape[0]
    tm = _row_tile(n, 512)

    def body(zc_ref, za_ref, w_ref, h_ref, out_ref):
        out_ref[...] = (h_ref[...] + _dot(zc_ref[...], w_ref[0:CONV_DIM, :])
                        + _dot(za_ref[...], w_ref[CONV_DIM:, :]))

    half = pl.BlockSpec((tm, 512), lambda i: (i, 0))
    rows = pl.BlockSpec((tm, D_MODEL), lambda i: (i, 0))
    return pl.pallas_call(
        body, name="outproj_fwd", grid=(n // tm,),
        in_specs=[half, half, pl.BlockSpec((D_MODEL, D_MODEL), lambda i: (0, 0)), rows],
        out_specs=rows,
        out_shape=jax.ShapeDtypeStruct((n, D_MODEL), F32),
        compiler_params=_params(("parallel",)),
    )(zc, za, w_out, h)


def _outproj_bwd(dh, zc, za, w_out):
    n = dh.shape[0]
    tm = _row_tile(n, 512)
    n_i = n // tm

    def body(dh_ref, zc_ref, za_ref, w_ref, dzc_ref, dza_ref, dw_ref, acc_scr):
        i = pl.program_id(0)

        @pl.when(i == 0)
        def _():
            acc_scr[...] = jnp.zeros_like(acc_scr)

        dhb = dh_ref[...].astype(BF16)
        dzc_ref[...] = _dot_nt(dhb, w_ref[0:CONV_DIM, :]).astype(BF16)
        dza_ref[...] = _dot_nt(dhb, w_ref[CONV_DIM:, :]).astype(BF16)
        acc_scr[0:CONV_DIM, :] += _dot_tn(zc_ref[...], dhb)
        acc_scr[CONV_DIM:, :] += _dot_tn(za_ref[...], dhb)

        @pl.when(i == n_i - 1)
        def _():
            dw_ref[...] = acc_scr[...].astype(BF16)

    half = pl.BlockSpec((tm, 512), lambda i: (i, 0))
    wspec = pl.BlockSpec((D_MODEL, D_MODEL), lambda i: (0, 0))
    return pl.pallas_call(
        body, name="outproj_bwd", grid=(n_i,),
        in_specs=[pl.BlockSpec((tm, D_MODEL), lambda i: (i, 0)), half, half, wspec],
        out_specs=[half, half, wspec],
        out_shape=[jax.ShapeDtypeStruct((n, 512), BF16), jax.ShapeDtypeStruct((n, 512), BF16),
                   jax.ShapeDtypeStruct((D_MODEL, D_MODEL), BF16)],
        scratch_shapes=[pltpu.VMEM((D_MODEL, D_MODEL), F32)],
        compiler_params=_params(("arbitrary",)),
    )(dh, zc, za, w_out)


def _group_matrix():
    r = lax.broadcasted_iota(jnp.int32, (128, 128), 0) // HEAD_DIM
    c = lax.broadcasted_iota(jnp.int32, (128, 128), 1) // HEAD_DIM
    return jnp.where(r == c, 1.0 / HEAD_DIM, 0.0).astype(BF16)


def _group_mean(x, gmat):
    hi = x.astype(BF16)
    lo = (x - hi.astype(F32)).astype(BF16)
    return _dot(hi, gmat) + _dot(lo, gmat)


def _shift_rows(x, s):
    rows = x.shape[0]
    t = lax.broadcasted_iota(jnp.int32, x.shape, 0)
    rolled = pltpu.roll(x, s % rows, 0)
    keep = (t >= s) if s > 0 else (t < rows + s)
    return jnp.where(keep, rolled, 0.0)


def _conv_parts(bg_ref, cg_ref, hc_ref, w_ref):
    bg = bg_ref[...].astype(F32)
    cg = cg_ref[...].astype(F32)
    hc = hc_ref[...].astype(F32)
    u = cg * hc
    u1 = _shift_rows(u, 1)
    u2 = _shift_rows(u, 2)
    conv = w_ref[2:3, :] * u + w_ref[1:2, :] * u1 + w_ref[0:1, :] * u2
    return bg, cg, hc, u, u1, u2, conv


def _conv_fwd(bg, cg, hc, conv_w, gain, gmat, lp):
    n = bg.shape[0]
    nb = n // lp

    def body(bg_ref, cg_ref, hc_ref, w_ref, g_ref, gm_ref, z_ref):
        bgv, _, _, _, _, _, conv = _conv_parts(bg_ref, cg_ref, hc_ref, w_ref)
        yc = bgv * conv
        r = lax.rsqrt(_group_mean(yc * yc, gm_ref[...]) + EPS)
        z_ref[...] = (yc * r * g_ref[...]).astype(BF16)

    blk = pl.BlockSpec((lp, 128), lambda c, b: (b, c))
    return pl.pallas_call(
        body, name="conv_fwd", grid=(CONV_DIM // 128, nb),
        in_specs=[blk, blk, blk, pl.BlockSpec((3, 128), lambda c, b: (0, c)),
                  pl.BlockSpec((1, 128), lambda c, b: (0, c)), pl.BlockSpec((128, 128), lambda c, b: (0, 0))],
        out_specs=blk,
        out_shape=jax.ShapeDtypeStruct((n, CONV_DIM), BF16),
        compiler_params=_params(("parallel", "parallel")),
    )(bg, cg, hc, conv_w, gain, gmat)


def _conv_bwd(dz, bg, cg, hc, conv_w, gain, gmat, lp):
    n = bg.shape[0]
    nb = n // lp

    def body(dz_ref, bg_ref, cg_ref, hc_ref, w_ref, g_ref, gm_ref,
             dbg_ref, dcg_ref, dhc_ref, dw_ref, dgain_ref):
        b = pl.program_id(1)

        @pl.when(b == 0)
        def _():
            dw_ref[...] = jnp.zeros_like(dw_ref)
            dgain_ref[...] = jnp.zeros_like(dgain_ref)

        bgv, cgv, hcv, u, u1, u2, conv = _conv_parts(bg_ref, cg_ref, hc_ref, w_ref)
        gm = gm_ref[...]
        yc = bgv * conv
        r = lax.rsqrt(_group_mean(yc * yc, gm) + EPS)
        yhat = yc * r
        dzv = dz_ref[...].astype(F32)
        dyhat = dzv * g_ref[...]
        dgain_ref[...] += jnp.sum(dzv * yhat, axis=0, keepdims=True)
        dyc = r * (dyhat - yhat * _group_mean(dyhat * yhat, gm))
        dbg_ref[...] = (dyc * conv).astype(BF16)
        dconv = dyc * bgv
        du = (w_ref[2:3, :] * dconv + w_ref[1:2, :] * _shift_rows(dconv, -1)
              + w_ref[0:1, :] * _shift_rows(dconv, -2))
        dcg_ref[...] = (du * hcv).astype(BF16)
        dhc_ref[...] = (du * cgv).astype(BF16)
        dw_ref[0:1, :] += jnp.sum(dconv * u2, axis=0, keepdims=True)
        dw_ref[1:2, :] += jnp.sum(dconv * u1, axis=0, keepdims=True)
        dw_ref[2:3, :] += jnp.sum(dconv * u, axis=0, keepdims=True)

    blk = pl.BlockSpec((lp, 128), lambda c, b: (b, c))
    wspec = pl.BlockSpec((3, 128), lambda c, b: (0, c))
    gspec = pl.BlockSpec((1, 128), lambda c, b: (0, c))
    return pl.pallas_call(
        body, name="conv_bwd", grid=(CONV_DIM // 128, nb),
        in_specs=[blk, blk, blk, blk, wspec, gspec, pl.BlockSpec((128, 128), lambda c, b: (0, 0))],
        out_specs=[blk, blk, blk, wspec, gspec],
        out_shape=[jax.ShapeDtypeStruct((n, CONV_DIM), BF16)] * 3
        + [jax.ShapeDtypeStruct((3, CONV_DIM), F32), jax.ShapeDtypeStruct((1, CONV_DIM), F32)],
        compiler_params=_params(("parallel", "arbitrary")),
    )(dz, bg, cg, hc, conv_w, gain, gmat)


def _scan_steps(rows):
    s, out = 1, []
    while s < rows:
        out.append(s)
        s *= 2
    return out


KEY_MASKED = 1e30
ONE_LANE = 24


def _fgate_fwd(fg, b_f, lp):
    n = fg.shape[0]
    nb = n // lp

    def body(fg_ref, b_ref, ka_ref, qa_ref):
        x = fg_ref[...] + b_ref[...]
        logf = jnp.minimum(x, 0.0) - jnp.log(1.0 + jnp.exp(-jnp.abs(x)))
        t = lax.broadcasted_iota(jnp.int32, (lp, 128), 0)
        lane = lax.broadcasted_iota(jnp.int32, (lp, 128), 1)
        f = jnp.where((t >= PAD) & (lane < N_HEADS), logf, 0.0)
        for s in _scan_steps(lp):
            f = f + _shift_rows(f, s)
        hi = f.astype(BF16).astype(F32)
        rest = f - hi
        mid = rest.astype(BF16).astype(F32)
        lo = (rest - mid).astype(BF16).astype(F32)
        ones = jnp.where((lane >= ONE_LANE) & (lane < ONE_LANE + 3), 1.0, 0.0)
        hi_key = jnp.where((t < PAD) & (lane < N_HEADS), KEY_MASKED, hi)
        ka_ref[...] = (hi_key + pltpu.roll(mid, 8, 1) + pltpu.roll(lo, 16, 1) + ones).astype(BF16)
        for h in range(N_HEADS):
            minus = jnp.where((lane == h) | (lane == 8 + h) | (lane == 16 + h), -1.0, 0.0)
            terms = (jnp.where(lane == ONE_LANE, pltpu.roll(hi, ONE_LANE - h, 1), 0.0)
                     + jnp.where(lane == ONE_LANE + 1, pltpu.roll(mid, ONE_LANE + 1 - h, 1), 0.0)
                     + jnp.where(lane == ONE_LANE + 2, pltpu.roll(lo, ONE_LANE + 2 - h, 1), 0.0))
            qa_ref[:, 128 * h:128 * (h + 1)] = (minus + terms).astype(BF16)

    return pl.pallas_call(
        body, name="fgate_fwd", grid=(nb,),
        in_specs=[pl.BlockSpec((lp, 128), lambda b: (b, 0)), pl.BlockSpec((1, 128), lambda b: (0, 0))],
        out_specs=[pl.BlockSpec((lp, 128), lambda b: (b, 0)), pl.BlockSpec((lp, N_HEADS * 128), lambda b: (b, 0))],
        out_shape=[jax.ShapeDtypeStruct((n, 128), BF16), jax.ShapeDtypeStruct((n, N_HEADS * 128), BF16)],
        compiler_params=_params(("parallel",)),
    )(fg, b_f)


def _fgate_bwd(dka, dfr, fg, b_f, lp):
    n = fg.shape[0]
    nb = n // lp

    def body(dka_ref, dfr_ref, fg_ref, b_ref, dfg_ref, db_ref):
        b = pl.program_id(0)

        @pl.when(b == 0)
        def _():
            db_ref[...] = jnp.zeros_like(db_ref)

        wide = jnp.concatenate([dfr_ref[0], jnp.zeros((128 - N_HEADS, lp), F32)], axis=0)
        lane0 = lax.broadcasted_iota(jnp.int32, (lp, 128), 1)
        d = jnp.where(lane0 < N_HEADS, dka_ref[...], 0.0) + wide.T
        for s in _scan_steps(lp):
            d = d + _shift_rows(d, -s)
        t = lax.broadcasted_iota(jnp.int32, (lp, 128), 0)
        lane = lax.broadcasted_iota(jnp.int32, (lp, 128), 1)
        x = fg_ref[...] + b_ref[...]
        dx = jnp.where((t >= PAD) & (lane < N_HEADS), d * _sigmoid(-x), 0.0)
        dfg_ref[...] = dx
        db_ref[...] += jnp.sum(dx, axis=0, keepdims=True)

    return pl.pallas_call(
        body, name="fgate_bwd", grid=(nb,),
        in_specs=[pl.BlockSpec((lp, 128), lambda b: (b, 0)), pl.BlockSpec((1, N_HEADS, lp), lambda b: (b, 0, 0)),
                  pl.BlockSpec((lp, 128), lambda b: (b, 0)), pl.BlockSpec((1, 128), lambda b: (0, 0))],
        out_specs=[pl.BlockSpec((lp, 128), lambda b: (b, 0)), pl.BlockSpec((1, 128), lambda b: (0, 0))],
        out_shape=[jax.ShapeDtypeStruct((n, 128), F32), jax.ShapeDtypeStruct((1, 128), F32)],
        compiler_params=_params(("arbitrary",)),
    )(dka, dfr, fg, b_f)


def _head_masks():
    lane = lax.broadcasted_iota(jnp.int32, (1, 128), 1)
    return lane < HEAD_DIM


def _stack_heads(x2, first):
    zero = jnp.zeros_like(x2)
    return jnp.concatenate([jnp.where(first, x2, zero), jnp.where(first, zero, x2)], axis=0)


def _pair_cols(col0, col1, first):
    return jnp.where(first, col0, col1)


def _score_mask(i, j):
    r = lax.broadcasted_iota(jnp.int32, (2 * TQ, TQ), 0)
    c = lax.broadcasted_iota(jnp.int32, (2 * TQ, TQ), 1)
    qpos = i * TQ + (r & (TQ - 1))
    kpos = j * TQ + c
    return (kpos <= qpos) & (kpos >= PAD)


def _causal_t():
    r = lax.broadcasted_iota(jnp.int32, (TQ, 2 * TQ), 0)
    c = lax.broadcasted_iota(jnp.int32, (TQ, 2 * TQ), 1)
    return r <= (c & (TQ - 1))


def _query_side(q_ref, qa_ref, p, first):
    q2 = q_ref[:, 128 * p:128 * (p + 1)] * 0.125
    zero = jnp.zeros_like(q2)
    top = jnp.concatenate([jnp.where(first, q2, zero), qa_ref[:, 128 * (2 * p):128 * (2 * p + 1)]], axis=1)
    bot = jnp.concatenate([jnp.where(first, zero, q2), qa_ref[:, 128 * (2 * p + 1):128 * (2 * p + 2)]], axis=1)
    return jnp.concatenate([top, bot], axis=0)


def _pair_rows(row0, row1):
    r = lax.broadcasted_iota(jnp.int32, (128, TQ), 0)
    return jnp.where(r < HEAD_DIM, row0, row1)


TK = 256


def _key_chunks(lp):
    return (lp + TK - 1) // TK


def _chunk_mask(i, c):
    r = lax.broadcasted_iota(jnp.int32, (TK, 2 * TQ), 0)
    col = lax.broadcasted_iota(jnp.int32, (TK, 2 * TQ), 1)
    return (c * TK + r) <= (i * TQ + (col & (TQ - 1)))


def _transpose_bf16(x):
    return x.astype(F32).T.astype(BF16)


def _stack_heads_lanes(xt):
    r = lax.broadcasted_iota(jnp.int32, xt.shape, 0)
    zero = jnp.zeros_like(xt)
    return jnp.concatenate([jnp.where(r < HEAD_DIM, xt, zero), jnp.where(r < HEAD_DIM, zero, xt)], axis=1)


def _attn_fwd(q, qa, k, v, ka, gain, lp):
    n = q.shape[0]
    nb = n // lp
    nq = lp // TQ
    lpp = _key_chunks(lp) * TK

    def body(q_ref, qa_ref, k_ref, v_ref, ka_ref, g_ref, z_ref, o_ref, lse_ref, kx_scr, vt_scr):
        i = pl.program_id(1)
        first = _head_masks()

        @pl.when(i == 0)
        def _():
            if lpp > lp:
                kx_scr[lp:lpp, :] = jnp.zeros((lpp - lp, 2 * ATTN_DIM), BF16)
                vt_scr[:, lp:lpp] = jnp.zeros((ATTN_DIM, lpp - lp), BF16)
            for p in range(N_PAIRS):
                kx_scr[0:lp, 256 * p:256 * p + 128] = k_ref[:, 128 * p:128 * (p + 1)]
                kx_scr[0:lp, 256 * p + 128:256 * (p + 1)] = ka_ref[...]
            vt_scr[:, 0:lp] = _transpose_bf16(v_ref[...])

        rhs_t = [_transpose_bf16(_query_side(q_ref, qa_ref, p, first)) for p in range(N_PAIRS)]

        def step(c, carry):
            koff = pl.multiple_of(c * TK, TK)
            valid = _chunk_mask(i, c)
            new = []
            for p in range(N_PAIRS):
                m, l, acc = carry[p]
                st = _dot(kx_scr[pl.ds(koff, TK), 256 * p:256 * (p + 1)], rhs_t[p])
                st = jnp.where(valid, st, NEG)
                m_new = jnp.maximum(m, jnp.max(st, axis=0, keepdims=True))
                pt = jnp.exp(st - m_new)
                alpha = jnp.exp(m - m_new)
                l = alpha * l + jnp.sum(pt, axis=0, keepdims=True)
                pb = pt.astype(BF16)
                vt = _stack_heads_lanes(vt_scr[128 * p:128 * (p + 1), pl.ds(koff, TK)])
                pv = _dot(vt, jnp.concatenate([pb[:, 0:TQ], pb[:, TQ:]], axis=0))
                acc = acc * _pair_rows(alpha[:, 0:TQ], alpha[:, TQ:]) + pv
                new.append((m_new, l, acc))
            return tuple(new)

        init = tuple((jnp.full((1, 2 * TQ), NEG, F32), jnp.zeros((1, 2 * TQ), F32), jnp.zeros((128, TQ), F32))
                     for _ in range(N_PAIRS))
        final = lax.fori_loop(0, (i + 2) // 2, step, init)

        row = lax.broadcasted_iota(jnp.int32, (TQ, 128), 0)
        real = (i * TQ + row) >= PAD
        for p in range(N_PAIRS):
            m, l, acc = final[p]
            inv = 1.0 / l
            ot = acc * _pair_rows(inv[:, 0:TQ], inv[:, TQ:])
            sq = ot * ot
            r0 = lax.rsqrt(jnp.sum(sq[0:HEAD_DIM], axis=0, keepdims=True) * (1.0 / HEAD_DIM) + EPS)
            r1 = lax.rsqrt(jnp.sum(sq[HEAD_DIM:], axis=0, keepdims=True) * (1.0 / HEAD_DIM) + EPS)
            cols = slice(128 * p, 128 * (p + 1))
            o_ref[:, cols] = jnp.where(real, ot.T, 0.0).astype(BF16)
            z_ref[:, cols] = (jnp.where(real, (ot * _pair_rows(r0, r1)).T, 0.0) * g_ref[:, cols]).astype(BF16)
            lse = m + jnp.log(l)
            lse_ref[0, 2 * p:2 * p + 1, :] = lse[:, 0:TQ]
            lse_ref[0, 2 * p + 1:2 * p + 2, :] = lse[:, TQ:]

    qblk = pl.BlockSpec((TQ, ATTN_DIM), lambda b, i: (b * nq + i, 0))
    qablk = pl.BlockSpec((TQ, N_HEADS * 128), lambda b, i: (b * nq + i, 0))
    seq = pl.BlockSpec((lp, ATTN_DIM), lambda b, i: (b, 0))
    rowblk = pl.BlockSpec((1, N_HEADS, TQ), lambda b, i: (b, 0, i))
    return pl.pallas_call(
        body, name="attn_fwd", grid=(nb, nq),
        in_specs=[qblk, qablk, seq, seq, pl.BlockSpec((lp, 128), lambda b, i: (b, 0)),
                  pl.BlockSpec((1, ATTN_DIM), lambda b, i: (0, 0))],
        out_specs=[qblk, qblk, rowblk],
        out_shape=[jax.ShapeDtypeStruct((n, ATTN_DIM), BF16), jax.ShapeDtypeStruct((n, ATTN_DIM), BF16),
                   jax.ShapeDtypeStruct((nb, N_HEADS, lp), F32)],
        scratch_shapes=[pltpu.VMEM((lpp, 2 * ATTN_DIM), BF16), pltpu.VMEM((ATTN_DIM, lpp), BF16)],
        compiler_params=_params(("parallel", "arbitrary")),
    )(q, qa, k, v, ka, gain)


def _attn_bwd(dz, q, qa, k, v, ka, o, lse, gain, lp):
    n = q.shape[0]
    nb = n // lp
    nq = lp // TQ
    lpp = _key_chunks(lp) * TK

    def body(dz_ref, q_ref, qa_ref, k_ref, v_ref, ka_ref, o_ref, lse_ref, g_ref,
             dq_ref, dk_ref, dv_ref, dka_ref, dfr_ref, dgain_ref,
             kx_scr, vx_scr, kt_scr, dkx_scr, dvx_scr):
        b = pl.program_id(0)
        i = pl.program_id(1)
        first = _head_masks()

        @pl.when((b == 0) & (i == 0))
        def _():
            dgain_ref[...] = jnp.zeros_like(dgain_ref)

        @pl.when(i == 0)
        def _():
            if lpp > lp:
                kx_scr[lp:lpp, :] = jnp.zeros((lpp - lp, 2 * ATTN_DIM), BF16)
                vx_scr[lp:lpp, :] = jnp.zeros((lpp - lp, ATTN_DIM), BF16)
                kt_scr[:, lp:lpp] = jnp.zeros((ATTN_DIM, lpp - lp), BF16)
            for p in range(N_PAIRS):
                kx_scr[0:lp, 256 * p:256 * p + 128] = k_ref[:, 128 * p:128 * (p + 1)]
                kx_scr[0:lp, 256 * p + 128:256 * (p + 1)] = ka_ref[...]
            vx_scr[0:lp, :] = v_ref[...]
            kt_scr[:, 0:lp] = _transpose_bf16(k_ref[...])
            dkx_scr[...] = jnp.zeros_like(dkx_scr)
            dvx_scr[...] = jnp.zeros_like(dvx_scr)

        rhs, rhs_t, lses, dos, dos_t, deltas = [], [], [], [], [], []
        for p in range(N_PAIRS):
            cols = slice(128 * p, 128 * (p + 1))
            side = _query_side(q_ref, qa_ref, p, first)
            rhs.append(side)
            rhs_t.append(_transpose_bf16(side))
            lses.append(jnp.concatenate([lse_ref[0, 2 * p:2 * p + 1, :], lse_ref[0, 2 * p + 1:2 * p + 2, :]], axis=1))
            ov = o_ref[:, cols].astype(F32)
            dzv = dz_ref[:, cols].astype(F32)
            gv = g_ref[:, cols]
            sq = ov * ov
            ms0 = jnp.sum(jnp.where(first, sq, 0.0), axis=1, keepdims=True) * (1.0 / HEAD_DIM)
            ms1 = jnp.sum(jnp.where(first, 0.0, sq), axis=1, keepdims=True) * (1.0 / HEAD_DIM)
            r = _pair_cols(lax.rsqrt(ms0 + EPS), lax.rsqrt(ms1 + EPS), first)
            ohat = ov * r
            dyhat = dzv * gv
            dgain_ref[:, cols] += jnp.sum(dzv * ohat, axis=0, keepdims=True)
            pr = dyhat * ohat
            mean0 = jnp.sum(jnp.where(first, pr, 0.0), axis=1, keepdims=True) * (1.0 / HEAD_DIM)
            mean1 = jnp.sum(jnp.where(first, 0.0, pr), axis=1, keepdims=True) * (1.0 / HEAD_DIM)
            do = r * (dyhat - ohat * _pair_cols(mean0, mean1, first))
            ddt = (do * ov).T
            deltas.append(jnp.concatenate([jnp.sum(ddt[0:HEAD_DIM], axis=0, keepdims=True),
                                           jnp.sum(ddt[HEAD_DIM:], axis=0, keepdims=True)], axis=1))
            do_st = _stack_heads(do.astype(BF16), first)
            dos.append(do_st)
            dos_t.append(_transpose_bf16(do_st))

        def step(c, carry):
            koff = pl.multiple_of(c * TK, TK)
            valid = _chunk_mask(i, c)
            new = []
            for p in range(N_PAIRS):
                dqt, dfq = carry[p]
                ext = slice(256 * p, 256 * (p + 1))
                cols = slice(128 * p, 128 * (p + 1))
                st = _dot(kx_scr[pl.ds(koff, TK), ext], rhs_t[p])
                st = jnp.where(valid, st, NEG)
                pt = jnp.exp(st - lses[p])
                dpt = _dot(vx_scr[pl.ds(koff, TK), cols], dos_t[p])
                dst = pt * (dpt - deltas[p])
                dsb = dst.astype(BF16)
                dfq = dfq + jnp.sum(dsb.astype(F32), axis=0, keepdims=True)
                dkx_scr[pl.ds(koff, TK), ext] += _dot(dsb, rhs[p])
                dvx_scr[pl.ds(koff, TK), cols] += _dot(pt.astype(BF16), dos[p])
                kt = _stack_heads_lanes(kt_scr[cols, pl.ds(koff, TK)])
                dqt = dqt + _dot(kt, jnp.concatenate([dsb[:, 0:TQ], dsb[:, TQ:]], axis=0))
                new.append((dqt, dfq))
            return tuple(new)

        init = tuple((jnp.zeros((128, TQ), F32), jnp.zeros((1, 2 * TQ), F32)) for _ in range(N_PAIRS))
        final = lax.fori_loop(0, (i + 2) // 2, step, init)

        for p in range(N_PAIRS):
            dqt, dfq = final[p]
            dq_ref[:, 128 * p:128 * (p + 1)] = (dqt.T * 0.125).astype(BF16)
            dfr_ref[0, 2 * p:2 * p + 1, :] = dfq[:, 0:TQ]
            dfr_ref[0, 2 * p + 1:2 * p + 2, :] = dfq[:, TQ:]

        @pl.when(i == nq - 1)
        def _():
            dka = jnp.zeros((lp, 128), F32)
            for p in range(N_PAIRS):
                dk_ref[:, 128 * p:128 * (p + 1)] = dkx_scr[0:lp, 256 * p:256 * p + 128].astype(BF16)
                dka = dka + dkx_scr[0:lp, 256 * p + 128:256 * (p + 1)]
            dka_ref[...] = dka
            dv_ref[...] = dvx_scr[0:lp, :].astype(BF16)

    qblk = pl.BlockSpec((TQ, ATTN_DIM), lambda b, i: (b * nq + i, 0))
    qablk = pl.BlockSpec((TQ, N_HEADS * 128), lambda b, i: (b * nq + i, 0))
    seq = pl.BlockSpec((lp, ATTN_DIM), lambda b, i: (b, 0))
    kaseq = pl.BlockSpec((lp, 128), lambda b, i: (b, 0))
    rowblk = pl.BlockSpec((1, N_HEADS, TQ), lambda b, i: (b, 0, i))
    gspec = pl.BlockSpec((1, ATTN_DIM), lambda b, i: (0, 0))
    return pl.pallas_call(
        body, name="attn_bwd", grid=(nb, nq),
        in_specs=[qblk, qblk, qablk, seq, seq, kaseq, qblk, rowblk, gspec],
        out_specs=[qblk, seq, seq, kaseq, rowblk, gspec],
        out_shape=[jax.ShapeDtypeStruct((n, ATTN_DIM), BF16), jax.ShapeDtypeStruct((n, ATTN_DIM), BF16),
                   jax.ShapeDtypeStruct((n, ATTN_DIM), BF16), jax.ShapeDtypeStruct((n, 128), F32),
                   jax.ShapeDtypeStruct((nb, N_HEADS, lp), F32), jax.ShapeDtypeStruct((1, ATTN_DIM), F32)],
        scratch_shapes=[pltpu.VMEM((lpp, 2 * ATTN_DIM), BF16), pltpu.VMEM((lpp, ATTN_DIM), BF16),
                        pltpu.VMEM((ATTN_DIM, lpp), BF16), pltpu.VMEM((lpp, 2 * ATTN_DIM), F32),
                        pltpu.VMEM((lpp, ATTN_DIM), F32)],
        compiler_params=_params(("arbitrary", "arbitrary")),
    )(dz, q, qa, k, v, ka, o, lse, gain)


def _attn_fwd_v2(q, qa, k, v, ka, gain, lp):
    n = q.shape[0]
    nb = n // lp
    nq = lp // TQ

    def body(q_ref, qa_ref, k_ref, v_ref, ka_ref, g_ref, z_ref, o_ref, lse_ref):
        i = pl.program_id(1)
        first = _head_masks()
        rhs = [_query_side(q_ref, qa_ref, p, first) for p in range(N_PAIRS)]

        def step(j, carry, diag):
            koff = pl.multiple_of(j * TQ, TQ)
            kav = ka_ref[pl.ds(koff, TQ), :]
            new = []
            for p in range(N_PAIRS):
                m, l, acc = carry[p]
                cols = slice(128 * p, 128 * (p + 1))
                k2 = k_ref[pl.ds(koff, TQ), cols]
                v2 = v_ref[pl.ds(koff, TQ), cols]
                st = _dot_nt(jnp.concatenate([k2, kav], axis=1), rhs[p])
                if diag:
                    st = jnp.where(_causal_t(), st, NEG)
                m_new = jnp.maximum(m, jnp.max(st, axis=0, keepdims=True))
                pt = jnp.exp(st - m_new)
                alpha = jnp.exp(m - m_new)
                l = alpha * l + jnp.sum(pt, axis=0, keepdims=True)
                pb = pt.astype(BF16)
                pv = _dot_tn(_stack_heads(v2, first), jnp.concatenate([pb[:, 0:TQ], pb[:, TQ:]], axis=0))
                acc = acc * _pair_rows(alpha[:, 0:TQ], alpha[:, TQ:]) + pv
                new.append((m_new, l, acc))
            return tuple(new)

        init = tuple((jnp.full((1, 2 * TQ), NEG, F32), jnp.zeros((1, 2 * TQ), F32), jnp.zeros((128, TQ), F32))
                     for _ in range(N_PAIRS))
        carry = lax.fori_loop(0, i, lambda j, c: step(j, c, False), init)
        final = step(i, carry, True)

        row = lax.broadcasted_iota(jnp.int32, (TQ, 128), 0)
        real = (i * TQ + row) >= PAD
        for p in range(N_PAIRS):
            m, l, acc = final[p]
            inv = 1.0 / l
            ot = acc * _pair_rows(inv[:, 0:TQ], inv[:, TQ:])
            sq = ot * ot
            r0 = lax.rsqrt(jnp.sum(sq[0:HEAD_DIM], axis=0, keepdims=True) * (1.0 / HEAD_DIM) + EPS)
            r1 = lax.rsqrt(jnp.sum(sq[HEAD_DIM:], axis=0, keepdims=True) * (1.0 / HEAD_DIM) + EPS)
            cols = slice(128 * p, 128 * (p + 1))
            o_ref[:, cols] = jnp.where(real, ot.T, 0.0).astype(BF16)
            z_ref[:, cols] = (jnp.where(real, (ot * _pair_rows(r0, r1)).T, 0.0) * g_ref[:, cols]).astype(BF16)
            lse = m + jnp.log(l)
            lse_ref[0, 2 * p:2 * p + 1, :] = lse[:, 0:TQ]
            lse_ref[0, 2 * p + 1:2 * p + 2, :] = lse[:, TQ:]

    qblk = pl.BlockSpec((TQ, ATTN_DIM), lambda b, i: (b * nq + i, 0))
    qablk = pl.BlockSpec((TQ, N_HEADS * 128), lambda b, i: (b * nq + i, 0))
    seq = pl.BlockSpec((lp, ATTN_DIM), lambda b, i: (b, 0))
    rowblk = pl.BlockSpec((1, N_HEADS, TQ), lambda b, i: (b, 0, i))
    return pl.pallas_call(
        body, name="attn_fwd", grid=(nb, nq),
        in_specs=[qblk, qablk, seq, seq, pl.BlockSpec((lp, 128), lambda b, i: (b, 0)),
                  pl.BlockSpec((1, ATTN_DIM), lambda b, i: (0, 0))],
        out_specs=[qblk, qblk, rowblk],
        out_shape=[jax.ShapeDtypeStruct((n, ATTN_DIM), BF16), jax.ShapeDtypeStruct((n, ATTN_DIM), BF16),
                   jax.ShapeDtypeStruct((nb, N_HEADS, lp), F32)],
        compiler_params=_params(("parallel", "parallel")),
    )(q, qa, k, v, ka, gain)


def _attn_bwd_v2(dz, q, qa, k, v, ka, o, lse, gain, lp):
    n = q.shape[0]
    nb = n // lp
    nq = lp // TQ

    def body(dz_ref, q_ref, qa_ref, k_ref, v_ref, ka_ref, o_ref, lse_ref, g_ref,
             dq_ref, dk_ref, dv_ref, dka_ref, dfr_ref, dgain_ref):
        b = pl.program_id(0)
        i = pl.program_id(1)
        first = _head_masks()

        @pl.when((b == 0) & (i == 0))
        def _():
            dgain_ref[...] = jnp.zeros_like(dgain_ref)

        @pl.when(i == 0)
        def _():
            dk_ref[...] = jnp.zeros_like(dk_ref)
            dv_ref[...] = jnp.zeros_like(dv_ref)
            dka_ref[...] = jnp.zeros_like(dka_ref)

        rhs, lses, dos, deltas = [], [], [], []
        for p in range(N_PAIRS):
            cols = slice(128 * p, 128 * (p + 1))
            rhs.append(_query_side(q_ref, qa_ref, p, first))
            lses.append(jnp.concatenate([lse_ref[0, 2 * p:2 * p + 1, :], lse_ref[0, 2 * p + 1:2 * p + 2, :]], axis=1))
            ov = o_ref[:, cols].astype(F32)
            dzv = dz_ref[:, cols].astype(F32)
            gv = g_ref[:, cols]
            sq = ov * ov
            ms0 = jnp.sum(jnp.where(first, sq, 0.0), axis=1, keepdims=True) * (1.0 / HEAD_DIM)
            ms1 = jnp.sum(jnp.where(first, 0.0, sq), axis=1, keepdims=True) * (1.0 / HEAD_DIM)
            r = _pair_cols(lax.rsqrt(ms0 + EPS), lax.rsqrt(ms1 + EPS), first)
            ohat = ov * r
            dyhat = dzv * gv
            dgain_ref[:, cols] += jnp.sum(dzv * ohat, axis=0, keepdims=True)
            pr = dyhat * ohat
            mean0 = jnp.sum(jnp.where(first, pr, 0.0), axis=1, keepdims=True) * (1.0 / HEAD_DIM)
            mean1 = jnp.sum(jnp.where(first, 0.0, pr), axis=1, keepdims=True) * (1.0 / HEAD_DIM)
            do = r * (dyhat - ohat * _pair_cols(mean0, mean1, first))
            ddt = (do * ov).T
            deltas.append(jnp.concatenate([jnp.sum(ddt[0:HEAD_DIM], axis=0, keepdims=True),
                                           jnp.sum(ddt[HEAD_DIM:], axis=0, keepdims=True)], axis=1))
            dos.append(_stack_heads(do.astype(BF16), first))

        def step(j, carry, diag):
            koff = pl.multiple_of(j * TQ, TQ)
            kav = ka_ref[pl.ds(koff, TQ), :]
            dka = jnp.zeros((TQ, 128), F32)
            new = []
            for p in range(N_PAIRS):
                dq_acc, dfq = carry[p]
                cols = slice(128 * p, 128 * (p + 1))
                k2 = k_ref[pl.ds(koff, TQ), cols]
                v2 = v_ref[pl.ds(koff, TQ), cols]
                st = _dot_nt(jnp.concatenate([k2, kav], axis=1), rhs[p])
                if diag:
                    st = jnp.where(_causal_t(), st, NEG)
                pt = jnp.exp(st - lses[p])
                dpt = _dot_nt(v2, dos[p])
                dst = pt * (dpt - deltas[p])
                dsb = dst.astype(BF16)
                dfq = dfq + jnp.sum(dsb.astype(F32), axis=0, keepdims=True)
                dk_ext = _dot(dsb, rhs[p])
                dk_ref[pl.ds(koff, TQ), cols] += dk_ext[:, 0:128]
                dka = dka + dk_ext[:, 128:]
                dv_ref[pl.ds(koff, TQ), cols] += _dot(pt.astype(BF16), dos[p])
                dq_acc = dq_acc + _dot_tn(jnp.concatenate([dsb[:, 0:TQ], dsb[:, TQ:]], axis=0),
                                          _stack_heads(k2, first))
                new.append((dq_acc, dfq))
            dka_ref[pl.ds(koff, TQ), :] += dka
            return tuple(new)

        init = tuple((jnp.zeros((TQ, 128), F32), jnp.zeros((1, 2 * TQ), F32)) for _ in range(N_PAIRS))
        carry = lax.fori_loop(0, i, lambda j, c: step(j, c, False), init)
        final = step(i, carry, True)

        for p in range(N_PAIRS):
            dq_acc, dfq = final[p]
            dq_ref[:, 128 * p:128 * (p + 1)] = (dq_acc * 0.125).astype(BF16)
            dfr_ref[0, 2 * p:2 * p + 1, :] = dfq[:, 0:TQ]
            dfr_ref[0, 2 * p + 1:2 * p + 2, :] = dfq[:, TQ:]

    qblk = pl.BlockSpec((TQ, ATTN_DIM), lambda b, i: (b * nq + i, 0))
    qablk = pl.BlockSpec((TQ, N_HEADS * 128), lambda b, i: (b * nq + i, 0))
    seq = pl.BlockSpec((lp, ATTN_DIM), lambda b, i: (b, 0))
    kaseq = pl.BlockSpec((lp, 128), lambda b, i: (b, 0))
    rowblk = pl.BlockSpec((1, N_HEADS, TQ), lambda b, i: (b, 0, i))
    gspec = pl.BlockSpec((1, ATTN_DIM), lambda b, i: (0, 0))
    return pl.pallas_call(
        body, name="attn_bwd", grid=(nb, nq),
        in_specs=[qblk, qblk, qablk, seq, seq, kaseq, qblk, rowblk, gspec],
        out_specs=[qblk, seq, seq, kaseq, rowblk, gspec],
        out_shape=[jax.ShapeDtypeStruct((n, ATTN_DIM), BF16), jax.ShapeDtypeStruct((n, ATTN_DIM), F32),
                   jax.ShapeDtypeStruct((n, ATTN_DIM), F32), jax.ShapeDtypeStruct((n, 128), F32),
                   jax.ShapeDtypeStruct((nb, N_HEADS, lp), F32), jax.ShapeDtypeStruct((1, ATTN_DIM), F32)],
        compiler_params=_params(("arbitrary", "arbitrary")),
    )(dz, q, qa, k, v, ka, o, lse, gain)


def _attn_fwd_old(q, k, v, fc, fr, gain, lp):
    n = q.shape[0]
    nb = n // lp
    nq = lp // TQ

    def body(q_ref, k_ref, v_ref, fc_ref, fr_ref, g_ref, z_ref, o_ref, lse_ref):
        i = pl.program_id(1)
        first = _head_masks()
        fcv = fc_ref[...]
        qs, fq = [], []
        for p in range(N_PAIRS):
            q2 = q_ref[:, 128 * p:128 * (p + 1)] * 0.125
            qs.append(_stack_heads(q2, first))
            fq.append(jnp.concatenate([fcv[:, 2 * p:2 * p + 1], fcv[:, 2 * p + 1:2 * p + 2]], axis=0))

        def step(j, carry):
            koff = pl.multiple_of(j * TQ, TQ)
            valid = _score_mask(i, j)
            new = []
            for p in range(N_PAIRS):
                m, l, acc = carry[p]
                k2 = k_ref[pl.ds(koff, TQ), 128 * p:128 * (p + 1)]
                v2 = v_ref[pl.ds(koff, TQ), 128 * p:128 * (p + 1)]
                fk = jnp.concatenate(
                    [jnp.broadcast_to(fr_ref[0, 2 * p:2 * p + 1, pl.ds(koff, TQ)], (TQ, TQ)),
                     jnp.broadcast_to(fr_ref[0, 2 * p + 1:2 * p + 2, pl.ds(koff, TQ)], (TQ, TQ))], axis=0)
                s = _dot_nt(qs[p], k2) + fq[p] - fk
                s = jnp.where(valid, s, NEG)
                m_new = jnp.maximum(m, jnp.max(s, axis=1, keepdims=True))
                pe = jnp.exp(s - m_new)
                alpha = jnp.exp(m - m_new)
                l = alpha * l + jnp.sum(pe, axis=1, keepdims=True)
                pb = pe.astype(BF16)
                pv = _dot(jnp.concatenate([pb[0:TQ], pb[TQ:]], axis=1), _stack_heads(v2, first))
                acc = acc * _pair_cols(alpha[0:TQ], alpha[TQ:], first) + pv
                new.append((m_new, l, acc))
            return tuple(new)

        init = tuple((jnp.full((2 * TQ, 1), NEG, F32), jnp.zeros((2 * TQ, 1), F32), jnp.zeros((TQ, 128), F32))
                     for _ in range(N_PAIRS))
        final = lax.fori_loop(0, i + 1, step, init)

        row = lax.broadcasted_iota(jnp.int32, (TQ, 128), 0)
        lane = lax.broadcasted_iota(jnp.int32, (TQ, 128), 1)
        real = (i * TQ + row) >= PAD
        lse_out = jnp.zeros((TQ, 128), F32)
        for p in range(N_PAIRS):
            m, l, acc = final[p]
            inv = 1.0 / l
            o = jnp.where(real, acc * _pair_cols(inv[0:TQ], inv[TQ:], first), 0.0)
            lse = m + jnp.log(l)
            lse_out = jnp.where(lane == 2 * p, lse[0:TQ], lse_out)
            lse_out = jnp.where(lane == 2 * p + 1, lse[TQ:], lse_out)
            sq = o * o
            ms0 = jnp.sum(jnp.where(first, sq, 0.0), axis=1, keepdims=True) * (1.0 / HEAD_DIM)
            ms1 = jnp.sum(jnp.where(first, 0.0, sq), axis=1, keepdims=True) * (1.0 / HEAD_DIM)
            r = _pair_cols(lax.rsqrt(ms0 + EPS), lax.rsqrt(ms1 + EPS), first)
            cols = slice(128 * p, 128 * (p + 1))
            o_ref[:, cols] = o.astype(BF16)
            z_ref[:, cols] = (o * r * g_ref[:, cols]).astype(BF16)
        lse_ref[...] = lse_out

    qblk = pl.BlockSpec((TQ, ATTN_DIM), lambda b, i: (b * nq + i, 0))
    seq = pl.BlockSpec((lp, ATTN_DIM), lambda b, i: (b, 0))
    colblk = pl.BlockSpec((TQ, 128), lambda b, i: (b * nq + i, 0))
    return pl.pallas_call(
        body, name="attn_fwd", grid=(nb, nq),
        in_specs=[qblk, seq, seq, colblk, pl.BlockSpec((1, N_HEADS, lp), lambda b, i: (b, 0, 0)),
                  pl.BlockSpec((1, ATTN_DIM), lambda b, i: (0, 0))],
        out_specs=[qblk, qblk, colblk],
        out_shape=[jax.ShapeDtypeStruct((n, ATTN_DIM), BF16), jax.ShapeDtypeStruct((n, ATTN_DIM), BF16),
                   jax.ShapeDtypeStruct((n, 128), F32)],
        compiler_params=_params(("parallel", "parallel")),
    )(q, k, v, fc, fr, gain)


def _attn_bwd_old(dz, q, k, v, fc, fr, o, lse, gain, lp):
    n = q.shape[0]
    nb = n // lp
    nq = lp // TQ

    def body(dz_ref, q_ref, k_ref, v_ref, fc_ref, fr_ref, o_ref, lse_ref, g_ref,
             dq_ref, dk_ref, dv_ref, dfc_ref, dfr_ref, dgain_ref):
        b = pl.program_id(0)
        i = pl.program_id(1)
        first = _head_masks()

        @pl.when((b == 0) & (i == 0))
        def _():
            dgain_ref[...] = jnp.zeros_like(dgain_ref)

        @pl.when(i == 0)
        def _():
            dk_ref[...] = jnp.zeros_like(dk_ref)
            dv_ref[...] = jnp.zeros_like(dv_ref)
            dfr_ref[...] = jnp.zeros_like(dfr_ref)

        fcv = fc_ref[...]
        lsev = lse_ref[...]
        qs, fq, lses, dos, deltas = [], [], [], [], []
        for p in range(N_PAIRS):
            cols = slice(128 * p, 128 * (p + 1))
            q2 = q_ref[:, cols] * 0.125
            qs.append(_stack_heads(q2, first))
            fq.append(jnp.concatenate([fcv[:, 2 * p:2 * p + 1], fcv[:, 2 * p + 1:2 * p + 2]], axis=0))
            lses.append(jnp.concatenate([lsev[:, 2 * p:2 * p + 1], lsev[:, 2 * p + 1:2 * p + 2]], axis=0))
            ov = o_ref[:, cols].astype(F32)
            dzv = dz_ref[:, cols].astype(F32)
            gv = g_ref[:, cols]
            sq = ov * ov
            ms0 = jnp.sum(jnp.where(first, sq, 0.0), axis=1, keepdims=True) * (1.0 / HEAD_DIM)
            ms1 = jnp.sum(jnp.where(first, 0.0, sq), axis=1, keepdims=True) * (1.0 / HEAD_DIM)
            r = _pair_cols(lax.rsqrt(ms0 + EPS), lax.rsqrt(ms1 + EPS), first)
            ohat = ov * r
            dyhat = dzv * gv
            dgain_ref[:, cols] += jnp.sum(dzv * ohat, axis=0, keepdims=True)
            pr = dyhat * ohat
            mean0 = jnp.sum(jnp.where(first, pr, 0.0), axis=1, keepdims=True) * (1.0 / HEAD_DIM)
            mean1 = jnp.sum(jnp.where(first, 0.0, pr), axis=1, keepdims=True) * (1.0 / HEAD_DIM)
            do = r * (dyhat - ohat * _pair_cols(mean0, mean1, first))
            dd = do * ov
            deltas.append(jnp.concatenate([jnp.sum(jnp.where(first, dd, 0.0), axis=1, keepdims=True),
                                           jnp.sum(jnp.where(first, 0.0, dd), axis=1, keepdims=True)], axis=0))
            dos.append(_stack_heads(do.astype(BF16), first))

        def step(j, carry):
            koff = pl.multiple_of(j * TQ, TQ)
            valid = _score_mask(i, j)
            new = []
            for p in range(N_PAIRS):
                dq_acc, dfq = carry[p]
                cols = slice(128 * p, 128 * (p + 1))
                k2 = k_ref[pl.ds(koff, TQ), cols]
                v2 = v_ref[pl.ds(koff, TQ), cols]
                fk = jnp.concatenate(
                    [jnp.broadcast_to(fr_ref[0, 2 * p:2 * p + 1, pl.ds(koff, TQ)], (TQ, TQ)),
                     jnp.broadcast_to(fr_ref[0, 2 * p + 1:2 * p + 2, pl.ds(koff, TQ)], (TQ, TQ))], axis=0)
                s = _dot_nt(qs[p], k2) + fq[p] - fk
                s = jnp.where(valid, s, NEG)
                pe = jnp.exp(s - lses[p])
                dp = _dot_nt(dos[p], v2)
                ds = pe * (dp - deltas[p])
                dfq = dfq + jnp.sum(ds, axis=1, keepdims=True)
                dfr_ref[0, 2 * p:2 * p + 1, pl.ds(koff, TQ)] -= jnp.sum(ds[0:TQ], axis=0, keepdims=True)
                dfr_ref[0, 2 * p + 1:2 * p + 2, pl.ds(koff, TQ)] -= jnp.sum(ds[TQ:], axis=0, keepdims=True)
                dsb = ds.astype(BF16)
                dq_acc = dq_acc + _dot(jnp.concatenate([dsb[0:TQ], dsb[TQ:]], axis=1), _stack_heads(k2, first))
                dk_ref[pl.ds(koff, TQ), cols] += _dot_tn(dsb, qs[p])
                dv_ref[pl.ds(koff, TQ), cols] += _dot_tn(pe.astype(BF16), dos[p])
                new.append((dq_acc, dfq))
            return tuple(new)

        init = tuple((jnp.zeros((TQ, 128), F32), jnp.zeros((2 * TQ, 1), F32)) for _ in range(N_PAIRS))
        final = lax.fori_loop(0, i + 1, step, init)

        lane = lax.broadcasted_iota(jnp.int32, (TQ, 128), 1)
        dfc_out = jnp.zeros((TQ, 128), F32)
        for p in range(N_PAIRS):
            dq_acc, dfq = final[p]
            dq_ref[:, 128 * p:128 * (p + 1)] = (dq_acc * 0.125).astype(BF16)
            dfc_out = jnp.where(lane == 2 * p, dfq[0:TQ], dfc_out)
            dfc_out = jnp.where(lane == 2 * p + 1, dfq[TQ:], dfc_out)
        dfc_ref[...] = dfc_out

    qblk = pl.BlockSpec((TQ, ATTN_DIM), lambda b, i: (b * nq + i, 0))
    seq = pl.BlockSpec((lp, ATTN_DIM), lambda b, i: (b, 0))
    colblk = pl.BlockSpec((TQ, 128), lambda b, i: (b * nq + i, 0))
    rowblk = pl.BlockSpec((1, N_HEADS, lp), lambda b, i: (b, 0, 0))
    gspec = pl.BlockSpec((1, ATTN_DIM), lambda b, i: (0, 0))
    return pl.pallas_call(
        body, name="attn_bwd", grid=(nb, nq),
        in_specs=[qblk, qblk, seq, seq, colblk, rowblk, qblk, colblk, gspec],
        out_specs=[qblk, seq, seq, colblk, rowblk, gspec],
        out_shape=[jax.ShapeDtypeStruct((n, ATTN_DIM), BF16), jax.ShapeDtypeStruct((n, ATTN_DIM), F32),
                   jax.ShapeDtypeStruct((n, ATTN_DIM), F32), jax.ShapeDtypeStruct((n, 128), F32),
                   jax.ShapeDtypeStruct((nb, N_HEADS, lp), F32), jax.ShapeDtypeStruct((1, ATTN_DIM), F32)],
        compiler_params=_params(("arbitrary", "arbitrary")),
    )(dz, q, k, v, fc, fr, o, lse, gain)


def _loss_head(h, gain, target, lp):
    n = h.shape[0]
    nb = n // lp
    nq = lp // 128

    def body(h_ref, g_ref, t_ref, loss_ref, dh_ref, dgain_ref):
        b = pl.program_id(0)
        i = pl.program_id(1)

        @pl.when((b == 0) & (i == 0))
        def _():
            loss_ref[...] = jnp.zeros_like(loss_ref)
            dgain_ref[...] = jnp.zeros_like(dgain_ref)

        @pl.when(i == 0)
        def _():
            dh_ref[...] = jnp.zeros_like(dh_ref)

        @pl.when(i > 0)
        def _():
            gain_v = g_ref[...]
            y, xhat, r = _rms(h_ref[...], gain_v)
            err = y - t_ref[...]
            loss_ref[...] += 0.5 * jnp.sum(jnp.sum(err * err, axis=1, keepdims=True), axis=0,
                                           keepdims=True) * (1.0 / D_MODEL)
            dy = err * (1.0 / D_MODEL)
            dh_ref[...] = _rms_bwd(dy, xhat, r, gain_v)
            dgain_ref[...] += jnp.sum(dy * xhat, axis=0, keepdims=True)

    rows = pl.BlockSpec((128, D_MODEL), lambda b, i: (b * nq + i, 0))
    trows = pl.BlockSpec((128, D_MODEL), lambda b, i: (b * (nq - 1) + jnp.maximum(i, 1) - 1, 0))
    return pl.pallas_call(
        body, name="loss_head", grid=(nb, nq),
        in_specs=[rows, pl.BlockSpec((1, D_MODEL), lambda b, i: (0, 0)), trows],
        out_specs=[pl.BlockSpec((1, 1), lambda b, i: (0, 0)), rows, pl.BlockSpec((1, D_MODEL), lambda b, i: (0, 0))],
        out_shape=[jax.ShapeDtypeStruct((1, 1), F32), jax.ShapeDtypeStruct((n, D_MODEL), F32),
                   jax.ShapeDtypeStruct((1, D_MODEL), F32)],
        compiler_params=_params(("arbitrary", "arbitrary")),
    )(h, gain, target)


def _place():
    return lax.axis_index("x"), lax.axis_index("y"), lax.axis_index("c")


def _all_gather(xs, name):
    nw = len(xs)

    def body(*refs):
        ins, outs = refs[:nw], refs[nw:2 * nw]
        send_sems, recv_sems, local_sems = refs[2 * nw:]
        x, y, c = _place()
        me, sibling = (x, y, c), (x, y, 1 - c)
        chips = [(1 - x, y), (x, 1 - y), (1 - x, 1 - y)]

        def copy(w, k, block, to, src=None):
            slot = outs[w].at[4 * block[0] + 2 * block[1] + block[2]]
            return pltpu.make_async_remote_copy(
                src_ref=slot if src is None else src, dst_ref=slot,
                send_sem=send_sems.at[w, k], recv_sem=recv_sems.at[w, k], device_id=to, device_id_type=MESH)

        started = []
        for w in range(nw):
            mine = pltpu.make_async_copy(ins[w], outs[w].at[4 * x + 2 * y + c], local_sems.at[w])
            mine.start()
            started.append(mine)
        sends = []
        for w in range(nw):
            sends.append(copy(w, 0, me, sibling, src=ins[w]))
            sends += [copy(w, 1 + j, me, (*chip, c), src=ins[w]) for j, chip in enumerate(chips)]
        for cp in sends:
            cp.start()
        for w in range(nw):
            for j, chip in enumerate(chips):
                copy(w, 1 + j, (*chip, c), me).wait_recv()
                passed = copy(w, 4 + j, (*chip, c), sibling)
                passed.start()
                sends.append(passed)
        for w in range(nw):
            copy(w, 0, sibling, me).wait_recv()
            for j, chip in enumerate(chips):
                copy(w, 4 + j, (*chip, 1 - c), me).wait_recv()
        for cp in sends:
            cp.wait_send()
        for mine in started:
            mine.wait()

    return pl.pallas_call(
        body, name=name,
        in_specs=[ANY] * nw, out_specs=[ANY] * nw,
        out_shape=[jax.ShapeDtypeStruct((N_DEV,) + a.shape, a.dtype) for a in xs],
        scratch_shapes=[pltpu.SemaphoreType.DMA((nw, 7)), pltpu.SemaphoreType.DMA((nw, 7)),
                        pltpu.SemaphoreType.DMA((nw,))],
    )(*xs)


def _exchange_partials(xs, name):
    nw = len(xs)

    def body(*refs):
        ins, outs = refs[:nw], refs[nw:2 * nw]
        send_sems, recv_sems, local_sems = refs[2 * nw:]
        x, y, c = _place()
        me = 4 * x + 2 * y + c

        def flip(v, bit):
            return 1 - v if bit else v

        def copy(w, k):
            bx, by, bc = ((k + 1) >> 2) & 1, ((k + 1) >> 1) & 1, (k + 1) & 1
            peer = (flip(x, bx), flip(y, by), flip(c, bc))
            peer_idx = 4 * peer[0] + 2 * peer[1] + peer[2]
            return pltpu.make_async_remote_copy(
                src_ref=ins[w].at[peer_idx], dst_ref=outs[w].at[me],
                send_sem=send_sems.at[w, k], recv_sem=recv_sems.at[w, k], device_id=peer, device_id_type=MESH)

        local = [pltpu.make_async_copy(ins[w].at[me], outs[w].at[me], local_sems.at[w]) for w in range(nw)]
        for cp in local:
            cp.start()
        remote = [copy(w, k) for w in range(nw) for k in range(N_DEV - 1)]
        for cp in remote:
            cp.start()
        for cp in remote:
            cp.wait()
        for cp in local:
            cp.wait()

    return pl.pallas_call(
        body, name=name,
        in_specs=[ANY] * nw, out_specs=[ANY] * nw,
        out_shape=[jax.ShapeDtypeStruct(a.shape, a.dtype) for a in xs],
        scratch_shapes=[pltpu.SemaphoreType.DMA((nw, 7)), pltpu.SemaphoreType.DMA((nw, 7)),
                        pltpu.SemaphoreType.DMA((nw,))],
    )(*xs)


def _adamw(parts, w, m, v, name):
    s_parts, r, c = parts.shape
    tr = r
    for t in (256, 128, 64, 32, 16):
        if r % t == 0 and r > t:
            tr = t
            break

    def body(p_ref, w_ref, m_ref, v_ref, g_ref, d_ref, nm_ref, nv_ref):
        g = p_ref[0].astype(F32)
        for s in range(1, s_parts):
            g = g + p_ref[s].astype(F32)
        nm = ADAM_B1 * m_ref[...] + (1.0 - ADAM_B1) * g
        nv = ADAM_B2 * v_ref[...] + (1.0 - ADAM_B2) * (g * g)
        m_hat = nm / (1.0 - ADAM_B1 ** ADAM_STEP)
        v_hat = nv / (1.0 - ADAM_B2 ** ADAM_STEP)
        g_ref[...] = g
        d_ref[...] = -ADAM_LR * (m_hat / (jnp.sqrt(v_hat) + ADAM_EPS) + ADAM_WD * w_ref[...])
        nm_ref[...] = nm
        nv_ref[...] = nv

    blk = pl.BlockSpec((tr, c), lambda i: (i, 0))
    return pl.pallas_call(
        body, name=name, grid=(r // tr,),
        in_specs=[pl.BlockSpec((s_parts, tr, c), lambda i: (0, i, 0)), blk, blk, blk],
        out_specs=[blk] * 4,
        out_shape=[jax.ShapeDtypeStruct((r, c), F32)] * 4,
        compiler_params=_params(("parallel",)),
    )(parts, w, m, v)


def _sum_parts(parts, name):
    s_parts, r, c = parts.shape

    def body(p_ref, out_ref):
        acc = p_ref[0]
        for s in range(1, s_parts):
            acc = acc + p_ref[s]
        out_ref[...] = acc

    return pl.pallas_call(
        body, name=name, out_shape=jax.ShapeDtypeStruct((r, c), F32),
        in_specs=[pl.BlockSpec(memory_space=pltpu.VMEM)], out_specs=pl.BlockSpec(memory_space=pltpu.VMEM),
    )(parts)


SMALL_ROWS = 184


def _pack_small(d_gains, d_gc, d_ga, d_bf, d_conv, d_meta):
    rows = [g.reshape(8, 128) for g in d_gains]
    rows += [d_gc.reshape(4, 128), d_ga.reshape(4, 128), d_bf.reshape(1, 128)]
    rows += [d_conv.reshape(12, 128), d_meta.reshape(128, 128)]
    packed = jnp.concatenate(rows, axis=0)
    return jnp.pad(packed, ((0, SMALL_ROWS - packed.shape[0]), (0, 0)))


def kernel(x, meta_tokens, ffn1_norm, ffn1_w_gu, ffn1_w_down, mix_norm, w_in, conv_w, b_f, out_norm_conv, out_norm_attn, w_out, ffn2_norm, ffn2_w_gu, ffn2_w_down, final_norm, loss_target, m_meta_tokens, m_ffn1_norm, m_ffn1_w_gu, m_ffn1_w_down, m_mix_norm, m_w_in, m_conv_w, m_b_f, m_out_norm_conv, m_out_norm_attn, m_w_out, m_ffn2_norm, m_ffn2_w_gu, m_ffn2_w_down, m_final_norm, v_meta_tokens, v_ffn1_norm, v_ffn1_w_gu, v_ffn1_w_down, v_mix_norm, v_w_in, v_conv_w, v_b_f, v_out_norm_conv, v_out_norm_attn, v_w_out, v_ffn2_norm, v_ffn2_w_gu, v_ffn2_w_down, v_final_norm):
    nb, seq, _ = x.shape
    lp = PAD + N_META + seq
    n = nb * lp
    me = 4 * lax.axis_index("x") + 2 * lax.axis_index("y") + lax.axis_index("c")

    big = [ffn1_w_gu[0], ffn1_w_down[0], w_in[0], w_out[0], ffn2_w_gu[0], ffn2_w_down[0]]
    wgu1_8, wd1_8, win_8, wout_8, wgu2_8, wd2_8 = _all_gather([a.astype(BF16) for a in big], "gather_weights")
    small_in = jnp.concatenate(
        [meta_tokens, jnp.pad(conv_w[0], ((0, 0), (0, 128 - conv_w.shape[2]))), jnp.zeros((5, 128), F32)], axis=0)
    (small_8,) = _all_gather([small_in], "gather_small")
    meta_full = small_8[:, 0:N_META, :].transpose(1, 0, 2).reshape(N_META, D_MODEL)
    conv_full = small_8[:, N_META:N_META + 3, 0:CONV_DIM // N_DEV].transpose(1, 0, 2).reshape(3, CONV_DIM)

    wgu1 = wgu1_8.reshape(2, N_CHUNK, D_MODEL, F_CHUNK)
    wgu2 = wgu2_8.reshape(2, N_CHUNK, D_MODEL, F_CHUNK)
    wd1 = wd1_8.reshape(N_CHUNK, F_CHUNK, D_MODEL)
    wd2 = wd2_8.reshape(N_CHUNK, F_CHUNK, D_MODEL)
    w_in_full = jnp.pad(win_8.transpose(1, 0, 2).reshape(D_MODEL, IN_DIM), ((0, 0), (0, IN_PAD - IN_DIM)))
    w_out_full = wout_8.reshape(D_MODEL, D_MODEL)
    b_f_row = jnp.pad(b_f, ((0, 0), (0, 128 - N_HEADS)))
    gmat = _group_matrix()

    h0 = jnp.concatenate([jnp.zeros((nb, PAD, D_MODEL), F32),
                          jnp.broadcast_to(meta_full[None], (nb, N_META, D_MODEL)), x], axis=1).reshape(n, D_MODEL)
    h1, gate1, up1 = _ffn_fwd(h0, ffn1_norm, wgu1, wd1, "ffn1_fwd")
    bg, cg, hc, q, k, v, fg = _inproj_fwd(h1, mix_norm, w_in_full)
    zc = _conv_fwd(bg, cg, hc, conv_full, out_norm_conv, gmat, lp)
    ka, qa = _fgate_fwd(fg, b_f_row, lp)
    za, o, lse = _attn_fwd(q, qa, k, v, ka, out_norm_attn, lp)
    h2 = _outproj_fwd(zc, za, w_out_full, h1)
    h3, gate2, up2 = _ffn_fwd(h2, ffn2_norm, wgu2, wd2, "ffn2_fwd")
    loss_part, dh3, d_final = _loss_head(h3, final_norm.reshape(1, D_MODEL), loss_target.reshape(nb * seq, D_MODEL), lp)

    dh2, dgate2, dup2, d_ffn2 = _ffn_bwd_x(dh3, h2, ffn2_norm, gate2, up2, wgu2, wd2, "ffn2_bwd_x")
    dwgu2, dwd2 = _ffn_bwd_w(dh3, h2, ffn2_norm, gate2, up2, dgate2, dup2, "ffn2_bwd_w")
    dzc, dza, dwout = _outproj_bwd(dh2, zc, za, w_out_full)
    dq, dk, dv, dka, dfr, d_ga = _attn_bwd(dza, q, qa, k, v, ka, o, lse, out_norm_attn, lp)
    dfg, d_bf = _fgate_bwd(dka, dfr, fg, b_f_row, lp)
    dbg, dcg, dhc, d_conv, d_gc = _conv_bwd(dzc, bg, cg, hc, conv_full, out_norm_conv, gmat, lp)
    dh1, dwin, d_mix = _inproj_bwd([dbg, dcg, dhc, dq, dk, dv], dfg, dh2, h1, mix_norm, w_in_full)
    dh0, dgate1, dup1, d_ffn1 = _ffn_bwd_x(dh1, h0, ffn1_norm, gate1, up1, wgu1, wd1, "ffn1_bwd_x")
    dwgu1, dwd1 = _ffn_bwd_w(dh1, h0, ffn1_norm, gate1, up1, dgate1, dup1, "ffn1_bwd_w")

    dh0 = dh0.reshape(nb, lp, D_MODEL)
    grad_x = dh0[:, PAD + N_META:, :]
    d_meta = jnp.sum(dh0[:, PAD:PAD + N_META, :], axis=0)

    dwin_8 = dwin[:, 0:IN_DIM].reshape(D_MODEL, N_DEV, IN_DIM // N_DEV).transpose(1, 0, 2)
    send = [dwgu1.reshape(N_DEV, D_MODEL, F_CHUNK), dwd1.reshape(N_DEV, F_CHUNK // 2, D_MODEL), dwin_8,
            dwout.reshape(N_DEV, D_MODEL // N_DEV, D_MODEL),
            dwgu2.reshape(N_DEV, D_MODEL, F_CHUNK), dwd2.reshape(N_DEV, F_CHUNK // 2, D_MODEL)]
    parts = _exchange_partials(send, "exchange_grads")

    small = _pack_small([d_ffn1, d_mix, d_ffn2, d_final], d_gc, d_ga, d_bf, d_conv, d_meta)
    (small_all,) = _all_gather([small], "gather_small_grads")
    small_sum = _sum_parts(small_all, "sum_small_grads")
    g_ffn1n, g_mixn, g_ffn2n, g_finaln = (small_sum[8 * t:8 * t + 8].reshape(1, D_MODEL) for t in range(4))
    g_gc = small_sum[32:36].reshape(1, CONV_DIM)
    g_ga = small_sum[36:40].reshape(1, ATTN_DIM)
    g_bf = small_sum[40:41, 0:N_HEADS]
    g_conv_full = small_sum[41:53].reshape(3, CONV_DIM)
    g_meta_full = small_sum[53:181].reshape(N_META, D_MODEL)
    g_conv = lax.dynamic_slice_in_dim(g_conv_full, me * (CONV_DIM // N_DEV), CONV_DIM // N_DEV, axis=1)
    g_meta = lax.dynamic_slice_in_dim(g_meta_full, me * (D_MODEL // N_DEV), D_MODEL // N_DEV, axis=1)

    weights = {
        "meta_tokens": (g_meta[None], meta_tokens, m_meta_tokens, v_meta_tokens),
        "ffn1_norm": (g_ffn1n[None], ffn1_norm, m_ffn1_norm, v_ffn1_norm),
        "ffn1_w_gu": (parts[0], ffn1_w_gu[0], m_ffn1_w_gu[0], v_ffn1_w_gu[0]),
        "ffn1_w_down": (parts[1], ffn1_w_down[0], m_ffn1_w_down[0], v_ffn1_w_down[0]),
        "mix_norm": (g_mixn[None], mix_norm, m_mix_norm, v_mix_norm),
        "w_in": (parts[2], w_in[0], m_w_in[0], v_w_in[0]),
        "conv_w": (g_conv[None], conv_w[0], m_conv_w[0], v_conv_w[0]),
        "b_f": (g_bf[None], b_f, m_b_f, v_b_f),
        "out_norm_conv": (g_gc[None], out_norm_conv, m_out_norm_conv, v_out_norm_conv),
        "out_norm_attn": (g_ga[None], out_norm_attn, m_out_norm_attn, v_out_norm_attn),
        "w_out": (parts[3], w_out[0], m_w_out[0], v_w_out[0]),
        "ffn2_norm": (g_ffn2n[None], ffn2_norm, m_ffn2_norm, v_ffn2_norm),
        "ffn2_w_gu": (parts[4], ffn2_w_gu[0], m_ffn2_w_gu[0], v_ffn2_w_gu[0]),
        "ffn2_w_down": (parts[5], ffn2_w_down[0], m_ffn2_w_down[0], v_ffn2_w_down[0]),
        "final_norm": (g_finaln[None], final_norm.reshape(1, D_MODEL), m_final_norm.reshape(1, D_MODEL),
                       v_final_norm.reshape(1, D_MODEL)),
    }
    shapes = {"meta_tokens": meta_tokens.shape, "ffn1_norm": ffn1_norm.shape, "ffn1_w_gu": ffn1_w_gu.shape,
              "ffn1_w_down": ffn1_w_down.shape, "mix_norm": mix_norm.shape, "w_in": w_in.shape,
              "conv_w": conv_w.shape, "b_f": b_f.shape, "out_norm_conv": out_norm_conv.shape,
              "out_norm_attn": out_norm_attn.shape, "w_out": w_out.shape, "ffn2_norm": ffn2_norm.shape,
              "ffn2_w_gu": ffn2_w_gu.shape, "ffn2_w_down": ffn2_w_down.shape, "final_norm": final_norm.shape}
    grads, deltas, new_m, new_v = [], [], [], []
    for name, (p, w, m, vv) in weights.items():
        g, d, nm, nv = _adamw(p, w, m, vv, "adamw_" + name)
        shape = shapes[name]
        grads.append(g.reshape(shape))
        deltas.append(d.reshape(shape))
        new_m.append(nm.reshape(shape))
        new_v.append(nv.reshape(shape))

    loss = lax.psum(loss_part[0, 0], ("x", "y", "c"))
    return (loss, grad_x, *grads, *deltas, *new_m, *new_v)
```

```python
import jax
import jax.numpy as jnp
from jax import lax
from jax.experimental import pallas as pl
from jax.experimental.pallas import tpu as pltpu

F32 = jnp.float32
BF16 = jnp.bfloat16

N_DEV = 8
D_MODEL = 1024
N_META = 16
PAD = 128 - N_META
CONV_DIM = 512
ATTN_DIM = 512
HEAD_DIM = 64
N_HEADS = 8
N_PAIRS = N_HEADS // 2
D_FF = 2816
N_CHUNK = 4
F_CHUNK = D_FF // N_CHUNK
IN_DIM = 3080
IN_PAD = 3200
IN_MAIN = 3072
EPS = 1e-6
NEG = -1e30
TQ = 128
TK = 256
VMEM_LIMIT = 56 * 1024 * 1024

ADAM_LR = 0.001
ADAM_B1 = 0.9
ADAM_B2 = 0.999
ADAM_EPS = 1e-08
ADAM_WD = 0.01
ADAM_STEP = 10

MESH = pl.DeviceIdType.MESH
ANY = pl.BlockSpec(memory_space=pl.ANY)


def _params(sem=None):
    return pltpu.CompilerParams(dimension_semantics=sem, vmem_limit_bytes=VMEM_LIMIT)


def _row_tile(n, prefer):
    for t in (prefer, 512, 256, 128):
        if t <= n and n % t == 0:
            return t
    raise ValueError(f"no row tile for {n}")


def _dot(a, b):
    return jnp.dot(a, b, preferred_element_type=F32)


def _dot_nt(a, b):
    return lax.dot_general(a, b, (((1,), (1,)), ((), ())), preferred_element_type=F32)


def _dot_tn(a, b):
    return lax.dot_general(a, b, (((0,), (0,)), ((), ())), preferred_element_type=F32)


def _rms(x, g):
    r = lax.rsqrt(jnp.mean(x * x, axis=-1, keepdims=True) + EPS)
    xhat = x * r
    return xhat * g, xhat, r


def _rms_bwd(dn, xhat, r, g):
    dxhat = dn * g
    return r * (dxhat - xhat * jnp.mean(dxhat * xhat, axis=-1, keepdims=True))


def _sigmoid(x):
    return 1.0 / (1.0 + jnp.exp(-x))


def _place():
    return lax.axis_index("x"), lax.axis_index("y"), lax.axis_index("c")


def _comm_sems(nw):
    return [pltpu.SemaphoreType.DMA((nw, 7)), pltpu.SemaphoreType.DMA((nw, 7)), pltpu.SemaphoreType.DMA((nw,))]


class _Gather:
    def __init__(self, ins, outs, sems):
        self.ins, self.outs = ins, outs
        self.send, self.recv, self.local = sems
        x, y, c = _place()
        self.c = c
        self.me, self.sibling = (x, y, c), (x, y, 1 - c)
        self.chips = [(1 - x, y), (x, 1 - y), (1 - x, 1 - y)]

    def _copy(self, w, k, block, to, own=False):
        slot = self.outs[w].at[4 * block[0] + 2 * block[1] + block[2]]
        return pltpu.make_async_remote_copy(
            src_ref=self.ins[w] if own else slot, dst_ref=slot,
            send_sem=self.send.at[w, k], recv_sem=self.recv.at[w, k], device_id=to, device_id_type=MESH)

    def _mine(self, w):
        x, y, c = self.me
        return pltpu.make_async_copy(self.ins[w], self.outs[w].at[4 * x + 2 * y + c], self.local.at[w])

    def _first(self, w):
        return ([self._copy(w, 0, self.me, self.sibling, own=True)]
                + [self._copy(w, 1 + j, self.me, (*chip, self.c), own=True) for j, chip in enumerate(self.chips)])

    def _passed(self, w):
        return [self._copy(w, 4 + j, (*chip, self.c), self.sibling) for j, chip in enumerate(self.chips)]

    def start(self):
        for w in range(len(self.ins)):
            self._mine(w).start()
        for w in range(len(self.ins)):
            for cp in self._first(w):
                cp.start()

    def forward(self):
        for w in range(len(self.ins)):
            for j, chip in enumerate(self.chips):
                self._copy(w, 1 + j, (*chip, self.c), self.me).wait_recv()
                self._passed(w)[j].start()

    def finish(self):
        for w in range(len(self.ins)):
            self._copy(w, 0, self.sibling, self.me).wait_recv()
            for j, chip in enumerate(self.chips):
                self._copy(w, 4 + j, (*chip, 1 - self.c), self.me).wait_recv()
        for w in range(len(self.ins)):
            for cp in self._first(w) + self._passed(w):
                cp.wait_send()
            self._mine(w).wait()


class _Exchange:
    def __init__(self, ins, outs, sems):
        self.ins, self.outs = ins, outs
        self.send, self.recv, self.local = sems
        self.x, self.y, self.c = _place()
        self.me = 4 * self.x + 2 * self.y + self.c

    def _copy(self, w, k):
        flip = lambda v, bit: 1 - v if bit else v
        peer = (flip(self.x, ((k + 1) >> 2) & 1), flip(self.y, ((k + 1) >> 1) & 1), flip(self.c, (k + 1) & 1))
        return pltpu.make_async_remote_copy(
            src_ref=self.ins[w].at[4 * peer[0] + 2 * peer[1] + peer[2]], dst_ref=self.outs[w].at[self.me],
            send_sem=self.send.at[w, k], recv_sem=self.recv.at[w, k], device_id=peer, device_id_type=MESH)

    def _mine(self, w):
        return pltpu.make_async_copy(self.ins[w].at[self.me], self.outs[w].at[self.me], self.local.at[w])

    def start(self):
        for w in range(len(self.ins)):
            self._mine(w).start()
            for k in range(N_DEV - 1):
                self._copy(w, k).start()

    def finish(self):
        for w in range(len(self.ins)):
            for k in range(N_DEV - 1):
                self._copy(w, k).wait()
            self._mine(w).wait()


def _split_refs(refs, n_in, n_comm, n_out, n_scr):
    a = n_in
    b = a + n_comm
    c = b + n_out
    d = c + n_comm
    e = d + n_scr
    return refs[:a], refs[a:b], refs[b:c], refs[c:d], refs[d:e], refs[e:]


def _all_gather(xs, name):
    nw = len(xs)

    def body(*refs):
        comm = _Gather(refs[:nw], refs[nw:2 * nw], refs[2 * nw:])
        comm.start()
        comm.forward()
        comm.finish()

    return pl.pallas_call(
        body, name=name, in_specs=[ANY] * nw, out_specs=[ANY] * nw,
        out_shape=[jax.ShapeDtypeStruct((N_DEV,) + a.shape, a.dtype) for a in xs],
        scratch_shapes=_comm_sems(nw),
    )(*xs)


def _ffn_fwd(h, gain, wgu, wd, name, gather=()):
    n = h.shape[0]
    tm = _row_tile(n, 512)
    n_i = n // tm
    nw = len(gather)

    def body(*refs):
        (h_ref, g_ref, wgu_ref, wd_ref), gin, (out_ref, gate_ref, up_ref), gout, (n_scr, acc_scr), sems = \
            _split_refs(refs, 4, nw, 3, 2)
        i = pl.program_id(0)
        j = pl.program_id(1)
        if nw:
            comm = _Gather(gin, gout, sems)
            pl.when((i == 0) & (j == 0))(comm.start)
            pl.when((i == (3 * n_i) // 4) & (j == 0))(comm.forward)

        @pl.when(j == 0)
        def _():
            y, _, _ = _rms(h_ref[...], g_ref[...])
            n_scr[...] = y.astype(BF16)
            acc_scr[...] = jnp.zeros_like(acc_scr)

        nb = n_scr[...]
        gate = _dot(nb, wgu_ref[0, 0])
        up = _dot(nb, wgu_ref[1, 0])
        gate_ref[0] = gate.astype(BF16)
        up_ref[0] = up.astype(BF16)
        act = (gate * _sigmoid(gate) * up).astype(BF16)
        acc_scr[...] += _dot(act, wd_ref[0])

        @pl.when(j == N_CHUNK - 1)
        def _():
            out_ref[...] = h_ref[...] + 0.5 * acc_scr[...]

        if nw:
            pl.when((i == n_i - 1) & (j == N_CHUNK - 1))(comm.finish)

    return pl.pallas_call(
        body, name=name, grid=(n_i, N_CHUNK),
        in_specs=[pl.BlockSpec((tm, D_MODEL), lambda i, j: (i, 0)),
                  pl.BlockSpec((1, D_MODEL), lambda i, j: (0, 0)),
                  pl.BlockSpec((2, 1, D_MODEL, F_CHUNK), lambda i, j: (0, j, 0, 0)),
                  pl.BlockSpec((1, F_CHUNK, D_MODEL), lambda i, j: (j, 0, 0))] + [ANY] * nw,
        out_specs=[pl.BlockSpec((tm, D_MODEL), lambda i, j: (i, 0)),
                   pl.BlockSpec((1, tm, F_CHUNK), lambda i, j: (j, i, 0)),
                   pl.BlockSpec((1, tm, F_CHUNK), lambda i, j: (j, i, 0))] + [ANY] * nw,
        out_shape=[jax.ShapeDtypeStruct((n, D_MODEL), F32),
                   jax.ShapeDtypeStruct((N_CHUNK, n, F_CHUNK), BF16),
                   jax.ShapeDtypeStruct((N_CHUNK, n, F_CHUNK), BF16)]
        + [jax.ShapeDtypeStruct((N_DEV,) + a.shape, a.dtype) for a in gather],
        scratch_shapes=[pltpu.VMEM((tm, D_MODEL), BF16), pltpu.VMEM((tm, D_MODEL), F32)]
        + (_comm_sems(nw) if nw else []),
        compiler_params=_params(("arbitrary", "arbitrary")),
    )(h, gain, wgu, wd, *gather)


def _ffn_bwd_x(dh_out, h_in, gain, gate, up, wgu, wd, name):
    n = h_in.shape[0]
    tm = _row_tile(n, 512)

    def body(dh_ref, h_ref, g_ref, gate_ref, up_ref, wgu_ref, wd_ref,
             dhin_ref, dgate_ref, dup_ref, dgain_ref, dhb_scr, acc_scr):
        i = pl.program_id(0)
        j = pl.program_id(1)

        @pl.when((i == 0) & (j == 0))
        def _():
            dgain_ref[...] = jnp.zeros_like(dgain_ref)

        @pl.when(j == 0)
        def _():
            dhb_scr[...] = (0.5 * dh_ref[...]).astype(BF16)
            acc_scr[...] = jnp.zeros_like(acc_scr)

        da = _dot_nt(dhb_scr[...], wd_ref[0])
        g = gate_ref[0].astype(F32)
        u = up_ref[0].astype(F32)
        sig = _sigmoid(g)
        dgate = (da * u * (sig * (1.0 + g * (1.0 - sig)))).astype(BF16)
        dup = (da * (g * sig)).astype(BF16)
        dgate_ref[0] = dgate
        dup_ref[0] = dup
        acc_scr[...] += _dot_nt(dgate, wgu_ref[0, 0]) + _dot_nt(dup, wgu_ref[1, 0])

        @pl.when(j == N_CHUNK - 1)
        def _():
            gain_v = g_ref[...]
            _, xhat, r = _rms(h_ref[...], gain_v)
            dn = acc_scr[...]
            dhin_ref[...] = dh_ref[...] + _rms_bwd(dn, xhat, r, gain_v)
            dgain_ref[...] += jnp.sum(dn * xhat, axis=0, keepdims=True)

    chunk = pl.BlockSpec((1, tm, F_CHUNK), lambda i, j: (j, i, 0))
    rows = pl.BlockSpec((tm, D_MODEL), lambda i, j: (i, 0))
    vec = pl.BlockSpec((1, D_MODEL), lambda i, j: (0, 0))
    return pl.pallas_call(
        body, name=name, grid=(n // tm, N_CHUNK),
        in_specs=[rows, rows, vec, chunk, chunk,
                  pl.BlockSpec((2, 1, D_MODEL, F_CHUNK), lambda i, j: (0, j, 0, 0)),
                  pl.BlockSpec((1, F_CHUNK, D_MODEL), lambda i, j: (j, 0, 0))],
        out_specs=[rows, chunk, chunk, vec],
        out_shape=[jax.ShapeDtypeStruct((n, D_MODEL), F32),
                   jax.ShapeDtypeStruct((N_CHUNK, n, F_CHUNK), BF16),
                   jax.ShapeDtypeStruct((N_CHUNK, n, F_CHUNK), BF16),
                   jax.ShapeDtypeStruct((1, D_MODEL), F32)],
        scratch_shapes=[pltpu.VMEM((tm, D_MODEL), BF16), pltpu.VMEM((tm, D_MODEL), F32)],
        compiler_params=_params(("arbitrary", "arbitrary")),
    )(dh_out, h_in, gain, gate, up, wgu, wd)


def _ffn_bwd_act(dh_out, gate, up, wd, name):
    n = dh_out.shape[0]
    tm = _row_tile(n, 512)

    def body(dh_ref, gate_ref, up_ref, wd_ref, dgate_ref, dup_ref, dhb_scr):
        @pl.when(pl.program_id(1) == 0)
        def _():
            dhb_scr[...] = (0.5 * dh_ref[...]).astype(BF16)

        da = _dot_nt(dhb_scr[...], wd_ref[0])
        g = gate_ref[0].astype(F32)
        u = up_ref[0].astype(F32)
        sig = _sigmoid(g)
        dgate_ref[0] = (da * u * (sig * (1.0 + g * (1.0 - sig)))).astype(BF16)
        dup_ref[0] = (da * (g * sig)).astype(BF16)

    chunk = pl.BlockSpec((1, tm, F_CHUNK), lambda i, j: (j, i, 0))
    return pl.pallas_call(
        body, name=name, grid=(n // tm, N_CHUNK),
        in_specs=[pl.BlockSpec((tm, D_MODEL), lambda i, j: (i, 0)), chunk, chunk,
                  pl.BlockSpec((1, F_CHUNK, D_MODEL), lambda i, j: (j, 0, 0))],
        out_specs=[chunk, chunk],
        out_shape=[jax.ShapeDtypeStruct((N_CHUNK, n, F_CHUNK), BF16)] * 2,
        scratch_shapes=[pltpu.VMEM((tm, D_MODEL), BF16)],
        compiler_params=_params(("parallel", "arbitrary")),
    )(dh_out, gate, up, wd)


def _ffn_bwd_in(dh_out, h_in, gain, dgate, dup, wgu, name, exchange=()):
    n = h_in.shape[0]
    tm = _row_tile(n, 512)
    n_i = n // tm
    nw = len(exchange)

    def body(*refs):
        (dh_ref, h_ref, g_ref, dgate_ref, dup_ref, wgu_ref), xin, (dhin_ref, dgain_ref), xout, (acc_scr,), sems = \
            _split_refs(refs, 6, nw, 2, 1)
        i = pl.program_id(0)
        j = pl.program_id(1)
        if nw:
            comm = _Exchange(xin, xout, sems)
            pl.when((i == 0) & (j == 0))(comm.start)

        @pl.when((i == 0) & (j == 0))
        def _():
            dgain_ref[...] = jnp.zeros_like(dgain_ref)

        @pl.when(j == 0)
        def _():
            acc_scr[...] = jnp.zeros_like(acc_scr)

        acc_scr[...] += _dot_nt(dgate_ref[0], wgu_ref[0, 0]) + _dot_nt(dup_ref[0], wgu_ref[1, 0])

        @pl.when(j == N_CHUNK - 1)
        def _():
            gain_v = g_ref[...]
            _, xhat, r = _rms(h_ref[...], gain_v)
            dn = acc_scr[...]
            dhin_ref[...] = dh_ref[...] + _rms_bwd(dn, xhat, r, gain_v)
            dgain_ref[...] += jnp.sum(dn * xhat, axis=0, keepdims=True)

        if nw:
            pl.when((i == n_i - 1) & (j == N_CHUNK - 1))(comm.finish)

    chunk = pl.BlockSpec((1, tm, F_CHUNK), lambda i, j: (j, i, 0))
    rows = pl.BlockSpec((tm, D_MODEL), lambda i, j: (i, 0))
    vec = pl.BlockSpec((1, D_MODEL), lambda i, j: (0, 0))
    return pl.pallas_call(
        body, name=name, grid=(n_i, N_CHUNK),
        in_specs=[rows, rows, vec, chunk, chunk,
                  pl.BlockSpec((2, 1, D_MODEL, F_CHUNK), lambda i, j: (0, j, 0, 0))] + [ANY] * nw,
        out_specs=[rows, vec] + [ANY] * nw,
        out_shape=[jax.ShapeDtypeStruct((n, D_MODEL), F32), jax.ShapeDtypeStruct((1, D_MODEL), F32)]
        + [jax.ShapeDtypeStruct(a.shape, a.dtype) for a in exchange],
        scratch_shapes=[pltpu.VMEM((tm, D_MODEL), F32)] + (_comm_sems(nw) if nw else []),
        compiler_params=_params(("arbitrary", "arbitrary")),
    )(dh_out, h_in, gain, dgate, dup, wgu, *exchange)


def _ffn_bwd_w(dh_out, h_in, gain, gate, up, dgate, dup, name):
    n = h_in.shape[0]
    tm = _row_tile(n, 512)
    n_i = n // tm

    def body(dh_ref, h_ref, g_ref, gate_ref, up_ref, dgate_ref, dup_ref, dwgu_ref, dwd_ref,
             ag_scr, au_scr, ad_scr):
        i = pl.program_id(1)

        @pl.when(i == 0)
        def _():
            ag_scr[...] = jnp.zeros_like(ag_scr)
            au_scr[...] = jnp.zeros_like(au_scr)
            ad_scr[...] = jnp.zeros_like(ad_scr)

        y, _, _ = _rms(h_ref[...], g_ref[...])
        nb = y.astype(BF16)
        ag_scr[...] += _dot_tn(nb, dgate_ref[0])
        au_scr[...] += _dot_tn(nb, dup_ref[0])
        g = gate_ref[0].astype(F32)
        act = (g * _sigmoid(g) * up_ref[0].astype(F32)).astype(BF16)
        ad_scr[...] += _dot_tn(act, (0.5 * dh_ref[...]).astype(BF16))

        @pl.when(i == n_i - 1)
        def _():
            dwgu_ref[0, 0] = ag_scr[...].astype(BF16)
            dwgu_ref[1, 0] = au_scr[...].astype(BF16)
            dwd_ref[0] = ad_scr[...].astype(BF16)

    chunk = pl.BlockSpec((1, tm, F_CHUNK), lambda j, i: (j, i, 0))
    rows = pl.BlockSpec((tm, D_MODEL), lambda j, i: (i, 0))
    return pl.pallas_call(
        body, name=name, grid=(N_CHUNK, n_i),
        in_specs=[rows, rows, pl.BlockSpec((1, D_MODEL), lambda j, i: (0, 0)), chunk, chunk, chunk, chunk],
        out_specs=[pl.BlockSpec((2, 1, D_MODEL, F_CHUNK), lambda j, i: (0, j, 0, 0)),
                   pl.BlockSpec((1, F_CHUNK, D_MODEL), lambda j, i: (j, 0, 0))],
        out_shape=[jax.ShapeDtypeStruct((2, N_CHUNK, D_MODEL, F_CHUNK), BF16),
                   jax.ShapeDtypeStruct((N_CHUNK, F_CHUNK, D_MODEL), BF16)],
        scratch_shapes=[pltpu.VMEM((D_MODEL, F_CHUNK), F32), pltpu.VMEM((D_MODEL, F_CHUNK), F32),
                        pltpu.VMEM((F_CHUNK, D_MODEL), F32)],
        compiler_params=_params(("parallel", "arbitrary")),
    )(dh_out, h_in, gain, gate, up, dgate, dup)


N_PIECE = IN_MAIN // 512


def _inproj_fwd(h, gain, w_in):
    n = h.shape[0]
    tm = _row_tile(n, 512)

    def body(h_ref, g_ref, w_ref, *outs):
        y, _, _ = _rms(h_ref[...], g_ref[...])
        nb = y.astype(BF16)
        for p in range(N_PIECE):
            outs[p][...] = _dot(nb, w_ref[:, 512 * p:512 * (p + 1)]).astype(BF16)
        outs[N_PIECE][...] = _dot(nb, w_ref[:, IN_MAIN:IN_PAD])

    piece = pl.BlockSpec((tm, 512), lambda i: (i, 0))
    return pl.pallas_call(
        body, name="inproj_fwd", grid=(n // tm,),
        in_specs=[pl.BlockSpec((tm, D_MODEL), lambda i: (i, 0)),
                  pl.BlockSpec((1, D_MODEL), lambda i: (0, 0)),
                  pl.BlockSpec((D_MODEL, IN_PAD), lambda i: (0, 0))],
        out_specs=[piece] * N_PIECE + [pl.BlockSpec((tm, 128), lambda i: (i, 0))],
        out_shape=[jax.ShapeDtypeStruct((n, 512), BF16)] * N_PIECE + [jax.ShapeDtypeStruct((n, 128), F32)],
        compiler_params=_params(("parallel",)),
    )(h, gain, w_in)


def _inproj_bwd(dpieces, dfg, dh_out, h_in, gain, w_in):
    n = h_in.shape[0]
    tm = _row_tile(n, 512)
    n_i = n // tm

    def body(*refs):
        dp_refs = refs[:N_PIECE]
        dfg_ref, dh_ref, h_ref, g_ref, w_ref, dhin_ref, dw_ref, dgain_ref, acc_scr = refs[N_PIECE:]
        i = pl.program_id(0)

        @pl.when(i == 0)
        def _():
            acc_scr[...] = jnp.zeros_like(acc_scr)
            dgain_ref[...] = jnp.zeros_like(dgain_ref)

        gain_v = g_ref[...]
        y, xhat, r = _rms(h_ref[...], gain_v)
        nb = y.astype(BF16)
        dn = jnp.zeros((tm, D_MODEL), F32)
        for p in range(N_PIECE + 1):
            lo, hi = (512 * p, 512 * (p + 1)) if p < N_PIECE else (IN_MAIN, IN_PAD)
            dp = (dp_refs[p][...] if p < N_PIECE else dfg_ref[...]).astype(BF16)
            dn = dn + _dot_nt(dp, w_ref[:, lo:hi])
            acc_scr[:, lo:hi] += _dot_tn(nb, dp)
        dhin_ref[...] = dh_ref[...] + _rms_bwd(dn, xhat, r, gain_v)
        dgain_ref[...] += jnp.sum(dn * xhat, axis=0, keepdims=True)

        @pl.when(i == n_i - 1)
        def _():
            dw_ref[...] = acc_scr[...].astype(BF16)

    piece = pl.BlockSpec((tm, 512), lambda i: (i, 0))
    rows = pl.BlockSpec((tm, D_MODEL), lambda i: (i, 0))
    vec = pl.BlockSpec((1, D_MODEL), lambda i: (0, 0))
    wspec = pl.BlockSpec((D_MODEL, IN_PAD), lambda i: (0, 0))
    return pl.pallas_call(
        body, name="inproj_bwd", grid=(n_i,),
        in_specs=[piece] * N_PIECE + [pl.BlockSpec((tm, 128), lambda i: (i, 0)), rows, rows, vec, wspec],
        out_specs=[rows, wspec, vec],
        out_shape=[jax.ShapeDtypeStruct((n, D_MODEL), F32),
                   jax.ShapeDtypeStruct((D_MODEL, IN_PAD), BF16),
                   jax.ShapeDtypeStruct((1, D_MODEL), F32)],
        scratch_shapes=[pltpu.VMEM((D_MODEL, IN_PAD), F32)],
        compiler_params=_params(("arbitrary",)),
    )(*dpieces, dfg, dh_out, h_in, gain, w_in)


def _outproj_fwd(zc, za, w_out, h):
    n = h.shape[0]
    tm = _row_tile(n, 512)

    def body(zc_ref, za_ref, w_ref, h_ref, out_ref):
        out_ref[...] = (h_ref[...] + _dot(zc_ref[...], w_ref[0:CONV_DIM, :])
                        + _dot(za_ref[...], w_ref[CONV_DIM:, :]))

    half = pl.BlockSpec((tm, 512), lambda i: (i, 0))
    rows = pl.BlockSpec((tm, D_MODEL), lambda i: (i, 0))
    return pl.pallas_call(
        body, name="outproj_fwd", grid=(n // tm,),
        in_specs=[half, half, pl.BlockSpec((D_MODEL, D_MODEL), lambda i: (0, 0)), rows],
        out_specs=rows,
        out_shape=jax.ShapeDtypeStruct((n, D_MODEL), F32),
        compiler_params=_params(("parallel",)),
    )(zc, za, w_out, h)


def _outproj_bwd(dh, zc, za, w_out):
    n = dh.shape[0]
    tm = _row_tile(n, 512)
    n_i = n // tm

    def body(dh_ref, zc_ref, za_ref, w_ref, dzc_ref, dza_ref, dw_ref, acc_scr):
        i = pl.program_id(0)

        @pl.when(i == 0)
        def _():
            acc_scr[...] = jnp.zeros_like(acc_scr)

        dhb = dh_ref[...].astype(BF16)
        dzc_ref[...] = _dot_nt(dhb, w_ref[0:CONV_DIM, :]).astype(BF16)
        dza_ref[...] = _dot_nt(dhb, w_ref[CONV_DIM:, :]).astype(BF16)
        acc_scr[0:CONV_DIM, :] += _dot_tn(zc_ref[...], dhb)
        acc_scr[CONV_DIM:, :] += _dot_tn(za_ref[...], dhb)

        @pl.when(i == n_i - 1)
        def _():
            dw_ref[...] = acc_scr[...].astype(BF16)

    half = pl.BlockSpec((tm, 512), lambda i: (i, 0))
    wspec = pl.BlockSpec((D_MODEL, D_MODEL), lambda i: (0, 0))
    return pl.pallas_call(
        body, name="outproj_bwd", grid=(n_i,),
        in_specs=[pl.BlockSpec((tm, D_MODEL), lambda i: (i, 0)), half, half, wspec],
        out_specs=[half, half, wspec],
        out_shape=[jax.ShapeDtypeStruct((n, 512), BF16), jax.ShapeDtypeStruct((n, 512), BF16),
                   jax.ShapeDtypeStruct((D_MODEL, D_MODEL), BF16)],
        scratch_shapes=[pltpu.VMEM((D_MODEL, D_MODEL), F32)],
        compiler_params=_params(("arbitrary",)),
    )(dh, zc, za, w_out)


def _group_matrix():
    r = lax.broadcasted_iota(jnp.int32, (128, 128), 0) // HEAD_DIM
    c = lax.broadcasted_iota(jnp.int32, (128, 128), 1) // HEAD_DIM
    return jnp.where(r == c, 1.0 / HEAD_DIM, 0.0).astype(BF16)


def _group_mean(x, gmat):
    hi = x.astype(BF16)
    lo = (x - hi.astype(F32)).astype(BF16)
    return _dot(hi, gmat) + _dot(lo, gmat)


def _shift_rows(x, s):
    rows = x.shape[0]
    t = lax.broadcasted_iota(jnp.int32, x.shape, 0)
    rolled = pltpu.roll(x, s % rows, 0)
    keep = (t >= s) if s > 0 else (t < rows + s)
    return jnp.where(keep, rolled, 0.0)


def _conv_parts(bg_ref, cg_ref, hc_ref, w_ref):
    bg = bg_ref[...].astype(F32)
    cg = cg_ref[...].astype(F32)
    hc = hc_ref[...].astype(F32)
    u = cg * hc
    u1 = _shift_rows(u, 1)
    u2 = _shift_rows(u, 2)
    conv = w_ref[2:3, :] * u + w_ref[1:2, :] * u1 + w_ref[0:1, :] * u2
    return bg, cg, hc, u, u1, u2, conv


def _conv_fwd(bg, cg, hc, conv_w, gain, gmat, lp):
    n = bg.shape[0]
    nb = n // lp

    def body(bg_ref, cg_ref, hc_ref, w_ref, g_ref, gm_ref, z_ref):
        bgv, _, _, _, _, _, conv = _conv_parts(bg_ref, cg_ref, hc_ref, w_ref)
        yc = bgv * conv
        r = lax.rsqrt(_group_mean(yc * yc, gm_ref[...]) + EPS)
        z_ref[...] = (yc * r * g_ref[...]).astype(BF16)

    blk = pl.BlockSpec((lp, 128), lambda c, b: (b, c))
    return pl.pallas_call(
        body, name="conv_fwd", grid=(CONV_DIM // 128, nb),
        in_specs=[blk, blk, blk, pl.BlockSpec((3, 128), lambda c, b: (0, c)),
                  pl.BlockSpec((1, 128), lambda c, b: (0, c)), pl.BlockSpec((128, 128), lambda c, b: (0, 0))],
        out_specs=blk,
        out_shape=jax.ShapeDtypeStruct((n, CONV_DIM), BF16),
        compiler_params=_params(("parallel", "parallel")),
    )(bg, cg, hc, conv_w, gain, gmat)


def _conv_bwd(dz, bg, cg, hc, conv_w, gain, gmat, lp):
    n = bg.shape[0]
    nb = n // lp

    def body(dz_ref, bg_ref, cg_ref, hc_ref, w_ref, g_ref, gm_ref,
             dbg_ref, dcg_ref, dhc_ref, dw_ref, dgain_ref):
        b = pl.program_id(1)

        @pl.when(b == 0)
        def _():
            dw_ref[...] = jnp.zeros_like(dw_ref)
            dgain_ref[...] = jnp.zeros_like(dgain_ref)

        bgv, cgv, hcv, u, u1, u2, conv = _conv_parts(bg_ref, cg_ref, hc_ref, w_ref)
        gm = gm_ref[...]
        yc = bgv * conv
        r = lax.rsqrt(_group_mean(yc * yc, gm) + EPS)
        yhat = yc * r
        dzv = dz_ref[...].astype(F32)
        dyhat = dzv * g_ref[...]
        dgain_ref[...] += jnp.sum(dzv * yhat, axis=0, keepdims=True)
        dyc = r * (dyhat - yhat * _group_mean(dyhat * yhat, gm))
        dbg_ref[...] = (dyc * conv).astype(BF16)
        dconv = dyc * bgv
        du = (w_ref[2:3, :] * dconv + w_ref[1:2, :] * _shift_rows(dconv, -1)
              + w_ref[0:1, :] * _shift_rows(dconv, -2))
        dcg_ref[...] = (du * hcv).astype(BF16)
        dhc_ref[...] = (du * cgv).astype(BF16)
        dw_ref[0:1, :] += jnp.sum(dconv * u2, axis=0, keepdims=True)
        dw_ref[1:2, :] += jnp.sum(dconv * u1, axis=0, keepdims=True)
        dw_ref[2:3, :] += jnp.sum(dconv * u, axis=0, keepdims=True)

    blk = pl.BlockSpec((lp, 128), lambda c, b: (b, c))
    wspec = pl.BlockSpec((3, 128), lambda c, b: (0, c))
    gspec = pl.BlockSpec((1, 128), lambda c, b: (0, c))
    return pl.pallas_call(
        body, name="conv_bwd", grid=(CONV_DIM // 128, nb),
        in_specs=[blk, blk, blk, blk, wspec, gspec, pl.BlockSpec((128, 128), lambda c, b: (0, 0))],
        out_specs=[blk, blk, blk, wspec, gspec],
        out_shape=[jax.ShapeDtypeStruct((n, CONV_DIM), BF16)] * 3
        + [jax.ShapeDtypeStruct((3, CONV_DIM), F32), jax.ShapeDtypeStruct((1, CONV_DIM), F32)],
        compiler_params=_params(("parallel", "arbitrary")),
    )(dz, bg, cg, hc, conv_w, gain, gmat)


KEY_MASKED = 1e30
ONE_LANE = 24


def _scan_steps(rows):
    s, out = 1, []
    while s < rows:
        out.append(s)
        s *= 2
    return out


def _fgate_fwd(fg, b_f, lp):
    n = fg.shape[0]
    nb = n // lp

    def body(fg_ref, b_ref, ka_ref, qa_ref):
        x = fg_ref[...] + b_ref[...]
        logf = jnp.minimum(x, 0.0) - jnp.log(1.0 + jnp.exp(-jnp.abs(x)))
        t = lax.broadcasted_iota(jnp.int32, (lp, 128), 0)
        lane = lax.broadcasted_iota(jnp.int32, (lp, 128), 1)
        f = jnp.where((t >= PAD) & (lane < N_HEADS), logf, 0.0)
        for s in _scan_steps(lp):
            f = f + _shift_rows(f, s)
        hi = f.astype(BF16).astype(F32)
        rest = f - hi
        mid = rest.astype(BF16).astype(F32)
        lo = (rest - mid).astype(BF16).astype(F32)
        ones = jnp.where((lane >= ONE_LANE) & (lane < ONE_LANE + 3), 1.0, 0.0)
        hi_key = jnp.where((t < PAD) & (lane < N_HEADS), KEY_MASKED, hi)
        ka_ref[...] = (hi_key + pltpu.roll(mid, 8, 1) + pltpu.roll(lo, 16, 1) + ones).astype(BF16)
        for h in range(N_HEADS):
            minus = jnp.where((lane == h) | (lane == 8 + h) | (lane == 16 + h), -1.0, 0.0)
            terms = (jnp.where(lane == ONE_LANE, pltpu.roll(hi, ONE_LANE - h, 1), 0.0)
                     + jnp.where(lane == ONE_LANE + 1, pltpu.roll(mid, ONE_LANE + 1 - h, 1), 0.0)
                     + jnp.where(lane == ONE_LANE + 2, pltpu.roll(lo, ONE_LANE + 2 - h, 1), 0.0))
            qa_ref[:, 128 * h:128 * (h + 1)] = (minus + terms).astype(BF16)

    return pl.pallas_call(
        body, name="fgate_fwd", grid=(nb,),
        in_specs=[pl.BlockSpec((lp, 128), lambda b: (b, 0)), pl.BlockSpec((1, 128), lambda b: (0, 0))],
        out_specs=[pl.BlockSpec((lp, 128), lambda b: (b, 0)), pl.BlockSpec((lp, N_HEADS * 128), lambda b: (b, 0))],
        out_shape=[jax.ShapeDtypeStruct((n, 128), BF16), jax.ShapeDtypeStruct((n, N_HEADS * 128), BF16)],
        compiler_params=_params(("parallel",)),
    )(fg, b_f)


def _fgate_bwd(dka, dfr, fg, b_f, lp):
    n = fg.shape[0]
    nb = n // lp

    def body(dka_ref, dfr_ref, fg_ref, b_ref, dfg_ref, db_ref):
        b = pl.program_id(0)

        @pl.when(b == 0)
        def _():
            db_ref[...] = jnp.zeros_like(db_ref)

        wide = jnp.concatenate([dfr_ref[0], jnp.zeros((128 - N_HEADS, lp), F32)], axis=0)
        t = lax.broadcasted_iota(jnp.int32, (lp, 128), 0)
        lane = lax.broadcasted_iota(jnp.int32, (lp, 128), 1)
        d = jnp.where(lane < N_HEADS, dka_ref[...], 0.0) + wide.T
        for s in _scan_steps(lp):
            d = d + _shift_rows(d, -s)
        x = fg_ref[...] + b_ref[...]
        dx = jnp.where((t >= PAD) & (lane < N_HEADS), d * _sigmoid(-x), 0.0)
        dfg_ref[...] = dx
        db_ref[...] += jnp.sum(dx, axis=0, keepdims=True)

    return pl.pallas_call(
        body, name="fgate_bwd", grid=(nb,),
        in_specs=[pl.BlockSpec((lp, 128), lambda b: (b, 0)), pl.BlockSpec((1, N_HEADS, lp), lambda b: (b, 0, 0)),
                  pl.BlockSpec((lp, 128), lambda b: (b, 0)), pl.BlockSpec((1, 128), lambda b: (0, 0))],
        out_specs=[pl.BlockSpec((lp, 128), lambda b: (b, 0)), pl.BlockSpec((1, 128), lambda b: (0, 0))],
        out_shape=[jax.ShapeDtypeStruct((n, 128), F32), jax.ShapeDtypeStruct((1, 128), F32)],
        compiler_params=_params(("arbitrary",)),
    )(dka, dfr, fg, b_f)


def _head_masks():
    lane = lax.broadcasted_iota(jnp.int32, (1, 128), 1)
    return lane < HEAD_DIM


def _stack_heads(x2, first):
    zero = jnp.zeros_like(x2)
    return jnp.concatenate([jnp.where(first, x2, zero), jnp.where(first, zero, x2)], axis=0)


def _stack_heads_lanes(xt):
    r = lax.broadcasted_iota(jnp.int32, xt.shape, 0)
    zero = jnp.zeros_like(xt)
    return jnp.concatenate([jnp.where(r < HEAD_DIM, xt, zero), jnp.where(r < HEAD_DIM, zero, xt)], axis=1)


def _pair_cols(col0, col1, first):
    return jnp.where(first, col0, col1)


def _pair_rows(row0, row1):
    r = lax.broadcasted_iota(jnp.int32, (128, TQ), 0)
    return jnp.where(r < HEAD_DIM, row0, row1)


def _query_side(q_ref, qa_ref, p, first):
    q2 = q_ref[:, 128 * p:128 * (p + 1)] * 0.125
    zero = jnp.zeros_like(q2)
    top = jnp.concatenate([jnp.where(first, q2, zero), qa_ref[:, 128 * (2 * p):128 * (2 * p + 1)]], axis=1)
    bot = jnp.concatenate([jnp.where(first, zero, q2), qa_ref[:, 128 * (2 * p + 1):128 * (2 * p + 2)]], axis=1)
    return jnp.concatenate([top, bot], axis=0)


def _key_chunks(lp):
    return (lp + TK - 1) // TK


def _chunk_mask(i, c):
    r = lax.broadcasted_iota(jnp.int32, (TK, 2 * TQ), 0)
    col = lax.broadcasted_iota(jnp.int32, (TK, 2 * TQ), 1)
    return (c * TK + r) <= (i * TQ + (col & (TQ - 1)))


def _transpose_bf16(x):
    return x.astype(F32).T.astype(BF16)


def _attn_fwd(q, qa, k, v, ka, gain, lp):
    n = q.shape[0]
    nb = n // lp
    nq = lp // TQ
    lpp = _key_chunks(lp) * TK

    def body(q_ref, qa_ref, k_ref, v_ref, ka_ref, g_ref, z_ref, o_ref, lse_ref, kx_scr, vt_scr):
        i = pl.program_id(1)
        first = _head_masks()

        @pl.when(i == 0)
        def _():
            if lpp > lp:
                kx_scr[lp:lpp, :] = jnp.zeros((lpp - lp, 2 * ATTN_DIM), BF16)
                vt_scr[:, lp:lpp] = jnp.zeros((ATTN_DIM, lpp - lp), BF16)
            for p in range(N_PAIRS):
                kx_scr[0:lp, 256 * p:256 * p + 128] = k_ref[:, 128 * p:128 * (p + 1)]
                kx_scr[0:lp, 256 * p + 128:256 * (p + 1)] = ka_ref[...]
            vt_scr[:, 0:lp] = _transpose_bf16(v_ref[...])

        rhs_t = [_transpose_bf16(_query_side(q_ref, qa_ref, p, first)) for p in range(N_PAIRS)]

        def step(c, carry):
            koff = pl.multiple_of(c * TK, TK)
            valid = _chunk_mask(i, c)
            new = []
            for p in range(N_PAIRS):
                m, l, acc = carry[p]
                st = _dot(kx_scr[pl.ds(koff, TK), 256 * p:256 * (p + 1)], rhs_t[p])
                st = jnp.where(valid, st, NEG)
                m_new = jnp.maximum(m, jnp.max(st, axis=0, keepdims=True))
                pt = jnp.exp(st - m_new)
                alpha = jnp.exp(m - m_new)
                l = alpha * l + jnp.sum(pt, axis=0, keepdims=True)
                pb = pt.astype(BF16)
                vt = _stack_heads_lanes(vt_scr[128 * p:128 * (p + 1), pl.ds(koff, TK)])
                pv = _dot(vt, jnp.concatenate([pb[:, 0:TQ], pb[:, TQ:]], axis=0))
                acc = acc * _pair_rows(alpha[:, 0:TQ], alpha[:, TQ:]) + pv
                new.append((m_new, l, acc))
            return tuple(new)

        init = tuple((jnp.full((1, 2 * TQ), NEG, F32), jnp.zeros((1, 2 * TQ), F32), jnp.zeros((128, TQ), F32))
                     for _ in range(N_PAIRS))
        final = lax.fori_loop(0, (i + 2) // 2, step, init)

        row = lax.broadcasted_iota(jnp.int32, (TQ, 128), 0)
        real = (i * TQ + row) >= PAD
        for p in range(N_PAIRS):
            m, l, acc = final[p]
            inv = 1.0 / l
            ot = acc * _pair_rows(inv[:, 0:TQ], inv[:, TQ:])
            sq = ot * ot
            r0 = lax.rsqrt(jnp.sum(sq[0:HEAD_DIM], axis=0, keepdims=True) * (1.0 / HEAD_DIM) + EPS)
            r1 = lax.rsqrt(jnp.sum(sq[HEAD_DIM:], axis=0, keepdims=True) * (1.0 / HEAD_DIM) + EPS)
            cols = slice(128 * p, 128 * (p + 1))
            o_ref[:, cols] = jnp.where(real, ot.T, 0.0).astype(BF16)
            z_ref[:, cols] = (jnp.where(real, (ot * _pair_rows(r0, r1)).T, 0.0) * g_ref[:, cols]).astype(BF16)
            lse = m + jnp.log(l)
            lse_ref[0, 2 * p:2 * p + 1, :] = lse[:, 0:TQ]
            lse_ref[0, 2 * p + 1:2 * p + 2, :] = lse[:, TQ:]

    qblk = pl.BlockSpec((TQ, ATTN_DIM), lambda b, i: (b * nq + i, 0))
    qablk = pl.BlockSpec((TQ, N_HEADS * 128), lambda b, i: (b * nq + i, 0))
    seq = pl.BlockSpec((lp, ATTN_DIM), lambda b, i: (b, 0))
    rowblk = pl.BlockSpec((1, N_HEADS, TQ), lambda b, i: (b, 0, i))
    return pl.pallas_call(
        body, name="attn_fwd", grid=(nb, nq),
        in_specs=[qblk, qablk, seq, seq, pl.BlockSpec((lp, 128), lambda b, i: (b, 0)),
                  pl.BlockSpec((1, ATTN_DIM), lambda b, i: (0, 0))],
        out_specs=[qblk, qblk, rowblk],
        out_shape=[jax.ShapeDtypeStruct((n, ATTN_DIM), BF16), jax.ShapeDtypeStruct((n, ATTN_DIM), BF16),
                   jax.ShapeDtypeStruct((nb, N_HEADS, lp), F32)],
        scratch_shapes=[pltpu.VMEM((lpp, 2 * ATTN_DIM), BF16), pltpu.VMEM((ATTN_DIM, lpp), BF16)],
        compiler_params=_params(("parallel", "arbitrary")),
    )(q, qa, k, v, ka, gain)


def _attn_bwd(dz, q, qa, k, v, ka, o, lse, gain, lp, exchange=()):
    n = q.shape[0]
    nb = n // lp
    nq = lp // TQ
    lpp = _key_chunks(lp) * TK
    nw = len(exchange)

    def body(*refs):
        ((dz_ref, q_ref, qa_ref, k_ref, v_ref, ka_ref, o_ref, lse_ref, g_ref), xin,
         (dq_ref, dk_ref, dv_ref, dka_ref, dfr_ref, dgain_ref), xout,
         (kx_scr, vx_scr, kt_scr, dkx_scr, dvx_scr), sems) = _split_refs(refs, 9, nw, 6, 5)
        b = pl.program_id(0)
        i = pl.program_id(1)
        first = _head_masks()
        if nw:
            comm = _Exchange(xin, xout, sems)
            pl.when((b == 0) & (i == 0))(comm.start)

        @pl.when((b == 0) & (i == 0))
        def _():
            dgain_ref[...] = jnp.zeros_like(dgain_ref)

        @pl.when(i == 0)
        def _():
            if lpp > lp:
                kx_scr[lp:lpp, :] = jnp.zeros((lpp - lp, 2 * ATTN_DIM), BF16)
                vx_scr[lp:lpp, :] = jnp.zeros((lpp - lp, ATTN_DIM), BF16)
                kt_scr[:, lp:lpp] = jnp.zeros((ATTN_DIM, lpp - lp), BF16)
            for p in range(N_PAIRS):
                kx_scr[0:lp, 256 * p:256 * p + 128] = k_ref[:, 128 * p:128 * (p + 1)]
                kx_scr[0:lp, 256 * p + 128:256 * (p + 1)] = ka_ref[...]
            vx_scr[0:lp, :] = v_ref[...]
            kt_scr[:, 0:lp] = _transpose_bf16(k_ref[...])
            dkx_scr[...] = jnp.zeros_like(dkx_scr)
            dvx_scr[...] = jnp.zeros_like(dvx_scr)

        rhs, rhs_t, lses, dos, dos_t, deltas = [], [], [], [], [], []
        for p in range(N_PAIRS):
            cols = slice(128 * p, 128 * (p + 1))
            side = _query_side(q_ref, qa_ref, p, first)
            rhs.append(side)
            rhs_t.append(_transpose_bf16(side))
            lses.append(jnp.concatenate([lse_ref[0, 2 * p:2 * p + 1, :], lse_ref[0, 2 * p + 1:2 * p + 2, :]], axis=1))
            ov = o_ref[:, cols].astype(F32)
            dzv = dz_ref[:, cols].astype(F32)
            gv = g_ref[:, cols]
            sq = ov * ov
            ms0 = jnp.sum(jnp.where(first, sq, 0.0), axis=1, keepdims=True) * (1.0 / HEAD_DIM)
            ms1 = jnp.sum(jnp.where(first, 0.0, sq), axis=1, keepdims=True) * (1.0 / HEAD_DIM)
            r = _pair_cols(lax.rsqrt(ms0 + EPS), lax.rsqrt(ms1 + EPS), first)
            ohat = ov * r
            dyhat = dzv * gv
            dgain_ref[:, cols] += jnp.sum(dzv * ohat, axis=0, keepdims=True)
            pr = dyhat * ohat
            mean0 = jnp.sum(jnp.where(first, pr, 0.0), axis=1, keepdims=True) * (1.0 / HEAD_DIM)
            mean1 = jnp.sum(jnp.where(first, 0.0, pr), axis=1, keepdims=True) * (1.0 / HEAD_DIM)
            do = r * (dyhat - ohat * _pair_cols(mean0, mean1, first))
            ddt = (do * ov).T
            deltas.append(jnp.concatenate([jnp.sum(ddt[0:HEAD_DIM], axis=0, keepdims=True),
                                           jnp.sum(ddt[HEAD_DIM:], axis=0, keepdims=True)], axis=1))
            do_st = _stack_heads(do.astype(BF16), first)
            dos.append(do_st)
            dos_t.append(_transpose_bf16(do_st))

        def step(c, carry):
            koff = pl.multiple_of(c * TK, TK)
            valid = _chunk_mask(i, c)
            new = []
            for p in range(N_PAIRS):
                dqt, dfq = carry[p]
                ext = slice(256 * p, 256 * (p + 1))
                cols = slice(128 * p, 128 * (p + 1))
                st = _dot(kx_scr[pl.ds(koff, TK), ext], rhs_t[p])
                st = jnp.where(valid, st, NEG)
                pt = jnp.exp(st - lses[p])
                dpt = _dot(vx_scr[pl.ds(koff, TK), cols], dos_t[p])
                dst = pt * (dpt - deltas[p])
                dsb = dst.astype(BF16)
                dfq = dfq + jnp.sum(dsb.astype(F32), axis=0, keepdims=True)
                dkx_scr[pl.ds(koff, TK), ext] += _dot(dsb, rhs[p])
                dvx_scr[pl.ds(koff, TK), cols] += _dot(pt.astype(BF16), dos[p])
                kt = _stack_heads_lanes(kt_scr[cols, pl.ds(koff, TK)])
                dqt = dqt + _dot(kt, jnp.concatenate([dsb[:, 0:TQ], dsb[:, TQ:]], axis=0))
                new.append((dqt, dfq))
            return tuple(new)

        init = tuple((jnp.zeros((128, TQ), F32), jnp.zeros((1, 2 * TQ), F32)) for _ in range(N_PAIRS))
        final = lax.fori_loop(0, (i + 2) // 2, step, init)

        for p in range(N_PAIRS):
            dqt, dfq = final[p]
            dq_ref[:, 128 * p:128 * (p + 1)] = (dqt.T * 0.125).astype(BF16)
            dfr_ref[0, 2 * p:2 * p + 1, :] = dfq[:, 0:TQ]
            dfr_ref[0, 2 * p + 1:2 * p + 2, :] = dfq[:, TQ:]

        @pl.when(i == nq - 1)
        def _():
            dka = jnp.zeros((lp, 128), F32)
            for p in range(N_PAIRS):
                dk_ref[:, 128 * p:128 * (p + 1)] = dkx_scr[0:lp, 256 * p:256 * p + 128].astype(BF16)
                dka = dka + dkx_scr[0:lp, 256 * p + 128:256 * (p + 1)]
            dka_ref[...] = dka
            dv_ref[...] = dvx_scr[0:lp, :].astype(BF16)

        if nw:
            pl.when((b == nb - 1) & (i == nq - 1))(comm.finish)

    qblk = pl.BlockSpec((TQ, ATTN_DIM), lambda b, i: (b * nq + i, 0))
    qablk = pl.BlockSpec((TQ, N_HEADS * 128), lambda b, i: (b * nq + i, 0))
    seq = pl.BlockSpec((lp, ATTN_DIM), lambda b, i: (b, 0))
    kaseq = pl.BlockSpec((lp, 128), lambda b, i: (b, 0))
    rowblk = pl.BlockSpec((1, N_HEADS, TQ), lambda b, i: (b, 0, i))
    gspec = pl.BlockSpec((1, ATTN_DIM), lambda b, i: (0, 0))
    return pl.pallas_call(
        body, name="attn_bwd", grid=(nb, nq),
        in_specs=[qblk, qblk, qablk, seq, seq, kaseq, qblk, rowblk, gspec] + [ANY] * nw,
        out_specs=[qblk, seq, seq, kaseq, rowblk, gspec] + [ANY] * nw,
        out_shape=[jax.ShapeDtypeStruct((n, ATTN_DIM), BF16), jax.ShapeDtypeStruct((n, ATTN_DIM), BF16),
                   jax.ShapeDtypeStruct((n, ATTN_DIM), BF16), jax.ShapeDtypeStruct((n, 128), F32),
                   jax.ShapeDtypeStruct((nb, N_HEADS, lp), F32), jax.ShapeDtypeStruct((1, ATTN_DIM), F32)]
        + [jax.ShapeDtypeStruct(a.shape, a.dtype) for a in exchange],
        scratch_shapes=[pltpu.VMEM((lpp, 2 * ATTN_DIM), BF16), pltpu.VMEM((lpp, ATTN_DIM), BF16),
                        pltpu.VMEM((ATTN_DIM, lpp), BF16), pltpu.VMEM((lpp, 2 * ATTN_DIM), F32),
                        pltpu.VMEM((lpp, ATTN_DIM), F32)] + (_comm_sems(nw) if nw else []),
        compiler_params=_params(("arbitrary", "arbitrary")),
    )(dz, q, qa, k, v, ka, o, lse, gain, *exchange)


def _loss_head(h, gain, target, lp):
    n = h.shape[0]
    nb = n // lp
    nq = lp // 128

    def body(h_ref, g_ref, t_ref, loss_ref, dh_ref, dgain_ref):
        b = pl.program_id(0)
        i = pl.program_id(1)

        @pl.when((b == 0) & (i == 0))
        def _():
            loss_ref[...] = jnp.zeros_like(loss_ref)
            dgain_ref[...] = jnp.zeros_like(dgain_ref)

        @pl.when(i == 0)
        def _():
            dh_ref[...] = jnp.zeros_like(dh_ref)

        @pl.when(i > 0)
        def _():
            gain_v = g_ref[...]
            y, xhat, r = _rms(h_ref[...], gain_v)
            err = y - t_ref[...]
            loss_ref[...] += 0.5 * jnp.sum(jnp.sum(err * err, axis=1, keepdims=True), axis=0,
                                           keepdims=True) * (1.0 / D_MODEL)
            dy = err * (1.0 / D_MODEL)
            dh_ref[...] = _rms_bwd(dy, xhat, r, gain_v)
            dgain_ref[...] += jnp.sum(dy * xhat, axis=0, keepdims=True)

    rows = pl.BlockSpec((128, D_MODEL), lambda b, i: (b * nq + i, 0))
    trows = pl.BlockSpec((128, D_MODEL), lambda b, i: (b * (nq - 1) + jnp.maximum(i, 1) - 1, 0))
    return pl.pallas_call(
        body, name="loss_head", grid=(nb, nq),
        in_specs=[rows, pl.BlockSpec((1, D_MODEL), lambda b, i: (0, 0)), trows],
        out_specs=[pl.BlockSpec((1, 1), lambda b, i: (0, 0)), rows, pl.BlockSpec((1, D_MODEL), lambda b, i: (0, 0))],
        out_shape=[jax.ShapeDtypeStruct((1, 1), F32), jax.ShapeDtypeStruct((n, D_MODEL), F32),
                   jax.ShapeDtypeStruct((1, D_MODEL), F32)],
        compiler_params=_params(("arbitrary", "arbitrary")),
    )(h, gain, target)


def _adamw(parts, w, m, v, name):
    s_parts, r, c = parts.shape
    tr = r
    for t in (256, 128, 64, 32, 16):
        if r % t == 0 and r > t:
            tr = t
            break

    def body(p_ref, w_ref, m_ref, v_ref, g_ref, d_ref, nm_ref, nv_ref):
        g = p_ref[0].astype(F32)
        for s in range(1, s_parts):
            g = g + p_ref[s].astype(F32)
        nm = ADAM_B1 * m_ref[...] + (1.0 - ADAM_B1) * g
        nv = ADAM_B2 * v_ref[...] + (1.0 - ADAM_B2) * (g * g)
        m_hat = nm / (1.0 - ADAM_B1 ** ADAM_STEP)
        v_hat = nv / (1.0 - ADAM_B2 ** ADAM_STEP)
        g_ref[...] = g
        d_ref[...] = -ADAM_LR * (m_hat / (jnp.sqrt(v_hat) + ADAM_EPS) + ADAM_WD * w_ref[...])
        nm_ref[...] = nm
        nv_ref[...] = nv

    blk = pl.BlockSpec((tr, c), lambda i: (i, 0))
    return pl.pallas_call(
        body, name=name, grid=(r // tr,),
        in_specs=[pl.BlockSpec((s_parts, tr, c), lambda i: (0, i, 0)), blk, blk, blk],
        out_specs=[blk] * 4,
        out_shape=[jax.ShapeDtypeStruct((r, c), F32)] * 4,
        compiler_params=_params(("parallel",)),
    )(parts, w, m, v)


def _sum_parts(parts, name):
    s_parts, r, c = parts.shape

    def body(p_ref, out_ref):
        acc = p_ref[0]
        for s in range(1, s_parts):
            acc = acc + p_ref[s]
        out_ref[...] = acc

    return pl.pallas_call(
        body, name=name, out_shape=jax.ShapeDtypeStruct((r, c), F32),
        in_specs=[pl.BlockSpec(memory_space=pltpu.VMEM)], out_specs=pl.BlockSpec(memory_space=pltpu.VMEM),
    )(parts)


SMALL_ROWS = 184


def _pack_small(d_gains, d_gc, d_ga, d_bf, d_conv, d_meta):
    rows = [g.reshape(8, 128) for g in d_gains]
    rows += [d_gc.reshape(4, 128), d_ga.reshape(4, 128), d_bf.reshape(1, 128)]
    rows += [d_conv.reshape(12, 128), d_meta.reshape(128, 128)]
    packed = jnp.concatenate(rows, axis=0)
    return jnp.pad(packed, ((0, SMALL_ROWS - packed.shape[0]), (0, 0)))


def kernel(x, meta_tokens, ffn1_norm, ffn1_w_gu, ffn1_w_down, mix_norm, w_in, conv_w, b_f, out_norm_conv, out_norm_attn, w_out, ffn2_norm, ffn2_w_gu, ffn2_w_down, final_norm, loss_target, m_meta_tokens, m_ffn1_norm, m_ffn1_w_gu, m_ffn1_w_down, m_mix_norm, m_w_in, m_conv_w, m_b_f, m_out_norm_conv, m_out_norm_attn, m_w_out, m_ffn2_norm, m_ffn2_w_gu, m_ffn2_w_down, m_final_norm, v_meta_tokens, v_ffn1_norm, v_ffn1_w_gu, v_ffn1_w_down, v_mix_norm, v_w_in, v_conv_w, v_b_f, v_out_norm_conv, v_out_norm_attn, v_w_out, v_ffn2_norm, v_ffn2_w_gu, v_ffn2_w_down, v_final_norm):
    nb, seq, _ = x.shape
    lp = PAD + N_META + seq
    n = nb * lp
    me = 4 * lax.axis_index("x") + 2 * lax.axis_index("y") + lax.axis_index("c")

    wgu1_8, wd1_8 = _all_gather([ffn1_w_gu[0].astype(BF16), ffn1_w_down[0].astype(BF16)], "gather_ffn1")
    small_in = jnp.concatenate(
        [meta_tokens, jnp.pad(conv_w[0], ((0, 0), (0, 128 - conv_w.shape[2]))), jnp.zeros((5, 128), F32)], axis=0)
    (small_8,) = _all_gather([small_in], "gather_small")
    meta_full = small_8[:, 0:N_META, :].transpose(1, 0, 2).reshape(N_META, D_MODEL)
    conv_full = small_8[:, N_META:N_META + 3, 0:CONV_DIM // N_DEV].transpose(1, 0, 2).reshape(3, CONV_DIM)
    wgu1 = wgu1_8.reshape(2, N_CHUNK, D_MODEL, F_CHUNK)
    wd1 = wd1_8.reshape(N_CHUNK, F_CHUNK, D_MODEL)
    b_f_row = jnp.pad(b_f, ((0, 0), (0, 128 - N_HEADS)))
    gmat = _group_matrix()

    h0 = jnp.concatenate([jnp.zeros((nb, PAD, D_MODEL), F32),
                          jnp.broadcast_to(meta_full[None], (nb, N_META, D_MODEL)), x], axis=1).reshape(n, D_MODEL)
    later = [w_in[0].astype(BF16), w_out[0].astype(BF16), ffn2_w_gu[0].astype(BF16), ffn2_w_down[0].astype(BF16)]
    h1, gate1, up1, win_8, wout_8, wgu2_8, wd2_8 = _ffn_fwd(h0, ffn1_norm, wgu1, wd1, "ffn1_fwd", gather=later)
    wgu2 = wgu2_8.reshape(2, N_CHUNK, D_MODEL, F_CHUNK)
    wd2 = wd2_8.reshape(N_CHUNK, F_CHUNK, D_MODEL)
    w_in_full = jnp.pad(win_8.transpose(1, 0, 2).reshape(D_MODEL, IN_DIM), ((0, 0), (0, IN_PAD - IN_DIM)))
    w_out_full = wout_8.reshape(D_MODEL, D_MODEL)

    bg, cg, hc, q, k, v, fg = _inproj_fwd(h1, mix_norm, w_in_full)
    zc = _conv_fwd(bg, cg, hc, conv_full, out_norm_conv, gmat, lp)
    ka, qa = _fgate_fwd(fg, b_f_row, lp)
    za, o, lse = _attn_fwd(q, qa, k, v, ka, out_norm_attn, lp)
    h2 = _outproj_fwd(zc, za, w_out_full, h1)
    h3, gate2, up2 = _ffn_fwd(h2, ffn2_norm, wgu2, wd2, "ffn2_fwd")
    loss_part, dh3, d_final = _loss_head(h3, final_norm.reshape(1, D_MODEL), loss_target.reshape(nb * seq, D_MODEL), lp)

    dh2, dgate2, dup2, d_ffn2 = _ffn_bwd_x(dh3, h2, ffn2_norm, gate2, up2, wgu2, wd2, "ffn2_bwd_x")
    dwgu2, dwd2 = _ffn_bwd_w(dh3, h2, ffn2_norm, gate2, up2, dgate2, dup2, "ffn2_bwd_w")
    dzc, dza, dwout = _outproj_bwd(dh2, zc, za, w_out_full)
    send_a = [dwgu2.reshape(N_DEV, D_MODEL, F_CHUNK), dwd2.reshape(N_DEV, F_CHUNK // 2, D_MODEL),
              dwout.reshape(N_DEV, D_MODEL // N_DEV, D_MODEL)]
    dq, dk, dv, dka, dfr, d_ga, p_wgu2, p_wd2, p_wout = _attn_bwd(
        dza, q, qa, k, v, ka, o, lse, out_norm_attn, lp, exchange=send_a)
    dfg, d_bf = _fgate_bwd(dka, dfr, fg, b_f_row, lp)
    dbg, dcg, dhc, d_conv, d_gc = _conv_bwd(dzc, bg, cg, hc, conv_full, out_norm_conv, gmat, lp)
    dh1, dwin, d_mix = _inproj_bwd([dbg, dcg, dhc, dq, dk, dv], dfg, dh2, h1, mix_norm, w_in_full)
    dgate1, dup1 = _ffn_bwd_act(dh1, gate1, up1, wd1, "ffn1_bwd_act")
    dwgu1, dwd1 = _ffn_bwd_w(dh1, h0, ffn1_norm, gate1, up1, dgate1, dup1, "ffn1_bwd_w")
    dwin_8 = dwin[:, 0:IN_DIM].reshape(D_MODEL, N_DEV, IN_DIM // N_DEV).transpose(1, 0, 2)
    send_b = [dwgu1.reshape(N_DEV, D_MODEL, F_CHUNK), dwd1.reshape(N_DEV, F_CHUNK // 2, D_MODEL), dwin_8]
    dh0, d_ffn1, p_wgu1, p_wd1, p_win = _ffn_bwd_in(
        dh1, h0, ffn1_norm, dgate1, dup1, wgu1, "ffn1_bwd_in", exchange=send_b)

    dh0 = dh0.reshape(nb, lp, D_MODEL)
    grad_x = dh0[:, PAD + N_META:, :]
    d_meta = jnp.sum(dh0[:, PAD:PAD + N_META, :], axis=0)

    small = _pack_small([d_ffn1, d_mix, d_ffn2, d_final], d_gc, d_ga, d_bf, d_conv, d_meta)
    (small_all,) = _all_gather([small], "gather_small_grads")
    small_sum = _sum_parts(small_all, "sum_small_grads")
    g_ffn1n, g_mixn, g_ffn2n, g_finaln = (small_sum[8 * t:8 * t + 8].reshape(1, D_MODEL) for t in range(4))
    g_gc = small_sum[32:36].reshape(1, CONV_DIM)
    g_ga = small_sum[36:40].reshape(1, ATTN_DIM)
    g_bf = small_sum[40:41, 0:N_HEADS]
    g_conv_full = small_sum[41:53].reshape(3, CONV_DIM)
    g_meta_full = small_sum[53:181].reshape(N_META, D_MODEL)
    g_conv = lax.dynamic_slice_in_dim(g_conv_full, me * (CONV_DIM // N_DEV), CONV_DIM // N_DEV, axis=1)
    g_meta = lax.dynamic_slice_in_dim(g_meta_full, me * (D_MODEL // N_DEV), D_MODEL // N_DEV, axis=1)

    weights = {
        "meta_tokens": (g_meta[None], meta_tokens, m_meta_tokens, v_meta_tokens),
        "ffn1_norm": (g_ffn1n[None], ffn1_norm, m_ffn1_norm, v_ffn1_norm),
        "ffn1_w_gu": (p_wgu1, ffn1_w_gu[0], m_ffn1_w_gu[0], v_ffn1_w_gu[0]),
        "ffn1_w_down": (p_wd1, ffn1_w_down[0], m_ffn1_w_down[0], v_ffn1_w_down[0]),
        "mix_norm": (g_mixn[None], mix_norm, m_mix_norm, v_mix_norm),
        "w_in": (p_win, w_in[0], m_w_in[0], v_w_in[0]),
        "conv_w": (g_conv[None], conv_w[0], m_conv_w[0], v_conv_w[0]),
        "b_f": (g_bf[None], b_f, m_b_f, v_b_f),
        "out_norm_conv": (g_gc[None], out_norm_conv, m_out_norm_conv, v_out_norm_conv),
        "out_norm_attn": (g_ga[None], out_norm_attn, m_out_norm_attn, v_out_norm_attn),
        "w_out": (p_wout, w_out[0], m_w_out[0], v_w_out[0]),
        "ffn2_norm": (g_ffn2n[None], ffn2_norm, m_ffn2_norm, v_ffn2_norm),
        "ffn2_w_gu": (p_wgu2, ffn2_w_gu[0], m_ffn2_w_gu[0], v_ffn2_w_gu[0]),
        "ffn2_w_down": (p_wd2, ffn2_w_down[0], m_ffn2_w_down[0], v_ffn2_w_down[0]),
        "final_norm": (g_finaln[None], final_norm.reshape(1, D_MODEL), m_final_norm.reshape(1, D_MODEL),
                       v_final_norm.reshape(1, D_MODEL)),
    }
    shapes = {"meta_tokens": meta_tokens.shape, "ffn1_norm": ffn1_norm.shape, "ffn1_w_gu": ffn1_w_gu.shape,
              "ffn1_w_down": ffn1_w_down.shape, "mix_norm": mix_norm.shape, "w_in": w_in.shape,
              "conv_w": conv_w.shape, "b_f": b_f.shape, "out_norm_conv": out_norm_conv.shape,
              "out_norm_attn": out_norm_attn.shape, "w_out": w_out.shape, "ffn2_norm": ffn2_norm.shape,
              "ffn2_w_gu": ffn2_w_gu.shape, "ffn2_w_down": ffn2_w_down.shape, "final_norm": final_norm.shape}
    grads, deltas, new_m, new_v = [], [], [], []
    for name, (p, w, m, vv) in weights.items():
        g, d, nm, nv = _adamw(p, w, m, vv, "adamw_" + name)
        shape = shapes[name]
        grads.append(g.reshape(shape))
        deltas.append(d.reshape(shape))
        new_m.append(nm.reshape(shape))
        new_v.append(nv.reshape(shape))

    loss = lax.psum(loss_part[0, 0], ("x", "y", "c"))
    return (loss, grad_x, *grads, *deltas, *new_m, *new_v)
```

```python
import jax
import jax.numpy as jnp
from jax import lax
from jax.experimental import pallas as pl
from jax.experimental.pallas import tpu as pltpu

F32 = jnp.float32
BF16 = jnp.bfloat16

N_DEV = 8
D_MODEL = 1024
N_META = 16
PAD = 128 - N_META
CONV_DIM = 512
ATTN_DIM = 512
HEAD_DIM = 64
N_HEADS = 8
N_PAIRS = N_HEADS // 2
D_FF = 2816
N_CHUNK = 4
F_CHUNK = D_FF // N_CHUNK
IN_DIM = 3080
IN_PAD = 3200
IN_MAIN = 3072
EPS = 1e-6
NEG = -1e30
TQ = 128
TK = 256
VMEM_LIMIT = 56 * 1024 * 1024

ADAM_LR = 0.001
ADAM_B1 = 0.9
ADAM_B2 = 0.999
ADAM_EPS = 1e-08
ADAM_WD = 0.01
ADAM_STEP = 10

MESH = pl.DeviceIdType.MESH
ANY = pl.BlockSpec(memory_space=pl.ANY)


def _params(sem=None):
    return pltpu.CompilerParams(dimension_semantics=sem, vmem_limit_bytes=VMEM_LIMIT)


def _row_tile(n, prefer):
    for t in (prefer, 512, 256, 128):
        if t <= n and n % t == 0:
            return t
    raise ValueError(f"no row tile for {n}")


def _dot(a, b):
    return jnp.dot(a, b, preferred_element_type=F32)


def _dot_nt(a, b):
    return lax.dot_general(a, b, (((1,), (1,)), ((), ())), preferred_element_type=F32)


def _dot_tn(a, b):
    return lax.dot_general(a, b, (((0,), (0,)), ((), ())), preferred_element_type=F32)


def _rms(x, g):
    r = lax.rsqrt(jnp.mean(x * x, axis=-1, keepdims=True) + EPS)
    xhat = x * r
    return xhat * g, xhat, r


def _rms_bwd(dn, xhat, r, g):
    dxhat = dn * g
    return r * (dxhat - xhat * jnp.mean(dxhat * xhat, axis=-1, keepdims=True))


def _sigmoid(x):
    return 1.0 / (1.0 + jnp.exp(-x))


def _place():
    return lax.axis_index("x"), lax.axis_index("y"), lax.axis_index("c")


def _comm_sems(nw):
    return [pltpu.SemaphoreType.DMA((nw, 7)), pltpu.SemaphoreType.DMA((nw, 7)), pltpu.SemaphoreType.DMA((nw,))]


class _Gather:
    def __init__(self, ins, outs, sems):
        self.ins, self.outs = ins, outs
        self.send, self.recv, self.local = sems
        x, y, c = _place()
        self.c = c
        self.me, self.sibling = (x, y, c), (x, y, 1 - c)
        self.chips = [(1 - x, y), (x, 1 - y), (1 - x, 1 - y)]

    def _copy(self, w, k, block, to, own=False):
        slot = self.outs[w].at[4 * block[0] + 2 * block[1] + block[2]]
        return pltpu.make_async_remote_copy(
            src_ref=self.ins[w] if own else slot, dst_ref=slot,
            send_sem=self.send.at[w, k], recv_sem=self.recv.at[w, k], device_id=to, device_id_type=MESH)

    def _mine(self, w):
        x, y, c = self.me
        return pltpu.make_async_copy(self.ins[w], self.outs[w].at[4 * x + 2 * y + c], self.local.at[w])

    def _first(self, w):
        return ([self._copy(w, 0, self.me, self.sibling, own=True)]
                + [self._copy(w, 1 + j, self.me, (*chip, self.c), own=True) for j, chip in enumerate(self.chips)])

    def _passed(self, w):
        return [self._copy(w, 4 + j, (*chip, self.c), self.sibling) for j, chip in enumerate(self.chips)]

    def start(self):
        for w in range(len(self.ins)):
            self._mine(w).start()
        for w in range(len(self.ins)):
            for cp in self._first(w):
                cp.start()

    def forward(self):
        for w in range(len(self.ins)):
            for j, chip in enumerate(self.chips):
                self._copy(w, 1 + j, (*chip, self.c), self.me).wait_recv()
                self._passed(w)[j].start()

    def finish(self):
        for w in range(len(self.ins)):
            self._copy(w, 0, self.sibling, self.me).wait_recv()
            for j, chip in enumerate(self.chips):
                self._copy(w, 4 + j, (*chip, 1 - self.c), self.me).wait_recv()
        for w in range(len(self.ins)):
            for cp in self._first(w) + self._passed(w):
                cp.wait_send()
            self._mine(w).wait()


class _Exchange:
    def __init__(self, ins, outs, sems):
        self.ins, self.outs = ins, outs
        self.send, self.recv, self.local = sems
        self.x, self.y, self.c = _place()
        self.me = 4 * self.x + 2 * self.y + self.c

    def _copy(self, w, k):
        flip = lambda v, bit: 1 - v if bit else v
        peer = (flip(self.x, ((k + 1) >> 2) & 1), flip(self.y, ((k + 1) >> 1) & 1), flip(self.c, (k + 1) & 1))
        return pltpu.make_async_remote_copy(
            src_ref=self.ins[w].at[4 * peer[0] + 2 * peer[1] + peer[2]], dst_ref=self.outs[w].at[self.me],
            send_sem=self.send.at[w, k], recv_sem=self.recv.at[w, k], device_id=peer, device_id_type=MESH)

    def _mine(self, w):
        return pltpu.make_async_copy(self.ins[w].at[self.me], self.outs[w].at[self.me], self.local.at[w])

    def start(self):
        for w in range(len(self.ins)):
            self._mine(w).start()
            for k in range(N_DEV - 1):
                self._copy(w, k).start()

    def finish(self):
        for w in range(len(self.ins)):
            for k in range(N_DEV - 1):
                self._copy(w, k).wait()
            self._mine(w).wait()


def _split_refs(refs, n_in, n_comm, n_out, n_scr):
    a = n_in
    b = a + n_comm
    c = b + n_out
    d = c + n_comm
    e = d + n_scr
    return refs[:a], refs[a:b], refs[b:c], refs[c:d], refs[d:e], refs[e:]


def _all_gather(xs, name):
    nw = len(xs)

    def body(*refs):
        comm = _Gather(refs[:nw], refs[nw:2 * nw], refs[2 * nw:])
        comm.start()
        comm.forward()
        comm.finish()

    return pl.pallas_call(
        body, name=name, in_specs=[ANY] * nw, out_specs=[ANY] * nw,
        out_shape=[jax.ShapeDtypeStruct((N_DEV,) + a.shape, a.dtype) for a in xs],
        scratch_shapes=_comm_sems(nw),
    )(*xs)


def _ffn_fwd(h, gain, wgu, wd, name, gather=()):
    n = h.shape[0]
    tm = _row_tile(n, 512)
    n_i = n // tm
    nw = len(gather)

    def body(*refs):
        (h_ref, g_ref, wgu_ref, wd_ref), gin, (out_ref, gate_ref, up_ref), gout, (n_scr, acc_scr), sems = \
            _split_refs(refs, 4, nw, 3, 2)
        i = pl.program_id(0)
        j = pl.program_id(1)
        if nw:
            comm = _Gather(gin, gout, sems)
            pl.when((i == 0) & (j == 0))(comm.start)
            pl.when((i == (3 * n_i) // 4) & (j == 0))(comm.forward)

        @pl.when(j == 0)
        def _():
            y, _, _ = _rms(h_ref[...], g_ref[...])
            n_scr[...] = y.astype(BF16)
            acc_scr[...] = jnp.zeros_like(acc_scr)

        nb = n_scr[...]
        gate = _dot(nb, wgu_ref[0, 0])
        up = _dot(nb, wgu_ref[1, 0])
        gate_ref[0] = gate.astype(BF16)
        up_ref[0] = up.astype(BF16)
        act = (gate * _sigmoid(gate) * up).astype(BF16)
        acc_scr[...] += _dot(act, wd_ref[0])

        @pl.when(j == N_CHUNK - 1)
        def _():
            out_ref[...] = h_ref[...] + 0.5 * acc_scr[...]

        if nw:
            pl.when((i == n_i - 1) & (j == N_CHUNK - 1))(comm.finish)

    return pl.pallas_call(
        body, name=name, grid=(n_i, N_CHUNK),
        in_specs=[pl.BlockSpec((tm, D_MODEL), lambda i, j: (i, 0)),
                  pl.BlockSpec((1, D_MODEL), lambda i, j: (0, 0)),
                  pl.BlockSpec((2, 1, D_MODEL, F_CHUNK), lambda i, j: (0, j, 0, 0)),
                  pl.BlockSpec((1, F_CHUNK, D_MODEL), lambda i, j: (j, 0, 0))] + [ANY] * nw,
        out_specs=[pl.BlockSpec((tm, D_MODEL), lambda i, j: (i, 0)),
                   pl.BlockSpec((1, tm, F_CHUNK), lambda i, j: (j, i, 0)),
                   pl.BlockSpec((1, tm, F_CHUNK), lambda i, j: (j, i, 0))] + [ANY] * nw,
        out_shape=[jax.ShapeDtypeStruct((n, D_MODEL), F32),
                   jax.ShapeDtypeStruct((N_CHUNK, n, F_CHUNK), BF16),
                   jax.ShapeDtypeStruct((N_CHUNK, n, F_CHUNK), BF16)]
        + [jax.ShapeDtypeStruct((N_DEV,) + a.shape, a.dtype) for a in gather],
        scratch_shapes=[pltpu.VMEM((tm, D_MODEL), BF16), pltpu.VMEM((tm, D_MODEL), F32)]
        + (_comm_sems(nw) if nw else []),
        compiler_params=_params(("arbitrary", "arbitrary")),
    )(h, gain, wgu, wd, *gather)


def _ffn_bwd_x(dh_out, h_in, gain, gate, up, wgu, wd, name):
    n = h_in.shape[0]
    tm = _row_tile(n, 512)

    def body(dh_ref, h_ref, g_ref, gate_ref, up_ref, wgu_ref, wd_ref,
             dhin_ref, dgate_ref, dup_ref, dgain_ref, dhb_scr, acc_scr):
        i = pl.program_id(0)
        j = pl.program_id(1)

        @pl.when((i == 0) & (j == 0))
        def _():
            dgain_ref[...] = jnp.zeros_like(dgain_ref)

        @pl.when(j == 0)
        def _():
            dhb_scr[...] = (0.5 * dh_ref[...]).astype(BF16)
            acc_scr[...] = jnp.zeros_like(acc_scr)

        da = _dot_nt(dhb_scr[...], wd_ref[0])
        g = gate_ref[0].astype(F32)
        u = up_ref[0].astype(F32)
        sig = _sigmoid(g)
        dgate = (da * u * (sig * (1.0 + g * (1.0 - sig)))).astype(BF16)
        dup = (da * (g * sig)).astype(BF16)
        dgate_ref[0] = dgate
        dup_ref[0] = dup
        acc_scr[...] += _dot_nt(dgate, wgu_ref[0, 0]) + _dot_nt(dup, wgu_ref[1, 0])

        @pl.when(j == N_CHUNK - 1)
        def _():
            gain_v = g_ref[...]
            _, xhat, r = _rms(h_ref[...], gain_v)
            dn = acc_scr[...]
            dhin_ref[...] = dh_ref[...] + _rms_bwd(dn, xhat, r, gain_v)
            dgain_ref[...] += jnp.sum(dn * xhat, axis=0, keepdims=True)

    chunk = pl.BlockSpec((1, tm, F_CHUNK), lambda i, j: (j, i, 0))
    rows = pl.BlockSpec((tm, D_MODEL), lambda i, j: (i, 0))
    vec = pl.BlockSpec((1, D_MODEL), lambda i, j: (0, 0))
    return pl.pallas_call(
        body, name=name, grid=(n // tm, N_CHUNK),
        in_specs=[rows, rows, vec, chunk, chunk,
                  pl.BlockSpec((2, 1, D_MODEL, F_CHUNK), lambda i, j: (0, j, 0, 0)),
                  pl.BlockSpec((1, F_CHUNK, D_MODEL), lambda i, j: (j, 0, 0))],
        out_specs=[rows, chunk, chunk, vec],
        out_shape=[jax.ShapeDtypeStruct((n, D_MODEL), F32),
                   jax.ShapeDtypeStruct((N_CHUNK, n, F_CHUNK), BF16),
                   jax.ShapeDtypeStruct((N_CHUNK, n, F_CHUNK), BF16),
                   jax.ShapeDtypeStruct((1, D_MODEL), F32)],
        scratch_shapes=[pltpu.VMEM((tm, D_MODEL), BF16), pltpu.VMEM((tm, D_MODEL), F32)],
        compiler_params=_params(("arbitrary", "arbitrary")),
    )(dh_out, h_in, gain, gate, up, wgu, wd)


def _ffn_bwd_act(dh_out, gate, up, wd, name):
    n = dh_out.shape[0]
    tm = _row_tile(n, 512)

    def body(dh_ref, gate_ref, up_ref, wd_ref, dgate_ref, dup_ref, dhb_scr):
        @pl.when(pl.program_id(1) == 0)
        def _():
            dhb_scr[...] = (0.5 * dh_ref[...]).astype(BF16)

        da = _dot_nt(dhb_scr[...], wd_ref[0])
        g = gate_ref[0].astype(F32)
        u = up_ref[0].astype(F32)
        sig = _sigmoid(g)
        dgate_ref[0] = (da * u * (sig * (1.0 + g * (1.0 - sig)))).astype(BF16)
        dup_ref[0] = (da * (g * sig)).astype(BF16)

    chunk = pl.BlockSpec((1, tm, F_CHUNK), lambda i, j: (j, i, 0))
    return pl.pallas_call(
        body, name=name, grid=(n // tm, N_CHUNK),
        in_specs=[pl.BlockSpec((tm, D_MODEL), lambda i, j: (i, 0)), chunk, chunk,
                  pl.BlockSpec((1, F_CHUNK, D_MODEL), lambda i, j: (j, 0, 0))],
        out_specs=[chunk, chunk],
        out_shape=[jax.ShapeDtypeStruct((N_CHUNK, n, F_CHUNK), BF16)] * 2,
        scratch_shapes=[pltpu.VMEM((tm, D_MODEL), BF16)],
        compiler_params=_params(("parallel", "arbitrary")),
    )(dh_out, gate, up, wd)


def _ffn_bwd_in(dh_out, h_in, gain, dgate, dup, wgu, name, exchange=()):
    n = h_in.shape[0]
    tm = _row_tile(n, 512)
    n_i = n // tm
    nw = len(exchange)

    def body(*refs):
        (dh_ref, h_ref, g_ref, dgate_ref, dup_ref, wgu_ref), xin, (dhin_ref, dgain_ref), xout, (acc_scr,), sems = \
            _split_refs(refs, 6, nw, 2, 1)
        i = pl.program_id(0)
        j = pl.program_id(1)
        if nw:
            comm = _Exchange(xin, xout, sems)
            pl.when((i == 0) & (j == 0))(comm.start)

        @pl.when((i == 0) & (j == 0))
        def _():
            dgain_ref[...] = jnp.zeros_like(dgain_ref)

        @pl.when(j == 0)
        def _():
            acc_scr[...] = jnp.zeros_like(acc_scr)

        acc_scr[...] += _dot_nt(dgate_ref[0], wgu_ref[0, 0]) + _dot_nt(dup_ref[0], wgu_ref[1, 0])

        @pl.when(j == N_CHUNK - 1)
        def _():
            gain_v = g_ref[...]
            _, xhat, r = _rms(h_ref[...], gain_v)
            dn = acc_scr[...]
            dhin_ref[...] = dh_ref[...] + _rms_bwd(dn, xhat, r, gain_v)
            dgain_ref[...] += jnp.sum(dn * xhat, axis=0, keepdims=True)

        if nw:
            pl.when((i == n_i - 1) & (j == N_CHUNK - 1))(comm.finish)

    chunk = pl.BlockSpec((1, tm, F_CHUNK), lambda i, j: (j, i, 0))
    rows = pl.BlockSpec((tm, D_MODEL), lambda i, j: (i, 0))
    vec = pl.BlockSpec((1, D_MODEL), lambda i, j: (0, 0))
    return pl.pallas_call(
        body, name=name, grid=(n_i, N_CHUNK),
        in_specs=[rows, rows, vec, chunk, chunk,
                  pl.BlockSpec((2, 1, D_MODEL, F_CHUNK), lambda i, j: (0, j, 0, 0))] + [ANY] * nw,
        out_specs=[rows, vec] + [ANY] * nw,
        out_shape=[jax.ShapeDtypeStruct((n, D_MODEL), F32), jax.ShapeDtypeStruct((1, D_MODEL), F32)]
        + [jax.ShapeDtypeStruct(a.shape, a.dtype) for a in exchange],
        scratch_shapes=[pltpu.VMEM((tm, D_MODEL), F32)] + (_comm_sems(nw) if nw else []),
        compiler_params=_params(("arbitrary", "arbitrary")),
    )(dh_out, h_in, gain, dgate, dup, wgu, *exchange)


def _ffn_bwd_w(dh_out, h_in, gain, gate, up, dgate, dup, name):
    n = h_in.shape[0]
    tm = _row_tile(n, 512)
    n_i = n // tm

    def body(dh_ref, h_ref, g_ref, gate_ref, up_ref, dgate_ref, dup_ref, dwgu_ref, dwd_ref,
             ag_scr, au_scr, ad_scr):
        i = pl.program_id(1)

        @pl.when(i == 0)
        def _():
            ag_scr[...] = jnp.zeros_like(ag_scr)
            au_scr[...] = jnp.zeros_like(au_scr)
            ad_scr[...] = jnp.zeros_like(ad_scr)

        y, _, _ = _rms(h_ref[...], g_ref[...])
        nb = y.astype(BF16)
        ag_scr[...] += _dot_tn(nb, dgate_ref[0])
        au_scr[...] += _dot_tn(nb, dup_ref[0])
        g = gate_ref[0].astype(F32)
        act = (g * _sigmoid(g) * up_ref[0].astype(F32)).astype(BF16)
        ad_scr[...] += _dot_tn(act, (0.5 * dh_ref[...]).astype(BF16))

        @pl.when(i == n_i - 1)
        def _():
            dwgu_ref[0, 0] = ag_scr[...].astype(BF16)
            dwgu_ref[1, 0] = au_scr[...].astype(BF16)
            dwd_ref[0] = ad_scr[...].astype(BF16)

    chunk = pl.BlockSpec((1, tm, F_CHUNK), lambda j, i: (j, i, 0))
    rows = pl.BlockSpec((tm, D_MODEL), lambda j, i: (i, 0))
    return pl.pallas_call(
        body, name=name, grid=(N_CHUNK, n_i),
        in_specs=[rows, rows, pl.BlockSpec((1, D_MODEL), lambda j, i: (0, 0)), chunk, chunk, chunk, chunk],
        out_specs=[pl.BlockSpec((2, 1, D_MODEL, F_CHUNK), lambda j, i: (0, j, 0, 0)),
                   pl.BlockSpec((1, F_CHUNK, D_MODEL), lambda j, i: (j, 0, 0))],
        out_shape=[jax.ShapeDtypeStruct((2, N_CHUNK, D_MODEL, F_CHUNK), BF16),
                   jax.ShapeDtypeStruct((N_CHUNK, F_CHUNK, D_MODEL), BF16)],
        scratch_shapes=[pltpu.VMEM((D_MODEL, F_CHUNK), F32), pltpu.VMEM((D_MODEL, F_CHUNK), F32),
                        pltpu.VMEM((F_CHUNK, D_MODEL), F32)],
        compiler_params=_params(("parallel", "arbitrary")),
    )(dh_out, h_in, gain, gate, up, dgate, dup)


def _resident(shape, rank):
    zeros = (0,) * len(shape)
    index_map = (lambda i: zeros) if rank == 1 else (lambda i, j: zeros)
    return pl.BlockSpec(shape, index_map, pipeline_mode=pl.Buffered(1))


W_GU_SHAPE = (2, N_CHUNK, D_MODEL, F_CHUNK)
W_D_SHAPE = (N_CHUNK, F_CHUNK, D_MODEL)


def _ffn_fwd(h, gain, wgu, wd, name, gather=()):
    n = h.shape[0]
    tm = _row_tile(n, 512)
    n_i = n // tm
    nw = len(gather)

    def body(*refs):
        (h_ref, g_ref, wgu_ref, wd_ref), gin, (out_ref, nrm_ref, gate_ref, up_ref), gout, _, sems = \
            _split_refs(refs, 4, nw, 4, 0)
        i = pl.program_id(0)
        if nw:
            comm = _Gather(gin, gout, sems)
            pl.when(i == 0)(comm.start)
            pl.when(i == (3 * n_i) // 4)(comm.forward)

        hv = h_ref[...]
        y, _, _ = _rms(hv, g_ref[...])
        nb = y.astype(BF16)
        nrm_ref[...] = nb
        acc = jnp.zeros((tm, D_MODEL), F32)
        for j in range(N_CHUNK):
            gate = _dot(nb, wgu_ref[0, j])
            up = _dot(nb, wgu_ref[1, j])
            gate_ref[j] = gate.astype(BF16)
            up_ref[j] = up.astype(BF16)
            acc = acc + _dot((gate * _sigmoid(gate) * up).astype(BF16), wd_ref[j])
        out_ref[...] = hv + 0.5 * acc

        if nw:
            pl.when(i == n_i - 1)(comm.finish)

    rows = pl.BlockSpec((tm, D_MODEL), lambda i: (i, 0))
    chunks = pl.BlockSpec((N_CHUNK, tm, F_CHUNK), lambda i: (0, i, 0))
    return pl.pallas_call(
        body, name=name, grid=(n_i,),
        in_specs=[rows, pl.BlockSpec((1, D_MODEL), lambda i: (0, 0)), _resident(W_GU_SHAPE, 1),
                  _resident(W_D_SHAPE, 1)] + [ANY] * nw,
        out_specs=[rows, rows, chunks, chunks] + [ANY] * nw,
        out_shape=[jax.ShapeDtypeStruct((n, D_MODEL), F32), jax.ShapeDtypeStruct((n, D_MODEL), BF16),
                   jax.ShapeDtypeStruct((N_CHUNK, n, F_CHUNK), BF16),
                   jax.ShapeDtypeStruct((N_CHUNK, n, F_CHUNK), BF16)]
        + [jax.ShapeDtypeStruct((N_DEV,) + a.shape, a.dtype) for a in gather],
        scratch_shapes=_comm_sems(nw) if nw else [],
        compiler_params=_params(("arbitrary",)),
    )(h, gain, wgu, wd, *gather)


def _swiglu_bwd(da, gate_ref, up_ref, j):
    g = gate_ref[j].astype(F32)
    u = up_ref[j].astype(F32)
    sig = _sigmoid(g)
    return (da * u * (sig * (1.0 + g * (1.0 - sig)))).astype(BF16), (da * (g * sig)).astype(BF16)


def _ffn_bwd_x(dh_out, h_in, gain, gate, up, wgu, wd, name):
    n = h_in.shape[0]
    tm = _row_tile(n, 256)

    def body(dh_ref, h_ref, g_ref, gate_ref, up_ref, wgu_ref, wd_ref,
             dhin_ref, dhb_ref, dgate_ref, dup_ref, dgain_ref):
        @pl.when(pl.program_id(0) == 0)
        def _():
            dgain_ref[...] = jnp.zeros_like(dgain_ref)

        dhv = dh_ref[...]
        dhb = (0.5 * dhv).astype(BF16)
        dhb_ref[...] = dhb
        dn = jnp.zeros((tm, D_MODEL), F32)
        for j in range(N_CHUNK):
            dgate, dup = _swiglu_bwd(_dot_nt(dhb, wd_ref[j]), gate_ref, up_ref, j)
            dgate_ref[j] = dgate
            dup_ref[j] = dup
            dn = dn + _dot_nt(dgate, wgu_ref[0, j]) + _dot_nt(dup, wgu_ref[1, j])
        gain_v = g_ref[...]
        _, xhat, r = _rms(h_ref[...], gain_v)
        dhin_ref[...] = dhv + _rms_bwd(dn, xhat, r, gain_v)
        dgain_ref[...] += jnp.sum(dn * xhat, axis=0, keepdims=True)

    rows = pl.BlockSpec((tm, D_MODEL), lambda i: (i, 0))
    chunks = pl.BlockSpec((N_CHUNK, tm, F_CHUNK), lambda i: (0, i, 0))
    vec = pl.BlockSpec((1, D_MODEL), lambda i: (0, 0))
    return pl.pallas_call(
        body, name=name, grid=(n // tm,),
        in_specs=[rows, rows, vec, chunks, chunks, _resident(W_GU_SHAPE, 1), _resident(W_D_SHAPE, 1)],
        out_specs=[rows, rows, chunks, chunks, vec],
        out_shape=[jax.ShapeDtypeStruct((n, D_MODEL), F32), jax.ShapeDtypeStruct((n, D_MODEL), BF16),
                   jax.ShapeDtypeStruct((N_CHUNK, n, F_CHUNK), BF16),
                   jax.ShapeDtypeStruct((N_CHUNK, n, F_CHUNK), BF16),
                   jax.ShapeDtypeStruct((1, D_MODEL), F32)],
        compiler_params=_params(("arbitrary",)),
    )(dh_out, h_in, gain, gate, up, wgu, wd)


def _ffn_bwd_act(dh_out, gate, up, wd, name):
    n = dh_out.shape[0]
    tm = _row_tile(n, 512)

    def body(dh_ref, gate_ref, up_ref, wd_ref, dhb_ref, dgate_ref, dup_ref):
        dhb = (0.5 * dh_ref[...]).astype(BF16)
        dhb_ref[...] = dhb
        for j in range(N_CHUNK):
            dgate_ref[j], dup_ref[j] = _swiglu_bwd(_dot_nt(dhb, wd_ref[j]), gate_ref, up_ref, j)

    rows = pl.BlockSpec((tm, D_MODEL), lambda i: (i, 0))
    chunks = pl.BlockSpec((N_CHUNK, tm, F_CHUNK), lambda i: (0, i, 0))
    return pl.pallas_call(
        body, name=name, grid=(n // tm,),
        in_specs=[rows, chunks, chunks, _resident(W_D_SHAPE, 1)],
        out_specs=[rows, chunks, chunks],
        out_shape=[jax.ShapeDtypeStruct((n, D_MODEL), BF16)] + [jax.ShapeDtypeStruct((N_CHUNK, n, F_CHUNK), BF16)] * 2,
        compiler_params=_params(("parallel",)),
    )(dh_out, gate, up, wd)


def _ffn_bwd_in(dh_out, h_in, gain, dgate, dup, wgu, name, exchange=()):
    n = h_in.shape[0]
    tm = _row_tile(n, 512)
    n_i = n // tm
    nw = len(exchange)

    def body(*refs):
        (dh_ref, h_ref, g_ref, dgate_ref, dup_ref, wgu_ref), xin, (dhin_ref, dgain_ref), xout, _, sems = \
            _split_refs(refs, 6, nw, 2, 0)
        i = pl.program_id(0)
        if nw:
            comm = _Exchange(xin, xout, sems)
            pl.when(i == 0)(comm.start)

        @pl.when(i == 0)
        def _():
            dgain_ref[...] = jnp.zeros_like(dgain_ref)

        dn = jnp.zeros((tm, D_MODEL), F32)
        for j in range(N_CHUNK):
            dn = dn + _dot_nt(dgate_ref[j], wgu_ref[0, j]) + _dot_nt(dup_ref[j], wgu_ref[1, j])
        gain_v = g_ref[...]
        _, xhat, r = _rms(h_ref[...], gain_v)
        dhin_ref[...] = dh_ref[...] + _rms_bwd(dn, xhat, r, gain_v)
        dgain_ref[...] += jnp.sum(dn * xhat, axis=0, keepdims=True)

        if nw:
            pl.when(i == n_i - 1)(comm.finish)

    rows = pl.BlockSpec((tm, D_MODEL), lambda i: (i, 0))
    chunks = pl.BlockSpec((N_CHUNK, tm, F_CHUNK), lambda i: (0, i, 0))
    vec = pl.BlockSpec((1, D_MODEL), lambda i: (0, 0))
    return pl.pallas_call(
        body, name=name, grid=(n_i,),
        in_specs=[rows, rows, vec, chunks, chunks, _resident(W_GU_SHAPE, 1)] + [ANY] * nw,
        out_specs=[rows, vec] + [ANY] * nw,
        out_shape=[jax.ShapeDtypeStruct((n, D_MODEL), F32), jax.ShapeDtypeStruct((1, D_MODEL), F32)]
        + [jax.ShapeDtypeStruct(a.shape, a.dtype) for a in exchange],
        scratch_shapes=_comm_sems(nw) if nw else [],
        compiler_params=_params(("arbitrary",)),
    )(dh_out, h_in, gain, dgate, dup, wgu, *exchange)


W_GROUP = 2


def _ffn_bwd_w(dhb, nrm, gate, up, dgate, dup, name):
    n = nrm.shape[0]
    tm = _row_tile(n, 512)
    n_i = n // tm

    def body(dhb_ref, nrm_ref, gate_ref, up_ref, dgate_ref, dup_ref, dwgu_ref, dwd_ref, ag_scr, au_scr, ad_scr):
        i = pl.program_id(1)

        @pl.when(i == 0)
        def _():
            ag_scr[...] = jnp.zeros_like(ag_scr)
            au_scr[...] = jnp.zeros_like(au_scr)
            ad_scr[...] = jnp.zeros_like(ad_scr)

        nb = nrm_ref[...]
        dhv = dhb_ref[...]
        for jj in range(W_GROUP):
            ag_scr[jj] += _dot_tn(nb, dgate_ref[jj])
            au_scr[jj] += _dot_tn(nb, dup_ref[jj])
            g = gate_ref[jj].astype(F32)
            act = (g * _sigmoid(g) * up_ref[jj].astype(F32)).astype(BF16)
            ad_scr[jj] += _dot_tn(act, dhv)

        @pl.when(i == n_i - 1)
        def _():
            dwgu_ref[0] = ag_scr[...].astype(BF16)
            dwgu_ref[1] = au_scr[...].astype(BF16)
            dwd_ref[...] = ad_scr[...].astype(BF16)

    chunks = pl.BlockSpec((W_GROUP, tm, F_CHUNK), lambda g, i: (g, i, 0))
    rows = pl.BlockSpec((tm, D_MODEL), lambda g, i: (i, 0))
    return pl.pallas_call(
        body, name=name, grid=(N_CHUNK // W_GROUP, n_i),
        in_specs=[rows, rows, chunks, chunks, chunks, chunks],
        out_specs=[pl.BlockSpec((2, W_GROUP, D_MODEL, F_CHUNK), lambda g, i: (0, g, 0, 0)),
                   pl.BlockSpec((W_GROUP, F_CHUNK, D_MODEL), lambda g, i: (g, 0, 0))],
        out_shape=[jax.ShapeDtypeStruct(W_GU_SHAPE, BF16), jax.ShapeDtypeStruct(W_D_SHAPE, BF16)],
        scratch_shapes=[pltpu.VMEM((W_GROUP, D_MODEL, F_CHUNK), F32), pltpu.VMEM((W_GROUP, D_MODEL, F_CHUNK), F32),
                        pltpu.VMEM((W_GROUP, F_CHUNK, D_MODEL), F32)],
        compiler_params=_params(("parallel", "arbitrary")),
    )(dhb, nrm, gate, up, dgate, dup)


N_PIECE = IN_MAIN // 512


def _inproj_fwd(h, gain, w_in):
    n = h.shape[0]
    tm = _row_tile(n, 512)

    def body(h_ref, g_ref, w_ref, *outs):
        y, _, _ = _rms(h_ref[...], g_ref[...])
        nb = y.astype(BF16)
        for p in range(N_PIECE):
            outs[p][...] = _dot(nb, w_ref[:, 512 * p:512 * (p + 1)]).astype(BF16)
        outs[N_PIECE][...] = _dot(nb, w_ref[:, IN_MAIN:IN_PAD])

    piece = pl.BlockSpec((tm, 512), lambda i: (i, 0))
    return pl.pallas_call(
        body, name="inproj_fwd", grid=(n // tm,),
        in_specs=[pl.BlockSpec((tm, D_MODEL), lambda i: (i, 0)),
                  pl.BlockSpec((1, D_MODEL), lambda i: (0, 0)),
                  pl.BlockSpec((D_MODEL, IN_PAD), lambda i: (0, 0))],
        out_specs=[piece] * N_PIECE + [pl.BlockSpec((tm, 128), lambda i: (i, 0))],
        out_shape=[jax.ShapeDtypeStruct((n, 512), BF16)] * N_PIECE + [jax.ShapeDtypeStruct((n, 128), F32)],
        compiler_params=_params(("parallel",)),
    )(h, gain, w_in)


def _inproj_bwd(dpieces, dfg, dh_out, h_in, gain, w_in):
    n = h_in.shape[0]
    tm = _row_tile(n, 512)
    n_i = n // tm

    def body(*refs):
        dp_refs = refs[:N_PIECE]
        dfg_ref, dh_ref, h_ref, g_ref, w_ref, dhin_ref, dw_ref, dgain_ref, acc_scr = refs[N_PIECE:]
        i = pl.program_id(0)

        @pl.when(i == 0)
        def _():
            acc_scr[...] = jnp.zeros_like(acc_scr)
            dgain_ref[...] = jnp.zeros_like(dgain_ref)

        gain_v = g_ref[...]
        y, xhat, r = _rms(h_ref[...], gain_v)
        nb = y.astype(BF16)
        dn = jnp.zeros((tm, D_MODEL), F32)
        for p in range(N_PIECE + 1):
            lo, hi = (512 * p, 512 * (p + 1)) if p < N_PIECE else (IN_MAIN, IN_PAD)
            dp = (dp_refs[p][...] if p < N_PIECE else dfg_ref[...]).astype(BF16)
            dn = dn + _dot_nt(dp, w_ref[:, lo:hi])
            acc_scr[:, lo:hi] += _dot_tn(nb, dp)
        dhin_ref[...] = dh_ref[...] + _rms_bwd(dn, xhat, r, gain_v)
        dgain_ref[...] += jnp.sum(dn * xhat, axis=0, keepdims=True)

        @pl.when(i == n_i - 1)
        def _():
            dw_ref[...] = acc_scr[...].astype(BF16)

    piece = pl.BlockSpec((tm, 512), lambda i: (i, 0))
    rows = pl.BlockSpec((tm, D_MODEL), lambda i: (i, 0))
    vec = pl.BlockSpec((1, D_MODEL), lambda i: (0, 0))
    wspec = pl.BlockSpec((D_MODEL, IN_PAD), lambda i: (0, 0))
    return pl.pallas_call(
        body, name="inproj_bwd", grid=(n_i,),
        in_specs=[piece] * N_PIECE + [pl.BlockSpec((tm, 128), lambda i: (i, 0)), rows, rows, vec, wspec],
        out_specs=[rows, wspec, vec],
        out_shape=[jax.ShapeDtypeStruct((n, D_MODEL), F32),
                   jax.ShapeDtypeStruct((D_MODEL, IN_PAD), BF16),
                   jax.ShapeDtypeStruct((1, D_MODEL), F32)],
        scratch_shapes=[pltpu.VMEM((D_MODEL, IN_PAD), F32)],
        compiler_params=_params(("arbitrary",)),
    )(*dpieces, dfg, dh_out, h_in, gain, w_in)


def _outproj_fwd(zc, za, w_out, h):
    n = h.shape[0]
    tm = _row_tile(n, 512)

    def body(zc_ref, za_ref, w_ref, h_ref, out_ref):
        out_ref[...] = (h_ref[...] + _dot(zc_ref[...], w_ref[0:CONV_DIM, :])
                        + _dot(za_ref[...], w_ref[CONV_DIM:, :]))

    half = pl.BlockSpec((tm, 512), lambda i: (i, 0))
    rows = pl.BlockSpec((tm, D_MODEL), lambda i: (i, 0))
    return pl.pallas_call(
        body, name="outproj_fwd", grid=(n // tm,),
        in_specs=[half, half, pl.BlockSpec((D_MODEL, D_MODEL), lambda i: (0, 0)), rows],
        out_specs=rows,
        out_shape=jax.ShapeDtypeStruct((n, D_MODEL), F32),
        compiler_params=_params(("parallel",)),
    )(zc, za, w_out, h)


def _outproj_bwd(dh, zc, za, w_out):
    n = dh.shape[0]
    tm = _row_tile(n, 512)
    n_i = n // tm

    def body(dh_ref, zc_ref, za_ref, w_ref, dzc_ref, dza_ref, dw_ref, acc_scr):
        i = pl.program_id(0)

        @pl.when(i == 0)
        def _():
            acc_scr[...] = jnp.zeros_like(acc_scr)

        dhb = dh_ref[...].astype(BF16)
        dzc_ref[...] = _dot_nt(dhb, w_ref[0:CONV_DIM, :]).astype(BF16)
        dza_ref[...] = _dot_nt(dhb, w_ref[CONV_DIM:, :]).astype(BF16)
        acc_scr[0:CONV_DIM, :] += _dot_tn(zc_ref[...], dhb)
        acc_scr[CONV_DIM:, :] += _dot_tn(za_ref[...], dhb)

        @pl.when(i == n_i - 1)
        def _():
            dw_ref[...] = acc_scr[...].astype(BF16)

    half = pl.BlockSpec((tm, 512), lambda i: (i, 0))
    wspec = pl.BlockSpec((D_MODEL, D_MODEL), lambda i: (0, 0))
    return pl.pallas_call(
        body, name="outproj_bwd", grid=(n_i,),
        in_specs=[pl.BlockSpec((tm, D_MODEL), lambda i: (i, 0)), half, half, wspec],
        out_specs=[half, half, wspec],
        out_shape=[jax.ShapeDtypeStruct((n, 512), BF16), jax.ShapeDtypeStruct((n, 512), BF16),
                   jax.ShapeDtypeStruct((D_MODEL, D_MODEL), BF16)],
        scratch_shapes=[pltpu.VMEM((D_MODEL, D_MODEL), F32)],
        compiler_params=_params(("arbitrary",)),
    )(dh, zc, za, w_out)


def _group_matrix():
    r = lax.broadcasted_iota(jnp.int32, (128, 128), 0) // HEAD_DIM
    c = lax.broadcasted_iota(jnp.int32, (128, 128), 1) // HEAD_DIM
    return jnp.where(r == c, 1.0 / HEAD_DIM, 0.0).astype(BF16)


def _group_mean(x, gmat):
    hi = x.astype(BF16)
    lo = (x - hi.astype(F32)).astype(BF16)
    return _dot(hi, gmat) + _dot(lo, gmat)


def _shift_rows(x, s):
    rows = x.shape[0]
    t = lax.broadcasted_iota(jnp.int32, x.shape, 0)
    rolled = pltpu.roll(x, s % rows, 0)
    keep = (t >= s) if s > 0 else (t < rows + s)
    return jnp.where(keep, rolled, 0.0)


def _conv_parts(bg_ref, cg_ref, hc_ref, w_ref):
    bg = bg_ref[...].astype(F32)
    cg = cg_ref[...].astype(F32)
    hc = hc_ref[...].astype(F32)
    u = cg * hc
    u1 = _shift_rows(u, 1)
    u2 = _shift_rows(u, 2)
    conv = w_ref[2:3, :] * u + w_ref[1:2, :] * u1 + w_ref[0:1, :] * u2
    return bg, cg, hc, u, u1, u2, conv


def _conv_fwd(bg, cg, hc, conv_w, gain, gmat, lp):
    n = bg.shape[0]
    nb = n // lp

    def body(bg_ref, cg_ref, hc_ref, w_ref, g_ref, gm_ref, z_ref):
        bgv, _, _, _, _, _, conv = _conv_parts(bg_ref, cg_ref, hc_ref, w_ref)
        yc = bgv * conv
        r = lax.rsqrt(_group_mean(yc * yc, gm_ref[...]) + EPS)
        z_ref[...] = (yc * r * g_ref[...]).astype(BF16)

    blk = pl.BlockSpec((lp, 128), lambda c, b: (b, c))
    return pl.pallas_call(
        body, name="conv_fwd", grid=(CONV_DIM // 128, nb),
        in_specs=[blk, blk, blk, pl.BlockSpec((3, 128), lambda c, b: (0, c)),
                  pl.BlockSpec((1, 128), lambda c, b: (0, c)), pl.BlockSpec((128, 128), lambda c, b: (0, 0))],
        out_specs=blk,
        out_shape=jax.ShapeDtypeStruct((n, CONV_DIM), BF16),
        compiler_params=_params(("parallel", "parallel")),
    )(bg, cg, hc, conv_w, gain, gmat)


def _conv_bwd(dz, bg, cg, hc, conv_w, gain, gmat, lp):
    n = bg.shape[0]
    nb = n // lp

    def body(dz_ref, bg_ref, cg_ref, hc_ref, w_ref, g_ref, gm_ref,
             dbg_ref, dcg_ref, dhc_ref, dw_ref, dgain_ref):
        b = pl.program_id(1)

        @pl.when(b == 0)
        def _():
            dw_ref[...] = jnp.zeros_like(dw_ref)
            dgain_ref[...] = jnp.zeros_like(dgain_ref)

        bgv, cgv, hcv, u, u1, u2, conv = _conv_parts(bg_ref, cg_ref, hc_ref, w_ref)
        gm = gm_ref[...]
        yc = bgv * conv
        r = lax.rsqrt(_group_mean(yc * yc, gm) + EPS)
        yhat = yc * r
        dzv = dz_ref[...].astype(F32)
        dyhat = dzv * g_ref[...]
        dgain_ref[...] += jnp.sum(dzv * yhat, axis=0, keepdims=True)
        dyc = r * (dyhat - yhat * _group_mean(dyhat * yhat, gm))
        dbg_ref[...] = (dyc * conv).astype(BF16)
        dconv = dyc * bgv
        du = (w_ref[2:3, :] * dconv + w_ref[1:2, :] * _shift_rows(dconv, -1)
              + w_ref[0:1, :] * _shift_rows(dconv, -2))
        dcg_ref[...] = (du * hcv).astype(BF16)
        dhc_ref[...] = (du * cgv).astype(BF16)
        dw_ref[0:1, :] += jnp.sum(dconv * u2, axis=0, keepdims=True)
        dw_ref[1:2, :] += jnp.sum(dconv * u1, axis=0, keepdims=True)
        dw_ref[2:3, :] += jnp.sum(dconv * u, axis=0, keepdims=True)

    blk = pl.BlockSpec((lp, 128), lambda c, b: (b, c))
    wspec = pl.BlockSpec((3, 128), lambda c, b: (0, c))
    gspec = pl.BlockSpec((1, 128), lambda c, b: (0, c))
    return pl.pallas_call(
        body, name="conv_bwd", grid=(CONV_DIM // 128, nb),
        in_specs=[blk, blk, blk, blk, wspec, gspec, pl.BlockSpec((128, 128), lambda c, b: (0, 0))],
        out_specs=[blk, blk, blk, wspec, gspec],
        out_shape=[jax.ShapeDtypeStruct((n, CONV_DIM), BF16)] * 3
        + [jax.ShapeDtypeStruct((3, CONV_DIM), F32), jax.ShapeDtypeStruct((1, CONV_DIM), F32)],
        compiler_params=_params(("parallel", "arbitrary")),
    )(dz, bg, cg, hc, conv_w, gain, gmat)


KEY_MASKED = 1e30
ONE_LANE = 24


def _scan_steps(rows):
    s, out = 1, []
    while s < rows:
        out.append(s)
        s *= 2
    return out


def _fgate_fwd(fg, b_f, lp):
    n = fg.shape[0]
    nb = n // lp

    def body(fg_ref, b_ref, ka_ref, qa_ref):
        x = fg_ref[...] + b_ref[...]
        logf = jnp.minimum(x, 0.0) - jnp.log(1.0 + jnp.exp(-jnp.abs(x)))
        t = lax.broadcasted_iota(jnp.int32, (lp, 128), 0)
        lane = lax.broadcasted_iota(jnp.int32, (lp, 128), 1)
        f = jnp.where((t >= PAD) & (lane < N_HEADS), logf, 0.0)
        for s in _scan_steps(lp):
            f = f + _shift_rows(f, s)
        hi = f.astype(BF16).astype(F32)
        rest = f - hi
        mid = rest.astype(BF16).astype(F32)
        lo = (rest - mid).astype(BF16).astype(F32)
        ones = jnp.where((lane >= ONE_LANE) & (lane < ONE_LANE + 3), 1.0, 0.0)
        hi_key = jnp.where((t < PAD) & (lane < N_HEADS), KEY_MASKED, hi)
        ka_ref[...] = (hi_key + pltpu.roll(mid, 8, 1) + pltpu.roll(lo, 16, 1) + ones).astype(BF16)
        for h in range(N_HEADS):
            minus = jnp.where((lane == h) | (lane == 8 + h) | (lane == 16 + h), -1.0, 0.0)
            terms = (jnp.where(lane == ONE_LANE, pltpu.roll(hi, ONE_LANE - h, 1), 0.0)
                     + jnp.where(lane == ONE_LANE + 1, pltpu.roll(mid, ONE_LANE + 1 - h, 1), 0.0)
                     + jnp.where(lane == ONE_LANE + 2, pltpu.roll(lo, ONE_LANE + 2 - h, 1), 0.0))
            qa_ref[:, 128 * h:128 * (h + 1)] = (minus + terms).astype(BF16)

    return pl.pallas_call(
        body, name="fgate_fwd", grid=(nb,),
        in_specs=[pl.BlockSpec((lp, 128), lambda b: (b, 0)), pl.BlockSpec((1, 128), lambda b: (0, 0))],
        out_specs=[pl.BlockSpec((lp, 128), lambda b: (b, 0)), pl.BlockSpec((lp, N_HEADS * 128), lambda b: (b, 0))],
        out_shape=[jax.ShapeDtypeStruct((n, 128), BF16), jax.ShapeDtypeStruct((n, N_HEADS * 128), BF16)],
        compiler_params=_params(("parallel",)),
    )(fg, b_f)


def _fgate_bwd(dka, dfr, fg, b_f, lp):
    n = fg.shape[0]
    nb = n // lp

    def body(dka_ref, dfr_ref, fg_ref, b_ref, dfg_ref, db_ref):
        b = pl.program_id(0)

        @pl.when(b == 0)
        def _():
            db_ref[...] = jnp.zeros_like(db_ref)

        wide = jnp.concatenate([dfr_ref[0], jnp.zeros((128 - N_HEADS, lp), F32)], axis=0)
        t = lax.broadcasted_iota(jnp.int32, (lp, 128), 0)
        lane = lax.broadcasted_iota(jnp.int32, (lp, 128), 1)
        d = jnp.where(lane < N_HEADS, dka_ref[...], 0.0) + wide.T
        for s in _scan_steps(lp):
            d = d + _shift_rows(d, -s)
        x = fg_ref[...] + b_ref[...]
        dx = jnp.where((t >= PAD) & (lane < N_HEADS), d * _sigmoid(-x), 0.0)
        dfg_ref[...] = dx
        db_ref[...] += jnp.sum(dx, axis=0, keepdims=True)

    return pl.pallas_call(
        body, name="fgate_bwd", grid=(nb,),
        in_specs=[pl.BlockSpec((lp, 128), lambda b: (b, 0)), pl.BlockSpec((1, N_HEADS, lp), lambda b: (b, 0, 0)),
                  pl.BlockSpec((lp, 128), lambda b: (b, 0)), pl.BlockSpec((1, 128), lambda b: (0, 0))],
        out_specs=[pl.BlockSpec((lp, 128), lambda b: (b, 0)), pl.BlockSpec((1, 128), lambda b: (0, 0))],
        out_shape=[jax.ShapeDtypeStruct((n, 128), F32), jax.ShapeDtypeStruct((1, 128), F32)],
        compiler_params=_params(("arbitrary",)),
    )(dka, dfr, fg, b_f)


def _head_masks():
    lane = lax.broadcasted_iota(jnp.int32, (1, 128), 1)
    return lane < HEAD_DIM


def _stack_heads(x2, first):
    zero = jnp.zeros_like(x2)
    return jnp.concatenate([jnp.where(first, x2, zero), jnp.where(first, zero, x2)], axis=0)


def _stack_heads_lanes(xt):
    r = lax.broadcasted_iota(jnp.int32, xt.shape, 0)
    zero = jnp.zeros_like(xt)
    return jnp.concatenate([jnp.where(r < HEAD_DIM, xt, zero), jnp.where(r < HEAD_DIM, zero, xt)], axis=1)


def _pair_cols(col0, col1, first):
    return jnp.where(first, col0, col1)


def _pair_rows(row0, row1):
    r = lax.broadcasted_iota(jnp.int32, (128, TQ), 0)
    return jnp.where(r < HEAD_DIM, row0, row1)


def _query_side(q_ref, qa_ref, p, first):
    q2 = q_ref[:, 128 * p:128 * (p + 1)] * 0.125
    zero = jnp.zeros_like(q2)
    top = jnp.concatenate([jnp.where(first, q2, zero), qa_ref[:, 128 * (2 * p):128 * (2 * p + 1)]], axis=1)
    bot = jnp.concatenate([jnp.where(first, zero, q2), qa_ref[:, 128 * (2 * p + 1):128 * (2 * p + 2)]], axis=1)
    return jnp.concatenate([top, bot], axis=0)


def _key_chunks(lp):
    return (lp + TK - 1) // TK


def _chunk_mask(i, c):
    r = lax.broadcasted_iota(jnp.int32, (TK, 2 * TQ), 0)
    col = lax.broadcasted_iota(jnp.int32, (TK, 2 * TQ), 1)
    return (c * TK + r) <= (i * TQ + (col & (TQ - 1)))


def _transpose_bf16(x):
    return x.astype(F32).T.astype(BF16)


def _attn_fwd(q, qa, k, v, ka, gain, lp):
    n = q.shape[0]
    nb = n // lp
    nq = lp // TQ
    lpp = _key_chunks(lp) * TK

    def body(q_ref, qa_ref, k_ref, v_ref, ka_ref, g_ref, z_ref, o_ref, lse_ref, kx_scr, vt_scr):
        i = pl.program_id(1)
        first = _head_masks()

        @pl.when(i == 0)
        def _():
            if lpp > lp:
                kx_scr[lp:lpp, :] = jnp.zeros((lpp - lp, 2 * ATTN_DIM), BF16)
                vt_scr[:, lp:lpp] = jnp.zeros((ATTN_DIM, lpp - lp), BF16)
            for p in range(N_PAIRS):
                kx_scr[0:lp, 256 * p:256 * p + 128] = k_ref[:, 128 * p:128 * (p + 1)]
                kx_scr[0:lp, 256 * p + 128:256 * (p + 1)] = ka_ref[...]
            vt_scr[:, 0:lp] = _transpose_bf16(v_ref[...])

        rhs_t = [_transpose_bf16(_query_side(q_ref, qa_ref, p, first)) for p in range(N_PAIRS)]

        def step(c, carry):
            koff = pl.multiple_of(c * TK, TK)
            valid = _chunk_mask(i, c)
            new = []
            for p in range(N_PAIRS):
                m, l, acc = carry[p]
                st = _dot(kx_scr[pl.ds(koff, TK), 256 * p:256 * (p + 1)], rhs_t[p])
                st = jnp.where(valid, st, NEG)
                m_new = jnp.maximum(m, jnp.max(st, axis=0, keepdims=True))
                pt = jnp.exp(st - m_new)
                alpha = jnp.exp(m - m_new)
                l = alpha * l + jnp.sum(pt, axis=0, keepdims=True)
                pb = pt.astype(BF16)
                vt = _stack_heads_lanes(vt_scr[128 * p:128 * (p + 1), pl.ds(koff, TK)])
                pv = _dot(vt, jnp.concatenate([pb[:, 0:TQ], pb[:, TQ:]], axis=0))
                acc = acc * _pair_rows(alpha[:, 0:TQ], alpha[:, TQ:]) + pv
                new.append((m_new, l, acc))
            return tuple(new)

        init = tuple((jnp.full((1, 2 * TQ), NEG, F32), jnp.zeros((1, 2 * TQ), F32), jnp.zeros((128, TQ), F32))
                     for _ in range(N_PAIRS))
        final = lax.fori_loop(0, (i + 2) // 2, step, init)

        row = lax.broadcasted_iota(jnp.int32, (TQ, 128), 0)
        real = (i * TQ + row) >= PAD
        for p in range(N_PAIRS):
            m, l, acc = final[p]
            inv = 1.0 / l
            ot = acc * _pair_rows(inv[:, 0:TQ], inv[:, TQ:])
            sq = ot * ot
            r0 = lax.rsqrt(jnp.sum(sq[0:HEAD_DIM], axis=0, keepdims=True) * (1.0 / HEAD_DIM) + EPS)
            r1 = lax.rsqrt(jnp.sum(sq[HEAD_DIM:], axis=0, keepdims=True) * (1.0 / HEAD_DIM) + EPS)
            cols = slice(128 * p, 128 * (p + 1))
            o_ref[:, cols] = jnp.where(real, ot.T, 0.0).astype(BF16)
            z_ref[:, cols] = (jnp.where(real, (ot * _pair_rows(r0, r1)).T, 0.0) * g_ref[:, cols]).astype(BF16)
            lse = m + jnp.log(l)
            lse_ref[0, 2 * p:2 * p + 1, :] = lse[:, 0:TQ]
            lse_ref[0, 2 * p + 1:2 * p + 2, :] = lse[:, TQ:]

    qblk = pl.BlockSpec((TQ, ATTN_DIM), lambda b, i: (b * nq + i, 0))
    qablk = pl.BlockSpec((TQ, N_HEADS * 128), lambda b, i: (b * nq + i, 0))
    seq = pl.BlockSpec((lp, ATTN_DIM), lambda b, i: (b, 0))
    rowblk = pl.BlockSpec((1, N_HEADS, TQ), lambda b, i: (b, 0, i))
    return pl.pallas_call(
        body, name="attn_fwd", grid=(nb, nq),
        in_specs=[qblk, qablk, seq, seq, pl.BlockSpec((lp, 128), lambda b, i: (b, 0)),
                  pl.BlockSpec((1, ATTN_DIM), lambda b, i: (0, 0))],
        out_specs=[qblk, qblk, rowblk],
        out_shape=[jax.ShapeDtypeStruct((n, ATTN_DIM), BF16), jax.ShapeDtypeStruct((n, ATTN_DIM), BF16),
                   jax.ShapeDtypeStruct((nb, N_HEADS, lp), F32)],
        scratch_shapes=[pltpu.VMEM((lpp, 2 * ATTN_DIM), BF16), pltpu.VMEM((ATTN_DIM, lpp), BF16)],
        compiler_params=_params(("parallel", "arbitrary")),
    )(q, qa, k, v, ka, gain)


def _attn_bwd(dz, q, qa, k, v, ka, o, lse, gain, lp, exchange=()):
    n = q.shape[0]
    nb = n // lp
    nq = lp // TQ
    lpp = _key_chunks(lp) * TK
    nw = len(exchange)

    def body(*refs):
        ((dz_ref, q_ref, qa_ref, k_ref, v_ref, ka_ref, o_ref, lse_ref, g_ref), xin,
         (dq_ref, dk_ref, dv_ref, dka_ref, dfr_ref, dgain_ref), xout,
         (kx_scr, vx_scr, kt_scr, dkx_scr, dvx_scr), sems) = _split_refs(refs, 9, nw, 6, 5)
        b = pl.program_id(0)
        i = pl.program_id(1)
        first = _head_masks()
        if nw:
            comm = _Exchange(xin, xout, sems)
            pl.when((b == 0) & (i == 0))(comm.start)

        @pl.when((b == 0) & (i == 0))
        def _():
            dgain_ref[...] = jnp.zeros_like(dgain_ref)

        @pl.when(i == 0)
        def _():
            if lpp > lp:
                kx_scr[lp:lpp, :] = jnp.zeros((lpp - lp, 2 * ATTN_DIM), BF16)
                vx_scr[lp:lpp, :] = jnp.zeros((lpp - lp, ATTN_DIM), BF16)
                kt_scr[:, lp:lpp] = jnp.zeros((ATTN_DIM, lpp - lp), BF16)
            for p in range(N_PAIRS):
                kx_scr[0:lp, 256 * p:256 * p + 128] = k_ref[:, 128 * p:128 * (p + 1)]
                kx_scr[0:lp, 256 * p + 128:256 * (p + 1)] = ka_ref[...]
            vx_scr[0:lp, :] = v_ref[...]
            kt_scr[:, 0:lp] = _transpose_bf16(k_ref[...])
            dkx_scr[...] = jnp.zeros_like(dkx_scr)
            dvx_scr[...] = jnp.zeros_like(dvx_scr)

        rhs, rhs_t, lses, dos, dos_t, deltas = [], [], [], [], [], []
        for p in range(N_PAIRS):
            cols = slice(128 * p, 128 * (p + 1))
            side = _query_side(q_ref, qa_ref, p, first)
            rhs.append(side)
            rhs_t.append(_transpose_bf16(side))
            lses.append(jnp.concatenate([lse_ref[0, 2 * p:2 * p + 1, :], lse_ref[0, 2 * p + 1:2 * p + 2, :]], axis=1))
            ov = o_ref[:, cols].astype(F32)
            dzv = dz_ref[:, cols].astype(F32)
            gv = g_ref[:, cols]
            sq = ov * ov
            ms0 = jnp.sum(jnp.where(first, sq, 0.0), axis=1, keepdims=True) * (1.0 / HEAD_DIM)
            ms1 = jnp.sum(jnp.where(first, 0.0, sq), axis=1, keepdims=True) * (1.0 / HEAD_DIM)
            r = _pair_cols(lax.rsqrt(ms0 + EPS), lax.rsqrt(ms1 + EPS), first)
            ohat = ov * r
            dyhat = dzv * gv
            dgain_ref[:, cols] += jnp.sum(dzv * ohat, axis=0, keepdims=True)
            pr = dyhat * ohat
            mean0 = jnp.sum(jnp.where(first, pr, 0.0), axis=1, keepdims=True) * (1.0 / HEAD_DIM)
            mean1 = jnp.sum(jnp.where(first, 0.0, pr), axis=1, keepdims=True) * (1.0 / HEAD_DIM)
            do = r * (dyhat - ohat * _pair_cols(mean0, mean1, first))
            ddt = (do * ov).T
            deltas.append(jnp.concatenate([jnp.sum(ddt[0:HEAD_DIM], axis=0, keepdims=True),
                                           jnp.sum(ddt[HEAD_DIM:], axis=0, keepdims=True)], axis=1))
            do_st = _stack_heads(do.astype(BF16), first)
            dos.append(do_st)
            dos_t.append(_transpose_bf16(do_st))

        def step(c, carry):
            koff = pl.multiple_of(c * TK, TK)
            valid = _chunk_mask(i, c)
            new = []
            for p in range(N_PAIRS):
                dqt, dfq = carry[p]
                ext = slice(256 * p, 256 * (p + 1))
                cols = slice(128 * p, 128 * (p + 1))
                st = _dot(kx_scr[pl.ds(koff, TK), ext], rhs_t[p])
                st = jnp.where(valid, st, NEG)
                pt = jnp.exp(st - lses[p])
                dpt = _dot(vx_scr[pl.ds(koff, TK), cols], dos_t[p])
                dst = pt * (dpt - deltas[p])
                dsb = dst.astype(BF16)
                dfq = dfq + jnp.sum(dsb.astype(F32), axis=0, keepdims=True)
                dkx_scr[pl.ds(koff, TK), ext] += _dot(dsb, rhs[p])
                dvx_scr[pl.ds(koff, TK), cols] += _dot(pt.astype(BF16), dos[p])
                kt = _stack_heads_lanes(kt_scr[cols, pl.ds(koff, TK)])
                dqt = dqt + _dot(kt, jnp.concatenate([dsb[:, 0:TQ], dsb[:, TQ:]], axis=0))
                new.append((dqt, dfq))
            return tuple(new)

        init = tuple((jnp.zeros((128, TQ), F32), jnp.zeros((1, 2 * TQ), F32)) for _ in range(N_PAIRS))
        final = lax.fori_loop(0, (i + 2) // 2, step, init)

        for p in range(N_PAIRS):
            dqt, dfq = final[p]
            dq_ref[:, 128 * p:128 * (p + 1)] = (dqt.T * 0.125).astype(BF16)
            dfr_ref[0, 2 * p:2 * p + 1, :] = dfq[:, 0:TQ]
            dfr_ref[0, 2 * p + 1:2 * p + 2, :] = dfq[:, TQ:]

        @pl.when(i == nq - 1)
        def _():
            dka = jnp.zeros((lp, 128), F32)
            for p in range(N_PAIRS):
                dk_ref[:, 128 * p:128 * (p + 1)] = dkx_scr[0:lp, 256 * p:256 * p + 128].astype(BF16)
                dka = dka + dkx_scr[0:lp, 256 * p + 128:256 * (p + 1)]
            dka_ref[...] = dka
            dv_ref[...] = dvx_scr[0:lp, :].astype(BF16)

        if nw:
            pl.when((b == nb - 1) & (i == nq - 1))(comm.finish)

    qblk = pl.BlockSpec((TQ, ATTN_DIM), lambda b, i: (b * nq + i, 0))
    qablk = pl.BlockSpec((TQ, N_HEADS * 128), lambda b, i: (b * nq + i, 0))
    seq = pl.BlockSpec((lp, ATTN_DIM), lambda b, i: (b, 0))
    kaseq = pl.BlockSpec((lp, 128), lambda b, i: (b, 0))
    rowblk = pl.BlockSpec((1, N_HEADS, TQ), lambda b, i: (b, 0, i))
    gspec = pl.BlockSpec((1, ATTN_DIM), lambda b, i: (0, 0))
    return pl.pallas_call(
        body, name="attn_bwd", grid=(nb, nq),
        in_specs=[qblk, qblk, qablk, seq, seq, kaseq, qblk, rowblk, gspec] + [ANY] * nw,
        out_specs=[qblk, seq, seq, kaseq, rowblk, gspec] + [ANY] * nw,
        out_shape=[jax.ShapeDtypeStruct((n, ATTN_DIM), BF16), jax.ShapeDtypeStruct((n, ATTN_DIM), BF16),
                   jax.ShapeDtypeStruct((n, ATTN_DIM), BF16), jax.ShapeDtypeStruct((n, 128), F32),
                   jax.ShapeDtypeStruct((nb, N_HEADS, lp), F32), jax.ShapeDtypeStruct((1, ATTN_DIM), F32)]
        + [jax.ShapeDtypeStruct(a.shape, a.dtype) for a in exchange],
        scratch_shapes=[pltpu.VMEM((lpp, 2 * ATTN_DIM), BF16), pltpu.VMEM((lpp, ATTN_DIM), BF16),
                        pltpu.VMEM((ATTN_DIM, lpp), BF16), pltpu.VMEM((lpp, 2 * ATTN_DIM), F32),
                        pltpu.VMEM((lpp, ATTN_DIM), F32)] + (_comm_sems(nw) if nw else []),
        compiler_params=_params(("arbitrary", "arbitrary")),
    )(dz, q, qa, k, v, ka, o, lse, gain, *exchange)


def _loss_head(h, gain, target, lp):
    n = h.shape[0]
    nb = n // lp
    nq = lp // 128

    def body(h_ref, g_ref, t_ref, loss_ref, dh_ref, dgain_ref):
        b = pl.program_id(0)
        i = pl.program_id(1)

        @pl.when((b == 0) & (i == 0))
        def _():
            loss_ref[...] = jnp.zeros_like(loss_ref)
            dgain_ref[...] = jnp.zeros_like(dgain_ref)

        @pl.when(i == 0)
        def _():
            dh_ref[...] = jnp.zeros_like(dh_ref)

        @pl.when(i > 0)
        def _():
            gain_v = g_ref[...]
            y, xhat, r = _rms(h_ref[...], gain_v)
            err = y - t_ref[...]
            loss_ref[...] += 0.5 * jnp.sum(jnp.sum(err * err, axis=1, keepdims=True), axis=0,
                                           keepdims=True) * (1.0 / D_MODEL)
            dy = err * (1.0 / D_MODEL)
            dh_ref[...] = _rms_bwd(dy, xhat, r, gain_v)
            dgain_ref[...] += jnp.sum(dy * xhat, axis=0, keepdims=True)

    rows = pl.BlockSpec((128, D_MODEL), lambda b, i: (b * nq + i, 0))
    trows = pl.BlockSpec((128, D_MODEL), lambda b, i: (b * (nq - 1) + jnp.maximum(i, 1) - 1, 0))
    return pl.pallas_call(
        body, name="loss_head", grid=(nb, nq),
        in_specs=[rows, pl.BlockSpec((1, D_MODEL), lambda b, i: (0, 0)), trows],
        out_specs=[pl.BlockSpec((1, 1), lambda b, i: (0, 0)), rows, pl.BlockSpec((1, D_MODEL), lambda b, i: (0, 0))],
        out_shape=[jax.ShapeDtypeStruct((1, 1), F32), jax.ShapeDtypeStruct((n, D_MODEL), F32),
                   jax.ShapeDtypeStruct((1, D_MODEL), F32)],
        compiler_params=_params(("arbitrary", "arbitrary")),
    )(h, gain, target)


def _adamw(parts, w, m, v, name):
    s_parts, r, c = parts.shape
    tr = r
    for t in (256, 128, 64, 32, 16):
        if r % t == 0 and r > t:
            tr = t
            break

    def body(p_ref, w_ref, m_ref, v_ref, g_ref, d_ref, nm_ref, nv_ref):
        g = p_ref[0].astype(F32)
        for s in range(1, s_parts):
            g = g + p_ref[s].astype(F32)
        nm = ADAM_B1 * m_ref[...] + (1.0 - ADAM_B1) * g
        nv = ADAM_B2 * v_ref[...] + (1.0 - ADAM_B2) * (g * g)
        m_hat = nm / (1.0 - ADAM_B1 ** ADAM_STEP)
        v_hat = nv / (1.0 - ADAM_B2 ** ADAM_STEP)
        g_ref[...] = g
        d_ref[...] = -ADAM_LR * (m_hat / (jnp.sqrt(v_hat) + ADAM_EPS) + ADAM_WD * w_ref[...])
        nm_ref[...] = nm
        nv_ref[...] = nv

    blk = pl.BlockSpec((tr, c), lambda i: (i, 0))
    return pl.pallas_call(
        body, name=name, grid=(r // tr,),
        in_specs=[pl.BlockSpec((s_parts, tr, c), lambda i: (0, i, 0)), blk, blk, blk],
        out_specs=[blk] * 4,
        out_shape=[jax.ShapeDtypeStruct((r, c), F32)] * 4,
        compiler_params=_params(("parallel",)),
    )(parts, w, m, v)


def _sum_parts(parts, name):
    s_parts, r, c = parts.shape

    def body(p_ref, out_ref):
        acc = p_ref[0]
        for s in range(1, s_parts):
            acc = acc + p_ref[s]
        out_ref[...] = acc

    return pl.pallas_call(
        body, name=name, out_shape=jax.ShapeDtypeStruct((r, c), F32),
        in_specs=[pl.BlockSpec(memory_space=pltpu.VMEM)], out_specs=pl.BlockSpec(memory_space=pltpu.VMEM),
    )(parts)


SMALL_ROWS = 184


def _pack_small(d_gains, d_gc, d_ga, d_bf, d_conv, d_meta):
    rows = [g.reshape(8, 128) for g in d_gains]
    rows += [d_gc.reshape(4, 128), d_ga.reshape(4, 128), d_bf.reshape(1, 128)]
    rows += [d_conv.reshape(12, 128), d_meta.reshape(128, 128)]
    packed = jnp.concatenate(rows, axis=0)
    return jnp.pad(packed, ((0, SMALL_ROWS - packed.shape[0]), (0, 0)))


def kernel(x, meta_tokens, ffn1_norm, ffn1_w_gu, ffn1_w_down, mix_norm, w_in, conv_w, b_f, out_norm_conv, out_norm_attn, w_out, ffn2_norm, ffn2_w_gu, ffn2_w_down, final_norm, loss_target, m_meta_tokens, m_ffn1_norm, m_ffn1_w_gu, m_ffn1_w_down, m_mix_norm, m_w_in, m_conv_w, m_b_f, m_out_norm_conv, m_out_norm_attn, m_w_out, m_ffn2_norm, m_ffn2_w_gu, m_ffn2_w_down, m_final_norm, v_meta_tokens, v_ffn1_norm, v_ffn1_w_gu, v_ffn1_w_down, v_mix_norm, v_w_in, v_conv_w, v_b_f, v_out_norm_conv, v_out_norm_attn, v_w_out, v_ffn2_norm, v_ffn2_w_gu, v_ffn2_w_down, v_final_norm):
    nb, seq, _ = x.shape
    lp = PAD + N_META + seq
    n = nb * lp
    me = 4 * lax.axis_index("x") + 2 * lax.axis_index("y") + lax.axis_index("c")

    wgu1_8, wd1_8 = _all_gather([ffn1_w_gu[0].astype(BF16), ffn1_w_down[0].astype(BF16)], "gather_ffn1")
    small_in = jnp.concatenate(
        [meta_tokens, jnp.pad(conv_w[0], ((0, 0), (0, 128 - conv_w.shape[2]))), jnp.zeros((5, 128), F32)], axis=0)
    (small_8,) = _all_gather([small_in], "gather_small")
    meta_full = small_8[:, 0:N_META, :].transpose(1, 0, 2).reshape(N_META, D_MODEL)
    conv_full = small_8[:, N_META:N_META + 3, 0:CONV_DIM // N_DEV].transpose(1, 0, 2).reshape(3, CONV_DIM)
    wgu1 = wgu1_8.reshape(2, N_CHUNK, D_MODEL, F_CHUNK)
    wd1 = wd1_8.reshape(N_CHUNK, F_CHUNK, D_MODEL)
    b_f_row = jnp.pad(b_f, ((0, 0), (0, 128 - N_HEADS)))
    gmat = _group_matrix()

    h0 = jnp.concatenate([jnp.zeros((nb, PAD, D_MODEL), F32),
                          jnp.broadcast_to(meta_full[None], (nb, N_META, D_MODEL)), x], axis=1).reshape(n, D_MODEL)
    later = [w_in[0].astype(BF16), w_out[0].astype(BF16), ffn2_w_gu[0].astype(BF16), ffn2_w_down[0].astype(BF16)]
    h1, n1, gate1, up1, win_8, wout_8, wgu2_8, wd2_8 = _ffn_fwd(h0, ffn1_norm, wgu1, wd1, "ffn1_fwd", gather=later)
    wgu2 = wgu2_8.reshape(2, N_CHUNK, D_MODEL, F_CHUNK)
    wd2 = wd2_8.reshape(N_CHUNK, F_CHUNK, D_MODEL)
    w_in_full = jnp.pad(win_8.transpose(1, 0, 2).reshape(D_MODEL, IN_DIM), ((0, 0), (0, IN_PAD - IN_DIM)))
    w_out_full = wout_8.reshape(D_MODEL, D_MODEL)

    bg, cg, hc, q, k, v, fg = _inproj_fwd(h1, mix_norm, w_in_full)
    zc = _conv_fwd(bg, cg, hc, conv_full, out_norm_conv, gmat, lp)
    ka, qa = _fgate_fwd(fg, b_f_row, lp)
    za, o, lse = _attn_fwd(q, qa, k, v, ka, out_norm_attn, lp)
    h2 = _outproj_fwd(zc, za, w_out_full, h1)
    h3, n3, gate2, up2 = _ffn_fwd(h2, ffn2_norm, wgu2, wd2, "ffn2_fwd")
    loss_part, dh3, d_final = _loss_head(h3, final_norm.reshape(1, D_MODEL), loss_target.reshape(nb * seq, D_MODEL), lp)

    dh2, dhb3, dgate2, dup2, d_ffn2 = _ffn_bwd_x(dh3, h2, ffn2_norm, gate2, up2, wgu2, wd2, "ffn2_bwd_x")
    dwgu2, dwd2 = _ffn_bwd_w(dhb3, n3, gate2, up2, dgate2, dup2, "ffn2_bwd_w")
    dzc, dza, dwout = _outproj_bwd(dh2, zc, za, w_out_full)
    send_a = [dwgu2.reshape(N_DEV, D_MODEL, F_CHUNK), dwd2.reshape(N_DEV, F_CHUNK // 2, D_MODEL),
              dwout.reshape(N_DEV, D_MODEL // N_DEV, D_MODEL)]
    dq, dk, dv, dka, dfr, d_ga, p_wgu2, p_wd2, p_wout = _attn_bwd(
        dza, q, qa, k, v, ka, o, lse, out_norm_attn, lp, exchange=send_a)
    dfg, d_bf = _fgate_bwd(dka, dfr, fg, b_f_row, lp)
    dbg, dcg, dhc, d_conv, d_gc = _conv_bwd(dzc, bg, cg, hc, conv_full, out_norm_conv, gmat, lp)
    dh1, dwin, d_mix = _inproj_bwd([dbg, dcg, dhc, dq, dk, dv], dfg, dh2, h1, mix_norm, w_in_full)
    dhb1, dgate1, dup1 = _ffn_bwd_act(dh1, gate1, up1, wd1, "ffn1_bwd_act")
    dwgu1, dwd1 = _ffn_bwd_w(dhb1, n1, gate1, up1, dgate1, dup1, "ffn1_bwd_w")
    dwin_8 = dwin[:, 0:IN_DIM].reshape(D_MODEL, N_DEV, IN_DIM // N_DEV).transpose(1, 0, 2)
    send_b = [dwgu1.reshape(N_DEV, D_MODEL, F_CHUNK), dwd1.reshape(N_DEV, F_CHUNK // 2, D_MODEL), dwin_8]
    dh0, d_ffn1, p_wgu1, p_wd1, p_win = _ffn_bwd_in(
        dh1, h0, ffn1_norm, dgate1, dup1, wgu1, "ffn1_bwd_in", exchange=send_b)

    dh0 = dh0.reshape(nb, lp, D_MODEL)
    grad_x = dh0[:, PAD + N_META:, :]
    d_meta = jnp.sum(dh0[:, PAD:PAD + N_META, :], axis=0)

    small = _pack_small([d_ffn1, d_mix, d_ffn2, d_final], d_gc, d_ga, d_bf, d_conv, d_meta)
    (small_all,) = _all_gather([small], "gather_small_grads")
    small_sum = _sum_parts(small_all, "sum_small_grads")
    g_ffn1n, g_mixn, g_ffn2n, g_finaln = (small_sum[8 * t:8 * t + 8].reshape(1, D_MODEL) for t in range(4))
    g_gc = small_sum[32:36].reshape(1, CONV_DIM)
    g_ga = small_sum[36:40].reshape(1, ATTN_DIM)
    g_bf = small_sum[40:41, 0:N_HEADS]
    g_conv_full = small_sum[41:53].reshape(3, CONV_DIM)
    g_meta_full = small_sum[53:181].reshape(N_META, D_MODEL)
    g_conv = lax.dynamic_slice_in_dim(g_conv_full, me * (CONV_DIM // N_DEV), CONV_DIM // N_DEV, axis=1)
    g_meta = lax.dynamic_slice_in_dim(g_meta_full, me * (D_MODEL // N_DEV), D_MODEL // N_DEV, axis=1)

    weights = {
        "meta_tokens": (g_meta[None], meta_tokens, m_meta_tokens, v_meta_tokens),
        "ffn1_norm": (g_ffn1n[None], ffn1_norm, m_ffn1_norm, v_ffn1_norm),
        "ffn1_w_gu": (p_wgu1, ffn1_w_gu[0], m_ffn1_w_gu[0], v_ffn1_w_gu[0]),
        "ffn1_w_down": (p_wd1, ffn1_w_down[0], m_ffn1_w_down[0], v_ffn1_w_down[0]),
        "mix_norm": (g_mixn[None], mix_norm, m_mix_norm, v_mix_norm),
        "w_in": (p_win, w_in[0], m_w_in[0], v_w_in[0]),
        "conv_w": (g_conv[None], conv_w[0], m_conv_w[0], v_conv_w[0]),
        "b_f": (g_bf[None], b_f, m_b_f, v_b_f),
        "out_norm_conv": (g_gc[None], out_norm_conv, m_out_norm_conv, v_out_norm_conv),
        "out_norm_attn": (g_ga[None], out_norm_attn, m_out_norm_attn, v_out_norm_attn),
        "w_out": (p_wout, w_out[0], m_w_out[0], v_w_out[0]),
        "ffn2_norm": (g_ffn2n[None], ffn2_norm, m_ffn2_norm, v_ffn2_norm),
        "ffn2_w_gu": (p_wgu2, ffn2_w_gu[0], m_ffn2_w_gu[0], v_ffn2_w_gu[0]),
        "ffn2_w_down": (p_wd2, ffn2_w_down[0], m_ffn2_w_down[0], v_ffn2_w_down[0]),
        "final_norm": (g_finaln[None], final_norm.reshape(1, D_MODEL), m_final_norm.reshape(1, D_MODEL),
                       v_final_norm.reshape(1, D_MODEL)),
    }
    shapes = {"meta_tokens": meta_tokens.shape, "ffn1_norm": ffn1_norm.shape, "ffn1_w_gu": ffn1_w_gu.shape,
              "ffn1_w_down": ffn1_w_down.shape, "mix_norm": mix_norm.shape, "w_in": w_in.shape,
              "conv_w": conv_w.shape, "b_f": b_f.shape, "out_norm_conv": out_norm_conv.shape,
              "out_norm_attn": out_norm_attn.shape, "w_out": w_out.shape, "ffn2_norm": ffn2_norm.shape,
              "ffn2_w_gu": ffn2_w_gu.shape, "ffn2_w_down": ffn2_w_down.shape, "final_norm": final_norm.shape}
    grads, deltas, new_m, new_v = [], [], [], []
    for name, (p, w, m, vv) in weights.items():
        g, d, nm, nv = _adamw(p, w, m, vv, "adamw_" + name)
        shape = shapes[name]
        grads.append(g.reshape(shape))
        deltas.append(d.reshape(shape))
        new_m.append(nm.reshape(shape))
        new_v.append(nv.reshape(shape))

    loss = lax.psum(loss_part[0, 0], ("x", "y", "c"))
    return (loss, grad_x, *grads, *deltas, *new_m, *new_v)
```

```python
import jax
import jax.numpy as jnp
from jax import lax
from jax.experimental import pallas as pl
from jax.experimental.pallas import tpu as pltpu

F32 = jnp.float32
BF16 = jnp.bfloat16

N_DEV = 8
D_MODEL = 1024
N_META = 16
PAD = 128 - N_META
CONV_DIM = 512
ATTN_DIM = 512
HEAD_DIM = 64
N_HEADS = 8
N_PAIRS = N_HEADS // 2
D_FF = 2816
N_CHUNK = 4
F_CHUNK = D_FF // N_CHUNK
IN_DIM = 3080
IN_PAD = 3200
IN_MAIN = 3072
EPS = 1e-6
NEG = -1e30
TQ = 128
TK = 256
VMEM_LIMIT = 56 * 1024 * 1024

ADAM_LR = 0.001
ADAM_B1 = 0.9
ADAM_B2 = 0.999
ADAM_EPS = 1e-08
ADAM_WD = 0.01
ADAM_STEP = 10

MESH = pl.DeviceIdType.MESH
ANY = pl.BlockSpec(memory_space=pl.ANY)


def _params(sem=None):
    return pltpu.CompilerParams(dimension_semantics=sem, vmem_limit_bytes=VMEM_LIMIT)


def _row_tile(n, prefer):
    for t in (prefer, 512, 256, 128):
        if t <= n and n % t == 0:
            return t
    raise ValueError(f"no row tile for {n}")


def _dot(a, b):
    return jnp.dot(a, b, preferred_element_type=F32)


def _dot_nt(a, b):
    return lax.dot_general(a, b, (((1,), (1,)), ((), ())), preferred_element_type=F32)


def _dot_tn(a, b):
    return lax.dot_general(a, b, (((0,), (0,)), ((), ())), preferred_element_type=F32)


def _rms(x, g):
    r = lax.rsqrt(jnp.mean(x * x, axis=-1, keepdims=True) + EPS)
    xhat = x * r
    return xhat * g, xhat, r


def _rms_bwd(dn, xhat, r, g):
    dxhat = dn * g
    return r * (dxhat - xhat * jnp.mean(dxhat * xhat, axis=-1, keepdims=True))


def _sigmoid(x):
    return 1.0 / (1.0 + jnp.exp(-x))


def _place():
    return lax.axis_index("x"), lax.axis_index("y"), lax.axis_index("c")


def _comm_sems(nw):
    return [pltpu.SemaphoreType.DMA((nw, 7)), pltpu.SemaphoreType.DMA((nw, 7)), pltpu.SemaphoreType.DMA((nw,))]


class _Gather:
    def __init__(self, ins, outs, sems):
        self.ins, self.outs = ins, outs
        self.send, self.recv, self.local = sems
        x, y, c = _place()
        self.c = c
        self.me, self.sibling = (x, y, c), (x, y, 1 - c)
        self.chips = [(1 - x, y), (x, 1 - y), (1 - x, 1 - y)]

    def _copy(self, w, k, block, to, own=False):
        slot = self.outs[w].at[4 * block[0] + 2 * block[1] + block[2]]
        return pltpu.make_async_remote_copy(
            src_ref=self.ins[w] if own else slot, dst_ref=slot,
            send_sem=self.send.at[w, k], recv_sem=self.recv.at[w, k], device_id=to, device_id_type=MESH)

    def _mine(self, w):
        x, y, c = self.me
        return pltpu.make_async_copy(self.ins[w], self.outs[w].at[4 * x + 2 * y + c], self.local.at[w])

    def _first(self, w):
        return ([self._copy(w, 0, self.me, self.sibling, own=True)]
                + [self._copy(w, 1 + j, self.me, (*chip, self.c), own=True) for j, chip in enumerate(self.chips)])

    def _passed(self, w):
        return [self._copy(w, 4 + j, (*chip, self.c), self.sibling) for j, chip in enumerate(self.chips)]

    def start(self):
        for w in range(len(self.ins)):
            self._mine(w).start()
        for w in range(len(self.ins)):
            for cp in self._first(w):
                cp.start()

    def forward(self):
        for w in range(len(self.ins)):
            for j, chip in enumerate(self.chips):
                self._copy(w, 1 + j, (*chip, self.c), self.me).wait_recv()
                self._passed(w)[j].start()

    def finish(self):
        for w in range(len(self.ins)):
            self._copy(w, 0, self.sibling, self.me).wait_recv()
            for j, chip in enumerate(self.chips):
                self._copy(w, 4 + j, (*chip, 1 - self.c), self.me).wait_recv()
        for w in range(len(self.ins)):
            for cp in self._first(w) + self._passed(w):
                cp.wait_send()
            self._mine(w).wait()


class _Exchange:
    def __init__(self, ins, outs, sems):
        self.ins, self.outs = ins, outs
        self.send, self.recv, self.local = sems
        self.x, self.y, self.c = _place()
        self.me = 4 * self.x + 2 * self.y + self.c

    def _copy(self, w, k):
        flip = lambda v, bit: 1 - v if bit else v
        peer = (flip(self.x, ((k + 1) >> 2) & 1), flip(self.y, ((k + 1) >> 1) & 1), flip(self.c, (k + 1) & 1))
        return pltpu.make_async_remote_copy(
            src_ref=self.ins[w].at[4 * peer[0] + 2 * peer[1] + peer[2]], dst_ref=self.outs[w].at[self.me],
            send_sem=self.send.at[w, k], recv_sem=self.recv.at[w, k], device_id=peer, device_id_type=MESH)

    def _mine(self, w):
        return pltpu.make_async_copy(self.ins[w].at[self.me], self.outs[w].at[self.me], self.local.at[w])

    def start(self):
        for w in range(len(self.ins)):
            self._mine(w).start()
            for k in range(N_DEV - 1):
                self._copy(w, k).start()

    def finish(self):
        for w in range(len(self.ins)):
            for k in range(N_DEV - 1):
                self._copy(w, k).wait()
            self._mine(w).wait()


class _PairExchange:
    def __init__(self, ins, outs, sems):
        self.ins, self.outs = ins, outs
        self.send, self.recv, _ = sems
        x, y, self.c = _place()
        self.sibling = (x, y, 1 - self.c)

    def _copy(self, w, t):
        return pltpu.make_async_remote_copy(
            src_ref=self.ins[w].at[2 * t + 1 - self.c], dst_ref=self.outs[w].at[t],
            send_sem=self.send.at[w, t], recv_sem=self.recv.at[w, t], device_id=self.sibling, device_id_type=MESH)

    def start(self):
        for w in range(len(self.ins)):
            for t in range(4):
                self._copy(w, t).start()

    def finish(self):
        for w in range(len(self.ins)):
            for t in range(4):
                self._copy(w, t).wait()


class _ChipExchange:
    def __init__(self, ins, outs, sems):
        self.ins, self.outs = ins, outs
        self.send, self.recv, self.local = sems
        self.x, self.y, self.c = _place()
        self.chip = 2 * self.x + self.y

    def _copy(self, w, k):
        flip = lambda v, bit: 1 - v if bit else v
        px, py = flip(self.x, ((k + 1) >> 1) & 1), flip(self.y, (k + 1) & 1)
        return pltpu.make_async_remote_copy(
            src_ref=self.ins[w].at[2 * px + py], dst_ref=self.outs[w].at[self.chip],
            send_sem=self.send.at[w, k], recv_sem=self.recv.at[w, k], device_id=(px, py, self.c),
            device_id_type=MESH)

    def _mine(self, w):
        return pltpu.make_async_copy(self.ins[w].at[self.chip], self.outs[w].at[self.chip], self.local.at[w])

    def start(self):
        for w in range(len(self.ins)):
            self._mine(w).start()
            for k in range(3):
                self._copy(w, k).start()

    def finish(self):
        for w in range(len(self.ins)):
            for k in range(3):
                self._copy(w, k).wait()
            self._mine(w).wait()


def _pair_exchange(xs, name):
    nw = len(xs)

    def body(*refs):
        comm = _PairExchange(refs[:nw], refs[nw:2 * nw], refs[2 * nw:])
        comm.start()
        comm.finish()

    return pl.pallas_call(
        body, name=name, in_specs=[ANY] * nw, out_specs=[ANY] * nw,
        out_shape=[jax.ShapeDtypeStruct((4,) + a.shape[1:], a.dtype) for a in xs],
        scratch_shapes=_comm_sems(nw),
    )(*xs)


def _pair_sum(own, got, name):
    _, r, c = own.shape
    tr = r
    for t in (256, 128, 64, 32, 16):
        if r % t == 0 and r > t:
            tr = t
            break

    def body(own_ref, got_ref, out_ref):
        mine = jnp.where(lax.axis_index("c") == 0, own_ref[:, 0].astype(F32), own_ref[:, 1].astype(F32))
        out_ref[...] = (mine + got_ref[...].astype(F32)).astype(BF16)

    return pl.pallas_call(
        body, name=name, grid=(r // tr,),
        in_specs=[pl.BlockSpec((4, 2, tr, c), lambda i: (0, 0, i, 0)), pl.BlockSpec((4, tr, c), lambda i: (0, i, 0))],
        out_specs=pl.BlockSpec((4, tr, c), lambda i: (0, i, 0)),
        out_shape=jax.ShapeDtypeStruct((4, r, c), BF16),
        compiler_params=_params(("parallel",)),
    )(own.reshape(4, 2, r, c), got)


def _split_refs(refs, n_in, n_comm, n_out, n_scr):
    a = n_in
    b = a + n_comm
    c = b + n_out
    d = c + n_comm
    e = d + n_scr
    return refs[:a], refs[a:b], refs[b:c], refs[c:d], refs[d:e], refs[e:]


def _all_gather(xs, name):
    nw = len(xs)

    def body(*refs):
        comm = _Gather(refs[:nw], refs[nw:2 * nw], refs[2 * nw:])
        comm.start()
        comm.forward()
        comm.finish()

    return pl.pallas_call(
        body, name=name, in_specs=[ANY] * nw, out_specs=[ANY] * nw,
        out_shape=[jax.ShapeDtypeStruct((N_DEV,) + a.shape, a.dtype) for a in xs],
        scratch_shapes=_comm_sems(nw),
    )(*xs)


def _ffn_fwd(h, gain, wgu, wd, name, gather=()):
    n = h.shape[0]
    tm = _row_tile(n, 512)
    n_i = n // tm
    nw = len(gather)

    def body(*refs):
        (h_ref, g_ref, wgu_ref, wd_ref), gin, (out_ref, gate_ref, up_ref), gout, (n_scr, acc_scr), sems = \
            _split_refs(refs, 4, nw, 3, 2)
        i = pl.program_id(0)
        j = pl.program_id(1)
        if nw:
            comm = _Gather(gin, gout, sems)
            pl.when((i == 0) & (j == 0))(comm.start)
            pl.when((i == (3 * n_i) // 4) & (j == 0))(comm.forward)

        @pl.when(j == 0)
        def _():
            y, _, _ = _rms(h_ref[...], g_ref[...])
            n_scr[...] = y.astype(BF16)
            acc_scr[...] = jnp.zeros_like(acc_scr)

        nb = n_scr[...]
        gate = _dot(nb, wgu_ref[0, 0])
        up = _dot(nb, wgu_ref[1, 0])
        gate_ref[0] = gate.astype(BF16)
        up_ref[0] = up.astype(BF16)
        act = (gate * _sigmoid(gate) * up).astype(BF16)
        acc_scr[...] += _dot(act, wd_ref[0])

        @pl.when(j == N_CHUNK - 1)
        def _():
            out_ref[...] = h_ref[...] + 0.5 * acc_scr[...]

        if nw:
            pl.when((i == n_i - 1) & (j == N_CHUNK - 1))(comm.finish)

    return pl.pallas_call(
        body, name=name, grid=(n_i, N_CHUNK),
        in_specs=[pl.BlockSpec((tm, D_MODEL), lambda i, j: (i, 0)),
                  pl.BlockSpec((1, D_MODEL), lambda i, j: (0, 0)),
                  pl.BlockSpec((2, 1, D_MODEL, F_CHUNK), lambda i, j: (0, j, 0, 0)),
                  pl.BlockSpec((1, F_CHUNK, D_MODEL), lambda i, j: (j, 0, 0))] + [ANY] * nw,
        out_specs=[pl.BlockSpec((tm, D_MODEL), lambda i, j: (i, 0)),
                   pl.BlockSpec((1, tm, F_CHUNK), lambda i, j: (j, i, 0)),
                   pl.BlockSpec((1, tm, F_CHUNK), lambda i, j: (j, i, 0))] + [ANY] * nw,
        out_shape=[jax.ShapeDtypeStruct((n, D_MODEL), F32),
                   jax.ShapeDtypeStruct((N_CHUNK, n, F_CHUNK), BF16),
                   jax.ShapeDtypeStruct((N_CHUNK, n, F_CHUNK), BF16)]
        + [jax.ShapeDtypeStruct((N_DEV,) + a.shape, a.dtype) for a in gather],
        scratch_shapes=[pltpu.VMEM((tm, D_MODEL), BF16), pltpu.VMEM((tm, D_MODEL), F32)]
        + (_comm_sems(nw) if nw else []),
        compiler_params=_params(("arbitrary", "arbitrary")),
    )(h, gain, wgu, wd, *gather)


def _ffn_bwd_x(dh_out, h_in, gain, gate, up, wgu, wd, name):
    n = h_in.shape[0]
    tm = _row_tile(n, 512)

    def body(dh_ref, h_ref, g_ref, gate_ref, up_ref, wgu_ref, wd_ref,
             dhin_ref, dgate_ref, dup_ref, dgain_ref, dhb_scr, acc_scr):
        i = pl.program_id(0)
        j = pl.program_id(1)

        @pl.when((i == 0) & (j == 0))
        def _():
            dgain_ref[...] = jnp.zeros_like(dgain_ref)

        @pl.when(j == 0)
        def _():
            dhb_scr[...] = (0.5 * dh_ref[...]).astype(BF16)
            acc_scr[...] = jnp.zeros_like(acc_scr)

        da = _dot_nt(dhb_scr[...], wd_ref[0])
        g = gate_ref[0].astype(F32)
        u = up_ref[0].astype(F32)
        sig = _sigmoid(g)
        dgate = (da * u * (sig * (1.0 + g * (1.0 - sig)))).astype(BF16)
        dup = (da * (g * sig)).astype(BF16)
        dgate_ref[0] = dgate
        dup_ref[0] = dup
        acc_scr[...] += _dot_nt(dgate, wgu_ref[0, 0]) + _dot_nt(dup, wgu_ref[1, 0])

        @pl.when(j == N_CHUNK - 1)
        def _():
            gain_v = g_ref[...]
            _, xhat, r = _rms(h_ref[...], gain_v)
            dn = acc_scr[...]
            dhin_ref[...] = dh_ref[...] + _rms_bwd(dn, xhat, r, gain_v)
            dgain_ref[...] += jnp.sum(dn * xhat, axis=0, keepdims=True)

    chunk = pl.BlockSpec((1, tm, F_CHUNK), lambda i, j: (j, i, 0))
    rows = pl.BlockSpec((tm, D_MODEL), lambda i, j: (i, 0))
    vec = pl.BlockSpec((1, D_MODEL), lambda i, j: (0, 0))
    return pl.pallas_call(
        body, name=name, grid=(n // tm, N_CHUNK),
        in_specs=[rows, rows, vec, chunk, chunk,
                  pl.BlockSpec((2, 1, D_MODEL, F_CHUNK), lambda i, j: (0, j, 0, 0)),
                  pl.BlockSpec((1, F_CHUNK, D_MODEL), lambda i, j: (j, 0, 0))],
        out_specs=[rows, chunk, chunk, vec],
        out_shape=[jax.ShapeDtypeStruct((n, D_MODEL), F32),
                   jax.ShapeDtypeStruct((N_CHUNK, n, F_CHUNK), BF16),
                   jax.ShapeDtypeStruct((N_CHUNK, n, F_CHUNK), BF16),
                   jax.ShapeDtypeStruct((1, D_MODEL), F32)],
        scratch_shapes=[pltpu.VMEM((tm, D_MODEL), BF16), pltpu.VMEM((tm, D_MODEL), F32)],
        compiler_params=_params(("arbitrary", "arbitrary")),
    )(dh_out, h_in, gain, gate, up, wgu, wd)


def _ffn_bwd_act(dh_out, gate, up, wd, name):
    n = dh_out.shape[0]
    tm = _row_tile(n, 512)

    def body(dh_ref, gate_ref, up_ref, wd_ref, dgate_ref, dup_ref, dhb_scr):
        @pl.when(pl.program_id(1) == 0)
        def _():
            dhb_scr[...] = (0.5 * dh_ref[...]).astype(BF16)

        da = _dot_nt(dhb_scr[...], wd_ref[0])
        g = gate_ref[0].astype(F32)
        u = up_ref[0].astype(F32)
        sig = _sigmoid(g)
        dgate_ref[0] = (da * u * (sig * (1.0 + g * (1.0 - sig)))).astype(BF16)
        dup_ref[0] = (da * (g * sig)).astype(BF16)

    chunk = pl.BlockSpec((1, tm, F_CHUNK), lambda i, j: (j, i, 0))
    return pl.pallas_call(
        body, name=name, grid=(n // tm, N_CHUNK),
        in_specs=[pl.BlockSpec((tm, D_MODEL), lambda i, j: (i, 0)), chunk, chunk,
                  pl.BlockSpec((1, F_CHUNK, D_MODEL), lambda i, j: (j, 0, 0))],
        out_specs=[chunk, chunk],
        out_shape=[jax.ShapeDtypeStruct((N_CHUNK, n, F_CHUNK), BF16)] * 2,
        scratch_shapes=[pltpu.VMEM((tm, D_MODEL), BF16)],
        compiler_params=_params(("parallel", "arbitrary")),
    )(dh_out, gate, up, wd)


def _ffn_bwd_in(dh_out, h_in, gain, dgate, dup, wgu, name, exchange=()):
    n = h_in.shape[0]
    tm = _row_tile(n, 512)
    n_i = n // tm
    nw = len(exchange)

    def body(*refs):
        (dh_ref, h_ref, g_ref, dgate_ref, dup_ref, wgu_ref), xin, (dhin_ref, dgain_ref), xout, (acc_scr,), sems = \
            _split_refs(refs, 6, nw, 2, 1)
        i = pl.program_id(0)
        j = pl.program_id(1)
        if nw:
            comm = _Exchange(xin, xout, sems)
            pl.when((i == 0) & (j == 0))(comm.start)

        @pl.when((i == 0) & (j == 0))
        def _():
            dgain_ref[...] = jnp.zeros_like(dgain_ref)

        @pl.when(j == 0)
        def _():
            acc_scr[...] = jnp.zeros_like(acc_scr)

        acc_scr[...] += _dot_nt(dgate_ref[0], wgu_ref[0, 0]) + _dot_nt(dup_ref[0], wgu_ref[1, 0])

        @pl.when(j == N_CHUNK - 1)
        def _():
            gain_v = g_ref[...]
            _, xhat, r = _rms(h_ref[...], gain_v)
            dn = acc_scr[...]
            dhin_ref[...] = dh_ref[...] + _rms_bwd(dn, xhat, r, gain_v)
            dgain_ref[...] += jnp.sum(dn * xhat, axis=0, keepdims=True)

        if nw:
            pl.when((i == n_i - 1) & (j == N_CHUNK - 1))(comm.finish)

    chunk = pl.BlockSpec((1, tm, F_CHUNK), lambda i, j: (j, i, 0))
    rows = pl.BlockSpec((tm, D_MODEL), lambda i, j: (i, 0))
    vec = pl.BlockSpec((1, D_MODEL), lambda i, j: (0, 0))
    return pl.pallas_call(
        body, name=name, grid=(n_i, N_CHUNK),
        in_specs=[rows, rows, vec, chunk, chunk,
                  pl.BlockSpec((2, 1, D_MODEL, F_CHUNK), lambda i, j: (0, j, 0, 0))] + [ANY] * nw,
        out_specs=[rows, vec] + [ANY] * nw,
        out_shape=[jax.ShapeDtypeStruct((n, D_MODEL), F32), jax.ShapeDtypeStruct((1, D_MODEL), F32)]
        + [jax.ShapeDtypeStruct(a.shape, a.dtype) for a in exchange],
        scratch_shapes=[pltpu.VMEM((tm, D_MODEL), F32)] + (_comm_sems(nw) if nw else []),
        compiler_params=_params(("arbitrary", "arbitrary")),
    )(dh_out, h_in, gain, dgate, dup, wgu, *exchange)


def _ffn_bwd_w(dh_out, h_in, gain, gate, up, dgate, dup, name):
    n = h_in.shape[0]
    tm = _row_tile(n, 512)
    n_i = n // tm

    def body(dh_ref, h_ref, g_ref, gate_ref, up_ref, dgate_ref, dup_ref, dwgu_ref, dwd_ref,
             ag_scr, au_scr, ad_scr):
        i = pl.program_id(1)

        @pl.when(i == 0)
        def _():
            ag_scr[...] = jnp.zeros_like(ag_scr)
            au_scr[...] = jnp.zeros_like(au_scr)
            ad_scr[...] = jnp.zeros_like(ad_scr)

        y, _, _ = _rms(h_ref[...], g_ref[...])
        nb = y.astype(BF16)
        ag_scr[...] += _dot_tn(nb, dgate_ref[0])
        au_scr[...] += _dot_tn(nb, dup_ref[0])
        g = gate_ref[0].astype(F32)
        act = (g * _sigmoid(g) * up_ref[0].astype(F32)).astype(BF16)
        ad_scr[...] += _dot_tn(act, (0.5 * dh_ref[...]).astype(BF16))

        @pl.when(i == n_i - 1)
        def _():
            dwgu_ref[0, 0] = ag_scr[...].astype(BF16)
            dwgu_ref[1, 0] = au_scr[...].astype(BF16)
            dwd_ref[0] = ad_scr[...].astype(BF16)

    chunk = pl.BlockSpec((1, tm, F_CHUNK), lambda j, i: (j, i, 0))
    rows = pl.BlockSpec((tm, D_MODEL), lambda j, i: (i, 0))
    return pl.pallas_call(
        body, name=name, grid=(N_CHUNK, n_i),
        in_specs=[rows, rows, pl.BlockSpec((1, D_MODEL), lambda j, i: (0, 0)), chunk, chunk, chunk, chunk],
        out_specs=[pl.BlockSpec((2, 1, D_MODEL, F_CHUNK), lambda j, i: (0, j, 0, 0)),
                   pl.BlockSpec((1, F_CHUNK, D_MODEL), lambda j, i: (j, 0, 0))],
        out_shape=[jax.ShapeDtypeStruct((2, N_CHUNK, D_MODEL, F_CHUNK), BF16),
                   jax.ShapeDtypeStruct((N_CHUNK, F_CHUNK, D_MODEL), BF16)],
        scratch_shapes=[pltpu.VMEM((D_MODEL, F_CHUNK), F32), pltpu.VMEM((D_MODEL, F_CHUNK), F32),
                        pltpu.VMEM((F_CHUNK, D_MODEL), F32)],
        compiler_params=_params(("parallel", "arbitrary")),
    )(dh_out, h_in, gain, gate, up, dgate, dup)


def _resident(shape, rank):
    zeros = (0,) * len(shape)
    index_map = (lambda i: zeros) if rank == 1 else (lambda i, j: zeros)
    return pl.BlockSpec(shape, index_map, pipeline_mode=pl.Buffered(1))


W_GU_SHAPE = (2, N_CHUNK, D_MODEL, F_CHUNK)
W_D_SHAPE = (N_CHUNK, F_CHUNK, D_MODEL)


def _ffn_fwd(h, gain, wgu, wd, name, gather=()):
    n = h.shape[0]
    tm = _row_tile(n, 512)
    n_i = n // tm
    nw = len(gather)

    def body(*refs):
        (h_ref, g_ref, wgu_ref, wd_ref), gin, (out_ref, nrm_ref, gate_ref, up_ref), gout, _, sems = \
            _split_refs(refs, 4, nw, 4, 0)
        i = pl.program_id(0)
        if nw:
            comm = _Gather(gin, gout, sems)
            pl.when(i == 0)(comm.start)
            pl.when(i == (3 * n_i) // 4)(comm.forward)

        hv = h_ref[...]
        y, _, _ = _rms(hv, g_ref[...])
        nb = y.astype(BF16)
        nrm_ref[...] = nb
        acc = jnp.zeros((tm, D_MODEL), F32)
        for j in range(N_CHUNK):
            gate = _dot(nb, wgu_ref[0, j])
            up = _dot(nb, wgu_ref[1, j])
            gate_ref[j] = gate.astype(BF16)
            up_ref[j] = up.astype(BF16)
            acc = acc + _dot((gate * _sigmoid(gate) * up).astype(BF16), wd_ref[j])
        out_ref[...] = hv + 0.5 * acc

        if nw:
            pl.when(i == n_i - 1)(comm.finish)

    rows = pl.BlockSpec((tm, D_MODEL), lambda i: (i, 0))
    chunks = pl.BlockSpec((N_CHUNK, tm, F_CHUNK), lambda i: (0, i, 0))
    return pl.pallas_call(
        body, name=name, grid=(n_i,),
        in_specs=[rows, pl.BlockSpec((1, D_MODEL), lambda i: (0, 0)), _resident(W_GU_SHAPE, 1),
                  _resident(W_D_SHAPE, 1)] + [ANY] * nw,
        out_specs=[rows, rows, chunks, chunks] + [ANY] * nw,
        out_shape=[jax.ShapeDtypeStruct((n, D_MODEL), F32), jax.ShapeDtypeStruct((n, D_MODEL), BF16),
                   jax.ShapeDtypeStruct((N_CHUNK, n, F_CHUNK), BF16),
                   jax.ShapeDtypeStruct((N_CHUNK, n, F_CHUNK), BF16)]
        + [jax.ShapeDtypeStruct((N_DEV,) + a.shape, a.dtype) for a in gather],
        scratch_shapes=_comm_sems(nw) if nw else [],
        compiler_params=_params(("arbitrary",)),
    )(h, gain, wgu, wd, *gather)


def _swiglu_bwd(da, gate_ref, up_ref, j):
    g = gate_ref[j].astype(F32)
    u = up_ref[j].astype(F32)
    sig = _sigmoid(g)
    return (da * u * (sig * (1.0 + g * (1.0 - sig)))).astype(BF16), (da * (g * sig)).astype(BF16)


def _ffn_bwd_x(dh_out, h_in, gain, gate, up, wgu, wd, name):
    n = h_in.shape[0]
    tm = _row_tile(n, 256)

    def body(dh_ref, h_ref, g_ref, gate_ref, up_ref, wgu_ref, wd_ref,
             dhin_ref, dhb_ref, dgate_ref, dup_ref, dgain_ref):
        @pl.when(pl.program_id(0) == 0)
        def _():
            dgain_ref[...] = jnp.zeros_like(dgain_ref)

        dhv = dh_ref[...]
        dhb = (0.5 * dhv).astype(BF16)
        dhb_ref[...] = dhb
        dn = jnp.zeros((tm, D_MODEL), F32)
        for j in range(N_CHUNK):
            dgate, dup = _swiglu_bwd(_dot_nt(dhb, wd_ref[j]), gate_ref, up_ref, j)
            dgate_ref[j] = dgate
            dup_ref[j] = dup
            dn = dn + _dot_nt(dgate, wgu_ref[0, j]) + _dot_nt(dup, wgu_ref[1, j])
        gain_v = g_ref[...]
        _, xhat, r = _rms(h_ref[...], gain_v)
        dhin_ref[...] = dhv + _rms_bwd(dn, xhat, r, gain_v)
        dgain_ref[...] += jnp.sum(dn * xhat, axis=0, keepdims=True)

    rows = pl.BlockSpec((tm, D_MODEL), lambda i: (i, 0))
    chunks = pl.BlockSpec((N_CHUNK, tm, F_CHUNK), lambda i: (0, i, 0))
    vec = pl.BlockSpec((1, D_MODEL), lambda i: (0, 0))
    return pl.pallas_call(
        body, name=name, grid=(n // tm,),
        in_specs=[rows, rows, vec, chunks, chunks, _resident(W_GU_SHAPE, 1), _resident(W_D_SHAPE, 1)],
        out_specs=[rows, rows, chunks, chunks, vec],
        out_shape=[jax.ShapeDtypeStruct((n, D_MODEL), F32), jax.ShapeDtypeStruct((n, D_MODEL), BF16),
                   jax.ShapeDtypeStruct((N_CHUNK, n, F_CHUNK), BF16),
                   jax.ShapeDtypeStruct((N_CHUNK, n, F_CHUNK), BF16),
                   jax.ShapeDtypeStruct((1, D_MODEL), F32)],
        compiler_params=_params(("arbitrary",)),
    )(dh_out, h_in, gain, gate, up, wgu, wd)


def _ffn_bwd_act(dh_out, gate, up, wd, name, exchange=()):
    n = dh_out.shape[0]
    tm = _row_tile(n, 512)
    n_i = n // tm
    nw = len(exchange)

    def body(*refs):
        (dh_ref, gate_ref, up_ref, wd_ref), xin, (dhb_ref, dgate_ref, dup_ref), xout, _, sems = \
            _split_refs(refs, 4, nw, 3, 0)
        i = pl.program_id(0)
        if nw:
            comm = _Exchange(xin, xout, sems)
            pl.when(i == 0)(comm.start)

        dhb = (0.5 * dh_ref[...]).astype(BF16)
        dhb_ref[...] = dhb
        for j in range(N_CHUNK):
            dgate_ref[j], dup_ref[j] = _swiglu_bwd(_dot_nt(dhb, wd_ref[j]), gate_ref, up_ref, j)

        if nw:
            pl.when(i == n_i - 1)(comm.finish)

    rows = pl.BlockSpec((tm, D_MODEL), lambda i: (i, 0))
    chunks = pl.BlockSpec((N_CHUNK, tm, F_CHUNK), lambda i: (0, i, 0))
    return pl.pallas_call(
        body, name=name, grid=(n_i,),
        in_specs=[rows, chunks, chunks, _resident(W_D_SHAPE, 1)] + [ANY] * nw,
        out_specs=[rows, chunks, chunks] + [ANY] * nw,
        out_shape=[jax.ShapeDtypeStruct((n, D_MODEL), BF16)] + [jax.ShapeDtypeStruct((N_CHUNK, n, F_CHUNK), BF16)] * 2
        + [jax.ShapeDtypeStruct(a.shape, a.dtype) for a in exchange],
        scratch_shapes=_comm_sems(nw) if nw else [],
        compiler_params=_params(("arbitrary",)),
    )(dh_out, gate, up, wd, *exchange)


def _ffn_bwd_in(dh_out, h_in, gain, dgate, dup, wgu, name, exchange=()):
    n = h_in.shape[0]
    tm = _row_tile(n, 512)
    n_i = n // tm
    nw = len(exchange)

    def body(*refs):
        (dh_ref, h_ref, g_ref, dgate_ref, dup_ref, wgu_ref), xin, (dhin_ref, dgain_ref), xout, _, sems = \
            _split_refs(refs, 6, nw, 2, 0)
        i = pl.program_id(0)
        if nw:
            comm = _ChipExchange(xin, xout, sems)
            pl.when(i == 0)(comm.start)

        @pl.when(i == 0)
        def _():
            dgain_ref[...] = jnp.zeros_like(dgain_ref)

        dn = jnp.zeros((tm, D_MODEL), F32)
        for j in range(N_CHUNK):
            dn = dn + _dot_nt(dgate_ref[j], wgu_ref[0, j]) + _dot_nt(dup_ref[j], wgu_ref[1, j])
        gain_v = g_ref[...]
        _, xhat, r = _rms(h_ref[...], gain_v)
        dhin_ref[...] = dh_ref[...] + _rms_bwd(dn, xhat, r, gain_v)
        dgain_ref[...] += jnp.sum(dn * xhat, axis=0, keepdims=True)

        if nw:
            pl.when(i == n_i - 1)(comm.finish)

    rows = pl.BlockSpec((tm, D_MODEL), lambda i: (i, 0))
    chunks = pl.BlockSpec((N_CHUNK, tm, F_CHUNK), lambda i: (0, i, 0))
    vec = pl.BlockSpec((1, D_MODEL), lambda i: (0, 0))
    return pl.pallas_call(
        body, name=name, grid=(n_i,),
        in_specs=[rows, rows, vec, chunks, chunks, _resident(W_GU_SHAPE, 1)] + [ANY] * nw,
        out_specs=[rows, vec] + [ANY] * nw,
        out_shape=[jax.ShapeDtypeStruct((n, D_MODEL), F32), jax.ShapeDtypeStruct((1, D_MODEL), F32)]
        + [jax.ShapeDtypeStruct(a.shape, a.dtype) for a in exchange],
        scratch_shapes=_comm_sems(nw) if nw else [],
        compiler_params=_params(("arbitrary",)),
    )(dh_out, h_in, gain, dgate, dup, wgu, *exchange)


W_GROUP = 2


def _ffn_bwd_w(dhb, nrm, gate, up, dgate, dup, name):
    n = nrm.shape[0]
    tm = _row_tile(n, 512)
    n_i = n // tm

    def body(dhb_ref, nrm_ref, gate_ref, up_ref, dgate_ref, dup_ref, dwgu_ref, dwd_ref, ag_scr, au_scr, ad_scr):
        i = pl.program_id(1)

        @pl.when(i == 0)
        def _():
            ag_scr[...] = jnp.zeros_like(ag_scr)
            au_scr[...] = jnp.zeros_like(au_scr)
            ad_scr[...] = jnp.zeros_like(ad_scr)

        nb = nrm_ref[...]
        dhv = dhb_ref[...]
        for jj in range(W_GROUP):
            ag_scr[jj] += _dot_tn(nb, dgate_ref[jj])
            au_scr[jj] += _dot_tn(nb, dup_ref[jj])
            g = gate_ref[jj].astype(F32)
            act = (g * _sigmoid(g) * up_ref[jj].astype(F32)).astype(BF16)
            ad_scr[jj] += _dot_tn(act, dhv)

        @pl.when(i == n_i - 1)
        def _():
            dwgu_ref[0] = ag_scr[...].astype(BF16)
            dwgu_ref[1] = au_scr[...].astype(BF16)
            dwd_ref[...] = ad_scr[...].astype(BF16)

    chunks = pl.BlockSpec((W_GROUP, tm, F_CHUNK), lambda g, i: (g, i, 0))
    rows = pl.BlockSpec((tm, D_MODEL), lambda g, i: (i, 0))
    return pl.pallas_call(
        body, name=name, grid=(N_CHUNK // W_GROUP, n_i),
        in_specs=[rows, rows, chunks, chunks, chunks, chunks],
        out_specs=[pl.BlockSpec((2, W_GROUP, D_MODEL, F_CHUNK), lambda g, i: (0, g, 0, 0)),
                   pl.BlockSpec((W_GROUP, F_CHUNK, D_MODEL), lambda g, i: (g, 0, 0))],
        out_shape=[jax.ShapeDtypeStruct(W_GU_SHAPE, BF16), jax.ShapeDtypeStruct(W_D_SHAPE, BF16)],
        scratch_shapes=[pltpu.VMEM((W_GROUP, D_MODEL, F_CHUNK), F32), pltpu.VMEM((W_GROUP, D_MODEL, F_CHUNK), F32),
                        pltpu.VMEM((W_GROUP, F_CHUNK, D_MODEL), F32)],
        compiler_params=_params(("parallel", "arbitrary")),
    )(dhb, nrm, gate, up, dgate, dup)


N_PIECE = IN_MAIN // 512


def _inproj_fwd(h, gain, w_in):
    n = h.shape[0]
    tm = _row_tile(n, 512)

    def body(h_ref, g_ref, w_ref, *outs):
        y, _, _ = _rms(h_ref[...], g_ref[...])
        nb = y.astype(BF16)
        for p in range(N_PIECE):
            outs[p][...] = _dot(nb, w_ref[:, 512 * p:512 * (p + 1)]).astype(BF16)
        outs[N_PIECE][...] = _dot(nb, w_ref[:, IN_MAIN:IN_PAD])

    piece = pl.BlockSpec((tm, 512), lambda i: (i, 0))
    return pl.pallas_call(
        body, name="inproj_fwd", grid=(n // tm,),
        in_specs=[pl.BlockSpec((tm, D_MODEL), lambda i: (i, 0)),
                  pl.BlockSpec((1, D_MODEL), lambda i: (0, 0)),
                  pl.BlockSpec((D_MODEL, IN_PAD), lambda i: (0, 0))],
        out_specs=[piece] * N_PIECE + [pl.BlockSpec((tm, 128), lambda i: (i, 0))],
        out_shape=[jax.ShapeDtypeStruct((n, 512), BF16)] * N_PIECE + [jax.ShapeDtypeStruct((n, 128), F32)],
        compiler_params=_params(("parallel",)),
    )(h, gain, w_in)


def _inproj_bwd(dpieces, dfg, dh_out, h_in, gain, w_in):
    n = h_in.shape[0]
    tm = _row_tile(n, 512)
    n_i = n // tm

    def body(*refs):
        dp_refs = refs[:N_PIECE]
        dfg_ref, dh_ref, h_ref, g_ref, w_ref, dhin_ref, dw_ref, dgain_ref, acc_scr = refs[N_PIECE:]
        i = pl.program_id(0)

        @pl.when(i == 0)
        def _():
            acc_scr[...] = jnp.zeros_like(acc_scr)
            dgain_ref[...] = jnp.zeros_like(dgain_ref)

        gain_v = g_ref[...]
        y, xhat, r = _rms(h_ref[...], gain_v)
        nb = y.astype(BF16)
        dn = jnp.zeros((tm, D_MODEL), F32)
        for p in range(N_PIECE + 1):
            lo, hi = (512 * p, 512 * (p + 1)) if p < N_PIECE else (IN_MAIN, IN_PAD)
            dp = (dp_refs[p][...] if p < N_PIECE else dfg_ref[...]).astype(BF16)
            dn = dn + _dot_nt(dp, w_ref[:, lo:hi])
            acc_scr[:, lo:hi] += _dot_tn(nb, dp)
        dhin_ref[...] = dh_ref[...] + _rms_bwd(dn, xhat, r, gain_v)
        dgain_ref[...] += jnp.sum(dn * xhat, axis=0, keepdims=True)

        @pl.when(i == n_i - 1)
        def _():
            dw_ref[...] = acc_scr[...].astype(BF16)

    piece = pl.BlockSpec((tm, 512), lambda i: (i, 0))
    rows = pl.BlockSpec((tm, D_MODEL), lambda i: (i, 0))
    vec = pl.BlockSpec((1, D_MODEL), lambda i: (0, 0))
    wspec = pl.BlockSpec((D_MODEL, IN_PAD), lambda i: (0, 0))
    return pl.pallas_call(
        body, name="inproj_bwd", grid=(n_i,),
        in_specs=[piece] * N_PIECE + [pl.BlockSpec((tm, 128), lambda i: (i, 0)), rows, rows, vec, wspec],
        out_specs=[rows, wspec, vec],
        out_shape=[jax.ShapeDtypeStruct((n, D_MODEL), F32),
                   jax.ShapeDtypeStruct((D_MODEL, IN_PAD), BF16),
                   jax.ShapeDtypeStruct((1, D_MODEL), F32)],
        scratch_shapes=[pltpu.VMEM((D_MODEL, IN_PAD), F32)],
        compiler_params=_params(("arbitrary",)),
    )(*dpieces, dfg, dh_out, h_in, gain, w_in)


def _outproj_fwd(zc, za, w_out, h):
    n = h.shape[0]
    tm = _row_tile(n, 512)

    def body(zc_ref, za_ref, w_ref, h_ref, out_ref):
        out_ref[...] = (h_ref[...] + _dot(zc_ref[...], w_ref[0:CONV_DIM, :])
                        + _dot(za_ref[...], w_ref[CONV_DIM:, :]))

    half = pl.BlockSpec((tm, 512), lambda i: (i, 0))
    rows = pl.BlockSpec((tm, D_MODEL), lambda i: (i, 0))
    return pl.pallas_call(
        body, name="outproj_fwd", grid=(n // tm,),
        in_specs=[half, half, pl.BlockSpec((D_MODEL, D_MODEL), lambda i: (0, 0)), rows],
        out_specs=rows,
        out_shape=jax.ShapeDtypeStruct((n, D_MODEL), F32),
        compiler_params=_params(("parallel",)),
    )(zc, za, w_out, h)


def _outproj_bwd(dh, zc, za, w_out):
    n = dh.shape[0]
    tm = _row_tile(n, 512)
    n_i = n // tm

    def body(dh_ref, zc_ref, za_ref, w_ref, dzc_ref, dza_ref, dw_ref, acc_scr):
        i = pl.program_id(0)

        @pl.when(i == 0)
        def _():
            acc_scr[...] = jnp.zeros_like(acc_scr)

        dhb = dh_ref[...].astype(BF16)
        dzc_ref[...] = _dot_nt(dhb, w_ref[0:CONV_DIM, :]).astype(BF16)
        dza_ref[...] = _dot_nt(dhb, w_ref[CONV_DIM:, :]).astype(BF16)
        acc_scr[0:CONV_DIM, :] += _dot_tn(zc_ref[...], dhb)
        acc_scr[CONV_DIM:, :] += _dot_tn(za_ref[...], dhb)

        @pl.when(i == n_i - 1)
        def _():
            dw_ref[...] = acc_scr[...].astype(BF16)

    half = pl.BlockSpec((tm, 512), lambda i: (i, 0))
    wspec = pl.BlockSpec((D_MODEL, D_MODEL), lambda i: (0, 0))
    return pl.pallas_call(
        body, name="outproj_bwd", grid=(n_i,),
        in_specs=[pl.BlockSpec((tm, D_MODEL), lambda i: (i, 0)), half, half, wspec],
        out_specs=[half, half, wspec],
        out_shape=[jax.ShapeDtypeStruct((n, 512), BF16), jax.ShapeDtypeStruct((n, 512), BF16),
                   jax.ShapeDtypeStruct((D_MODEL, D_MODEL), BF16)],
        scratch_shapes=[pltpu.VMEM((D_MODEL, D_MODEL), F32)],
        compiler_params=_params(("arbitrary",)),
    )(dh, zc, za, w_out)


def _group_matrix():
    r = lax.broadcasted_iota(jnp.int32, (128, 128), 0) // HEAD_DIM
    c = lax.broadcasted_iota(jnp.int32, (128, 128), 1) // HEAD_DIM
    return jnp.where(r == c, 1.0 / HEAD_DIM, 0.0).astype(BF16)


def _group_mean(x, gmat):
    hi = x.astype(BF16)
    lo = (x - hi.astype(F32)).astype(BF16)
    return _dot(hi, gmat) + _dot(lo, gmat)


def _shift_rows(x, s):
    rows = x.shape[0]
    t = lax.broadcasted_iota(jnp.int32, x.shape, 0)
    rolled = pltpu.roll(x, s % rows, 0)
    keep = (t >= s) if s > 0 else (t < rows + s)
    return jnp.where(keep, rolled, 0.0)


def _conv_parts(bg_ref, cg_ref, hc_ref, w_ref):
    bg = bg_ref[...].astype(F32)
    cg = cg_ref[...].astype(F32)
    hc = hc_ref[...].astype(F32)
    u = cg * hc
    u1 = _shift_rows(u, 1)
    u2 = _shift_rows(u, 2)
    conv = w_ref[2:3, :] * u + w_ref[1:2, :] * u1 + w_ref[0:1, :] * u2
    return bg, cg, hc, u, u1, u2, conv


def _conv_fwd(bg, cg, hc, conv_w, gain, gmat, lp):
    n = bg.shape[0]
    nb = n // lp

    def body(bg_ref, cg_ref, hc_ref, w_ref, g_ref, gm_ref, z_ref):
        bgv, _, _, _, _, _, conv = _conv_parts(bg_ref, cg_ref, hc_ref, w_ref)
        yc = bgv * conv
        r = lax.rsqrt(_group_mean(yc * yc, gm_ref[...]) + EPS)
        z_ref[...] = (yc * r * g_ref[...]).astype(BF16)

    blk = pl.BlockSpec((lp, 128), lambda c, b: (b, c))
    return pl.pallas_call(
        body, name="conv_fwd", grid=(CONV_DIM // 128, nb),
        in_specs=[blk, blk, blk, pl.BlockSpec((3, 128), lambda c, b: (0, c)),
                  pl.BlockSpec((1, 128), lambda c, b: (0, c)), pl.BlockSpec((128, 128), lambda c, b: (0, 0))],
        out_specs=blk,
        out_shape=jax.ShapeDtypeStruct((n, CONV_DIM), BF16),
        compiler_params=_params(("parallel", "parallel")),
    )(bg, cg, hc, conv_w, gain, gmat)


def _conv_bwd(dz, bg, cg, hc, conv_w, gain, gmat, lp):
    n = bg.shape[0]
    nb = n // lp

    def body(dz_ref, bg_ref, cg_ref, hc_ref, w_ref, g_ref, gm_ref,
             dbg_ref, dcg_ref, dhc_ref, dw_ref, dgain_ref):
        b = pl.program_id(1)

        @pl.when(b == 0)
        def _():
            dw_ref[...] = jnp.zeros_like(dw_ref)
            dgain_ref[...] = jnp.zeros_like(dgain_ref)

        bgv, cgv, hcv, u, u1, u2, conv = _conv_parts(bg_ref, cg_ref, hc_ref, w_ref)
        gm = gm_ref[...]
        yc = bgv * conv
        r = lax.rsqrt(_group_mean(yc * yc, gm) + EPS)
        yhat = yc * r
        dzv = dz_ref[...].astype(F32)
        dyhat = dzv * g_ref[...]
        dgain_ref[...] += jnp.sum(dzv * yhat, axis=0, keepdims=True)
        dyc = r * (dyhat - yhat * _group_mean(dyhat * yhat, gm))
        dbg_ref[...] = (dyc * conv).astype(BF16)
        dconv = dyc * bgv
        du = (w_ref[2:3, :] * dconv + w_ref[1:2, :] * _shift_rows(dconv, -1)
              + w_ref[0:1, :] * _shift_rows(dconv, -2))
        dcg_ref[...] = (du * hcv).astype(BF16)
        dhc_ref[...] = (du * cgv).astype(BF16)
        dw_ref[0:1, :] += jnp.sum(dconv * u2, axis=0, keepdims=True)
        dw_ref[1:2, :] += jnp.sum(dconv * u1, axis=0, keepdims=True)
        dw_ref[2:3, :] += jnp.sum(dconv * u, axis=0, keepdims=True)

    blk = pl.BlockSpec((lp, 128), lambda c, b: (b, c))
    wspec = pl.BlockSpec((3, 128), lambda c, b: (0, c))
    gspec = pl.BlockSpec((1, 128), lambda c, b: (0, c))
    return pl.pallas_call(
        body, name="conv_bwd", grid=(CONV_DIM // 128, nb),
        in_specs=[blk, blk, blk, blk, wspec, gspec, pl.BlockSpec((128, 128), lambda c, b: (0, 0))],
        out_specs=[blk, blk, blk, wspec, gspec],
        out_shape=[jax.ShapeDtypeStruct((n, CONV_DIM), BF16)] * 3
        + [jax.ShapeDtypeStruct((3, CONV_DIM), F32), jax.ShapeDtypeStruct((1, CONV_DIM), F32)],
        compiler_params=_params(("parallel", "arbitrary")),
    )(dz, bg, cg, hc, conv_w, gain, gmat)


KEY_MASKED = 1e30
ONE_LANE = 24


def _scan_steps(rows):
    s, out = 1, []
    while s < rows:
        out.append(s)
        s *= 2
    return out


def _fgate_fwd(fg, b_f, lp):
    n = fg.shape[0]
    nb = n // lp

    def body(fg_ref, b_ref, ka_ref, qa_ref):
        x = fg_ref[...] + b_ref[...]
        logf = jnp.minimum(x, 0.0) - jnp.log(1.0 + jnp.exp(-jnp.abs(x)))
        t = lax.broadcasted_iota(jnp.int32, (lp, 128), 0)
        lane = lax.broadcasted_iota(jnp.int32, (lp, 128), 1)
        f = jnp.where((t >= PAD) & (lane < N_HEADS), logf, 0.0)
        for s in _scan_steps(lp):
            f = f + _shift_rows(f, s)
        hi = f.astype(BF16).astype(F32)
        rest = f - hi
        mid = rest.astype(BF16).astype(F32)
        lo = (rest - mid).astype(BF16).astype(F32)
        ones = jnp.where((lane >= ONE_LANE) & (lane < ONE_LANE + 3), 1.0, 0.0)
        hi_key = jnp.where((t < PAD) & (lane < N_HEADS), KEY_MASKED, hi)
        ka_ref[...] = (hi_key + pltpu.roll(mid, 8, 1) + pltpu.roll(lo, 16, 1) + ones).astype(BF16)
        for h in range(N_HEADS):
            minus = jnp.where((lane == h) | (lane == 8 + h) | (lane == 16 + h), -1.0, 0.0)
            terms = (jnp.where(lane == ONE_LANE, pltpu.roll(hi, ONE_LANE - h, 1), 0.0)
                     + jnp.where(lane == ONE_LANE + 1, pltpu.roll(mid, ONE_LANE + 1 - h, 1), 0.0)
                     + jnp.where(lane == ONE_LANE + 2, pltpu.roll(lo, ONE_LANE + 2 - h, 1), 0.0))
            qa_ref[:, 128 * h:128 * (h + 1)] = (minus + terms).astype(BF16)

    return pl.pallas_call(
        body, name="fgate_fwd", grid=(nb,),
        in_specs=[pl.BlockSpec((lp, 128), lambda b: (b, 0)), pl.BlockSpec((1, 128), lambda b: (0, 0))],
        out_specs=[pl.BlockSpec((lp, 128), lambda b: (b, 0)), pl.BlockSpec((lp, N_HEADS * 128), lambda b: (b, 0))],
        out_shape=[jax.ShapeDtypeStruct((n, 128), BF16), jax.ShapeDtypeStruct((n, N_HEADS * 128), BF16)],
        compiler_params=_params(("parallel",)),
    )(fg, b_f)


def _fgate_bwd(dka, dfr, fg, b_f, lp):
    n = fg.shape[0]
    nb = n // lp

    def body(dka_ref, dfr_ref, fg_ref, b_ref, dfg_ref, db_ref):
        b = pl.program_id(0)

        @pl.when(b == 0)
        def _():
            db_ref[...] = jnp.zeros_like(db_ref)

        wide = jnp.concatenate([dfr_ref[0], jnp.zeros((128 - N_HEADS, lp), F32)], axis=0)
        t = lax.broadcasted_iota(jnp.int32, (lp, 128), 0)
        lane = lax.broadcasted_iota(jnp.int32, (lp, 128), 1)
        d = jnp.where(lane < N_HEADS, dka_ref[...], 0.0) + wide.T
        for s in _scan_steps(lp):
            d = d + _shift_rows(d, -s)
        x = fg_ref[...] + b_ref[...]
        dx = jnp.where((t >= PAD) & (lane < N_HEADS), d * _sigmoid(-x), 0.0)
        dfg_ref[...] = dx
        db_ref[...] += jnp.sum(dx, axis=0, keepdims=True)

    return pl.pallas_call(
        body, name="fgate_bwd", grid=(nb,),
        in_specs=[pl.BlockSpec((lp, 128), lambda b: (b, 0)), pl.BlockSpec((1, N_HEADS, lp), lambda b: (b, 0, 0)),
                  pl.BlockSpec((lp, 128), lambda b: (b, 0)), pl.BlockSpec((1, 128), lambda b: (0, 0))],
        out_specs=[pl.BlockSpec((lp, 128), lambda b: (b, 0)), pl.BlockSpec((1, 128), lambda b: (0, 0))],
        out_shape=[jax.ShapeDtypeStruct((n, 128), F32), jax.ShapeDtypeStruct((1, 128), F32)],
        compiler_params=_params(("arbitrary",)),
    )(dka, dfr, fg, b_f)


def _head_masks():
    lane = lax.broadcasted_iota(jnp.int32, (1, 128), 1)
    return lane < HEAD_DIM


def _stack_heads(x2, first):
    zero = jnp.zeros_like(x2)
    return jnp.concatenate([jnp.where(first, x2, zero), jnp.where(first, zero, x2)], axis=0)


def _stack_heads_lanes(xt):
    r = lax.broadcasted_iota(jnp.int32, xt.shape, 0)
    zero = jnp.zeros_like(xt)
    return jnp.concatenate([jnp.where(r < HEAD_DIM, xt, zero), jnp.where(r < HEAD_DIM, zero, xt)], axis=1)


def _pair_cols(col0, col1, first):
    return jnp.where(first, col0, col1)


def _pair_rows(row0, row1):
    r = lax.broadcasted_iota(jnp.int32, (128, TQ), 0)
    return jnp.where(r < HEAD_DIM, row0, row1)


def _query_side(q_ref, qa_ref, p, first):
    q2 = q_ref[:, 128 * p:128 * (p + 1)] * 0.125
    zero = jnp.zeros_like(q2)
    top = jnp.concatenate([jnp.where(first, q2, zero), qa_ref[:, 128 * (2 * p):128 * (2 * p + 1)]], axis=1)
    bot = jnp.concatenate([jnp.where(first, zero, q2), qa_ref[:, 128 * (2 * p + 1):128 * (2 * p + 2)]], axis=1)
    return jnp.concatenate([top, bot], axis=0)


def _key_chunks(lp):
    return (lp + TK - 1) // TK


def _chunk_mask(i, c):
    r = lax.broadcasted_iota(jnp.int32, (TK, 2 * TQ), 0)
    col = lax.broadcasted_iota(jnp.int32, (TK, 2 * TQ), 1)
    return (c * TK + r) <= (i * TQ + (col & (TQ - 1)))


def _transpose_bf16(x):
    return x.astype(F32).T.astype(BF16)


def _attn_fwd(q, qa, k, v, ka, gain, lp):
    n = q.shape[0]
    nb = n // lp
    nq = lp // TQ
    lpp = _key_chunks(lp) * TK

    def body(q_ref, qa_ref, k_ref, v_ref, ka_ref, g_ref, z_ref, o_ref, lse_ref, kx_scr, vt_scr):
        i = pl.program_id(1)
        first = _head_masks()

        @pl.when(i == 0)
        def _():
            if lpp > lp:
                kx_scr[lp:lpp, :] = jnp.zeros((lpp - lp, 2 * ATTN_DIM), BF16)
                vt_scr[:, lp:lpp] = jnp.zeros((ATTN_DIM, lpp - lp), BF16)
            for p in range(N_PAIRS):
                kx_scr[0:lp, 256 * p:256 * p + 128] = k_ref[:, 128 * p:128 * (p + 1)]
                kx_scr[0:lp, 256 * p + 128:256 * (p + 1)] = ka_ref[...]
            vt_scr[:, 0:lp] = _transpose_bf16(v_ref[...])

        rhs_t = [_transpose_bf16(_query_side(q_ref, qa_ref, p, first)) for p in range(N_PAIRS)]

        def step(c, carry):
            koff = pl.multiple_of(c * TK, TK)
            valid = _chunk_mask(i, c)
            new = []
            for p in range(N_PAIRS):
                m, l, acc = carry[p]
                st = _dot(kx_scr[pl.ds(koff, TK), 256 * p:256 * (p + 1)], rhs_t[p])
                st = jnp.where(valid, st, NEG)
                m_new = jnp.maximum(m, jnp.max(st, axis=0, keepdims=True))
                pt = jnp.exp(st - m_new)
                alpha = jnp.exp(m - m_new)
                l = alpha * l + jnp.sum(pt, axis=0, keepdims=True)
                pb = pt.astype(BF16)
                vt = _stack_heads_lanes(vt_scr[128 * p:128 * (p + 1), pl.ds(koff, TK)])
                pv = _dot(vt, jnp.concatenate([pb[:, 0:TQ], pb[:, TQ:]], axis=0))
                acc = acc * _pair_rows(alpha[:, 0:TQ], alpha[:, TQ:]) + pv
                new.append((m_new, l, acc))
            return tuple(new)

        init = tuple((jnp.full((1, 2 * TQ), NEG, F32), jnp.zeros((1, 2 * TQ), F32), jnp.zeros((128, TQ), F32))
                     for _ in range(N_PAIRS))
        final = lax.fori_loop(0, (i + 2) // 2, step, init)

        row = lax.broadcasted_iota(jnp.int32, (TQ, 128), 0)
        real = (i * TQ + row) >= PAD
        for p in range(N_PAIRS):
            m, l, acc = final[p]
            inv = 1.0 / l
            ot = acc * _pair_rows(inv[:, 0:TQ], inv[:, TQ:])
            sq = ot * ot
            r0 = lax.rsqrt(jnp.sum(sq[0:HEAD_DIM], axis=0, keepdims=True) * (1.0 / HEAD_DIM) + EPS)
            r1 = lax.rsqrt(jnp.sum(sq[HEAD_DIM:], axis=0, keepdims=True) * (1.0 / HEAD_DIM) + EPS)
            cols = slice(128 * p, 128 * (p + 1))
            o_ref[:, cols] = jnp.where(real, ot.T, 0.0).astype(BF16)
            z_ref[:, cols] = (jnp.where(real, (ot * _pair_rows(r0, r1)).T, 0.0) * g_ref[:, cols]).astype(BF16)
            lse = m + jnp.log(l)
            lse_ref[0, 2 * p:2 * p + 1, :] = lse[:, 0:TQ]
            lse_ref[0, 2 * p + 1:2 * p + 2, :] = lse[:, TQ:]

    qblk = pl.BlockSpec((TQ, ATTN_DIM), lambda b, i: (b * nq + i, 0))
    qablk = pl.BlockSpec((TQ, N_HEADS * 128), lambda b, i: (b * nq + i, 0))
    seq = pl.BlockSpec((lp, ATTN_DIM), lambda b, i: (b, 0))
    rowblk = pl.BlockSpec((1, N_HEADS, TQ), lambda b, i: (b, 0, i))
    return pl.pallas_call(
        body, name="attn_fwd", grid=(nb, nq),
        in_specs=[qblk, qablk, seq, seq, pl.BlockSpec((lp, 128), lambda b, i: (b, 0)),
                  pl.BlockSpec((1, ATTN_DIM), lambda b, i: (0, 0))],
        out_specs=[qblk, qblk, rowblk],
        out_shape=[jax.ShapeDtypeStruct((n, ATTN_DIM), BF16), jax.ShapeDtypeStruct((n, ATTN_DIM), BF16),
                   jax.ShapeDtypeStruct((nb, N_HEADS, lp), F32)],
        scratch_shapes=[pltpu.VMEM((lpp, 2 * ATTN_DIM), BF16), pltpu.VMEM((ATTN_DIM, lpp), BF16)],
        compiler_params=_params(("parallel", "arbitrary")),
    )(q, qa, k, v, ka, gain)


def _attn_bwd(dz, q, qa, k, v, ka, o, lse, gain, lp, exchange=()):
    n = q.shape[0]
    nb = n // lp
    nq = lp // TQ
    lpp = _key_chunks(lp) * TK
    nw = len(exchange)

    def body(*refs):
        ((dz_ref, q_ref, qa_ref, k_ref, v_ref, ka_ref, o_ref, lse_ref, g_ref), xin,
         (dq_ref, dk_ref, dv_ref, dka_ref, dfr_ref, dgain_ref), xout,
         (kx_scr, vx_scr, kt_scr, dkx_scr, dvx_scr), sems) = _split_refs(refs, 9, nw, 6, 5)
        b = pl.program_id(0)
        i = pl.program_id(1)
        first = _head_masks()
        if nw:
            comm = _Exchange(xin, xout, sems)
            pl.when((b == 0) & (i == 0))(comm.start)

        @pl.when((b == 0) & (i == 0))
        def _():
            dgain_ref[...] = jnp.zeros_like(dgain_ref)

        @pl.when(i == 0)
        def _():
            if lpp > lp:
                kx_scr[lp:lpp, :] = jnp.zeros((lpp - lp, 2 * ATTN_DIM), BF16)
                vx_scr[lp:lpp, :] = jnp.zeros((lpp - lp, ATTN_DIM), BF16)
                kt_scr[:, lp:lpp] = jnp.zeros((ATTN_DIM, lpp - lp), BF16)
            for p in range(N_PAIRS):
                kx_scr[0:lp, 256 * p:256 * p + 128] = k_ref[:, 128 * p:128 * (p + 1)]
                kx_scr[0:lp, 256 * p + 128:256 * (p + 1)] = ka_ref[...]
            vx_scr[0:lp, :] = v_ref[...]
            kt_scr[:, 0:lp] = _transpose_bf16(k_ref[...])
            dkx_scr[...] = jnp.zeros_like(dkx_scr)
            dvx_scr[...] = jnp.zeros_like(dvx_scr)

        rhs, rhs_t, lses, dos, dos_t, deltas = [], [], [], [], [], []
        for p in range(N_PAIRS):
            cols = slice(128 * p, 128 * (p + 1))
            side = _query_side(q_ref, qa_ref, p, first)
            rhs.append(side)
            rhs_t.append(_transpose_bf16(side))
            lses.append(jnp.concatenate([lse_ref[0, 2 * p:2 * p + 1, :], lse_ref[0, 2 * p + 1:2 * p + 2, :]], axis=1))
            ov = o_ref[:, cols].astype(F32)
            dzv = dz_ref[:, cols].astype(F32)
            gv = g_ref[:, cols]
            sq = ov * ov
            ms0 = jnp.sum(jnp.where(first, sq, 0.0), axis=1, keepdims=True) * (1.0 / HEAD_DIM)
            ms1 = jnp.sum(jnp.where(first, 0.0, sq), axis=1, keepdims=True) * (1.0 / HEAD_DIM)
            r = _pair_cols(lax.rsqrt(ms0 + EPS), lax.rsqrt(ms1 + EPS), first)
            ohat = ov * r
            dyhat = dzv * gv
            dgain_ref[:, cols] += jnp.sum(dzv * ohat, axis=0, keepdims=True)
            pr = dyhat * ohat
            mean0 = jnp.sum(jnp.where(first, pr, 0.0), axis=1, keepdims=True) * (1.0 / HEAD_DIM)
            mean1 = jnp.sum(jnp.where(first, 0.0, pr), axis=1, keepdims=True) * (1.0 / HEAD_DIM)
            do = r * (dyhat - ohat * _pair_cols(mean0, mean1, first))
            ddt = (do * ov).T
            deltas.append(jnp.concatenate([jnp.sum(ddt[0:HEAD_DIM], axis=0, keepdims=True),
                                           jnp.sum(ddt[HEAD_DIM:], axis=0, keepdims=True)], axis=1))
            do_st = _stack_heads(do.astype(BF16), first)
            dos.append(do_st)
            dos_t.append(_transpose_bf16(do_st))

        def step(c, carry):
            koff = pl.multiple_of(c * TK, TK)
            valid = _chunk_mask(i, c)
            new = []
            for p in range(N_PAIRS):
                dqt, dfq = carry[p]
                ext = slice(256 * p, 256 * (p + 1))
                cols = slice(128 * p, 128 * (p + 1))
                st = _dot(kx_scr[pl.ds(koff, TK), ext], rhs_t[p])
                st = jnp.where(valid, st, NEG)
                pt = jnp.exp(st - lses[p])
                dpt = _dot(vx_scr[pl.ds(koff, TK), cols], dos_t[p])
                dst = pt * (dpt - deltas[p])
                dsb = dst.astype(BF16)
                dfq = dfq + jnp.sum(dsb.astype(F32), axis=0, keepdims=True)
                dkx_scr[pl.ds(koff, TK), ext] += _dot(dsb, rhs[p])
                dvx_scr[pl.ds(koff, TK), cols] += _dot(pt.astype(BF16), dos[p])
                kt = _stack_heads_lanes(kt_scr[cols, pl.ds(koff, TK)])
                dqt = dqt + _dot(kt, jnp.concatenate([dsb[:, 0:TQ], dsb[:, TQ:]], axis=0))
                new.append((dqt, dfq))
            return tuple(new)

        init = tuple((jnp.zeros((128, TQ), F32), jnp.zeros((1, 2 * TQ), F32)) for _ in range(N_PAIRS))
        final = lax.fori_loop(0, (i + 2) // 2, step, init)

        for p in range(N_PAIRS):
            dqt, dfq = final[p]
            dq_ref[:, 128 * p:128 * (p + 1)] = (dqt.T * 0.125).astype(BF16)
            dfr_ref[0, 2 * p:2 * p + 1, :] = dfq[:, 0:TQ]
            dfr_ref[0, 2 * p + 1:2 * p + 2, :] = dfq[:, TQ:]

        @pl.when(i == nq - 1)
        def _():
            dka = jnp.zeros((lp, 128), F32)
            for p in range(N_PAIRS):
                dk_ref[:, 128 * p:128 * (p + 1)] = dkx_scr[0:lp, 256 * p:256 * p + 128].astype(BF16)
                dka = dka + dkx_scr[0:lp, 256 * p + 128:256 * (p + 1)]
            dka_ref[...] = dka
            dv_ref[...] = dvx_scr[0:lp, :].astype(BF16)

        if nw:
            pl.when((b == nb - 1) & (i == nq - 1))(comm.finish)

    qblk = pl.BlockSpec((TQ, ATTN_DIM), lambda b, i: (b * nq + i, 0))
    qablk = pl.BlockSpec((TQ, N_HEADS * 128), lambda b, i: (b * nq + i, 0))
    seq = pl.BlockSpec((lp, ATTN_DIM), lambda b, i: (b, 0))
    kaseq = pl.BlockSpec((lp, 128), lambda b, i: (b, 0))
    rowblk = pl.BlockSpec((1, N_HEADS, TQ), lambda b, i: (b, 0, i))
    gspec = pl.BlockSpec((1, ATTN_DIM), lambda b, i: (0, 0))
    return pl.pallas_call(
        body, name="attn_bwd", grid=(nb, nq),
        in_specs=[qblk, qblk, qablk, seq, seq, kaseq, qblk, rowblk, gspec] + [ANY] * nw,
        out_specs=[qblk, seq, seq, kaseq, rowblk, gspec] + [ANY] * nw,
        out_shape=[jax.ShapeDtypeStruct((n, ATTN_DIM), BF16), jax.ShapeDtypeStruct((n, ATTN_DIM), BF16),
                   jax.ShapeDtypeStruct((n, ATTN_DIM), BF16), jax.ShapeDtypeStruct((n, 128), F32),
                   jax.ShapeDtypeStruct((nb, N_HEADS, lp), F32), jax.ShapeDtypeStruct((1, ATTN_DIM), F32)]
        + [jax.ShapeDtypeStruct(a.shape, a.dtype) for a in exchange],
        scratch_shapes=[pltpu.VMEM((lpp, 2 * ATTN_DIM), BF16), pltpu.VMEM((lpp, ATTN_DIM), BF16),
                        pltpu.VMEM((ATTN_DIM, lpp), BF16), pltpu.VMEM((lpp, 2 * ATTN_DIM), F32),
                        pltpu.VMEM((lpp, ATTN_DIM), F32)] + (_comm_sems(nw) if nw else []),
        compiler_params=_params(("arbitrary", "arbitrary")),
    )(dz, q, qa, k, v, ka, o, lse, gain, *exchange)


def _loss_head(h, gain, target, lp):
    n = h.shape[0]
    nb = n // lp
    nq = lp // 128

    def body(h_ref, g_ref, t_ref, loss_ref, dh_ref, dgain_ref):
        b = pl.program_id(0)
        i = pl.program_id(1)

        @pl.when((b == 0) & (i == 0))
        def _():
            loss_ref[...] = jnp.zeros_like(loss_ref)
            dgain_ref[...] = jnp.zeros_like(dgain_ref)

        @pl.when(i == 0)
        def _():
            dh_ref[...] = jnp.zeros_like(dh_ref)

        @pl.when(i > 0)
        def _():
            gain_v = g_ref[...]
            y, xhat, r = _rms(h_ref[...], gain_v)
            err = y - t_ref[...]
            loss_ref[...] += 0.5 * jnp.sum(jnp.sum(err * err, axis=1, keepdims=True), axis=0,
                                           keepdims=True) * (1.0 / D_MODEL)
            dy = err * (1.0 / D_MODEL)
            dh_ref[...] = _rms_bwd(dy, xhat, r, gain_v)
            dgain_ref[...] += jnp.sum(dy * xhat, axis=0, keepdims=True)

    rows = pl.BlockSpec((128, D_MODEL), lambda b, i: (b * nq + i, 0))
    trows = pl.BlockSpec((128, D_MODEL), lambda b, i: (b * (nq - 1) + jnp.maximum(i, 1) - 1, 0))
    return pl.pallas_call(
        body, name="loss_head", grid=(nb, nq),
        in_specs=[rows, pl.BlockSpec((1, D_MODEL), lambda b, i: (0, 0)), trows],
        out_specs=[pl.BlockSpec((1, 1), lambda b, i: (0, 0)), rows, pl.BlockSpec((1, D_MODEL), lambda b, i: (0, 0))],
        out_shape=[jax.ShapeDtypeStruct((1, 1), F32), jax.ShapeDtypeStruct((n, D_MODEL), F32),
                   jax.ShapeDtypeStruct((1, D_MODEL), F32)],
        compiler_params=_params(("arbitrary", "arbitrary")),
    )(h, gain, target)


def _adamw(parts, w, m, v, name):
    s_parts, r, c = parts.shape
    tr = r
    for t in (256, 128, 64, 32, 16):
        if r % t == 0 and r > t:
            tr = t
            break

    def body(p_ref, w_ref, m_ref, v_ref, g_ref, d_ref, nm_ref, nv_ref):
        g = p_ref[0].astype(F32)
        for s in range(1, s_parts):
            g = g + p_ref[s].astype(F32)
        nm = ADAM_B1 * m_ref[...] + (1.0 - ADAM_B1) * g
        nv = ADAM_B2 * v_ref[...] + (1.0 - ADAM_B2) * (g * g)
        m_hat = nm / (1.0 - ADAM_B1 ** ADAM_STEP)
        v_hat = nv / (1.0 - ADAM_B2 ** ADAM_STEP)
        g_ref[...] = g
        d_ref[...] = -ADAM_LR * (m_hat / (jnp.sqrt(v_hat) + ADAM_EPS) + ADAM_WD * w_ref[...])
        nm_ref[...] = nm
        nv_ref[...] = nv

    blk = pl.BlockSpec((tr, c), lambda i: (i, 0))
    return pl.pallas_call(
        body, name=name, grid=(r // tr,),
        in_specs=[pl.BlockSpec((s_parts, tr, c), lambda i: (0, i, 0)), blk, blk, blk],
        out_specs=[blk] * 4,
        out_shape=[jax.ShapeDtypeStruct((r, c), F32)] * 4,
        compiler_params=_params(("parallel",)),
    )(parts, w, m, v)


def _sum_parts(parts, name):
    s_parts, r, c = parts.shape

    def body(p_ref, out_ref):
        acc = p_ref[0]
        for s in range(1, s_parts):
            acc = acc + p_ref[s]
        out_ref[...] = acc

    return pl.pallas_call(
        body, name=name, out_shape=jax.ShapeDtypeStruct((r, c), F32),
        in_specs=[pl.BlockSpec(memory_space=pltpu.VMEM)], out_specs=pl.BlockSpec(memory_space=pltpu.VMEM),
    )(parts)


SMALL_ROWS = 184


def _pack_small(d_gains, d_gc, d_ga, d_bf, d_conv, d_meta):
    rows = [g.reshape(8, 128) for g in d_gains]
    rows += [d_gc.reshape(4, 128), d_ga.reshape(4, 128), d_bf.reshape(1, 128)]
    rows += [d_conv.reshape(12, 128), d_meta.reshape(128, 128)]
    packed = jnp.concatenate(rows, axis=0)
    return jnp.pad(packed, ((0, SMALL_ROWS - packed.shape[0]), (0, 0)))


def kernel(x, meta_tokens, ffn1_norm, ffn1_w_gu, ffn1_w_down, mix_norm, w_in, conv_w, b_f, out_norm_conv, out_norm_attn, w_out, ffn2_norm, ffn2_w_gu, ffn2_w_down, final_norm, loss_target, m_meta_tokens, m_ffn1_norm, m_ffn1_w_gu, m_ffn1_w_down, m_mix_norm, m_w_in, m_conv_w, m_b_f, m_out_norm_conv, m_out_norm_attn, m_w_out, m_ffn2_norm, m_ffn2_w_gu, m_ffn2_w_down, m_final_norm, v_meta_tokens, v_ffn1_norm, v_ffn1_w_gu, v_ffn1_w_down, v_mix_norm, v_w_in, v_conv_w, v_b_f, v_out_norm_conv, v_out_norm_attn, v_w_out, v_ffn2_norm, v_ffn2_w_gu, v_ffn2_w_down, v_final_norm):
    nb, seq, _ = x.shape
    lp = PAD + N_META + seq
    n = nb * lp
    me = 4 * lax.axis_index("x") + 2 * lax.axis_index("y") + lax.axis_index("c")

    wgu1_8, wd1_8 = _all_gather([ffn1_w_gu[0].astype(BF16), ffn1_w_down[0].astype(BF16)], "gather_ffn1")
    small_in = jnp.concatenate(
        [meta_tokens, jnp.pad(conv_w[0], ((0, 0), (0, 128 - conv_w.shape[2]))), jnp.zeros((5, 128), F32)], axis=0)
    (small_8,) = _all_gather([small_in], "gather_small")
    meta_full = small_8[:, 0:N_META, :].transpose(1, 0, 2).reshape(N_META, D_MODEL)
    conv_full = small_8[:, N_META:N_META + 3, 0:CONV_DIM // N_DEV].transpose(1, 0, 2).reshape(3, CONV_DIM)
    wgu1 = wgu1_8.reshape(2, N_CHUNK, D_MODEL, F_CHUNK)
    wd1 = wd1_8.reshape(N_CHUNK, F_CHUNK, D_MODEL)
    b_f_row = jnp.pad(b_f, ((0, 0), (0, 128 - N_HEADS)))
    gmat = _group_matrix()

    h0 = jnp.concatenate([jnp.zeros((nb, PAD, D_MODEL), F32),
                          jnp.broadcast_to(meta_full[None], (nb, N_META, D_MODEL)), x], axis=1).reshape(n, D_MODEL)
    later = [w_in[0].astype(BF16), w_out[0].astype(BF16), ffn2_w_gu[0].astype(BF16), ffn2_w_down[0].astype(BF16)]
    h1, n1, gate1, up1, win_8, wout_8, wgu2_8, wd2_8 = _ffn_fwd(h0, ffn1_norm, wgu1, wd1, "ffn1_fwd", gather=later)
    wgu2 = wgu2_8.reshape(2, N_CHUNK, D_MODEL, F_CHUNK)
    wd2 = wd2_8.reshape(N_CHUNK, F_CHUNK, D_MODEL)
    w_in_full = jnp.pad(win_8.transpose(1, 0, 2).reshape(D_MODEL, IN_DIM), ((0, 0), (0, IN_PAD - IN_DIM)))
    w_out_full = wout_8.reshape(D_MODEL, D_MODEL)

    bg, cg, hc, q, k, v, fg = _inproj_fwd(h1, mix_norm, w_in_full)
    zc = _conv_fwd(bg, cg, hc, conv_full, out_norm_conv, gmat, lp)
    ka, qa = _fgate_fwd(fg, b_f_row, lp)
    za, o, lse = _attn_fwd(q, qa, k, v, ka, out_norm_attn, lp)
    h2 = _outproj_fwd(zc, za, w_out_full, h1)
    h3, n3, gate2, up2 = _ffn_fwd(h2, ffn2_norm, wgu2, wd2, "ffn2_fwd")
    loss_part, dh3, d_final = _loss_head(h3, final_norm.reshape(1, D_MODEL), loss_target.reshape(nb * seq, D_MODEL), lp)

    dh2, dhb3, dgate2, dup2, d_ffn2 = _ffn_bwd_x(dh3, h2, ffn2_norm, gate2, up2, wgu2, wd2, "ffn2_bwd_x")
    dwgu2, dwd2 = _ffn_bwd_w(dhb3, n3, gate2, up2, dgate2, dup2, "ffn2_bwd_w")
    dzc, dza, dwout = _outproj_bwd(dh2, zc, za, w_out_full)
    send_a = [dwgu2.reshape(N_DEV, D_MODEL, F_CHUNK), dwd2.reshape(N_DEV, F_CHUNK // 2, D_MODEL),
              dwout.reshape(N_DEV, D_MODEL // N_DEV, D_MODEL)]
    dq, dk, dv, dka, dfr, d_ga, p_wgu2, p_wd2, p_wout = _attn_bwd(
        dza, q, qa, k, v, ka, o, lse, out_norm_attn, lp, exchange=send_a)
    dfg, d_bf = _fgate_bwd(dka, dfr, fg, b_f_row, lp)
    dbg, dcg, dhc, d_conv, d_gc = _conv_bwd(dzc, bg, cg, hc, conv_full, out_norm_conv, gmat, lp)
    dh1, dwin, d_mix = _inproj_bwd([dbg, dcg, dhc, dq, dk, dv], dfg, dh2, h1, mix_norm, w_in_full)
    dwin_8 = dwin[:, 0:IN_DIM].reshape(D_MODEL, N_DEV, IN_DIM // N_DEV).transpose(1, 0, 2)
    dhb1, dgate1, dup1, p_win = _ffn_bwd_act(dh1, gate1, up1, wd1, "ffn1_bwd_act", exchange=[dwin_8])
    dwgu1, dwd1 = _ffn_bwd_w(dhb1, n1, gate1, up1, dgate1, dup1, "ffn1_bwd_w")
    own = [dwgu1.reshape(N_DEV, D_MODEL, F_CHUNK), dwd1.reshape(N_DEV, F_CHUNK // 2, D_MODEL)]
    got = _pair_exchange(own, "pair_exchange_ffn1")
    chip_sums = [_pair_sum(own[0], got[0], "pair_sum_wgu1"), _pair_sum(own[1], got[1], "pair_sum_wd1")]
    dh0, d_ffn1, p_wgu1, p_wd1 = _ffn_bwd_in(
        dh1, h0, ffn1_norm, dgate1, dup1, wgu1, "ffn1_bwd_in", exchange=chip_sums)

    dh0 = dh0.reshape(nb, lp, D_MODEL)
    grad_x = dh0[:, PAD + N_META:, :]
    d_meta = jnp.sum(dh0[:, PAD:PAD + N_META, :], axis=0)

    small = _pack_small([d_ffn1, d_mix, d_ffn2, d_final], d_gc, d_ga, d_bf, d_conv, d_meta)
    (small_all,) = _all_gather([small], "gather_small_grads")
    small_sum = _sum_parts(small_all, "sum_small_grads")
    g_ffn1n, g_mixn, g_ffn2n, g_finaln = (small_sum[8 * t:8 * t + 8].reshape(1, D_MODEL) for t in range(4))
    g_gc = small_sum[32:36].reshape(1, CONV_DIM)
    g_ga = small_sum[36:40].reshape(1, ATTN_DIM)
    g_bf = small_sum[40:41, 0:N_HEADS]
    g_conv_full = small_sum[41:53].reshape(3, CONV_DIM)
    g_meta_full = small_sum[53:181].reshape(N_META, D_MODEL)
    g_conv = lax.dynamic_slice_in_dim(g_conv_full, me * (CONV_DIM // N_DEV), CONV_DIM // N_DEV, axis=1)
    g_meta = lax.dynamic_slice_in_dim(g_meta_full, me * (D_MODEL // N_DEV), D_MODEL // N_DEV, axis=1)

    weights = {
        "meta_tokens": (g_meta[None], meta_tokens, m_meta_tokens, v_meta_tokens),
        "ffn1_norm": (g_ffn1n[None], ffn1_norm, m_ffn1_norm, v_ffn1_norm),
        "ffn1_w_gu": (p_wgu1, ffn1_w_gu[0], m_ffn1_w_gu[0], v_ffn1_w_gu[0]),
        "ffn1_w_down": (p_wd1, ffn1_w_down[0], m_ffn1_w_down[0], v_ffn1_w_down[0]),
        "mix_norm": (g_mixn[None], mix_norm, m_mix_norm, v_mix_norm),
        "w_in": (p_win, w_in[0], m_w_in[0], v_w_in[0]),
        "conv_w": (g_conv[None], conv_w[0], m_conv_w[0], v_conv_w[0]),
        "b_f": (g_bf[None], b_f, m_b_f, v_b_f),
        "out_norm_conv": (g_gc[None], out_norm_conv, m_out_norm_conv, v_out_norm_conv),
        "out_norm_attn": (g_ga[None], out_norm_attn, m_out_norm_attn, v_out_norm_attn),
        "w_out": (p_wout, w_out[0], m_w_out[0], v_w_out[0]),
        "ffn2_norm": (g_ffn2n[None], ffn2_norm, m_ffn2_norm, v_ffn2_norm),
        "ffn2_w_gu": (p_wgu2, ffn2_w_gu[0], m_ffn2_w_gu[0], v_ffn2_w_gu[0]),
        "ffn2_w_down": (p_wd2, ffn2_w_down[0], m_ffn2_w_down[0], v_ffn2_w_down[0]),
        "final_norm": (g_finaln[None], final_norm.reshape(1, D_MODEL), m_final_norm.reshape(1, D_MODEL),
                       v_final_norm.reshape(1, D_MODEL)),
    }
    shapes = {"meta_tokens": meta_tokens.shape, "ffn1_norm": ffn1_norm.shape, "ffn1_w_gu": ffn1_w_gu.shape,
              "ffn1_w_down": ffn1_w_down.shape, "mix_norm": mix_norm.shape, "w_in": w_in.shape,
              "conv_w": conv_w.shape, "b_f": b_f.shape, "out_norm_conv": out_norm_conv.shape,
              "out_norm_attn": out_norm_attn.shape, "w_out": w_out.shape, "ffn2_norm": ffn2_norm.shape,
              "ffn2_w_gu": ffn2_w_gu.shape, "ffn2_w_down": ffn2_w_down.shape, "final_norm": final_norm.shape}
    grads, deltas, new_m, new_v = [], [], [], []
    for name, (p, w, m, vv) in weights.items():
        g, d, nm, nv = _adamw(p, w, m, vv, "adamw_" + name)
        shape = shapes[name]
        grads.append(g.reshape(shape))
        deltas.append(d.reshape(shape))
        new_m.append(nm.reshape(shape))
        new_v.append(nv.reshape(shape))

    loss = lax.psum(loss_part[0, 0], ("x", "y", "c"))
    return (loss, grad_x, *grads, *deltas, *new_m, *new_v)
```

```python
import jax
import jax.numpy as jnp
from jax import lax
from jax.experimental import pallas as pl
from jax.experimental.pallas import tpu as pltpu

F32 = jnp.float32
BF16 = jnp.bfloat16

N_DEV = 8
D_MODEL = 1024
N_META = 16
PAD = 128 - N_META
CONV_DIM = 512
ATTN_DIM = 512
HEAD_DIM = 64
N_HEADS = 8
N_PAIRS = N_HEADS // 2
D_FF = 2816
N_CHUNK = 4
F_CHUNK = D_FF // N_CHUNK
IN_DIM = 3080
IN_PAD = 3200
IN_MAIN = 3072
EPS = 1e-6
NEG = -1e30
TQ = 128
TK = 256
VMEM_LIMIT = 56 * 1024 * 1024

ADAM_LR = 0.001
ADAM_B1 = 0.9
ADAM_B2 = 0.999
ADAM_EPS = 1e-08
ADAM_WD = 0.01
ADAM_STEP = 10

MESH = pl.DeviceIdType.MESH
ANY = pl.BlockSpec(memory_space=pl.ANY)


def _params(sem=None):
    return pltpu.CompilerParams(dimension_semantics=sem, vmem_limit_bytes=VMEM_LIMIT)


def _row_tile(n, prefer):
    for t in (prefer, 512, 256, 128):
        if t <= n and n % t == 0:
            return t
    raise ValueError(f"no row tile for {n}")


def _dot(a, b):
    return jnp.dot(a, b, preferred_element_type=F32)


def _dot_nt(a, b):
    return lax.dot_general(a, b, (((1,), (1,)), ((), ())), preferred_element_type=F32)


def _dot_tn(a, b):
    return lax.dot_general(a, b, (((0,), (0,)), ((), ())), preferred_element_type=F32)


def _rms(x, g):
    r = lax.rsqrt(jnp.mean(x * x, axis=-1, keepdims=True) + EPS)
    xhat = x * r
    return xhat * g, xhat, r


def _rms_bwd(dn, xhat, r, g):
    dxhat = dn * g
    return r * (dxhat - xhat * jnp.mean(dxhat * xhat, axis=-1, keepdims=True))


def _sigmoid(x):
    return 1.0 / (1.0 + jnp.exp(-x))


def _place():
    return lax.axis_index("x"), lax.axis_index("y"), lax.axis_index("c")


def _comm_sems(nw):
    return [pltpu.SemaphoreType.DMA((nw, 7)), pltpu.SemaphoreType.DMA((nw, 7)), pltpu.SemaphoreType.DMA((nw,))]


class _Gather:
    def __init__(self, ins, outs, sems):
        self.ins, self.outs = ins, outs
        self.send, self.recv, self.local = sems
        x, y, c = _place()
        self.c = c
        self.me, self.sibling = (x, y, c), (x, y, 1 - c)
        self.chips = [(1 - x, y), (x, 1 - y), (1 - x, 1 - y)]

    def _copy(self, w, k, block, to, own=False):
        slot = self.outs[w].at[4 * block[0] + 2 * block[1] + block[2]]
        return pltpu.make_async_remote_copy(
            src_ref=self.ins[w] if own else slot, dst_ref=slot,
            send_sem=self.send.at[w, k], recv_sem=self.recv.at[w, k], device_id=to, device_id_type=MESH)

    def _mine(self, w):
        x, y, c = self.me
        return pltpu.make_async_copy(self.ins[w], self.outs[w].at[4 * x + 2 * y + c], self.local.at[w])

    def _first(self, w):
        return ([self._copy(w, 0, self.me, self.sibling, own=True)]
                + [self._copy(w, 1 + j, self.me, (*chip, self.c), own=True) for j, chip in enumerate(self.chips)])

    def _passed(self, w):
        return [self._copy(w, 4 + j, (*chip, self.c), self.sibling) for j, chip in enumerate(self.chips)]

    def start(self):
        for w in range(len(self.ins)):
            self._mine(w).start()
        for w in range(len(self.ins)):
            for cp in self._first(w):
                cp.start()

    def forward(self):
        for w in range(len(self.ins)):
            for j, chip in enumerate(self.chips):
                self._copy(w, 1 + j, (*chip, self.c), self.me).wait_recv()
                self._passed(w)[j].start()

    def finish(self):
        for w in range(len(self.ins)):
            self._copy(w, 0, self.sibling, self.me).wait_recv()
            for j, chip in enumerate(self.chips):
                self._copy(w, 4 + j, (*chip, 1 - self.c), self.me).wait_recv()
        for w in range(len(self.ins)):
            for cp in self._first(w) + self._passed(w):
                cp.wait_send()
            self._mine(w).wait()


class _Exchange:
    def __init__(self, ins, outs, sems):
        self.ins, self.outs = ins, outs
        self.send, self.recv, self.local = sems
        self.x, self.y, self.c = _place()
        self.me = 4 * self.x + 2 * self.y + self.c

    def _copy(self, w, k):
        flip = lambda v, bit: 1 - v if bit else v
        peer = (flip(self.x, ((k + 1) >> 2) & 1), flip(self.y, ((k + 1) >> 1) & 1), flip(self.c, (k + 1) & 1))
        return pltpu.make_async_remote_copy(
            src_ref=self.ins[w].at[4 * peer[0] + 2 * peer[1] + peer[2]], dst_ref=self.outs[w].at[self.me],
            send_sem=self.send.at[w, k], recv_sem=self.recv.at[w, k], device_id=peer, device_id_type=MESH)

    def _mine(self, w):
        return pltpu.make_async_copy(self.ins[w].at[self.me], self.outs[w].at[self.me], self.local.at[w])

    def start(self):
        for w in range(len(self.ins)):
            self._mine(w).start()
            for k in range(N_DEV - 1):
                self._copy(w, k).start()

    def finish(self):
        for w in range(len(self.ins)):
            for k in range(N_DEV - 1):
                self._copy(w, k).wait()
            self._mine(w).wait()


class _PairExchange:
    def __init__(self, ins, outs, sems):
        self.ins, self.outs = ins, outs
        self.send, self.recv, _ = sems
        x, y, self.c = _place()
        self.sibling = (x, y, 1 - self.c)

    def _copy(self, w, t):
        return pltpu.make_async_remote_copy(
            src_ref=self.ins[w].at[2 * t + 1 - self.c], dst_ref=self.outs[w].at[t],
            send_sem=self.send.at[w, t], recv_sem=self.recv.at[w, t], device_id=self.sibling, device_id_type=MESH)

    def start(self):
        for w in range(len(self.ins)):
            for t in range(4):
                self._copy(w, t).start()

    def finish(self):
        for w in range(len(self.ins)):
            for t in range(4):
                self._copy(w, t).wait()


class _ChipExchange:
    def __init__(self, ins, outs, sems):
        self.ins, self.outs = ins, outs
        self.send, self.recv, self.local = sems
        self.x, self.y, self.c = _place()
        self.chip = 2 * self.x + self.y

    def _copy(self, w, k):
        flip = lambda v, bit: 1 - v if bit else v
        px, py = flip(self.x, ((k + 1) >> 1) & 1), flip(self.y, (k + 1) & 1)
        return pltpu.make_async_remote_copy(
            src_ref=self.ins[w].at[2 * px + py], dst_ref=self.outs[w].at[self.chip],
            send_sem=self.send.at[w, k], recv_sem=self.recv.at[w, k], device_id=(px, py, self.c),
            device_id_type=MESH)

    def _mine(self, w):
        return pltpu.make_async_copy(self.ins[w].at[self.chip], self.outs[w].at[self.chip], self.local.at[w])

    def start(self):
        for w in range(len(self.ins)):
            self._mine(w).start()
            for k in range(3):
                self._copy(w, k).start()

    def finish(self):
        for w in range(len(self.ins)):
            for k in range(3):
                self._copy(w, k).wait()
            self._mine(w).wait()


def _pair_exchange(xs, name):
    nw = len(xs)

    def body(*refs):
        comm = _PairExchange(refs[:nw], refs[nw:2 * nw], refs[2 * nw:])
        comm.start()
        comm.finish()

    return pl.pallas_call(
        body, name=name, in_specs=[ANY] * nw, out_specs=[ANY] * nw,
        out_shape=[jax.ShapeDtypeStruct((4,) + a.shape[1:], a.dtype) for a in xs],
        scratch_shapes=_comm_sems(nw),
    )(*xs)


def _pair_sum(own, got, name):
    _, r, c = own.shape
    tr = r
    for t in (256, 128, 64, 32, 16):
        if r % t == 0 and r > t:
            tr = t
            break

    def body(own_ref, got_ref, out_ref):
        mine = jnp.where(lax.axis_index("c") == 0, own_ref[:, 0].astype(F32), own_ref[:, 1].astype(F32))
        out_ref[...] = (mine + got_ref[...].astype(F32)).astype(BF16)

    return pl.pallas_call(
        body, name=name, grid=(r // tr,),
        in_specs=[pl.BlockSpec((4, 2, tr, c), lambda i: (0, 0, i, 0)), pl.BlockSpec((4, tr, c), lambda i: (0, i, 0))],
        out_specs=pl.BlockSpec((4, tr, c), lambda i: (0, i, 0)),
        out_shape=jax.ShapeDtypeStruct((4, r, c), BF16),
        compiler_params=_params(("parallel",)),
    )(own.reshape(4, 2, r, c), got)


def _split_refs(refs, n_in, n_comm, n_out, n_scr):
    a = n_in
    b = a + n_comm
    c = b + n_out
    d = c + n_comm
    e = d + n_scr
    return refs[:a], refs[a:b], refs[b:c], refs[c:d], refs[d:e], refs[e:]


def _all_gather(xs, name):
    nw = len(xs)

    def body(*refs):
        comm = _Gather(refs[:nw], refs[nw:2 * nw], refs[2 * nw:])
        comm.start()
        comm.forward()
        comm.finish()

    return pl.pallas_call(
        body, name=name, in_specs=[ANY] * nw, out_specs=[ANY] * nw,
        out_shape=[jax.ShapeDtypeStruct((N_DEV,) + a.shape, a.dtype) for a in xs],
        scratch_shapes=_comm_sems(nw),
    )(*xs)


def _ffn_fwd(h, gain, wgu, wd, name, gather=()):
    n = h.shape[0]
    tm = _row_tile(n, 512)
    n_i = n // tm
    nw = len(gather)

    def body(*refs):
        (h_ref, g_ref, wgu_ref, wd_ref), gin, (out_ref, gate_ref, up_ref), gout, (n_scr, acc_scr), sems = \
            _split_refs(refs, 4, nw, 3, 2)
        i = pl.program_id(0)
        j = pl.program_id(1)
        if nw:
            comm = _Gather(gin, gout, sems)
            pl.when((i == 0) & (j == 0))(comm.start)
            pl.when((i == (3 * n_i) // 4) & (j == 0))(comm.forward)

        @pl.when(j == 0)
        def _():
            y, _, _ = _rms(h_ref[...], g_ref[...])
            n_scr[...] = y.astype(BF16)
            acc_scr[...] = jnp.zeros_like(acc_scr)

        nb = n_scr[...]
        gate = _dot(nb, wgu_ref[0, 0])
        up = _dot(nb, wgu_ref[1, 0])
        gate_ref[0] = gate.astype(BF16)
        up_ref[0] = up.astype(BF16)
        act = (gate * _sigmoid(gate) * up).astype(BF16)
        acc_scr[...] += _dot(act, wd_ref[0])

        @pl.when(j == N_CHUNK - 1)
        def _():
            out_ref[...] = h_ref[...] + 0.5 * acc_scr[...]

        if nw:
            pl.when((i == n_i - 1) & (j == N_CHUNK - 1))(comm.finish)

    return pl.pallas_call(
        body, name=name, grid=(n_i, N_CHUNK),
        in_specs=[pl.BlockSpec((tm, D_MODEL), lambda i, j: (i, 0)),
                  pl.BlockSpec((1, D_MODEL), lambda i, j: (0, 0)),
                  pl.BlockSpec((2, 1, D_MODEL, F_CHUNK), lambda i, j: (0, j, 0, 0)),
                  pl.BlockSpec((1, F_CHUNK, D_MODEL), lambda i, j: (j, 0, 0))] + [ANY] * nw,
        out_specs=[pl.BlockSpec((tm, D_MODEL), lambda i, j: (i, 0)),
                   pl.BlockSpec((1, tm, F_CHUNK), lambda i, j: (j, i, 0)),
                   pl.BlockSpec((1, tm, F_CHUNK), lambda i, j: (j, i, 0))] + [ANY] * nw,
        out_shape=[jax.ShapeDtypeStruct((n, D_MODEL), F32),
                   jax.ShapeDtypeStruct((N_CHUNK, n, F_CHUNK), BF16),
                   jax.ShapeDtypeStruct((N_CHUNK, n, F_CHUNK), BF16)]
        + [jax.ShapeDtypeStruct((N_DEV,) + a.shape, a.dtype) for a in gather],
        scratch_shapes=[pltpu.VMEM((tm, D_MODEL), BF16), pltpu.VMEM((tm, D_MODEL), F32)]
        + (_comm_sems(nw) if nw else []),
        compiler_params=_params(("arbitrary", "arbitrary")),
    )(h, gain, wgu, wd, *gather)


def _ffn_bwd_x(dh_out, h_in, gain, gate, up, wgu, wd, name):
    n = h_in.shape[0]
    tm = _row_tile(n, 512)

    def body(dh_ref, h_ref, g_ref, gate_ref, up_ref, wgu_ref, wd_ref,
             dhin_ref, dgate_ref, dup_ref, dgain_ref, dhb_scr, acc_scr):
        i = pl.program_id(0)
        j = pl.program_id(1)

        @pl.when((i == 0) & (j == 0))
        def _():
            dgain_ref[...] = jnp.zeros_like(dgain_ref)

        @pl.when(j == 0)
        def _():
            dhb_scr[...] = (0.5 * dh_ref[...]).astype(BF16)
            acc_scr[...] = jnp.zeros_like(acc_scr)

        da = _dot_nt(dhb_scr[...], wd_ref[0])
        g = gate_ref[0].astype(F32)
        u = up_ref[0].astype(F32)
        sig = _sigmoid(g)
        dgate = (da * u * (sig * (1.0 + g * (1.0 - sig)))).astype(BF16)
        dup = (da * (g * sig)).astype(BF16)
        dgate_ref[0] = dgate
        dup_ref[0] = dup
        acc_scr[...] += _dot_nt(dgate, wgu_ref[0, 0]) + _dot_nt(dup, wgu_ref[1, 0])

        @pl.when(j == N_CHUNK - 1)
        def _():
            gain_v = g_ref[...]
            _, xhat, r = _rms(h_ref[...], gain_v)
            dn = acc_scr[...]
            dhin_ref[...] = dh_ref[...] + _rms_bwd(dn, xhat, r, gain_v)
            dgain_ref[...] += jnp.sum(dn * xhat, axis=0, keepdims=True)

    chunk = pl.BlockSpec((1, tm, F_CHUNK), lambda i, j: (j, i, 0))
    rows = pl.BlockSpec((tm, D_MODEL), lambda i, j: (i, 0))
    vec = pl.BlockSpec((1, D_MODEL), lambda i, j: (0, 0))
    return pl.pallas_call(
        body, name=name, grid=(n // tm, N_CHUNK),
        in_specs=[rows, rows, vec, chunk, chunk,
                  pl.BlockSpec((2, 1, D_MODEL, F_CHUNK), lambda i, j: (0, j, 0, 0)),
                  pl.BlockSpec((1, F_CHUNK, D_MODEL), lambda i, j: (j, 0, 0))],
        out_specs=[rows, chunk, chunk, vec],
        out_shape=[jax.ShapeDtypeStruct((n, D_MODEL), F32),
                   jax.ShapeDtypeStruct((N_CHUNK, n, F_CHUNK), BF16),
                   jax.ShapeDtypeStruct((N_CHUNK, n, F_CHUNK), BF16),
                   jax.ShapeDtypeStruct((1, D_MODEL), F32)],
        scratch_shapes=[pltpu.VMEM((tm, D_MODEL), BF16), pltpu.VMEM((tm, D_MODEL), F32)],
        compiler_params=_params(("arbitrary", "arbitrary")),
    )(dh_out, h_in, gain, gate, up, wgu, wd)


def _ffn_bwd_act(dh_out, gate, up, wd, name):
    n = dh_out.shape[0]
    tm = _row_tile(n, 512)

    def body(dh_ref, gate_ref, up_ref, wd_ref, dgate_ref, dup_ref, dhb_scr):
        @pl.when(pl.program_id(1) == 0)
        def _():
            dhb_scr[...] = (0.5 * dh_ref[...]).astype(BF16)

        da = _dot_nt(dhb_scr[...], wd_ref[0])
        g = gate_ref[0].astype(F32)
        u = up_ref[0].astype(F32)
        sig = _sigmoid(g)
        dgate_ref[0] = (da * u * (sig * (1.0 + g * (1.0 - sig)))).astype(BF16)
        dup_ref[0] = (da * (g * sig)).astype(BF16)

    chunk = pl.BlockSpec((1, tm, F_CHUNK), lambda i, j: (j, i, 0))
    return pl.pallas_call(
        body, name=name, grid=(n // tm, N_CHUNK),
        in_specs=[pl.BlockSpec((tm, D_MODEL), lambda i, j: (i, 0)), chunk, chunk,
                  pl.BlockSpec((1, F_CHUNK, D_MODEL), lambda i, j: (j, 0, 0))],
        out_specs=[chunk, chunk],
        out_shape=[jax.ShapeDtypeStruct((N_CHUNK, n, F_CHUNK), BF16)] * 2,
        scratch_shapes=[pltpu.VMEM((tm, D_MODEL), BF16)],
        compiler_params=_params(("parallel", "arbitrary")),
    )(dh_out, gate, up, wd)


def _ffn_bwd_in(dh_out, h_in, gain, dgate, dup, wgu, name, exchange=()):
    n = h_in.shape[0]
    tm = _row_tile(n, 512)
    n_i = n // tm
    nw = len(exchange)

    def body(*refs):
        (dh_ref, h_ref, g_ref, dgate_ref, dup_ref, wgu_ref), xin, (dhin_ref, dgain_ref), xout, (acc_scr,), sems = \
            _split_refs(refs, 6, nw, 2, 1)
        i = pl.program_id(0)
        j = pl.program_id(1)
        if nw:
            comm = _Exchange(xin, xout, sems)
            pl.when((i == 0) & (j == 0))(comm.start)

        @pl.when((i == 0) & (j == 0))
        def _():
            dgain_ref[...] = jnp.zeros_like(dgain_ref)

        @pl.when(j == 0)
        def _():
            acc_scr[...] = jnp.zeros_like(acc_scr)

        acc_scr[...] += _dot_nt(dgate_ref[0], wgu_ref[0, 0]) + _dot_nt(dup_ref[0], wgu_ref[1, 0])

        @pl.when(j == N_CHUNK - 1)
        def _():
            gain_v = g_ref[...]
            _, xhat, r = _rms(h_ref[...], gain_v)
            dn = acc_scr[...]
            dhin_ref[...] = dh_ref[...] + _rms_bwd(dn, xhat, r, gain_v)
            dgain_ref[...] += jnp.sum(dn * xhat, axis=0, keepdims=True)

        if nw:
            pl.when((i == n_i - 1) & (j == N_CHUNK - 1))(comm.finish)

    chunk = pl.BlockSpec((1, tm, F_CHUNK), lambda i, j: (j, i, 0))
    rows = pl.BlockSpec((tm, D_MODEL), lambda i, j: (i, 0))
    vec = pl.BlockSpec((1, D_MODEL), lambda i, j: (0, 0))
    return pl.pallas_call(
        body, name=name, grid=(n_i, N_CHUNK),
        in_specs=[rows, rows, vec, chunk, chunk,
                  pl.BlockSpec((2, 1, D_MODEL, F_CHUNK), lambda i, j: (0, j, 0, 0))] + [ANY] * nw,
        out_specs=[rows, vec] + [ANY] * nw,
        out_shape=[jax.ShapeDtypeStruct((n, D_MODEL), F32), jax.ShapeDtypeStruct((1, D_MODEL), F32)]
        + [jax.ShapeDtypeStruct(a.shape, a.dtype) for a in exchange],
        scratch_shapes=[pltpu.VMEM((tm, D_MODEL), F32)] + (_comm_sems(nw) if nw else []),
        compiler_params=_params(("arbitrary", "arbitrary")),
    )(dh_out, h_in, gain, dgate, dup, wgu, *exchange)


def _ffn_bwd_w(dh_out, h_in, gain, gate, up, dgate, dup, name):
    n = h_in.shape[0]
    tm = _row_tile(n, 512)
    n_i = n // tm

    def body(dh_ref, h_ref, g_ref, gate_ref, up_ref, dgate_ref, dup_ref, dwgu_ref, dwd_ref,
             ag_scr, au_scr, ad_scr):
        i = pl.program_id(1)

        @pl.when(i == 0)
        def _():
            ag_scr[...] = jnp.zeros_like(ag_scr)
            au_scr[...] = jnp.zeros_like(au_scr)
            ad_scr[...] = jnp.zeros_like(ad_scr)

        y, _, _ = _rms(h_ref[...], g_ref[...])
        nb = y.astype(BF16)
        ag_scr[...] += _dot_tn(nb, dgate_ref[0])
        au_scr[...] += _dot_tn(nb, dup_ref[0])
        g = gate_ref[0].astype(F32)
        act = (g * _sigmoid(g) * up_ref[0].astype(F32)).astype(BF16)
        ad_scr[...] += _dot_tn(act, (0.5 * dh_ref[...]).astype(BF16))

        @pl.when(i == n_i - 1)
        def _():
            dwgu_ref[0, 0] = ag_scr[...].astype(BF16)
            dwgu_ref[1, 0] = au_scr[...].astype(BF16)
            dwd_ref[0] = ad_scr[...].astype(BF16)

    chunk = pl.BlockSpec((1, tm, F_CHUNK), lambda j, i: (j, i, 0))
    rows = pl.BlockSpec((tm, D_MODEL), lambda j, i: (i, 0))
    return pl.pallas_call(
        body, name=name, grid=(N_CHUNK, n_i),
        in_specs=[rows, rows, pl.BlockSpec((1, D_MODEL), lambda j, i: (0, 0)), chunk, chunk, chunk, chunk],
        out_specs=[pl.BlockSpec((2, 1, D_MODEL, F_CHUNK), lambda j, i: (0, j, 0, 0)),
                   pl.BlockSpec((1, F_CHUNK, D_MODEL), lambda j, i: (j, 0, 0))],
        out_shape=[jax.ShapeDtypeStruct((2, N_CHUNK, D_MODEL, F_CHUNK), BF16),
                   jax.ShapeDtypeStruct((N_CHUNK, F_CHUNK, D_MODEL), BF16)],
        scratch_shapes=[pltpu.VMEM((D_MODEL, F_CHUNK), F32), pltpu.VMEM((D_MODEL, F_CHUNK), F32),
                        pltpu.VMEM((F_CHUNK, D_MODEL), F32)],
        compiler_params=_params(("parallel", "arbitrary")),
    )(dh_out, h_in, gain, gate, up, dgate, dup)


def _resident(shape, rank):
    zeros = (0,) * len(shape)
    index_map = (lambda i: zeros) if rank == 1 else (lambda i, j: zeros)
    return pl.BlockSpec(shape, index_map, pipeline_mode=pl.Buffered(1))


W_GU_SHAPE = (2, N_CHUNK, F_CHUNK, D_MODEL)
W_D_SHAPE = (N_CHUNK, F_CHUNK, D_MODEL)


def _ffn_fwd(h, gain, wgu, wd, name, gather=()):
    n = h.shape[0]
    tm = _row_tile(n, 512)
    n_i = n // tm
    nw = len(gather)

    def body(*refs):
        (h_ref, g_ref, wgu_ref, wd_ref), gin, (out_ref, nrm_ref, gate_ref, up_ref), gout, _, sems = \
            _split_refs(refs, 4, nw, 4, 0)
        i = pl.program_id(0)
        if nw:
            comm = _Gather(gin, gout, sems)
            pl.when(i == 0)(comm.start)
            pl.when(i == (3 * n_i) // 4)(comm.forward)

        hv = h_ref[...]
        y, _, _ = _rms(hv, g_ref[...])
        nb = y.astype(BF16)
        nrm_ref[...] = nb
        acc = jnp.zeros((tm, D_MODEL), F32)
        for j in range(N_CHUNK):
            gate = _dot_nt(nb, wgu_ref[0, j])
            up = _dot_nt(nb, wgu_ref[1, j])
            gate_ref[j] = gate.astype(BF16)
            up_ref[j] = up.astype(BF16)
            acc = acc + _dot((gate * _sigmoid(gate) * up).astype(BF16), wd_ref[j])
        out_ref[...] = hv + 0.5 * acc

        if nw:
            pl.when(i == n_i - 1)(comm.finish)

    rows = pl.BlockSpec((tm, D_MODEL), lambda i: (i, 0))
    chunks = pl.BlockSpec((N_CHUNK, tm, F_CHUNK), lambda i: (0, i, 0))
    return pl.pallas_call(
        body, name=name, grid=(n_i,),
        in_specs=[rows, pl.BlockSpec((1, D_MODEL), lambda i: (0, 0)), _resident(W_GU_SHAPE, 1),
                  _resident(W_D_SHAPE, 1)] + [ANY] * nw,
        out_specs=[rows, rows, chunks, chunks] + [ANY] * nw,
        out_shape=[jax.ShapeDtypeStruct((n, D_MODEL), F32), jax.ShapeDtypeStruct((n, D_MODEL), BF16),
                   jax.ShapeDtypeStruct((N_CHUNK, n, F_CHUNK), BF16),
                   jax.ShapeDtypeStruct((N_CHUNK, n, F_CHUNK), BF16)]
        + [jax.ShapeDtypeStruct((N_DEV,) + a.shape, a.dtype) for a in gather],
        scratch_shapes=_comm_sems(nw) if nw else [],
        compiler_params=_params(("arbitrary",)),
    )(h, gain, wgu, wd, *gather)


def _swiglu_bwd(da, gate_ref, up_ref, j):
    g = gate_ref[j].astype(F32)
    u = up_ref[j].astype(F32)
    sig = _sigmoid(g)
    return (da * u * (sig * (1.0 + g * (1.0 - sig)))).astype(BF16), (da * (g * sig)).astype(BF16)


def _ffn_bwd_x(dh_out, h_in, gain, gate, up, wgu, wd, name):
    n = h_in.shape[0]
    tm = _row_tile(n, 256)

    def body(dh_ref, h_ref, g_ref, gate_ref, up_ref, wgu_ref, wd_ref,
             dhin_ref, dhb_ref, dgate_ref, dup_ref, dgain_ref):
        @pl.when(pl.program_id(0) == 0)
        def _():
            dgain_ref[...] = jnp.zeros_like(dgain_ref)

        dhv = dh_ref[...]
        dhb = (0.5 * dhv).astype(BF16)
        dhb_ref[...] = dhb
        dn = jnp.zeros((tm, D_MODEL), F32)
        for j in range(N_CHUNK):
            dgate, dup = _swiglu_bwd(_dot_nt(dhb, wd_ref[j]), gate_ref, up_ref, j)
            dgate_ref[j] = dgate
            dup_ref[j] = dup
            dn = dn + _dot(dgate, wgu_ref[0, j]) + _dot(dup, wgu_ref[1, j])
        gain_v = g_ref[...]
        _, xhat, r = _rms(h_ref[...], gain_v)
        dhin_ref[...] = dhv + _rms_bwd(dn, xhat, r, gain_v)
        dgain_ref[...] += jnp.sum(dn * xhat, axis=0, keepdims=True)

    rows = pl.BlockSpec((tm, D_MODEL), lambda i: (i, 0))
    chunks = pl.BlockSpec((N_CHUNK, tm, F_CHUNK), lambda i: (0, i, 0))
    vec = pl.BlockSpec((1, D_MODEL), lambda i: (0, 0))
    return pl.pallas_call(
        body, name=name, grid=(n // tm,),
        in_specs=[rows, rows, vec, chunks, chunks, _resident(W_GU_SHAPE, 1), _resident(W_D_SHAPE, 1)],
        out_specs=[rows, rows, chunks, chunks, vec],
        out_shape=[jax.ShapeDtypeStruct((n, D_MODEL), F32), jax.ShapeDtypeStruct((n, D_MODEL), BF16),
                   jax.ShapeDtypeStruct((N_CHUNK, n, F_CHUNK), BF16),
                   jax.ShapeDtypeStruct((N_CHUNK, n, F_CHUNK), BF16),
                   jax.ShapeDtypeStruct((1, D_MODEL), F32)],
        compiler_params=_params(("arbitrary",)),
    )(dh_out, h_in, gain, gate, up, wgu, wd)


def _ffn_bwd_act(dh_out, gate, up, wd, name, exchange=()):
    n = dh_out.shape[0]
    tm = _row_tile(n, 512)
    n_i = n // tm
    nw = len(exchange)

    def body(*refs):
        (dh_ref, gate_ref, up_ref, wd_ref), xin, (dhb_ref, dgate_ref, dup_ref), xout, _, sems = \
            _split_refs(refs, 4, nw, 3, 0)
        i = pl.program_id(0)
        if nw:
            comm = _Exchange(xin, xout, sems)
            pl.when(i == 0)(comm.start)

        dhb = (0.5 * dh_ref[...]).astype(BF16)
        dhb_ref[...] = dhb
        for j in range(N_CHUNK):
            dgate_ref[j], dup_ref[j] = _swiglu_bwd(_dot_nt(dhb, wd_ref[j]), gate_ref, up_ref, j)

        if nw:
            pl.when(i == n_i - 1)(comm.finish)

    rows = pl.BlockSpec((tm, D_MODEL), lambda i: (i, 0))
    chunks = pl.BlockSpec((N_CHUNK, tm, F_CHUNK), lambda i: (0, i, 0))
    return pl.pallas_call(
        body, name=name, grid=(n_i,),
        in_specs=[rows, chunks, chunks, _resident(W_D_SHAPE, 1)] + [ANY] * nw,
        out_specs=[rows, chunks, chunks] + [ANY] * nw,
        out_shape=[jax.ShapeDtypeStruct((n, D_MODEL), BF16)] + [jax.ShapeDtypeStruct((N_CHUNK, n, F_CHUNK), BF16)] * 2
        + [jax.ShapeDtypeStruct(a.shape, a.dtype) for a in exchange],
        scratch_shapes=_comm_sems(nw) if nw else [],
        compiler_params=_params(("arbitrary",)),
    )(dh_out, gate, up, wd, *exchange)


def _ffn_bwd_in(dh_out, h_in, gain, dgate, dup, wgu, name, exchange=()):
    n = h_in.shape[0]
    tm = _row_tile(n, 512)
    n_i = n // tm
    nw = len(exchange)

    def body(*refs):
        (dh_ref, h_ref, g_ref, dgate_ref, dup_ref, wgu_ref), xin, (dhin_ref, dgain_ref), xout, _, sems = \
            _split_refs(refs, 6, nw, 2, 0)
        i = pl.program_id(0)
        if nw:
            comm = _ChipExchange(xin, xout, sems)
            pl.when(i == 0)(comm.start)

        @pl.when(i == 0)
        def _():
            dgain_ref[...] = jnp.zeros_like(dgain_ref)

        dn = jnp.zeros((tm, D_MODEL), F32)
        for j in range(N_CHUNK):
            dn = dn + _dot(dgate_ref[j], wgu_ref[0, j]) + _dot(dup_ref[j], wgu_ref[1, j])
        gain_v = g_ref[...]
        _, xhat, r = _rms(h_ref[...], gain_v)
        dhin_ref[...] = dh_ref[...] + _rms_bwd(dn, xhat, r, gain_v)
        dgain_ref[...] += jnp.sum(dn * xhat, axis=0, keepdims=True)

        if nw:
            pl.when(i == n_i - 1)(comm.finish)

    rows = pl.BlockSpec((tm, D_MODEL), lambda i: (i, 0))
    chunks = pl.BlockSpec((N_CHUNK, tm, F_CHUNK), lambda i: (0, i, 0))
    vec = pl.BlockSpec((1, D_MODEL), lambda i: (0, 0))
    return pl.pallas_call(
        body, name=name, grid=(n_i,),
        in_specs=[rows, rows, vec, chunks, chunks, _resident(W_GU_SHAPE, 1)] + [ANY] * nw,
        out_specs=[rows, vec] + [ANY] * nw,
        out_shape=[jax.ShapeDtypeStruct((n, D_MODEL), F32), jax.ShapeDtypeStruct((1, D_MODEL), F32)]
        + [jax.ShapeDtypeStruct(a.shape, a.dtype) for a in exchange],
        scratch_shapes=_comm_sems(nw) if nw else [],
        compiler_params=_params(("arbitrary",)),
    )(dh_out, h_in, gain, dgate, dup, wgu, *exchange)


W_GROUP = 2


def _ffn_bwd_w(dhb, nrm, gate, up, dgate, dup, name):
    n = nrm.shape[0]
    tm = _row_tile(n, 512)
    n_i = n // tm

    def body(dhb_ref, nrm_ref, gate_ref, up_ref, dgate_ref, dup_ref, dwgu_ref, dwd_ref, ag_scr, au_scr, ad_scr):
        i = pl.program_id(1)

        @pl.when(i == 0)
        def _():
            ag_scr[...] = jnp.zeros_like(ag_scr)
            au_scr[...] = jnp.zeros_like(au_scr)
            ad_scr[...] = jnp.zeros_like(ad_scr)

        nb = nrm_ref[...]
        dhv = dhb_ref[...]
        for jj in range(W_GROUP):
            ag_scr[jj] += _dot_tn(dgate_ref[jj], nb)
            au_scr[jj] += _dot_tn(dup_ref[jj], nb)
            g = gate_ref[jj].astype(F32)
            act = (g * _sigmoid(g) * up_ref[jj].astype(F32)).astype(BF16)
            ad_scr[jj] += _dot_tn(act, dhv)

        @pl.when(i == n_i - 1)
        def _():
            dwgu_ref[0] = ag_scr[...].astype(BF16)
            dwgu_ref[1] = au_scr[...].astype(BF16)
            dwd_ref[...] = ad_scr[...].astype(BF16)

    chunks = pl.BlockSpec((W_GROUP, tm, F_CHUNK), lambda g, i: (g, i, 0))
    rows = pl.BlockSpec((tm, D_MODEL), lambda g, i: (i, 0))
    return pl.pallas_call(
        body, name=name, grid=(N_CHUNK // W_GROUP, n_i),
        in_specs=[rows, rows, chunks, chunks, chunks, chunks],
        out_specs=[pl.BlockSpec((2, W_GROUP, F_CHUNK, D_MODEL), lambda g, i: (0, g, 0, 0)),
                   pl.BlockSpec((W_GROUP, F_CHUNK, D_MODEL), lambda g, i: (g, 0, 0))],
        out_shape=[jax.ShapeDtypeStruct(W_GU_SHAPE, BF16), jax.ShapeDtypeStruct(W_D_SHAPE, BF16)],
        scratch_shapes=[pltpu.VMEM((W_GROUP, F_CHUNK, D_MODEL), F32), pltpu.VMEM((W_GROUP, F_CHUNK, D_MODEL), F32),
                        pltpu.VMEM((W_GROUP, F_CHUNK, D_MODEL), F32)],
        compiler_params=_params(("parallel", "arbitrary")),
    )(dhb, nrm, gate, up, dgate, dup)


N_PIECE = IN_MAIN // 512


def _inproj_fwd(h, gain, w_in):
    n = h.shape[0]
    tm = _row_tile(n, 512)

    def body(h_ref, g_ref, w_ref, *outs):
        y, _, _ = _rms(h_ref[...], g_ref[...])
        nb = y.astype(BF16)
        for p in range(N_PIECE):
            outs[p][...] = _dot_nt(nb, w_ref[512 * p:512 * (p + 1), :]).astype(BF16)
        outs[N_PIECE][...] = _dot_nt(nb, w_ref[IN_MAIN:IN_PAD, :])

    piece = pl.BlockSpec((tm, 512), lambda i: (i, 0))
    return pl.pallas_call(
        body, name="inproj_fwd", grid=(n // tm,),
        in_specs=[pl.BlockSpec((tm, D_MODEL), lambda i: (i, 0)),
                  pl.BlockSpec((1, D_MODEL), lambda i: (0, 0)),
                  pl.BlockSpec((IN_PAD, D_MODEL), lambda i: (0, 0))],
        out_specs=[piece] * N_PIECE + [pl.BlockSpec((tm, 128), lambda i: (i, 0))],
        out_shape=[jax.ShapeDtypeStruct((n, 512), BF16)] * N_PIECE + [jax.ShapeDtypeStruct((n, 128), F32)],
        compiler_params=_params(("parallel",)),
    )(h, gain, w_in)


def _inproj_bwd(dpieces, dfg, dh_out, h_in, gain, w_in):
    n = h_in.shape[0]
    tm = _row_tile(n, 512)
    n_i = n // tm

    def body(*refs):
        dp_refs = refs[:N_PIECE]
        dfg_ref, dh_ref, h_ref, g_ref, w_ref, dhin_ref, dw_ref, dgain_ref, acc_scr = refs[N_PIECE:]
        i = pl.program_id(0)

        @pl.when(i == 0)
        def _():
            acc_scr[...] = jnp.zeros_like(acc_scr)
            dgain_ref[...] = jnp.zeros_like(dgain_ref)

        gain_v = g_ref[...]
        y, xhat, r = _rms(h_ref[...], gain_v)
        nb = y.astype(BF16)
        dn = jnp.zeros((tm, D_MODEL), F32)
        for p in range(N_PIECE + 1):
            lo, hi = (512 * p, 512 * (p + 1)) if p < N_PIECE else (IN_MAIN, IN_PAD)
            dp = (dp_refs[p][...] if p < N_PIECE else dfg_ref[...]).astype(BF16)
            dn = dn + _dot(dp, w_ref[lo:hi, :])
            acc_scr[lo:hi, :] += _dot_tn(dp, nb)
        dhin_ref[...] = dh_ref[...] + _rms_bwd(dn, xhat, r, gain_v)
        dgain_ref[...] += jnp.sum(dn * xhat, axis=0, keepdims=True)

        @pl.when(i == n_i - 1)
        def _():
            dw_ref[...] = acc_scr[...].astype(BF16)

    piece = pl.BlockSpec((tm, 512), lambda i: (i, 0))
    rows = pl.BlockSpec((tm, D_MODEL), lambda i: (i, 0))
    vec = pl.BlockSpec((1, D_MODEL), lambda i: (0, 0))
    wspec = pl.BlockSpec((IN_PAD, D_MODEL), lambda i: (0, 0))
    return pl.pallas_call(
        body, name="inproj_bwd", grid=(n_i,),
        in_specs=[piece] * N_PIECE + [pl.BlockSpec((tm, 128), lambda i: (i, 0)), rows, rows, vec, wspec],
        out_specs=[rows, wspec, vec],
        out_shape=[jax.ShapeDtypeStruct((n, D_MODEL), F32),
                   jax.ShapeDtypeStruct((IN_PAD, D_MODEL), BF16),
                   jax.ShapeDtypeStruct((1, D_MODEL), F32)],
        scratch_shapes=[pltpu.VMEM((IN_PAD, D_MODEL), F32)],
        compiler_params=_params(("arbitrary",)),
    )(*dpieces, dfg, dh_out, h_in, gain, w_in)


def _outproj_fwd(zc, za, w_out, h):
    n = h.shape[0]
    tm = _row_tile(n, 512)

    def body(zc_ref, za_ref, w_ref, h_ref, out_ref):
        out_ref[...] = (h_ref[...] + _dot(zc_ref[...], w_ref[0:CONV_DIM, :])
                        + _dot(za_ref[...], w_ref[CONV_DIM:, :]))

    half = pl.BlockSpec((tm, 512), lambda i: (i, 0))
    rows = pl.BlockSpec((tm, D_MODEL), lambda i: (i, 0))
    return pl.pallas_call(
        body, name="outproj_fwd", grid=(n // tm,),
        in_specs=[half, half, pl.BlockSpec((D_MODEL, D_MODEL), lambda i: (0, 0)), rows],
        out_specs=rows,
        out_shape=jax.ShapeDtypeStruct((n, D_MODEL), F32),
        compiler_params=_params(("parallel",)),
    )(zc, za, w_out, h)


def _outproj_bwd(dh, zc, za, w_out):
    n = dh.shape[0]
    tm = _row_tile(n, 512)
    n_i = n // tm

    def body(dh_ref, zc_ref, za_ref, w_ref, dzc_ref, dza_ref, dw_ref, acc_scr):
        i = pl.program_id(0)

        @pl.when(i == 0)
        def _():
            acc_scr[...] = jnp.zeros_like(acc_scr)

        dhb = dh_ref[...].astype(BF16)
        dzc_ref[...] = _dot_nt(dhb, w_ref[0:CONV_DIM, :]).astype(BF16)
        dza_ref[...] = _dot_nt(dhb, w_ref[CONV_DIM:, :]).astype(BF16)
        acc_scr[0:CONV_DIM, :] += _dot_tn(zc_ref[...], dhb)
        acc_scr[CONV_DIM:, :] += _dot_tn(za_ref[...], dhb)

        @pl.when(i == n_i - 1)
        def _():
            dw_ref[...] = acc_scr[...].astype(BF16)

    half = pl.BlockSpec((tm, 512), lambda i: (i, 0))
    wspec = pl.BlockSpec((D_MODEL, D_MODEL), lambda i: (0, 0))
    return pl.pallas_call(
        body, name="outproj_bwd", grid=(n_i,),
        in_specs=[pl.BlockSpec((tm, D_MODEL), lambda i: (i, 0)), half, half, wspec],
        out_specs=[half, half, wspec],
        out_shape=[jax.ShapeDtypeStruct((n, 512), BF16), jax.ShapeDtypeStruct((n, 512), BF16),
                   jax.ShapeDtypeStruct((D_MODEL, D_MODEL), BF16)],
        scratch_shapes=[pltpu.VMEM((D_MODEL, D_MODEL), F32)],
        compiler_params=_params(("arbitrary",)),
    )(dh, zc, za, w_out)


def _group_matrix():
    r = lax.broadcasted_iota(jnp.int32, (128, 128), 0) // HEAD_DIM
    c = lax.broadcasted_iota(jnp.int32, (128, 128), 1) // HEAD_DIM
    return jnp.where(r == c, 1.0 / HEAD_DIM, 0.0).astype(BF16)


def _group_mean(x, gmat):
    hi = x.astype(BF16)
    lo = (x - hi.astype(F32)).astype(BF16)
    return _dot(hi, gmat) + _dot(lo, gmat)


def _shift_rows(x, s):
    rows = x.shape[0]
    t = lax.broadcasted_iota(jnp.int32, x.shape, 0)
    rolled = pltpu.roll(x, s % rows, 0)
    keep = (t >= s) if s > 0 else (t < rows + s)
    return jnp.where(keep, rolled, 0.0)


def _conv_parts(bg_ref, cg_ref, hc_ref, w_ref):
    bg = bg_ref[...].astype(F32)
    cg = cg_ref[...].astype(F32)
    hc = hc_ref[...].astype(F32)
    u = cg * hc
    u1 = _shift_rows(u, 1)
    u2 = _shift_rows(u, 2)
    conv = w_ref[2:3, :] * u + w_ref[1:2, :] * u1 + w_ref[0:1, :] * u2
    return bg, cg, hc, u, u1, u2, conv


def _conv_fwd(bg, cg, hc, conv_w, gain, gmat, lp):
    n = bg.shape[0]
    nb = n // lp

    def body(bg_ref, cg_ref, hc_ref, w_ref, g_ref, gm_ref, z_ref):
        bgv, _, _, _, _, _, conv = _conv_parts(bg_ref, cg_ref, hc_ref, w_ref)
        yc = bgv * conv
        r = lax.rsqrt(_group_mean(yc * yc, gm_ref[...]) + EPS)
        z_ref[...] = (yc * r * g_ref[...]).astype(BF16)

    blk = pl.BlockSpec((lp, 128), lambda c, b: (b, c))
    return pl.pallas_call(
        body, name="conv_fwd", grid=(CONV_DIM // 128, nb),
        in_specs=[blk, blk, blk, pl.BlockSpec((3, 128), lambda c, b: (0, c)),
                  pl.BlockSpec((1, 128), lambda c, b: (0, c)), pl.BlockSpec((128, 128), lambda c, b: (0, 0))],
        out_specs=blk,
        out_shape=jax.ShapeDtypeStruct((n, CONV_DIM), BF16),
        compiler_params=_params(("parallel", "parallel")),
    )(bg, cg, hc, conv_w, gain, gmat)


def _conv_bwd(dz, bg, cg, hc, conv_w, gain, gmat, lp):
    n = bg.shape[0]
    nb = n // lp

    def body(dz_ref, bg_ref, cg_ref, hc_ref, w_ref, g_ref, gm_ref,
             dbg_ref, dcg_ref, dhc_ref, dw_ref, dgain_ref):
        b = pl.program_id(1)

        @pl.when(b == 0)
        def _():
            dw_ref[...] = jnp.zeros_like(dw_ref)
            dgain_ref[...] = jnp.zeros_like(dgain_ref)

        bgv, cgv, hcv, u, u1, u2, conv = _conv_parts(bg_ref, cg_ref, hc_ref, w_ref)
        gm = gm_ref[...]
        yc = bgv * conv
        r = lax.rsqrt(_group_mean(yc * yc, gm) + EPS)
        yhat = yc * r
        dzv = dz_ref[...].astype(F32)
        dyhat = dzv * g_ref[...]
        dgain_ref[...] += jnp.sum(dzv * yhat, axis=0, keepdims=True)
        dyc = r * (dyhat - yhat * _group_mean(dyhat * yhat, gm))
        dbg_ref[...] = (dyc * conv).astype(BF16)
        dconv = dyc * bgv
        du = (w_ref[2:3, :] * dconv + w_ref[1:2, :] * _shift_rows(dconv, -1)
              + w_ref[0:1, :] * _shift_rows(dconv, -2))
        dcg_ref[...] = (du * hcv).astype(BF16)
        dhc_ref[...] = (du * cgv).astype(BF16)
        dw_ref[0:1, :] += jnp.sum(dconv * u2, axis=0, keepdims=True)
        dw_ref[1:2, :] += jnp.sum(dconv * u1, axis=0, keepdims=True)
        dw_ref[2:3, :] += jnp.sum(dconv * u, axis=0, keepdims=True)

    blk = pl.BlockSpec((lp, 128), lambda c, b: (b, c))
    wspec = pl.BlockSpec((3, 128), lambda c, b: (0, c))
    gspec = pl.BlockSpec((1, 128), lambda c, b: (0, c))
    return pl.pallas_call(
        body, name="conv_bwd", grid=(CONV_DIM // 128, nb),
        in_specs=[blk, blk, blk, blk, wspec, gspec, pl.BlockSpec((128, 128), lambda c, b: (0, 0))],
        out_specs=[blk, blk, blk, wspec, gspec],
        out_shape=[jax.ShapeDtypeStruct((n, CONV_DIM), BF16)] * 3
        + [jax.ShapeDtypeStruct((3, CONV_DIM), F32), jax.ShapeDtypeStruct((1, CONV_DIM), F32)],
        compiler_params=_params(("parallel", "arbitrary")),
    )(dz, bg, cg, hc, conv_w, gain, gmat)


KEY_MASKED = 1e30
ONE_LANE = 24


def _scan_steps(rows):
    s, out = 1, []
    while s < rows:
        out.append(s)
        s *= 2
    return out


def _fgate_fwd(fg, b_f, lp):
    n = fg.shape[0]
    nb = n // lp

    def body(fg_ref, b_ref, ka_ref, qa_ref):
        x = fg_ref[...] + b_ref[...]
        logf = jnp.minimum(x, 0.0) - jnp.log(1.0 + jnp.exp(-jnp.abs(x)))
        t = lax.broadcasted_iota(jnp.int32, (lp, 128), 0)
        lane = lax.broadcasted_iota(jnp.int32, (lp, 128), 1)
        f = jnp.where((t >= PAD) & (lane < N_HEADS), logf, 0.0)
        for s in _scan_steps(lp):
            f = f + _shift_rows(f, s)
        hi = f.astype(BF16).astype(F32)
        rest = f - hi
        mid = rest.astype(BF16).astype(F32)
        lo = (rest - mid).astype(BF16).astype(F32)
        ones = jnp.where((lane >= ONE_LANE) & (lane < ONE_LANE + 3), 1.0, 0.0)
        hi_key = jnp.where((t < PAD) & (lane < N_HEADS), KEY_MASKED, hi)
        ka_ref[...] = (hi_key + pltpu.roll(mid, 8, 1) + pltpu.roll(lo, 16, 1) + ones).astype(BF16)
        for h in range(N_HEADS):
            minus = jnp.where((lane == h) | (lane == 8 + h) | (lane == 16 + h), -1.0, 0.0)
            terms = (jnp.where(lane == ONE_LANE, pltpu.roll(hi, ONE_LANE - h, 1), 0.0)
                     + jnp.where(lane == ONE_LANE + 1, pltpu.roll(mid, ONE_LANE + 1 - h, 1), 0.0)
                     + jnp.where(lane == ONE_LANE + 2, pltpu.roll(lo, ONE_LANE + 2 - h, 1), 0.0))
            qa_ref[:, 128 * h:128 * (h + 1)] = (minus + terms).astype(BF16)

    return pl.pallas_call(
        body, name="fgate_fwd", grid=(nb,),
        in_specs=[pl.BlockSpec((lp, 128), lambda b: (b, 0)), pl.BlockSpec((1, 128), lambda b: (0, 0))],
        out_specs=[pl.BlockSpec((lp, 128), lambda b: (b, 0)), pl.BlockSpec((lp, N_HEADS * 128), lambda b: (b, 0))],
        out_shape=[jax.ShapeDtypeStruct((n, 128), BF16), jax.ShapeDtypeStruct((n, N_HEADS * 128), BF16)],
        compiler_params=_params(("parallel",)),
    )(fg, b_f)


def _fgate_bwd(dka, dfr, fg, b_f, lp):
    n = fg.shape[0]
    nb = n // lp

    def body(dka_ref, dfr_ref, fg_ref, b_ref, dfg_ref, db_ref):
        b = pl.program_id(0)

        @pl.when(b == 0)
        def _():
            db_ref[...] = jnp.zeros_like(db_ref)

        wide = jnp.concatenate([dfr_ref[0], jnp.zeros((128 - N_HEADS, lp), F32)], axis=0)
        t = lax.broadcasted_iota(jnp.int32, (lp, 128), 0)
        lane = lax.broadcasted_iota(jnp.int32, (lp, 128), 1)
        d = jnp.where(lane < N_HEADS, dka_ref[...], 0.0) + wide.T
        for s in _scan_steps(lp):
            d = d + _shift_rows(d, -s)
        x = fg_ref[...] + b_ref[...]
        dx = jnp.where((t >= PAD) & (lane < N_HEADS), d * _sigmoid(-x), 0.0)
        dfg_ref[...] = dx
        db_ref[...] += jnp.sum(dx, axis=0, keepdims=True)

    return pl.pallas_call(
        body, name="fgate_bwd", grid=(nb,),
        in_specs=[pl.BlockSpec((lp, 128), lambda b: (b, 0)), pl.BlockSpec((1, N_HEADS, lp), lambda b: (b, 0, 0)),
                  pl.BlockSpec((lp, 128), lambda b: (b, 0)), pl.BlockSpec((1, 128), lambda b: (0, 0))],
        out_specs=[pl.BlockSpec((lp, 128), lambda b: (b, 0)), pl.BlockSpec((1, 128), lambda b: (0, 0))],
        out_shape=[jax.ShapeDtypeStruct((n, 128), F32), jax.ShapeDtypeStruct((1, 128), F32)],
        compiler_params=_params(("arbitrary",)),
    )(dka, dfr, fg, b_f)


def _head_masks():
    lane = lax.broadcasted_iota(jnp.int32, (1, 128), 1)
    return lane < HEAD_DIM


def _stack_heads(x2, first):
    zero = jnp.zeros_like(x2)
    return jnp.concatenate([jnp.where(first, x2, zero), jnp.where(first, zero, x2)], axis=0)


def _stack_heads_lanes(xt):
    r = lax.broadcasted_iota(jnp.int32, xt.shape, 0)
    zero = jnp.zeros_like(xt)
    return jnp.concatenate([jnp.where(r < HEAD_DIM, xt, zero), jnp.where(r < HEAD_DIM, zero, xt)], axis=1)


def _pair_cols(col0, col1, first):
    return jnp.where(first, col0, col1)


def _pair_rows(row0, row1):
    r = lax.broadcasted_iota(jnp.int32, (128, TQ), 0)
    return jnp.where(r < HEAD_DIM, row0, row1)


def _query_side(q_ref, qa_ref, p, first):
    q2 = q_ref[:, 128 * p:128 * (p + 1)] * 0.125
    zero = jnp.zeros_like(q2)
    top = jnp.concatenate([jnp.where(first, q2, zero), qa_ref[:, 128 * (2 * p):128 * (2 * p + 1)]], axis=1)
    bot = jnp.concatenate([jnp.where(first, zero, q2), qa_ref[:, 128 * (2 * p + 1):128 * (2 * p + 2)]], axis=1)
    return jnp.concatenate([top, bot], axis=0)


def _key_chunks(lp):
    return (lp + TK - 1) // TK


def _chunk_mask(i, c):
    r = lax.broadcasted_iota(jnp.int32, (TK, 2 * TQ), 0)
    col = lax.broadcasted_iota(jnp.int32, (TK, 2 * TQ), 1)
    return (c * TK + r) <= (i * TQ + (col & (TQ - 1)))


def _transpose_bf16(x):
    return x.astype(F32).T.astype(BF16)


def _attn_fwd(q, qa, k, v, ka, gain, lp):
    n = q.shape[0]
    nb = n // lp
    nq = lp // TQ
    lpp = _key_chunks(lp) * TK

    def body(q_ref, qa_ref, k_ref, v_ref, ka_ref, g_ref, z_ref, o_ref, lse_ref, kx_scr, vt_scr):
        i = pl.program_id(1)
        first = _head_masks()

        @pl.when(i == 0)
        def _():
            if lpp > lp:
                kx_scr[lp:lpp, :] = jnp.zeros((lpp - lp, 2 * ATTN_DIM), BF16)
                vt_scr[:, lp:lpp] = jnp.zeros((ATTN_DIM, lpp - lp), BF16)
            for p in range(N_PAIRS):
                kx_scr[0:lp, 256 * p:256 * p + 128] = k_ref[:, 128 * p:128 * (p + 1)]
                kx_scr[0:lp, 256 * p + 128:256 * (p + 1)] = ka_ref[...]
            vt_scr[:, 0:lp] = _transpose_bf16(v_ref[...])

        rhs_t = [_transpose_bf16(_query_side(q_ref, qa_ref, p, first)) for p in range(N_PAIRS)]

        def step(c, carry):
            koff = pl.multiple_of(c * TK, TK)
            valid = _chunk_mask(i, c)
            new = []
            for p in range(N_PAIRS):
                m, l, acc = carry[p]
                st = _dot(kx_scr[pl.ds(koff, TK), 256 * p:256 * (p + 1)], rhs_t[p])
                st = jnp.where(valid, st, NEG)
                m_new = jnp.maximum(m, jnp.max(st, axis=0, keepdims=True))
                pt = jnp.exp(st - m_new)
                alpha = jnp.exp(m - m_new)
                l = alpha * l + jnp.sum(pt, axis=0, keepdims=True)
                pb = pt.astype(BF16)
                vt = _stack_heads_lanes(vt_scr[128 * p:128 * (p + 1), pl.ds(koff, TK)])
                pv = _dot(vt, jnp.concatenate([pb[:, 0:TQ], pb[:, TQ:]], axis=0))
                acc = acc * _pair_rows(alpha[:, 0:TQ], alpha[:, TQ:]) + pv
                new.append((m_new, l, acc))
            return tuple(new)

        init = tuple((jnp.full((1, 2 * TQ), NEG, F32), jnp.zeros((1, 2 * TQ), F32), jnp.zeros((128, TQ), F32))
                     for _ in range(N_PAIRS))
        final = lax.fori_loop(0, (i + 2) // 2, step, init)

        row = lax.broadcasted_iota(jnp.int32, (TQ, 128), 0)
        real = (i * TQ + row) >= PAD
        for p in range(N_PAIRS):
            m, l, acc = final[p]
            inv = 1.0 / l
            ot = acc * _pair_rows(inv[:, 0:TQ], inv[:, TQ:])
            sq = ot * ot
            r0 = lax.rsqrt(jnp.sum(sq[0:HEAD_DIM], axis=0, keepdims=True) * (1.0 / HEAD_DIM) + EPS)
            r1 = lax.rsqrt(jnp.sum(sq[HEAD_DIM:], axis=0, keepdims=True) * (1.0 / HEAD_DIM) + EPS)
            cols = slice(128 * p, 128 * (p + 1))
            o_ref[:, cols] = jnp.where(real, ot.T, 0.0).astype(BF16)
            z_ref[:, cols] = (jnp.where(real, (ot * _pair_rows(r0, r1)).T, 0.0) * g_ref[:, cols]).astype(BF16)
            lse = m + jnp.log(l)
            lse_ref[0, 2 * p:2 * p + 1, :] = lse[:, 0:TQ]
            lse_ref[0, 2 * p + 1:2 * p + 2, :] = lse[:, TQ:]

    qblk = pl.BlockSpec((TQ, ATTN_DIM), lambda b, i: (b * nq + i, 0))
    qablk = pl.BlockSpec((TQ, N_HEADS * 128), lambda b, i: (b * nq + i, 0))
    seq = pl.BlockSpec((lp, ATTN_DIM), lambda b, i: (b, 0))
    rowblk = pl.BlockSpec((1, N_HEADS, TQ), lambda b, i: (b, 0, i))
    return pl.pallas_call(
        body, name="attn_fwd", grid=(nb, nq),
        in_specs=[qblk, qablk, seq, seq, pl.BlockSpec((lp, 128), lambda b, i: (b, 0)),
                  pl.BlockSpec((1, ATTN_DIM), lambda b, i: (0, 0))],
        out_specs=[qblk, qblk, rowblk],
        out_shape=[jax.ShapeDtypeStruct((n, ATTN_DIM), BF16), jax.ShapeDtypeStruct((n, ATTN_DIM), BF16),
                   jax.ShapeDtypeStruct((nb, N_HEADS, lp), F32)],
        scratch_shapes=[pltpu.VMEM((lpp, 2 * ATTN_DIM), BF16), pltpu.VMEM((ATTN_DIM, lpp), BF16)],
        compiler_params=_params(("parallel", "arbitrary")),
    )(q, qa, k, v, ka, gain)


def _attn_bwd(dz, q, qa, k, v, ka, o, lse, gain, lp, exchange=()):
    n = q.shape[0]
    nb = n // lp
    nq = lp // TQ
    lpp = _key_chunks(lp) * TK
    nw = len(exchange)

    def body(*refs):
        ((dz_ref, q_ref, qa_ref, k_ref, v_ref, ka_ref, o_ref, lse_ref, g_ref), xin,
         (dq_ref, dk_ref, dv_ref, dka_ref, dfr_ref, dgain_ref), xout,
         (kx_scr, vx_scr, kt_scr, dkx_scr, dvx_scr), sems) = _split_refs(refs, 9, nw, 6, 5)
        b = pl.program_id(0)
        i = pl.program_id(1)
        first = _head_masks()
        if nw:
            comm = _Exchange(xin, xout, sems)
            pl.when((b == 0) & (i == 0))(comm.start)

        @pl.when((b == 0) & (i == 0))
        def _():
            dgain_ref[...] = jnp.zeros_like(dgain_ref)

        @pl.when(i == 0)
        def _():
            if lpp > lp:
                kx_scr[lp:lpp, :] = jnp.zeros((lpp - lp, 2 * ATTN_DIM), BF16)
                vx_scr[lp:lpp, :] = jnp.zeros((lpp - lp, ATTN_DIM), BF16)
                kt_scr[:, lp:lpp] = jnp.zeros((ATTN_DIM, lpp - lp), BF16)
            for p in range(N_PAIRS):
                kx_scr[0:lp, 256 * p:256 * p + 128] = k_ref[:, 128 * p:128 * (p + 1)]
                kx_scr[0:lp, 256 * p + 128:256 * (p + 1)] = ka_ref[...]
            vx_scr[0:lp, :] = v_ref[...]
            kt_scr[:, 0:lp] = _transpose_bf16(k_ref[...])
            dkx_scr[...] = jnp.zeros_like(dkx_scr)
            dvx_scr[...] = jnp.zeros_like(dvx_scr)

        rhs, rhs_t, lses, dos, dos_t, deltas = [], [], [], [], [], []
        for p in range(N_PAIRS):
            cols = slice(128 * p, 128 * (p + 1))
            side = _query_side(q_ref, qa_ref, p, first)
            rhs.append(side)
            rhs_t.append(_transpose_bf16(side))
            lses.append(jnp.concatenate([lse_ref[0, 2 * p:2 * p + 1, :], lse_ref[0, 2 * p + 1:2 * p + 2, :]], axis=1))
            ov = o_ref[:, cols].astype(F32)
            dzv = dz_ref[:, cols].astype(F32)
            gv = g_ref[:, cols]
            sq = ov * ov
            ms0 = jnp.sum(jnp.where(first, sq, 0.0), axis=1, keepdims=True) * (1.0 / HEAD_DIM)
            ms1 = jnp.sum(jnp.where(first, 0.0, sq), axis=1, keepdims=True) * (1.0 / HEAD_DIM)
            r = _pair_cols(lax.rsqrt(ms0 + EPS), lax.rsqrt(ms1 + EPS), first)
            ohat = ov * r
            dyhat = dzv * gv
            dgain_ref[:, cols] += jnp.sum(dzv * ohat, axis=0, keepdims=True)
            pr = dyhat * ohat
            mean0 = jnp.sum(jnp.where(first, pr, 0.0), axis=1, keepdims=True) * (1.0 / HEAD_DIM)
            mean1 = jnp.sum(jnp.where(first, 0.0, pr), axis=1, keepdims=True) * (1.0 / HEAD_DIM)
            do = r * (dyhat - ohat * _pair_cols(mean0, mean1, first))
            ddt = (do * ov).T
            deltas.append(jnp.concatenate([jnp.sum(ddt[0:HEAD_DIM], axis=0, keepdims=True),
                                           jnp.sum(ddt[HEAD_DIM:], axis=0, keepdims=True)], axis=1))
            do_st = _stack_heads(do.astype(BF16), first)
            dos.append(do_st)
            dos_t.append(_transpose_bf16(do_st))

        def step(c, carry):
            koff = pl.multiple_of(c * TK, TK)
            valid = _chunk_mask(i, c)
            new = []
            for p in range(N_PAIRS):
                dqt, dfq = carry[p]
                ext = slice(256 * p, 256 * (p + 1))
                cols = slice(128 * p, 128 * (p + 1))
                st = _dot(kx_scr[pl.ds(koff, TK), ext], rhs_t[p])
                st = jnp.where(valid, st, NEG)
                pt = jnp.exp(st - lses[p])
                dpt = _dot(vx_scr[pl.ds(koff, TK), cols], dos_t[p])
                dst = pt * (dpt - deltas[p])
                dsb = dst.astype(BF16)
                dfq = dfq + jnp.sum(dsb.astype(F32), axis=0, keepdims=True)
                dkx_scr[pl.ds(koff, TK), ext] += _dot(dsb, rhs[p])
                dvx_scr[pl.ds(koff, TK), cols] += _dot(pt.astype(BF16), dos[p])
                kt = _stack_heads_lanes(kt_scr[cols, pl.ds(koff, TK)])
                dqt = dqt + _dot(kt, jnp.concatenate([dsb[:, 0:TQ], dsb[:, TQ:]], axis=0))
                new.append((dqt, dfq))
            return tuple(new)

        init = tuple((jnp.zeros((128, TQ), F32), jnp.zeros((1, 2 * TQ), F32)) for _ in range(N_PAIRS))
        final = lax.fori_loop(0, (i + 2) // 2, step, init)

        for p in range(N_PAIRS):
            dqt, dfq = final[p]
            dq_ref[:, 128 * p:128 * (p + 1)] = (dqt.T * 0.125).astype(BF16)
            dfr_ref[0, 2 * p:2 * p + 1, :] = dfq[:, 0:TQ]
            dfr_ref[0, 2 * p + 1:2 * p + 2, :] = dfq[:, TQ:]

        @pl.when(i == nq - 1)
        def _():
            dka = jnp.zeros((lp, 128), F32)
            for p in range(N_PAIRS):
                dk_ref[:, 128 * p:128 * (p + 1)] = dkx_scr[0:lp, 256 * p:256 * p + 128].astype(BF16)
                dka = dka + dkx_scr[0:lp, 256 * p + 128:256 * (p + 1)]
            dka_ref[...] = dka
            dv_ref[...] = dvx_scr[0:lp, :].astype(BF16)

        if nw:
            pl.when((b == nb - 1) & (i == nq - 1))(comm.finish)

    qblk = pl.BlockSpec((TQ, ATTN_DIM), lambda b, i: (b * nq + i, 0))
    qablk = pl.BlockSpec((TQ, N_HEADS * 128), lambda b, i: (b * nq + i, 0))
    seq = pl.BlockSpec((lp, ATTN_DIM), lambda b, i: (b, 0))
    kaseq = pl.BlockSpec((lp, 128), lambda b, i: (b, 0))
    rowblk = pl.BlockSpec((1, N_HEADS, TQ), lambda b, i: (b, 0, i))
    gspec = pl.BlockSpec((1, ATTN_DIM), lambda b, i: (0, 0))
    return pl.pallas_call(
        body, name="attn_bwd", grid=(nb, nq),
        in_specs=[qblk, qblk, qablk, seq, seq, kaseq, qblk, rowblk, gspec] + [ANY] * nw,
        out_specs=[qblk, seq, seq, kaseq, rowblk, gspec] + [ANY] * nw,
        out_shape=[jax.ShapeDtypeStruct((n, ATTN_DIM), BF16), jax.ShapeDtypeStruct((n, ATTN_DIM), BF16),
                   jax.ShapeDtypeStruct((n, ATTN_DIM), BF16), jax.ShapeDtypeStruct((n, 128), F32),
                   jax.ShapeDtypeStruct((nb, N_HEADS, lp), F32), jax.ShapeDtypeStruct((1, ATTN_DIM), F32)]
        + [jax.ShapeDtypeStruct(a.shape, a.dtype) for a in exchange],
        scratch_shapes=[pltpu.VMEM((lpp, 2 * ATTN_DIM), BF16), pltpu.VMEM((lpp, ATTN_DIM), BF16),
                        pltpu.VMEM((ATTN_DIM, lpp), BF16), pltpu.VMEM((lpp, 2 * ATTN_DIM), F32),
                        pltpu.VMEM((lpp, ATTN_DIM), F32)] + (_comm_sems(nw) if nw else []),
        compiler_params=_params(("arbitrary", "arbitrary")),
    )(dz, q, qa, k, v, ka, o, lse, gain, *exchange)


def _loss_head(h, gain, target, lp):
    n = h.shape[0]
    nb = n // lp
    nq = lp // 128

    def body(h_ref, g_ref, t_ref, loss_ref, dh_ref, dgain_ref):
        b = pl.program_id(0)
        i = pl.program_id(1)

        @pl.when((b == 0) & (i == 0))
        def _():
            loss_ref[...] = jnp.zeros_like(loss_ref)
            dgain_ref[...] = jnp.zeros_like(dgain_ref)

        @pl.when(i == 0)
        def _():
            dh_ref[...] = jnp.zeros_like(dh_ref)

        @pl.when(i > 0)
        def _():
            gain_v = g_ref[...]
            y, xhat, r = _rms(h_ref[...], gain_v)
            err = y - t_ref[...]
            loss_ref[...] += 0.5 * jnp.sum(jnp.sum(err * err, axis=1, keepdims=True), axis=0,
                                           keepdims=True) * (1.0 / D_MODEL)
            dy = err * (1.0 / D_MODEL)
            dh_ref[...] = _rms_bwd(dy, xhat, r, gain_v)
            dgain_ref[...] += jnp.sum(dy * xhat, axis=0, keepdims=True)

    rows = pl.BlockSpec((128, D_MODEL), lambda b, i: (b * nq + i, 0))
    trows = pl.BlockSpec((128, D_MODEL), lambda b, i: (b * (nq - 1) + jnp.maximum(i, 1) - 1, 0))
    return pl.pallas_call(
        body, name="loss_head", grid=(nb, nq),
        in_specs=[rows, pl.BlockSpec((1, D_MODEL), lambda b, i: (0, 0)), trows],
        out_specs=[pl.BlockSpec((1, 1), lambda b, i: (0, 0)), rows, pl.BlockSpec((1, D_MODEL), lambda b, i: (0, 0))],
        out_shape=[jax.ShapeDtypeStruct((1, 1), F32), jax.ShapeDtypeStruct((n, D_MODEL), F32),
                   jax.ShapeDtypeStruct((1, D_MODEL), F32)],
        compiler_params=_params(("arbitrary", "arbitrary")),
    )(h, gain, target)


def _adamw(parts, w, m, v, name):
    s_parts, r, c = parts.shape
    tr = r
    for t in (256, 128, 64, 32, 16):
        if r % t == 0 and r > t:
            tr = t
            break

    def body(p_ref, w_ref, m_ref, v_ref, g_ref, d_ref, nm_ref, nv_ref):
        g = p_ref[0].astype(F32)
        for s in range(1, s_parts):
            g = g + p_ref[s].astype(F32)
        nm = ADAM_B1 * m_ref[...] + (1.0 - ADAM_B1) * g
        nv = ADAM_B2 * v_ref[...] + (1.0 - ADAM_B2) * (g * g)
        m_hat = nm / (1.0 - ADAM_B1 ** ADAM_STEP)
        v_hat = nv / (1.0 - ADAM_B2 ** ADAM_STEP)
        g_ref[...] = g
        d_ref[...] = -ADAM_LR * (m_hat / (jnp.sqrt(v_hat) + ADAM_EPS) + ADAM_WD * w_ref[...])
        nm_ref[...] = nm
        nv_ref[...] = nv

    blk = pl.BlockSpec((tr, c), lambda i: (i, 0))
    return pl.pallas_call(
        body, name=name, grid=(r // tr,),
        in_specs=[pl.BlockSpec((s_parts, tr, c), lambda i: (0, i, 0)), blk, blk, blk],
        out_specs=[blk] * 4,
        out_shape=[jax.ShapeDtypeStruct((r, c), F32)] * 4,
        compiler_params=_params(("parallel",)),
    )(parts, w, m, v)


def _sum_parts(parts, name):
    s_parts, r, c = parts.shape

    def body(p_ref, out_ref):
        acc = p_ref[0]
        for s in range(1, s_parts):
            acc = acc + p_ref[s]
        out_ref[...] = acc

    return pl.pallas_call(
        body, name=name, out_shape=jax.ShapeDtypeStruct((r, c), F32),
        in_specs=[pl.BlockSpec(memory_space=pltpu.VMEM)], out_specs=pl.BlockSpec(memory_space=pltpu.VMEM),
    )(parts)


SMALL_ROWS = 184


def _pack_small(d_gains, d_gc, d_ga, d_bf, d_conv, d_meta):
    rows = [g.reshape(8, 128) for g in d_gains]
    rows += [d_gc.reshape(4, 128), d_ga.reshape(4, 128), d_bf.reshape(1, 128)]
    rows += [d_conv.reshape(12, 128), d_meta.reshape(128, 128)]
    packed = jnp.concatenate(rows, axis=0)
    return jnp.pad(packed, ((0, SMALL_ROWS - packed.shape[0]), (0, 0)))


def kernel(x, meta_tokens, ffn1_norm, ffn1_w_gu, ffn1_w_down, mix_norm, w_in, conv_w, b_f, out_norm_conv, out_norm_attn, w_out, ffn2_norm, ffn2_w_gu, ffn2_w_down, final_norm, loss_target, m_meta_tokens, m_ffn1_norm, m_ffn1_w_gu, m_ffn1_w_down, m_mix_norm, m_w_in, m_conv_w, m_b_f, m_out_norm_conv, m_out_norm_attn, m_w_out, m_ffn2_norm, m_ffn2_w_gu, m_ffn2_w_down, m_final_norm, v_meta_tokens, v_ffn1_norm, v_ffn1_w_gu, v_ffn1_w_down, v_mix_norm, v_w_in, v_conv_w, v_b_f, v_out_norm_conv, v_out_norm_attn, v_w_out, v_ffn2_norm, v_ffn2_w_gu, v_ffn2_w_down, v_final_norm):
    nb, seq, _ = x.shape
    lp = PAD + N_META + seq
    n = nb * lp
    me = 4 * lax.axis_index("x") + 2 * lax.axis_index("y") + lax.axis_index("c")

    wgu1_8, wd1_8 = _all_gather([ffn1_w_gu[0].T.astype(BF16), ffn1_w_down[0].astype(BF16)], "gather_ffn1")
    small_in = jnp.concatenate(
        [meta_tokens, jnp.pad(conv_w[0], ((0, 0), (0, 128 - conv_w.shape[2]))), jnp.zeros((5, 128), F32)], axis=0)
    (small_8,) = _all_gather([small_in], "gather_small")
    meta_full = small_8[:, 0:N_META, :].transpose(1, 0, 2).reshape(N_META, D_MODEL)
    conv_full = small_8[:, N_META:N_META + 3, 0:CONV_DIM // N_DEV].transpose(1, 0, 2).reshape(3, CONV_DIM)
    wgu1 = wgu1_8.reshape(W_GU_SHAPE)
    wd1 = wd1_8.reshape(W_D_SHAPE)
    b_f_row = jnp.pad(b_f, ((0, 0), (0, 128 - N_HEADS)))
    gmat = _group_matrix()

    h0 = jnp.concatenate([jnp.zeros((nb, PAD, D_MODEL), F32),
                          jnp.broadcast_to(meta_full[None], (nb, N_META, D_MODEL)), x], axis=1).reshape(n, D_MODEL)
    later = [w_in[0].T.astype(BF16), w_out[0].astype(BF16), ffn2_w_gu[0].T.astype(BF16), ffn2_w_down[0].astype(BF16)]
    h1, n1, gate1, up1, win_8, wout_8, wgu2_8, wd2_8 = _ffn_fwd(h0, ffn1_norm, wgu1, wd1, "ffn1_fwd", gather=later)
    wgu2 = wgu2_8.reshape(W_GU_SHAPE)
    wd2 = wd2_8.reshape(W_D_SHAPE)
    w_in_full = jnp.pad(win_8.reshape(IN_DIM, D_MODEL), ((0, IN_PAD - IN_DIM), (0, 0)))
    w_out_full = wout_8.reshape(D_MODEL, D_MODEL)

    bg, cg, hc, q, k, v, fg = _inproj_fwd(h1, mix_norm, w_in_full)
    zc = _conv_fwd(bg, cg, hc, conv_full, out_norm_conv, gmat, lp)
    ka, qa = _fgate_fwd(fg, b_f_row, lp)
    za, o, lse = _attn_fwd(q, qa, k, v, ka, out_norm_attn, lp)
    h2 = _outproj_fwd(zc, za, w_out_full, h1)
    h3, n3, gate2, up2 = _ffn_fwd(h2, ffn2_norm, wgu2, wd2, "ffn2_fwd")
    loss_part, dh3, d_final = _loss_head(h3, final_norm.reshape(1, D_MODEL), loss_target.reshape(nb * seq, D_MODEL), lp)

    dh2, dhb3, dgate2, dup2, d_ffn2 = _ffn_bwd_x(dh3, h2, ffn2_norm, gate2, up2, wgu2, wd2, "ffn2_bwd_x")
    dwgu2, dwd2 = _ffn_bwd_w(dhb3, n3, gate2, up2, dgate2, dup2, "ffn2_bwd_w")
    dzc, dza, dwout = _outproj_bwd(dh2, zc, za, w_out_full)
    send_a = [dwgu2.reshape(N_DEV, F_CHUNK, D_MODEL), dwd2.reshape(N_DEV, F_CHUNK // 2, D_MODEL),
              dwout.reshape(N_DEV, D_MODEL // N_DEV, D_MODEL)]
    dq, dk, dv, dka, dfr, d_ga, p_wgu2, p_wd2, p_wout = _attn_bwd(
        dza, q, qa, k, v, ka, o, lse, out_norm_attn, lp, exchange=send_a)
    dfg, d_bf = _fgate_bwd(dka, dfr, fg, b_f_row, lp)
    dbg, dcg, dhc, d_conv, d_gc = _conv_bwd(dzc, bg, cg, hc, conv_full, out_norm_conv, gmat, lp)
    dh1, dwin, d_mix = _inproj_bwd([dbg, dcg, dhc, dq, dk, dv], dfg, dh2, h1, mix_norm, w_in_full)
    dwin_8 = dwin[0:IN_DIM].reshape(N_DEV, IN_DIM // N_DEV, D_MODEL)
    dhb1, dgate1, dup1, p_win = _ffn_bwd_act(dh1, gate1, up1, wd1, "ffn1_bwd_act", exchange=[dwin_8])
    dwgu1, dwd1 = _ffn_bwd_w(dhb1, n1, gate1, up1, dgate1, dup1, "ffn1_bwd_w")
    own = [dwgu1.reshape(N_DEV, F_CHUNK, D_MODEL), dwd1.reshape(N_DEV, F_CHUNK // 2, D_MODEL)]
    got = _pair_exchange(own, "pair_exchange_ffn1")
    chip_sums = [_pair_sum(own[0], got[0], "pair_sum_wgu1"), _pair_sum(own[1], got[1], "pair_sum_wd1")]
    dh0, d_ffn1, p_wgu1, p_wd1 = _ffn_bwd_in(
        dh1, h0, ffn1_norm, dgate1, dup1, wgu1, "ffn1_bwd_in", exchange=chip_sums)

    dh0 = dh0.reshape(nb, lp, D_MODEL)
    grad_x = dh0[:, PAD + N_META:, :]
    d_meta = jnp.sum(dh0[:, PAD:PAD + N_META, :], axis=0)

    small = _pack_small([d_ffn1, d_mix, d_ffn2, d_final], d_gc, d_ga, d_bf, d_conv, d_meta)
    (small_all,) = _all_gather([small], "gather_small_grads")
    small_sum = _sum_parts(small_all, "sum_small_grads")
    g_ffn1n, g_mixn, g_ffn2n, g_finaln = (small_sum[8 * t:8 * t + 8].reshape(1, D_MODEL) for t in range(4))
    g_gc = small_sum[32:36].reshape(1, CONV_DIM)
    g_ga = small_sum[36:40].reshape(1, ATTN_DIM)
    g_bf = small_sum[40:41, 0:N_HEADS]
    g_conv_full = small_sum[41:53].reshape(3, CONV_DIM)
    g_meta_full = small_sum[53:181].reshape(N_META, D_MODEL)
    g_conv = lax.dynamic_slice_in_dim(g_conv_full, me * (CONV_DIM // N_DEV), CONV_DIM // N_DEV, axis=1)
    g_meta = lax.dynamic_slice_in_dim(g_meta_full, me * (D_MODEL // N_DEV), D_MODEL // N_DEV, axis=1)

    weights = {
        "meta_tokens": (g_meta[None], meta_tokens, m_meta_tokens, v_meta_tokens),
        "ffn1_norm": (g_ffn1n[None], ffn1_norm, m_ffn1_norm, v_ffn1_norm),
        "ffn1_w_gu": (p_wgu1, ffn1_w_gu[0].T, m_ffn1_w_gu[0].T, v_ffn1_w_gu[0].T),
        "ffn1_w_down": (p_wd1, ffn1_w_down[0], m_ffn1_w_down[0], v_ffn1_w_down[0]),
        "mix_norm": (g_mixn[None], mix_norm, m_mix_norm, v_mix_norm),
        "w_in": (p_win, w_in[0].T, m_w_in[0].T, v_w_in[0].T),
        "conv_w": (g_conv[None], conv_w[0], m_conv_w[0], v_conv_w[0]),
        "b_f": (g_bf[None], b_f, m_b_f, v_b_f),
        "out_norm_conv": (g_gc[None], out_norm_conv, m_out_norm_conv, v_out_norm_conv),
        "out_norm_attn": (g_ga[None], out_norm_attn, m_out_norm_attn, v_out_norm_attn),
        "w_out": (p_wout, w_out[0], m_w_out[0], v_w_out[0]),
        "ffn2_norm": (g_ffn2n[None], ffn2_norm, m_ffn2_norm, v_ffn2_norm),
        "ffn2_w_gu": (p_wgu2, ffn2_w_gu[0].T, m_ffn2_w_gu[0].T, v_ffn2_w_gu[0].T),
        "ffn2_w_down": (p_wd2, ffn2_w_down[0], m_ffn2_w_down[0], v_ffn2_w_down[0]),
        "final_norm": (g_finaln[None], final_norm.reshape(1, D_MODEL), m_final_norm.reshape(1, D_MODEL),
                       v_final_norm.reshape(1, D_MODEL)),
    }
    shapes = {"meta_tokens": meta_tokens.shape, "ffn1_norm": ffn1_norm.shape, "ffn1_w_gu": ffn1_w_gu.shape,
              "ffn1_w_down": ffn1_w_down.shape, "mix_norm": mix_norm.shape, "w_in": w_in.shape,
              "conv_w": conv_w.shape, "b_f": b_f.shape, "out_norm_conv": out_norm_conv.shape,
              "out_norm_attn": out_norm_attn.shape, "w_out": w_out.shape, "ffn2_norm": ffn2_norm.shape,
              "ffn2_w_gu": ffn2_w_gu.shape, "ffn2_w_down": ffn2_w_down.shape, "final_norm": final_norm.shape}
    grads, deltas, new_m, new_v = [], [], [], []
    for name, (p, w, m, vv) in weights.items():
        g, d, nm, nv = _adamw(p, w, m, vv, "adamw_" + name)
        if name in ("ffn1_w_gu", "ffn2_w_gu", "w_in"):
            g, d, nm, nv = g.T, d.T, nm.T, nv.T
        shape = shapes[name]
        grads.append(g.reshape(shape))
        deltas.append(d.reshape(shape))
        new_m.append(nm.reshape(shape))
        new_v.append(nv.reshape(shape))

    loss = lax.psum(loss_part[0, 0], ("x", "y", "c"))
    return (loss, grad_x, *grads, *deltas, *new_m, *new_v)
```

```python
import jax
import jax.numpy as jnp
from jax import lax
from jax.experimental import pallas as pl
from jax.experimental.pallas import tpu as pltpu

F32 = jnp.float32
BF16 = jnp.bfloat16

N_DEV = 8
D_MODEL = 1024
N_META = 16
PAD = 128 - N_META
CONV_DIM = 512
ATTN_DIM = 512
HEAD_DIM = 64
N_HEADS = 8
N_PAIRS = N_HEADS // 2
D_FF = 2816
N_CHUNK = 4
F_CHUNK = D_FF // N_CHUNK
IN_DIM = 3080
IN_PAD = 3200
IN_MAIN = 3072
EPS = 1e-6
NEG = -1e30
TQ = 128
TK = 512
VMEM_LIMIT = 56 * 1024 * 1024

ADAM_LR = 0.001
ADAM_B1 = 0.9
ADAM_B2 = 0.999
ADAM_EPS = 1e-08
ADAM_WD = 0.01
ADAM_STEP = 10

MESH = pl.DeviceIdType.MESH
ANY = pl.BlockSpec(memory_space=pl.ANY)


def _params(sem=None):
    return pltpu.CompilerParams(dimension_semantics=sem, vmem_limit_bytes=VMEM_LIMIT)


def _row_tile(n, prefer):
    for t in (prefer, 512, 256, 128):
        if t <= n and n % t == 0:
            return t
    raise ValueError(f"no row tile for {n}")


def _dot(a, b):
    return jnp.dot(a, b, preferred_element_type=F32)


def _dot_nt(a, b):
    return lax.dot_general(a, b, (((1,), (1,)), ((), ())), preferred_element_type=F32)


def _dot_tn(a, b):
    return lax.dot_general(a, b, (((0,), (0,)), ((), ())), preferred_element_type=F32)


def _rms(x, g):
    r = lax.rsqrt(jnp.mean(x * x, axis=-1, keepdims=True) + EPS)
    xhat = x * r
    return xhat * g, xhat, r


def _rms_bwd(dn, xhat, r, g):
    dxhat = dn * g
    return r * (dxhat - xhat * jnp.mean(dxhat * xhat, axis=-1, keepdims=True))


def _sigmoid(x):
    return 1.0 / (1.0 + jnp.exp(-x))


def _place():
    return lax.axis_index("x"), lax.axis_index("y"), lax.axis_index("c")


def _comm_sems(nw):
    return [pltpu.SemaphoreType.DMA((nw, 7)), pltpu.SemaphoreType.DMA((nw, 7)), pltpu.SemaphoreType.DMA((nw,))]


class _Gather:
    def __init__(self, ins, outs, sems):
        self.ins, self.outs = ins, outs
        self.send, self.recv, self.local = sems
        x, y, c = _place()
        self.c = c
        self.me, self.sibling = (x, y, c), (x, y, 1 - c)
        self.chips = [(1 - x, y), (x, 1 - y), (1 - x, 1 - y)]

    def _copy(self, w, k, block, to, own=False):
        slot = self.outs[w].at[4 * block[0] + 2 * block[1] + block[2]]
        return pltpu.make_async_remote_copy(
            src_ref=self.ins[w] if own else slot, dst_ref=slot,
            send_sem=self.send.at[w, k], recv_sem=self.recv.at[w, k], device_id=to, device_id_type=MESH)

    def _mine(self, w):
        x, y, c = self.me
        return pltpu.make_async_copy(self.ins[w], self.outs[w].at[4 * x + 2 * y + c], self.local.at[w])

    def _first(self, w):
        return ([self._copy(w, 0, self.me, self.sibling, own=True)]
                + [self._copy(w, 1 + j, self.me, (*chip, self.c), own=True) for j, chip in enumerate(self.chips)])

    def _passed(self, w):
        return [self._copy(w, 4 + j, (*chip, self.c), self.sibling) for j, chip in enumerate(self.chips)]

    def start(self):
        for w in range(len(self.ins)):
            self._mine(w).start()
        for w in range(len(self.ins)):
            for cp in self._first(w):
                cp.start()

    def forward(self):
        for w in range(len(self.ins)):
            for j, chip in enumerate(self.chips):
                self._copy(w, 1 + j, (*chip, self.c), self.me).wait_recv()
                self._passed(w)[j].start()

    def finish(self):
        for w in range(len(self.ins)):
            self._copy(w, 0, self.sibling, self.me).wait_recv()
            for j, chip in enumerate(self.chips):
                self._copy(w, 4 + j, (*chip, 1 - self.c), self.me).wait_recv()
        for w in range(len(self.ins)):
            for cp in self._first(w) + self._passed(w):
                cp.wait_send()
            self._mine(w).wait()


class _Exchange:
    def __init__(self, ins, outs, sems):
        self.ins, self.outs = ins, outs
        self.send, self.recv, self.local = sems
        self.x, self.y, self.c = _place()
        self.me = 4 * self.x + 2 * self.y + self.c

    def _copy(self, w, k):
        flip = lambda v, bit: 1 - v if bit else v
        peer = (flip(self.x, ((k + 1) >> 2) & 1), flip(self.y, ((k + 1) >> 1) & 1), flip(self.c, (k + 1) & 1))
        return pltpu.make_async_remote_copy(
            src_ref=self.ins[w].at[4 * peer[0] + 2 * peer[1] + peer[2]], dst_ref=self.outs[w].at[self.me],
            send_sem=self.send.at[w, k], recv_sem=self.recv.at[w, k], device_id=peer, device_id_type=MESH)

    def _mine(self, w):
        return pltpu.make_async_copy(self.ins[w].at[self.me], self.outs[w].at[self.me], self.local.at[w])

    def start(self):
        for w in range(len(self.ins)):
            self._mine(w).start()
            for k in range(N_DEV - 1):
                self._copy(w, k).start()

    def finish(self):
        for w in range(len(self.ins)):
            for k in range(N_DEV - 1):
                self._copy(w, k).wait()
            self._mine(w).wait()


class _PairExchange:
    def __init__(self, ins, outs, sems):
        self.ins, self.outs = ins, outs
        self.send, self.recv, _ = sems
        x, y, self.c = _place()
        self.sibling = (x, y, 1 - self.c)

    def _copy(self, w, t):
        return pltpu.make_async_remote_copy(
            src_ref=self.ins[w].at[2 * t + 1 - self.c], dst_ref=self.outs[w].at[t],
            send_sem=self.send.at[w, t], recv_sem=self.recv.at[w, t], device_id=self.sibling, device_id_type=MESH)

    def start(self):
        for w in range(len(self.ins)):
            for t in range(4):
                self._copy(w, t).start()

    def finish(self):
        for w in range(len(self.ins)):
            for t in range(4):
                self._copy(w, t).wait()


class _ChipExchange:
    def __init__(self, ins, outs, sems):
        self.ins, self.outs = ins, outs
        self.send, self.recv, self.local = sems
        self.x, self.y, self.c = _place()
        self.chip = 2 * self.x + self.y

    def _copy(self, w, k):
        flip = lambda v, bit: 1 - v if bit else v
        px, py = flip(self.x, ((k + 1) >> 1) & 1), flip(self.y, (k + 1) & 1)
        return pltpu.make_async_remote_copy(
            src_ref=self.ins[w].at[2 * px + py], dst_ref=self.outs[w].at[self.chip],
            send_sem=self.send.at[w, k], recv_sem=self.recv.at[w, k], device_id=(px, py, self.c),
            device_id_type=MESH)

    def _mine(self, w):
        return pltpu.make_async_copy(self.ins[w].at[self.chip], self.outs[w].at[self.chip], self.local.at[w])

    def start(self):
        for w in range(len(self.ins)):
            self._mine(w).start()
            for k in range(3):
                self._copy(w, k).start()

    def finish(self):
        for w in range(len(self.ins)):
            for k in range(3):
                self._copy(w, k).wait()
            self._mine(w).wait()


def _pair_exchange(xs, name):
    nw = len(xs)

    def body(*refs):
        comm = _PairExchange(refs[:nw], refs[nw:2 * nw], refs[2 * nw:])
        comm.start()
        comm.finish()

    return pl.pallas_call(
        body, name=name, in_specs=[ANY] * nw, out_specs=[ANY] * nw,
        out_shape=[jax.ShapeDtypeStruct((4,) + a.shape[1:], a.dtype) for a in xs],
        scratch_shapes=_comm_sems(nw),
    )(*xs)


def _pair_sum(own, got, name):
    _, r, c = own.shape
    tr = r
    for t in (256, 128, 64, 32, 16):
        if r % t == 0 and r > t:
            tr = t
            break

    def body(own_ref, got_ref, out_ref):
        mine = jnp.where(lax.axis_index("c") == 0, own_ref[:, 0].astype(F32), own_ref[:, 1].astype(F32))
        out_ref[...] = (mine + got_ref[...].astype(F32)).astype(BF16)

    return pl.pallas_call(
        body, name=name, grid=(r // tr,),
        in_specs=[pl.BlockSpec((4, 2, tr, c), lambda i: (0, 0, i, 0)), pl.BlockSpec((4, tr, c), lambda i: (0, i, 0))],
        out_specs=pl.BlockSpec((4, tr, c), lambda i: (0, i, 0)),
        out_shape=jax.ShapeDtypeStruct((4, r, c), BF16),
        compiler_params=_params(("parallel",)),
    )(own.reshape(4, 2, r, c), got)


def _split_refs(refs, n_in, n_comm, n_out, n_scr):
    a = n_in
    b = a + n_comm
    c = b + n_out
    d = c + n_comm
    e = d + n_scr
    return refs[:a], refs[a:b], refs[b:c], refs[c:d], refs[d:e], refs[e:]


def _all_gather(xs, name):
    nw = len(xs)

    def body(*refs):
        comm = _Gather(refs[:nw], refs[nw:2 * nw], refs[2 * nw:])
        comm.start()
        comm.forward()
        comm.finish()

    return pl.pallas_call(
        body, name=name, in_specs=[ANY] * nw, out_specs=[ANY] * nw,
        out_shape=[jax.ShapeDtypeStruct((N_DEV,) + a.shape, a.dtype) for a in xs],
        scratch_shapes=_comm_sems(nw),
    )(*xs)


def _ffn_fwd(h, gain, wgu, wd, name, gather=()):
    n = h.shape[0]
    tm = _row_tile(n, 512)
    n_i = n // tm
    nw = len(gather)

    def body(*refs):
        (h_ref, g_ref, wgu_ref, wd_ref), gin, (out_ref, gate_ref, up_ref), gout, (n_scr, acc_scr), sems = \
            _split_refs(refs, 4, nw, 3, 2)
        i = pl.program_id(0)
        j = pl.program_id(1)
        if nw:
            comm = _Gather(gin, gout, sems)
            pl.when((i == 0) & (j == 0))(comm.start)
            pl.when((i == (3 * n_i) // 4) & (j == 0))(comm.forward)

        @pl.when(j == 0)
        def _():
            y, _, _ = _rms(h_ref[...], g_ref[...])
            n_scr[...] = y.astype(BF16)
            acc_scr[...] = jnp.zeros_like(acc_scr)

        nb = n_scr[...]
        gate = _dot(nb, wgu_ref[0, 0])
        up = _dot(nb, wgu_ref[1, 0])
        gate_ref[0] = gate.astype(BF16)
        up_ref[0] = up.astype(BF16)
        act = (gate * _sigmoid(gate) * up).astype(BF16)
        acc_scr[...] += _dot(act, wd_ref[0])

        @pl.when(j == N_CHUNK - 1)
        def _():
            out_ref[...] = h_ref[...] + 0.5 * acc_scr[...]

        if nw:
            pl.when((i == n_i - 1) & (j == N_CHUNK - 1))(comm.finish)

    return pl.pallas_call(
        body, name=name, grid=(n_i, N_CHUNK),
        in_specs=[pl.BlockSpec((tm, D_MODEL), lambda i, j: (i, 0)),
                  pl.BlockSpec((1, D_MODEL), lambda i, j: (0, 0)),
                  pl.BlockSpec((2, 1, D_MODEL, F_CHUNK), lambda i, j: (0, j, 0, 0)),
                  pl.BlockSpec((1, F_CHUNK, D_MODEL), lambda i, j: (j, 0, 0))] + [ANY] * nw,
        out_specs=[pl.BlockSpec((tm, D_MODEL), lambda i, j: (i, 0)),
                   pl.BlockSpec((1, tm, F_CHUNK), lambda i, j: (j, i, 0)),
                   pl.BlockSpec((1, tm, F_CHUNK), lambda i, j: (j, i, 0))] + [ANY] * nw,
        out_shape=[jax.ShapeDtypeStruct((n, D_MODEL), F32),
                   jax.ShapeDtypeStruct((N_CHUNK, n, F_CHUNK), BF16),
                   jax.ShapeDtypeStruct((N_CHUNK, n, F_CHUNK), BF16)]
        + [jax.ShapeDtypeStruct((N_DEV,) + a.shape, a.dtype) for a in gather],
        scratch_shapes=[pltpu.VMEM((tm, D_MODEL), BF16), pltpu.VMEM((tm, D_MODEL), F32)]
        + (_comm_sems(nw) if nw else []),
        compiler_params=_params(("arbitrary", "arbitrary")),
    )(h, gain, wgu, wd, *gather)


def _ffn_bwd_x(dh_out, h_in, gain, gate, up, wgu, wd, name):
    n = h_in.shape[0]
    tm = _row_tile(n, 512)

    def body(dh_ref, h_ref, g_ref, gate_ref, up_ref, wgu_ref, wd_ref,
             dhin_ref, dgate_ref, dup_ref, dgain_ref, dhb_scr, acc_scr):
        i = pl.program_id(0)
        j = pl.program_id(1)

        @pl.when((i == 0) & (j == 0))
        def _():
            dgain_ref[...] = jnp.zeros_like(dgain_ref)

        @pl.when(j == 0)
        def _():
            dhb_scr[...] = (0.5 * dh_ref[...]).astype(BF16)
            acc_scr[...] = jnp.zeros_like(acc_scr)

        da = _dot_nt(dhb_scr[...], wd_ref[0])
        g = gate_ref[0].astype(F32)
        u = up_ref[0].astype(F32)
        sig = _sigmoid(g)
        dgate = (da * u * (sig * (1.0 + g * (1.0 - sig)))).astype(BF16)
        dup = (da * (g * sig)).astype(BF16)
        dgate_ref[0] = dgate
        dup_ref[0] = dup
        acc_scr[...] += _dot_nt(dgate, wgu_ref[0, 0]) + _dot_nt(dup, wgu_ref[1, 0])

        @pl.when(j == N_CHUNK - 1)
        def _():
            gain_v = g_ref[...]
            _, xhat, r = _rms(h_ref[...], gain_v)
            dn = acc_scr[...]
            dhin_ref[...] = dh_ref[...] + _rms_bwd(dn, xhat, r, gain_v)
            dgain_ref[...] += jnp.sum(dn * xhat, axis=0, keepdims=True)

    chunk = pl.BlockSpec((1, tm, F_CHUNK), lambda i, j: (j, i, 0))
    rows = pl.BlockSpec((tm, D_MODEL), lambda i, j: (i, 0))
    vec = pl.BlockSpec((1, D_MODEL), lambda i, j: (0, 0))
    return pl.pallas_call(
        body, name=name, grid=(n // tm, N_CHUNK),
        in_specs=[rows, rows, vec, chunk, chunk,
                  pl.BlockSpec((2, 1, D_MODEL, F_CHUNK), lambda i, j: (0, j, 0, 0)),
                  pl.BlockSpec((1, F_CHUNK, D_MODEL), lambda i, j: (j, 0, 0))],
        out_specs=[rows, chunk, chunk, vec],
        out_shape=[jax.ShapeDtypeStruct((n, D_MODEL), F32),
                   jax.ShapeDtypeStruct((N_CHUNK, n, F_CHUNK), BF16),
                   jax.ShapeDtypeStruct((N_CHUNK, n, F_CHUNK), BF16),
                   jax.ShapeDtypeStruct((1, D_MODEL), F32)],
        scratch_shapes=[pltpu.VMEM((tm, D_MODEL), BF16), pltpu.VMEM((tm, D_MODEL), F32)],
        compiler_params=_params(("arbitrary", "arbitrary")),
    )(dh_out, h_in, gain, gate, up, wgu, wd)


def _ffn_bwd_act(dh_out, gate, up, wd, name):
    n = dh_out.shape[0]
    tm = _row_tile(n, 512)

    def body(dh_ref, gate_ref, up_ref, wd_ref, dgate_ref, dup_ref, dhb_scr):
        @pl.when(pl.program_id(1) == 0)
        def _():
            dhb_scr[...] = (0.5 * dh_ref[...]).astype(BF16)

        da = _dot_nt(dhb_scr[...], wd_ref[0])
        g = gate_ref[0].astype(F32)
        u = up_ref[0].astype(F32)
        sig = _sigmoid(g)
        dgate_ref[0] = (da * u * (sig * (1.0 + g * (1.0 - sig)))).astype(BF16)
        dup_ref[0] = (da * (g * sig)).astype(BF16)

    chunk = pl.BlockSpec((1, tm, F_CHUNK), lambda i, j: (j, i, 0))
    return pl.pallas_call(
        body, name=name, grid=(n // tm, N_CHUNK),
        in_specs=[pl.BlockSpec((tm, D_MODEL), lambda i, j: (i, 0)), chunk, chunk,
                  pl.BlockSpec((1, F_CHUNK, D_MODEL), lambda i, j: (j, 0, 0))],
        out_specs=[chunk, chunk],
        out_shape=[jax.ShapeDtypeStruct((N_CHUNK, n, F_CHUNK), BF16)] * 2,
        scratch_shapes=[pltpu.VMEM((tm, D_MODEL), BF16)],
        compiler_params=_params(("parallel", "arbitrary")),
    )(dh_out, gate, up, wd)


def _ffn_bwd_in(dh_out, h_in, gain, dgate, dup, wgu, name, exchange=()):
    n = h_in.shape[0]
    tm = _row_tile(n, 512)
    n_i = n // tm
    nw = len(exchange)

    def body(*refs):
        (dh_ref, h_ref, g_ref, dgate_ref, dup_ref, wgu_ref), xin, (dhin_ref, dgain_ref), xout, (acc_scr,), sems = \
            _split_refs(refs, 6, nw, 2, 1)
        i = pl.program_id(0)
        j = pl.program_id(1)
        if nw:
            comm = _Exchange(xin, xout, sems)
            pl.when((i == 0) & (j == 0))(comm.start)

        @pl.when((i == 0) & (j == 0))
        def _():
            dgain_ref[...] = jnp.zeros_like(dgain_ref)

        @pl.when(j == 0)
        def _():
            acc_scr[...] = jnp.zeros_like(acc_scr)

        acc_scr[...] += _dot_nt(dgate_ref[0], wgu_ref[0, 0]) + _dot_nt(dup_ref[0], wgu_ref[1, 0])

        @pl.when(j == N_CHUNK - 1)
        def _():
            gain_v = g_ref[...]
            _, xhat, r = _rms(h_ref[...], gain_v)
            dn = acc_scr[...]
            dhin_ref[...] = dh_ref[...] + _rms_bwd(dn, xhat, r, gain_v)
            dgain_ref[...] += jnp.sum(dn * xhat, axis=0, keepdims=True)

        if nw:
            pl.when((i == n_i - 1) & (j == N_CHUNK - 1))(comm.finish)

    chunk = pl.BlockSpec((1, tm, F_CHUNK), lambda i, j: (j, i, 0))
    rows = pl.BlockSpec((tm, D_MODEL), lambda i, j: (i, 0))
    vec = pl.BlockSpec((1, D_MODEL), lambda i, j: (0, 0))
    return pl.pallas_call(
        body, name=name, grid=(n_i, N_CHUNK),
        in_specs=[rows, rows, vec, chunk, chunk,
                  pl.BlockSpec((2, 1, D_MODEL, F_CHUNK), lambda i, j: (0, j, 0, 0))] + [ANY] * nw,
        out_specs=[rows, vec] + [ANY] * nw,
        out_shape=[jax.ShapeDtypeStruct((n, D_MODEL), F32), jax.ShapeDtypeStruct((1, D_MODEL), F32)]
        + [jax.ShapeDtypeStruct(a.shape, a.dtype) for a in exchange],
        scratch_shapes=[pltpu.VMEM((tm, D_MODEL), F32)] + (_comm_sems(nw) if nw else []),
        compiler_params=_params(("arbitrary", "arbitrary")),
    )(dh_out, h_in, gain, dgate, dup, wgu, *exchange)


def _ffn_bwd_w(dh_out, h_in, gain, gate, up, dgate, dup, name):
    n = h_in.shape[0]
    tm = _row_tile(n, 512)
    n_i = n // tm

    def body(dh_ref, h_ref, g_ref, gate_ref, up_ref, dgate_ref, dup_ref, dwgu_ref, dwd_ref,
             ag_scr, au_scr, ad_scr):
        i = pl.program_id(1)

        @pl.when(i == 0)
        def _():
            ag_scr[...] = jnp.zeros_like(ag_scr)
            au_scr[...] = jnp.zeros_like(au_scr)
            ad_scr[...] = jnp.zeros_like(ad_scr)

        y, _, _ = _rms(h_ref[...], g_ref[...])
        nb = y.astype(BF16)
        ag_scr[...] += _dot_tn(nb, dgate_ref[0])
        au_scr[...] += _dot_tn(nb, dup_ref[0])
        g = gate_ref[0].astype(F32)
        act = (g * _sigmoid(g) * up_ref[0].astype(F32)).astype(BF16)
        ad_scr[...] += _dot_tn(act, (0.5 * dh_ref[...]).astype(BF16))

        @pl.when(i == n_i - 1)
        def _():
            dwgu_ref[0, 0] = ag_scr[...].astype(BF16)
            dwgu_ref[1, 0] = au_scr[...].astype(BF16)
            dwd_ref[0] = ad_scr[...].astype(BF16)

    chunk = pl.BlockSpec((1, tm, F_CHUNK), lambda j, i: (j, i, 0))
    rows = pl.BlockSpec((tm, D_MODEL), lambda j, i: (i, 0))
    return pl.pallas_call(
        body, name=name, grid=(N_CHUNK, n_i),
        in_specs=[rows, rows, pl.BlockSpec((1, D_MODEL), lambda j, i: (0, 0)), chunk, chunk, chunk, chunk],
        out_specs=[pl.BlockSpec((2, 1, D_MODEL, F_CHUNK), lambda j, i: (0, j, 0, 0)),
                   pl.BlockSpec((1, F_CHUNK, D_MODEL), lambda j, i: (j, 0, 0))],
        out_shape=[jax.ShapeDtypeStruct((2, N_CHUNK, D_MODEL, F_CHUNK), BF16),
                   jax.ShapeDtypeStruct((N_CHUNK, F_CHUNK, D_MODEL), BF16)],
        scratch_shapes=[pltpu.VMEM((D_MODEL, F_CHUNK), F32), pltpu.VMEM((D_MODEL, F_CHUNK), F32),
                        pltpu.VMEM((F_CHUNK, D_MODEL), F32)],
        compiler_params=_params(("parallel", "arbitrary")),
    )(dh_out, h_in, gain, gate, up, dgate, dup)


def _resident(shape, rank):
    zeros = (0,) * len(shape)
    index_map = (lambda i: zeros) if rank == 1 else (lambda i, j: zeros)
    return pl.BlockSpec(shape, index_map, pipeline_mode=pl.Buffered(1))


W_GU_SHAPE = (2, N_CHUNK, F_CHUNK, D_MODEL)
W_D_SHAPE = (N_CHUNK, F_CHUNK, D_MODEL)


def _ffn_fwd(h, gain, wgu, wd, name, gather=()):
    n = h.shape[0]
    tm = _row_tile(n, 512)
    n_i = n // tm
    nw = len(gather)

    def body(*refs):
        (h_ref, g_ref, wgu_ref, wd_ref), gin, (out_ref, nrm_ref, gate_ref, up_ref), gout, _, sems = \
            _split_refs(refs, 4, nw, 4, 0)
        i = pl.program_id(0)
        if nw:
            comm = _Gather(gin, gout, sems)
            pl.when(i == 0)(comm.start)
            pl.when(i == (3 * n_i) // 4)(comm.forward)

        hv = h_ref[...]
        y, _, _ = _rms(hv, g_ref[...])
        nb = y.astype(BF16)
        nrm_ref[...] = nb
        acc = jnp.zeros((tm, D_MODEL), F32)
        for j in range(N_CHUNK):
            gate = _dot_nt(nb, wgu_ref[0, j])
            up = _dot_nt(nb, wgu_ref[1, j])
            gate_ref[j] = gate.astype(BF16)
            up_ref[j] = up.astype(BF16)
            acc = acc + _dot((gate * _sigmoid(gate) * up).astype(BF16), wd_ref[j])
        out_ref[...] = hv + 0.5 * acc

        if nw:
            pl.when(i == n_i - 1)(comm.finish)

    rows = pl.BlockSpec((tm, D_MODEL), lambda i: (i, 0))
    chunks = pl.BlockSpec((N_CHUNK, tm, F_CHUNK), lambda i: (0, i, 0))
    return pl.pallas_call(
        body, name=name, grid=(n_i,),
        in_specs=[rows, pl.BlockSpec((1, D_MODEL), lambda i: (0, 0)), _resident(W_GU_SHAPE, 1),
                  _resident(W_D_SHAPE, 1)] + [ANY] * nw,
        out_specs=[rows, rows, chunks, chunks] + [ANY] * nw,
        out_shape=[jax.ShapeDtypeStruct((n, D_MODEL), F32), jax.ShapeDtypeStruct((n, D_MODEL), BF16),
                   jax.ShapeDtypeStruct((N_CHUNK, n, F_CHUNK), BF16),
                   jax.ShapeDtypeStruct((N_CHUNK, n, F_CHUNK), BF16)]
        + [jax.ShapeDtypeStruct((N_DEV,) + a.shape, a.dtype) for a in gather],
        scratch_shapes=_comm_sems(nw) if nw else [],
        compiler_params=_params(("arbitrary",)),
    )(h, gain, wgu, wd, *gather)


def _swiglu_bwd(da, gate_ref, up_ref, j):
    g = gate_ref[j].astype(F32)
    u = up_ref[j].astype(F32)
    sig = _sigmoid(g)
    return (da * u * (sig * (1.0 + g * (1.0 - sig)))).astype(BF16), (da * (g * sig)).astype(BF16)


def _ffn_bwd_x(dh_out, h_in, gain, gate, up, wgu, wd, name):
    n = h_in.shape[0]
    tm = _row_tile(n, 256)

    def body(dh_ref, h_ref, g_ref, gate_ref, up_ref, wgu_ref, wd_ref,
             dhin_ref, dhb_ref, dgate_ref, dup_ref, dgain_ref):
        @pl.when(pl.program_id(0) == 0)
        def _():
            dgain_ref[...] = jnp.zeros_like(dgain_ref)

        dhv = dh_ref[...]
        dhb = (0.5 * dhv).astype(BF16)
        dhb_ref[...] = dhb
        dn = jnp.zeros((tm, D_MODEL), F32)
        for j in range(N_CHUNK):
            dgate, dup = _swiglu_bwd(_dot_nt(dhb, wd_ref[j]), gate_ref, up_ref, j)
            dgate_ref[j] = dgate
            dup_ref[j] = dup
            dn = dn + _dot(dgate, wgu_ref[0, j]) + _dot(dup, wgu_ref[1, j])
        gain_v = g_ref[...]
        _, xhat, r = _rms(h_ref[...], gain_v)
        dhin_ref[...] = dhv + _rms_bwd(dn, xhat, r, gain_v)
        dgain_ref[...] += jnp.sum(dn * xhat, axis=0, keepdims=True)

    rows = pl.BlockSpec((tm, D_MODEL), lambda i: (i, 0))
    chunks = pl.BlockSpec((N_CHUNK, tm, F_CHUNK), lambda i: (0, i, 0))
    vec = pl.BlockSpec((1, D_MODEL), lambda i: (0, 0))
    return pl.pallas_call(
        body, name=name, grid=(n // tm,),
        in_specs=[rows, rows, vec, chunks, chunks, _resident(W_GU_SHAPE, 1), _resident(W_D_SHAPE, 1)],
        out_specs=[rows, rows, chunks, chunks, vec],
        out_shape=[jax.ShapeDtypeStruct((n, D_MODEL), F32), jax.ShapeDtypeStruct((n, D_MODEL), BF16),
                   jax.ShapeDtypeStruct((N_CHUNK, n, F_CHUNK), BF16),
                   jax.ShapeDtypeStruct((N_CHUNK, n, F_CHUNK), BF16),
                   jax.ShapeDtypeStruct((1, D_MODEL), F32)],
        compiler_params=_params(("arbitrary",)),
    )(dh_out, h_in, gain, gate, up, wgu, wd)


def _ffn_bwd_act(dh_out, gate, up, wd, name, exchange=()):
    n = dh_out.shape[0]
    tm = _row_tile(n, 512)
    n_i = n // tm
    nw = len(exchange)

    def body(*refs):
        (dh_ref, gate_ref, up_ref, wd_ref), xin, (dhb_ref, dgate_ref, dup_ref), xout, _, sems = \
            _split_refs(refs, 4, nw, 3, 0)
        i = pl.program_id(0)
        if nw:
            comm = _Exchange(xin, xout, sems)
            pl.when(i == 0)(comm.start)

        dhb = (0.5 * dh_ref[...]).astype(BF16)
        dhb_ref[...] = dhb
        for j in range(N_CHUNK):
            dgate_ref[j], dup_ref[j] = _swiglu_bwd(_dot_nt(dhb, wd_ref[j]), gate_ref, up_ref, j)

        if nw:
            pl.when(i == n_i - 1)(comm.finish)

    rows = pl.BlockSpec((tm, D_MODEL), lambda i: (i, 0))
    chunks = pl.BlockSpec((N_CHUNK, tm, F_CHUNK), lambda i: (0, i, 0))
    return pl.pallas_call(
        body, name=name, grid=(n_i,),
        in_specs=[rows, chunks, chunks, _resident(W_D_SHAPE, 1)] + [ANY] * nw,
        out_specs=[rows, chunks, chunks] + [ANY] * nw,
        out_shape=[jax.ShapeDtypeStruct((n, D_MODEL), BF16)] + [jax.ShapeDtypeStruct((N_CHUNK, n, F_CHUNK), BF16)] * 2
        + [jax.ShapeDtypeStruct(a.shape, a.dtype) for a in exchange],
        scratch_shapes=_comm_sems(nw) if nw else [],
        compiler_params=_params(("arbitrary",)),
    )(dh_out, gate, up, wd, *exchange)


def _ffn_bwd_in(dh_out, h_in, gain, dgate, dup, wgu, name, exchange=()):
    n = h_in.shape[0]
    tm = _row_tile(n, 512)
    n_i = n // tm
    nw = len(exchange)

    def body(*refs):
        (dh_ref, h_ref, g_ref, dgate_ref, dup_ref, wgu_ref), xin, (dhin_ref, dgain_ref), xout, _, sems = \
            _split_refs(refs, 6, nw, 2, 0)
        i = pl.program_id(0)
        if nw:
            comm = _ChipExchange(xin, xout, sems)
            pl.when(i == 0)(comm.start)

        @pl.when(i == 0)
        def _():
            dgain_ref[...] = jnp.zeros_like(dgain_ref)

        dn = jnp.zeros((tm, D_MODEL), F32)
        for j in range(N_CHUNK):
            dn = dn + _dot(dgate_ref[j], wgu_ref[0, j]) + _dot(dup_ref[j], wgu_ref[1, j])
        gain_v = g_ref[...]
        _, xhat, r = _rms(h_ref[...], gain_v)
        dhin_ref[...] = dh_ref[...] + _rms_bwd(dn, xhat, r, gain_v)
        dgain_ref[...] += jnp.sum(dn * xhat, axis=0, keepdims=True)

        if nw:
            pl.when(i == n_i - 1)(comm.finish)

    rows = pl.BlockSpec((tm, D_MODEL), lambda i: (i, 0))
    chunks = pl.BlockSpec((N_CHUNK, tm, F_CHUNK), lambda i: (0, i, 0))
    vec = pl.BlockSpec((1, D_MODEL), lambda i: (0, 0))
    return pl.pallas_call(
        body, name=name, grid=(n_i,),
        in_specs=[rows, rows, vec, chunks, chunks, _resident(W_GU_SHAPE, 1)] + [ANY] * nw,
        out_specs=[rows, vec] + [ANY] * nw,
        out_shape=[jax.ShapeDtypeStruct((n, D_MODEL), F32), jax.ShapeDtypeStruct((1, D_MODEL), F32)]
        + [jax.ShapeDtypeStruct(a.shape, a.dtype) for a in exchange],
        scratch_shapes=_comm_sems(nw) if nw else [],
        compiler_params=_params(("arbitrary",)),
    )(dh_out, h_in, gain, dgate, dup, wgu, *exchange)


W_GROUP = 2


def _ffn_bwd_w(dhb, nrm, gate, up, dgate, dup, name):
    n = nrm.shape[0]
    tm = _row_tile(n, 512)
    n_i = n // tm

    def body(dhb_ref, nrm_ref, gate_ref, up_ref, dgate_ref, dup_ref, dwgu_ref, dwd_ref, ag_scr, au_scr, ad_scr):
        i = pl.program_id(1)

        @pl.when(i == 0)
        def _():
            ag_scr[...] = jnp.zeros_like(ag_scr)
            au_scr[...] = jnp.zeros_like(au_scr)
            ad_scr[...] = jnp.zeros_like(ad_scr)

        nb = nrm_ref[...]
        dhv = dhb_ref[...]
        for jj in range(W_GROUP):
            ag_scr[jj] += _dot_tn(dgate_ref[jj], nb)
            au_scr[jj] += _dot_tn(dup_ref[jj], nb)
            g = gate_ref[jj].astype(F32)
            act = (g * _sigmoid(g) * up_ref[jj].astype(F32)).astype(BF16)
            ad_scr[jj] += _dot_tn(act, dhv)

        @pl.when(i == n_i - 1)
        def _():
            dwgu_ref[0] = ag_scr[...].astype(BF16)
            dwgu_ref[1] = au_scr[...].astype(BF16)
            dwd_ref[...] = ad_scr[...].astype(BF16)

    chunks = pl.BlockSpec((W_GROUP, tm, F_CHUNK), lambda g, i: (g, i, 0))
    rows = pl.BlockSpec((tm, D_MODEL), lambda g, i: (i, 0))
    return pl.pallas_call(
        body, name=name, grid=(N_CHUNK // W_GROUP, n_i),
        in_specs=[rows, rows, chunks, chunks, chunks, chunks],
        out_specs=[pl.BlockSpec((2, W_GROUP, F_CHUNK, D_MODEL), lambda g, i: (0, g, 0, 0)),
                   pl.BlockSpec((W_GROUP, F_CHUNK, D_MODEL), lambda g, i: (g, 0, 0))],
        out_shape=[jax.ShapeDtypeStruct(W_GU_SHAPE, BF16), jax.ShapeDtypeStruct(W_D_SHAPE, BF16)],
        scratch_shapes=[pltpu.VMEM((W_GROUP, F_CHUNK, D_MODEL), F32), pltpu.VMEM((W_GROUP, F_CHUNK, D_MODEL), F32),
                        pltpu.VMEM((W_GROUP, F_CHUNK, D_MODEL), F32)],
        compiler_params=_params(("parallel", "arbitrary")),
    )(dhb, nrm, gate, up, dgate, dup)


N_PIECE = IN_MAIN // 512


def _inproj_fwd(h, gain, w_in):
    n = h.shape[0]
    tm = _row_tile(n, 512)

    def body(h_ref, g_ref, w_ref, *outs):
        y, _, _ = _rms(h_ref[...], g_ref[...])
        nb = y.astype(BF16)
        for p in range(N_PIECE):
            outs[p][...] = _dot_nt(nb, w_ref[512 * p:512 * (p + 1), :]).astype(BF16)
        outs[N_PIECE][...] = _dot_nt(nb, w_ref[IN_MAIN:IN_PAD, :])

    piece = pl.BlockSpec((tm, 512), lambda i: (i, 0))
    return pl.pallas_call(
        body, name="inproj_fwd", grid=(n // tm,),
        in_specs=[pl.BlockSpec((tm, D_MODEL), lambda i: (i, 0)),
                  pl.BlockSpec((1, D_MODEL), lambda i: (0, 0)),
                  pl.BlockSpec((IN_PAD, D_MODEL), lambda i: (0, 0))],
        out_specs=[piece] * N_PIECE + [pl.BlockSpec((tm, 128), lambda i: (i, 0))],
        out_shape=[jax.ShapeDtypeStruct((n, 512), BF16)] * N_PIECE + [jax.ShapeDtypeStruct((n, 128), F32)],
        compiler_params=_params(("parallel",)),
    )(h, gain, w_in)


def _inproj_bwd(dpieces, dfg, dh_out, h_in, gain, w_in):
    n = h_in.shape[0]
    tm = _row_tile(n, 512)
    n_i = n // tm

    def body(*refs):
        dp_refs = refs[:N_PIECE]
        dfg_ref, dh_ref, h_ref, g_ref, w_ref, dhin_ref, dw_ref, dgain_ref, acc_scr = refs[N_PIECE:]
        i = pl.program_id(0)

        @pl.when(i == 0)
        def _():
            acc_scr[...] = jnp.zeros_like(acc_scr)
            dgain_ref[...] = jnp.zeros_like(dgain_ref)

        gain_v = g_ref[...]
        y, xhat, r = _rms(h_ref[...], gain_v)
        nb = y.astype(BF16)
        dn = jnp.zeros((tm, D_MODEL), F32)
        for p in range(N_PIECE + 1):
            lo, hi = (512 * p, 512 * (p + 1)) if p < N_PIECE else (IN_MAIN, IN_PAD)
            dp = (dp_refs[p][...] if p < N_PIECE else dfg_ref[...]).astype(BF16)
            dn = dn + _dot(dp, w_ref[lo:hi, :])
            acc_scr[lo:hi, :] += _dot_tn(dp, nb)
        dhin_ref[...] = dh_ref[...] + _rms_bwd(dn, xhat, r, gain_v)
        dgain_ref[...] += jnp.sum(dn * xhat, axis=0, keepdims=True)

        @pl.when(i == n_i - 1)
        def _():
            dw_ref[...] = acc_scr[...].astype(BF16)

    piece = pl.BlockSpec((tm, 512), lambda i: (i, 0))
    rows = pl.BlockSpec((tm, D_MODEL), lambda i: (i, 0))
    vec = pl.BlockSpec((1, D_MODEL), lambda i: (0, 0))
    wspec = pl.BlockSpec((IN_PAD, D_MODEL), lambda i: (0, 0))
    return pl.pallas_call(
        body, name="inproj_bwd", grid=(n_i,),
        in_specs=[piece] * N_PIECE + [pl.BlockSpec((tm, 128), lambda i: (i, 0)), rows, rows, vec, wspec],
        out_specs=[rows, wspec, vec],
        out_shape=[jax.ShapeDtypeStruct((n, D_MODEL), F32),
                   jax.ShapeDtypeStruct((IN_PAD, D_MODEL), BF16),
                   jax.ShapeDtypeStruct((1, D_MODEL), F32)],
        scratch_shapes=[pltpu.VMEM((IN_PAD, D_MODEL), F32)],
        compiler_params=_params(("arbitrary",)),
    )(*dpieces, dfg, dh_out, h_in, gain, w_in)


def _outproj_fwd(zc, za, w_out, h):
    n = h.shape[0]
    tm = _row_tile(n, 512)

    def body(zc_ref, za_ref, w_ref, h_ref, out_ref):
        out_ref[...] = (h_ref[...] + _dot(zc_ref[...], w_ref[0:CONV_DIM, :])
                        + _dot(za_ref[...], w_ref[CONV_DIM:, :]))

    half = pl.BlockSpec((tm, 512), lambda i: (i, 0))
    rows = pl.BlockSpec((tm, D_MODEL), lambda i: (i, 0))
    return pl.pallas_call(
        body, name="outproj_fwd", grid=(n // tm,),
        in_specs=[half, half, pl.BlockSpec((D_MODEL, D_MODEL), lambda i: (0, 0)), rows],
        out_specs=rows,
        out_shape=jax.ShapeDtypeStruct((n, D_MODEL), F32),
        compiler_params=_params(("parallel",)),
    )(zc, za, w_out, h)


def _outproj_bwd(dh, zc, za, w_out):
    n = dh.shape[0]
    tm = _row_tile(n, 512)
    n_i = n // tm

    def body(dh_ref, zc_ref, za_ref, w_ref, dzc_ref, dza_ref, dw_ref, acc_scr):
        i = pl.program_id(0)

        @pl.when(i == 0)
        def _():
            acc_scr[...] = jnp.zeros_like(acc_scr)

        dhb = dh_ref[...].astype(BF16)
        dzc_ref[...] = _dot_nt(dhb, w_ref[0:CONV_DIM, :]).astype(BF16)
        dza_ref[...] = _dot_nt(dhb, w_ref[CONV_DIM:, :]).astype(BF16)
        acc_scr[0:CONV_DIM, :] += _dot_tn(zc_ref[...], dhb)
        acc_scr[CONV_DIM:, :] += _dot_tn(za_ref[...], dhb)

        @pl.when(i == n_i - 1)
        def _():
            dw_ref[...] = acc_scr[...].astype(BF16)

    half = pl.BlockSpec((tm, 512), lambda i: (i, 0))
    wspec = pl.BlockSpec((D_MODEL, D_MODEL), lambda i: (0, 0))
    return pl.pallas_call(
        body, name="outproj_bwd", grid=(n_i,),
        in_specs=[pl.BlockSpec((tm, D_MODEL), lambda i: (i, 0)), half, half, wspec],
        out_specs=[half, half, wspec],
        out_shape=[jax.ShapeDtypeStruct((n, 512), BF16), jax.ShapeDtypeStruct((n, 512), BF16),
                   jax.ShapeDtypeStruct((D_MODEL, D_MODEL), BF16)],
        scratch_shapes=[pltpu.VMEM((D_MODEL, D_MODEL), F32)],
        compiler_params=_params(("arbitrary",)),
    )(dh, zc, za, w_out)


def _group_matrix():
    r = lax.broadcasted_iota(jnp.int32, (128, 128), 0) // HEAD_DIM
    c = lax.broadcasted_iota(jnp.int32, (128, 128), 1) // HEAD_DIM
    return jnp.where(r == c, 1.0 / HEAD_DIM, 0.0).astype(BF16)


def _group_mean(x, gmat):
    hi = x.astype(BF16)
    lo = (x - hi.astype(F32)).astype(BF16)
    return _dot(hi, gmat) + _dot(lo, gmat)


def _shift_rows(x, s):
    rows = x.shape[0]
    t = lax.broadcasted_iota(jnp.int32, x.shape, 0)
    rolled = pltpu.roll(x, s % rows, 0)
    keep = (t >= s) if s > 0 else (t < rows + s)
    return jnp.where(keep, rolled, 0.0)


def _conv_parts(bg_ref, cg_ref, hc_ref, w_ref):
    bg = bg_ref[...].astype(F32)
    cg = cg_ref[...].astype(F32)
    hc = hc_ref[...].astype(F32)
    u = cg * hc
    u1 = _shift_rows(u, 1)
    u2 = _shift_rows(u, 2)
    conv = w_ref[2:3, :] * u + w_ref[1:2, :] * u1 + w_ref[0:1, :] * u2
    return bg, cg, hc, u, u1, u2, conv


def _conv_fwd(bg, cg, hc, conv_w, gain, gmat, lp):
    n = bg.shape[0]
    nb = n // lp

    def body(bg_ref, cg_ref, hc_ref, w_ref, g_ref, gm_ref, z_ref):
        bgv, _, _, _, _, _, conv = _conv_parts(bg_ref, cg_ref, hc_ref, w_ref)
        yc = bgv * conv
        r = lax.rsqrt(_group_mean(yc * yc, gm_ref[...]) + EPS)
        z_ref[...] = (yc * r * g_ref[...]).astype(BF16)

    blk = pl.BlockSpec((lp, 128), lambda c, b: (b, c))
    return pl.pallas_call(
        body, name="conv_fwd", grid=(CONV_DIM // 128, nb),
        in_specs=[blk, blk, blk, pl.BlockSpec((3, 128), lambda c, b: (0, c)),
                  pl.BlockSpec((1, 128), lambda c, b: (0, c)), pl.BlockSpec((128, 128), lambda c, b: (0, 0))],
        out_specs=blk,
        out_shape=jax.ShapeDtypeStruct((n, CONV_DIM), BF16),
        compiler_params=_params(("parallel", "parallel")),
    )(bg, cg, hc, conv_w, gain, gmat)


def _conv_bwd(dz, bg, cg, hc, conv_w, gain, gmat, lp):
    n = bg.shape[0]
    nb = n // lp

    def body(dz_ref, bg_ref, cg_ref, hc_ref, w_ref, g_ref, gm_ref,
             dbg_ref, dcg_ref, dhc_ref, dw_ref, dgain_ref):
        b = pl.program_id(1)

        @pl.when(b == 0)
        def _():
            dw_ref[...] = jnp.zeros_like(dw_ref)
            dgain_ref[...] = jnp.zeros_like(dgain_ref)

        bgv, cgv, hcv, u, u1, u2, conv = _conv_parts(bg_ref, cg_ref, hc_ref, w_ref)
        gm = gm_ref[...]
        yc = bgv * conv
        r = lax.rsqrt(_group_mean(yc * yc, gm) + EPS)
        yhat = yc * r
        dzv = dz_ref[...].astype(F32)
        dyhat = dzv * g_ref[...]
        dgain_ref[...] += jnp.sum(dzv * yhat, axis=0, keepdims=True)
        dyc = r * (dyhat - yhat * _group_mean(dyhat * yhat, gm))
        dbg_ref[...] = (dyc * conv).astype(BF16)
        dconv = dyc * bgv
        du = (w_ref[2:3, :] * dconv + w_ref[1:2, :] * _shift_rows(dconv, -1)
              + w_ref[0:1, :] * _shift_rows(dconv, -2))
        dcg_ref[...] = (du * hcv).astype(BF16)
        dhc_ref[...] = (du * cgv).astype(BF16)
        dw_ref[0:1, :] += jnp.sum(dconv * u2, axis=0, keepdims=True)
        dw_ref[1:2, :] += jnp.sum(dconv * u1, axis=0, keepdims=True)
        dw_ref[2:3, :] += jnp.sum(dconv * u, axis=0, keepdims=True)

    blk = pl.BlockSpec((lp, 128), lambda c, b: (b, c))
    wspec = pl.BlockSpec((3, 128), lambda c, b: (0, c))
    gspec = pl.BlockSpec((1, 128), lambda c, b: (0, c))
    return pl.pallas_call(
        body, name="conv_bwd", grid=(CONV_DIM // 128, nb),
        in_specs=[blk, blk, blk, blk, wspec, gspec, pl.BlockSpec((128, 128), lambda c, b: (0, 0))],
        out_specs=[blk, blk, blk, wspec, gspec],
        out_shape=[jax.ShapeDtypeStruct((n, CONV_DIM), BF16)] * 3
        + [jax.ShapeDtypeStruct((3, CONV_DIM), F32), jax.ShapeDtypeStruct((1, CONV_DIM), F32)],
        compiler_params=_params(("parallel", "arbitrary")),
    )(dz, bg, cg, hc, conv_w, gain, gmat)


KEY_MASKED = 1e30
ONE_LANE = 24


def _scan_steps(rows):
    s, out = 1, []
    while s < rows:
        out.append(s)
        s *= 2
    return out


def _fgate_fwd(fg, b_f, lp):
    n = fg.shape[0]
    nb = n // lp

    def body(fg_ref, b_ref, ka_ref, qa_ref):
        x = fg_ref[...] + b_ref[...]
        logf = jnp.minimum(x, 0.0) - jnp.log(1.0 + jnp.exp(-jnp.abs(x)))
        t = lax.broadcasted_iota(jnp.int32, (lp, 128), 0)
        lane = lax.broadcasted_iota(jnp.int32, (lp, 128), 1)
        f = jnp.where((t >= PAD) & (lane < N_HEADS), logf, 0.0)
        for s in _scan_steps(lp):
            f = f + _shift_rows(f, s)
        hi = f.astype(BF16).astype(F32)
        rest = f - hi
        mid = rest.astype(BF16).astype(F32)
        lo = (rest - mid).astype(BF16).astype(F32)
        ones = jnp.where((lane >= ONE_LANE) & (lane < ONE_LANE + 3), 1.0, 0.0)
        hi_key = jnp.where((t < PAD) & (lane < N_HEADS), KEY_MASKED, hi)
        ka_ref[...] = (hi_key + pltpu.roll(mid, 8, 1) + pltpu.roll(lo, 16, 1) + ones).astype(BF16)
        for h in range(N_HEADS):
            minus = jnp.where((lane == h) | (lane == 8 + h) | (lane == 16 + h), -1.0, 0.0)
            terms = (jnp.where(lane == ONE_LANE, pltpu.roll(hi, ONE_LANE - h, 1), 0.0)
                     + jnp.where(lane == ONE_LANE + 1, pltpu.roll(mid, ONE_LANE + 1 - h, 1), 0.0)
                     + jnp.where(lane == ONE_LANE + 2, pltpu.roll(lo, ONE_LANE + 2 - h, 1), 0.0))
            qa_ref[:, 128 * h:128 * (h + 1)] = (minus + terms).astype(BF16)

    return pl.pallas_call(
        body, name="fgate_fwd", grid=(nb,),
        in_specs=[pl.BlockSpec((lp, 128), lambda b: (b, 0)), pl.BlockSpec((1, 128), lambda b: (0, 0))],
        out_specs=[pl.BlockSpec((lp, 128), lambda b: (b, 0)), pl.BlockSpec((lp, N_HEADS * 128), lambda b: (b, 0))],
        out_shape=[jax.ShapeDtypeStruct((n, 128), BF16), jax.ShapeDtypeStruct((n, N_HEADS * 128), BF16)],
        compiler_params=_params(("parallel",)),
    )(fg, b_f)


def _fgate_bwd(dka, dfr, fg, b_f, lp):
    n = fg.shape[0]
    nb = n // lp

    def body(dka_ref, dfr_ref, fg_ref, b_ref, dfg_ref, db_ref):
        b = pl.program_id(0)

        @pl.when(b == 0)
        def _():
            db_ref[...] = jnp.zeros_like(db_ref)

        wide = jnp.concatenate([dfr_ref[0], jnp.zeros((128 - N_HEADS, lp), F32)], axis=0)
        t = lax.broadcasted_iota(jnp.int32, (lp, 128), 0)
        lane = lax.broadcasted_iota(jnp.int32, (lp, 128), 1)
        d = jnp.where(lane < N_HEADS, dka_ref[...], 0.0) + wide.T
        for s in _scan_steps(lp):
            d = d + _shift_rows(d, -s)
        x = fg_ref[...] + b_ref[...]
        dx = jnp.where((t >= PAD) & (lane < N_HEADS), d * _sigmoid(-x), 0.0)
        dfg_ref[...] = dx
        db_ref[...] += jnp.sum(dx, axis=0, keepdims=True)

    return pl.pallas_call(
        body, name="fgate_bwd", grid=(nb,),
        in_specs=[pl.BlockSpec((lp, 128), lambda b: (b, 0)), pl.BlockSpec((1, N_HEADS, lp), lambda b: (b, 0, 0)),
                  pl.BlockSpec((lp, 128), lambda b: (b, 0)), pl.BlockSpec((1, 128), lambda b: (0, 0))],
        out_specs=[pl.BlockSpec((lp, 128), lambda b: (b, 0)), pl.BlockSpec((1, 128), lambda b: (0, 0))],
        out_shape=[jax.ShapeDtypeStruct((n, 128), F32), jax.ShapeDtypeStruct((1, 128), F32)],
        compiler_params=_params(("arbitrary",)),
    )(dka, dfr, fg, b_f)


def _head_masks():
    lane = lax.broadcasted_iota(jnp.int32, (1, 128), 1)
    return lane < HEAD_DIM


def _stack_heads(x2, first):
    zero = jnp.zeros_like(x2)
    return jnp.concatenate([jnp.where(first, x2, zero), jnp.where(first, zero, x2)], axis=0)


def _stack_heads_lanes(xt):
    r = lax.broadcasted_iota(jnp.int32, xt.shape, 0)
    zero = jnp.zeros_like(xt)
    return jnp.concatenate([jnp.where(r < HEAD_DIM, xt, zero), jnp.where(r < HEAD_DIM, zero, xt)], axis=1)


def _pair_cols(col0, col1, first):
    return jnp.where(first, col0, col1)


def _pair_rows(row0, row1):
    r = lax.broadcasted_iota(jnp.int32, (128, TQ), 0)
    return jnp.where(r < HEAD_DIM, row0, row1)


def _query_side(q_ref, qa_ref, p, first):
    q2 = q_ref[:, 128 * p:128 * (p + 1)] * 0.125
    zero = jnp.zeros_like(q2)
    top = jnp.concatenate([jnp.where(first, q2, zero), qa_ref[:, 128 * (2 * p):128 * (2 * p + 1)]], axis=1)
    bot = jnp.concatenate([jnp.where(first, zero, q2), qa_ref[:, 128 * (2 * p + 1):128 * (2 * p + 2)]], axis=1)
    return jnp.concatenate([top, bot], axis=0)


def _key_chunks(lp):
    return (lp + TK - 1) // TK


def _chunk_mask(i, c):
    r = lax.broadcasted_iota(jnp.int32, (TK, 2 * TQ), 0)
    col = lax.broadcasted_iota(jnp.int32, (TK, 2 * TQ), 1)
    return (c * TK + r) <= (i * TQ + (col & (TQ - 1)))


def _transpose_bf16(x):
    return x.astype(F32).T.astype(BF16)


def _attn_fwd(q, qa, k, v, ka, gain, lp):
    n = q.shape[0]
    nb = n // lp
    nq = lp // TQ
    lpp = _key_chunks(lp) * TK

    def body(q_ref, qa_ref, k_ref, v_ref, ka_ref, g_ref, z_ref, o_ref, lse_ref, kx_scr, vt_scr):
        i = pl.program_id(1)
        first = _head_masks()

        @pl.when(i == 0)
        def _():
            if lpp > lp:
                kx_scr[lp:lpp, :] = jnp.zeros((lpp - lp, 2 * ATTN_DIM), BF16)
                vt_scr[:, lp:lpp] = jnp.zeros((ATTN_DIM, lpp - lp), BF16)
            for p in range(N_PAIRS):
                kx_scr[0:lp, 256 * p:256 * p + 128] = k_ref[:, 128 * p:128 * (p + 1)]
                kx_scr[0:lp, 256 * p + 128:256 * (p + 1)] = ka_ref[...]
            vt_scr[:, 0:lp] = _transpose_bf16(v_ref[...])

        rhs_t = [_transpose_bf16(_query_side(q_ref, qa_ref, p, first)) for p in range(N_PAIRS)]

        def step(c, carry):
            koff = pl.multiple_of(c * TK, TK)
            valid = _chunk_mask(i, c)
            new = []
            for p in range(N_PAIRS):
                m, l, acc = carry[p]
                st = _dot(kx_scr[pl.ds(koff, TK), 256 * p:256 * (p + 1)], rhs_t[p])
                st = jnp.where(valid, st, NEG)
                m_new = jnp.maximum(m, jnp.max(st, axis=0, keepdims=True))
                pt = jnp.exp(st - m_new)
                alpha = jnp.exp(m - m_new)
                l = alpha * l + jnp.sum(pt, axis=0, keepdims=True)
                pb = pt.astype(BF16)
                vt = _stack_heads_lanes(vt_scr[128 * p:128 * (p + 1), pl.ds(koff, TK)])
                pv = _dot(vt, jnp.concatenate([pb[:, 0:TQ], pb[:, TQ:]], axis=0))
                acc = acc * _pair_rows(alpha[:, 0:TQ], alpha[:, TQ:]) + pv
                new.append((m_new, l, acc))
            return tuple(new)

        init = tuple((jnp.full((1, 2 * TQ), NEG, F32), jnp.zeros((1, 2 * TQ), F32), jnp.zeros((128, TQ), F32))
                     for _ in range(N_PAIRS))
        final = lax.fori_loop(0, (i + TK // TQ) // (TK // TQ), step, init)

        row = lax.broadcasted_iota(jnp.int32, (TQ, 128), 0)
        real = (i * TQ + row) >= PAD
        for p in range(N_PAIRS):
            m, l, acc = final[p]
            inv = 1.0 / l
            ot = acc * _pair_rows(inv[:, 0:TQ], inv[:, TQ:])
            sq = ot * ot
            r0 = lax.rsqrt(jnp.sum(sq[0:HEAD_DIM], axis=0, keepdims=True) * (1.0 / HEAD_DIM) + EPS)
            r1 = lax.rsqrt(jnp.sum(sq[HEAD_DIM:], axis=0, keepdims=True) * (1.0 / HEAD_DIM) + EPS)
            cols = slice(128 * p, 128 * (p + 1))
            o_ref[:, cols] = jnp.where(real, ot.T, 0.0).astype(BF16)
            z_ref[:, cols] = (jnp.where(real, (ot * _pair_rows(r0, r1)).T, 0.0) * g_ref[:, cols]).astype(BF16)
            lse = m + jnp.log(l)
            lse_ref[0, 2 * p:2 * p + 1, :] = lse[:, 0:TQ]
            lse_ref[0, 2 * p + 1:2 * p + 2, :] = lse[:, TQ:]

    qblk = pl.BlockSpec((TQ, ATTN_DIM), lambda b, i: (b * nq + i, 0))
    qablk = pl.BlockSpec((TQ, N_HEADS * 128), lambda b, i: (b * nq + i, 0))
    seq = pl.BlockSpec((lp, ATTN_DIM), lambda b, i: (b, 0))
    rowblk = pl.BlockSpec((1, N_HEADS, TQ), lambda b, i: (b, 0, i))
    return pl.pallas_call(
        body, name="attn_fwd", grid=(nb, nq),
        in_specs=[qblk, qablk, seq, seq, pl.BlockSpec((lp, 128), lambda b, i: (b, 0)),
                  pl.BlockSpec((1, ATTN_DIM), lambda b, i: (0, 0))],
        out_specs=[qblk, qblk, rowblk],
        out_shape=[jax.ShapeDtypeStruct((n, ATTN_DIM), BF16), jax.ShapeDtypeStruct((n, ATTN_DIM), BF16),
                   jax.ShapeDtypeStruct((nb, N_HEADS, lp), F32)],
        scratch_shapes=[pltpu.VMEM((lpp, 2 * ATTN_DIM), BF16), pltpu.VMEM((ATTN_DIM, lpp), BF16)],
        compiler_params=_params(("parallel", "arbitrary")),
    )(q, qa, k, v, ka, gain)


def _attn_bwd(dz, q, qa, k, v, ka, o, lse, gain, lp, exchange=()):
    n = q.shape[0]
    nb = n // lp
    nq = lp // TQ
    lpp = _key_chunks(lp) * TK
    nw = len(exchange)

    def body(*refs):
        ((dz_ref, q_ref, qa_ref, k_ref, v_ref, ka_ref, o_ref, lse_ref, g_ref), xin,
         (dq_ref, dk_ref, dv_ref, dka_ref, dfr_ref, dgain_ref), xout,
         (kx_scr, vx_scr, kt_scr, dkx_scr, dvx_scr), sems) = _split_refs(refs, 9, nw, 6, 5)
        b = pl.program_id(0)
        i = pl.program_id(1)
        first = _head_masks()
        if nw:
            comm = _Exchange(xin, xout, sems)
            pl.when((b == 0) & (i == 0))(comm.start)

        @pl.when((b == 0) & (i == 0))
        def _():
            dgain_ref[...] = jnp.zeros_like(dgain_ref)

        @pl.when(i == 0)
        def _():
            if lpp > lp:
                kx_scr[lp:lpp, :] = jnp.zeros((lpp - lp, 2 * ATTN_DIM), BF16)
                vx_scr[lp:lpp, :] = jnp.zeros((lpp - lp, ATTN_DIM), BF16)
                kt_scr[:, lp:lpp] = jnp.zeros((ATTN_DIM, lpp - lp), BF16)
            for p in range(N_PAIRS):
                kx_scr[0:lp, 256 * p:256 * p + 128] = k_ref[:, 128 * p:128 * (p + 1)]
                kx_scr[0:lp, 256 * p + 128:256 * (p + 1)] = ka_ref[...]
            vx_scr[0:lp, :] = v_ref[...]
            kt_scr[:, 0:lp] = _transpose_bf16(k_ref[...])
            dkx_scr[...] = jnp.zeros_like(dkx_scr)
            dvx_scr[...] = jnp.zeros_like(dvx_scr)

        rhs, rhs_t, lses, dos, dos_t, deltas = [], [], [], [], [], []
        for p in range(N_PAIRS):
            cols = slice(128 * p, 128 * (p + 1))
            side = _query_side(q_ref, qa_ref, p, first)
            rhs.append(side)
            rhs_t.append(_transpose_bf16(side))
            lses.append(jnp.concatenate([lse_ref[0, 2 * p:2 * p + 1, :], lse_ref[0, 2 * p + 1:2 * p + 2, :]], axis=1))
            ov = o_ref[:, cols].astype(F32)
            dzv = dz_ref[:, cols].astype(F32)
            gv = g_ref[:, cols]
            sq = ov * ov
            ms0 = jnp.sum(jnp.where(first, sq, 0.0), axis=1, keepdims=True) * (1.0 / HEAD_DIM)
            ms1 = jnp.sum(jnp.where(first, 0.0, sq), axis=1, keepdims=True) * (1.0 / HEAD_DIM)
            r = _pair_cols(lax.rsqrt(ms0 + EPS), lax.rsqrt(ms1 + EPS), first)
            ohat = ov * r
            dyhat = dzv * gv
            dgain_ref[:, cols] += jnp.sum(dzv * ohat, axis=0, keepdims=True)
            pr = dyhat * ohat
            mean0 = jnp.sum(jnp.where(first, pr, 0.0), axis=1, keepdims=True) * (1.0 / HEAD_DIM)
            mean1 = jnp.sum(jnp.where(first, 0.0, pr), axis=1, keepdims=True) * (1.0 / HEAD_DIM)
            do = r * (dyhat - ohat * _pair_cols(mean0, mean1, first))
            ddt = (do * ov).T
            deltas.append(jnp.concatenate([jnp.sum(ddt[0:HEAD_DIM], axis=0, keepdims=True),
                                           jnp.sum(ddt[HEAD_DIM:], axis=0, keepdims=True)], axis=1))
            do_st = _stack_heads(do.astype(BF16), first)
            dos.append(do_st)
            dos_t.append(_transpose_bf16(do_st))

        def step(c, carry):
            koff = pl.multiple_of(c * TK, TK)
            valid = _chunk_mask(i, c)
            new = []
            for p in range(N_PAIRS):
                dqt, dfq = carry[p]
                ext = slice(256 * p, 256 * (p + 1))
                cols = slice(128 * p, 128 * (p + 1))
                st = _dot(kx_scr[pl.ds(koff, TK), ext], rhs_t[p])
                st = jnp.where(valid, st, NEG)
                pt = jnp.exp(st - lses[p])
                dpt = _dot(vx_scr[pl.ds(koff, TK), cols], dos_t[p])
                dst = pt * (dpt - deltas[p])
                dsb = dst.astype(BF16)
                dfq = dfq + jnp.sum(dsb.astype(F32), axis=0, keepdims=True)
                dkx_scr[pl.ds(koff, TK), ext] += _dot(dsb, rhs[p])
                dvx_scr[pl.ds(koff, TK), cols] += _dot(pt.astype(BF16), dos[p])
                kt = _stack_heads_lanes(kt_scr[cols, pl.ds(koff, TK)])
                dqt = dqt + _dot(kt, jnp.concatenate([dsb[:, 0:TQ], dsb[:, TQ:]], axis=0))
                new.append((dqt, dfq))
            return tuple(new)

        init = tuple((jnp.zeros((128, TQ), F32), jnp.zeros((1, 2 * TQ), F32)) for _ in range(N_PAIRS))
        final = lax.fori_loop(0, (i + TK // TQ) // (TK // TQ), step, init)

        for p in range(N_PAIRS):
            dqt, dfq = final[p]
            dq_ref[:, 128 * p:128 * (p + 1)] = (dqt.T * 0.125).astype(BF16)
            dfr_ref[0, 2 * p:2 * p + 1, :] = dfq[:, 0:TQ]
            dfr_ref[0, 2 * p + 1:2 * p + 2, :] = dfq[:, TQ:]

        @pl.when(i == nq - 1)
        def _():
            dka = jnp.zeros((lp, 128), F32)
            for p in range(N_PAIRS):
                dk_ref[:, 128 * p:128 * (p + 1)] = dkx_scr[0:lp, 256 * p:256 * p + 128].astype(BF16)
                dka = dka + dkx_scr[0:lp, 256 * p + 128:256 * (p + 1)]
            dka_ref[...] = dka
            dv_ref[...] = dvx_scr[0:lp, :].astype(BF16)

        if nw:
            pl.when((b == nb - 1) & (i == nq - 1))(comm.finish)

    qblk = pl.BlockSpec((TQ, ATTN_DIM), lambda b, i: (b * nq + i, 0))
    qablk = pl.BlockSpec((TQ, N_HEADS * 128), lambda b, i: (b * nq + i, 0))
    seq = pl.BlockSpec((lp, ATTN_DIM), lambda b, i: (b, 0))
    kaseq = pl.BlockSpec((lp, 128), lambda b, i: (b, 0))
    rowblk = pl.BlockSpec((1, N_HEADS, TQ), lambda b, i: (b, 0, i))
    gspec = pl.BlockSpec((1, ATTN_DIM), lambda b, i: (0, 0))
    return pl.pallas_call(
        body, name="attn_bwd", grid=(nb, nq),
        in_specs=[qblk, qblk, qablk, seq, seq, kaseq, qblk, rowblk, gspec] + [ANY] * nw,
        out_specs=[qblk, seq, seq, kaseq, rowblk, gspec] + [ANY] * nw,
        out_shape=[jax.ShapeDtypeStruct((n, ATTN_DIM), BF16), jax.ShapeDtypeStruct((n, ATTN_DIM), BF16),
                   jax.ShapeDtypeStruct((n, ATTN_DIM), BF16), jax.ShapeDtypeStruct((n, 128), F32),
                   jax.ShapeDtypeStruct((nb, N_HEADS, lp), F32), jax.ShapeDtypeStruct((1, ATTN_DIM), F32)]
        + [jax.ShapeDtypeStruct(a.shape, a.dtype) for a in exchange],
        scratch_shapes=[pltpu.VMEM((lpp, 2 * ATTN_DIM), BF16), pltpu.VMEM((lpp, ATTN_DIM), BF16),
                        pltpu.VMEM((ATTN_DIM, lpp), BF16), pltpu.VMEM((lpp, 2 * ATTN_DIM), F32),
                        pltpu.VMEM((lpp, ATTN_DIM), F32)] + (_comm_sems(nw) if nw else []),
        compiler_params=_params(("arbitrary", "arbitrary")),
    )(dz, q, qa, k, v, ka, o, lse, gain, *exchange)


def _loss_head(h, gain, target, lp):
    n = h.shape[0]
    nb = n // lp
    nq = lp // 128

    def body(h_ref, g_ref, t_ref, loss_ref, dh_ref, dgain_ref):
        b = pl.program_id(0)
        i = pl.program_id(1)

        @pl.when((b == 0) & (i == 0))
        def _():
            loss_ref[...] = jnp.zeros_like(loss_ref)
            dgain_ref[...] = jnp.zeros_like(dgain_ref)

        @pl.when(i == 0)
        def _():
            dh_ref[...] = jnp.zeros_like(dh_ref)

        @pl.when(i > 0)
        def _():
            gain_v = g_ref[...]
            y, xhat, r = _rms(h_ref[...], gain_v)
            err = y - t_ref[...]
            loss_ref[...] += 0.5 * jnp.sum(jnp.sum(err * err, axis=1, keepdims=True), axis=0,
                                           keepdims=True) * (1.0 / D_MODEL)
            dy = err * (1.0 / D_MODEL)
            dh_ref[...] = _rms_bwd(dy, xhat, r, gain_v)
            dgain_ref[...] += jnp.sum(dy * xhat, axis=0, keepdims=True)

    rows = pl.BlockSpec((128, D_MODEL), lambda b, i: (b * nq + i, 0))
    trows = pl.BlockSpec((128, D_MODEL), lambda b, i: (b * (nq - 1) + jnp.maximum(i, 1) - 1, 0))
    return pl.pallas_call(
        body, name="loss_head", grid=(nb, nq),
        in_specs=[rows, pl.BlockSpec((1, D_MODEL), lambda b, i: (0, 0)), trows],
        out_specs=[pl.BlockSpec((1, 1), lambda b, i: (0, 0)), rows, pl.BlockSpec((1, D_MODEL), lambda b, i: (0, 0))],
        out_shape=[jax.ShapeDtypeStruct((1, 1), F32), jax.ShapeDtypeStruct((n, D_MODEL), F32),
                   jax.ShapeDtypeStruct((1, D_MODEL), F32)],
        compiler_params=_params(("arbitrary", "arbitrary")),
    )(h, gain, target)


def _adamw(parts, w, m, v, name):
    s_parts, r, c = parts.shape
    tr = r
    for t in (256, 128, 64, 32, 16):
        if r % t == 0 and r > t:
            tr = t
            break

    def body(p_ref, w_ref, m_ref, v_ref, g_ref, d_ref, nm_ref, nv_ref):
        g = p_ref[0].astype(F32)
        for s in range(1, s_parts):
            g = g + p_ref[s].astype(F32)
        nm = ADAM_B1 * m_ref[...] + (1.0 - ADAM_B1) * g
        nv = ADAM_B2 * v_ref[...] + (1.0 - ADAM_B2) * (g * g)
        m_hat = nm / (1.0 - ADAM_B1 ** ADAM_STEP)
        v_hat = nv / (1.0 - ADAM_B2 ** ADAM_STEP)
        g_ref[...] = g
        d_ref[...] = -ADAM_LR * (m_hat / (jnp.sqrt(v_hat) + ADAM_EPS) + ADAM_WD * w_ref[...])
        nm_ref[...] = nm
        nv_ref[...] = nv

    blk = pl.BlockSpec((tr, c), lambda i: (i, 0))
    return pl.pallas_call(
        body, name=name, grid=(r // tr,),
        in_specs=[pl.BlockSpec((s_parts, tr, c), lambda i: (0, i, 0)), blk, blk, blk],
        out_specs=[blk] * 4,
        out_shape=[jax.ShapeDtypeStruct((r, c), F32)] * 4,
        compiler_params=_params(("parallel",)),
    )(parts, w, m, v)


def _sum_parts(parts, name):
    s_parts, r, c = parts.shape

    def body(p_ref, out_ref):
        acc = p_ref[0]
        for s in range(1, s_parts):
            acc = acc + p_ref[s]
        out_ref[...] = acc

    return pl.pallas_call(
        body, name=name, out_shape=jax.ShapeDtypeStruct((r, c), F32),
        in_specs=[pl.BlockSpec(memory_space=pltpu.VMEM)], out_specs=pl.BlockSpec(memory_space=pltpu.VMEM),
    )(parts)


SMALL_ROWS = 184


def _pack_small(d_gains, d_gc, d_ga, d_bf, d_conv, d_meta):
    rows = [g.reshape(8, 128) for g in d_gains]
    rows += [d_gc.reshape(4, 128), d_ga.reshape(4, 128), d_bf.reshape(1, 128)]
    rows += [d_conv.reshape(12, 128), d_meta.reshape(128, 128)]
    packed = jnp.concatenate(rows, axis=0)
    return jnp.pad(packed, ((0, SMALL_ROWS - packed.shape[0]), (0, 0)))


def kernel(x, meta_tokens, ffn1_norm, ffn1_w_gu, ffn1_w_down, mix_norm, w_in, conv_w, b_f, out_norm_conv, out_norm_attn, w_out, ffn2_norm, ffn2_w_gu, ffn2_w_down, final_norm, loss_target, m_meta_tokens, m_ffn1_norm, m_ffn1_w_gu, m_ffn1_w_down, m_mix_norm, m_w_in, m_conv_w, m_b_f, m_out_norm_conv, m_out_norm_attn, m_w_out, m_ffn2_norm, m_ffn2_w_gu, m_ffn2_w_down, m_final_norm, v_meta_tokens, v_ffn1_norm, v_ffn1_w_gu, v_ffn1_w_down, v_mix_norm, v_w_in, v_conv_w, v_b_f, v_out_norm_conv, v_out_norm_attn, v_w_out, v_ffn2_norm, v_ffn2_w_gu, v_ffn2_w_down, v_final_norm):
    nb, seq, _ = x.shape
    lp = PAD + N_META + seq
    n = nb * lp
    me = 4 * lax.axis_index("x") + 2 * lax.axis_index("y") + lax.axis_index("c")

    wgu1_8, wd1_8 = _all_gather([ffn1_w_gu[0].T.astype(BF16), ffn1_w_down[0].astype(BF16)], "gather_ffn1")
    small_in = jnp.concatenate(
        [meta_tokens, jnp.pad(conv_w[0], ((0, 0), (0, 128 - conv_w.shape[2]))), jnp.zeros((5, 128), F32)], axis=0)
    (small_8,) = _all_gather([small_in], "gather_small")
    meta_full = small_8[:, 0:N_META, :].transpose(1, 0, 2).reshape(N_META, D_MODEL)
    conv_full = small_8[:, N_META:N_META + 3, 0:CONV_DIM // N_DEV].transpose(1, 0, 2).reshape(3, CONV_DIM)
    wgu1 = wgu1_8.reshape(W_GU_SHAPE)
    wd1 = wd1_8.reshape(W_D_SHAPE)
    b_f_row = jnp.pad(b_f, ((0, 0), (0, 128 - N_HEADS)))
    gmat = _group_matrix()

    h0 = jnp.concatenate([jnp.zeros((nb, PAD, D_MODEL), F32),
                          jnp.broadcast_to(meta_full[None], (nb, N_META, D_MODEL)), x], axis=1).reshape(n, D_MODEL)
    later = [w_in[0].T.astype(BF16), w_out[0].astype(BF16), ffn2_w_gu[0].T.astype(BF16), ffn2_w_down[0].astype(BF16)]
    h1, n1, gate1, up1, win_8, wout_8, wgu2_8, wd2_8 = _ffn_fwd(h0, ffn1_norm, wgu1, wd1, "ffn1_fwd", gather=later)
    wgu2 = wgu2_8.reshape(W_GU_SHAPE)
    wd2 = wd2_8.reshape(W_D_SHAPE)
    w_in_full = jnp.pad(win_8.reshape(IN_DIM, D_MODEL), ((0, IN_PAD - IN_DIM), (0, 0)))
    w_out_full = wout_8.reshape(D_MODEL, D_MODEL)

    bg, cg, hc, q, k, v, fg = _inproj_fwd(h1, mix_norm, w_in_full)
    zc = _conv_fwd(bg, cg, hc, conv_full, out_norm_conv, gmat, lp)
    ka, qa = _fgate_fwd(fg, b_f_row, lp)
    za, o, lse = _attn_fwd(q, qa, k, v, ka, out_norm_attn, lp)
    h2 = _outproj_fwd(zc, za, w_out_full, h1)
    h3, n3, gate2, up2 = _ffn_fwd(h2, ffn2_norm, wgu2, wd2, "ffn2_fwd")
    loss_part, dh3, d_final = _loss_head(h3, final_norm.reshape(1, D_MODEL), loss_target.reshape(nb * seq, D_MODEL), lp)

    dh2, dhb3, dgate2, dup2, d_ffn2 = _ffn_bwd_x(dh3, h2, ffn2_norm, gate2, up2, wgu2, wd2, "ffn2_bwd_x")
    dwgu2, dwd2 = _ffn_bwd_w(dhb3, n3, gate2, up2, dgate2, dup2, "ffn2_bwd_w")
    dzc, dza, dwout = _outproj_bwd(dh2, zc, za, w_out_full)
    send_a = [dwgu2.reshape(N_DEV, F_CHUNK, D_MODEL), dwd2.reshape(N_DEV, F_CHUNK // 2, D_MODEL),
              dwout.reshape(N_DEV, D_MODEL // N_DEV, D_MODEL)]
    dq, dk, dv, dka, dfr, d_ga, p_wgu2, p_wd2, p_wout = _attn_bwd(
        dza, q, qa, k, v, ka, o, lse, out_norm_attn, lp, exchange=send_a)
    dfg, d_bf = _fgate_bwd(dka, dfr, fg, b_f_row, lp)
    dbg, dcg, dhc, d_conv, d_gc = _conv_bwd(dzc, bg, cg, hc, conv_full, out_norm_conv, gmat, lp)
    dh1, dwin, d_mix = _inproj_bwd([dbg, dcg, dhc, dq, dk, dv], dfg, dh2, h1, mix_norm, w_in_full)
    dwin_8 = dwin[0:IN_DIM].reshape(N_DEV, IN_DIM // N_DEV, D_MODEL)
    dhb1, dgate1, dup1, p_win = _ffn_bwd_act(dh1, gate1, up1, wd1, "ffn1_bwd_act", exchange=[dwin_8])
    dwgu1, dwd1 = _ffn_bwd_w(dhb1, n1, gate1, up1, dgate1, dup1, "ffn1_bwd_w")
    own = [dwgu1.reshape(N_DEV, F_CHUNK, D_MODEL), dwd1.reshape(N_DEV, F_CHUNK // 2, D_MODEL)]
    got = _pair_exchange(own, "pair_exchange_ffn1")
    chip_sums = [_pair_sum(own[0], got[0], "pair_sum_wgu1"), _pair_sum(own[1], got[1], "pair_sum_wd1")]
    dh0, d_ffn1, p_wgu1, p_wd1 = _ffn_bwd_in(
        dh1, h0, ffn1_norm, dgate1, dup1, wgu1, "ffn1_bwd_in", exchange=chip_sums)

    dh0 = dh0.reshape(nb, lp, D_MODEL)
    grad_x = dh0[:, PAD + N_META:, :]
    d_meta = jnp.sum(dh0[:, PAD:PAD + N_META, :], axis=0)

    small = _pack_small([d_ffn1, d_mix, d_ffn2, d_final], d_gc, d_ga, d_bf, d_conv, d_meta)
    (small_all,) = _all_gather([small], "gather_small_grads")
    small_sum = _sum_parts(small_all, "sum_small_grads")
    g_ffn1n, g_mixn, g_ffn2n, g_finaln = (small_sum[8 * t:8 * t + 8].reshape(1, D_MODEL) for t in range(4))
    g_gc = small_sum[32:36].reshape(1, CONV_DIM)
    g_ga = small_sum[36:40].reshape(1, ATTN_DIM)
    g_bf = small_sum[40:41, 0:N_HEADS]
    g_conv_full = small_sum[41:53].reshape(3, CONV_DIM)
    g_meta_full = small_sum[53:181].reshape(N_META, D_MODEL)
    g_conv = lax.dynamic_slice_in_dim(g_conv_full, me * (CONV_DIM // N_DEV), CONV_DIM // N_DEV, axis=1)
    g_meta = lax.dynamic_slice_in_dim(g_meta_full, me * (D_MODEL // N_DEV), D_MODEL // N_DEV, axis=1)

    weights = {
        "meta_tokens": (g_meta[None], meta_tokens, m_meta_tokens, v_meta_tokens),
        "ffn1_norm": (g_ffn1n[None], ffn1_norm, m_ffn1_norm, v_ffn1_norm),
        "ffn1_w_gu": (p_wgu1, ffn1_w_gu[0].T, m_ffn1_w_gu[0].T, v_ffn1_w_gu[0].T),
        "ffn1_w_down": (p_wd1, ffn1_w_down[0], m_ffn1_w_down[0], v_ffn1_w_down[0]),
        "mix_norm": (g_mixn[None], mix_norm, m_mix_norm, v_mix_norm),
        "w_in": (p_win, w_in[0].T, m_w_in[0].T, v_w_in[0].T),
        "conv_w": (g_conv[None], conv_w[0], m_conv_w[0], v_conv_w[0]),
        "b_f": (g_bf[None], b_f, m_b_f, v_b_f),
        "out_norm_conv": (g_gc[None], out_norm_conv, m_out_norm_conv, v_out_norm_conv),
        "out_norm_attn": (g_ga[None], out_norm_attn, m_out_norm_attn, v_out_norm_attn),
        "w_out": (p_wout, w_out[0], m_w_out[0], v_w_out[0]),
        "ffn2_norm": (g_ffn2n[None], ffn2_norm, m_ffn2_norm, v_ffn2_norm),
        "ffn2_w_gu": (p_wgu2, ffn2_w_gu[0].T, m_ffn2_w_gu[0].T, v_ffn2_w_gu[0].T),
        "ffn2_w_down": (p_wd2, ffn2_w_down[0], m_ffn2_w_down[0], v_ffn2_w_down[0]),
        "final_norm": (g_finaln[None], final_norm.reshape(1, D_MODEL), m_final_norm.reshape(1, D_MODEL),
                       v_final_norm.reshape(1, D_MODEL)),
    }
    shapes = {"meta_tokens": meta_tokens.shape, "ffn1_norm": ffn1_norm.shape, "ffn1_w_gu": ffn1_w_gu.shape,
              "ffn1_w_down": ffn1_w_down.shape, "mix_norm": mix_norm.shape, "w_in": w_in.shape,
              "conv_w": conv_w.shape, "b_f": b_f.shape, "out_norm_conv": out_norm_conv.shape,
              "out_norm_attn": out_norm_attn.shape, "w_out": w_out.shape, "ffn2_norm": ffn2_norm.shape,
              "ffn2_w_gu": ffn2_w_gu.shape, "ffn2_w_down": ffn2_w_down.shape, "final_norm": final_norm.shape}
    grads, deltas, new_m, new_v = [], [], [], []
    for name, (p, w, m, vv) in weights.items():
        g, d, nm, nv = _adamw(p, w, m, vv, "adamw_" + name)
        if name in ("ffn1_w_gu", "ffn2_w_gu", "w_in"):
            g, d, nm, nv = g.T, d.T, nm.T, nv.T
        shape = shapes[name]
        grads.append(g.reshape(shape))
        deltas.append(d.reshape(shape))
        new_m.append(nm.reshape(shape))
        new_v.append(nv.reshape(shape))

    loss = lax.psum(loss_part[0, 0], ("x", "y", "c"))
    return (loss, grad_x, *grads, *deltas, *new_m, *new_v)
```

```python
import jax
import jax.numpy as jnp
from jax import lax
from jax.experimental import pallas as pl
from jax.experimental.pallas import tpu as pltpu

F32 = jnp.float32
BF16 = jnp.bfloat16

N_DEV = 8
D_MODEL = 1024
N_META = 16
PAD = 128 - N_META
CONV_DIM = 512
ATTN_DIM = 512
HEAD_DIM = 64
N_HEADS = 8
N_PAIRS = N_HEADS // 2
D_FF = 2816
N_CHUNK = 4
F_CHUNK = D_FF // N_CHUNK
IN_DIM = 3080
IN_PAD = 3200
IN_MAIN = 3072
EPS = 1e-6
NEG = -1e30
TQ = 128
TK = 512
VMEM_LIMIT = 56 * 1024 * 1024

ADAM_LR = 0.001
ADAM_B1 = 0.9
ADAM_B2 = 0.999
ADAM_EPS = 1e-08
ADAM_WD = 0.01
ADAM_STEP = 10

MESH = pl.DeviceIdType.MESH
ANY = pl.BlockSpec(memory_space=pl.ANY)


def _params(sem=None):
    return pltpu.CompilerParams(dimension_semantics=sem, vmem_limit_bytes=VMEM_LIMIT)


def _row_tile(n, prefer):
    for t in (prefer, 512, 256, 128):
        if t <= n and n % t == 0:
            return t
    raise ValueError(f"no row tile for {n}")


def _dot(a, b):
    return jnp.dot(a, b, preferred_element_type=F32)


def _dot_nt(a, b):
    return lax.dot_general(a, b, (((1,), (1,)), ((), ())), preferred_element_type=F32)


def _dot_tn(a, b):
    return lax.dot_general(a, b, (((0,), (0,)), ((), ())), preferred_element_type=F32)


def _rms(x, g):
    r = lax.rsqrt(jnp.mean(x * x, axis=-1, keepdims=True) + EPS)
    xhat = x * r
    return xhat * g, xhat, r


def _rms_bwd(dn, xhat, r, g):
    dxhat = dn * g
    return r * (dxhat - xhat * jnp.mean(dxhat * xhat, axis=-1, keepdims=True))


def _sigmoid(x):
    return 1.0 / (1.0 + jnp.exp(-x))


def _place():
    return lax.axis_index("x"), lax.axis_index("y"), lax.axis_index("c")


def _comm_sems(nw):
    return [pltpu.SemaphoreType.DMA((nw, 7)), pltpu.SemaphoreType.DMA((nw, 7)), pltpu.SemaphoreType.DMA((nw,))]


class _Gather:
    def __init__(self, ins, outs, sems):
        self.ins, self.outs = ins, outs
        self.send, self.recv, self.local = sems
        x, y, c = _place()
        self.c = c
        self.me, self.sibling = (x, y, c), (x, y, 1 - c)
        self.chips = [(1 - x, y), (x, 1 - y), (1 - x, 1 - y)]

    def _copy(self, w, k, block, to, own=False):
        slot = self.outs[w].at[4 * block[0] + 2 * block[1] + block[2]]
        return pltpu.make_async_remote_copy(
            src_ref=self.ins[w] if own else slot, dst_ref=slot,
            send_sem=self.send.at[w, k], recv_sem=self.recv.at[w, k], device_id=to, device_id_type=MESH)

    def _mine(self, w):
        x, y, c = self.me
        return pltpu.make_async_copy(self.ins[w], self.outs[w].at[4 * x + 2 * y + c], self.local.at[w])

    def _first(self, w):
        return ([self._copy(w, 0, self.me, self.sibling, own=True)]
                + [self._copy(w, 1 + j, self.me, (*chip, self.c), own=True) for j, chip in enumerate(self.chips)])

    def _passed(self, w):
        return [self._copy(w, 4 + j, (*chip, self.c), self.sibling) for j, chip in enumerate(self.chips)]

    def start(self):
        for w in range(len(self.ins)):
            self._mine(w).start()
        for w in range(len(self.ins)):
            for cp in self._first(w):
                cp.start()

    def forward(self):
        for w in range(len(self.ins)):
            for j, chip in enumerate(self.chips):
                self._copy(w, 1 + j, (*chip, self.c), self.me).wait_recv()
                self._passed(w)[j].start()

    def finish(self):
        for w in range(len(self.ins)):
            self._copy(w, 0, self.sibling, self.me).wait_recv()
            for j, chip in enumerate(self.chips):
                self._copy(w, 4 + j, (*chip, 1 - self.c), self.me).wait_recv()
        for w in range(len(self.ins)):
            for cp in self._first(w) + self._passed(w):
                cp.wait_send()
            self._mine(w).wait()


class _Exchange:
    def __init__(self, ins, outs, sems):
        self.ins, self.outs = ins, outs
        self.send, self.recv, self.local = sems
        self.x, self.y, self.c = _place()
        self.me = 4 * self.x + 2 * self.y + self.c

    def _copy(self, w, k):
        flip = lambda v, bit: 1 - v if bit else v
        peer = (flip(self.x, ((k + 1) >> 2) & 1), flip(self.y, ((k + 1) >> 1) & 1), flip(self.c, (k + 1) & 1))
        return pltpu.make_async_remote_copy(
            src_ref=self.ins[w].at[4 * peer[0] + 2 * peer[1] + peer[2]], dst_ref=self.outs[w].at[self.me],
            send_sem=self.send.at[w, k], recv_sem=self.recv.at[w, k], device_id=peer, device_id_type=MESH)

    def _mine(self, w):
        return pltpu.make_async_copy(self.ins[w].at[self.me], self.outs[w].at[self.me], self.local.at[w])

    def start(self):
        for w in range(len(self.ins)):
            self._mine(w).start()
            for k in range(N_DEV - 1):
                self._copy(w, k).start()

    def finish(self):
        for w in range(len(self.ins)):
            for k in range(N_DEV - 1):
                self._copy(w, k).wait()
            self._mine(w).wait()


class _PairExchange:
    def __init__(self, ins, outs, sems):
        self.ins, self.outs = ins, outs
        self.send, self.recv, _ = sems
        x, y, self.c = _place()
        self.sibling = (x, y, 1 - self.c)

    def _copy(self, w, t):
        return pltpu.make_async_remote_copy(
            src_ref=self.ins[w].at[2 * t + 1 - self.c], dst_ref=self.outs[w].at[t],
            send_sem=self.send.at[w, t], recv_sem=self.recv.at[w, t], device_id=self.sibling, device_id_type=MESH)

    def start(self):
        for w in range(len(self.ins)):
            for t in range(4):
                self._copy(w, t).start()

    def finish(self):
        for w in range(len(self.ins)):
            for t in range(4):
                self._copy(w, t).wait()


class _ChipExchange:
    def __init__(self, ins, outs, sems):
        self.ins, self.outs = ins, outs
        self.send, self.recv, self.local = sems
        self.x, self.y, self.c = _place()
        self.chip = 2 * self.x + self.y

    def _copy(self, w, k):
        flip = lambda v, bit: 1 - v if bit else v
        px, py = flip(self.x, ((k + 1) >> 1) & 1), flip(self.y, (k + 1) & 1)
        return pltpu.make_async_remote_copy(
            src_ref=self.ins[w].at[2 * px + py], dst_ref=self.outs[w].at[self.chip],
            send_sem=self.send.at[w, k], recv_sem=self.recv.at[w, k], device_id=(px, py, self.c),
            device_id_type=MESH)

    def _mine(self, w):
        return pltpu.make_async_copy(self.ins[w].at[self.chip], self.outs[w].at[self.chip], self.local.at[w])

    def start(self):
        for w in range(len(self.ins)):
            self._mine(w).start()
            for k in range(3):
                self._copy(w, k).start()

    def finish(self):
        for w in range(len(self.ins)):
            for k in range(3):
                self._copy(w, k).wait()
            self._mine(w).wait()


def _pair_exchange(xs, name):
    nw = len(xs)

    def body(*refs):
        comm = _PairExchange(refs[:nw], refs[nw:2 * nw], refs[2 * nw:])
        comm.start()
        comm.finish()

    return pl.pallas_call(
        body, name=name, in_specs=[ANY] * nw, out_specs=[ANY] * nw,
        out_shape=[jax.ShapeDtypeStruct((4,) + a.shape[1:], a.dtype) for a in xs],
        scratch_shapes=_comm_sems(nw),
    )(*xs)


def _pair_sum(own, got, name):
    _, r, c = own.shape
    tr = r
    for t in (256, 128, 64, 32, 16):
        if r % t == 0 and r > t:
            tr = t
            break

    def body(own_ref, got_ref, out_ref):
        mine = jnp.where(lax.axis_index("c") == 0, own_ref[:, 0].astype(F32), own_ref[:, 1].astype(F32))
        out_ref[...] = (mine + got_ref[...].astype(F32)).astype(BF16)

    return pl.pallas_call(
        body, name=name, grid=(r // tr,),
        in_specs=[pl.BlockSpec((4, 2, tr, c), lambda i: (0, 0, i, 0)), pl.BlockSpec((4, tr, c), lambda i: (0, i, 0))],
        out_specs=pl.BlockSpec((4, tr, c), lambda i: (0, i, 0)),
        out_shape=jax.ShapeDtypeStruct((4, r, c), BF16),
        compiler_params=_params(("parallel",)),
    )(own.reshape(4, 2, r, c), got)


def _split_refs(refs, n_in, n_comm, n_out, n_scr):
    a = n_in
    b = a + n_comm
    c = b + n_out
    d = c + n_comm
    e = d + n_scr
    return refs[:a], refs[a:b], refs[b:c], refs[c:d], refs[d:e], refs[e:]


def _all_gather(xs, name):
    nw = len(xs)

    def body(*refs):
        comm = _Gather(refs[:nw], refs[nw:2 * nw], refs[2 * nw:])
        comm.start()
        comm.forward()
        comm.finish()

    return pl.pallas_call(
        body, name=name, in_specs=[ANY] * nw, out_specs=[ANY] * nw,
        out_shape=[jax.ShapeDtypeStruct((N_DEV,) + a.shape, a.dtype) for a in xs],
        scratch_shapes=_comm_sems(nw),
    )(*xs)


def _ffn_fwd(h, gain, wgu, wd, name, gather=()):
    n = h.shape[0]
    tm = _row_tile(n, 512)
    n_i = n // tm
    nw = len(gather)

    def body(*refs):
        (h_ref, g_ref, wgu_ref, wd_ref), gin, (out_ref, gate_ref, up_ref), gout, (n_scr, acc_scr), sems = \
            _split_refs(refs, 4, nw, 3, 2)
        i = pl.program_id(0)
        j = pl.program_id(1)
        if nw:
            comm = _Gather(gin, gout, sems)
            pl.when((i == 0) & (j == 0))(comm.start)
            pl.when((i == (3 * n_i) // 4) & (j == 0))(comm.forward)

        @pl.when(j == 0)
        def _():
            y, _, _ = _rms(h_ref[...], g_ref[...])
            n_scr[...] = y.astype(BF16)
            acc_scr[...] = jnp.zeros_like(acc_scr)

        nb = n_scr[...]
        gate = _dot(nb, wgu_ref[0, 0])
        up = _dot(nb, wgu_ref[1, 0])
        gate_ref[0] = gate.astype(BF16)
        up_ref[0] = up.astype(BF16)
        act = (gate * _sigmoid(gate) * up).astype(BF16)
        acc_scr[...] += _dot(act, wd_ref[0])

        @pl.when(j == N_CHUNK - 1)
        def _():
            out_ref[...] = h_ref[...] + 0.5 * acc_scr[...]

        if nw:
            pl.when((i == n_i - 1) & (j == N_CHUNK - 1))(comm.finish)

    return pl.pallas_call(
        body, name=name, grid=(n_i, N_CHUNK),
        in_specs=[pl.BlockSpec((tm, D_MODEL), lambda i, j: (i, 0)),
                  pl.BlockSpec((1, D_MODEL), lambda i, j: (0, 0)),
                  pl.BlockSpec((2, 1, D_MODEL, F_CHUNK), lambda i, j: (0, j, 0, 0)),
                  pl.BlockSpec((1, F_CHUNK, D_MODEL), lambda i, j: (j, 0, 0))] + [ANY] * nw,
        out_specs=[pl.BlockSpec((tm, D_MODEL), lambda i, j: (i, 0)),
                   pl.BlockSpec((1, tm, F_CHUNK), lambda i, j: (j, i, 0)),
                   pl.BlockSpec((1, tm, F_CHUNK), lambda i, j: (j, i, 0))] + [ANY] * nw,
        out_shape=[jax.ShapeDtypeStruct((n, D_MODEL), F32),
                   jax.ShapeDtypeStruct((N_CHUNK, n, F_CHUNK), BF16),
                   jax.ShapeDtypeStruct((N_CHUNK, n, F_CHUNK), BF16)]
        + [jax.ShapeDtypeStruct((N_DEV,) + a.shape, a.dtype) for a in gather],
        scratch_shapes=[pltpu.VMEM((tm, D_MODEL), BF16), pltpu.VMEM((tm, D_MODEL), F32)]
        + (_comm_sems(nw) if nw else []),
        compiler_params=_params(("arbitrary", "arbitrary")),
    )(h, gain, wgu, wd, *gather)


def _ffn_bwd_x(dh_out, h_in, gain, gate, up, wgu, wd, name):
    n = h_in.shape[0]
    tm = _row_tile(n, 512)

    def body(dh_ref, h_ref, g_ref, gate_ref, up_ref, wgu_ref, wd_ref,
             dhin_ref, dgate_ref, dup_ref, dgain_ref, dhb_scr, acc_scr):
        i = pl.program_id(0)
        j = pl.program_id(1)

        @pl.when((i == 0) & (j == 0))
        def _():
            dgain_ref[...] = jnp.zeros_like(dgain_ref)

        @pl.when(j == 0)
        def _():
            dhb_scr[...] = (0.5 * dh_ref[...]).astype(BF16)
            acc_scr[...] = jnp.zeros_like(acc_scr)

        da = _dot_nt(dhb_scr[...], wd_ref[0])
        g = gate_ref[0].astype(F32)
        u = up_ref[0].astype(F32)
        sig = _sigmoid(g)
        dgate = (da * u * (sig * (1.0 + g * (1.0 - sig)))).astype(BF16)
        dup = (da * (g * sig)).astype(BF16)
        dgate_ref[0] = dgate
        dup_ref[0] = dup
        acc_scr[...] += _dot_nt(dgate, wgu_ref[0, 0]) + _dot_nt(dup, wgu_ref[1, 0])

        @pl.when(j == N_CHUNK - 1)
        def _():
            gain_v = g_ref[...]
            _, xhat, r = _rms(h_ref[...], gain_v)
            dn = acc_scr[...]
            dhin_ref[...] = dh_ref[...] + _rms_bwd(dn, xhat, r, gain_v)
            dgain_ref[...] += jnp.sum(dn * xhat, axis=0, keepdims=True)

    chunk = pl.BlockSpec((1, tm, F_CHUNK), lambda i, j: (j, i, 0))
    rows = pl.BlockSpec((tm, D_MODEL), lambda i, j: (i, 0))
    vec = pl.BlockSpec((1, D_MODEL), lambda i, j: (0, 0))
    return pl.pallas_call(
        body, name=name, grid=(n // tm, N_CHUNK),
        in_specs=[rows, rows, vec, chunk, chunk,
                  pl.BlockSpec((2, 1, D_MODEL, F_CHUNK), lambda i, j: (0, j, 0, 0)),
                  pl.BlockSpec((1, F_CHUNK, D_MODEL), lambda i, j: (j, 0, 0))],
        out_specs=[rows, chunk, chunk, vec],
        out_shape=[jax.ShapeDtypeStruct((n, D_MODEL), F32),
                   jax.ShapeDtypeStruct((N_CHUNK, n, F_CHUNK), BF16),
                   jax.ShapeDtypeStruct((N_CHUNK, n, F_CHUNK), BF16),
                   jax.ShapeDtypeStruct((1, D_MODEL), F32)],
        scratch_shapes=[pltpu.VMEM((tm, D_MODEL), BF16), pltpu.VMEM((tm, D_MODEL), F32)],
        compiler_params=_params(("arbitrary", "arbitrary")),
    )(dh_out, h_in, gain, gate, up, wgu, wd)


def _ffn_bwd_act(dh_out, gate, up, wd, name):
    n = dh_out.shape[0]
    tm = _row_tile(n, 512)

    def body(dh_ref, gate_ref, up_ref, wd_ref, dgate_ref, dup_ref, dhb_scr):
        @pl.when(pl.program_id(1) == 0)
        def _():
            dhb_scr[...] = (0.5 * dh_ref[...]).astype(BF16)

        da = _dot_nt(dhb_scr[...], wd_ref[0])
        g = gate_ref[0].astype(F32)
        u = up_ref[0].astype(F32)
        sig = _sigmoid(g)
        dgate_ref[0] = (da * u * (sig * (1.0 + g * (1.0 - sig)))).astype(BF16)
        dup_ref[0] = (da * (g * sig)).astype(BF16)

    chunk = pl.BlockSpec((1, tm, F_CHUNK), lambda i, j: (j, i, 0))
    return pl.pallas_call(
        body, name=name, grid=(n // tm, N_CHUNK),
        in_specs=[pl.BlockSpec((tm, D_MODEL), lambda i, j: (i, 0)), chunk, chunk,
                  pl.BlockSpec((1, F_CHUNK, D_MODEL), lambda i, j: (j, 0, 0))],
        out_specs=[chunk, chunk],
        out_shape=[jax.ShapeDtypeStruct((N_CHUNK, n, F_CHUNK), BF16)] * 2,
        scratch_shapes=[pltpu.VMEM((tm, D_MODEL), BF16)],
        compiler_params=_params(("parallel", "arbitrary")),
    )(dh_out, gate, up, wd)


def _ffn_bwd_in(dh_out, h_in, gain, dgate, dup, wgu, name, exchange=()):
    n = h_in.shape[0]
    tm = _row_tile(n, 512)
    n_i = n // tm
    nw = len(exchange)

    def body(*refs):
        (dh_ref, h_ref, g_ref, dgate_ref, dup_ref, wgu_ref), xin, (dhin_ref, dgain_ref), xout, (acc_scr,), sems = \
            _split_refs(refs, 6, nw, 2, 1)
        i = pl.program_id(0)
        j = pl.program_id(1)
        if nw:
            comm = _Exchange(xin, xout, sems)
            pl.when((i == 0) & (j == 0))(comm.start)

        @pl.when((i == 0) & (j == 0))
        def _():
            dgain_ref[...] = jnp.zeros_like(dgain_ref)

        @pl.when(j == 0)
        def _():
            acc_scr[...] = jnp.zeros_like(acc_scr)

        acc_scr[...] += _dot_nt(dgate_ref[0], wgu_ref[0, 0]) + _dot_nt(dup_ref[0], wgu_ref[1, 0])

        @pl.when(j == N_CHUNK - 1)
        def _():
            gain_v = g_ref[...]
            _, xhat, r = _rms(h_ref[...], gain_v)
            dn = acc_scr[...]
            dhin_ref[...] = dh_ref[...] + _rms_bwd(dn, xhat, r, gain_v)
            dgain_ref[...] += jnp.sum(dn * xhat, axis=0, keepdims=True)

        if nw:
            pl.when((i == n_i - 1) & (j == N_CHUNK - 1))(comm.finish)

    chunk = pl.BlockSpec((1, tm, F_CHUNK), lambda i, j: (j, i, 0))
    rows = pl.BlockSpec((tm, D_MODEL), lambda i, j: (i, 0))
    vec = pl.BlockSpec((1, D_MODEL), lambda i, j: (0, 0))
    return pl.pallas_call(
        body, name=name, grid=(n_i, N_CHUNK),
        in_specs=[rows, rows, vec, chunk, chunk,
                  pl.BlockSpec((2, 1, D_MODEL, F_CHUNK), lambda i, j: (0, j, 0, 0))] + [ANY] * nw,
        out_specs=[rows, vec] + [ANY] * nw,
        out_shape=[jax.ShapeDtypeStruct((n, D_MODEL), F32), jax.ShapeDtypeStruct((1, D_MODEL), F32)]
        + [jax.ShapeDtypeStruct(a.shape, a.dtype) for a in exchange],
        scratch_shapes=[pltpu.VMEM((tm, D_MODEL), F32)] + (_comm_sems(nw) if nw else []),
        compiler_params=_params(("arbitrary", "arbitrary")),
    )(dh_out, h_in, gain, dgate, dup, wgu, *exchange)


def _ffn_bwd_w(dh_out, h_in, gain, gate, up, dgate, dup, name):
    n = h_in.shape[0]
    tm = _row_tile(n, 512)
    n_i = n // tm

    def body(dh_ref, h_ref, g_ref, gate_ref, up_ref, dgate_ref, dup_ref, dwgu_ref, dwd_ref,
             ag_scr, au_scr, ad_scr):
        i = pl.program_id(1)

        @pl.when(i == 0)
        def _():
            ag_scr[...] = jnp.zeros_like(ag_scr)
            au_scr[...] = jnp.zeros_like(au_scr)
            ad_scr[...] = jnp.zeros_like(ad_scr)

        y, _, _ = _rms(h_ref[...], g_ref[...])
        nb = y.astype(BF16)
        ag_scr[...] += _dot_tn(nb, dgate_ref[0])
        au_scr[...] += _dot_tn(nb, dup_ref[0])
        g = gate_ref[0].astype(F32)
        act = (g * _sigmoid(g) * up_ref[0].astype(F32)).astype(BF16)
        ad_scr[...] += _dot_tn(act, (0.5 * dh_ref[...]).astype(BF16))

        @pl.when(i == n_i - 1)
        def _():
            dwgu_ref[0, 0] = ag_scr[...].astype(BF16)
            dwgu_ref[1, 0] = au_scr[...].astype(BF16)
            dwd_ref[0] = ad_scr[...].astype(BF16)

    chunk = pl.BlockSpec((1, tm, F_CHUNK), lambda j, i: (j, i, 0))
    rows = pl.BlockSpec((tm, D_MODEL), lambda j, i: (i, 0))
    return pl.pallas_call(
        body, name=name, grid=(N_CHUNK, n_i),
        in_specs=[rows, rows, pl.BlockSpec((1, D_MODEL), lambda j, i: (0, 0)), chunk, chunk, chunk, chunk],
        out_specs=[pl.BlockSpec((2, 1, D_MODEL, F_CHUNK), lambda j, i: (0, j, 0, 0)),
                   pl.BlockSpec((1, F_CHUNK, D_MODEL), lambda j, i: (j, 0, 0))],
        out_shape=[jax.ShapeDtypeStruct((2, N_CHUNK, D_MODEL, F_CHUNK), BF16),
                   jax.ShapeDtypeStruct((N_CHUNK, F_CHUNK, D_MODEL), BF16)],
        scratch_shapes=[pltpu.VMEM((D_MODEL, F_CHUNK), F32), pltpu.VMEM((D_MODEL, F_CHUNK), F32),
                        pltpu.VMEM((F_CHUNK, D_MODEL), F32)],
        compiler_params=_params(("parallel", "arbitrary")),
    )(dh_out, h_in, gain, gate, up, dgate, dup)


def _resident(shape, rank):
    zeros = (0,) * len(shape)
    index_map = (lambda i: zeros) if rank == 1 else (lambda i, j: zeros)
    return pl.BlockSpec(shape, index_map, pipeline_mode=pl.Buffered(1))


W_GU_SHAPE = (2, N_CHUNK, F_CHUNK, D_MODEL)
W_D_SHAPE = (N_CHUNK, F_CHUNK, D_MODEL)


def _ffn_fwd(h, gain, wgu, wd, name, gather=()):
    n = h.shape[0]
    tm = _row_tile(n, 512)
    n_i = n // tm
    nw = len(gather)

    def body(*refs):
        (h_ref, g_ref, wgu_ref, wd_ref), gin, (out_ref, nrm_ref, gate_ref, up_ref), gout, _, sems = \
            _split_refs(refs, 4, nw, 4, 0)
        i = pl.program_id(0)
        if nw:
            comm = _Gather(gin, gout, sems)
            pl.when(i == 0)(comm.start)
            pl.when(i == (3 * n_i) // 4)(comm.forward)

        hv = h_ref[...]
        y, _, _ = _rms(hv, g_ref[...])
        nb = y.astype(BF16)
        nrm_ref[...] = nb
        acc = jnp.zeros((tm, D_MODEL), F32)
        for j in range(N_CHUNK):
            gate = _dot_nt(nb, wgu_ref[0, j])
            up = _dot_nt(nb, wgu_ref[1, j])
            gate_ref[j] = gate.astype(BF16)
            up_ref[j] = up.astype(BF16)
            acc = acc + _dot((gate * _sigmoid(gate) * up).astype(BF16), wd_ref[j])
        out_ref[...] = hv + 0.5 * acc

        if nw:
            pl.when(i == n_i - 1)(comm.finish)

    rows = pl.BlockSpec((tm, D_MODEL), lambda i: (i, 0))
    chunks = pl.BlockSpec((N_CHUNK, tm, F_CHUNK), lambda i: (0, i, 0))
    return pl.pallas_call(
        body, name=name, grid=(n_i,),
        in_specs=[rows, pl.BlockSpec((1, D_MODEL), lambda i: (0, 0)), _resident(W_GU_SHAPE, 1),
                  _resident(W_D_SHAPE, 1)] + [ANY] * nw,
        out_specs=[rows, rows, chunks, chunks] + [ANY] * nw,
        out_shape=[jax.ShapeDtypeStruct((n, D_MODEL), F32), jax.ShapeDtypeStruct((n, D_MODEL), BF16),
                   jax.ShapeDtypeStruct((N_CHUNK, n, F_CHUNK), BF16),
                   jax.ShapeDtypeStruct((N_CHUNK, n, F_CHUNK), BF16)]
        + [jax.ShapeDtypeStruct((N_DEV,) + a.shape, a.dtype) for a in gather],
        scratch_shapes=_comm_sems(nw) if nw else [],
        compiler_params=_params(("arbitrary",)),
    )(h, gain, wgu, wd, *gather)


def _swiglu_bwd(da, gate_ref, up_ref, j):
    g = gate_ref[j].astype(F32)
    u = up_ref[j].astype(F32)
    sig = _sigmoid(g)
    return (da * u * (sig * (1.0 + g * (1.0 - sig)))).astype(BF16), (da * (g * sig)).astype(BF16)


def _ffn_bwd_x(dh_out, h_in, gain, gate, up, wgu, wd, name):
    n = h_in.shape[0]
    tm = _row_tile(n, 256)

    def body(dh_ref, h_ref, g_ref, gate_ref, up_ref, wgu_ref, wd_ref,
             dhin_ref, dhb_ref, dgate_ref, dup_ref, dgain_ref):
        @pl.when(pl.program_id(0) == 0)
        def _():
            dgain_ref[...] = jnp.zeros_like(dgain_ref)

        dhv = dh_ref[...]
        dhb = (0.5 * dhv).astype(BF16)
        dhb_ref[...] = dhb
        dn = jnp.zeros((tm, D_MODEL), F32)
        for j in range(N_CHUNK):
            dgate, dup = _swiglu_bwd(_dot_nt(dhb, wd_ref[j]), gate_ref, up_ref, j)
            dgate_ref[j] = dgate
            dup_ref[j] = dup
            dn = dn + _dot(dgate, wgu_ref[0, j]) + _dot(dup, wgu_ref[1, j])
        gain_v = g_ref[...]
        _, xhat, r = _rms(h_ref[...], gain_v)
        dhin_ref[...] = dhv + _rms_bwd(dn, xhat, r, gain_v)
        dgain_ref[...] += jnp.sum(dn * xhat, axis=0, keepdims=True)

    rows = pl.BlockSpec((tm, D_MODEL), lambda i: (i, 0))
    chunks = pl.BlockSpec((N_CHUNK, tm, F_CHUNK), lambda i: (0, i, 0))
    vec = pl.BlockSpec((1, D_MODEL), lambda i: (0, 0))
    return pl.pallas_call(
        body, name=name, grid=(n // tm,),
        in_specs=[rows, rows, vec, chunks, chunks, _resident(W_GU_SHAPE, 1), _resident(W_D_SHAPE, 1)],
        out_specs=[rows, rows, chunks, chunks, vec],
        out_shape=[jax.ShapeDtypeStruct((n, D_MODEL), F32), jax.ShapeDtypeStruct((n, D_MODEL), BF16),
                   jax.ShapeDtypeStruct((N_CHUNK, n, F_CHUNK), BF16),
                   jax.ShapeDtypeStruct((N_CHUNK, n, F_CHUNK), BF16),
                   jax.ShapeDtypeStruct((1, D_MODEL), F32)],
        compiler_params=_params(("arbitrary",)),
    )(dh_out, h_in, gain, gate, up, wgu, wd)


def _ffn_bwd_act(dh_out, gate, up, wd, name, exchange=()):
    n = dh_out.shape[0]
    tm = _row_tile(n, 512)
    n_i = n // tm
    nw = len(exchange)

    def body(*refs):
        (dh_ref, gate_ref, up_ref, wd_ref), xin, (dhb_ref, dgate_ref, dup_ref), xout, _, sems = \
            _split_refs(refs, 4, nw, 3, 0)
        i = pl.program_id(0)
        if nw:
            comm = _Exchange(xin, xout, sems)
            pl.when(i == 0)(comm.start)

        dhb = (0.5 * dh_ref[...]).astype(BF16)
        dhb_ref[...] = dhb
        for j in range(N_CHUNK):
            dgate_ref[j], dup_ref[j] = _swiglu_bwd(_dot_nt(dhb, wd_ref[j]), gate_ref, up_ref, j)

        if nw:
            pl.when(i == n_i - 1)(comm.finish)

    rows = pl.BlockSpec((tm, D_MODEL), lambda i: (i, 0))
    chunks = pl.BlockSpec((N_CHUNK, tm, F_CHUNK), lambda i: (0, i, 0))
    return pl.pallas_call(
        body, name=name, grid=(n_i,),
        in_specs=[rows, chunks, chunks, _resident(W_D_SHAPE, 1)] + [ANY] * nw,
        out_specs=[rows, chunks, chunks] + [ANY] * nw,
        out_shape=[jax.ShapeDtypeStruct((n, D_MODEL), BF16)] + [jax.ShapeDtypeStruct((N_CHUNK, n, F_CHUNK), BF16)] * 2
        + [jax.ShapeDtypeStruct(a.shape, a.dtype) for a in exchange],
        scratch_shapes=_comm_sems(nw) if nw else [],
        compiler_params=_params(("arbitrary",)),
    )(dh_out, gate, up, wd, *exchange)


def _ffn_bwd_in(dh_out, h_in, gain, dgate, dup, wgu, name, exchange=()):
    n = h_in.shape[0]
    tm = _row_tile(n, 512)
    n_i = n // tm
    nw = len(exchange)

    def body(*refs):
        (dh_ref, h_ref, g_ref, dgate_ref, dup_ref, wgu_ref), xin, (dhin_ref, dgain_ref), xout, _, sems = \
            _split_refs(refs, 6, nw, 2, 0)
        i = pl.program_id(0)
        if nw:
            comm = _ChipExchange(xin, xout, sems)
            pl.when(i == 0)(comm.start)

        @pl.when(i == 0)
        def _():
            dgain_ref[...] = jnp.zeros_like(dgain_ref)

        dn = jnp.zeros((tm, D_MODEL), F32)
        for j in range(N_CHUNK):
            dn = dn + _dot(dgate_ref[j], wgu_ref[0, j]) + _dot(dup_ref[j], wgu_ref[1, j])
        gain_v = g_ref[...]
        _, xhat, r = _rms(h_ref[...], gain_v)
        dhin_ref[...] = dh_ref[...] + _rms_bwd(dn, xhat, r, gain_v)
        dgain_ref[...] += jnp.sum(dn * xhat, axis=0, keepdims=True)

        if nw:
            pl.when(i == n_i - 1)(comm.finish)

    rows = pl.BlockSpec((tm, D_MODEL), lambda i: (i, 0))
    chunks = pl.BlockSpec((N_CHUNK, tm, F_CHUNK), lambda i: (0, i, 0))
    vec = pl.BlockSpec((1, D_MODEL), lambda i: (0, 0))
    return pl.pallas_call(
        body, name=name, grid=(n_i,),
        in_specs=[rows, rows, vec, chunks, chunks, _resident(W_GU_SHAPE, 1)] + [ANY] * nw,
        out_specs=[rows, vec] + [ANY] * nw,
        out_shape=[jax.ShapeDtypeStruct((n, D_MODEL), F32), jax.ShapeDtypeStruct((1, D_MODEL), F32)]
        + [jax.ShapeDtypeStruct(a.shape, a.dtype) for a in exchange],
        scratch_shapes=_comm_sems(nw) if nw else [],
        compiler_params=_params(("arbitrary",)),
    )(dh_out, h_in, gain, dgate, dup, wgu, *exchange)


W_GROUP = 2


def _ffn_bwd_w(dhb, nrm, gate, up, dgate, dup, name):
    n = nrm.shape[0]
    tm = _row_tile(n, 512)
    n_i = n // tm

    def body(dhb_ref, nrm_ref, gate_ref, up_ref, dgate_ref, dup_ref, dwgu_ref, dwd_ref, ag_scr, au_scr, ad_scr):
        i = pl.program_id(1)

        @pl.when(i == 0)
        def _():
            ag_scr[...] = jnp.zeros_like(ag_scr)
            au_scr[...] = jnp.zeros_like(au_scr)
            ad_scr[...] = jnp.zeros_like(ad_scr)

        nb = nrm_ref[...]
        dhv = dhb_ref[...]
        for jj in range(W_GROUP):
            ag_scr[jj] += _dot_tn(dgate_ref[jj], nb)
            au_scr[jj] += _dot_tn(dup_ref[jj], nb)
            g = gate_ref[jj].astype(F32)
            act = (g * _sigmoid(g) * up_ref[jj].astype(F32)).astype(BF16)
            ad_scr[jj] += _dot_tn(act, dhv)

        @pl.when(i == n_i - 1)
        def _():
            dwgu_ref[0] = ag_scr[...].astype(BF16)
            dwgu_ref[1] = au_scr[...].astype(BF16)
            dwd_ref[...] = ad_scr[...].astype(BF16)

    chunks = pl.BlockSpec((W_GROUP, tm, F_CHUNK), lambda g, i: (g, i, 0))
    rows = pl.BlockSpec((tm, D_MODEL), lambda g, i: (i, 0))
    return pl.pallas_call(
        body, name=name, grid=(N_CHUNK // W_GROUP, n_i),
        in_specs=[rows, rows, chunks, chunks, chunks, chunks],
        out_specs=[pl.BlockSpec((2, W_GROUP, F_CHUNK, D_MODEL), lambda g, i: (0, g, 0, 0)),
                   pl.BlockSpec((W_GROUP, F_CHUNK, D_MODEL), lambda g, i: (g, 0, 0))],
        out_shape=[jax.ShapeDtypeStruct(W_GU_SHAPE, BF16), jax.ShapeDtypeStruct(W_D_SHAPE, BF16)],
        scratch_shapes=[pltpu.VMEM((W_GROUP, F_CHUNK, D_MODEL), F32), pltpu.VMEM((W_GROUP, F_CHUNK, D_MODEL), F32),
                        pltpu.VMEM((W_GROUP, F_CHUNK, D_MODEL), F32)],
        compiler_params=_params(("parallel", "arbitrary")),
    )(dhb, nrm, gate, up, dgate, dup)


HID_PIECES = ((0, 1024), (1024, 2048), (2048, D_FF))
W_GU_SHAPE = (2, D_FF, D_MODEL)
W_D_SHAPE = (D_FF, D_MODEL)


def _ffn_fwd(h, gain, wgu, wd, name, gather=()):
    n = h.shape[0]
    tm = _row_tile(n, 512)
    n_i = n // tm
    nw = len(gather)

    def body(*refs):
        (h_ref, g_ref, wgu_ref, wd_ref), gin, (out_ref, nrm_ref, gate_ref, up_ref), gout, _, sems = \
            _split_refs(refs, 4, nw, 4, 0)
        i = pl.program_id(0)
        if nw:
            comm = _Gather(gin, gout, sems)
            pl.when(i == 0)(comm.start)
            pl.when(i == (3 * n_i) // 4)(comm.forward)

        hv = h_ref[...]
        y, _, _ = _rms(hv, g_ref[...])
        nb = y.astype(BF16)
        nrm_ref[...] = nb
        acc = jnp.zeros((tm, D_MODEL), F32)
        for a, b in HID_PIECES:
            gate = _dot_nt(nb, wgu_ref[0, a:b, :])
            up = _dot_nt(nb, wgu_ref[1, a:b, :])
            gate_ref[:, a:b] = gate.astype(BF16)
            up_ref[:, a:b] = up.astype(BF16)
            acc = acc + _dot((gate * _sigmoid(gate) * up).astype(BF16), wd_ref[a:b, :])
        out_ref[...] = hv + 0.5 * acc

        if nw:
            pl.when(i == n_i - 1)(comm.finish)

    rows = pl.BlockSpec((tm, D_MODEL), lambda i: (i, 0))
    hid = pl.BlockSpec((tm, D_FF), lambda i: (i, 0))
    return pl.pallas_call(
        body, name=name, grid=(n_i,),
        in_specs=[rows, pl.BlockSpec((1, D_MODEL), lambda i: (0, 0)), _resident(W_GU_SHAPE, 1),
                  _resident(W_D_SHAPE, 1)] + [ANY] * nw,
        out_specs=[rows, rows, hid, hid] + [ANY] * nw,
        out_shape=[jax.ShapeDtypeStruct((n, D_MODEL), F32), jax.ShapeDtypeStruct((n, D_MODEL), BF16),
                   jax.ShapeDtypeStruct((n, D_FF), BF16), jax.ShapeDtypeStruct((n, D_FF), BF16)]
        + [jax.ShapeDtypeStruct((N_DEV,) + a.shape, a.dtype) for a in gather],
        scratch_shapes=_comm_sems(nw) if nw else [],
        compiler_params=_params(("arbitrary",)),
    )(h, gain, wgu, wd, *gather)


def _swiglu_bwd(da, gate_ref, up_ref, a, b):
    g = gate_ref[:, a:b].astype(F32)
    u = up_ref[:, a:b].astype(F32)
    sig = _sigmoid(g)
    return (da * u * (sig * (1.0 + g * (1.0 - sig)))).astype(BF16), (da * (g * sig)).astype(BF16)


def _ffn_bwd_x(dh_out, h_in, gain, gate, up, wgu, wd, name):
    n = h_in.shape[0]
    tm = _row_tile(n, 256)

    def body(dh_ref, h_ref, g_ref, gate_ref, up_ref, wgu_ref, wd_ref,
             dhin_ref, dhb_ref, dgate_ref, dup_ref, dgain_ref):
        @pl.when(pl.program_id(0) == 0)
        def _():
            dgain_ref[...] = jnp.zeros_like(dgain_ref)

        dhv = dh_ref[...]
        dhb = (0.5 * dhv).astype(BF16)
        dhb_ref[...] = dhb
        dn = jnp.zeros((tm, D_MODEL), F32)
        for a, b in HID_PIECES:
            dgate, dup = _swiglu_bwd(_dot_nt(dhb, wd_ref[a:b, :]), gate_ref, up_ref, a, b)
            dgate_ref[:, a:b] = dgate
            dup_ref[:, a:b] = dup
            dn = dn + _dot(dgate, wgu_ref[0, a:b, :]) + _dot(dup, wgu_ref[1, a:b, :])
        gain_v = g_ref[...]
        _, xhat, r = _rms(h_ref[...], gain_v)
        dhin_ref[...] = dhv + _rms_bwd(dn, xhat, r, gain_v)
        dgain_ref[...] += jnp.sum(dn * xhat, axis=0, keepdims=True)

    rows = pl.BlockSpec((tm, D_MODEL), lambda i: (i, 0))
    hid = pl.BlockSpec((tm, D_FF), lambda i: (i, 0))
    vec = pl.BlockSpec((1, D_MODEL), lambda i: (0, 0))
    return pl.pallas_call(
        body, name=name, grid=(n // tm,),
        in_specs=[rows, rows, vec, hid, hid, _resident(W_GU_SHAPE, 1), _resident(W_D_SHAPE, 1)],
        out_specs=[rows, rows, hid, hid, vec],
        out_shape=[jax.ShapeDtypeStruct((n, D_MODEL), F32), jax.ShapeDtypeStruct((n, D_MODEL), BF16),
                   jax.ShapeDtypeStruct((n, D_FF), BF16), jax.ShapeDtypeStruct((n, D_FF), BF16),
                   jax.ShapeDtypeStruct((1, D_MODEL), F32)],
        compiler_params=_params(("arbitrary",)),
    )(dh_out, h_in, gain, gate, up, wgu, wd)


def _ffn_bwd_act(dh_out, gate, up, wd, name, exchange=()):
    n = dh_out.shape[0]
    tm = _row_tile(n, 512)
    n_i = n // tm
    nw = len(exchange)

    def body(*refs):
        (dh_ref, gate_ref, up_ref, wd_ref), xin, (dhb_ref, dgate_ref, dup_ref), xout, _, sems = \
            _split_refs(refs, 4, nw, 3, 0)
        i = pl.program_id(0)
        if nw:
            comm = _Exchange(xin, xout, sems)
            pl.when(i == 0)(comm.start)

        dhb = (0.5 * dh_ref[...]).astype(BF16)
        dhb_ref[...] = dhb
        for a, b in HID_PIECES:
            dgate_ref[:, a:b], dup_ref[:, a:b] = _swiglu_bwd(_dot_nt(dhb, wd_ref[a:b, :]), gate_ref, up_ref, a, b)

        if nw:
            pl.when(i == n_i - 1)(comm.finish)

    rows = pl.BlockSpec((tm, D_MODEL), lambda i: (i, 0))
    hid = pl.BlockSpec((tm, D_FF), lambda i: (i, 0))
    return pl.pallas_call(
        body, name=name, grid=(n_i,),
        in_specs=[rows, hid, hid, _resident(W_D_SHAPE, 1)] + [ANY] * nw,
        out_specs=[rows, hid, hid] + [ANY] * nw,
        out_shape=[jax.ShapeDtypeStruct((n, D_MODEL), BF16)] + [jax.ShapeDtypeStruct((n, D_FF), BF16)] * 2
        + [jax.ShapeDtypeStruct(a.shape, a.dtype) for a in exchange],
        scratch_shapes=_comm_sems(nw) if nw else [],
        compiler_params=_params(("arbitrary",)),
    )(dh_out, gate, up, wd, *exchange)


def _ffn_bwd_in(dh_out, h_in, gain, dgate, dup, wgu, name, exchange=()):
    n = h_in.shape[0]
    tm = _row_tile(n, 512)
    n_i = n // tm
    nw = len(exchange)

    def body(*refs):
        (dh_ref, h_ref, g_ref, dgate_ref, dup_ref, wgu_ref), xin, (dhin_ref, dgain_ref), xout, _, sems = \
            _split_refs(refs, 6, nw, 2, 0)
        i = pl.program_id(0)
        if nw:
            comm = _ChipExchange(xin, xout, sems)
            pl.when(i == 0)(comm.start)

        @pl.when(i == 0)
        def _():
            dgain_ref[...] = jnp.zeros_like(dgain_ref)

        dn = jnp.zeros((tm, D_MODEL), F32)
        for a, b in HID_PIECES:
            dn = dn + _dot(dgate_ref[:, a:b], wgu_ref[0, a:b, :]) + _dot(dup_ref[:, a:b], wgu_ref[1, a:b, :])
        gain_v = g_ref[...]
        _, xhat, r = _rms(h_ref[...], gain_v)
        dhin_ref[...] = dh_ref[...] + _rms_bwd(dn, xhat, r, gain_v)
        dgain_ref[...] += jnp.sum(dn * xhat, axis=0, keepdims=True)

        if nw:
            pl.when(i == n_i - 1)(comm.finish)

    rows = pl.BlockSpec((tm, D_MODEL), lambda i: (i, 0))
    hid = pl.BlockSpec((tm, D_FF), lambda i: (i, 0))
    vec = pl.BlockSpec((1, D_MODEL), lambda i: (0, 0))
    return pl.pallas_call(
        body, name=name, grid=(n_i,),
        in_specs=[rows, rows, vec, hid, hid, _resident(W_GU_SHAPE, 1)] + [ANY] * nw,
        out_specs=[rows, vec] + [ANY] * nw,
        out_shape=[jax.ShapeDtypeStruct((n, D_MODEL), F32), jax.ShapeDtypeStruct((1, D_MODEL), F32)]
        + [jax.ShapeDtypeStruct(a.shape, a.dtype) for a in exchange],
        scratch_shapes=_comm_sems(nw) if nw else [],
        compiler_params=_params(("arbitrary",)),
    )(dh_out, h_in, gain, dgate, dup, wgu, *exchange)


def _ffn_bwd_wgu(nrm, dgate, dup, name):
    n = nrm.shape[0]
    tm = _row_tile(n, 256)
    n_i = n // tm

    def body(nrm_ref, dgate_ref, dup_ref, dw_ref, acc_scr):
        i = pl.program_id(0)

        @pl.when(i == 0)
        def _():
            acc_scr[...] = jnp.zeros_like(acc_scr)

        nb = nrm_ref[...]
        for a, b in HID_PIECES:
            acc_scr[0, a:b, :] += _dot_tn(dgate_ref[:, a:b], nb)
            acc_scr[1, a:b, :] += _dot_tn(dup_ref[:, a:b], nb)

        @pl.when(i == n_i - 1)
        def _():
            dw_ref[...] = acc_scr[...].astype(BF16)

    hid = pl.BlockSpec((tm, D_FF), lambda i: (i, 0))
    return pl.pallas_call(
        body, name=name, grid=(n_i,),
        in_specs=[pl.BlockSpec((tm, D_MODEL), lambda i: (i, 0)), hid, hid],
        out_specs=_resident(W_GU_SHAPE, 1),
        out_shape=jax.ShapeDtypeStruct(W_GU_SHAPE, BF16),
        scratch_shapes=[pltpu.VMEM(W_GU_SHAPE, F32)],
        compiler_params=_params(("arbitrary",)),
    )(nrm, dgate, dup)


def _ffn_bwd_wd(dhb, gate, up, name):
    n = dhb.shape[0]
    tm = _row_tile(n, 512)
    n_i = n // tm

    def body(dhb_ref, gate_ref, up_ref, dw_ref, acc_scr):
        i = pl.program_id(0)

        @pl.when(i == 0)
        def _():
            acc_scr[...] = jnp.zeros_like(acc_scr)

        dhv = dhb_ref[...]
        for a, b in HID_PIECES:
            g = gate_ref[:, a:b].astype(F32)
            act = (g * _sigmoid(g) * up_ref[:, a:b].astype(F32)).astype(BF16)
            acc_scr[a:b, :] += _dot_tn(act, dhv)

        @pl.when(i == n_i - 1)
        def _():
            dw_ref[...] = acc_scr[...].astype(BF16)

    hid = pl.BlockSpec((tm, D_FF), lambda i: (i, 0))
    return pl.pallas_call(
        body, name=name, grid=(n_i,),
        in_specs=[pl.BlockSpec((tm, D_MODEL), lambda i: (i, 0)), hid, hid],
        out_specs=_resident(W_D_SHAPE, 1),
        out_shape=jax.ShapeDtypeStruct(W_D_SHAPE, BF16),
        scratch_shapes=[pltpu.VMEM(W_D_SHAPE, F32)],
        compiler_params=_params(("arbitrary",)),
    )(dhb, gate, up)


N_PIECE = IN_MAIN // 512


def _inproj_fwd(h, gain, w_in):
    n = h.shape[0]
    tm = _row_tile(n, 512)

    def body(h_ref, g_ref, w_ref, *outs):
        y, _, _ = _rms(h_ref[...], g_ref[...])
        nb = y.astype(BF16)
        for p in range(N_PIECE):
            outs[p][...] = _dot_nt(nb, w_ref[512 * p:512 * (p + 1), :]).astype(BF16)
        outs[N_PIECE][...] = _dot_nt(nb, w_ref[IN_MAIN:IN_PAD, :])

    piece = pl.BlockSpec((tm, 512), lambda i: (i, 0))
    return pl.pallas_call(
        body, name="inproj_fwd", grid=(n // tm,),
        in_specs=[pl.BlockSpec((tm, D_MODEL), lambda i: (i, 0)),
                  pl.BlockSpec((1, D_MODEL), lambda i: (0, 0)),
                  pl.BlockSpec((IN_PAD, D_MODEL), lambda i: (0, 0))],
        out_specs=[piece] * N_PIECE + [pl.BlockSpec((tm, 128), lambda i: (i, 0))],
        out_shape=[jax.ShapeDtypeStruct((n, 512), BF16)] * N_PIECE + [jax.ShapeDtypeStruct((n, 128), F32)],
        compiler_params=_params(("parallel",)),
    )(h, gain, w_in)


def _inproj_bwd(dpieces, dfg, dh_out, h_in, gain, w_in):
    n = h_in.shape[0]
    tm = _row_tile(n, 512)
    n_i = n // tm

    def body(*refs):
        dp_refs = refs[:N_PIECE]
        dfg_ref, dh_ref, h_ref, g_ref, w_ref, dhin_ref, dw_ref, dgain_ref, acc_scr = refs[N_PIECE:]
        i = pl.program_id(0)

        @pl.when(i == 0)
        def _():
            acc_scr[...] = jnp.zeros_like(acc_scr)
            dgain_ref[...] = jnp.zeros_like(dgain_ref)

        gain_v = g_ref[...]
        y, xhat, r = _rms(h_ref[...], gain_v)
        nb = y.astype(BF16)
        dn = jnp.zeros((tm, D_MODEL), F32)
        for p in range(N_PIECE + 1):
            lo, hi = (512 * p, 512 * (p + 1)) if p < N_PIECE else (IN_MAIN, IN_PAD)
            dp = (dp_refs[p][...] if p < N_PIECE else dfg_ref[...]).astype(BF16)
            dn = dn + _dot(dp, w_ref[lo:hi, :])
            acc_scr[lo:hi, :] += _dot_tn(dp, nb)
        dhin_ref[...] = dh_ref[...] + _rms_bwd(dn, xhat, r, gain_v)
        dgain_ref[...] += jnp.sum(dn * xhat, axis=0, keepdims=True)

        @pl.when(i == n_i - 1)
        def _():
            dw_ref[...] = acc_scr[...].astype(BF16)

    piece = pl.BlockSpec((tm, 512), lambda i: (i, 0))
    rows = pl.BlockSpec((tm, D_MODEL), lambda i: (i, 0))
    vec = pl.BlockSpec((1, D_MODEL), lambda i: (0, 0))
    wspec = pl.BlockSpec((IN_PAD, D_MODEL), lambda i: (0, 0))
    return pl.pallas_call(
        body, name="inproj_bwd", grid=(n_i,),
        in_specs=[piece] * N_PIECE + [pl.BlockSpec((tm, 128), lambda i: (i, 0)), rows, rows, vec, wspec],
        out_specs=[rows, wspec, vec],
        out_shape=[jax.ShapeDtypeStruct((n, D_MODEL), F32),
                   jax.ShapeDtypeStruct((IN_PAD, D_MODEL), BF16),
                   jax.ShapeDtypeStruct((1, D_MODEL), F32)],
        scratch_shapes=[pltpu.VMEM((IN_PAD, D_MODEL), F32)],
        compiler_params=_params(("arbitrary",)),
    )(*dpieces, dfg, dh_out, h_in, gain, w_in)


def _outproj_fwd(zc, za, w_out, h):
    n = h.shape[0]
    tm = _row_tile(n, 512)

    def body(zc_ref, za_ref, w_ref, h_ref, out_ref):
        out_ref[...] = (h_ref[...] + _dot(zc_ref[...], w_ref[0:CONV_DIM, :])
                        + _dot(za_ref[...], w_ref[CONV_DIM:, :]))

    half = pl.BlockSpec((tm, 512), lambda i: (i, 0))
    rows = pl.BlockSpec((tm, D_MODEL), lambda i: (i, 0))
    return pl.pallas_call(
        body, name="outproj_fwd", grid=(n // tm,),
        in_specs=[half, half, pl.BlockSpec((D_MODEL, D_MODEL), lambda i: (0, 0)), rows],
        out_specs=rows,
        out_shape=jax.ShapeDtypeStruct((n, D_MODEL), F32),
        compiler_params=_params(("parallel",)),
    )(zc, za, w_out, h)


def _outproj_bwd(dh, zc, za, w_out):
    n = dh.shape[0]
    tm = _row_tile(n, 512)
    n_i = n // tm

    def body(dh_ref, zc_ref, za_ref, w_ref, dzc_ref, dza_ref, dw_ref, acc_scr):
        i = pl.program_id(0)

        @pl.when(i == 0)
        def _():
            acc_scr[...] = jnp.zeros_like(acc_scr)

        dhb = dh_ref[...].astype(BF16)
        dzc_ref[...] = _dot_nt(dhb, w_ref[0:CONV_DIM, :]).astype(BF16)
        dza_ref[...] = _dot_nt(dhb, w_ref[CONV_DIM:, :]).astype(BF16)
        acc_scr[0:CONV_DIM, :] += _dot_tn(zc_ref[...], dhb)
        acc_scr[CONV_DIM:, :] += _dot_tn(za_ref[...], dhb)

        @pl.when(i == n_i - 1)
        def _():
            dw_ref[...] = acc_scr[...].astype(BF16)

    half = pl.BlockSpec((tm, 512), lambda i: (i, 0))
    wspec = pl.BlockSpec((D_MODEL, D_MODEL), lambda i: (0, 0))
    return pl.pallas_call(
        body, name="outproj_bwd", grid=(n_i,),
        in_specs=[pl.BlockSpec((tm, D_MODEL), lambda i: (i, 0)), half, half, wspec],
        out_specs=[half, half, wspec],
        out_shape=[jax.ShapeDtypeStruct((n, 512), BF16), jax.ShapeDtypeStruct((n, 512), BF16),
                   jax.ShapeDtypeStruct((D_MODEL, D_MODEL), BF16)],
        scratch_shapes=[pltpu.VMEM((D_MODEL, D_MODEL), F32)],
        compiler_params=_params(("arbitrary",)),
    )(dh, zc, za, w_out)


def _group_matrix():
    r = lax.broadcasted_iota(jnp.int32, (128, 128), 0) // HEAD_DIM
    c = lax.broadcasted_iota(jnp.int32, (128, 128), 1) // HEAD_DIM
    return jnp.where(r == c, 1.0 / HEAD_DIM, 0.0).astype(BF16)


def _group_mean(x, gmat):
    hi = x.astype(BF16)
    lo = (x - hi.astype(F32)).astype(BF16)
    return _dot(hi, gmat) + _dot(lo, gmat)


def _shift_rows(x, s):
    rows = x.shape[0]
    t = lax.broadcasted_iota(jnp.int32, x.shape, 0)
    rolled = pltpu.roll(x, s % rows, 0)
    keep = (t >= s) if s > 0 else (t < rows + s)
    return jnp.where(keep, rolled, 0.0)


def _conv_parts(bg_ref, cg_ref, hc_ref, w_ref):
    bg = bg_ref[...].astype(F32)
    cg = cg_ref[...].astype(F32)
    hc = hc_ref[...].astype(F32)
    u = cg * hc
    u1 = _shift_rows(u, 1)
    u2 = _shift_rows(u, 2)
    conv = w_ref[2:3, :] * u + w_ref[1:2, :] * u1 + w_ref[0:1, :] * u2
    return bg, cg, hc, u, u1, u2, conv


def _conv_fwd(bg, cg, hc, conv_w, gain, gmat, lp):
    n = bg.shape[0]
    nb = n // lp

    def body(bg_ref, cg_ref, hc_ref, w_ref, g_ref, gm_ref, z_ref):
        bgv, _, _, _, _, _, conv = _conv_parts(bg_ref, cg_ref, hc_ref, w_ref)
        yc = bgv * conv
        r = lax.rsqrt(_group_mean(yc * yc, gm_ref[...]) + EPS)
        z_ref[...] = (yc * r * g_ref[...]).astype(BF16)

    blk = pl.BlockSpec((lp, 128), lambda c, b: (b, c))
    return pl.pallas_call(
        body, name="conv_fwd", grid=(CONV_DIM // 128, nb),
        in_specs=[blk, blk, blk, pl.BlockSpec((3, 128), lambda c, b: (0, c)),
                  pl.BlockSpec((1, 128), lambda c, b: (0, c)), pl.BlockSpec((128, 128), lambda c, b: (0, 0))],
        out_specs=blk,
        out_shape=jax.ShapeDtypeStruct((n, CONV_DIM), BF16),
        compiler_params=_params(("parallel", "parallel")),
    )(bg, cg, hc, conv_w, gain, gmat)


def _conv_bwd(dz, bg, cg, hc, conv_w, gain, gmat, lp):
    n = bg.shape[0]
    nb = n // lp

    def body(dz_ref, bg_ref, cg_ref, hc_ref, w_ref, g_ref, gm_ref,
             dbg_ref, dcg_ref, dhc_ref, dw_ref, dgain_ref):
        b = pl.program_id(1)

        @pl.when(b == 0)
        def _():
            dw_ref[...] = jnp.zeros_like(dw_ref)
            dgain_ref[...] = jnp.zeros_like(dgain_ref)

        bgv, cgv, hcv, u, u1, u2, conv = _conv_parts(bg_ref, cg_ref, hc_ref, w_ref)
        gm = gm_ref[...]
        yc = bgv * conv
        r = lax.rsqrt(_group_mean(yc * yc, gm) + EPS)
        yhat = yc * r
        dzv = dz_ref[...].astype(F32)
        dyhat = dzv * g_ref[...]
        dgain_ref[...] += jnp.sum(dzv * yhat, axis=0, keepdims=True)
        dyc = r * (dyhat - yhat * _group_mean(dyhat * yhat, gm))
        dbg_ref[...] = (dyc * conv).astype(BF16)
        dconv = dyc * bgv
        du = (w_ref[2:3, :] * dconv + w_ref[1:2, :] * _shift_rows(dconv, -1)
              + w_ref[0:1, :] * _shift_rows(dconv, -2))
        dcg_ref[...] = (du * hcv).astype(BF16)
        dhc_ref[...] = (du * cgv).astype(BF16)
        dw_ref[0:1, :] += jnp.sum(dconv * u2, axis=0, keepdims=True)
        dw_ref[1:2, :] += jnp.sum(dconv * u1, axis=0, keepdims=True)
        dw_ref[2:3, :] += jnp.sum(dconv * u, axis=0, keepdims=True)

    blk = pl.BlockSpec((lp, 128), lambda c, b: (b, c))
    wspec = pl.BlockSpec((3, 128), lambda c, b: (0, c))
    gspec = pl.BlockSpec((1, 128), lambda c, b: (0, c))
    return pl.pallas_call(
        body, name="conv_bwd", grid=(CONV_DIM // 128, nb),
        in_specs=[blk, blk, blk, blk, wspec, gspec, pl.BlockSpec((128, 128), lambda c, b: (0, 0))],
        out_specs=[blk, blk, blk, wspec, gspec],
        out_shape=[jax.ShapeDtypeStruct((n, CONV_DIM), BF16)] * 3
        + [jax.ShapeDtypeStruct((3, CONV_DIM), F32), jax.ShapeDtypeStruct((1, CONV_DIM), F32)],
        compiler_params=_params(("parallel", "arbitrary")),
    )(dz, bg, cg, hc, conv_w, gain, gmat)


KEY_MASKED = 1e30
ONE_LANE = 24


def _scan_steps(rows):
    s, out = 1, []
    while s < rows:
        out.append(s)
        s *= 2
    return out


def _fgate_fwd(fg, b_f, lp):
    n = fg.shape[0]
    nb = n // lp

    def body(fg_ref, b_ref, ka_ref, qa_ref):
        x = fg_ref[...] + b_ref[...]
        logf = jnp.minimum(x, 0.0) - jnp.log(1.0 + jnp.exp(-jnp.abs(x)))
        t = lax.broadcasted_iota(jnp.int32, (lp, 128), 0)
        lane = lax.broadcasted_iota(jnp.int32, (lp, 128), 1)
        f = jnp.where((t >= PAD) & (lane < N_HEADS), logf, 0.0)
        for s in _scan_steps(lp):
            f = f + _shift_rows(f, s)
        hi = f.astype(BF16).astype(F32)
        rest = f - hi
        mid = rest.astype(BF16).astype(F32)
        lo = (rest - mid).astype(BF16).astype(F32)
        ones = jnp.where((lane >= ONE_LANE) & (lane < ONE_LANE + 3), 1.0, 0.0)
        hi_key = jnp.where((t < PAD) & (lane < N_HEADS), KEY_MASKED, hi)
        ka_ref[...] = (hi_key + pltpu.roll(mid, 8, 1) + pltpu.roll(lo, 16, 1) + ones).astype(BF16)
        for h in range(N_HEADS):
            minus = jnp.where((lane == h) | (lane == 8 + h) | (lane == 16 + h), -1.0, 0.0)
            terms = (jnp.where(lane == ONE_LANE, pltpu.roll(hi, ONE_LANE - h, 1), 0.0)
                     + jnp.where(lane == ONE_LANE + 1, pltpu.roll(mid, ONE_LANE + 1 - h, 1), 0.0)
                     + jnp.where(lane == ONE_LANE + 2, pltpu.roll(lo, ONE_LANE + 2 - h, 1), 0.0))
            qa_ref[:, 128 * h:128 * (h + 1)] = (minus + terms).astype(BF16)

    return pl.pallas_call(
        body, name="fgate_fwd", grid=(nb,),
        in_specs=[pl.BlockSpec((lp, 128), lambda b: (b, 0)), pl.BlockSpec((1, 128), lambda b: (0, 0))],
        out_specs=[pl.BlockSpec((lp, 128), lambda b: (b, 0)), pl.BlockSpec((lp, N_HEADS * 128), lambda b: (b, 0))],
        out_shape=[jax.ShapeDtypeStruct((n, 128), BF16), jax.ShapeDtypeStruct((n, N_HEADS * 128), BF16)],
        compiler_params=_params(("parallel",)),
    )(fg, b_f)


def _fgate_bwd(dka, dfr, fg, b_f, lp):
    n = fg.shape[0]
    nb = n // lp

    def body(dka_ref, dfr_ref, fg_ref, b_ref, dfg_ref, db_ref):
        b = pl.program_id(0)

        @pl.when(b == 0)
        def _():
            db_ref[...] = jnp.zeros_like(db_ref)

        wide = jnp.concatenate([dfr_ref[0], jnp.zeros((128 - N_HEADS, lp), F32)], axis=0)
        t = lax.broadcasted_iota(jnp.int32, (lp, 128), 0)
        lane = lax.broadcasted_iota(jnp.int32, (lp, 128), 1)
        d = jnp.where(lane < N_HEADS, dka_ref[...], 0.0) + wide.T
        for s in _scan_steps(lp):
            d = d + _shift_rows(d, -s)
        x = fg_ref[...] + b_ref[...]
        dx = jnp.where((t >= PAD) & (lane < N_HEADS), d * _sigmoid(-x), 0.0)
        dfg_ref[...] = dx
        db_ref[...] += jnp.sum(dx, axis=0, keepdims=True)

    return pl.pallas_call(
        body, name="fgate_bwd", grid=(nb,),
        in_specs=[pl.BlockSpec((lp, 128), lambda b: (b, 0)), pl.BlockSpec((1, N_HEADS, lp), lambda b: (b, 0, 0)),
                  pl.BlockSpec((lp, 128), lambda b: (b, 0)), pl.BlockSpec((1, 128), lambda b: (0, 0))],
        out_specs=[pl.BlockSpec((lp, 128), lambda b: (b, 0)), pl.BlockSpec((1, 128), lambda b: (0, 0))],
        out_shape=[jax.ShapeDtypeStruct((n, 128), F32), jax.ShapeDtypeStruct((1, 128), F32)],
        compiler_params=_params(("arbitrary",)),
    )(dka, dfr, fg, b_f)


def _head_masks():
    lane = lax.broadcasted_iota(jnp.int32, (1, 128), 1)
    return lane < HEAD_DIM


def _stack_heads(x2, first):
    zero = jnp.zeros_like(x2)
    return jnp.concatenate([jnp.where(first, x2, zero), jnp.where(first, zero, x2)], axis=0)


def _stack_heads_lanes(xt):
    r = lax.broadcasted_iota(jnp.int32, xt.shape, 0)
    zero = jnp.zeros_like(xt)
    return jnp.concatenate([jnp.where(r < HEAD_DIM, xt, zero), jnp.where(r < HEAD_DIM, zero, xt)], axis=1)


def _pair_cols(col0, col1, first):
    return jnp.where(first, col0, col1)


def _pair_rows(row0, row1):
    r = lax.broadcasted_iota(jnp.int32, (128, TQ), 0)
    return jnp.where(r < HEAD_DIM, row0, row1)


def _query_side(q_ref, qa_ref, p, first):
    q2 = q_ref[:, 128 * p:128 * (p + 1)] * 0.125
    zero = jnp.zeros_like(q2)
    top = jnp.concatenate([jnp.where(first, q2, zero), qa_ref[:, 128 * (2 * p):128 * (2 * p + 1)]], axis=1)
    bot = jnp.concatenate([jnp.where(first, zero, q2), qa_ref[:, 128 * (2 * p + 1):128 * (2 * p + 2)]], axis=1)
    return jnp.concatenate([top, bot], axis=0)


def _key_chunks(lp):
    return (lp + TK - 1) // TK


def _chunk_mask(i, c):
    r = lax.broadcasted_iota(jnp.int32, (TK, 2 * TQ), 0)
    col = lax.broadcasted_iota(jnp.int32, (TK, 2 * TQ), 1)
    return (c * TK + r) <= (i * TQ + (col & (TQ - 1)))


def _transpose_bf16(x):
    return x.astype(F32).T.astype(BF16)


def _attn_fwd(q, qa, k, v, ka, gain, lp):
    n = q.shape[0]
    nb = n // lp
    nq = lp // TQ
    lpp = _key_chunks(lp) * TK

    def body(q_ref, qa_ref, k_ref, v_ref, ka_ref, g_ref, z_ref, o_ref, lse_ref, kx_scr, vt_scr):
        i = pl.program_id(1)
        first = _head_masks()

        @pl.when(i == 0)
        def _():
            if lpp > lp:
                kx_scr[lp:lpp, :] = jnp.zeros((lpp - lp, 2 * ATTN_DIM), BF16)
                vt_scr[:, lp:lpp] = jnp.zeros((ATTN_DIM, lpp - lp), BF16)
            for p in range(N_PAIRS):
                kx_scr[0:lp, 256 * p:256 * p + 128] = k_ref[:, 128 * p:128 * (p + 1)]
                kx_scr[0:lp, 256 * p + 128:256 * (p + 1)] = ka_ref[...]
            vt_scr[:, 0:lp] = _transpose_bf16(v_ref[...])

        rhs_t = [_transpose_bf16(_query_side(q_ref, qa_ref, p, first)) for p in range(N_PAIRS)]

        def step(c, carry):
            koff = pl.multiple_of(c * TK, TK)
            valid = _chunk_mask(i, c)
            new = []
            for p in range(N_PAIRS):
                m, l, acc = carry[p]
                st = _dot(kx_scr[pl.ds(koff, TK), 256 * p:256 * (p + 1)], rhs_t[p])
                st = jnp.where(valid, st, NEG)
                m_new = jnp.maximum(m, jnp.max(st, axis=0, keepdims=True))
                pt = jnp.exp(st - m_new)
                alpha = jnp.exp(m - m_new)
                l = alpha * l + jnp.sum(pt, axis=0, keepdims=True)
                pb = pt.astype(BF16)
                vt = _stack_heads_lanes(vt_scr[128 * p:128 * (p + 1), pl.ds(koff, TK)])
                pv = _dot(vt, jnp.concatenate([pb[:, 0:TQ], pb[:, TQ:]], axis=0))
                acc = acc * _pair_rows(alpha[:, 0:TQ], alpha[:, TQ:]) + pv
                new.append((m_new, l, acc))
            return tuple(new)

        init = tuple((jnp.full((1, 2 * TQ), NEG, F32), jnp.zeros((1, 2 * TQ), F32), jnp.zeros((128, TQ), F32))
                     for _ in range(N_PAIRS))
        final = lax.fori_loop(0, (i + TK // TQ) // (TK // TQ), step, init)

        row = lax.broadcasted_iota(jnp.int32, (TQ, 128), 0)
        real = (i * TQ + row) >= PAD
        for p in range(N_PAIRS):
            m, l, acc = final[p]
            inv = 1.0 / l
            ot = acc * _pair_rows(inv[:, 0:TQ], inv[:, TQ:])
            sq = ot * ot
            r0 = lax.rsqrt(jnp.sum(sq[0:HEAD_DIM], axis=0, keepdims=True) * (1.0 / HEAD_DIM) + EPS)
            r1 = lax.rsqrt(jnp.sum(sq[HEAD_DIM:], axis=0, keepdims=True) * (1.0 / HEAD_DIM) + EPS)
            cols = slice(128 * p, 128 * (p + 1))
            o_ref[:, cols] = jnp.where(real, ot.T, 0.0).astype(BF16)
            z_ref[:, cols] = (jnp.where(real, (ot * _pair_rows(r0, r1)).T, 0.0) * g_ref[:, cols]).astype(BF16)
            lse = m + jnp.log(l)
            lse_ref[0, 2 * p:2 * p + 1, :] = lse[:, 0:TQ]
            lse_ref[0, 2 * p + 1:2 * p + 2, :] = lse[:, TQ:]

    qblk = pl.BlockSpec((TQ, ATTN_DIM), lambda b, i: (b * nq + i, 0))
    qablk = pl.BlockSpec((TQ, N_HEADS * 128), lambda b, i: (b * nq + i, 0))
    seq = pl.BlockSpec((lp, ATTN_DIM), lambda b, i: (b, 0))
    rowblk = pl.BlockSpec((1, N_HEADS, TQ), lambda b, i: (b, 0, i))
    return pl.pallas_call(
        body, name="attn_fwd", grid=(nb, nq),
        in_specs=[qblk, qablk, seq, seq, pl.BlockSpec((lp, 128), lambda b, i: (b, 0)),
                  pl.BlockSpec((1, ATTN_DIM), lambda b, i: (0, 0))],
        out_specs=[qblk, qblk, rowblk],
        out_shape=[jax.ShapeDtypeStruct((n, ATTN_DIM), BF16), jax.ShapeDtypeStruct((n, ATTN_DIM), BF16),
                   jax.ShapeDtypeStruct((nb, N_HEADS, lp), F32)],
        scratch_shapes=[pltpu.VMEM((lpp, 2 * ATTN_DIM), BF16), pltpu.VMEM((ATTN_DIM, lpp), BF16)],
        compiler_params=_params(("parallel", "arbitrary")),
    )(q, qa, k, v, ka, gain)


def _attn_bwd(dz, q, qa, k, v, ka, o, lse, gain, lp, exchange=()):
    n = q.shape[0]
    nb = n // lp
    nq = lp // TQ
    lpp = _key_chunks(lp) * TK
    nw = len(exchange)

    def body(*refs):
        ((dz_ref, q_ref, qa_ref, k_ref, v_ref, ka_ref, o_ref, lse_ref, g_ref), xin,
         (dq_ref, dk_ref, dv_ref, dka_ref, dfr_ref, dgain_ref), xout,
         (kx_scr, vx_scr, kt_scr, dkx_scr, dvx_scr), sems) = _split_refs(refs, 9, nw, 6, 5)
        b = pl.program_id(0)
        i = pl.program_id(1)
        first = _head_masks()
        if nw:
            comm = _Exchange(xin, xout, sems)
            pl.when((b == 0) & (i == 0))(comm.start)

        @pl.when((b == 0) & (i == 0))
        def _():
            dgain_ref[...] = jnp.zeros_like(dgain_ref)

        @pl.when(i == 0)
        def _():
            if lpp > lp:
                kx_scr[lp:lpp, :] = jnp.zeros((lpp - lp, 2 * ATTN_DIM), BF16)
                vx_scr[lp:lpp, :] = jnp.zeros((lpp - lp, ATTN_DIM), BF16)
                kt_scr[:, lp:lpp] = jnp.zeros((ATTN_DIM, lpp - lp), BF16)
            for p in range(N_PAIRS):
                kx_scr[0:lp, 256 * p:256 * p + 128] = k_ref[:, 128 * p:128 * (p + 1)]
                kx_scr[0:lp, 256 * p + 128:256 * (p + 1)] = ka_ref[...]
            vx_scr[0:lp, :] = v_ref[...]
            kt_scr[:, 0:lp] = _transpose_bf16(k_ref[...])
            dkx_scr[...] = jnp.zeros_like(dkx_scr)
            dvx_scr[...] = jnp.zeros_like(dvx_scr)

        rhs, rhs_t, lses, dos, dos_t, deltas = [], [], [], [], [], []
        for p in range(N_PAIRS):
            cols = slice(128 * p, 128 * (p + 1))
            side = _query_side(q_ref, qa_ref, p, first)
            rhs.append(side)
            rhs_t.append(_transpose_bf16(side))
            lses.append(jnp.concatenate([lse_ref[0, 2 * p:2 * p + 1, :], lse_ref[0, 2 * p + 1:2 * p + 2, :]], axis=1))
            ov = o_ref[:, cols].astype(F32)
            dzv = dz_ref[:, cols].astype(F32)
            gv = g_ref[:, cols]
            sq = ov * ov
            ms0 = jnp.sum(jnp.where(first, sq, 0.0), axis=1, keepdims=True) * (1.0 / HEAD_DIM)
            ms1 = jnp.sum(jnp.where(first, 0.0, sq), axis=1, keepdims=True) * (1.0 / HEAD_DIM)
            r = _pair_cols(lax.rsqrt(ms0 + EPS), lax.rsqrt(ms1 + EPS), first)
            ohat = ov * r
            dyhat = dzv * gv
            dgain_ref[:, cols] += jnp.sum(dzv * ohat, axis=0, keepdims=True)
            pr = dyhat * ohat
            mean0 = jnp.sum(jnp.where(first, pr, 0.0), axis=1, keepdims=True) * (1.0 / HEAD_DIM)
            mean1 = jnp.sum(jnp.where(first, 0.0, pr), axis=1, keepdims=True) * (1.0 / HEAD_DIM)
            do = r * (dyhat - ohat * _pair_cols(mean0, mean1, first))
            ddt = (do * ov).T
            deltas.append(jnp.concatenate([jnp.sum(ddt[0:HEAD_DIM], axis=0, keepdims=True),
                                           jnp.sum(ddt[HEAD_DIM:], axis=0, keepdims=True)], axis=1))
            do_st = _stack_heads(do.astype(BF16), first)
            dos.append(do_st)
            dos_t.append(_transpose_bf16(do_st))

        def step(c, carry):
            koff = pl.multiple_of(c * TK, TK)
            valid = _chunk_mask(i, c)
            new = []
            for p in range(N_PAIRS):
                dqt, dfq = carry[p]
                ext = slice(256 * p, 256 * (p + 1))
                cols = slice(128 * p, 128 * (p + 1))
                st = _dot(kx_scr[pl.ds(koff, TK), ext], rhs_t[p])
                st = jnp.where(valid, st, NEG)
                pt = jnp.exp(st - lses[p])
                dpt = _dot(vx_scr[pl.ds(koff, TK), cols], dos_t[p])
                dst = pt * (dpt - deltas[p])
                dsb = dst.astype(BF16)
                dfq = dfq + jnp.sum(dsb.astype(F32), axis=0, keepdims=True)
                dkx_scr[pl.ds(koff, TK), ext] += _dot(dsb, rhs[p])
                dvx_scr[pl.ds(koff, TK), cols] += _dot(pt.astype(BF16), dos[p])
                kt = _stack_heads_lanes(kt_scr[cols, pl.ds(koff, TK)])
                dqt = dqt + _dot(kt, jnp.concatenate([dsb[:, 0:TQ], dsb[:, TQ:]], axis=0))
                new.append((dqt, dfq))
            return tuple(new)

        init = tuple((jnp.zeros((128, TQ), F32), jnp.zeros((1, 2 * TQ), F32)) for _ in range(N_PAIRS))
        final = lax.fori_loop(0, (i + TK // TQ) // (TK // TQ), step, init)

        for p in range(N_PAIRS):
            dqt, dfq = final[p]
            dq_ref[:, 128 * p:128 * (p + 1)] = (dqt.T * 0.125).astype(BF16)
            dfr_ref[0, 2 * p:2 * p + 1, :] = dfq[:, 0:TQ]
            dfr_ref[0, 2 * p + 1:2 * p + 2, :] = dfq[:, TQ:]

        @pl.when(i == nq - 1)
        def _():
            dka = jnp.zeros((lp, 128), F32)
            for p in range(N_PAIRS):
                dk_ref[:, 128 * p:128 * (p + 1)] = dkx_scr[0:lp, 256 * p:256 * p + 128].astype(BF16)
                dka = dka + dkx_scr[0:lp, 256 * p + 128:256 * (p + 1)]
            dka_ref[...] = dka
            dv_ref[...] = dvx_scr[0:lp, :].astype(BF16)

        if nw:
            pl.when((b == nb - 1) & (i == nq - 1))(comm.finish)

    qblk = pl.BlockSpec((TQ, ATTN_DIM), lambda b, i: (b * nq + i, 0))
    qablk = pl.BlockSpec((TQ, N_HEADS * 128), lambda b, i: (b * nq + i, 0))
    seq = pl.BlockSpec((lp, ATTN_DIM), lambda b, i: (b, 0))
    kaseq = pl.BlockSpec((lp, 128), lambda b, i: (b, 0))
    rowblk = pl.BlockSpec((1, N_HEADS, TQ), lambda b, i: (b, 0, i))
    gspec = pl.BlockSpec((1, ATTN_DIM), lambda b, i: (0, 0))
    return pl.pallas_call(
        body, name="attn_bwd", grid=(nb, nq),
        in_specs=[qblk, qblk, qablk, seq, seq, kaseq, qblk, rowblk, gspec] + [ANY] * nw,
        out_specs=[qblk, seq, seq, kaseq, rowblk, gspec] + [ANY] * nw,
        out_shape=[jax.ShapeDtypeStruct((n, ATTN_DIM), BF16), jax.ShapeDtypeStruct((n, ATTN_DIM), BF16),
                   jax.ShapeDtypeStruct((n, ATTN_DIM), BF16), jax.ShapeDtypeStruct((n, 128), F32),
                   jax.ShapeDtypeStruct((nb, N_HEADS, lp), F32), jax.ShapeDtypeStruct((1, ATTN_DIM), F32)]
        + [jax.ShapeDtypeStruct(a.shape, a.dtype) for a in exchange],
        scratch_shapes=[pltpu.VMEM((lpp, 2 * ATTN_DIM), BF16), pltpu.VMEM((lpp, ATTN_DIM), BF16),
                        pltpu.VMEM((ATTN_DIM, lpp), BF16), pltpu.VMEM((lpp, 2 * ATTN_DIM), F32),
                        pltpu.VMEM((lpp, ATTN_DIM), F32)] + (_comm_sems(nw) if nw else []),
        compiler_params=_params(("arbitrary", "arbitrary")),
    )(dz, q, qa, k, v, ka, o, lse, gain, *exchange)


def _loss_head(h, gain, target, lp):
    n = h.shape[0]
    nb = n // lp
    nq = lp // 128

    def body(h_ref, g_ref, t_ref, loss_ref, dh_ref, dgain_ref):
        b = pl.program_id(0)
        i = pl.program_id(1)

        @pl.when((b == 0) & (i == 0))
        def _():
            loss_ref[...] = jnp.zeros_like(loss_ref)
            dgain_ref[...] = jnp.zeros_like(dgain_ref)

        @pl.when(i == 0)
        def _():
            dh_ref[...] = jnp.zeros_like(dh_ref)

        @pl.when(i > 0)
        def _():
            gain_v = g_ref[...]
            y, xhat, r = _rms(h_ref[...], gain_v)
            err = y - t_ref[...]
            loss_ref[...] += 0.5 * jnp.sum(jnp.sum(err * err, axis=1, keepdims=True), axis=0,
                                           keepdims=True) * (1.0 / D_MODEL)
            dy = err * (1.0 / D_MODEL)
            dh_ref[...] = _rms_bwd(dy, xhat, r, gain_v)
            dgain_ref[...] += jnp.sum(dy * xhat, axis=0, keepdims=True)

    rows = pl.BlockSpec((128, D_MODEL), lambda b, i: (b * nq + i, 0))
    trows = pl.BlockSpec((128, D_MODEL), lambda b, i: (b * (nq - 1) + jnp.maximum(i, 1) - 1, 0))
    return pl.pallas_call(
        body, name="loss_head", grid=(nb, nq),
        in_specs=[rows, pl.BlockSpec((1, D_MODEL), lambda b, i: (0, 0)), trows],
        out_specs=[pl.BlockSpec((1, 1), lambda b, i: (0, 0)), rows, pl.BlockSpec((1, D_MODEL), lambda b, i: (0, 0))],
        out_shape=[jax.ShapeDtypeStruct((1, 1), F32), jax.ShapeDtypeStruct((n, D_MODEL), F32),
                   jax.ShapeDtypeStruct((1, D_MODEL), F32)],
        compiler_params=_params(("arbitrary", "arbitrary")),
    )(h, gain, target)


def _adamw(parts, w, m, v, name):
    s_parts, r, c = parts.shape
    tr = r
    for t in (256, 128, 64, 32, 16):
        if r % t == 0 and r > t:
            tr = t
            break

    def body(p_ref, w_ref, m_ref, v_ref, g_ref, d_ref, nm_ref, nv_ref):
        g = p_ref[0].astype(F32)
        for s in range(1, s_parts):
            g = g + p_ref[s].astype(F32)
        nm = ADAM_B1 * m_ref[...] + (1.0 - ADAM_B1) * g
        nv = ADAM_B2 * v_ref[...] + (1.0 - ADAM_B2) * (g * g)
        m_hat = nm / (1.0 - ADAM_B1 ** ADAM_STEP)
        v_hat = nv / (1.0 - ADAM_B2 ** ADAM_STEP)
        g_ref[...] = g
        d_ref[...] = -ADAM_LR * (m_hat / (jnp.sqrt(v_hat) + ADAM_EPS) + ADAM_WD * w_ref[...])
        nm_ref[...] = nm
        nv_ref[...] = nv

    blk = pl.BlockSpec((tr, c), lambda i: (i, 0))
    return pl.pallas_call(
        body, name=name, grid=(r // tr,),
        in_specs=[pl.BlockSpec((s_parts, tr, c), lambda i: (0, i, 0)), blk, blk, blk],
        out_specs=[blk] * 4,
        out_shape=[jax.ShapeDtypeStruct((r, c), F32)] * 4,
        compiler_params=_params(("parallel",)),
    )(parts, w, m, v)


def _sum_parts(parts, name):
    s_parts, r, c = parts.shape

    def body(p_ref, out_ref):
        acc = p_ref[0]
        for s in range(1, s_parts):
            acc = acc + p_ref[s]
        out_ref[...] = acc

    return pl.pallas_call(
        body, name=name, out_shape=jax.ShapeDtypeStruct((r, c), F32),
        in_specs=[pl.BlockSpec(memory_space=pltpu.VMEM)], out_specs=pl.BlockSpec(memory_space=pltpu.VMEM),
    )(parts)


SMALL_ROWS = 184


def _pack_small(d_gains, d_gc, d_ga, d_bf, d_conv, d_meta):
    rows = [g.reshape(8, 128) for g in d_gains]
    rows += [d_gc.reshape(4, 128), d_ga.reshape(4, 128), d_bf.reshape(1, 128)]
    rows += [d_conv.reshape(12, 128), d_meta.reshape(128, 128)]
    packed = jnp.concatenate(rows, axis=0)
    return jnp.pad(packed, ((0, SMALL_ROWS - packed.shape[0]), (0, 0)))


def kernel(x, meta_tokens, ffn1_norm, ffn1_w_gu, ffn1_w_down, mix_norm, w_in, conv_w, b_f, out_norm_conv, out_norm_attn, w_out, ffn2_norm, ffn2_w_gu, ffn2_w_down, final_norm, loss_target, m_meta_tokens, m_ffn1_norm, m_ffn1_w_gu, m_ffn1_w_down, m_mix_norm, m_w_in, m_conv_w, m_b_f, m_out_norm_conv, m_out_norm_attn, m_w_out, m_ffn2_norm, m_ffn2_w_gu, m_ffn2_w_down, m_final_norm, v_meta_tokens, v_ffn1_norm, v_ffn1_w_gu, v_ffn1_w_down, v_mix_norm, v_w_in, v_conv_w, v_b_f, v_out_norm_conv, v_out_norm_attn, v_w_out, v_ffn2_norm, v_ffn2_w_gu, v_ffn2_w_down, v_final_norm):
    nb, seq, _ = x.shape
    lp = PAD + N_META + seq
    n = nb * lp
    me = 4 * lax.axis_index("x") + 2 * lax.axis_index("y") + lax.axis_index("c")

    wgu1_8, wd1_8 = _all_gather([ffn1_w_gu[0].T.astype(BF16), ffn1_w_down[0].astype(BF16)], "gather_ffn1")
    small_in = jnp.concatenate(
        [meta_tokens, jnp.pad(conv_w[0], ((0, 0), (0, 128 - conv_w.shape[2]))), jnp.zeros((5, 128), F32)], axis=0)
    (small_8,) = _all_gather([small_in], "gather_small")
    meta_full = small_8[:, 0:N_META, :].transpose(1, 0, 2).reshape(N_META, D_MODEL)
    conv_full = small_8[:, N_META:N_META + 3, 0:CONV_DIM // N_DEV].transpose(1, 0, 2).reshape(3, CONV_DIM)
    wgu1 = wgu1_8.reshape(W_GU_SHAPE)
    wd1 = wd1_8.reshape(W_D_SHAPE)
    b_f_row = jnp.pad(b_f, ((0, 0), (0, 128 - N_HEADS)))
    gmat = _group_matrix()

    h0 = jnp.concatenate([jnp.zeros((nb, PAD, D_MODEL), F32),
                          jnp.broadcast_to(meta_full[None], (nb, N_META, D_MODEL)), x], axis=1).reshape(n, D_MODEL)
    later = [w_in[0].T.astype(BF16), w_out[0].astype(BF16), ffn2_w_gu[0].T.astype(BF16), ffn2_w_down[0].astype(BF16)]
    h1, n1, gate1, up1, win_8, wout_8, wgu2_8, wd2_8 = _ffn_fwd(h0, ffn1_norm, wgu1, wd1, "ffn1_fwd", gather=later)
    wgu2 = wgu2_8.reshape(W_GU_SHAPE)
    wd2 = wd2_8.reshape(W_D_SHAPE)
    w_in_full = jnp.pad(win_8.reshape(IN_DIM, D_MODEL), ((0, IN_PAD - IN_DIM), (0, 0)))
    w_out_full = wout_8.reshape(D_MODEL, D_MODEL)

    bg, cg, hc, q, k, v, fg = _inproj_fwd(h1, mix_norm, w_in_full)
    zc = _conv_fwd(bg, cg, hc, conv_full, out_norm_conv, gmat, lp)
    ka, qa = _fgate_fwd(fg, b_f_row, lp)
    za, o, lse = _attn_fwd(q, qa, k, v, ka, out_norm_attn, lp)
    h2 = _outproj_fwd(zc, za, w_out_full, h1)
    h3, n3, gate2, up2 = _ffn_fwd(h2, ffn2_norm, wgu2, wd2, "ffn2_fwd")
    loss_part, dh3, d_final = _loss_head(h3, final_norm.reshape(1, D_MODEL), loss_target.reshape(nb * seq, D_MODEL), lp)

    dh2, dhb3, dgate2, dup2, d_ffn2 = _ffn_bwd_x(dh3, h2, ffn2_norm, gate2, up2, wgu2, wd2, "ffn2_bwd_x")
    dwgu2 = _ffn_bwd_wgu(n3, dgate2, dup2, "ffn2_bwd_wgu")
    dwd2 = _ffn_bwd_wd(dhb3, gate2, up2, "ffn2_bwd_wd")
    dzc, dza, dwout = _outproj_bwd(dh2, zc, za, w_out_full)
    send_a = [dwgu2.reshape(N_DEV, F_CHUNK, D_MODEL), dwd2.reshape(N_DEV, F_CHUNK // 2, D_MODEL),
              dwout.reshape(N_DEV, D_MODEL // N_DEV, D_MODEL)]
    dq, dk, dv, dka, dfr, d_ga, p_wgu2, p_wd2, p_wout = _attn_bwd(
        dza, q, qa, k, v, ka, o, lse, out_norm_attn, lp, exchange=send_a)
    dfg, d_bf = _fgate_bwd(dka, dfr, fg, b_f_row, lp)
    dbg, dcg, dhc, d_conv, d_gc = _conv_bwd(dzc, bg, cg, hc, conv_full, out_norm_conv, gmat, lp)
    dh1, dwin, d_mix = _inproj_bwd([dbg, dcg, dhc, dq, dk, dv], dfg, dh2, h1, mix_norm, w_in_full)
    dwin_8 = dwin[0:IN_DIM].reshape(N_DEV, IN_DIM // N_DEV, D_MODEL)
    dhb1, dgate1, dup1, p_win = _ffn_bwd_act(dh1, gate1, up1, wd1, "ffn1_bwd_act", exchange=[dwin_8])
    dwgu1 = _ffn_bwd_wgu(n1, dgate1, dup1, "ffn1_bwd_wgu")
    dwd1 = _ffn_bwd_wd(dhb1, gate1, up1, "ffn1_bwd_wd")
    own = [dwgu1.reshape(N_DEV, F_CHUNK, D_MODEL), dwd1.reshape(N_DEV, F_CHUNK // 2, D_MODEL)]
    got = _pair_exchange(own, "pair_exchange_ffn1")
    chip_sums = [_pair_sum(own[0], got[0], "pair_sum_wgu1"), _pair_sum(own[1], got[1], "pair_sum_wd1")]
    dh0, d_ffn1, p_wgu1, p_wd1 = _ffn_bwd_in(
        dh1, h0, ffn1_norm, dgate1, dup1, wgu1, "ffn1_bwd_in", exchange=chip_sums)

    dh0 = dh0.reshape(nb, lp, D_MODEL)
    grad_x = dh0[:, PAD + N_META:, :]
    d_meta = jnp.sum(dh0[:, PAD:PAD + N_META, :], axis=0)

    small = _pack_small([d_ffn1, d_mix, d_ffn2, d_final], d_gc, d_ga, d_bf, d_conv, d_meta)
    (small_all,) = _all_gather([small], "gather_small_grads")
    small_sum = _sum_parts(small_all, "sum_small_grads")
    g_ffn1n, g_mixn, g_ffn2n, g_finaln = (small_sum[8 * t:8 * t + 8].reshape(1, D_MODEL) for t in range(4))
    g_gc = small_sum[32:36].reshape(1, CONV_DIM)
    g_ga = small_sum[36:40].reshape(1, ATTN_DIM)
    g_bf = small_sum[40:41, 0:N_HEADS]
    g_conv_full = small_sum[41:53].reshape(3, CONV_DIM)
    g_meta_full = small_sum[53:181].reshape(N_META, D_MODEL)
    g_conv = lax.dynamic_slice_in_dim(g_conv_full, me * (CONV_DIM // N_DEV), CONV_DIM // N_DEV, axis=1)
    g_meta = lax.dynamic_slice_in_dim(g_meta_full, me * (D_MODEL // N_DEV), D_MODEL // N_DEV, axis=1)

    weights = {
        "meta_tokens": (g_meta[None], meta_tokens, m_meta_tokens, v_meta_tokens),
        "ffn1_norm": (g_ffn1n[None], ffn1_norm, m_ffn1_norm, v_ffn1_norm),
        "ffn1_w_gu": (p_wgu1, ffn1_w_gu[0].T, m_ffn1_w_gu[0].T, v_ffn1_w_gu[0].T),
        "ffn1_w_down": (p_wd1, ffn1_w_down[0], m_ffn1_w_down[0], v_ffn1_w_down[0]),
        "mix_norm": (g_mixn[None], mix_norm, m_mix_norm, v_mix_norm),
        "w_in": (p_win, w_in[0].T, m_w_in[0].T, v_w_in[0].T),
        "conv_w": (g_conv[None], conv_w[0], m_conv_w[0], v_conv_w[0]),
        "b_f": (g_bf[None], b_f, m_b_f, v_b_f),
        "out_norm_conv": (g_gc[None], out_norm_conv, m_out_norm_conv, v_out_norm_conv),
        "out_norm_attn": (g_ga[None], out_norm_attn, m_out_norm_attn, v_out_norm_attn),
        "w_out": (p_wout, w_out[0], m_w_out[0], v_w_out[0]),
        "ffn2_norm": (g_ffn2n[None], ffn2_norm, m_ffn2_norm, v_ffn2_norm),
        "ffn2_w_gu": (p_wgu2, ffn2_w_gu[0].T, m_ffn2_w_gu[0].T, v_ffn2_w_gu[0].T),
        "ffn2_w_down": (p_wd2, ffn2_w_down[0], m_ffn2_w_down[0], v_ffn2_w_down[0]),
        "final_norm": (g_finaln[None], final_norm.reshape(1, D_MODEL), m_final_norm.reshape(1, D_MODEL),
                       v_final_norm.reshape(1, D_MODEL)),
    }
    shapes = {"meta_tokens": meta_tokens.shape, "ffn1_norm": ffn1_norm.shape, "ffn1_w_gu": ffn1_w_gu.shape,
              "ffn1_w_down": ffn1_w_down.shape, "mix_norm": mix_norm.shape, "w_in": w_in.shape,
              "conv_w": conv_w.shape, "b_f": b_f.shape, "out_norm_conv": out_norm_conv.shape,
              "out_norm_attn": out_norm_attn.shape, "w_out": w_out.shape, "ffn2_norm": ffn2_norm.shape,
              "ffn2_w_gu": ffn2_w_gu.shape, "ffn2_w_down": ffn2_w_down.shape, "final_norm": final_norm.shape}
    grads, deltas, new_m, new_v = [], [], [], []
    for name, (p, w, m, vv) in weights.items():
        g, d, nm, nv = _adamw(p, w, m, vv, "adamw_" + name)
        if name in ("ffn1_w_gu", "ffn2_w_gu", "w_in"):
            g, d, nm, nv = g.T, d.T, nm.T, nv.T
        shape = shapes[name]
        grads.append(g.reshape(shape))
        deltas.append(d.reshape(shape))
        new_m.append(nm.reshape(shape))
        new_v.append(nv.reshape(shape))

    loss = lax.psum(loss_part[0, 0], ("x", "y", "c"))
    return (loss, grad_x, *grads, *deltas, *new_m, *new_v)
```

```python
import jax
import jax.numpy as jnp
from jax import lax
from jax.experimental import pallas as pl
from jax.experimental.pallas import tpu as pltpu

F32 = jnp.float32
BF16 = jnp.bfloat16

N_DEV = 8
D_MODEL = 1024
N_META = 16
PAD = 128 - N_META
CONV_DIM = 512
ATTN_DIM = 512
HEAD_DIM = 64
N_HEADS = 8
N_PAIRS = N_HEADS // 2
D_FF = 2816
N_CHUNK = 4
F_CHUNK = D_FF // N_CHUNK
IN_DIM = 3080
IN_PAD = 3200
IN_MAIN = 3072
EPS = 1e-6
NEG = -1e30
TQ = 128
TK = 512
VMEM_LIMIT = 56 * 1024 * 1024

ADAM_LR = 0.001
ADAM_B1 = 0.9
ADAM_B2 = 0.999
ADAM_EPS = 1e-08
ADAM_WD = 0.01
ADAM_STEP = 10

MESH = pl.DeviceIdType.MESH
ANY = pl.BlockSpec(memory_space=pl.ANY)


def _params(sem=None):
    return pltpu.CompilerParams(dimension_semantics=sem, vmem_limit_bytes=VMEM_LIMIT)


def _row_tile(n, prefer):
    for t in (prefer, 512, 256, 128):
        if t <= n and n % t == 0:
            return t
    raise ValueError(f"no row tile for {n}")


def _dot(a, b):
    return jnp.dot(a, b, preferred_element_type=F32)


def _dot_nt(a, b):
    return lax.dot_general(a, b, (((1,), (1,)), ((), ())), preferred_element_type=F32)


def _dot_tn(a, b):
    return lax.dot_general(a, b, (((0,), (0,)), ((), ())), preferred_element_type=F32)


def _rms(x, g):
    r = lax.rsqrt(jnp.mean(x * x, axis=-1, keepdims=True) + EPS)
    xhat = x * r
    return xhat * g, xhat, r


def _rms_bwd(dn, xhat, r, g):
    dxhat = dn * g
    return r * (dxhat - xhat * jnp.mean(dxhat * xhat, axis=-1, keepdims=True))


def _sigmoid(x):
    return 1.0 / (1.0 + jnp.exp(-x))


def _place():
    return lax.axis_index("x"), lax.axis_index("y"), lax.axis_index("c")


def _comm_sems(nw):
    return [pltpu.SemaphoreType.DMA((nw, 7)), pltpu.SemaphoreType.DMA((nw, 7)), pltpu.SemaphoreType.DMA((nw,))]


class _Gather:
    def __init__(self, ins, outs, sems):
        self.ins, self.outs = ins, outs
        self.send, self.recv, self.local = sems
        x, y, c = _place()
        self.c = c
        self.me, self.sibling = (x, y, c), (x, y, 1 - c)
        self.chips = [(1 - x, y), (x, 1 - y), (1 - x, 1 - y)]

    def _copy(self, w, k, block, to, own=False):
        slot = self.outs[w].at[4 * block[0] + 2 * block[1] + block[2]]
        return pltpu.make_async_remote_copy(
            src_ref=self.ins[w] if own else slot, dst_ref=slot,
            send_sem=self.send.at[w, k], recv_sem=self.recv.at[w, k], device_id=to, device_id_type=MESH)

    def _mine(self, w):
        x, y, c = self.me
        return pltpu.make_async_copy(self.ins[w], self.outs[w].at[4 * x + 2 * y + c], self.local.at[w])

    def _first(self, w):
        return ([self._copy(w, 0, self.me, self.sibling, own=True)]
                + [self._copy(w, 1 + j, self.me, (*chip, self.c), own=True) for j, chip in enumerate(self.chips)])

    def _passed(self, w):
        return [self._copy(w, 4 + j, (*chip, self.c), self.sibling) for j, chip in enumerate(self.chips)]

    def start(self):
        for w in range(len(self.ins)):
            self._mine(w).start()
        for w in range(len(self.ins)):
            for cp in self._first(w):
                cp.start()

    def forward(self):
        for w in range(len(self.ins)):
            for j, chip in enumerate(self.chips):
                self._copy(w, 1 + j, (*chip, self.c), self.me).wait_recv()
                self._passed(w)[j].start()

    def finish(self):
        for w in range(len(self.ins)):
            self._copy(w, 0, self.sibling, self.me).wait_recv()
            for j, chip in enumerate(self.chips):
                self._copy(w, 4 + j, (*chip, 1 - self.c), self.me).wait_recv()
        for w in range(len(self.ins)):
            for cp in self._first(w) + self._passed(w):
                cp.wait_send()
            self._mine(w).wait()


class _Exchange:
    def __init__(self, ins, outs, sems):
        self.ins, self.outs = ins, outs
        self.send, self.recv, self.local = sems
        self.x, self.y, self.c = _place()
        self.me = 4 * self.x + 2 * self.y + self.c

    def _copy(self, w, k):
        flip = lambda v, bit: 1 - v if bit else v
        peer = (flip(self.x, ((k + 1) >> 2) & 1), flip(self.y, ((k + 1) >> 1) & 1), flip(self.c, (k + 1) & 1))
        return pltpu.make_async_remote_copy(
            src_ref=self.ins[w].at[4 * peer[0] + 2 * peer[1] + peer[2]], dst_ref=self.outs[w].at[self.me],
            send_sem=self.send.at[w, k], recv_sem=self.recv.at[w, k], device_id=peer, device_id_type=MESH)

    def _mine(self, w):
        return pltpu.make_async_copy(self.ins[w].at[self.me], self.outs[w].at[self.me], self.local.at[w])

    def start(self):
        for w in range(len(self.ins)):
            self._mine(w).start()
            for k in range(N_DEV - 1):
                self._copy(w, k).start()

    def finish(self):
        for w in range(len(self.ins)):
            for k in range(N_DEV - 1):
                self._copy(w, k).wait()
            self._mine(w).wait()


class _PairExchange:
    def __init__(self, ins, outs, sems):
        self.ins, self.outs = ins, outs
        self.send, self.recv, _ = sems
        x, y, self.c = _place()
        self.sibling = (x, y, 1 - self.c)

    def _copy(self, w, t):
        return pltpu.make_async_remote_copy(
            src_ref=self.ins[w].at[2 * t + 1 - self.c], dst_ref=self.outs[w].at[t],
            send_sem=self.send.at[w, t], recv_sem=self.recv.at[w, t], device_id=self.sibling, device_id_type=MESH)

    def start(self):
        for w in range(len(self.ins)):
            for t in range(4):
                self._copy(w, t).start()

    def finish(self):
        for w in range(len(self.ins)):
            for t in range(4):
                self._copy(w, t).wait()


class _ChipExchange:
    def __init__(self, ins, outs, sems):
        self.ins, self.outs = ins, outs
        self.send, self.recv, self.local = sems
        self.x, self.y, self.c = _place()
        self.chip = 2 * self.x + self.y

    def _copy(self, w, k):
        flip = lambda v, bit: 1 - v if bit else v
        px, py = flip(self.x, ((k + 1) >> 1) & 1), flip(self.y, (k + 1) & 1)
        return pltpu.make_async_remote_copy(
            src_ref=self.ins[w].at[2 * px + py], dst_ref=self.outs[w].at[self.chip],
            send_sem=self.send.at[w, k], recv_sem=self.recv.at[w, k], device_id=(px, py, self.c),
            device_id_type=MESH)

    def _mine(self, w):
        return pltpu.make_async_copy(self.ins[w].at[self.chip], self.outs[w].at[self.chip], self.local.at[w])

    def start(self):
        for w in range(len(self.ins)):
            self._mine(w).start()
            for k in range(3):
                self._copy(w, k).start()

    def finish(self):
        for w in range(len(self.ins)):
            for k in range(3):
                self._copy(w, k).wait()
            self._mine(w).wait()


def _pair_exchange(xs, name):
    nw = len(xs)

    def body(*refs):
        comm = _PairExchange(refs[:nw], refs[nw:2 * nw], refs[2 * nw:])
        comm.start()
        comm.finish()

    return pl.pallas_call(
        body, name=name, in_specs=[ANY] * nw, out_specs=[ANY] * nw,
        out_shape=[jax.ShapeDtypeStruct((4,) + a.shape[1:], a.dtype) for a in xs],
        scratch_shapes=_comm_sems(nw),
    )(*xs)


def _pair_sum(own, got, name):
    _, r, c = own.shape
    tr = r
    for t in (256, 128, 64, 32, 16):
        if r % t == 0 and r > t:
            tr = t
            break

    def body(own_ref, got_ref, out_ref):
        mine = jnp.where(lax.axis_index("c") == 0, own_ref[:, 0].astype(F32), own_ref[:, 1].astype(F32))
        out_ref[...] = (mine + got_ref[...].astype(F32)).astype(BF16)

    return pl.pallas_call(
        body, name=name, grid=(r // tr,),
        in_specs=[pl.BlockSpec((4, 2, tr, c), lambda i: (0, 0, i, 0)), pl.BlockSpec((4, tr, c), lambda i: (0, i, 0))],
        out_specs=pl.BlockSpec((4, tr, c), lambda i: (0, i, 0)),
        out_shape=jax.ShapeDtypeStruct((4, r, c), BF16),
        compiler_params=_params(("parallel",)),
    )(own.reshape(4, 2, r, c), got)


def _split_refs(refs, n_in, n_comm, n_out, n_scr):
    a = n_in
    b = a + n_comm
    c = b + n_out
    d = c + n_comm
    e = d + n_scr
    return refs[:a], refs[a:b], refs[b:c], refs[c:d], refs[d:e], refs[e:]


def _all_gather(xs, name):
    nw = len(xs)

    def body(*refs):
        comm = _Gather(refs[:nw], refs[nw:2 * nw], refs[2 * nw:])
        comm.start()
        comm.forward()
        comm.finish()

    return pl.pallas_call(
        body, name=name, in_specs=[ANY] * nw, out_specs=[ANY] * nw,
        out_shape=[jax.ShapeDtypeStruct((N_DEV,) + a.shape, a.dtype) for a in xs],
        scratch_shapes=_comm_sems(nw),
    )(*xs)


def _ffn_fwd(h, gain, wgu, wd, name, gather=()):
    n = h.shape[0]
    tm = _row_tile(n, 512)
    n_i = n // tm
    nw = len(gather)

    def body(*refs):
        (h_ref, g_ref, wgu_ref, wd_ref), gin, (out_ref, gate_ref, up_ref), gout, (n_scr, acc_scr), sems = \
            _split_refs(refs, 4, nw, 3, 2)
        i = pl.program_id(0)
        j = pl.program_id(1)
        if nw:
            comm = _Gather(gin, gout, sems)
            pl.when((i == 0) & (j == 0))(comm.start)
            pl.when((i == (3 * n_i) // 4) & (j == 0))(comm.forward)

        @pl.when(j == 0)
        def _():
            y, _, _ = _rms(h_ref[...], g_ref[...])
            n_scr[...] = y.astype(BF16)
            acc_scr[...] = jnp.zeros_like(acc_scr)

        nb = n_scr[...]
        gate = _dot(nb, wgu_ref[0, 0])
        up = _dot(nb, wgu_ref[1, 0])
        gate_ref[0] = gate.astype(BF16)
        up_ref[0] = up.astype(BF16)
        act = (gate * _sigmoid(gate) * up).astype(BF16)
        acc_scr[...] += _dot(act, wd_ref[0])

        @pl.when(j == N_CHUNK - 1)
        def _():
            out_ref[...] = h_ref[...] + 0.5 * acc_scr[...]

        if nw:
            pl.when((i == n_i - 1) & (j == N_CHUNK - 1))(comm.finish)

    return pl.pallas_call(
        body, name=name, grid=(n_i, N_CHUNK),
        in_specs=[pl.BlockSpec((tm, D_MODEL), lambda i, j: (i, 0)),
                  pl.BlockSpec((1, D_MODEL), lambda i, j: (0, 0)),
                  pl.BlockSpec((2, 1, D_MODEL, F_CHUNK), lambda i, j: (0, j, 0, 0)),
                  pl.BlockSpec((1, F_CHUNK, D_MODEL), lambda i, j: (j, 0, 0))] + [ANY] * nw,
        out_specs=[pl.BlockSpec((tm, D_MODEL), lambda i, j: (i, 0)),
                   pl.BlockSpec((1, tm, F_CHUNK), lambda i, j: (j, i, 0)),
                   pl.BlockSpec((1, tm, F_CHUNK), lambda i, j: (j, i, 0))] + [ANY] * nw,
        out_shape=[jax.ShapeDtypeStruct((n, D_MODEL), F32),
                   jax.ShapeDtypeStruct((N_CHUNK, n, F_CHUNK), BF16),
                   jax.ShapeDtypeStruct((N_CHUNK, n, F_CHUNK), BF16)]
        + [jax.ShapeDtypeStruct((N_DEV,) + a.shape, a.dtype) for a in gather],
        scratch_shapes=[pltpu.VMEM((tm, D_MODEL), BF16), pltpu.VMEM((tm, D_MODEL), F32)]
        + (_comm_sems(nw) if nw else []),
        compiler_params=_params(("arbitrary", "arbitrary")),
    )(h, gain, wgu, wd, *gather)


def _ffn_bwd_x(dh_out, h_in, gain, gate, up, wgu, wd, name):
    n = h_in.shape[0]
    tm = _row_tile(n, 512)

    def body(dh_ref, h_ref, g_ref, gate_ref, up_ref, wgu_ref, wd_ref,
             dhin_ref, dgate_ref, dup_ref, dgain_ref, dhb_scr, acc_scr):
        i = pl.program_id(0)
        j = pl.program_id(1)

        @pl.when((i == 0) & (j == 0))
        def _():
            dgain_ref[...] = jnp.zeros_like(dgain_ref)

        @pl.when(j == 0)
        def _():
            dhb_scr[...] = (0.5 * dh_ref[...]).astype(BF16)
            acc_scr[...] = jnp.zeros_like(acc_scr)

        da = _dot_nt(dhb_scr[...], wd_ref[0])
        g = gate_ref[0].astype(F32)
        u = up_ref[0].astype(F32)
        sig = _sigmoid(g)
        dgate = (da * u * (sig * (1.0 + g * (1.0 - sig)))).astype(BF16)
        dup = (da * (g * sig)).astype(BF16)
        dgate_ref[0] = dgate
        dup_ref[0] = dup
        acc_scr[...] += _dot_nt(dgate, wgu_ref[0, 0]) + _dot_nt(dup, wgu_ref[1, 0])

        @pl.when(j == N_CHUNK - 1)
        def _():
            gain_v = g_ref[...]
            _, xhat, r = _rms(h_ref[...], gain_v)
            dn = acc_scr[...]
            dhin_ref[...] = dh_ref[...] + _rms_bwd(dn, xhat, r, gain_v)
            dgain_ref[...] += jnp.sum(dn * xhat, axis=0, keepdims=True)

    chunk = pl.BlockSpec((1, tm, F_CHUNK), lambda i, j: (j, i, 0))
    rows = pl.BlockSpec((tm, D_MODEL), lambda i, j: (i, 0))
    vec = pl.BlockSpec((1, D_MODEL), lambda i, j: (0, 0))
    return pl.pallas_call(
        body, name=name, grid=(n // tm, N_CHUNK),
        in_specs=[rows, rows, vec, chunk, chunk,
                  pl.BlockSpec((2, 1, D_MODEL, F_CHUNK), lambda i, j: (0, j, 0, 0)),
                  pl.BlockSpec((1, F_CHUNK, D_MODEL), lambda i, j: (j, 0, 0))],
        out_specs=[rows, chunk, chunk, vec],
        out_shape=[jax.ShapeDtypeStruct((n, D_MODEL), F32),
                   jax.ShapeDtypeStruct((N_CHUNK, n, F_CHUNK), BF16),
                   jax.ShapeDtypeStruct((N_CHUNK, n, F_CHUNK), BF16),
                   jax.ShapeDtypeStruct((1, D_MODEL), F32)],
        scratch_shapes=[pltpu.VMEM((tm, D_MODEL), BF16), pltpu.VMEM((tm, D_MODEL), F32)],
        compiler_params=_params(("arbitrary", "arbitrary")),
    )(dh_out, h_in, gain, gate, up, wgu, wd)


def _ffn_bwd_act(dh_out, gate, up, wd, name):
    n = dh_out.shape[0]
    tm = _row_tile(n, 512)

    def body(dh_ref, gate_ref, up_ref, wd_ref, dgate_ref, dup_ref, dhb_scr):
        @pl.when(pl.program_id(1) == 0)
        def _():
            dhb_scr[...] = (0.5 * dh_ref[...]).astype(BF16)

        da = _dot_nt(dhb_scr[...], wd_ref[0])
        g = gate_ref[0].astype(F32)
        u = up_ref[0].astype(F32)
        sig = _sigmoid(g)
        dgate_ref[0] = (da * u * (sig * (1.0 + g * (1.0 - sig)))).astype(BF16)
        dup_ref[0] = (da * (g * sig)).astype(BF16)

    chunk = pl.BlockSpec((1, tm, F_CHUNK), lambda i, j: (j, i, 0))
    return pl.pallas_call(
        body, name=name, grid=(n // tm, N_CHUNK),
        in_specs=[pl.BlockSpec((tm, D_MODEL), lambda i, j: (i, 0)), chunk, chunk,
                  pl.BlockSpec((1, F_CHUNK, D_MODEL), lambda i, j: (j, 0, 0))],
        out_specs=[chunk, chunk],
        out_shape=[jax.ShapeDtypeStruct((N_CHUNK, n, F_CHUNK), BF16)] * 2,
        scratch_shapes=[pltpu.VMEM((tm, D_MODEL), BF16)],
        compiler_params=_params(("parallel", "arbitrary")),
    )(dh_out, gate, up, wd)


def _ffn_bwd_in(dh_out, h_in, gain, dgate, dup, wgu, name, exchange=()):
    n = h_in.shape[0]
    tm = _row_tile(n, 512)
    n_i = n // tm
    nw = len(exchange)

    def body(*refs):
        (dh_ref, h_ref, g_ref, dgate_ref, dup_ref, wgu_ref), xin, (dhin_ref, dgain_ref), xout, (acc_scr,), sems = \
            _split_refs(refs, 6, nw, 2, 1)
        i = pl.program_id(0)
        j = pl.program_id(1)
        if nw:
            comm = _Exchange(xin, xout, sems)
            pl.when((i == 0) & (j == 0))(comm.start)

        @pl.when((i == 0) & (j == 0))
        def _():
            dgain_ref[...] = jnp.zeros_like(dgain_ref)

        @pl.when(j == 0)
        def _():
            acc_scr[...] = jnp.zeros_like(acc_scr)

        acc_scr[...] += _dot_nt(dgate_ref[0], wgu_ref[0, 0]) + _dot_nt(dup_ref[0], wgu_ref[1, 0])

        @pl.when(j == N_CHUNK - 1)
        def _():
            gain_v = g_ref[...]
            _, xhat, r = _rms(h_ref[...], gain_v)
            dn = acc_scr[...]
            dhin_ref[...] = dh_ref[...] + _rms_bwd(dn, xhat, r, gain_v)
            dgain_ref[...] += jnp.sum(dn * xhat, axis=0, keepdims=True)

        if nw:
            pl.when((i == n_i - 1) & (j == N_CHUNK - 1))(comm.finish)

    chunk = pl.BlockSpec((1, tm, F_CHUNK), lambda i, j: (j, i, 0))
    rows = pl.BlockSpec((tm, D_MODEL), lambda i, j: (i, 0))
    vec = pl.BlockSpec((1, D_MODEL), lambda i, j: (0, 0))
    return pl.pallas_call(
        body, name=name, grid=(n_i, N_CHUNK),
        in_specs=[rows, rows, vec, chunk, chunk,
                  pl.BlockSpec((2, 1, D_MODEL, F_CHUNK), lambda i, j: (0, j, 0, 0))] + [ANY] * nw,
        out_specs=[rows, vec] + [ANY] * nw,
        out_shape=[jax.ShapeDtypeStruct((n, D_MODEL), F32), jax.ShapeDtypeStruct((1, D_MODEL), F32)]
        + [jax.ShapeDtypeStruct(a.shape, a.dtype) for a in exchange],
        scratch_shapes=[pltpu.VMEM((tm, D_MODEL), F32)] + (_comm_sems(nw) if nw else []),
        compiler_params=_params(("arbitrary", "arbitrary")),
    )(dh_out, h_in, gain, dgate, dup, wgu, *exchange)


def _ffn_bwd_w(dh_out, h_in, gain, gate, up, dgate, dup, name):
    n = h_in.shape[0]
    tm = _row_tile(n, 512)
    n_i = n // tm

    def body(dh_ref, h_ref, g_ref, gate_ref, up_ref, dgate_ref, dup_ref, dwgu_ref, dwd_ref,
             ag_scr, au_scr, ad_scr):
        i = pl.program_id(1)

        @pl.when(i == 0)
        def _():
            ag_scr[...] = jnp.zeros_like(ag_scr)
            au_scr[...] = jnp.zeros_like(au_scr)
            ad_scr[...] = jnp.zeros_like(ad_scr)

        y, _, _ = _rms(h_ref[...], g_ref[...])
        nb = y.astype(BF16)
        ag_scr[...] += _dot_tn(nb, dgate_ref[0])
        au_scr[...] += _dot_tn(nb, dup_ref[0])
        g = gate_ref[0].astype(F32)
        act = (g * _sigmoid(g) * up_ref[0].astype(F32)).astype(BF16)
        ad_scr[...] += _dot_tn(act, (0.5 * dh_ref[...]).astype(BF16))

        @pl.when(i == n_i - 1)
        def _():
            dwgu_ref[0, 0] = ag_scr[...].astype(BF16)
            dwgu_ref[1, 0] = au_scr[...].astype(BF16)
            dwd_ref[0] = ad_scr[...].astype(BF16)

    chunk = pl.BlockSpec((1, tm, F_CHUNK), lambda j, i: (j, i, 0))
    rows = pl.BlockSpec((tm, D_MODEL), lambda j, i: (i, 0))
    return pl.pallas_call(
        body, name=name, grid=(N_CHUNK, n_i),
        in_specs=[rows, rows, pl.BlockSpec((1, D_MODEL), lambda j, i: (0, 0)), chunk, chunk, chunk, chunk],
        out_specs=[pl.BlockSpec((2, 1, D_MODEL, F_CHUNK), lambda j, i: (0, j, 0, 0)),
                   pl.BlockSpec((1, F_CHUNK, D_MODEL), lambda j, i: (j, 0, 0))],
        out_shape=[jax.ShapeDtypeStruct((2, N_CHUNK, D_MODEL, F_CHUNK), BF16),
                   jax.ShapeDtypeStruct((N_CHUNK, F_CHUNK, D_MODEL), BF16)],
        scratch_shapes=[pltpu.VMEM((D_MODEL, F_CHUNK), F32), pltpu.VMEM((D_MODEL, F_CHUNK), F32),
                        pltpu.VMEM((F_CHUNK, D_MODEL), F32)],
        compiler_params=_params(("parallel", "arbitrary")),
    )(dh_out, h_in, gain, gate, up, dgate, dup)


def _resident(shape, rank):
    zeros = (0,) * len(shape)
    index_map = (lambda i: zeros) if rank == 1 else (lambda i, j: zeros)
    return pl.BlockSpec(shape, index_map, pipeline_mode=pl.Buffered(1))


W_GU_SHAPE = (2, N_CHUNK, F_CHUNK, D_MODEL)
W_D_SHAPE = (N_CHUNK, F_CHUNK, D_MODEL)


def _ffn_fwd(h, gain, wgu, wd, name, gather=()):
    n = h.shape[0]
    tm = _row_tile(n, 512)
    n_i = n // tm
    nw = len(gather)

    def body(*refs):
        (h_ref, g_ref, wgu_ref, wd_ref), gin, (out_ref, nrm_ref, gate_ref, up_ref), gout, _, sems = \
            _split_refs(refs, 4, nw, 4, 0)
        i = pl.program_id(0)
        if nw:
            comm = _Gather(gin, gout, sems)
            pl.when(i == 0)(comm.start)
            pl.when(i == (3 * n_i) // 4)(comm.forward)

        hv = h_ref[...]
        y, _, _ = _rms(hv, g_ref[...])
        nb = y.astype(BF16)
        nrm_ref[...] = nb
        acc = jnp.zeros((tm, D_MODEL), F32)
        for j in range(N_CHUNK):
            gate = _dot_nt(nb, wgu_ref[0, j])
            up = _dot_nt(nb, wgu_ref[1, j])
            gate_ref[j] = gate.astype(BF16)
            up_ref[j] = up.astype(BF16)
            acc = acc + _dot((gate * _sigmoid(gate) * up).astype(BF16), wd_ref[j])
        out_ref[...] = hv + 0.5 * acc

        if nw:
            pl.when(i == n_i - 1)(comm.finish)

    rows = pl.BlockSpec((tm, D_MODEL), lambda i: (i, 0))
    chunks = pl.BlockSpec((N_CHUNK, tm, F_CHUNK), lambda i: (0, i, 0))
    return pl.pallas_call(
        body, name=name, grid=(n_i,),
        in_specs=[rows, pl.BlockSpec((1, D_MODEL), lambda i: (0, 0)), _resident(W_GU_SHAPE, 1),
                  _resident(W_D_SHAPE, 1)] + [ANY] * nw,
        out_specs=[rows, rows, chunks, chunks] + [ANY] * nw,
        out_shape=[jax.ShapeDtypeStruct((n, D_MODEL), F32), jax.ShapeDtypeStruct((n, D_MODEL), BF16),
                   jax.ShapeDtypeStruct((N_CHUNK, n, F_CHUNK), BF16),
                   jax.ShapeDtypeStruct((N_CHUNK, n, F_CHUNK), BF16)]
        + [jax.ShapeDtypeStruct((N_DEV,) + a.shape, a.dtype) for a in gather],
        scratch_shapes=_comm_sems(nw) if nw else [],
        compiler_params=_params(("arbitrary",)),
    )(h, gain, wgu, wd, *gather)


def _swiglu_bwd(da, gate_ref, up_ref, j):
    g = gate_ref[j].astype(F32)
    u = up_ref[j].astype(F32)
    sig = _sigmoid(g)
    return (da * u * (sig * (1.0 + g * (1.0 - sig)))).astype(BF16), (da * (g * sig)).astype(BF16)


def _ffn_bwd_x(dh_out, h_in, gain, gate, up, wgu, wd, name):
    n = h_in.shape[0]
    tm = _row_tile(n, 256)

    def body(dh_ref, h_ref, g_ref, gate_ref, up_ref, wgu_ref, wd_ref,
             dhin_ref, dhb_ref, dgate_ref, dup_ref, dgain_ref):
        @pl.when(pl.program_id(0) == 0)
        def _():
            dgain_ref[...] = jnp.zeros_like(dgain_ref)

        dhv = dh_ref[...]
        dhb = (0.5 * dhv).astype(BF16)
        dhb_ref[...] = dhb
        dn = jnp.zeros((tm, D_MODEL), F32)
        for j in range(N_CHUNK):
            dgate, dup = _swiglu_bwd(_dot_nt(dhb, wd_ref[j]), gate_ref, up_ref, j)
            dgate_ref[j] = dgate
            dup_ref[j] = dup
            dn = dn + _dot(dgate, wgu_ref[0, j]) + _dot(dup, wgu_ref[1, j])
        gain_v = g_ref[...]
        _, xhat, r = _rms(h_ref[...], gain_v)
        dhin_ref[...] = dhv + _rms_bwd(dn, xhat, r, gain_v)
        dgain_ref[...] += jnp.sum(dn * xhat, axis=0, keepdims=True)

    rows = pl.BlockSpec((tm, D_MODEL), lambda i: (i, 0))
    chunks = pl.BlockSpec((N_CHUNK, tm, F_CHUNK), lambda i: (0, i, 0))
    vec = pl.BlockSpec((1, D_MODEL), lambda i: (0, 0))
    return pl.pallas_call(
        body, name=name, grid=(n // tm,),
        in_specs=[rows, rows, vec, chunks, chunks, _resident(W_GU_SHAPE, 1), _resident(W_D_SHAPE, 1)],
        out_specs=[rows, rows, chunks, chunks, vec],
        out_shape=[jax.ShapeDtypeStruct((n, D_MODEL), F32), jax.ShapeDtypeStruct((n, D_MODEL), BF16),
                   jax.ShapeDtypeStruct((N_CHUNK, n, F_CHUNK), BF16),
                   jax.ShapeDtypeStruct((N_CHUNK, n, F_CHUNK), BF16),
                   jax.ShapeDtypeStruct((1, D_MODEL), F32)],
        compiler_params=_params(("arbitrary",)),
    )(dh_out, h_in, gain, gate, up, wgu, wd)


def _ffn_bwd_act(dh_out, gate, up, wd, name, exchange=()):
    n = dh_out.shape[0]
    tm = _row_tile(n, 512)
    n_i = n // tm
    nw = len(exchange)

    def body(*refs):
        (dh_ref, gate_ref, up_ref, wd_ref), xin, (dhb_ref, dgate_ref, dup_ref), xout, _, sems = \
            _split_refs(refs, 4, nw, 3, 0)
        i = pl.program_id(0)
        if nw:
            comm = _Exchange(xin, xout, sems)
            pl.when(i == 0)(comm.start)

        dhb = (0.5 * dh_ref[...]).astype(BF16)
        dhb_ref[...] = dhb
        for j in range(N_CHUNK):
            dgate_ref[j], dup_ref[j] = _swiglu_bwd(_dot_nt(dhb, wd_ref[j]), gate_ref, up_ref, j)

        if nw:
            pl.when(i == n_i - 1)(comm.finish)

    rows = pl.BlockSpec((tm, D_MODEL), lambda i: (i, 0))
    chunks = pl.BlockSpec((N_CHUNK, tm, F_CHUNK), lambda i: (0, i, 0))
    return pl.pallas_call(
        body, name=name, grid=(n_i,),
        in_specs=[rows, chunks, chunks, _resident(W_D_SHAPE, 1)] + [ANY] * nw,
        out_specs=[rows, chunks, chunks] + [ANY] * nw,
        out_shape=[jax.ShapeDtypeStruct((n, D_MODEL), BF16)] + [jax.ShapeDtypeStruct((N_CHUNK, n, F_CHUNK), BF16)] * 2
        + [jax.ShapeDtypeStruct(a.shape, a.dtype) for a in exchange],
        scratch_shapes=_comm_sems(nw) if nw else [],
        compiler_params=_params(("arbitrary",)),
    )(dh_out, gate, up, wd, *exchange)


def _ffn_bwd_in(dh_out, h_in, gain, dgate, dup, wgu, name, exchange=()):
    n = h_in.shape[0]
    tm = _row_tile(n, 512)
    n_i = n // tm
    nw = len(exchange)

    def body(*refs):
        (dh_ref, h_ref, g_ref, dgate_ref, dup_ref, wgu_ref), xin, (dhin_ref, dgain_ref), xout, _, sems = \
            _split_refs(refs, 6, nw, 2, 0)
        i = pl.program_id(0)
        if nw:
            comm = _ChipExchange(xin, xout, sems)
            pl.when(i == 0)(comm.start)

        @pl.when(i == 0)
        def _():
            dgain_ref[...] = jnp.zeros_like(dgain_ref)

        dn = jnp.zeros((tm, D_MODEL), F32)
        for j in range(N_CHUNK):
            dn = dn + _dot(dgate_ref[j], wgu_ref[0, j]) + _dot(dup_ref[j], wgu_ref[1, j])
        gain_v = g_ref[...]
        _, xhat, r = _rms(h_ref[...], gain_v)
        dhin_ref[...] = dh_ref[...] + _rms_bwd(dn, xhat, r, gain_v)
        dgain_ref[...] += jnp.sum(dn * xhat, axis=0, keepdims=True)

        if nw:
            pl.when(i == n_i - 1)(comm.finish)

    rows = pl.BlockSpec((tm, D_MODEL), lambda i: (i, 0))
    chunks = pl.BlockSpec((N_CHUNK, tm, F_CHUNK), lambda i: (0, i, 0))
    vec = pl.BlockSpec((1, D_MODEL), lambda i: (0, 0))
    return pl.pallas_call(
        body, name=name, grid=(n_i,),
        in_specs=[rows, rows, vec, chunks, chunks, _resident(W_GU_SHAPE, 1)] + [ANY] * nw,
        out_specs=[rows, vec] + [ANY] * nw,
        out_shape=[jax.ShapeDtypeStruct((n, D_MODEL), F32), jax.ShapeDtypeStruct((1, D_MODEL), F32)]
        + [jax.ShapeDtypeStruct(a.shape, a.dtype) for a in exchange],
        scratch_shapes=_comm_sems(nw) if nw else [],
        compiler_params=_params(("arbitrary",)),
    )(dh_out, h_in, gain, dgate, dup, wgu, *exchange)


W_GROUP = 2


def _ffn_bwd_w(dhb, nrm, gate, up, dgate, dup, name):
    n = nrm.shape[0]
    tm = _row_tile(n, 512)
    n_i = n // tm

    def body(dhb_ref, nrm_ref, gate_ref, up_ref, dgate_ref, dup_ref, dwgu_ref, dwd_ref, ag_scr, au_scr, ad_scr):
        i = pl.program_id(1)

        @pl.when(i == 0)
        def _():
            ag_scr[...] = jnp.zeros_like(ag_scr)
            au_scr[...] = jnp.zeros_like(au_scr)
            ad_scr[...] = jnp.zeros_like(ad_scr)

        nb = nrm_ref[...]
        dhv = dhb_ref[...]
        for jj in range(W_GROUP):
            ag_scr[jj] += _dot_tn(dgate_ref[jj], nb)
            au_scr[jj] += _dot_tn(dup_ref[jj], nb)
            g = gate_ref[jj].astype(F32)
            act = (g * _sigmoid(g) * up_ref[jj].astype(F32)).astype(BF16)
            ad_scr[jj] += _dot_tn(act, dhv)

        @pl.when(i == n_i - 1)
        def _():
            dwgu_ref[0] = ag_scr[...].astype(BF16)
            dwgu_ref[1] = au_scr[...].astype(BF16)
            dwd_ref[...] = ad_scr[...].astype(BF16)

    chunks = pl.BlockSpec((W_GROUP, tm, F_CHUNK), lambda g, i: (g, i, 0))
    rows = pl.BlockSpec((tm, D_MODEL), lambda g, i: (i, 0))
    return pl.pallas_call(
        body, name=name, grid=(N_CHUNK // W_GROUP, n_i),
        in_specs=[rows, rows, chunks, chunks, chunks, chunks],
        out_specs=[pl.BlockSpec((2, W_GROUP, F_CHUNK, D_MODEL), lambda g, i: (0, g, 0, 0)),
                   pl.BlockSpec((W_GROUP, F_CHUNK, D_MODEL), lambda g, i: (g, 0, 0))],
        out_shape=[jax.ShapeDtypeStruct(W_GU_SHAPE, BF16), jax.ShapeDtypeStruct(W_D_SHAPE, BF16)],
        scratch_shapes=[pltpu.VMEM((W_GROUP, F_CHUNK, D_MODEL), F32), pltpu.VMEM((W_GROUP, F_CHUNK, D_MODEL), F32),
                        pltpu.VMEM((W_GROUP, F_CHUNK, D_MODEL), F32)],
        compiler_params=_params(("parallel", "arbitrary")),
    )(dhb, nrm, gate, up, dgate, dup)


HID_PIECES = ((0, 1024), (1024, 2048), (2048, D_FF))
W_GU_SHAPE = (2, D_FF, D_MODEL)
W_D_SHAPE = (D_FF, D_MODEL)


def _ffn_fwd(h, gain, wgu, wd, name, gather=()):
    n = h.shape[0]
    tm = _row_tile(n, 512)
    n_i = n // tm
    nw = len(gather)

    def body(*refs):
        (h_ref, g_ref, wgu_ref, wd_ref), gin, (out_ref, nrm_ref, gate_ref, up_ref), gout, _, sems = \
            _split_refs(refs, 4, nw, 4, 0)
        i = pl.program_id(0)
        if nw:
            comm = _Gather(gin, gout, sems)
            pl.when(i == 0)(comm.start)
            pl.when(i == (3 * n_i) // 4)(comm.forward)

        hv = h_ref[...]
        y, _, _ = _rms(hv, g_ref[...])
        nb = y.astype(BF16)
        nrm_ref[...] = nb
        acc = jnp.zeros((tm, D_MODEL), F32)
        for a, b in HID_PIECES:
            gate = _dot_nt(nb, wgu_ref[0, a:b, :])
            up = _dot_nt(nb, wgu_ref[1, a:b, :])
            gate_ref[:, a:b] = gate.astype(BF16)
            up_ref[:, a:b] = up.astype(BF16)
            acc = acc + _dot((gate * _sigmoid(gate) * up).astype(BF16), wd_ref[a:b, :])
        out_ref[...] = hv + 0.5 * acc

        if nw:
            pl.when(i == n_i - 1)(comm.finish)

    rows = pl.BlockSpec((tm, D_MODEL), lambda i: (i, 0))
    hid = pl.BlockSpec((tm, D_FF), lambda i: (i, 0))
    return pl.pallas_call(
        body, name=name, grid=(n_i,),
        in_specs=[rows, pl.BlockSpec((1, D_MODEL), lambda i: (0, 0)), _resident(W_GU_SHAPE, 1),
                  _resident(W_D_SHAPE, 1)] + [ANY] * nw,
        out_specs=[rows, rows, hid, hid] + [ANY] * nw,
        out_shape=[jax.ShapeDtypeStruct((n, D_MODEL), F32), jax.ShapeDtypeStruct((n, D_MODEL), BF16),
                   jax.ShapeDtypeStruct((n, D_FF), BF16), jax.ShapeDtypeStruct((n, D_FF), BF16)]
        + [jax.ShapeDtypeStruct((N_DEV,) + a.shape, a.dtype) for a in gather],
        scratch_shapes=_comm_sems(nw) if nw else [],
        compiler_params=_params(("arbitrary",)),
    )(h, gain, wgu, wd, *gather)


def _swiglu_bwd(da, gate_ref, up_ref, a, b):
    g = gate_ref[:, a:b].astype(F32)
    u = up_ref[:, a:b].astype(F32)
    sig = _sigmoid(g)
    return (da * u * (sig * (1.0 + g * (1.0 - sig)))).astype(BF16), (da * (g * sig)).astype(BF16)


def _ffn_bwd_x(dh_out, h_in, gain, gate, up, wgu, wd, name):
    n = h_in.shape[0]
    tm = _row_tile(n, 256)

    def body(dh_ref, h_ref, g_ref, gate_ref, up_ref, wgu_ref, wd_ref,
             dhin_ref, dhb_ref, dgate_ref, dup_ref, dgain_ref):
        @pl.when(pl.program_id(0) == 0)
        def _():
            dgain_ref[...] = jnp.zeros_like(dgain_ref)

        dhv = dh_ref[...]
        dhb = (0.5 * dhv).astype(BF16)
        dhb_ref[...] = dhb
        dn = jnp.zeros((tm, D_MODEL), F32)
        for a, b in HID_PIECES:
            dgate, dup = _swiglu_bwd(_dot_nt(dhb, wd_ref[a:b, :]), gate_ref, up_ref, a, b)
            dgate_ref[:, a:b] = dgate
            dup_ref[:, a:b] = dup
            dn = dn + _dot(dgate, wgu_ref[0, a:b, :]) + _dot(dup, wgu_ref[1, a:b, :])
        gain_v = g_ref[...]
        _, xhat, r = _rms(h_ref[...], gain_v)
        dhin_ref[...] = dhv + _rms_bwd(dn, xhat, r, gain_v)
        dgain_ref[...] += jnp.sum(dn * xhat, axis=0, keepdims=True)

    rows = pl.BlockSpec((tm, D_MODEL), lambda i: (i, 0))
    hid = pl.BlockSpec((tm, D_FF), lambda i: (i, 0))
    vec = pl.BlockSpec((1, D_MODEL), lambda i: (0, 0))
    return pl.pallas_call(
        body, name=name, grid=(n // tm,),
        in_specs=[rows, rows, vec, hid, hid, _resident(W_GU_SHAPE, 1), _resident(W_D_SHAPE, 1)],
        out_specs=[rows, rows, hid, hid, vec],
        out_shape=[jax.ShapeDtypeStruct((n, D_MODEL), F32), jax.ShapeDtypeStruct((n, D_MODEL), BF16),
                   jax.ShapeDtypeStruct((n, D_FF), BF16), jax.ShapeDtypeStruct((n, D_FF), BF16),
                   jax.ShapeDtypeStruct((1, D_MODEL), F32)],
        compiler_params=_params(("arbitrary",)),
    )(dh_out, h_in, gain, gate, up, wgu, wd)


def _ffn_bwd_act(dh_out, gate, up, wd, name, exchange=()):
    n = dh_out.shape[0]
    tm = _row_tile(n, 512)
    n_i = n // tm
    nw = len(exchange)

    def body(*refs):
        (dh_ref, gate_ref, up_ref, wd_ref), xin, (dhb_ref, dgate_ref, dup_ref), xout, _, sems = \
            _split_refs(refs, 4, nw, 3, 0)
        i = pl.program_id(0)
        if nw:
            comm = _Exchange(xin, xout, sems)
            pl.when(i == 0)(comm.start)

        dhb = (0.5 * dh_ref[...]).astype(BF16)
        dhb_ref[...] = dhb
        for a, b in HID_PIECES:
            dgate_ref[:, a:b], dup_ref[:, a:b] = _swiglu_bwd(_dot_nt(dhb, wd_ref[a:b, :]), gate_ref, up_ref, a, b)

        if nw:
            pl.when(i == n_i - 1)(comm.finish)

    rows = pl.BlockSpec((tm, D_MODEL), lambda i: (i, 0))
    hid = pl.BlockSpec((tm, D_FF), lambda i: (i, 0))
    return pl.pallas_call(
        body, name=name, grid=(n_i,),
        in_specs=[rows, hid, hid, _resident(W_D_SHAPE, 1)] + [ANY] * nw,
        out_specs=[rows, hid, hid] + [ANY] * nw,
        out_shape=[jax.ShapeDtypeStruct((n, D_MODEL), BF16)] + [jax.ShapeDtypeStruct((n, D_FF), BF16)] * 2
        + [jax.ShapeDtypeStruct(a.shape, a.dtype) for a in exchange],
        scratch_shapes=_comm_sems(nw) if nw else [],
        compiler_params=_params(("arbitrary",)),
    )(dh_out, gate, up, wd, *exchange)


def _ffn_bwd_in(dh_out, h_in, gain, dgate, dup, wgu, name, exchange=()):
    n = h_in.shape[0]
    tm = _row_tile(n, 512)
    n_i = n // tm
    nw = len(exchange)

    def body(*refs):
        (dh_ref, h_ref, g_ref, dgate_ref, dup_ref, wgu_ref), xin, (dhin_ref, dgain_ref), xout, _, sems = \
            _split_refs(refs, 6, nw, 2, 0)
        i = pl.program_id(0)
        if nw:
            comm = _ChipExchange(xin, xout, sems)
            pl.when(i == 0)(comm.start)

        @pl.when(i == 0)
        def _():
            dgain_ref[...] = jnp.zeros_like(dgain_ref)

        dn = jnp.zeros((tm, D_MODEL), F32)
        for a, b in HID_PIECES:
            dn = dn + _dot(dgate_ref[:, a:b], wgu_ref[0, a:b, :]) + _dot(dup_ref[:, a:b], wgu_ref[1, a:b, :])
        gain_v = g_ref[...]
        _, xhat, r = _rms(h_ref[...], gain_v)
        dhin_ref[...] = dh_ref[...] + _rms_bwd(dn, xhat, r, gain_v)
        dgain_ref[...] += jnp.sum(dn * xhat, axis=0, keepdims=True)

        if nw:
            pl.when(i == n_i - 1)(comm.finish)

    rows = pl.BlockSpec((tm, D_MODEL), lambda i: (i, 0))
    hid = pl.BlockSpec((tm, D_FF), lambda i: (i, 0))
    vec = pl.BlockSpec((1, D_MODEL), lambda i: (0, 0))
    return pl.pallas_call(
        body, name=name, grid=(n_i,),
        in_specs=[rows, rows, vec, hid, hid, _resident(W_GU_SHAPE, 1)] + [ANY] * nw,
        out_specs=[rows, vec] + [ANY] * nw,
        out_shape=[jax.ShapeDtypeStruct((n, D_MODEL), F32), jax.ShapeDtypeStruct((1, D_MODEL), F32)]
        + [jax.ShapeDtypeStruct(a.shape, a.dtype) for a in exchange],
        scratch_shapes=_comm_sems(nw) if nw else [],
        compiler_params=_params(("arbitrary",)),
    )(dh_out, h_in, gain, dgate, dup, wgu, *exchange)


def _ffn_bwd_act_wd(dh_out, gate, up, wd, name, exchange=()):
    n = dh_out.shape[0]
    tm = _row_tile(n, 256)
    n_i = n // tm
    nw = len(exchange)

    def body(*refs):
        (dh_ref, gate_ref, up_ref, wd_ref), xin, (dgate_ref, dup_ref, dw_ref), xout, (acc_scr,), sems = \
            _split_refs(refs, 4, nw, 3, 1)
        i = pl.program_id(0)
        if nw:
            comm = _Exchange(xin, xout, sems)
            pl.when(i == 0)(comm.start)

        @pl.when(i == 0)
        def _():
            acc_scr[...] = jnp.zeros_like(acc_scr)

        dhb = (0.5 * dh_ref[...]).astype(BF16)
        for a, b in HID_PIECES:
            da = _dot_nt(dhb, wd_ref[a:b, :])
            g = gate_ref[:, a:b].astype(F32)
            u = up_ref[:, a:b].astype(F32)
            sig = _sigmoid(g)
            silu = g * sig
            dgate_ref[:, a:b] = (da * u * (sig * (1.0 + g * (1.0 - sig)))).astype(BF16)
            dup_ref[:, a:b] = (da * silu).astype(BF16)
            acc_scr[a:b, :] += _dot_tn((silu * u).astype(BF16), dhb)

        @pl.when(i == n_i - 1)
        def _():
            dw_ref[...] = acc_scr[...].astype(BF16)

        if nw:
            pl.when(i == n_i - 1)(comm.finish)

    rows = pl.BlockSpec((tm, D_MODEL), lambda i: (i, 0))
    hid = pl.BlockSpec((tm, D_FF), lambda i: (i, 0))
    return pl.pallas_call(
        body, name=name, grid=(n_i,),
        in_specs=[rows, hid, hid, _resident(W_D_SHAPE, 1)] + [ANY] * nw,
        out_specs=[hid, hid, _resident(W_D_SHAPE, 1)] + [ANY] * nw,
        out_shape=[jax.ShapeDtypeStruct((n, D_FF), BF16)] * 2 + [jax.ShapeDtypeStruct(W_D_SHAPE, BF16)]
        + [jax.ShapeDtypeStruct(a.shape, a.dtype) for a in exchange],
        scratch_shapes=[pltpu.VMEM(W_D_SHAPE, F32)] + (_comm_sems(nw) if nw else []),
        compiler_params=_params(("arbitrary",)),
    )(dh_out, gate, up, wd, *exchange)


def _ffn_bwd_wgu(nrm, dgate, dup, name):
    n = nrm.shape[0]
    tm = _row_tile(n, 256)
    n_i = n // tm

    def body(nrm_ref, dgate_ref, dup_ref, dw_ref, acc_scr):
        i = pl.program_id(0)

        @pl.when(i == 0)
        def _():
            acc_scr[...] = jnp.zeros_like(acc_scr)

        nb = nrm_ref[...]
        for a, b in HID_PIECES:
            acc_scr[0, a:b, :] += _dot_tn(dgate_ref[:, a:b], nb)
            acc_scr[1, a:b, :] += _dot_tn(dup_ref[:, a:b], nb)

        @pl.when(i == n_i - 1)
        def _():
            dw_ref[...] = acc_scr[...].astype(BF16)

    hid = pl.BlockSpec((tm, D_FF), lambda i: (i, 0))
    return pl.pallas_call(
        body, name=name, grid=(n_i,),
        in_specs=[pl.BlockSpec((tm, D_MODEL), lambda i: (i, 0)), hid, hid],
        out_specs=_resident(W_GU_SHAPE, 1),
        out_shape=jax.ShapeDtypeStruct(W_GU_SHAPE, BF16),
        scratch_shapes=[pltpu.VMEM(W_GU_SHAPE, F32)],
        compiler_params=_params(("arbitrary",)),
    )(nrm, dgate, dup)


def _ffn_bwd_wd(dhb, gate, up, name):
    n = dhb.shape[0]
    tm = _row_tile(n, 512)
    n_i = n // tm

    def body(dhb_ref, gate_ref, up_ref, dw_ref, acc_scr):
        i = pl.program_id(0)

        @pl.when(i == 0)
        def _():
            acc_scr[...] = jnp.zeros_like(acc_scr)

        dhv = dhb_ref[...]
        for a, b in HID_PIECES:
            g = gate_ref[:, a:b].astype(F32)
            act = (g * _sigmoid(g) * up_ref[:, a:b].astype(F32)).astype(BF16)
            acc_scr[a:b, :] += _dot_tn(act, dhv)

        @pl.when(i == n_i - 1)
        def _():
            dw_ref[...] = acc_scr[...].astype(BF16)

    hid = pl.BlockSpec((tm, D_FF), lambda i: (i, 0))
    return pl.pallas_call(
        body, name=name, grid=(n_i,),
        in_specs=[pl.BlockSpec((tm, D_MODEL), lambda i: (i, 0)), hid, hid],
        out_specs=_resident(W_D_SHAPE, 1),
        out_shape=jax.ShapeDtypeStruct(W_D_SHAPE, BF16),
        scratch_shapes=[pltpu.VMEM(W_D_SHAPE, F32)],
        compiler_params=_params(("arbitrary",)),
    )(dhb, gate, up)


N_PIECE = IN_MAIN // 512


def _inproj_fwd(h, gain, w_in):
    n = h.shape[0]
    tm = _row_tile(n, 512)

    def body(h_ref, g_ref, w_ref, *outs):
        y, _, _ = _rms(h_ref[...], g_ref[...])
        nb = y.astype(BF16)
        for p in range(N_PIECE):
            outs[p][...] = _dot_nt(nb, w_ref[512 * p:512 * (p + 1), :]).astype(BF16)
        outs[N_PIECE][...] = _dot_nt(nb, w_ref[IN_MAIN:IN_PAD, :])

    piece = pl.BlockSpec((tm, 512), lambda i: (i, 0))
    return pl.pallas_call(
        body, name="inproj_fwd", grid=(n // tm,),
        in_specs=[pl.BlockSpec((tm, D_MODEL), lambda i: (i, 0)),
                  pl.BlockSpec((1, D_MODEL), lambda i: (0, 0)),
                  pl.BlockSpec((IN_PAD, D_MODEL), lambda i: (0, 0))],
        out_specs=[piece] * N_PIECE + [pl.BlockSpec((tm, 128), lambda i: (i, 0))],
        out_shape=[jax.ShapeDtypeStruct((n, 512), BF16)] * N_PIECE + [jax.ShapeDtypeStruct((n, 128), F32)],
        compiler_params=_params(("parallel",)),
    )(h, gain, w_in)


def _inproj_bwd(dpieces, dfg, dh_out, h_in, gain, w_in):
    n = h_in.shape[0]
    tm = _row_tile(n, 512)
    n_i = n // tm

    def body(*refs):
        dp_refs = refs[:N_PIECE]
        dfg_ref, dh_ref, h_ref, g_ref, w_ref, dhin_ref, dw_ref, dgain_ref, acc_scr = refs[N_PIECE:]
        i = pl.program_id(0)

        @pl.when(i == 0)
        def _():
            acc_scr[...] = jnp.zeros_like(acc_scr)
            dgain_ref[...] = jnp.zeros_like(dgain_ref)

        gain_v = g_ref[...]
        y, xhat, r = _rms(h_ref[...], gain_v)
        nb = y.astype(BF16)
        dn = jnp.zeros((tm, D_MODEL), F32)
        for p in range(N_PIECE + 1):
            lo, hi = (512 * p, 512 * (p + 1)) if p < N_PIECE else (IN_MAIN, IN_PAD)
            dp = (dp_refs[p][...] if p < N_PIECE else dfg_ref[...]).astype(BF16)
            dn = dn + _dot(dp, w_ref[lo:hi, :])
            acc_scr[lo:hi, :] += _dot_tn(dp, nb)
        dhin_ref[...] = dh_ref[...] + _rms_bwd(dn, xhat, r, gain_v)
        dgain_ref[...] += jnp.sum(dn * xhat, axis=0, keepdims=True)

        @pl.when(i == n_i - 1)
        def _():
            dw_ref[...] = acc_scr[...].astype(BF16)

    piece = pl.BlockSpec((tm, 512), lambda i: (i, 0))
    rows = pl.BlockSpec((tm, D_MODEL), lambda i: (i, 0))
    vec = pl.BlockSpec((1, D_MODEL), lambda i: (0, 0))
    wspec = pl.BlockSpec((IN_PAD, D_MODEL), lambda i: (0, 0))
    return pl.pallas_call(
        body, name="inproj_bwd", grid=(n_i,),
        in_specs=[piece] * N_PIECE + [pl.BlockSpec((tm, 128), lambda i: (i, 0)), rows, rows, vec, wspec],
        out_specs=[rows, wspec, vec],
        out_shape=[jax.ShapeDtypeStruct((n, D_MODEL), F32),
                   jax.ShapeDtypeStruct((IN_PAD, D_MODEL), BF16),
                   jax.ShapeDtypeStruct((1, D_MODEL), F32)],
        scratch_shapes=[pltpu.VMEM((IN_PAD, D_MODEL), F32)],
        compiler_params=_params(("arbitrary",)),
    )(*dpieces, dfg, dh_out, h_in, gain, w_in)


def _outproj_fwd(zc, za, w_out, h):
    n = h.shape[0]
    tm = _row_tile(n, 512)

    def body(zc_ref, za_ref, w_ref, h_ref, out_ref):
        out_ref[...] = (h_ref[...] + _dot(zc_ref[...], w_ref[0:CONV_DIM, :])
                        + _dot(za_ref[...], w_ref[CONV_DIM:, :]))

    half = pl.BlockSpec((tm, 512), lambda i: (i, 0))
    rows = pl.BlockSpec((tm, D_MODEL), lambda i: (i, 0))
    return pl.pallas_call(
        body, name="outproj_fwd", grid=(n // tm,),
        in_specs=[half, half, pl.BlockSpec((D_MODEL, D_MODEL), lambda i: (0, 0)), rows],
        out_specs=rows,
        out_shape=jax.ShapeDtypeStruct((n, D_MODEL), F32),
        compiler_params=_params(("parallel",)),
    )(zc, za, w_out, h)


def _outproj_bwd(dh, zc, za, w_out):
    n = dh.shape[0]
    tm = _row_tile(n, 512)
    n_i = n // tm

    def body(dh_ref, zc_ref, za_ref, w_ref, dzc_ref, dza_ref, dw_ref, acc_scr):
        i = pl.program_id(0)

        @pl.when(i == 0)
        def _():
            acc_scr[...] = jnp.zeros_like(acc_scr)

        dhb = dh_ref[...].astype(BF16)
        dzc_ref[...] = _dot_nt(dhb, w_ref[0:CONV_DIM, :]).astype(BF16)
        dza_ref[...] = _dot_nt(dhb, w_ref[CONV_DIM:, :]).astype(BF16)
        acc_scr[0:CONV_DIM, :] += _dot_tn(zc_ref[...], dhb)
        acc_scr[CONV_DIM:, :] += _dot_tn(za_ref[...], dhb)

        @pl.when(i == n_i - 1)
        def _():
            dw_ref[...] = acc_scr[...].astype(BF16)

    half = pl.BlockSpec((tm, 512), lambda i: (i, 0))
    wspec = pl.BlockSpec((D_MODEL, D_MODEL), lambda i: (0, 0))
    return pl.pallas_call(
        body, name="outproj_bwd", grid=(n_i,),
        in_specs=[pl.BlockSpec((tm, D_MODEL), lambda i: (i, 0)), half, half, wspec],
        out_specs=[half, half, wspec],
        out_shape=[jax.ShapeDtypeStruct((n, 512), BF16), jax.ShapeDtypeStruct((n, 512), BF16),
                   jax.ShapeDtypeStruct((D_MODEL, D_MODEL), BF16)],
        scratch_shapes=[pltpu.VMEM((D_MODEL, D_MODEL), F32)],
        compiler_params=_params(("arbitrary",)),
    )(dh, zc, za, w_out)


def _group_matrix():
    r = lax.broadcasted_iota(jnp.int32, (128, 128), 0) // HEAD_DIM
    c = lax.broadcasted_iota(jnp.int32, (128, 128), 1) // HEAD_DIM
    return jnp.where(r == c, 1.0 / HEAD_DIM, 0.0).astype(BF16)


def _group_mean(x, gmat):
    hi = x.astype(BF16)
    lo = (x - hi.astype(F32)).astype(BF16)
    return _dot(hi, gmat) + _dot(lo, gmat)


def _shift_rows(x, s):
    rows = x.shape[0]
    t = lax.broadcasted_iota(jnp.int32, x.shape, 0)
    rolled = pltpu.roll(x, s % rows, 0)
    keep = (t >= s) if s > 0 else (t < rows + s)
    return jnp.where(keep, rolled, 0.0)


def _conv_parts(bg_ref, cg_ref, hc_ref, w_ref):
    bg = bg_ref[...].astype(F32)
    cg = cg_ref[...].astype(F32)
    hc = hc_ref[...].astype(F32)
    u = cg * hc
    u1 = _shift_rows(u, 1)
    u2 = _shift_rows(u, 2)
    conv = w_ref[2:3, :] * u + w_ref[1:2, :] * u1 + w_ref[0:1, :] * u2
    return bg, cg, hc, u, u1, u2, conv


def _conv_fwd(bg, cg, hc, conv_w, gain, gmat, lp):
    n = bg.shape[0]
    nb = n // lp

    def body(bg_ref, cg_ref, hc_ref, w_ref, g_ref, gm_ref, z_ref):
        bgv, _, _, _, _, _, conv = _conv_parts(bg_ref, cg_ref, hc_ref, w_ref)
        yc = bgv * conv
        r = lax.rsqrt(_group_mean(yc * yc, gm_ref[...]) + EPS)
        z_ref[...] = (yc * r * g_ref[...]).astype(BF16)

    blk = pl.BlockSpec((lp, 128), lambda c, b: (b, c))
    return pl.pallas_call(
        body, name="conv_fwd", grid=(CONV_DIM // 128, nb),
        in_specs=[blk, blk, blk, pl.BlockSpec((3, 128), lambda c, b: (0, c)),
                  pl.BlockSpec((1, 128), lambda c, b: (0, c)), pl.BlockSpec((128, 128), lambda c, b: (0, 0))],
        out_specs=blk,
        out_shape=jax.ShapeDtypeStruct((n, CONV_DIM), BF16),
        compiler_params=_params(("parallel", "parallel")),
    )(bg, cg, hc, conv_w, gain, gmat)


def _conv_bwd(dz, bg, cg, hc, conv_w, gain, gmat, lp):
    n = bg.shape[0]
    nb = n // lp

    def body(dz_ref, bg_ref, cg_ref, hc_ref, w_ref, g_ref, gm_ref,
             dbg_ref, dcg_ref, dhc_ref, dw_ref, dgain_ref):
        b = pl.program_id(1)

        @pl.when(b == 0)
        def _():
            dw_ref[...] = jnp.zeros_like(dw_ref)
            dgain_ref[...] = jnp.zeros_like(dgain_ref)

        bgv, cgv, hcv, u, u1, u2, conv = _conv_parts(bg_ref, cg_ref, hc_ref, w_ref)
        gm = gm_ref[...]
        yc = bgv * conv
        r = lax.rsqrt(_group_mean(yc * yc, gm) + EPS)
        yhat = yc * r
        dzv = dz_ref[...].astype(F32)
        dyhat = dzv * g_ref[...]
        dgain_ref[...] += jnp.sum(dzv * yhat, axis=0, keepdims=True)
        dyc = r * (dyhat - yhat * _group_mean(dyhat * yhat, gm))
        dbg_ref[...] = (dyc * conv).astype(BF16)
        dconv = dyc * bgv
        du = (w_ref[2:3, :] * dconv + w_ref[1:2, :] * _shift_rows(dconv, -1)
              + w_ref[0:1, :] * _shift_rows(dconv, -2))
        dcg_ref[...] = (du * hcv).astype(BF16)
        dhc_ref[...] = (du * cgv).astype(BF16)
        dw_ref[0:1, :] += jnp.sum(dconv * u2, axis=0, keepdims=True)
        dw_ref[1:2, :] += jnp.sum(dconv * u1, axis=0, keepdims=True)
        dw_ref[2:3, :] += jnp.sum(dconv * u, axis=0, keepdims=True)

    blk = pl.BlockSpec((lp, 128), lambda c, b: (b, c))
    wspec = pl.BlockSpec((3, 128), lambda c, b: (0, c))
    gspec = pl.BlockSpec((1, 128), lambda c, b: (0, c))
    return pl.pallas_call(
        body, name="conv_bwd", grid=(CONV_DIM // 128, nb),
        in_specs=[blk, blk, blk, blk, wspec, gspec, pl.BlockSpec((128, 128), lambda c, b: (0, 0))],
        out_specs=[blk, blk, blk, wspec, gspec],
        out_shape=[jax.ShapeDtypeStruct((n, CONV_DIM), BF16)] * 3
        + [jax.ShapeDtypeStruct((3, CONV_DIM), F32), jax.ShapeDtypeStruct((1, CONV_DIM), F32)],
        compiler_params=_params(("parallel", "arbitrary")),
    )(dz, bg, cg, hc, conv_w, gain, gmat)


KEY_MASKED = 1e30
ONE_LANE = 24


def _scan_steps(rows):
    s, out = 1, []
    while s < rows:
        out.append(s)
        s *= 2
    return out


def _fgate_fwd(fg, b_f, lp):
    n = fg.shape[0]
    nb = n // lp

    def body(fg_ref, b_ref, ka_ref, qa_ref):
        x = fg_ref[...] + b_ref[...]
        logf = jnp.minimum(x, 0.0) - jnp.log(1.0 + jnp.exp(-jnp.abs(x)))
        t = lax.broadcasted_iota(jnp.int32, (lp, 128), 0)
        lane = lax.broadcasted_iota(jnp.int32, (lp, 128), 1)
        f = jnp.where((t >= PAD) & (lane < N_HEADS), logf, 0.0)
        for s in _scan_steps(lp):
            f = f + _shift_rows(f, s)
        hi = f.astype(BF16).astype(F32)
        rest = f - hi
        mid = rest.astype(BF16).astype(F32)
        lo = (rest - mid).astype(BF16).astype(F32)
        ones = jnp.where((lane >= ONE_LANE) & (lane < ONE_LANE + 3), 1.0, 0.0)
        hi_key = jnp.where((t < PAD) & (lane < N_HEADS), KEY_MASKED, hi)
        ka_ref[...] = (hi_key + pltpu.roll(mid, 8, 1) + pltpu.roll(lo, 16, 1) + ones).astype(BF16)
        for h in range(N_HEADS):
            minus = jnp.where((lane == h) | (lane == 8 + h) | (lane == 16 + h), -1.0, 0.0)
            terms = (jnp.where(lane == ONE_LANE, pltpu.roll(hi, ONE_LANE - h, 1), 0.0)
                     + jnp.where(lane == ONE_LANE + 1, pltpu.roll(mid, ONE_LANE + 1 - h, 1), 0.0)
                     + jnp.where(lane == ONE_LANE + 2, pltpu.roll(lo, ONE_LANE + 2 - h, 1), 0.0))
            qa_ref[:, 128 * h:128 * (h + 1)] = (minus + terms).astype(BF16)

    return pl.pallas_call(
        body, name="fgate_fwd", grid=(nb,),
        in_specs=[pl.BlockSpec((lp, 128), lambda b: (b, 0)), pl.BlockSpec((1, 128), lambda b: (0, 0))],
        out_specs=[pl.BlockSpec((lp, 128), lambda b: (b, 0)), pl.BlockSpec((lp, N_HEADS * 128), lambda b: (b, 0))],
        out_shape=[jax.ShapeDtypeStruct((n, 128), BF16), jax.ShapeDtypeStruct((n, N_HEADS * 128), BF16)],
        compiler_params=_params(("parallel",)),
    )(fg, b_f)


def _fgate_bwd(dka, dfr, fg, b_f, lp):
    n = fg.shape[0]
    nb = n // lp

    def body(dka_ref, dfr_ref, fg_ref, b_ref, dfg_ref, db_ref):
        b = pl.program_id(0)

        @pl.when(b == 0)
        def _():
            db_ref[...] = jnp.zeros_like(db_ref)

        wide = jnp.concatenate([dfr_ref[0], jnp.zeros((128 - N_HEADS, lp), F32)], axis=0)
        t = lax.broadcasted_iota(jnp.int32, (lp, 128), 0)
        lane = lax.broadcasted_iota(jnp.int32, (lp, 128), 1)
        d = jnp.where(lane < N_HEADS, dka_ref[...], 0.0) + wide.T
        for s in _scan_steps(lp):
            d = d + _shift_rows(d, -s)
        x = fg_ref[...] + b_ref[...]
        dx = jnp.where((t >= PAD) & (lane < N_HEADS), d * _sigmoid(-x), 0.0)
        dfg_ref[...] = dx
        db_ref[...] += jnp.sum(dx, axis=0, keepdims=True)

    return pl.pallas_call(
        body, name="fgate_bwd", grid=(nb,),
        in_specs=[pl.BlockSpec((lp, 128), lambda b: (b, 0)), pl.BlockSpec((1, N_HEADS, lp), lambda b: (b, 0, 0)),
                  pl.BlockSpec((lp, 128), lambda b: (b, 0)), pl.BlockSpec((1, 128), lambda b: (0, 0))],
        out_specs=[pl.BlockSpec((lp, 128), lambda b: (b, 0)), pl.BlockSpec((1, 128), lambda b: (0, 0))],
        out_shape=[jax.ShapeDtypeStruct((n, 128), F32), jax.ShapeDtypeStruct((1, 128), F32)],
        compiler_params=_params(("arbitrary",)),
    )(dka, dfr, fg, b_f)


def _head_masks():
    lane = lax.broadcasted_iota(jnp.int32, (1, 128), 1)
    return lane < HEAD_DIM


def _stack_heads(x2, first):
    zero = jnp.zeros_like(x2)
    return jnp.concatenate([jnp.where(first, x2, zero), jnp.where(first, zero, x2)], axis=0)


def _stack_heads_lanes(xt):
    r = lax.broadcasted_iota(jnp.int32, xt.shape, 0)
    zero = jnp.zeros_like(xt)
    return jnp.concatenate([jnp.where(r < HEAD_DIM, xt, zero), jnp.where(r < HEAD_DIM, zero, xt)], axis=1)


def _pair_cols(col0, col1, first):
    return jnp.where(first, col0, col1)


def _pair_rows(row0, row1):
    r = lax.broadcasted_iota(jnp.int32, (128, TQ), 0)
    return jnp.where(r < HEAD_DIM, row0, row1)


def _query_side(q_ref, qa_ref, p, first):
    q2 = q_ref[:, 128 * p:128 * (p + 1)] * 0.125
    zero = jnp.zeros_like(q2)
    top = jnp.concatenate([jnp.where(first, q2, zero), qa_ref[:, 128 * (2 * p):128 * (2 * p + 1)]], axis=1)
    bot = jnp.concatenate([jnp.where(first, zero, q2), qa_ref[:, 128 * (2 * p + 1):128 * (2 * p + 2)]], axis=1)
    return jnp.concatenate([top, bot], axis=0)


def _key_chunks(lp):
    return (lp + TK - 1) // TK


def _chunk_mask(i, c):
    r = lax.broadcasted_iota(jnp.int32, (TK, 2 * TQ), 0)
    col = lax.broadcasted_iota(jnp.int32, (TK, 2 * TQ), 1)
    return (c * TK + r) <= (i * TQ + (col & (TQ - 1)))


def _transpose_bf16(x):
    return x.astype(F32).T.astype(BF16)


def _attn_fwd(q, qa, k, v, ka, gain, lp):
    n = q.shape[0]
    nb = n // lp
    nq = lp // TQ
    lpp = _key_chunks(lp) * TK

    def body(q_ref, qa_ref, k_ref, v_ref, ka_ref, g_ref, z_ref, o_ref, lse_ref, kx_scr, vt_scr):
        i = pl.program_id(1)
        first = _head_masks()

        @pl.when(i == 0)
        def _():
            if lpp > lp:
                kx_scr[lp:lpp, :] = jnp.zeros((lpp - lp, 2 * ATTN_DIM), BF16)
                vt_scr[:, lp:lpp] = jnp.zeros((ATTN_DIM, lpp - lp), BF16)
            for p in range(N_PAIRS):
                kx_scr[0:lp, 256 * p:256 * p + 128] = k_ref[:, 128 * p:128 * (p + 1)]
                kx_scr[0:lp, 256 * p + 128:256 * (p + 1)] = ka_ref[...]
            vt_scr[:, 0:lp] = _transpose_bf16(v_ref[...])

        rhs_t = [_transpose_bf16(_query_side(q_ref, qa_ref, p, first)) for p in range(N_PAIRS)]

        def step(c, carry):
            koff = pl.multiple_of(c * TK, TK)
            valid = _chunk_mask(i, c)
            new = []
            for p in range(N_PAIRS):
                m, l, acc = carry[p]
                st = _dot(kx_scr[pl.ds(koff, TK), 256 * p:256 * (p + 1)], rhs_t[p])
                st = jnp.where(valid, st, NEG)
                m_new = jnp.maximum(m, jnp.max(st, axis=0, keepdims=True))
                pt = jnp.exp(st - m_new)
                alpha = jnp.exp(m - m_new)
                l = alpha * l + jnp.sum(pt, axis=0, keepdims=True)
                pb = pt.astype(BF16)
                vt = _stack_heads_lanes(vt_scr[128 * p:128 * (p + 1), pl.ds(koff, TK)])
                pv = _dot(vt, jnp.concatenate([pb[:, 0:TQ], pb[:, TQ:]], axis=0))
                acc = acc * _pair_rows(alpha[:, 0:TQ], alpha[:, TQ:]) + pv
                new.append((m_new, l, acc))
            return tuple(new)

        init = tuple((jnp.full((1, 2 * TQ), NEG, F32), jnp.zeros((1, 2 * TQ), F32), jnp.zeros((128, TQ), F32))
                     for _ in range(N_PAIRS))
        final = lax.fori_loop(0, (i + TK // TQ) // (TK // TQ), step, init)

        row = lax.broadcasted_iota(jnp.int32, (TQ, 128), 0)
        real = (i * TQ + row) >= PAD
        for p in range(N_PAIRS):
            m, l, acc = final[p]
            inv = 1.0 / l
            ot = acc * _pair_rows(inv[:, 0:TQ], inv[:, TQ:])
            sq = ot * ot
            r0 = lax.rsqrt(jnp.sum(sq[0:HEAD_DIM], axis=0, keepdims=True) * (1.0 / HEAD_DIM) + EPS)
            r1 = lax.rsqrt(jnp.sum(sq[HEAD_DIM:], axis=0, keepdims=True) * (1.0 / HEAD_DIM) + EPS)
            cols = slice(128 * p, 128 * (p + 1))
            o_ref[:, cols] = jnp.where(real, ot.T, 0.0).astype(BF16)
            z_ref[:, cols] = (jnp.where(real, (ot * _pair_rows(r0, r1)).T, 0.0) * g_ref[:, cols]).astype(BF16)
            lse = m + jnp.log(l)
            lse_ref[0, 2 * p:2 * p + 1, :] = lse[:, 0:TQ]
            lse_ref[0, 2 * p + 1:2 * p + 2, :] = lse[:, TQ:]

    qblk = pl.BlockSpec((TQ, ATTN_DIM), lambda b, i: (b * nq + i, 0))
    qablk = pl.BlockSpec((TQ, N_HEADS * 128), lambda b, i: (b * nq + i, 0))
    seq = pl.BlockSpec((lp, ATTN_DIM), lambda b, i: (b, 0))
    rowblk = pl.BlockSpec((1, N_HEADS, TQ), lambda b, i: (b, 0, i))
    return pl.pallas_call(
        body, name="attn_fwd", grid=(nb, nq),
        in_specs=[qblk, qablk, seq, seq, pl.BlockSpec((lp, 128), lambda b, i: (b, 0)),
                  pl.BlockSpec((1, ATTN_DIM), lambda b, i: (0, 0))],
        out_specs=[qblk, qblk, rowblk],
        out_shape=[jax.ShapeDtypeStruct((n, ATTN_DIM), BF16), jax.ShapeDtypeStruct((n, ATTN_DIM), BF16),
                   jax.ShapeDtypeStruct((nb, N_HEADS, lp), F32)],
        scratch_shapes=[pltpu.VMEM((lpp, 2 * ATTN_DIM), BF16), pltpu.VMEM((ATTN_DIM, lpp), BF16)],
        compiler_params=_params(("parallel", "arbitrary")),
    )(q, qa, k, v, ka, gain)


def _attn_bwd(dz, q, qa, k, v, ka, o, lse, gain, lp, exchange=()):
    n = q.shape[0]
    nb = n // lp
    nq = lp // TQ
    lpp = _key_chunks(lp) * TK
    nw = len(exchange)

    def body(*refs):
        ((dz_ref, q_ref, qa_ref, k_ref, v_ref, ka_ref, o_ref, lse_ref, g_ref), xin,
         (dq_ref, dk_ref, dv_ref, dka_ref, dfr_ref, dgain_ref), xout,
         (kx_scr, vx_scr, kt_scr, dkx_scr, dvx_scr), sems) = _split_refs(refs, 9, nw, 6, 5)
        b = pl.program_id(0)
        i = pl.program_id(1)
        first = _head_masks()
        if nw:
            comm = _Exchange(xin, xout, sems)
            pl.when((b == 0) & (i == 0))(comm.start)

        @pl.when((b == 0) & (i == 0))
        def _():
            dgain_ref[...] = jnp.zeros_like(dgain_ref)

        @pl.when(i == 0)
        def _():
            if lpp > lp:
                kx_scr[lp:lpp, :] = jnp.zeros((lpp - lp, 2 * ATTN_DIM), BF16)
                vx_scr[lp:lpp, :] = jnp.zeros((lpp - lp, ATTN_DIM), BF16)
                kt_scr[:, lp:lpp] = jnp.zeros((ATTN_DIM, lpp - lp), BF16)
            for p in range(N_PAIRS):
                kx_scr[0:lp, 256 * p:256 * p + 128] = k_ref[:, 128 * p:128 * (p + 1)]
                kx_scr[0:lp, 256 * p + 128:256 * (p + 1)] = ka_ref[...]
            vx_scr[0:lp, :] = v_ref[...]
            kt_scr[:, 0:lp] = _transpose_bf16(k_ref[...])
            dkx_scr[...] = jnp.zeros_like(dkx_scr)
            dvx_scr[...] = jnp.zeros_like(dvx_scr)

        rhs, rhs_t, lses, dos, dos_t, deltas = [], [], [], [], [], []
        for p in range(N_PAIRS):
            cols = slice(128 * p, 128 * (p + 1))
            side = _query_side(q_ref, qa_ref, p, first)
            rhs.append(side)
            rhs_t.append(_transpose_bf16(side))
            lses.append(jnp.concatenate([lse_ref[0, 2 * p:2 * p + 1, :], lse_ref[0, 2 * p + 1:2 * p + 2, :]], axis=1))
            ov = o_ref[:, cols].astype(F32)
            dzv = dz_ref[:, cols].astype(F32)
            gv = g_ref[:, cols]
            sq = ov * ov
            ms0 = jnp.sum(jnp.where(first, sq, 0.0), axis=1, keepdims=True) * (1.0 / HEAD_DIM)
            ms1 = jnp.sum(jnp.where(first, 0.0, sq), axis=1, keepdims=True) * (1.0 / HEAD_DIM)
            r = _pair_cols(lax.rsqrt(ms0 + EPS), lax.rsqrt(ms1 + EPS), first)
            ohat = ov * r
            dyhat = dzv * gv
            dgain_ref[:, cols] += jnp.sum(dzv * ohat, axis=0, keepdims=True)
            pr = dyhat * ohat
            mean0 = jnp.sum(jnp.where(first, pr, 0.0), axis=1, keepdims=True) * (1.0 / HEAD_DIM)
            mean1 = jnp.sum(jnp.where(first, 0.0, pr), axis=1, keepdims=True) * (1.0 / HEAD_DIM)
            do = r * (dyhat - ohat * _pair_cols(mean0, mean1, first))
            ddt = (do * ov).T
            deltas.append(jnp.concatenate([jnp.sum(ddt[0:HEAD_DIM], axis=0, keepdims=True),
                                           jnp.sum(ddt[HEAD_DIM:], axis=0, keepdims=True)], axis=1))
            do_st = _stack_heads(do.astype(BF16), first)
            dos.append(do_st)
            dos_t.append(_transpose_bf16(do_st))

        def step(c, carry):
            koff = pl.multiple_of(c * TK, TK)
            valid = _chunk_mask(i, c)
            new = []
            for p in range(N_PAIRS):
                dqt, dfq = carry[p]
                ext = slice(256 * p, 256 * (p + 1))
                cols = slice(128 * p, 128 * (p + 1))
                st = _dot(kx_scr[pl.ds(koff, TK), ext], rhs_t[p])
                st = jnp.where(valid, st, NEG)
                pt = jnp.exp(st - lses[p])
                dpt = _dot(vx_scr[pl.ds(koff, TK), cols], dos_t[p])
                dst = pt * (dpt - deltas[p])
                dsb = dst.astype(BF16)
                dfq = dfq + jnp.sum(dsb.astype(F32), axis=0, keepdims=True)
                dkx_scr[pl.ds(koff, TK), ext] += _dot(dsb, rhs[p])
                dvx_scr[pl.ds(koff, TK), cols] += _dot(pt.astype(BF16), dos[p])
                kt = _stack_heads_lanes(kt_scr[cols, pl.ds(koff, TK)])
                dqt = dqt + _dot(kt, jnp.concatenate([dsb[:, 0:TQ], dsb[:, TQ:]], axis=0))
                new.append((dqt, dfq))
            return tuple(new)

        init = tuple((jnp.zeros((128, TQ), F32), jnp.zeros((1, 2 * TQ), F32)) for _ in range(N_PAIRS))
        final = lax.fori_loop(0, (i + TK // TQ) // (TK // TQ), step, init)

        for p in range(N_PAIRS):
            dqt, dfq = final[p]
            dq_ref[:, 128 * p:128 * (p + 1)] = (dqt.T * 0.125).astype(BF16)
            dfr_ref[0, 2 * p:2 * p + 1, :] = dfq[:, 0:TQ]
            dfr_ref[0, 2 * p + 1:2 * p + 2, :] = dfq[:, TQ:]

        @pl.when(i == nq - 1)
        def _():
            dka = jnp.zeros((lp, 128), F32)
            for p in range(N_PAIRS):
                dk_ref[:, 128 * p:128 * (p + 1)] = dkx_scr[0:lp, 256 * p:256 * p + 128].astype(BF16)
                dka = dka + dkx_scr[0:lp, 256 * p + 128:256 * (p + 1)]
            dka_ref[...] = dka
            dv_ref[...] = dvx_scr[0:lp, :].astype(BF16)

        if nw:
            pl.when((b == nb - 1) & (i == nq - 1))(comm.finish)

    qblk = pl.BlockSpec((TQ, ATTN_DIM), lambda b, i: (b * nq + i, 0))
    qablk = pl.BlockSpec((TQ, N_HEADS * 128), lambda b, i: (b * nq + i, 0))
    seq = pl.BlockSpec((lp, ATTN_DIM), lambda b, i: (b, 0))
    kaseq = pl.BlockSpec((lp, 128), lambda b, i: (b, 0))
    rowblk = pl.BlockSpec((1, N_HEADS, TQ), lambda b, i: (b, 0, i))
    gspec = pl.BlockSpec((1, ATTN_DIM), lambda b, i: (0, 0))
    return pl.pallas_call(
        body, name="attn_bwd", grid=(nb, nq),
        in_specs=[qblk, qblk, qablk, seq, seq, kaseq, qblk, rowblk, gspec] + [ANY] * nw,
        out_specs=[qblk, seq, seq, kaseq, rowblk, gspec] + [ANY] * nw,
        out_shape=[jax.ShapeDtypeStruct((n, ATTN_DIM), BF16), jax.ShapeDtypeStruct((n, ATTN_DIM), BF16),
                   jax.ShapeDtypeStruct((n, ATTN_DIM), BF16), jax.ShapeDtypeStruct((n, 128), F32),
                   jax.ShapeDtypeStruct((nb, N_HEADS, lp), F32), jax.ShapeDtypeStruct((1, ATTN_DIM), F32)]
        + [jax.ShapeDtypeStruct(a.shape, a.dtype) for a in exchange],
        scratch_shapes=[pltpu.VMEM((lpp, 2 * ATTN_DIM), BF16), pltpu.VMEM((lpp, ATTN_DIM), BF16),
                        pltpu.VMEM((ATTN_DIM, lpp), BF16), pltpu.VMEM((lpp, 2 * ATTN_DIM), F32),
                        pltpu.VMEM((lpp, ATTN_DIM), F32)] + (_comm_sems(nw) if nw else []),
        compiler_params=_params(("arbitrary", "arbitrary")),
    )(dz, q, qa, k, v, ka, o, lse, gain, *exchange)


def _loss_head(h, gain, target, lp):
    n = h.shape[0]
    nb = n // lp
    nq = lp // 128

    def body(h_ref, g_ref, t_ref, loss_ref, dh_ref, dgain_ref):
        b = pl.program_id(0)
        i = pl.program_id(1)

        @pl.when((b == 0) & (i == 0))
        def _():
            loss_ref[...] = jnp.zeros_like(loss_ref)
            dgain_ref[...] = jnp.zeros_like(dgain_ref)

        @pl.when(i == 0)
        def _():
            dh_ref[...] = jnp.zeros_like(dh_ref)

        @pl.when(i > 0)
        def _():
            gain_v = g_ref[...]
            y, xhat, r = _rms(h_ref[...], gain_v)
            err = y - t_ref[...]
            loss_ref[...] += 0.5 * jnp.sum(jnp.sum(err * err, axis=1, keepdims=True), axis=0,
                                           keepdims=True) * (1.0 / D_MODEL)
            dy = err * (1.0 / D_MODEL)
            dh_ref[...] = _rms_bwd(dy, xhat, r, gain_v)
            dgain_ref[...] += jnp.sum(dy * xhat, axis=0, keepdims=True)

    rows = pl.BlockSpec((128, D_MODEL), lambda b, i: (b * nq + i, 0))
    trows = pl.BlockSpec((128, D_MODEL), lambda b, i: (b * (nq - 1) + jnp.maximum(i, 1) - 1, 0))
    return pl.pallas_call(
        body, name="loss_head", grid=(nb, nq),
        in_specs=[rows, pl.BlockSpec((1, D_MODEL), lambda b, i: (0, 0)), trows],
        out_specs=[pl.BlockSpec((1, 1), lambda b, i: (0, 0)), rows, pl.BlockSpec((1, D_MODEL), lambda b, i: (0, 0))],
        out_shape=[jax.ShapeDtypeStruct((1, 1), F32), jax.ShapeDtypeStruct((n, D_MODEL), F32),
                   jax.ShapeDtypeStruct((1, D_MODEL), F32)],
        compiler_params=_params(("arbitrary", "arbitrary")),
    )(h, gain, target)


def _adamw(parts, w, m, v, name):
    s_parts, r, c = parts.shape
    tr = r
    for t in (256, 128, 64, 32, 16):
        if r % t == 0 and r > t:
            tr = t
            break

    def body(p_ref, w_ref, m_ref, v_ref, g_ref, d_ref, nm_ref, nv_ref):
        g = p_ref[0].astype(F32)
        for s in range(1, s_parts):
            g = g + p_ref[s].astype(F32)
        nm = ADAM_B1 * m_ref[...] + (1.0 - ADAM_B1) * g
        nv = ADAM_B2 * v_ref[...] + (1.0 - ADAM_B2) * (g * g)
        m_hat = nm / (1.0 - ADAM_B1 ** ADAM_STEP)
        v_hat = nv / (1.0 - ADAM_B2 ** ADAM_STEP)
        g_ref[...] = g
        d_ref[...] = -ADAM_LR * (m_hat / (jnp.sqrt(v_hat) + ADAM_EPS) + ADAM_WD * w_ref[...])
        nm_ref[...] = nm
        nv_ref[...] = nv

    blk = pl.BlockSpec((tr, c), lambda i: (i, 0))
    return pl.pallas_call(
        body, name=name, grid=(r // tr,),
        in_specs=[pl.BlockSpec((s_parts, tr, c), lambda i: (0, i, 0)), blk, blk, blk],
        out_specs=[blk] * 4,
        out_shape=[jax.ShapeDtypeStruct((r, c), F32)] * 4,
        compiler_params=_params(("parallel",)),
    )(parts, w, m, v)


def _sum_parts(parts, name):
    s_parts, r, c = parts.shape

    def body(p_ref, out_ref):
        acc = p_ref[0]
        for s in range(1, s_parts):
            acc = acc + p_ref[s]
        out_ref[...] = acc

    return pl.pallas_call(
        body, name=name, out_shape=jax.ShapeDtypeStruct((r, c), F32),
        in_specs=[pl.BlockSpec(memory_space=pltpu.VMEM)], out_specs=pl.BlockSpec(memory_space=pltpu.VMEM),
    )(parts)


SMALL_ROWS = 184


def _pack_small(d_gains, d_gc, d_ga, d_bf, d_conv, d_meta):
    rows = [g.reshape(8, 128) for g in d_gains]
    rows += [d_gc.reshape(4, 128), d_ga.reshape(4, 128), d_bf.reshape(1, 128)]
    rows += [d_conv.reshape(12, 128), d_meta.reshape(128, 128)]
    packed = jnp.concatenate(rows, axis=0)
    return jnp.pad(packed, ((0, SMALL_ROWS - packed.shape[0]), (0, 0)))


def kernel(x, meta_tokens, ffn1_norm, ffn1_w_gu, ffn1_w_down, mix_norm, w_in, conv_w, b_f, out_norm_conv, out_norm_attn, w_out, ffn2_norm, ffn2_w_gu, ffn2_w_down, final_norm, loss_target, m_meta_tokens, m_ffn1_norm, m_ffn1_w_gu, m_ffn1_w_down, m_mix_norm, m_w_in, m_conv_w, m_b_f, m_out_norm_conv, m_out_norm_attn, m_w_out, m_ffn2_norm, m_ffn2_w_gu, m_ffn2_w_down, m_final_norm, v_meta_tokens, v_ffn1_norm, v_ffn1_w_gu, v_ffn1_w_down, v_mix_norm, v_w_in, v_conv_w, v_b_f, v_out_norm_conv, v_out_norm_attn, v_w_out, v_ffn2_norm, v_ffn2_w_gu, v_ffn2_w_down, v_final_norm):
    nb, seq, _ = x.shape
    lp = PAD + N_META + seq
    n = nb * lp
    me = 4 * lax.axis_index("x") + 2 * lax.axis_index("y") + lax.axis_index("c")

    wgu1_8, wd1_8 = _all_gather([ffn1_w_gu[0].T.astype(BF16), ffn1_w_down[0].astype(BF16)], "gather_ffn1")
    small_in = jnp.concatenate(
        [meta_tokens, jnp.pad(conv_w[0], ((0, 0), (0, 128 - conv_w.shape[2]))), jnp.zeros((5, 128), F32)], axis=0)
    (small_8,) = _all_gather([small_in], "gather_small")
    meta_full = small_8[:, 0:N_META, :].transpose(1, 0, 2).reshape(N_META, D_MODEL)
    conv_full = small_8[:, N_META:N_META + 3, 0:CONV_DIM // N_DEV].transpose(1, 0, 2).reshape(3, CONV_DIM)
    wgu1 = wgu1_8.reshape(W_GU_SHAPE)
    wd1 = wd1_8.reshape(W_D_SHAPE)
    b_f_row = jnp.pad(b_f, ((0, 0), (0, 128 - N_HEADS)))
    gmat = _group_matrix()

    h0 = jnp.concatenate([jnp.zeros((nb, PAD, D_MODEL), F32),
                          jnp.broadcast_to(meta_full[None], (nb, N_META, D_MODEL)), x], axis=1).reshape(n, D_MODEL)
    later = [w_in[0].T.astype(BF16), w_out[0].astype(BF16), ffn2_w_gu[0].T.astype(BF16), ffn2_w_down[0].astype(BF16)]
    h1, n1, gate1, up1, win_8, wout_8, wgu2_8, wd2_8 = _ffn_fwd(h0, ffn1_norm, wgu1, wd1, "ffn1_fwd", gather=later)
    wgu2 = wgu2_8.reshape(W_GU_SHAPE)
    wd2 = wd2_8.reshape(W_D_SHAPE)
    w_in_full = jnp.pad(win_8.reshape(IN_DIM, D_MODEL), ((0, IN_PAD - IN_DIM), (0, 0)))
    w_out_full = wout_8.reshape(D_MODEL, D_MODEL)

    bg, cg, hc, q, k, v, fg = _inproj_fwd(h1, mix_norm, w_in_full)
    zc = _conv_fwd(bg, cg, hc, conv_full, out_norm_conv, gmat, lp)
    ka, qa = _fgate_fwd(fg, b_f_row, lp)
    za, o, lse = _attn_fwd(q, qa, k, v, ka, out_norm_attn, lp)
    h2 = _outproj_fwd(zc, za, w_out_full, h1)
    h3, n3, gate2, up2 = _ffn_fwd(h2, ffn2_norm, wgu2, wd2, "ffn2_fwd")
    loss_part, dh3, d_final = _loss_head(h3, final_norm.reshape(1, D_MODEL), loss_target.reshape(nb * seq, D_MODEL), lp)

    dgate2, dup2, dwd2 = _ffn_bwd_act_wd(dh3, gate2, up2, wd2, "ffn2_bwd_act")
    dh2, d_ffn2 = _ffn_bwd_in(dh3, h2, ffn2_norm, dgate2, dup2, wgu2, "ffn2_bwd_in")
    dwgu2 = _ffn_bwd_wgu(n3, dgate2, dup2, "ffn2_bwd_wgu")
    dzc, dza, dwout = _outproj_bwd(dh2, zc, za, w_out_full)
    send_a = [dwgu2.reshape(N_DEV, F_CHUNK, D_MODEL), dwd2.reshape(N_DEV, F_CHUNK // 2, D_MODEL),
              dwout.reshape(N_DEV, D_MODEL // N_DEV, D_MODEL)]
    dq, dk, dv, dka, dfr, d_ga, p_wgu2, p_wd2, p_wout = _attn_bwd(
        dza, q, qa, k, v, ka, o, lse, out_norm_attn, lp, exchange=send_a)
    dfg, d_bf = _fgate_bwd(dka, dfr, fg, b_f_row, lp)
    dbg, dcg, dhc, d_conv, d_gc = _conv_bwd(dzc, bg, cg, hc, conv_full, out_norm_conv, gmat, lp)
    dh1, dwin, d_mix = _inproj_bwd([dbg, dcg, dhc, dq, dk, dv], dfg, dh2, h1, mix_norm, w_in_full)
    dwin_8 = dwin[0:IN_DIM].reshape(N_DEV, IN_DIM // N_DEV, D_MODEL)
    dgate1, dup1, dwd1, p_win = _ffn_bwd_act_wd(dh1, gate1, up1, wd1, "ffn1_bwd_act", exchange=[dwin_8])
    dwgu1 = _ffn_bwd_wgu(n1, dgate1, dup1, "ffn1_bwd_wgu")
    own = [dwgu1.reshape(N_DEV, F_CHUNK, D_MODEL), dwd1.reshape(N_DEV, F_CHUNK // 2, D_MODEL)]
    got = _pair_exchange(own, "pair_exchange_ffn1")
    chip_sums = [_pair_sum(own[0], got[0], "pair_sum_wgu1"), _pair_sum(own[1], got[1], "pair_sum_wd1")]
    dh0, d_ffn1, p_wgu1, p_wd1 = _ffn_bwd_in(
        dh1, h0, ffn1_norm, dgate1, dup1, wgu1, "ffn1_bwd_in", exchange=chip_sums)

    dh0 = dh0.reshape(nb, lp, D_MODEL)
    grad_x = dh0[:, PAD + N_META:, :]
    d_meta = jnp.sum(dh0[:, PAD:PAD + N_META, :], axis=0)

    small = _pack_small([d_ffn1, d_mix, d_ffn2, d_final], d_gc, d_ga, d_bf, d_conv, d_meta)
    (small_all,) = _all_gather([small], "gather_small_grads")
    small_sum = _sum_parts(small_all, "sum_small_grads")
    g_ffn1n, g_mixn, g_ffn2n, g_finaln = (small_sum[8 * t:8 * t + 8].reshape(1, D_MODEL) for t in range(4))
    g_gc = small_sum[32:36].reshape(1, CONV_DIM)
    g_ga = small_sum[36:40].reshape(1, ATTN_DIM)
    g_bf = small_sum[40:41, 0:N_HEADS]
    g_conv_full = small_sum[41:53].reshape(3, CONV_DIM)
    g_meta_full = small_sum[53:181].reshape(N_META, D_MODEL)
    g_conv = lax.dynamic_slice_in_dim(g_conv_full, me * (CONV_DIM // N_DEV), CONV_DIM // N_DEV, axis=1)
    g_meta = lax.dynamic_slice_in_dim(g_meta_full, me * (D_MODEL // N_DEV), D_MODEL // N_DEV, axis=1)

    weights = {
        "meta_tokens": (g_meta[None], meta_tokens, m_meta_tokens, v_meta_tokens),
        "ffn1_norm": (g_ffn1n[None], ffn1_norm, m_ffn1_norm, v_ffn1_norm),
        "ffn1_w_gu": (p_wgu1, ffn1_w_gu[0].T, m_ffn1_w_gu[0].T, v_ffn1_w_gu[0].T),
        "ffn1_w_down": (p_wd1, ffn1_w_down[0], m_ffn1_w_down[0], v_ffn1_w_down[0]),
        "mix_norm": (g_mixn[None], mix_norm, m_mix_norm, v_mix_norm),
        "w_in": (p_win, w_in[0].T, m_w_in[0].T, v_w_in[0].T),
        "conv_w": (g_conv[None], conv_w[0], m_conv_w[0], v_conv_w[0]),
        "b_f": (g_bf[None], b_f, m_b_f, v_b_f),
        "out_norm_conv": (g_gc[None], out_norm_conv, m_out_norm_conv, v_out_norm_conv),
        "out_norm_attn": (g_ga[None], out_norm_attn, m_out_norm_attn, v_out_norm_attn),
        "w_out": (p_wout, w_out[0], m_w_out[0], v_w_out[0]),
        "ffn2_norm": (g_ffn2n[None], ffn2_norm, m_ffn2_norm, v_ffn2_norm),
        "ffn2_w_gu": (p_wgu2, ffn2_w_gu[0].T, m_ffn2_w_gu[0].T, v_ffn2_w_gu[0].T),
        "ffn2_w_down": (p_wd2, ffn2_w_down[0], m_ffn2_w_down[0], v_ffn2_w_down[0]),
        "final_norm": (g_finaln[None], final_norm.reshape(1, D_MODEL), m_final_norm.reshape(1, D_MODEL),
                       v_final_norm.reshape(1, D_MODEL)),
    }
    shapes = {"meta_tokens": meta_tokens.shape, "ffn1_norm": ffn1_norm.shape, "ffn1_w_gu": ffn1_w_gu.shape,
              "ffn1_w_down": ffn1_w_down.shape, "mix_norm": mix_norm.shape, "w_in": w_in.shape,
              "conv_w": conv_w.shape, "b_f": b_f.shape, "out_norm_conv": out_norm_conv.shape,
              "out_norm_attn": out_norm_attn.shape, "w_out": w_out.shape, "ffn2_norm": ffn2_norm.shape,
              "ffn2_w_gu": ffn2_w_gu.shape, "ffn2_w_down": ffn2_w_down.shape, "final_norm": final_norm.shape}
    grads, deltas, new_m, new_v = [], [], [], []
    for name, (p, w, m, vv) in weights.items():
        g, d, nm, nv = _adamw(p, w, m, vv, "adamw_" + name)
        if name in ("ffn1_w_gu", "ffn2_w_gu", "w_in"):
            g, d, nm, nv = g.T, d.T, nm.T, nv.T
        shape = shapes[name]
        grads.append(g.reshape(shape))
        deltas.append(d.reshape(shape))
        new_m.append(nm.reshape(shape))
        new_v.append(nv.reshape(shape))

    loss = lax.psum(loss_part[0, 0], ("x", "y", "c"))
    return (loss, grad_x, *grads, *deltas, *new_m, *new_v)
```

```python
import jax
import jax.numpy as jnp
from jax import lax
from jax.experimental import pallas as pl
from jax.experimental.pallas import tpu as pltpu

F32 = jnp.float32
BF16 = jnp.bfloat16

N_DEV = 8
D_MODEL = 1024
N_META = 16
PAD = 128 - N_META
CONV_DIM = 512
ATTN_DIM = 512
HEAD_DIM = 64
N_HEADS = 8
N_PAIRS = N_HEADS // 2
D_FF = 2816
N_CHUNK = 4
F_CHUNK = D_FF // N_CHUNK
IN_DIM = 3080
IN_PAD = 3200
IN_MAIN = 3072
EPS = 1e-6
NEG = -1e30
TQ = 128
TK = 512
VMEM_LIMIT = 56 * 1024 * 1024

ADAM_LR = 0.001
ADAM_B1 = 0.9
ADAM_B2 = 0.999
ADAM_EPS = 1e-08
ADAM_WD = 0.01
ADAM_STEP = 10

MESH = pl.DeviceIdType.MESH
ANY = pl.BlockSpec(memory_space=pl.ANY)


def _params(sem=None):
    return pltpu.CompilerParams(dimension_semantics=sem, vmem_limit_bytes=VMEM_LIMIT)


def _row_tile(n, prefer):
    for t in (prefer, 512, 256, 128):
        if t <= n and n % t == 0:
            return t
    raise ValueError(f"no row tile for {n}")


def _dot(a, b):
    return jnp.dot(a, b, preferred_element_type=F32)


def _dot_nt(a, b):
    return lax.dot_general(a, b, (((1,), (1,)), ((), ())), preferred_element_type=F32)


def _dot_tn(a, b):
    return lax.dot_general(a, b, (((0,), (0,)), ((), ())), preferred_element_type=F32)


def _rms(x, g):
    r = lax.rsqrt(jnp.mean(x * x, axis=-1, keepdims=True) + EPS)
    xhat = x * r
    return xhat * g, xhat, r


def _rms_bwd(dn, xhat, r, g):
    dxhat = dn * g
    return r * (dxhat - xhat * jnp.mean(dxhat * xhat, axis=-1, keepdims=True))


def _sigmoid(x):
    return 1.0 / (1.0 + jnp.exp(-x))


def _place():
    return lax.axis_index("x"), lax.axis_index("y"), lax.axis_index("c")


def _comm_sems(nw):
    return [pltpu.SemaphoreType.DMA((nw, 7)), pltpu.SemaphoreType.DMA((nw, 7)), pltpu.SemaphoreType.DMA((nw,))]


class _Gather:
    def __init__(self, ins, outs, sems):
        self.ins, self.outs = ins, outs
        self.send, self.recv, self.local = sems
        x, y, c = _place()
        self.c = c
        self.me, self.sibling = (x, y, c), (x, y, 1 - c)
        self.chips = [(1 - x, y), (x, 1 - y), (1 - x, 1 - y)]

    def _copy(self, w, k, block, to, own=False):
        slot = self.outs[w].at[4 * block[0] + 2 * block[1] + block[2]]
        return pltpu.make_async_remote_copy(
            src_ref=self.ins[w] if own else slot, dst_ref=slot,
            send_sem=self.send.at[w, k], recv_sem=self.recv.at[w, k], device_id=to, device_id_type=MESH)

    def _mine(self, w):
        x, y, c = self.me
        return pltpu.make_async_copy(self.ins[w], self.outs[w].at[4 * x + 2 * y + c], self.local.at[w])

    def _first(self, w):
        return ([self._copy(w, 0, self.me, self.sibling, own=True)]
                + [self._copy(w, 1 + j, self.me, (*chip, self.c), own=True) for j, chip in enumerate(self.chips)])

    def _passed(self, w):
        return [self._copy(w, 4 + j, (*chip, self.c), self.sibling) for j, chip in enumerate(self.chips)]

    def start(self):
        for w in range(len(self.ins)):
            self._mine(w).start()
        for w in range(len(self.ins)):
            for cp in self._first(w):
                cp.start()

    def forward(self):
        for w in range(len(self.ins)):
            for j, chip in enumerate(self.chips):
                self._copy(w, 1 + j, (*chip, self.c), self.me).wait_recv()
                self._passed(w)[j].start()

    def finish(self):
        for w in range(len(self.ins)):
            self._copy(w, 0, self.sibling, self.me).wait_recv()
            for j, chip in enumerate(self.chips):
                self._copy(w, 4 + j, (*chip, 1 - self.c), self.me).wait_recv()
        for w in range(len(self.ins)):
            for cp in self._first(w) + self._passed(w):
                cp.wait_send()
            self._mine(w).wait()


class _Exchange:
    def __init__(self, ins, outs, sems):
        self.ins, self.outs = ins, outs
        self.send, self.recv, self.local = sems
        self.x, self.y, self.c = _place()
        self.me = 4 * self.x + 2 * self.y + self.c

    def _copy(self, w, k):
        flip = lambda v, bit: 1 - v if bit else v
        peer = (flip(self.x, ((k + 1) >> 2) & 1), flip(self.y, ((k + 1) >> 1) & 1), flip(self.c, (k + 1) & 1))
        return pltpu.make_async_remote_copy(
            src_ref=self.ins[w].at[4 * peer[0] + 2 * peer[1] + peer[2]], dst_ref=self.outs[w].at[self.me],
            send_sem=self.send.at[w, k], recv_sem=self.recv.at[w, k], device_id=peer, device_id_type=MESH)

    def _mine(self, w):
        return pltpu.make_async_copy(self.ins[w].at[self.me], self.outs[w].at[self.me], self.local.at[w])

    def start(self):
        for w in range(len(self.ins)):
            self._mine(w).start()
            for k in range(N_DEV - 1):
                self._copy(w, k).start()

    def finish(self):
        for w in range(len(self.ins)):
            for k in range(N_DEV - 1):
                self._copy(w, k).wait()
            self._mine(w).wait()


class _PairExchange:
    def __init__(self, ins, outs, sems):
        self.ins, self.outs = ins, outs
        self.send, self.recv, _ = sems
        x, y, self.c = _place()
        self.sibling = (x, y, 1 - self.c)

    def _copy(self, w, t):
        return pltpu.make_async_remote_copy(
            src_ref=self.ins[w].at[2 * t + 1 - self.c], dst_ref=self.outs[w].at[t],
            send_sem=self.send.at[w, t], recv_sem=self.recv.at[w, t], device_id=self.sibling, device_id_type=MESH)

    def start(self):
        for w in range(len(self.ins)):
            for t in range(4):
                self._copy(w, t).start()

    def finish(self):
        for w in range(len(self.ins)):
            for t in range(4):
                self._copy(w, t).wait()


class _ChipExchange:
    def __init__(self, ins, outs, sems):
        self.ins, self.outs = ins, outs
        self.send, self.recv, self.local = sems
        self.x, self.y, self.c = _place()
        self.chip = 2 * self.x + self.y

    def _copy(self, w, k):
        flip = lambda v, bit: 1 - v if bit else v
        px, py = flip(self.x, ((k + 1) >> 1) & 1), flip(self.y, (k + 1) & 1)
        return pltpu.make_async_remote_copy(
            src_ref=self.ins[w].at[2 * px + py], dst_ref=self.outs[w].at[self.chip],
            send_sem=self.send.at[w, k], recv_sem=self.recv.at[w, k], device_id=(px, py, self.c),
            device_id_type=MESH)

    def _mine(self, w):
        return pltpu.make_async_copy(self.ins[w].at[self.chip], self.outs[w].at[self.chip], self.local.at[w])

    def start(self):
        for w in range(len(self.ins)):
            self._mine(w).start()
            for k in range(3):
                self._copy(w, k).start()

    def finish(self):
        for w in range(len(self.ins)):
            for k in range(3):
                self._copy(w, k).wait()
            self._mine(w).wait()


def _pair_exchange(xs, name):
    nw = len(xs)

    def body(*refs):
        comm = _PairExchange(refs[:nw], refs[nw:2 * nw], refs[2 * nw:])
        comm.start()
        comm.finish()

    return pl.pallas_call(
        body, name=name, in_specs=[ANY] * nw, out_specs=[ANY] * nw,
        out_shape=[jax.ShapeDtypeStruct((4,) + a.shape[1:], a.dtype) for a in xs],
        scratch_shapes=_comm_sems(nw),
    )(*xs)


def _pair_sum(own, got, name):
    _, r, c = own.shape
    tr = r
    for t in (256, 128, 64, 32, 16):
        if r % t == 0 and r > t:
            tr = t
            break

    def body(own_ref, got_ref, out_ref):
        mine = jnp.where(lax.axis_index("c") == 0, own_ref[:, 0].astype(F32), own_ref[:, 1].astype(F32))
        out_ref[...] = (mine + got_ref[...].astype(F32)).astype(BF16)

    return pl.pallas_call(
        body, name=name, grid=(r // tr,),
        in_specs=[pl.BlockSpec((4, 2, tr, c), lambda i: (0, 0, i, 0)), pl.BlockSpec((4, tr, c), lambda i: (0, i, 0))],
        out_specs=pl.BlockSpec((4, tr, c), lambda i: (0, i, 0)),
        out_shape=jax.ShapeDtypeStruct((4, r, c), BF16),
        compiler_params=_params(("parallel",)),
    )(own.reshape(4, 2, r, c), got)


def _split_refs(refs, n_in, n_comm, n_out, n_scr):
    a = n_in
    b = a + n_comm
    c = b + n_out
    d = c + n_comm
    e = d + n_scr
    return refs[:a], refs[a:b], refs[b:c], refs[c:d], refs[d:e], refs[e:]


def _all_gather(xs, name):
    nw = len(xs)

    def body(*refs):
        comm = _Gather(refs[:nw], refs[nw:2 * nw], refs[2 * nw:])
        comm.start()
        comm.forward()
        comm.finish()

    return pl.pallas_call(
        body, name=name, in_specs=[ANY] * nw, out_specs=[ANY] * nw,
        out_shape=[jax.ShapeDtypeStruct((N_DEV,) + a.shape, a.dtype) for a in xs],
        scratch_shapes=_comm_sems(nw),
    )(*xs)


def _ffn_fwd(h, gain, wgu, wd, name, gather=()):
    n = h.shape[0]
    tm = _row_tile(n, 512)
    n_i = n // tm
    nw = len(gather)

    def body(*refs):
        (h_ref, g_ref, wgu_ref, wd_ref), gin, (out_ref, gate_ref, up_ref), gout, (n_scr, acc_scr), sems = \
            _split_refs(refs, 4, nw, 3, 2)
        i = pl.program_id(0)
        j = pl.program_id(1)
        if nw:
            comm = _Gather(gin, gout, sems)
            pl.when((i == 0) & (j == 0))(comm.start)
            pl.when((i == (3 * n_i) // 4) & (j == 0))(comm.forward)

        @pl.when(j == 0)
        def _():
            y, _, _ = _rms(h_ref[...], g_ref[...])
            n_scr[...] = y.astype(BF16)
            acc_scr[...] = jnp.zeros_like(acc_scr)

        nb = n_scr[...]
        gate = _dot(nb, wgu_ref[0, 0])
        up = _dot(nb, wgu_ref[1, 0])
        gate_ref[0] = gate.astype(BF16)
        up_ref[0] = up.astype(BF16)
        act = (gate * _sigmoid(gate) * up).astype(BF16)
        acc_scr[...] += _dot(act, wd_ref[0])

        @pl.when(j == N_CHUNK - 1)
        def _():
            out_ref[...] = h_ref[...] + 0.5 * acc_scr[...]

        if nw:
            pl.when((i == n_i - 1) & (j == N_CHUNK - 1))(comm.finish)

    return pl.pallas_call(
        body, name=name, grid=(n_i, N_CHUNK),
        in_specs=[pl.BlockSpec((tm, D_MODEL), lambda i, j: (i, 0)),
                  pl.BlockSpec((1, D_MODEL), lambda i, j: (0, 0)),
                  pl.BlockSpec((2, 1, D_MODEL, F_CHUNK), lambda i, j: (0, j, 0, 0)),
                  pl.BlockSpec((1, F_CHUNK, D_MODEL), lambda i, j: (j, 0, 0))] + [ANY] * nw,
        out_specs=[pl.BlockSpec((tm, D_MODEL), lambda i, j: (i, 0)),
                   pl.BlockSpec((1, tm, F_CHUNK), lambda i, j: (j, i, 0)),
                   pl.BlockSpec((1, tm, F_CHUNK), lambda i, j: (j, i, 0))] + [ANY] * nw,
        out_shape=[jax.ShapeDtypeStruct((n, D_MODEL), F32),
                   jax.ShapeDtypeStruct((N_CHUNK, n, F_CHUNK), BF16),
                   jax.ShapeDtypeStruct((N_CHUNK, n, F_CHUNK), BF16)]
        + [jax.ShapeDtypeStruct((N_DEV,) + a.shape, a.dtype) for a in gather],
        scratch_shapes=[pltpu.VMEM((tm, D_MODEL), BF16), pltpu.VMEM((tm, D_MODEL), F32)]
        + (_comm_sems(nw) if nw else []),
        compiler_params=_params(("arbitrary", "arbitrary")),
    )(h, gain, wgu, wd, *gather)


def _ffn_bwd_x(dh_out, h_in, gain, gate, up, wgu, wd, name):
    n = h_in.shape[0]
    tm = _row_tile(n, 512)

    def body(dh_ref, h_ref, g_ref, gate_ref, up_ref, wgu_ref, wd_ref,
             dhin_ref, dgate_ref, dup_ref, dgain_ref, dhb_scr, acc_scr):
        i = pl.program_id(0)
        j = pl.program_id(1)

        @pl.when((i == 0) & (j == 0))
        def _():
            dgain_ref[...] = jnp.zeros_like(dgain_ref)

        @pl.when(j == 0)
        def _():
            dhb_scr[...] = (0.5 * dh_ref[...]).astype(BF16)
            acc_scr[...] = jnp.zeros_like(acc_scr)

        da = _dot_nt(dhb_scr[...], wd_ref[0])
        g = gate_ref[0].astype(F32)
        u = up_ref[0].astype(F32)
        sig = _sigmoid(g)
        dgate = (da * u * (sig * (1.0 + g * (1.0 - sig)))).astype(BF16)
        dup = (da * (g * sig)).astype(BF16)
        dgate_ref[0] = dgate
        dup_ref[0] = dup
        acc_scr[...] += _dot_nt(dgate, wgu_ref[0, 0]) + _dot_nt(dup, wgu_ref[1, 0])

        @pl.when(j == N_CHUNK - 1)
        def _():
            gain_v = g_ref[...]
            _, xhat, r = _rms(h_ref[...], gain_v)
            dn = acc_scr[...]
            dhin_ref[...] = dh_ref[...] + _rms_bwd(dn, xhat, r, gain_v)
            dgain_ref[...] += jnp.sum(dn * xhat, axis=0, keepdims=True)

    chunk = pl.BlockSpec((1, tm, F_CHUNK), lambda i, j: (j, i, 0))
    rows = pl.BlockSpec((tm, D_MODEL), lambda i, j: (i, 0))
    vec = pl.BlockSpec((1, D_MODEL), lambda i, j: (0, 0))
    return pl.pallas_call(
        body, name=name, grid=(n // tm, N_CHUNK),
        in_specs=[rows, rows, vec, chunk, chunk,
                  pl.BlockSpec((2, 1, D_MODEL, F_CHUNK), lambda i, j: (0, j, 0, 0)),
                  pl.BlockSpec((1, F_CHUNK, D_MODEL), lambda i, j: (j, 0, 0))],
        out_specs=[rows, chunk, chunk, vec],
        out_shape=[jax.ShapeDtypeStruct((n, D_MODEL), F32),
                   jax.ShapeDtypeStruct((N_CHUNK, n, F_CHUNK), BF16),
                   jax.ShapeDtypeStruct((N_CHUNK, n, F_CHUNK), BF16),
                   jax.ShapeDtypeStruct((1, D_MODEL), F32)],
        scratch_shapes=[pltpu.VMEM((tm, D_MODEL), BF16), pltpu.VMEM((tm, D_MODEL), F32)],
        compiler_params=_params(("arbitrary", "arbitrary")),
    )(dh_out, h_in, gain, gate, up, wgu, wd)


def _ffn_bwd_act(dh_out, gate, up, wd, name):
    n = dh_out.shape[0]
    tm = _row_tile(n, 512)

    def body(dh_ref, gate_ref, up_ref, wd_ref, dgate_ref, dup_ref, dhb_scr):
        @pl.when(pl.program_id(1) == 0)
        def _():
            dhb_scr[...] = (0.5 * dh_ref[...]).astype(BF16)

        da = _dot_nt(dhb_scr[...], wd_ref[0])
        g = gate_ref[0].astype(F32)
        u = up_ref[0].astype(F32)
        sig = _sigmoid(g)
        dgate_ref[0] = (da * u * (sig * (1.0 + g * (1.0 - sig)))).astype(BF16)
        dup_ref[0] = (da * (g * sig)).astype(BF16)

    chunk = pl.BlockSpec((1, tm, F_CHUNK), lambda i, j: (j, i, 0))
    return pl.pallas_call(
        body, name=name, grid=(n // tm, N_CHUNK),
        in_specs=[pl.BlockSpec((tm, D_MODEL), lambda i, j: (i, 0)), chunk, chunk,
                  pl.BlockSpec((1, F_CHUNK, D_MODEL), lambda i, j: (j, 0, 0))],
        out_specs=[chunk, chunk],
        out_shape=[jax.ShapeDtypeStruct((N_CHUNK, n, F_CHUNK), BF16)] * 2,
        scratch_shapes=[pltpu.VMEM((tm, D_MODEL), BF16)],
        compiler_params=_params(("parallel", "arbitrary")),
    )(dh_out, gate, up, wd)


def _ffn_bwd_in(dh_out, h_in, gain, dgate, dup, wgu, name, exchange=()):
    n = h_in.shape[0]
    tm = _row_tile(n, 512)
    n_i = n // tm
    nw = len(exchange)

    def body(*refs):
        (dh_ref, h_ref, g_ref, dgate_ref, dup_ref, wgu_ref), xin, (dhin_ref, dgain_ref), xout, (acc_scr,), sems = \
            _split_refs(refs, 6, nw, 2, 1)
        i = pl.program_id(0)
        j = pl.program_id(1)
        if nw:
            comm = _Exchange(xin, xout, sems)
            pl.when((i == 0) & (j == 0))(comm.start)

        @pl.when((i == 0) & (j == 0))
        def _():
            dgain_ref[...] = jnp.zeros_like(dgain_ref)

        @pl.when(j == 0)
        def _():
            acc_scr[...] = jnp.zeros_like(acc_scr)

        acc_scr[...] += _dot_nt(dgate_ref[0], wgu_ref[0, 0]) + _dot_nt(dup_ref[0], wgu_ref[1, 0])

        @pl.when(j == N_CHUNK - 1)
        def _():
            gain_v = g_ref[...]
            _, xhat, r = _rms(h_ref[...], gain_v)
            dn = acc_scr[...]
            dhin_ref[...] = dh_ref[...] + _rms_bwd(dn, xhat, r, gain_v)
            dgain_ref[...] += jnp.sum(dn * xhat, axis=0, keepdims=True)

        if nw:
            pl.when((i == n_i - 1) & (j == N_CHUNK - 1))(comm.finish)

    chunk = pl.BlockSpec((1, tm, F_CHUNK), lambda i, j: (j, i, 0))
    rows = pl.BlockSpec((tm, D_MODEL), lambda i, j: (i, 0))
    vec = pl.BlockSpec((1, D_MODEL), lambda i, j: (0, 0))
    return pl.pallas_call(
        body, name=name, grid=(n_i, N_CHUNK),
        in_specs=[rows, rows, vec, chunk, chunk,
                  pl.BlockSpec((2, 1, D_MODEL, F_CHUNK), lambda i, j: (0, j, 0, 0))] + [ANY] * nw,
        out_specs=[rows, vec] + [ANY] * nw,
        out_shape=[jax.ShapeDtypeStruct((n, D_MODEL), F32), jax.ShapeDtypeStruct((1, D_MODEL), F32)]
        + [jax.ShapeDtypeStruct(a.shape, a.dtype) for a in exchange],
        scratch_shapes=[pltpu.VMEM((tm, D_MODEL), F32)] + (_comm_sems(nw) if nw else []),
        compiler_params=_params(("arbitrary", "arbitrary")),
    )(dh_out, h_in, gain, dgate, dup, wgu, *exchange)


def _ffn_bwd_w(dh_out, h_in, gain, gate, up, dgate, dup, name):
    n = h_in.shape[0]
    tm = _row_tile(n, 512)
    n_i = n // tm

    def body(dh_ref, h_ref, g_ref, gate_ref, up_ref, dgate_ref, dup_ref, dwgu_ref, dwd_ref,
             ag_scr, au_scr, ad_scr):
        i = pl.program_id(1)

        @pl.when(i == 0)
        def _():
            ag_scr[...] = jnp.zeros_like(ag_scr)
            au_scr[...] = jnp.zeros_like(au_scr)
            ad_scr[...] = jnp.zeros_like(ad_scr)

        y, _, _ = _rms(h_ref[...], g_ref[...])
        nb = y.astype(BF16)
        ag_scr[...] += _dot_tn(nb, dgate_ref[0])
        au_scr[...] += _dot_tn(nb, dup_ref[0])
        g = gate_ref[0].astype(F32)
        act = (g * _sigmoid(g) * up_ref[0].astype(F32)).astype(BF16)
        ad_scr[...] += _dot_tn(act, (0.5 * dh_ref[...]).astype(BF16))

        @pl.when(i == n_i - 1)
        def _():
            dwgu_ref[0, 0] = ag_scr[...].astype(BF16)
            dwgu_ref[1, 0] = au_scr[...].astype(BF16)
            dwd_ref[0] = ad_scr[...].astype(BF16)

    chunk = pl.BlockSpec((1, tm, F_CHUNK), lambda j, i: (j, i, 0))
    rows = pl.BlockSpec((tm, D_MODEL), lambda j, i: (i, 0))
    return pl.pallas_call(
        body, name=name, grid=(N_CHUNK, n_i),
        in_specs=[rows, rows, pl.BlockSpec((1, D_MODEL), lambda j, i: (0, 0)), chunk, chunk, chunk, chunk],
        out_specs=[pl.BlockSpec((2, 1, D_MODEL, F_CHUNK), lambda j, i: (0, j, 0, 0)),
                   pl.BlockSpec((1, F_CHUNK, D_MODEL), lambda j, i: (j, 0, 0))],
        out_shape=[jax.ShapeDtypeStruct((2, N_CHUNK, D_MODEL, F_CHUNK), BF16),
                   jax.ShapeDtypeStruct((N_CHUNK, F_CHUNK, D_MODEL), BF16)],
        scratch_shapes=[pltpu.VMEM((D_MODEL, F_CHUNK), F32), pltpu.VMEM((D_MODEL, F_CHUNK), F32),
                        pltpu.VMEM((F_CHUNK, D_MODEL), F32)],
        compiler_params=_params(("parallel", "arbitrary")),
    )(dh_out, h_in, gain, gate, up, dgate, dup)


def _resident(shape, rank):
    zeros = (0,) * len(shape)
    index_map = (lambda i: zeros) if rank == 1 else (lambda i, j: zeros)
    return pl.BlockSpec(shape, index_map, pipeline_mode=pl.Buffered(1))


W_GU_SHAPE = (2, N_CHUNK, F_CHUNK, D_MODEL)
W_D_SHAPE = (N_CHUNK, F_CHUNK, D_MODEL)


def _ffn_fwd(h, gain, wgu, wd, name, gather=()):
    n = h.shape[0]
    tm = _row_tile(n, 512)
    n_i = n // tm
    nw = len(gather)

    def body(*refs):
        (h_ref, g_ref, wgu_ref, wd_ref), gin, (out_ref, nrm_ref, gate_ref, up_ref), gout, _, sems = \
            _split_refs(refs, 4, nw, 4, 0)
        i = pl.program_id(0)
        if nw:
            comm = _Gather(gin, gout, sems)
            pl.when(i == 0)(comm.start)
            pl.when(i == max(n_i - 3, 0))(comm.forward)

        hv = h_ref[...]
        y, _, _ = _rms(hv, g_ref[...])
        nb = y.astype(BF16)
        nrm_ref[...] = nb
        acc = jnp.zeros((tm, D_MODEL), F32)
        for j in range(N_CHUNK):
            gate = _dot_nt(nb, wgu_ref[0, j])
            up = _dot_nt(nb, wgu_ref[1, j])
            gate_ref[j] = gate.astype(BF16)
            up_ref[j] = up.astype(BF16)
            acc = acc + _dot((gate * _sigmoid(gate) * up).astype(BF16), wd_ref[j])
        out_ref[...] = hv + 0.5 * acc

        if nw:
            pl.when(i == n_i - 1)(comm.finish)

    rows = pl.BlockSpec((tm, D_MODEL), lambda i: (i, 0))
    chunks = pl.BlockSpec((N_CHUNK, tm, F_CHUNK), lambda i: (0, i, 0))
    return pl.pallas_call(
        body, name=name, grid=(n_i,),
        in_specs=[rows, pl.BlockSpec((1, D_MODEL), lambda i: (0, 0)), _resident(W_GU_SHAPE, 1),
                  _resident(W_D_SHAPE, 1)] + [ANY] * nw,
        out_specs=[rows, rows, chunks, chunks] + [ANY] * nw,
        out_shape=[jax.ShapeDtypeStruct((n, D_MODEL), F32), jax.ShapeDtypeStruct((n, D_MODEL), BF16),
                   jax.ShapeDtypeStruct((N_CHUNK, n, F_CHUNK), BF16),
                   jax.ShapeDtypeStruct((N_CHUNK, n, F_CHUNK), BF16)]
        + [jax.ShapeDtypeStruct((N_DEV,) + a.shape, a.dtype) for a in gather],
        scratch_shapes=_comm_sems(nw) if nw else [],
        compiler_params=_params(("arbitrary",)),
    )(h, gain, wgu, wd, *gather)


def _swiglu_bwd(da, gate_ref, up_ref, j):
    g = gate_ref[j].astype(F32)
    u = up_ref[j].astype(F32)
    sig = _sigmoid(g)
    return (da * u * (sig * (1.0 + g * (1.0 - sig)))).astype(BF16), (da * (g * sig)).astype(BF16)


def _ffn_bwd_x(dh_out, h_in, gain, gate, up, wgu, wd, name):
    n = h_in.shape[0]
    tm = _row_tile(n, 256)

    def body(dh_ref, h_ref, g_ref, gate_ref, up_ref, wgu_ref, wd_ref,
             dhin_ref, dhb_ref, dgate_ref, dup_ref, dgain_ref):
        @pl.when(pl.program_id(0) == 0)
        def _():
            dgain_ref[...] = jnp.zeros_like(dgain_ref)

        dhv = dh_ref[...]
        dhb = (0.5 * dhv).astype(BF16)
        dhb_ref[...] = dhb
        dn = jnp.zeros((tm, D_MODEL), F32)
        for j in range(N_CHUNK):
            dgate, dup = _swiglu_bwd(_dot_nt(dhb, wd_ref[j]), gate_ref, up_ref, j)
            dgate_ref[j] = dgate
            dup_ref[j] = dup
            dn = dn + _dot(dgate, wgu_ref[0, j]) + _dot(dup, wgu_ref[1, j])
        gain_v = g_ref[...]
        _, xhat, r = _rms(h_ref[...], gain_v)
        dhin_ref[...] = dhv + _rms_bwd(dn, xhat, r, gain_v)
        dgain_ref[...] += jnp.sum(dn * xhat, axis=0, keepdims=True)

    rows = pl.BlockSpec((tm, D_MODEL), lambda i: (i, 0))
    chunks = pl.BlockSpec((N_CHUNK, tm, F_CHUNK), lambda i: (0, i, 0))
    vec = pl.BlockSpec((1, D_MODEL), lambda i: (0, 0))
    return pl.pallas_call(
        body, name=name, grid=(n // tm,),
        in_specs=[rows, rows, vec, chunks, chunks, _resident(W_GU_SHAPE, 1), _resident(W_D_SHAPE, 1)],
        out_specs=[rows, rows, chunks, chunks, vec],
        out_shape=[jax.ShapeDtypeStruct((n, D_MODEL), F32), jax.ShapeDtypeStruct((n, D_MODEL), BF16),
                   jax.ShapeDtypeStruct((N_CHUNK, n, F_CHUNK), BF16),
                   jax.ShapeDtypeStruct((N_CHUNK, n, F_CHUNK), BF16),
                   jax.ShapeDtypeStruct((1, D_MODEL), F32)],
        compiler_params=_params(("arbitrary",)),
    )(dh_out, h_in, gain, gate, up, wgu, wd)


def _ffn_bwd_act(dh_out, gate, up, wd, name, exchange=()):
    n = dh_out.shape[0]
    tm = _row_tile(n, 512)
    n_i = n // tm
    nw = len(exchange)

    def body(*refs):
        (dh_ref, gate_ref, up_ref, wd_ref), xin, (dhb_ref, dgate_ref, dup_ref), xout, _, sems = \
            _split_refs(refs, 4, nw, 3, 0)
        i = pl.program_id(0)
        if nw:
            comm = _Exchange(xin, xout, sems)
            pl.when(i == 0)(comm.start)

        dhb = (0.5 * dh_ref[...]).astype(BF16)
        dhb_ref[...] = dhb
        for j in range(N_CHUNK):
            dgate_ref[j], dup_ref[j] = _swiglu_bwd(_dot_nt(dhb, wd_ref[j]), gate_ref, up_ref, j)

        if nw:
            pl.when(i == n_i - 1)(comm.finish)

    rows = pl.BlockSpec((tm, D_MODEL), lambda i: (i, 0))
    chunks = pl.BlockSpec((N_CHUNK, tm, F_CHUNK), lambda i: (0, i, 0))
    return pl.pallas_call(
        body, name=name, grid=(n_i,),
        in_specs=[rows, chunks, chunks, _resident(W_D_SHAPE, 1)] + [ANY] * nw,
        out_specs=[rows, chunks, chunks] + [ANY] * nw,
        out_shape=[jax.ShapeDtypeStruct((n, D_MODEL), BF16)] + [jax.ShapeDtypeStruct((N_CHUNK, n, F_CHUNK), BF16)] * 2
        + [jax.ShapeDtypeStruct(a.shape, a.dtype) for a in exchange],
        scratch_shapes=_comm_sems(nw) if nw else [],
        compiler_params=_params(("arbitrary",)),
    )(dh_out, gate, up, wd, *exchange)


def _ffn_bwd_in(dh_out, h_in, gain, dgate, dup, wgu, name, exchange=()):
    n = h_in.shape[0]
    tm = _row_tile(n, 512)
    n_i = n // tm
    nw = len(exchange)

    def body(*refs):
        (dh_ref, h_ref, g_ref, dgate_ref, dup_ref, wgu_ref), xin, (dhin_ref, dgain_ref), xout, _, sems = \
            _split_refs(refs, 6, nw, 2, 0)
        i = pl.program_id(0)
        if nw:
            comm = _ChipExchange(xin, xout, sems)
            pl.when(i == 0)(comm.start)

        @pl.when(i == 0)
        def _():
            dgain_ref[...] = jnp.zeros_like(dgain_ref)

        dn = jnp.zeros((tm, D_MODEL), F32)
        for j in range(N_CHUNK):
            dn = dn + _dot(dgate_ref[j], wgu_ref[0, j]) + _dot(dup_ref[j], wgu_ref[1, j])
        gain_v = g_ref[...]
        _, xhat, r = _rms(h_ref[...], gain_v)
        dhin_ref[...] = dh_ref[...] + _rms_bwd(dn, xhat, r, gain_v)
        dgain_ref[...] += jnp.sum(dn * xhat, axis=0, keepdims=True)

        if nw:
            pl.when(i == n_i - 1)(comm.finish)

    rows = pl.BlockSpec((tm, D_MODEL), lambda i: (i, 0))
    chunks = pl.BlockSpec((N_CHUNK, tm, F_CHUNK), lambda i: (0, i, 0))
    vec = pl.BlockSpec((1, D_MODEL), lambda i: (0, 0))
    return pl.pallas_call(
        body, name=name, grid=(n_i,),
        in_specs=[rows, rows, vec, chunks, chunks, _resident(W_GU_SHAPE, 1)] + [ANY] * nw,
        out_specs=[rows, vec] + [ANY] * nw,
        out_shape=[jax.ShapeDtypeStruct((n, D_MODEL), F32), jax.ShapeDtypeStruct((1, D_MODEL), F32)]
        + [jax.ShapeDtypeStruct(a.shape, a.dtype) for a in exchange],
        scratch_shapes=_comm_sems(nw) if nw else [],
        compiler_params=_params(("arbitrary",)),
    )(dh_out, h_in, gain, dgate, dup, wgu, *exchange)


W_GROUP = 2


def _ffn_bwd_w(dhb, nrm, gate, up, dgate, dup, name):
    n = nrm.shape[0]
    tm = _row_tile(n, 512)
    n_i = n // tm

    def body(dhb_ref, nrm_ref, gate_ref, up_ref, dgate_ref, dup_ref, dwgu_ref, dwd_ref, ag_scr, au_scr, ad_scr):
        i = pl.program_id(1)

        @pl.when(i == 0)
        def _():
            ag_scr[...] = jnp.zeros_like(ag_scr)
            au_scr[...] = jnp.zeros_like(au_scr)
            ad_scr[...] = jnp.zeros_like(ad_scr)

        nb = nrm_ref[...]
        dhv = dhb_ref[...]
        for jj in range(W_GROUP):
            ag_scr[jj] += _dot_tn(dgate_ref[jj], nb)
            au_scr[jj] += _dot_tn(dup_ref[jj], nb)
            g = gate_ref[jj].astype(F32)
            act = (g * _sigmoid(g) * up_ref[jj].astype(F32)).astype(BF16)
            ad_scr[jj] += _dot_tn(act, dhv)

        @pl.when(i == n_i - 1)
        def _():
            dwgu_ref[0] = ag_scr[...].astype(BF16)
            dwgu_ref[1] = au_scr[...].astype(BF16)
            dwd_ref[...] = ad_scr[...].astype(BF16)

    chunks = pl.BlockSpec((W_GROUP, tm, F_CHUNK), lambda g, i: (g, i, 0))
    rows = pl.BlockSpec((tm, D_MODEL), lambda g, i: (i, 0))
    return pl.pallas_call(
        body, name=name, grid=(N_CHUNK // W_GROUP, n_i),
        in_specs=[rows, rows, chunks, chunks, chunks, chunks],
        out_specs=[pl.BlockSpec((2, W_GROUP, F_CHUNK, D_MODEL), lambda g, i: (0, g, 0, 0)),
                   pl.BlockSpec((W_GROUP, F_CHUNK, D_MODEL), lambda g, i: (g, 0, 0))],
        out_shape=[jax.ShapeDtypeStruct(W_GU_SHAPE, BF16), jax.ShapeDtypeStruct(W_D_SHAPE, BF16)],
        scratch_shapes=[pltpu.VMEM((W_GROUP, F_CHUNK, D_MODEL), F32), pltpu.VMEM((W_GROUP, F_CHUNK, D_MODEL), F32),
                        pltpu.VMEM((W_GROUP, F_CHUNK, D_MODEL), F32)],
        compiler_params=_params(("parallel", "arbitrary")),
    )(dhb, nrm, gate, up, dgate, dup)


HID_PIECES = ((0, 1024), (1024, 2048), (2048, D_FF))
W_GU_SHAPE = (2, D_FF, D_MODEL)
W_D_SHAPE = (D_FF, D_MODEL)


def _ffn_fwd(h, gain, wgu, wd, name, gather=()):
    n = h.shape[0]
    tm = _row_tile(n, 512)
    n_i = n // tm
    nw = len(gather)

    def body(*refs):
        (h_ref, g_ref, wgu_ref, wd_ref), gin, (out_ref, nrm_ref, gate_ref, up_ref), gout, _, sems = \
            _split_refs(refs, 4, nw, 4, 0)
        i = pl.program_id(0)
        if nw:
            comm = _Gather(gin, gout, sems)
            pl.when(i == 0)(comm.start)
            pl.when(i == max(n_i - 3, 0))(comm.forward)

        hv = h_ref[...]
        y, _, _ = _rms(hv, g_ref[...])
        nb = y.astype(BF16)
        nrm_ref[...] = nb
        acc = jnp.zeros((tm, D_MODEL), F32)
        for a, b in HID_PIECES:
            gate = _dot_nt(nb, wgu_ref[0, a:b, :])
            up = _dot_nt(nb, wgu_ref[1, a:b, :])
            gate_ref[:, a:b] = gate.astype(BF16)
            up_ref[:, a:b] = up.astype(BF16)
            acc = acc + _dot((gate * _sigmoid(gate) * up).astype(BF16), wd_ref[a:b, :])
        out_ref[...] = hv + 0.5 * acc

        if nw:
            pl.when(i == n_i - 1)(comm.finish)

    rows = pl.BlockSpec((tm, D_MODEL), lambda i: (i, 0))
    hid = pl.BlockSpec((tm, D_FF), lambda i: (i, 0))
    return pl.pallas_call(
        body, name=name, grid=(n_i,),
        in_specs=[rows, pl.BlockSpec((1, D_MODEL), lambda i: (0, 0)), _resident(W_GU_SHAPE, 1),
                  _resident(W_D_SHAPE, 1)] + [ANY] * nw,
        out_specs=[rows, rows, hid, hid] + [ANY] * nw,
        out_shape=[jax.ShapeDtypeStruct((n, D_MODEL), F32), jax.ShapeDtypeStruct((n, D_MODEL), BF16),
                   jax.ShapeDtypeStruct((n, D_FF), BF16), jax.ShapeDtypeStruct((n, D_FF), BF16)]
        + [jax.ShapeDtypeStruct((N_DEV,) + a.shape, a.dtype) for a in gather],
        scratch_shapes=_comm_sems(nw) if nw else [],
        compiler_params=_params(("arbitrary",)),
    )(h, gain, wgu, wd, *gather)


def _swiglu_bwd(da, gate_ref, up_ref, a, b):
    g = gate_ref[:, a:b].astype(F32)
    u = up_ref[:, a:b].astype(F32)
    sig = _sigmoid(g)
    return (da * u * (sig * (1.0 + g * (1.0 - sig)))).astype(BF16), (da * (g * sig)).astype(BF16)


def _ffn_bwd_x(dh_out, h_in, gain, gate, up, wgu, wd, name):
    n = h_in.shape[0]
    tm = _row_tile(n, 256)

    def body(dh_ref, h_ref, g_ref, gate_ref, up_ref, wgu_ref, wd_ref,
             dhin_ref, dhb_ref, dgate_ref, dup_ref, dgain_ref):
        @pl.when(pl.program_id(0) == 0)
        def _():
            dgain_ref[...] = jnp.zeros_like(dgain_ref)

        dhv = dh_ref[...]
        dhb = (0.5 * dhv).astype(BF16)
        dhb_ref[...] = dhb
        dn = jnp.zeros((tm, D_MODEL), F32)
        for a, b in HID_PIECES:
            dgate, dup = _swiglu_bwd(_dot_nt(dhb, wd_ref[a:b, :]), gate_ref, up_ref, a, b)
            dgate_ref[:, a:b] = dgate
            dup_ref[:, a:b] = dup
            dn = dn + _dot(dgate, wgu_ref[0, a:b, :]) + _dot(dup, wgu_ref[1, a:b, :])
        gain_v = g_ref[...]
        _, xhat, r = _rms(h_ref[...], gain_v)
        dhin_ref[...] = dhv + _rms_bwd(dn, xhat, r, gain_v)
        dgain_ref[...] += jnp.sum(dn * xhat, axis=0, keepdims=True)

    rows = pl.BlockSpec((tm, D_MODEL), lambda i: (i, 0))
    hid = pl.BlockSpec((tm, D_FF), lambda i: (i, 0))
    vec = pl.BlockSpec((1, D_MODEL), lambda i: (0, 0))
    return pl.pallas_call(
        body, name=name, grid=(n // tm,),
        in_specs=[rows, rows, vec, hid, hid, _resident(W_GU_SHAPE, 1), _resident(W_D_SHAPE, 1)],
        out_specs=[rows, rows, hid, hid, vec],
        out_shape=[jax.ShapeDtypeStruct((n, D_MODEL), F32), jax.ShapeDtypeStruct((n, D_MODEL), BF16),
                   jax.ShapeDtypeStruct((n, D_FF), BF16), jax.ShapeDtypeStruct((n, D_FF), BF16),
                   jax.ShapeDtypeStruct((1, D_MODEL), F32)],
        compiler_params=_params(("arbitrary",)),
    )(dh_out, h_in, gain, gate, up, wgu, wd)


def _ffn_bwd_act(dh_out, gate, up, wd, name, exchange=()):
    n = dh_out.shape[0]
    tm = _row_tile(n, 512)
    n_i = n // tm
    nw = len(exchange)

    def body(*refs):
        (dh_ref, gate_ref, up_ref, wd_ref), xin, (dhb_ref, dgate_ref, dup_ref), xout, _, sems = \
            _split_refs(refs, 4, nw, 3, 0)
        i = pl.program_id(0)
        if nw:
            comm = _Exchange(xin, xout, sems)
            pl.when(i == 0)(comm.start)

        dhb = (0.5 * dh_ref[...]).astype(BF16)
        dhb_ref[...] = dhb
        for a, b in HID_PIECES:
            dgate_ref[:, a:b], dup_ref[:, a:b] = _swiglu_bwd(_dot_nt(dhb, wd_ref[a:b, :]), gate_ref, up_ref, a, b)

        if nw:
            pl.when(i == n_i - 1)(comm.finish)

    rows = pl.BlockSpec((tm, D_MODEL), lambda i: (i, 0))
    hid = pl.BlockSpec((tm, D_FF), lambda i: (i, 0))
    return pl.pallas_call(
        body, name=name, grid=(n_i,),
        in_specs=[rows, hid, hid, _resident(W_D_SHAPE, 1)] + [ANY] * nw,
        out_specs=[rows, hid, hid] + [ANY] * nw,
        out_shape=[jax.ShapeDtypeStruct((n, D_MODEL), BF16)] + [jax.ShapeDtypeStruct((n, D_FF), BF16)] * 2
        + [jax.ShapeDtypeStruct(a.shape, a.dtype) for a in exchange],
        scratch_shapes=_comm_sems(nw) if nw else [],
        compiler_params=_params(("arbitrary",)),
    )(dh_out, gate, up, wd, *exchange)


def _ffn_bwd_in(dh_out, h_in, gain, dgate, dup, wgu, name, exchange=()):
    n = h_in.shape[0]
    tm = _row_tile(n, 512)
    n_i = n // tm
    nw = len(exchange)

    def body(*refs):
        (dh_ref, h_ref, g_ref, dgate_ref, dup_ref, wgu_ref), xin, (dhin_ref, dgain_ref), xout, _, sems = \
            _split_refs(refs, 6, nw, 2, 0)
        i = pl.program_id(0)
        if nw:
            comm = _ChipExchange(xin, xout, sems)
            pl.when(i == 0)(comm.start)

        @pl.when(i == 0)
        def _():
            dgain_ref[...] = jnp.zeros_like(dgain_ref)

        dn = jnp.zeros((tm, D_MODEL), F32)
        for a, b in HID_PIECES:
            dn = dn + _dot(dgate_ref[:, a:b], wgu_ref[0, a:b, :]) + _dot(dup_ref[:, a:b], wgu_ref[1, a:b, :])
        gain_v = g_ref[...]
        _, xhat, r = _rms(h_ref[...], gain_v)
        dhin_ref[...] = dh_ref[...] + _rms_bwd(dn, xhat, r, gain_v)
        dgain_ref[...] += jnp.sum(dn * xhat, axis=0, keepdims=True)

        if nw:
            pl.when(i == n_i - 1)(comm.finish)

    rows = pl.BlockSpec((tm, D_MODEL), lambda i: (i, 0))
    hid = pl.BlockSpec((tm, D_FF), lambda i: (i, 0))
    vec = pl.BlockSpec((1, D_MODEL), lambda i: (0, 0))
    return pl.pallas_call(
        body, name=name, grid=(n_i,),
        in_specs=[rows, rows, vec, hid, hid, _resident(W_GU_SHAPE, 1)] + [ANY] * nw,
        out_specs=[rows, vec] + [ANY] * nw,
        out_shape=[jax.ShapeDtypeStruct((n, D_MODEL), F32), jax.ShapeDtypeStruct((1, D_MODEL), F32)]
        + [jax.ShapeDtypeStruct(a.shape, a.dtype) for a in exchange],
        scratch_shapes=_comm_sems(nw) if nw else [],
        compiler_params=_params(("arbitrary",)),
    )(dh_out, h_in, gain, dgate, dup, wgu, *exchange)


def _ffn_bwd_act_wd(dh_out, gate, up, wd, name, exchange=()):
    n = dh_out.shape[0]
    tm = _row_tile(n, 256)
    n_i = n // tm
    nw = len(exchange)

    def body(*refs):
        (dh_ref, gate_ref, up_ref, wd_ref), xin, (dgate_ref, dup_ref, dw_ref), xout, (acc_scr,), sems = \
            _split_refs(refs, 4, nw, 3, 1)
        i = pl.program_id(0)
        if nw:
            comm = _Exchange(xin, xout, sems)
            pl.when(i == 0)(comm.start)

        @pl.when(i == 0)
        def _():
            acc_scr[...] = jnp.zeros_like(acc_scr)

        dhb = (0.5 * dh_ref[...]).astype(BF16)
        for a, b in HID_PIECES:
            da = _dot_nt(dhb, wd_ref[a:b, :])
            g = gate_ref[:, a:b].astype(F32)
            u = up_ref[:, a:b].astype(F32)
            sig = _sigmoid(g)
            silu = g * sig
            dgate_ref[:, a:b] = (da * u * (sig * (1.0 + g * (1.0 - sig)))).astype(BF16)
            dup_ref[:, a:b] = (da * silu).astype(BF16)
            acc_scr[a:b, :] += _dot_tn((silu * u).astype(BF16), dhb)

        @pl.when(i == n_i - 1)
        def _():
            dw_ref[...] = acc_scr[...].astype(BF16)

        if nw:
            pl.when(i == n_i - 1)(comm.finish)

    rows = pl.BlockSpec((tm, D_MODEL), lambda i: (i, 0))
    hid = pl.BlockSpec((tm, D_FF), lambda i: (i, 0))
    return pl.pallas_call(
        body, name=name, grid=(n_i,),
        in_specs=[rows, hid, hid, _resident(W_D_SHAPE, 1)] + [ANY] * nw,
        out_specs=[hid, hid, _resident(W_D_SHAPE, 1)] + [ANY] * nw,
        out_shape=[jax.ShapeDtypeStruct((n, D_FF), BF16)] * 2 + [jax.ShapeDtypeStruct(W_D_SHAPE, BF16)]
        + [jax.ShapeDtypeStruct(a.shape, a.dtype) for a in exchange],
        scratch_shapes=[pltpu.VMEM(W_D_SHAPE, F32)] + (_comm_sems(nw) if nw else []),
        compiler_params=_params(("arbitrary",)),
    )(dh_out, gate, up, wd, *exchange)


def _ffn_bwd_wgu(nrm, dgate, dup, name):
    n = nrm.shape[0]
    tm = _row_tile(n, 256)
    n_i = n // tm

    def body(nrm_ref, dgate_ref, dup_ref, dw_ref, acc_scr):
        i = pl.program_id(0)

        @pl.when(i == 0)
        def _():
            acc_scr[...] = jnp.zeros_like(acc_scr)

        nb = nrm_ref[...]
        for a, b in HID_PIECES:
            acc_scr[0, a:b, :] += _dot_tn(dgate_ref[:, a:b], nb)
            acc_scr[1, a:b, :] += _dot_tn(dup_ref[:, a:b], nb)

        @pl.when(i == n_i - 1)
        def _():
            dw_ref[...] = acc_scr[...].astype(BF16)

    hid = pl.BlockSpec((tm, D_FF), lambda i: (i, 0))
    return pl.pallas_call(
        body, name=name, grid=(n_i,),
        in_specs=[pl.BlockSpec((tm, D_MODEL), lambda i: (i, 0)), hid, hid],
        out_specs=_resident(W_GU_SHAPE, 1),
        out_shape=jax.ShapeDtypeStruct(W_GU_SHAPE, BF16),
        scratch_shapes=[pltpu.VMEM(W_GU_SHAPE, F32)],
        compiler_params=_params(("arbitrary",)),
    )(nrm, dgate, dup)


def _ffn_bwd_wd(dhb, gate, up, name):
    n = dhb.shape[0]
    tm = _row_tile(n, 512)
    n_i = n // tm

    def body(dhb_ref, gate_ref, up_ref, dw_ref, acc_scr):
        i = pl.program_id(0)

        @pl.when(i == 0)
        def _():
            acc_scr[...] = jnp.zeros_like(acc_scr)

        dhv = dhb_ref[...]
        for a, b in HID_PIECES:
            g = gate_ref[:, a:b].astype(F32)
            act = (g * _sigmoid(g) * up_ref[:, a:b].astype(F32)).astype(BF16)
            acc_scr[a:b, :] += _dot_tn(act, dhv)

        @pl.when(i == n_i - 1)
        def _():
            dw_ref[...] = acc_scr[...].astype(BF16)

    hid = pl.BlockSpec((tm, D_FF), lambda i: (i, 0))
    return pl.pallas_call(
        body, name=name, grid=(n_i,),
        in_specs=[pl.BlockSpec((tm, D_MODEL), lambda i: (i, 0)), hid, hid],
        out_specs=_resident(W_D_SHAPE, 1),
        out_shape=jax.ShapeDtypeStruct(W_D_SHAPE, BF16),
        scratch_shapes=[pltpu.VMEM(W_D_SHAPE, F32)],
        compiler_params=_params(("arbitrary",)),
    )(dhb, gate, up)


N_PIECE = IN_MAIN // 512


def _inproj_fwd(h, gain, w_in):
    n = h.shape[0]
    tm = _row_tile(n, 512)

    def body(h_ref, g_ref, w_ref, *outs):
        y, _, _ = _rms(h_ref[...], g_ref[...])
        nb = y.astype(BF16)
        for p in range(N_PIECE):
            outs[p][...] = _dot_nt(nb, w_ref[512 * p:512 * (p + 1), :]).astype(BF16)
        outs[N_PIECE][...] = _dot_nt(nb, w_ref[IN_MAIN:IN_PAD, :])

    piece = pl.BlockSpec((tm, 512), lambda i: (i, 0))
    return pl.pallas_call(
        body, name="inproj_fwd", grid=(n // tm,),
        in_specs=[pl.BlockSpec((tm, D_MODEL), lambda i: (i, 0)),
                  pl.BlockSpec((1, D_MODEL), lambda i: (0, 0)),
                  pl.BlockSpec((IN_PAD, D_MODEL), lambda i: (0, 0))],
        out_specs=[piece] * N_PIECE + [pl.BlockSpec((tm, 128), lambda i: (i, 0))],
        out_shape=[jax.ShapeDtypeStruct((n, 512), BF16)] * N_PIECE + [jax.ShapeDtypeStruct((n, 128), F32)],
        compiler_params=_params(("parallel",)),
    )(h, gain, w_in)


def _inproj_bwd(dpieces, dfg, dh_out, h_in, gain, w_in):
    n = h_in.shape[0]
    tm = _row_tile(n, 512)
    n_i = n // tm

    def body(*refs):
        dp_refs = refs[:N_PIECE]
        dfg_ref, dh_ref, h_ref, g_ref, w_ref, dhin_ref, dw_ref, dgain_ref, acc_scr = refs[N_PIECE:]
        i = pl.program_id(0)

        @pl.when(i == 0)
        def _():
            acc_scr[...] = jnp.zeros_like(acc_scr)
            dgain_ref[...] = jnp.zeros_like(dgain_ref)

        gain_v = g_ref[...]
        y, xhat, r = _rms(h_ref[...], gain_v)
        nb = y.astype(BF16)
        dn = jnp.zeros((tm, D_MODEL), F32)
        for p in range(N_PIECE + 1):
            lo, hi = (512 * p, 512 * (p + 1)) if p < N_PIECE else (IN_MAIN, IN_PAD)
            dp = (dp_refs[p][...] if p < N_PIECE else dfg_ref[...]).astype(BF16)
            dn = dn + _dot(dp, w_ref[lo:hi, :])
            acc_scr[lo:hi, :] += _dot_tn(dp, nb)
        dhin_ref[...] = dh_ref[...] + _rms_bwd(dn, xhat, r, gain_v)
        dgain_ref[...] += jnp.sum(dn * xhat, axis=0, keepdims=True)

        @pl.when(i == n_i - 1)
        def _():
            dw_ref[...] = acc_scr[...].astype(BF16)

    piece = pl.BlockSpec((tm, 512), lambda i: (i, 0))
    rows = pl.BlockSpec((tm, D_MODEL), lambda i: (i, 0))
    vec = pl.BlockSpec((1, D_MODEL), lambda i: (0, 0))
    wspec = pl.BlockSpec((IN_PAD, D_MODEL), lambda i: (0, 0))
    return pl.pallas_call(
        body, name="inproj_bwd", grid=(n_i,),
        in_specs=[piece] * N_PIECE + [pl.BlockSpec((tm, 128), lambda i: (i, 0)), rows, rows, vec, wspec],
        out_specs=[rows, wspec, vec],
        out_shape=[jax.ShapeDtypeStruct((n, D_MODEL), F32),
                   jax.ShapeDtypeStruct((IN_PAD, D_MODEL), BF16),
                   jax.ShapeDtypeStruct((1, D_MODEL), F32)],
        scratch_shapes=[pltpu.VMEM((IN_PAD, D_MODEL), F32)],
        compiler_params=_params(("arbitrary",)),
    )(*dpieces, dfg, dh_out, h_in, gain, w_in)


def _outproj_fwd(zc, za, w_out, h):
    n = h.shape[0]
    tm = _row_tile(n, 512)

    def body(zc_ref, za_ref, w_ref, h_ref, out_ref):
        out_ref[...] = (h_ref[...] + _dot(zc_ref[...], w_ref[0:CONV_DIM, :])
                        + _dot(za_ref[...], w_ref[CONV_DIM:, :]))

    half = pl.BlockSpec((tm, 512), lambda i: (i, 0))
    rows = pl.BlockSpec((tm, D_MODEL), lambda i: (i, 0))
    return pl.pallas_call(
        body, name="outproj_fwd", grid=(n // tm,),
        in_specs=[half, half, pl.BlockSpec((D_MODEL, D_MODEL), lambda i: (0, 0)), rows],
        out_specs=rows,
        out_shape=jax.ShapeDtypeStruct((n, D_MODEL), F32),
        compiler_params=_params(("parallel",)),
    )(zc, za, w_out, h)


def _outproj_bwd(dh, zc, za, w_out):
    n = dh.shape[0]
    tm = _row_tile(n, 512)
    n_i = n // tm

    def body(dh_ref, zc_ref, za_ref, w_ref, dzc_ref, dza_ref, dw_ref, acc_scr):
        i = pl.program_id(0)

        @pl.when(i == 0)
        def _():
            acc_scr[...] = jnp.zeros_like(acc_scr)

        dhb = dh_ref[...].astype(BF16)
        dzc_ref[...] = _dot_nt(dhb, w_ref[0:CONV_DIM, :]).astype(BF16)
        dza_ref[...] = _dot_nt(dhb, w_ref[CONV_DIM:, :]).astype(BF16)
        acc_scr[0:CONV_DIM, :] += _dot_tn(zc_ref[...], dhb)
        acc_scr[CONV_DIM:, :] += _dot_tn(za_ref[...], dhb)

        @pl.when(i == n_i - 1)
        def _():
            dw_ref[...] = acc_scr[...].astype(BF16)

    half = pl.BlockSpec((tm, 512), lambda i: (i, 0))
    wspec = pl.BlockSpec((D_MODEL, D_MODEL), lambda i: (0, 0))
    return pl.pallas_call(
        body, name="outproj_bwd", grid=(n_i,),
        in_specs=[pl.BlockSpec((tm, D_MODEL), lambda i: (i, 0)), half, half, wspec],
        out_specs=[half, half, wspec],
        out_shape=[jax.ShapeDtypeStruct((n, 512), BF16), jax.ShapeDtypeStruct((n, 512), BF16),
                   jax.ShapeDtypeStruct((D_MODEL, D_MODEL), BF16)],
        scratch_shapes=[pltpu.VMEM((D_MODEL, D_MODEL), F32)],
        compiler_params=_params(("arbitrary",)),
    )(dh, zc, za, w_out)


def _group_matrix():
    r = lax.broadcasted_iota(jnp.int32, (128, 128), 0) // HEAD_DIM
    c = lax.broadcasted_iota(jnp.int32, (128, 128), 1) // HEAD_DIM
    return jnp.where(r == c, 1.0 / HEAD_DIM, 0.0).astype(BF16)


def _group_mean(x, gmat):
    hi = x.astype(BF16)
    lo = (x - hi.astype(F32)).astype(BF16)
    return _dot(hi, gmat) + _dot(lo, gmat)


def _shift_rows(x, s):
    rows = x.shape[0]
    t = lax.broadcasted_iota(jnp.int32, x.shape, 0)
    rolled = pltpu.roll(x, s % rows, 0)
    keep = (t >= s) if s > 0 else (t < rows + s)
    return jnp.where(keep, rolled, 0.0)


def _conv_parts(bg_ref, cg_ref, hc_ref, w_ref):
    bg = bg_ref[...].astype(F32)
    cg = cg_ref[...].astype(F32)
    hc = hc_ref[...].astype(F32)
    u = cg * hc
    u1 = _shift_rows(u, 1)
    u2 = _shift_rows(u, 2)
    conv = w_ref[2:3, :] * u + w_ref[1:2, :] * u1 + w_ref[0:1, :] * u2
    return bg, cg, hc, u, u1, u2, conv


def _conv_fwd(bg, cg, hc, conv_w, gain, gmat, lp):
    n = bg.shape[0]
    nb = n // lp

    def body(bg_ref, cg_ref, hc_ref, w_ref, g_ref, gm_ref, z_ref):
        bgv, _, _, _, _, _, conv = _conv_parts(bg_ref, cg_ref, hc_ref, w_ref)
        yc = bgv * conv
        r = lax.rsqrt(_group_mean(yc * yc, gm_ref[...]) + EPS)
        z_ref[...] = (yc * r * g_ref[...]).astype(BF16)

    blk = pl.BlockSpec((lp, 128), lambda c, b: (b, c))
    return pl.pallas_call(
        body, name="conv_fwd", grid=(CONV_DIM // 128, nb),
        in_specs=[blk, blk, blk, pl.BlockSpec((3, 128), lambda c, b: (0, c)),
                  pl.BlockSpec((1, 128), lambda c, b: (0, c)), pl.BlockSpec((128, 128), lambda c, b: (0, 0))],
        out_specs=blk,
        out_shape=jax.ShapeDtypeStruct((n, CONV_DIM), BF16),
        compiler_params=_params(("parallel", "parallel")),
    )(bg, cg, hc, conv_w, gain, gmat)


def _conv_bwd(dz, bg, cg, hc, conv_w, gain, gmat, lp):
    n = bg.shape[0]
    nb = n // lp

    def body(dz_ref, bg_ref, cg_ref, hc_ref, w_ref, g_ref, gm_ref,
             dbg_ref, dcg_ref, dhc_ref, dw_ref, dgain_ref):
        b = pl.program_id(1)

        @pl.when(b == 0)
        def _():
            dw_ref[...] = jnp.zeros_like(dw_ref)
            dgain_ref[...] = jnp.zeros_like(dgain_ref)

        bgv, cgv, hcv, u, u1, u2, conv = _conv_parts(bg_ref, cg_ref, hc_ref, w_ref)
        gm = gm_ref[...]
        yc = bgv * conv
        r = lax.rsqrt(_group_mean(yc * yc, gm) + EPS)
        yhat = yc * r
        dzv = dz_ref[...].astype(F32)
        dyhat = dzv * g_ref[...]
        dgain_ref[...] += jnp.sum(dzv * yhat, axis=0, keepdims=True)
        dyc = r * (dyhat - yhat * _group_mean(dyhat * yhat, gm))
        dbg_ref[...] = (dyc * conv).astype(BF16)
        dconv = dyc * bgv
        du = (w_ref[2:3, :] * dconv + w_ref[1:2, :] * _shift_rows(dconv, -1)
              + w_ref[0:1, :] * _shift_rows(dconv, -2))
        dcg_ref[...] = (du * hcv).astype(BF16)
        dhc_ref[...] = (du * cgv).astype(BF16)
        dw_ref[0:1, :] += jnp.sum(dconv * u2, axis=0, keepdims=True)
        dw_ref[1:2, :] += jnp.sum(dconv * u1, axis=0, keepdims=True)
        dw_ref[2:3, :] += jnp.sum(dconv * u, axis=0, keepdims=True)

    blk = pl.BlockSpec((lp, 128), lambda c, b: (b, c))
    wspec = pl.BlockSpec((3, 128), lambda c, b: (0, c))
    gspec = pl.BlockSpec((1, 128), lambda c, b: (0, c))
    return pl.pallas_call(
        body, name="conv_bwd", grid=(CONV_DIM // 128, nb),
        in_specs=[blk, blk, blk, blk, wspec, gspec, pl.BlockSpec((128, 128), lambda c, b: (0, 0))],
        out_specs=[blk, blk, blk, wspec, gspec],
        out_shape=[jax.ShapeDtypeStruct((n, CONV_DIM), BF16)] * 3
        + [jax.ShapeDtypeStruct((3, CONV_DIM), F32), jax.ShapeDtypeStruct((1, CONV_DIM), F32)],
        compiler_params=_params(("parallel", "arbitrary")),
    )(dz, bg, cg, hc, conv_w, gain, gmat)


KEY_MASKED = 1e30
ONE_LANE = 24


def _scan_steps(rows):
    s, out = 1, []
    while s < rows:
        out.append(s)
        s *= 2
    return out


def _fgate_fwd(fg, b_f, lp):
    n = fg.shape[0]
    nb = n // lp

    def body(fg_ref, b_ref, ka_ref, qa_ref):
        x = fg_ref[...] + b_ref[...]
        logf = jnp.minimum(x, 0.0) - jnp.log(1.0 + jnp.exp(-jnp.abs(x)))
        t = lax.broadcasted_iota(jnp.int32, (lp, 128), 0)
        lane = lax.broadcasted_iota(jnp.int32, (lp, 128), 1)
        f = jnp.where((t >= PAD) & (lane < N_HEADS), logf, 0.0)
        for s in _scan_steps(lp):
            f = f + _shift_rows(f, s)
        hi = f.astype(BF16).astype(F32)
        rest = f - hi
        mid = rest.astype(BF16).astype(F32)
        lo = (rest - mid).astype(BF16).astype(F32)
        ones = jnp.where((lane >= ONE_LANE) & (lane < ONE_LANE + 3), 1.0, 0.0)
        hi_key = jnp.where((t < PAD) & (lane < N_HEADS), KEY_MASKED, hi)
        ka_ref[...] = (hi_key + pltpu.roll(mid, 8, 1) + pltpu.roll(lo, 16, 1) + ones).astype(BF16)
        for h in range(N_HEADS):
            minus = jnp.where((lane == h) | (lane == 8 + h) | (lane == 16 + h), -1.0, 0.0)
            terms = (jnp.where(lane == ONE_LANE, pltpu.roll(hi, ONE_LANE - h, 1), 0.0)
                     + jnp.where(lane == ONE_LANE + 1, pltpu.roll(mid, ONE_LANE + 1 - h, 1), 0.0)
                     + jnp.where(lane == ONE_LANE + 2, pltpu.roll(lo, ONE_LANE + 2 - h, 1), 0.0))
            qa_ref[:, 128 * h:128 * (h + 1)] = (minus + terms).astype(BF16)

    return pl.pallas_call(
        body, name="fgate_fwd", grid=(nb,),
        in_specs=[pl.BlockSpec((lp, 128), lambda b: (b, 0)), pl.BlockSpec((1, 128), lambda b: (0, 0))],
        out_specs=[pl.BlockSpec((lp, 128), lambda b: (b, 0)), pl.BlockSpec((lp, N_HEADS * 128), lambda b: (b, 0))],
        out_shape=[jax.ShapeDtypeStruct((n, 128), BF16), jax.ShapeDtypeStruct((n, N_HEADS * 128), BF16)],
        compiler_params=_params(("parallel",)),
    )(fg, b_f)


def _fgate_bwd(dka, dfr, fg, b_f, lp):
    n = fg.shape[0]
    nb = n // lp

    def body(dka_ref, dfr_ref, fg_ref, b_ref, dfg_ref, db_ref):
        b = pl.program_id(0)

        @pl.when(b == 0)
        def _():
            db_ref[...] = jnp.zeros_like(db_ref)

        wide = jnp.concatenate([dfr_ref[0], jnp.zeros((128 - N_HEADS, lp), F32)], axis=0)
        t = lax.broadcasted_iota(jnp.int32, (lp, 128), 0)
        lane = lax.broadcasted_iota(jnp.int32, (lp, 128), 1)
        d = jnp.where(lane < N_HEADS, dka_ref[...], 0.0) + wide.T
        for s in _scan_steps(lp):
            d = d + _shift_rows(d, -s)
        x = fg_ref[...] + b_ref[...]
        dx = jnp.where((t >= PAD) & (lane < N_HEADS), d * _sigmoid(-x), 0.0)
        dfg_ref[...] = dx
        db_ref[...] += jnp.sum(dx, axis=0, keepdims=True)

    return pl.pallas_call(
        body, name="fgate_bwd", grid=(nb,),
        in_specs=[pl.BlockSpec((lp, 128), lambda b: (b, 0)), pl.BlockSpec((1, N_HEADS, lp), lambda b: (b, 0, 0)),
                  pl.BlockSpec((lp, 128), lambda b: (b, 0)), pl.BlockSpec((1, 128), lambda b: (0, 0))],
        out_specs=[pl.BlockSpec((lp, 128), lambda b: (b, 0)), pl.BlockSpec((1, 128), lambda b: (0, 0))],
        out_shape=[jax.ShapeDtypeStruct((n, 128), F32), jax.ShapeDtypeStruct((1, 128), F32)],
        compiler_params=_params(("arbitrary",)),
    )(dka, dfr, fg, b_f)


def _head_masks():
    lane = lax.broadcasted_iota(jnp.int32, (1, 128), 1)
    return lane < HEAD_DIM


def _stack_heads(x2, first):
    zero = jnp.zeros_like(x2)
    return jnp.concatenate([jnp.where(first, x2, zero), jnp.where(first, zero, x2)], axis=0)


def _stack_heads_lanes(xt):
    r = lax.broadcasted_iota(jnp.int32, xt.shape, 0)
    zero = jnp.zeros_like(xt)
    return jnp.concatenate([jnp.where(r < HEAD_DIM, xt, zero), jnp.where(r < HEAD_DIM, zero, xt)], axis=1)


def _pair_cols(col0, col1, first):
    return jnp.where(first, col0, col1)


def _pair_rows(row0, row1):
    r = lax.broadcasted_iota(jnp.int32, (128, TQ), 0)
    return jnp.where(r < HEAD_DIM, row0, row1)


def _query_side(q_ref, qa_ref, p, first):
    q2 = q_ref[:, 128 * p:128 * (p + 1)] * 0.125
    zero = jnp.zeros_like(q2)
    top = jnp.concatenate([jnp.where(first, q2, zero), qa_ref[:, 128 * (2 * p):128 * (2 * p + 1)]], axis=1)
    bot = jnp.concatenate([jnp.where(first, zero, q2), qa_ref[:, 128 * (2 * p + 1):128 * (2 * p + 2)]], axis=1)
    return jnp.concatenate([top, bot], axis=0)


def _key_chunks(lp):
    return (lp + TK - 1) // TK


def _chunk_mask(i, c):
    r = lax.broadcasted_iota(jnp.int32, (TK, 2 * TQ), 0)
    col = lax.broadcasted_iota(jnp.int32, (TK, 2 * TQ), 1)
    return (c * TK + r) <= (i * TQ + (col & (TQ - 1)))


def _transpose_bf16(x):
    return x.astype(F32).T.astype(BF16)


def _attn_fwd(q, qa, k, v, ka, gain, lp):
    n = q.shape[0]
    nb = n // lp
    nq = lp // TQ
    lpp = _key_chunks(lp) * TK

    def body(q_ref, qa_ref, k_ref, v_ref, ka_ref, g_ref, z_ref, o_ref, lse_ref, kx_scr, vt_scr):
        i = pl.program_id(1)
        first = _head_masks()

        @pl.when(i == 0)
        def _():
            if lpp > lp:
                kx_scr[lp:lpp, :] = jnp.zeros((lpp - lp, 2 * ATTN_DIM), BF16)
                vt_scr[:, lp:lpp] = jnp.zeros((ATTN_DIM, lpp - lp), BF16)
            for p in range(N_PAIRS):
                kx_scr[0:lp, 256 * p:256 * p + 128] = k_ref[:, 128 * p:128 * (p + 1)]
                kx_scr[0:lp, 256 * p + 128:256 * (p + 1)] = ka_ref[...]
            vt_scr[:, 0:lp] = _transpose_bf16(v_ref[...])

        rhs_t = [_transpose_bf16(_query_side(q_ref, qa_ref, p, first)) for p in range(N_PAIRS)]

        def step(c, carry, masked):
            koff = pl.multiple_of(c * TK, TK)
            valid = _chunk_mask(i, c) if masked else None
            new = []
            for p in range(N_PAIRS):
                m, l, acc = carry[p]
                st = _dot(kx_scr[pl.ds(koff, TK), 256 * p:256 * (p + 1)], rhs_t[p])
                if masked:
                    st = jnp.where(valid, st, NEG)
                m_new = jnp.maximum(m, jnp.max(st, axis=0, keepdims=True))
                pt = jnp.exp(st - m_new)
                alpha = jnp.exp(m - m_new)
                l = alpha * l + jnp.sum(pt, axis=0, keepdims=True)
                pb = pt.astype(BF16)
                vt = _stack_heads_lanes(vt_scr[128 * p:128 * (p + 1), pl.ds(koff, TK)])
                pv = _dot(vt, jnp.concatenate([pb[:, 0:TQ], pb[:, TQ:]], axis=0))
                acc = acc * _pair_rows(alpha[:, 0:TQ], alpha[:, TQ:]) + pv
                new.append((m_new, l, acc))
            return tuple(new)

        init = tuple((jnp.full((1, 2 * TQ), NEG, F32), jnp.zeros((1, 2 * TQ), F32), jnp.zeros((128, TQ), F32))
                     for _ in range(N_PAIRS))
        last = i // (TK // TQ)
        final = step(last, lax.fori_loop(0, last, lambda c, carry: step(c, carry, False), init), True)

        row = lax.broadcasted_iota(jnp.int32, (TQ, 128), 0)
        real = (i * TQ + row) >= PAD
        for p in range(N_PAIRS):
            m, l, acc = final[p]
            inv = 1.0 / l
            ot = acc * _pair_rows(inv[:, 0:TQ], inv[:, TQ:])
            sq = ot * ot
            r0 = lax.rsqrt(jnp.sum(sq[0:HEAD_DIM], axis=0, keepdims=True) * (1.0 / HEAD_DIM) + EPS)
            r1 = lax.rsqrt(jnp.sum(sq[HEAD_DIM:], axis=0, keepdims=True) * (1.0 / HEAD_DIM) + EPS)
            cols = slice(128 * p, 128 * (p + 1))
            o_ref[:, cols] = jnp.where(real, ot.T, 0.0).astype(BF16)
            z_ref[:, cols] = (jnp.where(real, (ot * _pair_rows(r0, r1)).T, 0.0) * g_ref[:, cols]).astype(BF16)
            lse = m + jnp.log(l)
            lse_ref[0, 2 * p:2 * p + 1, :] = lse[:, 0:TQ]
            lse_ref[0, 2 * p + 1:2 * p + 2, :] = lse[:, TQ:]

    qblk = pl.BlockSpec((TQ, ATTN_DIM), lambda b, i: (b * nq + i, 0))
    qablk = pl.BlockSpec((TQ, N_HEADS * 128), lambda b, i: (b * nq + i, 0))
    seq = pl.BlockSpec((lp, ATTN_DIM), lambda b, i: (b, 0))
    rowblk = pl.BlockSpec((1, N_HEADS, TQ), lambda b, i: (b, 0, i))
    return pl.pallas_call(
        body, name="attn_fwd", grid=(nb, nq),
        in_specs=[qblk, qablk, seq, seq, pl.BlockSpec((lp, 128), lambda b, i: (b, 0)),
                  pl.BlockSpec((1, ATTN_DIM), lambda b, i: (0, 0))],
        out_specs=[qblk, qblk, rowblk],
        out_shape=[jax.ShapeDtypeStruct((n, ATTN_DIM), BF16), jax.ShapeDtypeStruct((n, ATTN_DIM), BF16),
                   jax.ShapeDtypeStruct((nb, N_HEADS, lp), F32)],
        scratch_shapes=[pltpu.VMEM((lpp, 2 * ATTN_DIM), BF16), pltpu.VMEM((ATTN_DIM, lpp), BF16)],
        compiler_params=_params(("parallel", "arbitrary")),
    )(q, qa, k, v, ka, gain)


def _attn_bwd(dz, q, qa, k, v, ka, o, lse, gain, lp, exchange=()):
    n = q.shape[0]
    nb = n // lp
    nq = lp // TQ
    lpp = _key_chunks(lp) * TK
    nw = len(exchange)

    def body(*refs):
        ((dz_ref, q_ref, qa_ref, k_ref, v_ref, ka_ref, o_ref, lse_ref, g_ref), xin,
         (dq_ref, dk_ref, dv_ref, dka_ref, dfr_ref, dgain_ref), xout,
         (kx_scr, vx_scr, kt_scr, dkx_scr, dvx_scr), sems) = _split_refs(refs, 9, nw, 6, 5)
        b = pl.program_id(0)
        i = pl.program_id(1)
        first = _head_masks()
        if nw:
            comm = _Exchange(xin, xout, sems)
            pl.when((b == 0) & (i == 0))(comm.start)

        @pl.when((b == 0) & (i == 0))
        def _():
            dgain_ref[...] = jnp.zeros_like(dgain_ref)

        @pl.when(i == 0)
        def _():
            if lpp > lp:
                kx_scr[lp:lpp, :] = jnp.zeros((lpp - lp, 2 * ATTN_DIM), BF16)
                vx_scr[lp:lpp, :] = jnp.zeros((lpp - lp, ATTN_DIM), BF16)
                kt_scr[:, lp:lpp] = jnp.zeros((ATTN_DIM, lpp - lp), BF16)
            for p in range(N_PAIRS):
                kx_scr[0:lp, 256 * p:256 * p + 128] = k_ref[:, 128 * p:128 * (p + 1)]
                kx_scr[0:lp, 256 * p + 128:256 * (p + 1)] = ka_ref[...]
            vx_scr[0:lp, :] = v_ref[...]
            kt_scr[:, 0:lp] = _transpose_bf16(k_ref[...])
            dkx_scr[...] = jnp.zeros_like(dkx_scr)
            dvx_scr[...] = jnp.zeros_like(dvx_scr)

        rhs, rhs_t, lses, dos, dos_t, deltas = [], [], [], [], [], []
        for p in range(N_PAIRS):
            cols = slice(128 * p, 128 * (p + 1))
            side = _query_side(q_ref, qa_ref, p, first)
            rhs.append(side)
            rhs_t.append(_transpose_bf16(side))
            lses.append(jnp.concatenate([lse_ref[0, 2 * p:2 * p + 1, :], lse_ref[0, 2 * p + 1:2 * p + 2, :]], axis=1))
            ov = o_ref[:, cols].astype(F32)
            dzv = dz_ref[:, cols].astype(F32)
            gv = g_ref[:, cols]
            sq = ov * ov
            ms0 = jnp.sum(jnp.where(first, sq, 0.0), axis=1, keepdims=True) * (1.0 / HEAD_DIM)
            ms1 = jnp.sum(jnp.where(first, 0.0, sq), axis=1, keepdims=True) * (1.0 / HEAD_DIM)
            r = _pair_cols(lax.rsqrt(ms0 + EPS), lax.rsqrt(ms1 + EPS), first)
            ohat = ov * r
            dyhat = dzv * gv
            dgain_ref[:, cols] += jnp.sum(dzv * ohat, axis=0, keepdims=True)
            pr = dyhat * ohat
            mean0 = jnp.sum(jnp.where(first, pr, 0.0), axis=1, keepdims=True) * (1.0 / HEAD_DIM)
            mean1 = jnp.sum(jnp.where(first, 0.0, pr), axis=1, keepdims=True) * (1.0 / HEAD_DIM)
            do = r * (dyhat - ohat * _pair_cols(mean0, mean1, first))
            ddt = (do * ov).T
            deltas.append(jnp.concatenate([jnp.sum(ddt[0:HEAD_DIM], axis=0, keepdims=True),
                                           jnp.sum(ddt[HEAD_DIM:], axis=0, keepdims=True)], axis=1))
            do_st = _stack_heads(do.astype(BF16), first)
            dos.append(do_st)
            dos_t.append(_transpose_bf16(do_st))

        def step(c, carry, masked):
            koff = pl.multiple_of(c * TK, TK)
            valid = _chunk_mask(i, c) if masked else None
            new = []
            for p in range(N_PAIRS):
                dqt, dfq = carry[p]
                ext = slice(256 * p, 256 * (p + 1))
                cols = slice(128 * p, 128 * (p + 1))
                st = _dot(kx_scr[pl.ds(koff, TK), ext], rhs_t[p])
                if masked:
                    st = jnp.where(valid, st, NEG)
                pt = jnp.exp(st - lses[p])
                dpt = _dot(vx_scr[pl.ds(koff, TK), cols], dos_t[p])
                dst = pt * (dpt - deltas[p])
                dsb = dst.astype(BF16)
                dfq = dfq + jnp.sum(dsb.astype(F32), axis=0, keepdims=True)
                dkx_scr[pl.ds(koff, TK), ext] += _dot(dsb, rhs[p])
                dvx_scr[pl.ds(koff, TK), cols] += _dot(pt.astype(BF16), dos[p])
                kt = _stack_heads_lanes(kt_scr[cols, pl.ds(koff, TK)])
                dqt = dqt + _dot(kt, jnp.concatenate([dsb[:, 0:TQ], dsb[:, TQ:]], axis=0))
                new.append((dqt, dfq))
            return tuple(new)

        init = tuple((jnp.zeros((128, TQ), F32), jnp.zeros((1, 2 * TQ), F32)) for _ in range(N_PAIRS))
        last = i // (TK // TQ)
        final = step(last, lax.fori_loop(0, last, lambda c, carry: step(c, carry, False), init), True)

        for p in range(N_PAIRS):
            dqt, dfq = final[p]
            dq_ref[:, 128 * p:128 * (p + 1)] = (dqt.T * 0.125).astype(BF16)
            dfr_ref[0, 2 * p:2 * p + 1, :] = dfq[:, 0:TQ]
            dfr_ref[0, 2 * p + 1:2 * p + 2, :] = dfq[:, TQ:]

        @pl.when(i == nq - 1)
        def _():
            dka = jnp.zeros((lp, 128), F32)
            for p in range(N_PAIRS):
                dk_ref[:, 128 * p:128 * (p + 1)] = dkx_scr[0:lp, 256 * p:256 * p + 128].astype(BF16)
                dka = dka + dkx_scr[0:lp, 256 * p + 128:256 * (p + 1)]
            dka_ref[...] = dka
            dv_ref[...] = dvx_scr[0:lp, :].astype(BF16)

        if nw:
            pl.when((b == nb - 1) & (i == nq - 1))(comm.finish)

    qblk = pl.BlockSpec((TQ, ATTN_DIM), lambda b, i: (b * nq + i, 0))
    qablk = pl.BlockSpec((TQ, N_HEADS * 128), lambda b, i: (b * nq + i, 0))
    seq = pl.BlockSpec((lp, ATTN_DIM), lambda b, i: (b, 0))
    kaseq = pl.BlockSpec((lp, 128), lambda b, i: (b, 0))
    rowblk = pl.BlockSpec((1, N_HEADS, TQ), lambda b, i: (b, 0, i))
    gspec = pl.BlockSpec((1, ATTN_DIM), lambda b, i: (0, 0))
    return pl.pallas_call(
        body, name="attn_bwd", grid=(nb, nq),
        in_specs=[qblk, qblk, qablk, seq, seq, kaseq, qblk, rowblk, gspec] + [ANY] * nw,
        out_specs=[qblk, seq, seq, kaseq, rowblk, gspec] + [ANY] * nw,
        out_shape=[jax.ShapeDtypeStruct((n, ATTN_DIM), BF16), jax.ShapeDtypeStruct((n, ATTN_DIM), BF16),
                   jax.ShapeDtypeStruct((n, ATTN_DIM), BF16), jax.ShapeDtypeStruct((n, 128), F32),
                   jax.ShapeDtypeStruct((nb, N_HEADS, lp), F32), jax.ShapeDtypeStruct((1, ATTN_DIM), F32)]
        + [jax.ShapeDtypeStruct(a.shape, a.dtype) for a in exchange],
        scratch_shapes=[pltpu.VMEM((lpp, 2 * ATTN_DIM), BF16), pltpu.VMEM((lpp, ATTN_DIM), BF16),
                        pltpu.VMEM((ATTN_DIM, lpp), BF16), pltpu.VMEM((lpp, 2 * ATTN_DIM), F32),
                        pltpu.VMEM((lpp, ATTN_DIM), F32)] + (_comm_sems(nw) if nw else []),
        compiler_params=_params(("arbitrary", "arbitrary")),
    )(dz, q, qa, k, v, ka, o, lse, gain, *exchange)


def _loss_head(h, gain, target, lp):
    n = h.shape[0]
    nb = n // lp
    nq = lp // 128

    def body(h_ref, g_ref, t_ref, loss_ref, dh_ref, dgain_ref):
        b = pl.program_id(0)
        i = pl.program_id(1)

        @pl.when((b == 0) & (i == 0))
        def _():
            loss_ref[...] = jnp.zeros_like(loss_ref)
            dgain_ref[...] = jnp.zeros_like(dgain_ref)

        @pl.when(i == 0)
        def _():
            dh_ref[...] = jnp.zeros_like(dh_ref)

        @pl.when(i > 0)
        def _():
            gain_v = g_ref[...]
            y, xhat, r = _rms(h_ref[...], gain_v)
            err = y - t_ref[...]
            loss_ref[...] += 0.5 * jnp.sum(jnp.sum(err * err, axis=1, keepdims=True), axis=0,
                                           keepdims=True) * (1.0 / D_MODEL)
            dy = err * (1.0 / D_MODEL)
            dh_ref[...] = _rms_bwd(dy, xhat, r, gain_v)
            dgain_ref[...] += jnp.sum(dy * xhat, axis=0, keepdims=True)

    rows = pl.BlockSpec((128, D_MODEL), lambda b, i: (b * nq + i, 0))
    trows = pl.BlockSpec((128, D_MODEL), lambda b, i: (b * (nq - 1) + jnp.maximum(i, 1) - 1, 0))
    return pl.pallas_call(
        body, name="loss_head", grid=(nb, nq),
        in_specs=[rows, pl.BlockSpec((1, D_MODEL), lambda b, i: (0, 0)), trows],
        out_specs=[pl.BlockSpec((1, 1), lambda b, i: (0, 0)), rows, pl.BlockSpec((1, D_MODEL), lambda b, i: (0, 0))],
        out_shape=[jax.ShapeDtypeStruct((1, 1), F32), jax.ShapeDtypeStruct((n, D_MODEL), F32),
                   jax.ShapeDtypeStruct((1, D_MODEL), F32)],
        compiler_params=_params(("arbitrary", "arbitrary")),
    )(h, gain, target)


def _adamw(parts, w, m, v, name):
    s_parts, r, c = parts.shape
    tr = r
    for t in (256, 128, 64, 32, 16):
        if r % t == 0 and r > t:
            tr = t
            break

    def body(p_ref, w_ref, m_ref, v_ref, g_ref, d_ref, nm_ref, nv_ref):
        g = p_ref[0].astype(F32)
        for s in range(1, s_parts):
            g = g + p_ref[s].astype(F32)
        nm = ADAM_B1 * m_ref[...] + (1.0 - ADAM_B1) * g
        nv = ADAM_B2 * v_ref[...] + (1.0 - ADAM_B2) * (g * g)
        m_hat = nm / (1.0 - ADAM_B1 ** ADAM_STEP)
        v_hat = nv / (1.0 - ADAM_B2 ** ADAM_STEP)
        g_ref[...] = g
        d_ref[...] = -ADAM_LR * (m_hat / (jnp.sqrt(v_hat) + ADAM_EPS) + ADAM_WD * w_ref[...])
        nm_ref[...] = nm
        nv_ref[...] = nv

    blk = pl.BlockSpec((tr, c), lambda i: (i, 0))
    return pl.pallas_call(
        body, name=name, grid=(r // tr,),
        in_specs=[pl.BlockSpec((s_parts, tr, c), lambda i: (0, i, 0)), blk, blk, blk],
        out_specs=[blk] * 4,
        out_shape=[jax.ShapeDtypeStruct((r, c), F32)] * 4,
        compiler_params=_params(("parallel",)),
    )(parts, w, m, v)


def _sum_parts(parts, name):
    s_parts, r, c = parts.shape

    def body(p_ref, out_ref):
        acc = p_ref[0]
        for s in range(1, s_parts):
            acc = acc + p_ref[s]
        out_ref[...] = acc

    return pl.pallas_call(
        body, name=name, out_shape=jax.ShapeDtypeStruct((r, c), F32),
        in_specs=[pl.BlockSpec(memory_space=pltpu.VMEM)], out_specs=pl.BlockSpec(memory_space=pltpu.VMEM),
    )(parts)


SMALL_ROWS = 184


def _pack_small(d_gains, d_gc, d_ga, d_bf, d_conv, d_meta):
    rows = [g.reshape(8, 128) for g in d_gains]
    rows += [d_gc.reshape(4, 128), d_ga.reshape(4, 128), d_bf.reshape(1, 128)]
    rows += [d_conv.reshape(12, 128), d_meta.reshape(128, 128)]
    packed = jnp.concatenate(rows, axis=0)
    return jnp.pad(packed, ((0, SMALL_ROWS - packed.shape[0]), (0, 0)))


def kernel(x, meta_tokens, ffn1_norm, ffn1_w_gu, ffn1_w_down, mix_norm, w_in, conv_w, b_f, out_norm_conv, out_norm_attn, w_out, ffn2_norm, ffn2_w_gu, ffn2_w_down, final_norm, loss_target, m_meta_tokens, m_ffn1_norm, m_ffn1_w_gu, m_ffn1_w_down, m_mix_norm, m_w_in, m_conv_w, m_b_f, m_out_norm_conv, m_out_norm_attn, m_w_out, m_ffn2_norm, m_ffn2_w_gu, m_ffn2_w_down, m_final_norm, v_meta_tokens, v_ffn1_norm, v_ffn1_w_gu, v_ffn1_w_down, v_mix_norm, v_w_in, v_conv_w, v_b_f, v_out_norm_conv, v_out_norm_attn, v_w_out, v_ffn2_norm, v_ffn2_w_gu, v_ffn2_w_down, v_final_norm):
    nb, seq, _ = x.shape
    lp = PAD + N_META + seq
    n = nb * lp
    me = 4 * lax.axis_index("x") + 2 * lax.axis_index("y") + lax.axis_index("c")

    wgu1_8, wd1_8 = _all_gather([ffn1_w_gu[0].T.astype(BF16), ffn1_w_down[0].astype(BF16)], "gather_ffn1")
    small_in = jnp.concatenate(
        [meta_tokens, jnp.pad(conv_w[0], ((0, 0), (0, 128 - conv_w.shape[2]))), jnp.zeros((5, 128), F32)], axis=0)
    (small_8,) = _all_gather([small_in], "gather_small")
    meta_full = small_8[:, 0:N_META, :].transpose(1, 0, 2).reshape(N_META, D_MODEL)
    conv_full = small_8[:, N_META:N_META + 3, 0:CONV_DIM // N_DEV].transpose(1, 0, 2).reshape(3, CONV_DIM)
    wgu1 = wgu1_8.reshape(W_GU_SHAPE)
    wd1 = wd1_8.reshape(W_D_SHAPE)
    b_f_row = jnp.pad(b_f, ((0, 0), (0, 128 - N_HEADS)))
    gmat = _group_matrix()

    h0 = jnp.concatenate([jnp.zeros((nb, PAD, D_MODEL), F32),
                          jnp.broadcast_to(meta_full[None], (nb, N_META, D_MODEL)), x], axis=1).reshape(n, D_MODEL)
    later = [w_in[0].T.astype(BF16), w_out[0].astype(BF16), ffn2_w_gu[0].T.astype(BF16), ffn2_w_down[0].astype(BF16)]
    h1, n1, gate1, up1, win_8, wout_8, wgu2_8, wd2_8 = _ffn_fwd(h0, ffn1_norm, wgu1, wd1, "ffn1_fwd", gather=later)
    wgu2 = wgu2_8.reshape(W_GU_SHAPE)
    wd2 = wd2_8.reshape(W_D_SHAPE)
    w_in_full = jnp.pad(win_8.reshape(IN_DIM, D_MODEL), ((0, IN_PAD - IN_DIM), (0, 0)))
    w_out_full = wout_8.reshape(D_MODEL, D_MODEL)

    bg, cg, hc, q, k, v, fg = _inproj_fwd(h1, mix_norm, w_in_full)
    zc = _conv_fwd(bg, cg, hc, conv_full, out_norm_conv, gmat, lp)
    ka, qa = _fgate_fwd(fg, b_f_row, lp)
    za, o, lse = _attn_fwd(q, qa, k, v, ka, out_norm_attn, lp)
    h2 = _outproj_fwd(zc, za, w_out_full, h1)
    h3, n3, gate2, up2 = _ffn_fwd(h2, ffn2_norm, wgu2, wd2, "ffn2_fwd")
    loss_part, dh3, d_final = _loss_head(h3, final_norm.reshape(1, D_MODEL), loss_target.reshape(nb * seq, D_MODEL), lp)

    dgate2, dup2, dwd2 = _ffn_bwd_act_wd(dh3, gate2, up2, wd2, "ffn2_bwd_act")
    dh2, d_ffn2 = _ffn_bwd_in(dh3, h2, ffn2_norm, dgate2, dup2, wgu2, "ffn2_bwd_in")
    dwgu2 = _ffn_bwd_wgu(n3, dgate2, dup2, "ffn2_bwd_wgu")
    dzc, dza, dwout = _outproj_bwd(dh2, zc, za, w_out_full)
    send_a = [dwgu2.reshape(N_DEV, F_CHUNK, D_MODEL), dwd2.reshape(N_DEV, F_CHUNK // 2, D_MODEL),
              dwout.reshape(N_DEV, D_MODEL // N_DEV, D_MODEL)]
    dq, dk, dv, dka, dfr, d_ga, p_wgu2, p_wd2, p_wout = _attn_bwd(
        dza, q, qa, k, v, ka, o, lse, out_norm_attn, lp, exchange=send_a)
    dfg, d_bf = _fgate_bwd(dka, dfr, fg, b_f_row, lp)
    dbg, dcg, dhc, d_conv, d_gc = _conv_bwd(dzc, bg, cg, hc, conv_full, out_norm_conv, gmat, lp)
    dh1, dwin, d_mix = _inproj_bwd([dbg, dcg, dhc, dq, dk, dv], dfg, dh2, h1, mix_norm, w_in_full)
    dwin_8 = dwin[0:IN_DIM].reshape(N_DEV, IN_DIM // N_DEV, D_MODEL)
    dgate1, dup1, dwd1, p_win = _ffn_bwd_act_wd(dh1, gate1, up1, wd1, "ffn1_bwd_act", exchange=[dwin_8])
    dwgu1 = _ffn_bwd_wgu(n1, dgate1, dup1, "ffn1_bwd_wgu")
    own = [dwgu1.reshape(N_DEV, F_CHUNK, D_MODEL), dwd1.reshape(N_DEV, F_CHUNK // 2, D_MODEL)]
    got = _pair_exchange(own, "pair_exchange_ffn1")
    chip_sums = [_pair_sum(own[0], got[0], "pair_sum_wgu1"), _pair_sum(own[1], got[1], "pair_sum_wd1")]
    dh0, d_ffn1, p_wgu1, p_wd1 = _ffn_bwd_in(
        dh1, h0, ffn1_norm, dgate1, dup1, wgu1, "ffn1_bwd_in", exchange=chip_sums)

    dh0 = dh0.reshape(nb, lp, D_MODEL)
    grad_x = dh0[:, PAD + N_META:, :]
    d_meta = jnp.sum(dh0[:, PAD:PAD + N_META, :], axis=0)

    small = _pack_small([d_ffn1, d_mix, d_ffn2, d_final], d_gc, d_ga, d_bf, d_conv, d_meta)
    (small_all,) = _all_gather([small], "gather_small_grads")
    small_sum = _sum_parts(small_all, "sum_small_grads")
    g_ffn1n, g_mixn, g_ffn2n, g_finaln = (small_sum[8 * t:8 * t + 8].reshape(1, D_MODEL) for t in range(4))
    g_gc = small_sum[32:36].reshape(1, CONV_DIM)
    g_ga = small_sum[36:40].reshape(1, ATTN_DIM)
    g_bf = small_sum[40:41, 0:N_HEADS]
    g_conv_full = small_sum[41:53].reshape(3, CONV_DIM)
    g_meta_full = small_sum[53:181].reshape(N_META, D_MODEL)
    g_conv = lax.dynamic_slice_in_dim(g_conv_full, me * (CONV_DIM // N_DEV), CONV_DIM // N_DEV, axis=1)
    g_meta = lax.dynamic_slice_in_dim(g_meta_full, me * (D_MODEL // N_DEV), D_MODEL // N_DEV, axis=1)

    weights = {
        "meta_tokens": (g_meta[None], meta_tokens, m_meta_tokens, v_meta_tokens),
        "ffn1_norm": (g_ffn1n[None], ffn1_norm, m_ffn1_norm, v_ffn1_norm),
        "ffn1_w_gu": (p_wgu1, ffn1_w_gu[0].T, m_ffn1_w_gu[0].T, v_ffn1_w_gu[0].T),
        "ffn1_w_down": (p_wd1, ffn1_w_down[0], m_ffn1_w_down[0], v_ffn1_w_down[0]),
        "mix_norm": (g_mixn[None], mix_norm, m_mix_norm, v_mix_norm),
        "w_in": (p_win, w_in[0].T, m_w_in[0].T, v_w_in[0].T),
        "conv_w": (g_conv[None], conv_w[0], m_conv_w[0], v_conv_w[0]),
        "b_f": (g_bf[None], b_f, m_b_f, v_b_f),
        "out_norm_conv": (g_gc[None], out_norm_conv, m_out_norm_conv, v_out_norm_conv),
        "out_norm_attn": (g_ga[None], out_norm_attn, m_out_norm_attn, v_out_norm_attn),
        "w_out": (p_wout, w_out[0], m_w_out[0], v_w_out[0]),
        "ffn2_norm": (g_ffn2n[None], ffn2_norm, m_ffn2_norm, v_ffn2_norm),
        "ffn2_w_gu": (p_wgu2, ffn2_w_gu[0].T, m_ffn2_w_gu[0].T, v_ffn2_w_gu[0].T),
        "ffn2_w_down": (p_wd2, ffn2_w_down[0], m_ffn2_w_down[0], v_ffn2_w_down[0]),
        "final_norm": (g_finaln[None], final_norm.reshape(1, D_MODEL), m_final_norm.reshape(1, D_MODEL),
                       v_final_norm.reshape(1, D_MODEL)),
    }
    shapes = {"meta_tokens": meta_tokens.shape, "ffn1_norm": ffn1_norm.shape, "ffn1_w_gu": ffn1_w_gu.shape,
              "ffn1_w_down": ffn1_w_down.shape, "mix_norm": mix_norm.shape, "w_in": w_in.shape,
              "conv_w": conv_w.shape, "b_f": b_f.shape, "out_norm_conv": out_norm_conv.shape,
              "out_norm_attn": out_norm_attn.shape, "w_out": w_out.shape, "ffn2_norm": ffn2_norm.shape,
              "ffn2_w_gu": ffn2_w_gu.shape, "ffn2_w_down": ffn2_w_down.shape, "final_norm": final_norm.shape}
    grads, deltas, new_m, new_v = [], [], [], []
    for name, (p, w, m, vv) in weights.items():
        g, d, nm, nv = _adamw(p, w, m, vv, "adamw_" + name)
        if name in ("ffn1_w_gu", "ffn2_w_gu", "w_in"):
            g, d, nm, nv = g.T, d.T, nm.T, nv.T
        shape = shapes[name]
        grads.append(g.reshape(shape))
        deltas.append(d.reshape(shape))
        new_m.append(nm.reshape(shape))
        new_v.append(nv.reshape(shape))

    loss = lax.psum(loss_part[0, 0], ("x", "y", "c"))
    return (loss, grad_x, *grads, *deltas, *new_m, *new_v)
```

```python
import jax
import jax.numpy as jnp
from jax import lax
from jax.experimental import pallas as pl
from jax.experimental.pallas import tpu as pltpu

F32 = jnp.float32
BF16 = jnp.bfloat16

N_DEV = 8
D_MODEL = 1024
N_META = 16
PAD = 128 - N_META
CONV_DIM = 512
ATTN_DIM = 512
HEAD_DIM = 64
N_HEADS = 8
N_PAIRS = N_HEADS // 2
D_FF = 2816
N_CHUNK = 4
F_CHUNK = D_FF // N_CHUNK
IN_DIM = 3080
IN_PAD = 3200
IN_MAIN = 3072
EPS = 1e-6
NEG = -1e30
TQ = 128
TK = 512
VMEM_LIMIT = 56 * 1024 * 1024

ADAM_LR = 0.001
ADAM_B1 = 0.9
ADAM_B2 = 0.999
ADAM_EPS = 1e-08
ADAM_WD = 0.01
ADAM_STEP = 10

MESH = pl.DeviceIdType.MESH
ANY = pl.BlockSpec(memory_space=pl.ANY)


def _params(sem=None):
    return pltpu.CompilerParams(dimension_semantics=sem, vmem_limit_bytes=VMEM_LIMIT)


def _row_tile(n, prefer):
    for t in (prefer, 512, 256, 128):
        if t <= n and n % t == 0:
            return t
    raise ValueError(f"no row tile for {n}")


def _dot(a, b):
    return jnp.dot(a, b, preferred_element_type=F32)


def _dot_nt(a, b):
    return lax.dot_general(a, b, (((1,), (1,)), ((), ())), preferred_element_type=F32)


def _dot_tn(a, b):
    return lax.dot_general(a, b, (((0,), (0,)), ((), ())), preferred_element_type=F32)


def _rms(x, g):
    r = lax.rsqrt(jnp.mean(x * x, axis=-1, keepdims=True) + EPS)
    xhat = x * r
    return xhat * g, xhat, r


def _rms_bwd(dn, xhat, r, g):
    dxhat = dn * g
    return r * (dxhat - xhat * jnp.mean(dxhat * xhat, axis=-1, keepdims=True))


def _sigmoid(x):
    return 1.0 / (1.0 + jnp.exp(-x))


def _place():
    return lax.axis_index("x"), lax.axis_index("y"), lax.axis_index("c")


def _comm_sems(nw):
    return [pltpu.SemaphoreType.DMA((nw, 7)), pltpu.SemaphoreType.DMA((nw, 7)), pltpu.SemaphoreType.DMA((nw,))]


class _Gather:
    def __init__(self, ins, outs, sems):
        self.ins, self.outs = ins, outs
        self.send, self.recv, self.local = sems
        x, y, c = _place()
        self.c = c
        self.me, self.sibling = (x, y, c), (x, y, 1 - c)
        self.chips = [(1 - x, y), (x, 1 - y), (1 - x, 1 - y)]

    def _copy(self, w, k, block, to, own=False):
        slot = self.outs[w].at[4 * block[0] + 2 * block[1] + block[2]]
        return pltpu.make_async_remote_copy(
            src_ref=self.ins[w] if own else slot, dst_ref=slot,
            send_sem=self.send.at[w, k], recv_sem=self.recv.at[w, k], device_id=to, device_id_type=MESH)

    def _mine(self, w):
        x, y, c = self.me
        return pltpu.make_async_copy(self.ins[w], self.outs[w].at[4 * x + 2 * y + c], self.local.at[w])

    def _first(self, w):
        return ([self._copy(w, 0, self.me, self.sibling, own=True)]
                + [self._copy(w, 1 + j, self.me, (*chip, self.c), own=True) for j, chip in enumerate(self.chips)])

    def _passed(self, w):
        return [self._copy(w, 4 + j, (*chip, self.c), self.sibling) for j, chip in enumerate(self.chips)]

    def start(self):
        for w in range(len(self.ins)):
            self._mine(w).start()
        for w in range(len(self.ins)):
            for cp in self._first(w):
                cp.start()

    def forward(self):
        for w in range(len(self.ins)):
            for j, chip in enumerate(self.chips):
                self._copy(w, 1 + j, (*chip, self.c), self.me).wait_recv()
                self._passed(w)[j].start()

    def finish(self):
        for w in range(len(self.ins)):
            self._copy(w, 0, self.sibling, self.me).wait_recv()
            for j, chip in enumerate(self.chips):
                self._copy(w, 4 + j, (*chip, 1 - self.c), self.me).wait_recv()
        for w in range(len(self.ins)):
            for cp in self._first(w) + self._passed(w):
                cp.wait_send()
            self._mine(w).wait()


class _Exchange:
    def __init__(self, ins, outs, sems):
        self.ins, self.outs = ins, outs
        self.send, self.recv, self.local = sems
        self.x, self.y, self.c = _place()
        self.me = 4 * self.x + 2 * self.y + self.c

    def _copy(self, w, k):
        flip = lambda v, bit: 1 - v if bit else v
        peer = (flip(self.x, ((k + 1) >> 2) & 1), flip(self.y, ((k + 1) >> 1) & 1), flip(self.c, (k + 1) & 1))
        return pltpu.make_async_remote_copy(
            src_ref=self.ins[w].at[4 * peer[0] + 2 * peer[1] + peer[2]], dst_ref=self.outs[w].at[self.me],
            send_sem=self.send.at[w, k], recv_sem=self.recv.at[w, k], device_id=peer, device_id_type=MESH)

    def _mine(self, w):
        return pltpu.make_async_copy(self.ins[w].at[self.me], self.outs[w].at[self.me], self.local.at[w])

    def start(self):
        for w in range(len(self.ins)):
            self._mine(w).start()
            for k in range(N_DEV - 1):
                self._copy(w, k).start()

    def finish(self):
        for w in range(len(self.ins)):
            for k in range(N_DEV - 1):
                self._copy(w, k).wait()
            self._mine(w).wait()


class _PairExchange:
    def __init__(self, ins, outs, sems):
        self.ins, self.outs = ins, outs
        self.send, self.recv, _ = sems
        x, y, self.c = _place()
        self.sibling = (x, y, 1 - self.c)

    def _copy(self, w, t):
        return pltpu.make_async_remote_copy(
            src_ref=self.ins[w].at[2 * t + 1 - self.c], dst_ref=self.outs[w].at[t],
            send_sem=self.send.at[w, t], recv_sem=self.recv.at[w, t], device_id=self.sibling, device_id_type=MESH)

    def start(self):
        for w in range(len(self.ins)):
            for t in range(4):
                self._copy(w, t).start()

    def finish(self):
        for w in range(len(self.ins)):
            for t in range(4):
                self._copy(w, t).wait()


class _ChipExchange:
    def __init__(self, ins, outs, sems):
        self.ins, self.outs = ins, outs
        self.send, self.recv, self.local = sems
        self.x, self.y, self.c = _place()
        self.chip = 2 * self.x + self.y

    def _copy(self, w, k):
        flip = lambda v, bit: 1 - v if bit else v
        px, py = flip(self.x, ((k + 1) >> 1) & 1), flip(self.y, (k + 1) & 1)
        return pltpu.make_async_remote_copy(
            src_ref=self.ins[w].at[2 * px + py], dst_ref=self.outs[w].at[self.chip],
            send_sem=self.send.at[w, k], recv_sem=self.recv.at[w, k], device_id=(px, py, self.c),
            device_id_type=MESH)

    def _mine(self, w):
        return pltpu.make_async_copy(self.ins[w].at[self.chip], self.outs[w].at[self.chip], self.local.at[w])

    def start(self):
        for w in range(len(self.ins)):
            self._mine(w).start()
            for k in range(3):
                self._copy(w, k).start()

    def finish(self):
        for w in range(len(self.ins)):
            for k in range(3):
                self._copy(w, k).wait()
            self._mine(w).wait()


def _pair_exchange(xs, name):
    nw = len(xs)

    def body(*refs):
        comm = _PairExchange(refs[:nw], refs[nw:2 * nw], refs[2 * nw:])
        comm.start()
        comm.finish()

    return pl.pallas_call(
        body, name=name, in_specs=[ANY] * nw, out_specs=[ANY] * nw,
        out_shape=[jax.ShapeDtypeStruct((4,) + a.shape[1:], a.dtype) for a in xs],
        scratch_shapes=_comm_sems(nw),
    )(*xs)


def _pair_sum(own, got, name):
    _, r, c = own.shape
    tr = r
    for t in (256, 128, 64, 32, 16):
        if r % t == 0 and r > t:
            tr = t
            break

    def body(own_ref, got_ref, out_ref):
        mine = jnp.where(lax.axis_index("c") == 0, own_ref[:, 0].astype(F32), own_ref[:, 1].astype(F32))
        out_ref[...] = (mine + got_ref[...].astype(F32)).astype(BF16)

    return pl.pallas_call(
        body, name=name, grid=(r // tr,),
        in_specs=[pl.BlockSpec((4, 2, tr, c), lambda i: (0, 0, i, 0)), pl.BlockSpec((4, tr, c), lambda i: (0, i, 0))],
        out_specs=pl.BlockSpec((4, tr, c), lambda i: (0, i, 0)),
        out_shape=jax.ShapeDtypeStruct((4, r, c), BF16),
        compiler_params=_params(("parallel",)),
    )(own.reshape(4, 2, r, c), got)


def _split_refs(refs, n_in, n_comm, n_out, n_scr):
    a = n_in
    b = a + n_comm
    c = b + n_out
    d = c + n_comm
    e = d + n_scr
    return refs[:a], refs[a:b], refs[b:c], refs[c:d], refs[d:e], refs[e:]


def _all_gather(xs, name):
    nw = len(xs)

    def body(*refs):
        comm = _Gather(refs[:nw], refs[nw:2 * nw], refs[2 * nw:])
        comm.start()
        comm.forward()
        comm.finish()

    return pl.pallas_call(
        body, name=name, in_specs=[ANY] * nw, out_specs=[ANY] * nw,
        out_shape=[jax.ShapeDtypeStruct((N_DEV,) + a.shape, a.dtype) for a in xs],
        scratch_shapes=_comm_sems(nw),
    )(*xs)


def _ffn_fwd(h, gain, wgu, wd, name, gather=()):
    n = h.shape[0]
    tm = _row_tile(n, 512)
    n_i = n // tm
    nw = len(gather)

    def body(*refs):
        (h_ref, g_ref, wgu_ref, wd_ref), gin, (out_ref, gate_ref, up_ref), gout, (n_scr, acc_scr), sems = \
            _split_refs(refs, 4, nw, 3, 2)
        i = pl.program_id(0)
        j = pl.program_id(1)
        if nw:
            comm = _Gather(gin, gout, sems)
            pl.when((i == 0) & (j == 0))(comm.start)
            pl.when((i == (3 * n_i) // 4) & (j == 0))(comm.forward)

        @pl.when(j == 0)
        def _():
            y, _, _ = _rms(h_ref[...], g_ref[...])
            n_scr[...] = y.astype(BF16)
            acc_scr[...] = jnp.zeros_like(acc_scr)

        nb = n_scr[...]
        gate = _dot(nb, wgu_ref[0, 0])
        up = _dot(nb, wgu_ref[1, 0])
        gate_ref[0] = gate.astype(BF16)
        up_ref[0] = up.astype(BF16)
        act = (gate * _sigmoid(gate) * up).astype(BF16)
        acc_scr[...] += _dot(act, wd_ref[0])

        @pl.when(j == N_CHUNK - 1)
        def _():
            out_ref[...] = h_ref[...] + 0.5 * acc_scr[...]

        if nw:
            pl.when((i == n_i - 1) & (j == N_CHUNK - 1))(comm.finish)

    return pl.pallas_call(
        body, name=name, grid=(n_i, N_CHUNK),
        in_specs=[pl.BlockSpec((tm, D_MODEL), lambda i, j: (i, 0)),
                  pl.BlockSpec((1, D_MODEL), lambda i, j: (0, 0)),
                  pl.BlockSpec((2, 1, D_MODEL, F_CHUNK), lambda i, j: (0, j, 0, 0)),
                  pl.BlockSpec((1, F_CHUNK, D_MODEL), lambda i, j: (j, 0, 0))] + [ANY] * nw,
        out_specs=[pl.BlockSpec((tm, D_MODEL), lambda i, j: (i, 0)),
                   pl.BlockSpec((1, tm, F_CHUNK), lambda i, j: (j, i, 0)),
                   pl.BlockSpec((1, tm, F_CHUNK), lambda i, j: (j, i, 0))] + [ANY] * nw,
        out_shape=[jax.ShapeDtypeStruct((n, D_MODEL), F32),
                   jax.ShapeDtypeStruct((N_CHUNK, n, F_CHUNK), BF16),
                   jax.ShapeDtypeStruct((N_CHUNK, n, F_CHUNK), BF16)]
        + [jax.ShapeDtypeStruct((N_DEV,) + a.shape, a.dtype) for a in gather],
        scratch_shapes=[pltpu.VMEM((tm, D_MODEL), BF16), pltpu.VMEM((tm, D_MODEL), F32)]
        + (_comm_sems(nw) if nw else []),
        compiler_params=_params(("arbitrary", "arbitrary")),
    )(h, gain, wgu, wd, *gather)


def _ffn_bwd_x(dh_out, h_in, gain, gate, up, wgu, wd, name):
    n = h_in.shape[0]
    tm = _row_tile(n, 512)

    def body(dh_ref, h_ref, g_ref, gate_ref, up_ref, wgu_ref, wd_ref,
             dhin_ref, dgate_ref, dup_ref, dgain_ref, dhb_scr, acc_scr):
        i = pl.program_id(0)
        j = pl.program_id(1)

        @pl.when((i == 0) & (j == 0))
        def _():
            dgain_ref[...] = jnp.zeros_like(dgain_ref)

        @pl.when(j == 0)
        def _():
            dhb_scr[...] = (0.5 * dh_ref[...]).astype(BF16)
            acc_scr[...] = jnp.zeros_like(acc_scr)

        da = _dot_nt(dhb_scr[...], wd_ref[0])
        g = gate_ref[0].astype(F32)
        u = up_ref[0].astype(F32)
        sig = _sigmoid(g)
        dgate = (da * u * (sig * (1.0 + g * (1.0 - sig)))).astype(BF16)
        dup = (da * (g * sig)).astype(BF16)
        dgate_ref[0] = dgate
        dup_ref[0] = dup
        acc_scr[...] += _dot_nt(dgate, wgu_ref[0, 0]) + _dot_nt(dup, wgu_ref[1, 0])

        @pl.when(j == N_CHUNK - 1)
        def _():
            gain_v = g_ref[...]
            _, xhat, r = _rms(h_ref[...], gain_v)
            dn = acc_scr[...]
            dhin_ref[...] = dh_ref[...] + _rms_bwd(dn, xhat, r, gain_v)
            dgain_ref[...] += jnp.sum(dn * xhat, axis=0, keepdims=True)

    chunk = pl.BlockSpec((1, tm, F_CHUNK), lambda i, j: (j, i, 0))
    rows = pl.BlockSpec((tm, D_MODEL), lambda i, j: (i, 0))
    vec = pl.BlockSpec((1, D_MODEL), lambda i, j: (0, 0))
    return pl.pallas_call(
        body, name=name, grid=(n // tm, N_CHUNK),
        in_specs=[rows, rows, vec, chunk, chunk,
                  pl.BlockSpec((2, 1, D_MODEL, F_CHUNK), lambda i, j: (0, j, 0, 0)),
                  pl.BlockSpec((1, F_CHUNK, D_MODEL), lambda i, j: (j, 0, 0))],
        out_specs=[rows, chunk, chunk, vec],
        out_shape=[jax.ShapeDtypeStruct((n, D_MODEL), F32),
                   jax.ShapeDtypeStruct((N_CHUNK, n, F_CHUNK), BF16),
                   jax.ShapeDtypeStruct((N_CHUNK, n, F_CHUNK), BF16),
                   jax.ShapeDtypeStruct((1, D_MODEL), F32)],
        scratch_shapes=[pltpu.VMEM((tm, D_MODEL), BF16), pltpu.VMEM((tm, D_MODEL), F32)],
        compiler_params=_params(("arbitrary", "arbitrary")),
    )(dh_out, h_in, gain, gate, up, wgu, wd)


def _ffn_bwd_act(dh_out, gate, up, wd, name):
    n = dh_out.shape[0]
    tm = _row_tile(n, 512)

    def body(dh_ref, gate_ref, up_ref, wd_ref, dgate_ref, dup_ref, dhb_scr):
        @pl.when(pl.program_id(1) == 0)
        def _():
            dhb_scr[...] = (0.5 * dh_ref[...]).astype(BF16)

        da = _dot_nt(dhb_scr[...], wd_ref[0])
        g = gate_ref[0].astype(F32)
        u = up_ref[0].astype(F32)
        sig = _sigmoid(g)
        dgate_ref[0] = (da * u * (sig * (1.0 + g * (1.0 - sig)))).astype(BF16)
        dup_ref[0] = (da * (g * sig)).astype(BF16)

    chunk = pl.BlockSpec((1, tm, F_CHUNK), lambda i, j: (j, i, 0))
    return pl.pallas_call(
        body, name=name, grid=(n // tm, N_CHUNK),
        in_specs=[pl.BlockSpec((tm, D_MODEL), lambda i, j: (i, 0)), chunk, chunk,
                  pl.BlockSpec((1, F_CHUNK, D_MODEL), lambda i, j: (j, 0, 0))],
        out_specs=[chunk, chunk],
        out_shape=[jax.ShapeDtypeStruct((N_CHUNK, n, F_CHUNK), BF16)] * 2,
        scratch_shapes=[pltpu.VMEM((tm, D_MODEL), BF16)],
        compiler_params=_params(("parallel", "arbitrary")),
    )(dh_out, gate, up, wd)


def _ffn_bwd_in(dh_out, h_in, gain, dgate, dup, wgu, name, exchange=()):
    n = h_in.shape[0]
    tm = _row_tile(n, 512)
    n_i = n // tm
    nw = len(exchange)

    def body(*refs):
        (dh_ref, h_ref, g_ref, dgate_ref, dup_ref, wgu_ref), xin, (dhin_ref, dgain_ref), xout, (acc_scr,), sems = \
            _split_refs(refs, 6, nw, 2, 1)
        i = pl.program_id(0)
        j = pl.program_id(1)
        if nw:
            comm = _Exchange(xin, xout, sems)
            pl.when((i == 0) & (j == 0))(comm.start)

        @pl.when((i == 0) & (j == 0))
        def _():
            dgain_ref[...] = jnp.zeros_like(dgain_ref)

        @pl.when(j == 0)
        def _():
            acc_scr[...] = jnp.zeros_like(acc_scr)

        acc_scr[...] += _dot_nt(dgate_ref[0], wgu_ref[0, 0]) + _dot_nt(dup_ref[0], wgu_ref[1, 0])

        @pl.when(j == N_CHUNK - 1)
        def _():
            gain_v = g_ref[...]
            _, xhat, r = _rms(h_ref[...], gain_v)
            dn = acc_scr[...]
            dhin_ref[...] = dh_ref[...] + _rms_bwd(dn, xhat, r, gain_v)
            dgain_ref[...] += jnp.sum(dn * xhat, axis=0, keepdims=True)

        if nw:
            pl.when((i == n_i - 1) & (j == N_CHUNK - 1))(comm.finish)

    chunk = pl.BlockSpec((1, tm, F_CHUNK), lambda i, j: (j, i, 0))
    rows = pl.BlockSpec((tm, D_MODEL), lambda i, j: (i, 0))
    vec = pl.BlockSpec((1, D_MODEL), lambda i, j: (0, 0))
    return pl.pallas_call(
        body, name=name, grid=(n_i, N_CHUNK),
        in_specs=[rows, rows, vec, chunk, chunk,
                  pl.BlockSpec((2, 1, D_MODEL, F_CHUNK), lambda i, j: (0, j, 0, 0))] + [ANY] * nw,
        out_specs=[rows, vec] + [ANY] * nw,
        out_shape=[jax.ShapeDtypeStruct((n, D_MODEL), F32), jax.ShapeDtypeStruct((1, D_MODEL), F32)]
        + [jax.ShapeDtypeStruct(a.shape, a.dtype) for a in exchange],
        scratch_shapes=[pltpu.VMEM((tm, D_MODEL), F32)] + (_comm_sems(nw) if nw else []),
        compiler_params=_params(("arbitrary", "arbitrary")),
    )(dh_out, h_in, gain, dgate, dup, wgu, *exchange)


def _ffn_bwd_w(dh_out, h_in, gain, gate, up, dgate, dup, name):
    n = h_in.shape[0]
    tm = _row_tile(n, 512)
    n_i = n // tm

    def body(dh_ref, h_ref, g_ref, gate_ref, up_ref, dgate_ref, dup_ref, dwgu_ref, dwd_ref,
             ag_scr, au_scr, ad_scr):
        i = pl.program_id(1)

        @pl.when(i == 0)
        def _():
            ag_scr[...] = jnp.zeros_like(ag_scr)
            au_scr[...] = jnp.zeros_like(au_scr)
            ad_scr[...] = jnp.zeros_like(ad_scr)

        y, _, _ = _rms(h_ref[...], g_ref[...])
        nb = y.astype(BF16)
        ag_scr[...] += _dot_tn(nb, dgate_ref[0])
        au_scr[...] += _dot_tn(nb, dup_ref[0])
        g = gate_ref[0].astype(F32)
        act = (g * _sigmoid(g) * up_ref[0].astype(F32)).astype(BF16)
        ad_scr[...] += _dot_tn(act, (0.5 * dh_ref[...]).astype(BF16))

        @pl.when(i == n_i - 1)
        def _():
            dwgu_ref[0, 0] = ag_scr[...].astype(BF16)
            dwgu_ref[1, 0] = au_scr[...].astype(BF16)
            dwd_ref[0] = ad_scr[...].astype(BF16)

    chunk = pl.BlockSpec((1, tm, F_CHUNK), lambda j, i: (j, i, 0))
    rows = pl.BlockSpec((tm, D_MODEL), lambda j, i: (i, 0))
    return pl.pallas_call(
        body, name=name, grid=(N_CHUNK, n_i),
        in_specs=[rows, rows, pl.BlockSpec((1, D_MODEL), lambda j, i: (0, 0)), chunk, chunk, chunk, chunk],
        out_specs=[pl.BlockSpec((2, 1, D_MODEL, F_CHUNK), lambda j, i: (0, j, 0, 0)),
                   pl.BlockSpec((1, F_CHUNK, D_MODEL), lambda j, i: (j, 0, 0))],
        out_shape=[jax.ShapeDtypeStruct((2, N_CHUNK, D_MODEL, F_CHUNK), BF16),
                   jax.ShapeDtypeStruct((N_CHUNK, F_CHUNK, D_MODEL), BF16)],
        scratch_shapes=[pltpu.VMEM((D_MODEL, F_CHUNK), F32), pltpu.VMEM((D_MODEL, F_CHUNK), F32),
                        pltpu.VMEM((F_CHUNK, D_MODEL), F32)],
        compiler_params=_params(("parallel", "arbitrary")),
    )(dh_out, h_in, gain, gate, up, dgate, dup)


def _resident(shape, rank):
    zeros = (0,) * len(shape)
    index_map = (lambda i: zeros) if rank == 1 else (lambda i, j: zeros)
    return pl.BlockSpec(shape, index_map, pipeline_mode=pl.Buffered(1))


W_GU_SHAPE = (2, N_CHUNK, F_CHUNK, D_MODEL)
W_D_SHAPE = (N_CHUNK, F_CHUNK, D_MODEL)


def _ffn_fwd(h, gain, wgu, wd, name, gather=()):
    n = h.shape[0]
    tm = _row_tile(n, 512)
    n_i = n // tm
    nw = len(gather)

    def body(*refs):
        (h_ref, g_ref, wgu_ref, wd_ref), gin, (out_ref, nrm_ref, gate_ref, up_ref), gout, _, sems = \
            _split_refs(refs, 4, nw, 4, 0)
        i = pl.program_id(0)
        if nw:
            comm = _Gather(gin, gout, sems)
            pl.when(i == 0)(comm.start)
            pl.when(i == max(n_i - 3, 0))(comm.forward)

        hv = h_ref[...]
        y, _, _ = _rms(hv, g_ref[...])
        nb = y.astype(BF16)
        nrm_ref[...] = nb
        acc = jnp.zeros((tm, D_MODEL), F32)
        for j in range(N_CHUNK):
            gate = _dot_nt(nb, wgu_ref[0, j])
            up = _dot_nt(nb, wgu_ref[1, j])
            gate_ref[j] = gate.astype(BF16)
            up_ref[j] = up.astype(BF16)
            acc = acc + _dot((gate * _sigmoid(gate) * up).astype(BF16), wd_ref[j])
        out_ref[...] = hv + 0.5 * acc

        if nw:
            pl.when(i == n_i - 1)(comm.finish)

    rows = pl.BlockSpec((tm, D_MODEL), lambda i: (i, 0))
    chunks = pl.BlockSpec((N_CHUNK, tm, F_CHUNK), lambda i: (0, i, 0))
    return pl.pallas_call(
        body, name=name, grid=(n_i,),
        in_specs=[rows, pl.BlockSpec((1, D_MODEL), lambda i: (0, 0)), _resident(W_GU_SHAPE, 1),
                  _resident(W_D_SHAPE, 1)] + [ANY] * nw,
        out_specs=[rows, rows, chunks, chunks] + [ANY] * nw,
        out_shape=[jax.ShapeDtypeStruct((n, D_MODEL), F32), jax.ShapeDtypeStruct((n, D_MODEL), BF16),
                   jax.ShapeDtypeStruct((N_CHUNK, n, F_CHUNK), BF16),
                   jax.ShapeDtypeStruct((N_CHUNK, n, F_CHUNK), BF16)]
        + [jax.ShapeDtypeStruct((N_DEV,) + a.shape, a.dtype) for a in gather],
        scratch_shapes=_comm_sems(nw) if nw else [],
        compiler_params=_params(("arbitrary",)),
    )(h, gain, wgu, wd, *gather)


def _swiglu_bwd(da, gate_ref, up_ref, j):
    g = gate_ref[j].astype(F32)
    u = up_ref[j].astype(F32)
    sig = _sigmoid(g)
    return (da * u * (sig * (1.0 + g * (1.0 - sig)))).astype(BF16), (da * (g * sig)).astype(BF16)


def _ffn_bwd_x(dh_out, h_in, gain, gate, up, wgu, wd, name):
    n = h_in.shape[0]
    tm = _row_tile(n, 256)

    def body(dh_ref, h_ref, g_ref, gate_ref, up_ref, wgu_ref, wd_ref,
             dhin_ref, dhb_ref, dgate_ref, dup_ref, dgain_ref):
        @pl.when(pl.program_id(0) == 0)
        def _():
            dgain_ref[...] = jnp.zeros_like(dgain_ref)

        dhv = dh_ref[...]
        dhb = (0.5 * dhv).astype(BF16)
        dhb_ref[...] = dhb
        dn = jnp.zeros((tm, D_MODEL), F32)
        for j in range(N_CHUNK):
            dgate, dup = _swiglu_bwd(_dot_nt(dhb, wd_ref[j]), gate_ref, up_ref, j)
            dgate_ref[j] = dgate
            dup_ref[j] = dup
            dn = dn + _dot(dgate, wgu_ref[0, j]) + _dot(dup, wgu_ref[1, j])
        gain_v = g_ref[...]
        _, xhat, r = _rms(h_ref[...], gain_v)
        dhin_ref[...] = dhv + _rms_bwd(dn, xhat, r, gain_v)
        dgain_ref[...] += jnp.sum(dn * xhat, axis=0, keepdims=True)

    rows = pl.BlockSpec((tm, D_MODEL), lambda i: (i, 0))
    chunks = pl.BlockSpec((N_CHUNK, tm, F_CHUNK), lambda i: (0, i, 0))
    vec = pl.BlockSpec((1, D_MODEL), lambda i: (0, 0))
    return pl.pallas_call(
        body, name=name, grid=(n // tm,),
        in_specs=[rows, rows, vec, chunks, chunks, _resident(W_GU_SHAPE, 1), _resident(W_D_SHAPE, 1)],
        out_specs=[rows, rows, chunks, chunks, vec],
        out_shape=[jax.ShapeDtypeStruct((n, D_MODEL), F32), jax.ShapeDtypeStruct((n, D_MODEL), BF16),
                   jax.ShapeDtypeStruct((N_CHUNK, n, F_CHUNK), BF16),
                   jax.ShapeDtypeStruct((N_CHUNK, n, F_CHUNK), BF16),
                   jax.ShapeDtypeStruct((1, D_MODEL), F32)],
        compiler_params=_params(("arbitrary",)),
    )(dh_out, h_in, gain, gate, up, wgu, wd)


def _ffn_bwd_act(dh_out, gate, up, wd, name, exchange=()):
    n = dh_out.shape[0]
    tm = _row_tile(n, 512)
    n_i = n // tm
    nw = len(exchange)

    def body(*refs):
        (dh_ref, gate_ref, up_ref, wd_ref), xin, (dhb_ref, dgate_ref, dup_ref), xout, _, sems = \
            _split_refs(refs, 4, nw, 3, 0)
        i = pl.program_id(0)
        if nw:
            comm = _Exchange(xin, xout, sems)
            pl.when(i == 0)(comm.start)

        dhb = (0.5 * dh_ref[...]).astype(BF16)
        dhb_ref[...] = dhb
        for j in range(N_CHUNK):
            dgate_ref[j], dup_ref[j] = _swiglu_bwd(_dot_nt(dhb, wd_ref[j]), gate_ref, up_ref, j)

        if nw:
            pl.when(i == n_i - 1)(comm.finish)

    rows = pl.BlockSpec((tm, D_MODEL), lambda i: (i, 0))
    chunks = pl.BlockSpec((N_CHUNK, tm, F_CHUNK), lambda i: (0, i, 0))
    return pl.pallas_call(
        body, name=name, grid=(n_i,),
        in_specs=[rows, chunks, chunks, _resident(W_D_SHAPE, 1)] + [ANY] * nw,
        out_specs=[rows, chunks, chunks] + [ANY] * nw,
        out_shape=[jax.ShapeDtypeStruct((n, D_MODEL), BF16)] + [jax.ShapeDtypeStruct((N_CHUNK, n, F_CHUNK), BF16)] * 2
        + [jax.ShapeDtypeStruct(a.shape, a.dtype) for a in exchange],
        scratch_shapes=_comm_sems(nw) if nw else [],
        compiler_params=_params(("arbitrary",)),
    )(dh_out, gate, up, wd, *exchange)


def _ffn_bwd_in(dh_out, h_in, gain, dgate, dup, wgu, name, exchange=()):
    n = h_in.shape[0]
    tm = _row_tile(n, 512)
    n_i = n // tm
    nw = len(exchange)

    def body(*refs):
        (dh_ref, h_ref, g_ref, dgate_ref, dup_ref, wgu_ref), xin, (dhin_ref, dgain_ref), xout, _, sems = \
            _split_refs(refs, 6, nw, 2, 0)
        i = pl.program_id(0)
        if nw:
            comm = _ChipExchange(xin, xout, sems)
            pl.when(i == 0)(comm.start)

        @pl.when(i == 0)
        def _():
            dgain_ref[...] = jnp.zeros_like(dgain_ref)

        dn = jnp.zeros((tm, D_MODEL), F32)
        for j in range(N_CHUNK):
            dn = dn + _dot(dgate_ref[j], wgu_ref[0, j]) + _dot(dup_ref[j], wgu_ref[1, j])
        gain_v = g_ref[...]
        _, xhat, r = _rms(h_ref[...], gain_v)
        dhin_ref[...] = dh_ref[...] + _rms_bwd(dn, xhat, r, gain_v)
        dgain_ref[...] += jnp.sum(dn * xhat, axis=0, keepdims=True)

        if nw:
            pl.when(i == n_i - 1)(comm.finish)

    rows = pl.BlockSpec((tm, D_MODEL), lambda i: (i, 0))
    chunks = pl.BlockSpec((N_CHUNK, tm, F_CHUNK), lambda i: (0, i, 0))
    vec = pl.BlockSpec((1, D_MODEL), lambda i: (0, 0))
    return pl.pallas_call(
        body, name=name, grid=(n_i,),
        in_specs=[rows, rows, vec, chunks, chunks, _resident(W_GU_SHAPE, 1)] + [ANY] * nw,
        out_specs=[rows, vec] + [ANY] * nw,
        out_shape=[jax.ShapeDtypeStruct((n, D_MODEL), F32), jax.ShapeDtypeStruct((1, D_MODEL), F32)]
        + [jax.ShapeDtypeStruct(a.shape, a.dtype) for a in exchange],
        scratch_shapes=_comm_sems(nw) if nw else [],
        compiler_params=_params(("arbitrary",)),
    )(dh_out, h_in, gain, dgate, dup, wgu, *exchange)


W_GROUP = 2


def _ffn_bwd_w(dhb, nrm, gate, up, dgate, dup, name):
    n = nrm.shape[0]
    tm = _row_tile(n, 512)
    n_i = n // tm

    def body(dhb_ref, nrm_ref, gate_ref, up_ref, dgate_ref, dup_ref, dwgu_ref, dwd_ref, ag_scr, au_scr, ad_scr):
        i = pl.program_id(1)

        @pl.when(i == 0)
        def _():
            ag_scr[...] = jnp.zeros_like(ag_scr)
            au_scr[...] = jnp.zeros_like(au_scr)
            ad_scr[...] = jnp.zeros_like(ad_scr)

        nb = nrm_ref[...]
        dhv = dhb_ref[...]
        for jj in range(W_GROUP):
            ag_scr[jj] += _dot_tn(dgate_ref[jj], nb)
            au_scr[jj] += _dot_tn(dup_ref[jj], nb)
            g = gate_ref[jj].astype(F32)
            act = (g * _sigmoid(g) * up_ref[jj].astype(F32)).astype(BF16)
            ad_scr[jj] += _dot_tn(act, dhv)

        @pl.when(i == n_i - 1)
        def _():
            dwgu_ref[0] = ag_scr[...].astype(BF16)
            dwgu_ref[1] = au_scr[...].astype(BF16)
            dwd_ref[...] = ad_scr[...].astype(BF16)

    chunks = pl.BlockSpec((W_GROUP, tm, F_CHUNK), lambda g, i: (g, i, 0))
    rows = pl.BlockSpec((tm, D_MODEL), lambda g, i: (i, 0))
    return pl.pallas_call(
        body, name=name, grid=(N_CHUNK // W_GROUP, n_i),
        in_specs=[rows, rows, chunks, chunks, chunks, chunks],
        out_specs=[pl.BlockSpec((2, W_GROUP, F_CHUNK, D_MODEL), lambda g, i: (0, g, 0, 0)),
                   pl.BlockSpec((W_GROUP, F_CHUNK, D_MODEL), lambda g, i: (g, 0, 0))],
        out_shape=[jax.ShapeDtypeStruct(W_GU_SHAPE, BF16), jax.ShapeDtypeStruct(W_D_SHAPE, BF16)],
        scratch_shapes=[pltpu.VMEM((W_GROUP, F_CHUNK, D_MODEL), F32), pltpu.VMEM((W_GROUP, F_CHUNK, D_MODEL), F32),
                        pltpu.VMEM((W_GROUP, F_CHUNK, D_MODEL), F32)],
        compiler_params=_params(("parallel", "arbitrary")),
    )(dhb, nrm, gate, up, dgate, dup)


HID_PIECES = ((0, 1024), (1024, 2048), (2048, D_FF))
W_GU_SHAPE = (2, D_FF, D_MODEL)
W_D_SHAPE = (D_FF, D_MODEL)


def _ffn_fwd(h, gain, wgu, wd, name, gather=()):
    n = h.shape[0]
    tm = _row_tile(n, 512)
    n_i = n // tm
    nw = len(gather)

    def body(*refs):
        (h_ref, g_ref, wgu_ref, wd_ref), gin, (out_ref, nrm_ref, gate_ref, up_ref), gout, _, sems = \
            _split_refs(refs, 4, nw, 4, 0)
        i = pl.program_id(0)
        if nw:
            comm = _Gather(gin, gout, sems)
            pl.when(i == 0)(comm.start)
            pl.when(i == max(n_i - 3, 0))(comm.forward)

        hv = h_ref[...]
        y, _, _ = _rms(hv, g_ref[...])
        nb = y.astype(BF16)
        nrm_ref[...] = nb
        acc = jnp.zeros((tm, D_MODEL), F32)
        for a, b in HID_PIECES:
            gate = _dot_nt(nb, wgu_ref[0, a:b, :])
            up = _dot_nt(nb, wgu_ref[1, a:b, :])
            gate_ref[:, a:b] = gate.astype(BF16)
            up_ref[:, a:b] = up.astype(BF16)
            acc = acc + _dot((gate * _sigmoid(gate) * up).astype(BF16), wd_ref[a:b, :])
        out_ref[...] = hv + 0.5 * acc

        if nw:
            pl.when(i == n_i - 1)(comm.finish)

    rows = pl.BlockSpec((tm, D_MODEL), lambda i: (i, 0))
    hid = pl.BlockSpec((tm, D_FF), lambda i: (i, 0))
    return pl.pallas_call(
        body, name=name, grid=(n_i,),
        in_specs=[rows, pl.BlockSpec((1, D_MODEL), lambda i: (0, 0)), _resident(W_GU_SHAPE, 1),
                  _resident(W_D_SHAPE, 1)] + [ANY] * nw,
        out_specs=[rows, rows, hid, hid] + [ANY] * nw,
        out_shape=[jax.ShapeDtypeStruct((n, D_MODEL), F32), jax.ShapeDtypeStruct((n, D_MODEL), BF16),
                   jax.ShapeDtypeStruct((n, D_FF), BF16), jax.ShapeDtypeStruct((n, D_FF), BF16)]
        + [jax.ShapeDtypeStruct((N_DEV,) + a.shape, a.dtype) for a in gather],
        scratch_shapes=_comm_sems(nw) if nw else [],
        compiler_params=_params(("arbitrary",)),
    )(h, gain, wgu, wd, *gather)


def _swiglu_bwd(da, gate_ref, up_ref, a, b):
    g = gate_ref[:, a:b].astype(F32)
    u = up_ref[:, a:b].astype(F32)
    sig = _sigmoid(g)
    return (da * u * (sig * (1.0 + g * (1.0 - sig)))).astype(BF16), (da * (g * sig)).astype(BF16)


def _ffn_bwd_x(dh_out, h_in, gain, gate, up, wgu, wd, name):
    n = h_in.shape[0]
    tm = _row_tile(n, 256)

    def body(dh_ref, h_ref, g_ref, gate_ref, up_ref, wgu_ref, wd_ref,
             dhin_ref, dhb_ref, dgate_ref, dup_ref, dgain_ref):
        @pl.when(pl.program_id(0) == 0)
        def _():
            dgain_ref[...] = jnp.zeros_like(dgain_ref)

        dhv = dh_ref[...]
        dhb = (0.5 * dhv).astype(BF16)
        dhb_ref[...] = dhb
        dn = jnp.zeros((tm, D_MODEL), F32)
        for a, b in HID_PIECES:
            dgate, dup = _swiglu_bwd(_dot_nt(dhb, wd_ref[a:b, :]), gate_ref, up_ref, a, b)
            dgate_ref[:, a:b] = dgate
            dup_ref[:, a:b] = dup
            dn = dn + _dot(dgate, wgu_ref[0, a:b, :]) + _dot(dup, wgu_ref[1, a:b, :])
        gain_v = g_ref[...]
        _, xhat, r = _rms(h_ref[...], gain_v)
        dhin_ref[...] = dhv + _rms_bwd(dn, xhat, r, gain_v)
        dgain_ref[...] += jnp.sum(dn * xhat, axis=0, keepdims=True)

    rows = pl.BlockSpec((tm, D_MODEL), lambda i: (i, 0))
    hid = pl.BlockSpec((tm, D_FF), lambda i: (i, 0))
    vec = pl.BlockSpec((1, D_MODEL), lambda i: (0, 0))
    return pl.pallas_call(
        body, name=name, grid=(n // tm,),
        in_specs=[rows, rows, vec, hid, hid, _resident(W_GU_SHAPE, 1), _resident(W_D_SHAPE, 1)],
        out_specs=[rows, rows, hid, hid, vec],
        out_shape=[jax.ShapeDtypeStruct((n, D_MODEL), F32), jax.ShapeDtypeStruct((n, D_MODEL), BF16),
                   jax.ShapeDtypeStruct((n, D_FF), BF16), jax.ShapeDtypeStruct((n, D_FF), BF16),
                   jax.ShapeDtypeStruct((1, D_MODEL), F32)],
        compiler_params=_params(("arbitrary",)),
    )(dh_out, h_in, gain, gate, up, wgu, wd)


def _ffn_bwd_act(dh_out, gate, up, wd, name, exchange=()):
    n = dh_out.shape[0]
    tm = _row_tile(n, 512)
    n_i = n // tm
    nw = len(exchange)

    def body(*refs):
        (dh_ref, gate_ref, up_ref, wd_ref), xin, (dhb_ref, dgate_ref, dup_ref), xout, _, sems = \
            _split_refs(refs, 4, nw, 3, 0)
        i = pl.program_id(0)
        if nw:
            comm = _Exchange(xin, xout, sems)
            pl.when(i == 0)(comm.start)

        dhb = (0.5 * dh_ref[...]).astype(BF16)
        dhb_ref[...] = dhb
        for a, b in HID_PIECES:
            dgate_ref[:, a:b], dup_ref[:, a:b] = _swiglu_bwd(_dot_nt(dhb, wd_ref[a:b, :]), gate_ref, up_ref, a, b)

        if nw:
            pl.when(i == n_i - 1)(comm.finish)

    rows = pl.BlockSpec((tm, D_MODEL), lambda i: (i, 0))
    hid = pl.BlockSpec((tm, D_FF), lambda i: (i, 0))
    return pl.pallas_call(
        body, name=name, grid=(n_i,),
        in_specs=[rows, hid, hid, _resident(W_D_SHAPE, 1)] + [ANY] * nw,
        out_specs=[rows, hid, hid] + [ANY] * nw,
        out_shape=[jax.ShapeDtypeStruct((n, D_MODEL), BF16)] + [jax.ShapeDtypeStruct((n, D_FF), BF16)] * 2
        + [jax.ShapeDtypeStruct(a.shape, a.dtype) for a in exchange],
        scratch_shapes=_comm_sems(nw) if nw else [],
        compiler_params=_params(("arbitrary",)),
    )(dh_out, gate, up, wd, *exchange)


def _ffn_bwd_in(dh_out, h_in, gain, dgate, dup, wgu, name, exchange=()):
    n = h_in.shape[0]
    tm = _row_tile(n, 512)
    n_i = n // tm
    nw = len(exchange)

    def body(*refs):
        (dh_ref, h_ref, g_ref, dgate_ref, dup_ref, wgu_ref), xin, (dhin_ref, dgain_ref), xout, _, sems = \
            _split_refs(refs, 6, nw, 2, 0)
        i = pl.program_id(0)
        if nw:
            comm = _ChipExchange(xin, xout, sems)
            pl.when(i == 0)(comm.start)

        @pl.when(i == 0)
        def _():
            dgain_ref[...] = jnp.zeros_like(dgain_ref)

        dn = jnp.zeros((tm, D_MODEL), F32)
        for a, b in HID_PIECES:
            dn = dn + _dot(dgate_ref[:, a:b], wgu_ref[0, a:b, :]) + _dot(dup_ref[:, a:b], wgu_ref[1, a:b, :])
        gain_v = g_ref[...]
        _, xhat, r = _rms(h_ref[...], gain_v)
        dhin_ref[...] = dh_ref[...] + _rms_bwd(dn, xhat, r, gain_v)
        dgain_ref[...] += jnp.sum(dn * xhat, axis=0, keepdims=True)

        if nw:
            pl.when(i == n_i - 1)(comm.finish)

    rows = pl.BlockSpec((tm, D_MODEL), lambda i: (i, 0))
    hid = pl.BlockSpec((tm, D_FF), lambda i: (i, 0))
    vec = pl.BlockSpec((1, D_MODEL), lambda i: (0, 0))
    return pl.pallas_call(
        body, name=name, grid=(n_i,),
        in_specs=[rows, rows, vec, hid, hid, _resident(W_GU_SHAPE, 1)] + [ANY] * nw,
        out_specs=[rows, vec] + [ANY] * nw,
        out_shape=[jax.ShapeDtypeStruct((n, D_MODEL), F32), jax.ShapeDtypeStruct((1, D_MODEL), F32)]
        + [jax.ShapeDtypeStruct(a.shape, a.dtype) for a in exchange],
        scratch_shapes=_comm_sems(nw) if nw else [],
        compiler_params=_params(("arbitrary",)),
    )(dh_out, h_in, gain, dgate, dup, wgu, *exchange)


def _token_spec(k, ksub, nq):
    def index_map(i):
        s = ksub * i + k
        return ((s // nq) * (nq - 1) + jnp.maximum(s % nq, 1) - 1, 0)
    return pl.BlockSpec((128, D_MODEL), index_map)


def _is_lead(i, k, ksub, nq):
    return ((ksub * i + k) % nq) == 0


def _assemble_rows(i, x_refs, meta_ref, nq):
    ksub = len(x_refs)
    lead = jnp.concatenate([jnp.zeros((PAD, D_MODEL), F32), meta_ref[...]], axis=0)
    return jnp.concatenate([jnp.where(_is_lead(i, k, ksub, nq), lead, x_refs[k][...]) for k in range(ksub)], axis=0)


def _ffn_fwd_tokens(x2d, meta, lp, gain, wgu, wd, name, gather=()):
    nq = lp // 128
    n = (x2d.shape[0] // (nq - 1)) * nq
    tm = _row_tile(n, 512)
    ksub = tm // 128
    n_i = n // tm
    nw = len(gather)

    def body(*refs):
        x_refs = refs[:ksub]
        (meta_ref, g_ref, wgu_ref, wd_ref), gin, (out_ref, nrm_ref, gate_ref, up_ref), gout, _, sems = \
            _split_refs(refs[ksub:], 4, nw, 4, 0)
        i = pl.program_id(0)
        if nw:
            comm = _Gather(gin, gout, sems)
            pl.when(i == 0)(comm.start)
            pl.when(i == max(n_i - 3, 0))(comm.forward)

        hv = _assemble_rows(i, x_refs, meta_ref, nq)
        y, _, _ = _rms(hv, g_ref[...])
        nb = y.astype(BF16)
        nrm_ref[...] = nb
        acc = jnp.zeros((tm, D_MODEL), F32)
        for a, b in HID_PIECES:
            gate = _dot_nt(nb, wgu_ref[0, a:b, :])
            up = _dot_nt(nb, wgu_ref[1, a:b, :])
            gate_ref[:, a:b] = gate.astype(BF16)
            up_ref[:, a:b] = up.astype(BF16)
            acc = acc + _dot((gate * _sigmoid(gate) * up).astype(BF16), wd_ref[a:b, :])
        out_ref[...] = hv + 0.5 * acc

        if nw:
            pl.when(i == n_i - 1)(comm.finish)

    rows = pl.BlockSpec((tm, D_MODEL), lambda i: (i, 0))
    hid = pl.BlockSpec((tm, D_FF), lambda i: (i, 0))
    return pl.pallas_call(
        body, name=name, grid=(n_i,),
        in_specs=[_token_spec(k, ksub, nq) for k in range(ksub)]
        + [pl.BlockSpec((N_META, D_MODEL), lambda i: (0, 0)), pl.BlockSpec((1, D_MODEL), lambda i: (0, 0)),
           _resident(W_GU_SHAPE, 1), _resident(W_D_SHAPE, 1)] + [ANY] * nw,
        out_specs=[rows, rows, hid, hid] + [ANY] * nw,
        out_shape=[jax.ShapeDtypeStruct((n, D_MODEL), F32), jax.ShapeDtypeStruct((n, D_MODEL), BF16),
                   jax.ShapeDtypeStruct((n, D_FF), BF16), jax.ShapeDtypeStruct((n, D_FF), BF16)]
        + [jax.ShapeDtypeStruct((N_DEV,) + a.shape, a.dtype) for a in gather],
        scratch_shapes=_comm_sems(nw) if nw else [],
        compiler_params=_params(("arbitrary",)),
    )(*([x2d] * ksub), meta, gain, wgu, wd, *gather)


def _ffn_fwd_loss(h, gain, wgu, wd, gfinal, target, lp, name):
    n = h.shape[0]
    nq = lp // 128
    tm = _row_tile(n, 512)
    ksub = tm // 128
    n_i = n // tm

    def body(*refs):
        t_refs = refs[:ksub]
        h_ref, g_ref, wgu_ref, wd_ref, gf_ref, dh_ref, nrm_ref, gate_ref, up_ref, loss_ref, dgf_ref = refs[ksub:]
        i = pl.program_id(0)

        @pl.when(i == 0)
        def _():
            loss_ref[...] = jnp.zeros_like(loss_ref)
            dgf_ref[...] = jnp.zeros_like(dgf_ref)

        hv = h_ref[...]
        y, _, _ = _rms(hv, g_ref[...])
        nb = y.astype(BF16)
        nrm_ref[...] = nb
        acc = jnp.zeros((tm, D_MODEL), F32)
        for a, b in HID_PIECES:
            gate = _dot_nt(nb, wgu_ref[0, a:b, :])
            up = _dot_nt(nb, wgu_ref[1, a:b, :])
            gate_ref[:, a:b] = gate.astype(BF16)
            up_ref[:, a:b] = up.astype(BF16)
            acc = acc + _dot((gate * _sigmoid(gate) * up).astype(BF16), wd_ref[a:b, :])
        hout = hv + 0.5 * acc

        gf = gf_ref[...]
        loss = jnp.zeros((1, 1), F32)
        dgf = jnp.zeros((1, D_MODEL), F32)
        for k in range(ksub):
            yk, xhat, r = _rms(hout[128 * k:128 * (k + 1)], gf)
            err = jnp.where(_is_lead(i, k, ksub, nq), 0.0, yk - t_refs[k][...])
            loss = loss + 0.5 * jnp.sum(jnp.sum(err * err, axis=1, keepdims=True), axis=0,
                                        keepdims=True) * (1.0 / D_MODEL)
            dy = err * (1.0 / D_MODEL)
            dh_ref[128 * k:128 * (k + 1), :] = _rms_bwd(dy, xhat, r, gf)
            dgf = dgf + jnp.sum(dy * xhat, axis=0, keepdims=True)
        loss_ref[...] += loss
        dgf_ref[...] += dgf

    rows = pl.BlockSpec((tm, D_MODEL), lambda i: (i, 0))
    hid = pl.BlockSpec((tm, D_FF), lambda i: (i, 0))
    vec = pl.BlockSpec((1, D_MODEL), lambda i: (0, 0))
    return pl.pallas_call(
        body, name=name, grid=(n_i,),
        in_specs=[_token_spec(k, ksub, nq) for k in range(ksub)]
        + [rows, vec, _resident(W_GU_SHAPE, 1), _resident(W_D_SHAPE, 1), vec],
        out_specs=[rows, rows, hid, hid, pl.BlockSpec((1, 1), lambda i: (0, 0)), vec],
        out_shape=[jax.ShapeDtypeStruct((n, D_MODEL), F32), jax.ShapeDtypeStruct((n, D_MODEL), BF16),
                   jax.ShapeDtypeStruct((n, D_FF), BF16), jax.ShapeDtypeStruct((n, D_FF), BF16),
                   jax.ShapeDtypeStruct((1, 1), F32), jax.ShapeDtypeStruct((1, D_MODEL), F32)],
        compiler_params=_params(("arbitrary",)),
    )(*([target] * ksub), h, gain, wgu, wd, gfinal)


def _ffn_bwd_in_tokens(dh_out, x2d, meta, lp, gain, dgate, dup, wgu, name, exchange=()):
    n = dh_out.shape[0]
    nq = lp // 128
    tm = _row_tile(n, 512)
    ksub = tm // 128
    n_i = n // tm
    nw = len(exchange)

    def body(*refs):
        x_refs = refs[:ksub]
        (meta_ref, dh_ref, g_ref, dgate_ref, dup_ref, wgu_ref), xin, (dhin_ref, dgain_ref), xout, _, sems = \
            _split_refs(refs[ksub:], 6, nw, 2, 0)
        i = pl.program_id(0)
        if nw:
            comm = _ChipExchange(xin, xout, sems)
            pl.when(i == 0)(comm.start)

        @pl.when(i == 0)
        def _():
            dgain_ref[...] = jnp.zeros_like(dgain_ref)

        dn = jnp.zeros((tm, D_MODEL), F32)
        for a, b in HID_PIECES:
            dn = dn + _dot(dgate_ref[:, a:b], wgu_ref[0, a:b, :]) + _dot(dup_ref[:, a:b], wgu_ref[1, a:b, :])
        gain_v = g_ref[...]
        _, xhat, r = _rms(_assemble_rows(i, x_refs, meta_ref, nq), gain_v)
        dhin_ref[...] = dh_ref[...] + _rms_bwd(dn, xhat, r, gain_v)
        dgain_ref[...] += jnp.sum(dn * xhat, axis=0, keepdims=True)

        if nw:
            pl.when(i == n_i - 1)(comm.finish)

    rows = pl.BlockSpec((tm, D_MODEL), lambda i: (i, 0))
    hid = pl.BlockSpec((tm, D_FF), lambda i: (i, 0))
    vec = pl.BlockSpec((1, D_MODEL), lambda i: (0, 0))
    return pl.pallas_call(
        body, name=name, grid=(n_i,),
        in_specs=[_token_spec(k, ksub, nq) for k in range(ksub)]
        + [pl.BlockSpec((N_META, D_MODEL), lambda i: (0, 0)), rows, vec, hid, hid, _resident(W_GU_SHAPE, 1)]
        + [ANY] * nw,
        out_specs=[rows, vec] + [ANY] * nw,
        out_shape=[jax.ShapeDtypeStruct((n, D_MODEL), F32), jax.ShapeDtypeStruct((1, D_MODEL), F32)]
        + [jax.ShapeDtypeStruct(a.shape, a.dtype) for a in exchange],
        scratch_shapes=_comm_sems(nw) if nw else [],
        compiler_params=_params(("arbitrary",)),
    )(*([x2d] * ksub), meta, dh_out, gain, dgate, dup, wgu, *exchange)


def _ffn_bwd_act_wd(dh_out, gate, up, wd, name, exchange=()):
    n = dh_out.shape[0]
    tm = _row_tile(n, 256)
    n_i = n // tm
    nw = len(exchange)

    def body(*refs):
        (dh_ref, gate_ref, up_ref, wd_ref), xin, (dgate_ref, dup_ref, dw_ref), xout, (acc_scr,), sems = \
            _split_refs(refs, 4, nw, 3, 1)
        i = pl.program_id(0)
        if nw:
            comm = _Exchange(xin, xout, sems)
            pl.when(i == 0)(comm.start)

        @pl.when(i == 0)
        def _():
            acc_scr[...] = jnp.zeros_like(acc_scr)

        dhb = (0.5 * dh_ref[...]).astype(BF16)
        for a, b in HID_PIECES:
            da = _dot_nt(dhb, wd_ref[a:b, :])
            g = gate_ref[:, a:b].astype(F32)
            u = up_ref[:, a:b].astype(F32)
            sig = _sigmoid(g)
            silu = g * sig
            dgate_ref[:, a:b] = (da * u * (sig * (1.0 + g * (1.0 - sig)))).astype(BF16)
            dup_ref[:, a:b] = (da * silu).astype(BF16)
            acc_scr[a:b, :] += _dot_tn((silu * u).astype(BF16), dhb)

        @pl.when(i == n_i - 1)
        def _():
            dw_ref[...] = acc_scr[...].astype(BF16)

        if nw:
            pl.when(i == n_i - 1)(comm.finish)

    rows = pl.BlockSpec((tm, D_MODEL), lambda i: (i, 0))
    hid = pl.BlockSpec((tm, D_FF), lambda i: (i, 0))
    return pl.pallas_call(
        body, name=name, grid=(n_i,),
        in_specs=[rows, hid, hid, _resident(W_D_SHAPE, 1)] + [ANY] * nw,
        out_specs=[hid, hid, _resident(W_D_SHAPE, 1)] + [ANY] * nw,
        out_shape=[jax.ShapeDtypeStruct((n, D_FF), BF16)] * 2 + [jax.ShapeDtypeStruct(W_D_SHAPE, BF16)]
        + [jax.ShapeDtypeStruct(a.shape, a.dtype) for a in exchange],
        scratch_shapes=[pltpu.VMEM(W_D_SHAPE, F32)] + (_comm_sems(nw) if nw else []),
        compiler_params=_params(("arbitrary",)),
    )(dh_out, gate, up, wd, *exchange)


def _ffn_bwd_wgu(nrm, dgate, dup, name):
    n = nrm.shape[0]
    tm = _row_tile(n, 256)
    n_i = n // tm

    def body(nrm_ref, dgate_ref, dup_ref, dw_ref, acc_scr):
        i = pl.program_id(0)

        @pl.when(i == 0)
        def _():
            acc_scr[...] = jnp.zeros_like(acc_scr)

        nb = nrm_ref[...]
        for a, b in HID_PIECES:
            acc_scr[0, a:b, :] += _dot_tn(dgate_ref[:, a:b], nb)
            acc_scr[1, a:b, :] += _dot_tn(dup_ref[:, a:b], nb)

        @pl.when(i == n_i - 1)
        def _():
            dw_ref[...] = acc_scr[...].astype(BF16)

    hid = pl.BlockSpec((tm, D_FF), lambda i: (i, 0))
    return pl.pallas_call(
        body, name=name, grid=(n_i,),
        in_specs=[pl.BlockSpec((tm, D_MODEL), lambda i: (i, 0)), hid, hid],
        out_specs=_resident(W_GU_SHAPE, 1),
        out_shape=jax.ShapeDtypeStruct(W_GU_SHAPE, BF16),
        scratch_shapes=[pltpu.VMEM(W_GU_SHAPE, F32)],
        compiler_params=_params(("arbitrary",)),
    )(nrm, dgate, dup)


def _ffn_bwd_wd(dhb, gate, up, name):
    n = dhb.shape[0]
    tm = _row_tile(n, 512)
    n_i = n // tm

    def body(dhb_ref, gate_ref, up_ref, dw_ref, acc_scr):
        i = pl.program_id(0)

        @pl.when(i == 0)
        def _():
            acc_scr[...] = jnp.zeros_like(acc_scr)

        dhv = dhb_ref[...]
        for a, b in HID_PIECES:
            g = gate_ref[:, a:b].astype(F32)
            act = (g * _sigmoid(g) * up_ref[:, a:b].astype(F32)).astype(BF16)
            acc_scr[a:b, :] += _dot_tn(act, dhv)

        @pl.when(i == n_i - 1)
        def _():
            dw_ref[...] = acc_scr[...].astype(BF16)

    hid = pl.BlockSpec((tm, D_FF), lambda i: (i, 0))
    return pl.pallas_call(
        body, name=name, grid=(n_i,),
        in_specs=[pl.BlockSpec((tm, D_MODEL), lambda i: (i, 0)), hid, hid],
        out_specs=_resident(W_D_SHAPE, 1),
        out_shape=jax.ShapeDtypeStruct(W_D_SHAPE, BF16),
        scratch_shapes=[pltpu.VMEM(W_D_SHAPE, F32)],
        compiler_params=_params(("arbitrary",)),
    )(dhb, gate, up)


N_PIECE = IN_MAIN // 512


def _inproj_fwd(h, gain, w_in):
    n = h.shape[0]
    tm = _row_tile(n, 512)

    def body(h_ref, g_ref, w_ref, *outs):
        y, _, _ = _rms(h_ref[...], g_ref[...])
        nb = y.astype(BF16)
        for p in range(N_PIECE):
            outs[p][...] = _dot_nt(nb, w_ref[512 * p:512 * (p + 1), :]).astype(BF16)
        outs[N_PIECE][...] = _dot_nt(nb, w_ref[IN_MAIN:IN_PAD, :])

    piece = pl.BlockSpec((tm, 512), lambda i: (i, 0))
    return pl.pallas_call(
        body, name="inproj_fwd", grid=(n // tm,),
        in_specs=[pl.BlockSpec((tm, D_MODEL), lambda i: (i, 0)),
                  pl.BlockSpec((1, D_MODEL), lambda i: (0, 0)),
                  pl.BlockSpec((IN_PAD, D_MODEL), lambda i: (0, 0))],
        out_specs=[piece] * N_PIECE + [pl.BlockSpec((tm, 128), lambda i: (i, 0))],
        out_shape=[jax.ShapeDtypeStruct((n, 512), BF16)] * N_PIECE + [jax.ShapeDtypeStruct((n, 128), F32)],
        compiler_params=_params(("parallel",)),
    )(h, gain, w_in)


def _inproj_bwd(dpieces, dfg, dh_out, h_in, gain, w_in):
    n = h_in.shape[0]
    tm = _row_tile(n, 512)
    n_i = n // tm

    def body(*refs):
        dp_refs = refs[:N_PIECE]
        dfg_ref, dh_ref, h_ref, g_ref, w_ref, dhin_ref, dw_ref, dgain_ref, acc_scr = refs[N_PIECE:]
        i = pl.program_id(0)

        @pl.when(i == 0)
        def _():
            acc_scr[...] = jnp.zeros_like(acc_scr)
            dgain_ref[...] = jnp.zeros_like(dgain_ref)

        gain_v = g_ref[...]
        y, xhat, r = _rms(h_ref[...], gain_v)
        nb = y.astype(BF16)
        dn = jnp.zeros((tm, D_MODEL), F32)
        for p in range(N_PIECE + 1):
            lo, hi = (512 * p, 512 * (p + 1)) if p < N_PIECE else (IN_MAIN, IN_PAD)
            dp = (dp_refs[p][...] if p < N_PIECE else dfg_ref[...]).astype(BF16)
            dn = dn + _dot(dp, w_ref[lo:hi, :])
            acc_scr[lo:hi, :] += _dot_tn(dp, nb)
        dhin_ref[...] = dh_ref[...] + _rms_bwd(dn, xhat, r, gain_v)
        dgain_ref[...] += jnp.sum(dn * xhat, axis=0, keepdims=True)

        @pl.when(i == n_i - 1)
        def _():
            dw_ref[...] = acc_scr[...].astype(BF16)

    piece = pl.BlockSpec((tm, 512), lambda i: (i, 0))
    rows = pl.BlockSpec((tm, D_MODEL), lambda i: (i, 0))
    vec = pl.BlockSpec((1, D_MODEL), lambda i: (0, 0))
    wspec = pl.BlockSpec((IN_PAD, D_MODEL), lambda i: (0, 0))
    return pl.pallas_call(
        body, name="inproj_bwd", grid=(n_i,),
        in_specs=[piece] * N_PIECE + [pl.BlockSpec((tm, 128), lambda i: (i, 0)), rows, rows, vec, wspec],
        out_specs=[rows, wspec, vec],
        out_shape=[jax.ShapeDtypeStruct((n, D_MODEL), F32),
                   jax.ShapeDtypeStruct((IN_PAD, D_MODEL), BF16),
                   jax.ShapeDtypeStruct((1, D_MODEL), F32)],
        scratch_shapes=[pltpu.VMEM((IN_PAD, D_MODEL), F32)],
        compiler_params=_params(("arbitrary",)),
    )(*dpieces, dfg, dh_out, h_in, gain, w_in)


def _outproj_fwd(zc, za, w_out, h):
    n = h.shape[0]
    tm = _row_tile(n, 512)

    def body(zc_ref, za_ref, w_ref, h_ref, out_ref):
        out_ref[...] = (h_ref[...] + _dot(zc_ref[...], w_ref[0:CONV_DIM, :])
                        + _dot(za_ref[...], w_ref[CONV_DIM:, :]))

    half = pl.BlockSpec((tm, 512), lambda i: (i, 0))
    rows = pl.BlockSpec((tm, D_MODEL), lambda i: (i, 0))
    return pl.pallas_call(
        body, name="outproj_fwd", grid=(n // tm,),
        in_specs=[half, half, pl.BlockSpec((D_MODEL, D_MODEL), lambda i: (0, 0)), rows],
        out_specs=rows,
        out_shape=jax.ShapeDtypeStruct((n, D_MODEL), F32),
        compiler_params=_params(("parallel",)),
    )(zc, za, w_out, h)


def _outproj_bwd(dh, zc, za, w_out):
    n = dh.shape[0]
    tm = _row_tile(n, 512)
    n_i = n // tm

    def body(dh_ref, zc_ref, za_ref, w_ref, dzc_ref, dza_ref, dw_ref, acc_scr):
        i = pl.program_id(0)

        @pl.when(i == 0)
        def _():
            acc_scr[...] = jnp.zeros_like(acc_scr)

        dhb = dh_ref[...].astype(BF16)
        dzc_ref[...] = _dot_nt(dhb, w_ref[0:CONV_DIM, :]).astype(BF16)
        dza_ref[...] = _dot_nt(dhb, w_ref[CONV_DIM:, :]).astype(BF16)
        acc_scr[0:CONV_DIM, :] += _dot_tn(zc_ref[...], dhb)
        acc_scr[CONV_DIM:, :] += _dot_tn(za_ref[...], dhb)

        @pl.when(i == n_i - 1)
        def _():
            dw_ref[...] = acc_scr[...].astype(BF16)

    half = pl.BlockSpec((tm, 512), lambda i: (i, 0))
    wspec = pl.BlockSpec((D_MODEL, D_MODEL), lambda i: (0, 0))
    return pl.pallas_call(
        body, name="outproj_bwd", grid=(n_i,),
        in_specs=[pl.BlockSpec((tm, D_MODEL), lambda i: (i, 0)), half, half, wspec],
        out_specs=[half, half, wspec],
        out_shape=[jax.ShapeDtypeStruct((n, 512), BF16), jax.ShapeDtypeStruct((n, 512), BF16),
                   jax.ShapeDtypeStruct((D_MODEL, D_MODEL), BF16)],
        scratch_shapes=[pltpu.VMEM((D_MODEL, D_MODEL), F32)],
        compiler_params=_params(("arbitrary",)),
    )(dh, zc, za, w_out)


def _group_matrix():
    r = lax.broadcasted_iota(jnp.int32, (128, 128), 0) // HEAD_DIM
    c = lax.broadcasted_iota(jnp.int32, (128, 128), 1) // HEAD_DIM
    return jnp.where(r == c, 1.0 / HEAD_DIM, 0.0).astype(BF16)


def _group_mean(x, gmat):
    hi = x.astype(BF16)
    lo = (x - hi.astype(F32)).astype(BF16)
    return _dot(hi, gmat) + _dot(lo, gmat)


def _shift_rows(x, s):
    rows = x.shape[0]
    t = lax.broadcasted_iota(jnp.int32, x.shape, 0)
    rolled = pltpu.roll(x, s % rows, 0)
    keep = (t >= s) if s > 0 else (t < rows + s)
    return jnp.where(keep, rolled, 0.0)


def _conv_parts(bg_ref, cg_ref, hc_ref, w_ref):
    bg = bg_ref[...].astype(F32)
    cg = cg_ref[...].astype(F32)
    hc = hc_ref[...].astype(F32)
    u = cg * hc
    u1 = _shift_rows(u, 1)
    u2 = _shift_rows(u, 2)
    conv = w_ref[2:3, :] * u + w_ref[1:2, :] * u1 + w_ref[0:1, :] * u2
    return bg, cg, hc, u, u1, u2, conv


def _conv_fwd(bg, cg, hc, conv_w, gain, gmat, lp):
    n = bg.shape[0]
    nb = n // lp

    def body(bg_ref, cg_ref, hc_ref, w_ref, g_ref, gm_ref, z_ref):
        bgv, _, _, _, _, _, conv = _conv_parts(bg_ref, cg_ref, hc_ref, w_ref)
        yc = bgv * conv
        r = lax.rsqrt(_group_mean(yc * yc, gm_ref[...]) + EPS)
        z_ref[...] = (yc * r * g_ref[...]).astype(BF16)

    blk = pl.BlockSpec((lp, 128), lambda c, b: (b, c))
    return pl.pallas_call(
        body, name="conv_fwd", grid=(CONV_DIM // 128, nb),
        in_specs=[blk, blk, blk, pl.BlockSpec((3, 128), lambda c, b: (0, c)),
                  pl.BlockSpec((1, 128), lambda c, b: (0, c)), pl.BlockSpec((128, 128), lambda c, b: (0, 0))],
        out_specs=blk,
        out_shape=jax.ShapeDtypeStruct((n, CONV_DIM), BF16),
        compiler_params=_params(("parallel", "parallel")),
    )(bg, cg, hc, conv_w, gain, gmat)


def _conv_bwd(dz, bg, cg, hc, conv_w, gain, gmat, lp):
    n = bg.shape[0]
    nb = n // lp

    def body(dz_ref, bg_ref, cg_ref, hc_ref, w_ref, g_ref, gm_ref,
             dbg_ref, dcg_ref, dhc_ref, dw_ref, dgain_ref):
        b = pl.program_id(1)

        @pl.when(b == 0)
        def _():
            dw_ref[...] = jnp.zeros_like(dw_ref)
            dgain_ref[...] = jnp.zeros_like(dgain_ref)

        bgv, cgv, hcv, u, u1, u2, conv = _conv_parts(bg_ref, cg_ref, hc_ref, w_ref)
        gm = gm_ref[...]
        yc = bgv * conv
        r = lax.rsqrt(_group_mean(yc * yc, gm) + EPS)
        yhat = yc * r
        dzv = dz_ref[...].astype(F32)
        dyhat = dzv * g_ref[...]
        dgain_ref[...] += jnp.sum(dzv * yhat, axis=0, keepdims=True)
        dyc = r * (dyhat - yhat * _group_mean(dyhat * yhat, gm))
        dbg_ref[...] = (dyc * conv).astype(BF16)
        dconv = dyc * bgv
        du = (w_ref[2:3, :] * dconv + w_ref[1:2, :] * _shift_rows(dconv, -1)
              + w_ref[0:1, :] * _shift_rows(dconv, -2))
        dcg_ref[...] = (du * hcv).astype(BF16)
        dhc_ref[...] = (du * cgv).astype(BF16)
        dw_ref[0:1, :] += jnp.sum(dconv * u2, axis=0, keepdims=True)
        dw_ref[1:2, :] += jnp.sum(dconv * u1, axis=0, keepdims=True)
        dw_ref[2:3, :] += jnp.sum(dconv * u, axis=0, keepdims=True)

    blk = pl.BlockSpec((lp, 128), lambda c, b: (b, c))
    wspec = pl.BlockSpec((3, 128), lambda c, b: (0, c))
    gspec = pl.BlockSpec((1, 128), lambda c, b: (0, c))
    return pl.pallas_call(
        body, name="conv_bwd", grid=(CONV_DIM // 128, nb),
        in_specs=[blk, blk, blk, blk, wspec, gspec, pl.BlockSpec((128, 128), lambda c, b: (0, 0))],
        out_specs=[blk, blk, blk, wspec, gspec],
        out_shape=[jax.ShapeDtypeStruct((n, CONV_DIM), BF16)] * 3
        + [jax.ShapeDtypeStruct((3, CONV_DIM), F32), jax.ShapeDtypeStruct((1, CONV_DIM), F32)],
        compiler_params=_params(("parallel", "arbitrary")),
    )(dz, bg, cg, hc, conv_w, gain, gmat)


KEY_MASKED = 1e30
ONE_LANE = 24


def _scan_steps(rows):
    s, out = 1, []
    while s < rows:
        out.append(s)
        s *= 2
    return out


def _fgate_fwd(fg, b_f, lp):
    n = fg.shape[0]
    nb = n // lp

    def body(fg_ref, b_ref, ka_ref, qa_ref):
        x = fg_ref[...] + b_ref[...]
        logf = jnp.minimum(x, 0.0) - jnp.log(1.0 + jnp.exp(-jnp.abs(x)))
        t = lax.broadcasted_iota(jnp.int32, (lp, 128), 0)
        lane = lax.broadcasted_iota(jnp.int32, (lp, 128), 1)
        f = jnp.where((t >= PAD) & (lane < N_HEADS), logf, 0.0)
        for s in _scan_steps(lp):
            f = f + _shift_rows(f, s)
        hi = f.astype(BF16).astype(F32)
        rest = f - hi
        mid = rest.astype(BF16).astype(F32)
        lo = (rest - mid).astype(BF16).astype(F32)
        ones = jnp.where((lane >= ONE_LANE) & (lane < ONE_LANE + 3), 1.0, 0.0)
        hi_key = jnp.where((t < PAD) & (lane < N_HEADS), KEY_MASKED, hi)
        ka_ref[...] = (hi_key + pltpu.roll(mid, 8, 1) + pltpu.roll(lo, 16, 1) + ones).astype(BF16)
        for h in range(N_HEADS):
            minus = jnp.where((lane == h) | (lane == 8 + h) | (lane == 16 + h), -1.0, 0.0)
            terms = (jnp.where(lane == ONE_LANE, pltpu.roll(hi, ONE_LANE - h, 1), 0.0)
                     + jnp.where(lane == ONE_LANE + 1, pltpu.roll(mid, ONE_LANE + 1 - h, 1), 0.0)
                     + jnp.where(lane == ONE_LANE + 2, pltpu.roll(lo, ONE_LANE + 2 - h, 1), 0.0))
            qa_ref[:, 128 * h:128 * (h + 1)] = (minus + terms).astype(BF16)

    return pl.pallas_call(
        body, name="fgate_fwd", grid=(nb,),
        in_specs=[pl.BlockSpec((lp, 128), lambda b: (b, 0)), pl.BlockSpec((1, 128), lambda b: (0, 0))],
        out_specs=[pl.BlockSpec((lp, 128), lambda b: (b, 0)), pl.BlockSpec((lp, N_HEADS * 128), lambda b: (b, 0))],
        out_shape=[jax.ShapeDtypeStruct((n, 128), BF16), jax.ShapeDtypeStruct((n, N_HEADS * 128), BF16)],
        compiler_params=_params(("parallel",)),
    )(fg, b_f)


def _fgate_bwd(dka, dfr, fg, b_f, lp):
    n = fg.shape[0]
    nb = n // lp

    def body(dka_ref, dfr_ref, fg_ref, b_ref, dfg_ref, db_ref):
        b = pl.program_id(0)

        @pl.when(b == 0)
        def _():
            db_ref[...] = jnp.zeros_like(db_ref)

        wide = jnp.concatenate([dfr_ref[0], jnp.zeros((128 - N_HEADS, lp), F32)], axis=0)
        t = lax.broadcasted_iota(jnp.int32, (lp, 128), 0)
        lane = lax.broadcasted_iota(jnp.int32, (lp, 128), 1)
        d = jnp.where(lane < N_HEADS, dka_ref[...], 0.0) + wide.T
        for s in _scan_steps(lp):
            d = d + _shift_rows(d, -s)
        x = fg_ref[...] + b_ref[...]
        dx = jnp.where((t >= PAD) & (lane < N_HEADS), d * _sigmoid(-x), 0.0)
        dfg_ref[...] = dx
        db_ref[...] += jnp.sum(dx, axis=0, keepdims=True)

    return pl.pallas_call(
        body, name="fgate_bwd", grid=(nb,),
        in_specs=[pl.BlockSpec((lp, 128), lambda b: (b, 0)), pl.BlockSpec((1, N_HEADS, lp), lambda b: (b, 0, 0)),
                  pl.BlockSpec((lp, 128), lambda b: (b, 0)), pl.BlockSpec((1, 128), lambda b: (0, 0))],
        out_specs=[pl.BlockSpec((lp, 128), lambda b: (b, 0)), pl.BlockSpec((1, 128), lambda b: (0, 0))],
        out_shape=[jax.ShapeDtypeStruct((n, 128), F32), jax.ShapeDtypeStruct((1, 128), F32)],
        compiler_params=_params(("arbitrary",)),
    )(dka, dfr, fg, b_f)


def _head_masks():
    lane = lax.broadcasted_iota(jnp.int32, (1, 128), 1)
    return lane < HEAD_DIM


def _stack_heads(x2, first):
    zero = jnp.zeros_like(x2)
    return jnp.concatenate([jnp.where(first, x2, zero), jnp.where(first, zero, x2)], axis=0)


def _stack_heads_lanes(xt):
    r = lax.broadcasted_iota(jnp.int32, xt.shape, 0)
    zero = jnp.zeros_like(xt)
    return jnp.concatenate([jnp.where(r < HEAD_DIM, xt, zero), jnp.where(r < HEAD_DIM, zero, xt)], axis=1)


def _pair_cols(col0, col1, first):
    return jnp.where(first, col0, col1)


def _pair_rows(row0, row1):
    r = lax.broadcasted_iota(jnp.int32, (128, TQ), 0)
    return jnp.where(r < HEAD_DIM, row0, row1)


def _query_side(q_ref, qa_ref, p, first):
    q2 = q_ref[:, 128 * p:128 * (p + 1)] * 0.125
    zero = jnp.zeros_like(q2)
    top = jnp.concatenate([jnp.where(first, q2, zero), qa_ref[:, 128 * (2 * p):128 * (2 * p + 1)]], axis=1)
    bot = jnp.concatenate([jnp.where(first, zero, q2), qa_ref[:, 128 * (2 * p + 1):128 * (2 * p + 2)]], axis=1)
    return jnp.concatenate([top, bot], axis=0)


def _key_chunks(lp):
    return (lp + TK - 1) // TK


def _chunk_mask(i, c):
    r = lax.broadcasted_iota(jnp.int32, (TK, 2 * TQ), 0)
    col = lax.broadcasted_iota(jnp.int32, (TK, 2 * TQ), 1)
    return (c * TK + r) <= (i * TQ + (col & (TQ - 1)))


def _transpose_bf16(x):
    return x.astype(F32).T.astype(BF16)


def _attn_fwd(q, qa, k, v, ka, gain, lp):
    n = q.shape[0]
    nb = n // lp
    nq = lp // TQ
    lpp = _key_chunks(lp) * TK

    def body(q_ref, qa_ref, k_ref, v_ref, ka_ref, g_ref, z_ref, o_ref, lse_ref, kx_scr, vt_scr):
        i = pl.program_id(1)
        first = _head_masks()

        @pl.when(i == 0)
        def _():
            if lpp > lp:
                kx_scr[lp:lpp, :] = jnp.zeros((lpp - lp, 2 * ATTN_DIM), BF16)
                vt_scr[:, lp:lpp] = jnp.zeros((ATTN_DIM, lpp - lp), BF16)
            for p in range(N_PAIRS):
                kx_scr[0:lp, 256 * p:256 * p + 128] = k_ref[:, 128 * p:128 * (p + 1)]
                kx_scr[0:lp, 256 * p + 128:256 * (p + 1)] = ka_ref[...]
            vt_scr[:, 0:lp] = _transpose_bf16(v_ref[...])

        rhs_t = [_transpose_bf16(_query_side(q_ref, qa_ref, p, first)) for p in range(N_PAIRS)]

        def step(c, carry, masked):
            koff = pl.multiple_of(c * TK, TK)
            valid = _chunk_mask(i, c) if masked else None
            new = []
            for p in range(N_PAIRS):
                m, l, acc = carry[p]
                st = _dot(kx_scr[pl.ds(koff, TK), 256 * p:256 * (p + 1)], rhs_t[p])
                if masked:
                    st = jnp.where(valid, st, NEG)
                m_new = jnp.maximum(m, jnp.max(st, axis=0, keepdims=True))
                pt = jnp.exp(st - m_new)
                alpha = jnp.exp(m - m_new)
                l = alpha * l + jnp.sum(pt, axis=0, keepdims=True)
                pb = pt.astype(BF16)
                vt = _stack_heads_lanes(vt_scr[128 * p:128 * (p + 1), pl.ds(koff, TK)])
                pv = _dot(vt, jnp.concatenate([pb[:, 0:TQ], pb[:, TQ:]], axis=0))
                acc = acc * _pair_rows(alpha[:, 0:TQ], alpha[:, TQ:]) + pv
                new.append((m_new, l, acc))
            return tuple(new)

        init = tuple((jnp.full((1, 2 * TQ), NEG, F32), jnp.zeros((1, 2 * TQ), F32), jnp.zeros((128, TQ), F32))
                     for _ in range(N_PAIRS))
        last = i // (TK // TQ)
        final = step(last, lax.fori_loop(0, last, lambda c, carry: step(c, carry, False), init), True)

        row = lax.broadcasted_iota(jnp.int32, (TQ, 128), 0)
        real = (i * TQ + row) >= PAD
        for p in range(N_PAIRS):
            m, l, acc = final[p]
            inv = 1.0 / l
            ot = acc * _pair_rows(inv[:, 0:TQ], inv[:, TQ:])
            sq = ot * ot
            r0 = lax.rsqrt(jnp.sum(sq[0:HEAD_DIM], axis=0, keepdims=True) * (1.0 / HEAD_DIM) + EPS)
            r1 = lax.rsqrt(jnp.sum(sq[HEAD_DIM:], axis=0, keepdims=True) * (1.0 / HEAD_DIM) + EPS)
            cols = slice(128 * p, 128 * (p + 1))
            o_ref[:, cols] = jnp.where(real, ot.T, 0.0).astype(BF16)
            z_ref[:, cols] = (jnp.where(real, (ot * _pair_rows(r0, r1)).T, 0.0) * g_ref[:, cols]).astype(BF16)
            lse = m + jnp.log(l)
            lse_ref[0, 2 * p:2 * p + 1, :] = lse[:, 0:TQ]
            lse_ref[0, 2 * p + 1:2 * p + 2, :] = lse[:, TQ:]

    qblk = pl.BlockSpec((TQ, ATTN_DIM), lambda b, i: (b * nq + i, 0))
    qablk = pl.BlockSpec((TQ, N_HEADS * 128), lambda b, i: (b * nq + i, 0))
    seq = pl.BlockSpec((lp, ATTN_DIM), lambda b, i: (b, 0))
    rowblk = pl.BlockSpec((1, N_HEADS, TQ), lambda b, i: (b, 0, i))
    return pl.pallas_call(
        body, name="attn_fwd", grid=(nb, nq),
        in_specs=[qblk, qablk, seq, seq, pl.BlockSpec((lp, 128), lambda b, i: (b, 0)),
                  pl.BlockSpec((1, ATTN_DIM), lambda b, i: (0, 0))],
        out_specs=[qblk, qblk, rowblk],
        out_shape=[jax.ShapeDtypeStruct((n, ATTN_DIM), BF16), jax.ShapeDtypeStruct((n, ATTN_DIM), BF16),
                   jax.ShapeDtypeStruct((nb, N_HEADS, lp), F32)],
        scratch_shapes=[pltpu.VMEM((lpp, 2 * ATTN_DIM), BF16), pltpu.VMEM((ATTN_DIM, lpp), BF16)],
        compiler_params=_params(("parallel", "arbitrary")),
    )(q, qa, k, v, ka, gain)


def _attn_bwd(dz, q, qa, k, v, ka, o, lse, gain, lp, exchange=()):
    n = q.shape[0]
    nb = n // lp
    nq = lp // TQ
    lpp = _key_chunks(lp) * TK
    nw = len(exchange)

    def body(*refs):
        ((dz_ref, q_ref, qa_ref, k_ref, v_ref, ka_ref, o_ref, lse_ref, g_ref), xin,
         (dq_ref, dk_ref, dv_ref, dka_ref, dfr_ref, dgain_ref), xout,
         (kx_scr, vx_scr, kt_scr, dkx_scr, dvx_scr), sems) = _split_refs(refs, 9, nw, 6, 5)
        b = pl.program_id(0)
        i = pl.program_id(1)
        first = _head_masks()
        if nw:
            comm = _Exchange(xin, xout, sems)
            pl.when((b == 0) & (i == 0))(comm.start)

        @pl.when((b == 0) & (i == 0))
        def _():
            dgain_ref[...] = jnp.zeros_like(dgain_ref)

        @pl.when(i == 0)
        def _():
            if lpp > lp:
                kx_scr[lp:lpp, :] = jnp.zeros((lpp - lp, 2 * ATTN_DIM), BF16)
                vx_scr[lp:lpp, :] = jnp.zeros((lpp - lp, ATTN_DIM), BF16)
                kt_scr[:, lp:lpp] = jnp.zeros((ATTN_DIM, lpp - lp), BF16)
            for p in range(N_PAIRS):
                kx_scr[0:lp, 256 * p:256 * p + 128] = k_ref[:, 128 * p:128 * (p + 1)]
                kx_scr[0:lp, 256 * p + 128:256 * (p + 1)] = ka_ref[...]
            vx_scr[0:lp, :] = v_ref[...]
            kt_scr[:, 0:lp] = _transpose_bf16(k_ref[...])
            dkx_scr[...] = jnp.zeros_like(dkx_scr)
            dvx_scr[...] = jnp.zeros_like(dvx_scr)

        rhs, rhs_t, lses, dos, dos_t, deltas = [], [], [], [], [], []
        for p in range(N_PAIRS):
            cols = slice(128 * p, 128 * (p + 1))
            side = _query_side(q_ref, qa_ref, p, first)
            rhs.append(side)
            rhs_t.append(_transpose_bf16(side))
            lses.append(jnp.concatenate([lse_ref[0, 2 * p:2 * p + 1, :], lse_ref[0, 2 * p + 1:2 * p + 2, :]], axis=1))
            ov = o_ref[:, cols].astype(F32)
            dzv = dz_ref[:, cols].astype(F32)
            gv = g_ref[:, cols]
            sq = ov * ov
            ms0 = jnp.sum(jnp.where(first, sq, 0.0), axis=1, keepdims=True) * (1.0 / HEAD_DIM)
            ms1 = jnp.sum(jnp.where(first, 0.0, sq), axis=1, keepdims=True) * (1.0 / HEAD_DIM)
            r = _pair_cols(lax.rsqrt(ms0 + EPS), lax.rsqrt(ms1 + EPS), first)
            ohat = ov * r
            dyhat = dzv * gv
            dgain_ref[:, cols] += jnp.sum(dzv * ohat, axis=0, keepdims=True)
            pr = dyhat * ohat
            mean0 = jnp.sum(jnp.where(first, pr, 0.0), axis=1, keepdims=True) * (1.0 / HEAD_DIM)
            mean1 = jnp.sum(jnp.where(first, 0.0, pr), axis=1, keepdims=True) * (1.0 / HEAD_DIM)
            do = r * (dyhat - ohat * _pair_cols(mean0, mean1, first))
            ddt = (do * ov).T
            deltas.append(jnp.concatenate([jnp.sum(ddt[0:HEAD_DIM], axis=0, keepdims=True),
                                           jnp.sum(ddt[HEAD_DIM:], axis=0, keepdims=True)], axis=1))
            do_st = _stack_heads(do.astype(BF16), first)
            dos.append(do_st)
            dos_t.append(_transpose_bf16(do_st))

        def step(c, carry, masked):
            koff = pl.multiple_of(c * TK, TK)
            valid = _chunk_mask(i, c) if masked else None
            new = []
            for p in range(N_PAIRS):
                dqt, dfq = carry[p]
                ext = slice(256 * p, 256 * (p + 1))
                cols = slice(128 * p, 128 * (p + 1))
                st = _dot(kx_scr[pl.ds(koff, TK), ext], rhs_t[p])
                if masked:
                    st = jnp.where(valid, st, NEG)
                pt = jnp.exp(st - lses[p])
                dpt = _dot(vx_scr[pl.ds(koff, TK), cols], dos_t[p])
                dst = pt * (dpt - deltas[p])
                dsb = dst.astype(BF16)
                dfq = dfq + jnp.sum(dsb.astype(F32), axis=0, keepdims=True)
                dkx_scr[pl.ds(koff, TK), ext] += _dot(dsb, rhs[p])
                dvx_scr[pl.ds(koff, TK), cols] += _dot(pt.astype(BF16), dos[p])
                kt = _stack_heads_lanes(kt_scr[cols, pl.ds(koff, TK)])
                dqt = dqt + _dot(kt, jnp.concatenate([dsb[:, 0:TQ], dsb[:, TQ:]], axis=0))
                new.append((dqt, dfq))
            return tuple(new)

        init = tuple((jnp.zeros((128, TQ), F32), jnp.zeros((1, 2 * TQ), F32)) for _ in range(N_PAIRS))
        last = i // (TK // TQ)
        final = step(last, lax.fori_loop(0, last, lambda c, carry: step(c, carry, False), init), True)

        for p in range(N_PAIRS):
            dqt, dfq = final[p]
            dq_ref[:, 128 * p:128 * (p + 1)] = (dqt.T * 0.125).astype(BF16)
            dfr_ref[0, 2 * p:2 * p + 1, :] = dfq[:, 0:TQ]
            dfr_ref[0, 2 * p + 1:2 * p + 2, :] = dfq[:, TQ:]

        @pl.when(i == nq - 1)
        def _():
            dka = jnp.zeros((lp, 128), F32)
            for p in range(N_PAIRS):
                dk_ref[:, 128 * p:128 * (p + 1)] = dkx_scr[0:lp, 256 * p:256 * p + 128].astype(BF16)
                dka = dka + dkx_scr[0:lp, 256 * p + 128:256 * (p + 1)]
            dka_ref[...] = dka
            dv_ref[...] = dvx_scr[0:lp, :].astype(BF16)

        if nw:
            pl.when((b == nb - 1) & (i == nq - 1))(comm.finish)

    qblk = pl.BlockSpec((TQ, ATTN_DIM), lambda b, i: (b * nq + i, 0))
    qablk = pl.BlockSpec((TQ, N_HEADS * 128), lambda b, i: (b * nq + i, 0))
    seq = pl.BlockSpec((lp, ATTN_DIM), lambda b, i: (b, 0))
    kaseq = pl.BlockSpec((lp, 128), lambda b, i: (b, 0))
    rowblk = pl.BlockSpec((1, N_HEADS, TQ), lambda b, i: (b, 0, i))
    gspec = pl.BlockSpec((1, ATTN_DIM), lambda b, i: (0, 0))
    return pl.pallas_call(
        body, name="attn_bwd", grid=(nb, nq),
        in_specs=[qblk, qblk, qablk, seq, seq, kaseq, qblk, rowblk, gspec] + [ANY] * nw,
        out_specs=[qblk, seq, seq, kaseq, rowblk, gspec] + [ANY] * nw,
        out_shape=[jax.ShapeDtypeStruct((n, ATTN_DIM), BF16), jax.ShapeDtypeStruct((n, ATTN_DIM), BF16),
                   jax.ShapeDtypeStruct((n, ATTN_DIM), BF16), jax.ShapeDtypeStruct((n, 128), F32),
                   jax.ShapeDtypeStruct((nb, N_HEADS, lp), F32), jax.ShapeDtypeStruct((1, ATTN_DIM), F32)]
        + [jax.ShapeDtypeStruct(a.shape, a.dtype) for a in exchange],
        scratch_shapes=[pltpu.VMEM((lpp, 2 * ATTN_DIM), BF16), pltpu.VMEM((lpp, ATTN_DIM), BF16),
                        pltpu.VMEM((ATTN_DIM, lpp), BF16), pltpu.VMEM((lpp, 2 * ATTN_DIM), F32),
                        pltpu.VMEM((lpp, ATTN_DIM), F32)] + (_comm_sems(nw) if nw else []),
        compiler_params=_params(("arbitrary", "arbitrary")),
    )(dz, q, qa, k, v, ka, o, lse, gain, *exchange)


def _loss_head(h, gain, target, lp):
    n = h.shape[0]
    nb = n // lp
    nq = lp // 128

    def body(h_ref, g_ref, t_ref, loss_ref, dh_ref, dgain_ref):
        b = pl.program_id(0)
        i = pl.program_id(1)

        @pl.when((b == 0) & (i == 0))
        def _():
            loss_ref[...] = jnp.zeros_like(loss_ref)
            dgain_ref[...] = jnp.zeros_like(dgain_ref)

        @pl.when(i == 0)
        def _():
            dh_ref[...] = jnp.zeros_like(dh_ref)

        @pl.when(i > 0)
        def _():
            gain_v = g_ref[...]
            y, xhat, r = _rms(h_ref[...], gain_v)
            err = y - t_ref[...]
            loss_ref[...] += 0.5 * jnp.sum(jnp.sum(err * err, axis=1, keepdims=True), axis=0,
                                           keepdims=True) * (1.0 / D_MODEL)
            dy = err * (1.0 / D_MODEL)
            dh_ref[...] = _rms_bwd(dy, xhat, r, gain_v)
            dgain_ref[...] += jnp.sum(dy * xhat, axis=0, keepdims=True)

    rows = pl.BlockSpec((128, D_MODEL), lambda b, i: (b * nq + i, 0))
    trows = pl.BlockSpec((128, D_MODEL), lambda b, i: (b * (nq - 1) + jnp.maximum(i, 1) - 1, 0))
    return pl.pallas_call(
        body, name="loss_head", grid=(nb, nq),
        in_specs=[rows, pl.BlockSpec((1, D_MODEL), lambda b, i: (0, 0)), trows],
        out_specs=[pl.BlockSpec((1, 1), lambda b, i: (0, 0)), rows, pl.BlockSpec((1, D_MODEL), lambda b, i: (0, 0))],
        out_shape=[jax.ShapeDtypeStruct((1, 1), F32), jax.ShapeDtypeStruct((n, D_MODEL), F32),
                   jax.ShapeDtypeStruct((1, D_MODEL), F32)],
        compiler_params=_params(("arbitrary", "arbitrary")),
    )(h, gain, target)


def _adamw(parts, w, m, v, name):
    s_parts, r, c = parts.shape
    tr = r
    for t in (256, 128, 64, 32, 16):
        if r % t == 0 and r > t:
            tr = t
            break

    def body(p_ref, w_ref, m_ref, v_ref, g_ref, d_ref, nm_ref, nv_ref):
        g = p_ref[0].astype(F32)
        for s in range(1, s_parts):
            g = g + p_ref[s].astype(F32)
        nm = ADAM_B1 * m_ref[...] + (1.0 - ADAM_B1) * g
        nv = ADAM_B2 * v_ref[...] + (1.0 - ADAM_B2) * (g * g)
        m_hat = nm / (1.0 - ADAM_B1 ** ADAM_STEP)
        v_hat = nv / (1.0 - ADAM_B2 ** ADAM_STEP)
        g_ref[...] = g
        d_ref[...] = -ADAM_LR * (m_hat / (jnp.sqrt(v_hat) + ADAM_EPS) + ADAM_WD * w_ref[...])
        nm_ref[...] = nm
        nv_ref[...] = nv

    blk = pl.BlockSpec((tr, c), lambda i: (i, 0))
    return pl.pallas_call(
        body, name=name, grid=(r // tr,),
        in_specs=[pl.BlockSpec((s_parts, tr, c), lambda i: (0, i, 0)), blk, blk, blk],
        out_specs=[blk] * 4,
        out_shape=[jax.ShapeDtypeStruct((r, c), F32)] * 4,
        compiler_params=_params(("parallel",)),
    )(parts, w, m, v)


def _sum_parts(parts, name):
    s_parts, r, c = parts.shape

    def body(p_ref, out_ref):
        acc = p_ref[0]
        for s in range(1, s_parts):
            acc = acc + p_ref[s]
        out_ref[...] = acc

    return pl.pallas_call(
        body, name=name, out_shape=jax.ShapeDtypeStruct((r, c), F32),
        in_specs=[pl.BlockSpec(memory_space=pltpu.VMEM)], out_specs=pl.BlockSpec(memory_space=pltpu.VMEM),
    )(parts)


SMALL_ROWS = 184


def _pack_small(d_gains, d_gc, d_ga, d_bf, d_conv, d_meta):
    rows = [g.reshape(8, 128) for g in d_gains]
    rows += [d_gc.reshape(4, 128), d_ga.reshape(4, 128), d_bf.reshape(1, 128)]
    rows += [d_conv.reshape(12, 128), d_meta.reshape(128, 128)]
    packed = jnp.concatenate(rows, axis=0)
    return jnp.pad(packed, ((0, SMALL_ROWS - packed.shape[0]), (0, 0)))


def kernel(x, meta_tokens, ffn1_norm, ffn1_w_gu, ffn1_w_down, mix_norm, w_in, conv_w, b_f, out_norm_conv, out_norm_attn, w_out, ffn2_norm, ffn2_w_gu, ffn2_w_down, final_norm, loss_target, m_meta_tokens, m_ffn1_norm, m_ffn1_w_gu, m_ffn1_w_down, m_mix_norm, m_w_in, m_conv_w, m_b_f, m_out_norm_conv, m_out_norm_attn, m_w_out, m_ffn2_norm, m_ffn2_w_gu, m_ffn2_w_down, m_final_norm, v_meta_tokens, v_ffn1_norm, v_ffn1_w_gu, v_ffn1_w_down, v_mix_norm, v_w_in, v_conv_w, v_b_f, v_out_norm_conv, v_out_norm_attn, v_w_out, v_ffn2_norm, v_ffn2_w_gu, v_ffn2_w_down, v_final_norm):
    nb, seq, _ = x.shape
    lp = PAD + N_META + seq
    n = nb * lp
    me = 4 * lax.axis_index("x") + 2 * lax.axis_index("y") + lax.axis_index("c")

    wgu1_8, wd1_8 = _all_gather([ffn1_w_gu[0].T.astype(BF16), ffn1_w_down[0].astype(BF16)], "gather_ffn1")
    small_in = jnp.concatenate(
        [meta_tokens, jnp.pad(conv_w[0], ((0, 0), (0, 128 - conv_w.shape[2]))), jnp.zeros((5, 128), F32)], axis=0)
    (small_8,) = _all_gather([small_in], "gather_small")
    meta_full = small_8[:, 0:N_META, :].transpose(1, 0, 2).reshape(N_META, D_MODEL)
    conv_full = small_8[:, N_META:N_META + 3, 0:CONV_DIM // N_DEV].transpose(1, 0, 2).reshape(3, CONV_DIM)
    wgu1 = wgu1_8.reshape(W_GU_SHAPE)
    wd1 = wd1_8.reshape(W_D_SHAPE)
    b_f_row = jnp.pad(b_f, ((0, 0), (0, 128 - N_HEADS)))
    gmat = _group_matrix()

    x2d = x.reshape(nb * seq, D_MODEL)
    later = [w_in[0].T.astype(BF16), w_out[0].astype(BF16), ffn2_w_gu[0].T.astype(BF16), ffn2_w_down[0].astype(BF16)]
    h1, n1, gate1, up1, win_8, wout_8, wgu2_8, wd2_8 = _ffn_fwd_tokens(
        x2d, meta_full, lp, ffn1_norm, wgu1, wd1, "ffn1_fwd", gather=later)
    wgu2 = wgu2_8.reshape(W_GU_SHAPE)
    wd2 = wd2_8.reshape(W_D_SHAPE)
    w_in_full = jnp.pad(win_8.reshape(IN_DIM, D_MODEL), ((0, IN_PAD - IN_DIM), (0, 0)))
    w_out_full = wout_8.reshape(D_MODEL, D_MODEL)

    bg, cg, hc, q, k, v, fg = _inproj_fwd(h1, mix_norm, w_in_full)
    zc = _conv_fwd(bg, cg, hc, conv_full, out_norm_conv, gmat, lp)
    ka, qa = _fgate_fwd(fg, b_f_row, lp)
    za, o, lse = _attn_fwd(q, qa, k, v, ka, out_norm_attn, lp)
    h2 = _outproj_fwd(zc, za, w_out_full, h1)
    dh3, n3, gate2, up2, loss_part, d_final = _ffn_fwd_loss(
        h2, ffn2_norm, wgu2, wd2, final_norm.reshape(1, D_MODEL), loss_target.reshape(nb * seq, D_MODEL), lp,
        "ffn2_fwd_loss")

    dgate2, dup2, dwd2 = _ffn_bwd_act_wd(dh3, gate2, up2, wd2, "ffn2_bwd_act")
    dh2, d_ffn2 = _ffn_bwd_in(dh3, h2, ffn2_norm, dgate2, dup2, wgu2, "ffn2_bwd_in")
    dwgu2 = _ffn_bwd_wgu(n3, dgate2, dup2, "ffn2_bwd_wgu")
    dzc, dza, dwout = _outproj_bwd(dh2, zc, za, w_out_full)
    send_a = [dwgu2.reshape(N_DEV, F_CHUNK, D_MODEL), dwd2.reshape(N_DEV, F_CHUNK // 2, D_MODEL),
              dwout.reshape(N_DEV, D_MODEL // N_DEV, D_MODEL)]
    dq, dk, dv, dka, dfr, d_ga, p_wgu2, p_wd2, p_wout = _attn_bwd(
        dza, q, qa, k, v, ka, o, lse, out_norm_attn, lp, exchange=send_a)
    dfg, d_bf = _fgate_bwd(dka, dfr, fg, b_f_row, lp)
    dbg, dcg, dhc, d_conv, d_gc = _conv_bwd(dzc, bg, cg, hc, conv_full, out_norm_conv, gmat, lp)
    dh1, dwin, d_mix = _inproj_bwd([dbg, dcg, dhc, dq, dk, dv], dfg, dh2, h1, mix_norm, w_in_full)
    dwin_8 = dwin[0:IN_DIM].reshape(N_DEV, IN_DIM // N_DEV, D_MODEL)
    dgate1, dup1, dwd1, p_win = _ffn_bwd_act_wd(dh1, gate1, up1, wd1, "ffn1_bwd_act", exchange=[dwin_8])
    dwgu1 = _ffn_bwd_wgu(n1, dgate1, dup1, "ffn1_bwd_wgu")
    own = [dwgu1.reshape(N_DEV, F_CHUNK, D_MODEL), dwd1.reshape(N_DEV, F_CHUNK // 2, D_MODEL)]
    got = _pair_exchange(own, "pair_exchange_ffn1")
    chip_sums = [_pair_sum(own[0], got[0], "pair_sum_wgu1"), _pair_sum(own[1], got[1], "pair_sum_wd1")]
    dh0, d_ffn1, p_wgu1, p_wd1 = _ffn_bwd_in_tokens(
        dh1, x2d, meta_full, lp, ffn1_norm, dgate1, dup1, wgu1, "ffn1_bwd_in", exchange=chip_sums)

    dh0 = dh0.reshape(nb, lp, D_MODEL)
    grad_x = dh0[:, PAD + N_META:, :]
    d_meta = jnp.sum(dh0[:, PAD:PAD + N_META, :], axis=0)

    small = _pack_small([d_ffn1, d_mix, d_ffn2, d_final], d_gc, d_ga, d_bf, d_conv, d_meta)
    (small_all,) = _all_gather([small], "gather_small_grads")
    small_sum = _sum_parts(small_all, "sum_small_grads")
    g_ffn1n, g_mixn, g_ffn2n, g_finaln = (small_sum[8 * t:8 * t + 8].reshape(1, D_MODEL) for t in range(4))
    g_gc = small_sum[32:36].reshape(1, CONV_DIM)
    g_ga = small_sum[36:40].reshape(1, ATTN_DIM)
    g_bf = small_sum[40:41, 0:N_HEADS]
    g_conv_full = small_sum[41:53].reshape(3, CONV_DIM)
    g_meta_full = small_sum[53:181].reshape(N_META, D_MODEL)
    g_conv = lax.dynamic_slice_in_dim(g_conv_full, me * (CONV_DIM // N_DEV), CONV_DIM // N_DEV, axis=1)
    g_meta = lax.dynamic_slice_in_dim(g_meta_full, me * (D_MODEL // N_DEV), D_MODEL // N_DEV, axis=1)

    weights = {
        "meta_tokens": (g_meta[None], meta_tokens, m_meta_tokens, v_meta_tokens),
        "ffn1_norm": (g_ffn1n[None], ffn1_norm, m_ffn1_norm, v_ffn1_norm),
        "ffn1_w_gu": (p_wgu1, ffn1_w_gu[0].T, m_ffn1_w_gu[0].T, v_ffn1_w_gu[0].T),
        "ffn1_w_down": (p_wd1, ffn1_w_down[0], m_ffn1_w_down[0], v_ffn1_w_down[0]),
        "mix_norm": (g_mixn[None], mix_norm, m_mix_norm, v_mix_norm),
        "w_in": (p_win, w_in[0].T, m_w_in[0].T, v_w_in[0].T),
        "conv_w": (g_conv[None], conv_w[0], m_conv_w[0], v_conv_w[0]),
        "b_f": (g_bf[None], b_f, m_b_f, v_b_f),
        "out_norm_conv": (g_gc[None], out_norm_conv, m_out_norm_conv, v_out_norm_conv),
        "out_norm_attn": (g_ga[None], out_norm_attn, m_out_norm_attn, v_out_norm_attn),
        "w_out": (p_wout, w_out[0], m_w_out[0], v_w_out[0]),
        "ffn2_norm": (g_ffn2n[None], ffn2_norm, m_ffn2_norm, v_ffn2_norm),
        "ffn2_w_gu": (p_wgu2, ffn2_w_gu[0].T, m_ffn2_w_gu[0].T, v_ffn2_w_gu[0].T),
        "ffn2_w_down": (p_wd2, ffn2_w_down[0], m_ffn2_w_down[0], v_ffn2_w_down[0]),
        "final_norm": (g_finaln[None], final_norm.reshape(1, D_MODEL), m_final_norm.reshape(1, D_MODEL),
                       v_final_norm.reshape(1, D_MODEL)),
    }
    shapes = {"meta_tokens": meta_tokens.shape, "ffn1_norm": ffn1_norm.shape, "ffn1_w_gu": ffn1_w_gu.shape,
              "ffn1_w_down": ffn1_w_down.shape, "mix_norm": mix_norm.shape, "w_in": w_in.shape,
              "conv_w": conv_w.shape, "b_f": b_f.shape, "out_norm_conv": out_norm_conv.shape,
              "out_norm_attn": out_norm_attn.shape, "w_out": w_out.shape, "ffn2_norm": ffn2_norm.shape,
              "ffn2_w_gu": ffn2_w_gu.shape, "ffn2_w_down": ffn2_w_down.shape, "final_norm": final_norm.shape}
    grads, deltas, new_m, new_v = [], [], [], []
    for name, (p, w, m, vv) in weights.items():
        g, d, nm, nv = _adamw(p, w, m, vv, "adamw_" + name)
        if name in ("ffn1_w_gu", "ffn2_w_gu", "w_in"):
            g, d, nm, nv = g.T, d.T, nm.T, nv.T
        shape = shapes[name]
        grads.append(g.reshape(shape))
        deltas.append(d.reshape(shape))
        new_m.append(nm.reshape(shape))
        new_v.append(nv.reshape(shape))

    loss = lax.psum(loss_part[0, 0], ("x", "y", "c"))
    return (loss, grad_x, *grads, *deltas, *new_m, *new_v)
```

```python
import jax
import jax.numpy as jnp
from jax import lax
from jax.experimental import pallas as pl
from jax.experimental.pallas import tpu as pltpu

F32 = jnp.float32
BF16 = jnp.bfloat16

N_DEV = 8
D_MODEL = 1024
N_META = 16
PAD = 128 - N_META
CONV_DIM = 512
ATTN_DIM = 512
HEAD_DIM = 64
N_HEADS = 8
N_PAIRS = N_HEADS // 2
D_FF = 2816
N_CHUNK = 4
F_CHUNK = D_FF // N_CHUNK
IN_DIM = 3080
IN_PAD = 3200
IN_MAIN = 3072
EPS = 1e-6
NEG = -1e30
TQ = 128
TK = 512
VMEM_LIMIT = 56 * 1024 * 1024

ADAM_LR = 0.001
ADAM_B1 = 0.9
ADAM_B2 = 0.999
ADAM_EPS = 1e-08
ADAM_WD = 0.01
ADAM_STEP = 10

MESH = pl.DeviceIdType.MESH
ANY = pl.BlockSpec(memory_space=pl.ANY)


def _params(sem=None):
    return pltpu.CompilerParams(dimension_semantics=sem, vmem_limit_bytes=VMEM_LIMIT)


def _row_tile(n, prefer):
    for t in (prefer, 512, 256, 128):
        if t <= n and n % t == 0:
            return t
    raise ValueError(f"no row tile for {n}")


def _dot(a, b):
    return jnp.dot(a, b, preferred_element_type=F32)


def _dot_nt(a, b):
    return lax.dot_general(a, b, (((1,), (1,)), ((), ())), preferred_element_type=F32)


def _dot_tn(a, b):
    return lax.dot_general(a, b, (((0,), (0,)), ((), ())), preferred_element_type=F32)


def _rms(x, g):
    r = lax.rsqrt(jnp.mean(x * x, axis=-1, keepdims=True) + EPS)
    xhat = x * r
    return xhat * g, xhat, r


def _rms_bwd(dn, xhat, r, g):
    dxhat = dn * g
    return r * (dxhat - xhat * jnp.mean(dxhat * xhat, axis=-1, keepdims=True))


def _sigmoid(x):
    return 1.0 / (1.0 + jnp.exp(-x))


def _place():
    return lax.axis_index("x"), lax.axis_index("y"), lax.axis_index("c")


def _comm_sems(nw):
    return [pltpu.SemaphoreType.DMA((nw, 7)), pltpu.SemaphoreType.DMA((nw, 7)), pltpu.SemaphoreType.DMA((nw,))]


class _Gather:
    def __init__(self, ins, outs, sems):
        self.ins, self.outs = ins, outs
        self.send, self.recv, self.local = sems
        x, y, c = _place()
        self.c = c
        self.me, self.sibling = (x, y, c), (x, y, 1 - c)
        self.chips = [(1 - x, y), (x, 1 - y), (1 - x, 1 - y)]

    def _copy(self, w, k, block, to, own=False):
        slot = self.outs[w].at[4 * block[0] + 2 * block[1] + block[2]]
        return pltpu.make_async_remote_copy(
            src_ref=self.ins[w] if own else slot, dst_ref=slot,
            send_sem=self.send.at[w, k], recv_sem=self.recv.at[w, k], device_id=to, device_id_type=MESH)

    def _mine(self, w):
        x, y, c = self.me
        return pltpu.make_async_copy(self.ins[w], self.outs[w].at[4 * x + 2 * y + c], self.local.at[w])

    def _first(self, w):
        return ([self._copy(w, 0, self.me, self.sibling, own=True)]
                + [self._copy(w, 1 + j, self.me, (*chip, self.c), own=True) for j, chip in enumerate(self.chips)])

    def _passed(self, w):
        return [self._copy(w, 4 + j, (*chip, self.c), self.sibling) for j, chip in enumerate(self.chips)]

    def start(self):
        for w in range(len(self.ins)):
            self._mine(w).start()
        for w in range(len(self.ins)):
            for cp in self._first(w):
                cp.start()

    def forward(self):
        for w in range(len(self.ins)):
            for j, chip in enumerate(self.chips):
                self._copy(w, 1 + j, (*chip, self.c), self.me).wait_recv()
                self._passed(w)[j].start()

    def finish(self):
        for w in range(len(self.ins)):
            self._copy(w, 0, self.sibling, self.me).wait_recv()
            for j, chip in enumerate(self.chips):
                self._copy(w, 4 + j, (*chip, 1 - self.c), self.me).wait_recv()
        for w in range(len(self.ins)):
            for cp in self._first(w) + self._passed(w):
                cp.wait_send()
            self._mine(w).wait()


class _Exchange:
    def __init__(self, ins, outs, sems):
        self.ins, self.outs = ins, outs
        self.send, self.recv, self.local = sems
        self.x, self.y, self.c = _place()
        self.me = 4 * self.x + 2 * self.y + self.c

    def _copy(self, w, k):
        flip = lambda v, bit: 1 - v if bit else v
        peer = (flip(self.x, ((k + 1) >> 2) & 1), flip(self.y, ((k + 1) >> 1) & 1), flip(self.c, (k + 1) & 1))
        return pltpu.make_async_remote_copy(
            src_ref=self.ins[w].at[4 * peer[0] + 2 * peer[1] + peer[2]], dst_ref=self.outs[w].at[self.me],
            send_sem=self.send.at[w, k], recv_sem=self.recv.at[w, k], device_id=peer, device_id_type=MESH)

    def _mine(self, w):
        return pltpu.make_async_copy(self.ins[w].at[self.me], self.outs[w].at[self.me], self.local.at[w])

    def start(self):
        for w in range(len(self.ins)):
            self._mine(w).start()
            for k in range(N_DEV - 1):
                self._copy(w, k).start()

    def finish(self):
        for w in range(len(self.ins)):
            for k in range(N_DEV - 1):
                self._copy(w, k).wait()
            self._mine(w).wait()


class _PairExchange:
    def __init__(self, ins, outs, sems):
        self.ins, self.outs = ins, outs
        self.send, self.recv, _ = sems
        x, y, self.c = _place()
        self.sibling = (x, y, 1 - self.c)

    def _copy(self, w, t):
        return pltpu.make_async_remote_copy(
            src_ref=self.ins[w].at[2 * t + 1 - self.c], dst_ref=self.outs[w].at[t],
            send_sem=self.send.at[w, t], recv_sem=self.recv.at[w, t], device_id=self.sibling, device_id_type=MESH)

    def start(self):
        for w in range(len(self.ins)):
            for t in range(4):
                self._copy(w, t).start()

    def finish(self):
        for w in range(len(self.ins)):
            for t in range(4):
                self._copy(w, t).wait()


class _ChipExchange:
    def __init__(self, ins, outs, sems):
        self.ins, self.outs = ins, outs
        self.send, self.recv, self.local = sems
        self.x, self.y, self.c = _place()
        self.chip = 2 * self.x + self.y

    def _copy(self, w, k):
        flip = lambda v, bit: 1 - v if bit else v
        px, py = flip(self.x, ((k + 1) >> 1) & 1), flip(self.y, (k + 1) & 1)
        return pltpu.make_async_remote_copy(
            src_ref=self.ins[w].at[2 * px + py], dst_ref=self.outs[w].at[self.chip],
            send_sem=self.send.at[w, k], recv_sem=self.recv.at[w, k], device_id=(px, py, self.c),
            device_id_type=MESH)

    def _mine(self, w):
        return pltpu.make_async_copy(self.ins[w].at[self.chip], self.outs[w].at[self.chip], self.local.at[w])

    def start(self):
        for w in range(len(self.ins)):
            self._mine(w).start()
            for k in range(3):
                self._copy(w, k).start()

    def finish(self):
        for w in range(len(self.ins)):
            for k in range(3):
                self._copy(w, k).wait()
            self._mine(w).wait()


def _pair_exchange(xs, name):
    nw = len(xs)

    def body(*refs):
        comm = _PairExchange(refs[:nw], refs[nw:2 * nw], refs[2 * nw:])
        comm.start()
        comm.finish()

    return pl.pallas_call(
        body, name=name, in_specs=[ANY] * nw, out_specs=[ANY] * nw,
        out_shape=[jax.ShapeDtypeStruct((4,) + a.shape[1:], a.dtype) for a in xs],
        scratch_shapes=_comm_sems(nw),
    )(*xs)


def _pair_sum(own, got, name):
    _, r, c = own.shape
    tr = r
    for t in (256, 128, 64, 32, 16):
        if r % t == 0 and r > t:
            tr = t
            break

    def body(own_ref, got_ref, out_ref):
        mine = jnp.where(lax.axis_index("c") == 0, own_ref[:, 0].astype(F32), own_ref[:, 1].astype(F32))
        out_ref[...] = (mine + got_ref[...].astype(F32)).astype(BF16)

    return pl.pallas_call(
        body, name=name, grid=(r // tr,),
        in_specs=[pl.BlockSpec((4, 2, tr, c), lambda i: (0, 0, i, 0)), pl.BlockSpec((4, tr, c), lambda i: (0, i, 0))],
        out_specs=pl.BlockSpec((4, tr, c), lambda i: (0, i, 0)),
        out_shape=jax.ShapeDtypeStruct((4, r, c), BF16),
        compiler_params=_params(("parallel",)),
    )(own.reshape(4, 2, r, c), got)


def _split_refs(refs, n_in, n_comm, n_out, n_scr):
    a = n_in
    b = a + n_comm
    c = b + n_out
    d = c + n_comm
    e = d + n_scr
    return refs[:a], refs[a:b], refs[b:c], refs[c:d], refs[d:e], refs[e:]


def _all_gather(xs, name):
    nw = len(xs)

    def body(*refs):
        comm = _Gather(refs[:nw], refs[nw:2 * nw], refs[2 * nw:])
        comm.start()
        comm.forward()
        comm.finish()

    return pl.pallas_call(
        body, name=name, in_specs=[ANY] * nw, out_specs=[ANY] * nw,
        out_shape=[jax.ShapeDtypeStruct((N_DEV,) + a.shape, a.dtype) for a in xs],
        scratch_shapes=_comm_sems(nw),
    )(*xs)


def _ffn_fwd(h, gain, wgu, wd, name, gather=()):
    n = h.shape[0]
    tm = _row_tile(n, 512)
    n_i = n // tm
    nw = len(gather)

    def body(*refs):
        (h_ref, g_ref, wgu_ref, wd_ref), gin, (out_ref, gate_ref, up_ref), gout, (n_scr, acc_scr), sems = \
            _split_refs(refs, 4, nw, 3, 2)
        i = pl.program_id(0)
        j = pl.program_id(1)
        if nw:
            comm = _Gather(gin, gout, sems)
            pl.when((i == 0) & (j == 0))(comm.start)
            pl.when((i == (3 * n_i) // 4) & (j == 0))(comm.forward)

        @pl.when(j == 0)
        def _():
            y, _, _ = _rms(h_ref[...], g_ref[...])
            n_scr[...] = y.astype(BF16)
            acc_scr[...] = jnp.zeros_like(acc_scr)

        nb = n_scr[...]
        gate = _dot(nb, wgu_ref[0, 0])
        up = _dot(nb, wgu_ref[1, 0])
        gate_ref[0] = gate.astype(BF16)
        up_ref[0] = up.astype(BF16)
        act = (gate * _sigmoid(gate) * up).astype(BF16)
        acc_scr[...] += _dot(act, wd_ref[0])

        @pl.when(j == N_CHUNK - 1)
        def _():
            out_ref[...] = h_ref[...] + 0.5 * acc_scr[...]

        if nw:
            pl.when((i == n_i - 1) & (j == N_CHUNK - 1))(comm.finish)

    return pl.pallas_call(
        body, name=name, grid=(n_i, N_CHUNK),
        in_specs=[pl.BlockSpec((tm, D_MODEL), lambda i, j: (i, 0)),
                  pl.BlockSpec((1, D_MODEL), lambda i, j: (0, 0)),
                  pl.BlockSpec((2, 1, D_MODEL, F_CHUNK), lambda i, j: (0, j, 0, 0)),
                  pl.BlockSpec((1, F_CHUNK, D_MODEL), lambda i, j: (j, 0, 0))] + [ANY] * nw,
        out_specs=[pl.BlockSpec((tm, D_MODEL), lambda i, j: (i, 0)),
                   pl.BlockSpec((1, tm, F_CHUNK), lambda i, j: (j, i, 0)),
                   pl.BlockSpec((1, tm, F_CHUNK), lambda i, j: (j, i, 0))] + [ANY] * nw,
        out_shape=[jax.ShapeDtypeStruct((n, D_MODEL), F32),
                   jax.ShapeDtypeStruct((N_CHUNK, n, F_CHUNK), BF16),
                   jax.ShapeDtypeStruct((N_CHUNK, n, F_CHUNK), BF16)]
        + [jax.ShapeDtypeStruct((N_DEV,) + a.shape, a.dtype) for a in gather],
        scratch_shapes=[pltpu.VMEM((tm, D_MODEL), BF16), pltpu.VMEM((tm, D_MODEL), F32)]
        + (_comm_sems(nw) if nw else []),
        compiler_params=_params(("arbitrary", "arbitrary")),
    )(h, gain, wgu, wd, *gather)


def _ffn_bwd_x(dh_out, h_in, gain, gate, up, wgu, wd, name):
    n = h_in.shape[0]
    tm = _row_tile(n, 512)

    def body(dh_ref, h_ref, g_ref, gate_ref, up_ref, wgu_ref, wd_ref,
             dhin_ref, dgate_ref, dup_ref, dgain_ref, dhb_scr, acc_scr):
        i = pl.program_id(0)
        j = pl.program_id(1)

        @pl.when((i == 0) & (j == 0))
        def _():
            dgain_ref[...] = jnp.zeros_like(dgain_ref)

        @pl.when(j == 0)
        def _():
            dhb_scr[...] = (0.5 * dh_ref[...]).astype(BF16)
            acc_scr[...] = jnp.zeros_like(acc_scr)

        da = _dot_nt(dhb_scr[...], wd_ref[0])
        g = gate_ref[0].astype(F32)
        u = up_ref[0].astype(F32)
        sig = _sigmoid(g)
        dgate = (da * u * (sig * (1.0 + g * (1.0 - sig)))).astype(BF16)
        dup = (da * (g * sig)).astype(BF16)
        dgate_ref[0] = dgate
        dup_ref[0] = dup
        acc_scr[...] += _dot_nt(dgate, wgu_ref[0, 0]) + _dot_nt(dup, wgu_ref[1, 0])

        @pl.when(j == N_CHUNK - 1)
        def _():
            gain_v = g_ref[...]
            _, xhat, r = _rms(h_ref[...], gain_v)
            dn = acc_scr[...]
            dhin_ref[...] = dh_ref[...] + _rms_bwd(dn, xhat, r, gain_v)
            dgain_ref[...] += jnp.sum(dn * xhat, axis=0, keepdims=True)

    chunk = pl.BlockSpec((1, tm, F_CHUNK), lambda i, j: (j, i, 0))
    rows = pl.BlockSpec((tm, D_MODEL), lambda i, j: (i, 0))
    vec = pl.BlockSpec((1, D_MODEL), lambda i, j: (0, 0))
    return pl.pallas_call(
        body, name=name, grid=(n // tm, N_CHUNK),
        in_specs=[rows, rows, vec, chunk, chunk,
                  pl.BlockSpec((2, 1, D_MODEL, F_CHUNK), lambda i, j: (0, j, 0, 0)),
                  pl.BlockSpec((1, F_CHUNK, D_MODEL), lambda i, j: (j, 0, 0))],
        out_specs=[rows, chunk, chunk, vec],
        out_shape=[jax.ShapeDtypeStruct((n, D_MODEL), F32),
                   jax.ShapeDtypeStruct((N_CHUNK, n, F_CHUNK), BF16),
                   jax.ShapeDtypeStruct((N_CHUNK, n, F_CHUNK), BF16),
                   jax.ShapeDtypeStruct((1, D_MODEL), F32)],
        scratch_shapes=[pltpu.VMEM((tm, D_MODEL), BF16), pltpu.VMEM((tm, D_MODEL), F32)],
        compiler_params=_params(("arbitrary", "arbitrary")),
    )(dh_out, h_in, gain, gate, up, wgu, wd)


def _ffn_bwd_act(dh_out, gate, up, wd, name):
    n = dh_out.shape[0]
    tm = _row_tile(n, 512)

    def body(dh_ref, gate_ref, up_ref, wd_ref, dgate_ref, dup_ref, dhb_scr):
        @pl.when(pl.program_id(1) == 0)
        def _():
            dhb_scr[...] = (0.5 * dh_ref[...]).astype(BF16)

        da = _dot_nt(dhb_scr[...], wd_ref[0])
        g = gate_ref[0].astype(F32)
        u = up_ref[0].astype(F32)
        sig = _sigmoid(g)
        dgate_ref[0] = (da * u * (sig * (1.0 + g * (1.0 - sig)))).astype(BF16)
        dup_ref[0] = (da * (g * sig)).astype(BF16)

    chunk = pl.BlockSpec((1, tm, F_CHUNK), lambda i, j: (j, i, 0))
    return pl.pallas_call(
        body, name=name, grid=(n // tm, N_CHUNK),
        in_specs=[pl.BlockSpec((tm, D_MODEL), lambda i, j: (i, 0)), chunk, chunk,
                  pl.BlockSpec((1, F_CHUNK, D_MODEL), lambda i, j: (j, 0, 0))],
        out_specs=[chunk, chunk],
        out_shape=[jax.ShapeDtypeStruct((N_CHUNK, n, F_CHUNK), BF16)] * 2,
        scratch_shapes=[pltpu.VMEM((tm, D_MODEL), BF16)],
        compiler_params=_params(("parallel", "arbitrary")),
    )(dh_out, gate, up, wd)


def _ffn_bwd_in(dh_out, h_in, gain, dgate, dup, wgu, name, exchange=()):
    n = h_in.shape[0]
    tm = _row_tile(n, 512)
    n_i = n // tm
    nw = len(exchange)

    def body(*refs):
        (dh_ref, h_ref, g_ref, dgate_ref, dup_ref, wgu_ref), xin, (dhin_ref, dgain_ref), xout, (acc_scr,), sems = \
            _split_refs(refs, 6, nw, 2, 1)
        i = pl.program_id(0)
        j = pl.program_id(1)
        if nw:
            comm = _Exchange(xin, xout, sems)
            pl.when((i == 0) & (j == 0))(comm.start)

        @pl.when((i == 0) & (j == 0))
        def _():
            dgain_ref[...] = jnp.zeros_like(dgain_ref)

        @pl.when(j == 0)
        def _():
            acc_scr[...] = jnp.zeros_like(acc_scr)

        acc_scr[...] += _dot_nt(dgate_ref[0], wgu_ref[0, 0]) + _dot_nt(dup_ref[0], wgu_ref[1, 0])

        @pl.when(j == N_CHUNK - 1)
        def _():
            gain_v = g_ref[...]
            _, xhat, r = _rms(h_ref[...], gain_v)
            dn = acc_scr[...]
            dhin_ref[...] = dh_ref[...] + _rms_bwd(dn, xhat, r, gain_v)
            dgain_ref[...] += jnp.sum(dn * xhat, axis=0, keepdims=True)

        if nw:
            pl.when((i == n_i - 1) & (j == N_CHUNK - 1))(comm.finish)

    chunk = pl.BlockSpec((1, tm, F_CHUNK), lambda i, j: (j, i, 0))
    rows = pl.BlockSpec((tm, D_MODEL), lambda i, j: (i, 0))
    vec = pl.BlockSpec((1, D_MODEL), lambda i, j: (0, 0))
    return pl.pallas_call(
        body, name=name, grid=(n_i, N_CHUNK),
        in_specs=[rows, rows, vec, chunk, chunk,
                  pl.BlockSpec((2, 1, D_MODEL, F_CHUNK), lambda i, j: (0, j, 0, 0))] + [ANY] * nw,
        out_specs=[rows, vec] + [ANY] * nw,
        out_shape=[jax.ShapeDtypeStruct((n, D_MODEL), F32), jax.ShapeDtypeStruct((1, D_MODEL), F32)]
        + [jax.ShapeDtypeStruct(a.shape, a.dtype) for a in exchange],
        scratch_shapes=[pltpu.VMEM((tm, D_MODEL), F32)] + (_comm_sems(nw) if nw else []),
        compiler_params=_params(("arbitrary", "arbitrary")),
    )(dh_out, h_in, gain, dgate, dup, wgu, *exchange)


def _ffn_bwd_w(dh_out, h_in, gain, gate, up, dgate, dup, name):
    n = h_in.shape[0]
    tm = _row_tile(n, 512)
    n_i = n // tm

    def body(dh_ref, h_ref, g_ref, gate_ref, up_ref, dgate_ref, dup_ref, dwgu_ref, dwd_ref,
             ag_scr, au_scr, ad_scr):
        i = pl.program_id(1)

        @pl.when(i == 0)
        def _():
            ag_scr[...] = jnp.zeros_like(ag_scr)
            au_scr[...] = jnp.zeros_like(au_scr)
            ad_scr[...] = jnp.zeros_like(ad_scr)

        y, _, _ = _rms(h_ref[...], g_ref[...])
        nb = y.astype(BF16)
        ag_scr[...] += _dot_tn(nb, dgate_ref[0])
        au_scr[...] += _dot_tn(nb, dup_ref[0])
        g = gate_ref[0].astype(F32)
        act = (g * _sigmoid(g) * up_ref[0].astype(F32)).astype(BF16)
        ad_scr[...] += _dot_tn(act, (0.5 * dh_ref[...]).astype(BF16))

        @pl.when(i == n_i - 1)
        def _():
            dwgu_ref[0, 0] = ag_scr[...].astype(BF16)
            dwgu_ref[1, 0] = au_scr[...].astype(BF16)
            dwd_ref[0] = ad_scr[...].astype(BF16)

    chunk = pl.BlockSpec((1, tm, F_CHUNK), lambda j, i: (j, i, 0))
    rows = pl.BlockSpec((tm, D_MODEL), lambda j, i: (i, 0))
    return pl.pallas_call(
        body, name=name, grid=(N_CHUNK, n_i),
        in_specs=[rows, rows, pl.BlockSpec((1, D_MODEL), lambda j, i: (0, 0)), chunk, chunk, chunk, chunk],
        out_specs=[pl.BlockSpec((2, 1, D_MODEL, F_CHUNK), lambda j, i: (0, j, 0, 0)),
                   pl.BlockSpec((1, F_CHUNK, D_MODEL), lambda j, i: (j, 0, 0))],
        out_shape=[jax.ShapeDtypeStruct((2, N_CHUNK, D_MODEL, F_CHUNK), BF16),
                   jax.ShapeDtypeStruct((N_CHUNK, F_CHUNK, D_MODEL), BF16)],
        scratch_shapes=[pltpu.VMEM((D_MODEL, F_CHUNK), F32), pltpu.VMEM((D_MODEL, F_CHUNK), F32),
                        pltpu.VMEM((F_CHUNK, D_MODEL), F32)],
        compiler_params=_params(("parallel", "arbitrary")),
    )(dh_out, h_in, gain, gate, up, dgate, dup)


def _resident(shape, rank):
    zeros = (0,) * len(shape)
    index_map = (lambda i: zeros) if rank == 1 else (lambda i, j: zeros)
    return pl.BlockSpec(shape, index_map, pipeline_mode=pl.Buffered(1))


W_GU_SHAPE = (2, N_CHUNK, F_CHUNK, D_MODEL)
W_D_SHAPE = (N_CHUNK, F_CHUNK, D_MODEL)


def _ffn_fwd(h, gain, wgu, wd, name, gather=()):
    n = h.shape[0]
    tm = _row_tile(n, 512)
    n_i = n // tm
    nw = len(gather)

    def body(*refs):
        (h_ref, g_ref, wgu_ref, wd_ref), gin, (out_ref, nrm_ref, gate_ref, up_ref), gout, _, sems = \
            _split_refs(refs, 4, nw, 4, 0)
        i = pl.program_id(0)
        if nw:
            comm = _Gather(gin, gout, sems)
            pl.when(i == 0)(comm.start)
            pl.when(i == max(n_i - 3, 0))(comm.forward)

        hv = h_ref[...]
        y, _, _ = _rms(hv, g_ref[...])
        nb = y.astype(BF16)
        nrm_ref[...] = nb
        acc = jnp.zeros((tm, D_MODEL), F32)
        for j in range(N_CHUNK):
            gate = _dot_nt(nb, wgu_ref[0, j])
            up = _dot_nt(nb, wgu_ref[1, j])
            gate_ref[j] = gate.astype(BF16)
            up_ref[j] = up.astype(BF16)
            acc = acc + _dot((gate * _sigmoid(gate) * up).astype(BF16), wd_ref[j])
        out_ref[...] = hv + 0.5 * acc

        if nw:
            pl.when(i == n_i - 1)(comm.finish)

    rows = pl.BlockSpec((tm, D_MODEL), lambda i: (i, 0))
    chunks = pl.BlockSpec((N_CHUNK, tm, F_CHUNK), lambda i: (0, i, 0))
    return pl.pallas_call(
        body, name=name, grid=(n_i,),
        in_specs=[rows, pl.BlockSpec((1, D_MODEL), lambda i: (0, 0)), _resident(W_GU_SHAPE, 1),
                  _resident(W_D_SHAPE, 1)] + [ANY] * nw,
        out_specs=[rows, rows, chunks, chunks] + [ANY] * nw,
        out_shape=[jax.ShapeDtypeStruct((n, D_MODEL), F32), jax.ShapeDtypeStruct((n, D_MODEL), BF16),
                   jax.ShapeDtypeStruct((N_CHUNK, n, F_CHUNK), BF16),
                   jax.ShapeDtypeStruct((N_CHUNK, n, F_CHUNK), BF16)]
        + [jax.ShapeDtypeStruct((N_DEV,) + a.shape, a.dtype) for a in gather],
        scratch_shapes=_comm_sems(nw) if nw else [],
        compiler_params=_params(("arbitrary",)),
    )(h, gain, wgu, wd, *gather)


def _swiglu_bwd(da, gate_ref, up_ref, j):
    g = gate_ref[j].astype(F32)
    u = up_ref[j].astype(F32)
    sig = _sigmoid(g)
    return (da * u * (sig * (1.0 + g * (1.0 - sig)))).astype(BF16), (da * (g * sig)).astype(BF16)


def _ffn_bwd_x(dh_out, h_in, gain, gate, up, wgu, wd, name):
    n = h_in.shape[0]
    tm = _row_tile(n, 256)

    def body(dh_ref, h_ref, g_ref, gate_ref, up_ref, wgu_ref, wd_ref,
             dhin_ref, dhb_ref, dgate_ref, dup_ref, dgain_ref):
        @pl.when(pl.program_id(0) == 0)
        def _():
            dgain_ref[...] = jnp.zeros_like(dgain_ref)

        dhv = dh_ref[...]
        dhb = (0.5 * dhv).astype(BF16)
        dhb_ref[...] = dhb
        dn = jnp.zeros((tm, D_MODEL), F32)
        for j in range(N_CHUNK):
            dgate, dup = _swiglu_bwd(_dot_nt(dhb, wd_ref[j]), gate_ref, up_ref, j)
            dgate_ref[j] = dgate
            dup_ref[j] = dup
            dn = dn + _dot(dgate, wgu_ref[0, j]) + _dot(dup, wgu_ref[1, j])
        gain_v = g_ref[...]
        _, xhat, r = _rms(h_ref[...], gain_v)
        dhin_ref[...] = dhv + _rms_bwd(dn, xhat, r, gain_v)
        dgain_ref[...] += jnp.sum(dn * xhat, axis=0, keepdims=True)

    rows = pl.BlockSpec((tm, D_MODEL), lambda i: (i, 0))
    chunks = pl.BlockSpec((N_CHUNK, tm, F_CHUNK), lambda i: (0, i, 0))
    vec = pl.BlockSpec((1, D_MODEL), lambda i: (0, 0))
    return pl.pallas_call(
        body, name=name, grid=(n // tm,),
        in_specs=[rows, rows, vec, chunks, chunks, _resident(W_GU_SHAPE, 1), _resident(W_D_SHAPE, 1)],
        out_specs=[rows, rows, chunks, chunks, vec],
        out_shape=[jax.ShapeDtypeStruct((n, D_MODEL), F32), jax.ShapeDtypeStruct((n, D_MODEL), BF16),
                   jax.ShapeDtypeStruct((N_CHUNK, n, F_CHUNK), BF16),
                   jax.ShapeDtypeStruct((N_CHUNK, n, F_CHUNK), BF16),
                   jax.ShapeDtypeStruct((1, D_MODEL), F32)],
        compiler_params=_params(("arbitrary",)),
    )(dh_out, h_in, gain, gate, up, wgu, wd)


def _ffn_bwd_act(dh_out, gate, up, wd, name, exchange=()):
    n = dh_out.shape[0]
    tm = _row_tile(n, 512)
    n_i = n // tm
    nw = len(exchange)

    def body(*refs):
        (dh_ref, gate_ref, up_ref, wd_ref), xin, (dhb_ref, dgate_ref, dup_ref), xout, _, sems = \
            _split_refs(refs, 4, nw, 3, 0)
        i = pl.program_id(0)
        if nw:
            comm = _Exchange(xin, xout, sems)
            pl.when(i == 0)(comm.start)

        dhb = (0.5 * dh_ref[...]).astype(BF16)
        dhb_ref[...] = dhb
        for j in range(N_CHUNK):
            dgate_ref[j], dup_ref[j] = _swiglu_bwd(_dot_nt(dhb, wd_ref[j]), gate_ref, up_ref, j)

        if nw:
            pl.when(i == n_i - 1)(comm.finish)

    rows = pl.BlockSpec((tm, D_MODEL), lambda i: (i, 0))
    chunks = pl.BlockSpec((N_CHUNK, tm, F_CHUNK), lambda i: (0, i, 0))
    return pl.pallas_call(
        body, name=name, grid=(n_i,),
        in_specs=[rows, chunks, chunks, _resident(W_D_SHAPE, 1)] + [ANY] * nw,
        out_specs=[rows, chunks, chunks] + [ANY] * nw,
        out_shape=[jax.ShapeDtypeStruct((n, D_MODEL), BF16)] + [jax.ShapeDtypeStruct((N_CHUNK, n, F_CHUNK), BF16)] * 2
        + [jax.ShapeDtypeStruct(a.shape, a.dtype) for a in exchange],
        scratch_shapes=_comm_sems(nw) if nw else [],
        compiler_params=_params(("arbitrary",)),
    )(dh_out, gate, up, wd, *exchange)


def _ffn_bwd_in(dh_out, h_in, gain, dgate, dup, wgu, name, exchange=()):
    n = h_in.shape[0]
    tm = _row_tile(n, 512)
    n_i = n // tm
    nw = len(exchange)

    def body(*refs):
        (dh_ref, h_ref, g_ref, dgate_ref, dup_ref, wgu_ref), xin, (dhin_ref, dgain_ref), xout, _, sems = \
            _split_refs(refs, 6, nw, 2, 0)
        i = pl.program_id(0)
        if nw:
            comm = _ChipExchange(xin, xout, sems)
            pl.when(i == 0)(comm.start)

        @pl.when(i == 0)
        def _():
            dgain_ref[...] = jnp.zeros_like(dgain_ref)

        dn = jnp.zeros((tm, D_MODEL), F32)
        for j in range(N_CHUNK):
            dn = dn + _dot(dgate_ref[j], wgu_ref[0, j]) + _dot(dup_ref[j], wgu_ref[1, j])
        gain_v = g_ref[...]
        _, xhat, r = _rms(h_ref[...], gain_v)
        dhin_ref[...] = dh_ref[...] + _rms_bwd(dn, xhat, r, gain_v)
        dgain_ref[...] += jnp.sum(dn * xhat, axis=0, keepdims=True)

        if nw:
            pl.when(i == n_i - 1)(comm.finish)

    rows = pl.BlockSpec((tm, D_MODEL), lambda i: (i, 0))
    chunks = pl.BlockSpec((N_CHUNK, tm, F_CHUNK), lambda i: (0, i, 0))
    vec = pl.BlockSpec((1, D_MODEL), lambda i: (0, 0))
    return pl.pallas_call(
        body, name=name, grid=(n_i,),
        in_specs=[rows, rows, vec, chunks, chunks, _resident(W_GU_SHAPE, 1)] + [ANY] * nw,
        out_specs=[rows, vec] + [ANY] * nw,
        out_shape=[jax.ShapeDtypeStruct((n, D_MODEL), F32), jax.ShapeDtypeStruct((1, D_MODEL), F32)]
        + [jax.ShapeDtypeStruct(a.shape, a.dtype) for a in exchange],
        scratch_shapes=_comm_sems(nw) if nw else [],
        compiler_params=_params(("arbitrary",)),
    )(dh_out, h_in, gain, dgate, dup, wgu, *exchange)


W_GROUP = 2


def _ffn_bwd_w(dhb, nrm, gate, up, dgate, dup, name):
    n = nrm.shape[0]
    tm = _row_tile(n, 512)
    n_i = n // tm

    def body(dhb_ref, nrm_ref, gate_ref, up_ref, dgate_ref, dup_ref, dwgu_ref, dwd_ref, ag_scr, au_scr, ad_scr):
        i = pl.program_id(1)

        @pl.when(i == 0)
        def _():
            ag_scr[...] = jnp.zeros_like(ag_scr)
            au_scr[...] = jnp.zeros_like(au_scr)
            ad_scr[...] = jnp.zeros_like(ad_scr)

        nb = nrm_ref[...]
        dhv = dhb_ref[...]
        for jj in range(W_GROUP):
            ag_scr[jj] += _dot_tn(dgate_ref[jj], nb)
            au_scr[jj] += _dot_tn(dup_ref[jj], nb)
            g = gate_ref[jj].astype(F32)
            act = (g * _sigmoid(g) * up_ref[jj].astype(F32)).astype(BF16)
            ad_scr[jj] += _dot_tn(act, dhv)

        @pl.when(i == n_i - 1)
        def _():
            dwgu_ref[0] = ag_scr[...].astype(BF16)
            dwgu_ref[1] = au_scr[...].astype(BF16)
            dwd_ref[...] = ad_scr[...].astype(BF16)

    chunks = pl.BlockSpec((W_GROUP, tm, F_CHUNK), lambda g, i: (g, i, 0))
    rows = pl.BlockSpec((tm, D_MODEL), lambda g, i: (i, 0))
    return pl.pallas_call(
        body, name=name, grid=(N_CHUNK // W_GROUP, n_i),
        in_specs=[rows, rows, chunks, chunks, chunks, chunks],
        out_specs=[pl.BlockSpec((2, W_GROUP, F_CHUNK, D_MODEL), lambda g, i: (0, g, 0, 0)),
                   pl.BlockSpec((W_GROUP, F_CHUNK, D_MODEL), lambda g, i: (g, 0, 0))],
        out_shape=[jax.ShapeDtypeStruct(W_GU_SHAPE, BF16), jax.ShapeDtypeStruct(W_D_SHAPE, BF16)],
        scratch_shapes=[pltpu.VMEM((W_GROUP, F_CHUNK, D_MODEL), F32), pltpu.VMEM((W_GROUP, F_CHUNK, D_MODEL), F32),
                        pltpu.VMEM((W_GROUP, F_CHUNK, D_MODEL), F32)],
        compiler_params=_params(("parallel", "arbitrary")),
    )(dhb, nrm, gate, up, dgate, dup)


HID_PIECES = ((0, 1024), (1024, 2048), (2048, D_FF))
W_GU_SHAPE = (2, D_FF, D_MODEL)
W_D_SHAPE = (D_FF, D_MODEL)


def _ffn_fwd(h, gain, wgu, wd, name, gather=()):
    n = h.shape[0]
    tm = _row_tile(n, 512)
    n_i = n // tm
    nw = len(gather)

    def body(*refs):
        (h_ref, g_ref, wgu_ref, wd_ref), gin, (out_ref, nrm_ref, gate_ref, up_ref), gout, _, sems = \
            _split_refs(refs, 4, nw, 4, 0)
        i = pl.program_id(0)
        if nw:
            comm = _Gather(gin, gout, sems)
            pl.when(i == 0)(comm.start)
            pl.when(i == max(n_i - 3, 0))(comm.forward)

        hv = h_ref[...]
        y, _, _ = _rms(hv, g_ref[...])
        nb = y.astype(BF16)
        nrm_ref[...] = nb
        acc = jnp.zeros((tm, D_MODEL), F32)
        for a, b in HID_PIECES:
            gate = _dot_nt(nb, wgu_ref[0, a:b, :])
            up = _dot_nt(nb, wgu_ref[1, a:b, :])
            gate_ref[:, a:b] = gate.astype(BF16)
            up_ref[:, a:b] = up.astype(BF16)
            acc = acc + _dot((gate * _sigmoid(gate) * up).astype(BF16), wd_ref[a:b, :])
        out_ref[...] = hv + 0.5 * acc

        if nw:
            pl.when(i == n_i - 1)(comm.finish)

    rows = pl.BlockSpec((tm, D_MODEL), lambda i: (i, 0))
    hid = pl.BlockSpec((tm, D_FF), lambda i: (i, 0))
    return pl.pallas_call(
        body, name=name, grid=(n_i,),
        in_specs=[rows, pl.BlockSpec((1, D_MODEL), lambda i: (0, 0)), _resident(W_GU_SHAPE, 1),
                  _resident(W_D_SHAPE, 1)] + [ANY] * nw,
        out_specs=[rows, rows, hid, hid] + [ANY] * nw,
        out_shape=[jax.ShapeDtypeStruct((n, D_MODEL), F32), jax.ShapeDtypeStruct((n, D_MODEL), BF16),
                   jax.ShapeDtypeStruct((n, D_FF), BF16), jax.ShapeDtypeStruct((n, D_FF), BF16)]
        + [jax.ShapeDtypeStruct((N_DEV,) + a.shape, a.dtype) for a in gather],
        scratch_shapes=_comm_sems(nw) if nw else [],
        compiler_params=_params(("arbitrary",)),
    )(h, gain, wgu, wd, *gather)


def _swiglu_bwd(da, gate_ref, up_ref, a, b):
    g = gate_ref[:, a:b].astype(F32)
    u = up_ref[:, a:b].astype(F32)
    sig = _sigmoid(g)
    return (da * u * (sig * (1.0 + g * (1.0 - sig)))).astype(BF16), (da * (g * sig)).astype(BF16)


def _ffn_bwd_x(dh_out, h_in, gain, gate, up, wgu, wd, name):
    n = h_in.shape[0]
    tm = _row_tile(n, 256)

    def body(dh_ref, h_ref, g_ref, gate_ref, up_ref, wgu_ref, wd_ref,
             dhin_ref, dhb_ref, dgate_ref, dup_ref, dgain_ref):
        @pl.when(pl.program_id(0) == 0)
        def _():
            dgain_ref[...] = jnp.zeros_like(dgain_ref)

        dhv = dh_ref[...]
        dhb = (0.5 * dhv).astype(BF16)
        dhb_ref[...] = dhb
        dn = jnp.zeros((tm, D_MODEL), F32)
        for a, b in HID_PIECES:
            dgate, dup = _swiglu_bwd(_dot_nt(dhb, wd_ref[a:b, :]), gate_ref, up_ref, a, b)
            dgate_ref[:, a:b] = dgate
            dup_ref[:, a:b] = dup
            dn = dn + _dot(dgate, wgu_ref[0, a:b, :]) + _dot(dup, wgu_ref[1, a:b, :])
        gain_v = g_ref[...]
        _, xhat, r = _rms(h_ref[...], gain_v)
        dhin_ref[...] = dhv + _rms_bwd(dn, xhat, r, gain_v)
        dgain_ref[...] += jnp.sum(dn * xhat, axis=0, keepdims=True)

    rows = pl.BlockSpec((tm, D_MODEL), lambda i: (i, 0))
    hid = pl.BlockSpec((tm, D_FF), lambda i: (i, 0))
    vec = pl.BlockSpec((1, D_MODEL), lambda i: (0, 0))
    return pl.pallas_call(
        body, name=name, grid=(n // tm,),
        in_specs=[rows, rows, vec, hid, hid, _resident(W_GU_SHAPE, 1), _resident(W_D_SHAPE, 1)],
        out_specs=[rows, rows, hid, hid, vec],
        out_shape=[jax.ShapeDtypeStruct((n, D_MODEL), F32), jax.ShapeDtypeStruct((n, D_MODEL), BF16),
                   jax.ShapeDtypeStruct((n, D_FF), BF16), jax.ShapeDtypeStruct((n, D_FF), BF16),
                   jax.ShapeDtypeStruct((1, D_MODEL), F32)],
        compiler_params=_params(("arbitrary",)),
    )(dh_out, h_in, gain, gate, up, wgu, wd)


def _ffn_bwd_act(dh_out, gate, up, wd, name, exchange=()):
    n = dh_out.shape[0]
    tm = _row_tile(n, 512)
    n_i = n // tm
    nw = len(exchange)

    def body(*refs):
        (dh_ref, gate_ref, up_ref, wd_ref), xin, (dhb_ref, dgate_ref, dup_ref), xout, _, sems = \
            _split_refs(refs, 4, nw, 3, 0)
        i = pl.program_id(0)
        if nw:
            comm = _Exchange(xin, xout, sems)
            pl.when(i == 0)(comm.start)

        dhb = (0.5 * dh_ref[...]).astype(BF16)
        dhb_ref[...] = dhb
        for a, b in HID_PIECES:
            dgate_ref[:, a:b], dup_ref[:, a:b] = _swiglu_bwd(_dot_nt(dhb, wd_ref[a:b, :]), gate_ref, up_ref, a, b)

        if nw:
            pl.when(i == n_i - 1)(comm.finish)

    rows = pl.BlockSpec((tm, D_MODEL), lambda i: (i, 0))
    hid = pl.BlockSpec((tm, D_FF), lambda i: (i, 0))
    return pl.pallas_call(
        body, name=name, grid=(n_i,),
        in_specs=[rows, hid, hid, _resident(W_D_SHAPE, 1)] + [ANY] * nw,
        out_specs=[rows, hid, hid] + [ANY] * nw,
        out_shape=[jax.ShapeDtypeStruct((n, D_MODEL), BF16)] + [jax.ShapeDtypeStruct((n, D_FF), BF16)] * 2
        + [jax.ShapeDtypeStruct(a.shape, a.dtype) for a in exchange],
        scratch_shapes=_comm_sems(nw) if nw else [],
        compiler_params=_params(("arbitrary",)),
    )(dh_out, gate, up, wd, *exchange)


def _ffn_bwd_in(dh_out, h_in, gain, dgate, dup, wgu, name, exchange=()):
    n = h_in.shape[0]
    tm = _row_tile(n, 512)
    n_i = n // tm
    nw = len(exchange)

    def body(*refs):
        (dh_ref, h_ref, g_ref, dgate_ref, dup_ref, wgu_ref), xin, (dhin_ref, dgain_ref), xout, _, sems = \
            _split_refs(refs, 6, nw, 2, 0)
        i = pl.program_id(0)
        if nw:
            comm = _ChipExchange(xin, xout, sems)
            pl.when(i == 0)(comm.start)

        @pl.when(i == 0)
        def _():
            dgain_ref[...] = jnp.zeros_like(dgain_ref)

        dn = jnp.zeros((tm, D_MODEL), F32)
        for a, b in HID_PIECES:
            dn = dn + _dot(dgate_ref[:, a:b], wgu_ref[0, a:b, :]) + _dot(dup_ref[:, a:b], wgu_ref[1, a:b, :])
        gain_v = g_ref[...]
        _, xhat, r = _rms(h_ref[...], gain_v)
        dhin_ref[...] = dh_ref[...] + _rms_bwd(dn, xhat, r, gain_v)
        dgain_ref[...] += jnp.sum(dn * xhat, axis=0, keepdims=True)

        if nw:
            pl.when(i == n_i - 1)(comm.finish)

    rows = pl.BlockSpec((tm, D_MODEL), lambda i: (i, 0))
    hid = pl.BlockSpec((tm, D_FF), lambda i: (i, 0))
    vec = pl.BlockSpec((1, D_MODEL), lambda i: (0, 0))
    return pl.pallas_call(
        body, name=name, grid=(n_i,),
        in_specs=[rows, rows, vec, hid, hid, _resident(W_GU_SHAPE, 1)] + [ANY] * nw,
        out_specs=[rows, vec] + [ANY] * nw,
        out_shape=[jax.ShapeDtypeStruct((n, D_MODEL), F32), jax.ShapeDtypeStruct((1, D_MODEL), F32)]
        + [jax.ShapeDtypeStruct(a.shape, a.dtype) for a in exchange],
        scratch_shapes=_comm_sems(nw) if nw else [],
        compiler_params=_params(("arbitrary",)),
    )(dh_out, h_in, gain, dgate, dup, wgu, *exchange)


def _token_spec(k, ksub, nq):
    def index_map(i):
        s = ksub * i + k
        return ((s // nq) * (nq - 1) + jnp.maximum(s % nq, 1) - 1, 0)
    return pl.BlockSpec((128, D_MODEL), index_map)


def _is_lead(i, k, ksub, nq):
    return ((ksub * i + k) % nq) == 0


def _assemble_rows(i, x_refs, meta_ref, nq):
    ksub = len(x_refs)
    lead = jnp.concatenate([jnp.zeros((PAD, D_MODEL), F32), meta_ref[...]], axis=0)
    return jnp.concatenate([jnp.where(_is_lead(i, k, ksub, nq), lead, x_refs[k][...]) for k in range(ksub)], axis=0)


def _ffn_fwd_tokens(x2d, meta, lp, gain, wgu, wd, name, gather=()):
    nq = lp // 128
    n = (x2d.shape[0] // (nq - 1)) * nq
    tm = _row_tile(n, 512)
    ksub = tm // 128
    n_i = n // tm
    nw = len(gather)

    def body(*refs):
        x_refs = refs[:ksub]
        (meta_ref, g_ref, wgu_ref, wd_ref), gin, (out_ref, nrm_ref, gate_ref, up_ref), gout, _, sems = \
            _split_refs(refs[ksub:], 4, nw, 4, 0)
        i = pl.program_id(0)
        if nw:
            comm = _Gather(gin, gout, sems)
            pl.when(i == 0)(comm.start)
            pl.when(i == max(n_i - 3, 0))(comm.forward)

        hv = _assemble_rows(i, x_refs, meta_ref, nq)
        y, _, _ = _rms(hv, g_ref[...])
        nb = y.astype(BF16)
        nrm_ref[...] = nb
        acc = jnp.zeros((tm, D_MODEL), F32)
        for a, b in HID_PIECES:
            gate = _dot_nt(nb, wgu_ref[0, a:b, :])
            up = _dot_nt(nb, wgu_ref[1, a:b, :])
            gate_ref[:, a:b] = gate.astype(BF16)
            up_ref[:, a:b] = up.astype(BF16)
            acc = acc + _dot((gate * _sigmoid(gate) * up).astype(BF16), wd_ref[a:b, :])
        out_ref[...] = hv + 0.5 * acc

        if nw:
            pl.when(i == n_i - 1)(comm.finish)

    rows = pl.BlockSpec((tm, D_MODEL), lambda i: (i, 0))
    hid = pl.BlockSpec((tm, D_FF), lambda i: (i, 0))
    return pl.pallas_call(
        body, name=name, grid=(n_i,),
        in_specs=[_token_spec(k, ksub, nq) for k in range(ksub)]
        + [pl.BlockSpec((N_META, D_MODEL), lambda i: (0, 0)), pl.BlockSpec((1, D_MODEL), lambda i: (0, 0)),
           _resident(W_GU_SHAPE, 1), _resident(W_D_SHAPE, 1)] + [ANY] * nw,
        out_specs=[rows, rows, hid, hid] + [ANY] * nw,
        out_shape=[jax.ShapeDtypeStruct((n, D_MODEL), F32), jax.ShapeDtypeStruct((n, D_MODEL), BF16),
                   jax.ShapeDtypeStruct((n, D_FF), BF16), jax.ShapeDtypeStruct((n, D_FF), BF16)]
        + [jax.ShapeDtypeStruct((N_DEV,) + a.shape, a.dtype) for a in gather],
        scratch_shapes=_comm_sems(nw) if nw else [],
        compiler_params=_params(("arbitrary",)),
    )(*([x2d] * ksub), meta, gain, wgu, wd, *gather)


def _ffn_fwd_loss(h, gain, wgu, wd, gfinal, target, lp, name):
    n = h.shape[0]
    nq = lp // 128
    tm = _row_tile(n, 512)
    ksub = tm // 128
    n_i = n // tm

    def body(*refs):
        t_refs = refs[:ksub]
        h_ref, g_ref, wgu_ref, wd_ref, gf_ref, dh_ref, nrm_ref, gate_ref, up_ref, loss_ref, dgf_ref = refs[ksub:]
        i = pl.program_id(0)

        @pl.when(i == 0)
        def _():
            loss_ref[...] = jnp.zeros_like(loss_ref)
            dgf_ref[...] = jnp.zeros_like(dgf_ref)

        hv = h_ref[...]
        y, _, _ = _rms(hv, g_ref[...])
        nb = y.astype(BF16)
        nrm_ref[...] = nb
        acc = jnp.zeros((tm, D_MODEL), F32)
        for a, b in HID_PIECES:
            gate = _dot_nt(nb, wgu_ref[0, a:b, :])
            up = _dot_nt(nb, wgu_ref[1, a:b, :])
            gate_ref[:, a:b] = gate.astype(BF16)
            up_ref[:, a:b] = up.astype(BF16)
            acc = acc + _dot((gate * _sigmoid(gate) * up).astype(BF16), wd_ref[a:b, :])
        hout = hv + 0.5 * acc

        gf = gf_ref[...]
        loss = jnp.zeros((1, 1), F32)
        dgf = jnp.zeros((1, D_MODEL), F32)
        for k in range(ksub):
            yk, xhat, r = _rms(hout[128 * k:128 * (k + 1)], gf)
            err = jnp.where(_is_lead(i, k, ksub, nq), 0.0, yk - t_refs[k][...])
            loss = loss + 0.5 * jnp.sum(jnp.sum(err * err, axis=1, keepdims=True), axis=0,
                                        keepdims=True) * (1.0 / D_MODEL)
            dy = err * (1.0 / D_MODEL)
            dh_ref[128 * k:128 * (k + 1), :] = _rms_bwd(dy, xhat, r, gf)
            dgf = dgf + jnp.sum(dy * xhat, axis=0, keepdims=True)
        loss_ref[...] += loss
        dgf_ref[...] += dgf

    rows = pl.BlockSpec((tm, D_MODEL), lambda i: (i, 0))
    hid = pl.BlockSpec((tm, D_FF), lambda i: (i, 0))
    vec = pl.BlockSpec((1, D_MODEL), lambda i: (0, 0))
    return pl.pallas_call(
        body, name=name, grid=(n_i,),
        in_specs=[_token_spec(k, ksub, nq) for k in range(ksub)]
        + [rows, vec, _resident(W_GU_SHAPE, 1), _resident(W_D_SHAPE, 1), vec],
        out_specs=[rows, rows, hid, hid, pl.BlockSpec((1, 1), lambda i: (0, 0)), vec],
        out_shape=[jax.ShapeDtypeStruct((n, D_MODEL), F32), jax.ShapeDtypeStruct((n, D_MODEL), BF16),
                   jax.ShapeDtypeStruct((n, D_FF), BF16), jax.ShapeDtypeStruct((n, D_FF), BF16),
                   jax.ShapeDtypeStruct((1, 1), F32), jax.ShapeDtypeStruct((1, D_MODEL), F32)],
        compiler_params=_params(("arbitrary",)),
    )(*([target] * ksub), h, gain, wgu, wd, gfinal)


def _ffn_bwd_in_tokens(dh_out, x2d, meta, lp, gain, dgate, dup, wgu, name, exchange=()):
    n = dh_out.shape[0]
    nq = lp // 128
    tm = _row_tile(n, 512)
    ksub = tm // 128
    n_i = n // tm
    nw = len(exchange)

    def body(*refs):
        x_refs = refs[:ksub]
        (meta_ref, dh_ref, g_ref, dgate_ref, dup_ref, wgu_ref), xin, (dhin_ref, dgain_ref), xout, _, sems = \
            _split_refs(refs[ksub:], 6, nw, 2, 0)
        i = pl.program_id(0)
        if nw:
            comm = _ChipExchange(xin, xout, sems)
            pl.when(i == 0)(comm.start)

        @pl.when(i == 0)
        def _():
            dgain_ref[...] = jnp.zeros_like(dgain_ref)

        dn = jnp.zeros((tm, D_MODEL), F32)
        for a, b in HID_PIECES:
            dn = dn + _dot(dgate_ref[:, a:b], wgu_ref[0, a:b, :]) + _dot(dup_ref[:, a:b], wgu_ref[1, a:b, :])
        gain_v = g_ref[...]
        _, xhat, r = _rms(_assemble_rows(i, x_refs, meta_ref, nq), gain_v)
        dhin_ref[...] = dh_ref[...] + _rms_bwd(dn, xhat, r, gain_v)
        dgain_ref[...] += jnp.sum(dn * xhat, axis=0, keepdims=True)

        if nw:
            pl.when(i == n_i - 1)(comm.finish)

    rows = pl.BlockSpec((tm, D_MODEL), lambda i: (i, 0))
    hid = pl.BlockSpec((tm, D_FF), lambda i: (i, 0))
    vec = pl.BlockSpec((1, D_MODEL), lambda i: (0, 0))
    return pl.pallas_call(
        body, name=name, grid=(n_i,),
        in_specs=[_token_spec(k, ksub, nq) for k in range(ksub)]
        + [pl.BlockSpec((N_META, D_MODEL), lambda i: (0, 0)), rows, vec, hid, hid, _resident(W_GU_SHAPE, 1)]
        + [ANY] * nw,
        out_specs=[rows, vec] + [ANY] * nw,
        out_shape=[jax.ShapeDtypeStruct((n, D_MODEL), F32), jax.ShapeDtypeStruct((1, D_MODEL), F32)]
        + [jax.ShapeDtypeStruct(a.shape, a.dtype) for a in exchange],
        scratch_shapes=_comm_sems(nw) if nw else [],
        compiler_params=_params(("arbitrary",)),
    )(*([x2d] * ksub), meta, dh_out, gain, dgate, dup, wgu, *exchange)


def _ffn_bwd_act_wd(dh_out, gate, up, wd, name, exchange=()):
    n = dh_out.shape[0]
    tm = _row_tile(n, 256)
    n_i = n // tm
    nw = len(exchange)

    def body(*refs):
        (dh_ref, gate_ref, up_ref, wd_ref), xin, (dgate_ref, dup_ref, dw_ref), xout, (acc_scr,), sems = \
            _split_refs(refs, 4, nw, 3, 1)
        i = pl.program_id(0)
        if nw:
            comm = _Exchange(xin, xout, sems)
            pl.when(i == 0)(comm.start)

        @pl.when(i == 0)
        def _():
            acc_scr[...] = jnp.zeros_like(acc_scr)

        dhb = (0.5 * dh_ref[...]).astype(BF16)
        for a, b in HID_PIECES:
            da = _dot_nt(dhb, wd_ref[a:b, :])
            g = gate_ref[:, a:b].astype(F32)
            u = up_ref[:, a:b].astype(F32)
            sig = _sigmoid(g)
            silu = g * sig
            dgate_ref[:, a:b] = (da * u * (sig * (1.0 + g * (1.0 - sig)))).astype(BF16)
            dup_ref[:, a:b] = (da * silu).astype(BF16)
            acc_scr[a:b, :] += _dot_tn((silu * u).astype(BF16), dhb)

        @pl.when(i == n_i - 1)
        def _():
            dw_ref[...] = acc_scr[...].astype(BF16)

        if nw:
            pl.when(i == n_i - 1)(comm.finish)

    rows = pl.BlockSpec((tm, D_MODEL), lambda i: (i, 0))
    hid = pl.BlockSpec((tm, D_FF), lambda i: (i, 0))
    return pl.pallas_call(
        body, name=name, grid=(n_i,),
        in_specs=[rows, hid, hid, _resident(W_D_SHAPE, 1)] + [ANY] * nw,
        out_specs=[hid, hid, _resident(W_D_SHAPE, 1)] + [ANY] * nw,
        out_shape=[jax.ShapeDtypeStruct((n, D_FF), BF16)] * 2 + [jax.ShapeDtypeStruct(W_D_SHAPE, BF16)]
        + [jax.ShapeDtypeStruct(a.shape, a.dtype) for a in exchange],
        scratch_shapes=[pltpu.VMEM(W_D_SHAPE, F32)] + (_comm_sems(nw) if nw else []),
        compiler_params=_params(("arbitrary",)),
    )(dh_out, gate, up, wd, *exchange)


def _ffn_bwd_wgu(nrm, dgate, dup, name):
    n = nrm.shape[0]
    tm = _row_tile(n, 256)
    n_i = n // tm

    def body(nrm_ref, dgate_ref, dup_ref, dw_ref, acc_scr):
        i = pl.program_id(0)

        @pl.when(i == 0)
        def _():
            acc_scr[...] = jnp.zeros_like(acc_scr)

        nb = nrm_ref[...]
        for a, b in HID_PIECES:
            acc_scr[0, a:b, :] += _dot_tn(dgate_ref[:, a:b], nb)
            acc_scr[1, a:b, :] += _dot_tn(dup_ref[:, a:b], nb)

        @pl.when(i == n_i - 1)
        def _():
            dw_ref[...] = acc_scr[...].astype(BF16)

    hid = pl.BlockSpec((tm, D_FF), lambda i: (i, 0))
    return pl.pallas_call(
        body, name=name, grid=(n_i,),
        in_specs=[pl.BlockSpec((tm, D_MODEL), lambda i: (i, 0)), hid, hid],
        out_specs=_resident(W_GU_SHAPE, 1),
        out_shape=jax.ShapeDtypeStruct(W_GU_SHAPE, BF16),
        scratch_shapes=[pltpu.VMEM(W_GU_SHAPE, F32)],
        compiler_params=_params(("arbitrary",)),
    )(nrm, dgate, dup)


def _ffn_bwd_wd(dhb, gate, up, name):
    n = dhb.shape[0]
    tm = _row_tile(n, 512)
    n_i = n // tm

    def body(dhb_ref, gate_ref, up_ref, dw_ref, acc_scr):
        i = pl.program_id(0)

        @pl.when(i == 0)
        def _():
            acc_scr[...] = jnp.zeros_like(acc_scr)

        dhv = dhb_ref[...]
        for a, b in HID_PIECES:
            g = gate_ref[:, a:b].astype(F32)
            act = (g * _sigmoid(g) * up_ref[:, a:b].astype(F32)).astype(BF16)
            acc_scr[a:b, :] += _dot_tn(act, dhv)

        @pl.when(i == n_i - 1)
        def _():
            dw_ref[...] = acc_scr[...].astype(BF16)

    hid = pl.BlockSpec((tm, D_FF), lambda i: (i, 0))
    return pl.pallas_call(
        body, name=name, grid=(n_i,),
        in_specs=[pl.BlockSpec((tm, D_MODEL), lambda i: (i, 0)), hid, hid],
        out_specs=_resident(W_D_SHAPE, 1),
        out_shape=jax.ShapeDtypeStruct(W_D_SHAPE, BF16),
        scratch_shapes=[pltpu.VMEM(W_D_SHAPE, F32)],
        compiler_params=_params(("arbitrary",)),
    )(dhb, gate, up)


N_PIECE = IN_MAIN // 512


def _inproj_fwd(h, gain, w_in):
    n = h.shape[0]
    tm = _row_tile(n, 512)

    def body(h_ref, g_ref, w_ref, *outs):
        y, _, _ = _rms(h_ref[...], g_ref[...])
        nb = y.astype(BF16)
        for p in range(N_PIECE):
            outs[p][...] = _dot_nt(nb, w_ref[512 * p:512 * (p + 1), :]).astype(BF16)
        outs[N_PIECE][...] = _dot_nt(nb, w_ref[IN_MAIN:IN_PAD, :])

    piece = pl.BlockSpec((tm, 512), lambda i: (i, 0))
    return pl.pallas_call(
        body, name="inproj_fwd", grid=(n // tm,),
        in_specs=[pl.BlockSpec((tm, D_MODEL), lambda i: (i, 0)),
                  pl.BlockSpec((1, D_MODEL), lambda i: (0, 0)),
                  pl.BlockSpec((IN_PAD, D_MODEL), lambda i: (0, 0))],
        out_specs=[piece] * N_PIECE + [pl.BlockSpec((tm, 128), lambda i: (i, 0))],
        out_shape=[jax.ShapeDtypeStruct((n, 512), BF16)] * N_PIECE + [jax.ShapeDtypeStruct((n, 128), F32)],
        compiler_params=_params(("parallel",)),
    )(h, gain, w_in)


def _inproj_bwd(dpieces, dfg, dh_out, h_in, gain, w_in):
    n = h_in.shape[0]
    tm = _row_tile(n, 512)
    n_i = n // tm

    def body(*refs):
        dp_refs = refs[:N_PIECE]
        dfg_ref, dh_ref, h_ref, g_ref, w_ref, dhin_ref, dw_ref, dgain_ref, acc_scr = refs[N_PIECE:]
        i = pl.program_id(0)

        @pl.when(i == 0)
        def _():
            acc_scr[...] = jnp.zeros_like(acc_scr)
            dgain_ref[...] = jnp.zeros_like(dgain_ref)

        gain_v = g_ref[...]
        y, xhat, r = _rms(h_ref[...], gain_v)
        nb = y.astype(BF16)
        dn = jnp.zeros((tm, D_MODEL), F32)
        for p in range(N_PIECE + 1):
            lo, hi = (512 * p, 512 * (p + 1)) if p < N_PIECE else (IN_MAIN, IN_PAD)
            dp = (dp_refs[p][...] if p < N_PIECE else dfg_ref[...]).astype(BF16)
            dn = dn + _dot(dp, w_ref[lo:hi, :])
            acc_scr[lo:hi, :] += _dot_tn(dp, nb)
        dhin_ref[...] = dh_ref[...] + _rms_bwd(dn, xhat, r, gain_v)
        dgain_ref[...] += jnp.sum(dn * xhat, axis=0, keepdims=True)

        @pl.when(i == n_i - 1)
        def _():
            dw_ref[...] = acc_scr[...].astype(BF16)

    piece = pl.BlockSpec((tm, 512), lambda i: (i, 0))
    rows = pl.BlockSpec((tm, D_MODEL), lambda i: (i, 0))
    vec = pl.BlockSpec((1, D_MODEL), lambda i: (0, 0))
    wspec = pl.BlockSpec((IN_PAD, D_MODEL), lambda i: (0, 0))
    return pl.pallas_call(
        body, name="inproj_bwd", grid=(n_i,),
        in_specs=[piece] * N_PIECE + [pl.BlockSpec((tm, 128), lambda i: (i, 0)), rows, rows, vec, wspec],
        out_specs=[rows, wspec, vec],
        out_shape=[jax.ShapeDtypeStruct((n, D_MODEL), F32),
                   jax.ShapeDtypeStruct((IN_PAD, D_MODEL), BF16),
                   jax.ShapeDtypeStruct((1, D_MODEL), F32)],
        scratch_shapes=[pltpu.VMEM((IN_PAD, D_MODEL), F32)],
        compiler_params=_params(("arbitrary",)),
    )(*dpieces, dfg, dh_out, h_in, gain, w_in)


def _outproj_fwd(zc, za, w_out, h):
    n = h.shape[0]
    tm = _row_tile(n, 512)

    def body(zc_ref, za_ref, w_ref, h_ref, out_ref):
        out_ref[...] = (h_ref[...] + _dot(zc_ref[...], w_ref[0:CONV_DIM, :])
                        + _dot(za_ref[...], w_ref[CONV_DIM:, :]))

    half = pl.BlockSpec((tm, 512), lambda i: (i, 0))
    rows = pl.BlockSpec((tm, D_MODEL), lambda i: (i, 0))
    return pl.pallas_call(
        body, name="outproj_fwd", grid=(n // tm,),
        in_specs=[half, half, pl.BlockSpec((D_MODEL, D_MODEL), lambda i: (0, 0)), rows],
        out_specs=rows,
        out_shape=jax.ShapeDtypeStruct((n, D_MODEL), F32),
        compiler_params=_params(("parallel",)),
    )(zc, za, w_out, h)


def _outproj_bwd(dh, zc, za, w_out):
    n = dh.shape[0]
    tm = _row_tile(n, 512)
    n_i = n // tm

    def body(dh_ref, zc_ref, za_ref, w_ref, dzc_ref, dza_ref, dw_ref, acc_scr):
        i = pl.program_id(0)

        @pl.when(i == 0)
        def _():
            acc_scr[...] = jnp.zeros_like(acc_scr)

        dhb = dh_ref[...].astype(BF16)
        dzc_ref[...] = _dot_nt(dhb, w_ref[0:CONV_DIM, :]).astype(BF16)
        dza_ref[...] = _dot_nt(dhb, w_ref[CONV_DIM:, :]).astype(BF16)
        acc_scr[0:CONV_DIM, :] += _dot_tn(zc_ref[...], dhb)
        acc_scr[CONV_DIM:, :] += _dot_tn(za_ref[...], dhb)

        @pl.when(i == n_i - 1)
        def _():
            dw_ref[...] = acc_scr[...].astype(BF16)

    half = pl.BlockSpec((tm, 512), lambda i: (i, 0))
    wspec = pl.BlockSpec((D_MODEL, D_MODEL), lambda i: (0, 0))
    return pl.pallas_call(
        body, name="outproj_bwd", grid=(n_i,),
        in_specs=[pl.BlockSpec((tm, D_MODEL), lambda i: (i, 0)), half, half, wspec],
        out_specs=[half, half, wspec],
        out_shape=[jax.ShapeDtypeStruct((n, 512), BF16), jax.ShapeDtypeStruct((n, 512), BF16),
                   jax.ShapeDtypeStruct((D_MODEL, D_MODEL), BF16)],
        scratch_shapes=[pltpu.VMEM((D_MODEL, D_MODEL), F32)],
        compiler_params=_params(("arbitrary",)),
    )(dh, zc, za, w_out)


def _group_matrix():
    r = lax.broadcasted_iota(jnp.int32, (128, 128), 0) // HEAD_DIM
    c = lax.broadcasted_iota(jnp.int32, (128, 128), 1) // HEAD_DIM
    return jnp.where(r == c, 1.0 / HEAD_DIM, 0.0).astype(BF16)


def _group_mean(x, gmat):
    hi = x.astype(BF16)
    lo = (x - hi.astype(F32)).astype(BF16)
    return _dot(hi, gmat) + _dot(lo, gmat)


def _shift_rows(x, s):
    rows = x.shape[0]
    t = lax.broadcasted_iota(jnp.int32, x.shape, 0)
    rolled = pltpu.roll(x, s % rows, 0)
    keep = (t >= s) if s > 0 else (t < rows + s)
    return jnp.where(keep, rolled, 0.0)


def _conv_parts(bg_ref, cg_ref, hc_ref, w_ref):
    bg = bg_ref[...].astype(F32)
    cg = cg_ref[...].astype(F32)
    hc = hc_ref[...].astype(F32)
    u = cg * hc
    u1 = _shift_rows(u, 1)
    u2 = _shift_rows(u, 2)
    conv = w_ref[2:3, :] * u + w_ref[1:2, :] * u1 + w_ref[0:1, :] * u2
    return bg, cg, hc, u, u1, u2, conv


def _conv_fwd(bg, cg, hc, conv_w, gain, gmat, lp):
    n = bg.shape[0]
    nb = n // lp

    def body(bg_ref, cg_ref, hc_ref, w_ref, g_ref, gm_ref, z_ref):
        bgv, _, _, _, _, _, conv = _conv_parts(bg_ref, cg_ref, hc_ref, w_ref)
        yc = bgv * conv
        r = lax.rsqrt(_group_mean(yc * yc, gm_ref[...]) + EPS)
        z_ref[...] = (yc * r * g_ref[...]).astype(BF16)

    blk = pl.BlockSpec((lp, 128), lambda c, b: (b, c))
    return pl.pallas_call(
        body, name="conv_fwd", grid=(CONV_DIM // 128, nb),
        in_specs=[blk, blk, blk, pl.BlockSpec((3, 128), lambda c, b: (0, c)),
                  pl.BlockSpec((1, 128), lambda c, b: (0, c)), pl.BlockSpec((128, 128), lambda c, b: (0, 0))],
        out_specs=blk,
        out_shape=jax.ShapeDtypeStruct((n, CONV_DIM), BF16),
        compiler_params=_params(("parallel", "parallel")),
    )(bg, cg, hc, conv_w, gain, gmat)


def _conv_bwd(dz, bg, cg, hc, conv_w, gain, gmat, lp):
    n = bg.shape[0]
    nb = n // lp

    def body(dz_ref, bg_ref, cg_ref, hc_ref, w_ref, g_ref, gm_ref,
             dbg_ref, dcg_ref, dhc_ref, dw_ref, dgain_ref):
        b = pl.program_id(1)

        @pl.when(b == 0)
        def _():
            dw_ref[...] = jnp.zeros_like(dw_ref)
            dgain_ref[...] = jnp.zeros_like(dgain_ref)

        bgv, cgv, hcv, u, u1, u2, conv = _conv_parts(bg_ref, cg_ref, hc_ref, w_ref)
        gm = gm_ref[...]
        yc = bgv * conv
        r = lax.rsqrt(_group_mean(yc * yc, gm) + EPS)
        yhat = yc * r
        dzv = dz_ref[...].astype(F32)
        dyhat = dzv * g_ref[...]
        dgain_ref[...] += jnp.sum(dzv * yhat, axis=0, keepdims=True)
        dyc = r * (dyhat - yhat * _group_mean(dyhat * yhat, gm))
        dbg_ref[...] = (dyc * conv).astype(BF16)
        dconv = dyc * bgv
        du = (w_ref[2:3, :] * dconv + w_ref[1:2, :] * _shift_rows(dconv, -1)
              + w_ref[0:1, :] * _shift_rows(dconv, -2))
        dcg_ref[...] = (du * hcv).astype(BF16)
        dhc_ref[...] = (du * cgv).astype(BF16)
        dw_ref[0:1, :] += jnp.sum(dconv * u2, axis=0, keepdims=True)
        dw_ref[1:2, :] += jnp.sum(dconv * u1, axis=0, keepdims=True)
        dw_ref[2:3, :] += jnp.sum(dconv * u, axis=0, keepdims=True)

    blk = pl.BlockSpec((lp, 128), lambda c, b: (b, c))
    wspec = pl.BlockSpec((3, 128), lambda c, b: (0, c))
    gspec = pl.BlockSpec((1, 128), lambda c, b: (0, c))
    return pl.pallas_call(
        body, name="conv_bwd", grid=(CONV_DIM // 128, nb),
        in_specs=[blk, blk, blk, blk, wspec, gspec, pl.BlockSpec((128, 128), lambda c, b: (0, 0))],
        out_specs=[blk, blk, blk, wspec, gspec],
        out_shape=[jax.ShapeDtypeStruct((n, CONV_DIM), BF16)] * 3
        + [jax.ShapeDtypeStruct((3, CONV_DIM), F32), jax.ShapeDtypeStruct((1, CONV_DIM), F32)],
        compiler_params=_params(("parallel", "arbitrary")),
    )(dz, bg, cg, hc, conv_w, gain, gmat)


KEY_MASKED = 1e30
ONE_LANE = 24


def _scan_steps(rows):
    s, out = 1, []
    while s < rows:
        out.append(s)
        s *= 2
    return out


def _fgate_fwd(fg, b_f, lp):
    n = fg.shape[0]
    nb = n // lp

    def body(fg_ref, b_ref, ka_ref, qa_ref):
        x = fg_ref[...] + b_ref[...]
        logf = jnp.minimum(x, 0.0) - jnp.log(1.0 + jnp.exp(-jnp.abs(x)))
        t = lax.broadcasted_iota(jnp.int32, (lp, 128), 0)
        lane = lax.broadcasted_iota(jnp.int32, (lp, 128), 1)
        f = jnp.where((t >= PAD) & (lane < N_HEADS), logf, 0.0)
        for s in _scan_steps(lp):
            f = f + _shift_rows(f, s)
        hi = f.astype(BF16).astype(F32)
        rest = f - hi
        mid = rest.astype(BF16).astype(F32)
        lo = (rest - mid).astype(BF16).astype(F32)
        ones = jnp.where((lane >= ONE_LANE) & (lane < ONE_LANE + 3), 1.0, 0.0)
        hi_key = jnp.where((t < PAD) & (lane < N_HEADS), KEY_MASKED, hi)
        ka_ref[...] = (hi_key + pltpu.roll(mid, 8, 1) + pltpu.roll(lo, 16, 1) + ones).astype(BF16)
        for h in range(N_HEADS):
            minus = jnp.where((lane == h) | (lane == 8 + h) | (lane == 16 + h), -1.0, 0.0)
            terms = (jnp.where(lane == ONE_LANE, pltpu.roll(hi, ONE_LANE - h, 1), 0.0)
                     + jnp.where(lane == ONE_LANE + 1, pltpu.roll(mid, ONE_LANE + 1 - h, 1), 0.0)
                     + jnp.where(lane == ONE_LANE + 2, pltpu.roll(lo, ONE_LANE + 2 - h, 1), 0.0))
            qa_ref[:, 128 * h:128 * (h + 1)] = (minus + terms).astype(BF16)

    return pl.pallas_call(
        body, name="fgate_fwd", grid=(nb,),
        in_specs=[pl.BlockSpec((lp, 128), lambda b: (b, 0)), pl.BlockSpec((1, 128), lambda b: (0, 0))],
        out_specs=[pl.BlockSpec((lp, 128), lambda b: (b, 0)), pl.BlockSpec((lp, N_HEADS * 128), lambda b: (b, 0))],
        out_shape=[jax.ShapeDtypeStruct((n, 128), BF16), jax.ShapeDtypeStruct((n, N_HEADS * 128), BF16)],
        compiler_params=_params(("parallel",)),
    )(fg, b_f)


def _fgate_bwd(dka, dfr, fg, b_f, lp):
    n = fg.shape[0]
    nb = n // lp

    def body(dka_ref, dfr_ref, fg_ref, b_ref, dfg_ref, db_ref):
        b = pl.program_id(0)

        @pl.when(b == 0)
        def _():
            db_ref[...] = jnp.zeros_like(db_ref)

        wide = jnp.concatenate([dfr_ref[0], jnp.zeros((128 - N_HEADS, lp), F32)], axis=0)
        t = lax.broadcasted_iota(jnp.int32, (lp, 128), 0)
        lane = lax.broadcasted_iota(jnp.int32, (lp, 128), 1)
        d = jnp.where(lane < N_HEADS, dka_ref[...], 0.0) + wide.T
        for s in _scan_steps(lp):
            d = d + _shift_rows(d, -s)
        x = fg_ref[...] + b_ref[...]
        dx = jnp.where((t >= PAD) & (lane < N_HEADS), d * _sigmoid(-x), 0.0)
        dfg_ref[...] = dx
        db_ref[...] += jnp.sum(dx, axis=0, keepdims=True)

    return pl.pallas_call(
        body, name="fgate_bwd", grid=(nb,),
        in_specs=[pl.BlockSpec((lp, 128), lambda b: (b, 0)), pl.BlockSpec((1, N_HEADS, lp), lambda b: (b, 0, 0)),
                  pl.BlockSpec((lp, 128), lambda b: (b, 0)), pl.BlockSpec((1, 128), lambda b: (0, 0))],
        out_specs=[pl.BlockSpec((lp, 128), lambda b: (b, 0)), pl.BlockSpec((1, 128), lambda b: (0, 0))],
        out_shape=[jax.ShapeDtypeStruct((n, 128), F32), jax.ShapeDtypeStruct((1, 128), F32)],
        compiler_params=_params(("arbitrary",)),
    )(dka, dfr, fg, b_f)


def _head_masks():
    lane = lax.broadcasted_iota(jnp.int32, (1, 128), 1)
    return lane < HEAD_DIM


def _stack_heads(x2, first):
    zero = jnp.zeros_like(x2)
    return jnp.concatenate([jnp.where(first, x2, zero), jnp.where(first, zero, x2)], axis=0)


def _stack_heads_lanes(xt):
    r = lax.broadcasted_iota(jnp.int32, xt.shape, 0)
    zero = jnp.zeros_like(xt)
    return jnp.concatenate([jnp.where(r < HEAD_DIM, xt, zero), jnp.where(r < HEAD_DIM, zero, xt)], axis=1)


def _pair_cols(col0, col1, first):
    return jnp.where(first, col0, col1)


def _pair_rows(row0, row1):
    r = lax.broadcasted_iota(jnp.int32, (128, TQ), 0)
    return jnp.where(r < HEAD_DIM, row0, row1)


def _query_side(q_ref, qa_ref, p, first):
    q2 = q_ref[:, 128 * p:128 * (p + 1)] * 0.125
    zero = jnp.zeros_like(q2)
    top = jnp.concatenate([jnp.where(first, q2, zero), qa_ref[:, 128 * (2 * p):128 * (2 * p + 1)]], axis=1)
    bot = jnp.concatenate([jnp.where(first, zero, q2), qa_ref[:, 128 * (2 * p + 1):128 * (2 * p + 2)]], axis=1)
    return jnp.concatenate([top, bot], axis=0)


def _key_chunks(lp):
    return (lp + TK - 1) // TK


def _chunk_mask(i, c, tk):
    r = lax.broadcasted_iota(jnp.int32, (tk, 2 * TQ), 0)
    col = lax.broadcasted_iota(jnp.int32, (tk, 2 * TQ), 1)
    return (c * TK + r) <= (i * TQ + (col & (TQ - 1)))


def _causal_sweep(i, step, init):
    per = TK // TQ
    last = i // per
    carry = lax.fori_loop(0, last, lambda c, carry: step(c, carry, False, TK), init)
    return lax.cond((i % per) < per // 2,
                    lambda carry: step(last, carry, True, TK // 2),
                    lambda carry: step(last, carry, True, TK), carry)


def _transpose_bf16(x):
    return x.astype(F32).T.astype(BF16)


def _attn_fwd(q, qa, k, v, ka, gain, lp):
    n = q.shape[0]
    nb = n // lp
    nq = lp // TQ
    lpp = _key_chunks(lp) * TK

    def body(q_ref, qa_ref, k_ref, v_ref, ka_ref, g_ref, z_ref, o_ref, lse_ref, kx_scr, vt_scr):
        i = pl.program_id(1)
        first = _head_masks()

        @pl.when(i == 0)
        def _():
            if lpp > lp:
                kx_scr[lp:lpp, :] = jnp.zeros((lpp - lp, 2 * ATTN_DIM), BF16)
                vt_scr[:, lp:lpp] = jnp.zeros((ATTN_DIM, lpp - lp), BF16)
            for p in range(N_PAIRS):
                kx_scr[0:lp, 256 * p:256 * p + 128] = k_ref[:, 128 * p:128 * (p + 1)]
                kx_scr[0:lp, 256 * p + 128:256 * (p + 1)] = ka_ref[...]
            vt_scr[:, 0:lp] = _transpose_bf16(v_ref[...])

        rhs_t = [_transpose_bf16(_query_side(q_ref, qa_ref, p, first)) for p in range(N_PAIRS)]

        def step(c, carry, masked, tk):
            koff = pl.multiple_of(c * TK, TK)
            valid = _chunk_mask(i, c, tk) if masked else None
            new = []
            for p in range(N_PAIRS):
                m, l, acc = carry[p]
                st = _dot(kx_scr[pl.ds(koff, tk), 256 * p:256 * (p + 1)], rhs_t[p])
                if masked:
                    st = jnp.where(valid, st, NEG)
                m_new = jnp.maximum(m, jnp.max(st, axis=0, keepdims=True))
                pt = jnp.exp(st - m_new)
                alpha = jnp.exp(m - m_new)
                l = alpha * l + jnp.sum(pt, axis=0, keepdims=True)
                pb = pt.astype(BF16)
                vt = _stack_heads_lanes(vt_scr[128 * p:128 * (p + 1), pl.ds(koff, tk)])
                pv = _dot(vt, jnp.concatenate([pb[:, 0:TQ], pb[:, TQ:]], axis=0))
                acc = acc * _pair_rows(alpha[:, 0:TQ], alpha[:, TQ:]) + pv
                new.append((m_new, l, acc))
            return tuple(new)

        init = tuple((jnp.full((1, 2 * TQ), NEG, F32), jnp.zeros((1, 2 * TQ), F32), jnp.zeros((128, TQ), F32))
                     for _ in range(N_PAIRS))
        final = _causal_sweep(i, step, init)

        row = lax.broadcasted_iota(jnp.int32, (TQ, 128), 0)
        real = (i * TQ + row) >= PAD
        for p in range(N_PAIRS):
            m, l, acc = final[p]
            inv = 1.0 / l
            ot = acc * _pair_rows(inv[:, 0:TQ], inv[:, TQ:])
            sq = ot * ot
            r0 = lax.rsqrt(jnp.sum(sq[0:HEAD_DIM], axis=0, keepdims=True) * (1.0 / HEAD_DIM) + EPS)
            r1 = lax.rsqrt(jnp.sum(sq[HEAD_DIM:], axis=0, keepdims=True) * (1.0 / HEAD_DIM) + EPS)
            cols = slice(128 * p, 128 * (p + 1))
            o_ref[:, cols] = jnp.where(real, ot.T, 0.0).astype(BF16)
            z_ref[:, cols] = (jnp.where(real, (ot * _pair_rows(r0, r1)).T, 0.0) * g_ref[:, cols]).astype(BF16)
            lse = m + jnp.log(l)
            lse_ref[0, 2 * p:2 * p + 1, :] = lse[:, 0:TQ]
            lse_ref[0, 2 * p + 1:2 * p + 2, :] = lse[:, TQ:]

    qblk = pl.BlockSpec((TQ, ATTN_DIM), lambda b, i: (b * nq + i, 0))
    qablk = pl.BlockSpec((TQ, N_HEADS * 128), lambda b, i: (b * nq + i, 0))
    seq = pl.BlockSpec((lp, ATTN_DIM), lambda b, i: (b, 0))
    rowblk = pl.BlockSpec((1, N_HEADS, TQ), lambda b, i: (b, 0, i))
    return pl.pallas_call(
        body, name="attn_fwd", grid=(nb, nq),
        in_specs=[qblk, qablk, seq, seq, pl.BlockSpec((lp, 128), lambda b, i: (b, 0)),
                  pl.BlockSpec((1, ATTN_DIM), lambda b, i: (0, 0))],
        out_specs=[qblk, qblk, rowblk],
        out_shape=[jax.ShapeDtypeStruct((n, ATTN_DIM), BF16), jax.ShapeDtypeStruct((n, ATTN_DIM), BF16),
                   jax.ShapeDtypeStruct((nb, N_HEADS, lp), F32)],
        scratch_shapes=[pltpu.VMEM((lpp, 2 * ATTN_DIM), BF16), pltpu.VMEM((ATTN_DIM, lpp), BF16)],
        compiler_params=_params(("parallel", "arbitrary")),
    )(q, qa, k, v, ka, gain)


def _attn_bwd(dz, q, qa, k, v, ka, o, lse, gain, lp, exchange=()):
    n = q.shape[0]
    nb = n // lp
    nq = lp // TQ
    lpp = _key_chunks(lp) * TK
    nw = len(exchange)

    def body(*refs):
        ((dz_ref, q_ref, qa_ref, k_ref, v_ref, ka_ref, o_ref, lse_ref, g_ref), xin,
         (dq_ref, dk_ref, dv_ref, dka_ref, dfr_ref, dgain_ref), xout,
         (kx_scr, vx_scr, kt_scr, dkx_scr, dvx_scr), sems) = _split_refs(refs, 9, nw, 6, 5)
        b = pl.program_id(0)
        i = pl.program_id(1)
        first = _head_masks()
        if nw:
            comm = _Exchange(xin, xout, sems)
            pl.when((b == 0) & (i == 0))(comm.start)

        @pl.when((b == 0) & (i == 0))
        def _():
            dgain_ref[...] = jnp.zeros_like(dgain_ref)

        @pl.when(i == 0)
        def _():
            if lpp > lp:
                kx_scr[lp:lpp, :] = jnp.zeros((lpp - lp, 2 * ATTN_DIM), BF16)
                vx_scr[lp:lpp, :] = jnp.zeros((lpp - lp, ATTN_DIM), BF16)
                kt_scr[:, lp:lpp] = jnp.zeros((ATTN_DIM, lpp - lp), BF16)
            for p in range(N_PAIRS):
                kx_scr[0:lp, 256 * p:256 * p + 128] = k_ref[:, 128 * p:128 * (p + 1)]
                kx_scr[0:lp, 256 * p + 128:256 * (p + 1)] = ka_ref[...]
            vx_scr[0:lp, :] = v_ref[...]
            kt_scr[:, 0:lp] = _transpose_bf16(k_ref[...])
            dkx_scr[...] = jnp.zeros_like(dkx_scr)
            dvx_scr[...] = jnp.zeros_like(dvx_scr)

        rhs, rhs_t, lses, dos, dos_t, deltas = [], [], [], [], [], []
        for p in range(N_PAIRS):
            cols = slice(128 * p, 128 * (p + 1))
            side = _query_side(q_ref, qa_ref, p, first)
            rhs.append(side)
            rhs_t.append(_transpose_bf16(side))
            lses.append(jnp.concatenate([lse_ref[0, 2 * p:2 * p + 1, :], lse_ref[0, 2 * p + 1:2 * p + 2, :]], axis=1))
            ov = o_ref[:, cols].astype(F32)
            dzv = dz_ref[:, cols].astype(F32)
            gv = g_ref[:, cols]
            sq = ov * ov
            ms0 = jnp.sum(jnp.where(first, sq, 0.0), axis=1, keepdims=True) * (1.0 / HEAD_DIM)
            ms1 = jnp.sum(jnp.where(first, 0.0, sq), axis=1, keepdims=True) * (1.0 / HEAD_DIM)
            r = _pair_cols(lax.rsqrt(ms0 + EPS), lax.rsqrt(ms1 + EPS), first)
            ohat = ov * r
            dyhat = dzv * gv
            dgain_ref[:, cols] += jnp.sum(dzv * ohat, axis=0, keepdims=True)
            pr = dyhat * ohat
            mean0 = jnp.sum(jnp.where(first, pr, 0.0), axis=1, keepdims=True) * (1.0 / HEAD_DIM)
            mean1 = jnp.sum(jnp.where(first, 0.0, pr), axis=1, keepdims=True) * (1.0 / HEAD_DIM)
            do = r * (dyhat - ohat * _pair_cols(mean0, mean1, first))
            ddt = (do * ov).T
            deltas.append(jnp.concatenate([jnp.sum(ddt[0:HEAD_DIM], axis=0, keepdims=True),
                                           jnp.sum(ddt[HEAD_DIM:], axis=0, keepdims=True)], axis=1))
            do_st = _stack_heads(do.astype(BF16), first)
            dos.append(do_st)
            dos_t.append(_transpose_bf16(do_st))

        def step(c, carry, masked, tk):
            koff = pl.multiple_of(c * TK, TK)
            valid = _chunk_mask(i, c, tk) if masked else None
            new = []
            for p in range(N_PAIRS):
                dqt, dfq = carry[p]
                ext = slice(256 * p, 256 * (p + 1))
                cols = slice(128 * p, 128 * (p + 1))
                st = _dot(kx_scr[pl.ds(koff, tk), ext], rhs_t[p])
                if masked:
                    st = jnp.where(valid, st, NEG)
                pt = jnp.exp(st - lses[p])
                dpt = _dot(vx_scr[pl.ds(koff, tk), cols], dos_t[p])
                dst = pt * (dpt - deltas[p])
                dsb = dst.astype(BF16)
                dfq = dfq + jnp.sum(dsb.astype(F32), axis=0, keepdims=True)
                dkx_scr[pl.ds(koff, tk), ext] += _dot(dsb, rhs[p])
                dvx_scr[pl.ds(koff, tk), cols] += _dot(pt.astype(BF16), dos[p])
                kt = _stack_heads_lanes(kt_scr[cols, pl.ds(koff, tk)])
                dqt = dqt + _dot(kt, jnp.concatenate([dsb[:, 0:TQ], dsb[:, TQ:]], axis=0))
                new.append((dqt, dfq))
            return tuple(new)

        init = tuple((jnp.zeros((128, TQ), F32), jnp.zeros((1, 2 * TQ), F32)) for _ in range(N_PAIRS))
        final = _causal_sweep(i, step, init)

        for p in range(N_PAIRS):
            dqt, dfq = final[p]
            dq_ref[:, 128 * p:128 * (p + 1)] = (dqt.T * 0.125).astype(BF16)
            dfr_ref[0, 2 * p:2 * p + 1, :] = dfq[:, 0:TQ]
            dfr_ref[0, 2 * p + 1:2 * p + 2, :] = dfq[:, TQ:]

        @pl.when(i == nq - 1)
        def _():
            dka = jnp.zeros((lp, 128), F32)
            for p in range(N_PAIRS):
                dk_ref[:, 128 * p:128 * (p + 1)] = dkx_scr[0:lp, 256 * p:256 * p + 128].astype(BF16)
                dka = dka + dkx_scr[0:lp, 256 * p + 128:256 * (p + 1)]
            dka_ref[...] = dka
            dv_ref[...] = dvx_scr[0:lp, :].astype(BF16)

        if nw:
            pl.when((b == nb - 1) & (i == nq - 1))(comm.finish)

    qblk = pl.BlockSpec((TQ, ATTN_DIM), lambda b, i: (b * nq + i, 0))
    qablk = pl.BlockSpec((TQ, N_HEADS * 128), lambda b, i: (b * nq + i, 0))
    seq = pl.BlockSpec((lp, ATTN_DIM), lambda b, i: (b, 0))
    kaseq = pl.BlockSpec((lp, 128), lambda b, i: (b, 0))
    rowblk = pl.BlockSpec((1, N_HEADS, TQ), lambda b, i: (b, 0, i))
    gspec = pl.BlockSpec((1, ATTN_DIM), lambda b, i: (0, 0))
    return pl.pallas_call(
        body, name="attn_bwd", grid=(nb, nq),
        in_specs=[qblk, qblk, qablk, seq, seq, kaseq, qblk, rowblk, gspec] + [ANY] * nw,
        out_specs=[qblk, seq, seq, kaseq, rowblk, gspec] + [ANY] * nw,
        out_shape=[jax.ShapeDtypeStruct((n, ATTN_DIM), BF16), jax.ShapeDtypeStruct((n, ATTN_DIM), BF16),
                   jax.ShapeDtypeStruct((n, ATTN_DIM), BF16), jax.ShapeDtypeStruct((n, 128), F32),
                   jax.ShapeDtypeStruct((nb, N_HEADS, lp), F32), jax.ShapeDtypeStruct((1, ATTN_DIM), F32)]
        + [jax.ShapeDtypeStruct(a.shape, a.dtype) for a in exchange],
        scratch_shapes=[pltpu.VMEM((lpp, 2 * ATTN_DIM), BF16), pltpu.VMEM((lpp, ATTN_DIM), BF16),
                        pltpu.VMEM((ATTN_DIM, lpp), BF16), pltpu.VMEM((lpp, 2 * ATTN_DIM), F32),
                        pltpu.VMEM((lpp, ATTN_DIM), F32)] + (_comm_sems(nw) if nw else []),
        compiler_params=_params(("arbitrary", "arbitrary")),
    )(dz, q, qa, k, v, ka, o, lse, gain, *exchange)


def _loss_head(h, gain, target, lp):
    n = h.shape[0]
    nb = n // lp
    nq = lp // 128

    def body(h_ref, g_ref, t_ref, loss_ref, dh_ref, dgain_ref):
        b = pl.program_id(0)
        i = pl.program_id(1)

        @pl.when((b == 0) & (i == 0))
        def _():
            loss_ref[...] = jnp.zeros_like(loss_ref)
            dgain_ref[...] = jnp.zeros_like(dgain_ref)

        @pl.when(i == 0)
        def _():
            dh_ref[...] = jnp.zeros_like(dh_ref)

        @pl.when(i > 0)
        def _():
            gain_v = g_ref[...]
            y, xhat, r = _rms(h_ref[...], gain_v)
            err = y - t_ref[...]
            loss_ref[...] += 0.5 * jnp.sum(jnp.sum(err * err, axis=1, keepdims=True), axis=0,
                                           keepdims=True) * (1.0 / D_MODEL)
            dy = err * (1.0 / D_MODEL)
            dh_ref[...] = _rms_bwd(dy, xhat, r, gain_v)
            dgain_ref[...] += jnp.sum(dy * xhat, axis=0, keepdims=True)

    rows = pl.BlockSpec((128, D_MODEL), lambda b, i: (b * nq + i, 0))
    trows = pl.BlockSpec((128, D_MODEL), lambda b, i: (b * (nq - 1) + jnp.maximum(i, 1) - 1, 0))
    return pl.pallas_call(
        body, name="loss_head", grid=(nb, nq),
        in_specs=[rows, pl.BlockSpec((1, D_MODEL), lambda b, i: (0, 0)), trows],
        out_specs=[pl.BlockSpec((1, 1), lambda b, i: (0, 0)), rows, pl.BlockSpec((1, D_MODEL), lambda b, i: (0, 0))],
        out_shape=[jax.ShapeDtypeStruct((1, 1), F32), jax.ShapeDtypeStruct((n, D_MODEL), F32),
                   jax.ShapeDtypeStruct((1, D_MODEL), F32)],
        compiler_params=_params(("arbitrary", "arbitrary")),
    )(h, gain, target)


def _adamw(parts, w, m, v, name):
    s_parts, r, c = parts.shape
    tr = r
    for t in (256, 128, 64, 32, 16):
        if r % t == 0 and r > t:
            tr = t
            break

    def body(p_ref, w_ref, m_ref, v_ref, g_ref, d_ref, nm_ref, nv_ref):
        g = p_ref[0].astype(F32)
        for s in range(1, s_parts):
            g = g + p_ref[s].astype(F32)
        nm = ADAM_B1 * m_ref[...] + (1.0 - ADAM_B1) * g
        nv = ADAM_B2 * v_ref[...] + (1.0 - ADAM_B2) * (g * g)
        m_hat = nm / (1.0 - ADAM_B1 ** ADAM_STEP)
        v_hat = nv / (1.0 - ADAM_B2 ** ADAM_STEP)
        g_ref[...] = g
        d_ref[...] = -ADAM_LR * (m_hat / (jnp.sqrt(v_hat) + ADAM_EPS) + ADAM_WD * w_ref[...])
        nm_ref[...] = nm
        nv_ref[...] = nv

    blk = pl.BlockSpec((tr, c), lambda i: (i, 0))
    return pl.pallas_call(
        body, name=name, grid=(r // tr,),
        in_specs=[pl.BlockSpec((s_parts, tr, c), lambda i: (0, i, 0)), blk, blk, blk],
        out_specs=[blk] * 4,
        out_shape=[jax.ShapeDtypeStruct((r, c), F32)] * 4,
        compiler_params=_params(("parallel",)),
    )(parts, w, m, v)


def _sum_parts(parts, name):
    s_parts, r, c = parts.shape

    def body(p_ref, out_ref):
        acc = p_ref[0]
        for s in range(1, s_parts):
            acc = acc + p_ref[s]
        out_ref[...] = acc

    return pl.pallas_call(
        body, name=name, out_shape=jax.ShapeDtypeStruct((r, c), F32),
        in_specs=[pl.BlockSpec(memory_space=pltpu.VMEM)], out_specs=pl.BlockSpec(memory_space=pltpu.VMEM),
    )(parts)


SMALL_ROWS = 184


def _pack_small(d_gains, d_gc, d_ga, d_bf, d_conv, d_meta):
    rows = [g.reshape(8, 128) for g in d_gains]
    rows += [d_gc.reshape(4, 128), d_ga.reshape(4, 128), d_bf.reshape(1, 128)]
    rows += [d_conv.reshape(12, 128), d_meta.reshape(128, 128)]
    packed = jnp.concatenate(rows, axis=0)
    return jnp.pad(packed, ((0, SMALL_ROWS - packed.shape[0]), (0, 0)))


def kernel(x, meta_tokens, ffn1_norm, ffn1_w_gu, ffn1_w_down, mix_norm, w_in, conv_w, b_f, out_norm_conv, out_norm_attn, w_out, ffn2_norm, ffn2_w_gu, ffn2_w_down, final_norm, loss_target, m_meta_tokens, m_ffn1_norm, m_ffn1_w_gu, m_ffn1_w_down, m_mix_norm, m_w_in, m_conv_w, m_b_f, m_out_norm_conv, m_out_norm_attn, m_w_out, m_ffn2_norm, m_ffn2_w_gu, m_ffn2_w_down, m_final_norm, v_meta_tokens, v_ffn1_norm, v_ffn1_w_gu, v_ffn1_w_down, v_mix_norm, v_w_in, v_conv_w, v_b_f, v_out_norm_conv, v_out_norm_attn, v_w_out, v_ffn2_norm, v_ffn2_w_gu, v_ffn2_w_down, v_final_norm):
    nb, seq, _ = x.shape
    lp = PAD + N_META + seq
    n = nb * lp
    me = 4 * lax.axis_index("x") + 2 * lax.axis_index("y") + lax.axis_index("c")

    wgu1_8, wd1_8 = _all_gather([ffn1_w_gu[0].T.astype(BF16), ffn1_w_down[0].astype(BF16)], "gather_ffn1")
    small_in = jnp.concatenate(
        [meta_tokens, jnp.pad(conv_w[0], ((0, 0), (0, 128 - conv_w.shape[2]))), jnp.zeros((5, 128), F32)], axis=0)
    (small_8,) = _all_gather([small_in], "gather_small")
    meta_full = small_8[:, 0:N_META, :].transpose(1, 0, 2).reshape(N_META, D_MODEL)
    conv_full = small_8[:, N_META:N_META + 3, 0:CONV_DIM // N_DEV].transpose(1, 0, 2).reshape(3, CONV_DIM)
    wgu1 = wgu1_8.reshape(W_GU_SHAPE)
    wd1 = wd1_8.reshape(W_D_SHAPE)
    b_f_row = jnp.pad(b_f, ((0, 0), (0, 128 - N_HEADS)))
    gmat = _group_matrix()

    x2d = x.reshape(nb * seq, D_MODEL)
    later = [w_in[0].T.astype(BF16), w_out[0].astype(BF16), ffn2_w_gu[0].T.astype(BF16), ffn2_w_down[0].astype(BF16)]
    h1, n1, gate1, up1, win_8, wout_8, wgu2_8, wd2_8 = _ffn_fwd_tokens(
        x2d, meta_full, lp, ffn1_norm, wgu1, wd1, "ffn1_fwd", gather=later)
    wgu2 = wgu2_8.reshape(W_GU_SHAPE)
    wd2 = wd2_8.reshape(W_D_SHAPE)
    w_in_full = jnp.pad(win_8.reshape(IN_DIM, D_MODEL), ((0, IN_PAD - IN_DIM), (0, 0)))
    w_out_full = wout_8.reshape(D_MODEL, D_MODEL)

    bg, cg, hc, q, k, v, fg = _inproj_fwd(h1, mix_norm, w_in_full)
    zc = _conv_fwd(bg, cg, hc, conv_full, out_norm_conv, gmat, lp)
    ka, qa = _fgate_fwd(fg, b_f_row, lp)
    za, o, lse = _attn_fwd(q, qa, k, v, ka, out_norm_attn, lp)
    h2 = _outproj_fwd(zc, za, w_out_full, h1)
    dh3, n3, gate2, up2, loss_part, d_final = _ffn_fwd_loss(
        h2, ffn2_norm, wgu2, wd2, final_norm.reshape(1, D_MODEL), loss_target.reshape(nb * seq, D_MODEL), lp,
        "ffn2_fwd_loss")

    dgate2, dup2, dwd2 = _ffn_bwd_act_wd(dh3, gate2, up2, wd2, "ffn2_bwd_act")
    dh2, d_ffn2 = _ffn_bwd_in(dh3, h2, ffn2_norm, dgate2, dup2, wgu2, "ffn2_bwd_in")
    dwgu2 = _ffn_bwd_wgu(n3, dgate2, dup2, "ffn2_bwd_wgu")
    dzc, dza, dwout = _outproj_bwd(dh2, zc, za, w_out_full)
    send_a = [dwgu2.reshape(N_DEV, F_CHUNK, D_MODEL), dwd2.reshape(N_DEV, F_CHUNK // 2, D_MODEL),
              dwout.reshape(N_DEV, D_MODEL // N_DEV, D_MODEL)]
    dq, dk, dv, dka, dfr, d_ga, p_wgu2, p_wd2, p_wout = _attn_bwd(
        dza, q, qa, k, v, ka, o, lse, out_norm_attn, lp, exchange=send_a)
    dfg, d_bf = _fgate_bwd(dka, dfr, fg, b_f_row, lp)
    dbg, dcg, dhc, d_conv, d_gc = _conv_bwd(dzc, bg, cg, hc, conv_full, out_norm_conv, gmat, lp)
    dh1, dwin, d_mix = _inproj_bwd([dbg, dcg, dhc, dq, dk, dv], dfg, dh2, h1, mix_norm, w_in_full)
    dwin_8 = dwin[0:IN_DIM].reshape(N_DEV, IN_DIM // N_DEV, D_MODEL)
    dgate1, dup1, dwd1, p_win = _ffn_bwd_act_wd(dh1, gate1, up1, wd1, "ffn1_bwd_act", exchange=[dwin_8])
    dwgu1 = _ffn_bwd_wgu(n1, dgate1, dup1, "ffn1_bwd_wgu")
    own = [dwgu1.reshape(N_DEV, F_CHUNK, D_MODEL), dwd1.reshape(N_DEV, F_CHUNK // 2, D_MODEL)]
    got = _pair_exchange(own, "pair_exchange_ffn1")
    chip_sums = [_pair_sum(own[0], got[0], "pair_sum_wgu1"), _pair_sum(own[1], got[1], "pair_sum_wd1")]
    dh0, d_ffn1, p_wgu1, p_wd1 = _ffn_bwd_in_tokens(
        dh1, x2d, meta_full, lp, ffn1_norm, dgate1, dup1, wgu1, "ffn1_bwd_in", exchange=chip_sums)

    dh0 = dh0.reshape(nb, lp, D_MODEL)
    grad_x = dh0[:, PAD + N_META:, :]
    d_meta = jnp.sum(dh0[:, PAD:PAD + N_META, :], axis=0)

    small = _pack_small([d_ffn1, d_mix, d_ffn2, d_final], d_gc, d_ga, d_bf, d_conv, d_meta)
    (small_all,) = _all_gather([small], "gather_small_grads")
    small_sum = _sum_parts(small_all, "sum_small_grads")
    g_ffn1n, g_mixn, g_ffn2n, g_finaln = (small_sum[8 * t:8 * t + 8].reshape(1, D_MODEL) for t in range(4))
    g_gc = small_sum[32:36].reshape(1, CONV_DIM)
    g_ga = small_sum[36:40].reshape(1, ATTN_DIM)
    g_bf = small_sum[40:41, 0:N_HEADS]
    g_conv_full = small_sum[41:53].reshape(3, CONV_DIM)
    g_meta_full = small_sum[53:181].reshape(N_META, D_MODEL)
    g_conv = lax.dynamic_slice_in_dim(g_conv_full, me * (CONV_DIM // N_DEV), CONV_DIM // N_DEV, axis=1)
    g_meta = lax.dynamic_slice_in_dim(g_meta_full, me * (D_MODEL // N_DEV), D_MODEL // N_DEV, axis=1)

    weights = {
        "meta_tokens": (g_meta[None], meta_tokens, m_meta_tokens, v_meta_tokens),
        "ffn1_norm": (g_ffn1n[None], ffn1_norm, m_ffn1_norm, v_ffn1_norm),
        "ffn1_w_gu": (p_wgu1, ffn1_w_gu[0].T, m_ffn1_w_gu[0].T, v_ffn1_w_gu[0].T),
        "ffn1_w_down": (p_wd1, ffn1_w_down[0], m_ffn1_w_down[0], v_ffn1_w_down[0]),
        "mix_norm": (g_mixn[None], mix_norm, m_mix_norm, v_mix_norm),
        "w_in": (p_win, w_in[0].T, m_w_in[0].T, v_w_in[0].T),
        "conv_w": (g_conv[None], conv_w[0], m_conv_w[0], v_conv_w[0]),
        "b_f": (g_bf[None], b_f, m_b_f, v_b_f),
        "out_norm_conv": (g_gc[None], out_norm_conv, m_out_norm_conv, v_out_norm_conv),
        "out_norm_attn": (g_ga[None], out_norm_attn, m_out_norm_attn, v_out_norm_attn),
        "w_out": (p_wout, w_out[0], m_w_out[0], v_w_out[0]),
        "ffn2_norm": (g_ffn2n[None], ffn2_norm, m_ffn2_norm, v_ffn2_norm),
        "ffn2_w_gu": (p_wgu2, ffn2_w_gu[0].T, m_ffn2_w_gu[0].T, v_ffn2_w_gu[0].T),
        "ffn2_w_down": (p_wd2, ffn2_w_down[0], m_ffn2_w_down[0], v_ffn2_w_down[0]),
        "final_norm": (g_finaln[None], final_norm.reshape(1, D_MODEL), m_final_norm.reshape(1, D_MODEL),
                       v_final_norm.reshape(1, D_MODEL)),
    }
    shapes = {"meta_tokens": meta_tokens.shape, "ffn1_norm": ffn1_norm.shape, "ffn1_w_gu": ffn1_w_gu.shape,
              "ffn1_w_down": ffn1_w_down.shape, "mix_norm": mix_norm.shape, "w_in": w_in.shape,
              "conv_w": conv_w.shape, "b_f": b_f.shape, "out_norm_conv": out_norm_conv.shape,
              "out_norm_attn": out_norm_attn.shape, "w_out": w_out.shape, "ffn2_norm": ffn2_norm.shape,
              "ffn2_w_gu": ffn2_w_gu.shape, "ffn2_w_down": ffn2_w_down.shape, "final_norm": final_norm.shape}
    grads, deltas, new_m, new_v = [], [], [], []
    for name, (p, w, m, vv) in weights.items():
        g, d, nm, nv = _adamw(p, w, m, vv, "adamw_" + name)
        if name in ("ffn1_w_gu", "ffn2_w_gu", "w_in"):
            g, d, nm, nv = g.T, d.T, nm.T, nv.T
        shape = shapes[name]
        grads.append(g.reshape(shape))
        deltas.append(d.reshape(shape))
        new_m.append(nm.reshape(shape))
        new_v.append(nv.reshape(shape))

    loss = lax.psum(loss_part[0, 0], ("x", "y", "c"))
    return (loss, grad_x, *grads, *deltas, *new_m, *new_v)
```

```python
import jax
import jax.numpy as jnp
from jax import lax
from jax.experimental import pallas as pl
from jax.experimental.pallas import tpu as pltpu

F32 = jnp.float32
BF16 = jnp.bfloat16

N_DEV = 8
D_MODEL = 1024
N_META = 16
PAD = 128 - N_META
CONV_DIM = 512
ATTN_DIM = 512
HEAD_DIM = 64
N_HEADS = 8
N_PAIRS = N_HEADS // 2
D_FF = 2816
N_CHUNK = 4
F_CHUNK = D_FF // N_CHUNK
IN_DIM = 3080
IN_PAD = 3200
IN_MAIN = 3072
EPS = 1e-6
NEG = -1e30
TQ = 128
TK = 512
VMEM_LIMIT = 56 * 1024 * 1024

ADAM_LR = 0.001
ADAM_B1 = 0.9
ADAM_B2 = 0.999
ADAM_EPS = 1e-08
ADAM_WD = 0.01
ADAM_STEP = 10

MESH = pl.DeviceIdType.MESH
ANY = pl.BlockSpec(memory_space=pl.ANY)


def _params(sem=None):
    return pltpu.CompilerParams(dimension_semantics=sem, vmem_limit_bytes=VMEM_LIMIT)


def _row_tile(n, prefer):
    for t in (prefer, 512, 256, 128):
        if t <= n and n % t == 0:
            return t
    raise ValueError(f"no row tile for {n}")


def _dot(a, b):
    return jnp.dot(a, b, preferred_element_type=F32)


def _dot_nt(a, b):
    return lax.dot_general(a, b, (((1,), (1,)), ((), ())), preferred_element_type=F32)


def _dot_tn(a, b):
    return lax.dot_general(a, b, (((0,), (0,)), ((), ())), preferred_element_type=F32)


def _rms(x, g):
    r = lax.rsqrt(jnp.mean(x * x, axis=-1, keepdims=True) + EPS)
    xhat = x * r
    return xhat * g, xhat, r


def _rms_bwd(dn, xhat, r, g):
    dxhat = dn * g
    return r * (dxhat - xhat * jnp.mean(dxhat * xhat, axis=-1, keepdims=True))


def _sigmoid(x):
    return 1.0 / (1.0 + jnp.exp(-x))


def _place():
    return lax.axis_index("x"), lax.axis_index("y"), lax.axis_index("c")


def _comm_sems(nw):
    return [pltpu.SemaphoreType.DMA((nw, 7)), pltpu.SemaphoreType.DMA((nw, 7)), pltpu.SemaphoreType.DMA((nw,))]


class _Gather:
    def __init__(self, ins, outs, sems):
        self.ins, self.outs = ins, outs
        self.send, self.recv, self.local = sems
        x, y, c = _place()
        self.c = c
        self.me, self.sibling = (x, y, c), (x, y, 1 - c)
        self.chips = [(1 - x, y), (x, 1 - y), (1 - x, 1 - y)]

    def _copy(self, w, k, block, to, own=False):
        slot = self.outs[w].at[4 * block[0] + 2 * block[1] + block[2]]
        return pltpu.make_async_remote_copy(
            src_ref=self.ins[w] if own else slot, dst_ref=slot,
            send_sem=self.send.at[w, k], recv_sem=self.recv.at[w, k], device_id=to, device_id_type=MESH)

    def _mine(self, w):
        x, y, c = self.me
        return pltpu.make_async_copy(self.ins[w], self.outs[w].at[4 * x + 2 * y + c], self.local.at[w])

    def _first(self, w):
        return ([self._copy(w, 0, self.me, self.sibling, own=True)]
                + [self._copy(w, 1 + j, self.me, (*chip, self.c), own=True) for j, chip in enumerate(self.chips)])

    def _passed(self, w):
        return [self._copy(w, 4 + j, (*chip, self.c), self.sibling) for j, chip in enumerate(self.chips)]

    def start(self):
        for w in range(len(self.ins)):
            self._mine(w).start()
        for w in range(len(self.ins)):
            for cp in self._first(w):
                cp.start()

    def forward(self):
        for w in range(len(self.ins)):
            for j, chip in enumerate(self.chips):
                self._copy(w, 1 + j, (*chip, self.c), self.me).wait_recv()
                self._passed(w)[j].start()

    def finish(self):
        for w in range(len(self.ins)):
            self._copy(w, 0, self.sibling, self.me).wait_recv()
            for j, chip in enumerate(self.chips):
                self._copy(w, 4 + j, (*chip, 1 - self.c), self.me).wait_recv()
        for w in range(len(self.ins)):
            for cp in self._first(w) + self._passed(w):
                cp.wait_send()
            self._mine(w).wait()


class _Exchange:
    def __init__(self, ins, outs, sems):
        self.ins, self.outs = ins, outs
        self.send, self.recv, self.local = sems
        self.x, self.y, self.c = _place()
        self.me = 4 * self.x + 2 * self.y + self.c

    def _copy(self, w, k):
        flip = lambda v, bit: 1 - v if bit else v
        peer = (flip(self.x, ((k + 1) >> 2) & 1), flip(self.y, ((k + 1) >> 1) & 1), flip(self.c, (k + 1) & 1))
        return pltpu.make_async_remote_copy(
            src_ref=self.ins[w].at[4 * peer[0] + 2 * peer[1] + peer[2]], dst_ref=self.outs[w].at[self.me],
            send_sem=self.send.at[w, k], recv_sem=self.recv.at[w, k], device_id=peer, device_id_type=MESH)

    def _mine(self, w):
        return pltpu.make_async_copy(self.ins[w].at[self.me], self.outs[w].at[self.me], self.local.at[w])

    def start(self):
        for w in range(len(self.ins)):
            self._mine(w).start()
            for k in range(N_DEV - 1):
                self._copy(w, k).start()

    def finish(self):
        for w in range(len(self.ins)):
            for k in range(N_DEV - 1):
                self._copy(w, k).wait()
            self._mine(w).wait()


class _PairExchange:
    def __init__(self, ins, outs, sems):
        self.ins, self.outs = ins, outs
        self.send, self.recv, _ = sems
        x, y, self.c = _place()
        self.sibling = (x, y, 1 - self.c)

    def _copy(self, w, t):
        return pltpu.make_async_remote_copy(
            src_ref=self.ins[w].at[2 * t + 1 - self.c], dst_ref=self.outs[w].at[t],
            send_sem=self.send.at[w, t], recv_sem=self.recv.at[w, t], device_id=self.sibling, device_id_type=MESH)

    def start(self):
        for w in range(len(self.ins)):
            for t in range(4):
                self._copy(w, t).start()

    def finish(self):
        for w in range(len(self.ins)):
            for t in range(4):
                self._copy(w, t).wait()


class _ChipExchange:
    def __init__(self, ins, outs, sems):
        self.ins, self.outs = ins, outs
        self.send, self.recv, self.local = sems
        self.x, self.y, self.c = _place()
        self.chip = 2 * self.x + self.y

    def _copy(self, w, k):
        flip = lambda v, bit: 1 - v if bit else v
        px, py = flip(self.x, ((k + 1) >> 1) & 1), flip(self.y, (k + 1) & 1)
        return pltpu.make_async_remote_copy(
            src_ref=self.ins[w].at[2 * px + py], dst_ref=self.outs[w].at[self.chip],
            send_sem=self.send.at[w, k], recv_sem=self.recv.at[w, k], device_id=(px, py, self.c),
            device_id_type=MESH)

    def _mine(self, w):
        return pltpu.make_async_copy(self.ins[w].at[self.chip], self.outs[w].at[self.chip], self.local.at[w])

    def start(self):
        for w in range(len(self.ins)):
            self._mine(w).start()
            for k in range(3):
                self._copy(w, k).start()

    def finish(self):
        for w in range(len(self.ins)):
            for k in range(3):
                self._copy(w, k).wait()
            self._mine(w).wait()


def _pair_exchange(xs, name):
    nw = len(xs)

    def body(*refs):
        comm = _PairExchange(refs[:nw], refs[nw:2 * nw], refs[2 * nw:])
        comm.start()
        comm.finish()

    return pl.pallas_call(
        body, name=name, in_specs=[ANY] * nw, out_specs=[ANY] * nw,
        out_shape=[jax.ShapeDtypeStruct((4,) + a.shape[1:], a.dtype) for a in xs],
        scratch_shapes=_comm_sems(nw),
    )(*xs)


def _pair_sum(own, got, name):
    _, r, c = own.shape
    tr = r
    for t in (256, 128, 64, 32, 16):
        if r % t == 0 and r > t:
            tr = t
            break

    def body(own_ref, got_ref, out_ref):
        mine = jnp.where(lax.axis_index("c") == 0, own_ref[:, 0].astype(F32), own_ref[:, 1].astype(F32))
        out_ref[...] = (mine + got_ref[...].astype(F32)).astype(BF16)

    return pl.pallas_call(
        body, name=name, grid=(r // tr,),
        in_specs=[pl.BlockSpec((4, 2, tr, c), lambda i: (0, 0, i, 0)), pl.BlockSpec((4, tr, c), lambda i: (0, i, 0))],
        out_specs=pl.BlockSpec((4, tr, c), lambda i: (0, i, 0)),
        out_shape=jax.ShapeDtypeStruct((4, r, c), BF16),
        compiler_params=_params(("parallel",)),
    )(own.reshape(4, 2, r, c), got)


def _split_refs(refs, n_in, n_comm, n_out, n_scr):
    a = n_in
    b = a + n_comm
    c = b + n_out
    d = c + n_comm
    e = d + n_scr
    return refs[:a], refs[a:b], refs[b:c], refs[c:d], refs[d:e], refs[e:]


def _all_gather(xs, name):
    nw = len(xs)

    def body(*refs):
        comm = _Gather(refs[:nw], refs[nw:2 * nw], refs[2 * nw:])
        comm.start()
        comm.forward()
        comm.finish()

    return pl.pallas_call(
        body, name=name, in_specs=[ANY] * nw, out_specs=[ANY] * nw,
        out_shape=[jax.ShapeDtypeStruct((N_DEV,) + a.shape, a.dtype) for a in xs],
        scratch_shapes=_comm_sems(nw),
    )(*xs)


def _ffn_fwd(h, gain, wgu, wd, name, gather=()):
    n = h.shape[0]
    tm = _row_tile(n, 512)
    n_i = n // tm
    nw = len(gather)

    def body(*refs):
        (h_ref, g_ref, wgu_ref, wd_ref), gin, (out_ref, gate_ref, up_ref), gout, (n_scr, acc_scr), sems = \
            _split_refs(refs, 4, nw, 3, 2)
        i = pl.program_id(0)
        j = pl.program_id(1)
        if nw:
            comm = _Gather(gin, gout, sems)
            pl.when((i == 0) & (j == 0))(comm.start)
            pl.when((i == (3 * n_i) // 4) & (j == 0))(comm.forward)

        @pl.when(j == 0)
        def _():
            y, _, _ = _rms(h_ref[...], g_ref[...])
            n_scr[...] = y.astype(BF16)
            acc_scr[...] = jnp.zeros_like(acc_scr)

        nb = n_scr[...]
        gate = _dot(nb, wgu_ref[0, 0])
        up = _dot(nb, wgu_ref[1, 0])
        gate_ref[0] = gate.astype(BF16)
        up_ref[0] = up.astype(BF16)
        act = (gate * _sigmoid(gate) * up).astype(BF16)
        acc_scr[...] += _dot(act, wd_ref[0])

        @pl.when(j == N_CHUNK - 1)
        def _():
            out_ref[...] = h_ref[...] + 0.5 * acc_scr[...]

        if nw:
            pl.when((i == n_i - 1) & (j == N_CHUNK - 1))(comm.finish)

    return pl.pallas_call(
        body, name=name, grid=(n_i, N_CHUNK),
        in_specs=[pl.BlockSpec((tm, D_MODEL), lambda i, j: (i, 0)),
                  pl.BlockSpec((1, D_MODEL), lambda i, j: (0, 0)),
                  pl.BlockSpec((2, 1, D_MODEL, F_CHUNK), lambda i, j: (0, j, 0, 0)),
                  pl.BlockSpec((1, F_CHUNK, D_MODEL), lambda i, j: (j, 0, 0))] + [ANY] * nw,
        out_specs=[pl.BlockSpec((tm, D_MODEL), lambda i, j: (i, 0)),
                   pl.BlockSpec((1, tm, F_CHUNK), lambda i, j: (j, i, 0)),
                   pl.BlockSpec((1, tm, F_CHUNK), lambda i, j: (j, i, 0))] + [ANY] * nw,
        out_shape=[jax.ShapeDtypeStruct((n, D_MODEL), F32),
                   jax.ShapeDtypeStruct((N_CHUNK, n, F_CHUNK), BF16),
                   jax.ShapeDtypeStruct((N_CHUNK, n, F_CHUNK), BF16)]
        + [jax.ShapeDtypeStruct((N_DEV,) + a.shape, a.dtype) for a in gather],
        scratch_shapes=[pltpu.VMEM((tm, D_MODEL), BF16), pltpu.VMEM((tm, D_MODEL), F32)]
        + (_comm_sems(nw) if nw else []),
        compiler_params=_params(("arbitrary", "arbitrary")),
    )(h, gain, wgu, wd, *gather)


def _ffn_bwd_x(dh_out, h_in, gain, gate, up, wgu, wd, name):
    n = h_in.shape[0]
    tm = _row_tile(n, 512)

    def body(dh_ref, h_ref, g_ref, gate_ref, up_ref, wgu_ref, wd_ref,
             dhin_ref, dgate_ref, dup_ref, dgain_ref, dhb_scr, acc_scr):
        i = pl.program_id(0)
        j = pl.program_id(1)

        @pl.when((i == 0) & (j == 0))
        def _():
            dgain_ref[...] = jnp.zeros_like(dgain_ref)

        @pl.when(j == 0)
        def _():
            dhb_scr[...] = (0.5 * dh_ref[...]).astype(BF16)
            acc_scr[...] = jnp.zeros_like(acc_scr)

        da = _dot_nt(dhb_scr[...], wd_ref[0])
        g = gate_ref[0].astype(F32)
        u = up_ref[0].astype(F32)
        sig = _sigmoid(g)
        dgate = (da * u * (sig * (1.0 + g * (1.0 - sig)))).astype(BF16)
        dup = (da * (g * sig)).astype(BF16)
        dgate_ref[0] = dgate
        dup_ref[0] = dup
        acc_scr[...] += _dot_nt(dgate, wgu_ref[0, 0]) + _dot_nt(dup, wgu_ref[1, 0])

        @pl.when(j == N_CHUNK - 1)
        def _():
            gain_v = g_ref[...]
            _, xhat, r = _rms(h_ref[...], gain_v)
            dn = acc_scr[...]
            dhin_ref[...] = dh_ref[...] + _rms_bwd(dn, xhat, r, gain_v)
            dgain_ref[...] += jnp.sum(dn * xhat, axis=0, keepdims=True)

    chunk = pl.BlockSpec((1, tm, F_CHUNK), lambda i, j: (j, i, 0))
    rows = pl.BlockSpec((tm, D_MODEL), lambda i, j: (i, 0))
    vec = pl.BlockSpec((1, D_MODEL), lambda i, j: (0, 0))
    return pl.pallas_call(
        body, name=name, grid=(n // tm, N_CHUNK),
        in_specs=[rows, rows, vec, chunk, chunk,
                  pl.BlockSpec((2, 1, D_MODEL, F_CHUNK), lambda i, j: (0, j, 0, 0)),
                  pl.BlockSpec((1, F_CHUNK, D_MODEL), lambda i, j: (j, 0, 0))],
        out_specs=[rows, chunk, chunk, vec],
        out_shape=[jax.ShapeDtypeStruct((n, D_MODEL), F32),
                   jax.ShapeDtypeStruct((N_CHUNK, n, F_CHUNK), BF16),
                   jax.ShapeDtypeStruct((N_CHUNK, n, F_CHUNK), BF16),
                   jax.ShapeDtypeStruct((1, D_MODEL), F32)],
        scratch_shapes=[pltpu.VMEM((tm, D_MODEL), BF16), pltpu.VMEM((tm, D_MODEL), F32)],
        compiler_params=_params(("arbitrary", "arbitrary")),
    )(dh_out, h_in, gain, gate, up, wgu, wd)


def _ffn_bwd_act(dh_out, gate, up, wd, name):
    n = dh_out.shape[0]
    tm = _row_tile(n, 512)

    def body(dh_ref, gate_ref, up_ref, wd_ref, dgate_ref, dup_ref, dhb_scr):
        @pl.when(pl.program_id(1) == 0)
        def _():
            dhb_scr[...] = (0.5 * dh_ref[...]).astype(BF16)

        da = _dot_nt(dhb_scr[...], wd_ref[0])
        g = gate_ref[0].astype(F32)
        u = up_ref[0].astype(F32)
        sig = _sigmoid(g)
        dgate_ref[0] = (da * u * (sig * (1.0 + g * (1.0 - sig)))).astype(BF16)
        dup_ref[0] = (da * (g * sig)).astype(BF16)

    chunk = pl.BlockSpec((1, tm, F_CHUNK), lambda i, j: (j, i, 0))
    return pl.pallas_call(
        body, name=name, grid=(n // tm, N_CHUNK),
        in_specs=[pl.BlockSpec((tm, D_MODEL), lambda i, j: (i, 0)), chunk, chunk,
                  pl.BlockSpec((1, F_CHUNK, D_MODEL), lambda i, j: (j, 0, 0))],
        out_specs=[chunk, chunk],
        out_shape=[jax.ShapeDtypeStruct((N_CHUNK, n, F_CHUNK), BF16)] * 2,
        scratch_shapes=[pltpu.VMEM((tm, D_MODEL), BF16)],
        compiler_params=_params(("parallel", "arbitrary")),
    )(dh_out, gate, up, wd)


def _ffn_bwd_in(dh_out, h_in, gain, dgate, dup, wgu, name, exchange=()):
    n = h_in.shape[0]
    tm = _row_tile(n, 512)
    n_i = n // tm
    nw = len(exchange)

    def body(*refs):
        (dh_ref, h_ref, g_ref, dgate_ref, dup_ref, wgu_ref), xin, (dhin_ref, dgain_ref), xout, (acc_scr,), sems = \
            _split_refs(refs, 6, nw, 2, 1)
        i = pl.program_id(0)
        j = pl.program_id(1)
        if nw:
            comm = _Exchange(xin, xout, sems)
            pl.when((i == 0) & (j == 0))(comm.start)

        @pl.when((i == 0) & (j == 0))
        def _():
            dgain_ref[...] = jnp.zeros_like(dgain_ref)

        @pl.when(j == 0)
        def _():
            acc_scr[...] = jnp.zeros_like(acc_scr)

        acc_scr[...] += _dot_nt(dgate_ref[0], wgu_ref[0, 0]) + _dot_nt(dup_ref[0], wgu_ref[1, 0])

        @pl.when(j == N_CHUNK - 1)
        def _():
            gain_v = g_ref[...]
            _, xhat, r = _rms(h_ref[...], gain_v)
            dn = acc_scr[...]
            dhin_ref[...] = dh_ref[...] + _rms_bwd(dn, xhat, r, gain_v)
            dgain_ref[...] += jnp.sum(dn * xhat, axis=0, keepdims=True)

        if nw:
            pl.when((i == n_i - 1) & (j == N_CHUNK - 1))(comm.finish)

    chunk = pl.BlockSpec((1, tm, F_CHUNK), lambda i, j: (j, i, 0))
    rows = pl.BlockSpec((tm, D_MODEL), lambda i, j: (i, 0))
    vec = pl.BlockSpec((1, D_MODEL), lambda i, j: (0, 0))
    return pl.pallas_call(
        body, name=name, grid=(n_i, N_CHUNK),
        in_specs=[rows, rows, vec, chunk, chunk,
                  pl.BlockSpec((2, 1, D_MODEL, F_CHUNK), lambda i, j: (0, j, 0, 0))] + [ANY] * nw,
        out_specs=[rows, vec] + [ANY] * nw,
        out_shape=[jax.ShapeDtypeStruct((n, D_MODEL), F32), jax.ShapeDtypeStruct((1, D_MODEL), F32)]
        + [jax.ShapeDtypeStruct(a.shape, a.dtype) for a in exchange],
        scratch_shapes=[pltpu.VMEM((tm, D_MODEL), F32)] + (_comm_sems(nw) if nw else []),
        compiler_params=_params(("arbitrary", "arbitrary")),
    )(dh_out, h_in, gain, dgate, dup, wgu, *exchange)


def _ffn_bwd_w(dh_out, h_in, gain, gate, up, dgate, dup, name):
    n = h_in.shape[0]
    tm = _row_tile(n, 512)
    n_i = n // tm

    def body(dh_ref, h_ref, g_ref, gate_ref, up_ref, dgate_ref, dup_ref, dwgu_ref, dwd_ref,
             ag_scr, au_scr, ad_scr):
        i = pl.program_id(1)

        @pl.when(i == 0)
        def _():
            ag_scr[...] = jnp.zeros_like(ag_scr)
            au_scr[...] = jnp.zeros_like(au_scr)
            ad_scr[...] = jnp.zeros_like(ad_scr)

        y, _, _ = _rms(h_ref[...], g_ref[...])
        nb = y.astype(BF16)
        ag_scr[...] += _dot_tn(nb, dgate_ref[0])
        au_scr[...] += _dot_tn(nb, dup_ref[0])
        g = gate_ref[0].astype(F32)
        act = (g * _sigmoid(g) * up_ref[0].astype(F32)).astype(BF16)
        ad_scr[...] += _dot_tn(act, (0.5 * dh_ref[...]).astype(BF16))

        @pl.when(i == n_i - 1)
        def _():
            dwgu_ref[0, 0] = ag_scr[...].astype(BF16)
            dwgu_ref[1, 0] = au_scr[...].astype(BF16)
            dwd_ref[0] = ad_scr[...].astype(BF16)

    chunk = pl.BlockSpec((1, tm, F_CHUNK), lambda j, i: (j, i, 0))
    rows = pl.BlockSpec((tm, D_MODEL), lambda j, i: (i, 0))
    return pl.pallas_call(
        body, name=name, grid=(N_CHUNK, n_i),
        in_specs=[rows, rows, pl.BlockSpec((1, D_MODEL), lambda j, i: (0, 0)), chunk, chunk, chunk, chunk],
        out_specs=[pl.BlockSpec((2, 1, D_MODEL, F_CHUNK), lambda j, i: (0, j, 0, 0)),
                   pl.BlockSpec((1, F_CHUNK, D_MODEL), lambda j, i: (j, 0, 0))],
        out_shape=[jax.ShapeDtypeStruct((2, N_CHUNK, D_MODEL, F_CHUNK), BF16),
                   jax.ShapeDtypeStruct((N_CHUNK, F_CHUNK, D_MODEL), BF16)],
        scratch_shapes=[pltpu.VMEM((D_MODEL, F_CHUNK), F32), pltpu.VMEM((D_MODEL, F_CHUNK), F32),
                        pltpu.VMEM((F_CHUNK, D_MODEL), F32)],
        compiler_params=_params(("parallel", "arbitrary")),
    )(dh_out, h_in, gain, gate, up, dgate, dup)


def _resident(shape, rank):
    zeros = (0,) * len(shape)
    index_map = (lambda i: zeros) if rank == 1 else (lambda i, j: zeros)
    return pl.BlockSpec(shape, index_map, pipeline_mode=pl.Buffered(1))


W_GU_SHAPE = (2, N_CHUNK, F_CHUNK, D_MODEL)
W_D_SHAPE = (N_CHUNK, F_CHUNK, D_MODEL)


def _ffn_fwd(h, gain, wgu, wd, name, gather=()):
    n = h.shape[0]
    tm = _row_tile(n, 512)
    n_i = n // tm
    nw = len(gather)

    def body(*refs):
        (h_ref, g_ref, wgu_ref, wd_ref), gin, (out_ref, nrm_ref, gate_ref, up_ref), gout, _, sems = \
            _split_refs(refs, 4, nw, 4, 0)
        i = pl.program_id(0)
        if nw:
            comm = _Gather(gin, gout, sems)
            pl.when(i == 0)(comm.start)
            pl.when(i == max(n_i - 3, 0))(comm.forward)

        hv = h_ref[...]
        y, _, _ = _rms(hv, g_ref[...])
        nb = y.astype(BF16)
        nrm_ref[...] = nb
        acc = jnp.zeros((tm, D_MODEL), F32)
        for j in range(N_CHUNK):
            gate = _dot_nt(nb, wgu_ref[0, j])
            up = _dot_nt(nb, wgu_ref[1, j])
            gate_ref[j] = gate.astype(BF16)
            up_ref[j] = up.astype(BF16)
            acc = acc + _dot((gate * _sigmoid(gate) * up).astype(BF16), wd_ref[j])
        out_ref[...] = hv + 0.5 * acc

        if nw:
            pl.when(i == n_i - 1)(comm.finish)

    rows = pl.BlockSpec((tm, D_MODEL), lambda i: (i, 0))
    chunks = pl.BlockSpec((N_CHUNK, tm, F_CHUNK), lambda i: (0, i, 0))
    return pl.pallas_call(
        body, name=name, grid=(n_i,),
        in_specs=[rows, pl.BlockSpec((1, D_MODEL), lambda i: (0, 0)), _resident(W_GU_SHAPE, 1),
                  _resident(W_D_SHAPE, 1)] + [ANY] * nw,
        out_specs=[rows, rows, chunks, chunks] + [ANY] * nw,
        out_shape=[jax.ShapeDtypeStruct((n, D_MODEL), F32), jax.ShapeDtypeStruct((n, D_MODEL), BF16),
                   jax.ShapeDtypeStruct((N_CHUNK, n, F_CHUNK), BF16),
                   jax.ShapeDtypeStruct((N_CHUNK, n, F_CHUNK), BF16)]
        + [jax.ShapeDtypeStruct((N_DEV,) + a.shape, a.dtype) for a in gather],
        scratch_shapes=_comm_sems(nw) if nw else [],
        compiler_params=_params(("arbitrary",)),
    )(h, gain, wgu, wd, *gather)


def _swiglu_bwd(da, gate_ref, up_ref, j):
    g = gate_ref[j].astype(F32)
    u = up_ref[j].astype(F32)
    sig = _sigmoid(g)
    return (da * u * (sig * (1.0 + g * (1.0 - sig)))).astype(BF16), (da * (g * sig)).astype(BF16)


def _ffn_bwd_x(dh_out, h_in, gain, gate, up, wgu, wd, name):
    n = h_in.shape[0]
    tm = _row_tile(n, 256)

    def body(dh_ref, h_ref, g_ref, gate_ref, up_ref, wgu_ref, wd_ref,
             dhin_ref, dhb_ref, dgate_ref, dup_ref, dgain_ref):
        @pl.when(pl.program_id(0) == 0)
        def _():
            dgain_ref[...] = jnp.zeros_like(dgain_ref)

        dhv = dh_ref[...]
        dhb = (0.5 * dhv).astype(BF16)
        dhb_ref[...] = dhb
        dn = jnp.zeros((tm, D_MODEL), F32)
        for j in range(N_CHUNK):
            dgate, dup = _swiglu_bwd(_dot_nt(dhb, wd_ref[j]), gate_ref, up_ref, j)
            dgate_ref[j] = dgate
            dup_ref[j] = dup
            dn = dn + _dot(dgate, wgu_ref[0, j]) + _dot(dup, wgu_ref[1, j])
        gain_v = g_ref[...]
        _, xhat, r = _rms(h_ref[...], gain_v)
        dhin_ref[...] = dhv + _rms_bwd(dn, xhat, r, gain_v)
        dgain_ref[...] += jnp.sum(dn * xhat, axis=0, keepdims=True)

    rows = pl.BlockSpec((tm, D_MODEL), lambda i: (i, 0))
    chunks = pl.BlockSpec((N_CHUNK, tm, F_CHUNK), lambda i: (0, i, 0))
    vec = pl.BlockSpec((1, D_MODEL), lambda i: (0, 0))
    return pl.pallas_call(
        body, name=name, grid=(n // tm,),
        in_specs=[rows, rows, vec, chunks, chunks, _resident(W_GU_SHAPE, 1), _resident(W_D_SHAPE, 1)],
        out_specs=[rows, rows, chunks, chunks, vec],
        out_shape=[jax.ShapeDtypeStruct((n, D_MODEL), F32), jax.ShapeDtypeStruct((n, D_MODEL), BF16),
                   jax.ShapeDtypeStruct((N_CHUNK, n, F_CHUNK), BF16),
                   jax.ShapeDtypeStruct((N_CHUNK, n, F_CHUNK), BF16),
                   jax.ShapeDtypeStruct((1, D_MODEL), F32)],
        compiler_params=_params(("arbitrary",)),
    )(dh_out, h_in, gain, gate, up, wgu, wd)


def _ffn_bwd_act(dh_out, gate, up, wd, name, exchange=()):
    n = dh_out.shape[0]
    tm = _row_tile(n, 512)
    n_i = n // tm
    nw = len(exchange)

    def body(*refs):
        (dh_ref, gate_ref, up_ref, wd_ref), xin, (dhb_ref, dgate_ref, dup_ref), xout, _, sems = \
            _split_refs(refs, 4, nw, 3, 0)
        i = pl.program_id(0)
        if nw:
            comm = _Exchange(xin, xout, sems)
            pl.when(i == 0)(comm.start)

        dhb = (0.5 * dh_ref[...]).astype(BF16)
        dhb_ref[...] = dhb
        for j in range(N_CHUNK):
            dgate_ref[j], dup_ref[j] = _swiglu_bwd(_dot_nt(dhb, wd_ref[j]), gate_ref, up_ref, j)

        if nw:
            pl.when(i == n_i - 1)(comm.finish)

    rows = pl.BlockSpec((tm, D_MODEL), lambda i: (i, 0))
    chunks = pl.BlockSpec((N_CHUNK, tm, F_CHUNK), lambda i: (0, i, 0))
    return pl.pallas_call(
        body, name=name, grid=(n_i,),
        in_specs=[rows, chunks, chunks, _resident(W_D_SHAPE, 1)] + [ANY] * nw,
        out_specs=[rows, chunks, chunks] + [ANY] * nw,
        out_shape=[jax.ShapeDtypeStruct((n, D_MODEL), BF16)] + [jax.ShapeDtypeStruct((N_CHUNK, n, F_CHUNK), BF16)] * 2
        + [jax.ShapeDtypeStruct(a.shape, a.dtype) for a in exchange],
        scratch_shapes=_comm_sems(nw) if nw else [],
        compiler_params=_params(("arbitrary",)),
    )(dh_out, gate, up, wd, *exchange)


def _ffn_bwd_in(dh_out, h_in, gain, dgate, dup, wgu, name, exchange=()):
    n = h_in.shape[0]
    tm = _row_tile(n, 512)
    n_i = n // tm
    nw = len(exchange)

    def body(*refs):
        (dh_ref, h_ref, g_ref, dgate_ref, dup_ref, wgu_ref), xin, (dhin_ref, dgain_ref), xout, _, sems = \
            _split_refs(refs, 6, nw, 2, 0)
        i = pl.program_id(0)
        if nw:
            comm = _ChipExchange(xin, xout, sems)
            pl.when(i == 0)(comm.start)

        @pl.when(i == 0)
        def _():
            dgain_ref[...] = jnp.zeros_like(dgain_ref)

        dn = jnp.zeros((tm, D_MODEL), F32)
        for j in range(N_CHUNK):
            dn = dn + _dot(dgate_ref[j], wgu_ref[0, j]) + _dot(dup_ref[j], wgu_ref[1, j])
        gain_v = g_ref[...]
        _, xhat, r = _rms(h_ref[...], gain_v)
        dhin_ref[...] = dh_ref[...] + _rms_bwd(dn, xhat, r, gain_v)
        dgain_ref[...] += jnp.sum(dn * xhat, axis=0, keepdims=True)

        if nw:
            pl.when(i == n_i - 1)(comm.finish)

    rows = pl.BlockSpec((tm, D_MODEL), lambda i: (i, 0))
    chunks = pl.BlockSpec((N_CHUNK, tm, F_CHUNK), lambda i: (0, i, 0))
    vec = pl.BlockSpec((1, D_MODEL), lambda i: (0, 0))
    return pl.pallas_call(
        body, name=name, grid=(n_i,),
        in_specs=[rows, rows, vec, chunks, chunks, _resident(W_GU_SHAPE, 1)] + [ANY] * nw,
        out_specs=[rows, vec] + [ANY] * nw,
        out_shape=[jax.ShapeDtypeStruct((n, D_MODEL), F32), jax.ShapeDtypeStruct((1, D_MODEL), F32)]
        + [jax.ShapeDtypeStruct(a.shape, a.dtype) for a in exchange],
        scratch_shapes=_comm_sems(nw) if nw else [],
        compiler_params=_params(("arbitrary",)),
    )(dh_out, h_in, gain, dgate, dup, wgu, *exchange)


W_GROUP = 2


def _ffn_bwd_w(dhb, nrm, gate, up, dgate, dup, name):
    n = nrm.shape[0]
    tm = _row_tile(n, 512)
    n_i = n // tm

    def body(dhb_ref, nrm_ref, gate_ref, up_ref, dgate_ref, dup_ref, dwgu_ref, dwd_ref, ag_scr, au_scr, ad_scr):
        i = pl.program_id(1)

        @pl.when(i == 0)
        def _():
            ag_scr[...] = jnp.zeros_like(ag_scr)
            au_scr[...] = jnp.zeros_like(au_scr)
            ad_scr[...] = jnp.zeros_like(ad_scr)

        nb = nrm_ref[...]
        dhv = dhb_ref[...]
        for jj in range(W_GROUP):
            ag_scr[jj] += _dot_tn(dgate_ref[jj], nb)
            au_scr[jj] += _dot_tn(dup_ref[jj], nb)
            g = gate_ref[jj].astype(F32)
            act = (g * _sigmoid(g) * up_ref[jj].astype(F32)).astype(BF16)
            ad_scr[jj] += _dot_tn(act, dhv)

        @pl.when(i == n_i - 1)
        def _():
            dwgu_ref[0] = ag_scr[...].astype(BF16)
            dwgu_ref[1] = au_scr[...].astype(BF16)
            dwd_ref[...] = ad_scr[...].astype(BF16)

    chunks = pl.BlockSpec((W_GROUP, tm, F_CHUNK), lambda g, i: (g, i, 0))
    rows = pl.BlockSpec((tm, D_MODEL), lambda g, i: (i, 0))
    return pl.pallas_call(
        body, name=name, grid=(N_CHUNK // W_GROUP, n_i),
        in_specs=[rows, rows, chunks, chunks, chunks, chunks],
        out_specs=[pl.BlockSpec((2, W_GROUP, F_CHUNK, D_MODEL), lambda g, i: (0, g, 0, 0)),
                   pl.BlockSpec((W_GROUP, F_CHUNK, D_MODEL), lambda g, i: (g, 0, 0))],
        out_shape=[jax.ShapeDtypeStruct(W_GU_SHAPE, BF16), jax.ShapeDtypeStruct(W_D_SHAPE, BF16)],
        scratch_shapes=[pltpu.VMEM((W_GROUP, F_CHUNK, D_MODEL), F32), pltpu.VMEM((W_GROUP, F_CHUNK, D_MODEL), F32),
                        pltpu.VMEM((W_GROUP, F_CHUNK, D_MODEL), F32)],
        compiler_params=_params(("parallel", "arbitrary")),
    )(dhb, nrm, gate, up, dgate, dup)


HID_PIECES = ((0, 1024), (1024, 2048), (2048, D_FF))
W_GU_SHAPE = (2, D_FF, D_MODEL)
W_D_SHAPE = (D_FF, D_MODEL)


def _ffn_fwd(h, gain, wgu, wd, name, gather=()):
    n = h.shape[0]
    tm = _row_tile(n, 512)
    n_i = n // tm
    nw = len(gather)

    def body(*refs):
        (h_ref, g_ref, wgu_ref, wd_ref), gin, (out_ref, nrm_ref, gate_ref, up_ref), gout, _, sems = \
            _split_refs(refs, 4, nw, 4, 0)
        i = pl.program_id(0)
        if nw:
            comm = _Gather(gin, gout, sems)
            pl.when(i == 0)(comm.start)
            pl.when(i == max(n_i - 3, 0))(comm.forward)

        hv = h_ref[...]
        y, _, _ = _rms(hv, g_ref[...])
        nb = y.astype(BF16)
        nrm_ref[...] = nb
        acc = jnp.zeros((tm, D_MODEL), F32)
        for a, b in HID_PIECES:
            gate = _dot_nt(nb, wgu_ref[0, a:b, :])
            up = _dot_nt(nb, wgu_ref[1, a:b, :])
            gate_ref[:, a:b] = gate.astype(BF16)
            up_ref[:, a:b] = up.astype(BF16)
            acc = acc + _dot((gate * _sigmoid(gate) * up).astype(BF16), wd_ref[a:b, :])
        out_ref[...] = hv + 0.5 * acc

        if nw:
            pl.when(i == n_i - 1)(comm.finish)

    rows = pl.BlockSpec((tm, D_MODEL), lambda i: (i, 0))
    hid = pl.BlockSpec((tm, D_FF), lambda i: (i, 0))
    return pl.pallas_call(
        body, name=name, grid=(n_i,),
        in_specs=[rows, pl.BlockSpec((1, D_MODEL), lambda i: (0, 0)), _resident(W_GU_SHAPE, 1),
                  _resident(W_D_SHAPE, 1)] + [ANY] * nw,
        out_specs=[rows, rows, hid, hid] + [ANY] * nw,
        out_shape=[jax.ShapeDtypeStruct((n, D_MODEL), F32), jax.ShapeDtypeStruct((n, D_MODEL), BF16),
                   jax.ShapeDtypeStruct((n, D_FF), BF16), jax.ShapeDtypeStruct((n, D_FF), BF16)]
        + [jax.ShapeDtypeStruct((N_DEV,) + a.shape, a.dtype) for a in gather],
        scratch_shapes=_comm_sems(nw) if nw else [],
        compiler_params=_params(("arbitrary",)),
    )(h, gain, wgu, wd, *gather)


def _swiglu_bwd(da, gate_ref, up_ref, a, b):
    g = gate_ref[:, a:b].astype(F32)
    u = up_ref[:, a:b].astype(F32)
    sig = _sigmoid(g)
    return (da * u * (sig * (1.0 + g * (1.0 - sig)))).astype(BF16), (da * (g * sig)).astype(BF16)


def _ffn_bwd_x(dh_out, h_in, gain, gate, up, wgu, wd, name):
    n = h_in.shape[0]
    tm = _row_tile(n, 256)

    def body(dh_ref, h_ref, g_ref, gate_ref, up_ref, wgu_ref, wd_ref,
             dhin_ref, dhb_ref, dgate_ref, dup_ref, dgain_ref):
        @pl.when(pl.program_id(0) == 0)
        def _():
            dgain_ref[...] = jnp.zeros_like(dgain_ref)

        dhv = dh_ref[...]
        dhb = (0.5 * dhv).astype(BF16)
        dhb_ref[...] = dhb
        dn = jnp.zeros((tm, D_MODEL), F32)
        for a, b in HID_PIECES:
            dgate, dup = _swiglu_bwd(_dot_nt(dhb, wd_ref[a:b, :]), gate_ref, up_ref, a, b)
            dgate_ref[:, a:b] = dgate
            dup_ref[:, a:b] = dup
            dn = dn + _dot(dgate, wgu_ref[0, a:b, :]) + _dot(dup, wgu_ref[1, a:b, :])
        gain_v = g_ref[...]
        _, xhat, r = _rms(h_ref[...], gain_v)
        dhin_ref[...] = dhv + _rms_bwd(dn, xhat, r, gain_v)
        dgain_ref[...] += jnp.sum(dn * xhat, axis=0, keepdims=True)

    rows = pl.BlockSpec((tm, D_MODEL), lambda i: (i, 0))
    hid = pl.BlockSpec((tm, D_FF), lambda i: (i, 0))
    vec = pl.BlockSpec((1, D_MODEL), lambda i: (0, 0))
    return pl.pallas_call(
        body, name=name, grid=(n // tm,),
        in_specs=[rows, rows, vec, hid, hid, _resident(W_GU_SHAPE, 1), _resident(W_D_SHAPE, 1)],
        out_specs=[rows, rows, hid, hid, vec],
        out_shape=[jax.ShapeDtypeStruct((n, D_MODEL), F32), jax.ShapeDtypeStruct((n, D_MODEL), BF16),
                   jax.ShapeDtypeStruct((n, D_FF), BF16), jax.ShapeDtypeStruct((n, D_FF), BF16),
                   jax.ShapeDtypeStruct((1, D_MODEL), F32)],
        compiler_params=_params(("arbitrary",)),
    )(dh_out, h_in, gain, gate, up, wgu, wd)


def _ffn_bwd_act(dh_out, gate, up, wd, name, exchange=()):
    n = dh_out.shape[0]
    tm = _row_tile(n, 512)
    n_i = n // tm
    nw = len(exchange)

    def body(*refs):
        (dh_ref, gate_ref, up_ref, wd_ref), xin, (dhb_ref, dgate_ref, dup_ref), xout, _, sems = \
            _split_refs(refs, 4, nw, 3, 0)
        i = pl.program_id(0)
        if nw:
            comm = _Exchange(xin, xout, sems)
            pl.when(i == 0)(comm.start)

        dhb = (0.5 * dh_ref[...]).astype(BF16)
        dhb_ref[...] = dhb
        for a, b in HID_PIECES:
            dgate_ref[:, a:b], dup_ref[:, a:b] = _swiglu_bwd(_dot_nt(dhb, wd_ref[a:b, :]), gate_ref, up_ref, a, b)

        if nw:
            pl.when(i == n_i - 1)(comm.finish)

    rows = pl.BlockSpec((tm, D_MODEL), lambda i: (i, 0))
    hid = pl.BlockSpec((tm, D_FF), lambda i: (i, 0))
    return pl.pallas_call(
        body, name=name, grid=(n_i,),
        in_specs=[rows, hid, hid, _resident(W_D_SHAPE, 1)] + [ANY] * nw,
        out_specs=[rows, hid, hid] + [ANY] * nw,
        out_shape=[jax.ShapeDtypeStruct((n, D_MODEL), BF16)] + [jax.ShapeDtypeStruct((n, D_FF), BF16)] * 2
        + [jax.ShapeDtypeStruct(a.shape, a.dtype) for a in exchange],
        scratch_shapes=_comm_sems(nw) if nw else [],
        compiler_params=_params(("arbitrary",)),
    )(dh_out, gate, up, wd, *exchange)


def _ffn_bwd_in(dh_out, h_in, gain, dgate, dup, wgu, name, exchange=()):
    n = h_in.shape[0]
    tm = _row_tile(n, 512)
    n_i = n // tm
    nw = len(exchange)

    def body(*refs):
        (dh_ref, h_ref, g_ref, dgate_ref, dup_ref, wgu_ref), xin, (dhin_ref, dgain_ref), xout, _, sems = \
            _split_refs(refs, 6, nw, 2, 0)
        i = pl.program_id(0)
        if nw:
            comm = _ChipExchange(xin, xout, sems)
            pl.when(i == 0)(comm.start)

        @pl.when(i == 0)
        def _():
            dgain_ref[...] = jnp.zeros_like(dgain_ref)

        dn = jnp.zeros((tm, D_MODEL), F32)
        for a, b in HID_PIECES:
            dn = dn + _dot(dgate_ref[:, a:b], wgu_ref[0, a:b, :]) + _dot(dup_ref[:, a:b], wgu_ref[1, a:b, :])
        gain_v = g_ref[...]
        _, xhat, r = _rms(h_ref[...], gain_v)
        dhin_ref[...] = dh_ref[...] + _rms_bwd(dn, xhat, r, gain_v)
        dgain_ref[...] += jnp.sum(dn * xhat, axis=0, keepdims=True)

        if nw:
            pl.when(i == n_i - 1)(comm.finish)

    rows = pl.BlockSpec((tm, D_MODEL), lambda i: (i, 0))
    hid = pl.BlockSpec((tm, D_FF), lambda i: (i, 0))
    vec = pl.BlockSpec((1, D_MODEL), lambda i: (0, 0))
    return pl.pallas_call(
        body, name=name, grid=(n_i,),
        in_specs=[rows, rows, vec, hid, hid, _resident(W_GU_SHAPE, 1)] + [ANY] * nw,
        out_specs=[rows, vec] + [ANY] * nw,
        out_shape=[jax.ShapeDtypeStruct((n, D_MODEL), F32), jax.ShapeDtypeStruct((1, D_MODEL), F32)]
        + [jax.ShapeDtypeStruct(a.shape, a.dtype) for a in exchange],
        scratch_shapes=_comm_sems(nw) if nw else [],
        compiler_params=_params(("arbitrary",)),
    )(dh_out, h_in, gain, dgate, dup, wgu, *exchange)


def _token_spec(k, ksub, nq):
    def index_map(i):
        s = ksub * i + k
        return ((s // nq) * (nq - 1) + jnp.maximum(s % nq, 1) - 1, 0)
    return pl.BlockSpec((128, D_MODEL), index_map)


def _is_lead(i, k, ksub, nq):
    return ((ksub * i + k) % nq) == 0


def _assemble_rows(i, x_refs, meta_ref, nq):
    ksub = len(x_refs)
    lead = jnp.concatenate([jnp.zeros((PAD, D_MODEL), F32), meta_ref[...]], axis=0)
    return jnp.concatenate([jnp.where(_is_lead(i, k, ksub, nq), lead, x_refs[k][...]) for k in range(ksub)], axis=0)


def _ffn_fwd_tokens(x2d, meta, lp, gain, wgu, wd, name, gather=()):
    nq = lp // 128
    n = (x2d.shape[0] // (nq - 1)) * nq
    tm = _row_tile(n, 512)
    ksub = tm // 128
    n_i = n // tm
    nw = len(gather)

    def body(*refs):
        x_refs = refs[:ksub]
        (meta_ref, g_ref, wgu_ref, wd_ref), gin, (out_ref, nrm_ref, gate_ref, up_ref), gout, _, sems = \
            _split_refs(refs[ksub:], 4, nw, 4, 0)
        i = pl.program_id(0)
        if nw:
            comm = _Gather(gin, gout, sems)
            pl.when(i == 0)(comm.start)
            pl.when(i == max(n_i - 3, 0))(comm.forward)

        hv = _assemble_rows(i, x_refs, meta_ref, nq)
        y, _, _ = _rms(hv, g_ref[...])
        nb = y.astype(BF16)
        nrm_ref[...] = nb
        acc = jnp.zeros((tm, D_MODEL), F32)
        for a, b in HID_PIECES:
            gate = _dot_nt(nb, wgu_ref[0, a:b, :])
            up = _dot_nt(nb, wgu_ref[1, a:b, :])
            gate_ref[:, a:b] = gate.astype(BF16)
            up_ref[:, a:b] = up.astype(BF16)
            acc = acc + _dot((gate * _sigmoid(gate) * up).astype(BF16), wd_ref[a:b, :])
        out_ref[...] = hv + 0.5 * acc

        if nw:
            pl.when(i == n_i - 1)(comm.finish)

    rows = pl.BlockSpec((tm, D_MODEL), lambda i: (i, 0))
    hid = pl.BlockSpec((tm, D_FF), lambda i: (i, 0))
    return pl.pallas_call(
        body, name=name, grid=(n_i,),
        in_specs=[_token_spec(k, ksub, nq) for k in range(ksub)]
        + [pl.BlockSpec((N_META, D_MODEL), lambda i: (0, 0)), pl.BlockSpec((1, D_MODEL), lambda i: (0, 0)),
           _resident(W_GU_SHAPE, 1), _resident(W_D_SHAPE, 1)] + [ANY] * nw,
        out_specs=[rows, rows, hid, hid] + [ANY] * nw,
        out_shape=[jax.ShapeDtypeStruct((n, D_MODEL), F32), jax.ShapeDtypeStruct((n, D_MODEL), BF16),
                   jax.ShapeDtypeStruct((n, D_FF), BF16), jax.ShapeDtypeStruct((n, D_FF), BF16)]
        + [jax.ShapeDtypeStruct((N_DEV,) + a.shape, a.dtype) for a in gather],
        scratch_shapes=_comm_sems(nw) if nw else [],
        compiler_params=_params(("arbitrary",)),
    )(*([x2d] * ksub), meta, gain, wgu, wd, *gather)


def _ffn_fwd_loss(h, gain, wgu, wd, gfinal, target, lp, name):
    n = h.shape[0]
    nq = lp // 128
    tm = _row_tile(n, 512)
    ksub = tm // 128
    n_i = n // tm

    def body(*refs):
        t_refs = refs[:ksub]
        h_ref, g_ref, wgu_ref, wd_ref, gf_ref, dh_ref, nrm_ref, gate_ref, up_ref, loss_ref, dgf_ref = refs[ksub:]
        i = pl.program_id(0)

        @pl.when(i == 0)
        def _():
            loss_ref[...] = jnp.zeros_like(loss_ref)
            dgf_ref[...] = jnp.zeros_like(dgf_ref)

        hv = h_ref[...]
        y, _, _ = _rms(hv, g_ref[...])
        nb = y.astype(BF16)
        nrm_ref[...] = nb
        acc = jnp.zeros((tm, D_MODEL), F32)
        for a, b in HID_PIECES:
            gate = _dot_nt(nb, wgu_ref[0, a:b, :])
            up = _dot_nt(nb, wgu_ref[1, a:b, :])
            gate_ref[:, a:b] = gate.astype(BF16)
            up_ref[:, a:b] = up.astype(BF16)
            acc = acc + _dot((gate * _sigmoid(gate) * up).astype(BF16), wd_ref[a:b, :])
        hout = hv + 0.5 * acc

        gf = gf_ref[...]
        loss = jnp.zeros((1, 1), F32)
        dgf = jnp.zeros((1, D_MODEL), F32)
        for k in range(ksub):
            yk, xhat, r = _rms(hout[128 * k:128 * (k + 1)], gf)
            err = jnp.where(_is_lead(i, k, ksub, nq), 0.0, yk - t_refs[k][...])
            loss = loss + 0.5 * jnp.sum(jnp.sum(err * err, axis=1, keepdims=True), axis=0,
                                        keepdims=True) * (1.0 / D_MODEL)
            dy = err * (1.0 / D_MODEL)
            dh_ref[128 * k:128 * (k + 1), :] = _rms_bwd(dy, xhat, r, gf)
            dgf = dgf + jnp.sum(dy * xhat, axis=0, keepdims=True)
        loss_ref[...] += loss
        dgf_ref[...] += dgf

    rows = pl.BlockSpec((tm, D_MODEL), lambda i: (i, 0))
    hid = pl.BlockSpec((tm, D_FF), lambda i: (i, 0))
    vec = pl.BlockSpec((1, D_MODEL), lambda i: (0, 0))
    return pl.pallas_call(
        body, name=name, grid=(n_i,),
        in_specs=[_token_spec(k, ksub, nq) for k in range(ksub)]
        + [rows, vec, _resident(W_GU_SHAPE, 1), _resident(W_D_SHAPE, 1), vec],
        out_specs=[rows, rows, hid, hid, pl.BlockSpec((1, 1), lambda i: (0, 0)), vec],
        out_shape=[jax.ShapeDtypeStruct((n, D_MODEL), F32), jax.ShapeDtypeStruct((n, D_MODEL), BF16),
                   jax.ShapeDtypeStruct((n, D_FF), BF16), jax.ShapeDtypeStruct((n, D_FF), BF16),
                   jax.ShapeDtypeStruct((1, 1), F32), jax.ShapeDtypeStruct((1, D_MODEL), F32)],
        compiler_params=_params(("arbitrary",)),
    )(*([target] * ksub), h, gain, wgu, wd, gfinal)


def _ffn_bwd_in_tokens(dh_out, x2d, meta, lp, gain, dgate, dup, wgu, name, exchange=()):
    n = dh_out.shape[0]
    nq = lp // 128
    tm = _row_tile(n, 512)
    ksub = tm // 128
    n_i = n // tm
    nw = len(exchange)

    def body(*refs):
        x_refs = refs[:ksub]
        (meta_ref, dh_ref, g_ref, dgate_ref, dup_ref, wgu_ref), xin, (dhin_ref, dgain_ref), xout, _, sems = \
            _split_refs(refs[ksub:], 6, nw, 2, 0)
        i = pl.program_id(0)
        if nw:
            comm = _ChipExchange(xin, xout, sems)
            pl.when(i == 0)(comm.start)

        @pl.when(i == 0)
        def _():
            dgain_ref[...] = jnp.zeros_like(dgain_ref)

        dn = jnp.zeros((tm, D_MODEL), F32)
        for a, b in HID_PIECES:
            dn = dn + _dot(dgate_ref[:, a:b], wgu_ref[0, a:b, :]) + _dot(dup_ref[:, a:b], wgu_ref[1, a:b, :])
        gain_v = g_ref[...]
        _, xhat, r = _rms(_assemble_rows(i, x_refs, meta_ref, nq), gain_v)
        dhin_ref[...] = dh_ref[...] + _rms_bwd(dn, xhat, r, gain_v)
        dgain_ref[...] += jnp.sum(dn * xhat, axis=0, keepdims=True)

        if nw:
            pl.when(i == n_i - 1)(comm.finish)

    rows = pl.BlockSpec((tm, D_MODEL), lambda i: (i, 0))
    hid = pl.BlockSpec((tm, D_FF), lambda i: (i, 0))
    vec = pl.BlockSpec((1, D_MODEL), lambda i: (0, 0))
    return pl.pallas_call(
        body, name=name, grid=(n_i,),
        in_specs=[_token_spec(k, ksub, nq) for k in range(ksub)]
        + [pl.BlockSpec((N_META, D_MODEL), lambda i: (0, 0)), rows, vec, hid, hid, _resident(W_GU_SHAPE, 1)]
        + [ANY] * nw,
        out_specs=[rows, vec] + [ANY] * nw,
        out_shape=[jax.ShapeDtypeStruct((n, D_MODEL), F32), jax.ShapeDtypeStruct((1, D_MODEL), F32)]
        + [jax.ShapeDtypeStruct(a.shape, a.dtype) for a in exchange],
        scratch_shapes=_comm_sems(nw) if nw else [],
        compiler_params=_params(("arbitrary",)),
    )(*([x2d] * ksub), meta, dh_out, gain, dgate, dup, wgu, *exchange)


def _ffn_bwd_act_wd(dh_out, gate, up, wd, name, exchange=()):
    n = dh_out.shape[0]
    tm = _row_tile(n, 256)
    n_i = n // tm
    nw = len(exchange)

    def body(*refs):
        (dh_ref, gate_ref, up_ref, wd_ref), xin, (dgate_ref, dup_ref, dw_ref), xout, (acc_scr,), sems = \
            _split_refs(refs, 4, nw, 3, 1)
        i = pl.program_id(0)
        if nw:
            comm = _Exchange(xin, xout, sems)
            pl.when(i == 0)(comm.start)

        @pl.when(i == 0)
        def _():
            acc_scr[...] = jnp.zeros_like(acc_scr)

        dhb = (0.5 * dh_ref[...]).astype(BF16)
        for a, b in HID_PIECES:
            da = _dot_nt(dhb, wd_ref[a:b, :])
            g = gate_ref[:, a:b].astype(F32)
            u = up_ref[:, a:b].astype(F32)
            sig = _sigmoid(g)
            silu = g * sig
            dgate_ref[:, a:b] = (da * u * (sig * (1.0 + g * (1.0 - sig)))).astype(BF16)
            dup_ref[:, a:b] = (da * silu).astype(BF16)
            acc_scr[a:b, :] += _dot_tn((silu * u).astype(BF16), dhb)

        @pl.when(i == n_i - 1)
        def _():
            dw_ref[...] = acc_scr[...].astype(BF16)

        if nw:
            pl.when(i == n_i - 1)(comm.finish)

    rows = pl.BlockSpec((tm, D_MODEL), lambda i: (i, 0))
    hid = pl.BlockSpec((tm, D_FF), lambda i: (i, 0))
    return pl.pallas_call(
        body, name=name, grid=(n_i,),
        in_specs=[rows, hid, hid, _resident(W_D_SHAPE, 1)] + [ANY] * nw,
        out_specs=[hid, hid, _resident(W_D_SHAPE, 1)] + [ANY] * nw,
        out_shape=[jax.ShapeDtypeStruct((n, D_FF), BF16)] * 2 + [jax.ShapeDtypeStruct(W_D_SHAPE, BF16)]
        + [jax.ShapeDtypeStruct(a.shape, a.dtype) for a in exchange],
        scratch_shapes=[pltpu.VMEM(W_D_SHAPE, F32)] + (_comm_sems(nw) if nw else []),
        compiler_params=_params(("arbitrary",)),
    )(dh_out, gate, up, wd, *exchange)


def _ffn_bwd_wgu(nrm, dgate, dup, name, exchange=()):
    n = nrm.shape[0]
    tm = _row_tile(n, 256)
    n_i = n // tm
    nw = len(exchange)

    def body(*refs):
        (nrm_ref, dgate_ref, dup_ref), xin, (dw_ref,), xout, (acc_scr,), sems = _split_refs(refs, 3, nw, 1, 1)
        i = pl.program_id(0)
        if nw:
            comm = _Exchange(xin, xout, sems)
            pl.when(i == 0)(comm.start)

        @pl.when(i == 0)
        def _():
            acc_scr[...] = jnp.zeros_like(acc_scr)

        nb = nrm_ref[...]
        for a, b in HID_PIECES:
            acc_scr[0, a:b, :] += _dot_tn(dgate_ref[:, a:b], nb)
            acc_scr[1, a:b, :] += _dot_tn(dup_ref[:, a:b], nb)

        @pl.when(i == n_i - 1)
        def _():
            dw_ref[...] = acc_scr[...].astype(BF16)

        if nw:
            pl.when(i == n_i - 1)(comm.finish)

    hid = pl.BlockSpec((tm, D_FF), lambda i: (i, 0))
    res = pl.pallas_call(
        body, name=name, grid=(n_i,),
        in_specs=[pl.BlockSpec((tm, D_MODEL), lambda i: (i, 0)), hid, hid] + [ANY] * nw,
        out_specs=[_resident(W_GU_SHAPE, 1)] + [ANY] * nw,
        out_shape=[jax.ShapeDtypeStruct(W_GU_SHAPE, BF16)] + [jax.ShapeDtypeStruct(a.shape, a.dtype) for a in exchange],
        scratch_shapes=[pltpu.VMEM(W_GU_SHAPE, F32)] + (_comm_sems(nw) if nw else []),
        compiler_params=_params(("arbitrary",)),
    )(nrm, dgate, dup, *exchange)
    return res if nw else res[0]


def _ffn_bwd_wd(dhb, gate, up, name):
    n = dhb.shape[0]
    tm = _row_tile(n, 512)
    n_i = n // tm

    def body(dhb_ref, gate_ref, up_ref, dw_ref, acc_scr):
        i = pl.program_id(0)

        @pl.when(i == 0)
        def _():
            acc_scr[...] = jnp.zeros_like(acc_scr)

        dhv = dhb_ref[...]
        for a, b in HID_PIECES:
            g = gate_ref[:, a:b].astype(F32)
            act = (g * _sigmoid(g) * up_ref[:, a:b].astype(F32)).astype(BF16)
            acc_scr[a:b, :] += _dot_tn(act, dhv)

        @pl.when(i == n_i - 1)
        def _():
            dw_ref[...] = acc_scr[...].astype(BF16)

    hid = pl.BlockSpec((tm, D_FF), lambda i: (i, 0))
    return pl.pallas_call(
        body, name=name, grid=(n_i,),
        in_specs=[pl.BlockSpec((tm, D_MODEL), lambda i: (i, 0)), hid, hid],
        out_specs=_resident(W_D_SHAPE, 1),
        out_shape=jax.ShapeDtypeStruct(W_D_SHAPE, BF16),
        scratch_shapes=[pltpu.VMEM(W_D_SHAPE, F32)],
        compiler_params=_params(("arbitrary",)),
    )(dhb, gate, up)


N_PIECE = IN_MAIN // 512


def _inproj_fwd(h, gain, w_in):
    n = h.shape[0]
    tm = _row_tile(n, 512)

    def body(h_ref, g_ref, w_ref, *outs):
        y, _, _ = _rms(h_ref[...], g_ref[...])
        nb = y.astype(BF16)
        for p in range(N_PIECE):
            outs[p][...] = _dot_nt(nb, w_ref[512 * p:512 * (p + 1), :]).astype(BF16)
        outs[N_PIECE][...] = _dot_nt(nb, w_ref[IN_MAIN:IN_PAD, :])

    piece = pl.BlockSpec((tm, 512), lambda i: (i, 0))
    return pl.pallas_call(
        body, name="inproj_fwd", grid=(n // tm,),
        in_specs=[pl.BlockSpec((tm, D_MODEL), lambda i: (i, 0)),
                  pl.BlockSpec((1, D_MODEL), lambda i: (0, 0)),
                  pl.BlockSpec((IN_PAD, D_MODEL), lambda i: (0, 0))],
        out_specs=[piece] * N_PIECE + [pl.BlockSpec((tm, 128), lambda i: (i, 0))],
        out_shape=[jax.ShapeDtypeStruct((n, 512), BF16)] * N_PIECE + [jax.ShapeDtypeStruct((n, 128), F32)],
        compiler_params=_params(("parallel",)),
    )(h, gain, w_in)


def _inproj_bwd(dpieces, dfg, dh_out, h_in, gain, w_in):
    n = h_in.shape[0]
    tm = _row_tile(n, 512)
    n_i = n // tm

    def body(*refs):
        dp_refs = refs[:N_PIECE]
        dfg_ref, dh_ref, h_ref, g_ref, w_ref, dhin_ref, dw_ref, dgain_ref, acc_scr = refs[N_PIECE:]
        i = pl.program_id(0)

        @pl.when(i == 0)
        def _():
            acc_scr[...] = jnp.zeros_like(acc_scr)
            dgain_ref[...] = jnp.zeros_like(dgain_ref)

        gain_v = g_ref[...]
        y, xhat, r = _rms(h_ref[...], gain_v)
        nb = y.astype(BF16)
        dn = jnp.zeros((tm, D_MODEL), F32)
        for p in range(N_PIECE + 1):
            lo, hi = (512 * p, 512 * (p + 1)) if p < N_PIECE else (IN_MAIN, IN_PAD)
            dp = (dp_refs[p][...] if p < N_PIECE else dfg_ref[...]).astype(BF16)
            dn = dn + _dot(dp, w_ref[lo:hi, :])
            acc_scr[lo:hi, :] += _dot_tn(dp, nb)
        dhin_ref[...] = dh_ref[...] + _rms_bwd(dn, xhat, r, gain_v)
        dgain_ref[...] += jnp.sum(dn * xhat, axis=0, keepdims=True)

        @pl.when(i == n_i - 1)
        def _():
            dw_ref[...] = acc_scr[...].astype(BF16)

    piece = pl.BlockSpec((tm, 512), lambda i: (i, 0))
    rows = pl.BlockSpec((tm, D_MODEL), lambda i: (i, 0))
    vec = pl.BlockSpec((1, D_MODEL), lambda i: (0, 0))
    wspec = pl.BlockSpec((IN_PAD, D_MODEL), lambda i: (0, 0))
    return pl.pallas_call(
        body, name="inproj_bwd", grid=(n_i,),
        in_specs=[piece] * N_PIECE + [pl.BlockSpec((tm, 128), lambda i: (i, 0)), rows, rows, vec, wspec],
        out_specs=[rows, wspec, vec],
        out_shape=[jax.ShapeDtypeStruct((n, D_MODEL), F32),
                   jax.ShapeDtypeStruct((IN_PAD, D_MODEL), BF16),
                   jax.ShapeDtypeStruct((1, D_MODEL), F32)],
        scratch_shapes=[pltpu.VMEM((IN_PAD, D_MODEL), F32)],
        compiler_params=_params(("arbitrary",)),
    )(*dpieces, dfg, dh_out, h_in, gain, w_in)


def _outproj_fwd(zc, za, w_out, h):
    n = h.shape[0]
    tm = _row_tile(n, 512)

    def body(zc_ref, za_ref, w_ref, h_ref, out_ref):
        out_ref[...] = (h_ref[...] + _dot(zc_ref[...], w_ref[0:CONV_DIM, :])
                        + _dot(za_ref[...], w_ref[CONV_DIM:, :]))

    half = pl.BlockSpec((tm, 512), lambda i: (i, 0))
    rows = pl.BlockSpec((tm, D_MODEL), lambda i: (i, 0))
    return pl.pallas_call(
        body, name="outproj_fwd", grid=(n // tm,),
        in_specs=[half, half, pl.BlockSpec((D_MODEL, D_MODEL), lambda i: (0, 0)), rows],
        out_specs=rows,
        out_shape=jax.ShapeDtypeStruct((n, D_MODEL), F32),
        compiler_params=_params(("parallel",)),
    )(zc, za, w_out, h)


def _outproj_bwd(dh, zc, za, w_out):
    n = dh.shape[0]
    tm = _row_tile(n, 512)
    n_i = n // tm

    def body(dh_ref, zc_ref, za_ref, w_ref, dzc_ref, dza_ref, dw_ref, acc_scr):
        i = pl.program_id(0)

        @pl.when(i == 0)
        def _():
            acc_scr[...] = jnp.zeros_like(acc_scr)

        dhb = dh_ref[...].astype(BF16)
        dzc_ref[...] = _dot_nt(dhb, w_ref[0:CONV_DIM, :]).astype(BF16)
        dza_ref[...] = _dot_nt(dhb, w_ref[CONV_DIM:, :]).astype(BF16)
        acc_scr[0:CONV_DIM, :] += _dot_tn(zc_ref[...], dhb)
        acc_scr[CONV_DIM:, :] += _dot_tn(za_ref[...], dhb)

        @pl.when(i == n_i - 1)
        def _():
            dw_ref[...] = acc_scr[...].astype(BF16)

    half = pl.BlockSpec((tm, 512), lambda i: (i, 0))
    wspec = pl.BlockSpec((D_MODEL, D_MODEL), lambda i: (0, 0))
    return pl.pallas_call(
        body, name="outproj_bwd", grid=(n_i,),
        in_specs=[pl.BlockSpec((tm, D_MODEL), lambda i: (i, 0)), half, half, wspec],
        out_specs=[half, half, wspec],
        out_shape=[jax.ShapeDtypeStruct((n, 512), BF16), jax.ShapeDtypeStruct((n, 512), BF16),
                   jax.ShapeDtypeStruct((D_MODEL, D_MODEL), BF16)],
        scratch_shapes=[pltpu.VMEM((D_MODEL, D_MODEL), F32)],
        compiler_params=_params(("arbitrary",)),
    )(dh, zc, za, w_out)


def _group_matrix():
    r = lax.broadcasted_iota(jnp.int32, (128, 128), 0) // HEAD_DIM
    c = lax.broadcasted_iota(jnp.int32, (128, 128), 1) // HEAD_DIM
    return jnp.where(r == c, 1.0 / HEAD_DIM, 0.0).astype(BF16)


def _group_mean(x, gmat):
    hi = x.astype(BF16)
    lo = (x - hi.astype(F32)).astype(BF16)
    return _dot(hi, gmat) + _dot(lo, gmat)


def _shift_rows(x, s):
    rows = x.shape[0]
    t = lax.broadcasted_iota(jnp.int32, x.shape, 0)
    rolled = pltpu.roll(x, s % rows, 0)
    keep = (t >= s) if s > 0 else (t < rows + s)
    return jnp.where(keep, rolled, 0.0)


def _conv_parts(bg_ref, cg_ref, hc_ref, w_ref):
    bg = bg_ref[...].astype(F32)
    cg = cg_ref[...].astype(F32)
    hc = hc_ref[...].astype(F32)
    u = cg * hc
    u1 = _shift_rows(u, 1)
    u2 = _shift_rows(u, 2)
    conv = w_ref[2:3, :] * u + w_ref[1:2, :] * u1 + w_ref[0:1, :] * u2
    return bg, cg, hc, u, u1, u2, conv


def _conv_fwd(bg, cg, hc, conv_w, gain, gmat, lp):
    n = bg.shape[0]
    nb = n // lp

    def body(bg_ref, cg_ref, hc_ref, w_ref, g_ref, gm_ref, z_ref):
        bgv, _, _, _, _, _, conv = _conv_parts(bg_ref, cg_ref, hc_ref, w_ref)
        yc = bgv * conv
        r = lax.rsqrt(_group_mean(yc * yc, gm_ref[...]) + EPS)
        z_ref[...] = (yc * r * g_ref[...]).astype(BF16)

    blk = pl.BlockSpec((lp, 128), lambda c, b: (b, c))
    return pl.pallas_call(
        body, name="conv_fwd", grid=(CONV_DIM // 128, nb),
        in_specs=[blk, blk, blk, pl.BlockSpec((3, 128), lambda c, b: (0, c)),
                  pl.BlockSpec((1, 128), lambda c, b: (0, c)), pl.BlockSpec((128, 128), lambda c, b: (0, 0))],
        out_specs=blk,
        out_shape=jax.ShapeDtypeStruct((n, CONV_DIM), BF16),
        compiler_params=_params(("parallel", "parallel")),
    )(bg, cg, hc, conv_w, gain, gmat)


def _conv_bwd(dz, bg, cg, hc, conv_w, gain, gmat, lp):
    n = bg.shape[0]
    nb = n // lp

    def body(dz_ref, bg_ref, cg_ref, hc_ref, w_ref, g_ref, gm_ref,
             dbg_ref, dcg_ref, dhc_ref, dw_ref, dgain_ref):
        b = pl.program_id(1)

        @pl.when(b == 0)
        def _():
            dw_ref[...] = jnp.zeros_like(dw_ref)
            dgain_ref[...] = jnp.zeros_like(dgain_ref)

        bgv, cgv, hcv, u, u1, u2, conv = _conv_parts(bg_ref, cg_ref, hc_ref, w_ref)
        gm = gm_ref[...]
        yc = bgv * conv
        r = lax.rsqrt(_group_mean(yc * yc, gm) + EPS)
        yhat = yc * r
        dzv = dz_ref[...].astype(F32)
        dyhat = dzv * g_ref[...]
        dgain_ref[...] += jnp.sum(dzv * yhat, axis=0, keepdims=True)
        dyc = r * (dyhat - yhat * _group_mean(dyhat * yhat, gm))
        dbg_ref[...] = (dyc * conv).astype(BF16)
        dconv = dyc * bgv
        du = (w_ref[2:3, :] * dconv + w_ref[1:2, :] * _shift_rows(dconv, -1)
              + w_ref[0:1, :] * _shift_rows(dconv, -2))
        dcg_ref[...] = (du * hcv).astype(BF16)
        dhc_ref[...] = (du * cgv).astype(BF16)
        dw_ref[0:1, :] += jnp.sum(dconv * u2, axis=0, keepdims=True)
        dw_ref[1:2, :] += jnp.sum(dconv * u1, axis=0, keepdims=True)
        dw_ref[2:3, :] += jnp.sum(dconv * u, axis=0, keepdims=True)

    blk = pl.BlockSpec((lp, 128), lambda c, b: (b, c))
    wspec = pl.BlockSpec((3, 128), lambda c, b: (0, c))
    gspec = pl.BlockSpec((1, 128), lambda c, b: (0, c))
    return pl.pallas_call(
        body, name="conv_bwd", grid=(CONV_DIM // 128, nb),
        in_specs=[blk, blk, blk, blk, wspec, gspec, pl.BlockSpec((128, 128), lambda c, b: (0, 0))],
        out_specs=[blk, blk, blk, wspec, gspec],
        out_shape=[jax.ShapeDtypeStruct((n, CONV_DIM), BF16)] * 3
        + [jax.ShapeDtypeStruct((3, CONV_DIM), F32), jax.ShapeDtypeStruct((1, CONV_DIM), F32)],
        compiler_params=_params(("parallel", "arbitrary")),
    )(dz, bg, cg, hc, conv_w, gain, gmat)


KEY_MASKED = 1e30
ONE_LANE = 24


def _scan_steps(rows):
    s, out = 1, []
    while s < rows:
        out.append(s)
        s *= 2
    return out


def _fgate_fwd(fg, b_f, lp):
    n = fg.shape[0]
    nb = n // lp

    def body(fg_ref, b_ref, ka_ref, qa_ref):
        x = fg_ref[...] + b_ref[...]
        logf = jnp.minimum(x, 0.0) - jnp.log(1.0 + jnp.exp(-jnp.abs(x)))
        t = lax.broadcasted_iota(jnp.int32, (lp, 128), 0)
        lane = lax.broadcasted_iota(jnp.int32, (lp, 128), 1)
        f = jnp.where((t >= PAD) & (lane < N_HEADS), logf, 0.0)
        for s in _scan_steps(lp):
            f = f + _shift_rows(f, s)
        hi = f.astype(BF16).astype(F32)
        rest = f - hi
        mid = rest.astype(BF16).astype(F32)
        lo = (rest - mid).astype(BF16).astype(F32)
        ones = jnp.where((lane >= ONE_LANE) & (lane < ONE_LANE + 3), 1.0, 0.0)
        hi_key = jnp.where((t < PAD) & (lane < N_HEADS), KEY_MASKED, hi)
        ka_ref[...] = (hi_key + pltpu.roll(mid, 8, 1) + pltpu.roll(lo, 16, 1) + ones).astype(BF16)
        for h in range(N_HEADS):
            minus = jnp.where((lane == h) | (lane == 8 + h) | (lane == 16 + h), -1.0, 0.0)
            terms = (jnp.where(lane == ONE_LANE, pltpu.roll(hi, ONE_LANE - h, 1), 0.0)
                     + jnp.where(lane == ONE_LANE + 1, pltpu.roll(mid, ONE_LANE + 1 - h, 1), 0.0)
                     + jnp.where(lane == ONE_LANE + 2, pltpu.roll(lo, ONE_LANE + 2 - h, 1), 0.0))
            qa_ref[:, 128 * h:128 * (h + 1)] = (minus + terms).astype(BF16)

    return pl.pallas_call(
        body, name="fgate_fwd", grid=(nb,),
        in_specs=[pl.BlockSpec((lp, 128), lambda b: (b, 0)), pl.BlockSpec((1, 128), lambda b: (0, 0))],
        out_specs=[pl.BlockSpec((lp, 128), lambda b: (b, 0)), pl.BlockSpec((lp, N_HEADS * 128), lambda b: (b, 0))],
        out_shape=[jax.ShapeDtypeStruct((n, 128), BF16), jax.ShapeDtypeStruct((n, N_HEADS * 128), BF16)],
        compiler_params=_params(("parallel",)),
    )(fg, b_f)


def _fgate_bwd(dka, dfr, fg, b_f, lp):
    n = fg.shape[0]
    nb = n // lp

    def body(dka_ref, dfr_ref, fg_ref, b_ref, dfg_ref, db_ref):
        b = pl.program_id(0)

        @pl.when(b == 0)
        def _():
            db_ref[...] = jnp.zeros_like(db_ref)

        wide = jnp.concatenate([dfr_ref[0], jnp.zeros((128 - N_HEADS, lp), F32)], axis=0)
        t = lax.broadcasted_iota(jnp.int32, (lp, 128), 0)
        lane = lax.broadcasted_iota(jnp.int32, (lp, 128), 1)
        d = jnp.where(lane < N_HEADS, dka_ref[...], 0.0) + wide.T
        for s in _scan_steps(lp):
            d = d + _shift_rows(d, -s)
        x = fg_ref[...] + b_ref[...]
        dx = jnp.where((t >= PAD) & (lane < N_HEADS), d * _sigmoid(-x), 0.0)
        dfg_ref[...] = dx
        db_ref[...] += jnp.sum(dx, axis=0, keepdims=True)

    return pl.pallas_call(
        body, name="fgate_bwd", grid=(nb,),
        in_specs=[pl.BlockSpec((lp, 128), lambda b: (b, 0)), pl.BlockSpec((1, N_HEADS, lp), lambda b: (b, 0, 0)),
                  pl.BlockSpec((lp, 128), lambda b: (b, 0)), pl.BlockSpec((1, 128), lambda b: (0, 0))],
        out_specs=[pl.BlockSpec((lp, 128), lambda b: (b, 0)), pl.BlockSpec((1, 128), lambda b: (0, 0))],
        out_shape=[jax.ShapeDtypeStruct((n, 128), F32), jax.ShapeDtypeStruct((1, 128), F32)],
        compiler_params=_params(("arbitrary",)),
    )(dka, dfr, fg, b_f)


def _head_masks():
    lane = lax.broadcasted_iota(jnp.int32, (1, 128), 1)
    return lane < HEAD_DIM


def _stack_heads(x2, first):
    zero = jnp.zeros_like(x2)
    return jnp.concatenate([jnp.where(first, x2, zero), jnp.where(first, zero, x2)], axis=0)


def _stack_heads_lanes(xt):
    r = lax.broadcasted_iota(jnp.int32, xt.shape, 0)
    zero = jnp.zeros_like(xt)
    return jnp.concatenate([jnp.where(r < HEAD_DIM, xt, zero), jnp.where(r < HEAD_DIM, zero, xt)], axis=1)


def _pair_cols(col0, col1, first):
    return jnp.where(first, col0, col1)


def _pair_rows(row0, row1):
    r = lax.broadcasted_iota(jnp.int32, (128, TQ), 0)
    return jnp.where(r < HEAD_DIM, row0, row1)


def _query_side(q_ref, qa_ref, p, first):
    q2 = q_ref[:, 128 * p:128 * (p + 1)] * 0.125
    zero = jnp.zeros_like(q2)
    top = jnp.concatenate([jnp.where(first, q2, zero), qa_ref[:, 128 * (2 * p):128 * (2 * p + 1)]], axis=1)
    bot = jnp.concatenate([jnp.where(first, zero, q2), qa_ref[:, 128 * (2 * p + 1):128 * (2 * p + 2)]], axis=1)
    return jnp.concatenate([top, bot], axis=0)


def _key_chunks(lp):
    return (lp + TK - 1) // TK


def _chunk_mask(i, c, tk):
    r = lax.broadcasted_iota(jnp.int32, (tk, 2 * TQ), 0)
    col = lax.broadcasted_iota(jnp.int32, (tk, 2 * TQ), 1)
    return (c * TK + r) <= (i * TQ + (col & (TQ - 1)))


def _causal_sweep(i, step, init):
    per = TK // TQ
    last = i // per
    carry = lax.fori_loop(0, last, lambda c, carry: step(c, carry, False, TK), init)
    return lax.cond((i % per) < per // 2,
                    lambda carry: step(last, carry, True, TK // 2),
                    lambda carry: step(last, carry, True, TK), carry)


def _transpose_bf16(x):
    return x.astype(F32).T.astype(BF16)


def _attn_fwd(q, qa, k, v, ka, gain, zc, w_out, h, lp):
    n = q.shape[0]
    nb = n // lp
    nq = lp // TQ
    lpp = _key_chunks(lp) * TK

    def body(q_ref, qa_ref, k_ref, v_ref, ka_ref, g_ref, zc_ref, w_ref, h_ref,
             z_ref, o_ref, lse_ref, hout_ref, kx_scr, vt_scr):
        i = pl.program_id(1)
        first = _head_masks()

        @pl.when(i == 0)
        def _():
            if lpp > lp:
                kx_scr[lp:lpp, :] = jnp.zeros((lpp - lp, 2 * ATTN_DIM), BF16)
                vt_scr[:, lp:lpp] = jnp.zeros((ATTN_DIM, lpp - lp), BF16)
            for p in range(N_PAIRS):
                kx_scr[0:lp, 256 * p:256 * p + 128] = k_ref[:, 128 * p:128 * (p + 1)]
                kx_scr[0:lp, 256 * p + 128:256 * (p + 1)] = ka_ref[...]
            vt_scr[:, 0:lp] = _transpose_bf16(v_ref[...])

        rhs_t = [_transpose_bf16(_query_side(q_ref, qa_ref, p, first)) for p in range(N_PAIRS)]

        def step(c, carry, masked, tk):
            koff = pl.multiple_of(c * TK, TK)
            valid = _chunk_mask(i, c, tk) if masked else None
            new = []
            for p in range(N_PAIRS):
                m, l, acc = carry[p]
                st = _dot(kx_scr[pl.ds(koff, tk), 256 * p:256 * (p + 1)], rhs_t[p])
                if masked:
                    st = jnp.where(valid, st, NEG)
                m_new = jnp.maximum(m, jnp.max(st, axis=0, keepdims=True))
                pt = jnp.exp(st - m_new)
                alpha = jnp.exp(m - m_new)
                l = alpha * l + jnp.sum(pt, axis=0, keepdims=True)
                pb = pt.astype(BF16)
                vt = _stack_heads_lanes(vt_scr[128 * p:128 * (p + 1), pl.ds(koff, tk)])
                pv = _dot(vt, jnp.concatenate([pb[:, 0:TQ], pb[:, TQ:]], axis=0))
                acc = acc * _pair_rows(alpha[:, 0:TQ], alpha[:, TQ:]) + pv
                new.append((m_new, l, acc))
            return tuple(new)

        init = tuple((jnp.full((1, 2 * TQ), NEG, F32), jnp.zeros((1, 2 * TQ), F32), jnp.zeros((128, TQ), F32))
                     for _ in range(N_PAIRS))
        final = _causal_sweep(i, step, init)

        row = lax.broadcasted_iota(jnp.int32, (TQ, 128), 0)
        real = (i * TQ + row) >= PAD
        hout = h_ref[...] + _dot(zc_ref[...], w_ref[0:CONV_DIM, :])
        for p in range(N_PAIRS):
            m, l, acc = final[p]
            inv = 1.0 / l
            ot = acc * _pair_rows(inv[:, 0:TQ], inv[:, TQ:])
            sq = ot * ot
            r0 = lax.rsqrt(jnp.sum(sq[0:HEAD_DIM], axis=0, keepdims=True) * (1.0 / HEAD_DIM) + EPS)
            r1 = lax.rsqrt(jnp.sum(sq[HEAD_DIM:], axis=0, keepdims=True) * (1.0 / HEAD_DIM) + EPS)
            cols = slice(128 * p, 128 * (p + 1))
            o_ref[:, cols] = jnp.where(real, ot.T, 0.0).astype(BF16)
            z = (jnp.where(real, (ot * _pair_rows(r0, r1)).T, 0.0) * g_ref[:, cols]).astype(BF16)
            z_ref[:, cols] = z
            hout = hout + _dot(z, w_ref[CONV_DIM + 128 * p:CONV_DIM + 128 * (p + 1), :])
            lse = m + jnp.log(l)
            lse_ref[0, 2 * p:2 * p + 1, :] = lse[:, 0:TQ]
            lse_ref[0, 2 * p + 1:2 * p + 2, :] = lse[:, TQ:]
        hout_ref[...] = hout

    qblk = pl.BlockSpec((TQ, ATTN_DIM), lambda b, i: (b * nq + i, 0))
    qablk = pl.BlockSpec((TQ, N_HEADS * 128), lambda b, i: (b * nq + i, 0))
    seq = pl.BlockSpec((lp, ATTN_DIM), lambda b, i: (b, 0))
    rowblk = pl.BlockSpec((1, N_HEADS, TQ), lambda b, i: (b, 0, i))
    hblk = pl.BlockSpec((TQ, D_MODEL), lambda b, i: (b * nq + i, 0))
    return pl.pallas_call(
        body, name="attn_fwd", grid=(nb, nq),
        in_specs=[qblk, qablk, seq, seq, pl.BlockSpec((lp, 128), lambda b, i: (b, 0)),
                  pl.BlockSpec((1, ATTN_DIM), lambda b, i: (0, 0)), qblk,
                  pl.BlockSpec((D_MODEL, D_MODEL), lambda b, i: (0, 0)), hblk],
        out_specs=[qblk, qblk, rowblk, hblk],
        out_shape=[jax.ShapeDtypeStruct((n, ATTN_DIM), BF16), jax.ShapeDtypeStruct((n, ATTN_DIM), BF16),
                   jax.ShapeDtypeStruct((nb, N_HEADS, lp), F32), jax.ShapeDtypeStruct((n, D_MODEL), F32)],
        scratch_shapes=[pltpu.VMEM((lpp, 2 * ATTN_DIM), BF16), pltpu.VMEM((ATTN_DIM, lpp), BF16)],
        compiler_params=_params(("parallel", "arbitrary")),
    )(q, qa, k, v, ka, gain, zc, w_out, h)


def _attn_bwd(dz, q, qa, k, v, ka, o, lse, gain, lp, exchange=()):
    n = q.shape[0]
    nb = n // lp
    nq = lp // TQ
    lpp = _key_chunks(lp) * TK
    nw = len(exchange)

    def body(*refs):
        ((dz_ref, q_ref, qa_ref, k_ref, v_ref, ka_ref, o_ref, lse_ref, g_ref), xin,
         (dq_ref, dk_ref, dv_ref, dka_ref, dfr_ref, dgain_ref), xout,
         (kx_scr, vx_scr, kt_scr, dkx_scr, dvx_scr), sems) = _split_refs(refs, 9, nw, 6, 5)
        b = pl.program_id(0)
        i = pl.program_id(1)
        first = _head_masks()
        if nw:
            comm = _Exchange(xin, xout, sems)
            pl.when((b == 0) & (i == 0))(comm.start)

        @pl.when((b == 0) & (i == 0))
        def _():
            dgain_ref[...] = jnp.zeros_like(dgain_ref)

        @pl.when(i == 0)
        def _():
            if lpp > lp:
                kx_scr[lp:lpp, :] = jnp.zeros((lpp - lp, 2 * ATTN_DIM), BF16)
                vx_scr[lp:lpp, :] = jnp.zeros((lpp - lp, ATTN_DIM), BF16)
                kt_scr[:, lp:lpp] = jnp.zeros((ATTN_DIM, lpp - lp), BF16)
            for p in range(N_PAIRS):
                kx_scr[0:lp, 256 * p:256 * p + 128] = k_ref[:, 128 * p:128 * (p + 1)]
                kx_scr[0:lp, 256 * p + 128:256 * (p + 1)] = ka_ref[...]
            vx_scr[0:lp, :] = v_ref[...]
            kt_scr[:, 0:lp] = _transpose_bf16(k_ref[...])
            dkx_scr[...] = jnp.zeros_like(dkx_scr)
            dvx_scr[...] = jnp.zeros_like(dvx_scr)

        rhs, rhs_t, lses, dos, dos_t, deltas = [], [], [], [], [], []
        for p in range(N_PAIRS):
            cols = slice(128 * p, 128 * (p + 1))
            side = _query_side(q_ref, qa_ref, p, first)
            rhs.append(side)
            rhs_t.append(_transpose_bf16(side))
            lses.append(jnp.concatenate([lse_ref[0, 2 * p:2 * p + 1, :], lse_ref[0, 2 * p + 1:2 * p + 2, :]], axis=1))
            ov = o_ref[:, cols].astype(F32)
            dzv = dz_ref[:, cols].astype(F32)
            gv = g_ref[:, cols]
            sq = ov * ov
            ms0 = jnp.sum(jnp.where(first, sq, 0.0), axis=1, keepdims=True) * (1.0 / HEAD_DIM)
            ms1 = jnp.sum(jnp.where(first, 0.0, sq), axis=1, keepdims=True) * (1.0 / HEAD_DIM)
            r = _pair_cols(lax.rsqrt(ms0 + EPS), lax.rsqrt(ms1 + EPS), first)
            ohat = ov * r
            dyhat = dzv * gv
            dgain_ref[:, cols] += jnp.sum(dzv * ohat, axis=0, keepdims=True)
            pr = dyhat * ohat
            mean0 = jnp.sum(jnp.where(first, pr, 0.0), axis=1, keepdims=True) * (1.0 / HEAD_DIM)
            mean1 = jnp.sum(jnp.where(first, 0.0, pr), axis=1, keepdims=True) * (1.0 / HEAD_DIM)
            do = r * (dyhat - ohat * _pair_cols(mean0, mean1, first))
            ddt = (do * ov).T
            deltas.append(jnp.concatenate([jnp.sum(ddt[0:HEAD_DIM], axis=0, keepdims=True),
                                           jnp.sum(ddt[HEAD_DIM:], axis=0, keepdims=True)], axis=1))
            do_st = _stack_heads(do.astype(BF16), first)
            dos.append(do_st)
            dos_t.append(_transpose_bf16(do_st))

        def step(c, carry, masked, tk):
            koff = pl.multiple_of(c * TK, TK)
            valid = _chunk_mask(i, c, tk) if masked else None
            new = []
            for p in range(N_PAIRS):
                dqt, dfq = carry[p]
                ext = slice(256 * p, 256 * (p + 1))
                cols = slice(128 * p, 128 * (p + 1))
                st = _dot(kx_scr[pl.ds(koff, tk), ext], rhs_t[p])
                if masked:
                    st = jnp.where(valid, st, NEG)
                pt = jnp.exp(st - lses[p])
                dpt = _dot(vx_scr[pl.ds(koff, tk), cols], dos_t[p])
                dst = pt * (dpt - deltas[p])
                dsb = dst.astype(BF16)
                dfq = dfq + jnp.sum(dsb.astype(F32), axis=0, keepdims=True)
                dkx_scr[pl.ds(koff, tk), ext] += _dot(dsb, rhs[p])
                dvx_scr[pl.ds(koff, tk), cols] += _dot(pt.astype(BF16), dos[p])
                kt = _stack_heads_lanes(kt_scr[cols, pl.ds(koff, tk)])
                dqt = dqt + _dot(kt, jnp.concatenate([dsb[:, 0:TQ], dsb[:, TQ:]], axis=0))
                new.append((dqt, dfq))
            return tuple(new)

        init = tuple((jnp.zeros((128, TQ), F32), jnp.zeros((1, 2 * TQ), F32)) for _ in range(N_PAIRS))
        final = _causal_sweep(i, step, init)

        for p in range(N_PAIRS):
            dqt, dfq = final[p]
            dq_ref[:, 128 * p:128 * (p + 1)] = (dqt.T * 0.125).astype(BF16)
            dfr_ref[0, 2 * p:2 * p + 1, :] = dfq[:, 0:TQ]
            dfr_ref[0, 2 * p + 1:2 * p + 2, :] = dfq[:, TQ:]

        @pl.when(i == nq - 1)
        def _():
            dka = jnp.zeros((lp, 128), F32)
            for p in range(N_PAIRS):
                dk_ref[:, 128 * p:128 * (p + 1)] = dkx_scr[0:lp, 256 * p:256 * p + 128].astype(BF16)
                dka = dka + dkx_scr[0:lp, 256 * p + 128:256 * (p + 1)]
            dka_ref[...] = dka
            dv_ref[...] = dvx_scr[0:lp, :].astype(BF16)

        if nw:
            pl.when((b == nb - 1) & (i == nq - 1))(comm.finish)

    qblk = pl.BlockSpec((TQ, ATTN_DIM), lambda b, i: (b * nq + i, 0))
    qablk = pl.BlockSpec((TQ, N_HEADS * 128), lambda b, i: (b * nq + i, 0))
    seq = pl.BlockSpec((lp, ATTN_DIM), lambda b, i: (b, 0))
    kaseq = pl.BlockSpec((lp, 128), lambda b, i: (b, 0))
    rowblk = pl.BlockSpec((1, N_HEADS, TQ), lambda b, i: (b, 0, i))
    gspec = pl.BlockSpec((1, ATTN_DIM), lambda b, i: (0, 0))
    return pl.pallas_call(
        body, name="attn_bwd", grid=(nb, nq),
        in_specs=[qblk, qblk, qablk, seq, seq, kaseq, qblk, rowblk, gspec] + [ANY] * nw,
        out_specs=[qblk, seq, seq, kaseq, rowblk, gspec] + [ANY] * nw,
        out_shape=[jax.ShapeDtypeStruct((n, ATTN_DIM), BF16), jax.ShapeDtypeStruct((n, ATTN_DIM), BF16),
                   jax.ShapeDtypeStruct((n, ATTN_DIM), BF16), jax.ShapeDtypeStruct((n, 128), F32),
                   jax.ShapeDtypeStruct((nb, N_HEADS, lp), F32), jax.ShapeDtypeStruct((1, ATTN_DIM), F32)]
        + [jax.ShapeDtypeStruct(a.shape, a.dtype) for a in exchange],
        scratch_shapes=[pltpu.VMEM((lpp, 2 * ATTN_DIM), BF16), pltpu.VMEM((lpp, ATTN_DIM), BF16),
                        pltpu.VMEM((ATTN_DIM, lpp), BF16), pltpu.VMEM((lpp, 2 * ATTN_DIM), F32),
                        pltpu.VMEM((lpp, ATTN_DIM), F32)] + (_comm_sems(nw) if nw else []),
        compiler_params=_params(("arbitrary", "arbitrary")),
    )(dz, q, qa, k, v, ka, o, lse, gain, *exchange)


def _loss_head(h, gain, target, lp):
    n = h.shape[0]
    nb = n // lp
    nq = lp // 128

    def body(h_ref, g_ref, t_ref, loss_ref, dh_ref, dgain_ref):
        b = pl.program_id(0)
        i = pl.program_id(1)

        @pl.when((b == 0) & (i == 0))
        def _():
            loss_ref[...] = jnp.zeros_like(loss_ref)
            dgain_ref[...] = jnp.zeros_like(dgain_ref)

        @pl.when(i == 0)
        def _():
            dh_ref[...] = jnp.zeros_like(dh_ref)

        @pl.when(i > 0)
        def _():
            gain_v = g_ref[...]
            y, xhat, r = _rms(h_ref[...], gain_v)
            err = y - t_ref[...]
            loss_ref[...] += 0.5 * jnp.sum(jnp.sum(err * err, axis=1, keepdims=True), axis=0,
                                           keepdims=True) * (1.0 / D_MODEL)
            dy = err * (1.0 / D_MODEL)
            dh_ref[...] = _rms_bwd(dy, xhat, r, gain_v)
            dgain_ref[...] += jnp.sum(dy * xhat, axis=0, keepdims=True)

    rows = pl.BlockSpec((128, D_MODEL), lambda b, i: (b * nq + i, 0))
    trows = pl.BlockSpec((128, D_MODEL), lambda b, i: (b * (nq - 1) + jnp.maximum(i, 1) - 1, 0))
    return pl.pallas_call(
        body, name="loss_head", grid=(nb, nq),
        in_specs=[rows, pl.BlockSpec((1, D_MODEL), lambda b, i: (0, 0)), trows],
        out_specs=[pl.BlockSpec((1, 1), lambda b, i: (0, 0)), rows, pl.BlockSpec((1, D_MODEL), lambda b, i: (0, 0))],
        out_shape=[jax.ShapeDtypeStruct((1, 1), F32), jax.ShapeDtypeStruct((n, D_MODEL), F32),
                   jax.ShapeDtypeStruct((1, D_MODEL), F32)],
        compiler_params=_params(("arbitrary", "arbitrary")),
    )(h, gain, target)


def _adamw(parts, w, m, v, name):
    s_parts, r, c = parts.shape
    tr = r
    for t in (256, 128, 64, 32, 16):
        if r % t == 0 and r > t:
            tr = t
            break

    def body(p_ref, w_ref, m_ref, v_ref, g_ref, d_ref, nm_ref, nv_ref):
        g = p_ref[0].astype(F32)
        for s in range(1, s_parts):
            g = g + p_ref[s].astype(F32)
        nm = ADAM_B1 * m_ref[...] + (1.0 - ADAM_B1) * g
        nv = ADAM_B2 * v_ref[...] + (1.0 - ADAM_B2) * (g * g)
        m_hat = nm / (1.0 - ADAM_B1 ** ADAM_STEP)
        v_hat = nv / (1.0 - ADAM_B2 ** ADAM_STEP)
        g_ref[...] = g
        d_ref[...] = -ADAM_LR * (m_hat / (jnp.sqrt(v_hat) + ADAM_EPS) + ADAM_WD * w_ref[...])
        nm_ref[...] = nm
        nv_ref[...] = nv

    blk = pl.BlockSpec((tr, c), lambda i: (i, 0))
    return pl.pallas_call(
        body, name=name, grid=(r // tr,),
        in_specs=[pl.BlockSpec((s_parts, tr, c), lambda i: (0, i, 0)), blk, blk, blk],
        out_specs=[blk] * 4,
        out_shape=[jax.ShapeDtypeStruct((r, c), F32)] * 4,
        compiler_params=_params(("parallel",)),
    )(parts, w, m, v)


def _sum_parts(parts, name):
    s_parts, r, c = parts.shape

    def body(p_ref, out_ref):
        acc = p_ref[0]
        for s in range(1, s_parts):
            acc = acc + p_ref[s]
        out_ref[...] = acc

    return pl.pallas_call(
        body, name=name, out_shape=jax.ShapeDtypeStruct((r, c), F32),
        in_specs=[pl.BlockSpec(memory_space=pltpu.VMEM)], out_specs=pl.BlockSpec(memory_space=pltpu.VMEM),
    )(parts)


SMALL_ROWS = 184


def _pack_small(d_gains, d_gc, d_ga, d_bf, d_conv, d_meta):
    rows = [g.reshape(8, 128) for g in d_gains]
    rows += [d_gc.reshape(4, 128), d_ga.reshape(4, 128), d_bf.reshape(1, 128)]
    rows += [d_conv.reshape(12, 128), d_meta.reshape(128, 128)]
    packed = jnp.concatenate(rows, axis=0)
    return jnp.pad(packed, ((0, SMALL_ROWS - packed.shape[0]), (0, 0)))


def kernel(x, meta_tokens, ffn1_norm, ffn1_w_gu, ffn1_w_down, mix_norm, w_in, conv_w, b_f, out_norm_conv, out_norm_attn, w_out, ffn2_norm, ffn2_w_gu, ffn2_w_down, final_norm, loss_target, m_meta_tokens, m_ffn1_norm, m_ffn1_w_gu, m_ffn1_w_down, m_mix_norm, m_w_in, m_conv_w, m_b_f, m_out_norm_conv, m_out_norm_attn, m_w_out, m_ffn2_norm, m_ffn2_w_gu, m_ffn2_w_down, m_final_norm, v_meta_tokens, v_ffn1_norm, v_ffn1_w_gu, v_ffn1_w_down, v_mix_norm, v_w_in, v_conv_w, v_b_f, v_out_norm_conv, v_out_norm_attn, v_w_out, v_ffn2_norm, v_ffn2_w_gu, v_ffn2_w_down, v_final_norm):
    nb, seq, _ = x.shape
    lp = PAD + N_META + seq
    n = nb * lp
    me = 4 * lax.axis_index("x") + 2 * lax.axis_index("y") + lax.axis_index("c")

    small_in = jnp.concatenate(
        [meta_tokens, jnp.pad(conv_w[0], ((0, 0), (0, 128 - conv_w.shape[2]))), jnp.zeros((5, 128), F32)], axis=0)
    wgu1_8, wd1_8, small_8 = _all_gather(
        [ffn1_w_gu[0].T.astype(BF16), ffn1_w_down[0].astype(BF16), small_in], "gather_ffn1")
    meta_full = small_8[:, 0:N_META, :].transpose(1, 0, 2).reshape(N_META, D_MODEL)
    conv_full = small_8[:, N_META:N_META + 3, 0:CONV_DIM // N_DEV].transpose(1, 0, 2).reshape(3, CONV_DIM)
    wgu1 = wgu1_8.reshape(W_GU_SHAPE)
    wd1 = wd1_8.reshape(W_D_SHAPE)
    b_f_row = jnp.pad(b_f, ((0, 0), (0, 128 - N_HEADS)))
    gmat = _group_matrix()

    x2d = x.reshape(nb * seq, D_MODEL)
    later = [w_in[0].T.astype(BF16), w_out[0].astype(BF16), ffn2_w_gu[0].T.astype(BF16), ffn2_w_down[0].astype(BF16)]
    h1, n1, gate1, up1, win_8, wout_8, wgu2_8, wd2_8 = _ffn_fwd_tokens(
        x2d, meta_full, lp, ffn1_norm, wgu1, wd1, "ffn1_fwd", gather=later)
    wgu2 = wgu2_8.reshape(W_GU_SHAPE)
    wd2 = wd2_8.reshape(W_D_SHAPE)
    w_in_full = jnp.pad(win_8.reshape(IN_DIM, D_MODEL), ((0, IN_PAD - IN_DIM), (0, 0)))
    w_out_full = wout_8.reshape(D_MODEL, D_MODEL)

    bg, cg, hc, q, k, v, fg = _inproj_fwd(h1, mix_norm, w_in_full)
    zc = _conv_fwd(bg, cg, hc, conv_full, out_norm_conv, gmat, lp)
    ka, qa = _fgate_fwd(fg, b_f_row, lp)
    za, o, lse, h2 = _attn_fwd(q, qa, k, v, ka, out_norm_attn, zc, w_out_full, h1, lp)
    dh3, n3, gate2, up2, loss_part, d_final = _ffn_fwd_loss(
        h2, ffn2_norm, wgu2, wd2, final_norm.reshape(1, D_MODEL), loss_target.reshape(nb * seq, D_MODEL), lp,
        "ffn2_fwd_loss")

    dgate2, dup2, dwd2 = _ffn_bwd_act_wd(dh3, gate2, up2, wd2, "ffn2_bwd_act")
    dh2, d_ffn2 = _ffn_bwd_in(dh3, h2, ffn2_norm, dgate2, dup2, wgu2, "ffn2_bwd_in")
    dwgu2 = _ffn_bwd_wgu(n3, dgate2, dup2, "ffn2_bwd_wgu")
    dzc, dza, dwout = _outproj_bwd(dh2, zc, za, w_out_full)
    send_a = [dwgu2.reshape(N_DEV, F_CHUNK, D_MODEL), dwd2.reshape(N_DEV, F_CHUNK // 2, D_MODEL),
              dwout.reshape(N_DEV, D_MODEL // N_DEV, D_MODEL)]
    dq, dk, dv, dka, dfr, d_ga, p_wgu2, p_wd2, p_wout = _attn_bwd(
        dza, q, qa, k, v, ka, o, lse, out_norm_attn, lp, exchange=send_a)
    dfg, d_bf = _fgate_bwd(dka, dfr, fg, b_f_row, lp)
    dbg, dcg, dhc, d_conv, d_gc = _conv_bwd(dzc, bg, cg, hc, conv_full, out_norm_conv, gmat, lp)
    dh1, dwin, d_mix = _inproj_bwd([dbg, dcg, dhc, dq, dk, dv], dfg, dh2, h1, mix_norm, w_in_full)
    dwin_8 = dwin[0:IN_DIM].reshape(N_DEV, IN_DIM // N_DEV, D_MODEL)
    dgate1, dup1, dwd1, p_win = _ffn_bwd_act_wd(dh1, gate1, up1, wd1, "ffn1_bwd_act", exchange=[dwin_8])
    dwgu1, p_wd1 = _ffn_bwd_wgu(n1, dgate1, dup1, "ffn1_bwd_wgu",
                                exchange=[dwd1.reshape(N_DEV, F_CHUNK // 2, D_MODEL)])
    own = dwgu1.reshape(N_DEV, F_CHUNK, D_MODEL)
    (got,) = _pair_exchange([own], "pair_exchange_ffn1")
    chip_sum = _pair_sum(own, got, "pair_sum_wgu1")
    dh0, d_ffn1, p_wgu1 = _ffn_bwd_in_tokens(
        dh1, x2d, meta_full, lp, ffn1_norm, dgate1, dup1, wgu1, "ffn1_bwd_in", exchange=[chip_sum])

    dh0 = dh0.reshape(nb, lp, D_MODEL)
    grad_x = dh0[:, PAD + N_META:, :]
    d_meta = jnp.sum(dh0[:, PAD:PAD + N_META, :], axis=0)

    small = _pack_small([d_ffn1, d_mix, d_ffn2, d_final], d_gc, d_ga, d_bf, d_conv, d_meta)
    (small_all,) = _all_gather([small], "gather_small_grads")
    small_sum = _sum_parts(small_all, "sum_small_grads")
    g_ffn1n, g_mixn, g_ffn2n, g_finaln = (small_sum[8 * t:8 * t + 8].reshape(1, D_MODEL) for t in range(4))
    g_gc = small_sum[32:36].reshape(1, CONV_DIM)
    g_ga = small_sum[36:40].reshape(1, ATTN_DIM)
    g_bf = small_sum[40:41, 0:N_HEADS]
    g_conv_full = small_sum[41:53].reshape(3, CONV_DIM)
    g_meta_full = small_sum[53:181].reshape(N_META, D_MODEL)
    g_conv = lax.dynamic_slice_in_dim(g_conv_full, me * (CONV_DIM // N_DEV), CONV_DIM // N_DEV, axis=1)
    g_meta = lax.dynamic_slice_in_dim(g_meta_full, me * (D_MODEL // N_DEV), D_MODEL // N_DEV, axis=1)

    weights = {
        "meta_tokens": (g_meta[None], meta_tokens, m_meta_tokens, v_meta_tokens),
        "ffn1_norm": (g_ffn1n[None], ffn1_norm, m_ffn1_norm, v_ffn1_norm),
        "ffn1_w_gu": (p_wgu1, ffn1_w_gu[0].T, m_ffn1_w_gu[0].T, v_ffn1_w_gu[0].T),
        "ffn1_w_down": (p_wd1, ffn1_w_down[0], m_ffn1_w_down[0], v_ffn1_w_down[0]),
        "mix_norm": (g_mixn[None], mix_norm, m_mix_norm, v_mix_norm),
        "w_in": (p_win, w_in[0].T, m_w_in[0].T, v_w_in[0].T),
        "conv_w": (g_conv[None], conv_w[0], m_conv_w[0], v_conv_w[0]),
        "b_f": (g_bf[None], b_f, m_b_f, v_b_f),
        "out_norm_conv": (g_gc[None], out_norm_conv, m_out_norm_conv, v_out_norm_conv),
        "out_norm_attn": (g_ga[None], out_norm_attn, m_out_norm_attn, v_out_norm_attn),
        "w_out": (p_wout, w_out[0], m_w_out[0], v_w_out[0]),
        "ffn2_norm": (g_ffn2n[None], ffn2_norm, m_ffn2_norm, v_ffn2_norm),
        "ffn2_w_gu": (p_wgu2, ffn2_w_gu[0].T, m_ffn2_w_gu[0].T, v_ffn2_w_gu[0].T),
        "ffn2_w_down": (p_wd2, ffn2_w_down[0], m_ffn2_w_down[0], v_ffn2_w_down[0]),
        "final_norm": (g_finaln[None], final_norm.reshape(1, D_MODEL), m_final_norm.reshape(1, D_MODEL),
                       v_final_norm.reshape(1, D_MODEL)),
    }
    shapes = {"meta_tokens": meta_tokens.shape, "ffn1_norm": ffn1_norm.shape, "ffn1_w_gu": ffn1_w_gu.shape,
              "ffn1_w_down": ffn1_w_down.shape, "mix_norm": mix_norm.shape, "w_in": w_in.shape,
              "conv_w": conv_w.shape, "b_f": b_f.shape, "out_norm_conv": out_norm_conv.shape,
              "out_norm_attn": out_norm_attn.shape, "w_out": w_out.shape, "ffn2_norm": ffn2_norm.shape,
              "ffn2_w_gu": ffn2_w_gu.shape, "ffn2_w_down": ffn2_w_down.shape, "final_norm": final_norm.shape}
    grads, deltas, new_m, new_v = [], [], [], []
    for name, (p, w, m, vv) in weights.items():
        g, d, nm, nv = _adamw(p, w, m, vv, "adamw_" + name)
        if name in ("ffn1_w_gu", "ffn2_w_gu", "w_in"):
            g, d, nm, nv = g.T, d.T, nm.T, nv.T
        shape = shapes[name]
        grads.append(g.reshape(shape))
        deltas.append(d.reshape(shape))
        new_m.append(nm.reshape(shape))
        new_v.append(nv.reshape(shape))

    loss = lax.psum(loss_part[0, 0], ("x", "y", "c"))
    return (loss, grad_x, *grads, *deltas, *new_m, *new_v)
```

```python
import jax
import jax.numpy as jnp
from jax import lax
from jax.experimental import pallas as pl
from jax.experimental.pallas import tpu as pltpu

F32 = jnp.float32
BF16 = jnp.bfloat16

N_DEV = 8
D_MODEL = 1024
N_META = 16
PAD = 128 - N_META
CONV_DIM = 512
ATTN_DIM = 512
HEAD_DIM = 64
N_HEADS = 8
N_PAIRS = N_HEADS // 2
D_FF = 2816
N_CHUNK = 4
F_CHUNK = D_FF // N_CHUNK
IN_DIM = 3080
IN_PAD = 3200
IN_MAIN = 3072
EPS = 1e-6
NEG = -1e30
TQ = 128
TK = 512
VMEM_LIMIT = 56 * 1024 * 1024

ADAM_LR = 0.001
ADAM_B1 = 0.9
ADAM_B2 = 0.999
ADAM_EPS = 1e-08
ADAM_WD = 0.01
ADAM_STEP = 10

MESH = pl.DeviceIdType.MESH
ANY = pl.BlockSpec(memory_space=pl.ANY)


def _params(sem=None):
    return pltpu.CompilerParams(dimension_semantics=sem, vmem_limit_bytes=VMEM_LIMIT)


def _row_tile(n, prefer):
    for t in (prefer, 512, 256, 128):
        if t <= n and n % t == 0:
            return t
    raise ValueError(f"no row tile for {n}")


def _dot(a, b):
    return jnp.dot(a, b, preferred_element_type=F32)


def _dot_nt(a, b):
    return lax.dot_general(a, b, (((1,), (1,)), ((), ())), preferred_element_type=F32)


def _dot_tn(a, b):
    return lax.dot_general(a, b, (((0,), (0,)), ((), ())), preferred_element_type=F32)


def _rms(x, g):
    r = lax.rsqrt(jnp.mean(x * x, axis=-1, keepdims=True) + EPS)
    xhat = x * r
    return xhat * g, xhat, r


def _rms_bwd(dn, xhat, r, g):
    dxhat = dn * g
    return r * (dxhat - xhat * jnp.mean(dxhat * xhat, axis=-1, keepdims=True))


def _sigmoid(x):
    return 1.0 / (1.0 + jnp.exp(-x))


def _place():
    return lax.axis_index("x"), lax.axis_index("y"), lax.axis_index("c")


def _comm_sems(nw):
    return [pltpu.SemaphoreType.DMA((nw, 7)), pltpu.SemaphoreType.DMA((nw, 7)), pltpu.SemaphoreType.DMA((nw,))]


class _Gather:
    def __init__(self, ins, outs, sems):
        self.ins, self.outs = ins, outs
        self.send, self.recv, self.local = sems
        x, y, c = _place()
        self.c = c
        self.me, self.sibling = (x, y, c), (x, y, 1 - c)
        self.chips = [(1 - x, y), (x, 1 - y), (1 - x, 1 - y)]

    def _copy(self, w, k, block, to, own=False):
        slot = self.outs[w].at[4 * block[0] + 2 * block[1] + block[2]]
        return pltpu.make_async_remote_copy(
            src_ref=self.ins[w] if own else slot, dst_ref=slot,
            send_sem=self.send.at[w, k], recv_sem=self.recv.at[w, k], device_id=to, device_id_type=MESH)

    def _mine(self, w):
        x, y, c = self.me
        return pltpu.make_async_copy(self.ins[w], self.outs[w].at[4 * x + 2 * y + c], self.local.at[w])

    def _first(self, w):
        return ([self._copy(w, 0, self.me, self.sibling, own=True)]
                + [self._copy(w, 1 + j, self.me, (*chip, self.c), own=True) for j, chip in enumerate(self.chips)])

    def _passed(self, w):
        return [self._copy(w, 4 + j, (*chip, self.c), self.sibling) for j, chip in enumerate(self.chips)]

    def start(self):
        for w in range(len(self.ins)):
            self._mine(w).start()
        for w in range(len(self.ins)):
            for cp in self._first(w):
                cp.start()

    def forward(self):
        for w in range(len(self.ins)):
            for j, chip in enumerate(self.chips):
                self._copy(w, 1 + j, (*chip, self.c), self.me).wait_recv()
                self._passed(w)[j].start()

    def finish(self):
        for w in range(len(self.ins)):
            self._copy(w, 0, self.sibling, self.me).wait_recv()
            for j, chip in enumerate(self.chips):
                self._copy(w, 4 + j, (*chip, 1 - self.c), self.me).wait_recv()
        for w in range(len(self.ins)):
            for cp in self._first(w) + self._passed(w):
                cp.wait_send()
            self._mine(w).wait()


class _Exchange:
    def __init__(self, ins, outs, sems):
        self.ins, self.outs = ins, outs
        self.send, self.recv, self.local = sems
        self.x, self.y, self.c = _place()
        self.me = 4 * self.x + 2 * self.y + self.c

    def _copy(self, w, k):
        flip = lambda v, bit: 1 - v if bit else v
        peer = (flip(self.x, ((k + 1) >> 2) & 1), flip(self.y, ((k + 1) >> 1) & 1), flip(self.c, (k + 1) & 1))
        return pltpu.make_async_remote_copy(
            src_ref=self.ins[w].at[4 * peer[0] + 2 * peer[1] + peer[2]], dst_ref=self.outs[w].at[self.me],
            send_sem=self.send.at[w, k], recv_sem=self.recv.at[w, k], device_id=peer, device_id_type=MESH)

    def _mine(self, w):
        return pltpu.make_async_copy(self.ins[w].at[self.me], self.outs[w].at[self.me], self.local.at[w])

    def start(self):
        for w in range(len(self.ins)):
            self._mine(w).start()
            for k in range(N_DEV - 1):
                self._copy(w, k).start()

    def finish(self):
        for w in range(len(self.ins)):
            for k in range(N_DEV - 1):
                self._copy(w, k).wait()
            self._mine(w).wait()


class _PairExchange:
    def __init__(self, ins, outs, sems):
        self.ins, self.outs = ins, outs
        self.send, self.recv, _ = sems
        x, y, self.c = _place()
        self.sibling = (x, y, 1 - self.c)

    def _copy(self, w, t):
        return pltpu.make_async_remote_copy(
            src_ref=self.ins[w].at[2 * t + 1 - self.c], dst_ref=self.outs[w].at[t],
            send_sem=self.send.at[w, t], recv_sem=self.recv.at[w, t], device_id=self.sibling, device_id_type=MESH)

    def start(self):
        for w in range(len(self.ins)):
            for t in range(4):
                self._copy(w, t).start()

    def finish(self):
        for w in range(len(self.ins)):
            for t in range(4):
                self._copy(w, t).wait()


class _ChipExchange:
    def __init__(self, ins, outs, sems):
        self.ins, self.outs = ins, outs
        self.send, self.recv, self.local = sems
        self.x, self.y, self.c = _place()
        self.chip = 2 * self.x + self.y

    def _copy(self, w, k):
        flip = lambda v, bit: 1 - v if bit else v
        px, py = flip(self.x, ((k + 1) >> 1) & 1), flip(self.y, (k + 1) & 1)
        return pltpu.make_async_remote_copy(
            src_ref=self.ins[w].at[2 * px + py], dst_ref=self.outs[w].at[self.chip],
            send_sem=self.send.at[w, k], recv_sem=self.recv.at[w, k], device_id=(px, py, self.c),
            device_id_type=MESH)

    def _mine(self, w):
        return pltpu.make_async_copy(self.ins[w].at[self.chip], self.outs[w].at[self.chip], self.local.at[w])

    def start(self):
        for w in range(len(self.ins)):
            self._mine(w).start()
            for k in range(3):
                self._copy(w, k).start()

    def finish(self):
        for w in range(len(self.ins)):
            for k in range(3):
                self._copy(w, k).wait()
            self._mine(w).wait()


def _pair_exchange(xs, name):
    nw = len(xs)

    def body(*refs):
        comm = _PairExchange(refs[:nw], refs[nw:2 * nw], refs[2 * nw:])
        comm.start()
        comm.finish()

    return pl.pallas_call(
        body, name=name, in_specs=[ANY] * nw, out_specs=[ANY] * nw,
        out_shape=[jax.ShapeDtypeStruct((4,) + a.shape[1:], a.dtype) for a in xs],
        scratch_shapes=_comm_sems(nw),
    )(*xs)


def _pair_sum(own, got, name):
    _, r, c = own.shape
    tr = r
    for t in (256, 128, 64, 32, 16):
        if r % t == 0 and r > t:
            tr = t
            break

    def body(own_ref, got_ref, out_ref):
        mine = jnp.where(lax.axis_index("c") == 0, own_ref[:, 0].astype(F32), own_ref[:, 1].astype(F32))
        out_ref[...] = (mine + got_ref[...].astype(F32)).astype(BF16)

    return pl.pallas_call(
        body, name=name, grid=(r // tr,),
        in_specs=[pl.BlockSpec((4, 2, tr, c), lambda i: (0, 0, i, 0)), pl.BlockSpec((4, tr, c), lambda i: (0, i, 0))],
        out_specs=pl.BlockSpec((4, tr, c), lambda i: (0, i, 0)),
        out_shape=jax.ShapeDtypeStruct((4, r, c), BF16),
        compiler_params=_params(("parallel",)),
    )(own.reshape(4, 2, r, c), got)


def _split_refs(refs, n_in, n_comm, n_out, n_scr):
    a = n_in
    b = a + n_comm
    c = b + n_out
    d = c + n_comm
    e = d + n_scr
    return refs[:a], refs[a:b], refs[b:c], refs[c:d], refs[d:e], refs[e:]


def _all_gather(xs, name):
    nw = len(xs)

    def body(*refs):
        comm = _Gather(refs[:nw], refs[nw:2 * nw], refs[2 * nw:])
        comm.start()
        comm.forward()
        comm.finish()

    return pl.pallas_call(
        body, name=name, in_specs=[ANY] * nw, out_specs=[ANY] * nw,
        out_shape=[jax.ShapeDtypeStruct((N_DEV,) + a.shape, a.dtype) for a in xs],
        scratch_shapes=_comm_sems(nw),
    )(*xs)


def _ffn_fwd(h, gain, wgu, wd, name, gather=()):
    n = h.shape[0]
    tm = _row_tile(n, 512)
    n_i = n // tm
    nw = len(gather)

    def body(*refs):
        (h_ref, g_ref, wgu_ref, wd_ref), gin, (out_ref, gate_ref, up_ref), gout, (n_scr, acc_scr), sems = \
            _split_refs(refs, 4, nw, 3, 2)
        i = pl.program_id(0)
        j = pl.program_id(1)
        if nw:
            comm = _Gather(gin, gout, sems)
            pl.when((i == 0) & (j == 0))(comm.start)
            pl.when((i == (3 * n_i) // 4) & (j == 0))(comm.forward)

        @pl.when(j == 0)
        def _():
            y, _, _ = _rms(h_ref[...], g_ref[...])
            n_scr[...] = y.astype(BF16)
            acc_scr[...] = jnp.zeros_like(acc_scr)

        nb = n_scr[...]
        gate = _dot(nb, wgu_ref[0, 0])
        up = _dot(nb, wgu_ref[1, 0])
        gate_ref[0] = gate.astype(BF16)
        up_ref[0] = up.astype(BF16)
        act = (gate * _sigmoid(gate) * up).astype(BF16)
        acc_scr[...] += _dot(act, wd_ref[0])

        @pl.when(j == N_CHUNK - 1)
        def _():
            out_ref[...] = h_ref[...] + 0.5 * acc_scr[...]

        if nw:
            pl.when((i == n_i - 1) & (j == N_CHUNK - 1))(comm.finish)

    return pl.pallas_call(
        body, name=name, grid=(n_i, N_CHUNK),
        in_specs=[pl.BlockSpec((tm, D_MODEL), lambda i, j: (i, 0)),
                  pl.BlockSpec((1, D_MODEL), lambda i, j: (0, 0)),
                  pl.BlockSpec((2, 1, D_MODEL, F_CHUNK), lambda i, j: (0, j, 0, 0)),
                  pl.BlockSpec((1, F_CHUNK, D_MODEL), lambda i, j: (j, 0, 0))] + [ANY] * nw,
        out_specs=[pl.BlockSpec((tm, D_MODEL), lambda i, j: (i, 0)),
                   pl.BlockSpec((1, tm, F_CHUNK), lambda i, j: (j, i, 0)),
                   pl.BlockSpec((1, tm, F_CHUNK), lambda i, j: (j, i, 0))] + [ANY] * nw,
        out_shape=[jax.ShapeDtypeStruct((n, D_MODEL), F32),
                   jax.ShapeDtypeStruct((N_CHUNK, n, F_CHUNK), BF16),
                   jax.ShapeDtypeStruct((N_CHUNK, n, F_CHUNK), BF16)]
        + [jax.ShapeDtypeStruct((N_DEV,) + a.shape, a.dtype) for a in gather],
        scratch_shapes=[pltpu.VMEM((tm, D_MODEL), BF16), pltpu.VMEM((tm, D_MODEL), F32)]
        + (_comm_sems(nw) if nw else []),
        compiler_params=_params(("arbitrary", "arbitrary")),
    )(h, gain, wgu, wd, *gather)


def _ffn_bwd_x(dh_out, h_in, gain, gate, up, wgu, wd, name):
    n = h_in.shape[0]
    tm = _row_tile(n, 512)

    def body(dh_ref, h_ref, g_ref, gate_ref, up_ref, wgu_ref, wd_ref,
             dhin_ref, dgate_ref, dup_ref, dgain_ref, dhb_scr, acc_scr):
        i = pl.program_id(0)
        j = pl.program_id(1)

        @pl.when((i == 0) & (j == 0))
        def _():
            dgain_ref[...] = jnp.zeros_like(dgain_ref)

        @pl.when(j == 0)
        def _():
            dhb_scr[...] = (0.5 * dh_ref[...]).astype(BF16)
            acc_scr[...] = jnp.zeros_like(acc_scr)

        da = _dot_nt(dhb_scr[...], wd_ref[0])
        g = gate_ref[0].astype(F32)
        u = up_ref[0].astype(F32)
        sig = _sigmoid(g)
        dgate = (da * u * (sig * (1.0 + g * (1.0 - sig)))).astype(BF16)
        dup = (da * (g * sig)).astype(BF16)
        dgate_ref[0] = dgate
        dup_ref[0] = dup
        acc_scr[...] += _dot_nt(dgate, wgu_ref[0, 0]) + _dot_nt(dup, wgu_ref[1, 0])

        @pl.when(j == N_CHUNK - 1)
        def _():
            gain_v = g_ref[...]
            _, xhat, r = _rms(h_ref[...], gain_v)
            dn = acc_scr[...]
            dhin_ref[...] = dh_ref[...] + _rms_bwd(dn, xhat, r, gain_v)
            dgain_ref[...] += jnp.sum(dn * xhat, axis=0, keepdims=True)

    chunk = pl.BlockSpec((1, tm, F_CHUNK), lambda i, j: (j, i, 0))
    rows = pl.BlockSpec((tm, D_MODEL), lambda i, j: (i, 0))
    vec = pl.BlockSpec((1, D_MODEL), lambda i, j: (0, 0))
    return pl.pallas_call(
        body, name=name, grid=(n // tm, N_CHUNK),
        in_specs=[rows, rows, vec, chunk, chunk,
                  pl.BlockSpec((2, 1, D_MODEL, F_CHUNK), lambda i, j: (0, j, 0, 0)),
                  pl.BlockSpec((1, F_CHUNK, D_MODEL), lambda i, j: (j, 0, 0))],
        out_specs=[rows, chunk, chunk, vec],
        out_shape=[jax.ShapeDtypeStruct((n, D_MODEL), F32),
                   jax.ShapeDtypeStruct((N_CHUNK, n, F_CHUNK), BF16),
                   jax.ShapeDtypeStruct((N_CHUNK, n, F_CHUNK), BF16),
                   jax.ShapeDtypeStruct((1, D_MODEL), F32)],
        scratch_shapes=[pltpu.VMEM((tm, D_MODEL), BF16), pltpu.VMEM((tm, D_MODEL), F32)],
        compiler_params=_params(("arbitrary", "arbitrary")),
    )(dh_out, h_in, gain, gate, up, wgu, wd)


def _ffn_bwd_act(dh_out, gate, up, wd, name):
    n = dh_out.shape[0]
    tm = _row_tile(n, 512)

    def body(dh_ref, gate_ref, up_ref, wd_ref, dgate_ref, dup_ref, dhb_scr):
        @pl.when(pl.program_id(1) == 0)
        def _():
            dhb_scr[...] = (0.5 * dh_ref[...]).astype(BF16)

        da = _dot_nt(dhb_scr[...], wd_ref[0])
        g = gate_ref[0].astype(F32)
        u = up_ref[0].astype(F32)
        sig = _sigmoid(g)
        dgate_ref[0] = (da * u * (sig * (1.0 + g * (1.0 - sig)))).astype(BF16)
        dup_ref[0] = (da * (g * sig)).astype(BF16)

    chunk = pl.BlockSpec((1, tm, F_CHUNK), lambda i, j: (j, i, 0))
    return pl.pallas_call(
        body, name=name, grid=(n // tm, N_CHUNK),
        in_specs=[pl.BlockSpec((tm, D_MODEL), lambda i, j: (i, 0)), chunk, chunk,
                  pl.BlockSpec((1, F_CHUNK, D_MODEL), lambda i, j: (j, 0, 0))],
        out_specs=[chunk, chunk],
        out_shape=[jax.ShapeDtypeStruct((N_CHUNK, n, F_CHUNK), BF16)] * 2,
        scratch_shapes=[pltpu.VMEM((tm, D_MODEL), BF16)],
        compiler_params=_params(("parallel", "arbitrary")),
    )(dh_out, gate, up, wd)


def _ffn_bwd_in(dh_out, h_in, gain, dgate, dup, wgu, name, exchange=()):
    n = h_in.shape[0]
    tm = _row_tile(n, 512)
    n_i = n // tm
    nw = len(exchange)

    def body(*refs):
        (dh_ref, h_ref, g_ref, dgate_ref, dup_ref, wgu_ref), xin, (dhin_ref, dgain_ref), xout, (acc_scr,), sems = \
            _split_refs(refs, 6, nw, 2, 1)
        i = pl.program_id(0)
        j = pl.program_id(1)
        if nw:
            comm = _Exchange(xin, xout, sems)
            pl.when((i == 0) & (j == 0))(comm.start)

        @pl.when((i == 0) & (j == 0))
        def _():
            dgain_ref[...] = jnp.zeros_like(dgain_ref)

        @pl.when(j == 0)
        def _():
            acc_scr[...] = jnp.zeros_like(acc_scr)

        acc_scr[...] += _dot_nt(dgate_ref[0], wgu_ref[0, 0]) + _dot_nt(dup_ref[0], wgu_ref[1, 0])

        @pl.when(j == N_CHUNK - 1)
        def _():
            gain_v = g_ref[...]
            _, xhat, r = _rms(h_ref[...], gain_v)
            dn = acc_scr[...]
            dhin_ref[...] = dh_ref[...] + _rms_bwd(dn, xhat, r, gain_v)
            dgain_ref[...] += jnp.sum(dn * xhat, axis=0, keepdims=True)

        if nw:
            pl.when((i == n_i - 1) & (j == N_CHUNK - 1))(comm.finish)

    chunk = pl.BlockSpec((1, tm, F_CHUNK), lambda i, j: (j, i, 0))
    rows = pl.BlockSpec((tm, D_MODEL), lambda i, j: (i, 0))
    vec = pl.BlockSpec((1, D_MODEL), lambda i, j: (0, 0))
    return pl.pallas_call(
        body, name=name, grid=(n_i, N_CHUNK),
        in_specs=[rows, rows, vec, chunk, chunk,
                  pl.BlockSpec((2, 1, D_MODEL, F_CHUNK), lambda i, j: (0, j, 0, 0))] + [ANY] * nw,
        out_specs=[rows, vec] + [ANY] * nw,
        out_shape=[jax.ShapeDtypeStruct((n, D_MODEL), F32), jax.ShapeDtypeStruct((1, D_MODEL), F32)]
        + [jax.ShapeDtypeStruct(a.shape, a.dtype) for a in exchange],
        scratch_shapes=[pltpu.VMEM((tm, D_MODEL), F32)] + (_comm_sems(nw) if nw else []),
        compiler_params=_params(("arbitrary", "arbitrary")),
    )(dh_out, h_in, gain, dgate, dup, wgu, *exchange)


def _ffn_bwd_w(dh_out, h_in, gain, gate, up, dgate, dup, name):
    n = h_in.shape[0]
    tm = _row_tile(n, 512)
    n_i = n // tm

    def body(dh_ref, h_ref, g_ref, gate_ref, up_ref, dgate_ref, dup_ref, dwgu_ref, dwd_ref,
             ag_scr, au_scr, ad_scr):
        i = pl.program_id(1)

        @pl.when(i == 0)
        def _():
            ag_scr[...] = jnp.zeros_like(ag_scr)
            au_scr[...] = jnp.zeros_like(au_scr)
            ad_scr[...] = jnp.zeros_like(ad_scr)

        y, _, _ = _rms(h_ref[...], g_ref[...])
        nb = y.astype(BF16)
        ag_scr[...] += _dot_tn(nb, dgate_ref[0])
        au_scr[...] += _dot_tn(nb, dup_ref[0])
        g = gate_ref[0].astype(F32)
        act = (g * _sigmoid(g) * up_ref[0].astype(F32)).astype(BF16)
        ad_scr[...] += _dot_tn(act, (0.5 * dh_ref[...]).astype(BF16))

        @pl.when(i == n_i - 1)
        def _():
            dwgu_ref[0, 0] = ag_scr[...].astype(BF16)
            dwgu_ref[1, 0] = au_scr[...].astype(BF16)
            dwd_ref[0] = ad_scr[...].astype(BF16)

    chunk = pl.BlockSpec((1, tm, F_CHUNK), lambda j, i: (j, i, 0))
    rows = pl.BlockSpec((tm, D_MODEL), lambda j, i: (i, 0))
    return pl.pallas_call(
        body, name=name, grid=(N_CHUNK, n_i),
        in_specs=[rows, rows, pl.BlockSpec((1, D_MODEL), lambda j, i: (0, 0)), chunk, chunk, chunk, chunk],
        out_specs=[pl.BlockSpec((2, 1, D_MODEL, F_CHUNK), lambda j, i: (0, j, 0, 0)),
                   pl.BlockSpec((1, F_CHUNK, D_MODEL), lambda j, i: (j, 0, 0))],
        out_shape=[jax.ShapeDtypeStruct((2, N_CHUNK, D_MODEL, F_CHUNK), BF16),
                   jax.ShapeDtypeStruct((N_CHUNK, F_CHUNK, D_MODEL), BF16)],
        scratch_shapes=[pltpu.VMEM((D_MODEL, F_CHUNK), F32), pltpu.VMEM((D_MODEL, F_CHUNK), F32),
                        pltpu.VMEM((F_CHUNK, D_MODEL), F32)],
        compiler_params=_params(("parallel", "arbitrary")),
    )(dh_out, h_in, gain, gate, up, dgate, dup)


def _resident(shape, rank):
    zeros = (0,) * len(shape)
    index_map = (lambda i: zeros) if rank == 1 else (lambda i, j: zeros)
    return pl.BlockSpec(shape, index_map, pipeline_mode=pl.Buffered(1))


W_GU_SHAPE = (2, N_CHUNK, F_CHUNK, D_MODEL)
W_D_SHAPE = (N_CHUNK, F_CHUNK, D_MODEL)


def _ffn_fwd(h, gain, wgu, wd, name, gather=()):
    n = h.shape[0]
    tm = _row_tile(n, 512)
    n_i = n // tm
    nw = len(gather)

    def body(*refs):
        (h_ref, g_ref, wgu_ref, wd_ref), gin, (out_ref, nrm_ref, gate_ref, up_ref), gout, _, sems = \
            _split_refs(refs, 4, nw, 4, 0)
        i = pl.program_id(0)
        if nw:
            comm = _Gather(gin, gout, sems)
            pl.when(i == 0)(comm.start)
            pl.when(i == max(n_i - 3, 0))(comm.forward)

        hv = h_ref[...]
        y, _, _ = _rms(hv, g_ref[...])
        nb = y.astype(BF16)
        nrm_ref[...] = nb
        acc = jnp.zeros((tm, D_MODEL), F32)
        for j in range(N_CHUNK):
            gate = _dot_nt(nb, wgu_ref[0, j])
            up = _dot_nt(nb, wgu_ref[1, j])
            gate_ref[j] = gate.astype(BF16)
            up_ref[j] = up.astype(BF16)
            acc = acc + _dot((gate * _sigmoid(gate) * up).astype(BF16), wd_ref[j])
        out_ref[...] = hv + 0.5 * acc

        if nw:
            pl.when(i == n_i - 1)(comm.finish)

    rows = pl.BlockSpec((tm, D_MODEL), lambda i: (i, 0))
    chunks = pl.BlockSpec((N_CHUNK, tm, F_CHUNK), lambda i: (0, i, 0))
    return pl.pallas_call(
        body, name=name, grid=(n_i,),
        in_specs=[rows, pl.BlockSpec((1, D_MODEL), lambda i: (0, 0)), _resident(W_GU_SHAPE, 1),
                  _resident(W_D_SHAPE, 1)] + [ANY] * nw,
        out_specs=[rows, rows, chunks, chunks] + [ANY] * nw,
        out_shape=[jax.ShapeDtypeStruct((n, D_MODEL), F32), jax.ShapeDtypeStruct((n, D_MODEL), BF16),
                   jax.ShapeDtypeStruct((N_CHUNK, n, F_CHUNK), BF16),
                   jax.ShapeDtypeStruct((N_CHUNK, n, F_CHUNK), BF16)]
        + [jax.ShapeDtypeStruct((N_DEV,) + a.shape, a.dtype) for a in gather],
        scratch_shapes=_comm_sems(nw) if nw else [],
        compiler_params=_params(("arbitrary",)),
    )(h, gain, wgu, wd, *gather)


def _swiglu_bwd(da, gate_ref, up_ref, j):
    g = gate_ref[j].astype(F32)
    u = up_ref[j].astype(F32)
    sig = _sigmoid(g)
    return (da * u * (sig * (1.0 + g * (1.0 - sig)))).astype(BF16), (da * (g * sig)).astype(BF16)


def _ffn_bwd_x(dh_out, h_in, gain, gate, up, wgu, wd, name):
    n = h_in.shape[0]
    tm = _row_tile(n, 256)

    def body(dh_ref, h_ref, g_ref, gate_ref, up_ref, wgu_ref, wd_ref,
             dhin_ref, dhb_ref, dgate_ref, dup_ref, dgain_ref):
        @pl.when(pl.program_id(0) == 0)
        def _():
            dgain_ref[...] = jnp.zeros_like(dgain_ref)

        dhv = dh_ref[...]
        dhb = (0.5 * dhv).astype(BF16)
        dhb_ref[...] = dhb
        dn = jnp.zeros((tm, D_MODEL), F32)
        for j in range(N_CHUNK):
            dgate, dup = _swiglu_bwd(_dot_nt(dhb, wd_ref[j]), gate_ref, up_ref, j)
            dgate_ref[j] = dgate
            dup_ref[j] = dup
            dn = dn + _dot(dgate, wgu_ref[0, j]) + _dot(dup, wgu_ref[1, j])
        gain_v = g_ref[...]
        _, xhat, r = _rms(h_ref[...], gain_v)
        dhin_ref[...] = dhv + _rms_bwd(dn, xhat, r, gain_v)
        dgain_ref[...] += jnp.sum(dn * xhat, axis=0, keepdims=True)

    rows = pl.BlockSpec((tm, D_MODEL), lambda i: (i, 0))
    chunks = pl.BlockSpec((N_CHUNK, tm, F_CHUNK), lambda i: (0, i, 0))
    vec = pl.BlockSpec((1, D_MODEL), lambda i: (0, 0))
    return pl.pallas_call(
        body, name=name, grid=(n // tm,),
        in_specs=[rows, rows, vec, chunks, chunks, _resident(W_GU_SHAPE, 1), _resident(W_D_SHAPE, 1)],
        out_specs=[rows, rows, chunks, chunks, vec],
        out_shape=[jax.ShapeDtypeStruct((n, D_MODEL), F32), jax.ShapeDtypeStruct((n, D_MODEL), BF16),
                   jax.ShapeDtypeStruct((N_CHUNK, n, F_CHUNK), BF16),
                   jax.ShapeDtypeStruct((N_CHUNK, n, F_CHUNK), BF16),
                   jax.ShapeDtypeStruct((1, D_MODEL), F32)],
        compiler_params=_params(("arbitrary",)),
    )(dh_out, h_in, gain, gate, up, wgu, wd)


def _ffn_bwd_act(dh_out, gate, up, wd, name, exchange=()):
    n = dh_out.shape[0]
    tm = _row_tile(n, 512)
    n_i = n // tm
    nw = len(exchange)

    def body(*refs):
        (dh_ref, gate_ref, up_ref, wd_ref), xin, (dhb_ref, dgate_ref, dup_ref), xout, _, sems = \
            _split_refs(refs, 4, nw, 3, 0)
        i = pl.program_id(0)
        if nw:
            comm = _Exchange(xin, xout, sems)
            pl.when(i == 0)(comm.start)

        dhb = (0.5 * dh_ref[...]).astype(BF16)
        dhb_ref[...] = dhb
        for j in range(N_CHUNK):
            dgate_ref[j], dup_ref[j] = _swiglu_bwd(_dot_nt(dhb, wd_ref[j]), gate_ref, up_ref, j)

        if nw:
            pl.when(i == n_i - 1)(comm.finish)

    rows = pl.BlockSpec((tm, D_MODEL), lambda i: (i, 0))
    chunks = pl.BlockSpec((N_CHUNK, tm, F_CHUNK), lambda i: (0, i, 0))
    return pl.pallas_call(
        body, name=name, grid=(n_i,),
        in_specs=[rows, chunks, chunks, _resident(W_D_SHAPE, 1)] + [ANY] * nw,
        out_specs=[rows, chunks, chunks] + [ANY] * nw,
        out_shape=[jax.ShapeDtypeStruct((n, D_MODEL), BF16)] + [jax.ShapeDtypeStruct((N_CHUNK, n, F_CHUNK), BF16)] * 2
        + [jax.ShapeDtypeStruct(a.shape, a.dtype) for a in exchange],
        scratch_shapes=_comm_sems(nw) if nw else [],
        compiler_params=_params(("arbitrary",)),
    )(dh_out, gate, up, wd, *exchange)


def _ffn_bwd_in(dh_out, h_in, gain, dgate, dup, wgu, name, exchange=()):
    n = h_in.shape[0]
    tm = _row_tile(n, 512)
    n_i = n // tm
    nw = len(exchange)

    def body(*refs):
        (dh_ref, h_ref, g_ref, dgate_ref, dup_ref, wgu_ref), xin, (dhin_ref, dgain_ref), xout, _, sems = \
            _split_refs(refs, 6, nw, 2, 0)
        i = pl.program_id(0)
        if nw:
            comm = _ChipExchange(xin, xout, sems)
            pl.when(i == 0)(comm.start)

        @pl.when(i == 0)
        def _():
            dgain_ref[...] = jnp.zeros_like(dgain_ref)

        dn = jnp.zeros((tm, D_MODEL), F32)
        for j in range(N_CHUNK):
            dn = dn + _dot(dgate_ref[j], wgu_ref[0, j]) + _dot(dup_ref[j], wgu_ref[1, j])
        gain_v = g_ref[...]
        _, xhat, r = _rms(h_ref[...], gain_v)
        dhin_ref[...] = dh_ref[...] + _rms_bwd(dn, xhat, r, gain_v)
        dgain_ref[...] += jnp.sum(dn * xhat, axis=0, keepdims=True)

        if nw:
            pl.when(i == n_i - 1)(comm.finish)

    rows = pl.BlockSpec((tm, D_MODEL), lambda i: (i, 0))
    chunks = pl.BlockSpec((N_CHUNK, tm, F_CHUNK), lambda i: (0, i, 0))
    vec = pl.BlockSpec((1, D_MODEL), lambda i: (0, 0))
    return pl.pallas_call(
        body, name=name, grid=(n_i,),
        in_specs=[rows, rows, vec, chunks, chunks, _resident(W_GU_SHAPE, 1)] + [ANY] * nw,
        out_specs=[rows, vec] + [ANY] * nw,
        out_shape=[jax.ShapeDtypeStruct((n, D_MODEL), F32), jax.ShapeDtypeStruct((1, D_MODEL), F32)]
        + [jax.ShapeDtypeStruct(a.shape, a.dtype) for a in exchange],
        scratch_shapes=_comm_sems(nw) if nw else [],
        compiler_params=_params(("arbitrary",)),
    )(dh_out, h_in, gain, dgate, dup, wgu, *exchange)


W_GROUP = 2


def _ffn_bwd_w(dhb, nrm, gate, up, dgate, dup, name):
    n = nrm.shape[0]
    tm = _row_tile(n, 512)
    n_i = n // tm

    def body(dhb_ref, nrm_ref, gate_ref, up_ref, dgate_ref, dup_ref, dwgu_ref, dwd_ref, ag_scr, au_scr, ad_scr):
        i = pl.program_id(1)

        @pl.when(i == 0)
        def _():
            ag_scr[...] = jnp.zeros_like(ag_scr)
            au_scr[...] = jnp.zeros_like(au_scr)
            ad_scr[...] = jnp.zeros_like(ad_scr)

        nb = nrm_ref[...]
        dhv = dhb_ref[...]
        for jj in range(W_GROUP):
            ag_scr[jj] += _dot_tn(dgate_ref[jj], nb)
            au_scr[jj] += _dot_tn(dup_ref[jj], nb)
            g = gate_ref[jj].astype(F32)
            act = (g * _sigmoid(g) * up_ref[jj].astype(F32)).astype(BF16)
            ad_scr[jj] += _dot_tn(act, dhv)

        @pl.when(i == n_i - 1)
        def _():
            dwgu_ref[0] = ag_scr[...].astype(BF16)
            dwgu_ref[1] = au_scr[...].astype(BF16)
            dwd_ref[...] = ad_scr[...].astype(BF16)

    chunks = pl.BlockSpec((W_GROUP, tm, F_CHUNK), lambda g, i: (g, i, 0))
    rows = pl.BlockSpec((tm, D_MODEL), lambda g, i: (i, 0))
    return pl.pallas_call(
        body, name=name, grid=(N_CHUNK // W_GROUP, n_i),
        in_specs=[rows, rows, chunks, chunks, chunks, chunks],
        out_specs=[pl.BlockSpec((2, W_GROUP, F_CHUNK, D_MODEL), lambda g, i: (0, g, 0, 0)),
                   pl.BlockSpec((W_GROUP, F_CHUNK, D_MODEL), lambda g, i: (g, 0, 0))],
        out_shape=[jax.ShapeDtypeStruct(W_GU_SHAPE, BF16), jax.ShapeDtypeStruct(W_D_SHAPE, BF16)],
        scratch_shapes=[pltpu.VMEM((W_GROUP, F_CHUNK, D_MODEL), F32), pltpu.VMEM((W_GROUP, F_CHUNK, D_MODEL), F32),
                        pltpu.VMEM((W_GROUP, F_CHUNK, D_MODEL), F32)],
        compiler_params=_params(("parallel", "arbitrary")),
    )(dhb, nrm, gate, up, dgate, dup)


HID_PIECES = ((0, 1024), (1024, 2048), (2048, D_FF))
W_GU_SHAPE = (2, D_FF, D_MODEL)
W_D_SHAPE = (D_FF, D_MODEL)


def _ffn_fwd(h, gain, wgu, wd, name, gather=()):
    n = h.shape[0]
    tm = _row_tile(n, 512)
    n_i = n // tm
    nw = len(gather)

    def body(*refs):
        (h_ref, g_ref, wgu_ref, wd_ref), gin, (out_ref, nrm_ref, gate_ref, up_ref), gout, _, sems = \
            _split_refs(refs, 4, nw, 4, 0)
        i = pl.program_id(0)
        if nw:
            comm = _Gather(gin, gout, sems)
            pl.when(i == 0)(comm.start)
            pl.when(i == max(n_i - 3, 0))(comm.forward)

        hv = h_ref[...]
        y, _, _ = _rms(hv, g_ref[...])
        nb = y.astype(BF16)
        nrm_ref[...] = nb
        acc = jnp.zeros((tm, D_MODEL), F32)
        for a, b in HID_PIECES:
            gate = _dot_nt(nb, wgu_ref[0, a:b, :])
            up = _dot_nt(nb, wgu_ref[1, a:b, :])
            gate_ref[:, a:b] = gate.astype(BF16)
            up_ref[:, a:b] = up.astype(BF16)
            acc = acc + _dot((gate * _sigmoid(gate) * up).astype(BF16), wd_ref[a:b, :])
        out_ref[...] = hv + 0.5 * acc

        if nw:
            pl.when(i == n_i - 1)(comm.finish)

    rows = pl.BlockSpec((tm, D_MODEL), lambda i: (i, 0))
    hid = pl.BlockSpec((tm, D_FF), lambda i: (i, 0))
    return pl.pallas_call(
        body, name=name, grid=(n_i,),
        in_specs=[rows, pl.BlockSpec((1, D_MODEL), lambda i: (0, 0)), _resident(W_GU_SHAPE, 1),
                  _resident(W_D_SHAPE, 1)] + [ANY] * nw,
        out_specs=[rows, rows, hid, hid] + [ANY] * nw,
        out_shape=[jax.ShapeDtypeStruct((n, D_MODEL), F32), jax.ShapeDtypeStruct((n, D_MODEL), BF16),
                   jax.ShapeDtypeStruct((n, D_FF), BF16), jax.ShapeDtypeStruct((n, D_FF), BF16)]
        + [jax.ShapeDtypeStruct((N_DEV,) + a.shape, a.dtype) for a in gather],
        scratch_shapes=_comm_sems(nw) if nw else [],
        compiler_params=_params(("arbitrary",)),
    )(h, gain, wgu, wd, *gather)


def _swiglu_bwd(da, gate_ref, up_ref, a, b):
    g = gate_ref[:, a:b].astype(F32)
    u = up_ref[:, a:b].astype(F32)
    sig = _sigmoid(g)
    return (da * u * (sig * (1.0 + g * (1.0 - sig)))).astype(BF16), (da * (g * sig)).astype(BF16)


def _ffn_bwd_x(dh_out, h_in, gain, gate, up, wgu, wd, name):
    n = h_in.shape[0]
    tm = _row_tile(n, 256)

    def body(dh_ref, h_ref, g_ref, gate_ref, up_ref, wgu_ref, wd_ref,
             dhin_ref, dhb_ref, dgate_ref, dup_ref, dgain_ref):
        @pl.when(pl.program_id(0) == 0)
        def _():
            dgain_ref[...] = jnp.zeros_like(dgain_ref)

        dhv = dh_ref[...]
        dhb = (0.5 * dhv).astype(BF16)
        dhb_ref[...] = dhb
        dn = jnp.zeros((tm, D_MODEL), F32)
        for a, b in HID_PIECES:
            dgate, dup = _swiglu_bwd(_dot_nt(dhb, wd_ref[a:b, :]), gate_ref, up_ref, a, b)
            dgate_ref[:, a:b] = dgate
            dup_ref[:, a:b] = dup
            dn = dn + _dot(dgate, wgu_ref[0, a:b, :]) + _dot(dup, wgu_ref[1, a:b, :])
        gain_v = g_ref[...]
        _, xhat, r = _rms(h_ref[...], gain_v)
        dhin_ref[...] = dhv + _rms_bwd(dn, xhat, r, gain_v)
        dgain_ref[...] += jnp.sum(dn * xhat, axis=0, keepdims=True)

    rows = pl.BlockSpec((tm, D_MODEL), lambda i: (i, 0))
    hid = pl.BlockSpec((tm, D_FF), lambda i: (i, 0))
    vec = pl.BlockSpec((1, D_MODEL), lambda i: (0, 0))
    return pl.pallas_call(
        body, name=name, grid=(n // tm,),
        in_specs=[rows, rows, vec, hid, hid, _resident(W_GU_SHAPE, 1), _resident(W_D_SHAPE, 1)],
        out_specs=[rows, rows, hid, hid, vec],
        out_shape=[jax.ShapeDtypeStruct((n, D_MODEL), F32), jax.ShapeDtypeStruct((n, D_MODEL), BF16),
                   jax.ShapeDtypeStruct((n, D_FF), BF16), jax.ShapeDtypeStruct((n, D_FF), BF16),
                   jax.ShapeDtypeStruct((1, D_MODEL), F32)],
        compiler_params=_params(("arbitrary",)),
    )(dh_out, h_in, gain, gate, up, wgu, wd)


def _ffn_bwd_act(dh_out, gate, up, wd, name, exchange=()):
    n = dh_out.shape[0]
    tm = _row_tile(n, 512)
    n_i = n // tm
    nw = len(exchange)

    def body(*refs):
        (dh_ref, gate_ref, up_ref, wd_ref), xin, (dhb_ref, dgate_ref, dup_ref), xout, _, sems = \
            _split_refs(refs, 4, nw, 3, 0)
        i = pl.program_id(0)
        if nw:
            comm = _Exchange(xin, xout, sems)
            pl.when(i == 0)(comm.start)

        dhb = (0.5 * dh_ref[...]).astype(BF16)
        dhb_ref[...] = dhb
        for a, b in HID_PIECES:
            dgate_ref[:, a:b], dup_ref[:, a:b] = _swiglu_bwd(_dot_nt(dhb, wd_ref[a:b, :]), gate_ref, up_ref, a, b)

        if nw:
            pl.when(i == n_i - 1)(comm.finish)

    rows = pl.BlockSpec((tm, D_MODEL), lambda i: (i, 0))
    hid = pl.BlockSpec((tm, D_FF), lambda i: (i, 0))
    return pl.pallas_call(
        body, name=name, grid=(n_i,),
        in_specs=[rows, hid, hid, _resident(W_D_SHAPE, 1)] + [ANY] * nw,
        out_specs=[rows, hid, hid] + [ANY] * nw,
        out_shape=[jax.ShapeDtypeStruct((n, D_MODEL), BF16)] + [jax.ShapeDtypeStruct((n, D_FF), BF16)] * 2
        + [jax.ShapeDtypeStruct(a.shape, a.dtype) for a in exchange],
        scratch_shapes=_comm_sems(nw) if nw else [],
        compiler_params=_params(("arbitrary",)),
    )(dh_out, gate, up, wd, *exchange)


def _ffn_bwd_in(dh_out, h_in, gain, dgate, dup, wgu, name, exchange=()):
    n = h_in.shape[0]
    tm = _row_tile(n, 512)
    n_i = n // tm
    nw = len(exchange)

    def body(*refs):
        (dh_ref, h_ref, g_ref, dgate_ref, dup_ref, wgu_ref), xin, (dhin_ref, dgain_ref), xout, _, sems = \
            _split_refs(refs, 6, nw, 2, 0)
        i = pl.program_id(0)
        if nw:
            comm = _ChipExchange(xin, xout, sems)
            pl.when(i == 0)(comm.start)

        @pl.when(i == 0)
        def _():
            dgain_ref[...] = jnp.zeros_like(dgain_ref)

        dn = jnp.zeros((tm, D_MODEL), F32)
        for a, b in HID_PIECES:
            dn = dn + _dot(dgate_ref[:, a:b], wgu_ref[0, a:b, :]) + _dot(dup_ref[:, a:b], wgu_ref[1, a:b, :])
        gain_v = g_ref[...]
        _, xhat, r = _rms(h_ref[...], gain_v)
        dhin_ref[...] = dh_ref[...] + _rms_bwd(dn, xhat, r, gain_v)
        dgain_ref[...] += jnp.sum(dn * xhat, axis=0, keepdims=True)

        if nw:
            pl.when(i == n_i - 1)(comm.finish)

    rows = pl.BlockSpec((tm, D_MODEL), lambda i: (i, 0))
    hid = pl.BlockSpec((tm, D_FF), lambda i: (i, 0))
    vec = pl.BlockSpec((1, D_MODEL), lambda i: (0, 0))
    return pl.pallas_call(
        body, name=name, grid=(n_i,),
        in_specs=[rows, rows, vec, hid, hid, _resident(W_GU_SHAPE, 1)] + [ANY] * nw,
        out_specs=[rows, vec] + [ANY] * nw,
        out_shape=[jax.ShapeDtypeStruct((n, D_MODEL), F32), jax.ShapeDtypeStruct((1, D_MODEL), F32)]
        + [jax.ShapeDtypeStruct(a.shape, a.dtype) for a in exchange],
        scratch_shapes=_comm_sems(nw) if nw else [],
        compiler_params=_params(("arbitrary",)),
    )(dh_out, h_in, gain, dgate, dup, wgu, *exchange)


def _token_spec(k, ksub, nq):
    def index_map(i):
        s = ksub * i + k
        return ((s // nq) * (nq - 1) + jnp.maximum(s % nq, 1) - 1, 0)
    return pl.BlockSpec((128, D_MODEL), index_map)


def _is_lead(i, k, ksub, nq):
    return ((ksub * i + k) % nq) == 0


def _assemble_rows(i, x_refs, meta_ref, nq):
    ksub = len(x_refs)
    lead = jnp.concatenate([jnp.zeros((PAD, D_MODEL), F32), meta_ref[...]], axis=0)
    return jnp.concatenate([jnp.where(_is_lead(i, k, ksub, nq), lead, x_refs[k][...]) for k in range(ksub)], axis=0)


def _ffn_fwd_tokens(x2d, meta, lp, gain, wgu, wd, name, gather=()):
    nq = lp // 128
    n = (x2d.shape[0] // (nq - 1)) * nq
    tm = _row_tile(n, 512)
    ksub = tm // 128
    n_i = n // tm
    nw = len(gather)

    def body(*refs):
        x_refs = refs[:ksub]
        (meta_ref, g_ref, wgu_ref, wd_ref), gin, (out_ref, nrm_ref, gate_ref, up_ref), gout, _, sems = \
            _split_refs(refs[ksub:], 4, nw, 4, 0)
        i = pl.program_id(0)
        if nw:
            comm = _Gather(gin, gout, sems)
            pl.when(i == 0)(comm.start)
            pl.when(i == max(n_i - 3, 0))(comm.forward)

        hv = _assemble_rows(i, x_refs, meta_ref, nq)
        y, _, _ = _rms(hv, g_ref[...])
        nb = y.astype(BF16)
        nrm_ref[...] = nb
        acc = jnp.zeros((tm, D_MODEL), F32)
        for a, b in HID_PIECES:
            gate = _dot_nt(nb, wgu_ref[0, a:b, :])
            up = _dot_nt(nb, wgu_ref[1, a:b, :])
            gate_ref[:, a:b] = gate.astype(BF16)
            up_ref[:, a:b] = up.astype(BF16)
            acc = acc + _dot((gate * _sigmoid(gate) * up).astype(BF16), wd_ref[a:b, :])
        out_ref[...] = hv + 0.5 * acc

        if nw:
            pl.when(i == n_i - 1)(comm.finish)

    rows = pl.BlockSpec((tm, D_MODEL), lambda i: (i, 0))
    hid = pl.BlockSpec((tm, D_FF), lambda i: (i, 0))
    return pl.pallas_call(
        body, name=name, grid=(n_i,),
        in_specs=[_token_spec(k, ksub, nq) for k in range(ksub)]
        + [pl.BlockSpec((N_META, D_MODEL), lambda i: (0, 0)), pl.BlockSpec((1, D_MODEL), lambda i: (0, 0)),
           _resident(W_GU_SHAPE, 1), _resident(W_D_SHAPE, 1)] + [ANY] * nw,
        out_specs=[rows, rows, hid, hid] + [ANY] * nw,
        out_shape=[jax.ShapeDtypeStruct((n, D_MODEL), F32), jax.ShapeDtypeStruct((n, D_MODEL), BF16),
                   jax.ShapeDtypeStruct((n, D_FF), BF16), jax.ShapeDtypeStruct((n, D_FF), BF16)]
        + [jax.ShapeDtypeStruct((N_DEV,) + a.shape, a.dtype) for a in gather],
        scratch_shapes=_comm_sems(nw) if nw else [],
        compiler_params=_params(("arbitrary",)),
    )(*([x2d] * ksub), meta, gain, wgu, wd, *gather)


def _ffn_fwd_loss(h, gain, wgu, wd, gfinal, target, lp, name):
    n = h.shape[0]
    nq = lp // 128
    tm = _row_tile(n, 512)
    ksub = tm // 128
    n_i = n // tm

    def body(*refs):
        t_refs = refs[:ksub]
        h_ref, g_ref, wgu_ref, wd_ref, gf_ref, dh_ref, nrm_ref, gate_ref, up_ref, loss_ref, dgf_ref = refs[ksub:]
        i = pl.program_id(0)

        @pl.when(i == 0)
        def _():
            loss_ref[...] = jnp.zeros_like(loss_ref)
            dgf_ref[...] = jnp.zeros_like(dgf_ref)

        hv = h_ref[...]
        y, _, _ = _rms(hv, g_ref[...])
        nb = y.astype(BF16)
        nrm_ref[...] = nb
        acc = jnp.zeros((tm, D_MODEL), F32)
        for a, b in HID_PIECES:
            gate = _dot_nt(nb, wgu_ref[0, a:b, :])
            up = _dot_nt(nb, wgu_ref[1, a:b, :])
            gate_ref[:, a:b] = gate.astype(BF16)
            up_ref[:, a:b] = up.astype(BF16)
            acc = acc + _dot((gate * _sigmoid(gate) * up).astype(BF16), wd_ref[a:b, :])
        hout = hv + 0.5 * acc

        gf = gf_ref[...]
        loss = jnp.zeros((1, 1), F32)
        dgf = jnp.zeros((1, D_MODEL), F32)
        for k in range(ksub):
            yk, xhat, r = _rms(hout[128 * k:128 * (k + 1)], gf)
            err = jnp.where(_is_lead(i, k, ksub, nq), 0.0, yk - t_refs[k][...])
            loss = loss + 0.5 * jnp.sum(jnp.sum(err * err, axis=1, keepdims=True), axis=0,
                                        keepdims=True) * (1.0 / D_MODEL)
            dy = err * (1.0 / D_MODEL)
            dh_ref[128 * k:128 * (k + 1), :] = _rms_bwd(dy, xhat, r, gf)
            dgf = dgf + jnp.sum(dy * xhat, axis=0, keepdims=True)
        loss_ref[...] += loss
        dgf_ref[...] += dgf

    rows = pl.BlockSpec((tm, D_MODEL), lambda i: (i, 0))
    hid = pl.BlockSpec((tm, D_FF), lambda i: (i, 0))
    vec = pl.BlockSpec((1, D_MODEL), lambda i: (0, 0))
    return pl.pallas_call(
        body, name=name, grid=(n_i,),
        in_specs=[_token_spec(k, ksub, nq) for k in range(ksub)]
        + [rows, vec, _resident(W_GU_SHAPE, 1), _resident(W_D_SHAPE, 1), vec],
        out_specs=[rows, rows, hid, hid, pl.BlockSpec((1, 1), lambda i: (0, 0)), vec],
        out_shape=[jax.ShapeDtypeStruct((n, D_MODEL), F32), jax.ShapeDtypeStruct((n, D_MODEL), BF16),
                   jax.ShapeDtypeStruct((n, D_FF), BF16), jax.ShapeDtypeStruct((n, D_FF), BF16),
                   jax.ShapeDtypeStruct((1, 1), F32), jax.ShapeDtypeStruct((1, D_MODEL), F32)],
        compiler_params=_params(("arbitrary",)),
    )(*([target] * ksub), h, gain, wgu, wd, gfinal)


def _ffn_bwd_in_tokens(dh_out, x2d, meta, lp, gain, dgate, dup, wgu, name, exchange=()):
    n = dh_out.shape[0]
    nq = lp // 128
    tm = _row_tile(n, 512)
    ksub = tm // 128
    n_i = n // tm
    nw = len(exchange)

    def body(*refs):
        x_refs = refs[:ksub]
        ((meta_ref, dh_ref, g_ref, dgate_ref, dup_ref, wgu_ref), xin, (gx_ref, dmeta_ref, dgain_ref), xout,
         (dx_scr, out_sems), sems) = _split_refs(refs[ksub:], 6, nw, 3, 2)
        i = pl.program_id(0)
        slot = i % 2
        if nw:
            comm = _ChipExchange(xin, xout, sems)
            pl.when(i == 0)(comm.start)

        def wait_rows(step, at):
            for k in range(ksub):
                @pl.when(jnp.logical_not(_is_lead(step, k, ksub, nq)))
                def _():
                    out_copy(step, k, at).wait()

        def out_copy(step, k, at):
            s = ksub * step + k
            to = pl.multiple_of(((s // nq) * (nq - 1) + (s % nq) - 1) * 128, 128)
            return pltpu.make_async_copy(dx_scr.at[at, pl.ds(128 * k, 128)], gx_ref.at[pl.ds(to, 128)],
                                         out_sems.at[at, k])

        @pl.when(i == 0)
        def _():
            dgain_ref[...] = jnp.zeros_like(dgain_ref)
            dmeta_ref[...] = jnp.zeros_like(dmeta_ref)

        @pl.when(i >= 2)
        def _():
            wait_rows(i - 2, slot)

        dn = jnp.zeros((tm, D_MODEL), F32)
        for a, b in HID_PIECES:
            dn = dn + _dot(dgate_ref[:, a:b], wgu_ref[0, a:b, :]) + _dot(dup_ref[:, a:b], wgu_ref[1, a:b, :])
        gain_v = g_ref[...]
        _, xhat, r = _rms(_assemble_rows(i, x_refs, meta_ref, nq), gain_v)
        dx = dh_ref[...] + _rms_bwd(dn, xhat, r, gain_v)
        dx_scr[slot] = dx
        dgain_ref[...] += jnp.sum(dn * xhat, axis=0, keepdims=True)
        for k in range(ksub):
            lead = _is_lead(i, k, ksub, nq)

            @pl.when(lead)
            def _():
                dmeta_ref[...] += dx[128 * k + PAD:128 * (k + 1), :]

            @pl.when(jnp.logical_not(lead))
            def _():
                out_copy(i, k, slot).start()

        @pl.when(i == n_i - 1)
        def _():
            if n_i >= 2:
                wait_rows(i - 1, 1 - slot)
            wait_rows(i, slot)

        if nw:
            pl.when(i == n_i - 1)(comm.finish)

    rows = pl.BlockSpec((tm, D_MODEL), lambda i: (i, 0))
    hid = pl.BlockSpec((tm, D_FF), lambda i: (i, 0))
    vec = pl.BlockSpec((1, D_MODEL), lambda i: (0, 0))
    return pl.pallas_call(
        body, name=name, grid=(n_i,),
        in_specs=[_token_spec(k, ksub, nq) for k in range(ksub)]
        + [pl.BlockSpec((N_META, D_MODEL), lambda i: (0, 0)), rows, vec, hid, hid, _resident(W_GU_SHAPE, 1)]
        + [ANY] * nw,
        out_specs=[ANY, pl.BlockSpec((N_META, D_MODEL), lambda i: (0, 0)), vec] + [ANY] * nw,
        out_shape=[jax.ShapeDtypeStruct(x2d.shape, F32), jax.ShapeDtypeStruct((N_META, D_MODEL), F32),
                   jax.ShapeDtypeStruct((1, D_MODEL), F32)]
        + [jax.ShapeDtypeStruct(a.shape, a.dtype) for a in exchange],
        scratch_shapes=[pltpu.VMEM((2, tm, D_MODEL), F32), pltpu.SemaphoreType.DMA((2, ksub))]
        + (_comm_sems(nw) if nw else []),
        compiler_params=_params(("arbitrary",)),
    )(*([x2d] * ksub), meta, dh_out, gain, dgate, dup, wgu, *exchange)


def _ffn_bwd_act_wd(dh_out, gate, up, wd, name, exchange=()):
    n = dh_out.shape[0]
    tm = _row_tile(n, 256)
    n_i = n // tm
    nw = len(exchange)

    def body(*refs):
        (dh_ref, gate_ref, up_ref, wd_ref), xin, (dgate_ref, dup_ref, dw_ref), xout, (acc_scr,), sems = \
            _split_refs(refs, 4, nw, 3, 1)
        i = pl.program_id(0)
        if nw:
            comm = _Exchange(xin, xout, sems)
            pl.when(i == 0)(comm.start)

        @pl.when(i == 0)
        def _():
            acc_scr[...] = jnp.zeros_like(acc_scr)

        dhb = (0.5 * dh_ref[...]).astype(BF16)
        for a, b in HID_PIECES:
            da = _dot_nt(dhb, wd_ref[a:b, :])
            g = gate_ref[:, a:b].astype(F32)
            u = up_ref[:, a:b].astype(F32)
            sig = _sigmoid(g)
            silu = g * sig
            dgate_ref[:, a:b] = (da * u * (sig * (1.0 + g * (1.0 - sig)))).astype(BF16)
            dup_ref[:, a:b] = (da * silu).astype(BF16)
            acc_scr[a:b, :] += _dot_tn((silu * u).astype(BF16), dhb)

        @pl.when(i == n_i - 1)
        def _():
            dw_ref[...] = acc_scr[...].astype(BF16)

        if nw:
            pl.when(i == n_i - 1)(comm.finish)

    rows = pl.BlockSpec((tm, D_MODEL), lambda i: (i, 0))
    hid = pl.BlockSpec((tm, D_FF), lambda i: (i, 0))
    return pl.pallas_call(
        body, name=name, grid=(n_i,),
        in_specs=[rows, hid, hid, _resident(W_D_SHAPE, 1)] + [ANY] * nw,
        out_specs=[hid, hid, _resident(W_D_SHAPE, 1)] + [ANY] * nw,
        out_shape=[jax.ShapeDtypeStruct((n, D_FF), BF16)] * 2 + [jax.ShapeDtypeStruct(W_D_SHAPE, BF16)]
        + [jax.ShapeDtypeStruct(a.shape, a.dtype) for a in exchange],
        scratch_shapes=[pltpu.VMEM(W_D_SHAPE, F32)] + (_comm_sems(nw) if nw else []),
        compiler_params=_params(("arbitrary",)),
    )(dh_out, gate, up, wd, *exchange)


def _ffn_bwd_wgu(nrm, dgate, dup, name, exchange=()):
    n = nrm.shape[0]
    tm = _row_tile(n, 256)
    n_i = n // tm
    nw = len(exchange)

    def body(*refs):
        (nrm_ref, dgate_ref, dup_ref), xin, (dw_ref,), xout, (acc_scr,), sems = _split_refs(refs, 3, nw, 1, 1)
        i = pl.program_id(0)
        if nw:
            comm = _Exchange(xin, xout, sems)
            pl.when(i == 0)(comm.start)

        @pl.when(i == 0)
        def _():
            acc_scr[...] = jnp.zeros_like(acc_scr)

        nb = nrm_ref[...]
        for a, b in HID_PIECES:
            acc_scr[0, a:b, :] += _dot_tn(dgate_ref[:, a:b], nb)
            acc_scr[1, a:b, :] += _dot_tn(dup_ref[:, a:b], nb)

        @pl.when(i == n_i - 1)
        def _():
            dw_ref[...] = acc_scr[...].astype(BF16)

        if nw:
            pl.when(i == n_i - 1)(comm.finish)

    hid = pl.BlockSpec((tm, D_FF), lambda i: (i, 0))
    res = pl.pallas_call(
        body, name=name, grid=(n_i,),
        in_specs=[pl.BlockSpec((tm, D_MODEL), lambda i: (i, 0)), hid, hid] + [ANY] * nw,
        out_specs=[_resident(W_GU_SHAPE, 1)] + [ANY] * nw,
        out_shape=[jax.ShapeDtypeStruct(W_GU_SHAPE, BF16)] + [jax.ShapeDtypeStruct(a.shape, a.dtype) for a in exchange],
        scratch_shapes=[pltpu.VMEM(W_GU_SHAPE, F32)] + (_comm_sems(nw) if nw else []),
        compiler_params=_params(("arbitrary",)),
    )(nrm, dgate, dup, *exchange)
    return res if nw else res[0]


def _ffn_bwd_wd(dhb, gate, up, name):
    n = dhb.shape[0]
    tm = _row_tile(n, 512)
    n_i = n // tm

    def body(dhb_ref, gate_ref, up_ref, dw_ref, acc_scr):
        i = pl.program_id(0)

        @pl.when(i == 0)
        def _():
            acc_scr[...] = jnp.zeros_like(acc_scr)

        dhv = dhb_ref[...]
        for a, b in HID_PIECES:
            g = gate_ref[:, a:b].astype(F32)
            act = (g * _sigmoid(g) * up_ref[:, a:b].astype(F32)).astype(BF16)
            acc_scr[a:b, :] += _dot_tn(act, dhv)

        @pl.when(i == n_i - 1)
        def _():
            dw_ref[...] = acc_scr[...].astype(BF16)

    hid = pl.BlockSpec((tm, D_FF), lambda i: (i, 0))
    return pl.pallas_call(
        body, name=name, grid=(n_i,),
        in_specs=[pl.BlockSpec((tm, D_MODEL), lambda i: (i, 0)), hid, hid],
        out_specs=_resident(W_D_SHAPE, 1),
        out_shape=jax.ShapeDtypeStruct(W_D_SHAPE, BF16),
        scratch_shapes=[pltpu.VMEM(W_D_SHAPE, F32)],
        compiler_params=_params(("arbitrary",)),
    )(dhb, gate, up)


N_PIECE = IN_MAIN // 512


def _inproj_fwd(h, gain, w_in):
    n = h.shape[0]
    tm = _row_tile(n, 512)

    def body(h_ref, g_ref, w_ref, *outs):
        y, _, _ = _rms(h_ref[...], g_ref[...])
        nb = y.astype(BF16)
        for p in range(N_PIECE):
            outs[p][...] = _dot_nt(nb, w_ref[512 * p:512 * (p + 1), :]).astype(BF16)
        outs[N_PIECE][...] = _dot_nt(nb, w_ref[IN_MAIN:IN_PAD, :])

    piece = pl.BlockSpec((tm, 512), lambda i: (i, 0))
    return pl.pallas_call(
        body, name="inproj_fwd", grid=(n // tm,),
        in_specs=[pl.BlockSpec((tm, D_MODEL), lambda i: (i, 0)),
                  pl.BlockSpec((1, D_MODEL), lambda i: (0, 0)),
                  pl.BlockSpec((IN_PAD, D_MODEL), lambda i: (0, 0))],
        out_specs=[piece] * N_PIECE + [pl.BlockSpec((tm, 128), lambda i: (i, 0))],
        out_shape=[jax.ShapeDtypeStruct((n, 512), BF16)] * N_PIECE + [jax.ShapeDtypeStruct((n, 128), F32)],
        compiler_params=_params(("parallel",)),
    )(h, gain, w_in)


def _inproj_bwd(dpieces, dfg, dh_out, h_in, gain, w_in):
    n = h_in.shape[0]
    tm = _row_tile(n, 512)
    n_i = n // tm

    def body(*refs):
        dp_refs = refs[:N_PIECE]
        dfg_ref, dh_ref, h_ref, g_ref, w_ref, dhin_ref, dw_ref, dgain_ref, acc_scr = refs[N_PIECE:]
        i = pl.program_id(0)

        @pl.when(i == 0)
        def _():
            acc_scr[...] = jnp.zeros_like(acc_scr)
            dgain_ref[...] = jnp.zeros_like(dgain_ref)

        gain_v = g_ref[...]
        y, xhat, r = _rms(h_ref[...], gain_v)
        nb = y.astype(BF16)
        dn = jnp.zeros((tm, D_MODEL), F32)
        for p in range(N_PIECE + 1):
            lo, hi = (512 * p, 512 * (p + 1)) if p < N_PIECE else (IN_MAIN, IN_PAD)
            dp = (dp_refs[p][...] if p < N_PIECE else dfg_ref[...]).astype(BF16)
            dn = dn + _dot(dp, w_ref[lo:hi, :])
            acc_scr[lo:hi, :] += _dot_tn(dp, nb)
        dhin_ref[...] = dh_ref[...] + _rms_bwd(dn, xhat, r, gain_v)
        dgain_ref[...] += jnp.sum(dn * xhat, axis=0, keepdims=True)

        @pl.when(i == n_i - 1)
        def _():
            dw_ref[...] = acc_scr[...].astype(BF16)

    piece = pl.BlockSpec((tm, 512), lambda i: (i, 0))
    rows = pl.BlockSpec((tm, D_MODEL), lambda i: (i, 0))
    vec = pl.BlockSpec((1, D_MODEL), lambda i: (0, 0))
    wspec = pl.BlockSpec((IN_PAD, D_MODEL), lambda i: (0, 0))
    return pl.pallas_call(
        body, name="inproj_bwd", grid=(n_i,),
        in_specs=[piece] * N_PIECE + [pl.BlockSpec((tm, 128), lambda i: (i, 0)), rows, rows, vec, wspec],
        out_specs=[rows, wspec, vec],
        out_shape=[jax.ShapeDtypeStruct((n, D_MODEL), F32),
                   jax.ShapeDtypeStruct((IN_PAD, D_MODEL), BF16),
                   jax.ShapeDtypeStruct((1, D_MODEL), F32)],
        scratch_shapes=[pltpu.VMEM((IN_PAD, D_MODEL), F32)],
        compiler_params=_params(("arbitrary",)),
    )(*dpieces, dfg, dh_out, h_in, gain, w_in)


def _outproj_fwd(zc, za, w_out, h):
    n = h.shape[0]
    tm = _row_tile(n, 512)

    def body(zc_ref, za_ref, w_ref, h_ref, out_ref):
        out_ref[...] = (h_ref[...] + _dot(zc_ref[...], w_ref[0:CONV_DIM, :])
                        + _dot(za_ref[...], w_ref[CONV_DIM:, :]))

    half = pl.BlockSpec((tm, 512), lambda i: (i, 0))
    rows = pl.BlockSpec((tm, D_MODEL), lambda i: (i, 0))
    return pl.pallas_call(
        body, name="outproj_fwd", grid=(n // tm,),
        in_specs=[half, half, pl.BlockSpec((D_MODEL, D_MODEL), lambda i: (0, 0)), rows],
        out_specs=rows,
        out_shape=jax.ShapeDtypeStruct((n, D_MODEL), F32),
        compiler_params=_params(("parallel",)),
    )(zc, za, w_out, h)


def _outproj_bwd(dh, zc, za, w_out):
    n = dh.shape[0]
    tm = _row_tile(n, 512)
    n_i = n // tm

    def body(dh_ref, zc_ref, za_ref, w_ref, dzc_ref, dza_ref, dw_ref, acc_scr):
        i = pl.program_id(0)

        @pl.when(i == 0)
        def _():
            acc_scr[...] = jnp.zeros_like(acc_scr)

        dhb = dh_ref[...].astype(BF16)
        dzc_ref[...] = _dot_nt(dhb, w_ref[0:CONV_DIM, :]).astype(BF16)
        dza_ref[...] = _dot_nt(dhb, w_ref[CONV_DIM:, :]).astype(BF16)
        acc_scr[0:CONV_DIM, :] += _dot_tn(zc_ref[...], dhb)
        acc_scr[CONV_DIM:, :] += _dot_tn(za_ref[...], dhb)

        @pl.when(i == n_i - 1)
        def _():
            dw_ref[...] = acc_scr[...].astype(BF16)

    half = pl.BlockSpec((tm, 512), lambda i: (i, 0))
    wspec = pl.BlockSpec((D_MODEL, D_MODEL), lambda i: (0, 0))
    return pl.pallas_call(
        body, name="outproj_bwd", grid=(n_i,),
        in_specs=[pl.BlockSpec((tm, D_MODEL), lambda i: (i, 0)), half, half, wspec],
        out_specs=[half, half, wspec],
        out_shape=[jax.ShapeDtypeStruct((n, 512), BF16), jax.ShapeDtypeStruct((n, 512), BF16),
                   jax.ShapeDtypeStruct((D_MODEL, D_MODEL), BF16)],
        scratch_shapes=[pltpu.VMEM((D_MODEL, D_MODEL), F32)],
        compiler_params=_params(("arbitrary",)),
    )(dh, zc, za, w_out)


def _group_matrix():
    r = lax.broadcasted_iota(jnp.int32, (128, 128), 0) // HEAD_DIM
    c = lax.broadcasted_iota(jnp.int32, (128, 128), 1) // HEAD_DIM
    return jnp.where(r == c, 1.0 / HEAD_DIM, 0.0).astype(BF16)


def _group_mean(x, gmat):
    hi = x.astype(BF16)
    lo = (x - hi.astype(F32)).astype(BF16)
    return _dot(hi, gmat) + _dot(lo, gmat)


def _shift_rows(x, s):
    rows = x.shape[0]
    t = lax.broadcasted_iota(jnp.int32, x.shape, 0)
    rolled = pltpu.roll(x, s % rows, 0)
    keep = (t >= s) if s > 0 else (t < rows + s)
    return jnp.where(keep, rolled, 0.0)


def _conv_parts(bg_ref, cg_ref, hc_ref, w_ref):
    bg = bg_ref[...].astype(F32)
    cg = cg_ref[...].astype(F32)
    hc = hc_ref[...].astype(F32)
    u = cg * hc
    u1 = _shift_rows(u, 1)
    u2 = _shift_rows(u, 2)
    conv = w_ref[2:3, :] * u + w_ref[1:2, :] * u1 + w_ref[0:1, :] * u2
    return bg, cg, hc, u, u1, u2, conv


def _conv_fwd(bg, cg, hc, conv_w, gain, gmat, lp):
    n = bg.shape[0]
    nb = n // lp

    def body(bg_ref, cg_ref, hc_ref, w_ref, g_ref, gm_ref, z_ref):
        bgv, _, _, _, _, _, conv = _conv_parts(bg_ref, cg_ref, hc_ref, w_ref)
        yc = bgv * conv
        r = lax.rsqrt(_group_mean(yc * yc, gm_ref[...]) + EPS)
        z_ref[...] = (yc * r * g_ref[...]).astype(BF16)

    blk = pl.BlockSpec((lp, 128), lambda c, b: (b, c))
    return pl.pallas_call(
        body, name="conv_fwd", grid=(CONV_DIM // 128, nb),
        in_specs=[blk, blk, blk, pl.BlockSpec((3, 128), lambda c, b: (0, c)),
                  pl.BlockSpec((1, 128), lambda c, b: (0, c)), pl.BlockSpec((128, 128), lambda c, b: (0, 0))],
        out_specs=blk,
        out_shape=jax.ShapeDtypeStruct((n, CONV_DIM), BF16),
        compiler_params=_params(("parallel", "parallel")),
    )(bg, cg, hc, conv_w, gain, gmat)


def _conv_bwd(dz, bg, cg, hc, conv_w, gain, gmat, lp):
    n = bg.shape[0]
    nb = n // lp

    def body(dz_ref, bg_ref, cg_ref, hc_ref, w_ref, g_ref, gm_ref,
             dbg_ref, dcg_ref, dhc_ref, dw_ref, dgain_ref):
        b = pl.program_id(1)

        @pl.when(b == 0)
        def _():
            dw_ref[...] = jnp.zeros_like(dw_ref)
            dgain_ref[...] = jnp.zeros_like(dgain_ref)

        bgv, cgv, hcv, u, u1, u2, conv = _conv_parts(bg_ref, cg_ref, hc_ref, w_ref)
        gm = gm_ref[...]
        yc = bgv * conv
        r = lax.rsqrt(_group_mean(yc * yc, gm) + EPS)
        yhat = yc * r
        dzv = dz_ref[...].astype(F32)
        dyhat = dzv * g_ref[...]
        dgain_ref[...] += jnp.sum(dzv * yhat, axis=0, keepdims=True)
        dyc = r * (dyhat - yhat * _group_mean(dyhat * yhat, gm))
        dbg_ref[...] = (dyc * conv).astype(BF16)
        dconv = dyc * bgv
        du = (w_ref[2:3, :] * dconv + w_ref[1:2, :] * _shift_rows(dconv, -1)
              + w_ref[0:1, :] * _shift_rows(dconv, -2))
        dcg_ref[...] = (du * hcv).astype(BF16)
        dhc_ref[...] = (du * cgv).astype(BF16)
        dw_ref[0:1, :] += jnp.sum(dconv * u2, axis=0, keepdims=True)
        dw_ref[1:2, :] += jnp.sum(dconv * u1, axis=0, keepdims=True)
        dw_ref[2:3, :] += jnp.sum(dconv * u, axis=0, keepdims=True)

    blk = pl.BlockSpec((lp, 128), lambda c, b: (b, c))
    wspec = pl.BlockSpec((3, 128), lambda c, b: (0, c))
    gspec = pl.BlockSpec((1, 128), lambda c, b: (0, c))
    return pl.pallas_call(
        body, name="conv_bwd", grid=(CONV_DIM // 128, nb),
        in_specs=[blk, blk, blk, blk, wspec, gspec, pl.BlockSpec((128, 128), lambda c, b: (0, 0))],
        out_specs=[blk, blk, blk, wspec, gspec],
        out_shape=[jax.ShapeDtypeStruct((n, CONV_DIM), BF16)] * 3
        + [jax.ShapeDtypeStruct((3, CONV_DIM), F32), jax.ShapeDtypeStruct((1, CONV_DIM), F32)],
        compiler_params=_params(("parallel", "arbitrary")),
    )(dz, bg, cg, hc, conv_w, gain, gmat)


KEY_MASKED = 1e30
ONE_LANE = 24


def _scan_steps(rows):
    s, out = 1, []
    while s < rows:
        out.append(s)
        s *= 2
    return out


def _fgate_fwd(fg, b_f, lp):
    n = fg.shape[0]
    nb = n // lp

    def body(fg_ref, b_ref, ka_ref, qa_ref):
        x = fg_ref[...] + b_ref[...]
        logf = jnp.minimum(x, 0.0) - jnp.log(1.0 + jnp.exp(-jnp.abs(x)))
        t = lax.broadcasted_iota(jnp.int32, (lp, 128), 0)
        lane = lax.broadcasted_iota(jnp.int32, (lp, 128), 1)
        f = jnp.where((t >= PAD) & (lane < N_HEADS), logf, 0.0)
        for s in _scan_steps(lp):
            f = f + _shift_rows(f, s)
        hi = f.astype(BF16).astype(F32)
        rest = f - hi
        mid = rest.astype(BF16).astype(F32)
        lo = (rest - mid).astype(BF16).astype(F32)
        ones = jnp.where((lane >= ONE_LANE) & (lane < ONE_LANE + 3), 1.0, 0.0)
        hi_key = jnp.where((t < PAD) & (lane < N_HEADS), KEY_MASKED, hi)
        ka_ref[...] = (hi_key + pltpu.roll(mid, 8, 1) + pltpu.roll(lo, 16, 1) + ones).astype(BF16)
        for h in range(N_HEADS):
            minus = jnp.where((lane == h) | (lane == 8 + h) | (lane == 16 + h), -1.0, 0.0)
            terms = (jnp.where(lane == ONE_LANE, pltpu.roll(hi, ONE_LANE - h, 1), 0.0)
                     + jnp.where(lane == ONE_LANE + 1, pltpu.roll(mid, ONE_LANE + 1 - h, 1), 0.0)
                     + jnp.where(lane == ONE_LANE + 2, pltpu.roll(lo, ONE_LANE + 2 - h, 1), 0.0))
            qa_ref[:, 128 * h:128 * (h + 1)] = (minus + terms).astype(BF16)

    return pl.pallas_call(
        body, name="fgate_fwd", grid=(nb,),
        in_specs=[pl.BlockSpec((lp, 128), lambda b: (b, 0)), pl.BlockSpec((1, 128), lambda b: (0, 0))],
        out_specs=[pl.BlockSpec((lp, 128), lambda b: (b, 0)), pl.BlockSpec((lp, N_HEADS * 128), lambda b: (b, 0))],
        out_shape=[jax.ShapeDtypeStruct((n, 128), BF16), jax.ShapeDtypeStruct((n, N_HEADS * 128), BF16)],
        compiler_params=_params(("parallel",)),
    )(fg, b_f)


def _fgate_bwd(dka, dfr, fg, b_f, lp):
    n = fg.shape[0]
    nb = n // lp

    def body(dka_ref, dfr_ref, fg_ref, b_ref, dfg_ref, db_ref):
        b = pl.program_id(0)

        @pl.when(b == 0)
        def _():
            db_ref[...] = jnp.zeros_like(db_ref)

        wide = jnp.concatenate([dfr_ref[0], jnp.zeros((128 - N_HEADS, lp), F32)], axis=0)
        t = lax.broadcasted_iota(jnp.int32, (lp, 128), 0)
        lane = lax.broadcasted_iota(jnp.int32, (lp, 128), 1)
        d = jnp.where(lane < N_HEADS, dka_ref[...], 0.0) + wide.T
        for s in _scan_steps(lp):
            d = d + _shift_rows(d, -s)
        x = fg_ref[...] + b_ref[...]
        dx = jnp.where((t >= PAD) & (lane < N_HEADS), d * _sigmoid(-x), 0.0)
        dfg_ref[...] = dx
        db_ref[...] += jnp.sum(dx, axis=0, keepdims=True)

    return pl.pallas_call(
        body, name="fgate_bwd", grid=(nb,),
        in_specs=[pl.BlockSpec((lp, 128), lambda b: (b, 0)), pl.BlockSpec((1, N_HEADS, lp), lambda b: (b, 0, 0)),
                  pl.BlockSpec((lp, 128), lambda b: (b, 0)), pl.BlockSpec((1, 128), lambda b: (0, 0))],
        out_specs=[pl.BlockSpec((lp, 128), lambda b: (b, 0)), pl.BlockSpec((1, 128), lambda b: (0, 0))],
        out_shape=[jax.ShapeDtypeStruct((n, 128), F32), jax.ShapeDtypeStruct((1, 128), F32)],
        compiler_params=_params(("arbitrary",)),
    )(dka, dfr, fg, b_f)


def _head_masks():
    lane = lax.broadcasted_iota(jnp.int32, (1, 128), 1)
    return lane < HEAD_DIM


def _stack_heads(x2, first):
    zero = jnp.zeros_like(x2)
    return jnp.concatenate([jnp.where(first, x2, zero), jnp.where(first, zero, x2)], axis=0)


def _stack_heads_lanes(xt):
    r = lax.broadcasted_iota(jnp.int32, xt.shape, 0)
    zero = jnp.zeros_like(xt)
    return jnp.concatenate([jnp.where(r < HEAD_DIM, xt, zero), jnp.where(r < HEAD_DIM, zero, xt)], axis=1)


def _pair_cols(col0, col1, first):
    return jnp.where(first, col0, col1)


def _pair_rows(row0, row1):
    r = lax.broadcasted_iota(jnp.int32, (128, TQ), 0)
    return jnp.where(r < HEAD_DIM, row0, row1)


def _query_side(q_ref, qa_ref, p, first):
    q2 = q_ref[:, 128 * p:128 * (p + 1)] * 0.125
    zero = jnp.zeros_like(q2)
    top = jnp.concatenate([jnp.where(first, q2, zero), qa_ref[:, 128 * (2 * p):128 * (2 * p + 1)]], axis=1)
    bot = jnp.concatenate([jnp.where(first, zero, q2), qa_ref[:, 128 * (2 * p + 1):128 * (2 * p + 2)]], axis=1)
    return jnp.concatenate([top, bot], axis=0)


def _key_chunks(lp):
    return (lp + TK - 1) // TK


def _chunk_mask(i, c, tk):
    r = lax.broadcasted_iota(jnp.int32, (tk, 2 * TQ), 0)
    col = lax.broadcasted_iota(jnp.int32, (tk, 2 * TQ), 1)
    return (c * TK + r) <= (i * TQ + (col & (TQ - 1)))


def _causal_sweep(i, step, init):
    per = TK // TQ
    last = i // per
    carry = lax.fori_loop(0, last, lambda c, carry: step(c, carry, False, TK), init)
    return lax.cond((i % per) < per // 2,
                    lambda carry: step(last, carry, True, TK // 2),
                    lambda carry: step(last, carry, True, TK), carry)


def _transpose_bf16(x):
    return x.astype(F32).T.astype(BF16)


def _attn_fwd(q, qa, k, v, ka, gain, zc, w_out, h, lp):
    n = q.shape[0]
    nb = n // lp
    nq = lp // TQ
    lpp = _key_chunks(lp) * TK

    def body(q_ref, qa_ref, k_ref, v_ref, ka_ref, g_ref, zc_ref, w_ref, h_ref,
             z_ref, o_ref, lse_ref, hout_ref, kx_scr, vt_scr):
        i = pl.program_id(1)
        first = _head_masks()

        @pl.when(i == 0)
        def _():
            if lpp > lp:
                kx_scr[lp:lpp, :] = jnp.zeros((lpp - lp, 2 * ATTN_DIM), BF16)
                vt_scr[:, lp:lpp] = jnp.zeros((ATTN_DIM, lpp - lp), BF16)
            for p in range(N_PAIRS):
                kx_scr[0:lp, 256 * p:256 * p + 128] = k_ref[:, 128 * p:128 * (p + 1)]
                kx_scr[0:lp, 256 * p + 128:256 * (p + 1)] = ka_ref[...]
            vt_scr[:, 0:lp] = _transpose_bf16(v_ref[...])

        rhs_t = [_transpose_bf16(_query_side(q_ref, qa_ref, p, first)) for p in range(N_PAIRS)]

        def step(c, carry, masked, tk):
            koff = pl.multiple_of(c * TK, TK)
            valid = _chunk_mask(i, c, tk) if masked else None
            new = []
            for p in range(N_PAIRS):
                m, l, acc = carry[p]
                st = _dot(kx_scr[pl.ds(koff, tk), 256 * p:256 * (p + 1)], rhs_t[p])
                if masked:
                    st = jnp.where(valid, st, NEG)
                m_new = jnp.maximum(m, jnp.max(st, axis=0, keepdims=True))
                pt = jnp.exp(st - m_new)
                alpha = jnp.exp(m - m_new)
                l = alpha * l + jnp.sum(pt, axis=0, keepdims=True)
                pb = pt.astype(BF16)
                vt = _stack_heads_lanes(vt_scr[128 * p:128 * (p + 1), pl.ds(koff, tk)])
                pv = _dot(vt, jnp.concatenate([pb[:, 0:TQ], pb[:, TQ:]], axis=0))
                acc = acc * _pair_rows(alpha[:, 0:TQ], alpha[:, TQ:]) + pv
                new.append((m_new, l, acc))
            return tuple(new)

        init = tuple((jnp.full((1, 2 * TQ), NEG, F32), jnp.zeros((1, 2 * TQ), F32), jnp.zeros((128, TQ), F32))
                     for _ in range(N_PAIRS))
        final = _causal_sweep(i, step, init)

        row = lax.broadcasted_iota(jnp.int32, (TQ, 128), 0)
        real = (i * TQ + row) >= PAD
        hout = h_ref[...] + _dot(zc_ref[...], w_ref[0:CONV_DIM, :])
        for p in range(N_PAIRS):
            m, l, acc = final[p]
            inv = 1.0 / l
            ot = acc * _pair_rows(inv[:, 0:TQ], inv[:, TQ:])
            sq = ot * ot
            r0 = lax.rsqrt(jnp.sum(sq[0:HEAD_DIM], axis=0, keepdims=True) * (1.0 / HEAD_DIM) + EPS)
            r1 = lax.rsqrt(jnp.sum(sq[HEAD_DIM:], axis=0, keepdims=True) * (1.0 / HEAD_DIM) + EPS)
            cols = slice(128 * p, 128 * (p + 1))
            o_ref[:, cols] = jnp.where(real, ot.T, 0.0).astype(BF16)
            z = (jnp.where(real, (ot * _pair_rows(r0, r1)).T, 0.0) * g_ref[:, cols]).astype(BF16)
            z_ref[:, cols] = z
            hout = hout + _dot(z, w_ref[CONV_DIM + 128 * p:CONV_DIM + 128 * (p + 1), :])
            lse = m + jnp.log(l)
            lse_ref[0, 2 * p:2 * p + 1, :] = lse[:, 0:TQ]
            lse_ref[0, 2 * p + 1:2 * p + 2, :] = lse[:, TQ:]
        hout_ref[...] = hout

    qblk = pl.BlockSpec((TQ, ATTN_DIM), lambda b, i: (b * nq + i, 0))
    qablk = pl.BlockSpec((TQ, N_HEADS * 128), lambda b, i: (b * nq + i, 0))
    seq = pl.BlockSpec((lp, ATTN_DIM), lambda b, i: (b, 0))
    rowblk = pl.BlockSpec((1, N_HEADS, TQ), lambda b, i: (b, 0, i))
    hblk = pl.BlockSpec((TQ, D_MODEL), lambda b, i: (b * nq + i, 0))
    return pl.pallas_call(
        body, name="attn_fwd", grid=(nb, nq),
        in_specs=[qblk, qablk, seq, seq, pl.BlockSpec((lp, 128), lambda b, i: (b, 0)),
                  pl.BlockSpec((1, ATTN_DIM), lambda b, i: (0, 0)), qblk,
                  pl.BlockSpec((D_MODEL, D_MODEL), lambda b, i: (0, 0)), hblk],
        out_specs=[qblk, qblk, rowblk, hblk],
        out_shape=[jax.ShapeDtypeStruct((n, ATTN_DIM), BF16), jax.ShapeDtypeStruct((n, ATTN_DIM), BF16),
                   jax.ShapeDtypeStruct((nb, N_HEADS, lp), F32), jax.ShapeDtypeStruct((n, D_MODEL), F32)],
        scratch_shapes=[pltpu.VMEM((lpp, 2 * ATTN_DIM), BF16), pltpu.VMEM((ATTN_DIM, lpp), BF16)],
        compiler_params=_params(("parallel", "arbitrary")),
    )(q, qa, k, v, ka, gain, zc, w_out, h)


def _attn_bwd(dz, q, qa, k, v, ka, o, lse, gain, lp, exchange=()):
    n = q.shape[0]
    nb = n // lp
    nq = lp // TQ
    lpp = _key_chunks(lp) * TK
    nw = len(exchange)

    def body(*refs):
        ((dz_ref, q_ref, qa_ref, k_ref, v_ref, ka_ref, o_ref, lse_ref, g_ref), xin,
         (dq_ref, dk_ref, dv_ref, dka_ref, dfr_ref, dgain_ref), xout,
         (kx_scr, vx_scr, kt_scr, dkx_scr, dvx_scr), sems) = _split_refs(refs, 9, nw, 6, 5)
        b = pl.program_id(0)
        i = pl.program_id(1)
        first = _head_masks()
        if nw:
            comm = _Exchange(xin, xout, sems)
            pl.when((b == 0) & (i == 0))(comm.start)

        @pl.when((b == 0) & (i == 0))
        def _():
            dgain_ref[...] = jnp.zeros_like(dgain_ref)

        @pl.when(i == 0)
        def _():
            if lpp > lp:
                kx_scr[lp:lpp, :] = jnp.zeros((lpp - lp, 2 * ATTN_DIM), BF16)
                vx_scr[lp:lpp, :] = jnp.zeros((lpp - lp, ATTN_DIM), BF16)
                kt_scr[:, lp:lpp] = jnp.zeros((ATTN_DIM, lpp - lp), BF16)
            for p in range(N_PAIRS):
                kx_scr[0:lp, 256 * p:256 * p + 128] = k_ref[:, 128 * p:128 * (p + 1)]
                kx_scr[0:lp, 256 * p + 128:256 * (p + 1)] = ka_ref[...]
            vx_scr[0:lp, :] = v_ref[...]
            kt_scr[:, 0:lp] = _transpose_bf16(k_ref[...])
            dkx_scr[...] = jnp.zeros_like(dkx_scr)
            dvx_scr[...] = jnp.zeros_like(dvx_scr)

        rhs, rhs_t, lses, dos, dos_t, deltas = [], [], [], [], [], []
        for p in range(N_PAIRS):
            cols = slice(128 * p, 128 * (p + 1))
            side = _query_side(q_ref, qa_ref, p, first)
            rhs.append(side)
            rhs_t.append(_transpose_bf16(side))
            lses.append(jnp.concatenate([lse_ref[0, 2 * p:2 * p + 1, :], lse_ref[0, 2 * p + 1:2 * p + 2, :]], axis=1))
            ov = o_ref[:, cols].astype(F32)
            dzv = dz_ref[:, cols].astype(F32)
            gv = g_ref[:, cols]
            sq = ov * ov
            ms0 = jnp.sum(jnp.where(first, sq, 0.0), axis=1, keepdims=True) * (1.0 / HEAD_DIM)
            ms1 = jnp.sum(jnp.where(first, 0.0, sq), axis=1, keepdims=True) * (1.0 / HEAD_DIM)
            r = _pair_cols(lax.rsqrt(ms0 + EPS), lax.rsqrt(ms1 + EPS), first)
            ohat = ov * r
            dyhat = dzv * gv
            dgain_ref[:, cols] += jnp.sum(dzv * ohat, axis=0, keepdims=True)
            pr = dyhat * ohat
            mean0 = jnp.sum(jnp.where(first, pr, 0.0), axis=1, keepdims=True) * (1.0 / HEAD_DIM)
            mean1 = jnp.sum(jnp.where(first, 0.0, pr), axis=1, keepdims=True) * (1.0 / HEAD_DIM)
            do = r * (dyhat - ohat * _pair_cols(mean0, mean1, first))
            ddt = (do * ov).T
            deltas.append(jnp.concatenate([jnp.sum(ddt[0:HEAD_DIM], axis=0, keepdims=True),
                                           jnp.sum(ddt[HEAD_DIM:], axis=0, keepdims=True)], axis=1))
            do_st = _stack_heads(do.astype(BF16), first)
            dos.append(do_st)
            dos_t.append(_transpose_bf16(do_st))

        def step(c, carry, masked, tk):
            koff = pl.multiple_of(c * TK, TK)
            valid = _chunk_mask(i, c, tk) if masked else None
            new = []
            for p in range(N_PAIRS):
                dqt, dfq = carry[p]
                ext = slice(256 * p, 256 * (p + 1))
                cols = slice(128 * p, 128 * (p + 1))
                st = _dot(kx_scr[pl.ds(koff, tk), ext], rhs_t[p])
                if masked:
                    st = jnp.where(valid, st, NEG)
                pt = jnp.exp(st - lses[p])
                dpt = _dot(vx_scr[pl.ds(koff, tk), cols], dos_t[p])
                dst = pt * (dpt - deltas[p])
                dsb = dst.astype(BF16)
                dfq = dfq + jnp.sum(dsb.astype(F32), axis=0, keepdims=True)
                dkx_scr[pl.ds(koff, tk), ext] += _dot(dsb, rhs[p])
                dvx_scr[pl.ds(koff, tk), cols] += _dot(pt.astype(BF16), dos[p])
                kt = _stack_heads_lanes(kt_scr[cols, pl.ds(koff, tk)])
                dqt = dqt + _dot(kt, jnp.concatenate([dsb[:, 0:TQ], dsb[:, TQ:]], axis=0))
                new.append((dqt, dfq))
            return tuple(new)

        init = tuple((jnp.zeros((128, TQ), F32), jnp.zeros((1, 2 * TQ), F32)) for _ in range(N_PAIRS))
        final = _causal_sweep(i, step, init)

        for p in range(N_PAIRS):
            dqt, dfq = final[p]
            dq_ref[:, 128 * p:128 * (p + 1)] = (dqt.T * 0.125).astype(BF16)
            dfr_ref[0, 2 * p:2 * p + 1, :] = dfq[:, 0:TQ]
            dfr_ref[0, 2 * p + 1:2 * p + 2, :] = dfq[:, TQ:]

        @pl.when(i == nq - 1)
        def _():
            dka = jnp.zeros((lp, 128), F32)
            for p in range(N_PAIRS):
                dk_ref[:, 128 * p:128 * (p + 1)] = dkx_scr[0:lp, 256 * p:256 * p + 128].astype(BF16)
                dka = dka + dkx_scr[0:lp, 256 * p + 128:256 * (p + 1)]
            dka_ref[...] = dka
            dv_ref[...] = dvx_scr[0:lp, :].astype(BF16)

        if nw:
            pl.when((b == nb - 1) & (i == nq - 1))(comm.finish)

    qblk = pl.BlockSpec((TQ, ATTN_DIM), lambda b, i: (b * nq + i, 0))
    qablk = pl.BlockSpec((TQ, N_HEADS * 128), lambda b, i: (b * nq + i, 0))
    seq = pl.BlockSpec((lp, ATTN_DIM), lambda b, i: (b, 0))
    kaseq = pl.BlockSpec((lp, 128), lambda b, i: (b, 0))
    rowblk = pl.BlockSpec((1, N_HEADS, TQ), lambda b, i: (b, 0, i))
    gspec = pl.BlockSpec((1, ATTN_DIM), lambda b, i: (0, 0))
    return pl.pallas_call(
        body, name="attn_bwd", grid=(nb, nq),
        in_specs=[qblk, qblk, qablk, seq, seq, kaseq, qblk, rowblk, gspec] + [ANY] * nw,
        out_specs=[qblk, seq, seq, kaseq, rowblk, gspec] + [ANY] * nw,
        out_shape=[jax.ShapeDtypeStruct((n, ATTN_DIM), BF16), jax.ShapeDtypeStruct((n, ATTN_DIM), BF16),
                   jax.ShapeDtypeStruct((n, ATTN_DIM), BF16), jax.ShapeDtypeStruct((n, 128), F32),
                   jax.ShapeDtypeStruct((nb, N_HEADS, lp), F32), jax.ShapeDtypeStruct((1, ATTN_DIM), F32)]
        + [jax.ShapeDtypeStruct(a.shape, a.dtype) for a in exchange],
        scratch_shapes=[pltpu.VMEM((lpp, 2 * ATTN_DIM), BF16), pltpu.VMEM((lpp, ATTN_DIM), BF16),
                        pltpu.VMEM((ATTN_DIM, lpp), BF16), pltpu.VMEM((lpp, 2 * ATTN_DIM), F32),
                        pltpu.VMEM((lpp, ATTN_DIM), F32)] + (_comm_sems(nw) if nw else []),
        compiler_params=_params(("arbitrary", "arbitrary")),
    )(dz, q, qa, k, v, ka, o, lse, gain, *exchange)


def _loss_head(h, gain, target, lp):
    n = h.shape[0]
    nb = n // lp
    nq = lp // 128

    def body(h_ref, g_ref, t_ref, loss_ref, dh_ref, dgain_ref):
        b = pl.program_id(0)
        i = pl.program_id(1)

        @pl.when((b == 0) & (i == 0))
        def _():
            loss_ref[...] = jnp.zeros_like(loss_ref)
            dgain_ref[...] = jnp.zeros_like(dgain_ref)

        @pl.when(i == 0)
        def _():
            dh_ref[...] = jnp.zeros_like(dh_ref)

        @pl.when(i > 0)
        def _():
            gain_v = g_ref[...]
            y, xhat, r = _rms(h_ref[...], gain_v)
            err = y - t_ref[...]
            loss_ref[...] += 0.5 * jnp.sum(jnp.sum(err * err, axis=1, keepdims=True), axis=0,
                                           keepdims=True) * (1.0 / D_MODEL)
            dy = err * (1.0 / D_MODEL)
            dh_ref[...] = _rms_bwd(dy, xhat, r, gain_v)
            dgain_ref[...] += jnp.sum(dy * xhat, axis=0, keepdims=True)

    rows = pl.BlockSpec((128, D_MODEL), lambda b, i: (b * nq + i, 0))
    trows = pl.BlockSpec((128, D_MODEL), lambda b, i: (b * (nq - 1) + jnp.maximum(i, 1) - 1, 0))
    return pl.pallas_call(
        body, name="loss_head", grid=(nb, nq),
        in_specs=[rows, pl.BlockSpec((1, D_MODEL), lambda b, i: (0, 0)), trows],
        out_specs=[pl.BlockSpec((1, 1), lambda b, i: (0, 0)), rows, pl.BlockSpec((1, D_MODEL), lambda b, i: (0, 0))],
        out_shape=[jax.ShapeDtypeStruct((1, 1), F32), jax.ShapeDtypeStruct((n, D_MODEL), F32),
                   jax.ShapeDtypeStruct((1, D_MODEL), F32)],
        compiler_params=_params(("arbitrary", "arbitrary")),
    )(h, gain, target)


def _adamw(parts, w, m, v, name):
    s_parts, r, c = parts.shape
    tr = r
    for t in (256, 128, 64, 32, 16):
        if r % t == 0 and r > t:
            tr = t
            break

    def body(p_ref, w_ref, m_ref, v_ref, g_ref, d_ref, nm_ref, nv_ref):
        g = p_ref[0].astype(F32)
        for s in range(1, s_parts):
            g = g + p_ref[s].astype(F32)
        nm = ADAM_B1 * m_ref[...] + (1.0 - ADAM_B1) * g
        nv = ADAM_B2 * v_ref[...] + (1.0 - ADAM_B2) * (g * g)
        m_hat = nm / (1.0 - ADAM_B1 ** ADAM_STEP)
        v_hat = nv / (1.0 - ADAM_B2 ** ADAM_STEP)
        g_ref[...] = g
        d_ref[...] = -ADAM_LR * (m_hat / (jnp.sqrt(v_hat) + ADAM_EPS) + ADAM_WD * w_ref[...])
        nm_ref[...] = nm
        nv_ref[...] = nv

    blk = pl.BlockSpec((tr, c), lambda i: (i, 0))
    return pl.pallas_call(
        body, name=name, grid=(r // tr,),
        in_specs=[pl.BlockSpec((s_parts, tr, c), lambda i: (0, i, 0)), blk, blk, blk],
        out_specs=[blk] * 4,
        out_shape=[jax.ShapeDtypeStruct((r, c), F32)] * 4,
        compiler_params=_params(("parallel",)),
    )(parts, w, m, v)


def _sum_parts(parts, name):
    s_parts, r, c = parts.shape

    def body(p_ref, out_ref):
        acc = p_ref[0]
        for s in range(1, s_parts):
            acc = acc + p_ref[s]
        out_ref[...] = acc

    return pl.pallas_call(
        body, name=name, out_shape=jax.ShapeDtypeStruct((r, c), F32),
        in_specs=[pl.BlockSpec(memory_space=pltpu.VMEM)], out_specs=pl.BlockSpec(memory_space=pltpu.VMEM),
    )(parts)


SMALL_ROWS = 184


def _pack_small(d_gains, d_gc, d_ga, d_bf, d_conv, d_meta):
    rows = [g.reshape(8, 128) for g in d_gains]
    rows += [d_gc.reshape(4, 128), d_ga.reshape(4, 128), d_bf.reshape(1, 128)]
    rows += [d_conv.reshape(12, 128), d_meta.reshape(128, 128)]
    packed = jnp.concatenate(rows, axis=0)
    return jnp.pad(packed, ((0, SMALL_ROWS - packed.shape[0]), (0, 0)))


def kernel(x, meta_tokens, ffn1_norm, ffn1_w_gu, ffn1_w_down, mix_norm, w_in, conv_w, b_f, out_norm_conv, out_norm_attn, w_out, ffn2_norm, ffn2_w_gu, ffn2_w_down, final_norm, loss_target, m_meta_tokens, m_ffn1_norm, m_ffn1_w_gu, m_ffn1_w_down, m_mix_norm, m_w_in, m_conv_w, m_b_f, m_out_norm_conv, m_out_norm_attn, m_w_out, m_ffn2_norm, m_ffn2_w_gu, m_ffn2_w_down, m_final_norm, v_meta_tokens, v_ffn1_norm, v_ffn1_w_gu, v_ffn1_w_down, v_mix_norm, v_w_in, v_conv_w, v_b_f, v_out_norm_conv, v_out_norm_attn, v_w_out, v_ffn2_norm, v_ffn2_w_gu, v_ffn2_w_down, v_final_norm):
    nb, seq, _ = x.shape
    lp = PAD + N_META + seq
    n = nb * lp
    me = 4 * lax.axis_index("x") + 2 * lax.axis_index("y") + lax.axis_index("c")

    small_in = jnp.concatenate(
        [meta_tokens, jnp.pad(conv_w[0], ((0, 0), (0, 128 - conv_w.shape[2]))), jnp.zeros((5, 128), F32)], axis=0)
    wgu1_8, wd1_8, small_8 = _all_gather(
        [ffn1_w_gu[0].T.astype(BF16), ffn1_w_down[0].astype(BF16), small_in], "gather_ffn1")
    meta_full = small_8[:, 0:N_META, :].transpose(1, 0, 2).reshape(N_META, D_MODEL)
    conv_full = small_8[:, N_META:N_META + 3, 0:CONV_DIM // N_DEV].transpose(1, 0, 2).reshape(3, CONV_DIM)
    wgu1 = wgu1_8.reshape(W_GU_SHAPE)
    wd1 = wd1_8.reshape(W_D_SHAPE)
    b_f_row = jnp.pad(b_f, ((0, 0), (0, 128 - N_HEADS)))
    gmat = _group_matrix()

    x2d = x.reshape(nb * seq, D_MODEL)
    later = [w_in[0].T.astype(BF16), w_out[0].astype(BF16), ffn2_w_gu[0].T.astype(BF16), ffn2_w_down[0].astype(BF16)]
    h1, n1, gate1, up1, win_8, wout_8, wgu2_8, wd2_8 = _ffn_fwd_tokens(
        x2d, meta_full, lp, ffn1_norm, wgu1, wd1, "ffn1_fwd", gather=later)
    wgu2 = wgu2_8.reshape(W_GU_SHAPE)
    wd2 = wd2_8.reshape(W_D_SHAPE)
    w_in_full = jnp.pad(win_8.reshape(IN_DIM, D_MODEL), ((0, IN_PAD - IN_DIM), (0, 0)))
    w_out_full = wout_8.reshape(D_MODEL, D_MODEL)

    bg, cg, hc, q, k, v, fg = _inproj_fwd(h1, mix_norm, w_in_full)
    zc = _conv_fwd(bg, cg, hc, conv_full, out_norm_conv, gmat, lp)
    ka, qa = _fgate_fwd(fg, b_f_row, lp)
    za, o, lse, h2 = _attn_fwd(q, qa, k, v, ka, out_norm_attn, zc, w_out_full, h1, lp)
    dh3, n3, gate2, up2, loss_part, d_final = _ffn_fwd_loss(
        h2, ffn2_norm, wgu2, wd2, final_norm.reshape(1, D_MODEL), loss_target.reshape(nb * seq, D_MODEL), lp,
        "ffn2_fwd_loss")

    dgate2, dup2, dwd2 = _ffn_bwd_act_wd(dh3, gate2, up2, wd2, "ffn2_bwd_act")
    dh2, d_ffn2 = _ffn_bwd_in(dh3, h2, ffn2_norm, dgate2, dup2, wgu2, "ffn2_bwd_in")
    dwgu2 = _ffn_bwd_wgu(n3, dgate2, dup2, "ffn2_bwd_wgu")
    dzc, dza, dwout = _outproj_bwd(dh2, zc, za, w_out_full)
    send_a = [dwgu2.reshape(N_DEV, F_CHUNK, D_MODEL), dwd2.reshape(N_DEV, F_CHUNK // 2, D_MODEL),
              dwout.reshape(N_DEV, D_MODEL // N_DEV, D_MODEL)]
    dq, dk, dv, dka, dfr, d_ga, p_wgu2, p_wd2, p_wout = _attn_bwd(
        dza, q, qa, k, v, ka, o, lse, out_norm_attn, lp, exchange=send_a)
    dfg, d_bf = _fgate_bwd(dka, dfr, fg, b_f_row, lp)
    dbg, dcg, dhc, d_conv, d_gc = _conv_bwd(dzc, bg, cg, hc, conv_full, out_norm_conv, gmat, lp)
    dh1, dwin, d_mix = _inproj_bwd([dbg, dcg, dhc, dq, dk, dv], dfg, dh2, h1, mix_norm, w_in_full)
    dwin_8 = dwin[0:IN_DIM].reshape(N_DEV, IN_DIM // N_DEV, D_MODEL)
    dgate1, dup1, dwd1, p_win = _ffn_bwd_act_wd(dh1, gate1, up1, wd1, "ffn1_bwd_act", exchange=[dwin_8])
    dwgu1, p_wd1 = _ffn_bwd_wgu(n1, dgate1, dup1, "ffn1_bwd_wgu",
                                exchange=[dwd1.reshape(N_DEV, F_CHUNK // 2, D_MODEL)])
    own = dwgu1.reshape(N_DEV, F_CHUNK, D_MODEL)
    (got,) = _pair_exchange([own], "pair_exchange_ffn1")
    chip_sum = _pair_sum(own, got, "pair_sum_wgu1")
    grad_x2d, d_meta, d_ffn1, p_wgu1 = _ffn_bwd_in_tokens(
        dh1, x2d, meta_full, lp, ffn1_norm, dgate1, dup1, wgu1, "ffn1_bwd_in", exchange=[chip_sum])
    grad_x = grad_x2d.reshape(nb, seq, D_MODEL)

    small = _pack_small([d_ffn1, d_mix, d_ffn2, d_final], d_gc, d_ga, d_bf, d_conv, d_meta)
    (small_all,) = _all_gather([small], "gather_small_grads")
    small_sum = _sum_parts(small_all, "sum_small_grads")
    g_ffn1n, g_mixn, g_ffn2n, g_finaln = (small_sum[8 * t:8 * t + 8].reshape(1, D_MODEL) for t in range(4))
    g_gc = small_sum[32:36].reshape(1, CONV_DIM)
    g_ga = small_sum[36:40].reshape(1, ATTN_DIM)
    g_bf = small_sum[40:41, 0:N_HEADS]
    g_conv_full = small_sum[41:53].reshape(3, CONV_DIM)
    g_meta_full = small_sum[53:181].reshape(N_META, D_MODEL)
    g_conv = lax.dynamic_slice_in_dim(g_conv_full, me * (CONV_DIM // N_DEV), CONV_DIM // N_DEV, axis=1)
    g_meta = lax.dynamic_slice_in_dim(g_meta_full, me * (D_MODEL // N_DEV), D_MODEL // N_DEV, axis=1)

    weights = {
        "meta_tokens": (g_meta[None], meta_tokens, m_meta_tokens, v_meta_tokens),
        "ffn1_norm": (g_ffn1n[None], ffn1_norm, m_ffn1_norm, v_ffn1_norm),
        "ffn1_w_gu": (p_wgu1, ffn1_w_gu[0].T, m_ffn1_w_gu[0].T, v_ffn1_w_gu[0].T),
        "ffn1_w_down": (p_wd1, ffn1_w_down[0], m_ffn1_w_down[0], v_ffn1_w_down[0]),
        "mix_norm": (g_mixn[None], mix_norm, m_mix_norm, v_mix_norm),
        "w_in": (p_win, w_in[0].T, m_w_in[0].T, v_w_in[0].T),
        "conv_w": (g_conv[None], conv_w[0], m_conv_w[0], v_conv_w[0]),
        "b_f": (g_bf[None], b_f, m_b_f, v_b_f),
        "out_norm_conv": (g_gc[None], out_norm_conv, m_out_norm_conv, v_out_norm_conv),
        "out_norm_attn": (g_ga[None], out_norm_attn, m_out_norm_attn, v_out_norm_attn),
        "w_out": (p_wout, w_out[0], m_w_out[0], v_w_out[0]),
        "ffn2_norm": (g_ffn2n[None], ffn2_norm, m_ffn2_norm, v_ffn2_norm),
        "ffn2_w_gu": (p_wgu2, ffn2_w_gu[0].T, m_ffn2_w_gu[0].T, v_ffn2_w_gu[0].T),
        "ffn2_w_down": (p_wd2, ffn2_w_down[0], m_ffn2_w_down[0], v_ffn2_w_down[0]),
        "final_norm": (g_finaln[None], final_norm.reshape(1, D_MODEL), m_final_norm.reshape(1, D_MODEL),
                       v_final_norm.reshape(1, D_MODEL)),
    }
    shapes = {"meta_tokens": meta_tokens.shape, "ffn1_norm": ffn1_norm.shape, "ffn1_w_gu": ffn1_w_gu.shape,
              "ffn1_w_down": ffn1_w_down.shape, "mix_norm": mix_norm.shape, "w_in": w_in.shape,
              "conv_w": conv_w.shape, "b_f": b_f.shape, "out_norm_conv": out_norm_conv.shape,
              "out_norm_attn": out_norm_attn.shape, "w_out": w_out.shape, "ffn2_norm": ffn2_norm.shape,
              "ffn2_w_gu": ffn2_w_gu.shape, "ffn2_w_down": ffn2_w_down.shape, "final_norm": final_norm.shape}
    grads, deltas, new_m, new_v = [], [], [], []
    for name, (p, w, m, vv) in weights.items():
        g, d, nm, nv = _adamw(p, w, m, vv, "adamw_" + name)
        if name in ("ffn1_w_gu", "ffn2_w_gu", "w_in"):
            g, d, nm, nv = g.T, d.T, nm.T, nv.T
        shape = shapes[name]
        grads.append(g.reshape(shape))
        deltas.append(d.reshape(shape))
        new_m.append(nm.reshape(shape))
        new_v.append(nv.reshape(shape))

    loss = lax.psum(loss_part[0, 0], ("x", "y", "c"))
    return (loss, grad_x, *grads, *deltas, *new_m, *new_v)
```

```python
import jax
import jax.numpy as jnp
from jax import lax
from jax.experimental import pallas as pl
from jax.experimental.pallas import tpu as pltpu

F32 = jnp.float32
BF16 = jnp.bfloat16

N_DEV = 8
D_MODEL = 1024
N_META = 16
PAD = 128 - N_META
CONV_DIM = 512
ATTN_DIM = 512
HEAD_DIM = 64
N_HEADS = 8
N_PAIRS = N_HEADS // 2
D_FF = 2816
IN_DIM = 3080
IN_PAD = 3200
IN_MAIN = 3072
N_PIECE = IN_MAIN // 512
EPS = 1e-6
NEG = -1e30
TQ = 128
TK = 512
VMEM_LIMIT = 56 * 1024 * 1024

HID_PIECES = ((0, 1024), (1024, 2048), (2048, D_FF))
W_GU_SHAPE = (2, D_FF, D_MODEL)
W_D_SHAPE = (D_FF, D_MODEL)

ADAM_LR = 0.001
ADAM_B1 = 0.9
ADAM_B2 = 0.999
ADAM_EPS = 1e-08
ADAM_WD = 0.01
ADAM_STEP = 10

MESH = pl.DeviceIdType.MESH
ANY = pl.BlockSpec(memory_space=pl.ANY)


def _params(sem=None):
    return pltpu.CompilerParams(dimension_semantics=sem, vmem_limit_bytes=VMEM_LIMIT)


def _row_tile(n, prefer):
    for t in (prefer, 512, 256, 128):
        if t <= n and n % t == 0:
            return t
    raise ValueError(f"no row tile for {n}")


def _resident(shape):
    zeros = (0,) * len(shape)
    return pl.BlockSpec(shape, lambda i: zeros, pipeline_mode=pl.Buffered(1))


def _dot(a, b):
    return jnp.dot(a, b, preferred_element_type=F32)


def _dot_nt(a, b):
    return lax.dot_general(a, b, (((1,), (1,)), ((), ())), preferred_element_type=F32)


def _dot_tn(a, b):
    return lax.dot_general(a, b, (((0,), (0,)), ((), ())), preferred_element_type=F32)


def _rms(x, g):
    r = lax.rsqrt(jnp.mean(x * x, axis=-1, keepdims=True) + EPS)
    xhat = x * r
    return xhat * g, xhat, r


def _rms_bwd(dn, xhat, r, g):
    dxhat = dn * g
    return r * (dxhat - xhat * jnp.mean(dxhat * xhat, axis=-1, keepdims=True))


def _sigmoid(x):
    return 1.0 / (1.0 + jnp.exp(-x))


def _place():
    return lax.axis_index("x"), lax.axis_index("y"), lax.axis_index("c")


def _comm_sems(nw):
    return [pltpu.SemaphoreType.DMA((nw, 7)), pltpu.SemaphoreType.DMA((nw, 7)), pltpu.SemaphoreType.DMA((nw,))]


def _flip(v, bit):
    return 1 - v if bit else v


class _Gather:
    def __init__(self, ins, outs, sems):
        self.ins, self.outs = ins, outs
        self.send, self.recv, self.local = sems
        x, y, c = _place()
        self.c = c
        self.me, self.sibling = (x, y, c), (x, y, 1 - c)
        self.chips = [(1 - x, y), (x, 1 - y), (1 - x, 1 - y)]

    def _copy(self, w, k, block, to, own=False):
        slot = self.outs[w].at[4 * block[0] + 2 * block[1] + block[2]]
        return pltpu.make_async_remote_copy(
            src_ref=self.ins[w] if own else slot, dst_ref=slot,
            send_sem=self.send.at[w, k], recv_sem=self.recv.at[w, k], device_id=to, device_id_type=MESH)

    def _mine(self, w):
        x, y, c = self.me
        return pltpu.make_async_copy(self.ins[w], self.outs[w].at[4 * x + 2 * y + c], self.local.at[w])

    def _first(self, w):
        return ([self._copy(w, 0, self.me, self.sibling, own=True)]
                + [self._copy(w, 1 + j, self.me, (*chip, self.c), own=True) for j, chip in enumerate(self.chips)])

    def _passed(self, w):
        return [self._copy(w, 4 + j, (*chip, self.c), self.sibling) for j, chip in enumerate(self.chips)]

    def start(self):
        for w in range(len(self.ins)):
            self._mine(w).start()
        for w in range(len(self.ins)):
            for cp in self._first(w):
                cp.start()

    def forward(self):
        for w in range(len(self.ins)):
            for j, chip in enumerate(self.chips):
                self._copy(w, 1 + j, (*chip, self.c), self.me).wait_recv()
                self._passed(w)[j].start()

    def finish(self):
        for w in range(len(self.ins)):
            self._copy(w, 0, self.sibling, self.me).wait_recv()
            for j, chip in enumerate(self.chips):
                self._copy(w, 4 + j, (*chip, 1 - self.c), self.me).wait_recv()
        for w in range(len(self.ins)):
            for cp in self._first(w) + self._passed(w):
                cp.wait_send()
            self._mine(w).wait()


class _Exchange:
    def __init__(self, ins, outs, sems):
        self.ins, self.outs = ins, outs
        self.send, self.recv, self.local = sems
        self.x, self.y, self.c = _place()
        self.me = 4 * self.x + 2 * self.y + self.c

    def _copy(self, w, k):
        peer = (_flip(self.x, ((k + 1) >> 2) & 1), _flip(self.y, ((k + 1) >> 1) & 1), _flip(self.c, (k + 1) & 1))
        return pltpu.make_async_remote_copy(
            src_ref=self.ins[w].at[4 * peer[0] + 2 * peer[1] + peer[2]], dst_ref=self.outs[w].at[self.me],
            send_sem=self.send.at[w, k], recv_sem=self.recv.at[w, k], device_id=peer, device_id_type=MESH)

    def _mine(self, w):
        return pltpu.make_async_copy(self.ins[w].at[self.me], self.outs[w].at[self.me], self.local.at[w])

    def start(self):
        for w in range(len(self.ins)):
            self._mine(w).start()
            for k in range(N_DEV - 1):
                self._copy(w, k).start()

    def finish(self):
        for w in range(len(self.ins)):
            for k in range(N_DEV - 1):
                self._copy(w, k).wait()
            self._mine(w).wait()


class _PairExchange:
    def __init__(self, ins, outs, sems):
        self.ins, self.outs = ins, outs
        self.send, self.recv, _ = sems
        x, y, self.c = _place()
        self.sibling = (x, y, 1 - self.c)

    def _copy(self, w, t):
        return pltpu.make_async_remote_copy(
            src_ref=self.ins[w].at[2 * t + 1 - self.c], dst_ref=self.outs[w].at[t],
            send_sem=self.send.at[w, t], recv_sem=self.recv.at[w, t], device_id=self.sibling, device_id_type=MESH)

    def start(self):
        for w in range(len(self.ins)):
            for t in range(4):
                self._copy(w, t).start()

    def finish(self):
        for w in range(len(self.ins)):
            for t in range(4):
                self._copy(w, t).wait()


class _ChipExchange:
    def __init__(self, ins, outs, sems):
        self.ins, self.outs = ins, outs
        self.send, self.recv, self.local = sems
        self.x, self.y, self.c = _place()
        self.chip = 2 * self.x + self.y

    def _copy(self, w, k):
        px, py = _flip(self.x, ((k + 1) >> 1) & 1), _flip(self.y, (k + 1) & 1)
        return pltpu.make_async_remote_copy(
            src_ref=self.ins[w].at[2 * px + py], dst_ref=self.outs[w].at[self.chip],
            send_sem=self.send.at[w, k], recv_sem=self.recv.at[w, k], device_id=(px, py, self.c),
            device_id_type=MESH)

    def _mine(self, w):
        return pltpu.make_async_copy(self.ins[w].at[self.chip], self.outs[w].at[self.chip], self.local.at[w])

    def start(self):
        for w in range(len(self.ins)):
            self._mine(w).start()
            for k in range(3):
                self._copy(w, k).start()

    def finish(self):
        for w in range(len(self.ins)):
            for k in range(3):
                self._copy(w, k).wait()
            self._mine(w).wait()


def _split_refs(refs, n_in, n_comm, n_out, n_scr):
    a = n_in
    b = a + n_comm
    c = b + n_out
    d = c + n_comm
    e = d + n_scr
    return refs[:a], refs[a:b], refs[b:c], refs[c:d], refs[d:e], refs[e:]


def _all_gather(xs, name):
    nw = len(xs)

    def body(*refs):
        comm = _Gather(refs[:nw], refs[nw:2 * nw], refs[2 * nw:])
        comm.start()
        comm.forward()
        comm.finish()

    return pl.pallas_call(
        body, name=name, in_specs=[ANY] * nw, out_specs=[ANY] * nw,
        out_shape=[jax.ShapeDtypeStruct((N_DEV,) + a.shape, a.dtype) for a in xs],
        scratch_shapes=_comm_sems(nw),
    )(*xs)


def _pair_exchange(xs, name):
    nw = len(xs)

    def body(*refs):
        comm = _PairExchange(refs[:nw], refs[nw:2 * nw], refs[2 * nw:])
        comm.start()
        comm.finish()

    return pl.pallas_call(
        body, name=name, in_specs=[ANY] * nw, out_specs=[ANY] * nw,
        out_shape=[jax.ShapeDtypeStruct((4,) + a.shape[1:], a.dtype) for a in xs],
        scratch_shapes=_comm_sems(nw),
    )(*xs)


def _pair_sum(own, got, name):
    _, r, c = own.shape
    tr = r
    for t in (256, 128, 64, 32, 16):
        if r % t == 0 and r > t:
            tr = t
            break

    def body(own_ref, got_ref, out_ref):
        mine = jnp.where(lax.axis_index("c") == 0, own_ref[:, 0].astype(F32), own_ref[:, 1].astype(F32))
        out_ref[...] = (mine + got_ref[...].astype(F32)).astype(BF16)

    return pl.pallas_call(
        body, name=name, grid=(r // tr,),
        in_specs=[pl.BlockSpec((4, 2, tr, c), lambda i: (0, 0, i, 0)), pl.BlockSpec((4, tr, c), lambda i: (0, i, 0))],
        out_specs=pl.BlockSpec((4, tr, c), lambda i: (0, i, 0)),
        out_shape=jax.ShapeDtypeStruct((4, r, c), BF16),
        compiler_params=_params(("parallel",)),
    )(own.reshape(4, 2, r, c), got)


def _token_spec(k, ksub, nq):
    def index_map(i):
        s = ksub * i + k
        return ((s // nq) * (nq - 1) + jnp.maximum(s % nq, 1) - 1, 0)
    return pl.BlockSpec((128, D_MODEL), index_map)


def _is_lead(i, k, ksub, nq):
    return ((ksub * i + k) % nq) == 0


def _assemble_rows(i, x_refs, meta_ref, nq):
    ksub = len(x_refs)
    lead = jnp.concatenate([jnp.zeros((PAD, D_MODEL), F32), meta_ref[...]], axis=0)
    return jnp.concatenate([jnp.where(_is_lead(i, k, ksub, nq), lead, x_refs[k][...]) for k in range(ksub)], axis=0)


def _swiglu(nb, wgu_ref, wd_ref, gate_ref, up_ref):
    acc = jnp.zeros((nb.shape[0], D_MODEL), F32)
    for a, b in HID_PIECES:
        gate = _dot_nt(nb, wgu_ref[0, a:b, :])
        up = _dot_nt(nb, wgu_ref[1, a:b, :])
        gate_ref[:, a:b] = gate.astype(BF16)
        up_ref[:, a:b] = up.astype(BF16)
        acc = acc + _dot((gate * _sigmoid(gate) * up).astype(BF16), wd_ref[a:b, :])
    return acc


def _ffn1_fwd(x2d, meta, lp, gain, wgu, wd, gather):
    nq = lp // 128
    n = (x2d.shape[0] // (nq - 1)) * nq
    tm = _row_tile(n, 512)
    ksub = tm // 128
    n_i = n // tm
    nw = len(gather)

    def body(*refs):
        x_refs = refs[:ksub]
        (meta_ref, g_ref, wgu_ref, wd_ref), gin, (out_ref, nrm_ref, gate_ref, up_ref), gout, _, sems = \
            _split_refs(refs[ksub:], 4, nw, 4, 0)
        i = pl.program_id(0)
        comm = _Gather(gin, gout, sems)
        pl.when(i == 0)(comm.start)
        pl.when(i == max(n_i - 3, 0))(comm.forward)

        hv = _assemble_rows(i, x_refs, meta_ref, nq)
        y, _, _ = _rms(hv, g_ref[...])
        nb = y.astype(BF16)
        nrm_ref[...] = nb
        out_ref[...] = hv + 0.5 * _swiglu(nb, wgu_ref, wd_ref, gate_ref, up_ref)

        pl.when(i == n_i - 1)(comm.finish)

    rows = pl.BlockSpec((tm, D_MODEL), lambda i: (i, 0))
    hid = pl.BlockSpec((tm, D_FF), lambda i: (i, 0))
    return pl.pallas_call(
        body, name="ffn1_fwd", grid=(n_i,),
        in_specs=[_token_spec(k, ksub, nq) for k in range(ksub)]
        + [pl.BlockSpec((N_META, D_MODEL), lambda i: (0, 0)), pl.BlockSpec((1, D_MODEL), lambda i: (0, 0)),
           _resident(W_GU_SHAPE), _resident(W_D_SHAPE)] + [ANY] * nw,
        out_specs=[rows, rows, hid, hid] + [ANY] * nw,
        out_shape=[jax.ShapeDtypeStruct((n, D_MODEL), F32), jax.ShapeDtypeStruct((n, D_MODEL), BF16),
                   jax.ShapeDtypeStruct((n, D_FF), BF16), jax.ShapeDtypeStruct((n, D_FF), BF16)]
        + [jax.ShapeDtypeStruct((N_DEV,) + a.shape, a.dtype) for a in gather],
        scratch_shapes=_comm_sems(nw),
        compiler_params=_params(("arbitrary",)),
    )(*([x2d] * ksub), meta, gain, wgu, wd, *gather)


def _ffn2_fwd_loss(h, gain, wgu, wd, gfinal, target, lp):
    n = h.shape[0]
    nq = lp // 128
    tm = _row_tile(n, 512)
    ksub = tm // 128
    n_i = n // tm

    def body(*refs):
        t_refs = refs[:ksub]
        h_ref, g_ref, wgu_ref, wd_ref, gf_ref, dh_ref, nrm_ref, gate_ref, up_ref, loss_ref, dgf_ref = refs[ksub:]
        i = pl.program_id(0)

        @pl.when(i == 0)
        def _():
            loss_ref[...] = jnp.zeros_like(loss_ref)
            dgf_ref[...] = jnp.zeros_like(dgf_ref)

        hv = h_ref[...]
        y, _, _ = _rms(hv, g_ref[...])
        nb = y.astype(BF16)
        nrm_ref[...] = nb
        hout = hv + 0.5 * _swiglu(nb, wgu_ref, wd_ref, gate_ref, up_ref)

        gf = gf_ref[...]
        loss = jnp.zeros((1, 1), F32)
        dgf = jnp.zeros((1, D_MODEL), F32)
        for k in range(ksub):
            yk, xhat, r = _rms(hout[128 * k:128 * (k + 1)], gf)
            err = jnp.where(_is_lead(i, k, ksub, nq), 0.0, yk - t_refs[k][...])
            loss = loss + 0.5 * jnp.sum(jnp.sum(err * err, axis=1, keepdims=True), axis=0,
                                        keepdims=True) * (1.0 / D_MODEL)
            dy = err * (1.0 / D_MODEL)
            dh_ref[128 * k:128 * (k + 1), :] = _rms_bwd(dy, xhat, r, gf)
            dgf = dgf + jnp.sum(dy * xhat, axis=0, keepdims=True)
        loss_ref[...] += loss
        dgf_ref[...] += dgf

    rows = pl.BlockSpec((tm, D_MODEL), lambda i: (i, 0))
    hid = pl.BlockSpec((tm, D_FF), lambda i: (i, 0))
    vec = pl.BlockSpec((1, D_MODEL), lambda i: (0, 0))
    return pl.pallas_call(
        body, name="ffn2_fwd_loss", grid=(n_i,),
        in_specs=[_token_spec(k, ksub, nq) for k in range(ksub)]
        + [rows, vec, _resident(W_GU_SHAPE), _resident(W_D_SHAPE), vec],
        out_specs=[rows, rows, hid, hid, pl.BlockSpec((1, 1), lambda i: (0, 0)), vec],
        out_shape=[jax.ShapeDtypeStruct((n, D_MODEL), F32), jax.ShapeDtypeStruct((n, D_MODEL), BF16),
                   jax.ShapeDtypeStruct((n, D_FF), BF16), jax.ShapeDtypeStruct((n, D_FF), BF16),
                   jax.ShapeDtypeStruct((1, 1), F32), jax.ShapeDtypeStruct((1, D_MODEL), F32)],
        compiler_params=_params(("arbitrary",)),
    )(*([target] * ksub), h, gain, wgu, wd, gfinal)


def _ffn_bwd_act_wd(dh_out, gate, up, wd, name, exchange=()):
    n = dh_out.shape[0]
    tm = _row_tile(n, 256)
    n_i = n // tm
    nw = len(exchange)

    def body(*refs):
        (dh_ref, gate_ref, up_ref, wd_ref), xin, (dgate_ref, dup_ref, dw_ref), xout, (acc_scr,), sems = \
            _split_refs(refs, 4, nw, 3, 1)
        i = pl.program_id(0)
        if nw:
            comm = _Exchange(xin, xout, sems)
            pl.when(i == 0)(comm.start)

        @pl.when(i == 0)
        def _():
            acc_scr[...] = jnp.zeros_like(acc_scr)

        dhb = (0.5 * dh_ref[...]).astype(BF16)
        for a, b in HID_PIECES:
            da = _dot_nt(dhb, wd_ref[a:b, :])
            g = gate_ref[:, a:b].astype(F32)
            u = up_ref[:, a:b].astype(F32)
            sig = _sigmoid(g)
            silu = g * sig
            dgate_ref[:, a:b] = (da * u * (sig * (1.0 + g * (1.0 - sig)))).astype(BF16)
            dup_ref[:, a:b] = (da * silu).astype(BF16)
            acc_scr[a:b, :] += _dot_tn((silu * u).astype(BF16), dhb)

        @pl.when(i == n_i - 1)
        def _():
            dw_ref[...] = acc_scr[...].astype(BF16)

        if nw:
            pl.when(i == n_i - 1)(comm.finish)

    rows = pl.BlockSpec((tm, D_MODEL), lambda i: (i, 0))
    hid = pl.BlockSpec((tm, D_FF), lambda i: (i, 0))
    return pl.pallas_call(
        body, name=name, grid=(n_i,),
        in_specs=[rows, hid, hid, _resident(W_D_SHAPE)] + [ANY] * nw,
        out_specs=[hid, hid, _resident(W_D_SHAPE)] + [ANY] * nw,
        out_shape=[jax.ShapeDtypeStruct((n, D_FF), BF16)] * 2 + [jax.ShapeDtypeStruct(W_D_SHAPE, BF16)]
        + [jax.ShapeDtypeStruct(a.shape, a.dtype) for a in exchange],
        scratch_shapes=[pltpu.VMEM(W_D_SHAPE, F32)] + (_comm_sems(nw) if nw else []),
        compiler_params=_params(("arbitrary",)),
    )(dh_out, gate, up, wd, *exchange)


def _ffn_bwd_in(dh_out, h_in, gain, dgate, dup, wgu, name, tokens=None, exchange=()):
    n = dh_out.shape[0]
    tm = _row_tile(n, 512)
    n_i = n // tm
    nw = len(exchange)
    ksub, nq = (tm // 128, tokens[2] // 128) if tokens else (1, 0)

    def body(*refs):
        h_refs = refs[:ksub]
        (meta_ref, dh_ref, g_ref, dgate_ref, dup_ref, wgu_ref), xin, (dhin_ref, dgain_ref), xout, _, sems = \
            _split_refs(refs[ksub:], 6, nw, 2, 0)
        i = pl.program_id(0)
        if nw:
            comm = _ChipExchange(xin, xout, sems)
            pl.when(i == 0)(comm.start)

        @pl.when(i == 0)
        def _():
            dgain_ref[...] = jnp.zeros_like(dgain_ref)

        dn = jnp.zeros((tm, D_MODEL), F32)
        for a, b in HID_PIECES:
            dn = dn + _dot(dgate_ref[:, a:b], wgu_ref[0, a:b, :]) + _dot(dup_ref[:, a:b], wgu_ref[1, a:b, :])
        gain_v = g_ref[...]
        hv = _assemble_rows(i, h_refs, meta_ref, nq) if tokens else h_refs[0][...]
        _, xhat, r = _rms(hv, gain_v)
        dhin_ref[...] = dh_ref[...] + _rms_bwd(dn, xhat, r, gain_v)
        dgain_ref[...] += jnp.sum(dn * xhat, axis=0, keepdims=True)

        if nw:
            pl.when(i == n_i - 1)(comm.finish)

    rows = pl.BlockSpec((tm, D_MODEL), lambda i: (i, 0))
    hid = pl.BlockSpec((tm, D_FF), lambda i: (i, 0))
    vec = pl.BlockSpec((1, D_MODEL), lambda i: (0, 0))
    meta_spec = pl.BlockSpec((N_META, D_MODEL), lambda i: (0, 0))
    if tokens:
        h_specs, h_args, meta = [_token_spec(k, ksub, nq) for k in range(ksub)], [tokens[0]] * ksub, tokens[1]
    else:
        h_specs, h_args, meta = [rows], [h_in], jnp.zeros((N_META, D_MODEL), F32)
    return pl.pallas_call(
        body, name=name, grid=(n_i,),
        in_specs=h_specs + [meta_spec, rows, vec, hid, hid, _resident(W_GU_SHAPE)] + [ANY] * nw,
        out_specs=[rows, vec] + [ANY] * nw,
        out_shape=[jax.ShapeDtypeStruct((n, D_MODEL), F32), jax.ShapeDtypeStruct((1, D_MODEL), F32)]
        + [jax.ShapeDtypeStruct(a.shape, a.dtype) for a in exchange],
        scratch_shapes=_comm_sems(nw) if nw else [],
        compiler_params=_params(("arbitrary",)),
    )(*h_args, meta, dh_out, gain, dgate, dup, wgu, *exchange)


def _ffn_bwd_wgu(nrm, dgate, dup, name, exchange=()):
    n = nrm.shape[0]
    tm = _row_tile(n, 256)
    n_i = n // tm
    nw = len(exchange)

    def body(*refs):
        (nrm_ref, dgate_ref, dup_ref), xin, (dw_ref,), xout, (acc_scr,), sems = _split_refs(refs, 3, nw, 1, 1)
        i = pl.program_id(0)
        if nw:
            comm = _Exchange(xin, xout, sems)
            pl.when(i == 0)(comm.start)

        @pl.when(i == 0)
        def _():
            acc_scr[...] = jnp.zeros_like(acc_scr)

        nb = nrm_ref[...]
        for a, b in HID_PIECES:
            acc_scr[0, a:b, :] += _dot_tn(dgate_ref[:, a:b], nb)
            acc_scr[1, a:b, :] += _dot_tn(dup_ref[:, a:b], nb)

        @pl.when(i == n_i - 1)
        def _():
            dw_ref[...] = acc_scr[...].astype(BF16)

        if nw:
            pl.when(i == n_i - 1)(comm.finish)

    hid = pl.BlockSpec((tm, D_FF), lambda i: (i, 0))
    res = pl.pallas_call(
        body, name=name, grid=(n_i,),
        in_specs=[pl.BlockSpec((tm, D_MODEL), lambda i: (i, 0)), hid, hid] + [ANY] * nw,
        out_specs=[_resident(W_GU_SHAPE)] + [ANY] * nw,
        out_shape=[jax.ShapeDtypeStruct(W_GU_SHAPE, BF16)] + [jax.ShapeDtypeStruct(a.shape, a.dtype) for a in exchange],
        scratch_shapes=[pltpu.VMEM(W_GU_SHAPE, F32)] + (_comm_sems(nw) if nw else []),
        compiler_params=_params(("arbitrary",)),
    )(nrm, dgate, dup, *exchange)
    return res if nw else res[0]


def _inproj_fwd(h, gain, w_in):
    n = h.shape[0]
    tm = _row_tile(n, 512)

    def body(h_ref, g_ref, w_ref, *outs):
        y, _, _ = _rms(h_ref[...], g_ref[...])
        nb = y.astype(BF16)
        for p in range(N_PIECE):
            outs[p][...] = _dot_nt(nb, w_ref[512 * p:512 * (p + 1), :]).astype(BF16)
        outs[N_PIECE][...] = _dot_nt(nb, w_ref[IN_MAIN:IN_PAD, :])

    piece = pl.BlockSpec((tm, 512), lambda i: (i, 0))
    return pl.pallas_call(
        body, name="inproj_fwd", grid=(n // tm,),
        in_specs=[pl.BlockSpec((tm, D_MODEL), lambda i: (i, 0)),
                  pl.BlockSpec((1, D_MODEL), lambda i: (0, 0)),
                  pl.BlockSpec((IN_PAD, D_MODEL), lambda i: (0, 0))],
        out_specs=[piece] * N_PIECE + [pl.BlockSpec((tm, 128), lambda i: (i, 0))],
        out_shape=[jax.ShapeDtypeStruct((n, 512), BF16)] * N_PIECE + [jax.ShapeDtypeStruct((n, 128), F32)],
        compiler_params=_params(("parallel",)),
    )(h, gain, w_in)


def _inproj_bwd(dpieces, dfg, dh_out, h_in, gain, w_in):
    n = h_in.shape[0]
    tm = _row_tile(n, 512)
    n_i = n // tm

    def body(*refs):
        dp_refs = refs[:N_PIECE]
        dfg_ref, dh_ref, h_ref, g_ref, w_ref, dhin_ref, dw_ref, dgain_ref, acc_scr = refs[N_PIECE:]
        i = pl.program_id(0)

        @pl.when(i == 0)
        def _():
            acc_scr[...] = jnp.zeros_like(acc_scr)
            dgain_ref[...] = jnp.zeros_like(dgain_ref)

        gain_v = g_ref[...]
        y, xhat, r = _rms(h_ref[...], gain_v)
        nb = y.astype(BF16)
        dn = jnp.zeros((tm, D_MODEL), F32)
        for p in range(N_PIECE + 1):
            lo, hi = (512 * p, 512 * (p + 1)) if p < N_PIECE else (IN_MAIN, IN_PAD)
            dp = (dp_refs[p][...] if p < N_PIECE else dfg_ref[...]).astype(BF16)
            dn = dn + _dot(dp, w_ref[lo:hi, :])
            acc_scr[lo:hi, :] += _dot_tn(dp, nb)
        dhin_ref[...] = dh_ref[...] + _rms_bwd(dn, xhat, r, gain_v)
        dgain_ref[...] += jnp.sum(dn * xhat, axis=0, keepdims=True)

        @pl.when(i == n_i - 1)
        def _():
            dw_ref[...] = acc_scr[...].astype(BF16)

    piece = pl.BlockSpec((tm, 512), lambda i: (i, 0))
    rows = pl.BlockSpec((tm, D_MODEL), lambda i: (i, 0))
    vec = pl.BlockSpec((1, D_MODEL), lambda i: (0, 0))
    wspec = pl.BlockSpec((IN_PAD, D_MODEL), lambda i: (0, 0))
    return pl.pallas_call(
        body, name="inproj_bwd", grid=(n_i,),
        in_specs=[piece] * N_PIECE + [pl.BlockSpec((tm, 128), lambda i: (i, 0)), rows, rows, vec, wspec],
        out_specs=[rows, wspec, vec],
        out_shape=[jax.ShapeDtypeStruct((n, D_MODEL), F32),
                   jax.ShapeDtypeStruct((IN_PAD, D_MODEL), BF16),
                   jax.ShapeDtypeStruct((1, D_MODEL), F32)],
        scratch_shapes=[pltpu.VMEM((IN_PAD, D_MODEL), F32)],
        compiler_params=_params(("arbitrary",)),
    )(*dpieces, dfg, dh_out, h_in, gain, w_in)


def _outproj_bwd(dh, zc, za, w_out):
    n = dh.shape[0]
    tm = _row_tile(n, 512)
    n_i = n // tm

    def body(dh_ref, zc_ref, za_ref, w_ref, dzc_ref, dza_ref, dw_ref, acc_scr):
        i = pl.program_id(0)

        @pl.when(i == 0)
        def _():
            acc_scr[...] = jnp.zeros_like(acc_scr)

        dhb = dh_ref[...].astype(BF16)
        dzc_ref[...] = _dot_nt(dhb, w_ref[0:CONV_DIM, :]).astype(BF16)
        dza_ref[...] = _dot_nt(dhb, w_ref[CONV_DIM:, :]).astype(BF16)
        acc_scr[0:CONV_DIM, :] += _dot_tn(zc_ref[...], dhb)
        acc_scr[CONV_DIM:, :] += _dot_tn(za_ref[...], dhb)

        @pl.when(i == n_i - 1)
        def _():
            dw_ref[...] = acc_scr[...].astype(BF16)

    half = pl.BlockSpec((tm, 512), lambda i: (i, 0))
    wspec = pl.BlockSpec((D_MODEL, D_MODEL), lambda i: (0, 0))
    return pl.pallas_call(
        body, name="outproj_bwd", grid=(n_i,),
        in_specs=[pl.BlockSpec((tm, D_MODEL), lambda i: (i, 0)), half, half, wspec],
        out_specs=[half, half, wspec],
        out_shape=[jax.ShapeDtypeStruct((n, 512), BF16), jax.ShapeDtypeStruct((n, 512), BF16),
                   jax.ShapeDtypeStruct((D_MODEL, D_MODEL), BF16)],
        scratch_shapes=[pltpu.VMEM((D_MODEL, D_MODEL), F32)],
        compiler_params=_params(("arbitrary",)),
    )(dh, zc, za, w_out)


def _group_matrix():
    r = lax.broadcasted_iota(jnp.int32, (128, 128), 0) // HEAD_DIM
    c = lax.broadcasted_iota(jnp.int32, (128, 128), 1) // HEAD_DIM
    return jnp.where(r == c, 1.0 / HEAD_DIM, 0.0).astype(BF16)


def _group_mean(x, gmat):
    hi = x.astype(BF16)
    lo = (x - hi.astype(F32)).astype(BF16)
    return _dot(hi, gmat) + _dot(lo, gmat)


def _shift_rows(x, s):
    rows = x.shape[0]
    t = lax.broadcasted_iota(jnp.int32, x.shape, 0)
    rolled = pltpu.roll(x, s % rows, 0)
    keep = (t >= s) if s > 0 else (t < rows + s)
    return jnp.where(keep, rolled, 0.0)


def _conv_parts(bg_ref, cg_ref, hc_ref, w_ref):
    bg = bg_ref[...].astype(F32)
    cg = cg_ref[...].astype(F32)
    hc = hc_ref[...].astype(F32)
    u = cg * hc
    u1 = _shift_rows(u, 1)
    u2 = _shift_rows(u, 2)
    conv = w_ref[2:3, :] * u + w_ref[1:2, :] * u1 + w_ref[0:1, :] * u2
    return bg, cg, hc, u, u1, u2, conv


def _conv_fwd(bg, cg, hc, conv_w, gain, gmat, lp):
    n = bg.shape[0]
    nb = n // lp

    def body(bg_ref, cg_ref, hc_ref, w_ref, g_ref, gm_ref, z_ref):
        bgv, _, _, _, _, _, conv = _conv_parts(bg_ref, cg_ref, hc_ref, w_ref)
        yc = bgv * conv
        r = lax.rsqrt(_group_mean(yc * yc, gm_ref[...]) + EPS)
        z_ref[...] = (yc * r * g_ref[...]).astype(BF16)

    blk = pl.BlockSpec((lp, 128), lambda c, b: (b, c))
    return pl.pallas_call(
        body, name="conv_fwd", grid=(CONV_DIM // 128, nb),
        in_specs=[blk, blk, blk, pl.BlockSpec((3, 128), lambda c, b: (0, c)),
                  pl.BlockSpec((1, 128), lambda c, b: (0, c)), pl.BlockSpec((128, 128), lambda c, b: (0, 0))],
        out_specs=blk,
        out_shape=jax.ShapeDtypeStruct((n, CONV_DIM), BF16),
        compiler_params=_params(("parallel", "parallel")),
    )(bg, cg, hc, conv_w, gain, gmat)


def _conv_bwd(dz, bg, cg, hc, conv_w, gain, gmat, lp):
    n = bg.shape[0]
    nb = n // lp

    def body(dz_ref, bg_ref, cg_ref, hc_ref, w_ref, g_ref, gm_ref,
             dbg_ref, dcg_ref, dhc_ref, dw_ref, dgain_ref):
        b = pl.program_id(1)

        @pl.when(b == 0)
        def _():
            dw_ref[...] = jnp.zeros_like(dw_ref)
            dgain_ref[...] = jnp.zeros_like(dgain_ref)

        bgv, cgv, hcv, u, u1, u2, conv = _conv_parts(bg_ref, cg_ref, hc_ref, w_ref)
        gm = gm_ref[...]
        yc = bgv * conv
        r = lax.rsqrt(_group_mean(yc * yc, gm) + EPS)
        yhat = yc * r
        dzv = dz_ref[...].astype(F32)
        dyhat = dzv * g_ref[...]
        dgain_ref[...] += jnp.sum(dzv * yhat, axis=0, keepdims=True)
        dyc = r * (dyhat - yhat * _group_mean(dyhat * yhat, gm))
        dbg_ref[...] = (dyc * conv).astype(BF16)
        dconv = dyc * bgv
        du = (w_ref[2:3, :] * dconv + w_ref[1:2, :] * _shift_rows(dconv, -1)
              + w_ref[0:1, :] * _shift_rows(dconv, -2))
        dcg_ref[...] = (du * hcv).astype(BF16)
        dhc_ref[...] = (du * cgv).astype(BF16)
        dw_ref[0:1, :] += jnp.sum(dconv * u2, axis=0, keepdims=True)
        dw_ref[1:2, :] += jnp.sum(dconv * u1, axis=0, keepdims=True)
        dw_ref[2:3, :] += jnp.sum(dconv * u, axis=0, keepdims=True)

    blk = pl.BlockSpec((lp, 128), lambda c, b: (b, c))
    wspec = pl.BlockSpec((3, 128), lambda c, b: (0, c))
    gspec = pl.BlockSpec((1, 128), lambda c, b: (0, c))
    return pl.pallas_call(
        body, name="conv_bwd", grid=(CONV_DIM // 128, nb),
        in_specs=[blk, blk, blk, blk, wspec, gspec, pl.BlockSpec((128, 128), lambda c, b: (0, 0))],
        out_specs=[blk, blk, blk, wspec, gspec],
        out_shape=[jax.ShapeDtypeStruct((n, CONV_DIM), BF16)] * 3
        + [jax.ShapeDtypeStruct((3, CONV_DIM), F32), jax.ShapeDtypeStruct((1, CONV_DIM), F32)],
        compiler_params=_params(("parallel", "arbitrary")),
    )(dz, bg, cg, hc, conv_w, gain, gmat)


KEY_MASKED = 1e30
ONE_LANE = 24


def _scan_steps(rows):
    s, out = 1, []
    while s < rows:
        out.append(s)
        s *= 2
    return out


def _fgate_fwd(fg, b_f, lp):
    n = fg.shape[0]
    nb = n // lp

    def body(fg_ref, b_ref, ka_ref, qa_ref):
        x = fg_ref[...] + b_ref[...]
        logf = jnp.minimum(x, 0.0) - jnp.log(1.0 + jnp.exp(-jnp.abs(x)))
        t = lax.broadcasted_iota(jnp.int32, (lp, 128), 0)
        lane = lax.broadcasted_iota(jnp.int32, (lp, 128), 1)
        f = jnp.where((t >= PAD) & (lane < N_HEADS), logf, 0.0)
        for s in _scan_steps(lp):
            f = f + _shift_rows(f, s)
        hi = f.astype(BF16).astype(F32)
        rest = f - hi
        mid = rest.astype(BF16).astype(F32)
        lo = (rest - mid).astype(BF16).astype(F32)
        ones = jnp.where((lane >= ONE_LANE) & (lane < ONE_LANE + 3), 1.0, 0.0)
        hi_key = jnp.where((t < PAD) & (lane < N_HEADS), KEY_MASKED, hi)
        ka_ref[...] = (hi_key + pltpu.roll(mid, 8, 1) + pltpu.roll(lo, 16, 1) + ones).astype(BF16)
        for h in range(N_HEADS):
            minus = jnp.where((lane == h) | (lane == 8 + h) | (lane == 16 + h), -1.0, 0.0)
            terms = (jnp.where(lane == ONE_LANE, pltpu.roll(hi, ONE_LANE - h, 1), 0.0)
                     + jnp.where(lane == ONE_LANE + 1, pltpu.roll(mid, ONE_LANE + 1 - h, 1), 0.0)
                     + jnp.where(lane == ONE_LANE + 2, pltpu.roll(lo, ONE_LANE + 2 - h, 1), 0.0))
            qa_ref[:, 128 * h:128 * (h + 1)] = (minus + terms).astype(BF16)

    return pl.pallas_call(
        body, name="fgate_fwd", grid=(nb,),
        in_specs=[pl.BlockSpec((lp, 128), lambda b: (b, 0)), pl.BlockSpec((1, 128), lambda b: (0, 0))],
        out_specs=[pl.BlockSpec((lp, 128), lambda b: (b, 0)), pl.BlockSpec((lp, N_HEADS * 128), lambda b: (b, 0))],
        out_shape=[jax.ShapeDtypeStruct((n, 128), BF16), jax.ShapeDtypeStruct((n, N_HEADS * 128), BF16)],
        compiler_params=_params(("parallel",)),
    )(fg, b_f)


def _fgate_bwd(dka, dfr, fg, b_f, lp):
    n = fg.shape[0]
    nb = n // lp

    def body(dka_ref, dfr_ref, fg_ref, b_ref, dfg_ref, db_ref):
        b = pl.program_id(0)

        @pl.when(b == 0)
        def _():
            db_ref[...] = jnp.zeros_like(db_ref)

        wide = jnp.concatenate([dfr_ref[0], jnp.zeros((128 - N_HEADS, lp), F32)], axis=0)
        t = lax.broadcasted_iota(jnp.int32, (lp, 128), 0)
        lane = lax.broadcasted_iota(jnp.int32, (lp, 128), 1)
        d = jnp.where(lane < N_HEADS, dka_ref[...], 0.0) + wide.T
        for s in _scan_steps(lp):
            d = d + _shift_rows(d, -s)
        x = fg_ref[...] + b_ref[...]
        dx = jnp.where((t >= PAD) & (lane < N_HEADS), d * _sigmoid(-x), 0.0)
        dfg_ref[...] = dx
        db_ref[...] += jnp.sum(dx, axis=0, keepdims=True)

    return pl.pallas_call(
        body, name="fgate_bwd", grid=(nb,),
        in_specs=[pl.BlockSpec((lp, 128), lambda b: (b, 0)), pl.BlockSpec((1, N_HEADS, lp), lambda b: (b, 0, 0)),
                  pl.BlockSpec((lp, 128), lambda b: (b, 0)), pl.BlockSpec((1, 128), lambda b: (0, 0))],
        out_specs=[pl.BlockSpec((lp, 128), lambda b: (b, 0)), pl.BlockSpec((1, 128), lambda b: (0, 0))],
        out_shape=[jax.ShapeDtypeStruct((n, 128), F32), jax.ShapeDtypeStruct((1, 128), F32)],
        compiler_params=_params(("arbitrary",)),
    )(dka, dfr, fg, b_f)


def _head_masks():
    lane = lax.broadcasted_iota(jnp.int32, (1, 128), 1)
    return lane < HEAD_DIM


def _stack_heads(x2, first):
    zero = jnp.zeros_like(x2)
    return jnp.concatenate([jnp.where(first, x2, zero), jnp.where(first, zero, x2)], axis=0)


def _stack_heads_lanes(xt):
    r = lax.broadcasted_iota(jnp.int32, xt.shape, 0)
    zero = jnp.zeros_like(xt)
    return jnp.concatenate([jnp.where(r < HEAD_DIM, xt, zero), jnp.where(r < HEAD_DIM, zero, xt)], axis=1)


def _pair_cols(col0, col1, first):
    return jnp.where(first, col0, col1)


def _pair_rows(row0, row1):
    r = lax.broadcasted_iota(jnp.int32, (128, TQ), 0)
    return jnp.where(r < HEAD_DIM, row0, row1)


def _query_side(q_ref, qa_ref, p, first):
    q2 = q_ref[:, 128 * p:128 * (p + 1)] * 0.125
    zero = jnp.zeros_like(q2)
    top = jnp.concatenate([jnp.where(first, q2, zero), qa_ref[:, 128 * (2 * p):128 * (2 * p + 1)]], axis=1)
    bot = jnp.concatenate([jnp.where(first, zero, q2), qa_ref[:, 128 * (2 * p + 1):128 * (2 * p + 2)]], axis=1)
    return jnp.concatenate([top, bot], axis=0)


def _key_chunks(lp):
    return (lp + TK - 1) // TK


def _chunk_mask(i, c, tk):
    r = lax.broadcasted_iota(jnp.int32, (tk, 2 * TQ), 0)
    col = lax.broadcasted_iota(jnp.int32, (tk, 2 * TQ), 1)
    return (c * TK + r) <= (i * TQ + (col & (TQ - 1)))


def _causal_sweep(i, step, init):
    per = TK // TQ
    last = i // per
    carry = lax.fori_loop(0, last, lambda c, carry: step(c, carry, False, TK), init)
    tails = [lambda carry, r=r: step(last, carry, True, TQ * (r + 1)) for r in range(per)]
    return lax.switch(i % per, tails, carry)


def _transpose_bf16(x):
    return x.astype(F32).T.astype(BF16)


def _attn_fwd(q, qa, k, v, ka, gain, zc, w_out, h, lp):
    n = q.shape[0]
    nb = n // lp
    nq = lp // TQ
    lpp = _key_chunks(lp) * TK

    def body(q_ref, qa_ref, k_ref, v_ref, ka_ref, g_ref, zc_ref, w_ref, h_ref,
             z_ref, o_ref, lse_ref, hout_ref, kx_scr, vt_scr):
        i = pl.program_id(1)
        first = _head_masks()

        @pl.when(i == 0)
        def _():
            if lpp > lp:
                kx_scr[lp:lpp, :] = jnp.zeros((lpp - lp, 2 * ATTN_DIM), BF16)
                vt_scr[:, lp:lpp] = jnp.zeros((ATTN_DIM, lpp - lp), BF16)
            for p in range(N_PAIRS):
                kx_scr[0:lp, 256 * p:256 * p + 128] = k_ref[:, 128 * p:128 * (p + 1)]
                kx_scr[0:lp, 256 * p + 128:256 * (p + 1)] = ka_ref[...]
            vt_scr[:, 0:lp] = _transpose_bf16(v_ref[...])

        rhs_t = [_transpose_bf16(_query_side(q_ref, qa_ref, p, first)) for p in range(N_PAIRS)]

        def step(c, carry, masked, tk):
            koff = pl.multiple_of(c * TK, TK)
            valid = _chunk_mask(i, c, tk) if masked else None
            new = []
            for p in range(N_PAIRS):
                m, l, acc = carry[p]
                st = _dot(kx_scr[pl.ds(koff, tk), 256 * p:256 * (p + 1)], rhs_t[p])
                if masked:
                    st = jnp.where(valid, st, NEG)
                m_new = jnp.maximum(m, jnp.max(st, axis=0, keepdims=True))
                pt = jnp.exp(st - m_new)
                alpha = jnp.exp(m - m_new)
                l = alpha * l + jnp.sum(pt, axis=0, keepdims=True)
                pb = pt.astype(BF16)
                vt = _stack_heads_lanes(vt_scr[128 * p:128 * (p + 1), pl.ds(koff, tk)])
                pv = _dot(vt, jnp.concatenate([pb[:, 0:TQ], pb[:, TQ:]], axis=0))
                acc = acc * _pair_rows(alpha[:, 0:TQ], alpha[:, TQ:]) + pv
                new.append((m_new, l, acc))
            return tuple(new)

        init = tuple((jnp.full((1, 2 * TQ), NEG, F32), jnp.zeros((1, 2 * TQ), F32), jnp.zeros((128, TQ), F32))
                     for _ in range(N_PAIRS))
        final = _causal_sweep(i, step, init)

        row = lax.broadcasted_iota(jnp.int32, (TQ, 128), 0)
        real = (i * TQ + row) >= PAD
        hout = h_ref[...] + _dot(zc_ref[...], w_ref[0:CONV_DIM, :])
        for p in range(N_PAIRS):
            m, l, acc = final[p]
            inv = 1.0 / l
            ot = acc * _pair_rows(inv[:, 0:TQ], inv[:, TQ:])
            sq = ot * ot
            r0 = lax.rsqrt(jnp.sum(sq[0:HEAD_DIM], axis=0, keepdims=True) * (1.0 / HEAD_DIM) + EPS)
            r1 = lax.rsqrt(jnp.sum(sq[HEAD_DIM:], axis=0, keepdims=True) * (1.0 / HEAD_DIM) + EPS)
            cols = slice(128 * p, 128 * (p + 1))
            o_ref[:, cols] = jnp.where(real, ot.T, 0.0).astype(BF16)
            z = (jnp.where(real, (ot * _pair_rows(r0, r1)).T, 0.0) * g_ref[:, cols]).astype(BF16)
            z_ref[:, cols] = z
            hout = hout + _dot(z, w_ref[CONV_DIM + 128 * p:CONV_DIM + 128 * (p + 1), :])
            lse = m + jnp.log(l)
            lse_ref[0, 2 * p:2 * p + 1, :] = lse[:, 0:TQ]
            lse_ref[0, 2 * p + 1:2 * p + 2, :] = lse[:, TQ:]
        hout_ref[...] = hout

    qblk = pl.BlockSpec((TQ, ATTN_DIM), lambda b, i: (b * nq + i, 0))
    qablk = pl.BlockSpec((TQ, N_HEADS * 128), lambda b, i: (b * nq + i, 0))
    seq = pl.BlockSpec((lp, ATTN_DIM), lambda b, i: (b, 0))
    rowblk = pl.BlockSpec((1, N_HEADS, TQ), lambda b, i: (b, 0, i))
    hblk = pl.BlockSpec((TQ, D_MODEL), lambda b, i: (b * nq + i, 0))
    return pl.pallas_call(
        body, name="attn_fwd", grid=(nb, nq),
        in_specs=[qblk, qablk, seq, seq, pl.BlockSpec((lp, 128), lambda b, i: (b, 0)),
                  pl.BlockSpec((1, ATTN_DIM), lambda b, i: (0, 0)), qblk,
                  pl.BlockSpec((D_MODEL, D_MODEL), lambda b, i: (0, 0)), hblk],
        out_specs=[qblk, qblk, rowblk, hblk],
        out_shape=[jax.ShapeDtypeStruct((n, ATTN_DIM), BF16), jax.ShapeDtypeStruct((n, ATTN_DIM), BF16),
                   jax.ShapeDtypeStruct((nb, N_HEADS, lp), F32), jax.ShapeDtypeStruct((n, D_MODEL), F32)],
        scratch_shapes=[pltpu.VMEM((lpp, 2 * ATTN_DIM), BF16), pltpu.VMEM((ATTN_DIM, lpp), BF16)],
        compiler_params=_params(("parallel", "arbitrary")),
    )(q, qa, k, v, ka, gain, zc, w_out, h)


def _attn_bwd(dz, q, qa, k, v, ka, o, lse, gain, lp, exchange=()):
    n = q.shape[0]
    nb = n // lp
    nq = lp // TQ
    lpp = _key_chunks(lp) * TK
    nw = len(exchange)

    def body(*refs):
        ((dz_ref, q_ref, qa_ref, k_ref, v_ref, ka_ref, o_ref, lse_ref, g_ref), xin,
         (dq_ref, dk_ref, dv_ref, dka_ref, dfr_ref, dgain_ref), xout,
         (kx_scr, vx_scr, kt_scr, dkx_scr, dvx_scr), sems) = _split_refs(refs, 9, nw, 6, 5)
        b = pl.program_id(0)
        i = pl.program_id(1)
        first = _head_masks()
        if nw:
            comm = _Exchange(xin, xout, sems)
            pl.when((b == 0) & (i == 0))(comm.start)

        @pl.when((b == 0) & (i == 0))
        def _():
            dgain_ref[...] = jnp.zeros_like(dgain_ref)

        @pl.when(i == 0)
        def _():
            if lpp > lp:
                kx_scr[lp:lpp, :] = jnp.zeros((lpp - lp, 2 * ATTN_DIM), BF16)
                vx_scr[lp:lpp, :] = jnp.zeros((lpp - lp, ATTN_DIM), BF16)
                kt_scr[:, lp:lpp] = jnp.zeros((ATTN_DIM, lpp - lp), BF16)
            for p in range(N_PAIRS):
                kx_scr[0:lp, 256 * p:256 * p + 128] = k_ref[:, 128 * p:128 * (p + 1)]
                kx_scr[0:lp, 256 * p + 128:256 * (p + 1)] = ka_ref[...]
            vx_scr[0:lp, :] = v_ref[...]
            kt_scr[:, 0:lp] = _transpose_bf16(k_ref[...])
            dkx_scr[...] = jnp.zeros_like(dkx_scr)
            dvx_scr[...] = jnp.zeros_like(dvx_scr)

        rhs, rhs_t, lses, dos, dos_t, deltas = [], [], [], [], [], []
        for p in range(N_PAIRS):
            cols = slice(128 * p, 128 * (p + 1))
            side = _query_side(q_ref, qa_ref, p, first)
            rhs.append(side)
            rhs_t.append(_transpose_bf16(side))
            lses.append(jnp.concatenate([lse_ref[0, 2 * p:2 * p + 1, :], lse_ref[0, 2 * p + 1:2 * p + 2, :]], axis=1))
            ov = o_ref[:, cols].astype(F32)
            dzv = dz_ref[:, cols].astype(F32)
            gv = g_ref[:, cols]
            sq = ov * ov
            ms0 = jnp.sum(jnp.where(first, sq, 0.0), axis=1, keepdims=True) * (1.0 / HEAD_DIM)
            ms1 = jnp.sum(jnp.where(first, 0.0, sq), axis=1, keepdims=True) * (1.0 / HEAD_DIM)
            r = _pair_cols(lax.rsqrt(ms0 + EPS), lax.rsqrt(ms1 + EPS), first)
            ohat = ov * r
            dyhat = dzv * gv
            dgain_ref[:, cols] += jnp.sum(dzv * ohat, axis=0, keepdims=True)
            pr = dyhat * ohat
            mean0 = jnp.sum(jnp.where(first, pr, 0.0), axis=1, keepdims=True) * (1.0 / HEAD_DIM)
            mean1 = jnp.sum(jnp.where(first, 0.0, pr), axis=1, keepdims=True) * (1.0 / HEAD_DIM)
            do = r * (dyhat - ohat * _pair_cols(mean0, mean1, first))
            ddt = (do * ov).T
            deltas.append(jnp.concatenate([jnp.sum(ddt[0:HEAD_DIM], axis=0, keepdims=True),
                                           jnp.sum(ddt[HEAD_DIM:], axis=0, keepdims=True)], axis=1))
            do_st = _stack_heads(do.astype(BF16), first)
            dos.append(do_st)
            dos_t.append(_transpose_bf16(do_st))

        def step(c, carry, masked, tk):
            koff = pl.multiple_of(c * TK, TK)
            valid = _chunk_mask(i, c, tk) if masked else None
            new = []
            for p in range(N_PAIRS):
                dqt, dfq = carry[p]
                ext = slice(256 * p, 256 * (p + 1))
                cols = slice(128 * p, 128 * (p + 1))
                st = _dot(kx_scr[pl.ds(koff, tk), ext], rhs_t[p])
                if masked:
                    st = jnp.where(valid, st, NEG)
                pt = jnp.exp(st - lses[p])
                dpt = _dot(vx_scr[pl.ds(koff, tk), cols], dos_t[p])
                dst = pt * (dpt - deltas[p])
                dsb = dst.astype(BF16)
                dfq = dfq + jnp.sum(dsb.astype(F32), axis=0, keepdims=True)
                dkx_scr[pl.ds(koff, tk), ext] += _dot(dsb, rhs[p])
                dvx_scr[pl.ds(koff, tk), cols] += _dot(pt.astype(BF16), dos[p])
                kt = _stack_heads_lanes(kt_scr[cols, pl.ds(koff, tk)])
                dqt = dqt + _dot(kt, jnp.concatenate([dsb[:, 0:TQ], dsb[:, TQ:]], axis=0))
                new.append((dqt, dfq))
            return tuple(new)

        init = tuple((jnp.zeros((128, TQ), F32), jnp.zeros((1, 2 * TQ), F32)) for _ in range(N_PAIRS))
        final = _causal_sweep(i, step, init)

        for p in range(N_PAIRS):
            dqt, dfq = final[p]
            dq_ref[:, 128 * p:128 * (p + 1)] = (dqt.T * 0.125).astype(BF16)
            dfr_ref[0, 2 * p:2 * p + 1, :] = dfq[:, 0:TQ]
            dfr_ref[0, 2 * p + 1:2 * p + 2, :] = dfq[:, TQ:]

        @pl.when(i == nq - 1)
        def _():
            dka = jnp.zeros((lp, 128), F32)
            for p in range(N_PAIRS):
                dk_ref[:, 128 * p:128 * (p + 1)] = dkx_scr[0:lp, 256 * p:256 * p + 128].astype(BF16)
                dka = dka + dkx_scr[0:lp, 256 * p + 128:256 * (p + 1)]
            dka_ref[...] = dka
            dv_ref[...] = dvx_scr[0:lp, :].astype(BF16)

        if nw:
            pl.when((b == nb - 1) & (i == nq - 1))(comm.finish)

    qblk = pl.BlockSpec((TQ, ATTN_DIM), lambda b, i: (b * nq + i, 0))
    qablk = pl.BlockSpec((TQ, N_HEADS * 128), lambda b, i: (b * nq + i, 0))
    seq = pl.BlockSpec((lp, ATTN_DIM), lambda b, i: (b, 0))
    kaseq = pl.BlockSpec((lp, 128), lambda b, i: (b, 0))
    rowblk = pl.BlockSpec((1, N_HEADS, TQ), lambda b, i: (b, 0, i))
    gspec = pl.BlockSpec((1, ATTN_DIM), lambda b, i: (0, 0))
    return pl.pallas_call(
        body, name="attn_bwd", grid=(nb, nq),
        in_specs=[qblk, qblk, qablk, seq, seq, kaseq, qblk, rowblk, gspec] + [ANY] * nw,
        out_specs=[qblk, seq, seq, kaseq, rowblk, gspec] + [ANY] * nw,
        out_shape=[jax.ShapeDtypeStruct((n, ATTN_DIM), BF16), jax.ShapeDtypeStruct((n, ATTN_DIM), BF16),
                   jax.ShapeDtypeStruct((n, ATTN_DIM), BF16), jax.ShapeDtypeStruct((n, 128), F32),
                   jax.ShapeDtypeStruct((nb, N_HEADS, lp), F32), jax.ShapeDtypeStruct((1, ATTN_DIM), F32)]
        + [jax.ShapeDtypeStruct(a.shape, a.dtype) for a in exchange],
        scratch_shapes=[pltpu.VMEM((lpp, 2 * ATTN_DIM), BF16), pltpu.VMEM((lpp, ATTN_DIM), BF16),
                        pltpu.VMEM((ATTN_DIM, lpp), BF16), pltpu.VMEM((lpp, 2 * ATTN_DIM), F32),
                        pltpu.VMEM((lpp, ATTN_DIM), F32)] + (_comm_sems(nw) if nw else []),
        compiler_params=_params(("arbitrary", "arbitrary")),
    )(dz, q, qa, k, v, ka, o, lse, gain, *exchange)


def _adamw(parts, w, m, v, name):
    s_parts, r, c = parts.shape
    tr = r
    for t in (256, 128, 64, 32, 16):
        if r % t == 0 and r > t:
            tr = t
            break

    def body(p_ref, w_ref, m_ref, v_ref, g_ref, d_ref, nm_ref, nv_ref):
        g = p_ref[0].astype(F32)
        for s in range(1, s_parts):
            g = g + p_ref[s].astype(F32)
        nm = ADAM_B1 * m_ref[...] + (1.0 - ADAM_B1) * g
        nv = ADAM_B2 * v_ref[...] + (1.0 - ADAM_B2) * (g * g)
        m_hat = nm / (1.0 - ADAM_B1 ** ADAM_STEP)
        v_hat = nv / (1.0 - ADAM_B2 ** ADAM_STEP)
        g_ref[...] = g
        d_ref[...] = -ADAM_LR * (m_hat / (jnp.sqrt(v_hat) + ADAM_EPS) + ADAM_WD * w_ref[...])
        nm_ref[...] = nm
        nv_ref[...] = nv

    blk = pl.BlockSpec((tr, c), lambda i: (i, 0))
    return pl.pallas_call(
        body, name=name, grid=(r // tr,),
        in_specs=[pl.BlockSpec((s_parts, tr, c), lambda i: (0, i, 0)), blk, blk, blk],
        out_specs=[blk] * 4,
        out_shape=[jax.ShapeDtypeStruct((r, c), F32)] * 4,
        compiler_params=_params(("parallel",)),
    )(parts, w, m, v)


def _sum_parts(parts, name):
    s_parts, r, c = parts.shape

    def body(p_ref, out_ref):
        acc = p_ref[0]
        for s in range(1, s_parts):
            acc = acc + p_ref[s]
        out_ref[...] = acc

    return pl.pallas_call(
        body, name=name, out_shape=jax.ShapeDtypeStruct((r, c), F32),
        in_specs=[pl.BlockSpec(memory_space=pltpu.VMEM)], out_specs=pl.BlockSpec(memory_space=pltpu.VMEM),
    )(parts)


SMALL_ROWS = 184


def _pack_small(d_gains, d_gc, d_ga, d_bf, d_conv, d_meta):
    rows = [g.reshape(8, 128) for g in d_gains]
    rows += [d_gc.reshape(4, 128), d_ga.reshape(4, 128), d_bf.reshape(1, 128)]
    rows += [d_conv.reshape(12, 128), d_meta.reshape(128, 128)]
    packed = jnp.concatenate(rows, axis=0)
    return jnp.pad(packed, ((0, SMALL_ROWS - packed.shape[0]), (0, 0)))


def kernel(x, meta_tokens, ffn1_norm, ffn1_w_gu, ffn1_w_down, mix_norm, w_in, conv_w, b_f, out_norm_conv, out_norm_attn, w_out, ffn2_norm, ffn2_w_gu, ffn2_w_down, final_norm, loss_target, m_meta_tokens, m_ffn1_norm, m_ffn1_w_gu, m_ffn1_w_down, m_mix_norm, m_w_in, m_conv_w, m_b_f, m_out_norm_conv, m_out_norm_attn, m_w_out, m_ffn2_norm, m_ffn2_w_gu, m_ffn2_w_down, m_final_norm, v_meta_tokens, v_ffn1_norm, v_ffn1_w_gu, v_ffn1_w_down, v_mix_norm, v_w_in, v_conv_w, v_b_f, v_out_norm_conv, v_out_norm_attn, v_w_out, v_ffn2_norm, v_ffn2_w_gu, v_ffn2_w_down, v_final_norm):
    nb, seq, _ = x.shape
    lp = PAD + N_META + seq
    me = 4 * lax.axis_index("x") + 2 * lax.axis_index("y") + lax.axis_index("c")
    shard_gu = D_FF // 4
    shard_d = D_FF // N_DEV

    small_in = jnp.concatenate(
        [meta_tokens, jnp.pad(conv_w[0], ((0, 0), (0, 128 - conv_w.shape[2]))), jnp.zeros((5, 128), F32)], axis=0)
    wgu1_8, wd1_8, small_8 = _all_gather(
        [ffn1_w_gu[0].T.astype(BF16), ffn1_w_down[0].astype(BF16), small_in], "gather_ffn1")
    meta_full = small_8[:, 0:N_META, :].transpose(1, 0, 2).reshape(N_META, D_MODEL)
    conv_full = small_8[:, N_META:N_META + 3, 0:CONV_DIM // N_DEV].transpose(1, 0, 2).reshape(3, CONV_DIM)
    wgu1 = wgu1_8.reshape(W_GU_SHAPE)
    wd1 = wd1_8.reshape(W_D_SHAPE)
    b_f_row = jnp.pad(b_f, ((0, 0), (0, 128 - N_HEADS)))
    gmat = _group_matrix()

    x2d = x.reshape(nb * seq, D_MODEL)
    later = [w_in[0].T.astype(BF16), w_out[0].astype(BF16), ffn2_w_gu[0].T.astype(BF16), ffn2_w_down[0].astype(BF16)]
    h1, n1, gate1, up1, win_8, wout_8, wgu2_8, wd2_8 = _ffn1_fwd(x2d, meta_full, lp, ffn1_norm, wgu1, wd1, later)
    wgu2 = wgu2_8.reshape(W_GU_SHAPE)
    wd2 = wd2_8.reshape(W_D_SHAPE)
    w_in_full = jnp.pad(win_8.reshape(IN_DIM, D_MODEL), ((0, IN_PAD - IN_DIM), (0, 0)))
    w_out_full = wout_8.reshape(D_MODEL, D_MODEL)

    bg, cg, hc, q, k, v, fg = _inproj_fwd(h1, mix_norm, w_in_full)
    zc = _conv_fwd(bg, cg, hc, conv_full, out_norm_conv, gmat, lp)
    ka, qa = _fgate_fwd(fg, b_f_row, lp)
    za, o, lse, h2 = _attn_fwd(q, qa, k, v, ka, out_norm_attn, zc, w_out_full, h1, lp)
    dh3, n3, gate2, up2, loss_part, d_final = _ffn2_fwd_loss(
        h2, ffn2_norm, wgu2, wd2, final_norm.reshape(1, D_MODEL), loss_target.reshape(nb * seq, D_MODEL), lp)

    dgate2, dup2, dwd2 = _ffn_bwd_act_wd(dh3, gate2, up2, wd2, "ffn2_bwd_act")
    dh2, d_ffn2 = _ffn_bwd_in(dh3, h2, ffn2_norm, dgate2, dup2, wgu2, "ffn2_bwd_in")
    dwgu2 = _ffn_bwd_wgu(n3, dgate2, dup2, "ffn2_bwd_wgu")
    dzc, dza, dwout = _outproj_bwd(dh2, zc, za, w_out_full)
    send_a = [dwgu2.reshape(N_DEV, shard_gu, D_MODEL), dwd2.reshape(N_DEV, shard_d, D_MODEL),
              dwout.reshape(N_DEV, D_MODEL // N_DEV, D_MODEL)]
    dq, dk, dv, dka, dfr, d_ga, p_wgu2, p_wd2, p_wout = _attn_bwd(
        dza, q, qa, k, v, ka, o, lse, out_norm_attn, lp, exchange=send_a)
    dfg, d_bf = _fgate_bwd(dka, dfr, fg, b_f_row, lp)
    dbg, dcg, dhc, d_conv, d_gc = _conv_bwd(dzc, bg, cg, hc, conv_full, out_norm_conv, gmat, lp)
    dh1, dwin, d_mix = _inproj_bwd([dbg, dcg, dhc, dq, dk, dv], dfg, dh2, h1, mix_norm, w_in_full)
    dwin_8 = dwin[0:IN_DIM].reshape(N_DEV, IN_DIM // N_DEV, D_MODEL)
    dgate1, dup1, dwd1, p_win = _ffn_bwd_act_wd(dh1, gate1, up1, wd1, "ffn1_bwd_act", exchange=[dwin_8])
    dwgu1, p_wd1 = _ffn_bwd_wgu(n1, dgate1, dup1, "ffn1_bwd_wgu", exchange=[dwd1.reshape(N_DEV, shard_d, D_MODEL)])
    own = dwgu1.reshape(N_DEV, shard_gu, D_MODEL)
    (got,) = _pair_exchange([own], "pair_exchange_ffn1")
    chip_sum = _pair_sum(own, got, "pair_sum_wgu1")
    dh0, d_ffn1, p_wgu1 = _ffn_bwd_in(dh1, None, ffn1_norm, dgate1, dup1, wgu1, "ffn1_bwd_in",
                                      tokens=(x2d, meta_full, lp), exchange=[chip_sum])

    dh0 = dh0.reshape(nb, lp, D_MODEL)
    grad_x = dh0[:, PAD + N_META:, :]
    d_meta = jnp.sum(dh0[:, PAD:PAD + N_META, :], axis=0)

    small = _pack_small([d_ffn1, d_mix, d_ffn2, d_final], d_gc, d_ga, d_bf, d_conv, d_meta)
    (small_all,) = _all_gather([small], "gather_small_grads")
    small_sum = _sum_parts(small_all, "sum_small_grads")
    g_ffn1n, g_mixn, g_ffn2n, g_finaln = (small_sum[8 * t:8 * t + 8].reshape(1, D_MODEL) for t in range(4))
    g_gc = small_sum[32:36].reshape(1, CONV_DIM)
    g_ga = small_sum[36:40].reshape(1, ATTN_DIM)
    g_bf = small_sum[40:41, 0:N_HEADS]
    g_conv_full = small_sum[41:53].reshape(3, CONV_DIM)
    g_meta_full = small_sum[53:181].reshape(N_META, D_MODEL)
    g_conv = lax.dynamic_slice_in_dim(g_conv_full, me * (CONV_DIM // N_DEV), CONV_DIM // N_DEV, axis=1)
    g_meta = lax.dynamic_slice_in_dim(g_meta_full, me * (D_MODEL // N_DEV), D_MODEL // N_DEV, axis=1)

    weights = {
        "meta_tokens": (g_meta[None], meta_tokens, m_meta_tokens, v_meta_tokens),
        "ffn1_norm": (g_ffn1n[None], ffn1_norm, m_ffn1_norm, v_ffn1_norm),
        "ffn1_w_gu": (p_wgu1, ffn1_w_gu[0].T, m_ffn1_w_gu[0].T, v_ffn1_w_gu[0].T),
        "ffn1_w_down": (p_wd1, ffn1_w_down[0], m_ffn1_w_down[0], v_ffn1_w_down[0]),
        "mix_norm": (g_mixn[None], mix_norm, m_mix_norm, v_mix_norm),
        "w_in": (p_win, w_in[0].T, m_w_in[0].T, v_w_in[0].T),
        "conv_w": (g_conv[None], conv_w[0], m_conv_w[0], v_conv_w[0]),
        "b_f": (g_bf[None], b_f, m_b_f, v_b_f),
        "out_norm_conv": (g_gc[None], out_norm_conv, m_out_norm_conv, v_out_norm_conv),
        "out_norm_attn": (g_ga[None], out_norm_attn, m_out_norm_attn, v_out_norm_attn),
        "w_out": (p_wout, w_out[0], m_w_out[0], v_w_out[0]),
        "ffn2_norm": (g_ffn2n[None], ffn2_norm, m_ffn2_norm, v_ffn2_norm),
        "ffn2_w_gu": (p_wgu2, ffn2_w_gu[0].T, m_ffn2_w_gu[0].T, v_ffn2_w_gu[0].T),
        "ffn2_w_down": (p_wd2, ffn2_w_down[0], m_ffn2_w_down[0], v_ffn2_w_down[0]),
        "final_norm": (g_finaln[None], final_norm.reshape(1, D_MODEL), m_final_norm.reshape(1, D_MODEL),
                       v_final_norm.reshape(1, D_MODEL)),
    }
    shapes = {"meta_tokens": meta_tokens.shape, "ffn1_norm": ffn1_norm.shape, "ffn1_w_gu": ffn1_w_gu.shape,
              "ffn1_w_down": ffn1_w_down.shape, "mix_norm": mix_norm.shape, "w_in": w_in.shape,
              "conv_w": conv_w.shape, "b_f": b_f.shape, "out_norm_conv": out_norm_conv.shape,
              "out_norm_attn": out_norm_attn.shape, "w_out": w_out.shape, "ffn2_norm": ffn2_norm.shape,
              "ffn2_w_gu": ffn2_w_gu.shape, "ffn2_w_down": ffn2_w_down.shape, "final_norm": final_norm.shape}
    grads, deltas, new_m, new_v = [], [], [], []
    for name, (p, w, m, vv) in weights.items():
        g, d, nm, nv = _adamw(p, w, m, vv, "adamw_" + name)
        if name in ("ffn1_w_gu", "ffn2_w_gu", "w_in"):
            g, d, nm, nv = g.T, d.T, nm.T, nv.T
        shape = shapes[name]
        grads.append(g.reshape(shape))
        deltas.append(d.reshape(shape))
        new_m.append(nm.reshape(shape))
        new_v.append(nv.reshape(shape))

    loss = lax.psum(loss_part[0, 0], ("x", "y", "c"))
    return (loss, grad_x, *grads, *deltas, *new_m, *new_v)
```

```python
import jax
import jax.numpy as jnp
from jax import lax
from jax.experimental import pallas as pl
from jax.experimental.pallas import tpu as pltpu

F32 = jnp.float32
BF16 = jnp.bfloat16

N_DEV = 8
D_MODEL = 1024
N_META = 16
PAD = 128 - N_META
CONV_DIM = 512
ATTN_DIM = 512
HEAD_DIM = 64
N_HEADS = 8
N_PAIRS = N_HEADS // 2
D_FF = 2816
IN_DIM = 3080
IN_PAD = 3200
IN_MAIN = 3072
N_PIECE = IN_MAIN // 512
EPS = 1e-6
NEG = -1e30
TQ = 128
TK = 512
VMEM_LIMIT = 56 * 1024 * 1024

HID_PIECES = ((0, 1024), (1024, 2048), (2048, D_FF))
W_GU_SHAPE = (2, D_FF, D_MODEL)
W_D_SHAPE = (D_FF, D_MODEL)

ADAM_LR = 0.001
ADAM_B1 = 0.9
ADAM_B2 = 0.999
ADAM_EPS = 1e-08
ADAM_WD = 0.01
ADAM_STEP = 10

MESH = pl.DeviceIdType.MESH
ANY = pl.BlockSpec(memory_space=pl.ANY)


def _params(sem=None):
    return pltpu.CompilerParams(dimension_semantics=sem, vmem_limit_bytes=VMEM_LIMIT)


def _row_tile(n, prefer):
    for t in (prefer, 512, 256, 128):
        if t <= n and n % t == 0:
            return t
    raise ValueError(f"no row tile for {n}")


def _resident(shape):
    zeros = (0,) * len(shape)
    return pl.BlockSpec(shape, lambda i: zeros, pipeline_mode=pl.Buffered(1))


def _dot(a, b):
    return jnp.dot(a, b, preferred_element_type=F32)


def _dot_nt(a, b):
    return lax.dot_general(a, b, (((1,), (1,)), ((), ())), preferred_element_type=F32)


def _dot_tn(a, b):
    return lax.dot_general(a, b, (((0,), (0,)), ((), ())), preferred_element_type=F32)


def _rms(x, g):
    r = lax.rsqrt(jnp.mean(x * x, axis=-1, keepdims=True) + EPS)
    xhat = x * r
    return xhat * g, xhat, r


def _rms_bwd(dn, xhat, r, g):
    dxhat = dn * g
    return r * (dxhat - xhat * jnp.mean(dxhat * xhat, axis=-1, keepdims=True))


def _sigmoid(x):
    return 1.0 / (1.0 + jnp.exp(-x))


def _place():
    return lax.axis_index("x"), lax.axis_index("y"), lax.axis_index("c")


def _comm_sems(nw):
    return [pltpu.SemaphoreType.DMA((nw, 7)), pltpu.SemaphoreType.DMA((nw, 7)), pltpu.SemaphoreType.DMA((nw,))]


def _flip(v, bit):
    return 1 - v if bit else v


class _Gather:
    def __init__(self, ins, outs, sems):
        self.ins, self.outs = ins, outs
        self.send, self.recv, self.local = sems
        x, y, c = _place()
        self.c = c
        self.me, self.sibling = (x, y, c), (x, y, 1 - c)
        self.chips = [(1 - x, y), (x, 1 - y), (1 - x, 1 - y)]

    def _copy(self, w, k, block, to, own=False):
        slot = self.outs[w].at[4 * block[0] + 2 * block[1] + block[2]]
        return pltpu.make_async_remote_copy(
            src_ref=self.ins[w] if own else slot, dst_ref=slot,
            send_sem=self.send.at[w, k], recv_sem=self.recv.at[w, k], device_id=to, device_id_type=MESH)

    def _mine(self, w):
        x, y, c = self.me
        return pltpu.make_async_copy(self.ins[w], self.outs[w].at[4 * x + 2 * y + c], self.local.at[w])

    def _first(self, w):
        return ([self._copy(w, 0, self.me, self.sibling, own=True)]
                + [self._copy(w, 1 + j, self.me, (*chip, self.c), own=True) for j, chip in enumerate(self.chips)])

    def _passed(self, w):
        return [self._copy(w, 4 + j, (*chip, self.c), self.sibling) for j, chip in enumerate(self.chips)]

    def start(self):
        for w in range(len(self.ins)):
            self._mine(w).start()
        for w in range(len(self.ins)):
            for cp in self._first(w):
                cp.start()

    def forward(self):
        for w in range(len(self.ins)):
            for j, chip in enumerate(self.chips):
                self._copy(w, 1 + j, (*chip, self.c), self.me).wait_recv()
                self._passed(w)[j].start()

    def finish(self):
        for w in range(len(self.ins)):
            self._copy(w, 0, self.sibling, self.me).wait_recv()
            for j, chip in enumerate(self.chips):
                self._copy(w, 4 + j, (*chip, 1 - self.c), self.me).wait_recv()
        for w in range(len(self.ins)):
            for cp in self._first(w) + self._passed(w):
                cp.wait_send()
            self._mine(w).wait()


class _Exchange:
    def __init__(self, ins, outs, sems):
        self.ins, self.outs = ins, outs
        self.send, self.recv, self.local = sems
        self.x, self.y, self.c = _place()
        self.me = 4 * self.x + 2 * self.y + self.c

    def _copy(self, w, k):
        peer = (_flip(self.x, ((k + 1) >> 2) & 1), _flip(self.y, ((k + 1) >> 1) & 1), _flip(self.c, (k + 1) & 1))
        return pltpu.make_async_remote_copy(
            src_ref=self.ins[w].at[4 * peer[0] + 2 * peer[1] + peer[2]], dst_ref=self.outs[w].at[self.me],
            send_sem=self.send.at[w, k], recv_sem=self.recv.at[w, k], device_id=peer, device_id_type=MESH)

    def _mine(self, w):
        return pltpu.make_async_copy(self.ins[w].at[self.me], self.outs[w].at[self.me], self.local.at[w])

    def start(self):
        for w in range(len(self.ins)):
            self._mine(w).start()
            for k in range(N_DEV - 1):
                self._copy(w, k).start()

    def finish(self):
        for w in range(len(self.ins)):
            for k in range(N_DEV - 1):
                self._copy(w, k).wait()
            self._mine(w).wait()


class _PairExchange:
    def __init__(self, ins, outs, sems):
        self.ins, self.outs = ins, outs
        self.send, self.recv, _ = sems
        x, y, self.c = _place()
        self.sibling = (x, y, 1 - self.c)

    def _copy(self, w, t):
        return pltpu.make_async_remote_copy(
            src_ref=self.ins[w].at[2 * t + 1 - self.c], dst_ref=self.outs[w].at[t],
            send_sem=self.send.at[w, t], recv_sem=self.recv.at[w, t], device_id=self.sibling, device_id_type=MESH)

    def start(self):
        for w in range(len(self.ins)):
            for t in range(4):
                self._copy(w, t).start()

    def finish(self):
        for w in range(len(self.ins)):
            for t in range(4):
                self._copy(w, t).wait()


class _ChipExchange:
    def __init__(self, ins, outs, sems):
        self.ins, self.outs = ins, outs
        self.send, self.recv, self.local = sems
        self.x, self.y, self.c = _place()
        self.chip = 2 * self.x + self.y

    def _copy(self, w, k):
        px, py = _flip(self.x, ((k + 1) >> 1) & 1), _flip(self.y, (k + 1) & 1)
        return pltpu.make_async_remote_copy(
            src_ref=self.ins[w].at[2 * px + py], dst_ref=self.outs[w].at[self.chip],
            send_sem=self.send.at[w, k], recv_sem=self.recv.at[w, k], device_id=(px, py, self.c),
            device_id_type=MESH)

    def _mine(self, w):
        return pltpu.make_async_copy(self.ins[w].at[self.chip], self.outs[w].at[self.chip], self.local.at[w])

    def start(self):
        for w in range(len(self.ins)):
            self._mine(w).start()
            for k in range(3):
                self._copy(w, k).start()

    def finish(self):
        for w in range(len(self.ins)):
            for k in range(3):
                self._copy(w, k).wait()
            self._mine(w).wait()


def _split_refs(refs, n_in, n_comm, n_out, n_scr):
    a = n_in
    b = a + n_comm
    c = b + n_out
    d = c + n_comm
    e = d + n_scr
    return refs[:a], refs[a:b], refs[b:c], refs[c:d], refs[d:e], refs[e:]


def _all_gather(xs, name):
    nw = len(xs)

    def body(*refs):
        comm = _Gather(refs[:nw], refs[nw:2 * nw], refs[2 * nw:])
        comm.start()
        comm.forward()
        comm.finish()

    return pl.pallas_call(
        body, name=name, in_specs=[ANY] * nw, out_specs=[ANY] * nw,
        out_shape=[jax.ShapeDtypeStruct((N_DEV,) + a.shape, a.dtype) for a in xs],
        scratch_shapes=_comm_sems(nw),
    )(*xs)


def _pair_exchange(xs, name):
    nw = len(xs)

    def body(*refs):
        comm = _PairExchange(refs[:nw], refs[nw:2 * nw], refs[2 * nw:])
        comm.start()
        comm.finish()

    return pl.pallas_call(
        body, name=name, in_specs=[ANY] * nw, out_specs=[ANY] * nw,
        out_shape=[jax.ShapeDtypeStruct((4,) + a.shape[1:], a.dtype) for a in xs],
        scratch_shapes=_comm_sems(nw),
    )(*xs)


def _pair_sum(own, got, name):
    _, r, c = own.shape
    tr = r
    for t in (256, 128, 64, 32, 16):
        if r % t == 0 and r > t:
            tr = t
            break

    def body(own_ref, got_ref, out_ref):
        mine = jnp.where(lax.axis_index("c") == 0, own_ref[:, 0].astype(F32), own_ref[:, 1].astype(F32))
        out_ref[...] = (mine + got_ref[...].astype(F32)).astype(BF16)

    return pl.pallas_call(
        body, name=name, grid=(r // tr,),
        in_specs=[pl.BlockSpec((4, 2, tr, c), lambda i: (0, 0, i, 0)), pl.BlockSpec((4, tr, c), lambda i: (0, i, 0))],
        out_specs=pl.BlockSpec((4, tr, c), lambda i: (0, i, 0)),
        out_shape=jax.ShapeDtypeStruct((4, r, c), BF16),
        compiler_params=_params(("parallel",)),
    )(own.reshape(4, 2, r, c), got)


def _token_spec(k, ksub, nq):
    def index_map(i):
        s = ksub * i + k
        return ((s // nq) * (nq - 1) + jnp.maximum(s % nq, 1) - 1, 0)
    return pl.BlockSpec((128, D_MODEL), index_map)


def _is_lead(i, k, ksub, nq):
    return ((ksub * i + k) % nq) == 0


def _assemble_rows(i, x_refs, meta_ref, nq):
    ksub = len(x_refs)
    lead = jnp.concatenate([jnp.zeros((PAD, D_MODEL), F32), meta_ref[...]], axis=0)
    return jnp.concatenate([jnp.where(_is_lead(i, k, ksub, nq), lead, x_refs[k][...]) for k in range(ksub)], axis=0)


def _swiglu(nb, wgu_ref, wd_ref, gate_ref, up_ref):
    acc = jnp.zeros((nb.shape[0], D_MODEL), F32)
    for a, b in HID_PIECES:
        gate = _dot_nt(nb, wgu_ref[0, a:b, :])
        up = _dot_nt(nb, wgu_ref[1, a:b, :])
        gate_ref[:, a:b] = gate.astype(BF16)
        up_ref[:, a:b] = up.astype(BF16)
        acc = acc + _dot((gate * _sigmoid(gate) * up).astype(BF16), wd_ref[a:b, :])
    return acc


def _ffn1_fwd(x2d, meta, lp, gain, wgu, wd, gather):
    nq = lp // 128
    n = (x2d.shape[0] // (nq - 1)) * nq
    tm = _row_tile(n, 512)
    ksub = tm // 128
    n_i = n // tm
    nw = len(gather)

    def body(*refs):
        x_refs = refs[:ksub]
        (meta_ref, g_ref, wgu_ref, wd_ref), gin, (out_ref, nrm_ref, gate_ref, up_ref), gout, _, sems = \
            _split_refs(refs[ksub:], 4, nw, 4, 0)
        i = pl.program_id(0)
        comm = _Gather(gin, gout, sems)
        pl.when(i == 0)(comm.start)
        pl.when(i == max(n_i - 3, 0))(comm.forward)

        hv = _assemble_rows(i, x_refs, meta_ref, nq)
        y, _, _ = _rms(hv, g_ref[...])
        nb = y.astype(BF16)
        nrm_ref[...] = nb
        out_ref[...] = hv + 0.5 * _swiglu(nb, wgu_ref, wd_ref, gate_ref, up_ref)

        pl.when(i == n_i - 1)(comm.finish)

    rows = pl.BlockSpec((tm, D_MODEL), lambda i: (i, 0))
    hid = pl.BlockSpec((tm, D_FF), lambda i: (i, 0))
    return pl.pallas_call(
        body, name="ffn1_fwd", grid=(n_i,),
        in_specs=[_token_spec(k, ksub, nq) for k in range(ksub)]
        + [pl.BlockSpec((N_META, D_MODEL), lambda i: (0, 0)), pl.BlockSpec((1, D_MODEL), lambda i: (0, 0)),
           _resident(W_GU_SHAPE), _resident(W_D_SHAPE)] + [ANY] * nw,
        out_specs=[rows, rows, hid, hid] + [ANY] * nw,
        out_shape=[jax.ShapeDtypeStruct((n, D_MODEL), F32), jax.ShapeDtypeStruct((n, D_MODEL), BF16),
                   jax.ShapeDtypeStruct((n, D_FF), BF16), jax.ShapeDtypeStruct((n, D_FF), BF16)]
        + [jax.ShapeDtypeStruct((N_DEV,) + a.shape, a.dtype) for a in gather],
        scratch_shapes=_comm_sems(nw),
        compiler_params=_params(("arbitrary",)),
    )(*([x2d] * ksub), meta, gain, wgu, wd, *gather)


def _ffn2_fwd_loss(h, gain, wgu, wd, gfinal, target, lp):
    n = h.shape[0]
    nq = lp // 128
    tm = _row_tile(n, 512)
    ksub = tm // 128
    n_i = n // tm

    def body(*refs):
        t_refs = refs[:ksub]
        h_ref, g_ref, wgu_ref, wd_ref, gf_ref, dh_ref, nrm_ref, gate_ref, up_ref, loss_ref, dgf_ref = refs[ksub:]
        i = pl.program_id(0)

        @pl.when(i == 0)
        def _():
            loss_ref[...] = jnp.zeros_like(loss_ref)
            dgf_ref[...] = jnp.zeros_like(dgf_ref)

        hv = h_ref[...]
        y, _, _ = _rms(hv, g_ref[...])
        nb = y.astype(BF16)
        nrm_ref[...] = nb
        hout = hv + 0.5 * _swiglu(nb, wgu_ref, wd_ref, gate_ref, up_ref)

        gf = gf_ref[...]
        loss = jnp.zeros((1, 1), F32)
        dgf = jnp.zeros((1, D_MODEL), F32)
        for k in range(ksub):
            yk, xhat, r = _rms(hout[128 * k:128 * (k + 1)], gf)
            err = jnp.where(_is_lead(i, k, ksub, nq), 0.0, yk - t_refs[k][...])
            loss = loss + 0.5 * jnp.sum(jnp.sum(err * err, axis=1, keepdims=True), axis=0,
                                        keepdims=True) * (1.0 / D_MODEL)
            dy = err * (1.0 / D_MODEL)
            dh_ref[128 * k:128 * (k + 1), :] = _rms_bwd(dy, xhat, r, gf)
            dgf = dgf + jnp.sum(dy * xhat, axis=0, keepdims=True)
        loss_ref[...] += loss
        dgf_ref[...] += dgf

    rows = pl.BlockSpec((tm, D_MODEL), lambda i: (i, 0))
    hid = pl.BlockSpec((tm, D_FF), lambda i: (i, 0))
    vec = pl.BlockSpec((1, D_MODEL), lambda i: (0, 0))
    return pl.pallas_call(
        body, name="ffn2_fwd_loss", grid=(n_i,),
        in_specs=[_token_spec(k, ksub, nq) for k in range(ksub)]
        + [rows, vec, _resident(W_GU_SHAPE), _resident(W_D_SHAPE), vec],
        out_specs=[rows, rows, hid, hid, pl.BlockSpec((1, 1), lambda i: (0, 0)), vec],
        out_shape=[jax.ShapeDtypeStruct((n, D_MODEL), F32), jax.ShapeDtypeStruct((n, D_MODEL), BF16),
                   jax.ShapeDtypeStruct((n, D_FF), BF16), jax.ShapeDtypeStruct((n, D_FF), BF16),
                   jax.ShapeDtypeStruct((1, 1), F32), jax.ShapeDtypeStruct((1, D_MODEL), F32)],
        compiler_params=_params(("arbitrary",)),
    )(*([target] * ksub), h, gain, wgu, wd, gfinal)


def _ffn_bwd_act_wd(dh_out, gate, up, wd, name, exchange=()):
    n = dh_out.shape[0]
    tm = _row_tile(n, 256)
    n_i = n // tm
    nw = len(exchange)

    def body(*refs):
        (dh_ref, gate_ref, up_ref, wd_ref), xin, (dgu_ref, dw_ref), xout, (acc_scr,), sems = \
            _split_refs(refs, 4, nw, 2, 1)
        i = pl.program_id(0)
        if nw:
            comm = _Exchange(xin, xout, sems)
            pl.when(i == 0)(comm.start)

        @pl.when(i == 0)
        def _():
            acc_scr[...] = jnp.zeros_like(acc_scr)

        dhb = (0.5 * dh_ref[...]).astype(BF16)
        for a, b in HID_PIECES:
            da = _dot_nt(dhb, wd_ref[a:b, :])
            g = gate_ref[:, a:b].astype(F32)
            u = up_ref[:, a:b].astype(F32)
            sig = _sigmoid(g)
            silu = g * sig
            dgu_ref[:, a:b] = (da * u * (sig * (1.0 + g * (1.0 - sig)))).astype(BF16)
            dgu_ref[:, D_FF + a:D_FF + b] = (da * silu).astype(BF16)
            acc_scr[a:b, :] += _dot_tn((silu * u).astype(BF16), dhb)

        @pl.when(i == n_i - 1)
        def _():
            dw_ref[...] = acc_scr[...].astype(BF16)

        if nw:
            pl.when(i == n_i - 1)(comm.finish)

    rows = pl.BlockSpec((tm, D_MODEL), lambda i: (i, 0))
    hid = pl.BlockSpec((tm, D_FF), lambda i: (i, 0))
    return pl.pallas_call(
        body, name=name, grid=(n_i,),
        in_specs=[rows, hid, hid, _resident(W_D_SHAPE)] + [ANY] * nw,
        out_specs=[pl.BlockSpec((tm, 2 * D_FF), lambda i: (i, 0)), _resident(W_D_SHAPE)] + [ANY] * nw,
        out_shape=[jax.ShapeDtypeStruct((n, 2 * D_FF), BF16), jax.ShapeDtypeStruct(W_D_SHAPE, BF16)]
        + [jax.ShapeDtypeStruct(a.shape, a.dtype) for a in exchange],
        scratch_shapes=[pltpu.VMEM(W_D_SHAPE, F32)] + (_comm_sems(nw) if nw else []),
        compiler_params=_params(("arbitrary",)),
    )(dh_out, gate, up, wd, *exchange)


def _ffn_bwd_in(dh_out, h_in, gain, dgu, wgu, name, tokens=None, exchange=()):
    n = dh_out.shape[0]
    tm = _row_tile(n, 512)
    n_i = n // tm
    nw = len(exchange)
    ksub, nq = (tm // 128, tokens[2] // 128) if tokens else (1, 0)

    def body(*refs):
        h_refs = refs[:ksub]
        (meta_ref, dh_ref, g_ref, dgu_ref, wgu_ref), xin, (dhin_ref, dgain_ref), xout, _, sems = \
            _split_refs(refs[ksub:], 5, nw, 2, 0)
        i = pl.program_id(0)
        if nw:
            comm = _ChipExchange(xin, xout, sems)
            pl.when(i == 0)(comm.start)

        @pl.when(i == 0)
        def _():
            dgain_ref[...] = jnp.zeros_like(dgain_ref)

        dn = _dot(dgu_ref[...], wgu_ref[...])
        gain_v = g_ref[...]
        hv = _assemble_rows(i, h_refs, meta_ref, nq) if tokens else h_refs[0][...]
        _, xhat, r = _rms(hv, gain_v)
        dhin_ref[...] = dh_ref[...] + _rms_bwd(dn, xhat, r, gain_v)
        dgain_ref[...] += jnp.sum(dn * xhat, axis=0, keepdims=True)

        if nw:
            pl.when(i == n_i - 1)(comm.finish)

    rows = pl.BlockSpec((tm, D_MODEL), lambda i: (i, 0))
    hid = pl.BlockSpec((tm, D_FF), lambda i: (i, 0))
    vec = pl.BlockSpec((1, D_MODEL), lambda i: (0, 0))
    meta_spec = pl.BlockSpec((N_META, D_MODEL), lambda i: (0, 0))
    if tokens:
        h_specs, h_args, meta = [_token_spec(k, ksub, nq) for k in range(ksub)], [tokens[0]] * ksub, tokens[1]
    else:
        h_specs, h_args, meta = [rows], [h_in], jnp.zeros((N_META, D_MODEL), F32)
    return pl.pallas_call(
        body, name=name, grid=(n_i,),
        in_specs=h_specs + [meta_spec, rows, vec, pl.BlockSpec((tm, 2 * D_FF), lambda i: (i, 0)),
                            _resident((2 * D_FF, D_MODEL))] + [ANY] * nw,
        out_specs=[rows, vec] + [ANY] * nw,
        out_shape=[jax.ShapeDtypeStruct((n, D_MODEL), F32), jax.ShapeDtypeStruct((1, D_MODEL), F32)]
        + [jax.ShapeDtypeStruct(a.shape, a.dtype) for a in exchange],
        scratch_shapes=_comm_sems(nw) if nw else [],
        compiler_params=_params(("arbitrary",)),
    )(*h_args, meta, dh_out, gain, dgu, wgu.reshape(2 * D_FF, D_MODEL), *exchange)


def _ffn_bwd_wgu(nrm, dgu, name, exchange=()):
    n = nrm.shape[0]
    tm = _row_tile(n, 512)
    n_i = n // tm
    nw = len(exchange)

    def body(*refs):
        (nrm_ref, dgu_ref), xin, (dw_ref,), xout, (acc_scr,), sems = _split_refs(refs, 2, nw, 1, 1)
        i = pl.program_id(0)
        if nw:
            comm = _Exchange(xin, xout, sems)
            pl.when(i == 0)(comm.start)

        @pl.when(i == 0)
        def _():
            acc_scr[...] = jnp.zeros_like(acc_scr)

        nb = nrm_ref[...]
        for half in (0, D_FF):
            for a, b in HID_PIECES:
                acc_scr[half + a:half + b, :] += _dot_tn(dgu_ref[:, half + a:half + b], nb)

        @pl.when(i == n_i - 1)
        def _():
            dw_ref[...] = acc_scr[...].astype(BF16)

        if nw:
            pl.when(i == n_i - 1)(comm.finish)

    shape = (2 * D_FF, D_MODEL)
    res = pl.pallas_call(
        body, name=name, grid=(n_i,),
        in_specs=[pl.BlockSpec((tm, D_MODEL), lambda i: (i, 0)),
                  pl.BlockSpec((tm, 2 * D_FF), lambda i: (i, 0))] + [ANY] * nw,
        out_specs=[_resident(shape)] + [ANY] * nw,
        out_shape=[jax.ShapeDtypeStruct(shape, BF16)] + [jax.ShapeDtypeStruct(a.shape, a.dtype) for a in exchange],
        scratch_shapes=[pltpu.VMEM(shape, F32)] + (_comm_sems(nw) if nw else []),
        compiler_params=_params(("arbitrary",)),
    )(nrm, dgu, *exchange)
    return res if nw else res[0]


def _inproj_fwd(h, gain, w_in):
    n = h.shape[0]
    tm = _row_tile(n, 512)

    def body(h_ref, g_ref, w_ref, *outs):
        y, _, _ = _rms(h_ref[...], g_ref[...])
        nb = y.astype(BF16)
        for p in range(N_PIECE):
            outs[p][...] = _dot_nt(nb, w_ref[512 * p:512 * (p + 1), :]).astype(BF16)
        outs[N_PIECE][...] = _dot_nt(nb, w_ref[IN_MAIN:IN_PAD, :])

    piece = pl.BlockSpec((tm, 512), lambda i: (i, 0))
    return pl.pallas_call(
        body, name="inproj_fwd", grid=(n // tm,),
        in_specs=[pl.BlockSpec((tm, D_MODEL), lambda i: (i, 0)),
                  pl.BlockSpec((1, D_MODEL), lambda i: (0, 0)),
                  pl.BlockSpec((IN_PAD, D_MODEL), lambda i: (0, 0))],
        out_specs=[piece] * N_PIECE + [pl.BlockSpec((tm, 128), lambda i: (i, 0))],
        out_shape=[jax.ShapeDtypeStruct((n, 512), BF16)] * N_PIECE + [jax.ShapeDtypeStruct((n, 128), F32)],
        compiler_params=_params(("parallel",)),
    )(h, gain, w_in)


def _inproj_bwd(dpieces, dfg, dh_out, h_in, gain, w_in):
    n = h_in.shape[0]
    tm = _row_tile(n, 512)
    n_i = n // tm

    def body(*refs):
        dp_refs = refs[:N_PIECE]
        dfg_ref, dh_ref, h_ref, g_ref, w_ref, dhin_ref, dw_ref, dgain_ref, acc_scr = refs[N_PIECE:]
        i = pl.program_id(0)

        @pl.when(i == 0)
        def _():
            acc_scr[...] = jnp.zeros_like(acc_scr)
            dgain_ref[...] = jnp.zeros_like(dgain_ref)

        gain_v = g_ref[...]
        y, xhat, r = _rms(h_ref[...], gain_v)
        nb = y.astype(BF16)
        dn = jnp.zeros((tm, D_MODEL), F32)
        for p in range(N_PIECE + 1):
            lo, hi = (512 * p, 512 * (p + 1)) if p < N_PIECE else (IN_MAIN, IN_PAD)
            dp = (dp_refs[p][...] if p < N_PIECE else dfg_ref[...]).astype(BF16)
            dn = dn + _dot(dp, w_ref[lo:hi, :])
            acc_scr[lo:hi, :] += _dot_tn(dp, nb)
        dhin_ref[...] = dh_ref[...] + _rms_bwd(dn, xhat, r, gain_v)
        dgain_ref[...] += jnp.sum(dn * xhat, axis=0, keepdims=True)

        @pl.when(i == n_i - 1)
        def _():
            dw_ref[...] = acc_scr[...].astype(BF16)

    piece = pl.BlockSpec((tm, 512), lambda i: (i, 0))
    rows = pl.BlockSpec((tm, D_MODEL), lambda i: (i, 0))
    vec = pl.BlockSpec((1, D_MODEL), lambda i: (0, 0))
    wspec = pl.BlockSpec((IN_PAD, D_MODEL), lambda i: (0, 0))
    return pl.pallas_call(
        body, name="inproj_bwd", grid=(n_i,),
        in_specs=[piece] * N_PIECE + [pl.BlockSpec((tm, 128), lambda i: (i, 0)), rows, rows, vec, wspec],
        out_specs=[rows, wspec, vec],
        out_shape=[jax.ShapeDtypeStruct((n, D_MODEL), F32),
                   jax.ShapeDtypeStruct((IN_PAD, D_MODEL), BF16),
                   jax.ShapeDtypeStruct((1, D_MODEL), F32)],
        scratch_shapes=[pltpu.VMEM((IN_PAD, D_MODEL), F32)],
        compiler_params=_params(("arbitrary",)),
    )(*dpieces, dfg, dh_out, h_in, gain, w_in)


def _outproj_bwd(dh, zc, za, w_out):
    n = dh.shape[0]
    tm = _row_tile(n, 512)
    n_i = n // tm

    def body(dh_ref, zc_ref, za_ref, w_ref, dzc_ref, dza_ref, dw_ref, acc_scr):
        i = pl.program_id(0)

        @pl.when(i == 0)
        def _():
            acc_scr[...] = jnp.zeros_like(acc_scr)

        dhb = dh_ref[...].astype(BF16)
        dzc_ref[...] = _dot_nt(dhb, w_ref[0:CONV_DIM, :]).astype(BF16)
        dza_ref[...] = _dot_nt(dhb, w_ref[CONV_DIM:, :]).astype(BF16)
        acc_scr[0:CONV_DIM, :] += _dot_tn(zc_ref[...], dhb)
        acc_scr[CONV_DIM:, :] += _dot_tn(za_ref[...], dhb)

        @pl.when(i == n_i - 1)
        def _():
            dw_ref[...] = acc_scr[...].astype(BF16)

    half = pl.BlockSpec((tm, 512), lambda i: (i, 0))
    wspec = pl.BlockSpec((D_MODEL, D_MODEL), lambda i: (0, 0))
    return pl.pallas_call(
        body, name="outproj_bwd", grid=(n_i,),
        in_specs=[pl.BlockSpec((tm, D_MODEL), lambda i: (i, 0)), half, half, wspec],
        out_specs=[half, half, wspec],
        out_shape=[jax.ShapeDtypeStruct((n, 512), BF16), jax.ShapeDtypeStruct((n, 512), BF16),
                   jax.ShapeDtypeStruct((D_MODEL, D_MODEL), BF16)],
        scratch_shapes=[pltpu.VMEM((D_MODEL, D_MODEL), F32)],
        compiler_params=_params(("arbitrary",)),
    )(dh, zc, za, w_out)


def _group_matrix():
    r = lax.broadcasted_iota(jnp.int32, (128, 128), 0) // HEAD_DIM
    c = lax.broadcasted_iota(jnp.int32, (128, 128), 1) // HEAD_DIM
    return jnp.where(r == c, 1.0 / HEAD_DIM, 0.0).astype(BF16)


def _group_mean(x, gmat):
    hi = x.astype(BF16)
    lo = (x - hi.astype(F32)).astype(BF16)
    return _dot(hi, gmat) + _dot(lo, gmat)


def _shift_rows(x, s):
    rows = x.shape[0]
    t = lax.broadcasted_iota(jnp.int32, x.shape, 0)
    rolled = pltpu.roll(x, s % rows, 0)
    keep = (t >= s) if s > 0 else (t < rows + s)
    return jnp.where(keep, rolled, 0.0)


def _conv_parts(bg_ref, cg_ref, hc_ref, w_ref):
    bg = bg_ref[...].astype(F32)
    cg = cg_ref[...].astype(F32)
    hc = hc_ref[...].astype(F32)
    u = cg * hc
    u1 = _shift_rows(u, 1)
    u2 = _shift_rows(u, 2)
    conv = w_ref[2:3, :] * u + w_ref[1:2, :] * u1 + w_ref[0:1, :] * u2
    return bg, cg, hc, u, u1, u2, conv


def _conv_fwd(bg, cg, hc, conv_w, gain, gmat, lp):
    n = bg.shape[0]
    nb = n // lp

    def body(bg_ref, cg_ref, hc_ref, w_ref, g_ref, gm_ref, z_ref):
        bgv, _, _, _, _, _, conv = _conv_parts(bg_ref, cg_ref, hc_ref, w_ref)
        yc = bgv * conv
        r = lax.rsqrt(_group_mean(yc * yc, gm_ref[...]) + EPS)
        z_ref[...] = (yc * r * g_ref[...]).astype(BF16)

    blk = pl.BlockSpec((lp, 128), lambda c, b: (b, c))
    return pl.pallas_call(
        body, name="conv_fwd", grid=(CONV_DIM // 128, nb),
        in_specs=[blk, blk, blk, pl.BlockSpec((3, 128), lambda c, b: (0, c)),
                  pl.BlockSpec((1, 128), lambda c, b: (0, c)), pl.BlockSpec((128, 128), lambda c, b: (0, 0))],
        out_specs=blk,
        out_shape=jax.ShapeDtypeStruct((n, CONV_DIM), BF16),
        compiler_params=_params(("parallel", "parallel")),
    )(bg, cg, hc, conv_w, gain, gmat)


def _conv_bwd(dz, bg, cg, hc, conv_w, gain, gmat, lp):
    n = bg.shape[0]
    nb = n // lp

    def body(dz_ref, bg_ref, cg_ref, hc_ref, w_ref, g_ref, gm_ref,
             dbg_ref, dcg_ref, dhc_ref, dw_ref, dgain_ref):
        b = pl.program_id(1)

        @pl.when(b == 0)
        def _():
            dw_ref[...] = jnp.zeros_like(dw_ref)
            dgain_ref[...] = jnp.zeros_like(dgain_ref)

        bgv, cgv, hcv, u, u1, u2, conv = _conv_parts(bg_ref, cg_ref, hc_ref, w_ref)
        gm = gm_ref[...]
        yc = bgv * conv
        r = lax.rsqrt(_group_mean(yc * yc, gm) + EPS)
        yhat = yc * r
        dzv = dz_ref[...].astype(F32)
        dyhat = dzv * g_ref[...]
        dgain_ref[...] += jnp.sum(dzv * yhat, axis=0, keepdims=True)
        dyc = r * (dyhat - yhat * _group_mean(dyhat * yhat, gm))
        dbg_ref[...] = (dyc * conv).astype(BF16)
        dconv = dyc * bgv
        du = (w_ref[2:3, :] * dconv + w_ref[1:2, :] * _shift_rows(dconv, -1)
              + w_ref[0:1, :] * _shift_rows(dconv, -2))
        dcg_ref[...] = (du * hcv).astype(BF16)
        dhc_ref[...] = (du * cgv).astype(BF16)
        dw_ref[0:1, :] += jnp.sum(dconv * u2, axis=0, keepdims=True)
        dw_ref[1:2, :] += jnp.sum(dconv * u1, axis=0, keepdims=True)
        dw_ref[2:3, :] += jnp.sum(dconv * u, axis=0, keepdims=True)

    blk = pl.BlockSpec((lp, 128), lambda c, b: (b, c))
    wspec = pl.BlockSpec((3, 128), lambda c, b: (0, c))
    gspec = pl.BlockSpec((1, 128), lambda c, b: (0, c))
    return pl.pallas_call(
        body, name="conv_bwd", grid=(CONV_DIM // 128, nb),
        in_specs=[blk, blk, blk, blk, wspec, gspec, pl.BlockSpec((128, 128), lambda c, b: (0, 0))],
        out_specs=[blk, blk, blk, wspec, gspec],
        out_shape=[jax.ShapeDtypeStruct((n, CONV_DIM), BF16)] * 3
        + [jax.ShapeDtypeStruct((3, CONV_DIM), F32), jax.ShapeDtypeStruct((1, CONV_DIM), F32)],
        compiler_params=_params(("parallel", "arbitrary")),
    )(dz, bg, cg, hc, conv_w, gain, gmat)


KEY_MASKED = 1e30
ONE_LANE = 24


def _scan_steps(rows):
    s, out = 1, []
    while s < rows:
        out.append(s)
        s *= 2
    return out


def _fgate_fwd(fg, b_f, lp):
    n = fg.shape[0]
    nb = n // lp

    def body(fg_ref, b_ref, ka_ref, qa_ref):
        x = fg_ref[...] + b_ref[...]
        logf = jnp.minimum(x, 0.0) - jnp.log(1.0 + jnp.exp(-jnp.abs(x)))
        t = lax.broadcasted_iota(jnp.int32, (lp, 128), 0)
        lane = lax.broadcasted_iota(jnp.int32, (lp, 128), 1)
        f = jnp.where((t >= PAD) & (lane < N_HEADS), logf, 0.0)
        for s in _scan_steps(lp):
            f = f + _shift_rows(f, s)
        hi = f.astype(BF16).astype(F32)
        rest = f - hi
        mid = rest.astype(BF16).astype(F32)
        lo = (rest - mid).astype(BF16).astype(F32)
        ones = jnp.where((lane >= ONE_LANE) & (lane < ONE_LANE + 3), 1.0, 0.0)
        hi_key = jnp.where((t < PAD) & (lane < N_HEADS), KEY_MASKED, hi)
        ka_ref[...] = (hi_key + pltpu.roll(mid, 8, 1) + pltpu.roll(lo, 16, 1) + ones).astype(BF16)
        for h in range(N_HEADS):
            minus = jnp.where((lane == h) | (lane == 8 + h) | (lane == 16 + h), -1.0, 0.0)
            terms = (jnp.where(lane == ONE_LANE, pltpu.roll(hi, ONE_LANE - h, 1), 0.0)
                     + jnp.where(lane == ONE_LANE + 1, pltpu.roll(mid, ONE_LANE + 1 - h, 1), 0.0)
                     + jnp.where(lane == ONE_LANE + 2, pltpu.roll(lo, ONE_LANE + 2 - h, 1), 0.0))
            qa_ref[:, 128 * h:128 * (h + 1)] = (minus + terms).astype(BF16)

    return pl.pallas_call(
        body, name="fgate_fwd", grid=(nb,),
        in_specs=[pl.BlockSpec((lp, 128), lambda b: (b, 0)), pl.BlockSpec((1, 128), lambda b: (0, 0))],
        out_specs=[pl.BlockSpec((lp, 128), lambda b: (b, 0)), pl.BlockSpec((lp, N_HEADS * 128), lambda b: (b, 0))],
        out_shape=[jax.ShapeDtypeStruct((n, 128), BF16), jax.ShapeDtypeStruct((n, N_HEADS * 128), BF16)],
        compiler_params=_params(("parallel",)),
    )(fg, b_f)


def _fgate_bwd(dka, dfr, fg, b_f, lp):
    n = fg.shape[0]
    nb = n // lp

    def body(dka_ref, dfr_ref, fg_ref, b_ref, dfg_ref, db_ref):
        b = pl.program_id(0)

        @pl.when(b == 0)
        def _():
            db_ref[...] = jnp.zeros_like(db_ref)

        wide = jnp.concatenate([dfr_ref[0], jnp.zeros((128 - N_HEADS, lp), F32)], axis=0)
        t = lax.broadcasted_iota(jnp.int32, (lp, 128), 0)
        lane = lax.broadcasted_iota(jnp.int32, (lp, 128), 1)
        d = jnp.where(lane < N_HEADS, dka_ref[...], 0.0) + wide.T
        for s in _scan_steps(lp):
            d = d + _shift_rows(d, -s)
        x = fg_ref[...] + b_ref[...]
        dx = jnp.where((t >= PAD) & (lane < N_HEADS), d * _sigmoid(-x), 0.0)
        dfg_ref[...] = dx
        db_ref[...] += jnp.sum(dx, axis=0, keepdims=True)

    return pl.pallas_call(
        body, name="fgate_bwd", grid=(nb,),
        in_specs=[pl.BlockSpec((lp, 128), lambda b: (b, 0)), pl.BlockSpec((1, N_HEADS, lp), lambda b: (b, 0, 0)),
                  pl.BlockSpec((lp, 128), lambda b: (b, 0)), pl.BlockSpec((1, 128), lambda b: (0, 0))],
        out_specs=[pl.BlockSpec((lp, 128), lambda b: (b, 0)), pl.BlockSpec((1, 128), lambda b: (0, 0))],
        out_shape=[jax.ShapeDtypeStruct((n, 128), F32), jax.ShapeDtypeStruct((1, 128), F32)],
        compiler_params=_params(("arbitrary",)),
    )(dka, dfr, fg, b_f)


def _head_masks():
    lane = lax.broadcasted_iota(jnp.int32, (1, 128), 1)
    return lane < HEAD_DIM


def _stack_heads(x2, first):
    zero = jnp.zeros_like(x2)
    return jnp.concatenate([jnp.where(first, x2, zero), jnp.where(first, zero, x2)], axis=0)


def _stack_heads_lanes(xt):
    r = lax.broadcasted_iota(jnp.int32, xt.shape, 0)
    zero = jnp.zeros_like(xt)
    return jnp.concatenate([jnp.where(r < HEAD_DIM, xt, zero), jnp.where(r < HEAD_DIM, zero, xt)], axis=1)


def _pair_cols(col0, col1, first):
    return jnp.where(first, col0, col1)


def _pair_rows(row0, row1):
    r = lax.broadcasted_iota(jnp.int32, (128, TQ), 0)
    return jnp.where(r < HEAD_DIM, row0, row1)


def _query_side(q_ref, qa_ref, p, first):
    q2 = q_ref[:, 128 * p:128 * (p + 1)] * 0.125
    zero = jnp.zeros_like(q2)
    top = jnp.concatenate([jnp.where(first, q2, zero), qa_ref[:, 128 * (2 * p):128 * (2 * p + 1)]], axis=1)
    bot = jnp.concatenate([jnp.where(first, zero, q2), qa_ref[:, 128 * (2 * p + 1):128 * (2 * p + 2)]], axis=1)
    return jnp.concatenate([top, bot], axis=0)


def _key_chunks(lp):
    return (lp + TK - 1) // TK


def _chunk_mask(i, c, tk):
    r = lax.broadcasted_iota(jnp.int32, (tk, 2 * TQ), 0)
    col = lax.broadcasted_iota(jnp.int32, (tk, 2 * TQ), 1)
    return (c * TK + r) <= (i * TQ + (col & (TQ - 1)))


def _causal_sweep(i, step, init):
    per = TK // TQ
    last = i // per
    carry = lax.fori_loop(0, last, lambda c, carry: step(c, carry, False, TK), init)
    tails = [lambda carry, r=r: step(last, carry, True, TQ * (r + 1)) for r in range(per)]
    return lax.switch(i % per, tails, carry)


def _transpose_bf16(x):
    return x.astype(F32).T.astype(BF16)


def _attn_fwd(q, qa, k, v, ka, gain, zc, w_out, h, lp):
    n = q.shape[0]
    nb = n // lp
    nq = lp // TQ
    lpp = _key_chunks(lp) * TK

    def body(q_ref, qa_ref, k_ref, v_ref, ka_ref, g_ref, zc_ref, w_ref, h_ref,
             z_ref, o_ref, lse_ref, hout_ref, kx_scr, vt_scr):
        i = pl.program_id(1)
        first = _head_masks()

        @pl.when(i == 0)
        def _():
            if lpp > lp:
                kx_scr[lp:lpp, :] = jnp.zeros((lpp - lp, 2 * ATTN_DIM), BF16)
                vt_scr[:, lp:lpp] = jnp.zeros((ATTN_DIM, lpp - lp), BF16)
            for p in range(N_PAIRS):
                kx_scr[0:lp, 256 * p:256 * p + 128] = k_ref[:, 128 * p:128 * (p + 1)]
                kx_scr[0:lp, 256 * p + 128:256 * (p + 1)] = ka_ref[...]
            vt_scr[:, 0:lp] = _transpose_bf16(v_ref[...])

        rhs_t = [_transpose_bf16(_query_side(q_ref, qa_ref, p, first)) for p in range(N_PAIRS)]

        def step(c, carry, masked, tk):
            koff = pl.multiple_of(c * TK, TK)
            valid = _chunk_mask(i, c, tk) if masked else None
            new = []
            for p in range(N_PAIRS):
                m, l, acc = carry[p]
                st = _dot(kx_scr[pl.ds(koff, tk), 256 * p:256 * (p + 1)], rhs_t[p])
                if masked:
                    st = jnp.where(valid, st, NEG)
                m_new = jnp.maximum(m, jnp.max(st, axis=0, keepdims=True))
                pt = jnp.exp(st - m_new)
                alpha = jnp.exp(m - m_new)
                l = alpha * l + jnp.sum(pt, axis=0, keepdims=True)
                pb = pt.astype(BF16)
                vt = _stack_heads_lanes(vt_scr[128 * p:128 * (p + 1), pl.ds(koff, tk)])
                pv = _dot(vt, jnp.concatenate([pb[:, 0:TQ], pb[:, TQ:]], axis=0))
                acc = acc * _pair_rows(alpha[:, 0:TQ], alpha[:, TQ:]) + pv
                new.append((m_new, l, acc))
            return tuple(new)

        init = tuple((jnp.full((1, 2 * TQ), NEG, F32), jnp.zeros((1, 2 * TQ), F32), jnp.zeros((128, TQ), F32))
                     for _ in range(N_PAIRS))
        final = _causal_sweep(i, step, init)

        row = lax.broadcasted_iota(jnp.int32, (TQ, 128), 0)
        real = (i * TQ + row) >= PAD
        hout = h_ref[...] + _dot(zc_ref[...], w_ref[0:CONV_DIM, :])
        for p in range(N_PAIRS):
            m, l, acc = final[p]
            inv = 1.0 / l
            ot = acc * _pair_rows(inv[:, 0:TQ], inv[:, TQ:])
            sq = ot * ot
            r0 = lax.rsqrt(jnp.sum(sq[0:HEAD_DIM], axis=0, keepdims=True) * (1.0 / HEAD_DIM) + EPS)
            r1 = lax.rsqrt(jnp.sum(sq[HEAD_DIM:], axis=0, keepdims=True) * (1.0 / HEAD_DIM) + EPS)
            cols = slice(128 * p, 128 * (p + 1))
            o_ref[:, cols] = jnp.where(real, ot.T, 0.0).astype(BF16)
            z = (jnp.where(real, (ot * _pair_rows(r0, r1)).T, 0.0) * g_ref[:, cols]).astype(BF16)
            z_ref[:, cols] = z
            hout = hout + _dot(z, w_ref[CONV_DIM + 128 * p:CONV_DIM + 128 * (p + 1), :])
            lse = m + jnp.log(l)
            lse_ref[0, 2 * p:2 * p + 1, :] = lse[:, 0:TQ]
            lse_ref[0, 2 * p + 1:2 * p + 2, :] = lse[:, TQ:]
        hout_ref[...] = hout

    qblk = pl.BlockSpec((TQ, ATTN_DIM), lambda b, i: (b * nq + i, 0))
    qablk = pl.BlockSpec((TQ, N_HEADS * 128), lambda b, i: (b * nq + i, 0))
    seq = pl.BlockSpec((lp, ATTN_DIM), lambda b, i: (b, 0))
    rowblk = pl.BlockSpec((1, N_HEADS, TQ), lambda b, i: (b, 0, i))
    hblk = pl.BlockSpec((TQ, D_MODEL), lambda b, i: (b * nq + i, 0))
    return pl.pallas_call(
        body, name="attn_fwd", grid=(nb, nq),
        in_specs=[qblk, qablk, seq, seq, pl.BlockSpec((lp, 128), lambda b, i: (b, 0)),
                  pl.BlockSpec((1, ATTN_DIM), lambda b, i: (0, 0)), qblk,
                  pl.BlockSpec((D_MODEL, D_MODEL), lambda b, i: (0, 0)), hblk],
        out_specs=[qblk, qblk, rowblk, hblk],
        out_shape=[jax.ShapeDtypeStruct((n, ATTN_DIM), BF16), jax.ShapeDtypeStruct((n, ATTN_DIM), BF16),
                   jax.ShapeDtypeStruct((nb, N_HEADS, lp), F32), jax.ShapeDtypeStruct((n, D_MODEL), F32)],
        scratch_shapes=[pltpu.VMEM((lpp, 2 * ATTN_DIM), BF16), pltpu.VMEM((ATTN_DIM, lpp), BF16)],
        compiler_params=_params(("parallel", "arbitrary")),
    )(q, qa, k, v, ka, gain, zc, w_out, h)


def _attn_bwd(dz, q, qa, k, v, ka, o, lse, gain, lp, exchange=()):
    n = q.shape[0]
    nb = n // lp
    nq = lp // TQ
    lpp = _key_chunks(lp) * TK
    nw = len(exchange)

    def body(*refs):
        ((dz_ref, q_ref, qa_ref, k_ref, v_ref, ka_ref, o_ref, lse_ref, g_ref), xin,
         (dq_ref, dk_ref, dv_ref, dka_ref, dfr_ref, dgain_ref), xout,
         (kx_scr, vx_scr, kt_scr, dkx_scr, dvx_scr), sems) = _split_refs(refs, 9, nw, 6, 5)
        b = pl.program_id(0)
        i = pl.program_id(1)
        first = _head_masks()
        if nw:
            comm = _Exchange(xin, xout, sems)
            pl.when((b == 0) & (i == 0))(comm.start)

        @pl.when((b == 0) & (i == 0))
        def _():
            dgain_ref[...] = jnp.zeros_like(dgain_ref)

        @pl.when(i == 0)
        def _():
            if lpp > lp:
                kx_scr[lp:lpp, :] = jnp.zeros((lpp - lp, 2 * ATTN_DIM), BF16)
                vx_scr[lp:lpp, :] = jnp.zeros((lpp - lp, ATTN_DIM), BF16)
                kt_scr[:, lp:lpp] = jnp.zeros((ATTN_DIM, lpp - lp), BF16)
            for p in range(N_PAIRS):
                kx_scr[0:lp, 256 * p:256 * p + 128] = k_ref[:, 128 * p:128 * (p + 1)]
                kx_scr[0:lp, 256 * p + 128:256 * (p + 1)] = ka_ref[...]
            vx_scr[0:lp, :] = v_ref[...]
            kt_scr[:, 0:lp] = _transpose_bf16(k_ref[...])
            dkx_scr[...] = jnp.zeros_like(dkx_scr)
            dvx_scr[...] = jnp.zeros_like(dvx_scr)

        rhs, rhs_t, lses, dos, dos_t, deltas = [], [], [], [], [], []
        for p in range(N_PAIRS):
            cols = slice(128 * p, 128 * (p + 1))
            side = _query_side(q_ref, qa_ref, p, first)
            rhs.append(side)
            rhs_t.append(_transpose_bf16(side))
            lses.append(jnp.concatenate([lse_ref[0, 2 * p:2 * p + 1, :], lse_ref[0, 2 * p + 1:2 * p + 2, :]], axis=1))
            ov = o_ref[:, cols].astype(F32)
            dzv = dz_ref[:, cols].astype(F32)
            gv = g_ref[:, cols]
            sq = ov * ov
            ms0 = jnp.sum(jnp.where(first, sq, 0.0), axis=1, keepdims=True) * (1.0 / HEAD_DIM)
            ms1 = jnp.sum(jnp.where(first, 0.0, sq), axis=1, keepdims=True) * (1.0 / HEAD_DIM)
            r = _pair_cols(lax.rsqrt(ms0 + EPS), lax.rsqrt(ms1 + EPS), first)
            ohat = ov * r
            dyhat = dzv * gv
            dgain_ref[:, cols] += jnp.sum(dzv * ohat, axis=0, keepdims=True)
            pr = dyhat * ohat
            mean0 = jnp.sum(jnp.where(first, pr, 0.0), axis=1, keepdims=True) * (1.0 / HEAD_DIM)
            mean1 = jnp.sum(jnp.where(first, 0.0, pr), axis=1, keepdims=True) * (1.0 / HEAD_DIM)
            do = r * (dyhat - ohat * _pair_cols(mean0, mean1, first))
            ddt = (do * ov).T
            deltas.append(jnp.concatenate([jnp.sum(ddt[0:HEAD_DIM], axis=0, keepdims=True),
                                           jnp.sum(ddt[HEAD_DIM:], axis=0, keepdims=True)], axis=1))
            do_st = _stack_heads(do.astype(BF16), first)
            dos.append(do_st)
            dos_t.append(_transpose_bf16(do_st))

        def step(c, carry, masked, tk):
            koff = pl.multiple_of(c * TK, TK)
            valid = _chunk_mask(i, c, tk) if masked else None
            new = []
            for p in range(N_PAIRS):
                dqt, dfq = carry[p]
                ext = slice(256 * p, 256 * (p + 1))
                cols = slice(128 * p, 128 * (p + 1))
                st = _dot(kx_scr[pl.ds(koff, tk), ext], rhs_t[p])
                if masked:
                    st = jnp.where(valid, st, NEG)
                pt = jnp.exp(st - lses[p])
                dpt = _dot(vx_scr[pl.ds(koff, tk), cols], dos_t[p])
                dst = pt * (dpt - deltas[p])
                dsb = dst.astype(BF16)
                dfq = dfq + jnp.sum(dsb.astype(F32), axis=0, keepdims=True)
                dkx_scr[pl.ds(koff, tk), ext] += _dot(dsb, rhs[p])
                dvx_scr[pl.ds(koff, tk), cols] += _dot(pt.astype(BF16), dos[p])
                kt = _stack_heads_lanes(kt_scr[cols, pl.ds(koff, tk)])
                dqt = dqt + _dot(kt, jnp.concatenate([dsb[:, 0:TQ], dsb[:, TQ:]], axis=0))
                new.append((dqt, dfq))
            return tuple(new)

        init = tuple((jnp.zeros((128, TQ), F32), jnp.zeros((1, 2 * TQ), F32)) for _ in range(N_PAIRS))
        final = _causal_sweep(i, step, init)

        for p in range(N_PAIRS):
            dqt, dfq = final[p]
            dq_ref[:, 128 * p:128 * (p + 1)] = (dqt.T * 0.125).astype(BF16)
            dfr_ref[0, 2 * p:2 * p + 1, :] = dfq[:, 0:TQ]
            dfr_ref[0, 2 * p + 1:2 * p + 2, :] = dfq[:, TQ:]

        @pl.when(i == nq - 1)
        def _():
            dka = jnp.zeros((lp, 128), F32)
            for p in range(N_PAIRS):
                dk_ref[:, 128 * p:128 * (p + 1)] = dkx_scr[0:lp, 256 * p:256 * p + 128].astype(BF16)
                dka = dka + dkx_scr[0:lp, 256 * p + 128:256 * (p + 1)]
            dka_ref[...] = dka
            dv_ref[...] = dvx_scr[0:lp, :].astype(BF16)

        if nw:
            pl.when((b == nb - 1) & (i == nq - 1))(comm.finish)

    qblk = pl.BlockSpec((TQ, ATTN_DIM), lambda b, i: (b * nq + i, 0))
    qablk = pl.BlockSpec((TQ, N_HEADS * 128), lambda b, i: (b * nq + i, 0))
    seq = pl.BlockSpec((lp, ATTN_DIM), lambda b, i: (b, 0))
    kaseq = pl.BlockSpec((lp, 128), lambda b, i: (b, 0))
    rowblk = pl.BlockSpec((1, N_HEADS, TQ), lambda b, i: (b, 0, i))
    gspec = pl.BlockSpec((1, ATTN_DIM), lambda b, i: (0, 0))
    return pl.pallas_call(
        body, name="attn_bwd", grid=(nb, nq),
        in_specs=[qblk, qblk, qablk, seq, seq, kaseq, qblk, rowblk, gspec] + [ANY] * nw,
        out_specs=[qblk, seq, seq, kaseq, rowblk, gspec] + [ANY] * nw,
        out_shape=[jax.ShapeDtypeStruct((n, ATTN_DIM), BF16), jax.ShapeDtypeStruct((n, ATTN_DIM), BF16),
                   jax.ShapeDtypeStruct((n, ATTN_DIM), BF16), jax.ShapeDtypeStruct((n, 128), F32),
                   jax.ShapeDtypeStruct((nb, N_HEADS, lp), F32), jax.ShapeDtypeStruct((1, ATTN_DIM), F32)]
        + [jax.ShapeDtypeStruct(a.shape, a.dtype) for a in exchange],
        scratch_shapes=[pltpu.VMEM((lpp, 2 * ATTN_DIM), BF16), pltpu.VMEM((lpp, ATTN_DIM), BF16),
                        pltpu.VMEM((ATTN_DIM, lpp), BF16), pltpu.VMEM((lpp, 2 * ATTN_DIM), F32),
                        pltpu.VMEM((lpp, ATTN_DIM), F32)] + (_comm_sems(nw) if nw else []),
        compiler_params=_params(("arbitrary", "arbitrary")),
    )(dz, q, qa, k, v, ka, o, lse, gain, *exchange)


def _adamw(parts, w, m, v, name):
    s_parts, r, c = parts.shape
    tr = r
    for t in (256, 128, 64, 32, 16):
        if r % t == 0 and r > t:
            tr = t
            break

    def body(p_ref, w_ref, m_ref, v_ref, g_ref, d_ref, nm_ref, nv_ref):
        g = p_ref[0].astype(F32)
        for s in range(1, s_parts):
            g = g + p_ref[s].astype(F32)
        nm = ADAM_B1 * m_ref[...] + (1.0 - ADAM_B1) * g
        nv = ADAM_B2 * v_ref[...] + (1.0 - ADAM_B2) * (g * g)
        m_hat = nm / (1.0 - ADAM_B1 ** ADAM_STEP)
        v_hat = nv / (1.0 - ADAM_B2 ** ADAM_STEP)
        g_ref[...] = g
        d_ref[...] = -ADAM_LR * (m_hat / (jnp.sqrt(v_hat) + ADAM_EPS) + ADAM_WD * w_ref[...])
        nm_ref[...] = nm
        nv_ref[...] = nv

    blk = pl.BlockSpec((tr, c), lambda i: (i, 0))
    return pl.pallas_call(
        body, name=name, grid=(r // tr,),
        in_specs=[pl.BlockSpec((s_parts, tr, c), lambda i: (0, i, 0)), blk, blk, blk],
        out_specs=[blk] * 4,
        out_shape=[jax.ShapeDtypeStruct((r, c), F32)] * 4,
        compiler_params=_params(("parallel",)),
    )(parts, w, m, v)


def _sum_parts(parts, name):
    s_parts, r, c = parts.shape

    def body(p_ref, out_ref):
        acc = p_ref[0]
        for s in range(1, s_parts):
            acc = acc + p_ref[s]
        out_ref[...] = acc

    return pl.pallas_call(
        body, name=name, out_shape=jax.ShapeDtypeStruct((r, c), F32),
        in_specs=[pl.BlockSpec(memory_space=pltpu.VMEM)], out_specs=pl.BlockSpec(memory_space=pltpu.VMEM),
    )(parts)


SMALL_ROWS = 184


def _pack_small(d_gains, d_gc, d_ga, d_bf, d_conv, d_meta):
    rows = [g.reshape(8, 128) for g in d_gains]
    rows += [d_gc.reshape(4, 128), d_ga.reshape(4, 128), d_bf.reshape(1, 128)]
    rows += [d_conv.reshape(12, 128), d_meta.reshape(128, 128)]
    packed = jnp.concatenate(rows, axis=0)
    return jnp.pad(packed, ((0, SMALL_ROWS - packed.shape[0]), (0, 0)))


def kernel(x, meta_tokens, ffn1_norm, ffn1_w_gu, ffn1_w_down, mix_norm, w_in, conv_w, b_f, out_norm_conv, out_norm_attn, w_out, ffn2_norm, ffn2_w_gu, ffn2_w_down, final_norm, loss_target, m_meta_tokens, m_ffn1_norm, m_ffn1_w_gu, m_ffn1_w_down, m_mix_norm, m_w_in, m_conv_w, m_b_f, m_out_norm_conv, m_out_norm_attn, m_w_out, m_ffn2_norm, m_ffn2_w_gu, m_ffn2_w_down, m_final_norm, v_meta_tokens, v_ffn1_norm, v_ffn1_w_gu, v_ffn1_w_down, v_mix_norm, v_w_in, v_conv_w, v_b_f, v_out_norm_conv, v_out_norm_attn, v_w_out, v_ffn2_norm, v_ffn2_w_gu, v_ffn2_w_down, v_final_norm):
    nb, seq, _ = x.shape
    lp = PAD + N_META + seq
    me = 4 * lax.axis_index("x") + 2 * lax.axis_index("y") + lax.axis_index("c")
    shard_gu = D_FF // 4
    shard_d = D_FF // N_DEV

    small_in = jnp.concatenate(
        [meta_tokens, jnp.pad(conv_w[0], ((0, 0), (0, 128 - conv_w.shape[2]))), jnp.zeros((5, 128), F32)], axis=0)
    wgu1_8, wd1_8, small_8 = _all_gather(
        [ffn1_w_gu[0].T.astype(BF16), ffn1_w_down[0].astype(BF16), small_in], "gather_ffn1")
    meta_full = small_8[:, 0:N_META, :].transpose(1, 0, 2).reshape(N_META, D_MODEL)
    conv_full = small_8[:, N_META:N_META + 3, 0:CONV_DIM // N_DEV].transpose(1, 0, 2).reshape(3, CONV_DIM)
    wgu1 = wgu1_8.reshape(W_GU_SHAPE)
    wd1 = wd1_8.reshape(W_D_SHAPE)
    b_f_row = jnp.pad(b_f, ((0, 0), (0, 128 - N_HEADS)))
    gmat = _group_matrix()

    x2d = x.reshape(nb * seq, D_MODEL)
    later = [w_in[0].T.astype(BF16), w_out[0].astype(BF16), ffn2_w_gu[0].T.astype(BF16), ffn2_w_down[0].astype(BF16)]
    h1, n1, gate1, up1, win_8, wout_8, wgu2_8, wd2_8 = _ffn1_fwd(x2d, meta_full, lp, ffn1_norm, wgu1, wd1, later)
    wgu2 = wgu2_8.reshape(W_GU_SHAPE)
    wd2 = wd2_8.reshape(W_D_SHAPE)
    w_in_full = jnp.pad(win_8.reshape(IN_DIM, D_MODEL), ((0, IN_PAD - IN_DIM), (0, 0)))
    w_out_full = wout_8.reshape(D_MODEL, D_MODEL)

    bg, cg, hc, q, k, v, fg = _inproj_fwd(h1, mix_norm, w_in_full)
    zc = _conv_fwd(bg, cg, hc, conv_full, out_norm_conv, gmat, lp)
    ka, qa = _fgate_fwd(fg, b_f_row, lp)
    za, o, lse, h2 = _attn_fwd(q, qa, k, v, ka, out_norm_attn, zc, w_out_full, h1, lp)
    dh3, n3, gate2, up2, loss_part, d_final = _ffn2_fwd_loss(
        h2, ffn2_norm, wgu2, wd2, final_norm.reshape(1, D_MODEL), loss_target.reshape(nb * seq, D_MODEL), lp)

    dgu2, dwd2 = _ffn_bwd_act_wd(dh3, gate2, up2, wd2, "ffn2_bwd_act")
    dh2, d_ffn2 = _ffn_bwd_in(dh3, h2, ffn2_norm, dgu2, wgu2, "ffn2_bwd_in")
    dwgu2 = _ffn_bwd_wgu(n3, dgu2, "ffn2_bwd_wgu")
    dzc, dza, dwout = _outproj_bwd(dh2, zc, za, w_out_full)
    send_a = [dwgu2.reshape(N_DEV, shard_gu, D_MODEL), dwd2.reshape(N_DEV, shard_d, D_MODEL),
              dwout.reshape(N_DEV, D_MODEL // N_DEV, D_MODEL)]
    dq, dk, dv, dka, dfr, d_ga, p_wgu2, p_wd2, p_wout = _attn_bwd(
        dza, q, qa, k, v, ka, o, lse, out_norm_attn, lp, exchange=send_a)
    dfg, d_bf = _fgate_bwd(dka, dfr, fg, b_f_row, lp)
    dbg, dcg, dhc, d_conv, d_gc = _conv_bwd(dzc, bg, cg, hc, conv_full, out_norm_conv, gmat, lp)
    dh1, dwin, d_mix = _inproj_bwd([dbg, dcg, dhc, dq, dk, dv], dfg, dh2, h1, mix_norm, w_in_full)
    dwin_8 = dwin[0:IN_DIM].reshape(N_DEV, IN_DIM // N_DEV, D_MODEL)
    dgu1, dwd1, p_win = _ffn_bwd_act_wd(dh1, gate1, up1, wd1, "ffn1_bwd_act", exchange=[dwin_8])
    dwgu1, p_wd1 = _ffn_bwd_wgu(n1, dgu1, "ffn1_bwd_wgu", exchange=[dwd1.reshape(N_DEV, shard_d, D_MODEL)])
    own = dwgu1.reshape(N_DEV, shard_gu, D_MODEL)
    (got,) = _pair_exchange([own], "pair_exchange_ffn1")
    chip_sum = _pair_sum(own, got, "pair_sum_wgu1")
    dh0, d_ffn1, p_wgu1 = _ffn_bwd_in(dh1, None, ffn1_norm, dgu1, wgu1, "ffn1_bwd_in",
                                      tokens=(x2d, meta_full, lp), exchange=[chip_sum])

    dh0 = dh0.reshape(nb, lp, D_MODEL)
    grad_x = dh0[:, PAD + N_META:, :]
    d_meta = jnp.sum(dh0[:, PAD:PAD + N_META, :], axis=0)

    small = _pack_small([d_ffn1, d_mix, d_ffn2, d_final], d_gc, d_ga, d_bf, d_conv, d_meta)
    (small_all,) = _all_gather([small], "gather_small_grads")
    small_sum = _sum_parts(small_all, "sum_small_grads")
    g_ffn1n, g_mixn, g_ffn2n, g_finaln = (small_sum[8 * t:8 * t + 8].reshape(1, D_MODEL) for t in range(4))
    g_gc = small_sum[32:36].reshape(1, CONV_DIM)
    g_ga = small_sum[36:40].reshape(1, ATTN_DIM)
    g_bf = small_sum[40:41, 0:N_HEADS]
    g_conv_full = small_sum[41:53].reshape(3, CONV_DIM)
    g_meta_full = small_sum[53:181].reshape(N_META, D_MODEL)
    g_conv = lax.dynamic_slice_in_dim(g_conv_full, me * (CONV_DIM // N_DEV), CONV_DIM // N_DEV, axis=1)
    g_meta = lax.dynamic_slice_in_dim(g_meta_full, me * (D_MODEL // N_DEV), D_MODEL // N_DEV, axis=1)

    weights = {
        "meta_tokens": (g_meta[None], meta_tokens, m_meta_tokens, v_meta_tokens),
        "ffn1_norm": (g_ffn1n[None], ffn1_norm, m_ffn1_norm, v_ffn1_norm),
        "ffn1_w_gu": (p_wgu1, ffn1_w_gu[0].T, m_ffn1_w_gu[0].T, v_ffn1_w_gu[0].T),
        "ffn1_w_down": (p_wd1, ffn1_w_down[0], m_ffn1_w_down[0], v_ffn1_w_down[0]),
        "mix_norm": (g_mixn[None], mix_norm, m_mix_norm, v_mix_norm),
        "w_in": (p_win, w_in[0].T, m_w_in[0].T, v_w_in[0].T),
        "conv_w": (g_conv[None], conv_w[0], m_conv_w[0], v_conv_w[0]),
        "b_f": (g_bf[None], b_f, m_b_f, v_b_f),
        "out_norm_conv": (g_gc[None], out_norm_conv, m_out_norm_conv, v_out_norm_conv),
        "out_norm_attn": (g_ga[None], out_norm_attn, m_out_norm_attn, v_out_norm_attn),
        "w_out": (p_wout, w_out[0], m_w_out[0], v_w_out[0]),
        "ffn2_norm": (g_ffn2n[None], ffn2_norm, m_ffn2_norm, v_ffn2_norm),
        "ffn2_w_gu": (p_wgu2, ffn2_w_gu[0].T, m_ffn2_w_gu[0].T, v_ffn2_w_gu[0].T),
        "ffn2_w_down": (p_wd2, ffn2_w_down[0], m_ffn2_w_down[0], v_ffn2_w_down[0]),
        "final_norm": (g_finaln[None], final_norm.reshape(1, D_MODEL), m_final_norm.reshape(1, D_MODEL),
                       v_final_norm.reshape(1, D_MODEL)),
    }
    shapes = {"meta_tokens": meta_tokens.shape, "ffn1_norm": ffn1_norm.shape, "ffn1_w_gu": ffn1_w_gu.shape,
              "ffn1_w_down": ffn1_w_down.shape, "mix_norm": mix_norm.shape, "w_in": w_in.shape,
              "conv_w": conv_w.shape, "b_f": b_f.shape, "out_norm_conv": out_norm_conv.shape,
              "out_norm_attn": out_norm_attn.shape, "w_out": w_out.shape, "ffn2_norm": ffn2_norm.shape,
              "ffn2_w_gu": ffn2_w_gu.shape, "ffn2_w_down": ffn2_w_down.shape, "final_norm": final_norm.shape}
    grads, deltas, new_m, new_v = [], [], [], []
    for name, (p, w, m, vv) in weights.items():
        g, d, nm, nv = _adamw(p, w, m, vv, "adamw_" + name)
        if name in ("ffn1_w_gu", "ffn2_w_gu", "w_in"):
            g, d, nm, nv = g.T, d.T, nm.T, nv.T
        shape = shapes[name]
        grads.append(g.reshape(shape))
        deltas.append(d.reshape(shape))
        new_m.append(nm.reshape(shape))
        new_v.append(nv.reshape(shape))

    loss = lax.psum(loss_part[0, 0], ("x", "y", "c"))
    return (loss, grad_x, *grads, *deltas, *new_m, *new_v)
```

```python
import jax
import jax.numpy as jnp
from jax import lax
from jax.experimental import pallas as pl
from jax.experimental.pallas import tpu as pltpu

F32 = jnp.float32
BF16 = jnp.bfloat16

N_DEV = 8
D_MODEL = 1024
N_META = 16
PAD = 128 - N_META
CONV_DIM = 512
ATTN_DIM = 512
HEAD_DIM = 64
N_HEADS = 8
N_PAIRS = N_HEADS // 2
D_FF = 2816
IN_DIM = 3080
IN_PAD = 3200
IN_MAIN = 3072
N_PIECE = IN_MAIN // 512
EPS = 1e-6
NEG = -1e30
TQ = 128
TK = 512
VMEM_LIMIT = 56 * 1024 * 1024

HID_PIECES = ((0, 1024), (1024, 2048), (2048, D_FF))
W_GU_SHAPE = (2, D_FF, D_MODEL)
W_D_SHAPE = (D_FF, D_MODEL)

ADAM_LR = 0.001
ADAM_B1 = 0.9
ADAM_B2 = 0.999
ADAM_EPS = 1e-08
ADAM_WD = 0.01
ADAM_STEP = 10

MESH = pl.DeviceIdType.MESH
ANY = pl.BlockSpec(memory_space=pl.ANY)


def _params(sem=None):
    return pltpu.CompilerParams(dimension_semantics=sem, vmem_limit_bytes=VMEM_LIMIT)


def _row_tile(n, prefer):
    for t in (prefer, 512, 256, 128):
        if t <= n and n % t == 0:
            return t
    raise ValueError(f"no row tile for {n}")


def _resident(shape):
    zeros = (0,) * len(shape)
    return pl.BlockSpec(shape, lambda i: zeros, pipeline_mode=pl.Buffered(1))


def _dot(a, b):
    return jnp.dot(a, b, preferred_element_type=F32)


def _dot_nt(a, b):
    return lax.dot_general(a, b, (((1,), (1,)), ((), ())), preferred_element_type=F32)


def _dot_tn(a, b):
    return lax.dot_general(a, b, (((0,), (0,)), ((), ())), preferred_element_type=F32)


def _rms(x, g):
    r = lax.rsqrt(jnp.mean(x * x, axis=-1, keepdims=True) + EPS)
    xhat = x * r
    return xhat * g, xhat, r


def _rms_bwd(dn, xhat, r, g):
    dxhat = dn * g
    return r * (dxhat - xhat * jnp.mean(dxhat * xhat, axis=-1, keepdims=True))


def _sigmoid(x):
    return 1.0 / (1.0 + jnp.exp(-x))


def _place():
    return lax.axis_index("x"), lax.axis_index("y"), lax.axis_index("c")


def _comm_sems(nw):
    return [pltpu.SemaphoreType.DMA((nw, 7)), pltpu.SemaphoreType.DMA((nw, 7)), pltpu.SemaphoreType.DMA((nw,))]


def _flip(v, bit):
    return 1 - v if bit else v


class _Gather:
    def __init__(self, ins, outs, sems):
        self.ins, self.outs = ins, outs
        self.send, self.recv, self.local = sems
        x, y, c = _place()
        self.c = c
        self.me, self.sibling = (x, y, c), (x, y, 1 - c)
        first = ((x + 1 - c) % 2, (y + c) % 2)
        second = ((x + c) % 2, (y + 1 - c) % 2)
        self.chips = [first, second, (1 - x, 1 - y)]
        self.targets = [first, second, second]

    def _copy(self, w, k, block, to, own=False):
        slot = self.outs[w].at[4 * block[0] + 2 * block[1] + block[2]]
        return pltpu.make_async_remote_copy(
            src_ref=self.ins[w] if own else slot, dst_ref=slot,
            send_sem=self.send.at[w, k], recv_sem=self.recv.at[w, k], device_id=to, device_id_type=MESH)

    def _mine(self, w):
        x, y, c = self.me
        return pltpu.make_async_copy(self.ins[w], self.outs[w].at[4 * x + 2 * y + c], self.local.at[w])

    def _first(self, w):
        return ([self._copy(w, 0, self.me, self.sibling, own=True)]
                + [self._copy(w, 1 + j, self.me, (*self.targets[j], self.c), own=True) for j in range(2)])

    def _relay(self, w):
        return self._copy(w, 3, (*self.chips[0], self.c), (*self.targets[2], self.c))

    def _landed(self, w, j):
        return self._copy(w, 1 + j, (*self.chips[j], self.c), self.me)

    def _passed(self, w):
        return [self._copy(w, 4 + j, (*chip, self.c), self.sibling) for j, chip in enumerate(self.chips)]

    def start(self):
        for w in range(len(self.ins)):
            self._mine(w).start()
        for w in range(len(self.ins)):
            for cp in self._first(w):
                cp.start()

    def relay(self):
        for w in range(len(self.ins)):
            self._landed(w, 0).wait_recv()
            self._relay(w).start()
            self._passed(w)[0].start()

    def forward(self):
        for w in range(len(self.ins)):
            for j in (1, 2):
                self._landed(w, j).wait_recv()
                self._passed(w)[j].start()

    def finish(self):
        from_sibling = [self.chips[1], self.chips[0], self.chips[2]]
        for w in range(len(self.ins)):
            self._copy(w, 0, self.sibling, self.me).wait_recv()
            for j, chip in enumerate(from_sibling):
                self._copy(w, 4 + j, (*chip, 1 - self.c), self.me).wait_recv()
        for w in range(len(self.ins)):
            for cp in self._first(w) + [self._relay(w)] + self._passed(w):
                cp.wait_send()
            self._mine(w).wait()


class _Exchange:
    def __init__(self, ins, outs, sems):
        self.ins, self.outs = ins, outs
        self.send, self.recv, self.local = sems
        self.x, self.y, self.c = _place()
        self.me = 4 * self.x + 2 * self.y + self.c

    def _copy(self, w, k):
        peer = (_flip(self.x, ((k + 1) >> 2) & 1), _flip(self.y, ((k + 1) >> 1) & 1), _flip(self.c, (k + 1) & 1))
        return pltpu.make_async_remote_copy(
            src_ref=self.ins[w].at[4 * peer[0] + 2 * peer[1] + peer[2]], dst_ref=self.outs[w].at[self.me],
            send_sem=self.send.at[w, k], recv_sem=self.recv.at[w, k], device_id=peer, device_id_type=MESH)

    def _mine(self, w):
        return pltpu.make_async_copy(self.ins[w].at[self.me], self.outs[w].at[self.me], self.local.at[w])

    def start(self):
        for w in range(len(self.ins)):
            self._mine(w).start()
            for k in range(N_DEV - 1):
                self._copy(w, k).start()

    def finish(self):
        for w in range(len(self.ins)):
            for k in range(N_DEV - 1):
                self._copy(w, k).wait()
            self._mine(w).wait()


class _PairExchange:
    def __init__(self, ins, outs, sems):
        self.ins, self.outs = ins, outs
        self.send, self.recv, _ = sems
        x, y, self.c = _place()
        self.sibling = (x, y, 1 - self.c)

    def _copy(self, w, t):
        return pltpu.make_async_remote_copy(
            src_ref=self.ins[w].at[2 * t + 1 - self.c], dst_ref=self.outs[w].at[t],
            send_sem=self.send.at[w, t], recv_sem=self.recv.at[w, t], device_id=self.sibling, device_id_type=MESH)

    def start(self):
        for w in range(len(self.ins)):
            for t in range(4):
                self._copy(w, t).start()

    def finish(self):
        for w in range(len(self.ins)):
            for t in range(4):
                self._copy(w, t).wait()


class _ChipExchange:
    def __init__(self, ins, outs, sems):
        self.ins, self.outs = ins, outs
        self.send, self.recv, self.local = sems
        self.x, self.y, self.c = _place()
        self.chip = 2 * self.x + self.y

    def _copy(self, w, k):
        px, py = _flip(self.x, ((k + 1) >> 1) & 1), _flip(self.y, (k + 1) & 1)
        return pltpu.make_async_remote_copy(
            src_ref=self.ins[w].at[2 * px + py], dst_ref=self.outs[w].at[self.chip],
            send_sem=self.send.at[w, k], recv_sem=self.recv.at[w, k], device_id=(px, py, self.c),
            device_id_type=MESH)

    def _mine(self, w):
        return pltpu.make_async_copy(self.ins[w].at[self.chip], self.outs[w].at[self.chip], self.local.at[w])

    def start(self):
        for w in range(len(self.ins)):
            self._mine(w).start()
            for k in range(3):
                self._copy(w, k).start()

    def finish(self):
        for w in range(len(self.ins)):
            for k in range(3):
                self._copy(w, k).wait()
            self._mine(w).wait()


def _split_refs(refs, n_in, n_comm, n_out, n_scr):
    a = n_in
    b = a + n_comm
    c = b + n_out
    d = c + n_comm
    e = d + n_scr
    return refs[:a], refs[a:b], refs[b:c], refs[c:d], refs[d:e], refs[e:]


def _all_gather(xs, name):
    nw = len(xs)

    def body(*refs):
        comm = _Gather(refs[:nw], refs[nw:2 * nw], refs[2 * nw:])
        comm.start()
        comm.relay()
        comm.forward()
        comm.finish()

    return pl.pallas_call(
        body, name=name, in_specs=[ANY] * nw, out_specs=[ANY] * nw,
        out_shape=[jax.ShapeDtypeStruct((N_DEV,) + a.shape, a.dtype) for a in xs],
        scratch_shapes=_comm_sems(nw),
    )(*xs)


def _pair_exchange(xs, name):
    nw = len(xs)

    def body(*refs):
        comm = _PairExchange(refs[:nw], refs[nw:2 * nw], refs[2 * nw:])
        comm.start()
        comm.finish()

    return pl.pallas_call(
        body, name=name, in_specs=[ANY] * nw, out_specs=[ANY] * nw,
        out_shape=[jax.ShapeDtypeStruct((4,) + a.shape[1:], a.dtype) for a in xs],
        scratch_shapes=_comm_sems(nw),
    )(*xs)


def _pair_sum(own, got, name):
    _, r, c = own.shape
    tr = r
    for t in (256, 128, 64, 32, 16):
        if r % t == 0 and r > t:
            tr = t
            break

    def body(own_ref, got_ref, out_ref):
        mine = jnp.where(lax.axis_index("c") == 0, own_ref[:, 0].astype(F32), own_ref[:, 1].astype(F32))
        out_ref[...] = (mine + got_ref[...].astype(F32)).astype(BF16)

    return pl.pallas_call(
        body, name=name, grid=(r // tr,),
        in_specs=[pl.BlockSpec((4, 2, tr, c), lambda i: (0, 0, i, 0)), pl.BlockSpec((4, tr, c), lambda i: (0, i, 0))],
        out_specs=pl.BlockSpec((4, tr, c), lambda i: (0, i, 0)),
        out_shape=jax.ShapeDtypeStruct((4, r, c), BF16),
        compiler_params=_params(("parallel",)),
    )(own.reshape(4, 2, r, c), got)


def _token_spec(k, ksub, nq):
    def index_map(i):
        s = ksub * i + k
        return ((s // nq) * (nq - 1) + jnp.maximum(s % nq, 1) - 1, 0)
    return pl.BlockSpec((128, D_MODEL), index_map)


def _is_lead(i, k, ksub, nq):
    return ((ksub * i + k) % nq) == 0


def _assemble_rows(i, x_refs, meta_ref, nq):
    ksub = len(x_refs)
    lead = jnp.concatenate([jnp.zeros((PAD, D_MODEL), F32), meta_ref[...]], axis=0)
    return jnp.concatenate([jnp.where(_is_lead(i, k, ksub, nq), lead, x_refs[k][...]) for k in range(ksub)], axis=0)


def _swiglu(nb, wgu_ref, wd_ref, gate_ref, up_ref):
    acc = jnp.zeros((nb.shape[0], D_MODEL), F32)
    for a, b in HID_PIECES:
        gate = _dot_nt(nb, wgu_ref[0, a:b, :])
        up = _dot_nt(nb, wgu_ref[1, a:b, :])
        gate_ref[:, a:b] = gate.astype(BF16)
        up_ref[:, a:b] = up.astype(BF16)
        acc = acc + _dot((gate * _sigmoid(gate) * up).astype(BF16), wd_ref[a:b, :])
    return acc


def _ffn1_fwd(x2d, meta, lp, gain, wgu, wd, gather):
    nq = lp // 128
    n = (x2d.shape[0] // (nq - 1)) * nq
    tm = _row_tile(n, 512)
    ksub = tm // 128
    n_i = n // tm
    nw = len(gather)

    def body(*refs):
        x_refs = refs[:ksub]
        (meta_ref, g_ref, wgu_ref, wd_ref), gin, (out_ref, nrm_ref, gate_ref, up_ref), gout, _, sems = \
            _split_refs(refs[ksub:], 4, nw, 4, 0)
        i = pl.program_id(0)
        comm = _Gather(gin, gout, sems)
        pl.when(i == 0)(comm.start)
        pl.when(i == n_i // 2)(comm.relay)
        pl.when(i == max(n_i - 3, n_i // 2))(comm.forward)

        hv = _assemble_rows(i, x_refs, meta_ref, nq)
        y, _, _ = _rms(hv, g_ref[...])
        nb = y.astype(BF16)
        nrm_ref[...] = nb
        out_ref[...] = hv + 0.5 * _swiglu(nb, wgu_ref, wd_ref, gate_ref, up_ref)

        pl.when(i == n_i - 1)(comm.finish)

    rows = pl.BlockSpec((tm, D_MODEL), lambda i: (i, 0))
    hid = pl.BlockSpec((tm, D_FF), lambda i: (i, 0))
    return pl.pallas_call(
        body, name="ffn1_fwd", grid=(n_i,),
        in_specs=[_token_spec(k, ksub, nq) for k in range(ksub)]
        + [pl.BlockSpec((N_META, D_MODEL), lambda i: (0, 0)), pl.BlockSpec((1, D_MODEL), lambda i: (0, 0)),
           _resident(W_GU_SHAPE), _resident(W_D_SHAPE)] + [ANY] * nw,
        out_specs=[rows, rows, hid, hid] + [ANY] * nw,
        out_shape=[jax.ShapeDtypeStruct((n, D_MODEL), F32), jax.ShapeDtypeStruct((n, D_MODEL), BF16),
                   jax.ShapeDtypeStruct((n, D_FF), BF16), jax.ShapeDtypeStruct((n, D_FF), BF16)]
        + [jax.ShapeDtypeStruct((N_DEV,) + a.shape, a.dtype) for a in gather],
        scratch_shapes=_comm_sems(nw),
        compiler_params=_params(("arbitrary",)),
    )(*([x2d] * ksub), meta, gain, wgu, wd, *gather)


def _ffn2_fwd_loss(h, gain, wgu, wd, gfinal, target, lp):
    n = h.shape[0]
    nq = lp // 128
    tm = _row_tile(n, 512)
    ksub = tm // 128
    n_i = n // tm

    def body(*refs):
        t_refs = refs[:ksub]
        h_ref, g_ref, wgu_ref, wd_ref, gf_ref, dh_ref, nrm_ref, gate_ref, up_ref, loss_ref, dgf_ref = refs[ksub:]
        i = pl.program_id(0)

        @pl.when(i == 0)
        def _():
            loss_ref[...] = jnp.zeros_like(loss_ref)
            dgf_ref[...] = jnp.zeros_like(dgf_ref)

        hv = h_ref[...]
        y, _, _ = _rms(hv, g_ref[...])
        nb = y.astype(BF16)
        nrm_ref[...] = nb
        hout = hv + 0.5 * _swiglu(nb, wgu_ref, wd_ref, gate_ref, up_ref)

        gf = gf_ref[...]
        loss = jnp.zeros((1, 1), F32)
        dgf = jnp.zeros((1, D_MODEL), F32)
        for k in range(ksub):
            yk, xhat, r = _rms(hout[128 * k:128 * (k + 1)], gf)
            err = jnp.where(_is_lead(i, k, ksub, nq), 0.0, yk - t_refs[k][...])
            loss = loss + 0.5 * jnp.sum(jnp.sum(err * err, axis=1, keepdims=True), axis=0,
                                        keepdims=True) * (1.0 / D_MODEL)
            dy = err * (1.0 / D_MODEL)
            dh_ref[128 * k:128 * (k + 1), :] = _rms_bwd(dy, xhat, r, gf)
            dgf = dgf + jnp.sum(dy * xhat, axis=0, keepdims=True)
        loss_ref[...] += loss
        dgf_ref[...] += dgf

    rows = pl.BlockSpec((tm, D_MODEL), lambda i: (i, 0))
    hid = pl.BlockSpec((tm, D_FF), lambda i: (i, 0))
    vec = pl.BlockSpec((1, D_MODEL), lambda i: (0, 0))
    return pl.pallas_call(
        body, name="ffn2_fwd_loss", grid=(n_i,),
        in_specs=[_token_spec(k, ksub, nq) for k in range(ksub)]
        + [rows, vec, _resident(W_GU_SHAPE), _resident(W_D_SHAPE), vec],
        out_specs=[rows, rows, hid, hid, pl.BlockSpec((1, 1), lambda i: (0, 0)), vec],
        out_shape=[jax.ShapeDtypeStruct((n, D_MODEL), F32), jax.ShapeDtypeStruct((n, D_MODEL), BF16),
                   jax.ShapeDtypeStruct((n, D_FF), BF16), jax.ShapeDtypeStruct((n, D_FF), BF16),
                   jax.ShapeDtypeStruct((1, 1), F32), jax.ShapeDtypeStruct((1, D_MODEL), F32)],
        compiler_params=_params(("arbitrary",)),
    )(*([target] * ksub), h, gain, wgu, wd, gfinal)


def _ffn_bwd_act_wd(dh_out, gate, up, wd, name, exchange=()):
    n = dh_out.shape[0]
    tm = _row_tile(n, 256)
    n_i = n // tm
    nw = len(exchange)

    def body(*refs):
        (dh_ref, gate_ref, up_ref, wd_ref), xin, (dgu_ref, dw_ref), xout, (acc_scr,), sems = \
            _split_refs(refs, 4, nw, 2, 1)
        i = pl.program_id(0)
        if nw:
            comm = _Exchange(xin, xout, sems)
            pl.when(i == 0)(comm.start)

        @pl.when(i == 0)
        def _():
            acc_scr[...] = jnp.zeros_like(acc_scr)

        dhb = (0.5 * dh_ref[...]).astype(BF16)
        for a, b in HID_PIECES:
            da = _dot_nt(dhb, wd_ref[a:b, :])
            g = gate_ref[:, a:b].astype(F32)
            u = up_ref[:, a:b].astype(F32)
            sig = _sigmoid(g)
            silu = g * sig
            dgu_ref[:, a:b] = (da * u * (sig * (1.0 + g * (1.0 - sig)))).astype(BF16)
            dgu_ref[:, D_FF + a:D_FF + b] = (da * silu).astype(BF16)
            acc_scr[a:b, :] += _dot_tn((silu * u).astype(BF16), dhb)

        @pl.when(i == n_i - 1)
        def _():
            dw_ref[...] = acc_scr[...].astype(BF16)

        if nw:
            pl.when(i == n_i - 1)(comm.finish)

    rows = pl.BlockSpec((tm, D_MODEL), lambda i: (i, 0))
    hid = pl.BlockSpec((tm, D_FF), lambda i: (i, 0))
    return pl.pallas_call(
        body, name=name, grid=(n_i,),
        in_specs=[rows, hid, hid, _resident(W_D_SHAPE)] + [ANY] * nw,
        out_specs=[pl.BlockSpec((tm, 2 * D_FF), lambda i: (i, 0)), _resident(W_D_SHAPE)] + [ANY] * nw,
        out_shape=[jax.ShapeDtypeStruct((n, 2 * D_FF), BF16), jax.ShapeDtypeStruct(W_D_SHAPE, BF16)]
        + [jax.ShapeDtypeStruct(a.shape, a.dtype) for a in exchange],
        scratch_shapes=[pltpu.VMEM(W_D_SHAPE, F32)] + (_comm_sems(nw) if nw else []),
        compiler_params=_params(("arbitrary",)),
    )(dh_out, gate, up, wd, *exchange)


def _ffn_bwd_in(dh_out, h_in, gain, dgu, wgu, name, tokens=None, exchange=()):
    n = dh_out.shape[0]
    tm = _row_tile(n, 512)
    n_i = n // tm
    nw = len(exchange)
    ksub, nq = (tm // 128, tokens[2] // 128) if tokens else (1, 0)

    def body(*refs):
        h_refs = refs[:ksub]
        (meta_ref, dh_ref, g_ref, dgu_ref, wgu_ref), xin, (dhin_ref, dgain_ref), xout, _, sems = \
            _split_refs(refs[ksub:], 5, nw, 2, 0)
        i = pl.program_id(0)
        if nw:
            comm = _ChipExchange(xin, xout, sems)
            pl.when(i == 0)(comm.start)

        @pl.when(i == 0)
        def _():
            dgain_ref[...] = jnp.zeros_like(dgain_ref)

        dn = _dot(dgu_ref[...], wgu_ref[...])
        gain_v = g_ref[...]
        hv = _assemble_rows(i, h_refs, meta_ref, nq) if tokens else h_refs[0][...]
        _, xhat, r = _rms(hv, gain_v)
        dhin_ref[...] = dh_ref[...] + _rms_bwd(dn, xhat, r, gain_v)
        dgain_ref[...] += jnp.sum(dn * xhat, axis=0, keepdims=True)

        if nw:
            pl.when(i == n_i - 1)(comm.finish)

    rows = pl.BlockSpec((tm, D_MODEL), lambda i: (i, 0))
    hid = pl.BlockSpec((tm, D_FF), lambda i: (i, 0))
    vec = pl.BlockSpec((1, D_MODEL), lambda i: (0, 0))
    meta_spec = pl.BlockSpec((N_META, D_MODEL), lambda i: (0, 0))
    if tokens:
        h_specs, h_args, meta = [_token_spec(k, ksub, nq) for k in range(ksub)], [tokens[0]] * ksub, tokens[1]
    else:
        h_specs, h_args, meta = [rows], [h_in], jnp.zeros((N_META, D_MODEL), F32)
    return pl.pallas_call(
        body, name=name, grid=(n_i,),
        in_specs=h_specs + [meta_spec, rows, vec, pl.BlockSpec((tm, 2 * D_FF), lambda i: (i, 0)),
                            _resident((2 * D_FF, D_MODEL))] + [ANY] * nw,
        out_specs=[rows, vec] + [ANY] * nw,
        out_shape=[jax.ShapeDtypeStruct((n, D_MODEL), F32), jax.ShapeDtypeStruct((1, D_MODEL), F32)]
        + [jax.ShapeDtypeStruct(a.shape, a.dtype) for a in exchange],
        scratch_shapes=_comm_sems(nw) if nw else [],
        compiler_params=_params(("arbitrary",)),
    )(*h_args, meta, dh_out, gain, dgu, wgu.reshape(2 * D_FF, D_MODEL), *exchange)


def _ffn_bwd_wgu(nrm, dgu, name, exchange=()):
    n = nrm.shape[0]
    tm = _row_tile(n, 512)
    n_i = n // tm
    nw = len(exchange)

    def body(*refs):
        (nrm_ref, dgu_ref), xin, (dw_ref,), xout, (acc_scr,), sems = _split_refs(refs, 2, nw, 1, 1)
        i = pl.program_id(0)
        if nw:
            comm = _Exchange(xin, xout, sems)
            pl.when(i == 0)(comm.start)

        @pl.when(i == 0)
        def _():
            acc_scr[...] = jnp.zeros_like(acc_scr)

        nb = nrm_ref[...]
        for half in (0, D_FF):
            for a, b in HID_PIECES:
                acc_scr[half + a:half + b, :] += _dot_tn(dgu_ref[:, half + a:half + b], nb)

        @pl.when(i == n_i - 1)
        def _():
            dw_ref[...] = acc_scr[...].astype(BF16)

        if nw:
            pl.when(i == n_i - 1)(comm.finish)

    shape = (2 * D_FF, D_MODEL)
    res = pl.pallas_call(
        body, name=name, grid=(n_i,),
        in_specs=[pl.BlockSpec((tm, D_MODEL), lambda i: (i, 0)),
                  pl.BlockSpec((tm, 2 * D_FF), lambda i: (i, 0))] + [ANY] * nw,
        out_specs=[_resident(shape)] + [ANY] * nw,
        out_shape=[jax.ShapeDtypeStruct(shape, BF16)] + [jax.ShapeDtypeStruct(a.shape, a.dtype) for a in exchange],
        scratch_shapes=[pltpu.VMEM(shape, F32)] + (_comm_sems(nw) if nw else []),
        compiler_params=_params(("arbitrary",)),
    )(nrm, dgu, *exchange)
    return res if nw else res[0]


def _inproj_fwd(h, gain, w_in):
    n = h.shape[0]
    tm = _row_tile(n, 512)

    def body(h_ref, g_ref, w_ref, *outs):
        y, _, _ = _rms(h_ref[...], g_ref[...])
        nb = y.astype(BF16)
        for p in range(N_PIECE):
            outs[p][...] = _dot_nt(nb, w_ref[512 * p:512 * (p + 1), :]).astype(BF16)
        outs[N_PIECE][...] = _dot_nt(nb, w_ref[IN_MAIN:IN_PAD, :])

    piece = pl.BlockSpec((tm, 512), lambda i: (i, 0))
    return pl.pallas_call(
        body, name="inproj_fwd", grid=(n // tm,),
        in_specs=[pl.BlockSpec((tm, D_MODEL), lambda i: (i, 0)),
                  pl.BlockSpec((1, D_MODEL), lambda i: (0, 0)),
                  pl.BlockSpec((IN_PAD, D_MODEL), lambda i: (0, 0))],
        out_specs=[piece] * N_PIECE + [pl.BlockSpec((tm, 128), lambda i: (i, 0))],
        out_shape=[jax.ShapeDtypeStruct((n, 512), BF16)] * N_PIECE + [jax.ShapeDtypeStruct((n, 128), F32)],
        compiler_params=_params(("parallel",)),
    )(h, gain, w_in)


def _inproj_bwd(dpieces, dfg, dh_out, h_in, gain, w_in):
    n = h_in.shape[0]
    tm = _row_tile(n, 512)
    n_i = n // tm

    def body(*refs):
        dp_refs = refs[:N_PIECE]
        dfg_ref, dh_ref, h_ref, g_ref, w_ref, dhin_ref, dw_ref, dgain_ref, acc_scr = refs[N_PIECE:]
        i = pl.program_id(0)

        @pl.when(i == 0)
        def _():
            acc_scr[...] = jnp.zeros_like(acc_scr)
            dgain_ref[...] = jnp.zeros_like(dgain_ref)

        gain_v = g_ref[...]
        y, xhat, r = _rms(h_ref[...], gain_v)
        nb = y.astype(BF16)
        dn = jnp.zeros((tm, D_MODEL), F32)
        for p in range(N_PIECE + 1):
            lo, hi = (512 * p, 512 * (p + 1)) if p < N_PIECE else (IN_MAIN, IN_PAD)
            dp = (dp_refs[p][...] if p < N_PIECE else dfg_ref[...]).astype(BF16)
            dn = dn + _dot(dp, w_ref[lo:hi, :])
            acc_scr[lo:hi, :] += _dot_tn(dp, nb)
        dhin_ref[...] = dh_ref[...] + _rms_bwd(dn, xhat, r, gain_v)
        dgain_ref[...] += jnp.sum(dn * xhat, axis=0, keepdims=True)

        @pl.when(i == n_i - 1)
        def _():
            dw_ref[...] = acc_scr[...].astype(BF16)

    piece = pl.BlockSpec((tm, 512), lambda i: (i, 0))
    rows = pl.BlockSpec((tm, D_MODEL), lambda i: (i, 0))
    vec = pl.BlockSpec((1, D_MODEL), lambda i: (0, 0))
    wspec = pl.BlockSpec((IN_PAD, D_MODEL), lambda i: (0, 0))
    return pl.pallas_call(
        body, name="inproj_bwd", grid=(n_i,),
        in_specs=[piece] * N_PIECE + [pl.BlockSpec((tm, 128), lambda i: (i, 0)), rows, rows, vec, wspec],
        out_specs=[rows, wspec, vec],
        out_shape=[jax.ShapeDtypeStruct((n, D_MODEL), F32),
                   jax.ShapeDtypeStruct((IN_PAD, D_MODEL), BF16),
                   jax.ShapeDtypeStruct((1, D_MODEL), F32)],
        scratch_shapes=[pltpu.VMEM((IN_PAD, D_MODEL), F32)],
        compiler_params=_params(("arbitrary",)),
    )(*dpieces, dfg, dh_out, h_in, gain, w_in)


def _outproj_bwd(dh, zc, za, w_out):
    n = dh.shape[0]
    tm = _row_tile(n, 512)
    n_i = n // tm

    def body(dh_ref, zc_ref, za_ref, w_ref, dzc_ref, dza_ref, dw_ref, acc_scr):
        i = pl.program_id(0)

        @pl.when(i == 0)
        def _():
            acc_scr[...] = jnp.zeros_like(acc_scr)

        dhb = dh_ref[...].astype(BF16)
        dzc_ref[...] = _dot_nt(dhb, w_ref[0:CONV_DIM, :]).astype(BF16)
        dza_ref[...] = _dot_nt(dhb, w_ref[CONV_DIM:, :]).astype(BF16)
        acc_scr[0:CONV_DIM, :] += _dot_tn(zc_ref[...], dhb)
        acc_scr[CONV_DIM:, :] += _dot_tn(za_ref[...], dhb)

        @pl.when(i == n_i - 1)
        def _():
            dw_ref[...] = acc_scr[...].astype(BF16)

    half = pl.BlockSpec((tm, 512), lambda i: (i, 0))
    wspec = pl.BlockSpec((D_MODEL, D_MODEL), lambda i: (0, 0))
    return pl.pallas_call(
        body, name="outproj_bwd", grid=(n_i,),
        in_specs=[pl.BlockSpec((tm, D_MODEL), lambda i: (i, 0)), half, half, wspec],
        out_specs=[half, half, wspec],
        out_shape=[jax.ShapeDtypeStruct((n, 512), BF16), jax.ShapeDtypeStruct((n, 512), BF16),
                   jax.ShapeDtypeStruct((D_MODEL, D_MODEL), BF16)],
        scratch_shapes=[pltpu.VMEM((D_MODEL, D_MODEL), F32)],
        compiler_params=_params(("arbitrary",)),
    )(dh, zc, za, w_out)


def _group_matrix():
    r = lax.broadcasted_iota(jnp.int32, (128, 128), 0) // HEAD_DIM
    c = lax.broadcasted_iota(jnp.int32, (128, 128), 1) // HEAD_DIM
    return jnp.where(r == c, 1.0 / HEAD_DIM, 0.0).astype(BF16)


def _group_mean(x, gmat):
    hi = x.astype(BF16)
    lo = (x - hi.astype(F32)).astype(BF16)
    return _dot(hi, gmat) + _dot(lo, gmat)


def _shift_rows(x, s):
    rows = x.shape[0]
    t = lax.broadcasted_iota(jnp.int32, x.shape, 0)
    rolled = pltpu.roll(x, s % rows, 0)
    keep = (t >= s) if s > 0 else (t < rows + s)
    return jnp.where(keep, rolled, 0.0)


def _conv_parts(bg_ref, cg_ref, hc_ref, w_ref):
    bg = bg_ref[...].astype(F32)
    cg = cg_ref[...].astype(F32)
    hc = hc_ref[...].astype(F32)
    u = cg * hc
    u1 = _shift_rows(u, 1)
    u2 = _shift_rows(u, 2)
    conv = w_ref[2:3, :] * u + w_ref[1:2, :] * u1 + w_ref[0:1, :] * u2
    return bg, cg, hc, u, u1, u2, conv


def _conv_fwd(bg, cg, hc, conv_w, gain, gmat, lp):
    n = bg.shape[0]
    nb = n // lp

    def body(bg_ref, cg_ref, hc_ref, w_ref, g_ref, gm_ref, z_ref):
        bgv, _, _, _, _, _, conv = _conv_parts(bg_ref, cg_ref, hc_ref, w_ref)
        yc = bgv * conv
        r = lax.rsqrt(_group_mean(yc * yc, gm_ref[...]) + EPS)
        z_ref[...] = (yc * r * g_ref[...]).astype(BF16)

    blk = pl.BlockSpec((lp, 128), lambda c, b: (b, c))
    return pl.pallas_call(
        body, name="conv_fwd", grid=(CONV_DIM // 128, nb),
        in_specs=[blk, blk, blk, pl.BlockSpec((3, 128), lambda c, b: (0, c)),
                  pl.BlockSpec((1, 128), lambda c, b: (0, c)), pl.BlockSpec((128, 128), lambda c, b: (0, 0))],
        out_specs=blk,
        out_shape=jax.ShapeDtypeStruct((n, CONV_DIM), BF16),
        compiler_params=_params(("parallel", "parallel")),
    )(bg, cg, hc, conv_w, gain, gmat)


def _conv_bwd(dz, bg, cg, hc, conv_w, gain, gmat, lp):
    n = bg.shape[0]
    nb = n // lp

    def body(dz_ref, bg_ref, cg_ref, hc_ref, w_ref, g_ref, gm_ref,
             dbg_ref, dcg_ref, dhc_ref, dw_ref, dgain_ref):
        b = pl.program_id(1)

        @pl.when(b == 0)
        def _():
            dw_ref[...] = jnp.zeros_like(dw_ref)
            dgain_ref[...] = jnp.zeros_like(dgain_ref)

        bgv, cgv, hcv, u, u1, u2, conv = _conv_parts(bg_ref, cg_ref, hc_ref, w_ref)
        gm = gm_ref[...]
        yc = bgv * conv
        r = lax.rsqrt(_group_mean(yc * yc, gm) + EPS)
        yhat = yc * r
        dzv = dz_ref[...].astype(F32)
        dyhat = dzv * g_ref[...]
        dgain_ref[...] += jnp.sum(dzv * yhat, axis=0, keepdims=True)
        dyc = r * (dyhat - yhat * _group_mean(dyhat * yhat, gm))
        dbg_ref[...] = (dyc * conv).astype(BF16)
        dconv = dyc * bgv
        du = (w_ref[2:3, :] * dconv + w_ref[1:2, :] * _shift_rows(dconv, -1)
              + w_ref[0:1, :] * _shift_rows(dconv, -2))
        dcg_ref[...] = (du * hcv).astype(BF16)
        dhc_ref[...] = (du * cgv).astype(BF16)
        dw_ref[0:1, :] += jnp.sum(dconv * u2, axis=0, keepdims=True)
        dw_ref[1:2, :] += jnp.sum(dconv * u1, axis=0, keepdims=True)
        dw_ref[2:3, :] += jnp.sum(dconv * u, axis=0, keepdims=True)

    blk = pl.BlockSpec((lp, 128), lambda c, b: (b, c))
    wspec = pl.BlockSpec((3, 128), lambda c, b: (0, c))
    gspec = pl.BlockSpec((1, 128), lambda c, b: (0, c))
    return pl.pallas_call(
        body, name="conv_bwd", grid=(CONV_DIM // 128, nb),
        in_specs=[blk, blk, blk, blk, wspec, gspec, pl.BlockSpec((128, 128), lambda c, b: (0, 0))],
        out_specs=[blk, blk, blk, wspec, gspec],
        out_shape=[jax.ShapeDtypeStruct((n, CONV_DIM), BF16)] * 3
        + [jax.ShapeDtypeStruct((3, CONV_DIM), F32), jax.ShapeDtypeStruct((1, CONV_DIM), F32)],
        compiler_params=_params(("parallel", "arbitrary")),
    )(dz, bg, cg, hc, conv_w, gain, gmat)


KEY_MASKED = 1e30
ONE_LANE = 24


def _scan_steps(rows):
    s, out = 1, []
    while s < rows:
        out.append(s)
        s *= 2
    return out


def _fgate_fwd(fg, b_f, lp):
    n = fg.shape[0]
    nb = n // lp

    def body(fg_ref, b_ref, ka_ref, qa_ref):
        x = fg_ref[...] + b_ref[...]
        logf = jnp.minimum(x, 0.0) - jnp.log(1.0 + jnp.exp(-jnp.abs(x)))
        t = lax.broadcasted_iota(jnp.int32, (lp, 128), 0)
        lane = lax.broadcasted_iota(jnp.int32, (lp, 128), 1)
        f = jnp.where((t >= PAD) & (lane < N_HEADS), logf, 0.0)
        for s in _scan_steps(lp):
            f = f + _shift_rows(f, s)
        hi = f.astype(BF16).astype(F32)
        rest = f - hi
        mid = rest.astype(BF16).astype(F32)
        lo = (rest - mid).astype(BF16).astype(F32)
        ones = jnp.where((lane >= ONE_LANE) & (lane < ONE_LANE + 3), 1.0, 0.0)
        hi_key = jnp.where((t < PAD) & (lane < N_HEADS), KEY_MASKED, hi)
        ka_ref[...] = (hi_key + pltpu.roll(mid, 8, 1) + pltpu.roll(lo, 16, 1) + ones).astype(BF16)
        for h in range(N_HEADS):
            minus = jnp.where((lane == h) | (lane == 8 + h) | (lane == 16 + h), -1.0, 0.0)
            terms = (jnp.where(lane == ONE_LANE, pltpu.roll(hi, ONE_LANE - h, 1), 0.0)
                     + jnp.where(lane == ONE_LANE + 1, pltpu.roll(mid, ONE_LANE + 1 - h, 1), 0.0)
                     + jnp.where(lane == ONE_LANE + 2, pltpu.roll(lo, ONE_LANE + 2 - h, 1), 0.0))
            qa_ref[:, 128 * h:128 * (h + 1)] = (minus + terms).astype(BF16)

    return pl.pallas_call(
        body, name="fgate_fwd", grid=(nb,),
        in_specs=[pl.BlockSpec((lp, 128), lambda b: (b, 0)), pl.BlockSpec((1, 128), lambda b: (0, 0))],
        out_specs=[pl.BlockSpec((lp, 128), lambda b: (b, 0)), pl.BlockSpec((lp, N_HEADS * 128), lambda b: (b, 0))],
        out_shape=[jax.ShapeDtypeStruct((n, 128), BF16), jax.ShapeDtypeStruct((n, N_HEADS * 128), BF16)],
        compiler_params=_params(("parallel",)),
    )(fg, b_f)


def _fgate_bwd(dka, dfr, fg, b_f, lp):
    n = fg.shape[0]
    nb = n // lp

    def body(dka_ref, dfr_ref, fg_ref, b_ref, dfg_ref, db_ref):
        b = pl.program_id(0)

        @pl.when(b == 0)
        def _():
            db_ref[...] = jnp.zeros_like(db_ref)

        wide = jnp.concatenate([dfr_ref[0], jnp.zeros((128 - N_HEADS, lp), F32)], axis=0)
        t = lax.broadcasted_iota(jnp.int32, (lp, 128), 0)
        lane = lax.broadcasted_iota(jnp.int32, (lp, 128), 1)
        d = jnp.where(lane < N_HEADS, dka_ref[...], 0.0) + wide.T
        for s in _scan_steps(lp):
            d = d + _shift_rows(d, -s)
        x = fg_ref[...] + b_ref[...]
        dx = jnp.where((t >= PAD) & (lane < N_HEADS), d * _sigmoid(-x), 0.0)
        dfg_ref[...] = dx
        db_ref[...] += jnp.sum(dx, axis=0, keepdims=True)

    return pl.pallas_call(
        body, name="fgate_bwd", grid=(nb,),
        in_specs=[pl.BlockSpec((lp, 128), lambda b: (b, 0)), pl.BlockSpec((1, N_HEADS, lp), lambda b: (b, 0, 0)),
                  pl.BlockSpec((lp, 128), lambda b: (b, 0)), pl.BlockSpec((1, 128), lambda b: (0, 0))],
        out_specs=[pl.BlockSpec((lp, 128), lambda b: (b, 0)), pl.BlockSpec((1, 128), lambda b: (0, 0))],
        out_shape=[jax.ShapeDtypeStruct((n, 128), F32), jax.ShapeDtypeStruct((1, 128), F32)],
        compiler_params=_params(("arbitrary",)),
    )(dka, dfr, fg, b_f)


def _head_masks():
    lane = lax.broadcasted_iota(jnp.int32, (1, 128), 1)
    return lane < HEAD_DIM


def _stack_heads(x2, first):
    zero = jnp.zeros_like(x2)
    return jnp.concatenate([jnp.where(first, x2, zero), jnp.where(first, zero, x2)], axis=0)


def _stack_heads_lanes(xt):
    r = lax.broadcasted_iota(jnp.int32, xt.shape, 0)
    zero = jnp.zeros_like(xt)
    return jnp.concatenate([jnp.where(r < HEAD_DIM, xt, zero), jnp.where(r < HEAD_DIM, zero, xt)], axis=1)


def _pair_cols(col0, col1, first):
    return jnp.where(first, col0, col1)


def _pair_rows(row0, row1):
    r = lax.broadcasted_iota(jnp.int32, (128, TQ), 0)
    return jnp.where(r < HEAD_DIM, row0, row1)


def _query_side(q_ref, qa_ref, p, first):
    q2 = q_ref[:, 128 * p:128 * (p + 1)] * 0.125
    zero = jnp.zeros_like(q2)
    top = jnp.concatenate([jnp.where(first, q2, zero), qa_ref[:, 128 * (2 * p):128 * (2 * p + 1)]], axis=1)
    bot = jnp.concatenate([jnp.where(first, zero, q2), qa_ref[:, 128 * (2 * p + 1):128 * (2 * p + 2)]], axis=1)
    return jnp.concatenate([top, bot], axis=0)


def _key_chunks(lp):
    return (lp + TK - 1) // TK


def _chunk_mask(i, c, tk):
    r = lax.broadcasted_iota(jnp.int32, (tk, 2 * TQ), 0)
    col = lax.broadcasted_iota(jnp.int32, (tk, 2 * TQ), 1)
    return (c * TK + r) <= (i * TQ + (col & (TQ - 1)))


def _causal_sweep(i, step, init):
    per = TK // TQ
    last = i // per
    carry = lax.fori_loop(0, last, lambda c, carry: step(c, carry, False, TK), init)
    tails = [lambda carry, r=r: step(last, carry, True, TQ * (r + 1)) for r in range(per)]
    return lax.switch(i % per, tails, carry)


def _transpose_bf16(x):
    return x.astype(F32).T.astype(BF16)


def _attn_fwd(q, qa, k, v, ka, gain, zc, w_out, h, lp):
    n = q.shape[0]
    nb = n // lp
    nq = lp // TQ
    lpp = _key_chunks(lp) * TK

    def body(q_ref, qa_ref, k_ref, v_ref, ka_ref, g_ref, zc_ref, w_ref, h_ref,
             z_ref, o_ref, lse_ref, hout_ref, kx_scr, vt_scr):
        i = pl.program_id(1)
        first = _head_masks()

        @pl.when(i == 0)
        def _():
            if lpp > lp:
                kx_scr[lp:lpp, :] = jnp.zeros((lpp - lp, 2 * ATTN_DIM), BF16)
                vt_scr[:, lp:lpp] = jnp.zeros((ATTN_DIM, lpp - lp), BF16)
            for p in range(N_PAIRS):
                kx_scr[0:lp, 256 * p:256 * p + 128] = k_ref[:, 128 * p:128 * (p + 1)]
                kx_scr[0:lp, 256 * p + 128:256 * (p + 1)] = ka_ref[...]
            vt_scr[:, 0:lp] = _transpose_bf16(v_ref[...])

        rhs_t = [_transpose_bf16(_query_side(q_ref, qa_ref, p, first)) for p in range(N_PAIRS)]

        def step(c, carry, masked, tk):
            koff = pl.multiple_of(c * TK, TK)
            valid = _chunk_mask(i, c, tk) if masked else None
            new = []
            for p in range(N_PAIRS):
                m, l, acc = carry[p]
                st = _dot(kx_scr[pl.ds(koff, tk), 256 * p:256 * (p + 1)], rhs_t[p])
                if masked:
                    st = jnp.where(valid, st, NEG)
                m_new = jnp.maximum(m, jnp.max(st, axis=0, keepdims=True))
                pt = jnp.exp(st - m_new)
                alpha = jnp.exp(m - m_new)
                l = alpha * l + jnp.sum(pt, axis=0, keepdims=True)
                pb = pt.astype(BF16)
                vt = _stack_heads_lanes(vt_scr[128 * p:128 * (p + 1), pl.ds(koff, tk)])
                pv = _dot(vt, jnp.concatenate([pb[:, 0:TQ], pb[:, TQ:]], axis=0))
                acc = acc * _pair_rows(alpha[:, 0:TQ], alpha[:, TQ:]) + pv
                new.append((m_new, l, acc))
            return tuple(new)

        init = tuple((jnp.full((1, 2 * TQ), NEG, F32), jnp.zeros((1, 2 * TQ), F32), jnp.zeros((128, TQ), F32))
                     for _ in range(N_PAIRS))
        final = _causal_sweep(i, step, init)

        row = lax.broadcasted_iota(jnp.int32, (TQ, 128), 0)
        real = (i * TQ + row) >= PAD
        hout = h_ref[...] + _dot(zc_ref[...], w_ref[0:CONV_DIM, :])
        for p in range(N_PAIRS):
            m, l, acc = final[p]
            inv = 1.0 / l
            ot = acc * _pair_rows(inv[:, 0:TQ], inv[:, TQ:])
            sq = ot * ot
            r0 = lax.rsqrt(jnp.sum(sq[0:HEAD_DIM], axis=0, keepdims=True) * (1.0 / HEAD_DIM) + EPS)
            r1 = lax.rsqrt(jnp.sum(sq[HEAD_DIM:], axis=0, keepdims=True) * (1.0 / HEAD_DIM) + EPS)
            cols = slice(128 * p, 128 * (p + 1))
            o_ref[:, cols] = jnp.where(real, ot.T, 0.0).astype(BF16)
            z = (jnp.where(real, (ot * _pair_rows(r0, r1)).T, 0.0) * g_ref[:, cols]).astype(BF16)
            z_ref[:, cols] = z
            hout = hout + _dot(z, w_ref[CONV_DIM + 128 * p:CONV_DIM + 128 * (p + 1), :])
            lse = m + jnp.log(l)
            lse_ref[0, 2 * p:2 * p + 1, :] = lse[:, 0:TQ]
            lse_ref[0, 2 * p + 1:2 * p + 2, :] = lse[:, TQ:]
        hout_ref[...] = hout

    qblk = pl.BlockSpec((TQ, ATTN_DIM), lambda b, i: (b * nq + i, 0))
    qablk = pl.BlockSpec((TQ, N_HEADS * 128), lambda b, i: (b * nq + i, 0))
    seq = pl.BlockSpec((lp, ATTN_DIM), lambda b, i: (b, 0))
    rowblk = pl.BlockSpec((1, N_HEADS, TQ), lambda b, i: (b, 0, i))
    hblk = pl.BlockSpec((TQ, D_MODEL), lambda b, i: (b * nq + i, 0))
    return pl.pallas_call(
        body, name="attn_fwd", grid=(nb, nq),
        in_specs=[qblk, qablk, seq, seq, pl.BlockSpec((lp, 128), lambda b, i: (b, 0)),
                  pl.BlockSpec((1, ATTN_DIM), lambda b, i: (0, 0)), qblk,
                  pl.BlockSpec((D_MODEL, D_MODEL), lambda b, i: (0, 0)), hblk],
        out_specs=[qblk, qblk, rowblk, hblk],
        out_shape=[jax.ShapeDtypeStruct((n, ATTN_DIM), BF16), jax.ShapeDtypeStruct((n, ATTN_DIM), BF16),
                   jax.ShapeDtypeStruct((nb, N_HEADS, lp), F32), jax.ShapeDtypeStruct((n, D_MODEL), F32)],
        scratch_shapes=[pltpu.VMEM((lpp, 2 * ATTN_DIM), BF16), pltpu.VMEM((ATTN_DIM, lpp), BF16)],
        compiler_params=_params(("parallel", "arbitrary")),
    )(q, qa, k, v, ka, gain, zc, w_out, h)


def _attn_bwd(dz, q, qa, k, v, ka, o, lse, gain, lp, exchange=()):
    n = q.shape[0]
    nb = n // lp
    nq = lp // TQ
    lpp = _key_chunks(lp) * TK
    nw = len(exchange)

    def body(*refs):
        ((dz_ref, q_ref, qa_ref, k_ref, v_ref, ka_ref, o_ref, lse_ref, g_ref), xin,
         (dq_ref, dk_ref, dv_ref, dka_ref, dfr_ref, dgain_ref), xout,
         (kx_scr, vx_scr, kt_scr, dkx_scr, dvx_scr), sems) = _split_refs(refs, 9, nw, 6, 5)
        b = pl.program_id(0)
        i = pl.program_id(1)
        first = _head_masks()
        if nw:
            comm = _Exchange(xin, xout, sems)
            pl.when((b == 0) & (i == 0))(comm.start)

        @pl.when((b == 0) & (i == 0))
        def _():
            dgain_ref[...] = jnp.zeros_like(dgain_ref)

        @pl.when(i == 0)
        def _():
            if lpp > lp:
                kx_scr[lp:lpp, :] = jnp.zeros((lpp - lp, 2 * ATTN_DIM), BF16)
                vx_scr[lp:lpp, :] = jnp.zeros((lpp - lp, ATTN_DIM), BF16)
                kt_scr[:, lp:lpp] = jnp.zeros((ATTN_DIM, lpp - lp), BF16)
            for p in range(N_PAIRS):
                kx_scr[0:lp, 256 * p:256 * p + 128] = k_ref[:, 128 * p:128 * (p + 1)]
                kx_scr[0:lp, 256 * p + 128:256 * (p + 1)] = ka_ref[...]
            vx_scr[0:lp, :] = v_ref[...]
            kt_scr[:, 0:lp] = _transpose_bf16(k_ref[...])
            dkx_scr[...] = jnp.zeros_like(dkx_scr)
            dvx_scr[...] = jnp.zeros_like(dvx_scr)

        rhs, rhs_t, lses, dos, dos_t, deltas = [], [], [], [], [], []
        for p in range(N_PAIRS):
            cols = slice(128 * p, 128 * (p + 1))
            side = _query_side(q_ref, qa_ref, p, first)
            rhs.append(side)
            rhs_t.append(_transpose_bf16(side))
            lses.append(jnp.concatenate([lse_ref[0, 2 * p:2 * p + 1, :], lse_ref[0, 2 * p + 1:2 * p + 2, :]], axis=1))
            ov = o_ref[:, cols].astype(F32)
            dzv = dz_ref[:, cols].astype(F32)
            gv = g_ref[:, cols]
            sq = ov * ov
            ms0 = jnp.sum(jnp.where(first, sq, 0.0), axis=1, keepdims=True) * (1.0 / HEAD_DIM)
            ms1 = jnp.sum(jnp.where(first, 0.0, sq), axis=1, keepdims=True) * (1.0 / HEAD_DIM)
            r = _pair_cols(lax.rsqrt(ms0 + EPS), lax.rsqrt(ms1 + EPS), first)
            ohat = ov * r
            dyhat = dzv * gv
            dgain_ref[:, cols] += jnp.sum(dzv * ohat, axis=0, keepdims=True)
            pr = dyhat * ohat
            mean0 = jnp.sum(jnp.where(first, pr, 0.0), axis=1, keepdims=True) * (1.0 / HEAD_DIM)
            mean1 = jnp.sum(jnp.where(first, 0.0, pr), axis=1, keepdims=True) * (1.0 / HEAD_DIM)
            do = r * (dyhat - ohat * _pair_cols(mean0, mean1, first))
            ddt = (do * ov).T
            deltas.append(jnp.concatenate([jnp.sum(ddt[0:HEAD_DIM], axis=0, keepdims=True),
                                           jnp.sum(ddt[HEAD_DIM:], axis=0, keepdims=True)], axis=1))
            do_st = _stack_heads(do.astype(BF16), first)
            dos.append(do_st)
            dos_t.append(_transpose_bf16(do_st))

        def step(c, carry, masked, tk):
            koff = pl.multiple_of(c * TK, TK)
            valid = _chunk_mask(i, c, tk) if masked else None
            new = []
            for p in range(N_PAIRS):
                dqt, dfq = carry[p]
                ext = slice(256 * p, 256 * (p + 1))
                cols = slice(128 * p, 128 * (p + 1))
                st = _dot(kx_scr[pl.ds(koff, tk), ext], rhs_t[p])
                if masked:
                    st = jnp.where(valid, st, NEG)
                pt = jnp.exp(st - lses[p])
                dpt = _dot(vx_scr[pl.ds(koff, tk), cols], dos_t[p])
                dst = pt * (dpt - deltas[p])
                dsb = dst.astype(BF16)
                dfq = dfq + jnp.sum(dsb.astype(F32), axis=0, keepdims=True)
                dkx_scr[pl.ds(koff, tk), ext] += _dot(dsb, rhs[p])
                dvx_scr[pl.ds(koff, tk), cols] += _dot(pt.astype(BF16), dos[p])
                kt = _stack_heads_lanes(kt_scr[cols, pl.ds(koff, tk)])
                dqt = dqt + _dot(kt, jnp.concatenate([dsb[:, 0:TQ], dsb[:, TQ:]], axis=0))
                new.append((dqt, dfq))
            return tuple(new)

        init = tuple((jnp.zeros((128, TQ), F32), jnp.zeros((1, 2 * TQ), F32)) for _ in range(N_PAIRS))
        final = _causal_sweep(i, step, init)

        for p in range(N_PAIRS):
            dqt, dfq = final[p]
            dq_ref[:, 128 * p:128 * (p + 1)] = (dqt.T * 0.125).astype(BF16)
            dfr_ref[0, 2 * p:2 * p + 1, :] = dfq[:, 0:TQ]
            dfr_ref[0, 2 * p + 1:2 * p + 2, :] = dfq[:, TQ:]

        @pl.when(i == nq - 1)
        def _():
            dka = jnp.zeros((lp, 128), F32)
            for p in range(N_PAIRS):
                dk_ref[:, 128 * p:128 * (p + 1)] = dkx_scr[0:lp, 256 * p:256 * p + 128].astype(BF16)
                dka = dka + dkx_scr[0:lp, 256 * p + 128:256 * (p + 1)]
            dka_ref[...] = dka
            dv_ref[...] = dvx_scr[0:lp, :].astype(BF16)

        if nw:
            pl.when((b == nb - 1) & (i == nq - 1))(comm.finish)

    qblk = pl.BlockSpec((TQ, ATTN_DIM), lambda b, i: (b * nq + i, 0))
    qablk = pl.BlockSpec((TQ, N_HEADS * 128), lambda b, i: (b * nq + i, 0))
    seq = pl.BlockSpec((lp, ATTN_DIM), lambda b, i: (b, 0))
    kaseq = pl.BlockSpec((lp, 128), lambda b, i: (b, 0))
    rowblk = pl.BlockSpec((1, N_HEADS, TQ), lambda b, i: (b, 0, i))
    gspec = pl.BlockSpec((1, ATTN_DIM), lambda b, i: (0, 0))
    return pl.pallas_call(
        body, name="attn_bwd", grid=(nb, nq),
        in_specs=[qblk, qblk, qablk, seq, seq, kaseq, qblk, rowblk, gspec] + [ANY] * nw,
        out_specs=[qblk, seq, seq, kaseq, rowblk, gspec] + [ANY] * nw,
        out_shape=[jax.ShapeDtypeStruct((n, ATTN_DIM), BF16), jax.ShapeDtypeStruct((n, ATTN_DIM), BF16),
                   jax.ShapeDtypeStruct((n, ATTN_DIM), BF16), jax.ShapeDtypeStruct((n, 128), F32),
                   jax.ShapeDtypeStruct((nb, N_HEADS, lp), F32), jax.ShapeDtypeStruct((1, ATTN_DIM), F32)]
        + [jax.ShapeDtypeStruct(a.shape, a.dtype) for a in exchange],
        scratch_shapes=[pltpu.VMEM((lpp, 2 * ATTN_DIM), BF16), pltpu.VMEM((lpp, ATTN_DIM), BF16),
                        pltpu.VMEM((ATTN_DIM, lpp), BF16), pltpu.VMEM((lpp, 2 * ATTN_DIM), F32),
                        pltpu.VMEM((lpp, ATTN_DIM), F32)] + (_comm_sems(nw) if nw else []),
        compiler_params=_params(("arbitrary", "arbitrary")),
    )(dz, q, qa, k, v, ka, o, lse, gain, *exchange)


def _adamw(parts, w, m, v, name):
    s_parts, r, c = parts.shape
    tr = r
    for t in (256, 128, 64, 32, 16):
        if r % t == 0 and r > t:
            tr = t
            break

    def body(p_ref, w_ref, m_ref, v_ref, g_ref, d_ref, nm_ref, nv_ref):
        g = p_ref[0].astype(F32)
        for s in range(1, s_parts):
            g = g + p_ref[s].astype(F32)
        nm = ADAM_B1 * m_ref[...] + (1.0 - ADAM_B1) * g
        nv = ADAM_B2 * v_ref[...] + (1.0 - ADAM_B2) * (g * g)
        m_hat = nm / (1.0 - ADAM_B1 ** ADAM_STEP)
        v_hat = nv / (1.0 - ADAM_B2 ** ADAM_STEP)
        g_ref[...] = g
        d_ref[...] = -ADAM_LR * (m_hat / (jnp.sqrt(v_hat) + ADAM_EPS) + ADAM_WD * w_ref[...])
        nm_ref[...] = nm
        nv_ref[...] = nv

    blk = pl.BlockSpec((tr, c), lambda i: (i, 0))
    return pl.pallas_call(
        body, name=name, grid=(r // tr,),
        in_specs=[pl.BlockSpec((s_parts, tr, c), lambda i: (0, i, 0)), blk, blk, blk],
        out_specs=[blk] * 4,
        out_shape=[jax.ShapeDtypeStruct((r, c), F32)] * 4,
        compiler_params=_params(("parallel",)),
    )(parts, w, m, v)


def _sum_parts(parts, name):
    s_parts, r, c = parts.shape

    def body(p_ref, out_ref):
        acc = p_ref[0]
        for s in range(1, s_parts):
            acc = acc + p_ref[s]
        out_ref[...] = acc

    return pl.pallas_call(
        body, name=name, out_shape=jax.ShapeDtypeStruct((r, c), F32),
        in_specs=[pl.BlockSpec(memory_space=pltpu.VMEM)], out_specs=pl.BlockSpec(memory_space=pltpu.VMEM),
    )(parts)


SMALL_ROWS = 184


def _pack_small(d_gains, d_gc, d_ga, d_bf, d_conv, d_meta):
    rows = [g.reshape(8, 128) for g in d_gains]
    rows += [d_gc.reshape(4, 128), d_ga.reshape(4, 128), d_bf.reshape(1, 128)]
    rows += [d_conv.reshape(12, 128), d_meta.reshape(128, 128)]
    packed = jnp.concatenate(rows, axis=0)
    return jnp.pad(packed, ((0, SMALL_ROWS - packed.shape[0]), (0, 0)))


def kernel(x, meta_tokens, ffn1_norm, ffn1_w_gu, ffn1_w_down, mix_norm, w_in, conv_w, b_f, out_norm_conv, out_norm_attn, w_out, ffn2_norm, ffn2_w_gu, ffn2_w_down, final_norm, loss_target, m_meta_tokens, m_ffn1_norm, m_ffn1_w_gu, m_ffn1_w_down, m_mix_norm, m_w_in, m_conv_w, m_b_f, m_out_norm_conv, m_out_norm_attn, m_w_out, m_ffn2_norm, m_ffn2_w_gu, m_ffn2_w_down, m_final_norm, v_meta_tokens, v_ffn1_norm, v_ffn1_w_gu, v_ffn1_w_down, v_mix_norm, v_w_in, v_conv_w, v_b_f, v_out_norm_conv, v_out_norm_attn, v_w_out, v_ffn2_norm, v_ffn2_w_gu, v_ffn2_w_down, v_final_norm):
    nb, seq, _ = x.shape
    lp = PAD + N_META + seq
    me = 4 * lax.axis_index("x") + 2 * lax.axis_index("y") + lax.axis_index("c")
    shard_gu = D_FF // 4
    shard_d = D_FF // N_DEV

    small_in = jnp.concatenate(
        [meta_tokens, jnp.pad(conv_w[0], ((0, 0), (0, 128 - conv_w.shape[2]))), jnp.zeros((5, 128), F32)], axis=0)
    wgu1_8, wd1_8, small_8 = _all_gather(
        [ffn1_w_gu[0].T.astype(BF16), ffn1_w_down[0].astype(BF16), small_in], "gather_ffn1")
    meta_full = small_8[:, 0:N_META, :].transpose(1, 0, 2).reshape(N_META, D_MODEL)
    conv_full = small_8[:, N_META:N_META + 3, 0:CONV_DIM // N_DEV].transpose(1, 0, 2).reshape(3, CONV_DIM)
    wgu1 = wgu1_8.reshape(W_GU_SHAPE)
    wd1 = wd1_8.reshape(W_D_SHAPE)
    b_f_row = jnp.pad(b_f, ((0, 0), (0, 128 - N_HEADS)))
    gmat = _group_matrix()

    x2d = x.reshape(nb * seq, D_MODEL)
    later = [w_in[0].T.astype(BF16), w_out[0].astype(BF16), ffn2_w_gu[0].T.astype(BF16), ffn2_w_down[0].astype(BF16)]
    h1, n1, gate1, up1, win_8, wout_8, wgu2_8, wd2_8 = _ffn1_fwd(x2d, meta_full, lp, ffn1_norm, wgu1, wd1, later)
    wgu2 = wgu2_8.reshape(W_GU_SHAPE)
    wd2 = wd2_8.reshape(W_D_SHAPE)
    w_in_full = jnp.pad(win_8.reshape(IN_DIM, D_MODEL), ((0, IN_PAD - IN_DIM), (0, 0)))
    w_out_full = wout_8.reshape(D_MODEL, D_MODEL)

    bg, cg, hc, q, k, v, fg = _inproj_fwd(h1, mix_norm, w_in_full)
    zc = _conv_fwd(bg, cg, hc, conv_full, out_norm_conv, gmat, lp)
    ka, qa = _fgate_fwd(fg, b_f_row, lp)
    za, o, lse, h2 = _attn_fwd(q, qa, k, v, ka, out_norm_attn, zc, w_out_full, h1, lp)
    dh3, n3, gate2, up2, loss_part, d_final = _ffn2_fwd_loss(
        h2, ffn2_norm, wgu2, wd2, final_norm.reshape(1, D_MODEL), loss_target.reshape(nb * seq, D_MODEL), lp)

    dgu2, dwd2 = _ffn_bwd_act_wd(dh3, gate2, up2, wd2, "ffn2_bwd_act")
    dh2, d_ffn2 = _ffn_bwd_in(dh3, h2, ffn2_norm, dgu2, wgu2, "ffn2_bwd_in")
    dwgu2 = _ffn_bwd_wgu(n3, dgu2, "ffn2_bwd_wgu")
    dzc, dza, dwout = _outproj_bwd(dh2, zc, za, w_out_full)
    send_a = [dwgu2.reshape(N_DEV, shard_gu, D_MODEL), dwd2.reshape(N_DEV, shard_d, D_MODEL),
              dwout.reshape(N_DEV, D_MODEL // N_DEV, D_MODEL)]
    dq, dk, dv, dka, dfr, d_ga, p_wgu2, p_wd2, p_wout = _attn_bwd(
        dza, q, qa, k, v, ka, o, lse, out_norm_attn, lp, exchange=send_a)
    dfg, d_bf = _fgate_bwd(dka, dfr, fg, b_f_row, lp)
    dbg, dcg, dhc, d_conv, d_gc = _conv_bwd(dzc, bg, cg, hc, conv_full, out_norm_conv, gmat, lp)
    dh1, dwin, d_mix = _inproj_bwd([dbg, dcg, dhc, dq, dk, dv], dfg, dh2, h1, mix_norm, w_in_full)
    dwin_8 = dwin[0:IN_DIM].reshape(N_DEV, IN_DIM // N_DEV, D_MODEL)
    dgu1, dwd1, p_win = _ffn_bwd_act_wd(dh1, gate1, up1, wd1, "ffn1_bwd_act", exchange=[dwin_8])
    dwgu1, p_wd1 = _ffn_bwd_wgu(n1, dgu1, "ffn1_bwd_wgu", exchange=[dwd1.reshape(N_DEV, shard_d, D_MODEL)])
    own = dwgu1.reshape(N_DEV, shard_gu, D_MODEL)
    (got,) = _pair_exchange([own], "pair_exchange_ffn1")
    chip_sum = _pair_sum(own, got, "pair_sum_wgu1")
    dh0, d_ffn1, p_wgu1 = _ffn_bwd_in(dh1, None, ffn1_norm, dgu1, wgu1, "ffn1_bwd_in",
                                      tokens=(x2d, meta_full, lp), exchange=[chip_sum])

    dh0 = dh0.reshape(nb, lp, D_MODEL)
    grad_x = dh0[:, PAD + N_META:, :]
    d_meta = jnp.sum(dh0[:, PAD:PAD + N_META, :], axis=0)

    small = _pack_small([d_ffn1, d_mix, d_ffn2, d_final], d_gc, d_ga, d_bf, d_conv, d_meta)
    (small_all,) = _all_gather([small], "gather_small_grads")
    small_sum = _sum_parts(small_all, "sum_small_grads")
    g_ffn1n, g_mixn, g_ffn2n, g_finaln = (small_sum[8 * t:8 * t + 8].reshape(1, D_MODEL) for t in range(4))
    g_gc = small_sum[32:36].reshape(1, CONV_DIM)
    g_ga = small_sum[36:40].reshape(1, ATTN_DIM)
    g_bf = small_sum[40:41, 0:N_HEADS]
    g_conv_full = small_sum[41:53].reshape(3, CONV_DIM)
    g_meta_full = small_sum[53:181].reshape(N_META, D_MODEL)
    g_conv = lax.dynamic_slice_in_dim(g_conv_full, me * (CONV_DIM // N_DEV), CONV_DIM // N_DEV, axis=1)
    g_meta = lax.dynamic_slice_in_dim(g_meta_full, me * (D_MODEL // N_DEV), D_MODEL // N_DEV, axis=1)

    weights = {
        "meta_tokens": (g_meta[None], meta_tokens, m_meta_tokens, v_meta_tokens),
        "ffn1_norm": (g_ffn1n[None], ffn1_norm, m_ffn1_norm, v_ffn1_norm),
        "ffn1_w_gu": (p_wgu1, ffn1_w_gu[0].T, m_ffn1_w_gu[0].T, v_ffn1_w_gu[0].T),
        "ffn1_w_down": (p_wd1, ffn1_w_down[0], m_ffn1_w_down[0], v_ffn1_w_down[0]),
        "mix_norm": (g_mixn[None], mix_norm, m_mix_norm, v_mix_norm),
        "w_in": (p_win, w_in[0].T, m_w_in[0].T, v_w_in[0].T),
        "conv_w": (g_conv[None], conv_w[0], m_conv_w[0], v_conv_w[0]),
        "b_f": (g_bf[None], b_f, m_b_f, v_b_f),
        "out_norm_conv": (g_gc[None], out_norm_conv, m_out_norm_conv, v_out_norm_conv),
        "out_norm_attn": (g_ga[None], out_norm_attn, m_out_norm_attn, v_out_norm_attn),
        "w_out": (p_wout, w_out[0], m_w_out[0], v_w_out[0]),
        "ffn2_norm": (g_ffn2n[None], ffn2_norm, m_ffn2_norm, v_ffn2_norm),
        "ffn2_w_gu": (p_wgu2, ffn2_w_gu[0].T, m_ffn2_w_gu[0].T, v_ffn2_w_gu[0].T),
        "ffn2_w_down": (p_wd2, ffn2_w_down[0], m_ffn2_w_down[0], v_ffn2_w_down[0]),
        "final_norm": (g_finaln[None], final_norm.reshape(1, D_MODEL), m_final_norm.reshape(1, D_MODEL),
                       v_final_norm.reshape(1, D_MODEL)),
    }
    shapes = {"meta_tokens": meta_tokens.shape, "ffn1_norm": ffn1_norm.shape, "ffn1_w_gu": ffn1_w_gu.shape,
              "ffn1_w_down": ffn1_w_down.shape, "mix_norm": mix_norm.shape, "w_in": w_in.shape,
              "conv_w": conv_w.shape, "b_f": b_f.shape, "out_norm_conv": out_norm_conv.shape,
              "out_norm_attn": out_norm_attn.shape, "w_out": w_out.shape, "ffn2_norm": ffn2_norm.shape,
              "ffn2_w_gu": ffn2_w_gu.shape, "ffn2_w_down": ffn2_w_down.shape, "final_norm": final_norm.shape}
    grads, deltas, new_m, new_v = [], [], [], []
    for name, (p, w, m, vv) in weights.items():
        g, d, nm, nv = _adamw(p, w, m, vv, "adamw_" + name)
        if name in ("ffn1_w_gu", "ffn2_w_gu", "w_in"):
            g, d, nm, nv = g.T, d.T, nm.T, nv.T
        shape = shapes[name]
        grads.append(g.reshape(shape))
        deltas.append(d.reshape(shape))
        new_m.append(nm.reshape(shape))
        new_v.append(nv.reshape(shape))

    loss = lax.psum(loss_part[0, 0], ("x", "y", "c"))
    return (loss, grad_x, *grads, *deltas, *new_m, *new_v)
```

```python
import jax
import jax.numpy as jnp
from jax import lax
from jax.experimental import pallas as pl
from jax.experimental.pallas import tpu as pltpu

F32 = jnp.float32
BF16 = jnp.bfloat16

N_DEV = 8
D_MODEL = 1024
N_META = 16
PAD = 128 - N_META
CONV_DIM = 512
ATTN_DIM = 512
HEAD_DIM = 64
N_HEADS = 8
N_PAIRS = N_HEADS // 2
D_FF = 2816
IN_DIM = 3080
IN_PAD = 3200
IN_MAIN = 3072
N_PIECE = IN_MAIN // 512
EPS = 1e-6
NEG = -1e30
TQ = 128
TK = 512
VMEM_LIMIT = 56 * 1024 * 1024

HID_PIECES = ((0, 1024), (1024, 2048), (2048, D_FF))
W_GU_SHAPE = (2, D_FF, D_MODEL)
W_D_SHAPE = (D_FF, D_MODEL)

ADAM_LR = 0.001
ADAM_B1 = 0.9
ADAM_B2 = 0.999
ADAM_EPS = 1e-08
ADAM_WD = 0.01
ADAM_STEP = 10

MESH = pl.DeviceIdType.MESH
ANY = pl.BlockSpec(memory_space=pl.ANY)


def _params(sem=None):
    return pltpu.CompilerParams(dimension_semantics=sem, vmem_limit_bytes=VMEM_LIMIT)


def _row_tile(n, prefer):
    for t in (prefer, 512, 256, 128):
        if t <= n and n % t == 0:
            return t
    raise ValueError(f"no row tile for {n}")


def _resident(shape):
    zeros = (0,) * len(shape)
    return pl.BlockSpec(shape, lambda i: zeros, pipeline_mode=pl.Buffered(1))


def _dot(a, b):
    return jnp.dot(a, b, preferred_element_type=F32)


def _dot_nt(a, b):
    return lax.dot_general(a, b, (((1,), (1,)), ((), ())), preferred_element_type=F32)


def _dot_tn(a, b):
    return lax.dot_general(a, b, (((0,), (0,)), ((), ())), preferred_element_type=F32)


def _rms(x, g):
    r = lax.rsqrt(jnp.mean(x * x, axis=-1, keepdims=True) + EPS)
    xhat = x * r
    return xhat * g, xhat, r


def _rms_bwd(dn, xhat, r, g):
    dxhat = dn * g
    return r * (dxhat - xhat * jnp.mean(dxhat * xhat, axis=-1, keepdims=True))


def _sigmoid(x):
    return 1.0 / (1.0 + jnp.exp(-x))


def _place():
    return lax.axis_index("x"), lax.axis_index("y"), lax.axis_index("c")


def _comm_sems(nw):
    return [pltpu.SemaphoreType.DMA((nw, 7)), pltpu.SemaphoreType.DMA((nw, 7)), pltpu.SemaphoreType.DMA((nw,))]


def _flip(v, bit):
    return 1 - v if bit else v


class _Gather:
    def __init__(self, ins, outs, sems):
        self.ins, self.outs = ins, outs
        self.send, self.recv, self.local = sems
        x, y, c = _place()
        self.c = c
        self.me, self.sibling = (x, y, c), (x, y, 1 - c)
        first = ((x + 1 - c) % 2, (y + c) % 2)
        second = ((x + c) % 2, (y + 1 - c) % 2)
        self.chips = [first, second, (1 - x, 1 - y)]
        self.targets = [first, second, second]

    def _copy(self, w, k, block, to, own=False):
        slot = self.outs[w].at[4 * block[0] + 2 * block[1] + block[2]]
        return pltpu.make_async_remote_copy(
            src_ref=self.ins[w] if own else slot, dst_ref=slot,
            send_sem=self.send.at[w, k], recv_sem=self.recv.at[w, k], device_id=to, device_id_type=MESH)

    def _mine(self, w):
        x, y, c = self.me
        return pltpu.make_async_copy(self.ins[w], self.outs[w].at[4 * x + 2 * y + c], self.local.at[w])

    def _first(self, w):
        return ([self._copy(w, 0, self.me, self.sibling, own=True)]
                + [self._copy(w, 1 + j, self.me, (*self.targets[j], self.c), own=True) for j in range(2)])

    def _relay(self, w):
        return self._copy(w, 3, (*self.chips[0], self.c), (*self.targets[2], self.c))

    def _landed(self, w, j):
        return self._copy(w, 1 + j, (*self.chips[j], self.c), self.me)

    def _passed(self, w):
        return [self._copy(w, 4 + j, (*chip, self.c), self.sibling) for j, chip in enumerate(self.chips)]

    def start(self):
        for w in range(len(self.ins)):
            self._mine(w).start()
        for w in range(len(self.ins)):
            for cp in self._first(w):
                cp.start()

    def relay(self):
        for w in range(len(self.ins)):
            self._landed(w, 0).wait_recv()
            self._relay(w).start()
            self._passed(w)[0].start()

    def forward(self):
        for w in range(len(self.ins)):
            for j in (1, 2):
                self._landed(w, j).wait_recv()
                self._passed(w)[j].start()

    def finish(self):
        from_sibling = [self.chips[1], self.chips[0], self.chips[2]]
        for w in range(len(self.ins)):
            self._copy(w, 0, self.sibling, self.me).wait_recv()
            for j, chip in enumerate(from_sibling):
                self._copy(w, 4 + j, (*chip, 1 - self.c), self.me).wait_recv()
        for w in range(len(self.ins)):
            for cp in self._first(w) + [self._relay(w)] + self._passed(w):
                cp.wait_send()
            self._mine(w).wait()


class _Exchange:
    def __init__(self, ins, outs, sems):
        self.ins, self.outs = ins, outs
        self.send, self.recv, self.local = sems
        self.x, self.y, self.c = _place()
        self.me = 4 * self.x + 2 * self.y + self.c

    def _copy(self, w, k):
        peer = (_flip(self.x, ((k + 1) >> 2) & 1), _flip(self.y, ((k + 1) >> 1) & 1), _flip(self.c, (k + 1) & 1))
        return pltpu.make_async_remote_copy(
            src_ref=self.ins[w].at[4 * peer[0] + 2 * peer[1] + peer[2]], dst_ref=self.outs[w].at[self.me],
            send_sem=self.send.at[w, k], recv_sem=self.recv.at[w, k], device_id=peer, device_id_type=MESH)

    def _mine(self, w):
        return pltpu.make_async_copy(self.ins[w].at[self.me], self.outs[w].at[self.me], self.local.at[w])

    def start(self):
        for w in range(len(self.ins)):
            self._mine(w).start()
            for k in range(N_DEV - 1):
                self._copy(w, k).start()

    def finish(self):
        for w in range(len(self.ins)):
            for k in range(N_DEV - 1):
                self._copy(w, k).wait()
            self._mine(w).wait()


class _PairExchange:
    def __init__(self, ins, outs, sems):
        self.ins, self.outs = ins, outs
        self.send, self.recv, _ = sems
        x, y, self.c = _place()
        self.sibling = (x, y, 1 - self.c)

    def _copy(self, w, t):
        return pltpu.make_async_remote_copy(
            src_ref=self.ins[w].at[2 * t + 1 - self.c], dst_ref=self.outs[w].at[t],
            send_sem=self.send.at[w, t], recv_sem=self.recv.at[w, t], device_id=self.sibling, device_id_type=MESH)

    def start(self):
        for w in range(len(self.ins)):
            for t in range(4):
                self._copy(w, t).start()

    def finish(self):
        for w in range(len(self.ins)):
            for t in range(4):
                self._copy(w, t).wait()


class _ChipExchange:
    def __init__(self, ins, outs, sems):
        self.ins, self.outs = ins, outs
        self.send, self.recv, self.local = sems
        self.x, self.y, self.c = _place()
        self.chip = 2 * self.x + self.y

    def _copy(self, w, k):
        px, py = _flip(self.x, ((k + 1) >> 1) & 1), _flip(self.y, (k + 1) & 1)
        return pltpu.make_async_remote_copy(
            src_ref=self.ins[w].at[2 * px + py], dst_ref=self.outs[w].at[self.chip],
            send_sem=self.send.at[w, k], recv_sem=self.recv.at[w, k], device_id=(px, py, self.c),
            device_id_type=MESH)

    def _mine(self, w):
        return pltpu.make_async_copy(self.ins[w].at[self.chip], self.outs[w].at[self.chip], self.local.at[w])

    def start(self):
        for w in range(len(self.ins)):
            self._mine(w).start()
            for k in range(3):
                self._copy(w, k).start()

    def finish(self):
        for w in range(len(self.ins)):
            for k in range(3):
                self._copy(w, k).wait()
            self._mine(w).wait()


def _split_refs(refs, n_in, n_comm, n_out, n_scr):
    a = n_in
    b = a + n_comm
    c = b + n_out
    d = c + n_comm
    e = d + n_scr
    return refs[:a], refs[a:b], refs[b:c], refs[c:d], refs[d:e], refs[e:]


def _all_gather(xs, name):
    nw = len(xs)

    def body(*refs):
        comm = _Gather(refs[:nw], refs[nw:2 * nw], refs[2 * nw:])
        comm.start()
        comm.relay()
        comm.forward()
        comm.finish()

    return pl.pallas_call(
        body, name=name, in_specs=[ANY] * nw, out_specs=[ANY] * nw,
        out_shape=[jax.ShapeDtypeStruct((N_DEV,) + a.shape, a.dtype) for a in xs],
        scratch_shapes=_comm_sems(nw),
    )(*xs)


def _pair_exchange(xs, name):
    nw = len(xs)

    def body(*refs):
        comm = _PairExchange(refs[:nw], refs[nw:2 * nw], refs[2 * nw:])
        comm.start()
        comm.finish()

    return pl.pallas_call(
        body, name=name, in_specs=[ANY] * nw, out_specs=[ANY] * nw,
        out_shape=[jax.ShapeDtypeStruct((4,) + a.shape[1:], a.dtype) for a in xs],
        scratch_shapes=_comm_sems(nw),
    )(*xs)


def _pair_sum(own, got, name):
    _, r, c = own.shape
    tr = r
    for t in (256, 128, 64, 32, 16):
        if r % t == 0 and r > t:
            tr = t
            break

    def body(own_ref, got_ref, out_ref):
        mine = jnp.where(lax.axis_index("c") == 0, own_ref[:, 0].astype(F32), own_ref[:, 1].astype(F32))
        out_ref[...] = (mine + got_ref[...].astype(F32)).astype(BF16)

    return pl.pallas_call(
        body, name=name, grid=(r // tr,),
        in_specs=[pl.BlockSpec((4, 2, tr, c), lambda i: (0, 0, i, 0)), pl.BlockSpec((4, tr, c), lambda i: (0, i, 0))],
        out_specs=pl.BlockSpec((4, tr, c), lambda i: (0, i, 0)),
        out_shape=jax.ShapeDtypeStruct((4, r, c), BF16),
        compiler_params=_params(("parallel",)),
    )(own.reshape(4, 2, r, c), got)


def _token_spec(k, ksub, nq):
    def index_map(i):
        s = ksub * i + k
        return ((s // nq) * (nq - 1) + jnp.maximum(s % nq, 1) - 1, 0)
    return pl.BlockSpec((128, D_MODEL), index_map)


def _is_lead(i, k, ksub, nq):
    return ((ksub * i + k) % nq) == 0


def _assemble_rows(i, x_refs, meta_ref, nq):
    ksub = len(x_refs)
    lead = jnp.concatenate([jnp.zeros((PAD, D_MODEL), F32), meta_ref[...]], axis=0)
    return jnp.concatenate([jnp.where(_is_lead(i, k, ksub, nq), lead, x_refs[k][...]) for k in range(ksub)], axis=0)


def _swiglu(nb, wgu_ref, wd_ref, gate_ref, up_ref):
    acc = jnp.zeros((nb.shape[0], D_MODEL), F32)
    for a, b in HID_PIECES:
        gate = _dot_nt(nb, wgu_ref[0, a:b, :])
        up = _dot_nt(nb, wgu_ref[1, a:b, :])
        gate_ref[:, a:b] = gate.astype(BF16)
        up_ref[:, a:b] = up.astype(BF16)
        acc = acc + _dot((gate * _sigmoid(gate) * up).astype(BF16), wd_ref[a:b, :])
    return acc


def _ffn1_fwd(x2d, meta, lp, gain, wgu, wd, gather):
    nq = lp // 128
    n = (x2d.shape[0] // (nq - 1)) * nq
    tm = _row_tile(n, 512)
    ksub = tm // 128
    n_i = n // tm
    nw = len(gather)

    def body(*refs):
        x_refs = refs[:ksub]
        (meta_ref, g_ref, wgu_ref, wd_ref), gin, (out_ref, nrm_ref, gate_ref, up_ref), gout, _, sems = \
            _split_refs(refs[ksub:], 4, nw, 4, 0)
        i = pl.program_id(0)
        comm = _Gather(gin, gout, sems)
        pl.when(i == 0)(comm.start)
        pl.when(i == n_i // 2)(comm.relay)
        pl.when(i == max(n_i - 3, n_i // 2))(comm.forward)

        hv = _assemble_rows(i, x_refs, meta_ref, nq)
        y, _, _ = _rms(hv, g_ref[...])
        nb = y.astype(BF16)
        nrm_ref[...] = nb
        out_ref[...] = hv + 0.5 * _swiglu(nb, wgu_ref, wd_ref, gate_ref, up_ref)

        pl.when(i == n_i - 1)(comm.finish)

    rows = pl.BlockSpec((tm, D_MODEL), lambda i: (i, 0))
    hid = pl.BlockSpec((tm, D_FF), lambda i: (i, 0))
    return pl.pallas_call(
        body, name="ffn1_fwd", grid=(n_i,),
        in_specs=[_token_spec(k, ksub, nq) for k in range(ksub)]
        + [pl.BlockSpec((N_META, D_MODEL), lambda i: (0, 0)), pl.BlockSpec((1, D_MODEL), lambda i: (0, 0)),
           _resident(W_GU_SHAPE), _resident(W_D_SHAPE)] + [ANY] * nw,
        out_specs=[rows, rows, hid, hid] + [ANY] * nw,
        out_shape=[jax.ShapeDtypeStruct((n, D_MODEL), F32), jax.ShapeDtypeStruct((n, D_MODEL), BF16),
                   jax.ShapeDtypeStruct((n, D_FF), BF16), jax.ShapeDtypeStruct((n, D_FF), BF16)]
        + [jax.ShapeDtypeStruct((N_DEV,) + a.shape, a.dtype) for a in gather],
        scratch_shapes=_comm_sems(nw),
        compiler_params=_params(("arbitrary",)),
    )(*([x2d] * ksub), meta, gain, wgu, wd, *gather)


def _ffn2_fwd_loss(h, gain, wgu, wd, gfinal, target, lp):
    n = h.shape[0]
    nq = lp // 128
    tm = _row_tile(n, 512)
    ksub = tm // 128
    n_i = n // tm

    def body(*refs):
        t_refs = refs[:ksub]
        h_ref, g_ref, wgu_ref, wd_ref, gf_ref, dh_ref, nrm_ref, gate_ref, up_ref, loss_ref, dgf_ref = refs[ksub:]
        i = pl.program_id(0)

        @pl.when(i == 0)
        def _():
            loss_ref[...] = jnp.zeros_like(loss_ref)
            dgf_ref[...] = jnp.zeros_like(dgf_ref)

        hv = h_ref[...]
        y, _, _ = _rms(hv, g_ref[...])
        nb = y.astype(BF16)
        nrm_ref[...] = nb
        hout = hv + 0.5 * _swiglu(nb, wgu_ref, wd_ref, gate_ref, up_ref)

        gf = gf_ref[...]
        loss = jnp.zeros((1, 1), F32)
        dgf = jnp.zeros((1, D_MODEL), F32)
        for k in range(ksub):
            yk, xhat, r = _rms(hout[128 * k:128 * (k + 1)], gf)
            err = jnp.where(_is_lead(i, k, ksub, nq), 0.0, yk - t_refs[k][...])
            loss = loss + 0.5 * jnp.sum(jnp.sum(err * err, axis=1, keepdims=True), axis=0,
                                        keepdims=True) * (1.0 / D_MODEL)
            dy = err * (1.0 / D_MODEL)
            dh_ref[128 * k:128 * (k + 1), :] = _rms_bwd(dy, xhat, r, gf)
            dgf = dgf + jnp.sum(dy * xhat, axis=0, keepdims=True)
        loss_ref[...] += loss
        dgf_ref[...] += dgf

    rows = pl.BlockSpec((tm, D_MODEL), lambda i: (i, 0))
    hid = pl.BlockSpec((tm, D_FF), lambda i: (i, 0))
    vec = pl.BlockSpec((1, D_MODEL), lambda i: (0, 0))
    return pl.pallas_call(
        body, name="ffn2_fwd_loss", grid=(n_i,),
        in_specs=[_token_spec(k, ksub, nq) for k in range(ksub)]
        + [rows, vec, _resident(W_GU_SHAPE), _resident(W_D_SHAPE), vec],
        out_specs=[rows, rows, hid, hid, pl.BlockSpec((1, 1), lambda i: (0, 0)), vec],
        out_shape=[jax.ShapeDtypeStruct((n, D_MODEL), F32), jax.ShapeDtypeStruct((n, D_MODEL), BF16),
                   jax.ShapeDtypeStruct((n, D_FF), BF16), jax.ShapeDtypeStruct((n, D_FF), BF16),
                   jax.ShapeDtypeStruct((1, 1), F32), jax.ShapeDtypeStruct((1, D_MODEL), F32)],
        compiler_params=_params(("arbitrary",)),
    )(*([target] * ksub), h, gain, wgu, wd, gfinal)


def _ffn_bwd_act_wd(dh_out, gate, up, wd, name, exchange=()):
    n = dh_out.shape[0]
    tm = _row_tile(n, 256)
    n_i = n // tm
    nw = len(exchange)

    def body(*refs):
        (dh_ref, gate_ref, up_ref, wd_ref), xin, (dgu_ref, dw_ref), xout, (acc_scr,), sems = \
            _split_refs(refs, 4, nw, 2, 1)
        i = pl.program_id(0)
        if nw:
            comm = _Exchange(xin, xout, sems)
            pl.when(i == 0)(comm.start)

        @pl.when(i == 0)
        def _():
            acc_scr[...] = jnp.zeros_like(acc_scr)

        dhb = (0.5 * dh_ref[...]).astype(BF16)
        for a, b in HID_PIECES:
            da = _dot_nt(dhb, wd_ref[a:b, :])
            g = gate_ref[:, a:b].astype(F32)
            u = up_ref[:, a:b].astype(F32)
            sig = _sigmoid(g)
            silu = g * sig
            dgu_ref[:, a:b] = (da * u * (sig * (1.0 + g * (1.0 - sig)))).astype(BF16)
            dgu_ref[:, D_FF + a:D_FF + b] = (da * silu).astype(BF16)
            acc_scr[a:b, :] += _dot_tn((silu * u).astype(BF16), dhb)

        @pl.when(i == n_i - 1)
        def _():
            dw_ref[...] = acc_scr[...].astype(BF16)

        if nw:
            pl.when(i == n_i - 1)(comm.finish)

    rows = pl.BlockSpec((tm, D_MODEL), lambda i: (i, 0))
    hid = pl.BlockSpec((tm, D_FF), lambda i: (i, 0))
    return pl.pallas_call(
        body, name=name, grid=(n_i,),
        in_specs=[rows, hid, hid, _resident(W_D_SHAPE)] + [ANY] * nw,
        out_specs=[pl.BlockSpec((tm, 2 * D_FF), lambda i: (i, 0)), _resident(W_D_SHAPE)] + [ANY] * nw,
        out_shape=[jax.ShapeDtypeStruct((n, 2 * D_FF), BF16), jax.ShapeDtypeStruct(W_D_SHAPE, BF16)]
        + [jax.ShapeDtypeStruct(a.shape, a.dtype) for a in exchange],
        scratch_shapes=[pltpu.VMEM(W_D_SHAPE, F32)] + (_comm_sems(nw) if nw else []),
        compiler_params=_params(("arbitrary",)),
    )(dh_out, gate, up, wd, *exchange)


def _ffn_bwd_in(dh_out, h_in, gain, dgu, wgu, name, tokens=None, exchange=()):
    n = dh_out.shape[0]
    tm = _row_tile(n, 512)
    n_i = n // tm
    nw = len(exchange)
    ksub, nq = (tm // 128, tokens[2] // 128) if tokens else (1, 0)

    def body(*refs):
        h_refs = refs[:ksub]
        (meta_ref, dh_ref, g_ref, dgu_ref, wgu_ref), xin, (dhin_ref, dgain_ref), xout, _, sems = \
            _split_refs(refs[ksub:], 5, nw, 2, 0)
        i = pl.program_id(0)
        if nw:
            comm = _ChipExchange(xin, xout, sems)
            pl.when(i == 0)(comm.start)

        @pl.when(i == 0)
        def _():
            dgain_ref[...] = jnp.zeros_like(dgain_ref)

        dn = _dot(dgu_ref[...], wgu_ref[...])
        gain_v = g_ref[...]
        hv = _assemble_rows(i, h_refs, meta_ref, nq) if tokens else h_refs[0][...]
        _, xhat, r = _rms(hv, gain_v)
        dhin_ref[...] = dh_ref[...] + _rms_bwd(dn, xhat, r, gain_v)
        dgain_ref[...] += jnp.sum(dn * xhat, axis=0, keepdims=True)

        if nw:
            pl.when(i == n_i - 1)(comm.finish)

    rows = pl.BlockSpec((tm, D_MODEL), lambda i: (i, 0))
    hid = pl.BlockSpec((tm, D_FF), lambda i: (i, 0))
    vec = pl.BlockSpec((1, D_MODEL), lambda i: (0, 0))
    meta_spec = pl.BlockSpec((N_META, D_MODEL), lambda i: (0, 0))
    if tokens:
        h_specs, h_args, meta = [_token_spec(k, ksub, nq) for k in range(ksub)], [tokens[0]] * ksub, tokens[1]
    else:
        h_specs, h_args, meta = [rows], [h_in], jnp.zeros((N_META, D_MODEL), F32)
    return pl.pallas_call(
        body, name=name, grid=(n_i,),
        in_specs=h_specs + [meta_spec, rows, vec, pl.BlockSpec((tm, 2 * D_FF), lambda i: (i, 0)),
                            _resident((2 * D_FF, D_MODEL))] + [ANY] * nw,
        out_specs=[rows, vec] + [ANY] * nw,
        out_shape=[jax.ShapeDtypeStruct((n, D_MODEL), F32), jax.ShapeDtypeStruct((1, D_MODEL), F32)]
        + [jax.ShapeDtypeStruct(a.shape, a.dtype) for a in exchange],
        scratch_shapes=_comm_sems(nw) if nw else [],
        compiler_params=_params(("arbitrary",)),
    )(*h_args, meta, dh_out, gain, dgu, wgu.reshape(2 * D_FF, D_MODEL), *exchange)


def _ffn_bwd_wgu(nrm, dgu, name, exchange=()):
    n = nrm.shape[0]
    tm = _row_tile(n, 512)
    n_i = n // tm
    nw = len(exchange)

    def body(*refs):
        (nrm_ref, dgu_ref), xin, (dw_ref,), xout, (acc_scr,), sems = _split_refs(refs, 2, nw, 1, 1)
        i = pl.program_id(0)
        if nw:
            comm = _Exchange(xin, xout, sems)
            pl.when(i == 0)(comm.start)

        @pl.when(i == 0)
        def _():
            acc_scr[...] = jnp.zeros_like(acc_scr)

        nb = nrm_ref[...]
        for half in (0, D_FF):
            for a, b in HID_PIECES:
                acc_scr[half + a:half + b, :] += _dot_tn(dgu_ref[:, half + a:half + b], nb)

        @pl.when(i == n_i - 1)
        def _():
            dw_ref[...] = acc_scr[...].astype(BF16)

        if nw:
            pl.when(i == n_i - 1)(comm.finish)

    shape = (2 * D_FF, D_MODEL)
    res = pl.pallas_call(
        body, name=name, grid=(n_i,),
        in_specs=[pl.BlockSpec((tm, D_MODEL), lambda i: (i, 0)),
                  pl.BlockSpec((tm, 2 * D_FF), lambda i: (i, 0))] + [ANY] * nw,
        out_specs=[_resident(shape)] + [ANY] * nw,
        out_shape=[jax.ShapeDtypeStruct(shape, BF16)] + [jax.ShapeDtypeStruct(a.shape, a.dtype) for a in exchange],
        scratch_shapes=[pltpu.VMEM(shape, F32)] + (_comm_sems(nw) if nw else []),
        compiler_params=_params(("arbitrary",)),
    )(nrm, dgu, *exchange)
    return res if nw else res[0]


def _inproj_fwd(h, gain, w_in):
    n = h.shape[0]
    tm = _row_tile(n, 512)

    def body(h_ref, g_ref, w_ref, *outs):
        y, _, _ = _rms(h_ref[...], g_ref[...])
        nb = y.astype(BF16)
        for p in range(N_PIECE):
            outs[p][...] = _dot_nt(nb, w_ref[512 * p:512 * (p + 1), :]).astype(BF16)
        outs[N_PIECE][...] = _dot_nt(nb, w_ref[IN_MAIN:IN_PAD, :])

    piece = pl.BlockSpec((tm, 512), lambda i: (i, 0))
    return pl.pallas_call(
        body, name="inproj_fwd", grid=(n // tm,),
        in_specs=[pl.BlockSpec((tm, D_MODEL), lambda i: (i, 0)),
                  pl.BlockSpec((1, D_MODEL), lambda i: (0, 0)),
                  pl.BlockSpec((IN_PAD, D_MODEL), lambda i: (0, 0))],
        out_specs=[piece] * N_PIECE + [pl.BlockSpec((tm, 128), lambda i: (i, 0))],
        out_shape=[jax.ShapeDtypeStruct((n, 512), BF16)] * N_PIECE + [jax.ShapeDtypeStruct((n, 128), F32)],
        compiler_params=_params(("parallel",)),
    )(h, gain, w_in)


def _inproj_bwd(dpieces, dfg, dh_out, h_in, gain, w_in):
    n = h_in.shape[0]
    tm = _row_tile(n, 512)
    n_i = n // tm

    def body(*refs):
        dp_refs = refs[:N_PIECE]
        dfg_ref, dh_ref, h_ref, g_ref, w_ref, dhin_ref, dw_ref, dgain_ref, acc_scr = refs[N_PIECE:]
        i = pl.program_id(0)

        @pl.when(i == 0)
        def _():
            acc_scr[...] = jnp.zeros_like(acc_scr)
            dgain_ref[...] = jnp.zeros_like(dgain_ref)

        gain_v = g_ref[...]
        y, xhat, r = _rms(h_ref[...], gain_v)
        nb = y.astype(BF16)
        dn = jnp.zeros((tm, D_MODEL), F32)
        for p in range(N_PIECE + 1):
            lo, hi = (512 * p, 512 * (p + 1)) if p < N_PIECE else (IN_MAIN, IN_PAD)
            dp = (dp_refs[p][...] if p < N_PIECE else dfg_ref[...]).astype(BF16)
            dn = dn + _dot(dp, w_ref[lo:hi, :])
            acc_scr[lo:hi, :] += _dot_tn(dp, nb)
        dhin_ref[...] = dh_ref[...] + _rms_bwd(dn, xhat, r, gain_v)
        dgain_ref[...] += jnp.sum(dn * xhat, axis=0, keepdims=True)

        @pl.when(i == n_i - 1)
        def _():
            dw_ref[...] = acc_scr[...].astype(BF16)

    piece = pl.BlockSpec((tm, 512), lambda i: (i, 0))
    rows = pl.BlockSpec((tm, D_MODEL), lambda i: (i, 0))
    vec = pl.BlockSpec((1, D_MODEL), lambda i: (0, 0))
    wspec = pl.BlockSpec((IN_PAD, D_MODEL), lambda i: (0, 0))
    return pl.pallas_call(
        body, name="inproj_bwd", grid=(n_i,),
        in_specs=[piece] * N_PIECE + [pl.BlockSpec((tm, 128), lambda i: (i, 0)), rows, rows, vec, wspec],
        out_specs=[rows, wspec, vec],
        out_shape=[jax.ShapeDtypeStruct((n, D_MODEL), F32),
                   jax.ShapeDtypeStruct((IN_PAD, D_MODEL), BF16),
                   jax.ShapeDtypeStruct((1, D_MODEL), F32)],
        scratch_shapes=[pltpu.VMEM((IN_PAD, D_MODEL), F32)],
        compiler_params=_params(("arbitrary",)),
    )(*dpieces, dfg, dh_out, h_in, gain, w_in)


def _outproj_bwd(dh, zc, za, w_out):
    n = dh.shape[0]
    tm = _row_tile(n, 512)
    n_i = n // tm

    def body(dh_ref, zc_ref, za_ref, w_ref, dzc_ref, dza_ref, dw_ref, acc_scr):
        i = pl.program_id(0)

        @pl.when(i == 0)
        def _():
            acc_scr[...] = jnp.zeros_like(acc_scr)

        dhb = dh_ref[...].astype(BF16)
        dzc_ref[...] = _dot_nt(dhb, w_ref[0:CONV_DIM, :]).astype(BF16)
        dza_ref[...] = _dot_nt(dhb, w_ref[CONV_DIM:, :]).astype(BF16)
        acc_scr[0:CONV_DIM, :] += _dot_tn(zc_ref[...], dhb)
        acc_scr[CONV_DIM:, :] += _dot_tn(za_ref[...], dhb)

        @pl.when(i == n_i - 1)
        def _():
            dw_ref[...] = acc_scr[...].astype(BF16)

    half = pl.BlockSpec((tm, 512), lambda i: (i, 0))
    wspec = pl.BlockSpec((D_MODEL, D_MODEL), lambda i: (0, 0))
    return pl.pallas_call(
        body, name="outproj_bwd", grid=(n_i,),
        in_specs=[pl.BlockSpec((tm, D_MODEL), lambda i: (i, 0)), half, half, wspec],
        out_specs=[half, half, wspec],
        out_shape=[jax.ShapeDtypeStruct((n, 512), BF16), jax.ShapeDtypeStruct((n, 512), BF16),
                   jax.ShapeDtypeStruct((D_MODEL, D_MODEL), BF16)],
        scratch_shapes=[pltpu.VMEM((D_MODEL, D_MODEL), F32)],
        compiler_params=_params(("arbitrary",)),
    )(dh, zc, za, w_out)


def _group_matrix():
    r = lax.broadcasted_iota(jnp.int32, (128, 128), 0) // HEAD_DIM
    c = lax.broadcasted_iota(jnp.int32, (128, 128), 1) // HEAD_DIM
    return jnp.where(r == c, 1.0 / HEAD_DIM, 0.0).astype(BF16)


def _group_mean(x, gmat):
    hi = x.astype(BF16)
    lo = (x - hi.astype(F32)).astype(BF16)
    return _dot(hi, gmat) + _dot(lo, gmat)


def _shift_rows(x, s):
    rows = x.shape[0]
    t = lax.broadcasted_iota(jnp.int32, x.shape, 0)
    rolled = pltpu.roll(x, s % rows, 0)
    keep = (t >= s) if s > 0 else (t < rows + s)
    return jnp.where(keep, rolled, 0.0)


def _conv_parts(bg_ref, cg_ref, hc_ref, w_ref):
    bg = bg_ref[...].astype(F32)
    cg = cg_ref[...].astype(F32)
    hc = hc_ref[...].astype(F32)
    u = cg * hc
    u1 = _shift_rows(u, 1)
    u2 = _shift_rows(u, 2)
    conv = w_ref[2:3, :] * u + w_ref[1:2, :] * u1 + w_ref[0:1, :] * u2
    return bg, cg, hc, u, u1, u2, conv


def _conv_fwd(bg, cg, hc, conv_w, gain, gmat, lp):
    n = bg.shape[0]
    nb = n // lp

    def body(bg_ref, cg_ref, hc_ref, w_ref, g_ref, gm_ref, z_ref):
        bgv, _, _, _, _, _, conv = _conv_parts(bg_ref, cg_ref, hc_ref, w_ref)
        yc = bgv * conv
        r = lax.rsqrt(_group_mean(yc * yc, gm_ref[...]) + EPS)
        z_ref[...] = (yc * r * g_ref[...]).astype(BF16)

    blk = pl.BlockSpec((lp, 128), lambda c, b: (b, c))
    return pl.pallas_call(
        body, name="conv_fwd", grid=(CONV_DIM // 128, nb),
        in_specs=[blk, blk, blk, pl.BlockSpec((3, 128), lambda c, b: (0, c)),
                  pl.BlockSpec((1, 128), lambda c, b: (0, c)), pl.BlockSpec((128, 128), lambda c, b: (0, 0))],
        out_specs=blk,
        out_shape=jax.ShapeDtypeStruct((n, CONV_DIM), BF16),
        compiler_params=_params(("parallel", "parallel")),
    )(bg, cg, hc, conv_w, gain, gmat)


def _conv_bwd(dz, bg, cg, hc, conv_w, gain, gmat, lp):
    n = bg.shape[0]
    nb = n // lp

    def body(dz_ref, bg_ref, cg_ref, hc_ref, w_ref, g_ref, gm_ref,
             dbg_ref, dcg_ref, dhc_ref, dw_ref, dgain_ref):
        b = pl.program_id(1)

        @pl.when(b == 0)
        def _():
            dw_ref[...] = jnp.zeros_like(dw_ref)
            dgain_ref[...] = jnp.zeros_like(dgain_ref)

        bgv, cgv, hcv, u, u1, u2, conv = _conv_parts(bg_ref, cg_ref, hc_ref, w_ref)
        gm = gm_ref[...]
        yc = bgv * conv
        r = lax.rsqrt(_group_mean(yc * yc, gm) + EPS)
        yhat = yc * r
        dzv = dz_ref[...].astype(F32)
        dyhat = dzv * g_ref[...]
        dgain_ref[...] += jnp.sum(dzv * yhat, axis=0, keepdims=True)
        dyc = r * (dyhat - yhat * _group_mean(dyhat * yhat, gm))
        dbg_ref[...] = (dyc * conv).astype(BF16)
        dconv = dyc * bgv
        du = (w_ref[2:3, :] * dconv + w_ref[1:2, :] * _shift_rows(dconv, -1)
              + w_ref[0:1, :] * _shift_rows(dconv, -2))
        dcg_ref[...] = (du * hcv).astype(BF16)
        dhc_ref[...] = (du * cgv).astype(BF16)
        dw_ref[0:1, :] += jnp.sum(dconv * u2, axis=0, keepdims=True)
        dw_ref[1:2, :] += jnp.sum(dconv * u1, axis=0, keepdims=True)
        dw_ref[2:3, :] += jnp.sum(dconv * u, axis=0, keepdims=True)

    blk = pl.BlockSpec((lp, 128), lambda c, b: (b, c))
    wspec = pl.BlockSpec((3, 128), lambda c, b: (0, c))
    gspec = pl.BlockSpec((1, 128), lambda c, b: (0, c))
    return pl.pallas_call(
        body, name="conv_bwd", grid=(CONV_DIM // 128, nb),
        in_specs=[blk, blk, blk, blk, wspec, gspec, pl.BlockSpec((128, 128), lambda c, b: (0, 0))],
        out_specs=[blk, blk, blk, wspec, gspec],
        out_shape=[jax.ShapeDtypeStruct((n, CONV_DIM), BF16)] * 3
        + [jax.ShapeDtypeStruct((3, CONV_DIM), F32), jax.ShapeDtypeStruct((1, CONV_DIM), F32)],
        compiler_params=_params(("parallel", "arbitrary")),
    )(dz, bg, cg, hc, conv_w, gain, gmat)


KEY_MASKED = 1e30
ONE_LANE = 24


def _scan_steps(rows):
    s, out = 1, []
    while s < rows:
        out.append(s)
        s *= 2
    return out


def _fgate_fwd(fg, b_f, lp):
    n = fg.shape[0]
    nb = n // lp

    def body(fg_ref, b_ref, ka_ref, qa_ref):
        x = fg_ref[...] + b_ref[...]
        logf = jnp.minimum(x, 0.0) - jnp.log(1.0 + jnp.exp(-jnp.abs(x)))
        t = lax.broadcasted_iota(jnp.int32, (lp, 128), 0)
        lane = lax.broadcasted_iota(jnp.int32, (lp, 128), 1)
        f = jnp.where((t >= PAD) & (lane < N_HEADS), logf, 0.0)
        for s in _scan_steps(lp):
            f = f + _shift_rows(f, s)
        hi = f.astype(BF16).astype(F32)
        rest = f - hi
        mid = rest.astype(BF16).astype(F32)
        lo = (rest - mid).astype(BF16).astype(F32)
        ones = jnp.where((lane >= ONE_LANE) & (lane < ONE_LANE + 3), 1.0, 0.0)
        hi_key = jnp.where((t < PAD) & (lane < N_HEADS), KEY_MASKED, hi)
        ka_ref[...] = (hi_key + pltpu.roll(mid, 8, 1) + pltpu.roll(lo, 16, 1) + ones).astype(BF16)
        for h in range(N_HEADS):
            minus = jnp.where((lane == h) | (lane == 8 + h) | (lane == 16 + h), -1.0, 0.0)
            terms = (jnp.where(lane == ONE_LANE, pltpu.roll(hi, ONE_LANE - h, 1), 0.0)
                     + jnp.where(lane == ONE_LANE + 1, pltpu.roll(mid, ONE_LANE + 1 - h, 1), 0.0)
                     + jnp.where(lane == ONE_LANE + 2, pltpu.roll(lo, ONE_LANE + 2 - h, 1), 0.0))
            qa_ref[:, 128 * h:128 * (h + 1)] = (minus + terms).astype(BF16)

    return pl.pallas_call(
        body, name="fgate_fwd", grid=(nb,),
        in_specs=[pl.BlockSpec((lp, 128), lambda b: (b, 0)), pl.BlockSpec((1, 128), lambda b: (0, 0))],
        out_specs=[pl.BlockSpec((lp, 128), lambda b: (b, 0)), pl.BlockSpec((lp, N_HEADS * 128), lambda b: (b, 0))],
        out_shape=[jax.ShapeDtypeStruct((n, 128), BF16), jax.ShapeDtypeStruct((n, N_HEADS * 128), BF16)],
        compiler_params=_params(("parallel",)),
    )(fg, b_f)


def _fgate_bwd(dka, dfr, fg, b_f, lp):
    n = fg.shape[0]
    nb = n // lp

    def body(dka_ref, dfr_ref, fg_ref, b_ref, dfg_ref, db_ref):
        b = pl.program_id(0)

        @pl.when(b == 0)
        def _():
            db_ref[...] = jnp.zeros_like(db_ref)

        wide = jnp.concatenate([dfr_ref[0], jnp.zeros((128 - N_HEADS, lp), F32)], axis=0)
        t = lax.broadcasted_iota(jnp.int32, (lp, 128), 0)
        lane = lax.broadcasted_iota(jnp.int32, (lp, 128), 1)
        d = jnp.where(lane < N_HEADS, dka_ref[...], 0.0) + wide.T
        for s in _scan_steps(lp):
            d = d + _shift_rows(d, -s)
        x = fg_ref[...] + b_ref[...]
        dx = jnp.where((t >= PAD) & (lane < N_HEADS), d * _sigmoid(-x), 0.0)
        dfg_ref[...] = dx
        db_ref[...] += jnp.sum(dx, axis=0, keepdims=True)

    return pl.pallas_call(
        body, name="fgate_bwd", grid=(nb,),
        in_specs=[pl.BlockSpec((lp, 128), lambda b: (b, 0)), pl.BlockSpec((1, N_HEADS, lp), lambda b: (b, 0, 0)),
                  pl.BlockSpec((lp, 128), lambda b: (b, 0)), pl.BlockSpec((1, 128), lambda b: (0, 0))],
        out_specs=[pl.BlockSpec((lp, 128), lambda b: (b, 0)), pl.BlockSpec((1, 128), lambda b: (0, 0))],
        out_shape=[jax.ShapeDtypeStruct((n, 128), F32), jax.ShapeDtypeStruct((1, 128), F32)],
        compiler_params=_params(("arbitrary",)),
    )(dka, dfr, fg, b_f)


def _head_masks():
    lane = lax.broadcasted_iota(jnp.int32, (1, 128), 1)
    return lane < HEAD_DIM


def _stack_heads(x2, first):
    zero = jnp.zeros_like(x2)
    return jnp.concatenate([jnp.where(first, x2, zero), jnp.where(first, zero, x2)], axis=0)


def _stack_heads_lanes(xt):
    r = lax.broadcasted_iota(jnp.int32, xt.shape, 0)
    zero = jnp.zeros_like(xt)
    return jnp.concatenate([jnp.where(r < HEAD_DIM, xt, zero), jnp.where(r < HEAD_DIM, zero, xt)], axis=1)


def _pair_cols(col0, col1, first):
    return jnp.where(first, col0, col1)


def _pair_rows(row0, row1):
    r = lax.broadcasted_iota(jnp.int32, (128, TQ), 0)
    return jnp.where(r < HEAD_DIM, row0, row1)


def _query_side(q_ref, qa_ref, p, first):
    q2 = q_ref[:, 128 * p:128 * (p + 1)] * 0.125
    zero = jnp.zeros_like(q2)
    top = jnp.concatenate([jnp.where(first, q2, zero), qa_ref[:, 128 * (2 * p):128 * (2 * p + 1)]], axis=1)
    bot = jnp.concatenate([jnp.where(first, zero, q2), qa_ref[:, 128 * (2 * p + 1):128 * (2 * p + 2)]], axis=1)
    return jnp.concatenate([top, bot], axis=0)


def _key_chunks(lp):
    return (lp + TK - 1) // TK


def _chunk_mask(i, c, tk):
    r = lax.broadcasted_iota(jnp.int32, (tk, 2 * TQ), 0)
    col = lax.broadcasted_iota(jnp.int32, (tk, 2 * TQ), 1)
    return (c * TK + r) <= (i * TQ + (col & (TQ - 1)))


def _causal_sweep(i, step, init):
    per = TK // TQ
    last = i // per
    carry = lax.fori_loop(0, last, lambda c, carry: step(c, carry, False, TK), init)
    tails = [lambda carry, r=r: step(last, carry, True, TQ * (r + 1)) for r in range(per)]
    return lax.switch(i % per, tails, carry)


def _transpose_bf16(x):
    return x.astype(F32).T.astype(BF16)


def _attn_fwd(q, qa, k, v, ka, gain, zc, w_out, h, lp):
    n = q.shape[0]
    nb = n // lp
    nq = lp // TQ
    lpp = _key_chunks(lp) * TK

    def body(q_ref, qa_ref, k_ref, v_ref, ka_ref, g_ref, zc_ref, w_ref, h_ref,
             z_ref, o_ref, lse_ref, hout_ref, kx_scr, vt_scr):
        i = pl.program_id(1)
        first = _head_masks()

        @pl.when(i == 0)
        def _():
            if lpp > lp:
                kx_scr[lp:lpp, :] = jnp.zeros((lpp - lp, 2 * ATTN_DIM), BF16)
                vt_scr[:, lp:lpp] = jnp.zeros((ATTN_DIM, lpp - lp), BF16)
            for p in range(N_PAIRS):
                kx_scr[0:lp, 256 * p:256 * p + 128] = k_ref[:, 128 * p:128 * (p + 1)]
                kx_scr[0:lp, 256 * p + 128:256 * (p + 1)] = ka_ref[...]
            vt_scr[:, 0:lp] = _transpose_bf16(v_ref[...])

        rhs_t = [_transpose_bf16(_query_side(q_ref, qa_ref, p, first)) for p in range(N_PAIRS)]

        def step(c, carry, masked, tk):
            koff = pl.multiple_of(c * TK, TK)
            valid = _chunk_mask(i, c, tk) if masked else None
            new = []
            for p in range(N_PAIRS):
                m, l, acc = carry[p]
                st = _dot(kx_scr[pl.ds(koff, tk), 256 * p:256 * (p + 1)], rhs_t[p])
                if masked:
                    st = jnp.where(valid, st, NEG)
                m_new = jnp.maximum(m, jnp.max(st, axis=0, keepdims=True))
                pt = jnp.exp(st - m_new)
                alpha = jnp.exp(m - m_new)
                l = alpha * l + jnp.sum(pt, axis=0, keepdims=True)
                pb = pt.astype(BF16)
                vt = _stack_heads_lanes(vt_scr[128 * p:128 * (p + 1), pl.ds(koff, tk)])
                pv = _dot(vt, jnp.concatenate([pb[:, 0:TQ], pb[:, TQ:]], axis=0))
                acc = acc * _pair_rows(alpha[:, 0:TQ], alpha[:, TQ:]) + pv
                new.append((m_new, l, acc))
            return tuple(new)

        init = tuple((jnp.full((1, 2 * TQ), NEG, F32), jnp.zeros((1, 2 * TQ), F32), jnp.zeros((128, TQ), F32))
                     for _ in range(N_PAIRS))
        final = _causal_sweep(i, step, init)

        row = lax.broadcasted_iota(jnp.int32, (TQ, 128), 0)
        real = (i * TQ + row) >= PAD
        hout = h_ref[...] + _dot(zc_ref[...], w_ref[0:CONV_DIM, :])
        for p in range(N_PAIRS):
            m, l, acc = final[p]
            inv = 1.0 / l
            ot = acc * _pair_rows(inv[:, 0:TQ], inv[:, TQ:])
            sq = ot * ot
            r0 = lax.rsqrt(jnp.sum(sq[0:HEAD_DIM], axis=0, keepdims=True) * (1.0 / HEAD_DIM) + EPS)
            r1 = lax.rsqrt(jnp.sum(sq[HEAD_DIM:], axis=0, keepdims=True) * (1.0 / HEAD_DIM) + EPS)
            cols = slice(128 * p, 128 * (p + 1))
            o_ref[:, cols] = jnp.where(real, ot.T, 0.0).astype(BF16)
            z = (jnp.where(real, (ot * _pair_rows(r0, r1)).T, 0.0) * g_ref[:, cols]).astype(BF16)
            z_ref[:, cols] = z
            hout = hout + _dot(z, w_ref[CONV_DIM + 128 * p:CONV_DIM + 128 * (p + 1), :])
            lse = m + jnp.log(l)
            lse_ref[0, 2 * p:2 * p + 1, :] = lse[:, 0:TQ]
            lse_ref[0, 2 * p + 1:2 * p + 2, :] = lse[:, TQ:]
        hout_ref[...] = hout

    qblk = pl.BlockSpec((TQ, ATTN_DIM), lambda b, i: (b * nq + i, 0))
    qablk = pl.BlockSpec((TQ, N_HEADS * 128), lambda b, i: (b * nq + i, 0))
    seq = pl.BlockSpec((lp, ATTN_DIM), lambda b, i: (b, 0))
    rowblk = pl.BlockSpec((1, N_HEADS, TQ), lambda b, i: (b, 0, i))
    hblk = pl.BlockSpec((TQ, D_MODEL), lambda b, i: (b * nq + i, 0))
    return pl.pallas_call(
        body, name="attn_fwd", grid=(nb, nq),
        in_specs=[qblk, qablk, seq, seq, pl.BlockSpec((lp, 128), lambda b, i: (b, 0)),
                  pl.BlockSpec((1, ATTN_DIM), lambda b, i: (0, 0)), qblk,
                  pl.BlockSpec((D_MODEL, D_MODEL), lambda b, i: (0, 0)), hblk],
        out_specs=[qblk, qblk, rowblk, hblk],
        out_shape=[jax.ShapeDtypeStruct((n, ATTN_DIM), BF16), jax.ShapeDtypeStruct((n, ATTN_DIM), BF16),
                   jax.ShapeDtypeStruct((nb, N_HEADS, lp), F32), jax.ShapeDtypeStruct((n, D_MODEL), F32)],
        scratch_shapes=[pltpu.VMEM((lpp, 2 * ATTN_DIM), BF16), pltpu.VMEM((ATTN_DIM, lpp), BF16)],
        compiler_params=_params(("parallel", "arbitrary")),
    )(q, qa, k, v, ka, gain, zc, w_out, h)


def _attn_bwd(dz, q, qa, k, v, ka, o, lse, gain, lp, exchange=()):
    n = q.shape[0]
    nb = n // lp
    nq = lp // TQ
    lpp = _key_chunks(lp) * TK
    nw = len(exchange)

    def body(*refs):
        ((dz_ref, q_ref, qa_ref, k_ref, v_ref, ka_ref, o_ref, lse_ref, g_ref), xin,
         (dq_ref, dk_ref, dv_ref, dka_ref, dfr_ref, dgain_ref), xout,
         (kx_scr, vx_scr, kt_scr, dkx_scr, dvx_scr), sems) = _split_refs(refs, 9, nw, 6, 5)
        b = pl.program_id(0)
        i = pl.program_id(1)
        first = _head_masks()
        if nw:
            comm = _Exchange(xin, xout, sems)
            pl.when((b == 0) & (i == 0))(comm.start)

        @pl.when((b == 0) & (i == 0))
        def _():
            dgain_ref[...] = jnp.zeros_like(dgain_ref)

        @pl.when(i == 0)
        def _():
            if lpp > lp:
                kx_scr[lp:lpp, :] = jnp.zeros((lpp - lp, 2 * ATTN_DIM), BF16)
                vx_scr[lp:lpp, :] = jnp.zeros((lpp - lp, ATTN_DIM), BF16)
                kt_scr[:, lp:lpp] = jnp.zeros((ATTN_DIM, lpp - lp), BF16)
            for p in range(N_PAIRS):
                kx_scr[0:lp, 256 * p:256 * p + 128] = k_ref[:, 128 * p:128 * (p + 1)]
                kx_scr[0:lp, 256 * p + 128:256 * (p + 1)] = ka_ref[...]
            vx_scr[0:lp, :] = v_ref[...]
            kt_scr[:, 0:lp] = _transpose_bf16(k_ref[...])
            dkx_scr[...] = jnp.zeros_like(dkx_scr)
            dvx_scr[...] = jnp.zeros_like(dvx_scr)

        rhs, rhs_t, lses, dos, dos_t, deltas = [], [], [], [], [], []
        for p in range(N_PAIRS):
            cols = slice(128 * p, 128 * (p + 1))
            side = _query_side(q_ref, qa_ref, p, first)
            rhs.append(side)
            rhs_t.append(_transpose_bf16(side))
            lses.append(jnp.concatenate([lse_ref[0, 2 * p:2 * p + 1, :], lse_ref[0, 2 * p + 1:2 * p + 2, :]], axis=1))
            ov = o_ref[:, cols].astype(F32)
            dzv = dz_ref[:, cols].astype(F32)
            gv = g_ref[:, cols]
            sq = ov * ov
            ms0 = jnp.sum(jnp.where(first, sq, 0.0), axis=1, keepdims=True) * (1.0 / HEAD_DIM)
            ms1 = jnp.sum(jnp.where(first, 0.0, sq), axis=1, keepdims=True) * (1.0 / HEAD_DIM)
            r = _pair_cols(lax.rsqrt(ms0 + EPS), lax.rsqrt(ms1 + EPS), first)
            ohat = ov * r
            dyhat = dzv * gv
            dgain_ref[:, cols] += jnp.sum(dzv * ohat, axis=0, keepdims=True)
            pr = dyhat * ohat
            mean0 = jnp.sum(jnp.where(first, pr, 0.0), axis=1, keepdims=True) * (1.0 / HEAD_DIM)
            mean1 = jnp.sum(jnp.where(first, 0.0, pr), axis=1, keepdims=True) * (1.0 / HEAD_DIM)
            do = r * (dyhat - ohat * _pair_cols(mean0, mean1, first))
            ddt = (do * ov).T
            deltas.append(jnp.concatenate([jnp.sum(ddt[0:HEAD_DIM], axis=0, keepdims=True),
                                           jnp.sum(ddt[HEAD_DIM:], axis=0, keepdims=True)], axis=1))
            do_st = _stack_heads(do.astype(BF16), first)
            dos.append(do_st)
            dos_t.append(_transpose_bf16(do_st))

        def step(c, carry, masked, tk):
            koff = pl.multiple_of(c * TK, TK)
            valid = _chunk_mask(i, c, tk) if masked else None
            new = []
            for p in range(N_PAIRS):
                dqt, dfq = carry[p]
                ext = slice(256 * p, 256 * (p + 1))
                cols = slice(128 * p, 128 * (p + 1))
                st = _dot(kx_scr[pl.ds(koff, tk), ext], rhs_t[p])
                if masked:
                    st = jnp.where(valid, st, NEG)
                pt = jnp.exp(st - lses[p])
                dpt = _dot(vx_scr[pl.ds(koff, tk), cols], dos_t[p])
                dst = pt * (dpt - deltas[p])
                dsb = dst.astype(BF16)
                dfq = dfq + jnp.sum(dsb.astype(F32), axis=0, keepdims=True)
                dkx_scr[pl.ds(koff, tk), ext] += _dot(dsb, rhs[p])
                dvx_scr[pl.ds(koff, tk), cols] += _dot(pt.astype(BF16), dos[p])
                kt = _stack_heads_lanes(kt_scr[cols, pl.ds(koff, tk)])
                dqt = dqt + _dot(kt, jnp.concatenate([dsb[:, 0:TQ], dsb[:, TQ:]], axis=0))
                new.append((dqt, dfq))
            return tuple(new)

        init = tuple((jnp.zeros((128, TQ), F32), jnp.zeros((1, 2 * TQ), F32)) for _ in range(N_PAIRS))
        final = _causal_sweep(i, step, init)

        for p in range(N_PAIRS):
            dqt, dfq = final[p]
            dq_ref[:, 128 * p:128 * (p + 1)] = (dqt.T * 0.125).astype(BF16)
            dfr_ref[0, 2 * p:2 * p + 1, :] = dfq[:, 0:TQ]
            dfr_ref[0, 2 * p + 1:2 * p + 2, :] = dfq[:, TQ:]

        @pl.when(i == nq - 1)
        def _():
            dka = jnp.zeros((lp, 128), F32)
            for p in range(N_PAIRS):
                dk_ref[:, 128 * p:128 * (p + 1)] = dkx_scr[0:lp, 256 * p:256 * p + 128].astype(BF16)
                dka = dka + dkx_scr[0:lp, 256 * p + 128:256 * (p + 1)]
            dka_ref[...] = dka
            dv_ref[...] = dvx_scr[0:lp, :].astype(BF16)

        if nw:
            pl.when((b == nb - 1) & (i == nq - 1))(comm.finish)

    qblk = pl.BlockSpec((TQ, ATTN_DIM), lambda b, i: (b * nq + i, 0))
    qablk = pl.BlockSpec((TQ, N_HEADS * 128), lambda b, i: (b * nq + i, 0))
    seq = pl.BlockSpec((lp, ATTN_DIM), lambda b, i: (b, 0))
    kaseq = pl.BlockSpec((lp, 128), lambda b, i: (b, 0))
    rowblk = pl.BlockSpec((1, N_HEADS, TQ), lambda b, i: (b, 0, i))
    gspec = pl.BlockSpec((1, ATTN_DIM), lambda b, i: (0, 0))
    return pl.pallas_call(
        body, name="attn_bwd", grid=(nb, nq),
        in_specs=[qblk, qblk, qablk, seq, seq, kaseq, qblk, rowblk, gspec] + [ANY] * nw,
        out_specs=[qblk, seq, seq, kaseq, rowblk, gspec] + [ANY] * nw,
        out_shape=[jax.ShapeDtypeStruct((n, ATTN_DIM), BF16), jax.ShapeDtypeStruct((n, ATTN_DIM), BF16),
                   jax.ShapeDtypeStruct((n, ATTN_DIM), BF16), jax.ShapeDtypeStruct((n, 128), F32),
                   jax.ShapeDtypeStruct((nb, N_HEADS, lp), F32), jax.ShapeDtypeStruct((1, ATTN_DIM), F32)]
        + [jax.ShapeDtypeStruct(a.shape, a.dtype) for a in exchange],
        scratch_shapes=[pltpu.VMEM((lpp, 2 * ATTN_DIM), BF16), pltpu.VMEM((lpp, ATTN_DIM), BF16),
                        pltpu.VMEM((ATTN_DIM, lpp), BF16), pltpu.VMEM((lpp, 2 * ATTN_DIM), F32),
                        pltpu.VMEM((lpp, ATTN_DIM), F32)] + (_comm_sems(nw) if nw else []),
        compiler_params=_params(("arbitrary", "arbitrary")),
    )(dz, q, qa, k, v, ka, o, lse, gain, *exchange)


def _adamw(parts, w, m, v, name):
    s_parts, r, c = parts.shape
    tr = r
    for t in (256, 128, 64, 32, 16):
        if r % t == 0 and r > t:
            tr = t
            break

    def body(p_ref, w_ref, m_ref, v_ref, g_ref, d_ref, nm_ref, nv_ref):
        g = p_ref[0].astype(F32)
        for s in range(1, s_parts):
            g = g + p_ref[s].astype(F32)
        nm = ADAM_B1 * m_ref[...] + (1.0 - ADAM_B1) * g
        nv = ADAM_B2 * v_ref[...] + (1.0 - ADAM_B2) * (g * g)
        m_hat = nm / (1.0 - ADAM_B1 ** ADAM_STEP)
        v_hat = nv / (1.0 - ADAM_B2 ** ADAM_STEP)
        g_ref[...] = g
        d_ref[...] = -ADAM_LR * (m_hat / (jnp.sqrt(v_hat) + ADAM_EPS) + ADAM_WD * w_ref[...])
        nm_ref[...] = nm
        nv_ref[...] = nv

    blk = pl.BlockSpec((tr, c), lambda i: (i, 0))
    return pl.pallas_call(
        body, name=name, grid=(r // tr,),
        in_specs=[pl.BlockSpec((s_parts, tr, c), lambda i: (0, i, 0)), blk, blk, blk],
        out_specs=[blk] * 4,
        out_shape=[jax.ShapeDtypeStruct((r, c), F32)] * 4,
        compiler_params=_params(("parallel",)),
    )(parts, w, m, v)


def _sum_parts(parts, name):
    s_parts, r, c = parts.shape

    def body(p_ref, out_ref):
        acc = p_ref[0]
        for s in range(1, s_parts):
            acc = acc + p_ref[s]
        out_ref[...] = acc

    return pl.pallas_call(
        body, name=name, out_shape=jax.ShapeDtypeStruct((r, c), F32),
        in_specs=[pl.BlockSpec(memory_space=pltpu.VMEM)], out_specs=pl.BlockSpec(memory_space=pltpu.VMEM),
    )(parts)


SMALL_ROWS = 184
LOSS_ROW = 181


def _pack_small(d_gains, d_gc, d_ga, d_bf, d_conv, d_meta, loss_part):
    rows = [g.reshape(8, 128) for g in d_gains]
    rows += [d_gc.reshape(4, 128), d_ga.reshape(4, 128), d_bf.reshape(1, 128)]
    rows += [d_conv.reshape(12, 128), d_meta.reshape(128, 128), jnp.pad(loss_part, ((0, 0), (0, 127)))]
    packed = jnp.concatenate(rows, axis=0)
    return jnp.pad(packed, ((0, SMALL_ROWS - packed.shape[0]), (0, 0)))


def kernel(x, meta_tokens, ffn1_norm, ffn1_w_gu, ffn1_w_down, mix_norm, w_in, conv_w, b_f, out_norm_conv, out_norm_attn, w_out, ffn2_norm, ffn2_w_gu, ffn2_w_down, final_norm, loss_target, m_meta_tokens, m_ffn1_norm, m_ffn1_w_gu, m_ffn1_w_down, m_mix_norm, m_w_in, m_conv_w, m_b_f, m_out_norm_conv, m_out_norm_attn, m_w_out, m_ffn2_norm, m_ffn2_w_gu, m_ffn2_w_down, m_final_norm, v_meta_tokens, v_ffn1_norm, v_ffn1_w_gu, v_ffn1_w_down, v_mix_norm, v_w_in, v_conv_w, v_b_f, v_out_norm_conv, v_out_norm_attn, v_w_out, v_ffn2_norm, v_ffn2_w_gu, v_ffn2_w_down, v_final_norm):
    nb, seq, _ = x.shape
    lp = PAD + N_META + seq
    me = 4 * lax.axis_index("x") + 2 * lax.axis_index("y") + lax.axis_index("c")
    shard_gu = D_FF // 4
    shard_d = D_FF // N_DEV

    small_in = jnp.concatenate(
        [meta_tokens, jnp.pad(conv_w[0], ((0, 0), (0, 128 - conv_w.shape[2]))), jnp.zeros((5, 128), F32)], axis=0)
    wgu1_8, wd1_8, small_8 = _all_gather(
        [ffn1_w_gu[0].T.astype(BF16), ffn1_w_down[0].astype(BF16), small_in], "gather_ffn1")
    meta_full = small_8[:, 0:N_META, :].transpose(1, 0, 2).reshape(N_META, D_MODEL)
    conv_full = small_8[:, N_META:N_META + 3, 0:CONV_DIM // N_DEV].transpose(1, 0, 2).reshape(3, CONV_DIM)
    wgu1 = wgu1_8.reshape(W_GU_SHAPE)
    wd1 = wd1_8.reshape(W_D_SHAPE)
    b_f_row = jnp.pad(b_f, ((0, 0), (0, 128 - N_HEADS)))
    gmat = _group_matrix()

    x2d = x.reshape(nb * seq, D_MODEL)
    later = [w_in[0].T.astype(BF16), w_out[0].astype(BF16), ffn2_w_gu[0].T.astype(BF16), ffn2_w_down[0].astype(BF16)]
    h1, n1, gate1, up1, win_8, wout_8, wgu2_8, wd2_8 = _ffn1_fwd(x2d, meta_full, lp, ffn1_norm, wgu1, wd1, later)
    wgu2 = wgu2_8.reshape(W_GU_SHAPE)
    wd2 = wd2_8.reshape(W_D_SHAPE)
    w_in_full = jnp.pad(win_8.reshape(IN_DIM, D_MODEL), ((0, IN_PAD - IN_DIM), (0, 0)))
    w_out_full = wout_8.reshape(D_MODEL, D_MODEL)

    bg, cg, hc, q, k, v, fg = _inproj_fwd(h1, mix_norm, w_in_full)
    zc = _conv_fwd(bg, cg, hc, conv_full, out_norm_conv, gmat, lp)
    ka, qa = _fgate_fwd(fg, b_f_row, lp)
    za, o, lse, h2 = _attn_fwd(q, qa, k, v, ka, out_norm_attn, zc, w_out_full, h1, lp)
    dh3, n3, gate2, up2, loss_part, d_final = _ffn2_fwd_loss(
        h2, ffn2_norm, wgu2, wd2, final_norm.reshape(1, D_MODEL), loss_target.reshape(nb * seq, D_MODEL), lp)

    dgu2, dwd2 = _ffn_bwd_act_wd(dh3, gate2, up2, wd2, "ffn2_bwd_act")
    dh2, d_ffn2 = _ffn_bwd_in(dh3, h2, ffn2_norm, dgu2, wgu2, "ffn2_bwd_in")
    dwgu2 = _ffn_bwd_wgu(n3, dgu2, "ffn2_bwd_wgu")
    dzc, dza, dwout = _outproj_bwd(dh2, zc, za, w_out_full)
    send_a = [dwgu2.reshape(N_DEV, shard_gu, D_MODEL), dwd2.reshape(N_DEV, shard_d, D_MODEL),
              dwout.reshape(N_DEV, D_MODEL // N_DEV, D_MODEL)]
    dq, dk, dv, dka, dfr, d_ga, p_wgu2, p_wd2, p_wout = _attn_bwd(
        dza, q, qa, k, v, ka, o, lse, out_norm_attn, lp, exchange=send_a)
    dfg, d_bf = _fgate_bwd(dka, dfr, fg, b_f_row, lp)
    dbg, dcg, dhc, d_conv, d_gc = _conv_bwd(dzc, bg, cg, hc, conv_full, out_norm_conv, gmat, lp)
    dh1, dwin, d_mix = _inproj_bwd([dbg, dcg, dhc, dq, dk, dv], dfg, dh2, h1, mix_norm, w_in_full)
    dwin_8 = dwin[0:IN_DIM].reshape(N_DEV, IN_DIM // N_DEV, D_MODEL)
    dgu1, dwd1, p_win = _ffn_bwd_act_wd(dh1, gate1, up1, wd1, "ffn1_bwd_act", exchange=[dwin_8])
    dwgu1, p_wd1 = _ffn_bwd_wgu(n1, dgu1, "ffn1_bwd_wgu", exchange=[dwd1.reshape(N_DEV, shard_d, D_MODEL)])
    own = dwgu1.reshape(N_DEV, shard_gu, D_MODEL)
    (got,) = _pair_exchange([own], "pair_exchange_ffn1")
    chip_sum = _pair_sum(own, got, "pair_sum_wgu1")
    dh0, d_ffn1, p_wgu1 = _ffn_bwd_in(dh1, None, ffn1_norm, dgu1, wgu1, "ffn1_bwd_in",
                                      tokens=(x2d, meta_full, lp), exchange=[chip_sum])

    dh0 = dh0.reshape(nb, lp, D_MODEL)
    grad_x = dh0[:, PAD + N_META:, :]
    d_meta = jnp.sum(dh0[:, PAD:PAD + N_META, :], axis=0)

    small = _pack_small([d_ffn1, d_mix, d_ffn2, d_final], d_gc, d_ga, d_bf, d_conv, d_meta, loss_part)
    (small_all,) = _all_gather([small], "gather_small_grads")
    small_sum = _sum_parts(small_all, "sum_small_grads")
    g_ffn1n, g_mixn, g_ffn2n, g_finaln = (small_sum[8 * t:8 * t + 8].reshape(1, D_MODEL) for t in range(4))
    g_gc = small_sum[32:36].reshape(1, CONV_DIM)
    g_ga = small_sum[36:40].reshape(1, ATTN_DIM)
    g_bf = small_sum[40:41, 0:N_HEADS]
    g_conv_full = small_sum[41:53].reshape(3, CONV_DIM)
    g_meta_full = small_sum[53:181].reshape(N_META, D_MODEL)
    g_conv = lax.dynamic_slice_in_dim(g_conv_full, me * (CONV_DIM // N_DEV), CONV_DIM // N_DEV, axis=1)
    g_meta = lax.dynamic_slice_in_dim(g_meta_full, me * (D_MODEL // N_DEV), D_MODEL // N_DEV, axis=1)

    weights = {
        "meta_tokens": (g_meta[None], meta_tokens, m_meta_tokens, v_meta_tokens),
        "ffn1_norm": (g_ffn1n[None], ffn1_norm, m_ffn1_norm, v_ffn1_norm),
        "ffn1_w_gu": (p_wgu1, ffn1_w_gu[0].T, m_ffn1_w_gu[0].T, v_ffn1_w_gu[0].T),
        "ffn1_w_down": (p_wd1, ffn1_w_down[0], m_ffn1_w_down[0], v_ffn1_w_down[0]),
        "mix_norm": (g_mixn[None], mix_norm, m_mix_norm, v_mix_norm),
        "w_in": (p_win, w_in[0].T, m_w_in[0].T, v_w_in[0].T),
        "conv_w": (g_conv[None], conv_w[0], m_conv_w[0], v_conv_w[0]),
        "b_f": (g_bf[None], b_f, m_b_f, v_b_f),
        "out_norm_conv": (g_gc[None], out_norm_conv, m_out_norm_conv, v_out_norm_conv),
        "out_norm_attn": (g_ga[None], out_norm_attn, m_out_norm_attn, v_out_norm_attn),
        "w_out": (p_wout, w_out[0], m_w_out[0], v_w_out[0]),
        "ffn2_norm": (g_ffn2n[None], ffn2_norm, m_ffn2_norm, v_ffn2_norm),
        "ffn2_w_gu": (p_wgu2, ffn2_w_gu[0].T, m_ffn2_w_gu[0].T, v_ffn2_w_gu[0].T),
        "ffn2_w_down": (p_wd2, ffn2_w_down[0], m_ffn2_w_down[0], v_ffn2_w_down[0]),
        "final_norm": (g_finaln[None], final_norm.reshape(1, D_MODEL), m_final_norm.reshape(1, D_MODEL),
                       v_final_norm.reshape(1, D_MODEL)),
    }
    shapes = {"meta_tokens": meta_tokens.shape, "ffn1_norm": ffn1_norm.shape, "ffn1_w_gu": ffn1_w_gu.shape,
              "ffn1_w_down": ffn1_w_down.shape, "mix_norm": mix_norm.shape, "w_in": w_in.shape,
              "conv_w": conv_w.shape, "b_f": b_f.shape, "out_norm_conv": out_norm_conv.shape,
              "out_norm_attn": out_norm_attn.shape, "w_out": w_out.shape, "ffn2_norm": ffn2_norm.shape,
              "ffn2_w_gu": ffn2_w_gu.shape, "ffn2_w_down": ffn2_w_down.shape, "final_norm": final_norm.shape}
    grads, deltas, new_m, new_v = [], [], [], []
    for name, (p, w, m, vv) in weights.items():
        g, d, nm, nv = _adamw(p, w, m, vv, "adamw_" + name)
        if name in ("ffn1_w_gu", "ffn2_w_gu", "w_in"):
            g, d, nm, nv = g.T, d.T, nm.T, nv.T
        shape = shapes[name]
        grads.append(g.reshape(shape))
        deltas.append(d.reshape(shape))
        new_m.append(nm.reshape(shape))
        new_v.append(nv.reshape(shape))

    loss = small_sum[LOSS_ROW, 0]
    return (loss, grad_x, *grads, *deltas, *new_m, *new_v)
```

```python
import jax
import jax.numpy as jnp
from jax import lax
from jax.experimental import pallas as pl
from jax.experimental.pallas import tpu as pltpu

F32 = jnp.float32
BF16 = jnp.bfloat16

N_DEV = 8
D_MODEL = 1024
N_META = 16
PAD = 128 - N_META
CONV_DIM = 512
ATTN_DIM = 512
HEAD_DIM = 64
N_HEADS = 8
N_PAIRS = N_HEADS // 2
D_FF = 2816
IN_DIM = 3080
IN_PAD = 3200
IN_MAIN = 3072
N_PIECE = IN_MAIN // 512
EPS = 1e-6
NEG = -1e30
TQ = 128
TK = 512
VMEM_LIMIT = 56 * 1024 * 1024

HID_PIECES = ((0, 1024), (1024, 2048), (2048, D_FF))
ACT_PIECES = tuple((a, min(a + 256, D_FF)) for a in range(0, D_FF, 256))
W_GU_SHAPE = (2, D_FF, D_MODEL)
W_D_SHAPE = (D_FF, D_MODEL)

ADAM_LR = 0.001
ADAM_B1 = 0.9
ADAM_B2 = 0.999
ADAM_EPS = 1e-08
ADAM_WD = 0.01
ADAM_STEP = 10

MESH = pl.DeviceIdType.MESH
ANY = pl.BlockSpec(memory_space=pl.ANY)


def _params(sem=None):
    return pltpu.CompilerParams(dimension_semantics=sem, vmem_limit_bytes=VMEM_LIMIT)


def _row_tile(n, prefer):
    for t in (prefer, 512, 256, 128):
        if t <= n and n % t == 0:
            return t
    raise ValueError(f"no row tile for {n}")


def _resident(shape):
    zeros = (0,) * len(shape)
    return pl.BlockSpec(shape, lambda i: zeros, pipeline_mode=pl.Buffered(1))


def _dot(a, b):
    return jnp.dot(a, b, preferred_element_type=F32)


def _dot_nt(a, b):
    return lax.dot_general(a, b, (((1,), (1,)), ((), ())), preferred_element_type=F32)


def _dot_tn(a, b):
    return lax.dot_general(a, b, (((0,), (0,)), ((), ())), preferred_element_type=F32)


def _rms(x, g):
    r = lax.rsqrt(jnp.mean(x * x, axis=-1, keepdims=True) + EPS)
    xhat = x * r
    return xhat * g, xhat, r


def _rms_bwd(dn, xhat, r, g):
    dxhat = dn * g
    return r * (dxhat - xhat * jnp.mean(dxhat * xhat, axis=-1, keepdims=True))


def _sigmoid(x):
    return 1.0 / (1.0 + jnp.exp(-x))


def _place():
    return lax.axis_index("x"), lax.axis_index("y"), lax.axis_index("c")


def _comm_sems(nw):
    return [pltpu.SemaphoreType.DMA((nw, 7)), pltpu.SemaphoreType.DMA((nw, 7)), pltpu.SemaphoreType.DMA((nw,))]


def _flip(v, bit):
    return 1 - v if bit else v


class _Gather:
    def __init__(self, ins, outs, sems):
        self.ins, self.outs = ins, outs
        self.send, self.recv, self.local = sems
        x, y, c = _place()
        self.c = c
        self.me, self.sibling = (x, y, c), (x, y, 1 - c)
        first = ((x + 1 - c) % 2, (y + c) % 2)
        second = ((x + c) % 2, (y + 1 - c) % 2)
        self.chips = [first, second, (1 - x, 1 - y)]
        self.targets = [first, second, second]

    def _copy(self, w, k, block, to, own=False):
        slot = self.outs[w].at[4 * block[0] + 2 * block[1] + block[2]]
        return pltpu.make_async_remote_copy(
            src_ref=self.ins[w] if own else slot, dst_ref=slot,
            send_sem=self.send.at[w, k], recv_sem=self.recv.at[w, k], device_id=to, device_id_type=MESH)

    def _mine(self, w):
        x, y, c = self.me
        return pltpu.make_async_copy(self.ins[w], self.outs[w].at[4 * x + 2 * y + c], self.local.at[w])

    def _first(self, w):
        return ([self._copy(w, 0, self.me, self.sibling, own=True)]
                + [self._copy(w, 1 + j, self.me, (*self.targets[j], self.c), own=True) for j in range(2)])

    def _relay(self, w):
        return self._copy(w, 3, (*self.chips[0], self.c), (*self.targets[2], self.c))

    def _landed(self, w, j):
        return self._copy(w, 1 + j, (*self.chips[j], self.c), self.me)

    def _passed(self, w):
        return [self._copy(w, 4 + j, (*chip, self.c), self.sibling) for j, chip in enumerate(self.chips)]

    def start(self):
        for w in range(len(self.ins)):
            self._mine(w).start()
        for w in range(len(self.ins)):
            for cp in self._first(w):
                cp.start()

    def relay(self):
        for w in range(len(self.ins)):
            self._landed(w, 0).wait_recv()
            self._relay(w).start()
            self._passed(w)[0].start()

    def forward(self):
        for w in range(len(self.ins)):
            for j in (1, 2):
                self._landed(w, j).wait_recv()
                self._passed(w)[j].start()

    def finish(self):
        from_sibling = [self.chips[1], self.chips[0], self.chips[2]]
        for w in range(len(self.ins)):
            self._copy(w, 0, self.sibling, self.me).wait_recv()
            for j, chip in enumerate(from_sibling):
                self._copy(w, 4 + j, (*chip, 1 - self.c), self.me).wait_recv()
        for w in range(len(self.ins)):
            for cp in self._first(w) + [self._relay(w)] + self._passed(w):
                cp.wait_send()
            self._mine(w).wait()


class _Exchange:
    def __init__(self, ins, outs, sems):
        self.ins, self.outs = ins, outs
        self.send, self.recv, self.local = sems
        self.x, self.y, self.c = _place()
        self.me = 4 * self.x + 2 * self.y + self.c

    def _copy(self, w, k):
        peer = (_flip(self.x, ((k + 1) >> 2) & 1), _flip(self.y, ((k + 1) >> 1) & 1), _flip(self.c, (k + 1) & 1))
        return pltpu.make_async_remote_copy(
            src_ref=self.ins[w].at[4 * peer[0] + 2 * peer[1] + peer[2]], dst_ref=self.outs[w].at[self.me],
            send_sem=self.send.at[w, k], recv_sem=self.recv.at[w, k], device_id=peer, device_id_type=MESH)

    def _mine(self, w):
        return pltpu.make_async_copy(self.ins[w].at[self.me], self.outs[w].at[self.me], self.local.at[w])

    def start(self):
        for w in range(len(self.ins)):
            self._mine(w).start()
            for k in range(N_DEV - 1):
                self._copy(w, k).start()

    def finish(self):
        for w in range(len(self.ins)):
            for k in range(N_DEV - 1):
                self._copy(w, k).wait()
            self._mine(w).wait()


class _PairExchange:
    def __init__(self, ins, outs, sems):
        self.ins, self.outs = ins, outs
        self.send, self.recv, _ = sems
        x, y, self.c = _place()
        self.sibling = (x, y, 1 - self.c)

    def _copy(self, w, t):
        return pltpu.make_async_remote_copy(
            src_ref=self.ins[w].at[2 * t + 1 - self.c], dst_ref=self.outs[w].at[t],
            send_sem=self.send.at[w, t], recv_sem=self.recv.at[w, t], device_id=self.sibling, device_id_type=MESH)

    def start(self):
        for w in range(len(self.ins)):
            for t in range(4):
                self._copy(w, t).start()

    def finish(self):
        for w in range(len(self.ins)):
            for t in range(4):
                self._copy(w, t).wait()


class _ChipExchange:
    def __init__(self, ins, outs, sems):
        self.ins, self.outs = ins, outs
        self.send, self.recv, self.local = sems
        self.x, self.y, self.c = _place()
        self.chip = 2 * self.x + self.y

    def _copy(self, w, k):
        px, py = _flip(self.x, ((k + 1) >> 1) & 1), _flip(self.y, (k + 1) & 1)
        return pltpu.make_async_remote_copy(
            src_ref=self.ins[w].at[2 * px + py], dst_ref=self.outs[w].at[self.chip],
            send_sem=self.send.at[w, k], recv_sem=self.recv.at[w, k], device_id=(px, py, self.c),
            device_id_type=MESH)

    def _mine(self, w):
        return pltpu.make_async_copy(self.ins[w].at[self.chip], self.outs[w].at[self.chip], self.local.at[w])

    def start(self):
        for w in range(len(self.ins)):
            self._mine(w).start()
            for k in range(3):
                self._copy(w, k).start()

    def finish(self):
        for w in range(len(self.ins)):
            for k in range(3):
                self._copy(w, k).wait()
            self._mine(w).wait()


def _split_refs(refs, n_in, n_comm, n_out, n_scr):
    a = n_in
    b = a + n_comm
    c = b + n_out
    d = c + n_comm
    e = d + n_scr
    return refs[:a], refs[a:b], refs[b:c], refs[c:d], refs[d:e], refs[e:]


def _all_gather(xs, name):
    nw = len(xs)

    def body(*refs):
        comm = _Gather(refs[:nw], refs[nw:2 * nw], refs[2 * nw:])
        comm.start()
        comm.relay()
        comm.forward()
        comm.finish()

    return pl.pallas_call(
        body, name=name, in_specs=[ANY] * nw, out_specs=[ANY] * nw,
        out_shape=[jax.ShapeDtypeStruct((N_DEV,) + a.shape, a.dtype) for a in xs],
        scratch_shapes=_comm_sems(nw),
    )(*xs)


def _pair_exchange(xs, name):
    nw = len(xs)

    def body(*refs):
        comm = _PairExchange(refs[:nw], refs[nw:2 * nw], refs[2 * nw:])
        comm.start()
        comm.finish()

    return pl.pallas_call(
        body, name=name, in_specs=[ANY] * nw, out_specs=[ANY] * nw,
        out_shape=[jax.ShapeDtypeStruct((4,) + a.shape[1:], a.dtype) for a in xs],
        scratch_shapes=_comm_sems(nw),
    )(*xs)


def _pair_sum(own, got, name):
    _, r, c = own.shape
    tr = r
    for t in (256, 128, 64, 32, 16):
        if r % t == 0 and r > t:
            tr = t
            break

    def body(own_ref, got_ref, out_ref):
        mine = jnp.where(lax.axis_index("c") == 0, own_ref[:, 0].astype(F32), own_ref[:, 1].astype(F32))
        out_ref[...] = (mine + got_ref[...].astype(F32)).astype(BF16)

    return pl.pallas_call(
        body, name=name, grid=(r // tr,),
        in_specs=[pl.BlockSpec((4, 2, tr, c), lambda i: (0, 0, i, 0)), pl.BlockSpec((4, tr, c), lambda i: (0, i, 0))],
        out_specs=pl.BlockSpec((4, tr, c), lambda i: (0, i, 0)),
        out_shape=jax.ShapeDtypeStruct((4, r, c), BF16),
        compiler_params=_params(("parallel",)),
    )(own.reshape(4, 2, r, c), got)


def _token_spec(k, ksub, nq):
    def index_map(i):
        s = ksub * i + k
        return ((s // nq) * (nq - 1) + jnp.maximum(s % nq, 1) - 1, 0)
    return pl.BlockSpec((128, D_MODEL), index_map)


def _is_lead(i, k, ksub, nq):
    return ((ksub * i + k) % nq) == 0


def _assemble_rows(i, x_refs, meta_ref, nq):
    ksub = len(x_refs)
    lead = jnp.concatenate([jnp.zeros((PAD, D_MODEL), F32), meta_ref[...]], axis=0)
    return jnp.concatenate([jnp.where(_is_lead(i, k, ksub, nq), lead, x_refs[k][...]) for k in range(ksub)], axis=0)


def _swiglu(nb, wgu_ref, wd_ref, gate_ref, up_ref):
    acc = jnp.zeros((nb.shape[0], D_MODEL), F32)
    for a, b in HID_PIECES:
        gate = _dot_nt(nb, wgu_ref[0, a:b, :])
        up = _dot_nt(nb, wgu_ref[1, a:b, :])
        gate_ref[:, a:b] = gate.astype(BF16)
        up_ref[:, a:b] = up.astype(BF16)
        acc = acc + _dot((gate * _sigmoid(gate) * up).astype(BF16), wd_ref[a:b, :])
    return acc


def _ffn1_fwd(x2d, meta, lp, gain, wgu, wd, gather):
    nq = lp // 128
    n = (x2d.shape[0] // (nq - 1)) * nq
    tm = _row_tile(n, 512)
    ksub = tm // 128
    n_i = n // tm
    nw = len(gather)

    def body(*refs):
        x_refs = refs[:ksub]
        (meta_ref, g_ref, wgu_ref, wd_ref), gin, (out_ref, nrm_ref, gate_ref, up_ref), gout, _, sems = \
            _split_refs(refs[ksub:], 4, nw, 4, 0)
        i = pl.program_id(0)
        comm = _Gather(gin, gout, sems)
        pl.when(i == 0)(comm.start)
        pl.when(i == n_i // 2)(comm.relay)
        pl.when(i == max(n_i - 3, n_i // 2))(comm.forward)

        hv = _assemble_rows(i, x_refs, meta_ref, nq)
        y, _, _ = _rms(hv, g_ref[...])
        nb = y.astype(BF16)
        nrm_ref[...] = nb
        out_ref[...] = hv + 0.5 * _swiglu(nb, wgu_ref, wd_ref, gate_ref, up_ref)

        pl.when(i == n_i - 1)(comm.finish)

    rows = pl.BlockSpec((tm, D_MODEL), lambda i: (i, 0))
    hid = pl.BlockSpec((tm, D_FF), lambda i: (i, 0))
    return pl.pallas_call(
        body, name="ffn1_fwd", grid=(n_i,),
        in_specs=[_token_spec(k, ksub, nq) for k in range(ksub)]
        + [pl.BlockSpec((N_META, D_MODEL), lambda i: (0, 0)), pl.BlockSpec((1, D_MODEL), lambda i: (0, 0)),
           _resident(W_GU_SHAPE), _resident(W_D_SHAPE)] + [ANY] * nw,
        out_specs=[rows, rows, hid, hid] + [ANY] * nw,
        out_shape=[jax.ShapeDtypeStruct((n, D_MODEL), F32), jax.ShapeDtypeStruct((n, D_MODEL), BF16),
                   jax.ShapeDtypeStruct((n, D_FF), BF16), jax.ShapeDtypeStruct((n, D_FF), BF16)]
        + [jax.ShapeDtypeStruct((N_DEV,) + a.shape, a.dtype) for a in gather],
        scratch_shapes=_comm_sems(nw),
        compiler_params=_params(("arbitrary",)),
    )(*([x2d] * ksub), meta, gain, wgu, wd, *gather)


def _ffn2_fwd_loss(h, gain, wgu, wd, gfinal, target, lp):
    n = h.shape[0]
    nq = lp // 128
    tm = _row_tile(n, 512)
    ksub = tm // 128
    n_i = n // tm

    def body(*refs):
        t_refs = refs[:ksub]
        h_ref, g_ref, wgu_ref, wd_ref, gf_ref, dh_ref, nrm_ref, gate_ref, up_ref, loss_ref, dgf_ref = refs[ksub:]
        i = pl.program_id(0)

        @pl.when(i == 0)
        def _():
            loss_ref[...] = jnp.zeros_like(loss_ref)
            dgf_ref[...] = jnp.zeros_like(dgf_ref)

        hv = h_ref[...]
        y, _, _ = _rms(hv, g_ref[...])
        nb = y.astype(BF16)
        nrm_ref[...] = nb
        hout = hv + 0.5 * _swiglu(nb, wgu_ref, wd_ref, gate_ref, up_ref)

        gf = gf_ref[...]
        loss = jnp.zeros((1, 1), F32)
        dgf = jnp.zeros((1, D_MODEL), F32)
        for k in range(ksub):
            yk, xhat, r = _rms(hout[128 * k:128 * (k + 1)], gf)
            err = jnp.where(_is_lead(i, k, ksub, nq), 0.0, yk - t_refs[k][...])
            loss = loss + 0.5 * jnp.sum(jnp.sum(err * err, axis=1, keepdims=True), axis=0,
                                        keepdims=True) * (1.0 / D_MODEL)
            dy = err * (1.0 / D_MODEL)
            dh_ref[128 * k:128 * (k + 1), :] = _rms_bwd(dy, xhat, r, gf)
            dgf = dgf + jnp.sum(dy * xhat, axis=0, keepdims=True)
        loss_ref[...] += loss
        dgf_ref[...] += dgf

    rows = pl.BlockSpec((tm, D_MODEL), lambda i: (i, 0))
    hid = pl.BlockSpec((tm, D_FF), lambda i: (i, 0))
    vec = pl.BlockSpec((1, D_MODEL), lambda i: (0, 0))
    return pl.pallas_call(
        body, name="ffn2_fwd_loss", grid=(n_i,),
        in_specs=[_token_spec(k, ksub, nq) for k in range(ksub)]
        + [rows, vec, _resident(W_GU_SHAPE), _resident(W_D_SHAPE), vec],
        out_specs=[rows, rows, hid, hid, pl.BlockSpec((1, 1), lambda i: (0, 0)), vec],
        out_shape=[jax.ShapeDtypeStruct((n, D_MODEL), F32), jax.ShapeDtypeStruct((n, D_MODEL), BF16),
                   jax.ShapeDtypeStruct((n, D_FF), BF16), jax.ShapeDtypeStruct((n, D_FF), BF16),
                   jax.ShapeDtypeStruct((1, 1), F32), jax.ShapeDtypeStruct((1, D_MODEL), F32)],
        compiler_params=_params(("arbitrary",)),
    )(*([target] * ksub), h, gain, wgu, wd, gfinal)


def _ffn_bwd_act_wd(dh_out, gate, up, wd, name, exchange=()):
    n = dh_out.shape[0]
    tm = _row_tile(n, 256)
    n_i = n // tm
    nw = len(exchange)

    def body(*refs):
        (dh_ref, gate_ref, up_ref, wd_ref), xin, (dgu_ref, dw_ref), xout, (acc_scr,), sems = \
            _split_refs(refs, 4, nw, 2, 1)
        i = pl.program_id(0)
        if nw:
            comm = _Exchange(xin, xout, sems)
            pl.when(i == 0)(comm.start)

        @pl.when(i == 0)
        def _():
            acc_scr[...] = jnp.zeros_like(acc_scr)

        dhb = (0.5 * dh_ref[...]).astype(BF16)
        for a, b in ACT_PIECES:
            da = _dot_nt(dhb, wd_ref[a:b, :])
            g = gate_ref[:, a:b].astype(F32)
            u = up_ref[:, a:b].astype(F32)
            sig = _sigmoid(g)
            silu = g * sig
            dgu_ref[:, a:b] = (da * u * (sig * (1.0 + g * (1.0 - sig)))).astype(BF16)
            dgu_ref[:, D_FF + a:D_FF + b] = (da * silu).astype(BF16)
            acc_scr[a:b, :] += _dot_tn((silu * u).astype(BF16), dhb)

        @pl.when(i == n_i - 1)
        def _():
            dw_ref[...] = acc_scr[...].astype(BF16)

        if nw:
            pl.when(i == n_i - 1)(comm.finish)

    rows = pl.BlockSpec((tm, D_MODEL), lambda i: (i, 0))
    hid = pl.BlockSpec((tm, D_FF), lambda i: (i, 0))
    return pl.pallas_call(
        body, name=name, grid=(n_i,),
        in_specs=[rows, hid, hid, _resident(W_D_SHAPE)] + [ANY] * nw,
        out_specs=[pl.BlockSpec((tm, 2 * D_FF), lambda i: (i, 0)), _resident(W_D_SHAPE)] + [ANY] * nw,
        out_shape=[jax.ShapeDtypeStruct((n, 2 * D_FF), BF16), jax.ShapeDtypeStruct(W_D_SHAPE, BF16)]
        + [jax.ShapeDtypeStruct(a.shape, a.dtype) for a in exchange],
        scratch_shapes=[pltpu.VMEM(W_D_SHAPE, F32)] + (_comm_sems(nw) if nw else []),
        compiler_params=_params(("arbitrary",)),
    )(dh_out, gate, up, wd, *exchange)


def _ffn_bwd_in(dh_out, h_in, gain, dgu, wgu, name, tokens=None, exchange=()):
    n = dh_out.shape[0]
    tm = _row_tile(n, 512)
    n_i = n // tm
    nw = len(exchange)
    ksub, nq = (tm // 128, tokens[2] // 128) if tokens else (1, 0)

    def body(*refs):
        h_refs = refs[:ksub]
        (meta_ref, dh_ref, g_ref, dgu_ref, wgu_ref), xin, (dhin_ref, dgain_ref), xout, _, sems = \
            _split_refs(refs[ksub:], 5, nw, 2, 0)
        i = pl.program_id(0)
        if nw:
            comm = _ChipExchange(xin, xout, sems)
            pl.when(i == 0)(comm.start)

        @pl.when(i == 0)
        def _():
            dgain_ref[...] = jnp.zeros_like(dgain_ref)

        dn = _dot(dgu_ref[...], wgu_ref[...])
        gain_v = g_ref[...]
        hv = _assemble_rows(i, h_refs, meta_ref, nq) if tokens else h_refs[0][...]
        _, xhat, r = _rms(hv, gain_v)
        dhin_ref[...] = dh_ref[...] + _rms_bwd(dn, xhat, r, gain_v)
        dgain_ref[...] += jnp.sum(dn * xhat, axis=0, keepdims=True)

        if nw:
            pl.when(i == n_i - 1)(comm.finish)

    rows = pl.BlockSpec((tm, D_MODEL), lambda i: (i, 0))
    hid = pl.BlockSpec((tm, D_FF), lambda i: (i, 0))
    vec = pl.BlockSpec((1, D_MODEL), lambda i: (0, 0))
    meta_spec = pl.BlockSpec((N_META, D_MODEL), lambda i: (0, 0))
    if tokens:
        h_specs, h_args, meta = [_token_spec(k, ksub, nq) for k in range(ksub)], [tokens[0]] * ksub, tokens[1]
    else:
        h_specs, h_args, meta = [rows], [h_in], jnp.zeros((N_META, D_MODEL), F32)
    return pl.pallas_call(
        body, name=name, grid=(n_i,),
        in_specs=h_specs + [meta_spec, rows, vec, pl.BlockSpec((tm, 2 * D_FF), lambda i: (i, 0)),
                            _resident((2 * D_FF, D_MODEL))] + [ANY] * nw,
        out_specs=[rows, vec] + [ANY] * nw,
        out_shape=[jax.ShapeDtypeStruct((n, D_MODEL), F32), jax.ShapeDtypeStruct((1, D_MODEL), F32)]
        + [jax.ShapeDtypeStruct(a.shape, a.dtype) for a in exchange],
        scratch_shapes=_comm_sems(nw) if nw else [],
        compiler_params=_params(("arbitrary",)),
    )(*h_args, meta, dh_out, gain, dgu, wgu.reshape(2 * D_FF, D_MODEL), *exchange)


def _ffn_bwd_wgu(nrm, dgu, name, exchange=()):
    n = nrm.shape[0]
    tm = _row_tile(n, 512)
    n_i = n // tm
    nw = len(exchange)

    def body(*refs):
        (nrm_ref, dgu_ref), xin, (dw_ref,), xout, (acc_scr,), sems = _split_refs(refs, 2, nw, 1, 1)
        i = pl.program_id(0)
        if nw:
            comm = _Exchange(xin, xout, sems)
            pl.when(i == 0)(comm.start)

        @pl.when(i == 0)
        def _():
            acc_scr[...] = jnp.zeros_like(acc_scr)

        nb = nrm_ref[...]
        for half in (0, D_FF):
            for a, b in HID_PIECES:
                acc_scr[half + a:half + b, :] += _dot_tn(dgu_ref[:, half + a:half + b], nb)

        @pl.when(i == n_i - 1)
        def _():
            dw_ref[...] = acc_scr[...].astype(BF16)

        if nw:
            pl.when(i == n_i - 1)(comm.finish)

    shape = (2 * D_FF, D_MODEL)
    res = pl.pallas_call(
        body, name=name, grid=(n_i,),
        in_specs=[pl.BlockSpec((tm, D_MODEL), lambda i: (i, 0)),
                  pl.BlockSpec((tm, 2 * D_FF), lambda i: (i, 0))] + [ANY] * nw,
        out_specs=[_resident(shape)] + [ANY] * nw,
        out_shape=[jax.ShapeDtypeStruct(shape, BF16)] + [jax.ShapeDtypeStruct(a.shape, a.dtype) for a in exchange],
        scratch_shapes=[pltpu.VMEM(shape, F32)] + (_comm_sems(nw) if nw else []),
        compiler_params=_params(("arbitrary",)),
    )(nrm, dgu, *exchange)
    return res if nw else res[0]


def _inproj_fwd(h, gain, w_in):
    n = h.shape[0]
    tm = _row_tile(n, 512)

    def body(h_ref, g_ref, w_ref, *outs):
        y, _, _ = _rms(h_ref[...], g_ref[...])
        nb = y.astype(BF16)
        for p in range(N_PIECE):
            outs[p][...] = _dot_nt(nb, w_ref[512 * p:512 * (p + 1), :]).astype(BF16)
        outs[N_PIECE][...] = _dot_nt(nb, w_ref[IN_MAIN:IN_PAD, :])

    piece = pl.BlockSpec((tm, 512), lambda i: (i, 0))
    return pl.pallas_call(
        body, name="inproj_fwd", grid=(n // tm,),
        in_specs=[pl.BlockSpec((tm, D_MODEL), lambda i: (i, 0)),
                  pl.BlockSpec((1, D_MODEL), lambda i: (0, 0)),
                  pl.BlockSpec((IN_PAD, D_MODEL), lambda i: (0, 0))],
        out_specs=[piece] * N_PIECE + [pl.BlockSpec((tm, 128), lambda i: (i, 0))],
        out_shape=[jax.ShapeDtypeStruct((n, 512), BF16)] * N_PIECE + [jax.ShapeDtypeStruct((n, 128), F32)],
        compiler_params=_params(("parallel",)),
    )(h, gain, w_in)


def _inproj_bwd(dpieces, dfg, dh_out, h_in, gain, w_in):
    n = h_in.shape[0]
    tm = _row_tile(n, 512)
    n_i = n // tm

    def body(*refs):
        dp_refs = refs[:N_PIECE]
        dfg_ref, dh_ref, h_ref, g_ref, w_ref, dhin_ref, dw_ref, dgain_ref, acc_scr = refs[N_PIECE:]
        i = pl.program_id(0)

        @pl.when(i == 0)
        def _():
            acc_scr[...] = jnp.zeros_like(acc_scr)
            dgain_ref[...] = jnp.zeros_like(dgain_ref)

        gain_v = g_ref[...]
        y, xhat, r = _rms(h_ref[...], gain_v)
        nb = y.astype(BF16)
        dn = jnp.zeros((tm, D_MODEL), F32)
        for p in range(N_PIECE + 1):
            lo, hi = (512 * p, 512 * (p + 1)) if p < N_PIECE else (IN_MAIN, IN_PAD)
            dp = (dp_refs[p][...] if p < N_PIECE else dfg_ref[...]).astype(BF16)
            dn = dn + _dot(dp, w_ref[lo:hi, :])
            acc_scr[lo:hi, :] += _dot_tn(dp, nb)
        dhin_ref[...] = dh_ref[...] + _rms_bwd(dn, xhat, r, gain_v)
        dgain_ref[...] += jnp.sum(dn * xhat, axis=0, keepdims=True)

        @pl.when(i == n_i - 1)
        def _():
            dw_ref[...] = acc_scr[...].astype(BF16)

    piece = pl.BlockSpec((tm, 512), lambda i: (i, 0))
    rows = pl.BlockSpec((tm, D_MODEL), lambda i: (i, 0))
    vec = pl.BlockSpec((1, D_MODEL), lambda i: (0, 0))
    wspec = pl.BlockSpec((IN_PAD, D_MODEL), lambda i: (0, 0))
    return pl.pallas_call(
        body, name="inproj_bwd", grid=(n_i,),
        in_specs=[piece] * N_PIECE + [pl.BlockSpec((tm, 128), lambda i: (i, 0)), rows, rows, vec, wspec],
        out_specs=[rows, wspec, vec],
        out_shape=[jax.ShapeDtypeStruct((n, D_MODEL), F32),
                   jax.ShapeDtypeStruct((IN_PAD, D_MODEL), BF16),
                   jax.ShapeDtypeStruct((1, D_MODEL), F32)],
        scratch_shapes=[pltpu.VMEM((IN_PAD, D_MODEL), F32)],
        compiler_params=_params(("arbitrary",)),
    )(*dpieces, dfg, dh_out, h_in, gain, w_in)


def _outproj_bwd(dh, zc, za, w_out):
    n = dh.shape[0]
    tm = _row_tile(n, 512)
    n_i = n // tm

    def body(dh_ref, zc_ref, za_ref, w_ref, dzc_ref, dza_ref, dw_ref, acc_scr):
        i = pl.program_id(0)

        @pl.when(i == 0)
        def _():
            acc_scr[...] = jnp.zeros_like(acc_scr)

        dhb = dh_ref[...].astype(BF16)
        dzc_ref[...] = _dot_nt(dhb, w_ref[0:CONV_DIM, :]).astype(BF16)
        dza_ref[...] = _dot_nt(dhb, w_ref[CONV_DIM:, :]).astype(BF16)
        acc_scr[0:CONV_DIM, :] += _dot_tn(zc_ref[...], dhb)
        acc_scr[CONV_DIM:, :] += _dot_tn(za_ref[...], dhb)

        @pl.when(i == n_i - 1)
        def _():
            dw_ref[...] = acc_scr[...].astype(BF16)

    half = pl.BlockSpec((tm, 512), lambda i: (i, 0))
    wspec = pl.BlockSpec((D_MODEL, D_MODEL), lambda i: (0, 0))
    return pl.pallas_call(
        body, name="outproj_bwd", grid=(n_i,),
        in_specs=[pl.BlockSpec((tm, D_MODEL), lambda i: (i, 0)), half, half, wspec],
        out_specs=[half, half, wspec],
        out_shape=[jax.ShapeDtypeStruct((n, 512), BF16), jax.ShapeDtypeStruct((n, 512), BF16),
                   jax.ShapeDtypeStruct((D_MODEL, D_MODEL), BF16)],
        scratch_shapes=[pltpu.VMEM((D_MODEL, D_MODEL), F32)],
        compiler_params=_params(("arbitrary",)),
    )(dh, zc, za, w_out)


def _group_matrix():
    r = lax.broadcasted_iota(jnp.int32, (128, 128), 0) // HEAD_DIM
    c = lax.broadcasted_iota(jnp.int32, (128, 128), 1) // HEAD_DIM
    return jnp.where(r == c, 1.0 / HEAD_DIM, 0.0).astype(BF16)


def _group_mean(x, gmat):
    hi = x.astype(BF16)
    lo = (x - hi.astype(F32)).astype(BF16)
    return _dot(hi, gmat) + _dot(lo, gmat)


def _shift_rows(x, s):
    rows = x.shape[0]
    t = lax.broadcasted_iota(jnp.int32, x.shape, 0)
    rolled = pltpu.roll(x, s % rows, 0)
    keep = (t >= s) if s > 0 else (t < rows + s)
    return jnp.where(keep, rolled, 0.0)


def _conv_parts(bg_ref, cg_ref, hc_ref, w_ref):
    bg = bg_ref[...].astype(F32)
    cg = cg_ref[...].astype(F32)
    hc = hc_ref[...].astype(F32)
    u = cg * hc
    u1 = _shift_rows(u, 1)
    u2 = _shift_rows(u, 2)
    conv = w_ref[2:3, :] * u + w_ref[1:2, :] * u1 + w_ref[0:1, :] * u2
    return bg, cg, hc, u, u1, u2, conv


def _conv_fwd(bg, cg, hc, conv_w, gain, gmat, lp):
    n = bg.shape[0]
    nb = n // lp

    def body(bg_ref, cg_ref, hc_ref, w_ref, g_ref, gm_ref, z_ref):
        bgv, _, _, _, _, _, conv = _conv_parts(bg_ref, cg_ref, hc_ref, w_ref)
        yc = bgv * conv
        r = lax.rsqrt(_group_mean(yc * yc, gm_ref[...]) + EPS)
        z_ref[...] = (yc * r * g_ref[...]).astype(BF16)

    blk = pl.BlockSpec((lp, 128), lambda c, b: (b, c))
    return pl.pallas_call(
        body, name="conv_fwd", grid=(CONV_DIM // 128, nb),
        in_specs=[blk, blk, blk, pl.BlockSpec((3, 128), lambda c, b: (0, c)),
                  pl.BlockSpec((1, 128), lambda c, b: (0, c)), pl.BlockSpec((128, 128), lambda c, b: (0, 0))],
        out_specs=blk,
        out_shape=jax.ShapeDtypeStruct((n, CONV_DIM), BF16),
        compiler_params=_params(("parallel", "parallel")),
    )(bg, cg, hc, conv_w, gain, gmat)


def _conv_bwd(dz, bg, cg, hc, conv_w, gain, gmat, lp):
    n = bg.shape[0]
    nb = n // lp

    def body(dz_ref, bg_ref, cg_ref, hc_ref, w_ref, g_ref, gm_ref,
             dbg_ref, dcg_ref, dhc_ref, dw_ref, dgain_ref):
        b = pl.program_id(1)

        @pl.when(b == 0)
        def _():
            dw_ref[...] = jnp.zeros_like(dw_ref)
            dgain_ref[...] = jnp.zeros_like(dgain_ref)

        bgv, cgv, hcv, u, u1, u2, conv = _conv_parts(bg_ref, cg_ref, hc_ref, w_ref)
        gm = gm_ref[...]
        yc = bgv * conv
        r = lax.rsqrt(_group_mean(yc * yc, gm) + EPS)
        yhat = yc * r
        dzv = dz_ref[...].astype(F32)
        dyhat = dzv * g_ref[...]
        dgain_ref[...] += jnp.sum(dzv * yhat, axis=0, keepdims=True)
        dyc = r * (dyhat - yhat * _group_mean(dyhat * yhat, gm))
        dbg_ref[...] = (dyc * conv).astype(BF16)
        dconv = dyc * bgv
        du = (w_ref[2:3, :] * dconv + w_ref[1:2, :] * _shift_rows(dconv, -1)
              + w_ref[0:1, :] * _shift_rows(dconv, -2))
        dcg_ref[...] = (du * hcv).astype(BF16)
        dhc_ref[...] = (du * cgv).astype(BF16)
        dw_ref[0:1, :] += jnp.sum(dconv * u2, axis=0, keepdims=True)
        dw_ref[1:2, :] += jnp.sum(dconv * u1, axis=0, keepdims=True)
        dw_ref[2:3, :] += jnp.sum(dconv * u, axis=0, keepdims=True)

    blk = pl.BlockSpec((lp, 128), lambda c, b: (b, c))
    wspec = pl.BlockSpec((3, 128), lambda c, b: (0, c))
    gspec = pl.BlockSpec((1, 128), lambda c, b: (0, c))
    return pl.pallas_call(
        body, name="conv_bwd", grid=(CONV_DIM // 128, nb),
        in_specs=[blk, blk, blk, blk, wspec, gspec, pl.BlockSpec((128, 128), lambda c, b: (0, 0))],
        out_specs=[blk, blk, blk, wspec, gspec],
        out_shape=[jax.ShapeDtypeStruct((n, CONV_DIM), BF16)] * 3
        + [jax.ShapeDtypeStruct((3, CONV_DIM), F32), jax.ShapeDtypeStruct((1, CONV_DIM), F32)],
        compiler_params=_params(("parallel", "arbitrary")),
    )(dz, bg, cg, hc, conv_w, gain, gmat)


KEY_MASKED = 1e30
ONE_LANE = 24


def _scan_steps(rows):
    s, out = 1, []
    while s < rows:
        out.append(s)
        s *= 2
    return out


def _fgate_fwd(fg, b_f, lp):
    n = fg.shape[0]
    nb = n // lp

    def body(fg_ref, b_ref, ka_ref, qa_ref):
        x = fg_ref[...] + b_ref[...]
        logf = jnp.minimum(x, 0.0) - jnp.log(1.0 + jnp.exp(-jnp.abs(x)))
        t = lax.broadcasted_iota(jnp.int32, (lp, 128), 0)
        lane = lax.broadcasted_iota(jnp.int32, (lp, 128), 1)
        f = jnp.where((t >= PAD) & (lane < N_HEADS), logf, 0.0)
        for s in _scan_steps(lp):
            f = f + _shift_rows(f, s)
        hi = f.astype(BF16).astype(F32)
        rest = f - hi
        mid = rest.astype(BF16).astype(F32)
        lo = (rest - mid).astype(BF16).astype(F32)
        ones = jnp.where((lane >= ONE_LANE) & (lane < ONE_LANE + 3), 1.0, 0.0)
        hi_key = jnp.where((t < PAD) & (lane < N_HEADS), KEY_MASKED, hi)
        ka_ref[...] = (hi_key + pltpu.roll(mid, 8, 1) + pltpu.roll(lo, 16, 1) + ones).astype(BF16)
        for h in range(N_HEADS):
            minus = jnp.where((lane == h) | (lane == 8 + h) | (lane == 16 + h), -1.0, 0.0)
            terms = (jnp.where(lane == ONE_LANE, pltpu.roll(hi, ONE_LANE - h, 1), 0.0)
                     + jnp.where(lane == ONE_LANE + 1, pltpu.roll(mid, ONE_LANE + 1 - h, 1), 0.0)
                     + jnp.where(lane == ONE_LANE + 2, pltpu.roll(lo, ONE_LANE + 2 - h, 1), 0.0))
            qa_ref[:, 128 * h:128 * (h + 1)] = (minus + terms).astype(BF16)

    return pl.pallas_call(
        body, name="fgate_fwd", grid=(nb,),
        in_specs=[pl.BlockSpec((lp, 128), lambda b: (b, 0)), pl.BlockSpec((1, 128), lambda b: (0, 0))],
        out_specs=[pl.BlockSpec((lp, 128), lambda b: (b, 0)), pl.BlockSpec((lp, N_HEADS * 128), lambda b: (b, 0))],
        out_shape=[jax.ShapeDtypeStruct((n, 128), BF16), jax.ShapeDtypeStruct((n, N_HEADS * 128), BF16)],
        compiler_params=_params(("parallel",)),
    )(fg, b_f)


def _fgate_bwd(dka, dfr, fg, b_f, lp):
    n = fg.shape[0]
    nb = n // lp

    def body(dka_ref, dfr_ref, fg_ref, b_ref, dfg_ref, db_ref):
        b = pl.program_id(0)

        @pl.when(b == 0)
        def _():
            db_ref[...] = jnp.zeros_like(db_ref)

        wide = jnp.concatenate([dfr_ref[0], jnp.zeros((128 - N_HEADS, lp), F32)], axis=0)
        t = lax.broadcasted_iota(jnp.int32, (lp, 128), 0)
        lane = lax.broadcasted_iota(jnp.int32, (lp, 128), 1)
        d = jnp.where(lane < N_HEADS, dka_ref[...], 0.0) + wide.T
        for s in _scan_steps(lp):
            d = d + _shift_rows(d, -s)
        x = fg_ref[...] + b_ref[...]
        dx = jnp.where((t >= PAD) & (lane < N_HEADS), d * _sigmoid(-x), 0.0)
        dfg_ref[...] = dx
        db_ref[...] += jnp.sum(dx, axis=0, keepdims=True)

    return pl.pallas_call(
        body, name="fgate_bwd", grid=(nb,),
        in_specs=[pl.BlockSpec((lp, 128), lambda b: (b, 0)), pl.BlockSpec((1, N_HEADS, lp), lambda b: (b, 0, 0)),
                  pl.BlockSpec((lp, 128), lambda b: (b, 0)), pl.BlockSpec((1, 128), lambda b: (0, 0))],
        out_specs=[pl.BlockSpec((lp, 128), lambda b: (b, 0)), pl.BlockSpec((1, 128), lambda b: (0, 0))],
        out_shape=[jax.ShapeDtypeStruct((n, 128), F32), jax.ShapeDtypeStruct((1, 128), F32)],
        compiler_params=_params(("arbitrary",)),
    )(dka, dfr, fg, b_f)


def _head_masks():
    lane = lax.broadcasted_iota(jnp.int32, (1, 128), 1)
    return lane < HEAD_DIM


def _stack_heads(x2, first):
    zero = jnp.zeros_like(x2)
    return jnp.concatenate([jnp.where(first, x2, zero), jnp.where(first, zero, x2)], axis=0)


def _stack_heads_lanes(xt):
    r = lax.broadcasted_iota(jnp.int32, xt.shape, 0)
    zero = jnp.zeros_like(xt)
    return jnp.concatenate([jnp.where(r < HEAD_DIM, xt, zero), jnp.where(r < HEAD_DIM, zero, xt)], axis=1)


def _pair_cols(col0, col1, first):
    return jnp.where(first, col0, col1)


def _pair_rows(row0, row1):
    r = lax.broadcasted_iota(jnp.int32, (128, TQ), 0)
    return jnp.where(r < HEAD_DIM, row0, row1)


def _query_side(q_ref, qa_ref, p, first):
    q2 = q_ref[:, 128 * p:128 * (p + 1)] * 0.125
    zero = jnp.zeros_like(q2)
    top = jnp.concatenate([jnp.where(first, q2, zero), qa_ref[:, 128 * (2 * p):128 * (2 * p + 1)]], axis=1)
    bot = jnp.concatenate([jnp.where(first, zero, q2), qa_ref[:, 128 * (2 * p + 1):128 * (2 * p + 2)]], axis=1)
    return jnp.concatenate([top, bot], axis=0)


def _key_chunks(lp):
    return (lp + TK - 1) // TK


def _chunk_mask(i, c, tk):
    r = lax.broadcasted_iota(jnp.int32, (tk, 2 * TQ), 0)
    col = lax.broadcasted_iota(jnp.int32, (tk, 2 * TQ), 1)
    return (c * TK + r) <= (i * TQ + (col & (TQ - 1)))


def _causal_sweep(i, step, init):
    per = TK // TQ
    last = i // per
    carry = lax.fori_loop(0, last, lambda c, carry: step(c, carry, False, TK), init)
    tails = [lambda carry, r=r: step(last, carry, True, TQ * (r + 1)) for r in range(per)]
    return lax.switch(i % per, tails, carry)


def _transpose_bf16(x):
    return x.astype(F32).T.astype(BF16)


def _attn_fwd(q, qa, k, v, ka, gain, zc, w_out, h, lp):
    n = q.shape[0]
    nb = n // lp
    nq = lp // TQ
    lpp = _key_chunks(lp) * TK

    def body(q_ref, qa_ref, k_ref, v_ref, ka_ref, g_ref, zc_ref, w_ref, h_ref,
             z_ref, o_ref, lse_ref, hout_ref, kx_scr, vt_scr):
        i = pl.program_id(1)
        first = _head_masks()

        @pl.when(i == 0)
        def _():
            if lpp > lp:
                kx_scr[lp:lpp, :] = jnp.zeros((lpp - lp, 2 * ATTN_DIM), BF16)
                vt_scr[:, lp:lpp] = jnp.zeros((ATTN_DIM, lpp - lp), BF16)
            for p in range(N_PAIRS):
                kx_scr[0:lp, 256 * p:256 * p + 128] = k_ref[:, 128 * p:128 * (p + 1)]
                kx_scr[0:lp, 256 * p + 128:256 * (p + 1)] = ka_ref[...]
            vt_scr[:, 0:lp] = _transpose_bf16(v_ref[...])

        rhs_t = [_transpose_bf16(_query_side(q_ref, qa_ref, p, first)) for p in range(N_PAIRS)]

        def step(c, carry, masked, tk):
            koff = pl.multiple_of(c * TK, TK)
            valid = _chunk_mask(i, c, tk) if masked else None
            new = []
            for p in range(N_PAIRS):
                m, l, acc = carry[p]
                st = _dot(kx_scr[pl.ds(koff, tk), 256 * p:256 * (p + 1)], rhs_t[p])
                if masked:
                    st = jnp.where(valid, st, NEG)
                m_new = jnp.maximum(m, jnp.max(st, axis=0, keepdims=True))
                pt = jnp.exp(st - m_new)
                alpha = jnp.exp(m - m_new)
                l = alpha * l + jnp.sum(pt, axis=0, keepdims=True)
                pb = pt.astype(BF16)
                vt = _stack_heads_lanes(vt_scr[128 * p:128 * (p + 1), pl.ds(koff, tk)])
                pv = _dot(vt, jnp.concatenate([pb[:, 0:TQ], pb[:, TQ:]], axis=0))
                acc = acc * _pair_rows(alpha[:, 0:TQ], alpha[:, TQ:]) + pv
                new.append((m_new, l, acc))
            return tuple(new)

        init = tuple((jnp.full((1, 2 * TQ), NEG, F32), jnp.zeros((1, 2 * TQ), F32), jnp.zeros((128, TQ), F32))
                     for _ in range(N_PAIRS))
        final = _causal_sweep(i, step, init)

        row = lax.broadcasted_iota(jnp.int32, (TQ, 128), 0)
        real = (i * TQ + row) >= PAD
        hout = h_ref[...] + _dot(zc_ref[...], w_ref[0:CONV_DIM, :])
        for p in range(N_PAIRS):
            m, l, acc = final[p]
            inv = 1.0 / l
            ot = acc * _pair_rows(inv[:, 0:TQ], inv[:, TQ:])
            sq = ot * ot
            r0 = lax.rsqrt(jnp.sum(sq[0:HEAD_DIM], axis=0, keepdims=True) * (1.0 / HEAD_DIM) + EPS)
            r1 = lax.rsqrt(jnp.sum(sq[HEAD_DIM:], axis=0, keepdims=True) * (1.0 / HEAD_DIM) + EPS)
            cols = slice(128 * p, 128 * (p + 1))
            o_ref[:, cols] = jnp.where(real, ot.T, 0.0).astype(BF16)
            z = (jnp.where(real, (ot * _pair_rows(r0, r1)).T, 0.0) * g_ref[:, cols]).astype(BF16)
            z_ref[:, cols] = z
            hout = hout + _dot(z, w_ref[CONV_DIM + 128 * p:CONV_DIM + 128 * (p + 1), :])
            lse = m + jnp.log(l)
            lse_ref[0, 2 * p:2 * p + 1, :] = lse[:, 0:TQ]
            lse_ref[0, 2 * p + 1:2 * p + 2, :] = lse[:, TQ:]
        hout_ref[...] = hout

    qblk = pl.BlockSpec((TQ, ATTN_DIM), lambda b, i: (b * nq + i, 0))
    qablk = pl.BlockSpec((TQ, N_HEADS * 128), lambda b, i: (b * nq + i, 0))
    seq = pl.BlockSpec((lp, ATTN_DIM), lambda b, i: (b, 0))
    rowblk = pl.BlockSpec((1, N_HEADS, TQ), lambda b, i: (b, 0, i))
    hblk = pl.BlockSpec((TQ, D_MODEL), lambda b, i: (b * nq + i, 0))
    return pl.pallas_call(
        body, name="attn_fwd", grid=(nb, nq),
        in_specs=[qblk, qablk, seq, seq, pl.BlockSpec((lp, 128), lambda b, i: (b, 0)),
                  pl.BlockSpec((1, ATTN_DIM), lambda b, i: (0, 0)), qblk,
                  pl.BlockSpec((D_MODEL, D_MODEL), lambda b, i: (0, 0)), hblk],
        out_specs=[qblk, qblk, rowblk, hblk],
        out_shape=[jax.ShapeDtypeStruct((n, ATTN_DIM), BF16), jax.ShapeDtypeStruct((n, ATTN_DIM), BF16),
                   jax.ShapeDtypeStruct((nb, N_HEADS, lp), F32), jax.ShapeDtypeStruct((n, D_MODEL), F32)],
        scratch_shapes=[pltpu.VMEM((lpp, 2 * ATTN_DIM), BF16), pltpu.VMEM((ATTN_DIM, lpp), BF16)],
        compiler_params=_params(("parallel", "arbitrary")),
    )(q, qa, k, v, ka, gain, zc, w_out, h)


def _attn_bwd(dz, q, qa, k, v, ka, o, lse, gain, lp, exchange=()):
    n = q.shape[0]
    nb = n // lp
    nq = lp // TQ
    lpp = _key_chunks(lp) * TK
    nw = len(exchange)

    def body(*refs):
        ((dz_ref, q_ref, qa_ref, k_ref, v_ref, ka_ref, o_ref, lse_ref, g_ref), xin,
         (dq_ref, dk_ref, dv_ref, dka_ref, dfr_ref, dgain_ref), xout,
         (kx_scr, vx_scr, kt_scr, dkx_scr, dvx_scr), sems) = _split_refs(refs, 9, nw, 6, 5)
        b = pl.program_id(0)
        i = pl.program_id(1)
        first = _head_masks()
        if nw:
            comm = _Exchange(xin, xout, sems)
            pl.when((b == 0) & (i == 0))(comm.start)

        @pl.when((b == 0) & (i == 0))
        def _():
            dgain_ref[...] = jnp.zeros_like(dgain_ref)

        @pl.when(i == 0)
        def _():
            if lpp > lp:
                kx_scr[lp:lpp, :] = jnp.zeros((lpp - lp, 2 * ATTN_DIM), BF16)
                vx_scr[lp:lpp, :] = jnp.zeros((lpp - lp, ATTN_DIM), BF16)
                kt_scr[:, lp:lpp] = jnp.zeros((ATTN_DIM, lpp - lp), BF16)
            for p in range(N_PAIRS):
                kx_scr[0:lp, 256 * p:256 * p + 128] = k_ref[:, 128 * p:128 * (p + 1)]
                kx_scr[0:lp, 256 * p + 128:256 * (p + 1)] = ka_ref[...]
            vx_scr[0:lp, :] = v_ref[...]
            kt_scr[:, 0:lp] = _transpose_bf16(k_ref[...])
            dkx_scr[...] = jnp.zeros_like(dkx_scr)
            dvx_scr[...] = jnp.zeros_like(dvx_scr)

        rhs, rhs_t, lses, dos, dos_t, deltas = [], [], [], [], [], []
        for p in range(N_PAIRS):
            cols = slice(128 * p, 128 * (p + 1))
            side = _query_side(q_ref, qa_ref, p, first)
            rhs.append(side)
            rhs_t.append(_transpose_bf16(side))
            lses.append(jnp.concatenate([lse_ref[0, 2 * p:2 * p + 1, :], lse_ref[0, 2 * p + 1:2 * p + 2, :]], axis=1))
            ov = o_ref[:, cols].astype(F32)
            dzv = dz_ref[:, cols].astype(F32)
            gv = g_ref[:, cols]
            sq = ov * ov
            ms0 = jnp.sum(jnp.where(first, sq, 0.0), axis=1, keepdims=True) * (1.0 / HEAD_DIM)
            ms1 = jnp.sum(jnp.where(first, 0.0, sq), axis=1, keepdims=True) * (1.0 / HEAD_DIM)
            r = _pair_cols(lax.rsqrt(ms0 + EPS), lax.rsqrt(ms1 + EPS), first)
            ohat = ov * r
            dyhat = dzv * gv
            dgain_ref[:, cols] += jnp.sum(dzv * ohat, axis=0, keepdims=True)
            pr = dyhat * ohat
            mean0 = jnp.sum(jnp.where(first, pr, 0.0), axis=1, keepdims=True) * (1.0 / HEAD_DIM)
            mean1 = jnp.sum(jnp.where(first, 0.0, pr), axis=1, keepdims=True) * (1.0 / HEAD_DIM)
            do = r * (dyhat - ohat * _pair_cols(mean0, mean1, first))
            ddt = (do * ov).T
            deltas.append(jnp.concatenate([jnp.sum(ddt[0:HEAD_DIM], axis=0, keepdims=True),
                                           jnp.sum(ddt[HEAD_DIM:], axis=0, keepdims=True)], axis=1))
            do_st = _stack_heads(do.astype(BF16), first)
            dos.append(do_st)
            dos_t.append(_transpose_bf16(do_st))

        def step(c, carry, masked, tk):
            koff = pl.multiple_of(c * TK, TK)
            valid = _chunk_mask(i, c, tk) if masked else None
            new = []
            for p in range(N_PAIRS):
                dqt, dfq = carry[p]
                ext = slice(256 * p, 256 * (p + 1))
                cols = slice(128 * p, 128 * (p + 1))
                st = _dot(kx_scr[pl.ds(koff, tk), ext], rhs_t[p])
                if masked:
                    st = jnp.where(valid, st, NEG)
                pt = jnp.exp(st - lses[p])
                dpt = _dot(vx_scr[pl.ds(koff, tk), cols], dos_t[p])
                dst = pt * (dpt - deltas[p])
                dsb = dst.astype(BF16)
                dfq = dfq + jnp.sum(dsb.astype(F32), axis=0, keepdims=True)
                dkx_scr[pl.ds(koff, tk), ext] += _dot(dsb, rhs[p])
                dvx_scr[pl.ds(koff, tk), cols] += _dot(pt.astype(BF16), dos[p])
                kt = _stack_heads_lanes(kt_scr[cols, pl.ds(koff, tk)])
                dqt = dqt + _dot(kt, jnp.concatenate([dsb[:, 0:TQ], dsb[:, TQ:]], axis=0))
                new.append((dqt, dfq))
            return tuple(new)

        init = tuple((jnp.zeros((128, TQ), F32), jnp.zeros((1, 2 * TQ), F32)) for _ in range(N_PAIRS))
        final = _causal_sweep(i, step, init)

        for p in range(N_PAIRS):
            dqt, dfq = final[p]
            dq_ref[:, 128 * p:128 * (p + 1)] = (dqt.T * 0.125).astype(BF16)
            dfr_ref[0, 2 * p:2 * p + 1, :] = dfq[:, 0:TQ]
            dfr_ref[0, 2 * p + 1:2 * p + 2, :] = dfq[:, TQ:]

        @pl.when(i == nq - 1)
        def _():
            dka = jnp.zeros((lp, 128), F32)
            for p in range(N_PAIRS):
                dk_ref[:, 128 * p:128 * (p + 1)] = dkx_scr[0:lp, 256 * p:256 * p + 128].astype(BF16)
                dka = dka + dkx_scr[0:lp, 256 * p + 128:256 * (p + 1)]
            dka_ref[...] = dka
            dv_ref[...] = dvx_scr[0:lp, :].astype(BF16)

        if nw:
            pl.when((b == nb - 1) & (i == nq - 1))(comm.finish)

    qblk = pl.BlockSpec((TQ, ATTN_DIM), lambda b, i: (b * nq + i, 0))
    qablk = pl.BlockSpec((TQ, N_HEADS * 128), lambda b, i: (b * nq + i, 0))
    seq = pl.BlockSpec((lp, ATTN_DIM), lambda b, i: (b, 0))
    kaseq = pl.BlockSpec((lp, 128), lambda b, i: (b, 0))
    rowblk = pl.BlockSpec((1, N_HEADS, TQ), lambda b, i: (b, 0, i))
    gspec = pl.BlockSpec((1, ATTN_DIM), lambda b, i: (0, 0))
    return pl.pallas_call(
        body, name="attn_bwd", grid=(nb, nq),
        in_specs=[qblk, qblk, qablk, seq, seq, kaseq, qblk, rowblk, gspec] + [ANY] * nw,
        out_specs=[qblk, seq, seq, kaseq, rowblk, gspec] + [ANY] * nw,
        out_shape=[jax.ShapeDtypeStruct((n, ATTN_DIM), BF16), jax.ShapeDtypeStruct((n, ATTN_DIM), BF16),
                   jax.ShapeDtypeStruct((n, ATTN_DIM), BF16), jax.ShapeDtypeStruct((n, 128), F32),
                   jax.ShapeDtypeStruct((nb, N_HEADS, lp), F32), jax.ShapeDtypeStruct((1, ATTN_DIM), F32)]
        + [jax.ShapeDtypeStruct(a.shape, a.dtype) for a in exchange],
        scratch_shapes=[pltpu.VMEM((lpp, 2 * ATTN_DIM), BF16), pltpu.VMEM((lpp, ATTN_DIM), BF16),
                        pltpu.VMEM((ATTN_DIM, lpp), BF16), pltpu.VMEM((lpp, 2 * ATTN_DIM), F32),
                        pltpu.VMEM((lpp, ATTN_DIM), F32)] + (_comm_sems(nw) if nw else []),
        compiler_params=_params(("arbitrary", "arbitrary")),
    )(dz, q, qa, k, v, ka, o, lse, gain, *exchange)


def _adamw(parts, w, m, v, name):
    s_parts, r, c = parts.shape
    tr = r
    for t in (256, 128, 64, 32, 16):
        if r % t == 0 and r > t:
            tr = t
            break

    def body(p_ref, w_ref, m_ref, v_ref, g_ref, d_ref, nm_ref, nv_ref):
        g = p_ref[0].astype(F32)
        for s in range(1, s_parts):
            g = g + p_ref[s].astype(F32)
        nm = ADAM_B1 * m_ref[...] + (1.0 - ADAM_B1) * g
        nv = ADAM_B2 * v_ref[...] + (1.0 - ADAM_B2) * (g * g)
        m_hat = nm / (1.0 - ADAM_B1 ** ADAM_STEP)
        v_hat = nv / (1.0 - ADAM_B2 ** ADAM_STEP)
        g_ref[...] = g
        d_ref[...] = -ADAM_LR * (m_hat / (jnp.sqrt(v_hat) + ADAM_EPS) + ADAM_WD * w_ref[...])
        nm_ref[...] = nm
        nv_ref[...] = nv

    blk = pl.BlockSpec((tr, c), lambda i: (i, 0))
    return pl.pallas_call(
        body, name=name, grid=(r // tr,),
        in_specs=[pl.BlockSpec((s_parts, tr, c), lambda i: (0, i, 0)), blk, blk, blk],
        out_specs=[blk] * 4,
        out_shape=[jax.ShapeDtypeStruct((r, c), F32)] * 4,
        compiler_params=_params(("parallel",)),
    )(parts, w, m, v)


def _sum_parts(parts, name):
    s_parts, r, c = parts.shape

    def body(p_ref, out_ref):
        acc = p_ref[0]
        for s in range(1, s_parts):
            acc = acc + p_ref[s]
        out_ref[...] = acc

    return pl.pallas_call(
        body, name=name, out_shape=jax.ShapeDtypeStruct((r, c), F32),
        in_specs=[pl.BlockSpec(memory_space=pltpu.VMEM)], out_specs=pl.BlockSpec(memory_space=pltpu.VMEM),
    )(parts)


SMALL_ROWS = 184
LOSS_ROW = 181


def _pack_small(d_gains, d_gc, d_ga, d_bf, d_conv, d_meta, loss_part):
    rows = [g.reshape(8, 128) for g in d_gains]
    rows += [d_gc.reshape(4, 128), d_ga.reshape(4, 128), d_bf.reshape(1, 128)]
    rows += [d_conv.reshape(12, 128), d_meta.reshape(128, 128), jnp.pad(loss_part, ((0, 0), (0, 127)))]
    packed = jnp.concatenate(rows, axis=0)
    return jnp.pad(packed, ((0, SMALL_ROWS - packed.shape[0]), (0, 0)))


def kernel(x, meta_tokens, ffn1_norm, ffn1_w_gu, ffn1_w_down, mix_norm, w_in, conv_w, b_f, out_norm_conv, out_norm_attn, w_out, ffn2_norm, ffn2_w_gu, ffn2_w_down, final_norm, loss_target, m_meta_tokens, m_ffn1_norm, m_ffn1_w_gu, m_ffn1_w_down, m_mix_norm, m_w_in, m_conv_w, m_b_f, m_out_norm_conv, m_out_norm_attn, m_w_out, m_ffn2_norm, m_ffn2_w_gu, m_ffn2_w_down, m_final_norm, v_meta_tokens, v_ffn1_norm, v_ffn1_w_gu, v_ffn1_w_down, v_mix_norm, v_w_in, v_conv_w, v_b_f, v_out_norm_conv, v_out_norm_attn, v_w_out, v_ffn2_norm, v_ffn2_w_gu, v_ffn2_w_down, v_final_norm):
    nb, seq, _ = x.shape
    lp = PAD + N_META + seq
    me = 4 * lax.axis_index("x") + 2 * lax.axis_index("y") + lax.axis_index("c")
    shard_gu = D_FF // 4
    shard_d = D_FF // N_DEV

    small_in = jnp.concatenate(
        [meta_tokens, jnp.pad(conv_w[0], ((0, 0), (0, 128 - conv_w.shape[2]))), jnp.zeros((5, 128), F32)], axis=0)
    wgu1_8, wd1_8, small_8 = _all_gather(
        [ffn1_w_gu[0].T.astype(BF16), ffn1_w_down[0].astype(BF16), small_in], "gather_ffn1")
    meta_full = small_8[:, 0:N_META, :].transpose(1, 0, 2).reshape(N_META, D_MODEL)
    conv_full = small_8[:, N_META:N_META + 3, 0:CONV_DIM // N_DEV].transpose(1, 0, 2).reshape(3, CONV_DIM)
    wgu1 = wgu1_8.reshape(W_GU_SHAPE)
    wd1 = wd1_8.reshape(W_D_SHAPE)
    b_f_row = jnp.pad(b_f, ((0, 0), (0, 128 - N_HEADS)))
    gmat = _group_matrix()

    x2d = x.reshape(nb * seq, D_MODEL)
    later = [w_in[0].T.astype(BF16), w_out[0].astype(BF16), ffn2_w_gu[0].T.astype(BF16), ffn2_w_down[0].astype(BF16)]
    h1, n1, gate1, up1, win_8, wout_8, wgu2_8, wd2_8 = _ffn1_fwd(x2d, meta_full, lp, ffn1_norm, wgu1, wd1, later)
    wgu2 = wgu2_8.reshape(W_GU_SHAPE)
    wd2 = wd2_8.reshape(W_D_SHAPE)
    w_in_full = jnp.pad(win_8.reshape(IN_DIM, D_MODEL), ((0, IN_PAD - IN_DIM), (0, 0)))
    w_out_full = wout_8.reshape(D_MODEL, D_MODEL)

    bg, cg, hc, q, k, v, fg = _inproj_fwd(h1, mix_norm, w_in_full)
    zc = _conv_fwd(bg, cg, hc, conv_full, out_norm_conv, gmat, lp)
    ka, qa = _fgate_fwd(fg, b_f_row, lp)
    za, o, lse, h2 = _attn_fwd(q, qa, k, v, ka, out_norm_attn, zc, w_out_full, h1, lp)
    dh3, n3, gate2, up2, loss_part, d_final = _ffn2_fwd_loss(
        h2, ffn2_norm, wgu2, wd2, final_norm.reshape(1, D_MODEL), loss_target.reshape(nb * seq, D_MODEL), lp)

    dgu2, dwd2 = _ffn_bwd_act_wd(dh3, gate2, up2, wd2, "ffn2_bwd_act")
    dh2, d_ffn2 = _ffn_bwd_in(dh3, h2, ffn2_norm, dgu2, wgu2, "ffn2_bwd_in")
    dwgu2 = _ffn_bwd_wgu(n3, dgu2, "ffn2_bwd_wgu")
    dzc, dza, dwout = _outproj_bwd(dh2, zc, za, w_out_full)
    send_a = [dwgu2.reshape(N_DEV, shard_gu, D_MODEL), dwd2.reshape(N_DEV, shard_d, D_MODEL),
              dwout.reshape(N_DEV, D_MODEL // N_DEV, D_MODEL)]
    dq, dk, dv, dka, dfr, d_ga, p_wgu2, p_wd2, p_wout = _attn_bwd(
        dza, q, qa, k, v, ka, o, lse, out_norm_attn, lp, exchange=send_a)
    dfg, d_bf = _fgate_bwd(dka, dfr, fg, b_f_row, lp)
    dbg, dcg, dhc, d_conv, d_gc = _conv_bwd(dzc, bg, cg, hc, conv_full, out_norm_conv, gmat, lp)
    dh1, dwin, d_mix = _inproj_bwd([dbg, dcg, dhc, dq, dk, dv], dfg, dh2, h1, mix_norm, w_in_full)
    dwin_8 = dwin[0:IN_DIM].reshape(N_DEV, IN_DIM // N_DEV, D_MODEL)
    dgu1, dwd1, p_win = _ffn_bwd_act_wd(dh1, gate1, up1, wd1, "ffn1_bwd_act", exchange=[dwin_8])
    dwgu1, p_wd1 = _ffn_bwd_wgu(n1, dgu1, "ffn1_bwd_wgu", exchange=[dwd1.reshape(N_DEV, shard_d, D_MODEL)])
    own = dwgu1.reshape(N_DEV, shard_gu, D_MODEL)
    (got,) = _pair_exchange([own], "pair_exchange_ffn1")
    chip_sum = _pair_sum(own, got, "pair_sum_wgu1")
    dh0, d_ffn1, p_wgu1 = _ffn_bwd_in(dh1, None, ffn1_norm, dgu1, wgu1, "ffn1_bwd_in",
                                      tokens=(x2d, meta_full, lp), exchange=[chip_sum])

    dh0 = dh0.reshape(nb, lp, D_MODEL)
    grad_x = dh0[:, PAD + N_META:, :]
    d_meta = jnp.sum(dh0[:, PAD:PAD + N_META, :], axis=0)

    small = _pack_small([d_ffn1, d_mix, d_ffn2, d_final], d_gc, d_ga, d_bf, d_conv, d_meta, loss_part)
    (small_all,) = _all_gather([small], "gather_small_grads")
    small_sum = _sum_parts(small_all, "sum_small_grads")
    g_ffn1n, g_mixn, g_ffn2n, g_finaln = (small_sum[8 * t:8 * t + 8].reshape(1, D_MODEL) for t in range(4))
    g_gc = small_sum[32:36].reshape(1, CONV_DIM)
    g_ga = small_sum[36:40].reshape(1, ATTN_DIM)
    g_bf = small_sum[40:41, 0:N_HEADS]
    g_conv_full = small_sum[41:53].reshape(3, CONV_DIM)
    g_meta_full = small_sum[53:181].reshape(N_META, D_MODEL)
    g_conv = lax.dynamic_slice_in_dim(g_conv_full, me * (CONV_DIM // N_DEV), CONV_DIM // N_DEV, axis=1)
    g_meta = lax.dynamic_slice_in_dim(g_meta_full, me * (D_MODEL // N_DEV), D_MODEL // N_DEV, axis=1)

    weights = {
        "meta_tokens": (g_meta[None], meta_tokens, m_meta_tokens, v_meta_tokens),
        "ffn1_norm": (g_ffn1n[None], ffn1_norm, m_ffn1_norm, v_ffn1_norm),
        "ffn1_w_gu": (p_wgu1, ffn1_w_gu[0].T, m_ffn1_w_gu[0].T, v_ffn1_w_gu[0].T),
        "ffn1_w_down": (p_wd1, ffn1_w_down[0], m_ffn1_w_down[0], v_ffn1_w_down[0]),
        "mix_norm": (g_mixn[None], mix_norm, m_mix_norm, v_mix_norm),
        "w_in": (p_win, w_in[0].T, m_w_in[0].T, v_w_in[0].T),
        "conv_w": (g_conv[None], conv_w[0], m_conv_w[0], v_conv_w[0]),
        "b_f": (g_bf[None], b_f, m_b_f, v_b_f),
        "out_norm_conv": (g_gc[None], out_norm_conv, m_out_norm_conv, v_out_norm_conv),
        "out_norm_attn": (g_ga[None], out_norm_attn, m_out_norm_attn, v_out_norm_attn),
        "w_out": (p_wout, w_out[0], m_w_out[0], v_w_out[0]),
        "ffn2_norm": (g_ffn2n[None], ffn2_norm, m_ffn2_norm, v_ffn2_norm),
        "ffn2_w_gu": (p_wgu2, ffn2_w_gu[0].T, m_ffn2_w_gu[0].T, v_ffn2_w_gu[0].T),
        "ffn2_w_down": (p_wd2, ffn2_w_down[0], m_ffn2_w_down[0], v_ffn2_w_down[0]),
        "final_norm": (g_finaln[None], final_norm.reshape(1, D_MODEL), m_final_norm.reshape(1, D_MODEL),
                       v_final_norm.reshape(1, D_MODEL)),
    }
    shapes = {"meta_tokens": meta_tokens.shape, "ffn1_norm": ffn1_norm.shape, "ffn1_w_gu": ffn1_w_gu.shape,
              "ffn1_w_down": ffn1_w_down.shape, "mix_norm": mix_norm.shape, "w_in": w_in.shape,
              "conv_w": conv_w.shape, "b_f": b_f.shape, "out_norm_conv": out_norm_conv.shape,
              "out_norm_attn": out_norm_attn.shape, "w_out": w_out.shape, "ffn2_norm": ffn2_norm.shape,
              "ffn2_w_gu": ffn2_w_gu.shape, "ffn2_w_down": ffn2_w_down.shape, "final_norm": final_norm.shape}
    grads, deltas, new_m, new_v = [], [], [], []
    for name, (p, w, m, vv) in weights.items():
        g, d, nm, nv = _adamw(p, w, m, vv, "adamw_" + name)
        if name in ("ffn1_w_gu", "ffn2_w_gu", "w_in"):
            g, d, nm, nv = g.T, d.T, nm.T, nv.T
        shape = shapes[name]
        grads.append(g.reshape(shape))
        deltas.append(d.reshape(shape))
        new_m.append(nm.reshape(shape))
        new_v.append(nv.reshape(shape))

    loss = small_sum[LOSS_ROW, 0]
    return (loss, grad_x, *grads, *deltas, *new_m, *new_v)
```

```python
import jax
import jax.numpy as jnp
from jax import lax
from jax.experimental import pallas as pl
from jax.experimental.pallas import tpu as pltpu

F32 = jnp.float32
BF16 = jnp.bfloat16

N_DEV = 8
D_MODEL = 1024
N_META = 16
PAD = 128 - N_META
CONV_DIM = 512
ATTN_DIM = 512
HEAD_DIM = 64
N_HEADS = 8
N_PAIRS = N_HEADS // 2
D_FF = 2816
IN_DIM = 3080
IN_PAD = 3200
IN_MAIN = 3072
N_PIECE = IN_MAIN // 512
EPS = 1e-6
NEG = -1e30
TQ = 128
TK = 512
VMEM_LIMIT = 56 * 1024 * 1024

HID_PIECES = ((0, 1024), (1024, 2048), (2048, D_FF))
ACT_PIECES = tuple((a, min(a + 256, D_FF)) for a in range(0, D_FF, 256))
W_GU_SHAPE = (2, D_FF, D_MODEL)
W_D_SHAPE = (D_FF, D_MODEL)

ADAM_LR = 0.001
ADAM_B1 = 0.9
ADAM_B2 = 0.999
ADAM_EPS = 1e-08
ADAM_WD = 0.01
ADAM_STEP = 10

MESH = pl.DeviceIdType.MESH
ANY = pl.BlockSpec(memory_space=pl.ANY)


def _params(sem=None):
    return pltpu.CompilerParams(dimension_semantics=sem, vmem_limit_bytes=VMEM_LIMIT)


def _row_tile(n, prefer):
    for t in (prefer, 512, 256, 128):
        if t <= n and n % t == 0:
            return t
    raise ValueError(f"no row tile for {n}")


def _resident(shape):
    zeros = (0,) * len(shape)
    return pl.BlockSpec(shape, lambda i: zeros, pipeline_mode=pl.Buffered(1))


def _dot(a, b):
    return jnp.dot(a, b, preferred_element_type=F32)


def _dot_nt(a, b):
    return lax.dot_general(a, b, (((1,), (1,)), ((), ())), preferred_element_type=F32)


def _dot_tn(a, b):
    return lax.dot_general(a, b, (((0,), (0,)), ((), ())), preferred_element_type=F32)


def _rms(x, g):
    r = lax.rsqrt(jnp.mean(x * x, axis=-1, keepdims=True) + EPS)
    xhat = x * r
    return xhat * g, xhat, r


def _rms_bwd(dn, xhat, r, g):
    dxhat = dn * g
    return r * (dxhat - xhat * jnp.mean(dxhat * xhat, axis=-1, keepdims=True))


def _sigmoid(x):
    return 1.0 / (1.0 + jnp.exp(-x))


def _place():
    return lax.axis_index("x"), lax.axis_index("y"), lax.axis_index("c")


def _comm_sems(nw):
    return [pltpu.SemaphoreType.DMA((nw, 7)), pltpu.SemaphoreType.DMA((nw, 7)), pltpu.SemaphoreType.DMA((nw,))]


def _flip(v, bit):
    return 1 - v if bit else v


class _Gather:
    def __init__(self, ins, outs, sems):
        self.ins, self.outs = ins, outs
        self.send, self.recv, self.local = sems
        x, y, c = _place()
        self.c = c
        self.me, self.sibling = (x, y, c), (x, y, 1 - c)
        first = ((x + 1 - c) % 2, (y + c) % 2)
        second = ((x + c) % 2, (y + 1 - c) % 2)
        self.chips = [first, second, (1 - x, 1 - y)]
        self.targets = [first, second, second]

    def _copy(self, w, k, block, to, own=False):
        slot = self.outs[w].at[4 * block[0] + 2 * block[1] + block[2]]
        return pltpu.make_async_remote_copy(
            src_ref=self.ins[w] if own else slot, dst_ref=slot,
            send_sem=self.send.at[w, k], recv_sem=self.recv.at[w, k], device_id=to, device_id_type=MESH)

    def _mine(self, w):
        x, y, c = self.me
        return pltpu.make_async_copy(self.ins[w], self.outs[w].at[4 * x + 2 * y + c], self.local.at[w])

    def _first(self, w):
        return ([self._copy(w, 0, self.me, self.sibling, own=True)]
                + [self._copy(w, 1 + j, self.me, (*self.targets[j], self.c), own=True) for j in range(2)])

    def _relay(self, w):
        return self._copy(w, 3, (*self.chips[0], self.c), (*self.targets[2], self.c))

    def _landed(self, w, j):
        return self._copy(w, 1 + j, (*self.chips[j], self.c), self.me)

    def _passed(self, w):
        return [self._copy(w, 4 + j, (*chip, self.c), self.sibling) for j, chip in enumerate(self.chips)]

    def start(self):
        for w in range(len(self.ins)):
            self._mine(w).start()
        for w in range(len(self.ins)):
            for cp in self._first(w):
                cp.start()

    def relay(self):
        for w in range(len(self.ins)):
            self._landed(w, 0).wait_recv()
            self._relay(w).start()
            self._passed(w)[0].start()

    def forward(self):
        for w in range(len(self.ins)):
            for j in (1, 2):
                self._landed(w, j).wait_recv()
                self._passed(w)[j].start()

    def finish(self):
        from_sibling = [self.chips[1], self.chips[0], self.chips[2]]
        for w in range(len(self.ins)):
            self._copy(w, 0, self.sibling, self.me).wait_recv()
            for j, chip in enumerate(from_sibling):
                self._copy(w, 4 + j, (*chip, 1 - self.c), self.me).wait_recv()
        for w in range(len(self.ins)):
            for cp in self._first(w) + [self._relay(w)] + self._passed(w):
                cp.wait_send()
            self._mine(w).wait()


class _Exchange:
    def __init__(self, ins, outs, sems):
        self.ins, self.outs = ins, outs
        self.send, self.recv, self.local = sems
        self.x, self.y, self.c = _place()
        self.me = 4 * self.x + 2 * self.y + self.c

    def _copy(self, w, k):
        peer = (_flip(self.x, ((k + 1) >> 2) & 1), _flip(self.y, ((k + 1) >> 1) & 1), _flip(self.c, (k + 1) & 1))
        return pltpu.make_async_remote_copy(
            src_ref=self.ins[w].at[4 * peer[0] + 2 * peer[1] + peer[2]], dst_ref=self.outs[w].at[self.me],
            send_sem=self.send.at[w, k], recv_sem=self.recv.at[w, k], device_id=peer, device_id_type=MESH)

    def _mine(self, w):
        return pltpu.make_async_copy(self.ins[w].at[self.me], self.outs[w].at[self.me], self.local.at[w])

    def start(self):
        for w in range(len(self.ins)):
            self._mine(w).start()
            for k in range(N_DEV - 1):
                self._copy(w, k).start()

    def finish(self):
        for w in range(len(self.ins)):
            for k in range(N_DEV - 1):
                self._copy(w, k).wait()
            self._mine(w).wait()


class _PairExchange:
    def __init__(self, ins, outs, sems):
        self.ins, self.outs = ins, outs
        self.send, self.recv, _ = sems
        x, y, self.c = _place()
        self.sibling = (x, y, 1 - self.c)

    def _copy(self, w, t):
        return pltpu.make_async_remote_copy(
            src_ref=self.ins[w].at[2 * t + 1 - self.c], dst_ref=self.outs[w].at[t],
            send_sem=self.send.at[w, t], recv_sem=self.recv.at[w, t], device_id=self.sibling, device_id_type=MESH)

    def start(self):
        for w in range(len(self.ins)):
            for t in range(4):
                self._copy(w, t).start()

    def finish(self):
        for w in range(len(self.ins)):
            for t in range(4):
                self._copy(w, t).wait()


class _ChipExchange:
    def __init__(self, ins, outs, sems):
        self.ins, self.outs = ins, outs
        self.send, self.recv, self.local = sems
        self.x, self.y, self.c = _place()
        self.chip = 2 * self.x + self.y

    def _copy(self, w, k):
        px, py = _flip(self.x, ((k + 1) >> 1) & 1), _flip(self.y, (k + 1) & 1)
        return pltpu.make_async_remote_copy(
            src_ref=self.ins[w].at[2 * px + py], dst_ref=self.outs[w].at[self.chip],
            send_sem=self.send.at[w, k], recv_sem=self.recv.at[w, k], device_id=(px, py, self.c),
            device_id_type=MESH)

    def _mine(self, w):
        return pltpu.make_async_copy(self.ins[w].at[self.chip], self.outs[w].at[self.chip], self.local.at[w])

    def start(self):
        for w in range(len(self.ins)):
            self._mine(w).start()
            for k in range(3):
                self._copy(w, k).start()

    def finish(self):
        for w in range(len(self.ins)):
            for k in range(3):
                self._copy(w, k).wait()
            self._mine(w).wait()


def _split_refs(refs, n_in, n_comm, n_out, n_scr):
    a = n_in
    b = a + n_comm
    c = b + n_out
    d = c + n_comm
    e = d + n_scr
    return refs[:a], refs[a:b], refs[b:c], refs[c:d], refs[d:e], refs[e:]


def _all_gather(xs, name):
    nw = len(xs)

    def body(*refs):
        comm = _Gather(refs[:nw], refs[nw:2 * nw], refs[2 * nw:])
        comm.start()
        comm.relay()
        comm.forward()
        comm.finish()

    return pl.pallas_call(
        body, name=name, in_specs=[ANY] * nw, out_specs=[ANY] * nw,
        out_shape=[jax.ShapeDtypeStruct((N_DEV,) + a.shape, a.dtype) for a in xs],
        scratch_shapes=_comm_sems(nw),
    )(*xs)


def _pair_exchange(xs, name):
    nw = len(xs)

    def body(*refs):
        comm = _PairExchange(refs[:nw], refs[nw:2 * nw], refs[2 * nw:])
        comm.start()
        comm.finish()

    return pl.pallas_call(
        body, name=name, in_specs=[ANY] * nw, out_specs=[ANY] * nw,
        out_shape=[jax.ShapeDtypeStruct((4,) + a.shape[1:], a.dtype) for a in xs],
        scratch_shapes=_comm_sems(nw),
    )(*xs)


def _pair_sum(own, got, name):
    _, r, c = own.shape
    tr = r
    for t in (256, 128, 64, 32, 16):
        if r % t == 0 and r > t:
            tr = t
            break

    def body(own_ref, got_ref, out_ref):
        mine = jnp.where(lax.axis_index("c") == 0, own_ref[:, 0].astype(F32), own_ref[:, 1].astype(F32))
        out_ref[...] = (mine + got_ref[...].astype(F32)).astype(BF16)

    return pl.pallas_call(
        body, name=name, grid=(r // tr,),
        in_specs=[pl.BlockSpec((4, 2, tr, c), lambda i: (0, 0, i, 0)), pl.BlockSpec((4, tr, c), lambda i: (0, i, 0))],
        out_specs=pl.BlockSpec((4, tr, c), lambda i: (0, i, 0)),
        out_shape=jax.ShapeDtypeStruct((4, r, c), BF16),
        compiler_params=_params(("parallel",)),
    )(own.reshape(4, 2, r, c), got)


def _token_spec(k, ksub, nq):
    def index_map(i):
        s = ksub * i + k
        return ((s // nq) * (nq - 1) + jnp.maximum(s % nq, 1) - 1, 0)
    return pl.BlockSpec((128, D_MODEL), index_map)


def _is_lead(i, k, ksub, nq):
    return ((ksub * i + k) % nq) == 0


def _assemble_rows(i, x_refs, meta_ref, nq):
    ksub = len(x_refs)
    lead = jnp.concatenate([jnp.zeros((PAD, D_MODEL), F32), meta_ref[...]], axis=0)
    return jnp.concatenate([jnp.where(_is_lead(i, k, ksub, nq), lead, x_refs[k][...]) for k in range(ksub)], axis=0)


def _swiglu(nb, wgu_ref, wd_ref, gate_ref, up_ref):
    acc = jnp.zeros((nb.shape[0], D_MODEL), F32)
    for a, b in HID_PIECES:
        gate = _dot_nt(nb, wgu_ref[0, a:b, :])
        up = _dot_nt(nb, wgu_ref[1, a:b, :])
        gate_ref[:, a:b] = gate.astype(BF16)
        up_ref[:, a:b] = up.astype(BF16)
        acc = acc + _dot((gate * _sigmoid(gate) * up).astype(BF16), wd_ref[a:b, :])
    return acc


def _ffn1_fwd(x2d, meta, lp, gain, wgu, wd, gather):
    nq = lp // 128
    n = (x2d.shape[0] // (nq - 1)) * nq
    tm = _row_tile(n, 512)
    ksub = tm // 128
    n_i = n // tm
    nw = len(gather)

    def body(*refs):
        x_refs = refs[:ksub]
        (meta_ref, g_ref, wgu_ref, wd_ref), gin, (out_ref, nrm_ref, gate_ref, up_ref), gout, _, sems = \
            _split_refs(refs[ksub:], 4, nw, 4, 0)
        i = pl.program_id(0)
        comm = _Gather(gin, gout, sems)
        pl.when(i == 0)(comm.start)
        pl.when(i == n_i // 2)(comm.relay)
        pl.when(i == max(n_i - 3, n_i // 2))(comm.forward)

        hv = _assemble_rows(i, x_refs, meta_ref, nq)
        y, _, _ = _rms(hv, g_ref[...])
        nb = y.astype(BF16)
        nrm_ref[...] = nb
        out_ref[...] = hv + 0.5 * _swiglu(nb, wgu_ref, wd_ref, gate_ref, up_ref)

        pl.when(i == n_i - 1)(comm.finish)

    rows = pl.BlockSpec((tm, D_MODEL), lambda i: (i, 0))
    hid = pl.BlockSpec((tm, D_FF), lambda i: (i, 0))
    return pl.pallas_call(
        body, name="ffn1_fwd", grid=(n_i,),
        in_specs=[_token_spec(k, ksub, nq) for k in range(ksub)]
        + [pl.BlockSpec((N_META, D_MODEL), lambda i: (0, 0)), pl.BlockSpec((1, D_MODEL), lambda i: (0, 0)),
           _resident(W_GU_SHAPE), _resident(W_D_SHAPE)] + [ANY] * nw,
        out_specs=[rows, rows, hid, hid] + [ANY] * nw,
        out_shape=[jax.ShapeDtypeStruct((n, D_MODEL), F32), jax.ShapeDtypeStruct((n, D_MODEL), BF16),
                   jax.ShapeDtypeStruct((n, D_FF), BF16), jax.ShapeDtypeStruct((n, D_FF), BF16)]
        + [jax.ShapeDtypeStruct((N_DEV,) + a.shape, a.dtype) for a in gather],
        scratch_shapes=_comm_sems(nw),
        compiler_params=_params(("arbitrary",)),
    )(*([x2d] * ksub), meta, gain, wgu, wd, *gather)


def _ffn2_fwd_loss(h, gain, wgu, wd, gfinal, target, lp):
    n = h.shape[0]
    nq = lp // 128
    tm = _row_tile(n, 512)
    ksub = tm // 128
    n_i = n // tm

    def body(*refs):
        t_refs = refs[:ksub]
        h_ref, g_ref, wgu_ref, wd_ref, gf_ref, dh_ref, nrm_ref, gate_ref, up_ref, loss_ref, dgf_ref = refs[ksub:]
        i = pl.program_id(0)

        @pl.when(i == 0)
        def _():
            loss_ref[...] = jnp.zeros_like(loss_ref)
            dgf_ref[...] = jnp.zeros_like(dgf_ref)

        hv = h_ref[...]
        y, _, _ = _rms(hv, g_ref[...])
        nb = y.astype(BF16)
        nrm_ref[...] = nb
        hout = hv + 0.5 * _swiglu(nb, wgu_ref, wd_ref, gate_ref, up_ref)

        gf = gf_ref[...]
        loss = jnp.zeros((1, 1), F32)
        dgf = jnp.zeros((1, D_MODEL), F32)
        for k in range(ksub):
            yk, xhat, r = _rms(hout[128 * k:128 * (k + 1)], gf)
            err = jnp.where(_is_lead(i, k, ksub, nq), 0.0, yk - t_refs[k][...])
            loss = loss + 0.5 * jnp.sum(jnp.sum(err * err, axis=1, keepdims=True), axis=0,
                                        keepdims=True) * (1.0 / D_MODEL)
            dy = err * (1.0 / D_MODEL)
            dh_ref[128 * k:128 * (k + 1), :] = _rms_bwd(dy, xhat, r, gf)
            dgf = dgf + jnp.sum(dy * xhat, axis=0, keepdims=True)
        loss_ref[...] += loss
        dgf_ref[...] += dgf

    rows = pl.BlockSpec((tm, D_MODEL), lambda i: (i, 0))
    hid = pl.BlockSpec((tm, D_FF), lambda i: (i, 0))
    vec = pl.BlockSpec((1, D_MODEL), lambda i: (0, 0))
    return pl.pallas_call(
        body, name="ffn2_fwd_loss", grid=(n_i,),
        in_specs=[_token_spec(k, ksub, nq) for k in range(ksub)]
        + [rows, vec, _resident(W_GU_SHAPE), _resident(W_D_SHAPE), vec],
        out_specs=[rows, rows, hid, hid, pl.BlockSpec((1, 1), lambda i: (0, 0)), vec],
        out_shape=[jax.ShapeDtypeStruct((n, D_MODEL), F32), jax.ShapeDtypeStruct((n, D_MODEL), BF16),
                   jax.ShapeDtypeStruct((n, D_FF), BF16), jax.ShapeDtypeStruct((n, D_FF), BF16),
                   jax.ShapeDtypeStruct((1, 1), F32), jax.ShapeDtypeStruct((1, D_MODEL), F32)],
        compiler_params=_params(("arbitrary",)),
    )(*([target] * ksub), h, gain, wgu, wd, gfinal)


def _ffn_bwd_act_wd(dh_out, gate, up, wd, name, exchange=()):
    n = dh_out.shape[0]
    tm = _row_tile(n, 256)
    n_i = n // tm
    nw = len(exchange)

    def body(*refs):
        (dh_ref, gate_ref, up_ref, wd_ref), xin, (dgu_ref, dw_ref), xout, (acc_scr,), sems = \
            _split_refs(refs, 4, nw, 2, 1)
        i = pl.program_id(0)
        if nw:
            comm = _Exchange(xin, xout, sems)
            pl.when(i == 0)(comm.start)

        @pl.when(i == 0)
        def _():
            acc_scr[...] = jnp.zeros_like(acc_scr)

        dhb = (0.5 * dh_ref[...]).astype(BF16)
        for a, b in ACT_PIECES:
            da = _dot_nt(dhb, wd_ref[a:b, :])
            g = gate_ref[:, a:b].astype(F32)
            u = up_ref[:, a:b].astype(F32)
            sig = _sigmoid(g)
            silu = g * sig
            dgu_ref[:, a:b] = (da * u * (sig + silu * (1.0 - sig))).astype(BF16)
            dgu_ref[:, D_FF + a:D_FF + b] = (da * silu).astype(BF16)
            acc_scr[a:b, :] += _dot_tn((silu * u).astype(BF16), dhb)

        @pl.when(i == n_i - 1)
        def _():
            dw_ref[...] = acc_scr[...].astype(BF16)

        if nw:
            pl.when(i == n_i - 1)(comm.finish)

    rows = pl.BlockSpec((tm, D_MODEL), lambda i: (i, 0))
    hid = pl.BlockSpec((tm, D_FF), lambda i: (i, 0))
    return pl.pallas_call(
        body, name=name, grid=(n_i,),
        in_specs=[rows, hid, hid, _resident(W_D_SHAPE)] + [ANY] * nw,
        out_specs=[pl.BlockSpec((tm, 2 * D_FF), lambda i: (i, 0)), _resident(W_D_SHAPE)] + [ANY] * nw,
        out_shape=[jax.ShapeDtypeStruct((n, 2 * D_FF), BF16), jax.ShapeDtypeStruct(W_D_SHAPE, BF16)]
        + [jax.ShapeDtypeStruct(a.shape, a.dtype) for a in exchange],
        scratch_shapes=[pltpu.VMEM(W_D_SHAPE, F32)] + (_comm_sems(nw) if nw else []),
        compiler_params=_params(("arbitrary",)),
    )(dh_out, gate, up, wd, *exchange)


def _ffn_bwd_in(dh_out, h_in, gain, dgu, wgu, name, tokens=None, exchange=()):
    n = dh_out.shape[0]
    tm = _row_tile(n, 512)
    n_i = n // tm
    nw = len(exchange)
    ksub, nq = (tm // 128, tokens[2] // 128) if tokens else (1, 0)

    def body(*refs):
        h_refs = refs[:ksub]
        (meta_ref, dh_ref, g_ref, dgu_ref, wgu_ref), xin, (dhin_ref, dgain_ref), xout, _, sems = \
            _split_refs(refs[ksub:], 5, nw, 2, 0)
        i = pl.program_id(0)
        if nw:
            comm = _ChipExchange(xin, xout, sems)
            pl.when(i == 0)(comm.start)

        @pl.when(i == 0)
        def _():
            dgain_ref[...] = jnp.zeros_like(dgain_ref)

        dn = _dot(dgu_ref[...], wgu_ref[...])
        gain_v = g_ref[...]
        hv = _assemble_rows(i, h_refs, meta_ref, nq) if tokens else h_refs[0][...]
        _, xhat, r = _rms(hv, gain_v)
        dhin_ref[...] = dh_ref[...] + _rms_bwd(dn, xhat, r, gain_v)
        dgain_ref[...] += jnp.sum(dn * xhat, axis=0, keepdims=True)

        if nw:
            pl.when(i == n_i - 1)(comm.finish)

    rows = pl.BlockSpec((tm, D_MODEL), lambda i: (i, 0))
    hid = pl.BlockSpec((tm, D_FF), lambda i: (i, 0))
    vec = pl.BlockSpec((1, D_MODEL), lambda i: (0, 0))
    meta_spec = pl.BlockSpec((N_META, D_MODEL), lambda i: (0, 0))
    if tokens:
        h_specs, h_args, meta = [_token_spec(k, ksub, nq) for k in range(ksub)], [tokens[0]] * ksub, tokens[1]
    else:
        h_specs, h_args, meta = [rows], [h_in], jnp.zeros((N_META, D_MODEL), F32)
    return pl.pallas_call(
        body, name=name, grid=(n_i,),
        in_specs=h_specs + [meta_spec, rows, vec, pl.BlockSpec((tm, 2 * D_FF), lambda i: (i, 0)),
                            _resident((2 * D_FF, D_MODEL))] + [ANY] * nw,
        out_specs=[rows, vec] + [ANY] * nw,
        out_shape=[jax.ShapeDtypeStruct((n, D_MODEL), F32), jax.ShapeDtypeStruct((1, D_MODEL), F32)]
        + [jax.ShapeDtypeStruct(a.shape, a.dtype) for a in exchange],
        scratch_shapes=_comm_sems(nw) if nw else [],
        compiler_params=_params(("arbitrary",)),
    )(*h_args, meta, dh_out, gain, dgu, wgu.reshape(2 * D_FF, D_MODEL), *exchange)


def _ffn_bwd_wgu(nrm, dgu, name, exchange=()):
    n = nrm.shape[0]
    tm = _row_tile(n, 512)
    n_i = n // tm
    nw = len(exchange)

    def body(*refs):
        (nrm_ref, dgu_ref), xin, (dw_ref,), xout, (acc_scr,), sems = _split_refs(refs, 2, nw, 1, 1)
        i = pl.program_id(0)
        if nw:
            comm = _Exchange(xin, xout, sems)
            pl.when(i == 0)(comm.start)

        @pl.when(i == 0)
        def _():
            acc_scr[...] = jnp.zeros_like(acc_scr)

        nb = nrm_ref[...]
        for half in (0, D_FF):
            for a, b in HID_PIECES:
                acc_scr[half + a:half + b, :] += _dot_tn(dgu_ref[:, half + a:half + b], nb)

        @pl.when(i == n_i - 1)
        def _():
            dw_ref[...] = acc_scr[...].astype(BF16)

        if nw:
            pl.when(i == n_i - 1)(comm.finish)

    shape = (2 * D_FF, D_MODEL)
    res = pl.pallas_call(
        body, name=name, grid=(n_i,),
        in_specs=[pl.BlockSpec((tm, D_MODEL), lambda i: (i, 0)),
                  pl.BlockSpec((tm, 2 * D_FF), lambda i: (i, 0))] + [ANY] * nw,
        out_specs=[_resident(shape)] + [ANY] * nw,
        out_shape=[jax.ShapeDtypeStruct(shape, BF16)] + [jax.ShapeDtypeStruct(a.shape, a.dtype) for a in exchange],
        scratch_shapes=[pltpu.VMEM(shape, F32)] + (_comm_sems(nw) if nw else []),
        compiler_params=_params(("arbitrary",)),
    )(nrm, dgu, *exchange)
    return res if nw else res[0]


def _inproj_fwd(h, gain, w_in):
    n = h.shape[0]
    tm = _row_tile(n, 512)

    def body(h_ref, g_ref, w_ref, *outs):
        y, _, _ = _rms(h_ref[...], g_ref[...])
        nb = y.astype(BF16)
        for p in range(N_PIECE):
            outs[p][...] = _dot_nt(nb, w_ref[512 * p:512 * (p + 1), :]).astype(BF16)
        outs[N_PIECE][...] = _dot_nt(nb, w_ref[IN_MAIN:IN_PAD, :])

    piece = pl.BlockSpec((tm, 512), lambda i: (i, 0))
    return pl.pallas_call(
        body, name="inproj_fwd", grid=(n // tm,),
        in_specs=[pl.BlockSpec((tm, D_MODEL), lambda i: (i, 0)),
                  pl.BlockSpec((1, D_MODEL), lambda i: (0, 0)),
                  pl.BlockSpec((IN_PAD, D_MODEL), lambda i: (0, 0))],
        out_specs=[piece] * N_PIECE + [pl.BlockSpec((tm, 128), lambda i: (i, 0))],
        out_shape=[jax.ShapeDtypeStruct((n, 512), BF16)] * N_PIECE + [jax.ShapeDtypeStruct((n, 128), F32)],
        compiler_params=_params(("parallel",)),
    )(h, gain, w_in)


def _inproj_bwd(dpieces, dfg, dh_out, h_in, gain, w_in):
    n = h_in.shape[0]
    tm = _row_tile(n, 512)
    n_i = n // tm

    def body(*refs):
        dp_refs = refs[:N_PIECE]
        dfg_ref, dh_ref, h_ref, g_ref, w_ref, dhin_ref, dw_ref, dgain_ref, acc_scr = refs[N_PIECE:]
        i = pl.program_id(0)

        @pl.when(i == 0)
        def _():
            acc_scr[...] = jnp.zeros_like(acc_scr)
            dgain_ref[...] = jnp.zeros_like(dgain_ref)

        gain_v = g_ref[...]
        y, xhat, r = _rms(h_ref[...], gain_v)
        nb = y.astype(BF16)
        dn = jnp.zeros((tm, D_MODEL), F32)
        for p in range(N_PIECE + 1):
            lo, hi = (512 * p, 512 * (p + 1)) if p < N_PIECE else (IN_MAIN, IN_PAD)
            dp = (dp_refs[p][...] if p < N_PIECE else dfg_ref[...]).astype(BF16)
            dn = dn + _dot(dp, w_ref[lo:hi, :])
            acc_scr[lo:hi, :] += _dot_tn(dp, nb)
        dhin_ref[...] = dh_ref[...] + _rms_bwd(dn, xhat, r, gain_v)
        dgain_ref[...] += jnp.sum(dn * xhat, axis=0, keepdims=True)

        @pl.when(i == n_i - 1)
        def _():
            dw_ref[...] = acc_scr[...].astype(BF16)

    piece = pl.BlockSpec((tm, 512), lambda i: (i, 0))
    rows = pl.BlockSpec((tm, D_MODEL), lambda i: (i, 0))
    vec = pl.BlockSpec((1, D_MODEL), lambda i: (0, 0))
    wspec = pl.BlockSpec((IN_PAD, D_MODEL), lambda i: (0, 0))
    return pl.pallas_call(
        body, name="inproj_bwd", grid=(n_i,),
        in_specs=[piece] * N_PIECE + [pl.BlockSpec((tm, 128), lambda i: (i, 0)), rows, rows, vec, wspec],
        out_specs=[rows, wspec, vec],
        out_shape=[jax.ShapeDtypeStruct((n, D_MODEL), F32),
                   jax.ShapeDtypeStruct((IN_PAD, D_MODEL), BF16),
                   jax.ShapeDtypeStruct((1, D_MODEL), F32)],
        scratch_shapes=[pltpu.VMEM((IN_PAD, D_MODEL), F32)],
        compiler_params=_params(("arbitrary",)),
    )(*dpieces, dfg, dh_out, h_in, gain, w_in)


def _outproj_bwd(dh, zc, za, w_out):
    n = dh.shape[0]
    tm = _row_tile(n, 512)
    n_i = n // tm

    def body(dh_ref, zc_ref, za_ref, w_ref, dzc_ref, dza_ref, dw_ref, acc_scr):
        i = pl.program_id(0)

        @pl.when(i == 0)
        def _():
            acc_scr[...] = jnp.zeros_like(acc_scr)

        dhb = dh_ref[...].astype(BF16)
        dzc_ref[...] = _dot_nt(dhb, w_ref[0:CONV_DIM, :]).astype(BF16)
        dza_ref[...] = _dot_nt(dhb, w_ref[CONV_DIM:, :]).astype(BF16)
        acc_scr[0:CONV_DIM, :] += _dot_tn(zc_ref[...], dhb)
        acc_scr[CONV_DIM:, :] += _dot_tn(za_ref[...], dhb)

        @pl.when(i == n_i - 1)
        def _():
            dw_ref[...] = acc_scr[...].astype(BF16)

    half = pl.BlockSpec((tm, 512), lambda i: (i, 0))
    wspec = pl.BlockSpec((D_MODEL, D_MODEL), lambda i: (0, 0))
    return pl.pallas_call(
        body, name="outproj_bwd", grid=(n_i,),
        in_specs=[pl.BlockSpec((tm, D_MODEL), lambda i: (i, 0)), half, half, wspec],
        out_specs=[half, half, wspec],
        out_shape=[jax.ShapeDtypeStruct((n, 512), BF16), jax.ShapeDtypeStruct((n, 512), BF16),
                   jax.ShapeDtypeStruct((D_MODEL, D_MODEL), BF16)],
        scratch_shapes=[pltpu.VMEM((D_MODEL, D_MODEL), F32)],
        compiler_params=_params(("arbitrary",)),
    )(dh, zc, za, w_out)


def _group_matrix():
    r = lax.broadcasted_iota(jnp.int32, (128, 128), 0) // HEAD_DIM
    c = lax.broadcasted_iota(jnp.int32, (128, 128), 1) // HEAD_DIM
    return jnp.where(r == c, 1.0 / HEAD_DIM, 0.0).astype(BF16)


def _group_mean(x, gmat):
    hi = x.astype(BF16)
    lo = (x - hi.astype(F32)).astype(BF16)
    return _dot(hi, gmat) + _dot(lo, gmat)


def _shift_rows(x, s):
    rows = x.shape[0]
    t = lax.broadcasted_iota(jnp.int32, x.shape, 0)
    rolled = pltpu.roll(x, s % rows, 0)
    keep = (t >= s) if s > 0 else (t < rows + s)
    return jnp.where(keep, rolled, 0.0)


def _conv_parts(bg_ref, cg_ref, hc_ref, w_ref):
    bg = bg_ref[...].astype(F32)
    cg = cg_ref[...].astype(F32)
    hc = hc_ref[...].astype(F32)
    u = cg * hc
    u1 = _shift_rows(u, 1)
    u2 = _shift_rows(u, 2)
    conv = w_ref[2:3, :] * u + w_ref[1:2, :] * u1 + w_ref[0:1, :] * u2
    return bg, cg, hc, u, u1, u2, conv


def _conv_fwd(bg, cg, hc, conv_w, gain, gmat, lp):
    n = bg.shape[0]
    nb = n // lp

    def body(bg_ref, cg_ref, hc_ref, w_ref, g_ref, gm_ref, z_ref):
        bgv, _, _, _, _, _, conv = _conv_parts(bg_ref, cg_ref, hc_ref, w_ref)
        yc = bgv * conv
        r = lax.rsqrt(_group_mean(yc * yc, gm_ref[...]) + EPS)
        z_ref[...] = (yc * r * g_ref[...]).astype(BF16)

    blk = pl.BlockSpec((lp, 128), lambda c, b: (b, c))
    return pl.pallas_call(
        body, name="conv_fwd", grid=(CONV_DIM // 128, nb),
        in_specs=[blk, blk, blk, pl.BlockSpec((3, 128), lambda c, b: (0, c)),
                  pl.BlockSpec((1, 128), lambda c, b: (0, c)), pl.BlockSpec((128, 128), lambda c, b: (0, 0))],
        out_specs=blk,
        out_shape=jax.ShapeDtypeStruct((n, CONV_DIM), BF16),
        compiler_params=_params(("parallel", "parallel")),
    )(bg, cg, hc, conv_w, gain, gmat)


def _conv_bwd(dz, bg, cg, hc, conv_w, gain, gmat, lp):
    n = bg.shape[0]
    nb = n // lp

    def body(dz_ref, bg_ref, cg_ref, hc_ref, w_ref, g_ref, gm_ref,
             dbg_ref, dcg_ref, dhc_ref, dw_ref, dgain_ref):
        b = pl.program_id(1)

        @pl.when(b == 0)
        def _():
            dw_ref[...] = jnp.zeros_like(dw_ref)
            dgain_ref[...] = jnp.zeros_like(dgain_ref)

        bgv, cgv, hcv, u, u1, u2, conv = _conv_parts(bg_ref, cg_ref, hc_ref, w_ref)
        gm = gm_ref[...]
        yc = bgv * conv
        r = lax.rsqrt(_group_mean(yc * yc, gm) + EPS)
        yhat = yc * r
        dzv = dz_ref[...].astype(F32)
        dyhat = dzv * g_ref[...]
        dgain_ref[...] += jnp.sum(dzv * yhat, axis=0, keepdims=True)
        dyc = r * (dyhat - yhat * _group_mean(dyhat * yhat, gm))
        dbg_ref[...] = (dyc * conv).astype(BF16)
        dconv = dyc * bgv
        du = (w_ref[2:3, :] * dconv + w_ref[1:2, :] * _shift_rows(dconv, -1)
              + w_ref[0:1, :] * _shift_rows(dconv, -2))
        dcg_ref[...] = (du * hcv).astype(BF16)
        dhc_ref[...] = (du * cgv).astype(BF16)
        dw_ref[0:1, :] += jnp.sum(dconv * u2, axis=0, keepdims=True)
        dw_ref[1:2, :] += jnp.sum(dconv * u1, axis=0, keepdims=True)
        dw_ref[2:3, :] += jnp.sum(dconv * u, axis=0, keepdims=True)

    blk = pl.BlockSpec((lp, 128), lambda c, b: (b, c))
    wspec = pl.BlockSpec((3, 128), lambda c, b: (0, c))
    gspec = pl.BlockSpec((1, 128), lambda c, b: (0, c))
    return pl.pallas_call(
        body, name="conv_bwd", grid=(CONV_DIM // 128, nb),
        in_specs=[blk, blk, blk, blk, wspec, gspec, pl.BlockSpec((128, 128), lambda c, b: (0, 0))],
        out_specs=[blk, blk, blk, wspec, gspec],
        out_shape=[jax.ShapeDtypeStruct((n, CONV_DIM), BF16)] * 3
        + [jax.ShapeDtypeStruct((3, CONV_DIM), F32), jax.ShapeDtypeStruct((1, CONV_DIM), F32)],
        compiler_params=_params(("parallel", "arbitrary")),
    )(dz, bg, cg, hc, conv_w, gain, gmat)


KEY_MASKED = 1e30
ONE_LANE = 24


def _scan_steps(rows):
    s, out = 1, []
    while s < rows:
        out.append(s)
        s *= 2
    return out


def _fgate_fwd(fg, b_f, lp):
    n = fg.shape[0]
    nb = n // lp

    def body(fg_ref, b_ref, ka_ref, qa_ref):
        x = fg_ref[...] + b_ref[...]
        logf = jnp.minimum(x, 0.0) - jnp.log(1.0 + jnp.exp(-jnp.abs(x)))
        t = lax.broadcasted_iota(jnp.int32, (lp, 128), 0)
        lane = lax.broadcasted_iota(jnp.int32, (lp, 128), 1)
        f = jnp.where((t >= PAD) & (lane < N_HEADS), logf, 0.0)
        for s in _scan_steps(lp):
            f = f + _shift_rows(f, s)
        hi = f.astype(BF16).astype(F32)
        rest = f - hi
        mid = rest.astype(BF16).astype(F32)
        lo = (rest - mid).astype(BF16).astype(F32)
        ones = jnp.where((lane >= ONE_LANE) & (lane < ONE_LANE + 3), 1.0, 0.0)
        hi_key = jnp.where((t < PAD) & (lane < N_HEADS), KEY_MASKED, hi)
        ka_ref[...] = (hi_key + pltpu.roll(mid, 8, 1) + pltpu.roll(lo, 16, 1) + ones).astype(BF16)
        for h in range(N_HEADS):
            minus = jnp.where((lane == h) | (lane == 8 + h) | (lane == 16 + h), -1.0, 0.0)
            terms = (jnp.where(lane == ONE_LANE, pltpu.roll(hi, ONE_LANE - h, 1), 0.0)
                     + jnp.where(lane == ONE_LANE + 1, pltpu.roll(mid, ONE_LANE + 1 - h, 1), 0.0)
                     + jnp.where(lane == ONE_LANE + 2, pltpu.roll(lo, ONE_LANE + 2 - h, 1), 0.0))
            qa_ref[:, 128 * h:128 * (h + 1)] = (minus + terms).astype(BF16)

    return pl.pallas_call(
        body, name="fgate_fwd", grid=(nb,),
        in_specs=[pl.BlockSpec((lp, 128), lambda b: (b, 0)), pl.BlockSpec((1, 128), lambda b: (0, 0))],
        out_specs=[pl.BlockSpec((lp, 128), lambda b: (b, 0)), pl.BlockSpec((lp, N_HEADS * 128), lambda b: (b, 0))],
        out_shape=[jax.ShapeDtypeStruct((n, 128), BF16), jax.ShapeDtypeStruct((n, N_HEADS * 128), BF16)],
        compiler_params=_params(("parallel",)),
    )(fg, b_f)


def _fgate_bwd(dka, dfr, fg, b_f, lp):
    n = fg.shape[0]
    nb = n // lp

    def body(dka_ref, dfr_ref, fg_ref, b_ref, dfg_ref, db_ref):
        b = pl.program_id(0)

        @pl.when(b == 0)
        def _():
            db_ref[...] = jnp.zeros_like(db_ref)

        wide = jnp.concatenate([dfr_ref[0], jnp.zeros((128 - N_HEADS, lp), F32)], axis=0)
        t = lax.broadcasted_iota(jnp.int32, (lp, 128), 0)
        lane = lax.broadcasted_iota(jnp.int32, (lp, 128), 1)
        d = jnp.where(lane < N_HEADS, dka_ref[...], 0.0) + wide.T
        for s in _scan_steps(lp):
            d = d + _shift_rows(d, -s)
        x = fg_ref[...] + b_ref[...]
        dx = jnp.where((t >= PAD) & (lane < N_HEADS), d * _sigmoid(-x), 0.0)
        dfg_ref[...] = dx
        db_ref[...] += jnp.sum(dx, axis=0, keepdims=True)

    return pl.pallas_call(
        body, name="fgate_bwd", grid=(nb,),
        in_specs=[pl.BlockSpec((lp, 128), lambda b: (b, 0)), pl.BlockSpec((1, N_HEADS, lp), lambda b: (b, 0, 0)),
                  pl.BlockSpec((lp, 128), lambda b: (b, 0)), pl.BlockSpec((1, 128), lambda b: (0, 0))],
        out_specs=[pl.BlockSpec((lp, 128), lambda b: (b, 0)), pl.BlockSpec((1, 128), lambda b: (0, 0))],
        out_shape=[jax.ShapeDtypeStruct((n, 128), F32), jax.ShapeDtypeStruct((1, 128), F32)],
        compiler_params=_params(("arbitrary",)),
    )(dka, dfr, fg, b_f)


def _head_masks():
    lane = lax.broadcasted_iota(jnp.int32, (1, 128), 1)
    return lane < HEAD_DIM


def _stack_heads(x2, first):
    zero = jnp.zeros_like(x2)
    return jnp.concatenate([jnp.where(first, x2, zero), jnp.where(first, zero, x2)], axis=0)


def _stack_heads_lanes(xt):
    r = lax.broadcasted_iota(jnp.int32, xt.shape, 0)
    zero = jnp.zeros_like(xt)
    return jnp.concatenate([jnp.where(r < HEAD_DIM, xt, zero), jnp.where(r < HEAD_DIM, zero, xt)], axis=1)


def _pair_cols(col0, col1, first):
    return jnp.where(first, col0, col1)


def _pair_rows(row0, row1):
    r = lax.broadcasted_iota(jnp.int32, (128, TQ), 0)
    return jnp.where(r < HEAD_DIM, row0, row1)


def _query_side(q_ref, qa_ref, p, first):
    q2 = q_ref[:, 128 * p:128 * (p + 1)] * 0.125
    zero = jnp.zeros_like(q2)
    top = jnp.concatenate([jnp.where(first, q2, zero), qa_ref[:, 128 * (2 * p):128 * (2 * p + 1)]], axis=1)
    bot = jnp.concatenate([jnp.where(first, zero, q2), qa_ref[:, 128 * (2 * p + 1):128 * (2 * p + 2)]], axis=1)
    return jnp.concatenate([top, bot], axis=0)


def _key_chunks(lp):
    return (lp + TK - 1) // TK


def _chunk_mask(i, c, tk):
    r = lax.broadcasted_iota(jnp.int32, (tk, 2 * TQ), 0)
    col = lax.broadcasted_iota(jnp.int32, (tk, 2 * TQ), 1)
    return (c * TK + r) <= (i * TQ + (col & (TQ - 1)))


def _causal_sweep(i, step, init):
    per = TK // TQ
    last = i // per
    carry = lax.fori_loop(0, last, lambda c, carry: step(c, carry, False, TK), init)
    tails = [lambda carry, r=r: step(last, carry, True, TQ * (r + 1)) for r in range(per)]
    return lax.switch(i % per, tails, carry)


def _transpose_bf16(x):
    return x.astype(F32).T.astype(BF16)


def _attn_fwd(q, qa, k, v, ka, gain, zc, w_out, h, lp):
    n = q.shape[0]
    nb = n // lp
    nq = lp // TQ
    lpp = _key_chunks(lp) * TK

    def body(q_ref, qa_ref, k_ref, v_ref, ka_ref, g_ref, zc_ref, w_ref, h_ref,
             z_ref, o_ref, lse_ref, hout_ref, kx_scr, vt_scr):
        i = pl.program_id(1)
        first = _head_masks()

        @pl.when(i == 0)
        def _():
            if lpp > lp:
                kx_scr[lp:lpp, :] = jnp.zeros((lpp - lp, 2 * ATTN_DIM), BF16)
                vt_scr[:, lp:lpp] = jnp.zeros((ATTN_DIM, lpp - lp), BF16)
            for p in range(N_PAIRS):
                kx_scr[0:lp, 256 * p:256 * p + 128] = k_ref[:, 128 * p:128 * (p + 1)]
                kx_scr[0:lp, 256 * p + 128:256 * (p + 1)] = ka_ref[...]
            vt_scr[:, 0:lp] = _transpose_bf16(v_ref[...])

        rhs_t = [_transpose_bf16(_query_side(q_ref, qa_ref, p, first)) for p in range(N_PAIRS)]

        def step(c, carry, masked, tk):
            koff = pl.multiple_of(c * TK, TK)
            valid = _chunk_mask(i, c, tk) if masked else None
            new = []
            for p in range(N_PAIRS):
                m, l, acc = carry[p]
                st = _dot(kx_scr[pl.ds(koff, tk), 256 * p:256 * (p + 1)], rhs_t[p])
                if masked:
                    st = jnp.where(valid, st, NEG)
                m_new = jnp.maximum(m, jnp.max(st, axis=0, keepdims=True))
                pt = jnp.exp(st - m_new)
                alpha = jnp.exp(m - m_new)
                l = alpha * l + jnp.sum(pt, axis=0, keepdims=True)
                pb = pt.astype(BF16)
                vt = _stack_heads_lanes(vt_scr[128 * p:128 * (p + 1), pl.ds(koff, tk)])
                pv = _dot(vt, jnp.concatenate([pb[:, 0:TQ], pb[:, TQ:]], axis=0))
                acc = acc * _pair_rows(alpha[:, 0:TQ], alpha[:, TQ:]) + pv
                new.append((m_new, l, acc))
            return tuple(new)

        init = tuple((jnp.full((1, 2 * TQ), NEG, F32), jnp.zeros((1, 2 * TQ), F32), jnp.zeros((128, TQ), F32))
                     for _ in range(N_PAIRS))
        final = _causal_sweep(i, step, init)

        row = lax.broadcasted_iota(jnp.int32, (TQ, 128), 0)
        real = (i * TQ + row) >= PAD
        zs = [zc_ref[...]]
        for p in range(N_PAIRS):
            m, l, acc = final[p]
            inv = 1.0 / l
            ot = acc * _pair_rows(inv[:, 0:TQ], inv[:, TQ:])
            sq = ot * ot
            r0 = lax.rsqrt(jnp.sum(sq[0:HEAD_DIM], axis=0, keepdims=True) * (1.0 / HEAD_DIM) + EPS)
            r1 = lax.rsqrt(jnp.sum(sq[HEAD_DIM:], axis=0, keepdims=True) * (1.0 / HEAD_DIM) + EPS)
            cols = slice(128 * p, 128 * (p + 1))
            o_ref[:, cols] = jnp.where(real, ot.T, 0.0).astype(BF16)
            z = (jnp.where(real, (ot * _pair_rows(r0, r1)).T, 0.0) * g_ref[:, cols]).astype(BF16)
            z_ref[:, cols] = z
            zs.append(z)
            lse = m + jnp.log(l)
            lse_ref[0, 2 * p:2 * p + 1, :] = lse[:, 0:TQ]
            lse_ref[0, 2 * p + 1:2 * p + 2, :] = lse[:, TQ:]
        hout_ref[...] = h_ref[...] + _dot(jnp.concatenate(zs, axis=1), w_ref[...])

    qblk = pl.BlockSpec((TQ, ATTN_DIM), lambda b, i: (b * nq + i, 0))
    qablk = pl.BlockSpec((TQ, N_HEADS * 128), lambda b, i: (b * nq + i, 0))
    seq = pl.BlockSpec((lp, ATTN_DIM), lambda b, i: (b, 0))
    rowblk = pl.BlockSpec((1, N_HEADS, TQ), lambda b, i: (b, 0, i))
    hblk = pl.BlockSpec((TQ, D_MODEL), lambda b, i: (b * nq + i, 0))
    return pl.pallas_call(
        body, name="attn_fwd", grid=(nb, nq),
        in_specs=[qblk, qablk, seq, seq, pl.BlockSpec((lp, 128), lambda b, i: (b, 0)),
                  pl.BlockSpec((1, ATTN_DIM), lambda b, i: (0, 0)), qblk,
                  pl.BlockSpec((D_MODEL, D_MODEL), lambda b, i: (0, 0)), hblk],
        out_specs=[qblk, qblk, rowblk, hblk],
        out_shape=[jax.ShapeDtypeStruct((n, ATTN_DIM), BF16), jax.ShapeDtypeStruct((n, ATTN_DIM), BF16),
                   jax.ShapeDtypeStruct((nb, N_HEADS, lp), F32), jax.ShapeDtypeStruct((n, D_MODEL), F32)],
        scratch_shapes=[pltpu.VMEM((lpp, 2 * ATTN_DIM), BF16), pltpu.VMEM((ATTN_DIM, lpp), BF16)],
        compiler_params=_params(("parallel", "arbitrary")),
    )(q, qa, k, v, ka, gain, zc, w_out, h)


def _attn_bwd(dz, q, qa, k, v, ka, o, lse, gain, lp, exchange=()):
    n = q.shape[0]
    nb = n // lp
    nq = lp // TQ
    lpp = _key_chunks(lp) * TK
    nw = len(exchange)

    def body(*refs):
        ((dz_ref, q_ref, qa_ref, k_ref, v_ref, ka_ref, o_ref, lse_ref, g_ref), xin,
         (dq_ref, dk_ref, dv_ref, dka_ref, dfr_ref, dgain_ref), xout,
         (kx_scr, vx_scr, kt_scr, dkx_scr, dvx_scr), sems) = _split_refs(refs, 9, nw, 6, 5)
        b = pl.program_id(0)
        i = pl.program_id(1)
        first = _head_masks()
        if nw:
            comm = _Exchange(xin, xout, sems)
            pl.when((b == 0) & (i == 0))(comm.start)

        @pl.when((b == 0) & (i == 0))
        def _():
            dgain_ref[...] = jnp.zeros_like(dgain_ref)

        @pl.when(i == 0)
        def _():
            if lpp > lp:
                kx_scr[lp:lpp, :] = jnp.zeros((lpp - lp, 2 * ATTN_DIM), BF16)
                vx_scr[lp:lpp, :] = jnp.zeros((lpp - lp, ATTN_DIM), BF16)
                kt_scr[:, lp:lpp] = jnp.zeros((ATTN_DIM, lpp - lp), BF16)
            for p in range(N_PAIRS):
                kx_scr[0:lp, 256 * p:256 * p + 128] = k_ref[:, 128 * p:128 * (p + 1)]
                kx_scr[0:lp, 256 * p + 128:256 * (p + 1)] = ka_ref[...]
            vx_scr[0:lp, :] = v_ref[...]
            kt_scr[:, 0:lp] = _transpose_bf16(k_ref[...])
            dkx_scr[...] = jnp.zeros_like(dkx_scr)
            dvx_scr[...] = jnp.zeros_like(dvx_scr)

        rhs, rhs_t, lses, dos, dos_t, deltas = [], [], [], [], [], []
        for p in range(N_PAIRS):
            cols = slice(128 * p, 128 * (p + 1))
            side = _query_side(q_ref, qa_ref, p, first)
            rhs.append(side)
            rhs_t.append(_transpose_bf16(side))
            lses.append(jnp.concatenate([lse_ref[0, 2 * p:2 * p + 1, :], lse_ref[0, 2 * p + 1:2 * p + 2, :]], axis=1))
            ov = o_ref[:, cols].astype(F32)
            dzv = dz_ref[:, cols].astype(F32)
            gv = g_ref[:, cols]
            sq = ov * ov
            ms0 = jnp.sum(jnp.where(first, sq, 0.0), axis=1, keepdims=True) * (1.0 / HEAD_DIM)
            ms1 = jnp.sum(jnp.where(first, 0.0, sq), axis=1, keepdims=True) * (1.0 / HEAD_DIM)
            r = _pair_cols(lax.rsqrt(ms0 + EPS), lax.rsqrt(ms1 + EPS), first)
            ohat = ov * r
            dyhat = dzv * gv
            dgain_ref[:, cols] += jnp.sum(dzv * ohat, axis=0, keepdims=True)
            pr = dyhat * ohat
            mean0 = jnp.sum(jnp.where(first, pr, 0.0), axis=1, keepdims=True) * (1.0 / HEAD_DIM)
            mean1 = jnp.sum(jnp.where(first, 0.0, pr), axis=1, keepdims=True) * (1.0 / HEAD_DIM)
            do = r * (dyhat - ohat * _pair_cols(mean0, mean1, first))
            ddt = (do * ov).T
            deltas.append(jnp.concatenate([jnp.sum(ddt[0:HEAD_DIM], axis=0, keepdims=True),
                                           jnp.sum(ddt[HEAD_DIM:], axis=0, keepdims=True)], axis=1))
            do_st = _stack_heads(do.astype(BF16), first)
            dos.append(do_st)
            dos_t.append(_transpose_bf16(do_st))

        def step(c, carry, masked, tk):
            koff = pl.multiple_of(c * TK, TK)
            valid = _chunk_mask(i, c, tk) if masked else None
            new = []
            for p in range(N_PAIRS):
                dqt, dfq = carry[p]
                ext = slice(256 * p, 256 * (p + 1))
                cols = slice(128 * p, 128 * (p + 1))
                st = _dot(kx_scr[pl.ds(koff, tk), ext], rhs_t[p])
                if masked:
                    st = jnp.where(valid, st, NEG)
                pt = jnp.exp(st - lses[p])
                dpt = _dot(vx_scr[pl.ds(koff, tk), cols], dos_t[p])
                dst = pt * (dpt - deltas[p])
                dsb = dst.astype(BF16)
                dfq = dfq + jnp.sum(dsb.astype(F32), axis=0, keepdims=True)
                dkx_scr[pl.ds(koff, tk), ext] += _dot(dsb, rhs[p])
                dvx_scr[pl.ds(koff, tk), cols] += _dot(pt.astype(BF16), dos[p])
                kt = _stack_heads_lanes(kt_scr[cols, pl.ds(koff, tk)])
                dqt = dqt + _dot(kt, jnp.concatenate([dsb[:, 0:TQ], dsb[:, TQ:]], axis=0))
                new.append((dqt, dfq))
            return tuple(new)

        init = tuple((jnp.zeros((128, TQ), F32), jnp.zeros((1, 2 * TQ), F32)) for _ in range(N_PAIRS))
        final = _causal_sweep(i, step, init)

        for p in range(N_PAIRS):
            dqt, dfq = final[p]
            dq_ref[:, 128 * p:128 * (p + 1)] = (dqt.T * 0.125).astype(BF16)
            dfr_ref[0, 2 * p:2 * p + 1, :] = dfq[:, 0:TQ]
            dfr_ref[0, 2 * p + 1:2 * p + 2, :] = dfq[:, TQ:]

        @pl.when(i == nq - 1)
        def _():
            dka = jnp.zeros((lp, 128), F32)
            for p in range(N_PAIRS):
                dk_ref[:, 128 * p:128 * (p + 1)] = dkx_scr[0:lp, 256 * p:256 * p + 128].astype(BF16)
                dka = dka + dkx_scr[0:lp, 256 * p + 128:256 * (p + 1)]
            dka_ref[...] = dka
            dv_ref[...] = dvx_scr[0:lp, :].astype(BF16)

        if nw:
            pl.when((b == nb - 1) & (i == nq - 1))(comm.finish)

    qblk = pl.BlockSpec((TQ, ATTN_DIM), lambda b, i: (b * nq + i, 0))
    qablk = pl.BlockSpec((TQ, N_HEADS * 128), lambda b, i: (b * nq + i, 0))
    seq = pl.BlockSpec((lp, ATTN_DIM), lambda b, i: (b, 0))
    kaseq = pl.BlockSpec((lp, 128), lambda b, i: (b, 0))
    rowblk = pl.BlockSpec((1, N_HEADS, TQ), lambda b, i: (b, 0, i))
    gspec = pl.BlockSpec((1, ATTN_DIM), lambda b, i: (0, 0))
    return pl.pallas_call(
        body, name="attn_bwd", grid=(nb, nq),
        in_specs=[qblk, qblk, qablk, seq, seq, kaseq, qblk, rowblk, gspec] + [ANY] * nw,
        out_specs=[qblk, seq, seq, kaseq, rowblk, gspec] + [ANY] * nw,
        out_shape=[jax.ShapeDtypeStruct((n, ATTN_DIM), BF16), jax.ShapeDtypeStruct((n, ATTN_DIM), BF16),
                   jax.ShapeDtypeStruct((n, ATTN_DIM), BF16), jax.ShapeDtypeStruct((n, 128), F32),
                   jax.ShapeDtypeStruct((nb, N_HEADS, lp), F32), jax.ShapeDtypeStruct((1, ATTN_DIM), F32)]
        + [jax.ShapeDtypeStruct(a.shape, a.dtype) for a in exchange],
        scratch_shapes=[pltpu.VMEM((lpp, 2 * ATTN_DIM), BF16), pltpu.VMEM((lpp, ATTN_DIM), BF16),
                        pltpu.VMEM((ATTN_DIM, lpp), BF16), pltpu.VMEM((lpp, 2 * ATTN_DIM), F32),
                        pltpu.VMEM((lpp, ATTN_DIM), F32)] + (_comm_sems(nw) if nw else []),
        compiler_params=_params(("arbitrary", "arbitrary")),
    )(dz, q, qa, k, v, ka, o, lse, gain, *exchange)


def _adamw(parts, w, m, v, name):
    s_parts, r, c = parts.shape
    tr = r
    for t in (256, 128, 64, 32, 16):
        if r % t == 0 and r > t:
            tr = t
            break

    def body(p_ref, w_ref, m_ref, v_ref, g_ref, d_ref, nm_ref, nv_ref):
        g = p_ref[0].astype(F32)
        for s in range(1, s_parts):
            g = g + p_ref[s].astype(F32)
        nm = ADAM_B1 * m_ref[...] + (1.0 - ADAM_B1) * g
        nv = ADAM_B2 * v_ref[...] + (1.0 - ADAM_B2) * (g * g)
        m_hat = nm / (1.0 - ADAM_B1 ** ADAM_STEP)
        v_hat = nv / (1.0 - ADAM_B2 ** ADAM_STEP)
        g_ref[...] = g
        d_ref[...] = -ADAM_LR * (m_hat / (jnp.sqrt(v_hat) + ADAM_EPS) + ADAM_WD * w_ref[...])
        nm_ref[...] = nm
        nv_ref[...] = nv

    blk = pl.BlockSpec((tr, c), lambda i: (i, 0))
    return pl.pallas_call(
        body, name=name, grid=(r // tr,),
        in_specs=[pl.BlockSpec((s_parts, tr, c), lambda i: (0, i, 0)), blk, blk, blk],
        out_specs=[blk] * 4,
        out_shape=[jax.ShapeDtypeStruct((r, c), F32)] * 4,
        compiler_params=_params(("parallel",)),
    )(parts, w, m, v)


def _sum_parts(parts, name):
    s_parts, r, c = parts.shape

    def body(p_ref, out_ref):
        acc = p_ref[0]
        for s in range(1, s_parts):
            acc = acc + p_ref[s]
        out_ref[...] = acc

    return pl.pallas_call(
        body, name=name, out_shape=jax.ShapeDtypeStruct((r, c), F32),
        in_specs=[pl.BlockSpec(memory_space=pltpu.VMEM)], out_specs=pl.BlockSpec(memory_space=pltpu.VMEM),
    )(parts)


SMALL_ROWS = 184
LOSS_ROW = 181


def _pack_small(d_gains, d_gc, d_ga, d_bf, d_conv, d_meta, loss_part):
    rows = [g.reshape(8, 128) for g in d_gains]
    rows += [d_gc.reshape(4, 128), d_ga.reshape(4, 128), d_bf.reshape(1, 128)]
    rows += [d_conv.reshape(12, 128), d_meta.reshape(128, 128), jnp.pad(loss_part, ((0, 0), (0, 127)))]
    packed = jnp.concatenate(rows, axis=0)
    return jnp.pad(packed, ((0, SMALL_ROWS - packed.shape[0]), (0, 0)))


def kernel(x, meta_tokens, ffn1_norm, ffn1_w_gu, ffn1_w_down, mix_norm, w_in, conv_w, b_f, out_norm_conv, out_norm_attn, w_out, ffn2_norm, ffn2_w_gu, ffn2_w_down, final_norm, loss_target, m_meta_tokens, m_ffn1_norm, m_ffn1_w_gu, m_ffn1_w_down, m_mix_norm, m_w_in, m_conv_w, m_b_f, m_out_norm_conv, m_out_norm_attn, m_w_out, m_ffn2_norm, m_ffn2_w_gu, m_ffn2_w_down, m_final_norm, v_meta_tokens, v_ffn1_norm, v_ffn1_w_gu, v_ffn1_w_down, v_mix_norm, v_w_in, v_conv_w, v_b_f, v_out_norm_conv, v_out_norm_attn, v_w_out, v_ffn2_norm, v_ffn2_w_gu, v_ffn2_w_down, v_final_norm):
    nb, seq, _ = x.shape
    lp = PAD + N_META + seq
    me = 4 * lax.axis_index("x") + 2 * lax.axis_index("y") + lax.axis_index("c")
    shard_gu = D_FF // 4
    shard_d = D_FF // N_DEV

    small_in = jnp.concatenate(
        [meta_tokens, jnp.pad(conv_w[0], ((0, 0), (0, 128 - conv_w.shape[2]))), jnp.zeros((5, 128), F32)], axis=0)
    wgu1_8, wd1_8, small_8 = _all_gather(
        [ffn1_w_gu[0].T.astype(BF16), ffn1_w_down[0].astype(BF16), small_in], "gather_ffn1")
    meta_full = small_8[:, 0:N_META, :].transpose(1, 0, 2).reshape(N_META, D_MODEL)
    conv_full = small_8[:, N_META:N_META + 3, 0:CONV_DIM // N_DEV].transpose(1, 0, 2).reshape(3, CONV_DIM)
    wgu1 = wgu1_8.reshape(W_GU_SHAPE)
    wd1 = wd1_8.reshape(W_D_SHAPE)
    b_f_row = jnp.pad(b_f, ((0, 0), (0, 128 - N_HEADS)))
    gmat = _group_matrix()

    x2d = x.reshape(nb * seq, D_MODEL)
    later = [w_in[0].T.astype(BF16), w_out[0].astype(BF16), ffn2_w_gu[0].T.astype(BF16), ffn2_w_down[0].astype(BF16)]
    h1, n1, gate1, up1, win_8, wout_8, wgu2_8, wd2_8 = _ffn1_fwd(x2d, meta_full, lp, ffn1_norm, wgu1, wd1, later)
    wgu2 = wgu2_8.reshape(W_GU_SHAPE)
    wd2 = wd2_8.reshape(W_D_SHAPE)
    w_in_full = jnp.pad(win_8.reshape(IN_DIM, D_MODEL), ((0, IN_PAD - IN_DIM), (0, 0)))
    w_out_full = wout_8.reshape(D_MODEL, D_MODEL)

    bg, cg, hc, q, k, v, fg = _inproj_fwd(h1, mix_norm, w_in_full)
    zc = _conv_fwd(bg, cg, hc, conv_full, out_norm_conv, gmat, lp)
    ka, qa = _fgate_fwd(fg, b_f_row, lp)
    za, o, lse, h2 = _attn_fwd(q, qa, k, v, ka, out_norm_attn, zc, w_out_full, h1, lp)
    dh3, n3, gate2, up2, loss_part, d_final = _ffn2_fwd_loss(
        h2, ffn2_norm, wgu2, wd2, final_norm.reshape(1, D_MODEL), loss_target.reshape(nb * seq, D_MODEL), lp)

    dgu2, dwd2 = _ffn_bwd_act_wd(dh3, gate2, up2, wd2, "ffn2_bwd_act")
    dh2, d_ffn2 = _ffn_bwd_in(dh3, h2, ffn2_norm, dgu2, wgu2, "ffn2_bwd_in")
    dwgu2 = _ffn_bwd_wgu(n3, dgu2, "ffn2_bwd_wgu")
    dzc, dza, dwout = _outproj_bwd(dh2, zc, za, w_out_full)
    send_a = [dwgu2.reshape(N_DEV, shard_gu, D_MODEL), dwd2.reshape(N_DEV, shard_d, D_MODEL),
              dwout.reshape(N_DEV, D_MODEL // N_DEV, D_MODEL)]
    dq, dk, dv, dka, dfr, d_ga, p_wgu2, p_wd2, p_wout = _attn_bwd(
        dza, q, qa, k, v, ka, o, lse, out_norm_attn, lp, exchange=send_a)
    dfg, d_bf = _fgate_bwd(dka, dfr, fg, b_f_row, lp)
    dbg, dcg, dhc, d_conv, d_gc = _conv_bwd(dzc, bg, cg, hc, conv_full, out_norm_conv, gmat, lp)
    dh1, dwin, d_mix = _inproj_bwd([dbg, dcg, dhc, dq, dk, dv], dfg, dh2, h1, mix_norm, w_in_full)
    dwin_8 = dwin[0:IN_DIM].reshape(N_DEV, IN_DIM // N_DEV, D_MODEL)
    dgu1, dwd1, p_win = _ffn_bwd_act_wd(dh1, gate1, up1, wd1, "ffn1_bwd_act", exchange=[dwin_8])
    dwgu1, p_wd1 = _ffn_bwd_wgu(n1, dgu1, "ffn1_bwd_wgu", exchange=[dwd1.reshape(N_DEV, shard_d, D_MODEL)])
    own = dwgu1.reshape(N_DEV, shard_gu, D_MODEL)
    (got,) = _pair_exchange([own], "pair_exchange_ffn1")
    chip_sum = _pair_sum(own, got, "pair_sum_wgu1")
    dh0, d_ffn1, p_wgu1 = _ffn_bwd_in(dh1, None, ffn1_norm, dgu1, wgu1, "ffn1_bwd_in",
                                      tokens=(x2d, meta_full, lp), exchange=[chip_sum])

    dh0 = dh0.reshape(nb, lp, D_MODEL)
    grad_x = dh0[:, PAD + N_META:, :]
    d_meta = jnp.sum(dh0[:, PAD:PAD + N_META, :], axis=0)

    small = _pack_small([d_ffn1, d_mix, d_ffn2, d_final], d_gc, d_ga, d_bf, d_conv, d_meta, loss_part)
    (small_all,) = _all_gather([small], "gather_small_grads")
    small_sum = _sum_parts(small_all, "sum_small_grads")
    g_ffn1n, g_mixn, g_ffn2n, g_finaln = (small_sum[8 * t:8 * t + 8].reshape(1, D_MODEL) for t in range(4))
    g_gc = small_sum[32:36].reshape(1, CONV_DIM)
    g_ga = small_sum[36:40].reshape(1, ATTN_DIM)
    g_bf = small_sum[40:41, 0:N_HEADS]
    g_conv_full = small_sum[41:53].reshape(3, CONV_DIM)
    g_meta_full = small_sum[53:181].reshape(N_META, D_MODEL)
    g_conv = lax.dynamic_slice_in_dim(g_conv_full, me * (CONV_DIM // N_DEV), CONV_DIM // N_DEV, axis=1)
    g_meta = lax.dynamic_slice_in_dim(g_meta_full, me * (D_MODEL // N_DEV), D_MODEL // N_DEV, axis=1)

    weights = {
        "meta_tokens": (g_meta[None], meta_tokens, m_meta_tokens, v_meta_tokens),
        "ffn1_norm": (g_ffn1n[None], ffn1_norm, m_ffn1_norm, v_ffn1_norm),
        "ffn1_w_gu": (p_wgu1, ffn1_w_gu[0].T, m_ffn1_w_gu[0].T, v_ffn1_w_gu[0].T),
        "ffn1_w_down": (p_wd1, ffn1_w_down[0], m_ffn1_w_down[0], v_ffn1_w_down[0]),
        "mix_norm": (g_mixn[None], mix_norm, m_mix_norm, v_mix_norm),
        "w_in": (p_win, w_in[0].T, m_w_in[0].T, v_w_in[0].T),
        "conv_w": (g_conv[None], conv_w[0], m_conv_w[0], v_conv_w[0]),
        "b_f": (g_bf[None], b_f, m_b_f, v_b_f),
        "out_norm_conv": (g_gc[None], out_norm_conv, m_out_norm_conv, v_out_norm_conv),
        "out_norm_attn": (g_ga[None], out_norm_attn, m_out_norm_attn, v_out_norm_attn),
        "w_out": (p_wout, w_out[0], m_w_out[0], v_w_out[0]),
        "ffn2_norm": (g_ffn2n[None], ffn2_norm, m_ffn2_norm, v_ffn2_norm),
        "ffn2_w_gu": (p_wgu2, ffn2_w_gu[0].T, m_ffn2_w_gu[0].T, v_ffn2_w_gu[0].T),
        "ffn2_w_down": (p_wd2, ffn2_w_down[0], m_ffn2_w_down[0], v_ffn2_w_down[0]),
        "final_norm": (g_finaln[None], final_norm.reshape(1, D_MODEL), m_final_norm.reshape(1, D_MODEL),
                       v_final_norm.reshape(1, D_MODEL)),
    }
    shapes = {"meta_tokens": meta_tokens.shape, "ffn1_norm": ffn1_norm.shape, "ffn1_w_gu": ffn1_w_gu.shape,
              "ffn1_w_down": ffn1_w_down.shape, "mix_norm": mix_norm.shape, "w_in": w_in.shape,
              "conv_w": conv_w.shape, "b_f": b_f.shape, "out_norm_conv": out_norm_conv.shape,
              "out_norm_attn": out_norm_attn.shape, "w_out": w_out.shape, "ffn2_norm": ffn2_norm.shape,
              "ffn2_w_gu": ffn2_w_gu.shape, "ffn2_w_down": ffn2_w_down.shape, "final_norm": final_norm.shape}
    grads, deltas, new_m, new_v = [], [], [], []
    for name, (p, w, m, vv) in weights.items():
        g, d, nm, nv = _adamw(p, w, m, vv, "adamw_" + name)
        if name in ("ffn1_w_gu", "ffn2_w_gu", "w_in"):
            g, d, nm, nv = g.T, d.T, nm.T, nv.T
        shape = shapes[name]
        grads.append(g.reshape(shape))
        deltas.append(d.reshape(shape))
        new_m.append(nm.reshape(shape))
        new_v.append(nv.reshape(shape))

    loss = small_sum[LOSS_ROW, 0]
    return (loss, grad_x, *grads, *deltas, *new_m, *new_v)
```

```python
import jax
import jax.numpy as jnp
from jax import lax
from jax.experimental import pallas as pl
from jax.experimental.pallas import tpu as pltpu

F32 = jnp.float32
BF16 = jnp.bfloat16

N_DEV = 8
D_MODEL = 1024
N_META = 16
PAD = 128 - N_META
CONV_DIM = 512
ATTN_DIM = 512
HEAD_DIM = 64
N_HEADS = 8
N_PAIRS = N_HEADS // 2
D_FF = 2816
IN_DIM = 3080
IN_PAD = 3200
IN_MAIN = 3072
N_PIECE = IN_MAIN // 512
EPS = 1e-6
NEG = -1e30
TQ = 128
TK = 512
VMEM_LIMIT = 56 * 1024 * 1024

HID_PIECES = ((0, 1024), (1024, 2048), (2048, D_FF))
ACT_PIECES = tuple((a, min(a + 256, D_FF)) for a in range(0, D_FF, 256))
W_GU_SHAPE = (2, D_FF, D_MODEL)
W_D_SHAPE = (D_FF, D_MODEL)

ADAM_LR = 0.001
ADAM_B1 = 0.9
ADAM_B2 = 0.999
ADAM_EPS = 1e-08
ADAM_WD = 0.01
ADAM_STEP = 10

MESH = pl.DeviceIdType.MESH
ANY = pl.BlockSpec(memory_space=pl.ANY)


def _params(sem=None):
    return pltpu.CompilerParams(dimension_semantics=sem, vmem_limit_bytes=VMEM_LIMIT)


def _row_tile(n, prefer):
    for t in (prefer, 512, 256, 128):
        if t <= n and n % t == 0:
            return t
    raise ValueError(f"no row tile for {n}")


def _resident(shape):
    zeros = (0,) * len(shape)
    return pl.BlockSpec(shape, lambda i: zeros, pipeline_mode=pl.Buffered(1))


def _dot(a, b):
    return jnp.dot(a, b, preferred_element_type=F32)


def _dot_nt(a, b):
    return lax.dot_general(a, b, (((1,), (1,)), ((), ())), preferred_element_type=F32)


def _dot_tn(a, b):
    return lax.dot_general(a, b, (((0,), (0,)), ((), ())), preferred_element_type=F32)


def _rms(x, g):
    r = lax.rsqrt(jnp.mean(x * x, axis=-1, keepdims=True) + EPS)
    xhat = x * r
    return xhat * g, xhat, r


def _rms_bwd(dn, xhat, r, g):
    dxhat = dn * g
    return r * (dxhat - xhat * jnp.mean(dxhat * xhat, axis=-1, keepdims=True))


def _sigmoid(x):
    return 1.0 / (1.0 + jnp.exp(-x))


def _place():
    return lax.axis_index("x"), lax.axis_index("y"), lax.axis_index("c")


def _comm_sems(nw):
    return [pltpu.SemaphoreType.DMA((nw, 7)), pltpu.SemaphoreType.DMA((nw, 7)), pltpu.SemaphoreType.DMA((nw,))]


def _flip(v, bit):
    return 1 - v if bit else v


class _Gather:
    def __init__(self, ins, outs, sems):
        self.ins, self.outs = ins, outs
        self.send, self.recv, self.local = sems
        x, y, c = _place()
        self.c = c
        self.me, self.sibling = (x, y, c), (x, y, 1 - c)
        first = ((x + 1 - c) % 2, (y + c) % 2)
        second = ((x + c) % 2, (y + 1 - c) % 2)
        self.chips = [first, second, (1 - x, 1 - y)]
        self.targets = [first, second, second]

    def _copy(self, w, k, block, to, own=False):
        slot = self.outs[w].at[4 * block[0] + 2 * block[1] + block[2]]
        return pltpu.make_async_remote_copy(
            src_ref=self.ins[w] if own else slot, dst_ref=slot,
            send_sem=self.send.at[w, k], recv_sem=self.recv.at[w, k], device_id=to, device_id_type=MESH)

    def _mine(self, w):
        x, y, c = self.me
        return pltpu.make_async_copy(self.ins[w], self.outs[w].at[4 * x + 2 * y + c], self.local.at[w])

    def _first(self, w):
        return ([self._copy(w, 0, self.me, self.sibling, own=True)]
                + [self._copy(w, 1 + j, self.me, (*self.targets[j], self.c), own=True) for j in range(2)])

    def _relay(self, w):
        return self._copy(w, 3, (*self.chips[0], self.c), (*self.targets[2], self.c))

    def _landed(self, w, j):
        return self._copy(w, 1 + j, (*self.chips[j], self.c), self.me)

    def _passed(self, w):
        return [self._copy(w, 4 + j, (*chip, self.c), self.sibling) for j, chip in enumerate(self.chips)]

    def start(self):
        for w in range(len(self.ins)):
            self._mine(w).start()
        for w in range(len(self.ins)):
            for cp in self._first(w):
                cp.start()

    def relay(self):
        for w in range(len(self.ins)):
            self._landed(w, 0).wait_recv()
            self._relay(w).start()
            self._passed(w)[0].start()

    def forward(self):
        for w in range(len(self.ins)):
            for j in (1, 2):
                self._landed(w, j).wait_recv()
                self._passed(w)[j].start()

    def finish(self):
        from_sibling = [self.chips[1], self.chips[0], self.chips[2]]
        for w in range(len(self.ins)):
            self._copy(w, 0, self.sibling, self.me).wait_recv()
            for j, chip in enumerate(from_sibling):
                self._copy(w, 4 + j, (*chip, 1 - self.c), self.me).wait_recv()
        for w in range(len(self.ins)):
            for cp in self._first(w) + [self._relay(w)] + self._passed(w):
                cp.wait_send()
            self._mine(w).wait()


class _Exchange:
    def __init__(self, ins, outs, sems):
        self.ins, self.outs = ins, outs
        self.send, self.recv, self.local = sems
        self.x, self.y, self.c = _place()
        self.me = 4 * self.x + 2 * self.y + self.c

    def _copy(self, w, k):
        peer = (_flip(self.x, ((k + 1) >> 2) & 1), _flip(self.y, ((k + 1) >> 1) & 1), _flip(self.c, (k + 1) & 1))
        return pltpu.make_async_remote_copy(
            src_ref=self.ins[w].at[4 * peer[0] + 2 * peer[1] + peer[2]], dst_ref=self.outs[w].at[self.me],
            send_sem=self.send.at[w, k], recv_sem=self.recv.at[w, k], device_id=peer, device_id_type=MESH)

    def _mine(self, w):
        return pltpu.make_async_copy(self.ins[w].at[self.me], self.outs[w].at[self.me], self.local.at[w])

    def start(self):
        for w in range(len(self.ins)):
            self._mine(w).start()
            for k in range(N_DEV - 1):
                self._copy(w, k).start()

    def finish(self):
        for w in range(len(self.ins)):
            for k in range(N_DEV - 1):
                self._copy(w, k).wait()
            self._mine(w).wait()


class _PairExchange:
    def __init__(self, ins, outs, sems):
        self.ins, self.outs = ins, outs
        self.send, self.recv, _ = sems
        x, y, self.c = _place()
        self.sibling = (x, y, 1 - self.c)

    def _copy(self, w, t):
        return pltpu.make_async_remote_copy(
            src_ref=self.ins[w].at[2 * t + 1 - self.c], dst_ref=self.outs[w].at[t],
            send_sem=self.send.at[w, t], recv_sem=self.recv.at[w, t], device_id=self.sibling, device_id_type=MESH)

    def start(self):
        for w in range(len(self.ins)):
            for t in range(4):
                self._copy(w, t).start()

    def finish(self):
        for w in range(len(self.ins)):
            for t in range(4):
                self._copy(w, t).wait()


class _ChipExchange:
    def __init__(self, ins, outs, sems):
        self.ins, self.outs = ins, outs
        self.send, self.recv, self.local = sems
        self.x, self.y, self.c = _place()
        self.chip = 2 * self.x + self.y

    def _copy(self, w, k):
        px, py = _flip(self.x, ((k + 1) >> 1) & 1), _flip(self.y, (k + 1) & 1)
        return pltpu.make_async_remote_copy(
            src_ref=self.ins[w].at[2 * px + py], dst_ref=self.outs[w].at[self.chip],
            send_sem=self.send.at[w, k], recv_sem=self.recv.at[w, k], device_id=(px, py, self.c),
            device_id_type=MESH)

    def _mine(self, w):
        return pltpu.make_async_copy(self.ins[w].at[self.chip], self.outs[w].at[self.chip], self.local.at[w])

    def start(self):
        for w in range(len(self.ins)):
            self._mine(w).start()
            for k in range(3):
                self._copy(w, k).start()

    def finish(self):
        for w in range(len(self.ins)):
            for k in range(3):
                self._copy(w, k).wait()
            self._mine(w).wait()


def _split_refs(refs, n_in, n_comm, n_out, n_scr):
    a = n_in
    b = a + n_comm
    c = b + n_out
    d = c + n_comm
    e = d + n_scr
    return refs[:a], refs[a:b], refs[b:c], refs[c:d], refs[d:e], refs[e:]


def _all_gather(xs, name):
    nw = len(xs)

    def body(*refs):
        comm = _Gather(refs[:nw], refs[nw:2 * nw], refs[2 * nw:])
        comm.start()
        comm.relay()
        comm.forward()
        comm.finish()

    return pl.pallas_call(
        body, name=name, in_specs=[ANY] * nw, out_specs=[ANY] * nw,
        out_shape=[jax.ShapeDtypeStruct((N_DEV,) + a.shape, a.dtype) for a in xs],
        scratch_shapes=_comm_sems(nw),
    )(*xs)


def _pair_exchange(xs, name):
    nw = len(xs)

    def body(*refs):
        comm = _PairExchange(refs[:nw], refs[nw:2 * nw], refs[2 * nw:])
        comm.start()
        comm.finish()

    return pl.pallas_call(
        body, name=name, in_specs=[ANY] * nw, out_specs=[ANY] * nw,
        out_shape=[jax.ShapeDtypeStruct((4,) + a.shape[1:], a.dtype) for a in xs],
        scratch_shapes=_comm_sems(nw),
    )(*xs)


def _pair_sum(own, got, name):
    _, r, c = own.shape
    tr = r
    for t in (256, 128, 64, 32, 16):
        if r % t == 0 and r > t:
            tr = t
            break

    def body(own_ref, got_ref, out_ref):
        mine = jnp.where(lax.axis_index("c") == 0, own_ref[:, 0].astype(F32), own_ref[:, 1].astype(F32))
        out_ref[...] = (mine + got_ref[...].astype(F32)).astype(BF16)

    return pl.pallas_call(
        body, name=name, grid=(r // tr,),
        in_specs=[pl.BlockSpec((4, 2, tr, c), lambda i: (0, 0, i, 0)), pl.BlockSpec((4, tr, c), lambda i: (0, i, 0))],
        out_specs=pl.BlockSpec((4, tr, c), lambda i: (0, i, 0)),
        out_shape=jax.ShapeDtypeStruct((4, r, c), BF16),
        compiler_params=_params(("parallel",)),
    )(own.reshape(4, 2, r, c), got)


def _token_spec(k, ksub, nq):
    def index_map(i):
        s = ksub * i + k
        return ((s // nq) * (nq - 1) + jnp.maximum(s % nq, 1) - 1, 0)
    return pl.BlockSpec((128, D_MODEL), index_map)


def _is_lead(i, k, ksub, nq):
    return ((ksub * i + k) % nq) == 0


def _assemble_rows(i, x_refs, meta_ref, nq):
    ksub = len(x_refs)
    lead = jnp.concatenate([jnp.zeros((PAD, D_MODEL), F32), meta_ref[...]], axis=0)
    return jnp.concatenate([jnp.where(_is_lead(i, k, ksub, nq), lead, x_refs[k][...]) for k in range(ksub)], axis=0)


def _swiglu(nb, wgu_ref, wd_ref, gate_ref, up_ref):
    acc = jnp.zeros((nb.shape[0], D_MODEL), F32)
    for a, b in HID_PIECES:
        gate = _dot_nt(nb, wgu_ref[0, a:b, :])
        up = _dot_nt(nb, wgu_ref[1, a:b, :])
        gate_ref[:, a:b] = gate.astype(BF16)
        up_ref[:, a:b] = up.astype(BF16)
        acc = acc + _dot((gate * _sigmoid(gate) * up).astype(BF16), wd_ref[a:b, :])
    return acc


def _ffn1_fwd(x2d, meta, lp, gain, wgu, wd, gather):
    nq = lp // 128
    n = (x2d.shape[0] // (nq - 1)) * nq
    tm = _row_tile(n, 512)
    ksub = tm // 128
    n_i = n // tm
    nw = len(gather)

    def body(*refs):
        x_refs = refs[:ksub]
        (meta_ref, g_ref, wgu_ref, wd_ref), gin, (out_ref, nrm_ref, gate_ref, up_ref), gout, _, sems = \
            _split_refs(refs[ksub:], 4, nw, 4, 0)
        i = pl.program_id(0)
        comm = _Gather(gin, gout, sems)
        pl.when(i == 0)(comm.start)
        pl.when(i == n_i // 2)(comm.relay)
        pl.when(i == max(n_i - 3, n_i // 2))(comm.forward)

        hv = _assemble_rows(i, x_refs, meta_ref, nq)
        y, _, _ = _rms(hv, g_ref[...])
        nb = y.astype(BF16)
        nrm_ref[...] = nb
        out_ref[...] = hv + 0.5 * _swiglu(nb, wgu_ref, wd_ref, gate_ref, up_ref)

        pl.when(i == n_i - 1)(comm.finish)

    rows = pl.BlockSpec((tm, D_MODEL), lambda i: (i, 0))
    hid = pl.BlockSpec((tm, D_FF), lambda i: (i, 0))
    return pl.pallas_call(
        body, name="ffn1_fwd", grid=(n_i,),
        in_specs=[_token_spec(k, ksub, nq) for k in range(ksub)]
        + [pl.BlockSpec((N_META, D_MODEL), lambda i: (0, 0)), pl.BlockSpec((1, D_MODEL), lambda i: (0, 0)),
           _resident(W_GU_SHAPE), _resident(W_D_SHAPE)] + [ANY] * nw,
        out_specs=[rows, rows, hid, hid] + [ANY] * nw,
        out_shape=[jax.ShapeDtypeStruct((n, D_MODEL), F32), jax.ShapeDtypeStruct((n, D_MODEL), BF16),
                   jax.ShapeDtypeStruct((n, D_FF), BF16), jax.ShapeDtypeStruct((n, D_FF), BF16)]
        + [jax.ShapeDtypeStruct((N_DEV,) + a.shape, a.dtype) for a in gather],
        scratch_shapes=_comm_sems(nw),
        compiler_params=_params(("arbitrary",)),
    )(*([x2d] * ksub), meta, gain, wgu, wd, *gather)


def _ffn2_fwd_loss(h, gain, wgu, wd, gfinal, target, lp):
    n = h.shape[0]
    nq = lp // 128
    tm = _row_tile(n, 512)
    ksub = tm // 128
    n_i = n // tm

    def body(*refs):
        t_refs = refs[:ksub]
        h_ref, g_ref, wgu_ref, wd_ref, gf_ref, dh_ref, nrm_ref, gate_ref, up_ref, loss_ref, dgf_ref = refs[ksub:]
        i = pl.program_id(0)

        @pl.when(i == 0)
        def _():
            loss_ref[...] = jnp.zeros_like(loss_ref)
            dgf_ref[...] = jnp.zeros_like(dgf_ref)

        hv = h_ref[...]
        y, _, _ = _rms(hv, g_ref[...])
        nb = y.astype(BF16)
        nrm_ref[...] = nb
        hout = hv + 0.5 * _swiglu(nb, wgu_ref, wd_ref, gate_ref, up_ref)

        gf = gf_ref[...]
        loss = jnp.zeros((1, 1), F32)
        dgf = jnp.zeros((1, D_MODEL), F32)
        for k in range(ksub):
            yk, xhat, r = _rms(hout[128 * k:128 * (k + 1)], gf)
            err = jnp.where(_is_lead(i, k, ksub, nq), 0.0, yk - t_refs[k][...])
            loss = loss + 0.5 * jnp.sum(jnp.sum(err * err, axis=1, keepdims=True), axis=0,
                                        keepdims=True) * (1.0 / D_MODEL)
            dy = err * (1.0 / D_MODEL)
            dh_ref[128 * k:128 * (k + 1), :] = _rms_bwd(dy, xhat, r, gf)
            dgf = dgf + jnp.sum(dy * xhat, axis=0, keepdims=True)
        loss_ref[...] += loss
        dgf_ref[...] += dgf

    rows = pl.BlockSpec((tm, D_MODEL), lambda i: (i, 0))
    hid = pl.BlockSpec((tm, D_FF), lambda i: (i, 0))
    vec = pl.BlockSpec((1, D_MODEL), lambda i: (0, 0))
    return pl.pallas_call(
        body, name="ffn2_fwd_loss", grid=(n_i,),
        in_specs=[_token_spec(k, ksub, nq) for k in range(ksub)]
        + [rows, vec, _resident(W_GU_SHAPE), _resident(W_D_SHAPE), vec],
        out_specs=[rows, rows, hid, hid, pl.BlockSpec((1, 1), lambda i: (0, 0)), vec],
        out_shape=[jax.ShapeDtypeStruct((n, D_MODEL), F32), jax.ShapeDtypeStruct((n, D_MODEL), BF16),
                   jax.ShapeDtypeStruct((n, D_FF), BF16), jax.ShapeDtypeStruct((n, D_FF), BF16),
                   jax.ShapeDtypeStruct((1, 1), F32), jax.ShapeDtypeStruct((1, D_MODEL), F32)],
        compiler_params=_params(("arbitrary",)),
    )(*([target] * ksub), h, gain, wgu, wd, gfinal)


def _ffn_bwd_act_wd(dh_out, gate, up, wd, name, exchange=()):
    n = dh_out.shape[0]
    tm = _row_tile(n, 256)
    n_i = n // tm
    nw = len(exchange)

    def body(*refs):
        (dh_ref, gate_ref, up_ref, wd_ref), xin, (dgu_ref, dw_ref), xout, (acc_scr,), sems = \
            _split_refs(refs, 4, nw, 2, 1)
        i = pl.program_id(0)
        if nw:
            comm = _Exchange(xin, xout, sems)
            pl.when(i == 0)(comm.start)

        @pl.when(i == 0)
        def _():
            acc_scr[...] = jnp.zeros_like(acc_scr)

        dhb = (0.5 * dh_ref[...]).astype(BF16)
        for a, b in ACT_PIECES:
            da = _dot_nt(dhb, wd_ref[a:b, :])
            g = gate_ref[:, a:b].astype(F32)
            u = up_ref[:, a:b].astype(F32)
            sig = _sigmoid(g)
            silu = g * sig
            dgu_ref[:, a:b] = (da * u * (sig + silu * (1.0 - sig))).astype(BF16)
            dgu_ref[:, D_FF + a:D_FF + b] = (da * silu).astype(BF16)
            acc_scr[a:b, :] += _dot_tn((silu * u).astype(BF16), dhb)

        @pl.when(i == n_i - 1)
        def _():
            dw_ref[...] = acc_scr[...].astype(BF16)

        if nw:
            pl.when(i == n_i - 1)(comm.finish)

    rows = pl.BlockSpec((tm, D_MODEL), lambda i: (i, 0))
    hid = pl.BlockSpec((tm, D_FF), lambda i: (i, 0))
    return pl.pallas_call(
        body, name=name, grid=(n_i,),
        in_specs=[rows, hid, hid, _resident(W_D_SHAPE)] + [ANY] * nw,
        out_specs=[pl.BlockSpec((tm, 2 * D_FF), lambda i: (i, 0)), _resident(W_D_SHAPE)] + [ANY] * nw,
        out_shape=[jax.ShapeDtypeStruct((n, 2 * D_FF), BF16), jax.ShapeDtypeStruct(W_D_SHAPE, BF16)]
        + [jax.ShapeDtypeStruct(a.shape, a.dtype) for a in exchange],
        scratch_shapes=[pltpu.VMEM(W_D_SHAPE, F32)] + (_comm_sems(nw) if nw else []),
        compiler_params=_params(("arbitrary",)),
    )(dh_out, gate, up, wd, *exchange)


def _ffn_bwd_in(dh_out, h_in, gain, dgu, wgu, name, tokens=None, exchange=()):
    n = dh_out.shape[0]
    tm = _row_tile(n, 512)
    n_i = n // tm
    nw = len(exchange)
    ksub, nq = (tm // 128, tokens[2] // 128) if tokens else (1, 0)

    def body(*refs):
        h_refs = refs[:ksub]
        (meta_ref, dh_ref, g_ref, dgu_ref, wgu_ref), xin, (dhin_ref, dgain_ref), xout, _, sems = \
            _split_refs(refs[ksub:], 5, nw, 2, 0)
        i = pl.program_id(0)
        if nw:
            comm = _ChipExchange(xin, xout, sems)
            pl.when(i == 0)(comm.start)

        @pl.when(i == 0)
        def _():
            dgain_ref[...] = jnp.zeros_like(dgain_ref)

        dn = _dot(dgu_ref[...], wgu_ref[...])
        gain_v = g_ref[...]
        hv = _assemble_rows(i, h_refs, meta_ref, nq) if tokens else h_refs[0][...]
        _, xhat, r = _rms(hv, gain_v)
        dhin_ref[...] = dh_ref[...] + _rms_bwd(dn, xhat, r, gain_v)
        dgain_ref[...] += jnp.sum(dn * xhat, axis=0, keepdims=True)

        if nw:
            pl.when(i == n_i - 1)(comm.finish)

    rows = pl.BlockSpec((tm, D_MODEL), lambda i: (i, 0))
    hid = pl.BlockSpec((tm, D_FF), lambda i: (i, 0))
    vec = pl.BlockSpec((1, D_MODEL), lambda i: (0, 0))
    meta_spec = pl.BlockSpec((N_META, D_MODEL), lambda i: (0, 0))
    if tokens:
        h_specs, h_args, meta = [_token_spec(k, ksub, nq) for k in range(ksub)], [tokens[0]] * ksub, tokens[1]
    else:
        h_specs, h_args, meta = [rows], [h_in], jnp.zeros((N_META, D_MODEL), F32)
    return pl.pallas_call(
        body, name=name, grid=(n_i,),
        in_specs=h_specs + [meta_spec, rows, vec, pl.BlockSpec((tm, 2 * D_FF), lambda i: (i, 0)),
                            _resident((2 * D_FF, D_MODEL))] + [ANY] * nw,
        out_specs=[rows, vec] + [ANY] * nw,
        out_shape=[jax.ShapeDtypeStruct((n, D_MODEL), F32), jax.ShapeDtypeStruct((1, D_MODEL), F32)]
        + [jax.ShapeDtypeStruct(a.shape, a.dtype) for a in exchange],
        scratch_shapes=_comm_sems(nw) if nw else [],
        compiler_params=_params(("arbitrary",)),
    )(*h_args, meta, dh_out, gain, dgu, wgu.reshape(2 * D_FF, D_MODEL), *exchange)


def _ffn_bwd_wgu(nrm, dgu, name, exchange=()):
    n = nrm.shape[0]
    tm = _row_tile(n, 512)
    n_i = n // tm
    nw = len(exchange)

    def body(*refs):
        (nrm_ref, dgu_ref), xin, (dw_ref,), xout, (acc_scr,), sems = _split_refs(refs, 2, nw, 1, 1)
        i = pl.program_id(0)
        if nw:
            comm = _Exchange(xin, xout, sems)
            pl.when(i == 0)(comm.start)

        @pl.when(i == 0)
        def _():
            acc_scr[...] = jnp.zeros_like(acc_scr)

        nb = nrm_ref[...]
        for half in (0, D_FF):
            for a, b in HID_PIECES:
                acc_scr[half + a:half + b, :] += _dot_tn(dgu_ref[:, half + a:half + b], nb)

        @pl.when(i == n_i - 1)
        def _():
            dw_ref[...] = acc_scr[...].astype(BF16)

        if nw:
            pl.when(i == n_i - 1)(comm.finish)

    shape = (2 * D_FF, D_MODEL)
    res = pl.pallas_call(
        body, name=name, grid=(n_i,),
        in_specs=[pl.BlockSpec((tm, D_MODEL), lambda i: (i, 0)),
                  pl.BlockSpec((tm, 2 * D_FF), lambda i: (i, 0))] + [ANY] * nw,
        out_specs=[_resident(shape)] + [ANY] * nw,
        out_shape=[jax.ShapeDtypeStruct(shape, BF16)] + [jax.ShapeDtypeStruct(a.shape, a.dtype) for a in exchange],
        scratch_shapes=[pltpu.VMEM(shape, F32)] + (_comm_sems(nw) if nw else []),
        compiler_params=_params(("arbitrary",)),
    )(nrm, dgu, *exchange)
    return res if nw else res[0]


def _inproj_fwd(h, gain, w_in):
    n = h.shape[0]
    tm = _row_tile(n, 512)

    def body(h_ref, g_ref, w_ref, *outs):
        y, _, _ = _rms(h_ref[...], g_ref[...])
        nb = y.astype(BF16)
        for p in range(N_PIECE):
            outs[p][...] = _dot_nt(nb, w_ref[512 * p:512 * (p + 1), :]).astype(BF16)
        outs[N_PIECE][...] = _dot_nt(nb, w_ref[IN_MAIN:IN_PAD, :])

    piece = pl.BlockSpec((tm, 512), lambda i: (i, 0))
    return pl.pallas_call(
        body, name="inproj_fwd", grid=(n // tm,),
        in_specs=[pl.BlockSpec((tm, D_MODEL), lambda i: (i, 0)),
                  pl.BlockSpec((1, D_MODEL), lambda i: (0, 0)),
                  pl.BlockSpec((IN_PAD, D_MODEL), lambda i: (0, 0))],
        out_specs=[piece] * N_PIECE + [pl.BlockSpec((tm, 128), lambda i: (i, 0))],
        out_shape=[jax.ShapeDtypeStruct((n, 512), BF16)] * N_PIECE + [jax.ShapeDtypeStruct((n, 128), F32)],
        compiler_params=_params(("parallel",)),
    )(h, gain, w_in)


def _inproj_bwd(dpieces, dfg, dh_out, h_in, gain, w_in):
    n = h_in.shape[0]
    tm = _row_tile(n, 512)
    n_i = n // tm

    def body(*refs):
        dp_refs = refs[:N_PIECE]
        dfg_ref, dh_ref, h_ref, g_ref, w_ref, dhin_ref, dw_ref, dgain_ref, acc_scr = refs[N_PIECE:]
        i = pl.program_id(0)

        @pl.when(i == 0)
        def _():
            acc_scr[...] = jnp.zeros_like(acc_scr)
            dgain_ref[...] = jnp.zeros_like(dgain_ref)

        gain_v = g_ref[...]
        y, xhat, r = _rms(h_ref[...], gain_v)
        nb = y.astype(BF16)
        dn = jnp.zeros((tm, D_MODEL), F32)
        for p in range(N_PIECE + 1):
            lo, hi = (512 * p, 512 * (p + 1)) if p < N_PIECE else (IN_MAIN, IN_PAD)
            dp = (dp_refs[p][...] if p < N_PIECE else dfg_ref[...]).astype(BF16)
            dn = dn + _dot(dp, w_ref[lo:hi, :])
            acc_scr[lo:hi, :] += _dot_tn(dp, nb)
        dhin_ref[...] = dh_ref[...] + _rms_bwd(dn, xhat, r, gain_v)
        dgain_ref[...] += jnp.sum(dn * xhat, axis=0, keepdims=True)

        @pl.when(i == n_i - 1)
        def _():
            dw_ref[...] = acc_scr[...].astype(BF16)

    piece = pl.BlockSpec((tm, 512), lambda i: (i, 0))
    rows = pl.BlockSpec((tm, D_MODEL), lambda i: (i, 0))
    vec = pl.BlockSpec((1, D_MODEL), lambda i: (0, 0))
    wspec = pl.BlockSpec((IN_PAD, D_MODEL), lambda i: (0, 0))
    return pl.pallas_call(
        body, name="inproj_bwd", grid=(n_i,),
        in_specs=[piece] * N_PIECE + [pl.BlockSpec((tm, 128), lambda i: (i, 0)), rows, rows, vec, wspec],
        out_specs=[rows, wspec, vec],
        out_shape=[jax.ShapeDtypeStruct((n, D_MODEL), F32),
                   jax.ShapeDtypeStruct((IN_PAD, D_MODEL), BF16),
                   jax.ShapeDtypeStruct((1, D_MODEL), F32)],
        scratch_shapes=[pltpu.VMEM((IN_PAD, D_MODEL), F32)],
        compiler_params=_params(("arbitrary",)),
    )(*dpieces, dfg, dh_out, h_in, gain, w_in)


def _outproj_bwd(dh, zc, za, w_out):
    n = dh.shape[0]
    tm = _row_tile(n, 512)
    n_i = n // tm

    def body(dh_ref, zc_ref, za_ref, w_ref, dzc_ref, dza_ref, dw_ref, acc_scr):
        i = pl.program_id(0)

        @pl.when(i == 0)
        def _():
            acc_scr[...] = jnp.zeros_like(acc_scr)

        dhb = dh_ref[...].astype(BF16)
        dzc_ref[...] = _dot_nt(dhb, w_ref[0:CONV_DIM, :]).astype(BF16)
        dza_ref[...] = _dot_nt(dhb, w_ref[CONV_DIM:, :]).astype(BF16)
        acc_scr[0:CONV_DIM, :] += _dot_tn(zc_ref[...], dhb)
        acc_scr[CONV_DIM:, :] += _dot_tn(za_ref[...], dhb)

        @pl.when(i == n_i - 1)
        def _():
            dw_ref[...] = acc_scr[...].astype(BF16)

    half = pl.BlockSpec((tm, 512), lambda i: (i, 0))
    wspec = pl.BlockSpec((D_MODEL, D_MODEL), lambda i: (0, 0))
    return pl.pallas_call(
        body, name="outproj_bwd", grid=(n_i,),
        in_specs=[pl.BlockSpec((tm, D_MODEL), lambda i: (i, 0)), half, half, wspec],
        out_specs=[half, half, wspec],
        out_shape=[jax.ShapeDtypeStruct((n, 512), BF16), jax.ShapeDtypeStruct((n, 512), BF16),
                   jax.ShapeDtypeStruct((D_MODEL, D_MODEL), BF16)],
        scratch_shapes=[pltpu.VMEM((D_MODEL, D_MODEL), F32)],
        compiler_params=_params(("arbitrary",)),
    )(dh, zc, za, w_out)


def _group_matrix():
    r = lax.broadcasted_iota(jnp.int32, (128, 128), 0) // HEAD_DIM
    c = lax.broadcasted_iota(jnp.int32, (128, 128), 1) // HEAD_DIM
    return jnp.where(r == c, 1.0 / HEAD_DIM, 0.0).astype(BF16)


def _group_mean(x, gmat):
    hi = x.astype(BF16)
    lo = (x - hi.astype(F32)).astype(BF16)
    return _dot(hi, gmat) + _dot(lo, gmat)


def _shift_rows(x, s):
    rows = x.shape[0]
    t = lax.broadcasted_iota(jnp.int32, x.shape, 0)
    rolled = pltpu.roll(x, s % rows, 0)
    keep = (t >= s) if s > 0 else (t < rows + s)
    return jnp.where(keep, rolled, 0.0)


def _conv_parts(bg_ref, cg_ref, hc_ref, w_ref):
    bg = bg_ref[...].astype(F32)
    cg = cg_ref[...].astype(F32)
    hc = hc_ref[...].astype(F32)
    u = cg * hc
    u1 = _shift_rows(u, 1)
    u2 = _shift_rows(u, 2)
    conv = w_ref[2:3, :] * u + w_ref[1:2, :] * u1 + w_ref[0:1, :] * u2
    return bg, cg, hc, u, u1, u2, conv


def _conv_fwd(bg, cg, hc, conv_w, gain, gmat, lp):
    n = bg.shape[0]
    nb = n // lp

    def body(bg_ref, cg_ref, hc_ref, w_ref, g_ref, gm_ref, z_ref):
        bgv, _, _, _, _, _, conv = _conv_parts(bg_ref, cg_ref, hc_ref, w_ref)
        yc = bgv * conv
        r = lax.rsqrt(_group_mean(yc * yc, gm_ref[...]) + EPS)
        z_ref[...] = (yc * r * g_ref[...]).astype(BF16)

    blk = pl.BlockSpec((lp, 128), lambda c, b: (b, c))
    return pl.pallas_call(
        body, name="conv_fwd", grid=(CONV_DIM // 128, nb),
        in_specs=[blk, blk, blk, pl.BlockSpec((3, 128), lambda c, b: (0, c)),
                  pl.BlockSpec((1, 128), lambda c, b: (0, c)), pl.BlockSpec((128, 128), lambda c, b: (0, 0))],
        out_specs=blk,
        out_shape=jax.ShapeDtypeStruct((n, CONV_DIM), BF16),
        compiler_params=_params(("parallel", "parallel")),
    )(bg, cg, hc, conv_w, gain, gmat)


def _conv_bwd(dz, bg, cg, hc, conv_w, gain, gmat, lp):
    n = bg.shape[0]
    nb = n // lp

    def body(dz_ref, bg_ref, cg_ref, hc_ref, w_ref, g_ref, gm_ref,
             dbg_ref, dcg_ref, dhc_ref, dw_ref, dgain_ref):
        b = pl.program_id(1)

        @pl.when(b == 0)
        def _():
            dw_ref[...] = jnp.zeros_like(dw_ref)
            dgain_ref[...] = jnp.zeros_like(dgain_ref)

        bgv, cgv, hcv, u, u1, u2, conv = _conv_parts(bg_ref, cg_ref, hc_ref, w_ref)
        gm = gm_ref[...]
        yc = bgv * conv
        r = lax.rsqrt(_group_mean(yc * yc, gm) + EPS)
        yhat = yc * r
        dzv = dz_ref[...].astype(F32)
        dyhat = dzv * g_ref[...]
        dgain_ref[...] += jnp.sum(dzv * yhat, axis=0, keepdims=True)
        dyc = r * (dyhat - yhat * _group_mean(dyhat * yhat, gm))
        dbg_ref[...] = (dyc * conv).astype(BF16)
        dconv = dyc * bgv
        du = (w_ref[2:3, :] * dconv + w_ref[1:2, :] * _shift_rows(dconv, -1)
              + w_ref[0:1, :] * _shift_rows(dconv, -2))
        dcg_ref[...] = (du * hcv).astype(BF16)
        dhc_ref[...] = (du * cgv).astype(BF16)
        dw_ref[0:1, :] += jnp.sum(dconv * u2, axis=0, keepdims=True)
        dw_ref[1:2, :] += jnp.sum(dconv * u1, axis=0, keepdims=True)
        dw_ref[2:3, :] += jnp.sum(dconv * u, axis=0, keepdims=True)

    blk = pl.BlockSpec((lp, 128), lambda c, b: (b, c))
    wspec = pl.BlockSpec((3, 128), lambda c, b: (0, c))
    gspec = pl.BlockSpec((1, 128), lambda c, b: (0, c))
    return pl.pallas_call(
        body, name="conv_bwd", grid=(CONV_DIM // 128, nb),
        in_specs=[blk, blk, blk, blk, wspec, gspec, pl.BlockSpec((128, 128), lambda c, b: (0, 0))],
        out_specs=[blk, blk, blk, wspec, gspec],
        out_shape=[jax.ShapeDtypeStruct((n, CONV_DIM), BF16)] * 3
        + [jax.ShapeDtypeStruct((3, CONV_DIM), F32), jax.ShapeDtypeStruct((1, CONV_DIM), F32)],
        compiler_params=_params(("parallel", "arbitrary")),
    )(dz, bg, cg, hc, conv_w, gain, gmat)


KEY_MASKED = 1e30
ONE_LANE = 24


def _scan_steps(rows):
    s, out = 1, []
    while s < rows:
        out.append(s)
        s *= 2
    return out


def _fgate_fwd(fg, b_f, lp):
    n = fg.shape[0]
    nb = n // lp

    def body(fg_ref, b_ref, ka_ref, qa_ref):
        x = fg_ref[...] + b_ref[...]
        logf = jnp.minimum(x, 0.0) - jnp.log(1.0 + jnp.exp(-jnp.abs(x)))
        t = lax.broadcasted_iota(jnp.int32, (lp, 128), 0)
        lane = lax.broadcasted_iota(jnp.int32, (lp, 128), 1)
        f = jnp.where((t >= PAD) & (lane < N_HEADS), logf, 0.0)
        for s in _scan_steps(lp):
            f = f + _shift_rows(f, s)
        hi = f.astype(BF16).astype(F32)
        rest = f - hi
        mid = rest.astype(BF16).astype(F32)
        lo = (rest - mid).astype(BF16).astype(F32)
        ones = jnp.where((lane >= ONE_LANE) & (lane < ONE_LANE + 3), 1.0, 0.0)
        hi_key = jnp.where((t < PAD) & (lane < N_HEADS), KEY_MASKED, hi)
        ka_ref[...] = (hi_key + pltpu.roll(mid, 8, 1) + pltpu.roll(lo, 16, 1) + ones).astype(BF16)
        for h in range(N_HEADS):
            minus = jnp.where((lane == h) | (lane == 8 + h) | (lane == 16 + h), -1.0, 0.0)
            terms = (jnp.where(lane == ONE_LANE, pltpu.roll(hi, ONE_LANE - h, 1), 0.0)
                     + jnp.where(lane == ONE_LANE + 1, pltpu.roll(mid, ONE_LANE + 1 - h, 1), 0.0)
                     + jnp.where(lane == ONE_LANE + 2, pltpu.roll(lo, ONE_LANE + 2 - h, 1), 0.0))
            qa_ref[:, 128 * h:128 * (h + 1)] = (minus + terms).astype(BF16)

    return pl.pallas_call(
        body, name="fgate_fwd", grid=(nb,),
        in_specs=[pl.BlockSpec((lp, 128), lambda b: (b, 0)), pl.BlockSpec((1, 128), lambda b: (0, 0))],
        out_specs=[pl.BlockSpec((lp, 128), lambda b: (b, 0)), pl.BlockSpec((lp, N_HEADS * 128), lambda b: (b, 0))],
        out_shape=[jax.ShapeDtypeStruct((n, 128), BF16), jax.ShapeDtypeStruct((n, N_HEADS * 128), BF16)],
        compiler_params=_params(("parallel",)),
    )(fg, b_f)


def _fgate_bwd(dka, dfr, fg, b_f, lp):
    n = fg.shape[0]
    nb = n // lp

    def body(dka_ref, dfr_ref, fg_ref, b_ref, dfg_ref, db_ref):
        b = pl.program_id(0)

        @pl.when(b == 0)
        def _():
            db_ref[...] = jnp.zeros_like(db_ref)

        wide = jnp.concatenate([dfr_ref[0], jnp.zeros((128 - N_HEADS, lp), F32)], axis=0)
        t = lax.broadcasted_iota(jnp.int32, (lp, 128), 0)
        lane = lax.broadcasted_iota(jnp.int32, (lp, 128), 1)
        d = jnp.where(lane < N_HEADS, dka_ref[...], 0.0) + wide.T
        for s in _scan_steps(lp):
            d = d + _shift_rows(d, -s)
        x = fg_ref[...] + b_ref[...]
        dx = jnp.where((t >= PAD) & (lane < N_HEADS), d * _sigmoid(-x), 0.0)
        dfg_ref[...] = dx
        db_ref[...] += jnp.sum(dx, axis=0, keepdims=True)

    return pl.pallas_call(
        body, name="fgate_bwd", grid=(nb,),
        in_specs=[pl.BlockSpec((lp, 128), lambda b: (b, 0)), pl.BlockSpec((1, N_HEADS, lp), lambda b: (b, 0, 0)),
                  pl.BlockSpec((lp, 128), lambda b: (b, 0)), pl.BlockSpec((1, 128), lambda b: (0, 0))],
        out_specs=[pl.BlockSpec((lp, 128), lambda b: (b, 0)), pl.BlockSpec((1, 128), lambda b: (0, 0))],
        out_shape=[jax.ShapeDtypeStruct((n, 128), F32), jax.ShapeDtypeStruct((1, 128), F32)],
        compiler_params=_params(("arbitrary",)),
    )(dka, dfr, fg, b_f)


def _head_masks():
    lane = lax.broadcasted_iota(jnp.int32, (1, 128), 1)
    return lane < HEAD_DIM


def _stack_heads(x2, first):
    zero = jnp.zeros_like(x2)
    return jnp.concatenate([jnp.where(first, x2, zero), jnp.where(first, zero, x2)], axis=0)


def _stack_heads_lanes(xt):
    r = lax.broadcasted_iota(jnp.int32, xt.shape, 0)
    zero = jnp.zeros_like(xt)
    return jnp.concatenate([jnp.where(r < HEAD_DIM, xt, zero), jnp.where(r < HEAD_DIM, zero, xt)], axis=1)


def _pair_cols(col0, col1, first):
    return jnp.where(first, col0, col1)


def _pair_rows(row0, row1):
    r = lax.broadcasted_iota(jnp.int32, (128, TQ), 0)
    return jnp.where(r < HEAD_DIM, row0, row1)


def _query_side(q_ref, qa_ref, p, first):
    q2 = q_ref[:, 128 * p:128 * (p + 1)] * 0.125
    zero = jnp.zeros_like(q2)
    top = jnp.concatenate([jnp.where(first, q2, zero), qa_ref[:, 128 * (2 * p):128 * (2 * p + 1)]], axis=1)
    bot = jnp.concatenate([jnp.where(first, zero, q2), qa_ref[:, 128 * (2 * p + 1):128 * (2 * p + 2)]], axis=1)
    return jnp.concatenate([top, bot], axis=0)


def _key_chunks(lp):
    return (lp + TK - 1) // TK


def _chunk_mask(i, c, tk):
    r = lax.broadcasted_iota(jnp.int32, (tk, 2 * TQ), 0)
    col = lax.broadcasted_iota(jnp.int32, (tk, 2 * TQ), 1)
    return (c * TK + r) <= (i * TQ + (col & (TQ - 1)))


def _causal_sweep(i, step, init):
    per = TK // TQ
    last = i // per
    carry = lax.fori_loop(0, last, lambda c, carry: step(c, carry, False, TK), init)
    tails = [lambda carry, r=r: step(last, carry, True, TQ * (r + 1)) for r in range(per)]
    return lax.switch(i % per, tails, carry)


def _transpose_bf16(x):
    return x.astype(F32).T.astype(BF16)


def _attn_fwd(q, qa, k, v, ka, gain, zc, w_out, h, lp):
    n = q.shape[0]
    nb = n // lp
    nq = lp // TQ
    lpp = _key_chunks(lp) * TK

    def body(q_ref, qa_ref, k_ref, v_ref, ka_ref, g_ref, zc_ref, w_ref, h_ref,
             z_ref, o_ref, lse_ref, hout_ref, kx_scr, vt_scr):
        i = pl.program_id(1)
        first = _head_masks()

        @pl.when(i == 0)
        def _():
            if lpp > lp:
                kx_scr[lp:lpp, :] = jnp.zeros((lpp - lp, 2 * ATTN_DIM), BF16)
                vt_scr[:, lp:lpp] = jnp.zeros((ATTN_DIM, lpp - lp), BF16)
            for p in range(N_PAIRS):
                kx_scr[0:lp, 256 * p:256 * p + 128] = k_ref[:, 128 * p:128 * (p + 1)]
                kx_scr[0:lp, 256 * p + 128:256 * (p + 1)] = ka_ref[...]
            vt_scr[:, 0:lp] = _transpose_bf16(v_ref[...])

        rhs_t = [_transpose_bf16(_query_side(q_ref, qa_ref, p, first)) for p in range(N_PAIRS)]

        def step(c, carry, masked, tk):
            koff = pl.multiple_of(c * TK, TK)
            valid = _chunk_mask(i, c, tk) if masked else None
            new = []
            sts = [_dot(kx_scr[pl.ds(koff, tk), 256 * p:256 * (p + 1)], rhs_t[p]) for p in range(N_PAIRS)]
            for p in range(N_PAIRS):
                m, l, acc = carry[p]
                st = sts[p]
                if masked:
                    st = jnp.where(valid, st, NEG)
                m_new = jnp.maximum(m, jnp.max(st, axis=0, keepdims=True))
                pt = jnp.exp(st - m_new)
                alpha = jnp.exp(m - m_new)
                l = alpha * l + jnp.sum(pt, axis=0, keepdims=True)
                pb = pt.astype(BF16)
                vt = _stack_heads_lanes(vt_scr[128 * p:128 * (p + 1), pl.ds(koff, tk)])
                pv = _dot(vt, jnp.concatenate([pb[:, 0:TQ], pb[:, TQ:]], axis=0))
                acc = acc * _pair_rows(alpha[:, 0:TQ], alpha[:, TQ:]) + pv
                new.append((m_new, l, acc))
            return tuple(new)

        init = tuple((jnp.full((1, 2 * TQ), NEG, F32), jnp.zeros((1, 2 * TQ), F32), jnp.zeros((128, TQ), F32))
                     for _ in range(N_PAIRS))
        final = _causal_sweep(i, step, init)

        row = lax.broadcasted_iota(jnp.int32, (TQ, 128), 0)
        real = (i * TQ + row) >= PAD
        zs = [zc_ref[...]]
        for p in range(N_PAIRS):
            m, l, acc = final[p]
            inv = 1.0 / l
            ot = acc * _pair_rows(inv[:, 0:TQ], inv[:, TQ:])
            sq = ot * ot
            r0 = lax.rsqrt(jnp.sum(sq[0:HEAD_DIM], axis=0, keepdims=True) * (1.0 / HEAD_DIM) + EPS)
            r1 = lax.rsqrt(jnp.sum(sq[HEAD_DIM:], axis=0, keepdims=True) * (1.0 / HEAD_DIM) + EPS)
            cols = slice(128 * p, 128 * (p + 1))
            o_ref[:, cols] = jnp.where(real, ot.T, 0.0).astype(BF16)
            z = (jnp.where(real, (ot * _pair_rows(r0, r1)).T, 0.0) * g_ref[:, cols]).astype(BF16)
            z_ref[:, cols] = z
            zs.append(z)
            lse = m + jnp.log(l)
            lse_ref[0, 2 * p:2 * p + 1, :] = lse[:, 0:TQ]
            lse_ref[0, 2 * p + 1:2 * p + 2, :] = lse[:, TQ:]
        hout_ref[...] = h_ref[...] + _dot(jnp.concatenate(zs, axis=1), w_ref[...])

    qblk = pl.BlockSpec((TQ, ATTN_DIM), lambda b, i: (b * nq + i, 0))
    qablk = pl.BlockSpec((TQ, N_HEADS * 128), lambda b, i: (b * nq + i, 0))
    seq = pl.BlockSpec((lp, ATTN_DIM), lambda b, i: (b, 0))
    rowblk = pl.BlockSpec((1, N_HEADS, TQ), lambda b, i: (b, 0, i))
    hblk = pl.BlockSpec((TQ, D_MODEL), lambda b, i: (b * nq + i, 0))
    return pl.pallas_call(
        body, name="attn_fwd", grid=(nb, nq),
        in_specs=[qblk, qablk, seq, seq, pl.BlockSpec((lp, 128), lambda b, i: (b, 0)),
                  pl.BlockSpec((1, ATTN_DIM), lambda b, i: (0, 0)), qblk,
                  pl.BlockSpec((D_MODEL, D_MODEL), lambda b, i: (0, 0)), hblk],
        out_specs=[qblk, qblk, rowblk, hblk],
        out_shape=[jax.ShapeDtypeStruct((n, ATTN_DIM), BF16), jax.ShapeDtypeStruct((n, ATTN_DIM), BF16),
                   jax.ShapeDtypeStruct((nb, N_HEADS, lp), F32), jax.ShapeDtypeStruct((n, D_MODEL), F32)],
        scratch_shapes=[pltpu.VMEM((lpp, 2 * ATTN_DIM), BF16), pltpu.VMEM((ATTN_DIM, lpp), BF16)],
        compiler_params=_params(("parallel", "arbitrary")),
    )(q, qa, k, v, ka, gain, zc, w_out, h)


def _attn_bwd(dz, q, qa, k, v, ka, o, lse, gain, lp, exchange=()):
    n = q.shape[0]
    nb = n // lp
    nq = lp // TQ
    lpp = _key_chunks(lp) * TK
    nw = len(exchange)

    def body(*refs):
        ((dz_ref, q_ref, qa_ref, k_ref, v_ref, ka_ref, o_ref, lse_ref, g_ref), xin,
         (dq_ref, dk_ref, dv_ref, dka_ref, dfr_ref, dgain_ref), xout,
         (kx_scr, vx_scr, kt_scr, dkx_scr, dvx_scr), sems) = _split_refs(refs, 9, nw, 6, 5)
        b = pl.program_id(0)
        i = pl.program_id(1)
        first = _head_masks()
        if nw:
            comm = _Exchange(xin, xout, sems)
            pl.when((b == 0) & (i == 0))(comm.start)

        @pl.when((b == 0) & (i == 0))
        def _():
            dgain_ref[...] = jnp.zeros_like(dgain_ref)

        @pl.when(i == 0)
        def _():
            if lpp > lp:
                kx_scr[lp:lpp, :] = jnp.zeros((lpp - lp, 2 * ATTN_DIM), BF16)
                vx_scr[lp:lpp, :] = jnp.zeros((lpp - lp, ATTN_DIM), BF16)
                kt_scr[:, lp:lpp] = jnp.zeros((ATTN_DIM, lpp - lp), BF16)
            for p in range(N_PAIRS):
                kx_scr[0:lp, 256 * p:256 * p + 128] = k_ref[:, 128 * p:128 * (p + 1)]
                kx_scr[0:lp, 256 * p + 128:256 * (p + 1)] = ka_ref[...]
            vx_scr[0:lp, :] = v_ref[...]
            kt_scr[:, 0:lp] = _transpose_bf16(k_ref[...])
            dkx_scr[...] = jnp.zeros_like(dkx_scr)
            dvx_scr[...] = jnp.zeros_like(dvx_scr)

        rhs, rhs_t, lses, dos, dos_t, deltas = [], [], [], [], [], []
        for p in range(N_PAIRS):
            cols = slice(128 * p, 128 * (p + 1))
            side = _query_side(q_ref, qa_ref, p, first)
            rhs.append(side)
            rhs_t.append(_transpose_bf16(side))
            lses.append(jnp.concatenate([lse_ref[0, 2 * p:2 * p + 1, :], lse_ref[0, 2 * p + 1:2 * p + 2, :]], axis=1))
            ov = o_ref[:, cols].astype(F32)
            dzv = dz_ref[:, cols].astype(F32)
            gv = g_ref[:, cols]
            sq = ov * ov
            ms0 = jnp.sum(jnp.where(first, sq, 0.0), axis=1, keepdims=True) * (1.0 / HEAD_DIM)
            ms1 = jnp.sum(jnp.where(first, 0.0, sq), axis=1, keepdims=True) * (1.0 / HEAD_DIM)
            r = _pair_cols(lax.rsqrt(ms0 + EPS), lax.rsqrt(ms1 + EPS), first)
            ohat = ov * r
            dyhat = dzv * gv
            dgain_ref[:, cols] += jnp.sum(dzv * ohat, axis=0, keepdims=True)
            pr = dyhat * ohat
            mean0 = jnp.sum(jnp.where(first, pr, 0.0), axis=1, keepdims=True) * (1.0 / HEAD_DIM)
            mean1 = jnp.sum(jnp.where(first, 0.0, pr), axis=1, keepdims=True) * (1.0 / HEAD_DIM)
            do = r * (dyhat - ohat * _pair_cols(mean0, mean1, first))
            ddt = (do * ov).T
            deltas.append(jnp.concatenate([jnp.sum(ddt[0:HEAD_DIM], axis=0, keepdims=True),
                                           jnp.sum(ddt[HEAD_DIM:], axis=0, keepdims=True)], axis=1))
            do_st = _stack_heads(do.astype(BF16), first)
            dos.append(do_st)
            dos_t.append(_transpose_bf16(do_st))

        def step(c, carry, masked, tk):
            koff = pl.multiple_of(c * TK, TK)
            valid = _chunk_mask(i, c, tk) if masked else None
            new = []
            sts = [_dot(kx_scr[pl.ds(koff, tk), 256 * p:256 * (p + 1)], rhs_t[p]) for p in range(N_PAIRS)]
            dpts = [_dot(vx_scr[pl.ds(koff, tk), 128 * p:128 * (p + 1)], dos_t[p]) for p in range(N_PAIRS)]
            for p in range(N_PAIRS):
                dqt, dfq = carry[p]
                ext = slice(256 * p, 256 * (p + 1))
                cols = slice(128 * p, 128 * (p + 1))
                st = sts[p]
                if masked:
                    st = jnp.where(valid, st, NEG)
                pt = jnp.exp(st - lses[p])
                dst = pt * (dpts[p] - deltas[p])
                dsb = dst.astype(BF16)
                dfq = dfq + jnp.sum(dsb.astype(F32), axis=0, keepdims=True)
                dkx_scr[pl.ds(koff, tk), ext] += _dot(dsb, rhs[p])
                dvx_scr[pl.ds(koff, tk), cols] += _dot(pt.astype(BF16), dos[p])
                kt = _stack_heads_lanes(kt_scr[cols, pl.ds(koff, tk)])
                dqt = dqt + _dot(kt, jnp.concatenate([dsb[:, 0:TQ], dsb[:, TQ:]], axis=0))
                new.append((dqt, dfq))
            return tuple(new)

        init = tuple((jnp.zeros((128, TQ), F32), jnp.zeros((1, 2 * TQ), F32)) for _ in range(N_PAIRS))
        final = _causal_sweep(i, step, init)

        for p in range(N_PAIRS):
            dqt, dfq = final[p]
            dq_ref[:, 128 * p:128 * (p + 1)] = (dqt.T * 0.125).astype(BF16)
            dfr_ref[0, 2 * p:2 * p + 1, :] = dfq[:, 0:TQ]
            dfr_ref[0, 2 * p + 1:2 * p + 2, :] = dfq[:, TQ:]

        @pl.when(i == nq - 1)
        def _():
            dka = jnp.zeros((lp, 128), F32)
            for p in range(N_PAIRS):
                dk_ref[:, 128 * p:128 * (p + 1)] = dkx_scr[0:lp, 256 * p:256 * p + 128].astype(BF16)
                dka = dka + dkx_scr[0:lp, 256 * p + 128:256 * (p + 1)]
            dka_ref[...] = dka
            dv_ref[...] = dvx_scr[0:lp, :].astype(BF16)

        if nw:
            pl.when((b == nb - 1) & (i == nq - 1))(comm.finish)

    qblk = pl.BlockSpec((TQ, ATTN_DIM), lambda b, i: (b * nq + i, 0))
    qablk = pl.BlockSpec((TQ, N_HEADS * 128), lambda b, i: (b * nq + i, 0))
    seq = pl.BlockSpec((lp, ATTN_DIM), lambda b, i: (b, 0))
    kaseq = pl.BlockSpec((lp, 128), lambda b, i: (b, 0))
    rowblk = pl.BlockSpec((1, N_HEADS, TQ), lambda b, i: (b, 0, i))
    gspec = pl.BlockSpec((1, ATTN_DIM), lambda b, i: (0, 0))
    return pl.pallas_call(
        body, name="attn_bwd", grid=(nb, nq),
        in_specs=[qblk, qblk, qablk, seq, seq, kaseq, qblk, rowblk, gspec] + [ANY] * nw,
        out_specs=[qblk, seq, seq, kaseq, rowblk, gspec] + [ANY] * nw,
        out_shape=[jax.ShapeDtypeStruct((n, ATTN_DIM), BF16), jax.ShapeDtypeStruct((n, ATTN_DIM), BF16),
                   jax.ShapeDtypeStruct((n, ATTN_DIM), BF16), jax.ShapeDtypeStruct((n, 128), F32),
                   jax.ShapeDtypeStruct((nb, N_HEADS, lp), F32), jax.ShapeDtypeStruct((1, ATTN_DIM), F32)]
        + [jax.ShapeDtypeStruct(a.shape, a.dtype) for a in exchange],
        scratch_shapes=[pltpu.VMEM((lpp, 2 * ATTN_DIM), BF16), pltpu.VMEM((lpp, ATTN_DIM), BF16),
                        pltpu.VMEM((ATTN_DIM, lpp), BF16), pltpu.VMEM((lpp, 2 * ATTN_DIM), F32),
                        pltpu.VMEM((lpp, ATTN_DIM), F32)] + (_comm_sems(nw) if nw else []),
        compiler_params=_params(("arbitrary", "arbitrary")),
    )(dz, q, qa, k, v, ka, o, lse, gain, *exchange)


def _adamw(parts, w, m, v, name):
    s_parts, r, c = parts.shape
    tr = r
    for t in (256, 128, 64, 32, 16):
        if r % t == 0 and r > t:
            tr = t
            break

    def body(p_ref, w_ref, m_ref, v_ref, g_ref, d_ref, nm_ref, nv_ref):
        g = p_ref[0].astype(F32)
        for s in range(1, s_parts):
            g = g + p_ref[s].astype(F32)
        nm = ADAM_B1 * m_ref[...] + (1.0 - ADAM_B1) * g
        nv = ADAM_B2 * v_ref[...] + (1.0 - ADAM_B2) * (g * g)
        m_hat = nm / (1.0 - ADAM_B1 ** ADAM_STEP)
        v_hat = nv / (1.0 - ADAM_B2 ** ADAM_STEP)
        g_ref[...] = g
        d_ref[...] = -ADAM_LR * (m_hat / (jnp.sqrt(v_hat) + ADAM_EPS) + ADAM_WD * w_ref[...])
        nm_ref[...] = nm
        nv_ref[...] = nv

    blk = pl.BlockSpec((tr, c), lambda i: (i, 0))
    return pl.pallas_call(
        body, name=name, grid=(r // tr,),
        in_specs=[pl.BlockSpec((s_parts, tr, c), lambda i: (0, i, 0)), blk, blk, blk],
        out_specs=[blk] * 4,
        out_shape=[jax.ShapeDtypeStruct((r, c), F32)] * 4,
        compiler_params=_params(("parallel",)),
    )(parts, w, m, v)


def _sum_parts(parts, name):
    s_parts, r, c = parts.shape

    def body(p_ref, out_ref):
        acc = p_ref[0]
        for s in range(1, s_parts):
            acc = acc + p_ref[s]
        out_ref[...] = acc

    return pl.pallas_call(
        body, name=name, out_shape=jax.ShapeDtypeStruct((r, c), F32),
        in_specs=[pl.BlockSpec(memory_space=pltpu.VMEM)], out_specs=pl.BlockSpec(memory_space=pltpu.VMEM),
    )(parts)


SMALL_ROWS = 184
LOSS_ROW = 181


def _pack_small(d_gains, d_gc, d_ga, d_bf, d_conv, d_meta, loss_part):
    rows = [g.reshape(8, 128) for g in d_gains]
    rows += [d_gc.reshape(4, 128), d_ga.reshape(4, 128), d_bf.reshape(1, 128)]
    rows += [d_conv.reshape(12, 128), d_meta.reshape(128, 128), jnp.pad(loss_part, ((0, 0), (0, 127)))]
    packed = jnp.concatenate(rows, axis=0)
    return jnp.pad(packed, ((0, SMALL_ROWS - packed.shape[0]), (0, 0)))


def kernel(x, meta_tokens, ffn1_norm, ffn1_w_gu, ffn1_w_down, mix_norm, w_in, conv_w, b_f, out_norm_conv, out_norm_attn, w_out, ffn2_norm, ffn2_w_gu, ffn2_w_down, final_norm, loss_target, m_meta_tokens, m_ffn1_norm, m_ffn1_w_gu, m_ffn1_w_down, m_mix_norm, m_w_in, m_conv_w, m_b_f, m_out_norm_conv, m_out_norm_attn, m_w_out, m_ffn2_norm, m_ffn2_w_gu, m_ffn2_w_down, m_final_norm, v_meta_tokens, v_ffn1_norm, v_ffn1_w_gu, v_ffn1_w_down, v_mix_norm, v_w_in, v_conv_w, v_b_f, v_out_norm_conv, v_out_norm_attn, v_w_out, v_ffn2_norm, v_ffn2_w_gu, v_ffn2_w_down, v_final_norm):
    nb, seq, _ = x.shape
    lp = PAD + N_META + seq
    me = 4 * lax.axis_index("x") + 2 * lax.axis_index("y") + lax.axis_index("c")
    shard_gu = D_FF // 4
    shard_d = D_FF // N_DEV

    small_in = jnp.concatenate(
        [meta_tokens, jnp.pad(conv_w[0], ((0, 0), (0, 128 - conv_w.shape[2]))), jnp.zeros((5, 128), F32)], axis=0)
    wgu1_8, wd1_8, small_8 = _all_gather(
        [ffn1_w_gu[0].T.astype(BF16), ffn1_w_down[0].astype(BF16), small_in], "gather_ffn1")
    meta_full = small_8[:, 0:N_META, :].transpose(1, 0, 2).reshape(N_META, D_MODEL)
    conv_full = small_8[:, N_META:N_META + 3, 0:CONV_DIM // N_DEV].transpose(1, 0, 2).reshape(3, CONV_DIM)
    wgu1 = wgu1_8.reshape(W_GU_SHAPE)
    wd1 = wd1_8.reshape(W_D_SHAPE)
    b_f_row = jnp.pad(b_f, ((0, 0), (0, 128 - N_HEADS)))
    gmat = _group_matrix()

    x2d = x.reshape(nb * seq, D_MODEL)
    later = [w_in[0].T.astype(BF16), w_out[0].astype(BF16), ffn2_w_gu[0].T.astype(BF16), ffn2_w_down[0].astype(BF16)]
    h1, n1, gate1, up1, win_8, wout_8, wgu2_8, wd2_8 = _ffn1_fwd(x2d, meta_full, lp, ffn1_norm, wgu1, wd1, later)
    wgu2 = wgu2_8.reshape(W_GU_SHAPE)
    wd2 = wd2_8.reshape(W_D_SHAPE)
    w_in_full = jnp.pad(win_8.reshape(IN_DIM, D_MODEL), ((0, IN_PAD - IN_DIM), (0, 0)))
    w_out_full = wout_8.reshape(D_MODEL, D_MODEL)

    bg, cg, hc, q, k, v, fg = _inproj_fwd(h1, mix_norm, w_in_full)
    zc = _conv_fwd(bg, cg, hc, conv_full, out_norm_conv, gmat, lp)
    ka, qa = _fgate_fwd(fg, b_f_row, lp)
    za, o, lse, h2 = _attn_fwd(q, qa, k, v, ka, out_norm_attn, zc, w_out_full, h1, lp)
    dh3, n3, gate2, up2, loss_part, d_final = _ffn2_fwd_loss(
        h2, ffn2_norm, wgu2, wd2, final_norm.reshape(1, D_MODEL), loss_target.reshape(nb * seq, D_MODEL), lp)

    dgu2, dwd2 = _ffn_bwd_act_wd(dh3, gate2, up2, wd2, "ffn2_bwd_act")
    dh2, d_ffn2 = _ffn_bwd_in(dh3, h2, ffn2_norm, dgu2, wgu2, "ffn2_bwd_in")
    dwgu2 = _ffn_bwd_wgu(n3, dgu2, "ffn2_bwd_wgu")
    dzc, dza, dwout = _outproj_bwd(dh2, zc, za, w_out_full)
    send_a = [dwgu2.reshape(N_DEV, shard_gu, D_MODEL), dwd2.reshape(N_DEV, shard_d, D_MODEL),
              dwout.reshape(N_DEV, D_MODEL // N_DEV, D_MODEL)]
    dq, dk, dv, dka, dfr, d_ga, p_wgu2, p_wd2, p_wout = _attn_bwd(
        dza, q, qa, k, v, ka, o, lse, out_norm_attn, lp, exchange=send_a)
    dfg, d_bf = _fgate_bwd(dka, dfr, fg, b_f_row, lp)
    dbg, dcg, dhc, d_conv, d_gc = _conv_bwd(dzc, bg, cg, hc, conv_full, out_norm_conv, gmat, lp)
    dh1, dwin, d_mix = _inproj_bwd([dbg, dcg, dhc, dq, dk, dv], dfg, dh2, h1, mix_norm, w_in_full)
    dwin_8 = dwin[0:IN_DIM].reshape(N_DEV, IN_DIM // N_DEV, D_MODEL)
    dgu1, dwd1, p_win = _ffn_bwd_act_wd(dh1, gate1, up1, wd1, "ffn1_bwd_act", exchange=[dwin_8])
    dwgu1, p_wd1 = _ffn_bwd_wgu(n1, dgu1, "ffn1_bwd_wgu", exchange=[dwd1.reshape(N_DEV, shard_d, D_MODEL)])
    own = dwgu1.reshape(N_DEV, shard_gu, D_MODEL)
    (got,) = _pair_exchange([own], "pair_exchange_ffn1")
    chip_sum = _pair_sum(own, got, "pair_sum_wgu1")
    dh0, d_ffn1, p_wgu1 = _ffn_bwd_in(dh1, None, ffn1_norm, dgu1, wgu1, "ffn1_bwd_in",
                                      tokens=(x2d, meta_full, lp), exchange=[chip_sum])

    dh0 = dh0.reshape(nb, lp, D_MODEL)
    grad_x = dh0[:, PAD + N_META:, :]
    d_meta = jnp.sum(dh0[:, PAD:PAD + N_META, :], axis=0)

    small = _pack_small([d_ffn1, d_mix, d_ffn2, d_final], d_gc, d_ga, d_bf, d_conv, d_meta, loss_part)
    (small_all,) = _all_gather([small], "gather_small_grads")
    small_sum = _sum_parts(small_all, "sum_small_grads")
    g_ffn1n, g_mixn, g_ffn2n, g_finaln = (small_sum[8 * t:8 * t + 8].reshape(1, D_MODEL) for t in range(4))
    g_gc = small_sum[32:36].reshape(1, CONV_DIM)
    g_ga = small_sum[36:40].reshape(1, ATTN_DIM)
    g_bf = small_sum[40:41, 0:N_HEADS]
    g_conv_full = small_sum[41:53].reshape(3, CONV_DIM)
    g_meta_full = small_sum[53:181].reshape(N_META, D_MODEL)
    g_conv = lax.dynamic_slice_in_dim(g_conv_full, me * (CONV_DIM // N_DEV), CONV_DIM // N_DEV, axis=1)
    g_meta = lax.dynamic_slice_in_dim(g_meta_full, me * (D_MODEL // N_DEV), D_MODEL // N_DEV, axis=1)

    weights = {
        "meta_tokens": (g_meta[None], meta_tokens, m_meta_tokens, v_meta_tokens),
        "ffn1_norm": (g_ffn1n[None], ffn1_norm, m_ffn1_norm, v_ffn1_norm),
        "ffn1_w_gu": (p_wgu1, ffn1_w_gu[0].T, m_ffn1_w_gu[0].T, v_ffn1_w_gu[0].T),
        "ffn1_w_down": (p_wd1, ffn1_w_down[0], m_ffn1_w_down[0], v_ffn1_w_down[0]),
        "mix_norm": (g_mixn[None], mix_norm, m_mix_norm, v_mix_norm),
        "w_in": (p_win, w_in[0].T, m_w_in[0].T, v_w_in[0].T),
        "conv_w": (g_conv[None], conv_w[0], m_conv_w[0], v_conv_w[0]),
        "b_f": (g_bf[None], b_f, m_b_f, v_b_f),
        "out_norm_conv": (g_gc[None], out_norm_conv, m_out_norm_conv, v_out_norm_conv),
        "out_norm_attn": (g_ga[None], out_norm_attn, m_out_norm_attn, v_out_norm_attn),
        "w_out": (p_wout, w_out[0], m_w_out[0], v_w_out[0]),
        "ffn2_norm": (g_ffn2n[None], ffn2_norm, m_ffn2_norm, v_ffn2_norm),
        "ffn2_w_gu": (p_wgu2, ffn2_w_gu[0].T, m_ffn2_w_gu[0].T, v_ffn2_w_gu[0].T),
        "ffn2_w_down": (p_wd2, ffn2_w_down[0], m_ffn2_w_down[0], v_ffn2_w_down[0]),
        "final_norm": (g_finaln[None], final_norm.reshape(1, D_MODEL), m_final_norm.reshape(1, D_MODEL),
                       v_final_norm.reshape(1, D_MODEL)),
    }
    shapes = {"meta_tokens": meta_tokens.shape, "ffn1_norm": ffn1_norm.shape, "ffn1_w_gu": ffn1_w_gu.shape,
              "ffn1_w_down": ffn1_w_down.shape, "mix_norm": mix_norm.shape, "w_in": w_in.shape,
              "conv_w": conv_w.shape, "b_f": b_f.shape, "out_norm_conv": out_norm_conv.shape,
              "out_norm_attn": out_norm_attn.shape, "w_out": w_out.shape, "ffn2_norm": ffn2_norm.shape,
              "ffn2_w_gu": ffn2_w_gu.shape, "ffn2_w_down": ffn2_w_down.shape, "final_norm": final_norm.shape}
    grads, deltas, new_m, new_v = [], [], [], []
    for name, (p, w, m, vv) in weights.items():
        g, d, nm, nv = _adamw(p, w, m, vv, "adamw_" + name)
        if name in ("ffn1_w_gu", "ffn2_w_gu", "w_in"):
            g, d, nm, nv = g.T, d.T, nm.T, nv.T
        shape = shapes[name]
        grads.append(g.reshape(shape))
        deltas.append(d.reshape(shape))
        new_m.append(nm.reshape(shape))
        new_v.append(nv.reshape(shape))

    loss = small_sum[LOSS_ROW, 0]
    return (loss, grad_x, *grads, *deltas, *new_m, *new_v)
```

```python
import jax
import jax.numpy as jnp
from jax import lax
from jax.experimental import pallas as pl
from jax.experimental.pallas import tpu as pltpu

F32 = jnp.float32
BF16 = jnp.bfloat16

N_DEV = 8
D_MODEL = 1024
N_META = 16
PAD = 128 - N_META
CONV_DIM = 512
ATTN_DIM = 512
HEAD_DIM = 64
N_HEADS = 8
N_PAIRS = N_HEADS // 2
D_FF = 2816
IN_DIM = 3080
IN_PAD = 3200
IN_MAIN = 3072
N_PIECE = IN_MAIN // 512
EPS = 1e-6
NEG = -1e30
TQ = 128
TK = 512
TK_FWD = 1024
VMEM_LIMIT = 56 * 1024 * 1024

HID_PIECES = ((0, 1024), (1024, 2048), (2048, D_FF))
ACT_PIECES = tuple((a, min(a + 256, D_FF)) for a in range(0, D_FF, 256))
W_GU_SHAPE = (2, D_FF, D_MODEL)
W_D_SHAPE = (D_FF, D_MODEL)

ADAM_LR = 0.001
ADAM_B1 = 0.9
ADAM_B2 = 0.999
ADAM_EPS = 1e-08
ADAM_WD = 0.01
ADAM_STEP = 10

MESH = pl.DeviceIdType.MESH
ANY = pl.BlockSpec(memory_space=pl.ANY)


def _params(sem=None):
    return pltpu.CompilerParams(dimension_semantics=sem, vmem_limit_bytes=VMEM_LIMIT)


def _row_tile(n, prefer):
    for t in (prefer, 512, 256, 128):
        if t <= n and n % t == 0:
            return t
    raise ValueError(f"no row tile for {n}")


def _resident(shape):
    zeros = (0,) * len(shape)
    return pl.BlockSpec(shape, lambda i: zeros, pipeline_mode=pl.Buffered(1))


def _dot(a, b):
    return jnp.dot(a, b, preferred_element_type=F32)


def _dot_nt(a, b):
    return lax.dot_general(a, b, (((1,), (1,)), ((), ())), preferred_element_type=F32)


def _dot_tn(a, b):
    return lax.dot_general(a, b, (((0,), (0,)), ((), ())), preferred_element_type=F32)


def _rms(x, g):
    r = lax.rsqrt(jnp.mean(x * x, axis=-1, keepdims=True) + EPS)
    xhat = x * r
    return xhat * g, xhat, r


def _rms_bwd(dn, xhat, r, g):
    dxhat = dn * g
    return r * (dxhat - xhat * jnp.mean(dxhat * xhat, axis=-1, keepdims=True))


def _sigmoid(x):
    return 1.0 / (1.0 + jnp.exp(-x))


def _place():
    return lax.axis_index("x"), lax.axis_index("y"), lax.axis_index("c")


def _comm_sems(nw):
    return [pltpu.SemaphoreType.DMA((nw, 7)), pltpu.SemaphoreType.DMA((nw, 7)), pltpu.SemaphoreType.DMA((nw,))]


def _flip(v, bit):
    return 1 - v if bit else v


class _Gather:
    def __init__(self, ins, outs, sems):
        self.ins, self.outs = ins, outs
        self.send, self.recv, self.local = sems
        x, y, c = _place()
        self.c = c
        self.me, self.sibling = (x, y, c), (x, y, 1 - c)
        first = ((x + 1 - c) % 2, (y + c) % 2)
        second = ((x + c) % 2, (y + 1 - c) % 2)
        self.chips = [first, second, (1 - x, 1 - y)]
        self.targets = [first, second, second]

    def _copy(self, w, k, block, to, own=False):
        slot = self.outs[w].at[4 * block[0] + 2 * block[1] + block[2]]
        return pltpu.make_async_remote_copy(
            src_ref=self.ins[w] if own else slot, dst_ref=slot,
            send_sem=self.send.at[w, k], recv_sem=self.recv.at[w, k], device_id=to, device_id_type=MESH)

    def _mine(self, w):
        x, y, c = self.me
        return pltpu.make_async_copy(self.ins[w], self.outs[w].at[4 * x + 2 * y + c], self.local.at[w])

    def _first(self, w):
        return ([self._copy(w, 0, self.me, self.sibling, own=True)]
                + [self._copy(w, 1 + j, self.me, (*self.targets[j], self.c), own=True) for j in range(2)])

    def _relay(self, w):
        return self._copy(w, 3, (*self.chips[0], self.c), (*self.targets[2], self.c))

    def _landed(self, w, j):
        return self._copy(w, 1 + j, (*self.chips[j], self.c), self.me)

    def _passed(self, w):
        return [self._copy(w, 4 + j, (*chip, self.c), self.sibling) for j, chip in enumerate(self.chips)]

    def start(self):
        for w in range(len(self.ins)):
            self._mine(w).start()
        for w in range(len(self.ins)):
            for cp in self._first(w):
                cp.start()

    def relay(self):
        for w in range(len(self.ins)):
            self._landed(w, 0).wait_recv()
            self._relay(w).start()
            self._passed(w)[0].start()

    def forward(self):
        for w in range(len(self.ins)):
            for j in (1, 2):
                self._landed(w, j).wait_recv()
                self._passed(w)[j].start()

    def finish(self):
        from_sibling = [self.chips[1], self.chips[0], self.chips[2]]
        for w in range(len(self.ins)):
            self._copy(w, 0, self.sibling, self.me).wait_recv()
            for j, chip in enumerate(from_sibling):
                self._copy(w, 4 + j, (*chip, 1 - self.c), self.me).wait_recv()
        for w in range(len(self.ins)):
            for cp in self._first(w) + [self._relay(w)] + self._passed(w):
                cp.wait_send()
            self._mine(w).wait()


class _Exchange:
    def __init__(self, ins, outs, sems):
        self.ins, self.outs = ins, outs
        self.send, self.recv, self.local = sems
        self.x, self.y, self.c = _place()
        self.me = 4 * self.x + 2 * self.y + self.c

    def _copy(self, w, k):
        peer = (_flip(self.x, ((k + 1) >> 2) & 1), _flip(self.y, ((k + 1) >> 1) & 1), _flip(self.c, (k + 1) & 1))
        return pltpu.make_async_remote_copy(
            src_ref=self.ins[w].at[4 * peer[0] + 2 * peer[1] + peer[2]], dst_ref=self.outs[w].at[self.me],
            send_sem=self.send.at[w, k], recv_sem=self.recv.at[w, k], device_id=peer, device_id_type=MESH)

    def _mine(self, w):
        return pltpu.make_async_copy(self.ins[w].at[self.me], self.outs[w].at[self.me], self.local.at[w])

    def start(self):
        for w in range(len(self.ins)):
            self._mine(w).start()
            for k in range(N_DEV - 1):
                self._copy(w, k).start()

    def finish(self):
        for w in range(len(self.ins)):
            for k in range(N_DEV - 1):
                self._copy(w, k).wait()
            self._mine(w).wait()


class _PairExchange:
    def __init__(self, ins, outs, sems):
        self.ins, self.outs = ins, outs
        self.send, self.recv, _ = sems
        x, y, self.c = _place()
        self.sibling = (x, y, 1 - self.c)

    def _copy(self, w, t):
        return pltpu.make_async_remote_copy(
            src_ref=self.ins[w].at[2 * t + 1 - self.c], dst_ref=self.outs[w].at[t],
            send_sem=self.send.at[w, t], recv_sem=self.recv.at[w, t], device_id=self.sibling, device_id_type=MESH)

    def start(self):
        for w in range(len(self.ins)):
            for t in range(4):
                self._copy(w, t).start()

    def finish(self):
        for w in range(len(self.ins)):
            for t in range(4):
                self._copy(w, t).wait()


class _ChipExchange:
    def __init__(self, ins, outs, sems):
        self.ins, self.outs = ins, outs
        self.send, self.recv, self.local = sems
        self.x, self.y, self.c = _place()
        self.chip = 2 * self.x + self.y

    def _copy(self, w, k):
        px, py = _flip(self.x, ((k + 1) >> 1) & 1), _flip(self.y, (k + 1) & 1)
        return pltpu.make_async_remote_copy(
            src_ref=self.ins[w].at[2 * px + py], dst_ref=self.outs[w].at[self.chip],
            send_sem=self.send.at[w, k], recv_sem=self.recv.at[w, k], device_id=(px, py, self.c),
            device_id_type=MESH)

    def _mine(self, w):
        return pltpu.make_async_copy(self.ins[w].at[self.chip], self.outs[w].at[self.chip], self.local.at[w])

    def start(self):
        for w in range(len(self.ins)):
            self._mine(w).start()
            for k in range(3):
                self._copy(w, k).start()

    def finish(self):
        for w in range(len(self.ins)):
            for k in range(3):
                self._copy(w, k).wait()
            self._mine(w).wait()


def _split_refs(refs, n_in, n_comm, n_out, n_scr):
    a = n_in
    b = a + n_comm
    c = b + n_out
    d = c + n_comm
    e = d + n_scr
    return refs[:a], refs[a:b], refs[b:c], refs[c:d], refs[d:e], refs[e:]


def _all_gather(xs, name):
    nw = len(xs)

    def body(*refs):
        comm = _Gather(refs[:nw], refs[nw:2 * nw], refs[2 * nw:])
        comm.start()
        comm.relay()
        comm.forward()
        comm.finish()

    return pl.pallas_call(
        body, name=name, in_specs=[ANY] * nw, out_specs=[ANY] * nw,
        out_shape=[jax.ShapeDtypeStruct((N_DEV,) + a.shape, a.dtype) for a in xs],
        scratch_shapes=_comm_sems(nw),
    )(*xs)


def _pair_exchange(xs, name):
    nw = len(xs)

    def body(*refs):
        comm = _PairExchange(refs[:nw], refs[nw:2 * nw], refs[2 * nw:])
        comm.start()
        comm.finish()

    return pl.pallas_call(
        body, name=name, in_specs=[ANY] * nw, out_specs=[ANY] * nw,
        out_shape=[jax.ShapeDtypeStruct((4,) + a.shape[1:], a.dtype) for a in xs],
        scratch_shapes=_comm_sems(nw),
    )(*xs)


def _pair_sum(own, got, name):
    _, r, c = own.shape
    tr = r
    for t in (256, 128, 64, 32, 16):
        if r % t == 0 and r > t:
            tr = t
            break

    def body(own_ref, got_ref, out_ref):
        mine = jnp.where(lax.axis_index("c") == 0, own_ref[:, 0].astype(F32), own_ref[:, 1].astype(F32))
        out_ref[...] = (mine + got_ref[...].astype(F32)).astype(BF16)

    return pl.pallas_call(
        body, name=name, grid=(r // tr,),
        in_specs=[pl.BlockSpec((4, 2, tr, c), lambda i: (0, 0, i, 0)), pl.BlockSpec((4, tr, c), lambda i: (0, i, 0))],
        out_specs=pl.BlockSpec((4, tr, c), lambda i: (0, i, 0)),
        out_shape=jax.ShapeDtypeStruct((4, r, c), BF16),
        compiler_params=_params(("parallel",)),
    )(own.reshape(4, 2, r, c), got)


def _token_spec(k, ksub, nq):
    def index_map(i):
        s = ksub * i + k
        return ((s // nq) * (nq - 1) + jnp.maximum(s % nq, 1) - 1, 0)
    return pl.BlockSpec((128, D_MODEL), index_map)


def _is_lead(i, k, ksub, nq):
    return ((ksub * i + k) % nq) == 0


def _assemble_rows(i, x_refs, meta_ref, nq):
    ksub = len(x_refs)
    lead = jnp.concatenate([jnp.zeros((PAD, D_MODEL), F32), meta_ref[...]], axis=0)
    return jnp.concatenate([jnp.where(_is_lead(i, k, ksub, nq), lead, x_refs[k][...]) for k in range(ksub)], axis=0)


def _swiglu(nb, wgu_ref, wd_ref, gate_ref, up_ref):
    acc = jnp.zeros((nb.shape[0], D_MODEL), F32)
    for a, b in HID_PIECES:
        gate = _dot_nt(nb, wgu_ref[0, a:b, :])
        up = _dot_nt(nb, wgu_ref[1, a:b, :])
        gate_ref[:, a:b] = gate.astype(BF16)
        up_ref[:, a:b] = up.astype(BF16)
        acc = acc + _dot((gate * _sigmoid(gate) * up).astype(BF16), wd_ref[a:b, :])
    return acc


def _ffn1_fwd(x2d, meta, lp, gain, wgu, wd, gather):
    nq = lp // 128
    n = (x2d.shape[0] // (nq - 1)) * nq
    tm = _row_tile(n, 512)
    ksub = tm // 128
    n_i = n // tm
    nw = len(gather)

    def body(*refs):
        x_refs = refs[:ksub]
        (meta_ref, g_ref, wgu_ref, wd_ref), gin, (out_ref, nrm_ref, gate_ref, up_ref), gout, _, sems = \
            _split_refs(refs[ksub:], 4, nw, 4, 0)
        i = pl.program_id(0)
        comm = _Gather(gin, gout, sems)
        pl.when(i == 0)(comm.start)
        pl.when(i == n_i // 2)(comm.relay)
        pl.when(i == max(n_i - 3, n_i // 2))(comm.forward)

        hv = _assemble_rows(i, x_refs, meta_ref, nq)
        y, _, _ = _rms(hv, g_ref[...])
        nb = y.astype(BF16)
        nrm_ref[...] = nb
        out_ref[...] = hv + 0.5 * _swiglu(nb, wgu_ref, wd_ref, gate_ref, up_ref)

        pl.when(i == n_i - 1)(comm.finish)

    rows = pl.BlockSpec((tm, D_MODEL), lambda i: (i, 0))
    hid = pl.BlockSpec((tm, D_FF), lambda i: (i, 0))
    return pl.pallas_call(
        body, name="ffn1_fwd", grid=(n_i,),
        in_specs=[_token_spec(k, ksub, nq) for k in range(ksub)]
        + [pl.BlockSpec((N_META, D_MODEL), lambda i: (0, 0)), pl.BlockSpec((1, D_MODEL), lambda i: (0, 0)),
           _resident(W_GU_SHAPE), _resident(W_D_SHAPE)] + [ANY] * nw,
        out_specs=[rows, rows, hid, hid] + [ANY] * nw,
        out_shape=[jax.ShapeDtypeStruct((n, D_MODEL), F32), jax.ShapeDtypeStruct((n, D_MODEL), BF16),
                   jax.ShapeDtypeStruct((n, D_FF), BF16), jax.ShapeDtypeStruct((n, D_FF), BF16)]
        + [jax.ShapeDtypeStruct((N_DEV,) + a.shape, a.dtype) for a in gather],
        scratch_shapes=_comm_sems(nw),
        compiler_params=_params(("arbitrary",)),
    )(*([x2d] * ksub), meta, gain, wgu, wd, *gather)


def _ffn2_fwd_loss(h, gain, wgu, wd, gfinal, target, lp):
    n = h.shape[0]
    nq = lp // 128
    tm = _row_tile(n, 512)
    ksub = tm // 128
    n_i = n // tm

    def body(*refs):
        t_refs = refs[:ksub]
        h_ref, g_ref, wgu_ref, wd_ref, gf_ref, dh_ref, nrm_ref, gate_ref, up_ref, loss_ref, dgf_ref = refs[ksub:]
        i = pl.program_id(0)

        @pl.when(i == 0)
        def _():
            loss_ref[...] = jnp.zeros_like(loss_ref)
            dgf_ref[...] = jnp.zeros_like(dgf_ref)

        hv = h_ref[...]
        y, _, _ = _rms(hv, g_ref[...])
        nb = y.astype(BF16)
        nrm_ref[...] = nb
        hout = hv + 0.5 * _swiglu(nb, wgu_ref, wd_ref, gate_ref, up_ref)

        gf = gf_ref[...]
        loss = jnp.zeros((1, 1), F32)
        dgf = jnp.zeros((1, D_MODEL), F32)
        for k in range(ksub):
            yk, xhat, r = _rms(hout[128 * k:128 * (k + 1)], gf)
            err = jnp.where(_is_lead(i, k, ksub, nq), 0.0, yk - t_refs[k][...])
            loss = loss + 0.5 * jnp.sum(jnp.sum(err * err, axis=1, keepdims=True), axis=0,
                                        keepdims=True) * (1.0 / D_MODEL)
            dy = err * (1.0 / D_MODEL)
            dh_ref[128 * k:128 * (k + 1), :] = _rms_bwd(dy, xhat, r, gf)
            dgf = dgf + jnp.sum(dy * xhat, axis=0, keepdims=True)
        loss_ref[...] += loss
        dgf_ref[...] += dgf

    rows = pl.BlockSpec((tm, D_MODEL), lambda i: (i, 0))
    hid = pl.BlockSpec((tm, D_FF), lambda i: (i, 0))
    vec = pl.BlockSpec((1, D_MODEL), lambda i: (0, 0))
    return pl.pallas_call(
        body, name="ffn2_fwd_loss", grid=(n_i,),
        in_specs=[_token_spec(k, ksub, nq) for k in range(ksub)]
        + [rows, vec, _resident(W_GU_SHAPE), _resident(W_D_SHAPE), vec],
        out_specs=[rows, rows, hid, hid, pl.BlockSpec((1, 1), lambda i: (0, 0)), vec],
        out_shape=[jax.ShapeDtypeStruct((n, D_MODEL), F32), jax.ShapeDtypeStruct((n, D_MODEL), BF16),
                   jax.ShapeDtypeStruct((n, D_FF), BF16), jax.ShapeDtypeStruct((n, D_FF), BF16),
                   jax.ShapeDtypeStruct((1, 1), F32), jax.ShapeDtypeStruct((1, D_MODEL), F32)],
        compiler_params=_params(("arbitrary",)),
    )(*([target] * ksub), h, gain, wgu, wd, gfinal)


def _ffn_bwd_act_wd(dh_out, gate, up, wd, name, exchange=()):
    n = dh_out.shape[0]
    tm = _row_tile(n, 256)
    n_i = n // tm
    nw = len(exchange)

    def body(*refs):
        (dh_ref, gate_ref, up_ref, wd_ref), xin, (dgu_ref, dw_ref), xout, (acc_scr,), sems = \
            _split_refs(refs, 4, nw, 2, 1)
        i = pl.program_id(0)
        if nw:
            comm = _Exchange(xin, xout, sems)
            pl.when(i == 0)(comm.start)

        @pl.when(i == 0)
        def _():
            acc_scr[...] = jnp.zeros_like(acc_scr)

        dhb = (0.5 * dh_ref[...]).astype(BF16)
        for a, b in ACT_PIECES:
            da = _dot_nt(dhb, wd_ref[a:b, :])
            g = gate_ref[:, a:b].astype(F32)
            u = up_ref[:, a:b].astype(F32)
            sig = _sigmoid(g)
            silu = g * sig
            dgu_ref[:, a:b] = (da * u * (sig + silu * (1.0 - sig))).astype(BF16)
            dgu_ref[:, D_FF + a:D_FF + b] = (da * silu).astype(BF16)
            acc_scr[a:b, :] += _dot_tn((silu * u).astype(BF16), dhb)

        @pl.when(i == n_i - 1)
        def _():
            dw_ref[...] = acc_scr[...].astype(BF16)

        if nw:
            pl.when(i == n_i - 1)(comm.finish)

    rows = pl.BlockSpec((tm, D_MODEL), lambda i: (i, 0))
    hid = pl.BlockSpec((tm, D_FF), lambda i: (i, 0))
    return pl.pallas_call(
        body, name=name, grid=(n_i,),
        in_specs=[rows, hid, hid, _resident(W_D_SHAPE)] + [ANY] * nw,
        out_specs=[pl.BlockSpec((tm, 2 * D_FF), lambda i: (i, 0)), _resident(W_D_SHAPE)] + [ANY] * nw,
        out_shape=[jax.ShapeDtypeStruct((n, 2 * D_FF), BF16), jax.ShapeDtypeStruct(W_D_SHAPE, BF16)]
        + [jax.ShapeDtypeStruct(a.shape, a.dtype) for a in exchange],
        scratch_shapes=[pltpu.VMEM(W_D_SHAPE, F32)] + (_comm_sems(nw) if nw else []),
        compiler_params=_params(("arbitrary",)),
    )(dh_out, gate, up, wd, *exchange)


def _ffn_bwd_in(dh_out, h_in, gain, dgu, wgu, name, tokens=None, exchange=()):
    n = dh_out.shape[0]
    tm = _row_tile(n, 512)
    n_i = n // tm
    nw = len(exchange)
    ksub, nq = (tm // 128, tokens[2] // 128) if tokens else (1, 0)

    def body(*refs):
        h_refs = refs[:ksub]
        (meta_ref, dh_ref, g_ref, dgu_ref, wgu_ref), xin, (dhin_ref, dgain_ref), xout, _, sems = \
            _split_refs(refs[ksub:], 5, nw, 2, 0)
        i = pl.program_id(0)
        if nw:
            comm = _ChipExchange(xin, xout, sems)
            pl.when(i == 0)(comm.start)

        @pl.when(i == 0)
        def _():
            dgain_ref[...] = jnp.zeros_like(dgain_ref)

        dn = _dot(dgu_ref[...], wgu_ref[...])
        gain_v = g_ref[...]
        hv = _assemble_rows(i, h_refs, meta_ref, nq) if tokens else h_refs[0][...]
        _, xhat, r = _rms(hv, gain_v)
        dhin_ref[...] = dh_ref[...] + _rms_bwd(dn, xhat, r, gain_v)
        dgain_ref[...] += jnp.sum(dn * xhat, axis=0, keepdims=True)

        if nw:
            pl.when(i == n_i - 1)(comm.finish)

    rows = pl.BlockSpec((tm, D_MODEL), lambda i: (i, 0))
    hid = pl.BlockSpec((tm, D_FF), lambda i: (i, 0))
    vec = pl.BlockSpec((1, D_MODEL), lambda i: (0, 0))
    meta_spec = pl.BlockSpec((N_META, D_MODEL), lambda i: (0, 0))
    if tokens:
        h_specs, h_args, meta = [_token_spec(k, ksub, nq) for k in range(ksub)], [tokens[0]] * ksub, tokens[1]
    else:
        h_specs, h_args, meta = [rows], [h_in], jnp.zeros((N_META, D_MODEL), F32)
    return pl.pallas_call(
        body, name=name, grid=(n_i,),
        in_specs=h_specs + [meta_spec, rows, vec, pl.BlockSpec((tm, 2 * D_FF), lambda i: (i, 0)),
                            _resident((2 * D_FF, D_MODEL))] + [ANY] * nw,
        out_specs=[rows, vec] + [ANY] * nw,
        out_shape=[jax.ShapeDtypeStruct((n, D_MODEL), F32), jax.ShapeDtypeStruct((1, D_MODEL), F32)]
        + [jax.ShapeDtypeStruct(a.shape, a.dtype) for a in exchange],
        scratch_shapes=_comm_sems(nw) if nw else [],
        compiler_params=_params(("arbitrary",)),
    )(*h_args, meta, dh_out, gain, dgu, wgu.reshape(2 * D_FF, D_MODEL), *exchange)


def _ffn_bwd_wgu(nrm, dgu, name, exchange=()):
    n = nrm.shape[0]
    tm = _row_tile(n, 512)
    n_i = n // tm
    nw = len(exchange)

    def body(*refs):
        (nrm_ref, dgu_ref), xin, (dw_ref,), xout, (acc_scr,), sems = _split_refs(refs, 2, nw, 1, 1)
        i = pl.program_id(0)
        if nw:
            comm = _Exchange(xin, xout, sems)
            pl.when(i == 0)(comm.start)

        @pl.when(i == 0)
        def _():
            acc_scr[...] = jnp.zeros_like(acc_scr)

        nb = nrm_ref[...]
        for half in (0, D_FF):
            for a, b in HID_PIECES:
                acc_scr[half + a:half + b, :] += _dot_tn(dgu_ref[:, half + a:half + b], nb)

        @pl.when(i == n_i - 1)
        def _():
            dw_ref[...] = acc_scr[...].astype(BF16)

        if nw:
            pl.when(i == n_i - 1)(comm.finish)

    shape = (2 * D_FF, D_MODEL)
    res = pl.pallas_call(
        body, name=name, grid=(n_i,),
        in_specs=[pl.BlockSpec((tm, D_MODEL), lambda i: (i, 0)),
                  pl.BlockSpec((tm, 2 * D_FF), lambda i: (i, 0))] + [ANY] * nw,
        out_specs=[_resident(shape)] + [ANY] * nw,
        out_shape=[jax.ShapeDtypeStruct(shape, BF16)] + [jax.ShapeDtypeStruct(a.shape, a.dtype) for a in exchange],
        scratch_shapes=[pltpu.VMEM(shape, F32)] + (_comm_sems(nw) if nw else []),
        compiler_params=_params(("arbitrary",)),
    )(nrm, dgu, *exchange)
    return res if nw else res[0]


def _inproj_fwd(h, gain, w_in):
    n = h.shape[0]
    tm = _row_tile(n, 512)

    def body(h_ref, g_ref, w_ref, *outs):
        y, _, _ = _rms(h_ref[...], g_ref[...])
        nb = y.astype(BF16)
        for p in range(N_PIECE):
            outs[p][...] = _dot_nt(nb, w_ref[512 * p:512 * (p + 1), :]).astype(BF16)
        outs[N_PIECE][...] = _dot_nt(nb, w_ref[IN_MAIN:IN_PAD, :])

    piece = pl.BlockSpec((tm, 512), lambda i: (i, 0))
    return pl.pallas_call(
        body, name="inproj_fwd", grid=(n // tm,),
        in_specs=[pl.BlockSpec((tm, D_MODEL), lambda i: (i, 0)),
                  pl.BlockSpec((1, D_MODEL), lambda i: (0, 0)),
                  pl.BlockSpec((IN_PAD, D_MODEL), lambda i: (0, 0))],
        out_specs=[piece] * N_PIECE + [pl.BlockSpec((tm, 128), lambda i: (i, 0))],
        out_shape=[jax.ShapeDtypeStruct((n, 512), BF16)] * N_PIECE + [jax.ShapeDtypeStruct((n, 128), F32)],
        compiler_params=_params(("parallel",)),
    )(h, gain, w_in)


def _inproj_bwd(dpieces, dfg, dh_out, h_in, gain, w_in):
    n = h_in.shape[0]
    tm = _row_tile(n, 512)
    n_i = n // tm

    def body(*refs):
        dp_refs = refs[:N_PIECE]
        dfg_ref, dh_ref, h_ref, g_ref, w_ref, dhin_ref, dw_ref, dgain_ref, acc_scr = refs[N_PIECE:]
        i = pl.program_id(0)

        @pl.when(i == 0)
        def _():
            acc_scr[...] = jnp.zeros_like(acc_scr)
            dgain_ref[...] = jnp.zeros_like(dgain_ref)

        gain_v = g_ref[...]
        y, xhat, r = _rms(h_ref[...], gain_v)
        nb = y.astype(BF16)
        dn = jnp.zeros((tm, D_MODEL), F32)
        for p in range(N_PIECE + 1):
            lo, hi = (512 * p, 512 * (p + 1)) if p < N_PIECE else (IN_MAIN, IN_PAD)
            dp = (dp_refs[p][...] if p < N_PIECE else dfg_ref[...]).astype(BF16)
            dn = dn + _dot(dp, w_ref[lo:hi, :])
            acc_scr[lo:hi, :] += _dot_tn(dp, nb)
        dhin_ref[...] = dh_ref[...] + _rms_bwd(dn, xhat, r, gain_v)
        dgain_ref[...] += jnp.sum(dn * xhat, axis=0, keepdims=True)

        @pl.when(i == n_i - 1)
        def _():
            dw_ref[...] = acc_scr[...].astype(BF16)

    piece = pl.BlockSpec((tm, 512), lambda i: (i, 0))
    rows = pl.BlockSpec((tm, D_MODEL), lambda i: (i, 0))
    vec = pl.BlockSpec((1, D_MODEL), lambda i: (0, 0))
    wspec = pl.BlockSpec((IN_PAD, D_MODEL), lambda i: (0, 0))
    return pl.pallas_call(
        body, name="inproj_bwd", grid=(n_i,),
        in_specs=[piece] * N_PIECE + [pl.BlockSpec((tm, 128), lambda i: (i, 0)), rows, rows, vec, wspec],
        out_specs=[rows, wspec, vec],
        out_shape=[jax.ShapeDtypeStruct((n, D_MODEL), F32),
                   jax.ShapeDtypeStruct((IN_PAD, D_MODEL), BF16),
                   jax.ShapeDtypeStruct((1, D_MODEL), F32)],
        scratch_shapes=[pltpu.VMEM((IN_PAD, D_MODEL), F32)],
        compiler_params=_params(("arbitrary",)),
    )(*dpieces, dfg, dh_out, h_in, gain, w_in)


def _outproj_bwd(dh, zc, za, w_out):
    n = dh.shape[0]
    tm = _row_tile(n, 512)
    n_i = n // tm

    def body(dh_ref, zc_ref, za_ref, w_ref, dzc_ref, dza_ref, dw_ref, acc_scr):
        i = pl.program_id(0)

        @pl.when(i == 0)
        def _():
            acc_scr[...] = jnp.zeros_like(acc_scr)

        dhb = dh_ref[...].astype(BF16)
        dzc_ref[...] = _dot_nt(dhb, w_ref[0:CONV_DIM, :]).astype(BF16)
        dza_ref[...] = _dot_nt(dhb, w_ref[CONV_DIM:, :]).astype(BF16)
        acc_scr[0:CONV_DIM, :] += _dot_tn(zc_ref[...], dhb)
        acc_scr[CONV_DIM:, :] += _dot_tn(za_ref[...], dhb)

        @pl.when(i == n_i - 1)
        def _():
            dw_ref[...] = acc_scr[...].astype(BF16)

    half = pl.BlockSpec((tm, 512), lambda i: (i, 0))
    wspec = pl.BlockSpec((D_MODEL, D_MODEL), lambda i: (0, 0))
    return pl.pallas_call(
        body, name="outproj_bwd", grid=(n_i,),
        in_specs=[pl.BlockSpec((tm, D_MODEL), lambda i: (i, 0)), half, half, wspec],
        out_specs=[half, half, wspec],
        out_shape=[jax.ShapeDtypeStruct((n, 512), BF16), jax.ShapeDtypeStruct((n, 512), BF16),
                   jax.ShapeDtypeStruct((D_MODEL, D_MODEL), BF16)],
        scratch_shapes=[pltpu.VMEM((D_MODEL, D_MODEL), F32)],
        compiler_params=_params(("arbitrary",)),
    )(dh, zc, za, w_out)


def _group_matrix():
    r = lax.broadcasted_iota(jnp.int32, (128, 128), 0) // HEAD_DIM
    c = lax.broadcasted_iota(jnp.int32, (128, 128), 1) // HEAD_DIM
    return jnp.where(r == c, 1.0 / HEAD_DIM, 0.0).astype(BF16)


def _group_mean(x, gmat):
    hi = x.astype(BF16)
    lo = (x - hi.astype(F32)).astype(BF16)
    return _dot(hi, gmat) + _dot(lo, gmat)


def _shift_rows(x, s):
    rows = x.shape[0]
    t = lax.broadcasted_iota(jnp.int32, x.shape, 0)
    rolled = pltpu.roll(x, s % rows, 0)
    keep = (t >= s) if s > 0 else (t < rows + s)
    return jnp.where(keep, rolled, 0.0)


def _conv_parts(bg_ref, cg_ref, hc_ref, w_ref):
    bg = bg_ref[...].astype(F32)
    cg = cg_ref[...].astype(F32)
    hc = hc_ref[...].astype(F32)
    u = cg * hc
    u1 = _shift_rows(u, 1)
    u2 = _shift_rows(u, 2)
    conv = w_ref[2:3, :] * u + w_ref[1:2, :] * u1 + w_ref[0:1, :] * u2
    return bg, cg, hc, u, u1, u2, conv


def _conv_fwd(bg, cg, hc, conv_w, gain, gmat, lp):
    n = bg.shape[0]
    nb = n // lp

    def body(bg_ref, cg_ref, hc_ref, w_ref, g_ref, gm_ref, z_ref):
        bgv, _, _, _, _, _, conv = _conv_parts(bg_ref, cg_ref, hc_ref, w_ref)
        yc = bgv * conv
        r = lax.rsqrt(_group_mean(yc * yc, gm_ref[...]) + EPS)
        z_ref[...] = (yc * r * g_ref[...]).astype(BF16)

    blk = pl.BlockSpec((lp, 128), lambda c, b: (b, c))
    return pl.pallas_call(
        body, name="conv_fwd", grid=(CONV_DIM // 128, nb),
        in_specs=[blk, blk, blk, pl.BlockSpec((3, 128), lambda c, b: (0, c)),
                  pl.BlockSpec((1, 128), lambda c, b: (0, c)), pl.BlockSpec((128, 128), lambda c, b: (0, 0))],
        out_specs=blk,
        out_shape=jax.ShapeDtypeStruct((n, CONV_DIM), BF16),
        compiler_params=_params(("parallel", "parallel")),
    )(bg, cg, hc, conv_w, gain, gmat)


def _conv_bwd(dz, bg, cg, hc, conv_w, gain, gmat, lp):
    n = bg.shape[0]
    nb = n // lp

    def body(dz_ref, bg_ref, cg_ref, hc_ref, w_ref, g_ref, gm_ref,
             dbg_ref, dcg_ref, dhc_ref, dw_ref, dgain_ref):
        b = pl.program_id(1)

        @pl.when(b == 0)
        def _():
            dw_ref[...] = jnp.zeros_like(dw_ref)
            dgain_ref[...] = jnp.zeros_like(dgain_ref)

        bgv, cgv, hcv, u, u1, u2, conv = _conv_parts(bg_ref, cg_ref, hc_ref, w_ref)
        gm = gm_ref[...]
        yc = bgv * conv
        r = lax.rsqrt(_group_mean(yc * yc, gm) + EPS)
        yhat = yc * r
        dzv = dz_ref[...].astype(F32)
        dyhat = dzv * g_ref[...]
        dgain_ref[...] += jnp.sum(dzv * yhat, axis=0, keepdims=True)
        dyc = r * (dyhat - yhat * _group_mean(dyhat * yhat, gm))
        dbg_ref[...] = (dyc * conv).astype(BF16)
        dconv = dyc * bgv
        du = (w_ref[2:3, :] * dconv + w_ref[1:2, :] * _shift_rows(dconv, -1)
              + w_ref[0:1, :] * _shift_rows(dconv, -2))
        dcg_ref[...] = (du * hcv).astype(BF16)
        dhc_ref[...] = (du * cgv).astype(BF16)
        dw_ref[0:1, :] += jnp.sum(dconv * u2, axis=0, keepdims=True)
        dw_ref[1:2, :] += jnp.sum(dconv * u1, axis=0, keepdims=True)
        dw_ref[2:3, :] += jnp.sum(dconv * u, axis=0, keepdims=True)

    blk = pl.BlockSpec((lp, 128), lambda c, b: (b, c))
    wspec = pl.BlockSpec((3, 128), lambda c, b: (0, c))
    gspec = pl.BlockSpec((1, 128), lambda c, b: (0, c))
    return pl.pallas_call(
        body, name="conv_bwd", grid=(CONV_DIM // 128, nb),
        in_specs=[blk, blk, blk, blk, wspec, gspec, pl.BlockSpec((128, 128), lambda c, b: (0, 0))],
        out_specs=[blk, blk, blk, wspec, gspec],
        out_shape=[jax.ShapeDtypeStruct((n, CONV_DIM), BF16)] * 3
        + [jax.ShapeDtypeStruct((3, CONV_DIM), F32), jax.ShapeDtypeStruct((1, CONV_DIM), F32)],
        compiler_params=_params(("parallel", "arbitrary")),
    )(dz, bg, cg, hc, conv_w, gain, gmat)


KEY_MASKED = 1e30
ONE_LANE = 24


def _scan_steps(rows):
    s, out = 1, []
    while s < rows:
        out.append(s)
        s *= 2
    return out


def _fgate_fwd(fg, b_f, lp):
    n = fg.shape[0]
    nb = n // lp

    def body(fg_ref, b_ref, ka_ref, qa_ref):
        x = fg_ref[...] + b_ref[...]
        logf = jnp.minimum(x, 0.0) - jnp.log(1.0 + jnp.exp(-jnp.abs(x)))
        t = lax.broadcasted_iota(jnp.int32, (lp, 128), 0)
        lane = lax.broadcasted_iota(jnp.int32, (lp, 128), 1)
        f = jnp.where((t >= PAD) & (lane < N_HEADS), logf, 0.0)
        for s in _scan_steps(lp):
            f = f + _shift_rows(f, s)
        hi = f.astype(BF16).astype(F32)
        rest = f - hi
        mid = rest.astype(BF16).astype(F32)
        lo = (rest - mid).astype(BF16).astype(F32)
        ones = jnp.where((lane >= ONE_LANE) & (lane < ONE_LANE + 3), 1.0, 0.0)
        hi_key = jnp.where((t < PAD) & (lane < N_HEADS), KEY_MASKED, hi)
        ka_ref[...] = (hi_key + pltpu.roll(mid, 8, 1) + pltpu.roll(lo, 16, 1) + ones).astype(BF16)
        for h in range(N_HEADS):
            minus = jnp.where((lane == h) | (lane == 8 + h) | (lane == 16 + h), -1.0, 0.0)
            terms = (jnp.where(lane == ONE_LANE, pltpu.roll(hi, ONE_LANE - h, 1), 0.0)
                     + jnp.where(lane == ONE_LANE + 1, pltpu.roll(mid, ONE_LANE + 1 - h, 1), 0.0)
                     + jnp.where(lane == ONE_LANE + 2, pltpu.roll(lo, ONE_LANE + 2 - h, 1), 0.0))
            qa_ref[:, 128 * h:128 * (h + 1)] = (minus + terms).astype(BF16)

    return pl.pallas_call(
        body, name="fgate_fwd", grid=(nb,),
        in_specs=[pl.BlockSpec((lp, 128), lambda b: (b, 0)), pl.BlockSpec((1, 128), lambda b: (0, 0))],
        out_specs=[pl.BlockSpec((lp, 128), lambda b: (b, 0)), pl.BlockSpec((lp, N_HEADS * 128), lambda b: (b, 0))],
        out_shape=[jax.ShapeDtypeStruct((n, 128), BF16), jax.ShapeDtypeStruct((n, N_HEADS * 128), BF16)],
        compiler_params=_params(("parallel",)),
    )(fg, b_f)


def _fgate_bwd(dka, dfr, fg, b_f, lp):
    n = fg.shape[0]
    nb = n // lp

    def body(dka_ref, dfr_ref, fg_ref, b_ref, dfg_ref, db_ref):
        b = pl.program_id(0)

        @pl.when(b == 0)
        def _():
            db_ref[...] = jnp.zeros_like(db_ref)

        wide = jnp.concatenate([dfr_ref[0], jnp.zeros((128 - N_HEADS, lp), F32)], axis=0)
        t = lax.broadcasted_iota(jnp.int32, (lp, 128), 0)
        lane = lax.broadcasted_iota(jnp.int32, (lp, 128), 1)
        d = jnp.where(lane < N_HEADS, dka_ref[...], 0.0) + wide.T
        for s in _scan_steps(lp):
            d = d + _shift_rows(d, -s)
        x = fg_ref[...] + b_ref[...]
        dx = jnp.where((t >= PAD) & (lane < N_HEADS), d * _sigmoid(-x), 0.0)
        dfg_ref[...] = dx
        db_ref[...] += jnp.sum(dx, axis=0, keepdims=True)

    return pl.pallas_call(
        body, name="fgate_bwd", grid=(nb,),
        in_specs=[pl.BlockSpec((lp, 128), lambda b: (b, 0)), pl.BlockSpec((1, N_HEADS, lp), lambda b: (b, 0, 0)),
                  pl.BlockSpec((lp, 128), lambda b: (b, 0)), pl.BlockSpec((1, 128), lambda b: (0, 0))],
        out_specs=[pl.BlockSpec((lp, 128), lambda b: (b, 0)), pl.BlockSpec((1, 128), lambda b: (0, 0))],
        out_shape=[jax.ShapeDtypeStruct((n, 128), F32), jax.ShapeDtypeStruct((1, 128), F32)],
        compiler_params=_params(("arbitrary",)),
    )(dka, dfr, fg, b_f)


def _head_masks():
    lane = lax.broadcasted_iota(jnp.int32, (1, 128), 1)
    return lane < HEAD_DIM


def _stack_heads(x2, first):
    zero = jnp.zeros_like(x2)
    return jnp.concatenate([jnp.where(first, x2, zero), jnp.where(first, zero, x2)], axis=0)


def _stack_heads_lanes(xt):
    r = lax.broadcasted_iota(jnp.int32, xt.shape, 0)
    zero = jnp.zeros_like(xt)
    return jnp.concatenate([jnp.where(r < HEAD_DIM, xt, zero), jnp.where(r < HEAD_DIM, zero, xt)], axis=1)


def _pair_cols(col0, col1, first):
    return jnp.where(first, col0, col1)


def _pair_rows(row0, row1):
    r = lax.broadcasted_iota(jnp.int32, (128, TQ), 0)
    return jnp.where(r < HEAD_DIM, row0, row1)


def _query_side(q_ref, qa_ref, p, first):
    q2 = q_ref[:, 128 * p:128 * (p + 1)] * 0.125
    zero = jnp.zeros_like(q2)
    top = jnp.concatenate([jnp.where(first, q2, zero), qa_ref[:, 128 * (2 * p):128 * (2 * p + 1)]], axis=1)
    bot = jnp.concatenate([jnp.where(first, zero, q2), qa_ref[:, 128 * (2 * p + 1):128 * (2 * p + 2)]], axis=1)
    return jnp.concatenate([top, bot], axis=0)


def _padded_keys(lp, chunk):
    return ((lp + chunk - 1) // chunk) * chunk


def _chunk_mask(i, c, tk, chunk):
    r = lax.broadcasted_iota(jnp.int32, (tk, 2 * TQ), 0)
    col = lax.broadcasted_iota(jnp.int32, (tk, 2 * TQ), 1)
    return (c * chunk + r) <= (i * TQ + (col & (TQ - 1)))


def _causal_sweep(i, step, init, chunk):
    per = chunk // TQ
    last = i // per
    carry = lax.fori_loop(0, last, lambda c, carry: step(c, carry, False, chunk), init)
    tails = [lambda carry, r=r: step(last, carry, True, TQ * (r + 1)) for r in range(per)]
    return lax.switch(i % per, tails, carry)


def _transpose_bf16(x):
    return x.astype(F32).T.astype(BF16)


def _attn_fwd(q, qa, k, v, ka, gain, zc, w_out, h, lp):
    n = q.shape[0]
    nb = n // lp
    nq = lp // TQ
    lpp = _padded_keys(lp, TK_FWD)

    def body(q_ref, qa_ref, k_ref, v_ref, ka_ref, g_ref, zc_ref, w_ref, h_ref,
             z_ref, o_ref, lse_ref, hout_ref, kx_scr, vt_scr):
        i = pl.program_id(1)
        first = _head_masks()

        @pl.when(i == 0)
        def _():
            if lpp > lp:
                kx_scr[lp:lpp, :] = jnp.zeros((lpp - lp, 2 * ATTN_DIM), BF16)
                vt_scr[:, lp:lpp] = jnp.zeros((ATTN_DIM, lpp - lp), BF16)
            for p in range(N_PAIRS):
                kx_scr[0:lp, 256 * p:256 * p + 128] = k_ref[:, 128 * p:128 * (p + 1)]
                kx_scr[0:lp, 256 * p + 128:256 * (p + 1)] = ka_ref[...]
            vt_scr[:, 0:lp] = _transpose_bf16(v_ref[...])

        rhs_t = [_transpose_bf16(_query_side(q_ref, qa_ref, p, first)) for p in range(N_PAIRS)]

        def step(c, carry, masked, tk):
            koff = pl.multiple_of(c * TK_FWD, TK_FWD)
            valid = _chunk_mask(i, c, tk, TK_FWD) if masked else None
            new = []
            sts = [_dot(kx_scr[pl.ds(koff, tk), 256 * p:256 * (p + 1)], rhs_t[p]) for p in range(N_PAIRS)]
            for p in range(N_PAIRS):
                m, l, acc = carry[p]
                st = sts[p]
                if masked:
                    st = jnp.where(valid, st, NEG)
                m_new = jnp.maximum(m, jnp.max(st, axis=0, keepdims=True))
                pt = jnp.exp(st - m_new)
                alpha = jnp.exp(m - m_new)
                l = alpha * l + jnp.sum(pt, axis=0, keepdims=True)
                pb = pt.astype(BF16)
                vt = _stack_heads_lanes(vt_scr[128 * p:128 * (p + 1), pl.ds(koff, tk)])
                pv = _dot(vt, jnp.concatenate([pb[:, 0:TQ], pb[:, TQ:]], axis=0))
                acc = acc * _pair_rows(alpha[:, 0:TQ], alpha[:, TQ:]) + pv
                new.append((m_new, l, acc))
            return tuple(new)

        init = tuple((jnp.full((1, 2 * TQ), NEG, F32), jnp.zeros((1, 2 * TQ), F32), jnp.zeros((128, TQ), F32))
                     for _ in range(N_PAIRS))
        final = _causal_sweep(i, step, init, TK_FWD)

        row = lax.broadcasted_iota(jnp.int32, (TQ, 128), 0)
        real = (i * TQ + row) >= PAD
        zs = [zc_ref[...]]
        for p in range(N_PAIRS):
            m, l, acc = final[p]
            inv = 1.0 / l
            ot = acc * _pair_rows(inv[:, 0:TQ], inv[:, TQ:])
            sq = ot * ot
            r0 = lax.rsqrt(jnp.sum(sq[0:HEAD_DIM], axis=0, keepdims=True) * (1.0 / HEAD_DIM) + EPS)
            r1 = lax.rsqrt(jnp.sum(sq[HEAD_DIM:], axis=0, keepdims=True) * (1.0 / HEAD_DIM) + EPS)
            cols = slice(128 * p, 128 * (p + 1))
            o_ref[:, cols] = jnp.where(real, ot.T, 0.0).astype(BF16)
            z = (jnp.where(real, (ot * _pair_rows(r0, r1)).T, 0.0) * g_ref[:, cols]).astype(BF16)
            z_ref[:, cols] = z
            zs.append(z)
            lse = m + jnp.log(l)
            lse_ref[0, 2 * p:2 * p + 1, :] = lse[:, 0:TQ]
            lse_ref[0, 2 * p + 1:2 * p + 2, :] = lse[:, TQ:]
        hout_ref[...] = h_ref[...] + _dot(jnp.concatenate(zs, axis=1), w_ref[...])

    qblk = pl.BlockSpec((TQ, ATTN_DIM), lambda b, i: (b * nq + i, 0))
    qablk = pl.BlockSpec((TQ, N_HEADS * 128), lambda b, i: (b * nq + i, 0))
    seq = pl.BlockSpec((lp, ATTN_DIM), lambda b, i: (b, 0))
    rowblk = pl.BlockSpec((1, N_HEADS, TQ), lambda b, i: (b, 0, i))
    hblk = pl.BlockSpec((TQ, D_MODEL), lambda b, i: (b * nq + i, 0))
    return pl.pallas_call(
        body, name="attn_fwd", grid=(nb, nq),
        in_specs=[qblk, qablk, seq, seq, pl.BlockSpec((lp, 128), lambda b, i: (b, 0)),
                  pl.BlockSpec((1, ATTN_DIM), lambda b, i: (0, 0)), qblk,
                  pl.BlockSpec((D_MODEL, D_MODEL), lambda b, i: (0, 0)), hblk],
        out_specs=[qblk, qblk, rowblk, hblk],
        out_shape=[jax.ShapeDtypeStruct((n, ATTN_DIM), BF16), jax.ShapeDtypeStruct((n, ATTN_DIM), BF16),
                   jax.ShapeDtypeStruct((nb, N_HEADS, lp), F32), jax.ShapeDtypeStruct((n, D_MODEL), F32)],
        scratch_shapes=[pltpu.VMEM((lpp, 2 * ATTN_DIM), BF16), pltpu.VMEM((ATTN_DIM, lpp), BF16)],
        compiler_params=_params(("parallel", "arbitrary")),
    )(q, qa, k, v, ka, gain, zc, w_out, h)


def _attn_bwd(dz, q, qa, k, v, ka, o, lse, gain, lp, exchange=()):
    n = q.shape[0]
    nb = n // lp
    nq = lp // TQ
    lpp = _padded_keys(lp, TK)
    nw = len(exchange)

    def body(*refs):
        ((dz_ref, q_ref, qa_ref, k_ref, v_ref, ka_ref, o_ref, lse_ref, g_ref), xin,
         (dq_ref, dk_ref, dv_ref, dka_ref, dfr_ref, dgain_ref), xout,
         (kx_scr, vx_scr, kt_scr, dkx_scr, dvx_scr), sems) = _split_refs(refs, 9, nw, 6, 5)
        b = pl.program_id(0)
        i = pl.program_id(1)
        first = _head_masks()
        if nw:
            comm = _Exchange(xin, xout, sems)
            pl.when((b == 0) & (i == 0))(comm.start)

        @pl.when((b == 0) & (i == 0))
        def _():
            dgain_ref[...] = jnp.zeros_like(dgain_ref)

        @pl.when(i == 0)
        def _():
            if lpp > lp:
                kx_scr[lp:lpp, :] = jnp.zeros((lpp - lp, 2 * ATTN_DIM), BF16)
                vx_scr[lp:lpp, :] = jnp.zeros((lpp - lp, ATTN_DIM), BF16)
                kt_scr[:, lp:lpp] = jnp.zeros((ATTN_DIM, lpp - lp), BF16)
            for p in range(N_PAIRS):
                kx_scr[0:lp, 256 * p:256 * p + 128] = k_ref[:, 128 * p:128 * (p + 1)]
                kx_scr[0:lp, 256 * p + 128:256 * (p + 1)] = ka_ref[...]
            vx_scr[0:lp, :] = v_ref[...]
            kt_scr[:, 0:lp] = _transpose_bf16(k_ref[...])
            dkx_scr[...] = jnp.zeros_like(dkx_scr)
            dvx_scr[...] = jnp.zeros_like(dvx_scr)

        rhs, rhs_t, lses, dos, dos_t, deltas = [], [], [], [], [], []
        for p in range(N_PAIRS):
            cols = slice(128 * p, 128 * (p + 1))
            side = _query_side(q_ref, qa_ref, p, first)
            rhs.append(side)
            rhs_t.append(_transpose_bf16(side))
            lses.append(jnp.concatenate([lse_ref[0, 2 * p:2 * p + 1, :], lse_ref[0, 2 * p + 1:2 * p + 2, :]], axis=1))
            ov = o_ref[:, cols].astype(F32)
            dzv = dz_ref[:, cols].astype(F32)
            gv = g_ref[:, cols]
            sq = ov * ov
            ms0 = jnp.sum(jnp.where(first, sq, 0.0), axis=1, keepdims=True) * (1.0 / HEAD_DIM)
            ms1 = jnp.sum(jnp.where(first, 0.0, sq), axis=1, keepdims=True) * (1.0 / HEAD_DIM)
            r = _pair_cols(lax.rsqrt(ms0 + EPS), lax.rsqrt(ms1 + EPS), first)
            ohat = ov * r
            dyhat = dzv * gv
            dgain_ref[:, cols] += jnp.sum(dzv * ohat, axis=0, keepdims=True)
            pr = dyhat * ohat
            mean0 = jnp.sum(jnp.where(first, pr, 0.0), axis=1, keepdims=True) * (1.0 / HEAD_DIM)
            mean1 = jnp.sum(jnp.where(first, 0.0, pr), axis=1, keepdims=True) * (1.0 / HEAD_DIM)
            do = r * (dyhat - ohat * _pair_cols(mean0, mean1, first))
            ddt = (do * ov).T
            deltas.append(jnp.concatenate([jnp.sum(ddt[0:HEAD_DIM], axis=0, keepdims=True),
                                           jnp.sum(ddt[HEAD_DIM:], axis=0, keepdims=True)], axis=1))
            do_st = _stack_heads(do.astype(BF16), first)
            dos.append(do_st)
            dos_t.append(_transpose_bf16(do_st))

        def step(c, carry, masked, tk):
            koff = pl.multiple_of(c * TK, TK)
            valid = _chunk_mask(i, c, tk, TK) if masked else None
            new = []
            sts = [_dot(kx_scr[pl.ds(koff, tk), 256 * p:256 * (p + 1)], rhs_t[p]) for p in range(N_PAIRS)]
            dpts = [_dot(vx_scr[pl.ds(koff, tk), 128 * p:128 * (p + 1)], dos_t[p]) for p in range(N_PAIRS)]
            for p in range(N_PAIRS):
                dqt, dfq = carry[p]
                ext = slice(256 * p, 256 * (p + 1))
                cols = slice(128 * p, 128 * (p + 1))
                st = sts[p]
                if masked:
                    st = jnp.where(valid, st, NEG)
                pt = jnp.exp(st - lses[p])
                dst = pt * (dpts[p] - deltas[p])
                dsb = dst.astype(BF16)
                dfq = dfq + jnp.sum(dsb.astype(F32), axis=0, keepdims=True)
                dkx_scr[pl.ds(koff, tk), ext] += _dot(dsb, rhs[p])
                dvx_scr[pl.ds(koff, tk), cols] += _dot(pt.astype(BF16), dos[p])
                kt = _stack_heads_lanes(kt_scr[cols, pl.ds(koff, tk)])
                dqt = dqt + _dot(kt, jnp.concatenate([dsb[:, 0:TQ], dsb[:, TQ:]], axis=0))
                new.append((dqt, dfq))
            return tuple(new)

        init = tuple((jnp.zeros((128, TQ), F32), jnp.zeros((1, 2 * TQ), F32)) for _ in range(N_PAIRS))
        final = _causal_sweep(i, step, init, TK)

        for p in range(N_PAIRS):
            dqt, dfq = final[p]
            dq_ref[:, 128 * p:128 * (p + 1)] = (dqt.T * 0.125).astype(BF16)
            dfr_ref[0, 2 * p:2 * p + 1, :] = dfq[:, 0:TQ]
            dfr_ref[0, 2 * p + 1:2 * p + 2, :] = dfq[:, TQ:]

        @pl.when(i == nq - 1)
        def _():
            dka = jnp.zeros((lp, 128), F32)
            for p in range(N_PAIRS):
                dk_ref[:, 128 * p:128 * (p + 1)] = dkx_scr[0:lp, 256 * p:256 * p + 128].astype(BF16)
                dka = dka + dkx_scr[0:lp, 256 * p + 128:256 * (p + 1)]
            dka_ref[...] = dka
            dv_ref[...] = dvx_scr[0:lp, :].astype(BF16)

        if nw:
            pl.when((b == nb - 1) & (i == nq - 1))(comm.finish)

    qblk = pl.BlockSpec((TQ, ATTN_DIM), lambda b, i: (b * nq + i, 0))
    qablk = pl.BlockSpec((TQ, N_HEADS * 128), lambda b, i: (b * nq + i, 0))
    seq = pl.BlockSpec((lp, ATTN_DIM), lambda b, i: (b, 0))
    kaseq = pl.BlockSpec((lp, 128), lambda b, i: (b, 0))
    rowblk = pl.BlockSpec((1, N_HEADS, TQ), lambda b, i: (b, 0, i))
    gspec = pl.BlockSpec((1, ATTN_DIM), lambda b, i: (0, 0))
    return pl.pallas_call(
        body, name="attn_bwd", grid=(nb, nq),
        in_specs=[qblk, qblk, qablk, seq, seq, kaseq, qblk, rowblk, gspec] + [ANY] * nw,
        out_specs=[qblk, seq, seq, kaseq, rowblk, gspec] + [ANY] * nw,
        out_shape=[jax.ShapeDtypeStruct((n, ATTN_DIM), BF16), jax.ShapeDtypeStruct((n, ATTN_DIM), BF16),
                   jax.ShapeDtypeStruct((n, ATTN_DIM), BF16), jax.ShapeDtypeStruct((n, 128), F32),
                   jax.ShapeDtypeStruct((nb, N_HEADS, lp), F32), jax.ShapeDtypeStruct((1, ATTN_DIM), F32)]
        + [jax.ShapeDtypeStruct(a.shape, a.dtype) for a in exchange],
        scratch_shapes=[pltpu.VMEM((lpp, 2 * ATTN_DIM), BF16), pltpu.VMEM((lpp, ATTN_DIM), BF16),
                        pltpu.VMEM((ATTN_DIM, lpp), BF16), pltpu.VMEM((lpp, 2 * ATTN_DIM), F32),
                        pltpu.VMEM((lpp, ATTN_DIM), F32)] + (_comm_sems(nw) if nw else []),
        compiler_params=_params(("arbitrary", "arbitrary")),
    )(dz, q, qa, k, v, ka, o, lse, gain, *exchange)


def _adamw(parts, w, m, v, name):
    s_parts, r, c = parts.shape
    tr = r
    for t in (256, 128, 64, 32, 16):
        if r % t == 0 and r > t:
            tr = t
            break

    def body(p_ref, w_ref, m_ref, v_ref, g_ref, d_ref, nm_ref, nv_ref):
        g = p_ref[0].astype(F32)
        for s in range(1, s_parts):
            g = g + p_ref[s].astype(F32)
        nm = ADAM_B1 * m_ref[...] + (1.0 - ADAM_B1) * g
        nv = ADAM_B2 * v_ref[...] + (1.0 - ADAM_B2) * (g * g)
        m_hat = nm / (1.0 - ADAM_B1 ** ADAM_STEP)
        v_hat = nv / (1.0 - ADAM_B2 ** ADAM_STEP)
        g_ref[...] = g
        d_ref[...] = -ADAM_LR * (m_hat / (jnp.sqrt(v_hat) + ADAM_EPS) + ADAM_WD * w_ref[...])
        nm_ref[...] = nm
        nv_ref[...] = nv

    blk = pl.BlockSpec((tr, c), lambda i: (i, 0))
    return pl.pallas_call(
        body, name=name, grid=(r // tr,),
        in_specs=[pl.BlockSpec((s_parts, tr, c), lambda i: (0, i, 0)), blk, blk, blk],
        out_specs=[blk] * 4,
        out_shape=[jax.ShapeDtypeStruct((r, c), F32)] * 4,
        compiler_params=_params(("parallel",)),
    )(parts, w, m, v)


def _sum_parts(parts, name):
    s_parts, r, c = parts.shape

    def body(p_ref, out_ref):
        acc = p_ref[0]
        for s in range(1, s_parts):
            acc = acc + p_ref[s]
        out_ref[...] = acc

    return pl.pallas_call(
        body, name=name, out_shape=jax.ShapeDtypeStruct((r, c), F32),
        in_specs=[pl.BlockSpec(memory_space=pltpu.VMEM)], out_specs=pl.BlockSpec(memory_space=pltpu.VMEM),
    )(parts)


SMALL_ROWS = 184
LOSS_ROW = 181


def _pack_small(d_gains, d_gc, d_ga, d_bf, d_conv, d_meta, loss_part):
    rows = [g.reshape(8, 128) for g in d_gains]
    rows += [d_gc.reshape(4, 128), d_ga.reshape(4, 128), d_bf.reshape(1, 128)]
    rows += [d_conv.reshape(12, 128), d_meta.reshape(128, 128), jnp.pad(loss_part, ((0, 0), (0, 127)))]
    packed = jnp.concatenate(rows, axis=0)
    return jnp.pad(packed, ((0, SMALL_ROWS - packed.shape[0]), (0, 0)))


def kernel(x, meta_tokens, ffn1_norm, ffn1_w_gu, ffn1_w_down, mix_norm, w_in, conv_w, b_f, out_norm_conv, out_norm_attn, w_out, ffn2_norm, ffn2_w_gu, ffn2_w_down, final_norm, loss_target, m_meta_tokens, m_ffn1_norm, m_ffn1_w_gu, m_ffn1_w_down, m_mix_norm, m_w_in, m_conv_w, m_b_f, m_out_norm_conv, m_out_norm_attn, m_w_out, m_ffn2_norm, m_ffn2_w_gu, m_ffn2_w_down, m_final_norm, v_meta_tokens, v_ffn1_norm, v_ffn1_w_gu, v_ffn1_w_down, v_mix_norm, v_w_in, v_conv_w, v_b_f, v_out_norm_conv, v_out_norm_attn, v_w_out, v_ffn2_norm, v_ffn2_w_gu, v_ffn2_w_down, v_final_norm):
    nb, seq, _ = x.shape
    lp = PAD + N_META + seq
    me = 4 * lax.axis_index("x") + 2 * lax.axis_index("y") + lax.axis_index("c")
    shard_gu = D_FF // 4
    shard_d = D_FF // N_DEV

    small_in = jnp.concatenate(
        [meta_tokens, jnp.pad(conv_w[0], ((0, 0), (0, 128 - conv_w.shape[2]))), jnp.zeros((5, 128), F32)], axis=0)
    wgu1_8, wd1_8, small_8 = _all_gather(
        [ffn1_w_gu[0].T.astype(BF16), ffn1_w_down[0].astype(BF16), small_in], "gather_ffn1")
    meta_full = small_8[:, 0:N_META, :].transpose(1, 0, 2).reshape(N_META, D_MODEL)
    conv_full = small_8[:, N_META:N_META + 3, 0:CONV_DIM // N_DEV].transpose(1, 0, 2).reshape(3, CONV_DIM)
    wgu1 = wgu1_8.reshape(W_GU_SHAPE)
    wd1 = wd1_8.reshape(W_D_SHAPE)
    b_f_row = jnp.pad(b_f, ((0, 0), (0, 128 - N_HEADS)))
    gmat = _group_matrix()

    x2d = x.reshape(nb * seq, D_MODEL)
    later = [w_in[0].T.astype(BF16), w_out[0].astype(BF16), ffn2_w_gu[0].T.astype(BF16), ffn2_w_down[0].astype(BF16)]
    h1, n1, gate1, up1, win_8, wout_8, wgu2_8, wd2_8 = _ffn1_fwd(x2d, meta_full, lp, ffn1_norm, wgu1, wd1, later)
    wgu2 = wgu2_8.reshape(W_GU_SHAPE)
    wd2 = wd2_8.reshape(W_D_SHAPE)
    w_in_full = jnp.pad(win_8.reshape(IN_DIM, D_MODEL), ((0, IN_PAD - IN_DIM), (0, 0)))
    w_out_full = wout_8.reshape(D_MODEL, D_MODEL)

    bg, cg, hc, q, k, v, fg = _inproj_fwd(h1, mix_norm, w_in_full)
    zc = _conv_fwd(bg, cg, hc, conv_full, out_norm_conv, gmat, lp)
    ka, qa = _fgate_fwd(fg, b_f_row, lp)
    za, o, lse, h2 = _attn_fwd(q, qa, k, v, ka, out_norm_attn, zc, w_out_full, h1, lp)
    dh3, n3, gate2, up2, loss_part, d_final = _ffn2_fwd_loss(
        h2, ffn2_norm, wgu2, wd2, final_norm.reshape(1, D_MODEL), loss_target.reshape(nb * seq, D_MODEL), lp)

    dgu2, dwd2 = _ffn_bwd_act_wd(dh3, gate2, up2, wd2, "ffn2_bwd_act")
    dh2, d_ffn2 = _ffn_bwd_in(dh3, h2, ffn2_norm, dgu2, wgu2, "ffn2_bwd_in")
    dwgu2 = _ffn_bwd_wgu(n3, dgu2, "ffn2_bwd_wgu")
    dzc, dza, dwout = _outproj_bwd(dh2, zc, za, w_out_full)
    send_a = [dwgu2.reshape(N_DEV, shard_gu, D_MODEL), dwd2.reshape(N_DEV, shard_d, D_MODEL),
              dwout.reshape(N_DEV, D_MODEL // N_DEV, D_MODEL)]
    dq, dk, dv, dka, dfr, d_ga, p_wgu2, p_wd2, p_wout = _attn_bwd(
        dza, q, qa, k, v, ka, o, lse, out_norm_attn, lp, exchange=send_a)
    dfg, d_bf = _fgate_bwd(dka, dfr, fg, b_f_row, lp)
    dbg, dcg, dhc, d_conv, d_gc = _conv_bwd(dzc, bg, cg, hc, conv_full, out_norm_conv, gmat, lp)
    dh1, dwin, d_mix = _inproj_bwd([dbg, dcg, dhc, dq, dk, dv], dfg, dh2, h1, mix_norm, w_in_full)
    dwin_8 = dwin[0:IN_DIM].reshape(N_DEV, IN_DIM // N_DEV, D_MODEL)
    dgu1, dwd1, p_win = _ffn_bwd_act_wd(dh1, gate1, up1, wd1, "ffn1_bwd_act", exchange=[dwin_8])
    dwgu1, p_wd1 = _ffn_bwd_wgu(n1, dgu1, "ffn1_bwd_wgu", exchange=[dwd1.reshape(N_DEV, shard_d, D_MODEL)])
    own = dwgu1.reshape(N_DEV, shard_gu, D_MODEL)
    (got,) = _pair_exchange([own], "pair_exchange_ffn1")
    chip_sum = _pair_sum(own, got, "pair_sum_wgu1")
    dh0, d_ffn1, p_wgu1 = _ffn_bwd_in(dh1, None, ffn1_norm, dgu1, wgu1, "ffn1_bwd_in",
                                      tokens=(x2d, meta_full, lp), exchange=[chip_sum])

    dh0 = dh0.reshape(nb, lp, D_MODEL)
    grad_x = dh0[:, PAD + N_META:, :]
    d_meta = jnp.sum(dh0[:, PAD:PAD + N_META, :], axis=0)

    small = _pack_small([d_ffn1, d_mix, d_ffn2, d_final], d_gc, d_ga, d_bf, d_conv, d_meta, loss_part)
    (small_all,) = _all_gather([small], "gather_small_grads")
    small_sum = _sum_parts(small_all, "sum_small_grads")
    g_ffn1n, g_mixn, g_ffn2n, g_finaln = (small_sum[8 * t:8 * t + 8].reshape(1, D_MODEL) for t in range(4))
    g_gc = small_sum[32:36].reshape(1, CONV_DIM)
    g_ga = small_sum[36:40].reshape(1, ATTN_DIM)
    g_bf = small_sum[40:41, 0:N_HEADS]
    g_conv_full = small_sum[41:53].reshape(3, CONV_DIM)
    g_meta_full = small_sum[53:181].reshape(N_META, D_MODEL)
    g_conv = lax.dynamic_slice_in_dim(g_conv_full, me * (CONV_DIM // N_DEV), CONV_DIM // N_DEV, axis=1)
    g_meta = lax.dynamic_slice_in_dim(g_meta_full, me * (D_MODEL // N_DEV), D_MODEL // N_DEV, axis=1)

    weights = {
        "meta_tokens": (g_meta[None], meta_tokens, m_meta_tokens, v_meta_tokens),
        "ffn1_norm": (g_ffn1n[None], ffn1_norm, m_ffn1_norm, v_ffn1_norm),
        "ffn1_w_gu": (p_wgu1, ffn1_w_gu[0].T, m_ffn1_w_gu[0].T, v_ffn1_w_gu[0].T),
        "ffn1_w_down": (p_wd1, ffn1_w_down[0], m_ffn1_w_down[0], v_ffn1_w_down[0]),
        "mix_norm": (g_mixn[None], mix_norm, m_mix_norm, v_mix_norm),
        "w_in": (p_win, w_in[0].T, m_w_in[0].T, v_w_in[0].T),
        "conv_w": (g_conv[None], conv_w[0], m_conv_w[0], v_conv_w[0]),
        "b_f": (g_bf[None], b_f, m_b_f, v_b_f),
        "out_norm_conv": (g_gc[None], out_norm_conv, m_out_norm_conv, v_out_norm_conv),
        "out_norm_attn": (g_ga[None], out_norm_attn, m_out_norm_attn, v_out_norm_attn),
        "w_out": (p_wout, w_out[0], m_w_out[0], v_w_out[0]),
        "ffn2_norm": (g_ffn2n[None], ffn2_norm, m_ffn2_norm, v_ffn2_norm),
        "ffn2_w_gu": (p_wgu2, ffn2_w_gu[0].T, m_ffn2_w_gu[0].T, v_ffn2_w_gu[0].T),
        "ffn2_w_down": (p_wd2, ffn2_w_down[0], m_ffn2_w_down[0], v_ffn2_w_down[0]),
        "final_norm": (g_finaln[None], final_norm.reshape(1, D_MODEL), m_final_norm.reshape(1, D_MODEL),
                       v_final_norm.reshape(1, D_MODEL)),
    }
    shapes = {"meta_tokens": meta_tokens.shape, "ffn1_norm": ffn1_norm.shape, "ffn1_w_gu": ffn1_w_gu.shape,
              "ffn1_w_down": ffn1_w_down.shape, "mix_norm": mix_norm.shape, "w_in": w_in.shape,
              "conv_w": conv_w.shape, "b_f": b_f.shape, "out_norm_conv": out_norm_conv.shape,
              "out_norm_attn": out_norm_attn.shape, "w_out": w_out.shape, "ffn2_norm": ffn2_norm.shape,
              "ffn2_w_gu": ffn2_w_gu.shape, "ffn2_w_down": ffn2_w_down.shape, "final_norm": final_norm.shape}
    grads, deltas, new_m, new_v = [], [], [], []
    for name, (p, w, m, vv) in weights.items():
        g, d, nm, nv = _adamw(p, w, m, vv, "adamw_" + name)
        if name in ("ffn1_w_gu", "ffn2_w_gu", "w_in"):
            g, d, nm, nv = g.T, d.T, nm.T, nv.T
        shape = shapes[name]
        grads.append(g.reshape(shape))
        deltas.append(d.reshape(shape))
        new_m.append(nm.reshape(shape))
        new_v.append(nv.reshape(shape))

    loss = small_sum[LOSS_ROW, 0]
    return (loss, grad_x, *grads, *deltas, *new_m, *new_v)
```

```python
import jax
import jax.numpy as jnp
from jax import lax
from jax.experimental import pallas as pl
from jax.experimental.pallas import tpu as pltpu

F32 = jnp.float32
BF16 = jnp.bfloat16

N_DEV = 8
D_MODEL = 1024
N_META = 16
PAD = 128 - N_META
CONV_DIM = 512
ATTN_DIM = 512
HEAD_DIM = 64
N_HEADS = 8
N_PAIRS = N_HEADS // 2
D_FF = 2816
IN_DIM = 3080
IN_PAD = 3200
IN_MAIN = 3072
N_PIECE = IN_MAIN // 512
EPS = 1e-6
NEG = -1e30
TQ = 128
TK = 512
TK_FWD = 1024
VMEM_LIMIT = 56 * 1024 * 1024

HID_PIECES = ((0, 1024), (1024, 2048), (2048, D_FF))
ACT_PIECES = tuple((a, min(a + 256, D_FF)) for a in range(0, D_FF, 256))
W_GU_SHAPE = (2, D_FF, D_MODEL)
W_D_SHAPE = (D_FF, D_MODEL)

ADAM_LR = 0.001
ADAM_B1 = 0.9
ADAM_B2 = 0.999
ADAM_EPS = 1e-08
ADAM_WD = 0.01
ADAM_STEP = 10

MESH = pl.DeviceIdType.MESH
ANY = pl.BlockSpec(memory_space=pl.ANY)


def _params(sem=None):
    return pltpu.CompilerParams(dimension_semantics=sem, vmem_limit_bytes=VMEM_LIMIT)


def _row_tile(n, prefer):
    for t in (prefer, 512, 256, 128):
        if t <= n and n % t == 0:
            return t
    raise ValueError(f"no row tile for {n}")


def _resident(shape):
    zeros = (0,) * len(shape)
    return pl.BlockSpec(shape, lambda i: zeros, pipeline_mode=pl.Buffered(1))


def _dot(a, b):
    return jnp.dot(a, b, preferred_element_type=F32)


def _dot_nt(a, b):
    return lax.dot_general(a, b, (((1,), (1,)), ((), ())), preferred_element_type=F32)


def _dot_tn(a, b):
    return lax.dot_general(a, b, (((0,), (0,)), ((), ())), preferred_element_type=F32)


def _rms(x, g):
    r = lax.rsqrt(jnp.mean(x * x, axis=-1, keepdims=True) + EPS)
    xhat = x * r
    return xhat * g, xhat, r


def _rms_bwd(dn, xhat, r, g):
    dxhat = dn * g
    return r * (dxhat - xhat * jnp.mean(dxhat * xhat, axis=-1, keepdims=True))


def _sigmoid(x):
    return 1.0 / (1.0 + jnp.exp(-x))


def _place():
    return lax.axis_index("x"), lax.axis_index("y"), lax.axis_index("c")


def _comm_sems(nw):
    return [pltpu.SemaphoreType.DMA((nw, 7)), pltpu.SemaphoreType.DMA((nw, 7)), pltpu.SemaphoreType.DMA((nw,))]


def _flip(v, bit):
    return 1 - v if bit else v


class _Gather:
    def __init__(self, ins, outs, sems):
        self.ins, self.outs = ins, outs
        self.send, self.recv, self.local = sems
        x, y, c = _place()
        self.c = c
        self.me, self.sibling = (x, y, c), (x, y, 1 - c)
        first = ((x + 1 - c) % 2, (y + c) % 2)
        second = ((x + c) % 2, (y + 1 - c) % 2)
        self.chips = [first, second, (1 - x, 1 - y)]
        self.targets = [first, second, second]

    def _copy(self, w, k, block, to, own=False):
        slot = self.outs[w].at[4 * block[0] + 2 * block[1] + block[2]]
        return pltpu.make_async_remote_copy(
            src_ref=self.ins[w] if own else slot, dst_ref=slot,
            send_sem=self.send.at[w, k], recv_sem=self.recv.at[w, k], device_id=to, device_id_type=MESH)

    def _mine(self, w):
        x, y, c = self.me
        return pltpu.make_async_copy(self.ins[w], self.outs[w].at[4 * x + 2 * y + c], self.local.at[w])

    def _first(self, w):
        return ([self._copy(w, 0, self.me, self.sibling, own=True)]
                + [self._copy(w, 1 + j, self.me, (*self.targets[j], self.c), own=True) for j in range(2)])

    def _relay(self, w):
        return self._copy(w, 3, (*self.chips[0], self.c), (*self.targets[2], self.c))

    def _landed(self, w, j):
        return self._copy(w, 1 + j, (*self.chips[j], self.c), self.me)

    def _passed(self, w):
        return [self._copy(w, 4 + j, (*chip, self.c), self.sibling) for j, chip in enumerate(self.chips)]

    def start(self):
        for w in range(len(self.ins)):
            self._mine(w).start()
        for w in range(len(self.ins)):
            for cp in self._first(w):
                cp.start()

    def relay(self):
        for w in range(len(self.ins)):
            self._landed(w, 0).wait_recv()
            self._relay(w).start()
            self._passed(w)[0].start()

    def forward(self):
        for w in range(len(self.ins)):
            for j in (1, 2):
                self._landed(w, j).wait_recv()
                self._passed(w)[j].start()

    def finish(self):
        from_sibling = [self.chips[1], self.chips[0], self.chips[2]]
        for w in range(len(self.ins)):
            self._copy(w, 0, self.sibling, self.me).wait_recv()
            for j, chip in enumerate(from_sibling):
                self._copy(w, 4 + j, (*chip, 1 - self.c), self.me).wait_recv()
        for w in range(len(self.ins)):
            for cp in self._first(w) + [self._relay(w)] + self._passed(w):
                cp.wait_send()
            self._mine(w).wait()


class _Exchange:
    def __init__(self, ins, outs, sems):
        self.ins, self.outs = ins, outs
        self.send, self.recv, self.local = sems
        self.x, self.y, self.c = _place()
        self.me = 4 * self.x + 2 * self.y + self.c

    def _copy(self, w, k):
        peer = (_flip(self.x, ((k + 1) >> 2) & 1), _flip(self.y, ((k + 1) >> 1) & 1), _flip(self.c, (k + 1) & 1))
        return pltpu.make_async_remote_copy(
            src_ref=self.ins[w].at[4 * peer[0] + 2 * peer[1] + peer[2]], dst_ref=self.outs[w].at[self.me],
            send_sem=self.send.at[w, k], recv_sem=self.recv.at[w, k], device_id=peer, device_id_type=MESH)

    def _mine(self, w):
        return pltpu.make_async_copy(self.ins[w].at[self.me], self.outs[w].at[self.me], self.local.at[w])

    def start(self):
        for w in range(len(self.ins)):
            self._mine(w).start()
            for k in range(N_DEV - 1):
                self._copy(w, k).start()

    def finish(self):
        for w in range(len(self.ins)):
            for k in range(N_DEV - 1):
                self._copy(w, k).wait()
            self._mine(w).wait()


class _PairExchange:
    def __init__(self, ins, outs, sems):
        self.ins, self.outs = ins, outs
        self.send, self.recv, _ = sems
        x, y, self.c = _place()
        self.sibling = (x, y, 1 - self.c)

    def _copy(self, w, t):
        return pltpu.make_async_remote_copy(
            src_ref=self.ins[w].at[2 * t + 1 - self.c], dst_ref=self.outs[w].at[t],
            send_sem=self.send.at[w, t], recv_sem=self.recv.at[w, t], device_id=self.sibling, device_id_type=MESH)

    def start(self):
        for w in range(len(self.ins)):
            for t in range(4):
                self._copy(w, t).start()

    def finish(self):
        for w in range(len(self.ins)):
            for t in range(4):
                self._copy(w, t).wait()


class _ChipExchange:
    def __init__(self, ins, outs, sems):
        self.ins, self.outs = ins, outs
        self.send, self.recv, self.local = sems
        self.x, self.y, self.c = _place()
        self.chip = 2 * self.x + self.y

    def _copy(self, w, k):
        px, py = _flip(self.x, ((k + 1) >> 1) & 1), _flip(self.y, (k + 1) & 1)
        return pltpu.make_async_remote_copy(
            src_ref=self.ins[w].at[2 * px + py], dst_ref=self.outs[w].at[self.chip],
            send_sem=self.send.at[w, k], recv_sem=self.recv.at[w, k], device_id=(px, py, self.c),
            device_id_type=MESH)

    def _mine(self, w):
        return pltpu.make_async_copy(self.ins[w].at[self.chip], self.outs[w].at[self.chip], self.local.at[w])

    def start(self):
        for w in range(len(self.ins)):
            self._mine(w).start()
            for k in range(3):
                self._copy(w, k).start()

    def finish(self):
        for w in range(len(self.ins)):
            for k in range(3):
                self._copy(w, k).wait()
            self._mine(w).wait()


def _split_refs(refs, n_in, n_comm, n_out, n_scr):
    a = n_in
    b = a + n_comm
    c = b + n_out
    d = c + n_comm
    e = d + n_scr
    return refs[:a], refs[a:b], refs[b:c], refs[c:d], refs[d:e], refs[e:]


def _all_gather(xs, name):
    nw = len(xs)

    def body(*refs):
        comm = _Gather(refs[:nw], refs[nw:2 * nw], refs[2 * nw:])
        comm.start()
        comm.relay()
        comm.forward()
        comm.finish()

    return pl.pallas_call(
        body, name=name, in_specs=[ANY] * nw, out_specs=[ANY] * nw,
        out_shape=[jax.ShapeDtypeStruct((N_DEV,) + a.shape, a.dtype) for a in xs],
        scratch_shapes=_comm_sems(nw),
    )(*xs)


def _gather_direct(x, name):
    def body(in_ref, out_ref, *sems):
        comm = _Exchange([in_ref], [out_ref], sems)
        comm.start()
        comm.finish()

    return pl.pallas_call(
        body, name=name, in_specs=[ANY], out_specs=ANY,
        out_shape=jax.ShapeDtypeStruct((N_DEV,) + x.shape, x.dtype),
        scratch_shapes=_comm_sems(1),
    )(jnp.broadcast_to(x[None], (N_DEV,) + x.shape))


def _pair_exchange(xs, name):
    nw = len(xs)

    def body(*refs):
        comm = _PairExchange(refs[:nw], refs[nw:2 * nw], refs[2 * nw:])
        comm.start()
        comm.finish()

    return pl.pallas_call(
        body, name=name, in_specs=[ANY] * nw, out_specs=[ANY] * nw,
        out_shape=[jax.ShapeDtypeStruct((4,) + a.shape[1:], a.dtype) for a in xs],
        scratch_shapes=_comm_sems(nw),
    )(*xs)


def _pair_sum(own, got, name):
    _, r, c = own.shape
    tr = r
    for t in (256, 128, 64, 32, 16):
        if r % t == 0 and r > t:
            tr = t
            break

    def body(own_ref, got_ref, out_ref):
        mine = jnp.where(lax.axis_index("c") == 0, own_ref[:, 0].astype(F32), own_ref[:, 1].astype(F32))
        out_ref[...] = (mine + got_ref[...].astype(F32)).astype(BF16)

    return pl.pallas_call(
        body, name=name, grid=(r // tr,),
        in_specs=[pl.BlockSpec((4, 2, tr, c), lambda i: (0, 0, i, 0)), pl.BlockSpec((4, tr, c), lambda i: (0, i, 0))],
        out_specs=pl.BlockSpec((4, tr, c), lambda i: (0, i, 0)),
        out_shape=jax.ShapeDtypeStruct((4, r, c), BF16),
        compiler_params=_params(("parallel",)),
    )(own.reshape(4, 2, r, c), got)


def _token_spec(k, ksub, nq):
    def index_map(i):
        s = ksub * i + k
        return ((s // nq) * (nq - 1) + jnp.maximum(s % nq, 1) - 1, 0)
    return pl.BlockSpec((128, D_MODEL), index_map)


def _is_lead(i, k, ksub, nq):
    return ((ksub * i + k) % nq) == 0


def _assemble_rows(i, x_refs, meta_ref, nq):
    ksub = len(x_refs)
    lead = jnp.concatenate([jnp.zeros((PAD, D_MODEL), F32), meta_ref[...]], axis=0)
    return jnp.concatenate([jnp.where(_is_lead(i, k, ksub, nq), lead, x_refs[k][...]) for k in range(ksub)], axis=0)


def _swiglu(nb, wgu_ref, wd_ref, gate_ref, up_ref):
    acc = jnp.zeros((nb.shape[0], D_MODEL), F32)
    for a, b in HID_PIECES:
        gate = _dot_nt(nb, wgu_ref[0, a:b, :])
        up = _dot_nt(nb, wgu_ref[1, a:b, :])
        gate_ref[:, a:b] = gate.astype(BF16)
        up_ref[:, a:b] = up.astype(BF16)
        acc = acc + _dot((gate * _sigmoid(gate) * up).astype(BF16), wd_ref[a:b, :])
    return acc


def _ffn1_fwd(x2d, meta, lp, gain, wgu, wd, gather):
    nq = lp // 128
    n = (x2d.shape[0] // (nq - 1)) * nq
    tm = _row_tile(n, 512)
    ksub = tm // 128
    n_i = n // tm
    nw = len(gather)

    def body(*refs):
        x_refs = refs[:ksub]
        (meta_ref, g_ref, wgu_ref, wd_ref), gin, (out_ref, nrm_ref, gate_ref, up_ref), gout, _, sems = \
            _split_refs(refs[ksub:], 4, nw, 4, 0)
        i = pl.program_id(0)
        comm = _Gather(gin, gout, sems)
        pl.when(i == 0)(comm.start)
        pl.when(i == n_i // 2)(comm.relay)
        pl.when(i == max(n_i - 3, n_i // 2))(comm.forward)

        hv = _assemble_rows(i, x_refs, meta_ref, nq)
        y, _, _ = _rms(hv, g_ref[...])
        nb = y.astype(BF16)
        nrm_ref[...] = nb
        out_ref[...] = hv + 0.5 * _swiglu(nb, wgu_ref, wd_ref, gate_ref, up_ref)

        pl.when(i == n_i - 1)(comm.finish)

    rows = pl.BlockSpec((tm, D_MODEL), lambda i: (i, 0))
    hid = pl.BlockSpec((tm, D_FF), lambda i: (i, 0))
    return pl.pallas_call(
        body, name="ffn1_fwd", grid=(n_i,),
        in_specs=[_token_spec(k, ksub, nq) for k in range(ksub)]
        + [pl.BlockSpec((N_META, D_MODEL), lambda i: (0, 0)), pl.BlockSpec((1, D_MODEL), lambda i: (0, 0)),
           _resident(W_GU_SHAPE), _resident(W_D_SHAPE)] + [ANY] * nw,
        out_specs=[rows, rows, hid, hid] + [ANY] * nw,
        out_shape=[jax.ShapeDtypeStruct((n, D_MODEL), F32), jax.ShapeDtypeStruct((n, D_MODEL), BF16),
                   jax.ShapeDtypeStruct((n, D_FF), BF16), jax.ShapeDtypeStruct((n, D_FF), BF16)]
        + [jax.ShapeDtypeStruct((N_DEV,) + a.shape, a.dtype) for a in gather],
        scratch_shapes=_comm_sems(nw),
        compiler_params=_params(("arbitrary",)),
    )(*([x2d] * ksub), meta, gain, wgu, wd, *gather)


def _ffn2_fwd_loss(h, gain, wgu, wd, gfinal, target, lp):
    n = h.shape[0]
    nq = lp // 128
    tm = _row_tile(n, 512)
    ksub = tm // 128
    n_i = n // tm

    def body(*refs):
        t_refs = refs[:ksub]
        h_ref, g_ref, wgu_ref, wd_ref, gf_ref, dh_ref, nrm_ref, gate_ref, up_ref, loss_ref, dgf_ref = refs[ksub:]
        i = pl.program_id(0)

        @pl.when(i == 0)
        def _():
            loss_ref[...] = jnp.zeros_like(loss_ref)
            dgf_ref[...] = jnp.zeros_like(dgf_ref)

        hv = h_ref[...]
        y, _, _ = _rms(hv, g_ref[...])
        nb = y.astype(BF16)
        nrm_ref[...] = nb
        hout = hv + 0.5 * _swiglu(nb, wgu_ref, wd_ref, gate_ref, up_ref)

        gf = gf_ref[...]
        loss = jnp.zeros((1, 1), F32)
        dgf = jnp.zeros((1, D_MODEL), F32)
        for k in range(ksub):
            yk, xhat, r = _rms(hout[128 * k:128 * (k + 1)], gf)
            err = jnp.where(_is_lead(i, k, ksub, nq), 0.0, yk - t_refs[k][...])
            loss = loss + 0.5 * jnp.sum(jnp.sum(err * err, axis=1, keepdims=True), axis=0,
                                        keepdims=True) * (1.0 / D_MODEL)
            dy = err * (1.0 / D_MODEL)
            dh_ref[128 * k:128 * (k + 1), :] = _rms_bwd(dy, xhat, r, gf)
            dgf = dgf + jnp.sum(dy * xhat, axis=0, keepdims=True)
        loss_ref[...] += loss
        dgf_ref[...] += dgf

    rows = pl.BlockSpec((tm, D_MODEL), lambda i: (i, 0))
    hid = pl.BlockSpec((tm, D_FF), lambda i: (i, 0))
    vec = pl.BlockSpec((1, D_MODEL), lambda i: (0, 0))
    return pl.pallas_call(
        body, name="ffn2_fwd_loss", grid=(n_i,),
        in_specs=[_token_spec(k, ksub, nq) for k in range(ksub)]
        + [rows, vec, _resident(W_GU_SHAPE), _resident(W_D_SHAPE), vec],
        out_specs=[rows, rows, hid, hid, pl.BlockSpec((1, 1), lambda i: (0, 0)), vec],
        out_shape=[jax.ShapeDtypeStruct((n, D_MODEL), F32), jax.ShapeDtypeStruct((n, D_MODEL), BF16),
                   jax.ShapeDtypeStruct((n, D_FF), BF16), jax.ShapeDtypeStruct((n, D_FF), BF16),
                   jax.ShapeDtypeStruct((1, 1), F32), jax.ShapeDtypeStruct((1, D_MODEL), F32)],
        compiler_params=_params(("arbitrary",)),
    )(*([target] * ksub), h, gain, wgu, wd, gfinal)


def _ffn_bwd_act_wd(dh_out, gate, up, wd, name, exchange=()):
    n = dh_out.shape[0]
    tm = _row_tile(n, 256)
    n_i = n // tm
    nw = len(exchange)

    def body(*refs):
        (dh_ref, gate_ref, up_ref, wd_ref), xin, (dgu_ref, dw_ref), xout, (acc_scr,), sems = \
            _split_refs(refs, 4, nw, 2, 1)
        i = pl.program_id(0)
        if nw:
            comm = _Exchange(xin, xout, sems)
            pl.when(i == 0)(comm.start)

        @pl.when(i == 0)
        def _():
            acc_scr[...] = jnp.zeros_like(acc_scr)

        dhb = (0.5 * dh_ref[...]).astype(BF16)
        for a, b in ACT_PIECES:
            da = _dot_nt(dhb, wd_ref[a:b, :])
            g = gate_ref[:, a:b].astype(F32)
            u = up_ref[:, a:b].astype(F32)
            sig = _sigmoid(g)
            silu = g * sig
            dgu_ref[:, a:b] = (da * u * (sig + silu * (1.0 - sig))).astype(BF16)
            dgu_ref[:, D_FF + a:D_FF + b] = (da * silu).astype(BF16)
            acc_scr[a:b, :] += _dot_tn((silu * u).astype(BF16), dhb)

        @pl.when(i == n_i - 1)
        def _():
            dw_ref[...] = acc_scr[...].astype(BF16)

        if nw:
            pl.when(i == n_i - 1)(comm.finish)

    rows = pl.BlockSpec((tm, D_MODEL), lambda i: (i, 0))
    hid = pl.BlockSpec((tm, D_FF), lambda i: (i, 0))
    return pl.pallas_call(
        body, name=name, grid=(n_i,),
        in_specs=[rows, hid, hid, _resident(W_D_SHAPE)] + [ANY] * nw,
        out_specs=[pl.BlockSpec((tm, 2 * D_FF), lambda i: (i, 0)), _resident(W_D_SHAPE)] + [ANY] * nw,
        out_shape=[jax.ShapeDtypeStruct((n, 2 * D_FF), BF16), jax.ShapeDtypeStruct(W_D_SHAPE, BF16)]
        + [jax.ShapeDtypeStruct(a.shape, a.dtype) for a in exchange],
        scratch_shapes=[pltpu.VMEM(W_D_SHAPE, F32)] + (_comm_sems(nw) if nw else []),
        compiler_params=_params(("arbitrary",)),
    )(dh_out, gate, up, wd, *exchange)


def _ffn_bwd_in(dh_out, h_in, gain, dgu, wgu, name, tokens=None, exchange=()):
    n = dh_out.shape[0]
    tm = _row_tile(n, 512)
    n_i = n // tm
    nw = len(exchange)
    ksub, nq = (tm // 128, tokens[2] // 128) if tokens else (1, 0)

    def body(*refs):
        h_refs = refs[:ksub]
        (meta_ref, dh_ref, g_ref, dgu_ref, wgu_ref), xin, (dhin_ref, dgain_ref), xout, _, sems = \
            _split_refs(refs[ksub:], 5, nw, 2, 0)
        i = pl.program_id(0)
        if nw:
            comm = _ChipExchange(xin, xout, sems)
            pl.when(i == 0)(comm.start)

        @pl.when(i == 0)
        def _():
            dgain_ref[...] = jnp.zeros_like(dgain_ref)

        dn = _dot(dgu_ref[...], wgu_ref[...])
        gain_v = g_ref[...]
        hv = _assemble_rows(i, h_refs, meta_ref, nq) if tokens else h_refs[0][...]
        _, xhat, r = _rms(hv, gain_v)
        dhin_ref[...] = dh_ref[...] + _rms_bwd(dn, xhat, r, gain_v)
        dgain_ref[...] += jnp.sum(dn * xhat, axis=0, keepdims=True)

        if nw:
            pl.when(i == n_i - 1)(comm.finish)

    rows = pl.BlockSpec((tm, D_MODEL), lambda i: (i, 0))
    hid = pl.BlockSpec((tm, D_FF), lambda i: (i, 0))
    vec = pl.BlockSpec((1, D_MODEL), lambda i: (0, 0))
    meta_spec = pl.BlockSpec((N_META, D_MODEL), lambda i: (0, 0))
    if tokens:
        h_specs, h_args, meta = [_token_spec(k, ksub, nq) for k in range(ksub)], [tokens[0]] * ksub, tokens[1]
    else:
        h_specs, h_args, meta = [rows], [h_in], jnp.zeros((N_META, D_MODEL), F32)
    return pl.pallas_call(
        body, name=name, grid=(n_i,),
        in_specs=h_specs + [meta_spec, rows, vec, pl.BlockSpec((tm, 2 * D_FF), lambda i: (i, 0)),
                            _resident((2 * D_FF, D_MODEL))] + [ANY] * nw,
        out_specs=[rows, vec] + [ANY] * nw,
        out_shape=[jax.ShapeDtypeStruct((n, D_MODEL), F32), jax.ShapeDtypeStruct((1, D_MODEL), F32)]
        + [jax.ShapeDtypeStruct(a.shape, a.dtype) for a in exchange],
        scratch_shapes=_comm_sems(nw) if nw else [],
        compiler_params=_params(("arbitrary",)),
    )(*h_args, meta, dh_out, gain, dgu, wgu.reshape(2 * D_FF, D_MODEL), *exchange)


def _ffn_bwd_wgu(nrm, dgu, name, exchange=()):
    n = nrm.shape[0]
    tm = _row_tile(n, 512)
    n_i = n // tm
    nw = len(exchange)

    def body(*refs):
        (nrm_ref, dgu_ref), xin, (dw_ref,), xout, (acc_scr,), sems = _split_refs(refs, 2, nw, 1, 1)
        i = pl.program_id(0)
        if nw:
            comm = _Exchange(xin, xout, sems)
            pl.when(i == 0)(comm.start)

        @pl.when(i == 0)
        def _():
            acc_scr[...] = jnp.zeros_like(acc_scr)

        nb = nrm_ref[...]
        for half in (0, D_FF):
            for a, b in HID_PIECES:
                acc_scr[half + a:half + b, :] += _dot_tn(dgu_ref[:, half + a:half + b], nb)

        @pl.when(i == n_i - 1)
        def _():
            dw_ref[...] = acc_scr[...].astype(BF16)

        if nw:
            pl.when(i == n_i - 1)(comm.finish)

    shape = (2 * D_FF, D_MODEL)
    res = pl.pallas_call(
        body, name=name, grid=(n_i,),
        in_specs=[pl.BlockSpec((tm, D_MODEL), lambda i: (i, 0)),
                  pl.BlockSpec((tm, 2 * D_FF), lambda i: (i, 0))] + [ANY] * nw,
        out_specs=[_resident(shape)] + [ANY] * nw,
        out_shape=[jax.ShapeDtypeStruct(shape, BF16)] + [jax.ShapeDtypeStruct(a.shape, a.dtype) for a in exchange],
        scratch_shapes=[pltpu.VMEM(shape, F32)] + (_comm_sems(nw) if nw else []),
        compiler_params=_params(("arbitrary",)),
    )(nrm, dgu, *exchange)
    return res if nw else res[0]


def _inproj_fwd(h, gain, w_in):
    n = h.shape[0]
    tm = _row_tile(n, 512)

    def body(h_ref, g_ref, w_ref, *outs):
        y, _, _ = _rms(h_ref[...], g_ref[...])
        nb = y.astype(BF16)
        for p in range(N_PIECE):
            outs[p][...] = _dot_nt(nb, w_ref[512 * p:512 * (p + 1), :]).astype(BF16)
        outs[N_PIECE][...] = _dot_nt(nb, w_ref[IN_MAIN:IN_PAD, :])

    piece = pl.BlockSpec((tm, 512), lambda i: (i, 0))
    return pl.pallas_call(
        body, name="inproj_fwd", grid=(n // tm,),
        in_specs=[pl.BlockSpec((tm, D_MODEL), lambda i: (i, 0)),
                  pl.BlockSpec((1, D_MODEL), lambda i: (0, 0)),
                  pl.BlockSpec((IN_PAD, D_MODEL), lambda i: (0, 0))],
        out_specs=[piece] * N_PIECE + [pl.BlockSpec((tm, 128), lambda i: (i, 0))],
        out_shape=[jax.ShapeDtypeStruct((n, 512), BF16)] * N_PIECE + [jax.ShapeDtypeStruct((n, 128), F32)],
        compiler_params=_params(("parallel",)),
    )(h, gain, w_in)


def _inproj_bwd(dpieces, dfg, dh_out, h_in, gain, w_in):
    n = h_in.shape[0]
    tm = _row_tile(n, 512)
    n_i = n // tm

    def body(*refs):
        dp_refs = refs[:N_PIECE]
        dfg_ref, dh_ref, h_ref, g_ref, w_ref, dhin_ref, dw_ref, dgain_ref, acc_scr = refs[N_PIECE:]
        i = pl.program_id(0)

        @pl.when(i == 0)
        def _():
            acc_scr[...] = jnp.zeros_like(acc_scr)
            dgain_ref[...] = jnp.zeros_like(dgain_ref)

        gain_v = g_ref[...]
        y, xhat, r = _rms(h_ref[...], gain_v)
        nb = y.astype(BF16)
        dn = jnp.zeros((tm, D_MODEL), F32)
        for p in range(N_PIECE + 1):
            lo, hi = (512 * p, 512 * (p + 1)) if p < N_PIECE else (IN_MAIN, IN_PAD)
            dp = (dp_refs[p][...] if p < N_PIECE else dfg_ref[...]).astype(BF16)
            dn = dn + _dot(dp, w_ref[lo:hi, :])
            acc_scr[lo:hi, :] += _dot_tn(dp, nb)
        dhin_ref[...] = dh_ref[...] + _rms_bwd(dn, xhat, r, gain_v)
        dgain_ref[...] += jnp.sum(dn * xhat, axis=0, keepdims=True)

        @pl.when(i == n_i - 1)
        def _():
            dw_ref[...] = acc_scr[...].astype(BF16)

    piece = pl.BlockSpec((tm, 512), lambda i: (i, 0))
    rows = pl.BlockSpec((tm, D_MODEL), lambda i: (i, 0))
    vec = pl.BlockSpec((1, D_MODEL), lambda i: (0, 0))
    wspec = pl.BlockSpec((IN_PAD, D_MODEL), lambda i: (0, 0))
    return pl.pallas_call(
        body, name="inproj_bwd", grid=(n_i,),
        in_specs=[piece] * N_PIECE + [pl.BlockSpec((tm, 128), lambda i: (i, 0)), rows, rows, vec, wspec],
        out_specs=[rows, wspec, vec],
        out_shape=[jax.ShapeDtypeStruct((n, D_MODEL), F32),
                   jax.ShapeDtypeStruct((IN_PAD, D_MODEL), BF16),
                   jax.ShapeDtypeStruct((1, D_MODEL), F32)],
        scratch_shapes=[pltpu.VMEM((IN_PAD, D_MODEL), F32)],
        compiler_params=_params(("arbitrary",)),
    )(*dpieces, dfg, dh_out, h_in, gain, w_in)


def _outproj_bwd(dh, zc, za, w_out):
    n = dh.shape[0]
    tm = _row_tile(n, 512)
    n_i = n // tm

    def body(dh_ref, zc_ref, za_ref, w_ref, dzc_ref, dza_ref, dw_ref, acc_scr):
        i = pl.program_id(0)

        @pl.when(i == 0)
        def _():
            acc_scr[...] = jnp.zeros_like(acc_scr)

        dhb = dh_ref[...].astype(BF16)
        dzc_ref[...] = _dot_nt(dhb, w_ref[0:CONV_DIM, :]).astype(BF16)
        dza_ref[...] = _dot_nt(dhb, w_ref[CONV_DIM:, :]).astype(BF16)
        acc_scr[0:CONV_DIM, :] += _dot_tn(zc_ref[...], dhb)
        acc_scr[CONV_DIM:, :] += _dot_tn(za_ref[...], dhb)

        @pl.when(i == n_i - 1)
        def _():
            dw_ref[...] = acc_scr[...].astype(BF16)

    half = pl.BlockSpec((tm, 512), lambda i: (i, 0))
    wspec = pl.BlockSpec((D_MODEL, D_MODEL), lambda i: (0, 0))
    return pl.pallas_call(
        body, name="outproj_bwd", grid=(n_i,),
        in_specs=[pl.BlockSpec((tm, D_MODEL), lambda i: (i, 0)), half, half, wspec],
        out_specs=[half, half, wspec],
        out_shape=[jax.ShapeDtypeStruct((n, 512), BF16), jax.ShapeDtypeStruct((n, 512), BF16),
                   jax.ShapeDtypeStruct((D_MODEL, D_MODEL), BF16)],
        scratch_shapes=[pltpu.VMEM((D_MODEL, D_MODEL), F32)],
        compiler_params=_params(("arbitrary",)),
    )(dh, zc, za, w_out)


def _group_matrix():
    r = lax.broadcasted_iota(jnp.int32, (128, 128), 0) // HEAD_DIM
    c = lax.broadcasted_iota(jnp.int32, (128, 128), 1) // HEAD_DIM
    return jnp.where(r == c, 1.0 / HEAD_DIM, 0.0).astype(BF16)


def _group_mean(x, gmat):
    hi = x.astype(BF16)
    lo = (x - hi.astype(F32)).astype(BF16)
    return _dot(hi, gmat) + _dot(lo, gmat)


def _shift_rows(x, s):
    rows = x.shape[0]
    t = lax.broadcasted_iota(jnp.int32, x.shape, 0)
    rolled = pltpu.roll(x, s % rows, 0)
    keep = (t >= s) if s > 0 else (t < rows + s)
    return jnp.where(keep, rolled, 0.0)


def _conv_parts(bg_ref, cg_ref, hc_ref, w_ref):
    bg = bg_ref[...].astype(F32)
    cg = cg_ref[...].astype(F32)
    hc = hc_ref[...].astype(F32)
    u = cg * hc
    u1 = _shift_rows(u, 1)
    u2 = _shift_rows(u, 2)
    conv = w_ref[2:3, :] * u + w_ref[1:2, :] * u1 + w_ref[0:1, :] * u2
    return bg, cg, hc, u, u1, u2, conv


def _conv_fwd(bg, cg, hc, conv_w, gain, gmat, lp):
    n = bg.shape[0]
    nb = n // lp

    def body(bg_ref, cg_ref, hc_ref, w_ref, g_ref, gm_ref, z_ref):
        bgv, _, _, _, _, _, conv = _conv_parts(bg_ref, cg_ref, hc_ref, w_ref)
        yc = bgv * conv
        r = lax.rsqrt(_group_mean(yc * yc, gm_ref[...]) + EPS)
        z_ref[...] = (yc * r * g_ref[...]).astype(BF16)

    blk = pl.BlockSpec((lp, 128), lambda c, b: (b, c))
    return pl.pallas_call(
        body, name="conv_fwd", grid=(CONV_DIM // 128, nb),
        in_specs=[blk, blk, blk, pl.BlockSpec((3, 128), lambda c, b: (0, c)),
                  pl.BlockSpec((1, 128), lambda c, b: (0, c)), pl.BlockSpec((128, 128), lambda c, b: (0, 0))],
        out_specs=blk,
        out_shape=jax.ShapeDtypeStruct((n, CONV_DIM), BF16),
        compiler_params=_params(("parallel", "parallel")),
    )(bg, cg, hc, conv_w, gain, gmat)


def _conv_bwd(dz, bg, cg, hc, conv_w, gain, gmat, lp):
    n = bg.shape[0]
    nb = n // lp

    def body(dz_ref, bg_ref, cg_ref, hc_ref, w_ref, g_ref, gm_ref,
             dbg_ref, dcg_ref, dhc_ref, dw_ref, dgain_ref):
        b = pl.program_id(1)

        @pl.when(b == 0)
        def _():
            dw_ref[...] = jnp.zeros_like(dw_ref)
            dgain_ref[...] = jnp.zeros_like(dgain_ref)

        bgv, cgv, hcv, u, u1, u2, conv = _conv_parts(bg_ref, cg_ref, hc_ref, w_ref)
        gm = gm_ref[...]
        yc = bgv * conv
        r = lax.rsqrt(_group_mean(yc * yc, gm) + EPS)
        yhat = yc * r
        dzv = dz_ref[...].astype(F32)
        dyhat = dzv * g_ref[...]
        dgain_ref[...] += jnp.sum(dzv * yhat, axis=0, keepdims=True)
        dyc = r * (dyhat - yhat * _group_mean(dyhat * yhat, gm))
        dbg_ref[...] = (dyc * conv).astype(BF16)
        dconv = dyc * bgv
        du = (w_ref[2:3, :] * dconv + w_ref[1:2, :] * _shift_rows(dconv, -1)
              + w_ref[0:1, :] * _shift_rows(dconv, -2))
        dcg_ref[...] = (du * hcv).astype(BF16)
        dhc_ref[...] = (du * cgv).astype(BF16)
        dw_ref[0:1, :] += jnp.sum(dconv * u2, axis=0, keepdims=True)
        dw_ref[1:2, :] += jnp.sum(dconv * u1, axis=0, keepdims=True)
        dw_ref[2:3, :] += jnp.sum(dconv * u, axis=0, keepdims=True)

    blk = pl.BlockSpec((lp, 128), lambda c, b: (b, c))
    wspec = pl.BlockSpec((3, 128), lambda c, b: (0, c))
    gspec = pl.BlockSpec((1, 128), lambda c, b: (0, c))
    return pl.pallas_call(
        body, name="conv_bwd", grid=(CONV_DIM // 128, nb),
        in_specs=[blk, blk, blk, blk, wspec, gspec, pl.BlockSpec((128, 128), lambda c, b: (0, 0))],
        out_specs=[blk, blk, blk, wspec, gspec],
        out_shape=[jax.ShapeDtypeStruct((n, CONV_DIM), BF16)] * 3
        + [jax.ShapeDtypeStruct((3, CONV_DIM), F32), jax.ShapeDtypeStruct((1, CONV_DIM), F32)],
        compiler_params=_params(("parallel", "arbitrary")),
    )(dz, bg, cg, hc, conv_w, gain, gmat)


KEY_MASKED = 1e30
ONE_LANE = 24


def _scan_steps(rows):
    s, out = 1, []
    while s < rows:
        out.append(s)
        s *= 2
    return out


def _fgate_fwd(fg, b_f, lp):
    n = fg.shape[0]
    nb = n // lp

    def body(fg_ref, b_ref, ka_ref, qa_ref):
        x = fg_ref[...] + b_ref[...]
        logf = jnp.minimum(x, 0.0) - jnp.log(1.0 + jnp.exp(-jnp.abs(x)))
        t = lax.broadcasted_iota(jnp.int32, (lp, 128), 0)
        lane = lax.broadcasted_iota(jnp.int32, (lp, 128), 1)
        f = jnp.where((t >= PAD) & (lane < N_HEADS), logf, 0.0)
        for s in _scan_steps(lp):
            f = f + _shift_rows(f, s)
        hi = f.astype(BF16).astype(F32)
        rest = f - hi
        mid = rest.astype(BF16).astype(F32)
        lo = (rest - mid).astype(BF16).astype(F32)
        ones = jnp.where((lane >= ONE_LANE) & (lane < ONE_LANE + 3), 1.0, 0.0)
        hi_key = jnp.where((t < PAD) & (lane < N_HEADS), KEY_MASKED, hi)
        ka_ref[...] = (hi_key + pltpu.roll(mid, 8, 1) + pltpu.roll(lo, 16, 1) + ones).astype(BF16)
        for h in range(N_HEADS):
            minus = jnp.where((lane == h) | (lane == 8 + h) | (lane == 16 + h), -1.0, 0.0)
            terms = (jnp.where(lane == ONE_LANE, pltpu.roll(hi, ONE_LANE - h, 1), 0.0)
                     + jnp.where(lane == ONE_LANE + 1, pltpu.roll(mid, ONE_LANE + 1 - h, 1), 0.0)
                     + jnp.where(lane == ONE_LANE + 2, pltpu.roll(lo, ONE_LANE + 2 - h, 1), 0.0))
            qa_ref[:, 128 * h:128 * (h + 1)] = (minus + terms).astype(BF16)

    return pl.pallas_call(
        body, name="fgate_fwd", grid=(nb,),
        in_specs=[pl.BlockSpec((lp, 128), lambda b: (b, 0)), pl.BlockSpec((1, 128), lambda b: (0, 0))],
        out_specs=[pl.BlockSpec((lp, 128), lambda b: (b, 0)), pl.BlockSpec((lp, N_HEADS * 128), lambda b: (b, 0))],
        out_shape=[jax.ShapeDtypeStruct((n, 128), BF16), jax.ShapeDtypeStruct((n, N_HEADS * 128), BF16)],
        compiler_params=_params(("parallel",)),
    )(fg, b_f)


def _fgate_bwd(dka, dfr, fg, b_f, lp):
    n = fg.shape[0]
    nb = n // lp

    def body(dka_ref, dfr_ref, fg_ref, b_ref, dfg_ref, db_ref):
        b = pl.program_id(0)

        @pl.when(b == 0)
        def _():
            db_ref[...] = jnp.zeros_like(db_ref)

        wide = jnp.concatenate([dfr_ref[0], jnp.zeros((128 - N_HEADS, lp), F32)], axis=0)
        t = lax.broadcasted_iota(jnp.int32, (lp, 128), 0)
        lane = lax.broadcasted_iota(jnp.int32, (lp, 128), 1)
        d = jnp.where(lane < N_HEADS, dka_ref[...], 0.0) + wide.T
        for s in _scan_steps(lp):
            d = d + _shift_rows(d, -s)
        x = fg_ref[...] + b_ref[...]
        dx = jnp.where((t >= PAD) & (lane < N_HEADS), d * _sigmoid(-x), 0.0)
        dfg_ref[...] = dx
        db_ref[...] += jnp.sum(dx, axis=0, keepdims=True)

    return pl.pallas_call(
        body, name="fgate_bwd", grid=(nb,),
        in_specs=[pl.BlockSpec((lp, 128), lambda b: (b, 0)), pl.BlockSpec((1, N_HEADS, lp), lambda b: (b, 0, 0)),
                  pl.BlockSpec((lp, 128), lambda b: (b, 0)), pl.BlockSpec((1, 128), lambda b: (0, 0))],
        out_specs=[pl.BlockSpec((lp, 128), lambda b: (b, 0)), pl.BlockSpec((1, 128), lambda b: (0, 0))],
        out_shape=[jax.ShapeDtypeStruct((n, 128), F32), jax.ShapeDtypeStruct((1, 128), F32)],
        compiler_params=_params(("arbitrary",)),
    )(dka, dfr, fg, b_f)


def _head_masks():
    lane = lax.broadcasted_iota(jnp.int32, (1, 128), 1)
    return lane < HEAD_DIM


def _stack_heads(x2, first):
    zero = jnp.zeros_like(x2)
    return jnp.concatenate([jnp.where(first, x2, zero), jnp.where(first, zero, x2)], axis=0)


def _stack_heads_lanes(xt):
    r = lax.broadcasted_iota(jnp.int32, xt.shape, 0)
    zero = jnp.zeros_like(xt)
    return jnp.concatenate([jnp.where(r < HEAD_DIM, xt, zero), jnp.where(r < HEAD_DIM, zero, xt)], axis=1)


def _pair_cols(col0, col1, first):
    return jnp.where(first, col0, col1)


def _pair_rows(row0, row1):
    r = lax.broadcasted_iota(jnp.int32, (128, TQ), 0)
    return jnp.where(r < HEAD_DIM, row0, row1)


def _query_side(q_ref, qa_ref, p, first):
    q2 = q_ref[:, 128 * p:128 * (p + 1)] * 0.125
    zero = jnp.zeros_like(q2)
    top = jnp.concatenate([jnp.where(first, q2, zero), qa_ref[:, 128 * (2 * p):128 * (2 * p + 1)]], axis=1)
    bot = jnp.concatenate([jnp.where(first, zero, q2), qa_ref[:, 128 * (2 * p + 1):128 * (2 * p + 2)]], axis=1)
    return jnp.concatenate([top, bot], axis=0)


def _padded_keys(lp, chunk):
    return ((lp + chunk - 1) // chunk) * chunk


def _chunk_mask(i, c, tk, chunk):
    r = lax.broadcasted_iota(jnp.int32, (tk, 2 * TQ), 0)
    col = lax.broadcasted_iota(jnp.int32, (tk, 2 * TQ), 1)
    return (c * chunk + r) <= (i * TQ + (col & (TQ - 1)))


def _causal_sweep(i, step, init, chunk):
    per = chunk // TQ
    last = i // per
    carry = lax.fori_loop(0, last, lambda c, carry: step(c, carry, False, chunk), init)
    tails = [lambda carry, r=r: step(last, carry, True, TQ * (r + 1)) for r in range(per)]
    return lax.switch(i % per, tails, carry)


def _transpose_bf16(x):
    return x.astype(F32).T.astype(BF16)


def _attn_fwd(q, qa, k, v, ka, gain, zc, w_out, h, lp):
    n = q.shape[0]
    nb = n // lp
    nq = lp // TQ
    lpp = _padded_keys(lp, TK_FWD)

    def body(q_ref, qa_ref, k_ref, v_ref, ka_ref, g_ref, zc_ref, w_ref, h_ref,
             z_ref, o_ref, lse_ref, hout_ref, kx_scr, vt_scr):
        i = pl.program_id(1)
        first = _head_masks()

        @pl.when(i == 0)
        def _():
            if lpp > lp:
                kx_scr[lp:lpp, :] = jnp.zeros((lpp - lp, 2 * ATTN_DIM), BF16)
                vt_scr[:, lp:lpp] = jnp.zeros((ATTN_DIM, lpp - lp), BF16)
            for p in range(N_PAIRS):
                kx_scr[0:lp, 256 * p:256 * p + 128] = k_ref[:, 128 * p:128 * (p + 1)]
                kx_scr[0:lp, 256 * p + 128:256 * (p + 1)] = ka_ref[...]
            vt_scr[:, 0:lp] = _transpose_bf16(v_ref[...])

        rhs_t = [_transpose_bf16(_query_side(q_ref, qa_ref, p, first)) for p in range(N_PAIRS)]

        def step(c, carry, masked, tk):
            koff = pl.multiple_of(c * TK_FWD, TK_FWD)
            valid = _chunk_mask(i, c, tk, TK_FWD) if masked else None
            new = []
            sts = [_dot(kx_scr[pl.ds(koff, tk), 256 * p:256 * (p + 1)], rhs_t[p]) for p in range(N_PAIRS)]
            for p in range(N_PAIRS):
                m, l, acc = carry[p]
                st = sts[p]
                if masked:
                    st = jnp.where(valid, st, NEG)
                m_new = jnp.maximum(m, jnp.max(st, axis=0, keepdims=True))
                pt = jnp.exp(st - m_new)
                alpha = jnp.exp(m - m_new)
                l = alpha * l + jnp.sum(pt, axis=0, keepdims=True)
                pb = pt.astype(BF16)
                vt = _stack_heads_lanes(vt_scr[128 * p:128 * (p + 1), pl.ds(koff, tk)])
                pv = _dot(vt, jnp.concatenate([pb[:, 0:TQ], pb[:, TQ:]], axis=0))
                acc = acc * _pair_rows(alpha[:, 0:TQ], alpha[:, TQ:]) + pv
                new.append((m_new, l, acc))
            return tuple(new)

        init = tuple((jnp.full((1, 2 * TQ), NEG, F32), jnp.zeros((1, 2 * TQ), F32), jnp.zeros((128, TQ), F32))
                     for _ in range(N_PAIRS))
        final = _causal_sweep(i, step, init, TK_FWD)

        row = lax.broadcasted_iota(jnp.int32, (TQ, 128), 0)
        real = (i * TQ + row) >= PAD
        zs = [zc_ref[...]]
        for p in range(N_PAIRS):
            m, l, acc = final[p]
            inv = 1.0 / l
            ot = acc * _pair_rows(inv[:, 0:TQ], inv[:, TQ:])
            sq = ot * ot
            r0 = lax.rsqrt(jnp.sum(sq[0:HEAD_DIM], axis=0, keepdims=True) * (1.0 / HEAD_DIM) + EPS)
            r1 = lax.rsqrt(jnp.sum(sq[HEAD_DIM:], axis=0, keepdims=True) * (1.0 / HEAD_DIM) + EPS)
            cols = slice(128 * p, 128 * (p + 1))
            o_ref[:, cols] = jnp.where(real, ot.T, 0.0).astype(BF16)
            z = (jnp.where(real, (ot * _pair_rows(r0, r1)).T, 0.0) * g_ref[:, cols]).astype(BF16)
            z_ref[:, cols] = z
            zs.append(z)
            lse = m + jnp.log(l)
            lse_ref[0, 2 * p:2 * p + 1, :] = lse[:, 0:TQ]
            lse_ref[0, 2 * p + 1:2 * p + 2, :] = lse[:, TQ:]
        hout_ref[...] = h_ref[...] + _dot(jnp.concatenate(zs, axis=1), w_ref[...])

    qblk = pl.BlockSpec((TQ, ATTN_DIM), lambda b, i: (b * nq + i, 0))
    qablk = pl.BlockSpec((TQ, N_HEADS * 128), lambda b, i: (b * nq + i, 0))
    seq = pl.BlockSpec((lp, ATTN_DIM), lambda b, i: (b, 0))
    rowblk = pl.BlockSpec((1, N_HEADS, TQ), lambda b, i: (b, 0, i))
    hblk = pl.BlockSpec((TQ, D_MODEL), lambda b, i: (b * nq + i, 0))
    return pl.pallas_call(
        body, name="attn_fwd", grid=(nb, nq),
        in_specs=[qblk, qablk, seq, seq, pl.BlockSpec((lp, 128), lambda b, i: (b, 0)),
                  pl.BlockSpec((1, ATTN_DIM), lambda b, i: (0, 0)), qblk,
                  pl.BlockSpec((D_MODEL, D_MODEL), lambda b, i: (0, 0)), hblk],
        out_specs=[qblk, qblk, rowblk, hblk],
        out_shape=[jax.ShapeDtypeStruct((n, ATTN_DIM), BF16), jax.ShapeDtypeStruct((n, ATTN_DIM), BF16),
                   jax.ShapeDtypeStruct((nb, N_HEADS, lp), F32), jax.ShapeDtypeStruct((n, D_MODEL), F32)],
        scratch_shapes=[pltpu.VMEM((lpp, 2 * ATTN_DIM), BF16), pltpu.VMEM((ATTN_DIM, lpp), BF16)],
        compiler_params=_params(("parallel", "arbitrary")),
    )(q, qa, k, v, ka, gain, zc, w_out, h)


def _attn_bwd(dz, q, qa, k, v, ka, o, lse, gain, lp, exchange=()):
    n = q.shape[0]
    nb = n // lp
    nq = lp // TQ
    lpp = _padded_keys(lp, TK)
    nw = len(exchange)

    def body(*refs):
        ((dz_ref, q_ref, qa_ref, k_ref, v_ref, ka_ref, o_ref, lse_ref, g_ref), xin,
         (dq_ref, dk_ref, dv_ref, dka_ref, dfr_ref, dgain_ref), xout,
         (kx_scr, vx_scr, kt_scr, dkx_scr, dvx_scr), sems) = _split_refs(refs, 9, nw, 6, 5)
        b = pl.program_id(0)
        i = pl.program_id(1)
        first = _head_masks()
        if nw:
            comm = _Exchange(xin, xout, sems)
            pl.when((b == 0) & (i == 0))(comm.start)

        @pl.when((b == 0) & (i == 0))
        def _():
            dgain_ref[...] = jnp.zeros_like(dgain_ref)

        @pl.when(i == 0)
        def _():
            if lpp > lp:
                kx_scr[lp:lpp, :] = jnp.zeros((lpp - lp, 2 * ATTN_DIM), BF16)
                vx_scr[lp:lpp, :] = jnp.zeros((lpp - lp, ATTN_DIM), BF16)
                kt_scr[:, lp:lpp] = jnp.zeros((ATTN_DIM, lpp - lp), BF16)
            for p in range(N_PAIRS):
                kx_scr[0:lp, 256 * p:256 * p + 128] = k_ref[:, 128 * p:128 * (p + 1)]
                kx_scr[0:lp, 256 * p + 128:256 * (p + 1)] = ka_ref[...]
            vx_scr[0:lp, :] = v_ref[...]
            kt_scr[:, 0:lp] = _transpose_bf16(k_ref[...])
            dkx_scr[...] = jnp.zeros_like(dkx_scr)
            dvx_scr[...] = jnp.zeros_like(dvx_scr)

        rhs, rhs_t, lses, dos, dos_t, deltas = [], [], [], [], [], []
        for p in range(N_PAIRS):
            cols = slice(128 * p, 128 * (p + 1))
            side = _query_side(q_ref, qa_ref, p, first)
            rhs.append(side)
            rhs_t.append(_transpose_bf16(side))
            lses.append(jnp.concatenate([lse_ref[0, 2 * p:2 * p + 1, :], lse_ref[0, 2 * p + 1:2 * p + 2, :]], axis=1))
            ov = o_ref[:, cols].astype(F32)
            dzv = dz_ref[:, cols].astype(F32)
            gv = g_ref[:, cols]
            sq = ov * ov
            ms0 = jnp.sum(jnp.where(first, sq, 0.0), axis=1, keepdims=True) * (1.0 / HEAD_DIM)
            ms1 = jnp.sum(jnp.where(first, 0.0, sq), axis=1, keepdims=True) * (1.0 / HEAD_DIM)
            r = _pair_cols(lax.rsqrt(ms0 + EPS), lax.rsqrt(ms1 + EPS), first)
            ohat = ov * r
            dyhat = dzv * gv
            dgain_ref[:, cols] += jnp.sum(dzv * ohat, axis=0, keepdims=True)
            pr = dyhat * ohat
            mean0 = jnp.sum(jnp.where(first, pr, 0.0), axis=1, keepdims=True) * (1.0 / HEAD_DIM)
            mean1 = jnp.sum(jnp.where(first, 0.0, pr), axis=1, keepdims=True) * (1.0 / HEAD_DIM)
            do = r * (dyhat - ohat * _pair_cols(mean0, mean1, first))
            ddt = (do * ov).T
            deltas.append(jnp.concatenate([jnp.sum(ddt[0:HEAD_DIM], axis=0, keepdims=True),
                                           jnp.sum(ddt[HEAD_DIM:], axis=0, keepdims=True)], axis=1))
            do_st = _stack_heads(do.astype(BF16), first)
            dos.append(do_st)
            dos_t.append(_transpose_bf16(do_st))

        def step(c, carry, masked, tk):
            koff = pl.multiple_of(c * TK, TK)
            valid = _chunk_mask(i, c, tk, TK) if masked else None
            new = []
            sts = [_dot(kx_scr[pl.ds(koff, tk), 256 * p:256 * (p + 1)], rhs_t[p]) for p in range(N_PAIRS)]
            dpts = [_dot(vx_scr[pl.ds(koff, tk), 128 * p:128 * (p + 1)], dos_t[p]) for p in range(N_PAIRS)]
            for p in range(N_PAIRS):
                dqt, dfq = carry[p]
                ext = slice(256 * p, 256 * (p + 1))
                cols = slice(128 * p, 128 * (p + 1))
                st = sts[p]
                if masked:
                    st = jnp.where(valid, st, NEG)
                pt = jnp.exp(st - lses[p])
                dst = pt * (dpts[p] - deltas[p])
                dsb = dst.astype(BF16)
                dfq = dfq + jnp.sum(dsb.astype(F32), axis=0, keepdims=True)
                dkx_scr[pl.ds(koff, tk), ext] += _dot(dsb, rhs[p])
                dvx_scr[pl.ds(koff, tk), cols] += _dot(pt.astype(BF16), dos[p])
                kt = _stack_heads_lanes(kt_scr[cols, pl.ds(koff, tk)])
                dqt = dqt + _dot(kt, jnp.concatenate([dsb[:, 0:TQ], dsb[:, TQ:]], axis=0))
                new.append((dqt, dfq))
            return tuple(new)

        init = tuple((jnp.zeros((128, TQ), F32), jnp.zeros((1, 2 * TQ), F32)) for _ in range(N_PAIRS))
        final = _causal_sweep(i, step, init, TK)

        for p in range(N_PAIRS):
            dqt, dfq = final[p]
            dq_ref[:, 128 * p:128 * (p + 1)] = (dqt.T * 0.125).astype(BF16)
            dfr_ref[0, 2 * p:2 * p + 1, :] = dfq[:, 0:TQ]
            dfr_ref[0, 2 * p + 1:2 * p + 2, :] = dfq[:, TQ:]

        @pl.when(i == nq - 1)
        def _():
            dka = jnp.zeros((lp, 128), F32)
            for p in range(N_PAIRS):
                dk_ref[:, 128 * p:128 * (p + 1)] = dkx_scr[0:lp, 256 * p:256 * p + 128].astype(BF16)
                dka = dka + dkx_scr[0:lp, 256 * p + 128:256 * (p + 1)]
            dka_ref[...] = dka
            dv_ref[...] = dvx_scr[0:lp, :].astype(BF16)

        if nw:
            pl.when((b == nb - 1) & (i == nq - 1))(comm.finish)

    qblk = pl.BlockSpec((TQ, ATTN_DIM), lambda b, i: (b * nq + i, 0))
    qablk = pl.BlockSpec((TQ, N_HEADS * 128), lambda b, i: (b * nq + i, 0))
    seq = pl.BlockSpec((lp, ATTN_DIM), lambda b, i: (b, 0))
    kaseq = pl.BlockSpec((lp, 128), lambda b, i: (b, 0))
    rowblk = pl.BlockSpec((1, N_HEADS, TQ), lambda b, i: (b, 0, i))
    gspec = pl.BlockSpec((1, ATTN_DIM), lambda b, i: (0, 0))
    return pl.pallas_call(
        body, name="attn_bwd", grid=(nb, nq),
        in_specs=[qblk, qblk, qablk, seq, seq, kaseq, qblk, rowblk, gspec] + [ANY] * nw,
        out_specs=[qblk, seq, seq, kaseq, rowblk, gspec] + [ANY] * nw,
        out_shape=[jax.ShapeDtypeStruct((n, ATTN_DIM), BF16), jax.ShapeDtypeStruct((n, ATTN_DIM), BF16),
                   jax.ShapeDtypeStruct((n, ATTN_DIM), BF16), jax.ShapeDtypeStruct((n, 128), F32),
                   jax.ShapeDtypeStruct((nb, N_HEADS, lp), F32), jax.ShapeDtypeStruct((1, ATTN_DIM), F32)]
        + [jax.ShapeDtypeStruct(a.shape, a.dtype) for a in exchange],
        scratch_shapes=[pltpu.VMEM((lpp, 2 * ATTN_DIM), BF16), pltpu.VMEM((lpp, ATTN_DIM), BF16),
                        pltpu.VMEM((ATTN_DIM, lpp), BF16), pltpu.VMEM((lpp, 2 * ATTN_DIM), F32),
                        pltpu.VMEM((lpp, ATTN_DIM), F32)] + (_comm_sems(nw) if nw else []),
        compiler_params=_params(("arbitrary", "arbitrary")),
    )(dz, q, qa, k, v, ka, o, lse, gain, *exchange)


def _adamw(parts, w, m, v, name):
    s_parts, r, c = parts.shape
    tr = r
    for t in (256, 128, 64, 32, 16):
        if r % t == 0 and r > t:
            tr = t
            break

    def body(p_ref, w_ref, m_ref, v_ref, g_ref, d_ref, nm_ref, nv_ref):
        g = p_ref[0].astype(F32)
        for s in range(1, s_parts):
            g = g + p_ref[s].astype(F32)
        nm = ADAM_B1 * m_ref[...] + (1.0 - ADAM_B1) * g
        nv = ADAM_B2 * v_ref[...] + (1.0 - ADAM_B2) * (g * g)
        m_hat = nm / (1.0 - ADAM_B1 ** ADAM_STEP)
        v_hat = nv / (1.0 - ADAM_B2 ** ADAM_STEP)
        g_ref[...] = g
        d_ref[...] = -ADAM_LR * (m_hat / (jnp.sqrt(v_hat) + ADAM_EPS) + ADAM_WD * w_ref[...])
        nm_ref[...] = nm
        nv_ref[...] = nv

    blk = pl.BlockSpec((tr, c), lambda i: (i, 0))
    return pl.pallas_call(
        body, name=name, grid=(r // tr,),
        in_specs=[pl.BlockSpec((s_parts, tr, c), lambda i: (0, i, 0)), blk, blk, blk],
        out_specs=[blk] * 4,
        out_shape=[jax.ShapeDtypeStruct((r, c), F32)] * 4,
        compiler_params=_params(("parallel",)),
    )(parts, w, m, v)


def _sum_parts(parts, name):
    s_parts, r, c = parts.shape

    def body(p_ref, out_ref):
        acc = p_ref[0]
        for s in range(1, s_parts):
            acc = acc + p_ref[s]
        out_ref[...] = acc

    return pl.pallas_call(
        body, name=name, out_shape=jax.ShapeDtypeStruct((r, c), F32),
        in_specs=[pl.BlockSpec(memory_space=pltpu.VMEM)], out_specs=pl.BlockSpec(memory_space=pltpu.VMEM),
    )(parts)


SMALL_ROWS = 184
LOSS_ROW = 181


def _pack_small(d_gains, d_gc, d_ga, d_bf, d_conv, d_meta, loss_part):
    rows = [g.reshape(8, 128) for g in d_gains]
    rows += [d_gc.reshape(4, 128), d_ga.reshape(4, 128), d_bf.reshape(1, 128)]
    rows += [d_conv.reshape(12, 128), d_meta.reshape(128, 128), jnp.pad(loss_part, ((0, 0), (0, 127)))]
    packed = jnp.concatenate(rows, axis=0)
    return jnp.pad(packed, ((0, SMALL_ROWS - packed.shape[0]), (0, 0)))


def kernel(x, meta_tokens, ffn1_norm, ffn1_w_gu, ffn1_w_down, mix_norm, w_in, conv_w, b_f, out_norm_conv, out_norm_attn, w_out, ffn2_norm, ffn2_w_gu, ffn2_w_down, final_norm, loss_target, m_meta_tokens, m_ffn1_norm, m_ffn1_w_gu, m_ffn1_w_down, m_mix_norm, m_w_in, m_conv_w, m_b_f, m_out_norm_conv, m_out_norm_attn, m_w_out, m_ffn2_norm, m_ffn2_w_gu, m_ffn2_w_down, m_final_norm, v_meta_tokens, v_ffn1_norm, v_ffn1_w_gu, v_ffn1_w_down, v_mix_norm, v_w_in, v_conv_w, v_b_f, v_out_norm_conv, v_out_norm_attn, v_w_out, v_ffn2_norm, v_ffn2_w_gu, v_ffn2_w_down, v_final_norm):
    nb, seq, _ = x.shape
    lp = PAD + N_META + seq
    me = 4 * lax.axis_index("x") + 2 * lax.axis_index("y") + lax.axis_index("c")
    shard_gu = D_FF // 4
    shard_d = D_FF // N_DEV

    small_in = jnp.concatenate(
        [meta_tokens, jnp.pad(conv_w[0], ((0, 0), (0, 128 - conv_w.shape[2]))), jnp.zeros((5, 128), F32)], axis=0)
    wgu1_8, wd1_8, small_8 = _all_gather(
        [ffn1_w_gu[0].T.astype(BF16), ffn1_w_down[0].astype(BF16), small_in], "gather_ffn1")
    meta_full = small_8[:, 0:N_META, :].transpose(1, 0, 2).reshape(N_META, D_MODEL)
    conv_full = small_8[:, N_META:N_META + 3, 0:CONV_DIM // N_DEV].transpose(1, 0, 2).reshape(3, CONV_DIM)
    wgu1 = wgu1_8.reshape(W_GU_SHAPE)
    wd1 = wd1_8.reshape(W_D_SHAPE)
    b_f_row = jnp.pad(b_f, ((0, 0), (0, 128 - N_HEADS)))
    gmat = _group_matrix()

    x2d = x.reshape(nb * seq, D_MODEL)
    later = [w_in[0].T.astype(BF16), w_out[0].astype(BF16), ffn2_w_gu[0].T.astype(BF16), ffn2_w_down[0].astype(BF16)]
    h1, n1, gate1, up1, win_8, wout_8, wgu2_8, wd2_8 = _ffn1_fwd(x2d, meta_full, lp, ffn1_norm, wgu1, wd1, later)
    wgu2 = wgu2_8.reshape(W_GU_SHAPE)
    wd2 = wd2_8.reshape(W_D_SHAPE)
    w_in_full = jnp.pad(win_8.reshape(IN_DIM, D_MODEL), ((0, IN_PAD - IN_DIM), (0, 0)))
    w_out_full = wout_8.reshape(D_MODEL, D_MODEL)

    bg, cg, hc, q, k, v, fg = _inproj_fwd(h1, mix_norm, w_in_full)
    zc = _conv_fwd(bg, cg, hc, conv_full, out_norm_conv, gmat, lp)
    ka, qa = _fgate_fwd(fg, b_f_row, lp)
    za, o, lse, h2 = _attn_fwd(q, qa, k, v, ka, out_norm_attn, zc, w_out_full, h1, lp)
    dh3, n3, gate2, up2, loss_part, d_final = _ffn2_fwd_loss(
        h2, ffn2_norm, wgu2, wd2, final_norm.reshape(1, D_MODEL), loss_target.reshape(nb * seq, D_MODEL), lp)

    dgu2, dwd2 = _ffn_bwd_act_wd(dh3, gate2, up2, wd2, "ffn2_bwd_act")
    dh2, d_ffn2 = _ffn_bwd_in(dh3, h2, ffn2_norm, dgu2, wgu2, "ffn2_bwd_in")
    dwgu2 = _ffn_bwd_wgu(n3, dgu2, "ffn2_bwd_wgu")
    dzc, dza, dwout = _outproj_bwd(dh2, zc, za, w_out_full)
    send_a = [dwgu2.reshape(N_DEV, shard_gu, D_MODEL), dwd2.reshape(N_DEV, shard_d, D_MODEL),
              dwout.reshape(N_DEV, D_MODEL // N_DEV, D_MODEL)]
    dq, dk, dv, dka, dfr, d_ga, p_wgu2, p_wd2, p_wout = _attn_bwd(
        dza, q, qa, k, v, ka, o, lse, out_norm_attn, lp, exchange=send_a)
    dfg, d_bf = _fgate_bwd(dka, dfr, fg, b_f_row, lp)
    dbg, dcg, dhc, d_conv, d_gc = _conv_bwd(dzc, bg, cg, hc, conv_full, out_norm_conv, gmat, lp)
    dh1, dwin, d_mix = _inproj_bwd([dbg, dcg, dhc, dq, dk, dv], dfg, dh2, h1, mix_norm, w_in_full)
    dwin_8 = dwin[0:IN_DIM].reshape(N_DEV, IN_DIM // N_DEV, D_MODEL)
    dgu1, dwd1, p_win = _ffn_bwd_act_wd(dh1, gate1, up1, wd1, "ffn1_bwd_act", exchange=[dwin_8])
    dwgu1, p_wd1 = _ffn_bwd_wgu(n1, dgu1, "ffn1_bwd_wgu", exchange=[dwd1.reshape(N_DEV, shard_d, D_MODEL)])
    own = dwgu1.reshape(N_DEV, shard_gu, D_MODEL)
    (got,) = _pair_exchange([own], "pair_exchange_ffn1")
    chip_sum = _pair_sum(own, got, "pair_sum_wgu1")
    dh0, d_ffn1, p_wgu1 = _ffn_bwd_in(dh1, None, ffn1_norm, dgu1, wgu1, "ffn1_bwd_in",
                                      tokens=(x2d, meta_full, lp), exchange=[chip_sum])

    dh0 = dh0.reshape(nb, lp, D_MODEL)
    grad_x = dh0[:, PAD + N_META:, :]
    d_meta = jnp.sum(dh0[:, PAD:PAD + N_META, :], axis=0)

    small = _pack_small([d_ffn1, d_mix, d_ffn2, d_final], d_gc, d_ga, d_bf, d_conv, d_meta, loss_part)
    small_all = _gather_direct(small, "gather_small_grads")
    small_sum = _sum_parts(small_all, "sum_small_grads")
    g_ffn1n, g_mixn, g_ffn2n, g_finaln = (small_sum[8 * t:8 * t + 8].reshape(1, D_MODEL) for t in range(4))
    g_gc = small_sum[32:36].reshape(1, CONV_DIM)
    g_ga = small_sum[36:40].reshape(1, ATTN_DIM)
    g_bf = small_sum[40:41, 0:N_HEADS]
    g_conv_full = small_sum[41:53].reshape(3, CONV_DIM)
    g_meta_full = small_sum[53:181].reshape(N_META, D_MODEL)
    g_conv = lax.dynamic_slice_in_dim(g_conv_full, me * (CONV_DIM // N_DEV), CONV_DIM // N_DEV, axis=1)
    g_meta = lax.dynamic_slice_in_dim(g_meta_full, me * (D_MODEL // N_DEV), D_MODEL // N_DEV, axis=1)

    weights = {
        "meta_tokens": (g_meta[None], meta_tokens, m_meta_tokens, v_meta_tokens),
        "ffn1_norm": (g_ffn1n[None], ffn1_norm, m_ffn1_norm, v_ffn1_norm),
        "ffn1_w_gu": (p_wgu1, ffn1_w_gu[0].T, m_ffn1_w_gu[0].T, v_ffn1_w_gu[0].T),
        "ffn1_w_down": (p_wd1, ffn1_w_down[0], m_ffn1_w_down[0], v_ffn1_w_down[0]),
        "mix_norm": (g_mixn[None], mix_norm, m_mix_norm, v_mix_norm),
        "w_in": (p_win, w_in[0].T, m_w_in[0].T, v_w_in[0].T),
        "conv_w": (g_conv[None], conv_w[0], m_conv_w[0], v_conv_w[0]),
        "b_f": (g_bf[None], b_f, m_b_f, v_b_f),
        "out_norm_conv": (g_gc[None], out_norm_conv, m_out_norm_conv, v_out_norm_conv),
        "out_norm_attn": (g_ga[None], out_norm_attn, m_out_norm_attn, v_out_norm_attn),
        "w_out": (p_wout, w_out[0], m_w_out[0], v_w_out[0]),
        "ffn2_norm": (g_ffn2n[None], ffn2_norm, m_ffn2_norm, v_ffn2_norm),
        "ffn2_w_gu": (p_wgu2, ffn2_w_gu[0].T, m_ffn2_w_gu[0].T, v_ffn2_w_gu[0].T),
        "ffn2_w_down": (p_wd2, ffn2_w_down[0], m_ffn2_w_down[0], v_ffn2_w_down[0]),
        "final_norm": (g_finaln[None], final_norm.reshape(1, D_MODEL), m_final_norm.reshape(1, D_MODEL),
                       v_final_norm.reshape(1, D_MODEL)),
    }
    shapes = {"meta_tokens": meta_tokens.shape, "ffn1_norm": ffn1_norm.shape, "ffn1_w_gu": ffn1_w_gu.shape,
              "ffn1_w_down": ffn1_w_down.shape, "mix_norm": mix_norm.shape, "w_in": w_in.shape,
              "conv_w": conv_w.shape, "b_f": b_f.shape, "out_norm_conv": out_norm_conv.shape,
              "out_norm_attn": out_norm_attn.shape, "w_out": w_out.shape, "ffn2_norm": ffn2_norm.shape,
              "ffn2_w_gu": ffn2_w_gu.shape, "ffn2_w_down": ffn2_w_down.shape, "final_norm": final_norm.shape}
    grads, deltas, new_m, new_v = [], [], [], []
    for name, (p, w, m, vv) in weights.items():
        g, d, nm, nv = _adamw(p, w, m, vv, "adamw_" + name)
        if name in ("ffn1_w_gu", "ffn2_w_gu", "w_in"):
            g, d, nm, nv = g.T, d.T, nm.T, nv.T
        shape = shapes[name]
        grads.append(g.reshape(shape))
        deltas.append(d.reshape(shape))
        new_m.append(nm.reshape(shape))
        new_v.append(nv.reshape(shape))

    loss = small_sum[LOSS_ROW, 0]
    return (loss, grad_x, *grads, *deltas, *new_m, *new_v)
```

```python
import jax
import jax.numpy as jnp
from jax import lax
from jax.experimental import pallas as pl
from jax.experimental.pallas import tpu as pltpu

F32 = jnp.float32
BF16 = jnp.bfloat16

N_DEV = 8
D_MODEL = 1024
N_META = 16
PAD = 128 - N_META
CONV_DIM = 512
ATTN_DIM = 512
HEAD_DIM = 64
N_HEADS = 8
N_PAIRS = N_HEADS // 2
D_FF = 2816
IN_DIM = 3080
IN_PAD = 3200
IN_MAIN = 3072
N_PIECE = IN_MAIN // 512
EPS = 1e-6
NEG = -1e30
TQ = 128
TK = 512
TK_FWD = 1024
VMEM_LIMIT = 56 * 1024 * 1024

HID_PIECES = ((0, 1024), (1024, 2048), (2048, D_FF))
ACT_PIECES = tuple((a, min(a + 256, D_FF)) for a in range(0, D_FF, 256))
W_GU_SHAPE = (2, D_FF, D_MODEL)
W_D_SHAPE = (D_FF, D_MODEL)
RING = 3

ADAM_LR = 0.001
ADAM_B1 = 0.9
ADAM_B2 = 0.999
ADAM_EPS = 1e-08
ADAM_WD = 0.01
ADAM_STEP = 10

MESH = pl.DeviceIdType.MESH
ANY = pl.BlockSpec(memory_space=pl.ANY)


def _params(sem=None):
    return pltpu.CompilerParams(dimension_semantics=sem, vmem_limit_bytes=VMEM_LIMIT)


def _row_tile(n, prefer):
    for t in (prefer, 512, 256, 128):
        if t <= n and n % t == 0:
            return t
    raise ValueError(f"no row tile for {n}")


def _resident(shape):
    zeros = (0,) * len(shape)
    return pl.BlockSpec(shape, lambda i: zeros, pipeline_mode=pl.Buffered(1))


def _dot(a, b):
    return jnp.dot(a, b, preferred_element_type=F32)


def _dot_nt(a, b):
    return lax.dot_general(a, b, (((1,), (1,)), ((), ())), preferred_element_type=F32)


def _dot_tn(a, b):
    return lax.dot_general(a, b, (((0,), (0,)), ((), ())), preferred_element_type=F32)


def _rms(x, g):
    r = lax.rsqrt(jnp.mean(x * x, axis=-1, keepdims=True) + EPS)
    xhat = x * r
    return xhat * g, xhat, r


def _rms_bwd(dn, xhat, r, g):
    dxhat = dn * g
    return r * (dxhat - xhat * jnp.mean(dxhat * xhat, axis=-1, keepdims=True))


def _sigmoid(x):
    return 1.0 / (1.0 + jnp.exp(-x))


def _place():
    return lax.axis_index("x"), lax.axis_index("y"), lax.axis_index("c")


def _comm_sems(nw):
    return [pltpu.SemaphoreType.DMA((nw, 7)), pltpu.SemaphoreType.DMA((nw, 7)), pltpu.SemaphoreType.DMA((nw,))]


def _flip(v, bit):
    return 1 - v if bit else v


class _Gather:
    def __init__(self, ins, outs, sems):
        self.ins, self.outs = ins, outs
        self.send, self.recv, self.local = sems
        x, y, c = _place()
        self.c = c
        self.me, self.sibling = (x, y, c), (x, y, 1 - c)
        first = ((x + 1 - c) % 2, (y + c) % 2)
        second = ((x + c) % 2, (y + 1 - c) % 2)
        self.chips = [first, second, (1 - x, 1 - y)]
        self.targets = [first, second, second]

    def _copy(self, w, k, block, to, own=False):
        slot = self.outs[w].at[4 * block[0] + 2 * block[1] + block[2]]
        return pltpu.make_async_remote_copy(
            src_ref=self.ins[w] if own else slot, dst_ref=slot,
            send_sem=self.send.at[w, k], recv_sem=self.recv.at[w, k], device_id=to, device_id_type=MESH)

    def _mine(self, w):
        x, y, c = self.me
        return pltpu.make_async_copy(self.ins[w], self.outs[w].at[4 * x + 2 * y + c], self.local.at[w])

    def _first(self, w):
        return ([self._copy(w, 0, self.me, self.sibling, own=True)]
                + [self._copy(w, 1 + j, self.me, (*self.targets[j], self.c), own=True) for j in range(2)])

    def _relay(self, w):
        return self._copy(w, 3, (*self.chips[0], self.c), (*self.targets[2], self.c))

    def _landed(self, w, j):
        return self._copy(w, 1 + j, (*self.chips[j], self.c), self.me)

    def _passed(self, w):
        return [self._copy(w, 4 + j, (*chip, self.c), self.sibling) for j, chip in enumerate(self.chips)]

    def start(self):
        for w in range(len(self.ins)):
            self._mine(w).start()
        for w in range(len(self.ins)):
            for cp in self._first(w):
                cp.start()

    def relay(self):
        for w in range(len(self.ins)):
            self._landed(w, 0).wait_recv()
            self._relay(w).start()
            self._passed(w)[0].start()

    def forward(self):
        for w in range(len(self.ins)):
            for j in (1, 2):
                self._landed(w, j).wait_recv()
                self._passed(w)[j].start()

    def finish(self):
        from_sibling = [self.chips[1], self.chips[0], self.chips[2]]
        for w in range(len(self.ins)):
            self._copy(w, 0, self.sibling, self.me).wait_recv()
            for j, chip in enumerate(from_sibling):
                self._copy(w, 4 + j, (*chip, 1 - self.c), self.me).wait_recv()
        for w in range(len(self.ins)):
            for cp in self._first(w) + [self._relay(w)] + self._passed(w):
                cp.wait_send()
            self._mine(w).wait()


class _Exchange:
    def __init__(self, ins, outs, sems):
        self.ins, self.outs = ins, outs
        self.send, self.recv, self.local = sems
        self.x, self.y, self.c = _place()
        self.me = 4 * self.x + 2 * self.y + self.c

    def _copy(self, w, k):
        peer = (_flip(self.x, ((k + 1) >> 2) & 1), _flip(self.y, ((k + 1) >> 1) & 1), _flip(self.c, (k + 1) & 1))
        return pltpu.make_async_remote_copy(
            src_ref=self.ins[w].at[4 * peer[0] + 2 * peer[1] + peer[2]], dst_ref=self.outs[w].at[self.me],
            send_sem=self.send.at[w, k], recv_sem=self.recv.at[w, k], device_id=peer, device_id_type=MESH)

    def _mine(self, w):
        return pltpu.make_async_copy(self.ins[w].at[self.me], self.outs[w].at[self.me], self.local.at[w])

    def start(self):
        for w in range(len(self.ins)):
            self._mine(w).start()
            for k in range(N_DEV - 1):
                self._copy(w, k).start()

    def finish(self):
        for w in range(len(self.ins)):
            for k in range(N_DEV - 1):
                self._copy(w, k).wait()
            self._mine(w).wait()


class _PairExchange:
    def __init__(self, ins, outs, sems):
        self.ins, self.outs = ins, outs
        self.send, self.recv, _ = sems
        x, y, self.c = _place()
        self.sibling = (x, y, 1 - self.c)

    def _copy(self, w, t):
        return pltpu.make_async_remote_copy(
            src_ref=self.ins[w].at[2 * t + 1 - self.c], dst_ref=self.outs[w].at[t],
            send_sem=self.send.at[w, t], recv_sem=self.recv.at[w, t], device_id=self.sibling, device_id_type=MESH)

    def start(self):
        for w in range(len(self.ins)):
            for t in range(4):
                self._copy(w, t).start()

    def finish(self):
        for w in range(len(self.ins)):
            for t in range(4):
                self._copy(w, t).wait()


class _ChipExchange:
    def __init__(self, ins, outs, sems):
        self.ins, self.outs = ins, outs
        self.send, self.recv, self.local = sems
        self.x, self.y, self.c = _place()
        self.chip = 2 * self.x + self.y

    def _copy(self, w, k):
        px, py = _flip(self.x, ((k + 1) >> 1) & 1), _flip(self.y, (k + 1) & 1)
        return pltpu.make_async_remote_copy(
            src_ref=self.ins[w].at[2 * px + py], dst_ref=self.outs[w].at[self.chip],
            send_sem=self.send.at[w, k], recv_sem=self.recv.at[w, k], device_id=(px, py, self.c),
            device_id_type=MESH)

    def _mine(self, w):
        return pltpu.make_async_copy(self.ins[w].at[self.chip], self.outs[w].at[self.chip], self.local.at[w])

    def start(self):
        for w in range(len(self.ins)):
            self._mine(w).start()
            for k in range(3):
                self._copy(w, k).start()

    def finish(self):
        for w in range(len(self.ins)):
            for k in range(3):
                self._copy(w, k).wait()
            self._mine(w).wait()


def _split_refs(refs, n_in, n_comm, n_out, n_scr):
    a = n_in
    b = a + n_comm
    c = b + n_out
    d = c + n_comm
    e = d + n_scr
    return refs[:a], refs[a:b], refs[b:c], refs[c:d], refs[d:e], refs[e:]


def _all_gather(xs, name):
    nw = len(xs)

    def body(*refs):
        comm = _Gather(refs[:nw], refs[nw:2 * nw], refs[2 * nw:])
        comm.start()
        comm.relay()
        comm.forward()
        comm.finish()

    return pl.pallas_call(
        body, name=name, in_specs=[ANY] * nw, out_specs=[ANY] * nw,
        out_shape=[jax.ShapeDtypeStruct((N_DEV,) + a.shape, a.dtype) for a in xs],
        scratch_shapes=_comm_sems(nw),
    )(*xs)


def _pair_exchange(xs, name):
    nw = len(xs)

    def body(*refs):
        comm = _PairExchange(refs[:nw], refs[nw:2 * nw], refs[2 * nw:])
        comm.start()
        comm.finish()

    return pl.pallas_call(
        body, name=name, in_specs=[ANY] * nw, out_specs=[ANY] * nw,
        out_shape=[jax.ShapeDtypeStruct((4,) + a.shape[1:], a.dtype) for a in xs],
        scratch_shapes=_comm_sems(nw),
    )(*xs)


def _pair_sum(own, got, name):
    _, r, c = own.shape
    tr = r
    for t in (256, 128, 64, 32, 16):
        if r % t == 0 and r > t:
            tr = t
            break

    def body(own_ref, got_ref, out_ref):
        mine = jnp.where(lax.axis_index("c") == 0, own_ref[:, 0].astype(F32), own_ref[:, 1].astype(F32))
        out_ref[...] = (mine + got_ref[...].astype(F32)).astype(BF16)

    return pl.pallas_call(
        body, name=name, grid=(r // tr,),
        in_specs=[pl.BlockSpec((4, 2, tr, c), lambda i: (0, 0, i, 0)), pl.BlockSpec((4, tr, c), lambda i: (0, i, 0))],
        out_specs=pl.BlockSpec((4, tr, c), lambda i: (0, i, 0)),
        out_shape=jax.ShapeDtypeStruct((4, r, c), BF16),
        compiler_params=_params(("parallel",)),
    )(own.reshape(4, 2, r, c), got)


def _token_spec(k, ksub, nq):
    def index_map(i):
        s = ksub * i + k
        return ((s // nq) * (nq - 1) + jnp.maximum(s % nq, 1) - 1, 0)
    return pl.BlockSpec((128, D_MODEL), index_map)


def _is_lead(i, k, ksub, nq):
    return ((ksub * i + k) % nq) == 0


def _assemble_rows(i, x_refs, meta_ref, nq):
    ksub = len(x_refs)
    lead = jnp.concatenate([jnp.zeros((PAD, D_MODEL), F32), meta_ref[...]], axis=0)
    return jnp.concatenate([jnp.where(_is_lead(i, k, ksub, nq), lead, x_refs[k][...]) for k in range(ksub)], axis=0)


def _swiglu(nb, wgu_ref, wd_ref, gate_ref, up_ref):
    acc = jnp.zeros((nb.shape[0], D_MODEL), F32)
    for a, b in HID_PIECES:
        gate = _dot_nt(nb, wgu_ref[0, a:b, :])
        up = _dot_nt(nb, wgu_ref[1, a:b, :])
        gate_ref[:, a:b] = gate.astype(BF16)
        up_ref[:, a:b] = up.astype(BF16)
        acc = acc + _dot((gate * _sigmoid(gate) * up).astype(BF16), wd_ref[a:b, :])
    return acc


def _ffn1_fwd(x2d, meta, lp, gain, wgu, wd, gather):
    nq = lp // 128
    n = (x2d.shape[0] // (nq - 1)) * nq
    tm = _row_tile(n, 512)
    ksub = tm // 128
    n_i = n // tm
    nw = len(gather)

    def body(*refs):
        x_refs = refs[:ksub]
        (meta_ref, g_ref, wgu_ref, wd_ref), gin, (out_ref, nrm_ref, gate_ref, up_ref), gout, _, sems = \
            _split_refs(refs[ksub:], 4, nw, 4, 0)
        i = pl.program_id(0)
        comm = _Gather(gin, gout, sems)
        pl.when(i == 0)(comm.start)
        pl.when(i == n_i // 2)(comm.relay)
        pl.when(i == max(n_i - 3, n_i // 2))(comm.forward)

        hv = _assemble_rows(i, x_refs, meta_ref, nq)
        y, _, _ = _rms(hv, g_ref[...])
        nb = y.astype(BF16)
        nrm_ref[...] = nb
        out_ref[...] = hv + 0.5 * _swiglu(nb, wgu_ref, wd_ref, gate_ref, up_ref)

        pl.when(i == n_i - 1)(comm.finish)

    rows = pl.BlockSpec((tm, D_MODEL), lambda i: (i, 0))
    hid = pl.BlockSpec((tm, D_FF), lambda i: (i, 0))
    return pl.pallas_call(
        body, name="ffn1_fwd", grid=(n_i,),
        in_specs=[_token_spec(k, ksub, nq) for k in range(ksub)]
        + [pl.BlockSpec((N_META, D_MODEL), lambda i: (0, 0)), pl.BlockSpec((1, D_MODEL), lambda i: (0, 0)),
           _resident(W_GU_SHAPE), _resident(W_D_SHAPE)] + [ANY] * nw,
        out_specs=[rows, rows, hid, hid] + [ANY] * nw,
        out_shape=[jax.ShapeDtypeStruct((n, D_MODEL), F32), jax.ShapeDtypeStruct((n, D_MODEL), BF16),
                   jax.ShapeDtypeStruct((n, D_FF), BF16), jax.ShapeDtypeStruct((n, D_FF), BF16)]
        + [jax.ShapeDtypeStruct((N_DEV,) + a.shape, a.dtype) for a in gather],
        scratch_shapes=_comm_sems(nw),
        compiler_params=_params(("arbitrary",)),
    )(*([x2d] * ksub), meta, gain, wgu, wd, *gather)


def _ffn2_fwd_loss(h, gain, wgu, wd, gfinal, target, lp):
    n = h.shape[0]
    nq = lp // 128
    tm = _row_tile(n, 512)
    ksub = tm // 128
    n_i = n // tm

    def body(*refs):
        t_refs = refs[:ksub]
        h_ref, g_ref, wgu_ref, wd_ref, gf_ref, dh_ref, nrm_ref, gate_ref, up_ref, loss_ref, dgf_ref = refs[ksub:]
        i = pl.program_id(0)

        @pl.when(i == 0)
        def _():
            loss_ref[...] = jnp.zeros_like(loss_ref)
            dgf_ref[...] = jnp.zeros_like(dgf_ref)

        hv = h_ref[...]
        y, _, _ = _rms(hv, g_ref[...])
        nb = y.astype(BF16)
        nrm_ref[...] = nb
        hout = hv + 0.5 * _swiglu(nb, wgu_ref, wd_ref, gate_ref, up_ref)

        gf = gf_ref[...]
        loss = jnp.zeros((1, 1), F32)
        dgf = jnp.zeros((1, D_MODEL), F32)
        for k in range(ksub):
            yk, xhat, r = _rms(hout[128 * k:128 * (k + 1)], gf)
            err = jnp.where(_is_lead(i, k, ksub, nq), 0.0, yk - t_refs[k][...])
            loss = loss + 0.5 * jnp.sum(jnp.sum(err * err, axis=1, keepdims=True), axis=0,
                                        keepdims=True) * (1.0 / D_MODEL)
            dy = err * (1.0 / D_MODEL)
            dh_ref[128 * k:128 * (k + 1), :] = _rms_bwd(dy, xhat, r, gf)
            dgf = dgf + jnp.sum(dy * xhat, axis=0, keepdims=True)
        loss_ref[...] += loss
        dgf_ref[...] += dgf

    rows = pl.BlockSpec((tm, D_MODEL), lambda i: (i, 0))
    hid = pl.BlockSpec((tm, D_FF), lambda i: (i, 0))
    vec = pl.BlockSpec((1, D_MODEL), lambda i: (0, 0))
    return pl.pallas_call(
        body, name="ffn2_fwd_loss", grid=(n_i,),
        in_specs=[_token_spec(k, ksub, nq) for k in range(ksub)]
        + [rows, vec, _resident(W_GU_SHAPE), _resident(W_D_SHAPE), vec],
        out_specs=[rows, rows, hid, hid, pl.BlockSpec((1, 1), lambda i: (0, 0)), vec],
        out_shape=[jax.ShapeDtypeStruct((n, D_MODEL), F32), jax.ShapeDtypeStruct((n, D_MODEL), BF16),
                   jax.ShapeDtypeStruct((n, D_FF), BF16), jax.ShapeDtypeStruct((n, D_FF), BF16),
                   jax.ShapeDtypeStruct((1, 1), F32), jax.ShapeDtypeStruct((1, D_MODEL), F32)],
        compiler_params=_params(("arbitrary",)),
    )(*([target] * ksub), h, gain, wgu, wd, gfinal)


def _ffn_bwd_act_wd(dh_out, gate, up, wd, name, exchange=()):
    n = dh_out.shape[0]
    tm = _row_tile(n, 256)
    n_i = n // tm
    nw = len(exchange)

    def body(*refs):
        (dh_ref, gate_ref, up_ref, wd_ref), xin, (dgu_ref, dw_ref), xout, (acc_scr,), sems = \
            _split_refs(refs, 4, nw, 2, 1)
        i = pl.program_id(0)
        if nw:
            comm = _Exchange(xin, xout, sems)
            pl.when(i == 0)(comm.start)

        @pl.when(i == 0)
        def _():
            acc_scr[...] = jnp.zeros_like(acc_scr)

        dhb = (0.5 * dh_ref[...]).astype(BF16)
        for a, b in ACT_PIECES:
            da = _dot_nt(dhb, wd_ref[a:b, :])
            g = gate_ref[:, a:b].astype(F32)
            u = up_ref[:, a:b].astype(F32)
            sig = _sigmoid(g)
            silu = g * sig
            dgu_ref[:, a:b] = (da * u * (sig + silu * (1.0 - sig))).astype(BF16)
            dgu_ref[:, D_FF + a:D_FF + b] = (da * silu).astype(BF16)
            acc_scr[a:b, :] += _dot_tn((silu * u).astype(BF16), dhb)

        @pl.when(i == n_i - 1)
        def _():
            dw_ref[...] = acc_scr[...].astype(BF16)

        if nw:
            pl.when(i == n_i - 1)(comm.finish)

    rows = pl.BlockSpec((tm, D_MODEL), lambda i: (i, 0))
    hid = pl.BlockSpec((tm, D_FF), lambda i: (i, 0))
    return pl.pallas_call(
        body, name=name, grid=(n_i,),
        in_specs=[rows, hid, hid, _resident(W_D_SHAPE)] + [ANY] * nw,
        out_specs=[pl.BlockSpec((tm, 2 * D_FF), lambda i: (i, 0)), _resident(W_D_SHAPE)] + [ANY] * nw,
        out_shape=[jax.ShapeDtypeStruct((n, 2 * D_FF), BF16), jax.ShapeDtypeStruct(W_D_SHAPE, BF16)]
        + [jax.ShapeDtypeStruct(a.shape, a.dtype) for a in exchange],
        scratch_shapes=[pltpu.VMEM(W_D_SHAPE, F32)] + (_comm_sems(nw) if nw else []),
        compiler_params=_params(("arbitrary",)),
    )(dh_out, gate, up, wd, *exchange)


def _ffn_bwd_in(dh_out, h_in, gain, dgu, wgu, name, tokens=None, exchange=()):
    n = dh_out.shape[0]
    tm = _row_tile(n, 512)
    n_i = n // tm
    nw = len(exchange)
    ksub, nq = (tm // 128, tokens[2] // 128) if tokens else (1, 0)

    def body(*refs):
        h_refs = refs[:ksub]
        (meta_ref, dh_ref, g_ref, dgu_ref, wgu_ref), xin, (dhin_ref, dgain_ref), xout, (ring, ring_sems), sems = \
            _split_refs(refs[ksub:], 5, nw, 2, 2)
        i = pl.program_id(0)
        if nw:
            comm = _ChipExchange(xin, xout, sems)
            pl.when(i == 0)(comm.start)

        def fetch(step):
            slot = step % RING
            row = step * tm if isinstance(step, int) else pl.multiple_of(step * tm, tm)
            return pltpu.make_async_copy(dgu_ref.at[pl.ds(row, tm)], ring.at[slot], ring_sems.at[slot])

        @pl.when(i == 0)
        def _():
            dgain_ref[...] = jnp.zeros_like(dgain_ref)
            for s in range(min(RING - 1, n_i)):
                fetch(s).start()

        @pl.when(i + RING - 1 < n_i)
        def _():
            fetch(i + RING - 1).start()

        fetch(i).wait()
        dn = _dot(ring[i % RING], wgu_ref[...])
        gain_v = g_ref[...]
        hv = _assemble_rows(i, h_refs, meta_ref, nq) if tokens else h_refs[0][...]
        _, xhat, r = _rms(hv, gain_v)
        dhin_ref[...] = dh_ref[...] + _rms_bwd(dn, xhat, r, gain_v)
        dgain_ref[...] += jnp.sum(dn * xhat, axis=0, keepdims=True)

        if nw:
            pl.when(i == n_i - 1)(comm.finish)

    rows = pl.BlockSpec((tm, D_MODEL), lambda i: (i, 0))
    hid = pl.BlockSpec((tm, D_FF), lambda i: (i, 0))
    vec = pl.BlockSpec((1, D_MODEL), lambda i: (0, 0))
    meta_spec = pl.BlockSpec((N_META, D_MODEL), lambda i: (0, 0))
    if tokens:
        h_specs, h_args, meta = [_token_spec(k, ksub, nq) for k in range(ksub)], [tokens[0]] * ksub, tokens[1]
    else:
        h_specs, h_args, meta = [rows], [h_in], jnp.zeros((N_META, D_MODEL), F32)
    return pl.pallas_call(
        body, name=name, grid=(n_i,),
        in_specs=h_specs + [meta_spec, rows, vec, ANY, _resident((2 * D_FF, D_MODEL))] + [ANY] * nw,
        out_specs=[rows, vec] + [ANY] * nw,
        out_shape=[jax.ShapeDtypeStruct((n, D_MODEL), F32), jax.ShapeDtypeStruct((1, D_MODEL), F32)]
        + [jax.ShapeDtypeStruct(a.shape, a.dtype) for a in exchange],
        scratch_shapes=[pltpu.VMEM((RING, tm, 2 * D_FF), BF16), pltpu.SemaphoreType.DMA((RING,))]
        + (_comm_sems(nw) if nw else []),
        compiler_params=_params(("arbitrary",)),
    )(*h_args, meta, dh_out, gain, dgu, wgu.reshape(2 * D_FF, D_MODEL), *exchange)


def _ffn_bwd_wgu(nrm, dgu, name, exchange=()):
    n = nrm.shape[0]
    tm = _row_tile(n, 512)
    n_i = n // tm
    nw = len(exchange)

    def body(*refs):
        (nrm_ref, dgu_ref), xin, (dw_ref,), xout, (acc_scr,), sems = _split_refs(refs, 2, nw, 1, 1)
        i = pl.program_id(0)
        if nw:
            comm = _Exchange(xin, xout, sems)
            pl.when(i == 0)(comm.start)

        @pl.when(i == 0)
        def _():
            acc_scr[...] = jnp.zeros_like(acc_scr)

        nb = nrm_ref[...]
        for half in (0, D_FF):
            for a, b in HID_PIECES:
                acc_scr[half + a:half + b, :] += _dot_tn(dgu_ref[:, half + a:half + b], nb)

        @pl.when(i == n_i - 1)
        def _():
            dw_ref[...] = acc_scr[...].astype(BF16)

        if nw:
            pl.when(i == n_i - 1)(comm.finish)

    shape = (2 * D_FF, D_MODEL)
    res = pl.pallas_call(
        body, name=name, grid=(n_i,),
        in_specs=[pl.BlockSpec((tm, D_MODEL), lambda i: (i, 0)),
                  pl.BlockSpec((tm, 2 * D_FF), lambda i: (i, 0))] + [ANY] * nw,
        out_specs=[_resident(shape)] + [ANY] * nw,
        out_shape=[jax.ShapeDtypeStruct(shape, BF16)] + [jax.ShapeDtypeStruct(a.shape, a.dtype) for a in exchange],
        scratch_shapes=[pltpu.VMEM(shape, F32)] + (_comm_sems(nw) if nw else []),
        compiler_params=_params(("arbitrary",)),
    )(nrm, dgu, *exchange)
    return res if nw else res[0]


def _inproj_fwd(h, gain, w_in):
    n = h.shape[0]
    tm = _row_tile(n, 512)

    def body(h_ref, g_ref, w_ref, *outs):
        y, _, _ = _rms(h_ref[...], g_ref[...])
        nb = y.astype(BF16)
        for p in range(N_PIECE):
            outs[p][...] = _dot_nt(nb, w_ref[512 * p:512 * (p + 1), :]).astype(BF16)
        outs[N_PIECE][...] = _dot_nt(nb, w_ref[IN_MAIN:IN_PAD, :])

    piece = pl.BlockSpec((tm, 512), lambda i: (i, 0))
    return pl.pallas_call(
        body, name="inproj_fwd", grid=(n // tm,),
        in_specs=[pl.BlockSpec((tm, D_MODEL), lambda i: (i, 0)),
                  pl.BlockSpec((1, D_MODEL), lambda i: (0, 0)),
                  pl.BlockSpec((IN_PAD, D_MODEL), lambda i: (0, 0))],
        out_specs=[piece] * N_PIECE + [pl.BlockSpec((tm, 128), lambda i: (i, 0))],
        out_shape=[jax.ShapeDtypeStruct((n, 512), BF16)] * N_PIECE + [jax.ShapeDtypeStruct((n, 128), F32)],
        compiler_params=_params(("parallel",)),
    )(h, gain, w_in)


def _inproj_bwd(dpieces, dfg, dh_out, h_in, gain, w_in):
    n = h_in.shape[0]
    tm = _row_tile(n, 512)
    n_i = n // tm

    def body(*refs):
        dp_refs = refs[:N_PIECE]
        dfg_ref, dh_ref, h_ref, g_ref, w_ref, dhin_ref, dw_ref, dgain_ref, acc_scr = refs[N_PIECE:]
        i = pl.program_id(0)

        @pl.when(i == 0)
        def _():
            acc_scr[...] = jnp.zeros_like(acc_scr)
            dgain_ref[...] = jnp.zeros_like(dgain_ref)

        gain_v = g_ref[...]
        y, xhat, r = _rms(h_ref[...], gain_v)
        nb = y.astype(BF16)
        dn = jnp.zeros((tm, D_MODEL), F32)
        for p in range(N_PIECE + 1):
            lo, hi = (512 * p, 512 * (p + 1)) if p < N_PIECE else (IN_MAIN, IN_PAD)
            dp = (dp_refs[p][...] if p < N_PIECE else dfg_ref[...]).astype(BF16)
            dn = dn + _dot(dp, w_ref[lo:hi, :])
            acc_scr[lo:hi, :] += _dot_tn(dp, nb)
        dhin_ref[...] = dh_ref[...] + _rms_bwd(dn, xhat, r, gain_v)
        dgain_ref[...] += jnp.sum(dn * xhat, axis=0, keepdims=True)

        @pl.when(i == n_i - 1)
        def _():
            dw_ref[...] = acc_scr[...].astype(BF16)

    piece = pl.BlockSpec((tm, 512), lambda i: (i, 0))
    rows = pl.BlockSpec((tm, D_MODEL), lambda i: (i, 0))
    vec = pl.BlockSpec((1, D_MODEL), lambda i: (0, 0))
    wspec = pl.BlockSpec((IN_PAD, D_MODEL), lambda i: (0, 0))
    return pl.pallas_call(
        body, name="inproj_bwd", grid=(n_i,),
        in_specs=[piece] * N_PIECE + [pl.BlockSpec((tm, 128), lambda i: (i, 0)), rows, rows, vec, wspec],
        out_specs=[rows, wspec, vec],
        out_shape=[jax.ShapeDtypeStruct((n, D_MODEL), F32),
                   jax.ShapeDtypeStruct((IN_PAD, D_MODEL), BF16),
                   jax.ShapeDtypeStruct((1, D_MODEL), F32)],
        scratch_shapes=[pltpu.VMEM((IN_PAD, D_MODEL), F32)],
        compiler_params=_params(("arbitrary",)),
    )(*dpieces, dfg, dh_out, h_in, gain, w_in)


def _outproj_bwd(dh, zc, za, w_out):
    n = dh.shape[0]
    tm = _row_tile(n, 512)
    n_i = n // tm

    def body(dh_ref, zc_ref, za_ref, w_ref, dzc_ref, dza_ref, dw_ref, acc_scr):
        i = pl.program_id(0)

        @pl.when(i == 0)
        def _():
            acc_scr[...] = jnp.zeros_like(acc_scr)

        dhb = dh_ref[...].astype(BF16)
        dzc_ref[...] = _dot_nt(dhb, w_ref[0:CONV_DIM, :]).astype(BF16)
        dza_ref[...] = _dot_nt(dhb, w_ref[CONV_DIM:, :]).astype(BF16)
        acc_scr[0:CONV_DIM, :] += _dot_tn(zc_ref[...], dhb)
        acc_scr[CONV_DIM:, :] += _dot_tn(za_ref[...], dhb)

        @pl.when(i == n_i - 1)
        def _():
            dw_ref[...] = acc_scr[...].astype(BF16)

    half = pl.BlockSpec((tm, 512), lambda i: (i, 0))
    wspec = pl.BlockSpec((D_MODEL, D_MODEL), lambda i: (0, 0))
    return pl.pallas_call(
        body, name="outproj_bwd", grid=(n_i,),
        in_specs=[pl.BlockSpec((tm, D_MODEL), lambda i: (i, 0)), half, half, wspec],
        out_specs=[half, half, wspec],
        out_shape=[jax.ShapeDtypeStruct((n, 512), BF16), jax.ShapeDtypeStruct((n, 512), BF16),
                   jax.ShapeDtypeStruct((D_MODEL, D_MODEL), BF16)],
        scratch_shapes=[pltpu.VMEM((D_MODEL, D_MODEL), F32)],
        compiler_params=_params(("arbitrary",)),
    )(dh, zc, za, w_out)


def _group_matrix():
    r = lax.broadcasted_iota(jnp.int32, (128, 128), 0) // HEAD_DIM
    c = lax.broadcasted_iota(jnp.int32, (128, 128), 1) // HEAD_DIM
    return jnp.where(r == c, 1.0 / HEAD_DIM, 0.0).astype(BF16)


def _group_mean(x, gmat):
    hi = x.astype(BF16)
    lo = (x - hi.astype(F32)).astype(BF16)
    return _dot(hi, gmat) + _dot(lo, gmat)


def _shift_rows(x, s):
    rows = x.shape[0]
    t = lax.broadcasted_iota(jnp.int32, x.shape, 0)
    rolled = pltpu.roll(x, s % rows, 0)
    keep = (t >= s) if s > 0 else (t < rows + s)
    return jnp.where(keep, rolled, 0.0)


def _conv_parts(bg_ref, cg_ref, hc_ref, w_ref):
    bg = bg_ref[...].astype(F32)
    cg = cg_ref[...].astype(F32)
    hc = hc_ref[...].astype(F32)
    u = cg * hc
    u1 = _shift_rows(u, 1)
    u2 = _shift_rows(u, 2)
    conv = w_ref[2:3, :] * u + w_ref[1:2, :] * u1 + w_ref[0:1, :] * u2
    return bg, cg, hc, u, u1, u2, conv


def _conv_fwd(bg, cg, hc, conv_w, gain, gmat, lp):
    n = bg.shape[0]
    nb = n // lp

    def body(bg_ref, cg_ref, hc_ref, w_ref, g_ref, gm_ref, z_ref):
        bgv, _, _, _, _, _, conv = _conv_parts(bg_ref, cg_ref, hc_ref, w_ref)
        yc = bgv * conv
        r = lax.rsqrt(_group_mean(yc * yc, gm_ref[...]) + EPS)
        z_ref[...] = (yc * r * g_ref[...]).astype(BF16)

    blk = pl.BlockSpec((lp, 128), lambda c, b: (b, c))
    return pl.pallas_call(
        body, name="conv_fwd", grid=(CONV_DIM // 128, nb),
        in_specs=[blk, blk, blk, pl.BlockSpec((3, 128), lambda c, b: (0, c)),
                  pl.BlockSpec((1, 128), lambda c, b: (0, c)), pl.BlockSpec((128, 128), lambda c, b: (0, 0))],
        out_specs=blk,
        out_shape=jax.ShapeDtypeStruct((n, CONV_DIM), BF16),
        compiler_params=_params(("parallel", "parallel")),
    )(bg, cg, hc, conv_w, gain, gmat)


def _conv_bwd(dz, bg, cg, hc, conv_w, gain, gmat, lp):
    n = bg.shape[0]
    nb = n // lp

    def body(dz_ref, bg_ref, cg_ref, hc_ref, w_ref, g_ref, gm_ref,
             dbg_ref, dcg_ref, dhc_ref, dw_ref, dgain_ref):
        b = pl.program_id(1)

        @pl.when(b == 0)
        def _():
            dw_ref[...] = jnp.zeros_like(dw_ref)
            dgain_ref[...] = jnp.zeros_like(dgain_ref)

        bgv, cgv, hcv, u, u1, u2, conv = _conv_parts(bg_ref, cg_ref, hc_ref, w_ref)
        gm = gm_ref[...]
        yc = bgv * conv
        r = lax.rsqrt(_group_mean(yc * yc, gm) + EPS)
        yhat = yc * r
        dzv = dz_ref[...].astype(F32)
        dyhat = dzv * g_ref[...]
        dgain_ref[...] += jnp.sum(dzv * yhat, axis=0, keepdims=True)
        dyc = r * (dyhat - yhat * _group_mean(dyhat * yhat, gm))
        dbg_ref[...] = (dyc * conv).astype(BF16)
        dconv = dyc * bgv
        du = (w_ref[2:3, :] * dconv + w_ref[1:2, :] * _shift_rows(dconv, -1)
              + w_ref[0:1, :] * _shift_rows(dconv, -2))
        dcg_ref[...] = (du * hcv).astype(BF16)
        dhc_ref[...] = (du * cgv).astype(BF16)
        dw_ref[0:1, :] += jnp.sum(dconv * u2, axis=0, keepdims=True)
        dw_ref[1:2, :] += jnp.sum(dconv * u1, axis=0, keepdims=True)
        dw_ref[2:3, :] += jnp.sum(dconv * u, axis=0, keepdims=True)

    blk = pl.BlockSpec((lp, 128), lambda c, b: (b, c))
    wspec = pl.BlockSpec((3, 128), lambda c, b: (0, c))
    gspec = pl.BlockSpec((1, 128), lambda c, b: (0, c))
    return pl.pallas_call(
        body, name="conv_bwd", grid=(CONV_DIM // 128, nb),
        in_specs=[blk, blk, blk, blk, wspec, gspec, pl.BlockSpec((128, 128), lambda c, b: (0, 0))],
        out_specs=[blk, blk, blk, wspec, gspec],
        out_shape=[jax.ShapeDtypeStruct((n, CONV_DIM), BF16)] * 3
        + [jax.ShapeDtypeStruct((3, CONV_DIM), F32), jax.ShapeDtypeStruct((1, CONV_DIM), F32)],
        compiler_params=_params(("parallel", "arbitrary")),
    )(dz, bg, cg, hc, conv_w, gain, gmat)


KEY_MASKED = 1e30
ONE_LANE = 24


def _scan_steps(rows):
    s, out = 1, []
    while s < rows:
        out.append(s)
        s *= 2
    return out


def _fgate_fwd(fg, b_f, lp):
    n = fg.shape[0]
    nb = n // lp

    def body(fg_ref, b_ref, ka_ref, qa_ref):
        x = fg_ref[...] + b_ref[...]
        logf = jnp.minimum(x, 0.0) - jnp.log(1.0 + jnp.exp(-jnp.abs(x)))
        t = lax.broadcasted_iota(jnp.int32, (lp, 128), 0)
        lane = lax.broadcasted_iota(jnp.int32, (lp, 128), 1)
        f = jnp.where((t >= PAD) & (lane < N_HEADS), logf, 0.0)
        for s in _scan_steps(lp):
            f = f + _shift_rows(f, s)
        hi = f.astype(BF16).astype(F32)
        rest = f - hi
        mid = rest.astype(BF16).astype(F32)
        lo = (rest - mid).astype(BF16).astype(F32)
        ones = jnp.where((lane >= ONE_LANE) & (lane < ONE_LANE + 3), 1.0, 0.0)
        hi_key = jnp.where((t < PAD) & (lane < N_HEADS), KEY_MASKED, hi)
        ka_ref[...] = (hi_key + pltpu.roll(mid, 8, 1) + pltpu.roll(lo, 16, 1) + ones).astype(BF16)
        for h in range(N_HEADS):
            minus = jnp.where((lane == h) | (lane == 8 + h) | (lane == 16 + h), -1.0, 0.0)
            terms = (jnp.where(lane == ONE_LANE, pltpu.roll(hi, ONE_LANE - h, 1), 0.0)
                     + jnp.where(lane == ONE_LANE + 1, pltpu.roll(mid, ONE_LANE + 1 - h, 1), 0.0)
                     + jnp.where(lane == ONE_LANE + 2, pltpu.roll(lo, ONE_LANE + 2 - h, 1), 0.0))
            qa_ref[:, 128 * h:128 * (h + 1)] = (minus + terms).astype(BF16)

    return pl.pallas_call(
        body, name="fgate_fwd", grid=(nb,),
        in_specs=[pl.BlockSpec((lp, 128), lambda b: (b, 0)), pl.BlockSpec((1, 128), lambda b: (0, 0))],
        out_specs=[pl.BlockSpec((lp, 128), lambda b: (b, 0)), pl.BlockSpec((lp, N_HEADS * 128), lambda b: (b, 0))],
        out_shape=[jax.ShapeDtypeStruct((n, 128), BF16), jax.ShapeDtypeStruct((n, N_HEADS * 128), BF16)],
        compiler_params=_params(("parallel",)),
    )(fg, b_f)


def _fgate_bwd(dka, dfr, fg, b_f, lp):
    n = fg.shape[0]
    nb = n // lp

    def body(dka_ref, dfr_ref, fg_ref, b_ref, dfg_ref, db_ref):
        b = pl.program_id(0)

        @pl.when(b == 0)
        def _():
            db_ref[...] = jnp.zeros_like(db_ref)

        wide = jnp.concatenate([dfr_ref[0], jnp.zeros((128 - N_HEADS, lp), F32)], axis=0)
        t = lax.broadcasted_iota(jnp.int32, (lp, 128), 0)
        lane = lax.broadcasted_iota(jnp.int32, (lp, 128), 1)
        d = jnp.where(lane < N_HEADS, dka_ref[...], 0.0) + wide.T
        for s in _scan_steps(lp):
            d = d + _shift_rows(d, -s)
        x = fg_ref[...] + b_ref[...]
        dx = jnp.where((t >= PAD) & (lane < N_HEADS), d * _sigmoid(-x), 0.0)
        dfg_ref[...] = dx
        db_ref[...] += jnp.sum(dx, axis=0, keepdims=True)

    return pl.pallas_call(
        body, name="fgate_bwd", grid=(nb,),
        in_specs=[pl.BlockSpec((lp, 128), lambda b: (b, 0)), pl.BlockSpec((1, N_HEADS, lp), lambda b: (b, 0, 0)),
                  pl.BlockSpec((lp, 128), lambda b: (b, 0)), pl.BlockSpec((1, 128), lambda b: (0, 0))],
        out_specs=[pl.BlockSpec((lp, 128), lambda b: (b, 0)), pl.BlockSpec((1, 128), lambda b: (0, 0))],
        out_shape=[jax.ShapeDtypeStruct((n, 128), F32), jax.ShapeDtypeStruct((1, 128), F32)],
        compiler_params=_params(("arbitrary",)),
    )(dka, dfr, fg, b_f)


def _head_masks():
    lane = lax.broadcasted_iota(jnp.int32, (1, 128), 1)
    return lane < HEAD_DIM


def _stack_heads(x2, first):
    zero = jnp.zeros_like(x2)
    return jnp.concatenate([jnp.where(first, x2, zero), jnp.where(first, zero, x2)], axis=0)


def _stack_heads_lanes(xt):
    r = lax.broadcasted_iota(jnp.int32, xt.shape, 0)
    zero = jnp.zeros_like(xt)
    return jnp.concatenate([jnp.where(r < HEAD_DIM, xt, zero), jnp.where(r < HEAD_DIM, zero, xt)], axis=1)


def _pair_cols(col0, col1, first):
    return jnp.where(first, col0, col1)


def _pair_rows(row0, row1):
    r = lax.broadcasted_iota(jnp.int32, (128, TQ), 0)
    return jnp.where(r < HEAD_DIM, row0, row1)


def _query_side(q_ref, qa_ref, p, first):
    q2 = q_ref[:, 128 * p:128 * (p + 1)] * 0.125
    zero = jnp.zeros_like(q2)
    top = jnp.concatenate([jnp.where(first, q2, zero), qa_ref[:, 128 * (2 * p):128 * (2 * p + 1)]], axis=1)
    bot = jnp.concatenate([jnp.where(first, zero, q2), qa_ref[:, 128 * (2 * p + 1):128 * (2 * p + 2)]], axis=1)
    return jnp.concatenate([top, bot], axis=0)


def _padded_keys(lp, chunk):
    return ((lp + chunk - 1) // chunk) * chunk


def _chunk_mask(i, c, tk, chunk):
    r = lax.broadcasted_iota(jnp.int32, (tk, 2 * TQ), 0)
    col = lax.broadcasted_iota(jnp.int32, (tk, 2 * TQ), 1)
    return (c * chunk + r) <= (i * TQ + (col & (TQ - 1)))


def _causal_sweep(i, step, init, chunk):
    per = chunk // TQ
    last = i // per
    carry = lax.fori_loop(0, last, lambda c, carry: step(c, carry, False, chunk), init)
    tails = [lambda carry, r=r: step(last, carry, True, TQ * (r + 1)) for r in range(per)]
    return lax.switch(i % per, tails, carry)


def _transpose_bf16(x):
    return x.astype(F32).T.astype(BF16)


def _attn_fwd(q, qa, k, v, ka, gain, zc, w_out, h, lp):
    n = q.shape[0]
    nb = n // lp
    nq = lp // TQ
    lpp = _padded_keys(lp, TK_FWD)

    def body(q_ref, qa_ref, k_ref, v_ref, ka_ref, g_ref, zc_ref, w_ref, h_ref,
             z_ref, o_ref, lse_ref, hout_ref, kx_scr, vt_scr):
        i = pl.program_id(1)
        first = _head_masks()

        @pl.when(i == 0)
        def _():
            if lpp > lp:
                kx_scr[lp:lpp, :] = jnp.zeros((lpp - lp, 2 * ATTN_DIM), BF16)
                vt_scr[:, lp:lpp] = jnp.zeros((ATTN_DIM, lpp - lp), BF16)
            for p in range(N_PAIRS):
                kx_scr[0:lp, 256 * p:256 * p + 128] = k_ref[:, 128 * p:128 * (p + 1)]
                kx_scr[0:lp, 256 * p + 128:256 * (p + 1)] = ka_ref[...]
            vt_scr[:, 0:lp] = _transpose_bf16(v_ref[...])

        rhs_t = [_transpose_bf16(_query_side(q_ref, qa_ref, p, first)) for p in range(N_PAIRS)]

        def step(c, carry, masked, tk):
            koff = pl.multiple_of(c * TK_FWD, TK_FWD)
            valid = _chunk_mask(i, c, tk, TK_FWD) if masked else None
            new = []
            sts = [_dot(kx_scr[pl.ds(koff, tk), 256 * p:256 * (p + 1)], rhs_t[p]) for p in range(N_PAIRS)]
            for p in range(N_PAIRS):
                m, l, acc = carry[p]
                st = sts[p]
                if masked:
                    st = jnp.where(valid, st, NEG)
                m_new = jnp.maximum(m, jnp.max(st, axis=0, keepdims=True))
                pt = jnp.exp(st - m_new)
                alpha = jnp.exp(m - m_new)
                l = alpha * l + jnp.sum(pt, axis=0, keepdims=True)
                pb = pt.astype(BF16)
                vt = _stack_heads_lanes(vt_scr[128 * p:128 * (p + 1), pl.ds(koff, tk)])
                pv = _dot(vt, jnp.concatenate([pb[:, 0:TQ], pb[:, TQ:]], axis=0))
                acc = acc * _pair_rows(alpha[:, 0:TQ], alpha[:, TQ:]) + pv
                new.append((m_new, l, acc))
            return tuple(new)

        init = tuple((jnp.full((1, 2 * TQ), NEG, F32), jnp.zeros((1, 2 * TQ), F32), jnp.zeros((128, TQ), F32))
                     for _ in range(N_PAIRS))
        final = _causal_sweep(i, step, init, TK_FWD)

        row = lax.broadcasted_iota(jnp.int32, (TQ, 128), 0)
        real = (i * TQ + row) >= PAD
        zs = [zc_ref[...]]
        for p in range(N_PAIRS):
            m, l, acc = final[p]
            inv = 1.0 / l
            ot = acc * _pair_rows(inv[:, 0:TQ], inv[:, TQ:])
            sq = ot * ot
            r0 = lax.rsqrt(jnp.sum(sq[0:HEAD_DIM], axis=0, keepdims=True) * (1.0 / HEAD_DIM) + EPS)
            r1 = lax.rsqrt(jnp.sum(sq[HEAD_DIM:], axis=0, keepdims=True) * (1.0 / HEAD_DIM) + EPS)
            cols = slice(128 * p, 128 * (p + 1))
            o_ref[:, cols] = jnp.where(real, ot.T, 0.0).astype(BF16)
            z = (jnp.where(real, (ot * _pair_rows(r0, r1)).T, 0.0) * g_ref[:, cols]).astype(BF16)
            z_ref[:, cols] = z
            zs.append(z)
            lse = m + jnp.log(l)
            lse_ref[0, 2 * p:2 * p + 1, :] = lse[:, 0:TQ]
            lse_ref[0, 2 * p + 1:2 * p + 2, :] = lse[:, TQ:]
        hout_ref[...] = h_ref[...] + _dot(jnp.concatenate(zs, axis=1), w_ref[...])

    qblk = pl.BlockSpec((TQ, ATTN_DIM), lambda b, i: (b * nq + i, 0))
    qablk = pl.BlockSpec((TQ, N_HEADS * 128), lambda b, i: (b * nq + i, 0))
    seq = pl.BlockSpec((lp, ATTN_DIM), lambda b, i: (b, 0))
    rowblk = pl.BlockSpec((1, N_HEADS, TQ), lambda b, i: (b, 0, i))
    hblk = pl.BlockSpec((TQ, D_MODEL), lambda b, i: (b * nq + i, 0))
    return pl.pallas_call(
        body, name="attn_fwd", grid=(nb, nq),
        in_specs=[qblk, qablk, seq, seq, pl.BlockSpec((lp, 128), lambda b, i: (b, 0)),
                  pl.BlockSpec((1, ATTN_DIM), lambda b, i: (0, 0)), qblk,
                  pl.BlockSpec((D_MODEL, D_MODEL), lambda b, i: (0, 0)), hblk],
        out_specs=[qblk, qblk, rowblk, hblk],
        out_shape=[jax.ShapeDtypeStruct((n, ATTN_DIM), BF16), jax.ShapeDtypeStruct((n, ATTN_DIM), BF16),
                   jax.ShapeDtypeStruct((nb, N_HEADS, lp), F32), jax.ShapeDtypeStruct((n, D_MODEL), F32)],
        scratch_shapes=[pltpu.VMEM((lpp, 2 * ATTN_DIM), BF16), pltpu.VMEM((ATTN_DIM, lpp), BF16)],
        compiler_params=_params(("parallel", "arbitrary")),
    )(q, qa, k, v, ka, gain, zc, w_out, h)


def _attn_bwd(dz, q, qa, k, v, ka, o, lse, gain, lp, exchange=()):
    n = q.shape[0]
    nb = n // lp
    nq = lp // TQ
    lpp = _padded_keys(lp, TK)
    nw = len(exchange)

    def body(*refs):
        ((dz_ref, q_ref, qa_ref, k_ref, v_ref, ka_ref, o_ref, lse_ref, g_ref), xin,
         (dq_ref, dk_ref, dv_ref, dka_ref, dfr_ref, dgain_ref), xout,
         (kx_scr, vx_scr, kt_scr, dkx_scr, dvx_scr), sems) = _split_refs(refs, 9, nw, 6, 5)
        b = pl.program_id(0)
        i = pl.program_id(1)
        first = _head_masks()
        if nw:
            comm = _Exchange(xin, xout, sems)
            pl.when((b == 0) & (i == 0))(comm.start)

        @pl.when((b == 0) & (i == 0))
        def _():
            dgain_ref[...] = jnp.zeros_like(dgain_ref)

        @pl.when(i == 0)
        def _():
            if lpp > lp:
                kx_scr[lp:lpp, :] = jnp.zeros((lpp - lp, 2 * ATTN_DIM), BF16)
                vx_scr[lp:lpp, :] = jnp.zeros((lpp - lp, ATTN_DIM), BF16)
                kt_scr[:, lp:lpp] = jnp.zeros((ATTN_DIM, lpp - lp), BF16)
            for p in range(N_PAIRS):
                kx_scr[0:lp, 256 * p:256 * p + 128] = k_ref[:, 128 * p:128 * (p + 1)]
                kx_scr[0:lp, 256 * p + 128:256 * (p + 1)] = ka_ref[...]
            vx_scr[0:lp, :] = v_ref[...]
            kt_scr[:, 0:lp] = _transpose_bf16(k_ref[...])
            dkx_scr[...] = jnp.zeros_like(dkx_scr)
            dvx_scr[...] = jnp.zeros_like(dvx_scr)

        rhs, rhs_t, lses, dos, dos_t, deltas = [], [], [], [], [], []
        for p in range(N_PAIRS):
            cols = slice(128 * p, 128 * (p + 1))
            side = _query_side(q_ref, qa_ref, p, first)
            rhs.append(side)
            rhs_t.append(_transpose_bf16(side))
            lses.append(jnp.concatenate([lse_ref[0, 2 * p:2 * p + 1, :], lse_ref[0, 2 * p + 1:2 * p + 2, :]], axis=1))
            ov = o_ref[:, cols].astype(F32)
            dzv = dz_ref[:, cols].astype(F32)
            gv = g_ref[:, cols]
            sq = ov * ov
            ms0 = jnp.sum(jnp.where(first, sq, 0.0), axis=1, keepdims=True) * (1.0 / HEAD_DIM)
            ms1 = jnp.sum(jnp.where(first, 0.0, sq), axis=1, keepdims=True) * (1.0 / HEAD_DIM)
            r = _pair_cols(lax.rsqrt(ms0 + EPS), lax.rsqrt(ms1 + EPS), first)
            ohat = ov * r
            dyhat = dzv * gv
            dgain_ref[:, cols] += jnp.sum(dzv * ohat, axis=0, keepdims=True)
            pr = dyhat * ohat
            mean0 = jnp.sum(jnp.where(first, pr, 0.0), axis=1, keepdims=True) * (1.0 / HEAD_DIM)
            mean1 = jnp.sum(jnp.where(first, 0.0, pr), axis=1, keepdims=True) * (1.0 / HEAD_DIM)
            do = r * (dyhat - ohat * _pair_cols(mean0, mean1, first))
            ddt = (do * ov).T
            deltas.append(jnp.concatenate([jnp.sum(ddt[0:HEAD_DIM], axis=0, keepdims=True),
                                           jnp.sum(ddt[HEAD_DIM:], axis=0, keepdims=True)], axis=1))
            do_st = _stack_heads(do.astype(BF16), first)
            dos.append(do_st)
            dos_t.append(_transpose_bf16(do_st))

        def step(c, carry, masked, tk):
            koff = pl.multiple_of(c * TK, TK)
            valid = _chunk_mask(i, c, tk, TK) if masked else None
            new = []
            sts = [_dot(kx_scr[pl.ds(koff, tk), 256 * p:256 * (p + 1)], rhs_t[p]) for p in range(N_PAIRS)]
            dpts = [_dot(vx_scr[pl.ds(koff, tk), 128 * p:128 * (p + 1)], dos_t[p]) for p in range(N_PAIRS)]
            for p in range(N_PAIRS):
                dqt, dfq = carry[p]
                ext = slice(256 * p, 256 * (p + 1))
                cols = slice(128 * p, 128 * (p + 1))
                st = sts[p]
                if masked:
                    st = jnp.where(valid, st, NEG)
                pt = jnp.exp(st - lses[p])
                dst = pt * (dpts[p] - deltas[p])
                dsb = dst.astype(BF16)
                dfq = dfq + jnp.sum(dsb.astype(F32), axis=0, keepdims=True)
                dkx_scr[pl.ds(koff, tk), ext] += _dot(dsb, rhs[p])
                dvx_scr[pl.ds(koff, tk), cols] += _dot(pt.astype(BF16), dos[p])
                kt = _stack_heads_lanes(kt_scr[cols, pl.ds(koff, tk)])
                dqt = dqt + _dot(kt, jnp.concatenate([dsb[:, 0:TQ], dsb[:, TQ:]], axis=0))
                new.append((dqt, dfq))
            return tuple(new)

        init = tuple((jnp.zeros((128, TQ), F32), jnp.zeros((1, 2 * TQ), F32)) for _ in range(N_PAIRS))
        final = _causal_sweep(i, step, init, TK)

        for p in range(N_PAIRS):
            dqt, dfq = final[p]
            dq_ref[:, 128 * p:128 * (p + 1)] = (dqt.T * 0.125).astype(BF16)
            dfr_ref[0, 2 * p:2 * p + 1, :] = dfq[:, 0:TQ]
            dfr_ref[0, 2 * p + 1:2 * p + 2, :] = dfq[:, TQ:]

        @pl.when(i == nq - 1)
        def _():
            dka = jnp.zeros((lp, 128), F32)
            for p in range(N_PAIRS):
                dk_ref[:, 128 * p:128 * (p + 1)] = dkx_scr[0:lp, 256 * p:256 * p + 128].astype(BF16)
                dka = dka + dkx_scr[0:lp, 256 * p + 128:256 * (p + 1)]
            dka_ref[...] = dka
            dv_ref[...] = dvx_scr[0:lp, :].astype(BF16)

        if nw:
            pl.when((b == nb - 1) & (i == nq - 1))(comm.finish)

    qblk = pl.BlockSpec((TQ, ATTN_DIM), lambda b, i: (b * nq + i, 0))
    qablk = pl.BlockSpec((TQ, N_HEADS * 128), lambda b, i: (b * nq + i, 0))
    seq = pl.BlockSpec((lp, ATTN_DIM), lambda b, i: (b, 0))
    kaseq = pl.BlockSpec((lp, 128), lambda b, i: (b, 0))
    rowblk = pl.BlockSpec((1, N_HEADS, TQ), lambda b, i: (b, 0, i))
    gspec = pl.BlockSpec((1, ATTN_DIM), lambda b, i: (0, 0))
    return pl.pallas_call(
        body, name="attn_bwd", grid=(nb, nq),
        in_specs=[qblk, qblk, qablk, seq, seq, kaseq, qblk, rowblk, gspec] + [ANY] * nw,
        out_specs=[qblk, seq, seq, kaseq, rowblk, gspec] + [ANY] * nw,
        out_shape=[jax.ShapeDtypeStruct((n, ATTN_DIM), BF16), jax.ShapeDtypeStruct((n, ATTN_DIM), BF16),
                   jax.ShapeDtypeStruct((n, ATTN_DIM), BF16), jax.ShapeDtypeStruct((n, 128), F32),
                   jax.ShapeDtypeStruct((nb, N_HEADS, lp), F32), jax.ShapeDtypeStruct((1, ATTN_DIM), F32)]
        + [jax.ShapeDtypeStruct(a.shape, a.dtype) for a in exchange],
        scratch_shapes=[pltpu.VMEM((lpp, 2 * ATTN_DIM), BF16), pltpu.VMEM((lpp, ATTN_DIM), BF16),
                        pltpu.VMEM((ATTN_DIM, lpp), BF16), pltpu.VMEM((lpp, 2 * ATTN_DIM), F32),
                        pltpu.VMEM((lpp, ATTN_DIM), F32)] + (_comm_sems(nw) if nw else []),
        compiler_params=_params(("arbitrary", "arbitrary")),
    )(dz, q, qa, k, v, ka, o, lse, gain, *exchange)


def _adamw(parts, w, m, v, name):
    s_parts, r, c = parts.shape
    tr = r
    for t in (256, 128, 64, 32, 16):
        if r % t == 0 and r > t:
            tr = t
            break

    def body(p_ref, w_ref, m_ref, v_ref, g_ref, d_ref, nm_ref, nv_ref):
        g = p_ref[0].astype(F32)
        for s in range(1, s_parts):
            g = g + p_ref[s].astype(F32)
        nm = ADAM_B1 * m_ref[...] + (1.0 - ADAM_B1) * g
        nv = ADAM_B2 * v_ref[...] + (1.0 - ADAM_B2) * (g * g)
        m_hat = nm / (1.0 - ADAM_B1 ** ADAM_STEP)
        v_hat = nv / (1.0 - ADAM_B2 ** ADAM_STEP)
        g_ref[...] = g
        d_ref[...] = -ADAM_LR * (m_hat / (jnp.sqrt(v_hat) + ADAM_EPS) + ADAM_WD * w_ref[...])
        nm_ref[...] = nm
        nv_ref[...] = nv

    blk = pl.BlockSpec((tr, c), lambda i: (i, 0))
    return pl.pallas_call(
        body, name=name, grid=(r // tr,),
        in_specs=[pl.BlockSpec((s_parts, tr, c), lambda i: (0, i, 0)), blk, blk, blk],
        out_specs=[blk] * 4,
        out_shape=[jax.ShapeDtypeStruct((r, c), F32)] * 4,
        compiler_params=_params(("parallel",)),
    )(parts, w, m, v)


def _sum_parts(parts, name):
    s_parts, r, c = parts.shape

    def body(p_ref, out_ref):
        acc = p_ref[0]
        for s in range(1, s_parts):
            acc = acc + p_ref[s]
        out_ref[...] = acc

    return pl.pallas_call(
        body, name=name, out_shape=jax.ShapeDtypeStruct((r, c), F32),
        in_specs=[pl.BlockSpec(memory_space=pltpu.VMEM)], out_specs=pl.BlockSpec(memory_space=pltpu.VMEM),
    )(parts)


SMALL_ROWS = 184
LOSS_ROW = 181


def _pack_small(d_gains, d_gc, d_ga, d_bf, d_conv, d_meta, loss_part):
    rows = [g.reshape(8, 128) for g in d_gains]
    rows += [d_gc.reshape(4, 128), d_ga.reshape(4, 128), d_bf.reshape(1, 128)]
    rows += [d_conv.reshape(12, 128), d_meta.reshape(128, 128), jnp.pad(loss_part, ((0, 0), (0, 127)))]
    packed = jnp.concatenate(rows, axis=0)
    return jnp.pad(packed, ((0, SMALL_ROWS - packed.shape[0]), (0, 0)))


def kernel(x, meta_tokens, ffn1_norm, ffn1_w_gu, ffn1_w_down, mix_norm, w_in, conv_w, b_f, out_norm_conv, out_norm_attn, w_out, ffn2_norm, ffn2_w_gu, ffn2_w_down, final_norm, loss_target, m_meta_tokens, m_ffn1_norm, m_ffn1_w_gu, m_ffn1_w_down, m_mix_norm, m_w_in, m_conv_w, m_b_f, m_out_norm_conv, m_out_norm_attn, m_w_out, m_ffn2_norm, m_ffn2_w_gu, m_ffn2_w_down, m_final_norm, v_meta_tokens, v_ffn1_norm, v_ffn1_w_gu, v_ffn1_w_down, v_mix_norm, v_w_in, v_conv_w, v_b_f, v_out_norm_conv, v_out_norm_attn, v_w_out, v_ffn2_norm, v_ffn2_w_gu, v_ffn2_w_down, v_final_norm):
    nb, seq, _ = x.shape
    lp = PAD + N_META + seq
    me = 4 * lax.axis_index("x") + 2 * lax.axis_index("y") + lax.axis_index("c")
    shard_gu = D_FF // 4
    shard_d = D_FF // N_DEV

    small_in = jnp.concatenate(
        [meta_tokens, jnp.pad(conv_w[0], ((0, 0), (0, 128 - conv_w.shape[2]))), jnp.zeros((5, 128), F32)], axis=0)
    wgu1_8, wd1_8, small_8 = _all_gather(
        [ffn1_w_gu[0].T.astype(BF16), ffn1_w_down[0].astype(BF16), small_in], "gather_ffn1")
    meta_full = small_8[:, 0:N_META, :].transpose(1, 0, 2).reshape(N_META, D_MODEL)
    conv_full = small_8[:, N_META:N_META + 3, 0:CONV_DIM // N_DEV].transpose(1, 0, 2).reshape(3, CONV_DIM)
    wgu1 = wgu1_8.reshape(W_GU_SHAPE)
    wd1 = wd1_8.reshape(W_D_SHAPE)
    b_f_row = jnp.pad(b_f, ((0, 0), (0, 128 - N_HEADS)))
    gmat = _group_matrix()

    x2d = x.reshape(nb * seq, D_MODEL)
    later = [w_in[0].T.astype(BF16), w_out[0].astype(BF16), ffn2_w_gu[0].T.astype(BF16), ffn2_w_down[0].astype(BF16)]
    h1, n1, gate1, up1, win_8, wout_8, wgu2_8, wd2_8 = _ffn1_fwd(x2d, meta_full, lp, ffn1_norm, wgu1, wd1, later)
    wgu2 = wgu2_8.reshape(W_GU_SHAPE)
    wd2 = wd2_8.reshape(W_D_SHAPE)
    w_in_full = jnp.pad(win_8.reshape(IN_DIM, D_MODEL), ((0, IN_PAD - IN_DIM), (0, 0)))
    w_out_full = wout_8.reshape(D_MODEL, D_MODEL)

    bg, cg, hc, q, k, v, fg = _inproj_fwd(h1, mix_norm, w_in_full)
    zc = _conv_fwd(bg, cg, hc, conv_full, out_norm_conv, gmat, lp)
    ka, qa = _fgate_fwd(fg, b_f_row, lp)
    za, o, lse, h2 = _attn_fwd(q, qa, k, v, ka, out_norm_attn, zc, w_out_full, h1, lp)
    dh3, n3, gate2, up2, loss_part, d_final = _ffn2_fwd_loss(
        h2, ffn2_norm, wgu2, wd2, final_norm.reshape(1, D_MODEL), loss_target.reshape(nb * seq, D_MODEL), lp)

    dgu2, dwd2 = _ffn_bwd_act_wd(dh3, gate2, up2, wd2, "ffn2_bwd_act")
    dh2, d_ffn2 = _ffn_bwd_in(dh3, h2, ffn2_norm, dgu2, wgu2, "ffn2_bwd_in")
    dwgu2 = _ffn_bwd_wgu(n3, dgu2, "ffn2_bwd_wgu")
    dzc, dza, dwout = _outproj_bwd(dh2, zc, za, w_out_full)
    send_a = [dwgu2.reshape(N_DEV, shard_gu, D_MODEL), dwd2.reshape(N_DEV, shard_d, D_MODEL),
              dwout.reshape(N_DEV, D_MODEL // N_DEV, D_MODEL)]
    dq, dk, dv, dka, dfr, d_ga, p_wgu2, p_wd2, p_wout = _attn_bwd(
        dza, q, qa, k, v, ka, o, lse, out_norm_attn, lp, exchange=send_a)
    dfg, d_bf = _fgate_bwd(dka, dfr, fg, b_f_row, lp)
    dbg, dcg, dhc, d_conv, d_gc = _conv_bwd(dzc, bg, cg, hc, conv_full, out_norm_conv, gmat, lp)
    dh1, dwin, d_mix = _inproj_bwd([dbg, dcg, dhc, dq, dk, dv], dfg, dh2, h1, mix_norm, w_in_full)
    dwin_8 = dwin[0:IN_DIM].reshape(N_DEV, IN_DIM // N_DEV, D_MODEL)
    dgu1, dwd1, p_win = _ffn_bwd_act_wd(dh1, gate1, up1, wd1, "ffn1_bwd_act", exchange=[dwin_8])
    dwgu1, p_wd1 = _ffn_bwd_wgu(n1, dgu1, "ffn1_bwd_wgu", exchange=[dwd1.reshape(N_DEV, shard_d, D_MODEL)])
    own = dwgu1.reshape(N_DEV, shard_gu, D_MODEL)
    (got,) = _pair_exchange([own], "pair_exchange_ffn1")
    chip_sum = _pair_sum(own, got, "pair_sum_wgu1")
    dh0, d_ffn1, p_wgu1 = _ffn_bwd_in(dh1, None, ffn1_norm, dgu1, wgu1, "ffn1_bwd_in",
                                      tokens=(x2d, meta_full, lp), exchange=[chip_sum])

    dh0 = dh0.reshape(nb, lp, D_MODEL)
    grad_x = dh0[:, PAD + N_META:, :]
    d_meta = jnp.sum(dh0[:, PAD:PAD + N_META, :], axis=0)

    small = _pack_small([d_ffn1, d_mix, d_ffn2, d_final], d_gc, d_ga, d_bf, d_conv, d_meta, loss_part)
    (small_all,) = _all_gather([small], "gather_small_grads")
    small_sum = _sum_parts(small_all, "sum_small_grads")
    g_ffn1n, g_mixn, g_ffn2n, g_finaln = (small_sum[8 * t:8 * t + 8].reshape(1, D_MODEL) for t in range(4))
    g_gc = small_sum[32:36].reshape(1, CONV_DIM)
    g_ga = small_sum[36:40].reshape(1, ATTN_DIM)
    g_bf = small_sum[40:41, 0:N_HEADS]
    g_conv_full = small_sum[41:53].reshape(3, CONV_DIM)
    g_meta_full = small_sum[53:181].reshape(N_META, D_MODEL)
    g_conv = lax.dynamic_slice_in_dim(g_conv_full, me * (CONV_DIM // N_DEV), CONV_DIM // N_DEV, axis=1)
    g_meta = lax.dynamic_slice_in_dim(g_meta_full, me * (D_MODEL // N_DEV), D_MODEL // N_DEV, axis=1)

    weights = {
        "meta_tokens": (g_meta[None], meta_tokens, m_meta_tokens, v_meta_tokens),
        "ffn1_norm": (g_ffn1n[None], ffn1_norm, m_ffn1_norm, v_ffn1_norm),
        "ffn1_w_gu": (p_wgu1, ffn1_w_gu[0].T, m_ffn1_w_gu[0].T, v_ffn1_w_gu[0].T),
        "ffn1_w_down": (p_wd1, ffn1_w_down[0], m_ffn1_w_down[0], v_ffn1_w_down[0]),
        "mix_norm": (g_mixn[None], mix_norm, m_mix_norm, v_mix_norm),
        "w_in": (p_win, w_in[0].T, m_w_in[0].T, v_w_in[0].T),
        "conv_w": (g_conv[None], conv_w[0], m_conv_w[0], v_conv_w[0]),
        "b_f": (g_bf[None], b_f, m_b_f, v_b_f),
        "out_norm_conv": (g_gc[None], out_norm_conv, m_out_norm_conv, v_out_norm_conv),
        "out_norm_attn": (g_ga[None], out_norm_attn, m_out_norm_attn, v_out_norm_attn),
        "w_out": (p_wout, w_out[0], m_w_out[0], v_w_out[0]),
        "ffn2_norm": (g_ffn2n[None], ffn2_norm, m_ffn2_norm, v_ffn2_norm),
        "ffn2_w_gu": (p_wgu2, ffn2_w_gu[0].T, m_ffn2_w_gu[0].T, v_ffn2_w_gu[0].T),
        "ffn2_w_down": (p_wd2, ffn2_w_down[0], m_ffn2_w_down[0], v_ffn2_w_down[0]),
        "final_norm": (g_finaln[None], final_norm.reshape(1, D_MODEL), m_final_norm.reshape(1, D_MODEL),
                       v_final_norm.reshape(1, D_MODEL)),
    }
    shapes = {"meta_tokens": meta_tokens.shape, "ffn1_norm": ffn1_norm.shape, "ffn1_w_gu": ffn1_w_gu.shape,
              "ffn1_w_down": ffn1_w_down.shape, "mix_norm": mix_norm.shape, "w_in": w_in.shape,
              "conv_w": conv_w.shape, "b_f": b_f.shape, "out_norm_conv": out_norm_conv.shape,
              "out_norm_attn": out_norm_attn.shape, "w_out": w_out.shape, "ffn2_norm": ffn2_norm.shape,
              "ffn2_w_gu": ffn2_w_gu.shape, "ffn2_w_down": ffn2_w_down.shape, "final_norm": final_norm.shape}
    grads, deltas, new_m, new_v = [], [], [], []
    for name, (p, w, m, vv) in weights.items():
        g, d, nm, nv = _adamw(p, w, m, vv, "adamw_" + name)
        if name in ("ffn1_w_gu", "ffn2_w_gu", "w_in"):
            g, d, nm, nv = g.T, d.T, nm.T, nv.T
        shape = shapes[name]
        grads.append(g.reshape(shape))
        deltas.append(d.reshape(shape))
        new_m.append(nm.reshape(shape))
        new_v.append(nv.reshape(shape))

    loss = small_sum[LOSS_ROW, 0]
    return (loss, grad_x, *grads, *deltas, *new_m, *new_v)
```

```python
import jax
import jax.numpy as jnp
from jax import lax
from jax.experimental import pallas as pl
from jax.experimental.pallas import tpu as pltpu

F32 = jnp.float32
BF16 = jnp.bfloat16

N_DEV = 8
D_MODEL = 1024
N_META = 16
PAD = 128 - N_META
CONV_DIM = 512
ATTN_DIM = 512
HEAD_DIM = 64
N_HEADS = 8
N_PAIRS = N_HEADS // 2
D_FF = 2816
IN_DIM = 3080
IN_PAD = 3200
IN_MAIN = 3072
N_PIECE = IN_MAIN // 512
EPS = 1e-6
NEG = -1e30
TQ = 128
TK = 512
TK_FWD = 1024
VMEM_LIMIT = 56 * 1024 * 1024

HID_PIECES = ((0, 1024), (1024, 2048), (2048, D_FF))
ACT_PIECES = tuple((a, min(a + 256, D_FF)) for a in range(0, D_FF, 256))
W_GU_SHAPE = (2, D_FF, D_MODEL)
W_D_SHAPE = (D_FF, D_MODEL)

ADAM_LR = 0.001
ADAM_B1 = 0.9
ADAM_B2 = 0.999
ADAM_EPS = 1e-08
ADAM_WD = 0.01
ADAM_STEP = 10

MESH = pl.DeviceIdType.MESH
ANY = pl.BlockSpec(memory_space=pl.ANY)


def _params(sem=None):
    return pltpu.CompilerParams(dimension_semantics=sem, vmem_limit_bytes=VMEM_LIMIT)


def _row_tile(n, prefer):
    for t in (prefer, 512, 256, 128):
        if t <= n and n % t == 0:
            return t
    raise ValueError(f"no row tile for {n}")


def _resident(shape):
    zeros = (0,) * len(shape)
    return pl.BlockSpec(shape, lambda i: zeros, pipeline_mode=pl.Buffered(1))


def _dot(a, b):
    return jnp.dot(a, b, preferred_element_type=F32)


def _dot_nt(a, b):
    return lax.dot_general(a, b, (((1,), (1,)), ((), ())), preferred_element_type=F32)


def _dot_tn(a, b):
    return lax.dot_general(a, b, (((0,), (0,)), ((), ())), preferred_element_type=F32)


def _rms(x, g):
    r = lax.rsqrt(jnp.mean(x * x, axis=-1, keepdims=True) + EPS)
    xhat = x * r
    return xhat * g, xhat, r


def _rms_bwd(dn, xhat, r, g):
    dxhat = dn * g
    return r * (dxhat - xhat * jnp.mean(dxhat * xhat, axis=-1, keepdims=True))


def _sigmoid(x):
    return 1.0 / (1.0 + jnp.exp(-x))


def _place():
    return lax.axis_index("x"), lax.axis_index("y"), lax.axis_index("c")


def _comm_sems(nw):
    return [pltpu.SemaphoreType.DMA((nw, 7)), pltpu.SemaphoreType.DMA((nw, 7)), pltpu.SemaphoreType.DMA((nw,))]


def _flip(v, bit):
    return 1 - v if bit else v


class _Gather:
    def __init__(self, ins, outs, sems):
        self.ins, self.outs = ins, outs
        self.send, self.recv, self.local = sems
        x, y, c = _place()
        self.c = c
        self.me, self.sibling = (x, y, c), (x, y, 1 - c)
        first = ((x + 1 - c) % 2, (y + c) % 2)
        second = ((x + c) % 2, (y + 1 - c) % 2)
        self.chips = [first, second, (1 - x, 1 - y)]
        self.targets = [first, second, second]

    def _copy(self, w, k, block, to, own=False):
        slot = self.outs[w].at[4 * block[0] + 2 * block[1] + block[2]]
        return pltpu.make_async_remote_copy(
            src_ref=self.ins[w] if own else slot, dst_ref=slot,
            send_sem=self.send.at[w, k], recv_sem=self.recv.at[w, k], device_id=to, device_id_type=MESH)

    def _mine(self, w):
        x, y, c = self.me
        return pltpu.make_async_copy(self.ins[w], self.outs[w].at[4 * x + 2 * y + c], self.local.at[w])

    def _first(self, w):
        return ([self._copy(w, 0, self.me, self.sibling, own=True)]
                + [self._copy(w, 1 + j, self.me, (*self.targets[j], self.c), own=True) for j in range(2)])

    def _relay(self, w):
        return self._copy(w, 3, (*self.chips[0], self.c), (*self.targets[2], self.c))

    def _landed(self, w, j):
        return self._copy(w, 1 + j, (*self.chips[j], self.c), self.me)

    def _passed(self, w):
        return [self._copy(w, 4 + j, (*chip, self.c), self.sibling) for j, chip in enumerate(self.chips)]

    def start(self):
        for w in range(len(self.ins)):
            self._mine(w).start()
        for w in range(len(self.ins)):
            for cp in self._first(w):
                cp.start()

    def relay(self):
        for w in range(len(self.ins)):
            self._landed(w, 0).wait_recv()
            self._relay(w).start()
            self._passed(w)[0].start()

    def forward(self):
        for w in range(len(self.ins)):
            for j in (1, 2):
                self._landed(w, j).wait_recv()
                self._passed(w)[j].start()

    def finish(self):
        from_sibling = [self.chips[1], self.chips[0], self.chips[2]]
        for w in range(len(self.ins)):
            self._copy(w, 0, self.sibling, self.me).wait_recv()
            for j, chip in enumerate(from_sibling):
                self._copy(w, 4 + j, (*chip, 1 - self.c), self.me).wait_recv()
        for w in range(len(self.ins)):
            for cp in self._first(w) + [self._relay(w)] + self._passed(w):
                cp.wait_send()
            self._mine(w).wait()


class _Exchange:
    def __init__(self, ins, outs, sems):
        self.ins, self.outs = ins, outs
        self.send, self.recv, self.local = sems
        self.x, self.y, self.c = _place()
        self.me = 4 * self.x + 2 * self.y + self.c

    def _copy(self, w, k):
        peer = (_flip(self.x, ((k + 1) >> 2) & 1), _flip(self.y, ((k + 1) >> 1) & 1), _flip(self.c, (k + 1) & 1))
        return pltpu.make_async_remote_copy(
            src_ref=self.ins[w].at[4 * peer[0] + 2 * peer[1] + peer[2]], dst_ref=self.outs[w].at[self.me],
            send_sem=self.send.at[w, k], recv_sem=self.recv.at[w, k], device_id=peer, device_id_type=MESH)

    def _mine(self, w):
        return pltpu.make_async_copy(self.ins[w].at[self.me], self.outs[w].at[self.me], self.local.at[w])

    def start(self):
        for w in range(len(self.ins)):
            self._mine(w).start()
            for k in range(N_DEV - 1):
                self._copy(w, k).start()

    def finish(self):
        for w in range(len(self.ins)):
            for k in range(N_DEV - 1):
                self._copy(w, k).wait()
            self._mine(w).wait()


class _PairExchange:
    def __init__(self, ins, outs, sems):
        self.ins, self.outs = ins, outs
        self.send, self.recv, _ = sems
        x, y, self.c = _place()
        self.sibling = (x, y, 1 - self.c)

    def _copy(self, w, t):
        return pltpu.make_async_remote_copy(
            src_ref=self.ins[w].at[2 * t + 1 - self.c], dst_ref=self.outs[w].at[t],
            send_sem=self.send.at[w, t], recv_sem=self.recv.at[w, t], device_id=self.sibling, device_id_type=MESH)

    def start(self):
        for w in range(len(self.ins)):
            for t in range(4):
                self._copy(w, t).start()

    def finish(self):
        for w in range(len(self.ins)):
            for t in range(4):
                self._copy(w, t).wait()


class _ChipExchange:
    def __init__(self, ins, outs, sems):
        self.ins, self.outs = ins, outs
        self.send, self.recv, self.local = sems
        self.x, self.y, self.c = _place()
        self.chip = 2 * self.x + self.y

    def _copy(self, w, k):
        px, py = _flip(self.x, ((k + 1) >> 1) & 1), _flip(self.y, (k + 1) & 1)
        return pltpu.make_async_remote_copy(
            src_ref=self.ins[w].at[2 * px + py], dst_ref=self.outs[w].at[self.chip],
            send_sem=self.send.at[w, k], recv_sem=self.recv.at[w, k], device_id=(px, py, self.c),
            device_id_type=MESH)

    def _mine(self, w):
        return pltpu.make_async_copy(self.ins[w].at[self.chip], self.outs[w].at[self.chip], self.local.at[w])

    def start(self):
        for w in range(len(self.ins)):
            self._mine(w).start()
            for k in range(3):
                self._copy(w, k).start()

    def finish(self):
        for w in range(len(self.ins)):
            for k in range(3):
                self._copy(w, k).wait()
            self._mine(w).wait()


def _split_refs(refs, n_in, n_comm, n_out, n_scr):
    a = n_in
    b = a + n_comm
    c = b + n_out
    d = c + n_comm
    e = d + n_scr
    return refs[:a], refs[a:b], refs[b:c], refs[c:d], refs[d:e], refs[e:]


def _all_gather(xs, name):
    nw = len(xs)

    def body(*refs):
        comm = _Gather(refs[:nw], refs[nw:2 * nw], refs[2 * nw:])
        comm.start()
        comm.relay()
        comm.forward()
        comm.finish()

    return pl.pallas_call(
        body, name=name, in_specs=[ANY] * nw, out_specs=[ANY] * nw,
        out_shape=[jax.ShapeDtypeStruct((N_DEV,) + a.shape, a.dtype) for a in xs],
        scratch_shapes=_comm_sems(nw),
    )(*xs)


def _pair_exchange(xs, name):
    nw = len(xs)

    def body(*refs):
        comm = _PairExchange(refs[:nw], refs[nw:2 * nw], refs[2 * nw:])
        comm.start()
        comm.finish()

    return pl.pallas_call(
        body, name=name, in_specs=[ANY] * nw, out_specs=[ANY] * nw,
        out_shape=[jax.ShapeDtypeStruct((4,) + a.shape[1:], a.dtype) for a in xs],
        scratch_shapes=_comm_sems(nw),
    )(*xs)


def _pair_sum(own, got, name):
    _, r, c = own.shape
    tr = r
    for t in (256, 128, 64, 32, 16):
        if r % t == 0 and r > t:
            tr = t
            break

    def body(own_ref, got_ref, out_ref):
        mine = jnp.where(lax.axis_index("c") == 0, own_ref[:, 0].astype(F32), own_ref[:, 1].astype(F32))
        out_ref[...] = (mine + got_ref[...].astype(F32)).astype(BF16)

    return pl.pallas_call(
        body, name=name, grid=(r // tr,),
        in_specs=[pl.BlockSpec((4, 2, tr, c), lambda i: (0, 0, i, 0)), pl.BlockSpec((4, tr, c), lambda i: (0, i, 0))],
        out_specs=pl.BlockSpec((4, tr, c), lambda i: (0, i, 0)),
        out_shape=jax.ShapeDtypeStruct((4, r, c), BF16),
        compiler_params=_params(("parallel",)),
    )(own.reshape(4, 2, r, c), got)


def _token_spec(k, ksub, nq):
    def index_map(i):
        s = ksub * i + k
        return ((s // nq) * (nq - 1) + jnp.maximum(s % nq, 1) - 1, 0)
    return pl.BlockSpec((128, D_MODEL), index_map)


def _is_lead(i, k, ksub, nq):
    return ((ksub * i + k) % nq) == 0


def _assemble_rows(i, x_refs, meta_ref, nq):
    ksub = len(x_refs)
    lead = jnp.concatenate([jnp.zeros((PAD, D_MODEL), F32), meta_ref[...]], axis=0)
    return jnp.concatenate([jnp.where(_is_lead(i, k, ksub, nq), lead, x_refs[k][...]) for k in range(ksub)], axis=0)


def _swiglu(nb, wgu_ref, wd_ref, gate_ref, up_ref):
    acc = jnp.zeros((nb.shape[0], D_MODEL), F32)
    for a, b in HID_PIECES:
        gate = _dot_nt(nb, wgu_ref[0, a:b, :])
        up = _dot_nt(nb, wgu_ref[1, a:b, :])
        gate_ref[:, a:b] = gate.astype(BF16)
        up_ref[:, a:b] = up.astype(BF16)
        acc = acc + _dot((gate * _sigmoid(gate) * up).astype(BF16), wd_ref[a:b, :])
    return acc


def _ffn1_fwd(x2d, meta, lp, gain, wgu, wd, gather):
    nq = lp // 128
    n = (x2d.shape[0] // (nq - 1)) * nq
    tm = _row_tile(n, 512)
    ksub = tm // 128
    n_i = n // tm
    nw = len(gather)

    def body(*refs):
        x_refs = refs[:ksub]
        (meta_ref, g_ref, wgu_ref, wd_ref), gin, (out_ref, nrm_ref, gate_ref, up_ref), gout, _, sems = \
            _split_refs(refs[ksub:], 4, nw, 4, 0)
        i = pl.program_id(0)
        comm = _Gather(gin, gout, sems)
        pl.when(i == 0)(comm.start)
        pl.when(i == n_i // 2)(comm.relay)
        pl.when(i == max(n_i - 3, n_i // 2))(comm.forward)

        hv = _assemble_rows(i, x_refs, meta_ref, nq)
        y, _, _ = _rms(hv, g_ref[...])
        nb = y.astype(BF16)
        nrm_ref[...] = nb
        out_ref[...] = hv + 0.5 * _swiglu(nb, wgu_ref, wd_ref, gate_ref, up_ref)

        pl.when(i == n_i - 1)(comm.finish)

    rows = pl.BlockSpec((tm, D_MODEL), lambda i: (i, 0))
    hid = pl.BlockSpec((tm, D_FF), lambda i: (i, 0))
    return pl.pallas_call(
        body, name="ffn1_fwd", grid=(n_i,),
        in_specs=[_token_spec(k, ksub, nq) for k in range(ksub)]
        + [pl.BlockSpec((N_META, D_MODEL), lambda i: (0, 0)), pl.BlockSpec((1, D_MODEL), lambda i: (0, 0)),
           _resident(W_GU_SHAPE), _resident(W_D_SHAPE)] + [ANY] * nw,
        out_specs=[rows, rows, hid, hid] + [ANY] * nw,
        out_shape=[jax.ShapeDtypeStruct((n, D_MODEL), F32), jax.ShapeDtypeStruct((n, D_MODEL), BF16),
                   jax.ShapeDtypeStruct((n, D_FF), BF16), jax.ShapeDtypeStruct((n, D_FF), BF16)]
        + [jax.ShapeDtypeStruct((N_DEV,) + a.shape, a.dtype) for a in gather],
        scratch_shapes=_comm_sems(nw),
        compiler_params=_params(("arbitrary",)),
    )(*([x2d] * ksub), meta, gain, wgu, wd, *gather)


def _ffn2_fwd_loss(h, gain, wgu, wd, gfinal, target, lp):
    n = h.shape[0]
    nq = lp // 128
    tm = _row_tile(n, 512)
    ksub = tm // 128
    n_i = n // tm

    def body(*refs):
        t_refs = refs[:ksub]
        h_ref, g_ref, wgu_ref, wd_ref, gf_ref, dh_ref, nrm_ref, gate_ref, up_ref, loss_ref, dgf_ref = refs[ksub:]
        i = pl.program_id(0)

        @pl.when(i == 0)
        def _():
            loss_ref[...] = jnp.zeros_like(loss_ref)
            dgf_ref[...] = jnp.zeros_like(dgf_ref)

        hv = h_ref[...]
        y, _, _ = _rms(hv, g_ref[...])
        nb = y.astype(BF16)
        nrm_ref[...] = nb
        hout = hv + 0.5 * _swiglu(nb, wgu_ref, wd_ref, gate_ref, up_ref)

        gf = gf_ref[...]
        loss = jnp.zeros((1, 1), F32)
        dgf = jnp.zeros((1, D_MODEL), F32)
        for k in range(ksub):
            yk, xhat, r = _rms(hout[128 * k:128 * (k + 1)], gf)
            err = jnp.where(_is_lead(i, k, ksub, nq), 0.0, yk - t_refs[k][...])
            loss = loss + 0.5 * jnp.sum(jnp.sum(err * err, axis=1, keepdims=True), axis=0,
                                        keepdims=True) * (1.0 / D_MODEL)
            dy = err * (1.0 / D_MODEL)
            dh_ref[128 * k:128 * (k + 1), :] = _rms_bwd(dy, xhat, r, gf)
            dgf = dgf + jnp.sum(dy * xhat, axis=0, keepdims=True)
        loss_ref[...] += loss
        dgf_ref[...] += dgf

    rows = pl.BlockSpec((tm, D_MODEL), lambda i: (i, 0))
    hid = pl.BlockSpec((tm, D_FF), lambda i: (i, 0))
    vec = pl.BlockSpec((1, D_MODEL), lambda i: (0, 0))
    return pl.pallas_call(
        body, name="ffn2_fwd_loss", grid=(n_i,),
        in_specs=[_token_spec(k, ksub, nq) for k in range(ksub)]
        + [rows, vec, _resident(W_GU_SHAPE), _resident(W_D_SHAPE), vec],
        out_specs=[rows, rows, hid, hid, pl.BlockSpec((1, 1), lambda i: (0, 0)), vec],
        out_shape=[jax.ShapeDtypeStruct((n, D_MODEL), F32), jax.ShapeDtypeStruct((n, D_MODEL), BF16),
                   jax.ShapeDtypeStruct((n, D_FF), BF16), jax.ShapeDtypeStruct((n, D_FF), BF16),
                   jax.ShapeDtypeStruct((1, 1), F32), jax.ShapeDtypeStruct((1, D_MODEL), F32)],
        compiler_params=_params(("arbitrary",)),
    )(*([target] * ksub), h, gain, wgu, wd, gfinal)


def _ffn_bwd_act_wd(dh_out, gate, up, wd, name, exchange=()):
    n = dh_out.shape[0]
    tm = _row_tile(n, 256)
    n_i = n // tm
    nw = len(exchange)

    def body(*refs):
        (dh_ref, gate_ref, up_ref, wd_ref), xin, (dgu_ref, dw_ref), xout, (acc_scr,), sems = \
            _split_refs(refs, 4, nw, 2, 1)
        i = pl.program_id(0)
        if nw:
            comm = _Exchange(xin, xout, sems)
            pl.when(i == 0)(comm.start)

        @pl.when(i == 0)
        def _():
            acc_scr[...] = jnp.zeros_like(acc_scr)

        dhb = (0.5 * dh_ref[...]).astype(BF16)
        for a, b in ACT_PIECES:
            da = _dot_nt(dhb, wd_ref[a:b, :])
            g = gate_ref[:, a:b].astype(F32)
            u = up_ref[:, a:b].astype(F32)
            sig = _sigmoid(g)
            silu = g * sig
            dgu_ref[:, a:b] = (da * u * (sig + silu * (1.0 - sig))).astype(BF16)
            dgu_ref[:, D_FF + a:D_FF + b] = (da * silu).astype(BF16)
            acc_scr[a:b, :] += _dot_tn((silu * u).astype(BF16), dhb)

        @pl.when(i == n_i - 1)
        def _():
            dw_ref[...] = acc_scr[...].astype(BF16)

        if nw:
            pl.when(i == n_i - 1)(comm.finish)

    rows = pl.BlockSpec((tm, D_MODEL), lambda i: (i, 0))
    hid = pl.BlockSpec((tm, D_FF), lambda i: (i, 0))
    return pl.pallas_call(
        body, name=name, grid=(n_i,),
        in_specs=[rows, hid, hid, _resident(W_D_SHAPE)] + [ANY] * nw,
        out_specs=[pl.BlockSpec((tm, 2 * D_FF), lambda i: (i, 0)), _resident(W_D_SHAPE)] + [ANY] * nw,
        out_shape=[jax.ShapeDtypeStruct((n, 2 * D_FF), BF16), jax.ShapeDtypeStruct(W_D_SHAPE, BF16)]
        + [jax.ShapeDtypeStruct(a.shape, a.dtype) for a in exchange],
        scratch_shapes=[pltpu.VMEM(W_D_SHAPE, F32)] + (_comm_sems(nw) if nw else []),
        compiler_params=_params(("arbitrary",)),
    )(dh_out, gate, up, wd, *exchange)


def _ffn_bwd_in(dh_out, h_in, gain, dgu, wgu, name, tokens=None, exchange=()):
    n = dh_out.shape[0]
    tm = _row_tile(n, 512)
    n_i = n // tm
    nw = len(exchange)
    ksub, nq = (tm // 128, tokens[2] // 128) if tokens else (1, 0)

    def body(*refs):
        h_refs = refs[:ksub]
        (meta_ref, dh_ref, g_ref, dgu_ref, wgu_ref), xin, (dhin_ref, dgain_ref), xout, _, sems = \
            _split_refs(refs[ksub:], 5, nw, 2, 0)
        i = pl.program_id(0)
        if nw:
            comm = _Exchange(xin, xout, sems)
            pl.when(i == 0)(comm.start)

        @pl.when(i == 0)
        def _():
            dgain_ref[...] = jnp.zeros_like(dgain_ref)

        dn = _dot(dgu_ref[...], wgu_ref[...])
        gain_v = g_ref[...]
        hv = _assemble_rows(i, h_refs, meta_ref, nq) if tokens else h_refs[0][...]
        _, xhat, r = _rms(hv, gain_v)
        dhin_ref[...] = dh_ref[...] + _rms_bwd(dn, xhat, r, gain_v)
        dgain_ref[...] += jnp.sum(dn * xhat, axis=0, keepdims=True)

        if nw:
            pl.when(i == n_i - 1)(comm.finish)

    rows = pl.BlockSpec((tm, D_MODEL), lambda i: (i, 0))
    hid = pl.BlockSpec((tm, D_FF), lambda i: (i, 0))
    vec = pl.BlockSpec((1, D_MODEL), lambda i: (0, 0))
    meta_spec = pl.BlockSpec((N_META, D_MODEL), lambda i: (0, 0))
    if tokens:
        h_specs, h_args, meta = [_token_spec(k, ksub, nq) for k in range(ksub)], [tokens[0]] * ksub, tokens[1]
    else:
        h_specs, h_args, meta = [rows], [h_in], jnp.zeros((N_META, D_MODEL), F32)
    return pl.pallas_call(
        body, name=name, grid=(n_i,),
        in_specs=h_specs + [meta_spec, rows, vec, pl.BlockSpec((tm, 2 * D_FF), lambda i: (i, 0)),
                            _resident((2 * D_FF, D_MODEL))] + [ANY] * nw,
        out_specs=[rows, vec] + [ANY] * nw,
        out_shape=[jax.ShapeDtypeStruct((n, D_MODEL), F32), jax.ShapeDtypeStruct((1, D_MODEL), F32)]
        + [jax.ShapeDtypeStruct(a.shape, a.dtype) for a in exchange],
        scratch_shapes=_comm_sems(nw) if nw else [],
        compiler_params=_params(("arbitrary",)),
    )(*h_args, meta, dh_out, gain, dgu, wgu.reshape(2 * D_FF, D_MODEL), *exchange)


def _ffn_bwd_wgu(nrm, dgu, name, exchange=()):
    n = nrm.shape[0]
    tm = _row_tile(n, 512)
    n_i = n // tm
    nw = len(exchange)

    def body(*refs):
        (nrm_ref, dgu_ref), xin, (dw_ref,), xout, (acc_scr,), sems = _split_refs(refs, 2, nw, 1, 1)
        i = pl.program_id(0)
        if nw:
            comm = _Exchange(xin, xout, sems)
            pl.when(i == 0)(comm.start)

        @pl.when(i == 0)
        def _():
            acc_scr[...] = jnp.zeros_like(acc_scr)

        nb = nrm_ref[...]
        for half in (0, D_FF):
            for a, b in HID_PIECES:
                acc_scr[half + a:half + b, :] += _dot_tn(dgu_ref[:, half + a:half + b], nb)

        @pl.when(i == n_i - 1)
        def _():
            dw_ref[...] = acc_scr[...].astype(BF16)

        if nw:
            pl.when(i == n_i - 1)(comm.finish)

    shape = (2 * D_FF, D_MODEL)
    res = pl.pallas_call(
        body, name=name, grid=(n_i,),
        in_specs=[pl.BlockSpec((tm, D_MODEL), lambda i: (i, 0)),
                  pl.BlockSpec((tm, 2 * D_FF), lambda i: (i, 0))] + [ANY] * nw,
        out_specs=[_resident(shape)] + [ANY] * nw,
        out_shape=[jax.ShapeDtypeStruct(shape, BF16)] + [jax.ShapeDtypeStruct(a.shape, a.dtype) for a in exchange],
        scratch_shapes=[pltpu.VMEM(shape, F32)] + (_comm_sems(nw) if nw else []),
        compiler_params=_params(("arbitrary",)),
    )(nrm, dgu, *exchange)
    return res if nw else res[0]


def _inproj_fwd(h, gain, w_in):
    n = h.shape[0]
    tm = _row_tile(n, 512)

    def body(h_ref, g_ref, w_ref, *outs):
        y, _, _ = _rms(h_ref[...], g_ref[...])
        nb = y.astype(BF16)
        for p in range(N_PIECE):
            outs[p][...] = _dot_nt(nb, w_ref[512 * p:512 * (p + 1), :]).astype(BF16)
        outs[N_PIECE][...] = _dot_nt(nb, w_ref[IN_MAIN:IN_PAD, :])

    piece = pl.BlockSpec((tm, 512), lambda i: (i, 0))
    return pl.pallas_call(
        body, name="inproj_fwd", grid=(n // tm,),
        in_specs=[pl.BlockSpec((tm, D_MODEL), lambda i: (i, 0)),
                  pl.BlockSpec((1, D_MODEL), lambda i: (0, 0)),
                  pl.BlockSpec((IN_PAD, D_MODEL), lambda i: (0, 0))],
        out_specs=[piece] * N_PIECE + [pl.BlockSpec((tm, 128), lambda i: (i, 0))],
        out_shape=[jax.ShapeDtypeStruct((n, 512), BF16)] * N_PIECE + [jax.ShapeDtypeStruct((n, 128), F32)],
        compiler_params=_params(("parallel",)),
    )(h, gain, w_in)


def _inproj_bwd(dpieces, dfg, dh_out, h_in, gain, w_in):
    n = h_in.shape[0]
    tm = _row_tile(n, 512)
    n_i = n // tm

    def body(*refs):
        dp_refs = refs[:N_PIECE]
        dfg_ref, dh_ref, h_ref, g_ref, w_ref, dhin_ref, dw_ref, dgain_ref, acc_scr = refs[N_PIECE:]
        i = pl.program_id(0)

        @pl.when(i == 0)
        def _():
            acc_scr[...] = jnp.zeros_like(acc_scr)
            dgain_ref[...] = jnp.zeros_like(dgain_ref)

        gain_v = g_ref[...]
        y, xhat, r = _rms(h_ref[...], gain_v)
        nb = y.astype(BF16)
        dn = jnp.zeros((tm, D_MODEL), F32)
        for p in range(N_PIECE + 1):
            lo, hi = (512 * p, 512 * (p + 1)) if p < N_PIECE else (IN_MAIN, IN_PAD)
            dp = (dp_refs[p][...] if p < N_PIECE else dfg_ref[...]).astype(BF16)
            dn = dn + _dot(dp, w_ref[lo:hi, :])
            acc_scr[lo:hi, :] += _dot_tn(dp, nb)
        dhin_ref[...] = dh_ref[...] + _rms_bwd(dn, xhat, r, gain_v)
        dgain_ref[...] += jnp.sum(dn * xhat, axis=0, keepdims=True)

        @pl.when(i == n_i - 1)
        def _():
            dw_ref[...] = acc_scr[...].astype(BF16)

    piece = pl.BlockSpec((tm, 512), lambda i: (i, 0))
    rows = pl.BlockSpec((tm, D_MODEL), lambda i: (i, 0))
    vec = pl.BlockSpec((1, D_MODEL), lambda i: (0, 0))
    wspec = pl.BlockSpec((IN_PAD, D_MODEL), lambda i: (0, 0))
    return pl.pallas_call(
        body, name="inproj_bwd", grid=(n_i,),
        in_specs=[piece] * N_PIECE + [pl.BlockSpec((tm, 128), lambda i: (i, 0)), rows, rows, vec, wspec],
        out_specs=[rows, wspec, vec],
        out_shape=[jax.ShapeDtypeStruct((n, D_MODEL), F32),
                   jax.ShapeDtypeStruct((IN_PAD, D_MODEL), BF16),
                   jax.ShapeDtypeStruct((1, D_MODEL), F32)],
        scratch_shapes=[pltpu.VMEM((IN_PAD, D_MODEL), F32)],
        compiler_params=_params(("arbitrary",)),
    )(*dpieces, dfg, dh_out, h_in, gain, w_in)


def _outproj_bwd(dh, zc, za, w_out):
    n = dh.shape[0]
    tm = _row_tile(n, 512)
    n_i = n // tm

    def body(dh_ref, zc_ref, za_ref, w_ref, dzc_ref, dza_ref, dw_ref, acc_scr):
        i = pl.program_id(0)

        @pl.when(i == 0)
        def _():
            acc_scr[...] = jnp.zeros_like(acc_scr)

        dhb = dh_ref[...].astype(BF16)
        dzc_ref[...] = _dot_nt(dhb, w_ref[0:CONV_DIM, :]).astype(BF16)
        dza_ref[...] = _dot_nt(dhb, w_ref[CONV_DIM:, :]).astype(BF16)
        acc_scr[0:CONV_DIM, :] += _dot_tn(zc_ref[...], dhb)
        acc_scr[CONV_DIM:, :] += _dot_tn(za_ref[...], dhb)

        @pl.when(i == n_i - 1)
        def _():
            dw_ref[...] = acc_scr[...].astype(BF16)

    half = pl.BlockSpec((tm, 512), lambda i: (i, 0))
    wspec = pl.BlockSpec((D_MODEL, D_MODEL), lambda i: (0, 0))
    return pl.pallas_call(
        body, name="outproj_bwd", grid=(n_i,),
        in_specs=[pl.BlockSpec((tm, D_MODEL), lambda i: (i, 0)), half, half, wspec],
        out_specs=[half, half, wspec],
        out_shape=[jax.ShapeDtypeStruct((n, 512), BF16), jax.ShapeDtypeStruct((n, 512), BF16),
                   jax.ShapeDtypeStruct((D_MODEL, D_MODEL), BF16)],
        scratch_shapes=[pltpu.VMEM((D_MODEL, D_MODEL), F32)],
        compiler_params=_params(("arbitrary",)),
    )(dh, zc, za, w_out)


def _group_matrix():
    r = lax.broadcasted_iota(jnp.int32, (128, 128), 0) // HEAD_DIM
    c = lax.broadcasted_iota(jnp.int32, (128, 128), 1) // HEAD_DIM
    return jnp.where(r == c, 1.0 / HEAD_DIM, 0.0).astype(BF16)


def _group_mean(x, gmat):
    hi = x.astype(BF16)
    lo = (x - hi.astype(F32)).astype(BF16)
    return _dot(hi, gmat) + _dot(lo, gmat)


def _shift_rows(x, s):
    rows = x.shape[0]
    t = lax.broadcasted_iota(jnp.int32, x.shape, 0)
    rolled = pltpu.roll(x, s % rows, 0)
    keep = (t >= s) if s > 0 else (t < rows + s)
    return jnp.where(keep, rolled, 0.0)


def _conv_parts(bg_ref, cg_ref, hc_ref, w_ref):
    bg = bg_ref[...].astype(F32)
    cg = cg_ref[...].astype(F32)
    hc = hc_ref[...].astype(F32)
    u = cg * hc
    u1 = _shift_rows(u, 1)
    u2 = _shift_rows(u, 2)
    conv = w_ref[2:3, :] * u + w_ref[1:2, :] * u1 + w_ref[0:1, :] * u2
    return bg, cg, hc, u, u1, u2, conv


def _conv_fwd(bg, cg, hc, conv_w, gain, gmat, lp):
    n = bg.shape[0]
    nb = n // lp

    def body(bg_ref, cg_ref, hc_ref, w_ref, g_ref, gm_ref, z_ref):
        bgv, _, _, _, _, _, conv = _conv_parts(bg_ref, cg_ref, hc_ref, w_ref)
        yc = bgv * conv
        r = lax.rsqrt(_group_mean(yc * yc, gm_ref[...]) + EPS)
        z_ref[...] = (yc * r * g_ref[...]).astype(BF16)

    blk = pl.BlockSpec((lp, 128), lambda c, b: (b, c))
    return pl.pallas_call(
        body, name="conv_fwd", grid=(CONV_DIM // 128, nb),
        in_specs=[blk, blk, blk, pl.BlockSpec((3, 128), lambda c, b: (0, c)),
                  pl.BlockSpec((1, 128), lambda c, b: (0, c)), pl.BlockSpec((128, 128), lambda c, b: (0, 0))],
        out_specs=blk,
        out_shape=jax.ShapeDtypeStruct((n, CONV_DIM), BF16),
        compiler_params=_params(("parallel", "parallel")),
    )(bg, cg, hc, conv_w, gain, gmat)


def _conv_bwd(dz, bg, cg, hc, conv_w, gain, gmat, lp):
    n = bg.shape[0]
    nb = n // lp

    def body(dz_ref, bg_ref, cg_ref, hc_ref, w_ref, g_ref, gm_ref,
             dbg_ref, dcg_ref, dhc_ref, dw_ref, dgain_ref):
        b = pl.program_id(1)

        @pl.when(b == 0)
        def _():
            dw_ref[...] = jnp.zeros_like(dw_ref)
            dgain_ref[...] = jnp.zeros_like(dgain_ref)

        bgv, cgv, hcv, u, u1, u2, conv = _conv_parts(bg_ref, cg_ref, hc_ref, w_ref)
        gm = gm_ref[...]
        yc = bgv * conv
        r = lax.rsqrt(_group_mean(yc * yc, gm) + EPS)
        yhat = yc * r
        dzv = dz_ref[...].astype(F32)
        dyhat = dzv * g_ref[...]
        dgain_ref[...] += jnp.sum(dzv * yhat, axis=0, keepdims=True)
        dyc = r * (dyhat - yhat * _group_mean(dyhat * yhat, gm))
        dbg_ref[...] = (dyc * conv).astype(BF16)
        dconv = dyc * bgv
        du = (w_ref[2:3, :] * dconv + w_ref[1:2, :] * _shift_rows(dconv, -1)
              + w_ref[0:1, :] * _shift_rows(dconv, -2))
        dcg_ref[...] = (du * hcv).astype(BF16)
        dhc_ref[...] = (du * cgv).astype(BF16)
        dw_ref[0:1, :] += jnp.sum(dconv * u2, axis=0, keepdims=True)
        dw_ref[1:2, :] += jnp.sum(dconv * u1, axis=0, keepdims=True)
        dw_ref[2:3, :] += jnp.sum(dconv * u, axis=0, keepdims=True)

    blk = pl.BlockSpec((lp, 128), lambda c, b: (b, c))
    wspec = pl.BlockSpec((3, 128), lambda c, b: (0, c))
    gspec = pl.BlockSpec((1, 128), lambda c, b: (0, c))
    return pl.pallas_call(
        body, name="conv_bwd", grid=(CONV_DIM // 128, nb),
        in_specs=[blk, blk, blk, blk, wspec, gspec, pl.BlockSpec((128, 128), lambda c, b: (0, 0))],
        out_specs=[blk, blk, blk, wspec, gspec],
        out_shape=[jax.ShapeDtypeStruct((n, CONV_DIM), BF16)] * 3
        + [jax.ShapeDtypeStruct((3, CONV_DIM), F32), jax.ShapeDtypeStruct((1, CONV_DIM), F32)],
        compiler_params=_params(("parallel", "arbitrary")),
    )(dz, bg, cg, hc, conv_w, gain, gmat)


KEY_MASKED = 1e30
ONE_LANE = 24


def _scan_steps(rows):
    s, out = 1, []
    while s < rows:
        out.append(s)
        s *= 2
    return out


def _fgate_fwd(fg, b_f, lp):
    n = fg.shape[0]
    nb = n // lp

    def body(fg_ref, b_ref, ka_ref, qa_ref):
        x = fg_ref[...] + b_ref[...]
        logf = jnp.minimum(x, 0.0) - jnp.log(1.0 + jnp.exp(-jnp.abs(x)))
        t = lax.broadcasted_iota(jnp.int32, (lp, 128), 0)
        lane = lax.broadcasted_iota(jnp.int32, (lp, 128), 1)
        f = jnp.where((t >= PAD) & (lane < N_HEADS), logf, 0.0)
        for s in _scan_steps(lp):
            f = f + _shift_rows(f, s)
        hi = f.astype(BF16).astype(F32)
        rest = f - hi
        mid = rest.astype(BF16).astype(F32)
        lo = (rest - mid).astype(BF16).astype(F32)
        ones = jnp.where((lane >= ONE_LANE) & (lane < ONE_LANE + 3), 1.0, 0.0)
        hi_key = jnp.where((t < PAD) & (lane < N_HEADS), KEY_MASKED, hi)
        ka_ref[...] = (hi_key + pltpu.roll(mid, 8, 1) + pltpu.roll(lo, 16, 1) + ones).astype(BF16)
        for h in range(N_HEADS):
            minus = jnp.where((lane == h) | (lane == 8 + h) | (lane == 16 + h), -1.0, 0.0)
            terms = (jnp.where(lane == ONE_LANE, pltpu.roll(hi, ONE_LANE - h, 1), 0.0)
                     + jnp.where(lane == ONE_LANE + 1, pltpu.roll(mid, ONE_LANE + 1 - h, 1), 0.0)
                     + jnp.where(lane == ONE_LANE + 2, pltpu.roll(lo, ONE_LANE + 2 - h, 1), 0.0))
            qa_ref[:, 128 * h:128 * (h + 1)] = (minus + terms).astype(BF16)

    return pl.pallas_call(
        body, name="fgate_fwd", grid=(nb,),
        in_specs=[pl.BlockSpec((lp, 128), lambda b: (b, 0)), pl.BlockSpec((1, 128), lambda b: (0, 0))],
        out_specs=[pl.BlockSpec((lp, 128), lambda b: (b, 0)), pl.BlockSpec((lp, N_HEADS * 128), lambda b: (b, 0))],
        out_shape=[jax.ShapeDtypeStruct((n, 128), BF16), jax.ShapeDtypeStruct((n, N_HEADS * 128), BF16)],
        compiler_params=_params(("parallel",)),
    )(fg, b_f)


def _fgate_bwd(dka, dfr, fg, b_f, lp):
    n = fg.shape[0]
    nb = n // lp

    def body(dka_ref, dfr_ref, fg_ref, b_ref, dfg_ref, db_ref):
        b = pl.program_id(0)

        @pl.when(b == 0)
        def _():
            db_ref[...] = jnp.zeros_like(db_ref)

        wide = jnp.concatenate([dfr_ref[0], jnp.zeros((128 - N_HEADS, lp), F32)], axis=0)
        t = lax.broadcasted_iota(jnp.int32, (lp, 128), 0)
        lane = lax.broadcasted_iota(jnp.int32, (lp, 128), 1)
        d = jnp.where(lane < N_HEADS, dka_ref[...], 0.0) + wide.T
        for s in _scan_steps(lp):
            d = d + _shift_rows(d, -s)
        x = fg_ref[...] + b_ref[...]
        dx = jnp.where((t >= PAD) & (lane < N_HEADS), d * _sigmoid(-x), 0.0)
        dfg_ref[...] = dx
        db_ref[...] += jnp.sum(dx, axis=0, keepdims=True)

    return pl.pallas_call(
        body, name="fgate_bwd", grid=(nb,),
        in_specs=[pl.BlockSpec((lp, 128), lambda b: (b, 0)), pl.BlockSpec((1, N_HEADS, lp), lambda b: (b, 0, 0)),
                  pl.BlockSpec((lp, 128), lambda b: (b, 0)), pl.BlockSpec((1, 128), lambda b: (0, 0))],
        out_specs=[pl.BlockSpec((lp, 128), lambda b: (b, 0)), pl.BlockSpec((1, 128), lambda b: (0, 0))],
        out_shape=[jax.ShapeDtypeStruct((n, 128), F32), jax.ShapeDtypeStruct((1, 128), F32)],
        compiler_params=_params(("arbitrary",)),
    )(dka, dfr, fg, b_f)


def _head_masks():
    lane = lax.broadcasted_iota(jnp.int32, (1, 128), 1)
    return lane < HEAD_DIM


def _stack_heads(x2, first):
    zero = jnp.zeros_like(x2)
    return jnp.concatenate([jnp.where(first, x2, zero), jnp.where(first, zero, x2)], axis=0)


def _stack_heads_lanes(xt):
    r = lax.broadcasted_iota(jnp.int32, xt.shape, 0)
    zero = jnp.zeros_like(xt)
    return jnp.concatenate([jnp.where(r < HEAD_DIM, xt, zero), jnp.where(r < HEAD_DIM, zero, xt)], axis=1)


def _pair_cols(col0, col1, first):
    return jnp.where(first, col0, col1)


def _pair_rows(row0, row1):
    r = lax.broadcasted_iota(jnp.int32, (128, TQ), 0)
    return jnp.where(r < HEAD_DIM, row0, row1)


def _query_side(q_ref, qa_ref, p, first):
    q2 = q_ref[:, 128 * p:128 * (p + 1)] * 0.125
    zero = jnp.zeros_like(q2)
    top = jnp.concatenate([jnp.where(first, q2, zero), qa_ref[:, 128 * (2 * p):128 * (2 * p + 1)]], axis=1)
    bot = jnp.concatenate([jnp.where(first, zero, q2), qa_ref[:, 128 * (2 * p + 1):128 * (2 * p + 2)]], axis=1)
    return jnp.concatenate([top, bot], axis=0)


def _padded_keys(lp, chunk):
    return ((lp + chunk - 1) // chunk) * chunk


def _chunk_mask(i, c, tk, chunk):
    r = lax.broadcasted_iota(jnp.int32, (tk, 2 * TQ), 0)
    col = lax.broadcasted_iota(jnp.int32, (tk, 2 * TQ), 1)
    return (c * chunk + r) <= (i * TQ + (col & (TQ - 1)))


def _causal_sweep(i, step, init, chunk):
    per = chunk // TQ
    last = i // per
    carry = lax.fori_loop(0, last, lambda c, carry: step(c, carry, False, chunk), init)
    tails = [lambda carry, r=r: step(last, carry, True, TQ * (r + 1)) for r in range(per)]
    return lax.switch(i % per, tails, carry)


def _transpose_bf16(x):
    return x.astype(F32).T.astype(BF16)


def _attn_fwd(q, qa, k, v, ka, gain, zc, w_out, h, lp):
    n = q.shape[0]
    nb = n // lp
    nq = lp // TQ
    lpp = _padded_keys(lp, TK_FWD)

    def body(q_ref, qa_ref, k_ref, v_ref, ka_ref, g_ref, zc_ref, w_ref, h_ref,
             z_ref, o_ref, lse_ref, hout_ref, kx_scr, vt_scr):
        i = pl.program_id(1)
        first = _head_masks()

        @pl.when(i == 0)
        def _():
            if lpp > lp:
                kx_scr[lp:lpp, :] = jnp.zeros((lpp - lp, 2 * ATTN_DIM), BF16)
                vt_scr[:, lp:lpp] = jnp.zeros((ATTN_DIM, lpp - lp), BF16)
            for p in range(N_PAIRS):
                kx_scr[0:lp, 256 * p:256 * p + 128] = k_ref[:, 128 * p:128 * (p + 1)]
                kx_scr[0:lp, 256 * p + 128:256 * (p + 1)] = ka_ref[...]
            vt_scr[:, 0:lp] = _transpose_bf16(v_ref[...])

        rhs_t = [_transpose_bf16(_query_side(q_ref, qa_ref, p, first)) for p in range(N_PAIRS)]

        def step(c, carry, masked, tk):
            koff = pl.multiple_of(c * TK_FWD, TK_FWD)
            valid = _chunk_mask(i, c, tk, TK_FWD) if masked else None
            new = []
            sts = [_dot(kx_scr[pl.ds(koff, tk), 256 * p:256 * (p + 1)], rhs_t[p]) for p in range(N_PAIRS)]
            for p in range(N_PAIRS):
                m, l, acc = carry[p]
                st = sts[p]
                if masked:
                    st = jnp.where(valid, st, NEG)
                m_new = jnp.maximum(m, jnp.max(st, axis=0, keepdims=True))
                pt = jnp.exp(st - m_new)
                alpha = jnp.exp(m - m_new)
                l = alpha * l + jnp.sum(pt, axis=0, keepdims=True)
                pb = pt.astype(BF16)
                vt = _stack_heads_lanes(vt_scr[128 * p:128 * (p + 1), pl.ds(koff, tk)])
                pv = _dot(vt, jnp.concatenate([pb[:, 0:TQ], pb[:, TQ:]], axis=0))
                acc = acc * _pair_rows(alpha[:, 0:TQ], alpha[:, TQ:]) + pv
                new.append((m_new, l, acc))
            return tuple(new)

        init = tuple((jnp.full((1, 2 * TQ), NEG, F32), jnp.zeros((1, 2 * TQ), F32), jnp.zeros((128, TQ), F32))
                     for _ in range(N_PAIRS))
        final = _causal_sweep(i, step, init, TK_FWD)

        row = lax.broadcasted_iota(jnp.int32, (TQ, 128), 0)
        real = (i * TQ + row) >= PAD
        zs = [zc_ref[...]]
        for p in range(N_PAIRS):
            m, l, acc = final[p]
            inv = 1.0 / l
            ot = acc * _pair_rows(inv[:, 0:TQ], inv[:, TQ:])
            sq = ot * ot
            r0 = lax.rsqrt(jnp.sum(sq[0:HEAD_DIM], axis=0, keepdims=True) * (1.0 / HEAD_DIM) + EPS)
            r1 = lax.rsqrt(jnp.sum(sq[HEAD_DIM:], axis=0, keepdims=True) * (1.0 / HEAD_DIM) + EPS)
            cols = slice(128 * p, 128 * (p + 1))
            o_ref[:, cols] = jnp.where(real, ot.T, 0.0).astype(BF16)
            z = (jnp.where(real, (ot * _pair_rows(r0, r1)).T, 0.0) * g_ref[:, cols]).astype(BF16)
            z_ref[:, cols] = z
            zs.append(z)
            lse = m + jnp.log(l)
            lse_ref[0, 2 * p:2 * p + 1, :] = lse[:, 0:TQ]
            lse_ref[0, 2 * p + 1:2 * p + 2, :] = lse[:, TQ:]
        hout_ref[...] = h_ref[...] + _dot(jnp.concatenate(zs, axis=1), w_ref[...])

    qblk = pl.BlockSpec((TQ, ATTN_DIM), lambda b, i: (b * nq + i, 0))
    qablk = pl.BlockSpec((TQ, N_HEADS * 128), lambda b, i: (b * nq + i, 0))
    seq = pl.BlockSpec((lp, ATTN_DIM), lambda b, i: (b, 0))
    rowblk = pl.BlockSpec((1, N_HEADS, TQ), lambda b, i: (b, 0, i))
    hblk = pl.BlockSpec((TQ, D_MODEL), lambda b, i: (b * nq + i, 0))
    return pl.pallas_call(
        body, name="attn_fwd", grid=(nb, nq),
        in_specs=[qblk, qablk, seq, seq, pl.BlockSpec((lp, 128), lambda b, i: (b, 0)),
                  pl.BlockSpec((1, ATTN_DIM), lambda b, i: (0, 0)), qblk,
                  pl.BlockSpec((D_MODEL, D_MODEL), lambda b, i: (0, 0)), hblk],
        out_specs=[qblk, qblk, rowblk, hblk],
        out_shape=[jax.ShapeDtypeStruct((n, ATTN_DIM), BF16), jax.ShapeDtypeStruct((n, ATTN_DIM), BF16),
                   jax.ShapeDtypeStruct((nb, N_HEADS, lp), F32), jax.ShapeDtypeStruct((n, D_MODEL), F32)],
        scratch_shapes=[pltpu.VMEM((lpp, 2 * ATTN_DIM), BF16), pltpu.VMEM((ATTN_DIM, lpp), BF16)],
        compiler_params=_params(("parallel", "arbitrary")),
    )(q, qa, k, v, ka, gain, zc, w_out, h)


def _attn_bwd(dz, q, qa, k, v, ka, o, lse, gain, lp, exchange=()):
    n = q.shape[0]
    nb = n // lp
    nq = lp // TQ
    lpp = _padded_keys(lp, TK)
    nw = len(exchange)

    def body(*refs):
        ((dz_ref, q_ref, qa_ref, k_ref, v_ref, ka_ref, o_ref, lse_ref, g_ref), xin,
         (dq_ref, dk_ref, dv_ref, dka_ref, dfr_ref, dgain_ref), xout,
         (kx_scr, vx_scr, kt_scr, dkx_scr, dvx_scr), sems) = _split_refs(refs, 9, nw, 6, 5)
        b = pl.program_id(0)
        i = pl.program_id(1)
        first = _head_masks()
        if nw:
            comm = _Exchange(xin, xout, sems)
            pl.when((b == 0) & (i == 0))(comm.start)

        @pl.when((b == 0) & (i == 0))
        def _():
            dgain_ref[...] = jnp.zeros_like(dgain_ref)

        @pl.when(i == 0)
        def _():
            if lpp > lp:
                kx_scr[lp:lpp, :] = jnp.zeros((lpp - lp, 2 * ATTN_DIM), BF16)
                vx_scr[lp:lpp, :] = jnp.zeros((lpp - lp, ATTN_DIM), BF16)
                kt_scr[:, lp:lpp] = jnp.zeros((ATTN_DIM, lpp - lp), BF16)
            for p in range(N_PAIRS):
                kx_scr[0:lp, 256 * p:256 * p + 128] = k_ref[:, 128 * p:128 * (p + 1)]
                kx_scr[0:lp, 256 * p + 128:256 * (p + 1)] = ka_ref[...]
            vx_scr[0:lp, :] = v_ref[...]
            kt_scr[:, 0:lp] = _transpose_bf16(k_ref[...])
            dkx_scr[...] = jnp.zeros_like(dkx_scr)
            dvx_scr[...] = jnp.zeros_like(dvx_scr)

        rhs, rhs_t, lses, dos, dos_t, deltas = [], [], [], [], [], []
        for p in range(N_PAIRS):
            cols = slice(128 * p, 128 * (p + 1))
            side = _query_side(q_ref, qa_ref, p, first)
            rhs.append(side)
            rhs_t.append(_transpose_bf16(side))
            lses.append(jnp.concatenate([lse_ref[0, 2 * p:2 * p + 1, :], lse_ref[0, 2 * p + 1:2 * p + 2, :]], axis=1))
            ov = o_ref[:, cols].astype(F32)
            dzv = dz_ref[:, cols].astype(F32)
            gv = g_ref[:, cols]
            sq = ov * ov
            ms0 = jnp.sum(jnp.where(first, sq, 0.0), axis=1, keepdims=True) * (1.0 / HEAD_DIM)
            ms1 = jnp.sum(jnp.where(first, 0.0, sq), axis=1, keepdims=True) * (1.0 / HEAD_DIM)
            r = _pair_cols(lax.rsqrt(ms0 + EPS), lax.rsqrt(ms1 + EPS), first)
            ohat = ov * r
            dyhat = dzv * gv
            dgain_ref[:, cols] += jnp.sum(dzv * ohat, axis=0, keepdims=True)
            pr = dyhat * ohat
            mean0 = jnp.sum(jnp.where(first, pr, 0.0), axis=1, keepdims=True) * (1.0 / HEAD_DIM)
            mean1 = jnp.sum(jnp.where(first, 0.0, pr), axis=1, keepdims=True) * (1.0 / HEAD_DIM)
            do = r * (dyhat - ohat * _pair_cols(mean0, mean1, first))
            ddt = (do * ov).T
            deltas.append(jnp.concatenate([jnp.sum(ddt[0:HEAD_DIM], axis=0, keepdims=True),
                                           jnp.sum(ddt[HEAD_DIM:], axis=0, keepdims=True)], axis=1))
            do_st = _stack_heads(do.astype(BF16), first)
            dos.append(do_st)
            dos_t.append(_transpose_bf16(do_st))

        def step(c, carry, masked, tk):
            koff = pl.multiple_of(c * TK, TK)
            valid = _chunk_mask(i, c, tk, TK) if masked else None
            new = []
            sts = [_dot(kx_scr[pl.ds(koff, tk), 256 * p:256 * (p + 1)], rhs_t[p]) for p in range(N_PAIRS)]
            dpts = [_dot(vx_scr[pl.ds(koff, tk), 128 * p:128 * (p + 1)], dos_t[p]) for p in range(N_PAIRS)]
            for p in range(N_PAIRS):
                dqt, dfq = carry[p]
                ext = slice(256 * p, 256 * (p + 1))
                cols = slice(128 * p, 128 * (p + 1))
                st = sts[p]
                if masked:
                    st = jnp.where(valid, st, NEG)
                pt = jnp.exp(st - lses[p])
                dst = pt * (dpts[p] - deltas[p])
                dsb = dst.astype(BF16)
                dfq = dfq + jnp.sum(dsb.astype(F32), axis=0, keepdims=True)
                dkx_scr[pl.ds(koff, tk), ext] += _dot(dsb, rhs[p])
                dvx_scr[pl.ds(koff, tk), cols] += _dot(pt.astype(BF16), dos[p])
                kt = _stack_heads_lanes(kt_scr[cols, pl.ds(koff, tk)])
                dqt = dqt + _dot(kt, jnp.concatenate([dsb[:, 0:TQ], dsb[:, TQ:]], axis=0))
                new.append((dqt, dfq))
            return tuple(new)

        init = tuple((jnp.zeros((128, TQ), F32), jnp.zeros((1, 2 * TQ), F32)) for _ in range(N_PAIRS))
        final = _causal_sweep(i, step, init, TK)

        for p in range(N_PAIRS):
            dqt, dfq = final[p]
            dq_ref[:, 128 * p:128 * (p + 1)] = (dqt.T * 0.125).astype(BF16)
            dfr_ref[0, 2 * p:2 * p + 1, :] = dfq[:, 0:TQ]
            dfr_ref[0, 2 * p + 1:2 * p + 2, :] = dfq[:, TQ:]

        @pl.when(i == nq - 1)
        def _():
            dka = jnp.zeros((lp, 128), F32)
            for p in range(N_PAIRS):
                dk_ref[:, 128 * p:128 * (p + 1)] = dkx_scr[0:lp, 256 * p:256 * p + 128].astype(BF16)
                dka = dka + dkx_scr[0:lp, 256 * p + 128:256 * (p + 1)]
            dka_ref[...] = dka
            dv_ref[...] = dvx_scr[0:lp, :].astype(BF16)

        if nw:
            pl.when((b == nb - 1) & (i == nq - 1))(comm.finish)

    qblk = pl.BlockSpec((TQ, ATTN_DIM), lambda b, i: (b * nq + i, 0))
    qablk = pl.BlockSpec((TQ, N_HEADS * 128), lambda b, i: (b * nq + i, 0))
    seq = pl.BlockSpec((lp, ATTN_DIM), lambda b, i: (b, 0))
    kaseq = pl.BlockSpec((lp, 128), lambda b, i: (b, 0))
    rowblk = pl.BlockSpec((1, N_HEADS, TQ), lambda b, i: (b, 0, i))
    gspec = pl.BlockSpec((1, ATTN_DIM), lambda b, i: (0, 0))
    return pl.pallas_call(
        body, name="attn_bwd", grid=(nb, nq),
        in_specs=[qblk, qblk, qablk, seq, seq, kaseq, qblk, rowblk, gspec] + [ANY] * nw,
        out_specs=[qblk, seq, seq, kaseq, rowblk, gspec] + [ANY] * nw,
        out_shape=[jax.ShapeDtypeStruct((n, ATTN_DIM), BF16), jax.ShapeDtypeStruct((n, ATTN_DIM), BF16),
                   jax.ShapeDtypeStruct((n, ATTN_DIM), BF16), jax.ShapeDtypeStruct((n, 128), F32),
                   jax.ShapeDtypeStruct((nb, N_HEADS, lp), F32), jax.ShapeDtypeStruct((1, ATTN_DIM), F32)]
        + [jax.ShapeDtypeStruct(a.shape, a.dtype) for a in exchange],
        scratch_shapes=[pltpu.VMEM((lpp, 2 * ATTN_DIM), BF16), pltpu.VMEM((lpp, ATTN_DIM), BF16),
                        pltpu.VMEM((ATTN_DIM, lpp), BF16), pltpu.VMEM((lpp, 2 * ATTN_DIM), F32),
                        pltpu.VMEM((lpp, ATTN_DIM), F32)] + (_comm_sems(nw) if nw else []),
        compiler_params=_params(("arbitrary", "arbitrary")),
    )(dz, q, qa, k, v, ka, o, lse, gain, *exchange)


def _adamw(parts, w, m, v, name):
    s_parts, r, c = parts.shape
    tr = r
    for t in (256, 128, 64, 32, 16):
        if r % t == 0 and r > t:
            tr = t
            break

    def body(p_ref, w_ref, m_ref, v_ref, g_ref, d_ref, nm_ref, nv_ref):
        g = p_ref[0].astype(F32)
        for s in range(1, s_parts):
            g = g + p_ref[s].astype(F32)
        nm = ADAM_B1 * m_ref[...] + (1.0 - ADAM_B1) * g
        nv = ADAM_B2 * v_ref[...] + (1.0 - ADAM_B2) * (g * g)
        m_hat = nm / (1.0 - ADAM_B1 ** ADAM_STEP)
        v_hat = nv / (1.0 - ADAM_B2 ** ADAM_STEP)
        g_ref[...] = g
        d_ref[...] = -ADAM_LR * (m_hat / (jnp.sqrt(v_hat) + ADAM_EPS) + ADAM_WD * w_ref[...])
        nm_ref[...] = nm
        nv_ref[...] = nv

    blk = pl.BlockSpec((tr, c), lambda i: (i, 0))
    return pl.pallas_call(
        body, name=name, grid=(r // tr,),
        in_specs=[pl.BlockSpec((s_parts, tr, c), lambda i: (0, i, 0)), blk, blk, blk],
        out_specs=[blk] * 4,
        out_shape=[jax.ShapeDtypeStruct((r, c), F32)] * 4,
        compiler_params=_params(("parallel",)),
    )(parts, w, m, v)


def _sum_parts(parts, name):
    s_parts, r, c = parts.shape

    def body(p_ref, out_ref):
        acc = p_ref[0]
        for s in range(1, s_parts):
            acc = acc + p_ref[s]
        out_ref[...] = acc

    return pl.pallas_call(
        body, name=name, out_shape=jax.ShapeDtypeStruct((r, c), F32),
        in_specs=[pl.BlockSpec(memory_space=pltpu.VMEM)], out_specs=pl.BlockSpec(memory_space=pltpu.VMEM),
    )(parts)


SMALL_ROWS = 184
LOSS_ROW = 181


def _pack_small(d_gains, d_gc, d_ga, d_bf, d_conv, d_meta, loss_part):
    rows = [g.reshape(8, 128) for g in d_gains]
    rows += [d_gc.reshape(4, 128), d_ga.reshape(4, 128), d_bf.reshape(1, 128)]
    rows += [d_conv.reshape(12, 128), d_meta.reshape(128, 128), jnp.pad(loss_part, ((0, 0), (0, 127)))]
    packed = jnp.concatenate(rows, axis=0)
    return jnp.pad(packed, ((0, SMALL_ROWS - packed.shape[0]), (0, 0)))


def kernel(x, meta_tokens, ffn1_norm, ffn1_w_gu, ffn1_w_down, mix_norm, w_in, conv_w, b_f, out_norm_conv, out_norm_attn, w_out, ffn2_norm, ffn2_w_gu, ffn2_w_down, final_norm, loss_target, m_meta_tokens, m_ffn1_norm, m_ffn1_w_gu, m_ffn1_w_down, m_mix_norm, m_w_in, m_conv_w, m_b_f, m_out_norm_conv, m_out_norm_attn, m_w_out, m_ffn2_norm, m_ffn2_w_gu, m_ffn2_w_down, m_final_norm, v_meta_tokens, v_ffn1_norm, v_ffn1_w_gu, v_ffn1_w_down, v_mix_norm, v_w_in, v_conv_w, v_b_f, v_out_norm_conv, v_out_norm_attn, v_w_out, v_ffn2_norm, v_ffn2_w_gu, v_ffn2_w_down, v_final_norm):
    nb, seq, _ = x.shape
    lp = PAD + N_META + seq
    me = 4 * lax.axis_index("x") + 2 * lax.axis_index("y") + lax.axis_index("c")
    shard_gu = D_FF // 4
    shard_d = D_FF // N_DEV

    small_in = jnp.concatenate(
        [meta_tokens, jnp.pad(conv_w[0], ((0, 0), (0, 128 - conv_w.shape[2]))), jnp.zeros((5, 128), F32)], axis=0)
    wgu1_8, wd1_8, small_8 = _all_gather(
        [ffn1_w_gu[0].T.astype(BF16), ffn1_w_down[0].astype(BF16), small_in], "gather_ffn1")
    meta_full = small_8[:, 0:N_META, :].transpose(1, 0, 2).reshape(N_META, D_MODEL)
    conv_full = small_8[:, N_META:N_META + 3, 0:CONV_DIM // N_DEV].transpose(1, 0, 2).reshape(3, CONV_DIM)
    wgu1 = wgu1_8.reshape(W_GU_SHAPE)
    wd1 = wd1_8.reshape(W_D_SHAPE)
    b_f_row = jnp.pad(b_f, ((0, 0), (0, 128 - N_HEADS)))
    gmat = _group_matrix()

    x2d = x.reshape(nb * seq, D_MODEL)
    later = [w_in[0].T.astype(BF16), w_out[0].astype(BF16), ffn2_w_gu[0].T.astype(BF16), ffn2_w_down[0].astype(BF16)]
    h1, n1, gate1, up1, win_8, wout_8, wgu2_8, wd2_8 = _ffn1_fwd(x2d, meta_full, lp, ffn1_norm, wgu1, wd1, later)
    wgu2 = wgu2_8.reshape(W_GU_SHAPE)
    wd2 = wd2_8.reshape(W_D_SHAPE)
    w_in_full = jnp.pad(win_8.reshape(IN_DIM, D_MODEL), ((0, IN_PAD - IN_DIM), (0, 0)))
    w_out_full = wout_8.reshape(D_MODEL, D_MODEL)

    bg, cg, hc, q, k, v, fg = _inproj_fwd(h1, mix_norm, w_in_full)
    zc = _conv_fwd(bg, cg, hc, conv_full, out_norm_conv, gmat, lp)
    ka, qa = _fgate_fwd(fg, b_f_row, lp)
    za, o, lse, h2 = _attn_fwd(q, qa, k, v, ka, out_norm_attn, zc, w_out_full, h1, lp)
    dh3, n3, gate2, up2, loss_part, d_final = _ffn2_fwd_loss(
        h2, ffn2_norm, wgu2, wd2, final_norm.reshape(1, D_MODEL), loss_target.reshape(nb * seq, D_MODEL), lp)

    dgu2, dwd2 = _ffn_bwd_act_wd(dh3, gate2, up2, wd2, "ffn2_bwd_act")
    dh2, d_ffn2 = _ffn_bwd_in(dh3, h2, ffn2_norm, dgu2, wgu2, "ffn2_bwd_in")
    dwgu2 = _ffn_bwd_wgu(n3, dgu2, "ffn2_bwd_wgu")
    dzc, dza, dwout = _outproj_bwd(dh2, zc, za, w_out_full)
    send_a = [dwgu2.reshape(N_DEV, shard_gu, D_MODEL), dwd2.reshape(N_DEV, shard_d, D_MODEL),
              dwout.reshape(N_DEV, D_MODEL // N_DEV, D_MODEL)]
    dq, dk, dv, dka, dfr, d_ga, p_wgu2, p_wd2, p_wout = _attn_bwd(
        dza, q, qa, k, v, ka, o, lse, out_norm_attn, lp, exchange=send_a)
    dfg, d_bf = _fgate_bwd(dka, dfr, fg, b_f_row, lp)
    dbg, dcg, dhc, d_conv, d_gc = _conv_bwd(dzc, bg, cg, hc, conv_full, out_norm_conv, gmat, lp)
    dh1, dwin, d_mix = _inproj_bwd([dbg, dcg, dhc, dq, dk, dv], dfg, dh2, h1, mix_norm, w_in_full)
    dwin_8 = dwin[0:IN_DIM].reshape(N_DEV, IN_DIM // N_DEV, D_MODEL)
    dgu1, dwd1, p_win = _ffn_bwd_act_wd(dh1, gate1, up1, wd1, "ffn1_bwd_act", exchange=[dwin_8])
    dwgu1, p_wd1 = _ffn_bwd_wgu(n1, dgu1, "ffn1_bwd_wgu", exchange=[dwd1.reshape(N_DEV, shard_d, D_MODEL)])
    dh0, d_ffn1, p_wgu1 = _ffn_bwd_in(dh1, None, ffn1_norm, dgu1, wgu1, "ffn1_bwd_in",
                                      tokens=(x2d, meta_full, lp),
                                      exchange=[dwgu1.reshape(N_DEV, shard_gu, D_MODEL)])

    dh0 = dh0.reshape(nb, lp, D_MODEL)
    grad_x = dh0[:, PAD + N_META:, :]
    d_meta = jnp.sum(dh0[:, PAD:PAD + N_META, :], axis=0)

    small = _pack_small([d_ffn1, d_mix, d_ffn2, d_final], d_gc, d_ga, d_bf, d_conv, d_meta, loss_part)
    (small_all,) = _all_gather([small], "gather_small_grads")
    small_sum = _sum_parts(small_all, "sum_small_grads")
    g_ffn1n, g_mixn, g_ffn2n, g_finaln = (small_sum[8 * t:8 * t + 8].reshape(1, D_MODEL) for t in range(4))
    g_gc = small_sum[32:36].reshape(1, CONV_DIM)
    g_ga = small_sum[36:40].reshape(1, ATTN_DIM)
    g_bf = small_sum[40:41, 0:N_HEADS]
    g_conv_full = small_sum[41:53].reshape(3, CONV_DIM)
    g_meta_full = small_sum[53:181].reshape(N_META, D_MODEL)
    g_conv = lax.dynamic_slice_in_dim(g_conv_full, me * (CONV_DIM // N_DEV), CONV_DIM // N_DEV, axis=1)
    g_meta = lax.dynamic_slice_in_dim(g_meta_full, me * (D_MODEL // N_DEV), D_MODEL // N_DEV, axis=1)

    weights = {
        "meta_tokens": (g_meta[None], meta_tokens, m_meta_tokens, v_meta_tokens),
        "ffn1_norm": (g_ffn1n[None], ffn1_norm, m_ffn1_norm, v_ffn1_norm),
        "ffn1_w_gu": (p_wgu1, ffn1_w_gu[0].T, m_ffn1_w_gu[0].T, v_ffn1_w_gu[0].T),
        "ffn1_w_down": (p_wd1, ffn1_w_down[0], m_ffn1_w_down[0], v_ffn1_w_down[0]),
        "mix_norm": (g_mixn[None], mix_norm, m_mix_norm, v_mix_norm),
        "w_in": (p_win, w_in[0].T, m_w_in[0].T, v_w_in[0].T),
        "conv_w": (g_conv[None], conv_w[0], m_conv_w[0], v_conv_w[0]),
        "b_f": (g_bf[None], b_f, m_b_f, v_b_f),
        "out_norm_conv": (g_gc[None], out_norm_conv, m_out_norm_conv, v_out_norm_conv),
        "out_norm_attn": (g_ga[None], out_norm_attn, m_out_norm_attn, v_out_norm_attn),
        "w_out": (p_wout, w_out[0], m_w_out[0], v_w_out[0]),
        "ffn2_norm": (g_ffn2n[None], ffn2_norm, m_ffn2_norm, v_ffn2_norm),
        "ffn2_w_gu": (p_wgu2, ffn2_w_gu[0].T, m_ffn2_w_gu[0].T, v_ffn2_w_gu[0].T),
        "ffn2_w_down": (p_wd2, ffn2_w_down[0], m_ffn2_w_down[0], v_ffn2_w_down[0]),
        "final_norm": (g_finaln[None], final_norm.reshape(1, D_MODEL), m_final_norm.reshape(1, D_MODEL),
                       v_final_norm.reshape(1, D_MODEL)),
    }
    shapes = {"meta_tokens": meta_tokens.shape, "ffn1_norm": ffn1_norm.shape, "ffn1_w_gu": ffn1_w_gu.shape,
              "ffn1_w_down": ffn1_w_down.shape, "mix_norm": mix_norm.shape, "w_in": w_in.shape,
              "conv_w": conv_w.shape, "b_f": b_f.shape, "out_norm_conv": out_norm_conv.shape,
              "out_norm_attn": out_norm_attn.shape, "w_out": w_out.shape, "ffn2_norm": ffn2_norm.shape,
              "ffn2_w_gu": ffn2_w_gu.shape, "ffn2_w_down": ffn2_w_down.shape, "final_norm": final_norm.shape}
    grads, deltas, new_m, new_v = [], [], [], []
    for name, (p, w, m, vv) in weights.items():
        g, d, nm, nv = _adamw(p, w, m, vv, "adamw_" + name)
        if name in ("ffn1_w_gu", "ffn2_w_gu", "w_in"):
            g, d, nm, nv = g.T, d.T, nm.T, nv.T
        shape = shapes[name]
        grads.append(g.reshape(shape))
        deltas.append(d.reshape(shape))
        new_m.append(nm.reshape(shape))
        new_v.append(nv.reshape(shape))

    loss = small_sum[LOSS_ROW, 0]
    return (loss, grad_x, *grads, *deltas, *new_m, *new_v)
```

```python
import jax
import jax.numpy as jnp
from jax import lax
from jax.experimental import pallas as pl
from jax.experimental.pallas import tpu as pltpu

F32 = jnp.float32
BF16 = jnp.bfloat16

N_DEV = 8
D_MODEL = 1024
N_META = 16
PAD = 128 - N_META
CONV_DIM = 512
ATTN_DIM = 512
HEAD_DIM = 64
N_HEADS = 8
N_PAIRS = N_HEADS // 2
D_FF = 2816
IN_DIM = 3080
IN_PAD = 3200
IN_MAIN = 3072
N_PIECE = IN_MAIN // 512
EPS = 1e-6
NEG = -1e30
TQ = 128
TK = 512
TK_FWD = 1024
VMEM_LIMIT = 56 * 1024 * 1024

HID_PIECES = ((0, 1024), (1024, 2048), (2048, D_FF))
ACT_PIECES = tuple((a, min(a + 256, D_FF)) for a in range(0, D_FF, 256))
W_GU_SHAPE = (2, D_FF, D_MODEL)
W_D_SHAPE = (D_FF, D_MODEL)

ADAM_LR = 0.001
ADAM_B1 = 0.9
ADAM_B2 = 0.999
ADAM_EPS = 1e-08
ADAM_WD = 0.01
ADAM_STEP = 10

MESH = pl.DeviceIdType.MESH
ANY = pl.BlockSpec(memory_space=pl.ANY)


def _params(sem=None):
    return pltpu.CompilerParams(dimension_semantics=sem, vmem_limit_bytes=VMEM_LIMIT)


def _row_tile(n, prefer):
    for t in (prefer, 512, 256, 128):
        if t <= n and n % t == 0:
            return t
    raise ValueError(f"no row tile for {n}")


def _resident(shape):
    zeros = (0,) * len(shape)
    return pl.BlockSpec(shape, lambda i: zeros, pipeline_mode=pl.Buffered(1))


def _dot(a, b):
    return jnp.dot(a, b, preferred_element_type=F32)


def _dot_nt(a, b):
    return lax.dot_general(a, b, (((1,), (1,)), ((), ())), preferred_element_type=F32)


def _dot_tn(a, b):
    return lax.dot_general(a, b, (((0,), (0,)), ((), ())), preferred_element_type=F32)


def _rms(x, g):
    r = lax.rsqrt(jnp.mean(x * x, axis=-1, keepdims=True) + EPS)
    xhat = x * r
    return xhat * g, xhat, r


def _rms_bwd(dn, xhat, r, g):
    dxhat = dn * g
    return r * (dxhat - xhat * jnp.mean(dxhat * xhat, axis=-1, keepdims=True))


def _sigmoid(x):
    return 1.0 / (1.0 + jnp.exp(-x))


def _place():
    return lax.axis_index("x"), lax.axis_index("y"), lax.axis_index("c")


def _comm_sems(nw):
    return [pltpu.SemaphoreType.DMA((nw, 7)), pltpu.SemaphoreType.DMA((nw, 7)), pltpu.SemaphoreType.DMA((nw,))]


def _flip(v, bit):
    return 1 - v if bit else v


class _Gather:
    def __init__(self, ins, outs, sems):
        self.ins, self.outs = ins, outs
        self.send, self.recv, self.local = sems
        x, y, c = _place()
        self.c = c
        self.me, self.sibling = (x, y, c), (x, y, 1 - c)
        first = ((x + 1 - c) % 2, (y + c) % 2)
        second = ((x + c) % 2, (y + 1 - c) % 2)
        self.chips = [first, second, (1 - x, 1 - y)]
        self.targets = [first, second, second]

    def _copy(self, w, k, block, to, own=False):
        slot = self.outs[w].at[4 * block[0] + 2 * block[1] + block[2]]
        return pltpu.make_async_remote_copy(
            src_ref=self.ins[w] if own else slot, dst_ref=slot,
            send_sem=self.send.at[w, k], recv_sem=self.recv.at[w, k], device_id=to, device_id_type=MESH)

    def _mine(self, w):
        x, y, c = self.me
        return pltpu.make_async_copy(self.ins[w], self.outs[w].at[4 * x + 2 * y + c], self.local.at[w])

    def _first(self, w):
        return ([self._copy(w, 0, self.me, self.sibling, own=True)]
                + [self._copy(w, 1 + j, self.me, (*self.targets[j], self.c), own=True) for j in range(2)])

    def _relay(self, w):
        return self._copy(w, 3, (*self.chips[0], self.c), (*self.targets[2], self.c))

    def _landed(self, w, j):
        return self._copy(w, 1 + j, (*self.chips[j], self.c), self.me)

    def _passed(self, w):
        return [self._copy(w, 4 + j, (*chip, self.c), self.sibling) for j, chip in enumerate(self.chips)]

    def start(self):
        for w in range(len(self.ins)):
            self._mine(w).start()
        for w in range(len(self.ins)):
            for cp in self._first(w):
                cp.start()

    def relay(self):
        for w in range(len(self.ins)):
            self._landed(w, 0).wait_recv()
            self._relay(w).start()
            self._passed(w)[0].start()

    def forward(self):
        for w in range(len(self.ins)):
            for j in (1, 2):
                self._landed(w, j).wait_recv()
                self._passed(w)[j].start()

    def finish(self):
        from_sibling = [self.chips[1], self.chips[0], self.chips[2]]
        for w in range(len(self.ins)):
            self._copy(w, 0, self.sibling, self.me).wait_recv()
            for j, chip in enumerate(from_sibling):
                self._copy(w, 4 + j, (*chip, 1 - self.c), self.me).wait_recv()
        for w in range(len(self.ins)):
            for cp in self._first(w) + [self._relay(w)] + self._passed(w):
                cp.wait_send()
            self._mine(w).wait()


class _Exchange:
    def __init__(self, ins, outs, sems):
        self.ins, self.outs = ins, outs
        self.send, self.recv, self.local = sems
        self.x, self.y, self.c = _place()
        self.me = 4 * self.x + 2 * self.y + self.c

    def _copy(self, w, k):
        peer = (_flip(self.x, ((k + 1) >> 2) & 1), _flip(self.y, ((k + 1) >> 1) & 1), _flip(self.c, (k + 1) & 1))
        return pltpu.make_async_remote_copy(
            src_ref=self.ins[w].at[4 * peer[0] + 2 * peer[1] + peer[2]], dst_ref=self.outs[w].at[self.me],
            send_sem=self.send.at[w, k], recv_sem=self.recv.at[w, k], device_id=peer, device_id_type=MESH)

    def _mine(self, w):
        return pltpu.make_async_copy(self.ins[w].at[self.me], self.outs[w].at[self.me], self.local.at[w])

    def start(self):
        for w in range(len(self.ins)):
            self._mine(w).start()
            for k in range(N_DEV - 1):
                self._copy(w, k).start()

    def finish(self):
        for w in range(len(self.ins)):
            for k in range(N_DEV - 1):
                self._copy(w, k).wait()
            self._mine(w).wait()


def _split_refs(refs, n_in, n_comm, n_out, n_scr):
    a = n_in
    b = a + n_comm
    c = b + n_out
    d = c + n_comm
    e = d + n_scr
    return refs[:a], refs[a:b], refs[b:c], refs[c:d], refs[d:e], refs[e:]


def _all_gather(xs, name):
    nw = len(xs)

    def body(*refs):
        comm = _Gather(refs[:nw], refs[nw:2 * nw], refs[2 * nw:])
        comm.start()
        comm.relay()
        comm.forward()
        comm.finish()

    return pl.pallas_call(
        body, name=name, in_specs=[ANY] * nw, out_specs=[ANY] * nw,
        out_shape=[jax.ShapeDtypeStruct((N_DEV,) + a.shape, a.dtype) for a in xs],
        scratch_shapes=_comm_sems(nw),
    )(*xs)


def _token_spec(k, ksub, nq):
    def index_map(i):
        s = ksub * i + k
        return ((s // nq) * (nq - 1) + jnp.maximum(s % nq, 1) - 1, 0)
    return pl.BlockSpec((128, D_MODEL), index_map)


def _is_lead(i, k, ksub, nq):
    return ((ksub * i + k) % nq) == 0


def _assemble_rows(i, x_refs, meta_ref, nq):
    ksub = len(x_refs)
    lead = jnp.concatenate([jnp.zeros((PAD, D_MODEL), F32), meta_ref[...]], axis=0)
    return jnp.concatenate([jnp.where(_is_lead(i, k, ksub, nq), lead, x_refs[k][...]) for k in range(ksub)], axis=0)


def _swiglu(nb, wgu_ref, wd_ref, gate_ref, up_ref):
    acc = jnp.zeros((nb.shape[0], D_MODEL), F32)
    for a, b in HID_PIECES:
        gate = _dot_nt(nb, wgu_ref[0, a:b, :])
        up = _dot_nt(nb, wgu_ref[1, a:b, :])
        gate_ref[:, a:b] = gate.astype(BF16)
        up_ref[:, a:b] = up.astype(BF16)
        acc = acc + _dot((gate * _sigmoid(gate) * up).astype(BF16), wd_ref[a:b, :])
    return acc


def _ffn1_fwd(x2d, meta, lp, gain, wgu, wd, gather):
    nq = lp // 128
    n = (x2d.shape[0] // (nq - 1)) * nq
    tm = _row_tile(n, 512)
    ksub = tm // 128
    n_i = n // tm
    nw = len(gather)

    def body(*refs):
        x_refs = refs[:ksub]
        (meta_ref, g_ref, wgu_ref, wd_ref), gin, (out_ref, nrm_ref, gate_ref, up_ref), gout, _, sems = \
            _split_refs(refs[ksub:], 4, nw, 4, 0)
        i = pl.program_id(0)
        comm = _Gather(gin, gout, sems)
        pl.when(i == 0)(comm.start)
        pl.when(i == n_i // 2)(comm.relay)
        pl.when(i == max(n_i - 3, n_i // 2))(comm.forward)

        hv = _assemble_rows(i, x_refs, meta_ref, nq)
        y, _, _ = _rms(hv, g_ref[...])
        nb = y.astype(BF16)
        nrm_ref[...] = nb
        out_ref[...] = hv + 0.5 * _swiglu(nb, wgu_ref, wd_ref, gate_ref, up_ref)

        pl.when(i == n_i - 1)(comm.finish)

    rows = pl.BlockSpec((tm, D_MODEL), lambda i: (i, 0))
    hid = pl.BlockSpec((tm, D_FF), lambda i: (i, 0))
    return pl.pallas_call(
        body, name="ffn1_fwd", grid=(n_i,),
        in_specs=[_token_spec(k, ksub, nq) for k in range(ksub)]
        + [pl.BlockSpec((N_META, D_MODEL), lambda i: (0, 0)), pl.BlockSpec((1, D_MODEL), lambda i: (0, 0)),
           _resident(W_GU_SHAPE), _resident(W_D_SHAPE)] + [ANY] * nw,
        out_specs=[rows, rows, hid, hid] + [ANY] * nw,
        out_shape=[jax.ShapeDtypeStruct((n, D_MODEL), F32), jax.ShapeDtypeStruct((n, D_MODEL), BF16),
                   jax.ShapeDtypeStruct((n, D_FF), BF16), jax.ShapeDtypeStruct((n, D_FF), BF16)]
        + [jax.ShapeDtypeStruct((N_DEV,) + a.shape, a.dtype) for a in gather],
        scratch_shapes=_comm_sems(nw),
        compiler_params=_params(("arbitrary",)),
    )(*([x2d] * ksub), meta, gain, wgu, wd, *gather)


def _ffn2_fwd_loss(h, gain, wgu, wd, gfinal, target, lp):
    n = h.shape[0]
    nq = lp // 128
    tm = _row_tile(n, 512)
    ksub = tm // 128
    n_i = n // tm

    def body(*refs):
        t_refs = refs[:ksub]
        h_ref, g_ref, wgu_ref, wd_ref, gf_ref, dh_ref, nrm_ref, gate_ref, up_ref, loss_ref, dgf_ref = refs[ksub:]
        i = pl.program_id(0)

        @pl.when(i == 0)
        def _():
            loss_ref[...] = jnp.zeros_like(loss_ref)
            dgf_ref[...] = jnp.zeros_like(dgf_ref)

        hv = h_ref[...]
        y, _, _ = _rms(hv, g_ref[...])
        nb = y.astype(BF16)
        nrm_ref[...] = nb
        hout = hv + 0.5 * _swiglu(nb, wgu_ref, wd_ref, gate_ref, up_ref)

        gf = gf_ref[...]
        loss = jnp.zeros((1, 1), F32)
        dgf = jnp.zeros((1, D_MODEL), F32)
        for k in range(ksub):
            yk, xhat, r = _rms(hout[128 * k:128 * (k + 1)], gf)
            err = jnp.where(_is_lead(i, k, ksub, nq), 0.0, yk - t_refs[k][...])
            loss = loss + 0.5 * jnp.sum(jnp.sum(err * err, axis=1, keepdims=True), axis=0,
                                        keepdims=True) * (1.0 / D_MODEL)
            dy = err * (1.0 / D_MODEL)
            dh_ref[128 * k:128 * (k + 1), :] = _rms_bwd(dy, xhat, r, gf)
            dgf = dgf + jnp.sum(dy * xhat, axis=0, keepdims=True)
        loss_ref[...] += loss
        dgf_ref[...] += dgf

    rows = pl.BlockSpec((tm, D_MODEL), lambda i: (i, 0))
    hid = pl.BlockSpec((tm, D_FF), lambda i: (i, 0))
    vec = pl.BlockSpec((1, D_MODEL), lambda i: (0, 0))
    return pl.pallas_call(
        body, name="ffn2_fwd_loss", grid=(n_i,),
        in_specs=[_token_spec(k, ksub, nq) for k in range(ksub)]
        + [rows, vec, _resident(W_GU_SHAPE), _resident(W_D_SHAPE), vec],
        out_specs=[rows, rows, hid, hid, pl.BlockSpec((1, 1), lambda i: (0, 0)), vec],
        out_shape=[jax.ShapeDtypeStruct((n, D_MODEL), F32), jax.ShapeDtypeStruct((n, D_MODEL), BF16),
                   jax.ShapeDtypeStruct((n, D_FF), BF16), jax.ShapeDtypeStruct((n, D_FF), BF16),
                   jax.ShapeDtypeStruct((1, 1), F32), jax.ShapeDtypeStruct((1, D_MODEL), F32)],
        compiler_params=_params(("arbitrary",)),
    )(*([target] * ksub), h, gain, wgu, wd, gfinal)


def _ffn_bwd_act_wd(dh_out, gate, up, wd, name, exchange=()):
    n = dh_out.shape[0]
    tm = _row_tile(n, 256)
    n_i = n // tm
    nw = len(exchange)

    def body(*refs):
        (dh_ref, gate_ref, up_ref, wd_ref), xin, (dgu_ref, dw_ref), xout, (acc_scr,), sems = \
            _split_refs(refs, 4, nw, 2, 1)
        i = pl.program_id(0)
        if nw:
            comm = _Exchange(xin, xout, sems)
            pl.when(i == 0)(comm.start)

        @pl.when(i == 0)
        def _():
            acc_scr[...] = jnp.zeros_like(acc_scr)

        dhb = (0.5 * dh_ref[...]).astype(BF16)
        for a, b in ACT_PIECES:
            da = _dot_nt(dhb, wd_ref[a:b, :])
            g = gate_ref[:, a:b].astype(F32)
            u = up_ref[:, a:b].astype(F32)
            sig = _sigmoid(g)
            silu = g * sig
            dgu_ref[:, a:b] = (da * u * (sig + silu * (1.0 - sig))).astype(BF16)
            dgu_ref[:, D_FF + a:D_FF + b] = (da * silu).astype(BF16)
            acc_scr[a:b, :] += _dot_tn((silu * u).astype(BF16), dhb)

        @pl.when(i == n_i - 1)
        def _():
            dw_ref[...] = acc_scr[...].astype(BF16)

        if nw:
            pl.when(i == n_i - 1)(comm.finish)

    rows = pl.BlockSpec((tm, D_MODEL), lambda i: (i, 0))
    hid = pl.BlockSpec((tm, D_FF), lambda i: (i, 0))
    return pl.pallas_call(
        body, name=name, grid=(n_i,),
        in_specs=[rows, hid, hid, _resident(W_D_SHAPE)] + [ANY] * nw,
        out_specs=[pl.BlockSpec((tm, 2 * D_FF), lambda i: (i, 0)), _resident(W_D_SHAPE)] + [ANY] * nw,
        out_shape=[jax.ShapeDtypeStruct((n, 2 * D_FF), BF16), jax.ShapeDtypeStruct(W_D_SHAPE, BF16)]
        + [jax.ShapeDtypeStruct(a.shape, a.dtype) for a in exchange],
        scratch_shapes=[pltpu.VMEM(W_D_SHAPE, F32)] + (_comm_sems(nw) if nw else []),
        compiler_params=_params(("arbitrary",)),
    )(dh_out, gate, up, wd, *exchange)


def _ffn_bwd_in(dh_out, h_in, gain, dgu, wgu, name, tokens=None, exchange=()):
    n = dh_out.shape[0]
    tm = _row_tile(n, 512)
    n_i = n // tm
    nw = len(exchange)
    ksub, nq = (tm // 128, tokens[2] // 128) if tokens else (1, 0)

    def body(*refs):
        h_refs = refs[:ksub]
        (meta_ref, dh_ref, g_ref, dgu_ref, wgu_ref), xin, (dhin_ref, dgain_ref), xout, _, sems = \
            _split_refs(refs[ksub:], 5, nw, 2, 0)
        i = pl.program_id(0)
        if nw:
            comm = _Exchange(xin, xout, sems)
            pl.when(i == 0)(comm.start)

        @pl.when(i == 0)
        def _():
            dgain_ref[...] = jnp.zeros_like(dgain_ref)

        dn = _dot(dgu_ref[...], wgu_ref[...])
        gain_v = g_ref[...]
        hv = _assemble_rows(i, h_refs, meta_ref, nq) if tokens else h_refs[0][...]
        _, xhat, r = _rms(hv, gain_v)
        dhin_ref[...] = dh_ref[...] + _rms_bwd(dn, xhat, r, gain_v)
        dgain_ref[...] += jnp.sum(dn * xhat, axis=0, keepdims=True)

        if nw:
            pl.when(i == n_i - 1)(comm.finish)

    rows = pl.BlockSpec((tm, D_MODEL), lambda i: (i, 0))
    hid = pl.BlockSpec((tm, D_FF), lambda i: (i, 0))
    vec = pl.BlockSpec((1, D_MODEL), lambda i: (0, 0))
    meta_spec = pl.BlockSpec((N_META, D_MODEL), lambda i: (0, 0))
    if tokens:
        h_specs, h_args, meta = [_token_spec(k, ksub, nq) for k in range(ksub)], [tokens[0]] * ksub, tokens[1]
    else:
        h_specs, h_args, meta = [rows], [h_in], jnp.zeros((N_META, D_MODEL), F32)
    return pl.pallas_call(
        body, name=name, grid=(n_i,),
        in_specs=h_specs + [meta_spec, rows, vec, pl.BlockSpec((tm, 2 * D_FF), lambda i: (i, 0)),
                            _resident((2 * D_FF, D_MODEL))] + [ANY] * nw,
        out_specs=[rows, vec] + [ANY] * nw,
        out_shape=[jax.ShapeDtypeStruct((n, D_MODEL), F32), jax.ShapeDtypeStruct((1, D_MODEL), F32)]
        + [jax.ShapeDtypeStruct(a.shape, a.dtype) for a in exchange],
        scratch_shapes=_comm_sems(nw) if nw else [],
        compiler_params=_params(("arbitrary",)),
    )(*h_args, meta, dh_out, gain, dgu, wgu.reshape(2 * D_FF, D_MODEL), *exchange)


def _ffn_bwd_wgu(nrm, dgu, name, exchange=()):
    n = nrm.shape[0]
    tm = _row_tile(n, 512)
    n_i = n // tm
    nw = len(exchange)

    def body(*refs):
        (nrm_ref, dgu_ref), xin, (dw_ref,), xout, (acc_scr,), sems = _split_refs(refs, 2, nw, 1, 1)
        i = pl.program_id(0)
        if nw:
            comm = _Exchange(xin, xout, sems)
            pl.when(i == 0)(comm.start)

        @pl.when(i == 0)
        def _():
            acc_scr[...] = jnp.zeros_like(acc_scr)

        nb = nrm_ref[...]
        for half in (0, D_FF):
            for a, b in HID_PIECES:
                acc_scr[half + a:half + b, :] += _dot_tn(dgu_ref[:, half + a:half + b], nb)

        @pl.when(i == n_i - 1)
        def _():
            dw_ref[...] = acc_scr[...].astype(BF16)

        if nw:
            pl.when(i == n_i - 1)(comm.finish)

    shape = (2 * D_FF, D_MODEL)
    res = pl.pallas_call(
        body, name=name, grid=(n_i,),
        in_specs=[pl.BlockSpec((tm, D_MODEL), lambda i: (i, 0)),
                  pl.BlockSpec((tm, 2 * D_FF), lambda i: (i, 0))] + [ANY] * nw,
        out_specs=[_resident(shape)] + [ANY] * nw,
        out_shape=[jax.ShapeDtypeStruct(shape, BF16)] + [jax.ShapeDtypeStruct(a.shape, a.dtype) for a in exchange],
        scratch_shapes=[pltpu.VMEM(shape, F32)] + (_comm_sems(nw) if nw else []),
        compiler_params=_params(("arbitrary",)),
    )(nrm, dgu, *exchange)
    return res if nw else res[0]


def _inproj_fwd(h, gain, w_in):
    n = h.shape[0]
    tm = _row_tile(n, 512)

    def body(h_ref, g_ref, w_ref, *outs):
        y, _, _ = _rms(h_ref[...], g_ref[...])
        nb = y.astype(BF16)
        for p in range(N_PIECE):
            outs[p][...] = _dot_nt(nb, w_ref[512 * p:512 * (p + 1), :]).astype(BF16)
        outs[N_PIECE][...] = _dot_nt(nb, w_ref[IN_MAIN:IN_PAD, :])

    piece = pl.BlockSpec((tm, 512), lambda i: (i, 0))
    return pl.pallas_call(
        body, name="inproj_fwd", grid=(n // tm,),
        in_specs=[pl.BlockSpec((tm, D_MODEL), lambda i: (i, 0)),
                  pl.BlockSpec((1, D_MODEL), lambda i: (0, 0)),
                  pl.BlockSpec((IN_PAD, D_MODEL), lambda i: (0, 0))],
        out_specs=[piece] * N_PIECE + [pl.BlockSpec((tm, 128), lambda i: (i, 0))],
        out_shape=[jax.ShapeDtypeStruct((n, 512), BF16)] * N_PIECE + [jax.ShapeDtypeStruct((n, 128), F32)],
        compiler_params=_params(("parallel",)),
    )(h, gain, w_in)


def _inproj_bwd(dpieces, dfg, dh_out, h_in, gain, w_in):
    n = h_in.shape[0]
    tm = _row_tile(n, 512)
    n_i = n // tm

    def body(*refs):
        dp_refs = refs[:N_PIECE]
        dfg_ref, dh_ref, h_ref, g_ref, w_ref, dhin_ref, dw_ref, dgain_ref, acc_scr = refs[N_PIECE:]
        i = pl.program_id(0)

        @pl.when(i == 0)
        def _():
            acc_scr[...] = jnp.zeros_like(acc_scr)
            dgain_ref[...] = jnp.zeros_like(dgain_ref)

        gain_v = g_ref[...]
        y, xhat, r = _rms(h_ref[...], gain_v)
        nb = y.astype(BF16)
        dn = jnp.zeros((tm, D_MODEL), F32)
        for p in range(N_PIECE + 1):
            lo, hi = (512 * p, 512 * (p + 1)) if p < N_PIECE else (IN_MAIN, IN_PAD)
            dp = (dp_refs[p][...] if p < N_PIECE else dfg_ref[...]).astype(BF16)
            dn = dn + _dot(dp, w_ref[lo:hi, :])
            acc_scr[lo:hi, :] += _dot_tn(dp, nb)
        dhin_ref[...] = dh_ref[...] + _rms_bwd(dn, xhat, r, gain_v)
        dgain_ref[...] += jnp.sum(dn * xhat, axis=0, keepdims=True)

        @pl.when(i == n_i - 1)
        def _():
            dw_ref[...] = acc_scr[...].astype(BF16)

    piece = pl.BlockSpec((tm, 512), lambda i: (i, 0))
    rows = pl.BlockSpec((tm, D_MODEL), lambda i: (i, 0))
    vec = pl.BlockSpec((1, D_MODEL), lambda i: (0, 0))
    wspec = pl.BlockSpec((IN_PAD, D_MODEL), lambda i: (0, 0))
    return pl.pallas_call(
        body, name="inproj_bwd", grid=(n_i,),
        in_specs=[piece] * N_PIECE + [pl.BlockSpec((tm, 128), lambda i: (i, 0)), rows, rows, vec, wspec],
        out_specs=[rows, wspec, vec],
        out_shape=[jax.ShapeDtypeStruct((n, D_MODEL), F32),
                   jax.ShapeDtypeStruct((IN_PAD, D_MODEL), BF16),
                   jax.ShapeDtypeStruct((1, D_MODEL), F32)],
        scratch_shapes=[pltpu.VMEM((IN_PAD, D_MODEL), F32)],
        compiler_params=_params(("arbitrary",)),
    )(*dpieces, dfg, dh_out, h_in, gain, w_in)


def _outproj_bwd(dh, zc, za, w_out):
    n = dh.shape[0]
    tm = _row_tile(n, 512)
    n_i = n // tm

    def body(dh_ref, zc_ref, za_ref, w_ref, dzc_ref, dza_ref, dw_ref, acc_scr):
        i = pl.program_id(0)

        @pl.when(i == 0)
        def _():
            acc_scr[...] = jnp.zeros_like(acc_scr)

        dhb = dh_ref[...].astype(BF16)
        dzc_ref[...] = _dot_nt(dhb, w_ref[0:CONV_DIM, :]).astype(BF16)
        dza_ref[...] = _dot_nt(dhb, w_ref[CONV_DIM:, :]).astype(BF16)
        acc_scr[0:CONV_DIM, :] += _dot_tn(zc_ref[...], dhb)
        acc_scr[CONV_DIM:, :] += _dot_tn(za_ref[...], dhb)

        @pl.when(i == n_i - 1)
        def _():
            dw_ref[...] = acc_scr[...].astype(BF16)

    half = pl.BlockSpec((tm, 512), lambda i: (i, 0))
    wspec = pl.BlockSpec((D_MODEL, D_MODEL), lambda i: (0, 0))
    return pl.pallas_call(
        body, name="outproj_bwd", grid=(n_i,),
        in_specs=[pl.BlockSpec((tm, D_MODEL), lambda i: (i, 0)), half, half, wspec],
        out_specs=[half, half, wspec],
        out_shape=[jax.ShapeDtypeStruct((n, 512), BF16), jax.ShapeDtypeStruct((n, 512), BF16),
                   jax.ShapeDtypeStruct((D_MODEL, D_MODEL), BF16)],
        scratch_shapes=[pltpu.VMEM((D_MODEL, D_MODEL), F32)],
        compiler_params=_params(("arbitrary",)),
    )(dh, zc, za, w_out)


def _group_matrix():
    r = lax.broadcasted_iota(jnp.int32, (128, 128), 0) // HEAD_DIM
    c = lax.broadcasted_iota(jnp.int32, (128, 128), 1) // HEAD_DIM
    return jnp.where(r == c, 1.0 / HEAD_DIM, 0.0).astype(BF16)


def _group_mean(x, gmat):
    hi = x.astype(BF16)
    lo = (x - hi.astype(F32)).astype(BF16)
    return _dot(hi, gmat) + _dot(lo, gmat)


def _shift_rows(x, s):
    rows = x.shape[0]
    t = lax.broadcasted_iota(jnp.int32, x.shape, 0)
    rolled = pltpu.roll(x, s % rows, 0)
    keep = (t >= s) if s > 0 else (t < rows + s)
    return jnp.where(keep, rolled, 0.0)


def _conv_parts(bg_ref, cg_ref, hc_ref, w_ref):
    bg = bg_ref[...].astype(F32)
    cg = cg_ref[...].astype(F32)
    hc = hc_ref[...].astype(F32)
    u = cg * hc
    u1 = _shift_rows(u, 1)
    u2 = _shift_rows(u, 2)
    conv = w_ref[2:3, :] * u + w_ref[1:2, :] * u1 + w_ref[0:1, :] * u2
    return bg, cg, hc, u, u1, u2, conv


def _conv_fwd(bg, cg, hc, conv_w, gain, gmat, lp):
    n = bg.shape[0]
    nb = n // lp

    def body(bg_ref, cg_ref, hc_ref, w_ref, g_ref, gm_ref, z_ref):
        bgv, _, _, _, _, _, conv = _conv_parts(bg_ref, cg_ref, hc_ref, w_ref)
        yc = bgv * conv
        r = lax.rsqrt(_group_mean(yc * yc, gm_ref[...]) + EPS)
        z_ref[...] = (yc * r * g_ref[...]).astype(BF16)

    blk = pl.BlockSpec((lp, 128), lambda c, b: (b, c))
    return pl.pallas_call(
        body, name="conv_fwd", grid=(CONV_DIM // 128, nb),
        in_specs=[blk, blk, blk, pl.BlockSpec((3, 128), lambda c, b: (0, c)),
                  pl.BlockSpec((1, 128), lambda c, b: (0, c)), pl.BlockSpec((128, 128), lambda c, b: (0, 0))],
        out_specs=blk,
        out_shape=jax.ShapeDtypeStruct((n, CONV_DIM), BF16),
        compiler_params=_params(("parallel", "parallel")),
    )(bg, cg, hc, conv_w, gain, gmat)


def _conv_bwd(dz, bg, cg, hc, conv_w, gain, gmat, lp):
    n = bg.shape[0]
    nb = n // lp

    def body(dz_ref, bg_ref, cg_ref, hc_ref, w_ref, g_ref, gm_ref,
             dbg_ref, dcg_ref, dhc_ref, dw_ref, dgain_ref):
        b = pl.program_id(1)

        @pl.when(b == 0)
        def _():
            dw_ref[...] = jnp.zeros_like(dw_ref)
            dgain_ref[...] = jnp.zeros_like(dgain_ref)

        bgv, cgv, hcv, u, u1, u2, conv = _conv_parts(bg_ref, cg_ref, hc_ref, w_ref)
        gm = gm_ref[...]
        yc = bgv * conv
        r = lax.rsqrt(_group_mean(yc * yc, gm) + EPS)
        yhat = yc * r
        dzv = dz_ref[...].astype(F32)
        dyhat = dzv * g_ref[...]
        dgain_ref[...] += jnp.sum(dzv * yhat, axis=0, keepdims=True)
        dyc = r * (dyhat - yhat * _group_mean(dyhat * yhat, gm))
        dbg_ref[...] = (dyc * conv).astype(BF16)
        dconv = dyc * bgv
        du = (w_ref[2:3, :] * dconv + w_ref[1:2, :] * _shift_rows(dconv, -1)
              + w_ref[0:1, :] * _shift_rows(dconv, -2))
        dcg_ref[...] = (du * hcv).astype(BF16)
        dhc_ref[...] = (du * cgv).astype(BF16)
        dw_ref[0:1, :] += jnp.sum(dconv * u2, axis=0, keepdims=True)
        dw_ref[1:2, :] += jnp.sum(dconv * u1, axis=0, keepdims=True)
        dw_ref[2:3, :] += jnp.sum(dconv * u, axis=0, keepdims=True)

    blk = pl.BlockSpec((lp, 128), lambda c, b: (b, c))
    wspec = pl.BlockSpec((3, 128), lambda c, b: (0, c))
    gspec = pl.BlockSpec((1, 128), lambda c, b: (0, c))
    return pl.pallas_call(
        body, name="conv_bwd", grid=(CONV_DIM // 128, nb),
        in_specs=[blk, blk, blk, blk, wspec, gspec, pl.BlockSpec((128, 128), lambda c, b: (0, 0))],
        out_specs=[blk, blk, blk, wspec, gspec],
        out_shape=[jax.ShapeDtypeStruct((n, CONV_DIM), BF16)] * 3
        + [jax.ShapeDtypeStruct((3, CONV_DIM), F32), jax.ShapeDtypeStruct((1, CONV_DIM), F32)],
        compiler_params=_params(("parallel", "arbitrary")),
    )(dz, bg, cg, hc, conv_w, gain, gmat)


KEY_MASKED = 1e30
ONE_LANE = 24


def _scan_steps(rows):
    s, out = 1, []
    while s < rows:
        out.append(s)
        s *= 2
    return out


def _fgate_fwd(fg, b_f, lp):
    n = fg.shape[0]
    nb = n // lp

    def body(fg_ref, b_ref, ka_ref, qa_ref):
        x = fg_ref[...] + b_ref[...]
        logf = jnp.minimum(x, 0.0) - jnp.log(1.0 + jnp.exp(-jnp.abs(x)))
        t = lax.broadcasted_iota(jnp.int32, (lp, 128), 0)
        lane = lax.broadcasted_iota(jnp.int32, (lp, 128), 1)
        f = jnp.where((t >= PAD) & (lane < N_HEADS), logf, 0.0)
        for s in _scan_steps(lp):
            f = f + _shift_rows(f, s)
        hi = f.astype(BF16).astype(F32)
        rest = f - hi
        mid = rest.astype(BF16).astype(F32)
        lo = (rest - mid).astype(BF16).astype(F32)
        ones = jnp.where((lane >= ONE_LANE) & (lane < ONE_LANE + 3), 1.0, 0.0)
        hi_key = jnp.where((t < PAD) & (lane < N_HEADS), KEY_MASKED, hi)
        ka_ref[...] = (hi_key + pltpu.roll(mid, 8, 1) + pltpu.roll(lo, 16, 1) + ones).astype(BF16)
        for h in range(N_HEADS):
            minus = jnp.where((lane == h) | (lane == 8 + h) | (lane == 16 + h), -1.0, 0.0)
            terms = (jnp.where(lane == ONE_LANE, pltpu.roll(hi, ONE_LANE - h, 1), 0.0)
                     + jnp.where(lane == ONE_LANE + 1, pltpu.roll(mid, ONE_LANE + 1 - h, 1), 0.0)
                     + jnp.where(lane == ONE_LANE + 2, pltpu.roll(lo, ONE_LANE + 2 - h, 1), 0.0))
            qa_ref[:, 128 * h:128 * (h + 1)] = (minus + terms).astype(BF16)

    return pl.pallas_call(
        body, name="fgate_fwd", grid=(nb,),
        in_specs=[pl.BlockSpec((lp, 128), lambda b: (b, 0)), pl.BlockSpec((1, 128), lambda b: (0, 0))],
        out_specs=[pl.BlockSpec((lp, 128), lambda b: (b, 0)), pl.BlockSpec((lp, N_HEADS * 128), lambda b: (b, 0))],
        out_shape=[jax.ShapeDtypeStruct((n, 128), BF16), jax.ShapeDtypeStruct((n, N_HEADS * 128), BF16)],
        compiler_params=_params(("parallel",)),
    )(fg, b_f)


def _fgate_bwd(dka, dfr, fg, b_f, lp):
    n = fg.shape[0]
    nb = n // lp

    def body(dka_ref, dfr_ref, fg_ref, b_ref, dfg_ref, db_ref):
        b = pl.program_id(0)

        @pl.when(b == 0)
        def _():
            db_ref[...] = jnp.zeros_like(db_ref)

        wide = jnp.concatenate([dfr_ref[0], jnp.zeros((128 - N_HEADS, lp), F32)], axis=0)
        t = lax.broadcasted_iota(jnp.int32, (lp, 128), 0)
        lane = lax.broadcasted_iota(jnp.int32, (lp, 128), 1)
        d = jnp.where(lane < N_HEADS, dka_ref[...], 0.0) + wide.T
        for s in _scan_steps(lp):
            d = d + _shift_rows(d, -s)
        x = fg_ref[...] + b_ref[...]
        dx = jnp.where((t >= PAD) & (lane < N_HEADS), d * _sigmoid(-x), 0.0)
        dfg_ref[...] = dx
        db_ref[...] += jnp.sum(dx, axis=0, keepdims=True)

    return pl.pallas_call(
        body, name="fgate_bwd", grid=(nb,),
        in_specs=[pl.BlockSpec((lp, 128), lambda b: (b, 0)), pl.BlockSpec((1, N_HEADS, lp), lambda b: (b, 0, 0)),
                  pl.BlockSpec((lp, 128), lambda b: (b, 0)), pl.BlockSpec((1, 128), lambda b: (0, 0))],
        out_specs=[pl.BlockSpec((lp, 128), lambda b: (b, 0)), pl.BlockSpec((1, 128), lambda b: (0, 0))],
        out_shape=[jax.ShapeDtypeStruct((n, 128), F32), jax.ShapeDtypeStruct((1, 128), F32)],
        compiler_params=_params(("arbitrary",)),
    )(dka, dfr, fg, b_f)


def _head_masks():
    lane = lax.broadcasted_iota(jnp.int32, (1, 128), 1)
    return lane < HEAD_DIM


def _stack_heads(x2, first):
    zero = jnp.zeros_like(x2)
    return jnp.concatenate([jnp.where(first, x2, zero), jnp.where(first, zero, x2)], axis=0)


def _stack_heads_lanes(xt):
    r = lax.broadcasted_iota(jnp.int32, xt.shape, 0)
    zero = jnp.zeros_like(xt)
    return jnp.concatenate([jnp.where(r < HEAD_DIM, xt, zero), jnp.where(r < HEAD_DIM, zero, xt)], axis=1)


def _pair_cols(col0, col1, first):
    return jnp.where(first, col0, col1)


def _pair_rows(row0, row1):
    r = lax.broadcasted_iota(jnp.int32, (128, TQ), 0)
    return jnp.where(r < HEAD_DIM, row0, row1)


def _query_side(q_ref, qa_ref, p, first):
    q2 = q_ref[:, 128 * p:128 * (p + 1)] * 0.125
    zero = jnp.zeros_like(q2)
    top = jnp.concatenate([jnp.where(first, q2, zero), qa_ref[:, 128 * (2 * p):128 * (2 * p + 1)]], axis=1)
    bot = jnp.concatenate([jnp.where(first, zero, q2), qa_ref[:, 128 * (2 * p + 1):128 * (2 * p + 2)]], axis=1)
    return jnp.concatenate([top, bot], axis=0)


def _padded_keys(lp, chunk):
    return ((lp + chunk - 1) // chunk) * chunk


def _chunk_mask(i, c, tk, chunk):
    r = lax.broadcasted_iota(jnp.int32, (tk, 2 * TQ), 0)
    col = lax.broadcasted_iota(jnp.int32, (tk, 2 * TQ), 1)
    return (c * chunk + r) <= (i * TQ + (col & (TQ - 1)))


def _causal_sweep(i, step, init, chunk):
    per = chunk // TQ
    last = i // per
    carry = lax.fori_loop(0, last, lambda c, carry: step(c, carry, False, chunk), init)
    tails = [lambda carry, r=r: step(last, carry, True, TQ * (r + 1)) for r in range(per)]
    return lax.switch(i % per, tails, carry)


def _transpose_bf16(x):
    return x.astype(F32).T.astype(BF16)


def _attn_fwd(q, qa, k, v, ka, gain, zc, w_out, h, lp):
    n = q.shape[0]
    nb = n // lp
    nq = lp // TQ
    lpp = _padded_keys(lp, TK_FWD)

    def body(q_ref, qa_ref, k_ref, v_ref, ka_ref, g_ref, zc_ref, w_ref, h_ref,
             z_ref, o_ref, lse_ref, hout_ref, kx_scr, vt_scr):
        i = pl.program_id(1)
        first = _head_masks()

        @pl.when(i == 0)
        def _():
            if lpp > lp:
                kx_scr[lp:lpp, :] = jnp.zeros((lpp - lp, 2 * ATTN_DIM), BF16)
                vt_scr[:, lp:lpp] = jnp.zeros((ATTN_DIM, lpp - lp), BF16)
            for p in range(N_PAIRS):
                kx_scr[0:lp, 256 * p:256 * p + 128] = k_ref[:, 128 * p:128 * (p + 1)]
                kx_scr[0:lp, 256 * p + 128:256 * (p + 1)] = ka_ref[...]
            vt_scr[:, 0:lp] = _transpose_bf16(v_ref[...])

        rhs_t = [_transpose_bf16(_query_side(q_ref, qa_ref, p, first)) for p in range(N_PAIRS)]

        def step(c, carry, masked, tk):
            koff = pl.multiple_of(c * TK_FWD, TK_FWD)
            valid = _chunk_mask(i, c, tk, TK_FWD) if masked else None
            new = []
            sts = [_dot(kx_scr[pl.ds(koff, tk), 256 * p:256 * (p + 1)], rhs_t[p]) for p in range(N_PAIRS)]
            for p in range(N_PAIRS):
                m, l, acc = carry[p]
                st = sts[p]
                if masked:
                    st = jnp.where(valid, st, NEG)
                m_new = jnp.maximum(m, jnp.max(st, axis=0, keepdims=True))
                pt = jnp.exp(st - m_new)
                alpha = jnp.exp(m - m_new)
                l = alpha * l + jnp.sum(pt, axis=0, keepdims=True)
                pb = pt.astype(BF16)
                vt = _stack_heads_lanes(vt_scr[128 * p:128 * (p + 1), pl.ds(koff, tk)])
                pv = _dot(vt, jnp.concatenate([pb[:, 0:TQ], pb[:, TQ:]], axis=0))
                acc = acc * _pair_rows(alpha[:, 0:TQ], alpha[:, TQ:]) + pv
                new.append((m_new, l, acc))
            return tuple(new)

        init = tuple((jnp.full((1, 2 * TQ), NEG, F32), jnp.zeros((1, 2 * TQ), F32), jnp.zeros((128, TQ), F32))
                     for _ in range(N_PAIRS))
        final = _causal_sweep(i, step, init, TK_FWD)

        row = lax.broadcasted_iota(jnp.int32, (TQ, 128), 0)
        real = (i * TQ + row) >= PAD
        zs = [zc_ref[...]]
        for p in range(N_PAIRS):
            m, l, acc = final[p]
            inv = 1.0 / l
            ot = acc * _pair_rows(inv[:, 0:TQ], inv[:, TQ:])
            sq = ot * ot
            r0 = lax.rsqrt(jnp.sum(sq[0:HEAD_DIM], axis=0, keepdims=True) * (1.0 / HEAD_DIM) + EPS)
            r1 = lax.rsqrt(jnp.sum(sq[HEAD_DIM:], axis=0, keepdims=True) * (1.0 / HEAD_DIM) + EPS)
            cols = slice(128 * p, 128 * (p + 1))
            o_ref[:, cols] = jnp.where(real, ot.T, 0.0).astype(BF16)
            z = (jnp.where(real, (ot * _pair_rows(r0, r1)).T, 0.0) * g_ref[:, cols]).astype(BF16)
            z_ref[:, cols] = z
            zs.append(z)
            lse = m + jnp.log(l)
            lse_ref[0, 2 * p:2 * p + 1, :] = lse[:, 0:TQ]
            lse_ref[0, 2 * p + 1:2 * p + 2, :] = lse[:, TQ:]
        hout_ref[...] = h_ref[...] + _dot(jnp.concatenate(zs, axis=1), w_ref[...])

    qblk = pl.BlockSpec((TQ, ATTN_DIM), lambda b, i: (b * nq + i, 0))
    qablk = pl.BlockSpec((TQ, N_HEADS * 128), lambda b, i: (b * nq + i, 0))
    seq = pl.BlockSpec((lp, ATTN_DIM), lambda b, i: (b, 0))
    rowblk = pl.BlockSpec((1, N_HEADS, TQ), lambda b, i: (b, 0, i))
    hblk = pl.BlockSpec((TQ, D_MODEL), lambda b, i: (b * nq + i, 0))
    return pl.pallas_call(
        body, name="attn_fwd", grid=(nb, nq),
        in_specs=[qblk, qablk, seq, seq, pl.BlockSpec((lp, 128), lambda b, i: (b, 0)),
                  pl.BlockSpec((1, ATTN_DIM), lambda b, i: (0, 0)), qblk,
                  pl.BlockSpec((D_MODEL, D_MODEL), lambda b, i: (0, 0)), hblk],
        out_specs=[qblk, qblk, rowblk, hblk],
        out_shape=[jax.ShapeDtypeStruct((n, ATTN_DIM), BF16), jax.ShapeDtypeStruct((n, ATTN_DIM), BF16),
                   jax.ShapeDtypeStruct((nb, N_HEADS, lp), F32), jax.ShapeDtypeStruct((n, D_MODEL), F32)],
        scratch_shapes=[pltpu.VMEM((lpp, 2 * ATTN_DIM), BF16), pltpu.VMEM((ATTN_DIM, lpp), BF16)],
        compiler_params=_params(("parallel", "arbitrary")),
    )(q, qa, k, v, ka, gain, zc, w_out, h)


def _attn_bwd(dz, q, qa, k, v, ka, o, lse, gain, lp, exchange=()):
    n = q.shape[0]
    nb = n // lp
    nq = lp // TQ
    lpp = _padded_keys(lp, TK)
    nw = len(exchange)

    def body(*refs):
        ((dz_ref, q_ref, qa_ref, k_ref, v_ref, ka_ref, o_ref, lse_ref, g_ref), xin,
         (dq_ref, dk_ref, dv_ref, dka_ref, dfr_ref, dgain_ref), xout,
         (kx_scr, vx_scr, kt_scr, dkx_scr, dvx_scr), sems) = _split_refs(refs, 9, nw, 6, 5)
        b = pl.program_id(0)
        i = pl.program_id(1)
        first = _head_masks()
        if nw:
            comm = _Exchange(xin, xout, sems)
            pl.when((b == 0) & (i == 0))(comm.start)

        @pl.when((b == 0) & (i == 0))
        def _():
            dgain_ref[...] = jnp.zeros_like(dgain_ref)

        @pl.when(i == 0)
        def _():
            if lpp > lp:
                kx_scr[lp:lpp, :] = jnp.zeros((lpp - lp, 2 * ATTN_DIM), BF16)
                vx_scr[lp:lpp, :] = jnp.zeros((lpp - lp, ATTN_DIM), BF16)
                kt_scr[:, lp:lpp] = jnp.zeros((ATTN_DIM, lpp - lp), BF16)
            for p in range(N_PAIRS):
                kx_scr[0:lp, 256 * p:256 * p + 128] = k_ref[:, 128 * p:128 * (p + 1)]
                kx_scr[0:lp, 256 * p + 128:256 * (p + 1)] = ka_ref[...]
            vx_scr[0:lp, :] = v_ref[...]
            kt_scr[:, 0:lp] = _transpose_bf16(k_ref[...])
            dkx_scr[...] = jnp.zeros_like(dkx_scr)
            dvx_scr[...] = jnp.zeros_like(dvx_scr)

        rhs, rhs_t, lses, dos, dos_t, deltas = [], [], [], [], [], []
        for p in range(N_PAIRS):
            cols = slice(128 * p, 128 * (p + 1))
            side = _query_side(q_ref, qa_ref, p, first)
            rhs.append(side)
            rhs_t.append(_transpose_bf16(side))
            lses.append(jnp.concatenate([lse_ref[0, 2 * p:2 * p + 1, :], lse_ref[0, 2 * p + 1:2 * p + 2, :]], axis=1))
            ov = o_ref[:, cols].astype(F32)
            dzv = dz_ref[:, cols].astype(F32)
            gv = g_ref[:, cols]
            sq = ov * ov
            ms0 = jnp.sum(jnp.where(first, sq, 0.0), axis=1, keepdims=True) * (1.0 / HEAD_DIM)
            ms1 = jnp.sum(jnp.where(first, 0.0, sq), axis=1, keepdims=True) * (1.0 / HEAD_DIM)
            r = _pair_cols(lax.rsqrt(ms0 + EPS), lax.rsqrt(ms1 + EPS), first)
            ohat = ov * r
            dyhat = dzv * gv
            dgain_ref[:, cols] += jnp.sum(dzv * ohat, axis=0, keepdims=True)
            pr = dyhat * ohat
            mean0 = jnp.sum(jnp.where(first, pr, 0.0), axis=1, keepdims=True) * (1.0 / HEAD_DIM)
            mean1 = jnp.sum(jnp.where(first, 0.0, pr), axis=1, keepdims=True) * (1.0 / HEAD_DIM)
            do = r * (dyhat - ohat * _pair_cols(mean0, mean1, first))
            ddt = (do * ov).T
            deltas.append(jnp.concatenate([jnp.sum(ddt[0:HEAD_DIM], axis=0, keepdims=True),
                                           jnp.sum(ddt[HEAD_DIM:], axis=0, keepdims=True)], axis=1))
            do_st = _stack_heads(do.astype(BF16), first)
            dos.append(do_st)
            dos_t.append(_transpose_bf16(do_st))

        def step(c, carry, masked, tk):
            koff = pl.multiple_of(c * TK, TK)
            valid = _chunk_mask(i, c, tk, TK) if masked else None
            new = []
            sts = [_dot(kx_scr[pl.ds(koff, tk), 256 * p:256 * (p + 1)], rhs_t[p]) for p in range(N_PAIRS)]
            dpts = [_dot(vx_scr[pl.ds(koff, tk), 128 * p:128 * (p + 1)], dos_t[p]) for p in range(N_PAIRS)]
            for p in range(N_PAIRS):
                dqt, dfq = carry[p]
                ext = slice(256 * p, 256 * (p + 1))
                cols = slice(128 * p, 128 * (p + 1))
                st = sts[p]
                if masked:
                    st = jnp.where(valid, st, NEG)
                pt = jnp.exp(st - lses[p])
                dst = pt * (dpts[p] - deltas[p])
                dsb = dst.astype(BF16)
                dfq = dfq + jnp.sum(dsb.astype(F32), axis=0, keepdims=True)
                dkx_scr[pl.ds(koff, tk), ext] += _dot(dsb, rhs[p])
                dvx_scr[pl.ds(koff, tk), cols] += _dot(pt.astype(BF16), dos[p])
                kt = _stack_heads_lanes(kt_scr[cols, pl.ds(koff, tk)])
                dqt = dqt + _dot(kt, jnp.concatenate([dsb[:, 0:TQ], dsb[:, TQ:]], axis=0))
                new.append((dqt, dfq))
            return tuple(new)

        init = tuple((jnp.zeros((128, TQ), F32), jnp.zeros((1, 2 * TQ), F32)) for _ in range(N_PAIRS))
        final = _causal_sweep(i, step, init, TK)

        for p in range(N_PAIRS):
            dqt, dfq = final[p]
            dq_ref[:, 128 * p:128 * (p + 1)] = (dqt.T * 0.125).astype(BF16)
            dfr_ref[0, 2 * p:2 * p + 1, :] = dfq[:, 0:TQ]
            dfr_ref[0, 2 * p + 1:2 * p + 2, :] = dfq[:, TQ:]

        @pl.when(i == nq - 1)
        def _():
            dka = jnp.zeros((lp, 128), F32)
            for p in range(N_PAIRS):
                dk_ref[:, 128 * p:128 * (p + 1)] = dkx_scr[0:lp, 256 * p:256 * p + 128].astype(BF16)
                dka = dka + dkx_scr[0:lp, 256 * p + 128:256 * (p + 1)]
            dka_ref[...] = dka
            dv_ref[...] = dvx_scr[0:lp, :].astype(BF16)

        if nw:
            pl.when((b == nb - 1) & (i == nq - 1))(comm.finish)

    qblk = pl.BlockSpec((TQ, ATTN_DIM), lambda b, i: (b * nq + i, 0))
    qablk = pl.BlockSpec((TQ, N_HEADS * 128), lambda b, i: (b * nq + i, 0))
    seq = pl.BlockSpec((lp, ATTN_DIM), lambda b, i: (b, 0))
    kaseq = pl.BlockSpec((lp, 128), lambda b, i: (b, 0))
    rowblk = pl.BlockSpec((1, N_HEADS, TQ), lambda b, i: (b, 0, i))
    gspec = pl.BlockSpec((1, ATTN_DIM), lambda b, i: (0, 0))
    return pl.pallas_call(
        body, name="attn_bwd", grid=(nb, nq),
        in_specs=[qblk, qblk, qablk, seq, seq, kaseq, qblk, rowblk, gspec] + [ANY] * nw,
        out_specs=[qblk, seq, seq, kaseq, rowblk, gspec] + [ANY] * nw,
        out_shape=[jax.ShapeDtypeStruct((n, ATTN_DIM), BF16), jax.ShapeDtypeStruct((n, ATTN_DIM), BF16),
                   jax.ShapeDtypeStruct((n, ATTN_DIM), BF16), jax.ShapeDtypeStruct((n, 128), F32),
                   jax.ShapeDtypeStruct((nb, N_HEADS, lp), F32), jax.ShapeDtypeStruct((1, ATTN_DIM), F32)]
        + [jax.ShapeDtypeStruct(a.shape, a.dtype) for a in exchange],
        scratch_shapes=[pltpu.VMEM((lpp, 2 * ATTN_DIM), BF16), pltpu.VMEM((lpp, ATTN_DIM), BF16),
                        pltpu.VMEM((ATTN_DIM, lpp), BF16), pltpu.VMEM((lpp, 2 * ATTN_DIM), F32),
                        pltpu.VMEM((lpp, ATTN_DIM), F32)] + (_comm_sems(nw) if nw else []),
        compiler_params=_params(("arbitrary", "arbitrary")),
    )(dz, q, qa, k, v, ka, o, lse, gain, *exchange)


def _adamw(parts, w, m, v, name):
    s_parts, r, c = parts.shape
    tr = r
    for t in (352, 256, 176, 128, 64, 32, 16):
        if r % t == 0 and r > t:
            tr = t
            break

    def body(p_ref, w_ref, m_ref, v_ref, g_ref, d_ref, nm_ref, nv_ref):
        g = p_ref[0].astype(F32)
        for s in range(1, s_parts):
            g = g + p_ref[s].astype(F32)
        nm = ADAM_B1 * m_ref[...] + (1.0 - ADAM_B1) * g
        nv = ADAM_B2 * v_ref[...] + (1.0 - ADAM_B2) * (g * g)
        m_hat = nm / (1.0 - ADAM_B1 ** ADAM_STEP)
        v_hat = nv / (1.0 - ADAM_B2 ** ADAM_STEP)
        g_ref[...] = g
        d_ref[...] = -ADAM_LR * (m_hat / (jnp.sqrt(v_hat) + ADAM_EPS) + ADAM_WD * w_ref[...])
        nm_ref[...] = nm
        nv_ref[...] = nv

    blk = pl.BlockSpec((tr, c), lambda i: (i, 0))
    return pl.pallas_call(
        body, name=name, grid=(r // tr,),
        in_specs=[pl.BlockSpec((s_parts, tr, c), lambda i: (0, i, 0)), blk, blk, blk],
        out_specs=[blk] * 4,
        out_shape=[jax.ShapeDtypeStruct((r, c), F32)] * 4,
        compiler_params=_params(("parallel",)),
    )(parts, w, m, v)


def _sum_parts(parts, name):
    s_parts, r, c = parts.shape

    def body(p_ref, out_ref):
        acc = p_ref[0]
        for s in range(1, s_parts):
            acc = acc + p_ref[s]
        out_ref[...] = acc

    return pl.pallas_call(
        body, name=name, out_shape=jax.ShapeDtypeStruct((r, c), F32),
        in_specs=[pl.BlockSpec(memory_space=pltpu.VMEM)], out_specs=pl.BlockSpec(memory_space=pltpu.VMEM),
    )(parts)


SMALL_ROWS = 184
LOSS_ROW = 181


def _pack_small(d_gains, d_gc, d_ga, d_bf, d_conv, d_meta, loss_part):
    rows = [g.reshape(8, 128) for g in d_gains]
    rows += [d_gc.reshape(4, 128), d_ga.reshape(4, 128), d_bf.reshape(1, 128)]
    rows += [d_conv.reshape(12, 128), d_meta.reshape(128, 128), jnp.pad(loss_part, ((0, 0), (0, 127)))]
    packed = jnp.concatenate(rows, axis=0)
    return jnp.pad(packed, ((0, SMALL_ROWS - packed.shape[0]), (0, 0)))


def kernel(x, meta_tokens, ffn1_norm, ffn1_w_gu, ffn1_w_down, mix_norm, w_in, conv_w, b_f, out_norm_conv, out_norm_attn, w_out, ffn2_norm, ffn2_w_gu, ffn2_w_down, final_norm, loss_target, m_meta_tokens, m_ffn1_norm, m_ffn1_w_gu, m_ffn1_w_down, m_mix_norm, m_w_in, m_conv_w, m_b_f, m_out_norm_conv, m_out_norm_attn, m_w_out, m_ffn2_norm, m_ffn2_w_gu, m_ffn2_w_down, m_final_norm, v_meta_tokens, v_ffn1_norm, v_ffn1_w_gu, v_ffn1_w_down, v_mix_norm, v_w_in, v_conv_w, v_b_f, v_out_norm_conv, v_out_norm_attn, v_w_out, v_ffn2_norm, v_ffn2_w_gu, v_ffn2_w_down, v_final_norm):
    nb, seq, _ = x.shape
    lp = PAD + N_META + seq
    me = 4 * lax.axis_index("x") + 2 * lax.axis_index("y") + lax.axis_index("c")
    shard_gu = D_FF // 4
    shard_d = D_FF // N_DEV

    small_in = jnp.concatenate(
        [meta_tokens, jnp.pad(conv_w[0], ((0, 0), (0, 128 - conv_w.shape[2]))), jnp.zeros((5, 128), F32)], axis=0)
    wgu1_8, wd1_8, small_8 = _all_gather(
        [ffn1_w_gu[0].T.astype(BF16), ffn1_w_down[0].astype(BF16), small_in], "gather_ffn1")
    meta_full = small_8[:, 0:N_META, :].transpose(1, 0, 2).reshape(N_META, D_MODEL)
    conv_full = small_8[:, N_META:N_META + 3, 0:CONV_DIM // N_DEV].transpose(1, 0, 2).reshape(3, CONV_DIM)
    wgu1 = wgu1_8.reshape(W_GU_SHAPE)
    wd1 = wd1_8.reshape(W_D_SHAPE)
    b_f_row = jnp.pad(b_f, ((0, 0), (0, 128 - N_HEADS)))
    gmat = _group_matrix()

    x2d = x.reshape(nb * seq, D_MODEL)
    later = [w_in[0].T.astype(BF16), w_out[0].astype(BF16), ffn2_w_gu[0].T.astype(BF16), ffn2_w_down[0].astype(BF16)]
    h1, n1, gate1, up1, win_8, wout_8, wgu2_8, wd2_8 = _ffn1_fwd(x2d, meta_full, lp, ffn1_norm, wgu1, wd1, later)
    wgu2 = wgu2_8.reshape(W_GU_SHAPE)
    wd2 = wd2_8.reshape(W_D_SHAPE)
    w_in_full = jnp.pad(win_8.reshape(IN_DIM, D_MODEL), ((0, IN_PAD - IN_DIM), (0, 0)))
    w_out_full = wout_8.reshape(D_MODEL, D_MODEL)

    bg, cg, hc, q, k, v, fg = _inproj_fwd(h1, mix_norm, w_in_full)
    zc = _conv_fwd(bg, cg, hc, conv_full, out_norm_conv, gmat, lp)
    ka, qa = _fgate_fwd(fg, b_f_row, lp)
    za, o, lse, h2 = _attn_fwd(q, qa, k, v, ka, out_norm_attn, zc, w_out_full, h1, lp)
    dh3, n3, gate2, up2, loss_part, d_final = _ffn2_fwd_loss(
        h2, ffn2_norm, wgu2, wd2, final_norm.reshape(1, D_MODEL), loss_target.reshape(nb * seq, D_MODEL), lp)

    dgu2, dwd2 = _ffn_bwd_act_wd(dh3, gate2, up2, wd2, "ffn2_bwd_act")
    dh2, d_ffn2 = _ffn_bwd_in(dh3, h2, ffn2_norm, dgu2, wgu2, "ffn2_bwd_in")
    dwgu2 = _ffn_bwd_wgu(n3, dgu2, "ffn2_bwd_wgu")
    dzc, dza, dwout = _outproj_bwd(dh2, zc, za, w_out_full)
    send_a = [dwgu2.reshape(N_DEV, shard_gu, D_MODEL), dwd2.reshape(N_DEV, shard_d, D_MODEL),
              dwout.reshape(N_DEV, D_MODEL // N_DEV, D_MODEL)]
    dq, dk, dv, dka, dfr, d_ga, p_wgu2, p_wd2, p_wout = _attn_bwd(
        dza, q, qa, k, v, ka, o, lse, out_norm_attn, lp, exchange=send_a)
    dfg, d_bf = _fgate_bwd(dka, dfr, fg, b_f_row, lp)
    dbg, dcg, dhc, d_conv, d_gc = _conv_bwd(dzc, bg, cg, hc, conv_full, out_norm_conv, gmat, lp)
    dh1, dwin, d_mix = _inproj_bwd([dbg, dcg, dhc, dq, dk, dv], dfg, dh2, h1, mix_norm, w_in_full)
    dwin_8 = dwin[0:IN_DIM].reshape(N_DEV, IN_DIM // N_DEV, D_MODEL)
    dgu1, dwd1, p_win = _ffn_bwd_act_wd(dh1, gate1, up1, wd1, "ffn1_bwd_act", exchange=[dwin_8])
    dwgu1, p_wd1 = _ffn_bwd_wgu(n1, dgu1, "ffn1_bwd_wgu", exchange=[dwd1.reshape(N_DEV, shard_d, D_MODEL)])
    dh0, d_ffn1, p_wgu1 = _ffn_bwd_in(dh1, None, ffn1_norm, dgu1, wgu1, "ffn1_bwd_in",
                                      tokens=(x2d, meta_full, lp),
                                      exchange=[dwgu1.reshape(N_DEV, shard_gu, D_MODEL)])

    dh0 = dh0.reshape(nb, lp, D_MODEL)
    grad_x = dh0[:, PAD + N_META:, :]
    d_meta = jnp.sum(dh0[:, PAD:PAD + N_META, :], axis=0)

    small = _pack_small([d_ffn1, d_mix, d_ffn2, d_final], d_gc, d_ga, d_bf, d_conv, d_meta, loss_part)
    (small_all,) = _all_gather([small], "gather_small_grads")
    small_sum = _sum_parts(small_all, "sum_small_grads")
    g_ffn1n, g_mixn, g_ffn2n, g_finaln = (small_sum[8 * t:8 * t + 8].reshape(1, D_MODEL) for t in range(4))
    g_gc = small_sum[32:36].reshape(1, CONV_DIM)
    g_ga = small_sum[36:40].reshape(1, ATTN_DIM)
    g_bf = small_sum[40:41, 0:N_HEADS]
    g_conv_full = small_sum[41:53].reshape(3, CONV_DIM)
    g_meta_full = small_sum[53:181].reshape(N_META, D_MODEL)
    g_conv = lax.dynamic_slice_in_dim(g_conv_full, me * (CONV_DIM // N_DEV), CONV_DIM // N_DEV, axis=1)
    g_meta = lax.dynamic_slice_in_dim(g_meta_full, me * (D_MODEL // N_DEV), D_MODEL // N_DEV, axis=1)

    weights = {
        "meta_tokens": (g_meta[None], meta_tokens, m_meta_tokens, v_meta_tokens),
        "ffn1_norm": (g_ffn1n[None], ffn1_norm, m_ffn1_norm, v_ffn1_norm),
        "ffn1_w_gu": (p_wgu1, ffn1_w_gu[0].T, m_ffn1_w_gu[0].T, v_ffn1_w_gu[0].T),
        "ffn1_w_down": (p_wd1, ffn1_w_down[0], m_ffn1_w_down[0], v_ffn1_w_down[0]),
        "mix_norm": (g_mixn[None], mix_norm, m_mix_norm, v_mix_norm),
        "w_in": (p_win, w_in[0].T, m_w_in[0].T, v_w_in[0].T),
        "conv_w": (g_conv[None], conv_w[0], m_conv_w[0], v_conv_w[0]),
        "b_f": (g_bf[None], b_f, m_b_f, v_b_f),
        "out_norm_conv": (g_gc[None], out_norm_conv, m_out_norm_conv, v_out_norm_conv),
        "out_norm_attn": (g_ga[None], out_norm_attn, m_out_norm_attn, v_out_norm_attn),
        "w_out": (p_wout, w_out[0], m_w_out[0], v_w_out[0]),
        "ffn2_norm": (g_ffn2n[None], ffn2_norm, m_ffn2_norm, v_ffn2_norm),
        "ffn2_w_gu": (p_wgu2, ffn2_w_gu[0].T, m_ffn2_w_gu[0].T, v_ffn2_w_gu[0].T),
        "ffn2_w_down": (p_wd2, ffn2_w_down[0], m_ffn2_w_down[0], v_ffn2_w_down[0]),
        "final_norm": (g_finaln[None], final_norm.reshape(1, D_MODEL), m_final_norm.reshape(1, D_MODEL),
                       v_final_norm.reshape(1, D_MODEL)),
    }
    shapes = {"meta_tokens": meta_tokens.shape, "ffn1_norm": ffn1_norm.shape, "ffn1_w_gu": ffn1_w_gu.shape,
              "ffn1_w_down": ffn1_w_down.shape, "mix_norm": mix_norm.shape, "w_in": w_in.shape,
              "conv_w": conv_w.shape, "b_f": b_f.shape, "out_norm_conv": out_norm_conv.shape,
              "out_norm_attn": out_norm_attn.shape, "w_out": w_out.shape, "ffn2_norm": ffn2_norm.shape,
              "ffn2_w_gu": ffn2_w_gu.shape, "ffn2_w_down": ffn2_w_down.shape, "final_norm": final_norm.shape}
    grads, deltas, new_m, new_v = [], [], [], []
    for name, (p, w, m, vv) in weights.items():
        g, d, nm, nv = _adamw(p, w, m, vv, "adamw_" + name)
        if name in ("ffn1_w_gu", "ffn2_w_gu", "w_in"):
            g, d, nm, nv = g.T, d.T, nm.T, nv.T
        shape = shapes[name]
        grads.append(g.reshape(shape))
        deltas.append(d.reshape(shape))
        new_m.append(nm.reshape(shape))
        new_v.append(nv.reshape(shape))

    loss = small_sum[LOSS_ROW, 0]
    return (loss, grad_x, *grads, *deltas, *new_m, *new_v)
```
